```python
import jax, jax.numpy as jnp
from jax import lax
import numpy as np

D_MODEL = 2048
BATCH = 8
SEQ = 2048
DEPTH = 1

GMLP_GROUPS = 8
GMLP_GROUP_DIM = 128
GMLP_WIDTH = GMLP_GROUPS * GMLP_GROUP_DIM
GMLP_CHUNK = 128
HG_HEADS = 8
HG_DK = 128
HG_DV = 128
HG_WIDTH = HG_HEADS * HG_DK
HG_CHUNK = 64
FFN_HIDDEN = -(-8 * D_MODEL // (3 * 256)) * 256
IN_WIDTH = 2 * GMLP_WIDTH + 4 * HG_WIDTH + 2 * D_MODEL
EPS = 1e-6

kernel_name = "hybrid_gmlp_hgrn2_adaln_block"


def rmsnorm(x, g):
    xf = x.astype(jnp.float32)
    y = xf * lax.rsqrt(jnp.mean(xf * xf, axis=-1, keepdims=True) + EPS)
    return (y * g.astype(jnp.float32)).astype(x.dtype)


def layernorm(x, g, b):
    xf = x.astype(jnp.float32)
    mu = jnp.mean(xf, axis=-1, keepdims=True)
    var = jnp.mean(jnp.square(xf - mu), axis=-1, keepdims=True)
    y = (xf - mu) * lax.rsqrt(var + EPS)
    return (y * g.astype(jnp.float32) + b.astype(jnp.float32)).astype(x.dtype)


def modulate(h, shift, scale):
    return h * (1 + scale[:, None, :]) + shift[:, None, :]


def gmlp_branch(u, v, ln_g, ln_b, ws, bs):
    B, S, _ = v.shape
    nc = S // GMLP_CHUNK
    v = layernorm(v, ln_g, ln_b)
    vc = v.reshape(B, nc, GMLP_CHUNK, GMLP_GROUPS, GMLP_GROUP_DIM)
    mask = jnp.tril(jnp.ones((GMLP_CHUNK, GMLP_CHUNK), dtype=bool))
    w = jnp.where(mask[None], ws, 0).astype(v.dtype)
    s = jnp.einsum('gts,bcsgd->bctgd', w, vc)
    s = s + jnp.transpose(bs)[None, None, :, :, None].astype(v.dtype)
    return u * s.reshape(B, S, GMLP_WIDTH)


def hgrn2_branch(q, f_pre, i, g, lb, norm_g):
    B, S, _ = q.shape
    dt = q.dtype
    f = lb + (1.0 - lb) * jax.nn.sigmoid(f_pre.astype(jnp.float32))
    log_f = jnp.log(f)
    k = 1.0 - f
    qa = jax.nn.silu(q.astype(jnp.float32))
    nc = S // HG_CHUNK

    def heads(t, d):
        return t.reshape(B, nc, HG_CHUNK, HG_HEADS, d).transpose(1, 0, 3, 2, 4)

    qc, kc, lfc = heads(qa, HG_DK), heads(k, HG_DK), heads(log_f, HG_DK)
    ic = heads(i.astype(jnp.float32), HG_DV)
    mask = jnp.tril(jnp.ones((HG_CHUNK, HG_CHUNK), dtype=bool))

    def step(state, inp):
        qt, kt, it, lft = inp
        b = jnp.cumsum(lft, axis=2)
        o_inter = jnp.einsum('bhtk,bhkv->bhtv', qt * jnp.exp(b), state)
        rel = b[:, :, :, None, :] - b[:, :, None, :, :]
        decay = jnp.exp(jnp.where(mask[:, :, None], rel, -jnp.inf))
        attn = jnp.einsum('bhtk,bhsk,bhtsk->bhts', qt, kt, decay)
        o = o_inter + jnp.einsum('bhts,bhsv->bhtv', attn, it)
        b_last = b[:, :, -1:, :]
        new_state = jnp.exp(b_last[:, :, 0, :])[..., None] * state + jnp.einsum(
            'bhsk,bhsv->bhkv', kt * jnp.exp(b_last - b), it)
        return new_state, o

    s0 = jnp.zeros((B, HG_HEADS, HG_DK, HG_DV), jnp.float32)
    _, o = lax.scan(step, s0, (qc, kc, ic, lfc))
    o = o.transpose(1, 0, 3, 2, 4).reshape(B, S, HG_HEADS, HG_DV)
    o = o * lax.rsqrt(jnp.mean(o * o, axis=-1, keepdims=True) + EPS)
    o = o.reshape(B, S, HG_HEADS * HG_DV) * norm_g.astype(jnp.float32) * jax.nn.silu(g.astype(jnp.float32))
    return o.astype(dt)


def _fwd_setup_inputs(seed: int = 0) -> dict:
    key = jax.random.key(seed)
    ks = jax.random.split(key, 24)
    f32 = jnp.float32
    nrm = lambda k, shape, s: jax.random.normal(k, shape, f32) * s
    L, D = DEPTH, D_MODEL
    return {
        "x": nrm(ks[0], (BATCH, SEQ, D), 1.0),
        "c": nrm(ks[1], (BATCH, D), 1.0),
        "w_ada": nrm(ks[2], (L, D, 6 * D), 0.5 * D ** -0.5),
        "b_ada": nrm(ks[3], (L, 6 * D), 0.02),
        "norm1_g": 1.0 + nrm(ks[4], (L, D), 0.02),
        "w_in": nrm(ks[5], (L, D, IN_WIDTH), D ** -0.5),
        "b_gate": nrm(ks[6], (L, 2 * D), 0.02),
        "gmlp_ln_g": 1.0 + nrm(ks[7], (L, GMLP_WIDTH), 0.02),
        "gmlp_ln_b": nrm(ks[8], (L, GMLP_WIDTH), 0.02),
        "gmlp_ws": nrm(ks[9], (L, GMLP_GROUPS, GMLP_CHUNK, GMLP_CHUNK), GMLP_CHUNK ** -0.5),
        "gmlp_bs": 1.0 + nrm(ks[10], (L, GMLP_GROUPS, GMLP_CHUNK), 0.1),
        "hg_lb": nrm(ks[11], (DEPTH + 1, HG_WIDTH), 1.0),
        "hg_norm_g": 1.0 + nrm(ks[12], (L, HG_WIDTH), 0.02),
        "w_branch_gmlp": nrm(ks[13], (L, GMLP_WIDTH, D), GMLP_WIDTH ** -0.5),
        "w_branch_hg": nrm(ks[14], (L, HG_WIDTH, D), HG_WIDTH ** -0.5),
        "w_out": nrm(ks[15], (L, D, D), D ** -0.5),
        "norm2_g": 1.0 + nrm(ks[16], (L, D), 0.02),
        "w_ffn_in": nrm(ks[17], (L, D, 2 * FFN_HIDDEN), D ** -0.5),
        "w_ffn_out": nrm(ks[18], (L, FFN_HIDDEN, D), FFN_HIDDEN ** -0.5),
        "final_norm_g": 1.0 + nrm(ks[19], (D,), 0.02),
    }


def _fwd_reference(x, c, w_ada, b_ada, norm1_g, w_in, b_gate, gmlp_ln_g, gmlp_ln_b, gmlp_ws, gmlp_bs,
              hg_lb, hg_norm_g, w_branch_gmlp, w_branch_hg, w_out, norm2_g, w_ffn_in, w_ffn_out,
              final_norm_g):
    lb_all = jnp.cumsum(jax.nn.softmax(hg_lb.astype(jnp.float32), axis=0), axis=0)
    splits = np.cumsum([GMLP_WIDTH, GMLP_WIDTH, HG_WIDTH, HG_WIDTH, HG_WIDTH, HG_WIDTH, D_MODEL]).tolist()
    c_act = jax.nn.silu(c)
    for l in range(DEPTH):
        mod = c_act @ w_ada[l] + b_ada[l]
        sh1, sc1, gt1, sh2, sc2, gt2 = jnp.split(mod, 6, axis=-1)
        h = modulate(rmsnorm(x, norm1_g[l]), sh1, sc1)
        z = h @ w_in[l]
        u, v, q, f_pre, iv, og, ga, gb = jnp.split(z, splits, axis=-1)
        ya = gmlp_branch(jax.nn.gelu(u), jax.nn.gelu(v), gmlp_ln_g[l], gmlp_ln_b[l], gmlp_ws[l], gmlp_bs[l])
        yb = hgrn2_branch(q, f_pre, iv, og, lb_all[l], hg_norm_g[l])
        ga = jax.nn.sigmoid(ga + b_gate[l, :D_MODEL])
        gb = jax.nn.sigmoid(gb + b_gate[l, D_MODEL:])
        y = ga * (ya @ w_branch_gmlp[l]) + gb * (yb @ w_branch_hg[l])
        x = x + gt1[:, None, :] * (y @ w_out[l])
        h = modulate(rmsnorm(x, norm2_g[l]), sh2, sc2)
        a, up = jnp.split(h @ w_ffn_in[l], 2, axis=-1)
        x = x + gt2[:, None, :] * ((jax.nn.silu(a) * up) @ w_ffn_out[l])
    return rmsnorm(x, final_norm_g)


import jax as _jax
import jax.numpy as _jnp

TWIN_FORMAT = 'train_step'
FWD_PARAMS = ['x', 'c', 'w_ada', 'b_ada', 'norm1_g', 'w_in', 'b_gate', 'gmlp_ln_g', 'gmlp_ln_b', 'gmlp_ws', 'gmlp_bs', 'hg_lb', 'hg_norm_g', 'w_branch_gmlp', 'w_branch_hg', 'w_out', 'norm2_g', 'w_ffn_in', 'w_ffn_out', 'final_norm_g']
TWIN_WEIGHTS = ['w_ada', 'b_ada', 'norm1_g', 'w_in', 'b_gate', 'gmlp_ln_g', 'gmlp_ln_b', 'gmlp_ws', 'gmlp_bs', 'hg_lb', 'hg_norm_g', 'w_branch_gmlp', 'w_branch_hg', 'w_out', 'norm2_g', 'w_ffn_in', 'w_ffn_out', 'final_norm_g']
TWIN_DIFF_INPUT = 'x'
TWIN_INPUTS = ['x', 'c', 'w_ada', 'b_ada', 'norm1_g', 'w_in', 'b_gate', 'gmlp_ln_g', 'gmlp_ln_b', 'gmlp_ws', 'gmlp_bs', 'hg_lb', 'hg_norm_g', 'w_branch_gmlp', 'w_branch_hg', 'w_out', 'norm2_g', 'w_ffn_in', 'w_ffn_out', 'final_norm_g', 'loss_target', 'm_w_ada', 'm_b_ada', 'm_norm1_g', 'm_w_in', 'm_b_gate', 'm_gmlp_ln_g', 'm_gmlp_ln_b', 'm_gmlp_ws', 'm_gmlp_bs', 'm_hg_lb', 'm_hg_norm_g', 'm_w_branch_gmlp', 'm_w_branch_hg', 'm_w_out', 'm_norm2_g', 'm_w_ffn_in', 'm_w_ffn_out', 'm_final_norm_g', 'v_w_ada', 'v_b_ada', 'v_norm1_g', 'v_w_in', 'v_b_gate', 'v_gmlp_ln_g', 'v_gmlp_ln_b', 'v_gmlp_ws', 'v_gmlp_bs', 'v_hg_lb', 'v_hg_norm_g', 'v_w_branch_gmlp', 'v_w_branch_hg', 'v_w_out', 'v_norm2_g', 'v_w_ffn_in', 'v_w_ffn_out', 'v_final_norm_g']
TWIN_OUTPUTS = ['loss', 'grad_x', 'grad_w_ada', 'grad_b_ada', 'grad_norm1_g', 'grad_w_in', 'grad_b_gate', 'grad_gmlp_ln_g', 'grad_gmlp_ln_b', 'grad_gmlp_ws', 'grad_gmlp_bs', 'grad_hg_lb', 'grad_hg_norm_g', 'grad_w_branch_gmlp', 'grad_w_branch_hg', 'grad_w_out', 'grad_norm2_g', 'grad_w_ffn_in', 'grad_w_ffn_out', 'grad_final_norm_g', 'delta_w_ada', 'delta_b_ada', 'delta_norm1_g', 'delta_w_in', 'delta_b_gate', 'delta_gmlp_ln_g', 'delta_gmlp_ln_b', 'delta_gmlp_ws', 'delta_gmlp_bs', 'delta_hg_lb', 'delta_hg_norm_g', 'delta_w_branch_gmlp', 'delta_w_branch_hg', 'delta_w_out', 'delta_norm2_g', 'delta_w_ffn_in', 'delta_w_ffn_out', 'delta_final_norm_g', 'new_m_w_ada', 'new_m_b_ada', 'new_m_norm1_g', 'new_m_w_in', 'new_m_b_gate', 'new_m_gmlp_ln_g', 'new_m_gmlp_ln_b', 'new_m_gmlp_ws', 'new_m_gmlp_bs', 'new_m_hg_lb', 'new_m_hg_norm_g', 'new_m_w_branch_gmlp', 'new_m_w_branch_hg', 'new_m_w_out', 'new_m_norm2_g', 'new_m_w_ffn_in', 'new_m_w_ffn_out', 'new_m_final_norm_g', 'new_v_w_ada', 'new_v_b_ada', 'new_v_norm1_g', 'new_v_w_in', 'new_v_b_gate', 'new_v_gmlp_ln_g', 'new_v_gmlp_ln_b', 'new_v_gmlp_ws', 'new_v_gmlp_bs', 'new_v_hg_lb', 'new_v_hg_norm_g', 'new_v_w_branch_gmlp', 'new_v_w_branch_hg', 'new_v_w_out', 'new_v_norm2_g', 'new_v_w_ffn_in', 'new_v_w_ffn_out', 'new_v_final_norm_g']
TWIN_LEAF_KINDS = {'loss': 'loss', 'grad_x': 'grad_x', 'grad_w_ada': 'grad_w', 'grad_b_ada': 'grad_w', 'grad_norm1_g': 'grad_w', 'grad_w_in': 'grad_w', 'grad_b_gate': 'grad_w', 'grad_gmlp_ln_g': 'grad_w', 'grad_gmlp_ln_b': 'grad_w', 'grad_gmlp_ws': 'grad_w', 'grad_gmlp_bs': 'grad_w', 'grad_hg_lb': 'grad_w', 'grad_hg_norm_g': 'grad_w', 'grad_w_branch_gmlp': 'grad_w', 'grad_w_branch_hg': 'grad_w', 'grad_w_out': 'grad_w', 'grad_norm2_g': 'grad_w', 'grad_w_ffn_in': 'grad_w', 'grad_w_ffn_out': 'grad_w', 'grad_final_norm_g': 'grad_w', 'delta_w_ada': 'delta_w', 'delta_b_ada': 'delta_w', 'delta_norm1_g': 'delta_w', 'delta_w_in': 'delta_w', 'delta_b_gate': 'delta_w', 'delta_gmlp_ln_g': 'delta_w', 'delta_gmlp_ln_b': 'delta_w', 'delta_gmlp_ws': 'delta_w', 'delta_gmlp_bs': 'delta_w', 'delta_hg_lb': 'delta_w', 'delta_hg_norm_g': 'delta_w', 'delta_w_branch_gmlp': 'delta_w', 'delta_w_branch_hg': 'delta_w', 'delta_w_out': 'delta_w', 'delta_norm2_g': 'delta_w', 'delta_w_ffn_in': 'delta_w', 'delta_w_ffn_out': 'delta_w', 'delta_final_norm_g': 'delta_w', 'new_m_w_ada': 'new_m', 'new_m_b_ada': 'new_m', 'new_m_norm1_g': 'new_m', 'new_m_w_in': 'new_m', 'new_m_b_gate': 'new_m', 'new_m_gmlp_ln_g': 'new_m', 'new_m_gmlp_ln_b': 'new_m', 'new_m_gmlp_ws': 'new_m', 'new_m_gmlp_bs': 'new_m', 'new_m_hg_lb': 'new_m', 'new_m_hg_norm_g': 'new_m', 'new_m_w_branch_gmlp': 'new_m', 'new_m_w_branch_hg': 'new_m', 'new_m_w_out': 'new_m', 'new_m_norm2_g': 'new_m', 'new_m_w_ffn_in': 'new_m', 'new_m_w_ffn_out': 'new_m', 'new_m_final_norm_g': 'new_m', 'new_v_w_ada': 'new_v', 'new_v_b_ada': 'new_v', 'new_v_norm1_g': 'new_v', 'new_v_w_in': 'new_v', 'new_v_b_gate': 'new_v', 'new_v_gmlp_ln_g': 'new_v', 'new_v_gmlp_ln_b': 'new_v', 'new_v_gmlp_ws': 'new_v', 'new_v_gmlp_bs': 'new_v', 'new_v_hg_lb': 'new_v', 'new_v_hg_norm_g': 'new_v', 'new_v_w_branch_gmlp': 'new_v', 'new_v_w_branch_hg': 'new_v', 'new_v_w_out': 'new_v', 'new_v_norm2_g': 'new_v', 'new_v_w_ffn_in': 'new_v', 'new_v_w_ffn_out': 'new_v', 'new_v_final_norm_g': 'new_v'}


def _forward(args):
    return _fwd_reference(*[args[k] for k in FWD_PARAMS])


def _output_shape():
    out = _jax.eval_shape(lambda: _forward(_fwd_setup_inputs(0)))
    return out.shape, out.dtype

N_MICROBATCH = 1
ADAM_LR = 0.001
ADAM_B1 = 0.9
ADAM_B2 = 0.999
ADAM_EPS = 1e-08
ADAM_WD = 0.01
ADAM_STEP = 10
PER_EXAMPLE_BATCH_AXIS = {'x': 0, 'c': 0, 'loss_target': 0}
SHARED_INPUTS = []
_WEIGHT_DTYPES = {'w_ada': _jnp.float32, 'b_ada': _jnp.float32, 'norm1_g': _jnp.float32, 'w_in': _jnp.float32, 'b_gate': _jnp.float32, 'gmlp_ln_g': _jnp.float32, 'gmlp_ln_b': _jnp.float32, 'gmlp_ws': _jnp.float32, 'gmlp_bs': _jnp.float32, 'hg_lb': _jnp.float32, 'hg_norm_g': _jnp.float32, 'w_branch_gmlp': _jnp.float32, 'w_branch_hg': _jnp.float32, 'w_out': _jnp.float32, 'norm2_g': _jnp.float32, 'w_ffn_in': _jnp.float32, 'w_ffn_out': _jnp.float32, 'final_norm_g': _jnp.float32}
MOMENT_SCALE = {'w_ada': 1.695576e-02, 'b_ada': 2.786314e-02, 'norm1_g': 1.462125e-02, 'w_in': 7.033945e-03, 'b_gate': 3.104215e-03, 'gmlp_ln_g': 7.550766e-03, 'gmlp_ln_b': 7.667444e-03, 'gmlp_ws': 7.455407e-03, 'gmlp_bs': 1.070663e-02, 'hg_lb': 6.597453e-04, 'hg_norm_g': 1.020343e-02, 'w_branch_gmlp': 9.242965e-03, 'w_branch_hg': 6.893927e-03, 'w_out': 1.152160e-02, 'norm2_g': 1.855792e-02, 'w_ffn_in': 8.079256e-03, 'w_ffn_out': 1.317495e-02, 'final_norm_g': 8.005565e+00}


def _to_microbatches(a, axis):
    t = _jnp.moveaxis(a, axis, 0)
    t = t.reshape((N_MICROBATCH, t.shape[0] // N_MICROBATCH) + t.shape[1:])
    return _jnp.moveaxis(t, 1, axis + 1)


def setup_inputs(seed: int = 0) -> dict:
    inp = _fwd_setup_inputs(seed)
    key = _jax.random.fold_in(_jax.random.key(seed), 7919)
    shape, _ = _output_shape()
    out = dict(inp)
    out["loss_target"] = _jax.random.normal(_jax.random.fold_in(key, 0), shape, _jnp.float32)
    for i, name in enumerate(TWIN_WEIGHTS):
        w = inp[name].astype(_jnp.float32)
        if MOMENT_SCALE is None:
            s = _jnp.sqrt(_jnp.mean(_jnp.square(w)) + 1e-30)
        else:
            s = MOMENT_SCALE[name]
        km, kv = _jax.random.split(_jax.random.fold_in(key, i + 1))
        out[name] = w
        out["m_" + name] = s * _jax.random.normal(km, w.shape, _jnp.float32)
        out["v_" + name] = (s * s) * _jax.random.uniform(kv, w.shape, _jnp.float32, 0.5, 1.5)
    if N_MICROBATCH > 1:
        for name, axis in PER_EXAMPLE_BATCH_AXIS.items():
            out[name] = _to_microbatches(out[name], axis)
    return {'x': out['x'], 'c': out['c'], 'w_ada': out['w_ada'], 'b_ada': out['b_ada'], 'norm1_g': out['norm1_g'], 'w_in': out['w_in'], 'b_gate': out['b_gate'], 'gmlp_ln_g': out['gmlp_ln_g'], 'gmlp_ln_b': out['gmlp_ln_b'], 'gmlp_ws': out['gmlp_ws'], 'gmlp_bs': out['gmlp_bs'], 'hg_lb': out['hg_lb'], 'hg_norm_g': out['hg_norm_g'], 'w_branch_gmlp': out['w_branch_gmlp'], 'w_branch_hg': out['w_branch_hg'], 'w_out': out['w_out'], 'norm2_g': out['norm2_g'], 'w_ffn_in': out['w_ffn_in'], 'w_ffn_out': out['w_ffn_out'], 'final_norm_g': out['final_norm_g'], 'loss_target': out['loss_target'], 'm_w_ada': out['m_w_ada'], 'm_b_ada': out['m_b_ada'], 'm_norm1_g': out['m_norm1_g'], 'm_w_in': out['m_w_in'], 'm_b_gate': out['m_b_gate'], 'm_gmlp_ln_g': out['m_gmlp_ln_g'], 'm_gmlp_ln_b': out['m_gmlp_ln_b'], 'm_gmlp_ws': out['m_gmlp_ws'], 'm_gmlp_bs': out['m_gmlp_bs'], 'm_hg_lb': out['m_hg_lb'], 'm_hg_norm_g': out['m_hg_norm_g'], 'm_w_branch_gmlp': out['m_w_branch_gmlp'], 'm_w_branch_hg': out['m_w_branch_hg'], 'm_w_out': out['m_w_out'], 'm_norm2_g': out['m_norm2_g'], 'm_w_ffn_in': out['m_w_ffn_in'], 'm_w_ffn_out': out['m_w_ffn_out'], 'm_final_norm_g': out['m_final_norm_g'], 'v_w_ada': out['v_w_ada'], 'v_b_ada': out['v_b_ada'], 'v_norm1_g': out['v_norm1_g'], 'v_w_in': out['v_w_in'], 'v_b_gate': out['v_b_gate'], 'v_gmlp_ln_g': out['v_gmlp_ln_g'], 'v_gmlp_ln_b': out['v_gmlp_ln_b'], 'v_gmlp_ws': out['v_gmlp_ws'], 'v_gmlp_bs': out['v_gmlp_bs'], 'v_hg_lb': out['v_hg_lb'], 'v_hg_norm_g': out['v_hg_norm_g'], 'v_w_branch_gmlp': out['v_w_branch_gmlp'], 'v_w_branch_hg': out['v_w_branch_hg'], 'v_w_out': out['v_w_out'], 'v_norm2_g': out['v_norm2_g'], 'v_w_ffn_in': out['v_w_ffn_in'], 'v_w_ffn_out': out['v_w_ffn_out'], 'v_final_norm_g': out['v_final_norm_g']}


def _loss(weights, diff, rest, loss_target):
    with _jax.named_scope("forward"):
        args = {**rest, TWIN_DIFF_INPUT: diff, **{k: w.astype(_WEIGHT_DTYPES[k]) for k, w in weights.items()}}
        y = _forward(args)
    with _jax.named_scope("loss_head"):
        err = _jnp.square(y.astype(_jnp.float32) - loss_target)
        return 0.5 * _jnp.sum(_jnp.mean(err, axis=-1)) if err.ndim else 0.5 * err


def _adamw(w, g, m, v):
    m = ADAM_B1 * m + (1.0 - ADAM_B1) * g
    v = ADAM_B2 * v + (1.0 - ADAM_B2) * _jnp.square(g)
    m_hat = m / (1.0 - ADAM_B1 ** ADAM_STEP)
    v_hat = v / (1.0 - ADAM_B2 ** ADAM_STEP)
    delta = -ADAM_LR * (m_hat / (_jnp.sqrt(v_hat) + ADAM_EPS) + ADAM_WD * w)
    return delta, m, v


def reference(x, c, w_ada, b_ada, norm1_g, w_in, b_gate, gmlp_ln_g, gmlp_ln_b, gmlp_ws, gmlp_bs, hg_lb, hg_norm_g, w_branch_gmlp, w_branch_hg, w_out, norm2_g, w_ffn_in, w_ffn_out, final_norm_g, loss_target, m_w_ada, m_b_ada, m_norm1_g, m_w_in, m_b_gate, m_gmlp_ln_g, m_gmlp_ln_b, m_gmlp_ws, m_gmlp_bs, m_hg_lb, m_hg_norm_g, m_w_branch_gmlp, m_w_branch_hg, m_w_out, m_norm2_g, m_w_ffn_in, m_w_ffn_out, m_final_norm_g, v_w_ada, v_b_ada, v_norm1_g, v_w_in, v_b_gate, v_gmlp_ln_g, v_gmlp_ln_b, v_gmlp_ws, v_gmlp_bs, v_hg_lb, v_hg_norm_g, v_w_branch_gmlp, v_w_branch_hg, v_w_out, v_norm2_g, v_w_ffn_in, v_w_ffn_out, v_final_norm_g):
    given = dict(x=x, c=c, w_ada=w_ada, b_ada=b_ada, norm1_g=norm1_g, w_in=w_in, b_gate=b_gate, gmlp_ln_g=gmlp_ln_g, gmlp_ln_b=gmlp_ln_b, gmlp_ws=gmlp_ws, gmlp_bs=gmlp_bs, hg_lb=hg_lb, hg_norm_g=hg_norm_g, w_branch_gmlp=w_branch_gmlp, w_branch_hg=w_branch_hg, w_out=w_out, norm2_g=norm2_g, w_ffn_in=w_ffn_in, w_ffn_out=w_ffn_out, final_norm_g=final_norm_g, loss_target=loss_target, m_w_ada=m_w_ada, m_b_ada=m_b_ada, m_norm1_g=m_norm1_g, m_w_in=m_w_in, m_b_gate=m_b_gate, m_gmlp_ln_g=m_gmlp_ln_g, m_gmlp_ln_b=m_gmlp_ln_b, m_gmlp_ws=m_gmlp_ws, m_gmlp_bs=m_gmlp_bs, m_hg_lb=m_hg_lb, m_hg_norm_g=m_hg_norm_g, m_w_branch_gmlp=m_w_branch_gmlp, m_w_branch_hg=m_w_branch_hg, m_w_out=m_w_out, m_norm2_g=m_norm2_g, m_w_ffn_in=m_w_ffn_in, m_w_ffn_out=m_w_ffn_out, m_final_norm_g=m_final_norm_g, v_w_ada=v_w_ada, v_b_ada=v_b_ada, v_norm1_g=v_norm1_g, v_w_in=v_w_in, v_b_gate=v_b_gate, v_gmlp_ln_g=v_gmlp_ln_g, v_gmlp_ln_b=v_gmlp_ln_b, v_gmlp_ws=v_gmlp_ws, v_gmlp_bs=v_gmlp_bs, v_hg_lb=v_hg_lb, v_hg_norm_g=v_hg_norm_g, v_w_branch_gmlp=v_w_branch_gmlp, v_w_branch_hg=v_w_branch_hg, v_w_out=v_w_out, v_norm2_g=v_norm2_g, v_w_ffn_in=v_w_ffn_in, v_w_ffn_out=v_w_ffn_out, v_final_norm_g=v_final_norm_g)
    weights = {n: given[n] for n in TWIN_WEIGHTS}
    shared = {n: given[n] for n in SHARED_INPUTS}
    per_example = {n: given[n] for n in ['x', 'c']}
    grad_fn = _jax.value_and_grad(_loss, argnums=(0, 1))

    def one_microbatch(ex, loss_target):
        ex = dict(ex)
        diff = ex.pop(TWIN_DIFF_INPUT)
        return grad_fn(weights, diff, {**shared, **ex}, loss_target)

    if N_MICROBATCH == 1:
        loss, (grad_w, grad_x) = one_microbatch(per_example, given["loss_target"])
    else:
        def body(carry, xs):
            loss_sum, grad_sum = carry
            l_k, (gw_k, gx_k) = one_microbatch(xs[0], xs[1])
            with _jax.named_scope("update"):
                return (loss_sum + l_k, _jax.tree.map(_jnp.add, grad_sum, gw_k)), gx_k

        init = (_jnp.zeros((), _jnp.float32), _jax.tree.map(_jnp.zeros_like, weights))
        (loss, grad_w), grad_x = _jax.lax.scan(body, init, (per_example, given["loss_target"]))
    with _jax.named_scope("update"):
        delta_w, new_m, new_v = {}, {}, {}
        for n in TWIN_WEIGHTS:
            delta_w[n], new_m[n], new_v[n] = _adamw(weights[n], grad_w[n], given["m_" + n], given["v_" + n])
    return (loss, grad_x, *[grad_w[n] for n in TWIN_WEIGHTS], *[delta_w[n] for n in TWIN_WEIGHTS],
            *[new_m[n] for n in TWIN_WEIGHTS], *[new_v[n] for n in TWIN_WEIGHTS])
```

```python
import functools
import math

import jax
import jax.numpy as jnp
from jax import lax
from jax.experimental import pallas as pl
from jax.experimental.pallas import tpu as pltpu

F32 = jnp.float32
BF16 = jnp.bfloat16
N_DEV = 8
EPS = 1e-6
LANES = 128
HG_DK = 128
HG_CHUNK = 64
HG_MID = HG_CHUNK // 2 - 1
EXP_CLAMP = 80.0
VMEM_LIMIT = 48 * 1024 * 1024
ADAM_LR, ADAM_B1, ADAM_B2, ADAM_EPS, ADAM_WD, ADAM_STEP = 0.001, 0.9, 0.999, 1e-08, 0.01, 10
MESH = pl.DeviceIdType.MESH

_NN = (((1,), (0,)), ((), ()))
_NT = (((1,), (1,)), ((), ()))
_TN = (((0,), (0,)), ((), ()))


def _dot(a, b, dims=_NN):
    return lax.dot_general(a.astype(BF16), b.astype(BF16), dims, preferred_element_type=F32)


def _tile(n, target, mult=LANES):
    best = None
    for t in range(mult, min(n, target) + 1, mult):
        if n % t == 0:
            best = t
    return n if best is None else best


def _cparams(sem):
    return pltpu.CompilerParams(dimension_semantics=sem, vmem_limit_bytes=VMEM_LIMIT)


def _sigmoid(x):
    return 1.0 / (1.0 + jnp.exp(-x))


def _gelu_parts(x):
    k0 = math.sqrt(2.0 / math.pi)
    x2 = x * x
    t = jnp.tanh(k0 * (x + 0.044715 * x * x2))
    g = 0.5 * x * (1.0 + t)
    dg = 0.5 * (1.0 + t) + 0.5 * x * (1.0 - t * t) * (k0 * (1.0 + 3.0 * 0.044715 * x2))
    return g, dg


def _split3(x):
    h = x.astype(BF16)
    r = x - h.astype(F32)
    m = r.astype(BF16)
    lo = (r - m.astype(F32)).astype(BF16)
    return h, m, lo


def _ones_dot(mat01, x):
    h, m, lo = _split3(x)
    d = functools.partial(lax.dot_general, dimension_numbers=_NN, preferred_element_type=F32)
    return d(mat01, h) + d(mat01, m) + d(mat01, lo)


def _matmul(name, a, b, *, dims, grid_mnk, tiles, a_spec, b_spec, extras=(), extra_specs=(), out_shapes, out_specs, epilogue):
    gm, gn, nk = grid_mnk
    tm, tn = tiles
    n_ex, n_out = len(extras), len(out_shapes)

    def body(*refs):
        a_ref, b_ref = refs[0], refs[1]
        ex = refs[2:2 + n_ex]
        outs = refs[2 + n_ex:2 + n_ex + n_out]
        if nk == 1:
            epilogue(lax.dot_general(a_ref[...], b_ref[...], dims, preferred_element_type=F32), ex, outs)
            return
        acc = refs[-1]
        k = pl.program_id(2)

        @pl.when(k == 0)
        def _():
            acc[...] = jnp.zeros_like(acc)

        acc[...] += lax.dot_general(a_ref[...], b_ref[...], dims, preferred_element_type=F32)

        @pl.when(k == nk - 1)
        def _():
            epilogue(acc[...], ex, outs)

    return pl.pallas_call(
        body, name=name, grid=(gm, gn, nk), in_specs=[a_spec, b_spec, *extra_specs], out_specs=list(out_specs),
        out_shape=list(out_shapes), scratch_shapes=[] if nk == 1 else [pltpu.VMEM((tm, tn), F32)],
        compiler_params=_cparams(("parallel", "parallel", "arbitrary")),
    )(a, b, *extras)


def _store(dtype):
    def ep(acc, ex, outs):
        outs[0][...] = acc.astype(dtype)
    return ep


def _mm_nn_stacked(name, a, wg, *, tm, tn, tk, out_dtype=F32, extras=(), extra_specs=(), out_shapes=None, out_specs=None, epilogue=None):
    M, K = a.shape
    _, _, nloc = wg.shape
    N = nloc * N_DEV
    q = nloc // tn
    if out_shapes is None:
        out_shapes = [jax.ShapeDtypeStruct((M, N), out_dtype)]
        out_specs = [pl.BlockSpec((tm, tn), lambda i, j, k: (i, j))]
        epilogue = _store(out_dtype)
    return _matmul(
        name, a, wg, dims=_NN, grid_mnk=(M // tm, N // tn, K // tk), tiles=(tm, tn),
        a_spec=pl.BlockSpec((tm, tk), lambda i, j, k: (i, k)),
        b_spec=pl.BlockSpec((None, tk, tn), lambda i, j, k: (j // q, k, j % q)),
        extras=extras, extra_specs=extra_specs, out_shapes=out_shapes, out_specs=out_specs, epilogue=epilogue)


def _mm_nt_stacked(name, a_spec, a, wg, *, M, tm, tn, tk, out_dtype=F32):
    _, Kw, nloc = wg.shape
    q = nloc // tk
    return _matmul(
        name, a, wg, dims=_NT, grid_mnk=(M // tm, Kw // tn, (nloc * N_DEV) // tk), tiles=(tm, tn),
        a_spec=a_spec, b_spec=pl.BlockSpec((None, tn, tk), lambda i, j, k: (k // q, j, k % q)),
        out_shapes=[jax.ShapeDtypeStruct((M, Kw), out_dtype)], out_specs=[pl.BlockSpec((tm, tn), lambda i, j, k: (i, j))],
        epilogue=_store(out_dtype))[0]


def _mm_tn(name, a, b, b_spec, *, Mo, No, S, tm, tn, tk, stacked_nloc=None):
    if stacked_nloc is None:
        out_shape = jax.ShapeDtypeStruct((Mo, No), BF16)
        out_spec = pl.BlockSpec((tm, tn), lambda i, j, k: (i, j))
    else:
        q = stacked_nloc // tn
        out_shape = jax.ShapeDtypeStruct((N_DEV, Mo, stacked_nloc), BF16)
        out_spec = pl.BlockSpec((None, tm, tn), lambda i, j, k: (j // q, i, j % q))
    return _matmul(
        name, a, b, dims=_TN, grid_mnk=(Mo // tm, No // tn, S // tk), tiles=(tm, tn),
        a_spec=pl.BlockSpec((tk, tm), lambda i, j, k: (k, i)), b_spec=b_spec,
        out_shapes=[out_shape], out_specs=[out_spec], epilogue=_store(BF16))[0]


def _norm_mod(name, x, g, sc, sh):
    S, D = x.shape
    tm = _tile(S, 256, 8)

    def body(x_ref, g_ref, sc_ref, sh_ref, h_ref):
        xv = x_ref[...]
        r = lax.rsqrt(jnp.mean(xv * xv, axis=-1, keepdims=True) + EPS)
        h = (xv * r) * g_ref[...]
        h_ref[...] = (h * (1.0 + sc_ref[...]) + sh_ref[...]).astype(BF16)

    row = pl.BlockSpec((tm, D), lambda i: (i, 0))
    vec = pl.BlockSpec((1, D), lambda i: (0, 0))
    return pl.pallas_call(body, name=name, grid=(S // tm,), in_specs=[row, vec, vec, vec], out_specs=row,
                          out_shape=jax.ShapeDtypeStruct((S, D), BF16), compiler_params=_cparams(("parallel",)))(x, g, sc, sh)


def _norm_mod_bwd(name, dh, x, g, sc, dres, o=None, gt=None):
    S, D = x.shape
    tm = _tile(S, 256, 8)
    gated = o is not None

    def body(*refs):
        if gated:
            dh_ref, x_ref, g_ref, sc_ref, dres_ref, o_ref, gt_ref, dx_ref, vec_ref, do_ref = refs
        else:
            dh_ref, x_ref, g_ref, sc_ref, dres_ref, dx_ref, vec_ref = refs
        i = pl.program_id(0)

        @pl.when(i == 0)
        def _():
            vec_ref[...] = jnp.zeros_like(vec_ref)

        xv, dh_v, gv = x_ref[...], dh_ref[...], g_ref[...]
        r = lax.rsqrt(jnp.mean(xv * xv, axis=-1, keepdims=True) + EPS)
        xn = xv * r
        one_sc = 1.0 + sc_ref[...]
        vec_ref[0:1, :] += jnp.sum(dh_v, axis=0, keepdims=True)
        vec_ref[1:2, :] += jnp.sum(dh_v * (xn * gv), axis=0, keepdims=True)
        vec_ref[2:3, :] += jnp.sum(dh_v * one_sc * xn, axis=0, keepdims=True)
        dxn = dh_v * one_sc * gv
        dx = dres_ref[...] + r * (dxn - xn * jnp.mean(dxn * xn, axis=-1, keepdims=True))
        dx_ref[...] = dx
        if gated:
            vec_ref[3:4, :] += jnp.sum(dx * o_ref[...], axis=0, keepdims=True)
            do_ref[...] = (dx * gt_ref[...]).astype(BF16)

    row = pl.BlockSpec((tm, D), lambda i: (i, 0))
    vec = pl.BlockSpec((1, D), lambda i: (0, 0))
    acc = pl.BlockSpec((8, D), lambda i: (0, 0))
    ins = [dh, x, g, sc, dres] + ([o, gt] if gated else [])
    in_specs = [row, row, vec, vec, row] + ([row, vec] if gated else [])
    out_shape = [jax.ShapeDtypeStruct((S, D), F32), jax.ShapeDtypeStruct((8, D), F32)]
    out_specs = [row, acc]
    if gated:
        out_shape.append(jax.ShapeDtypeStruct((S, D), BF16))
        out_specs.append(row)
    return pl.pallas_call(body, name=name, grid=(S // tm,), in_specs=in_specs, out_specs=out_specs, out_shape=out_shape,
                          compiler_params=_cparams(("arbitrary",)))(*ins)


def _loss_head(x3, tgt, gf, o2, gt2):
    S, D = x3.shape
    tm = _tile(S, 256, 8)

    def body(x_ref, t_ref, g_ref, o_ref, gt_ref, dx_ref, do_ref, vec_ref):
        i = pl.program_id(0)

        @pl.when(i == 0)
        def _():
            vec_ref[...] = jnp.zeros_like(vec_ref)

        xv, gv = x_ref[...], g_ref[...]
        r = lax.rsqrt(jnp.mean(xv * xv, axis=-1, keepdims=True) + EPS)
        xn = xv * r
        e = xn * gv - t_ref[...]
        tok = 0.5 * jnp.mean(e * e, axis=-1, keepdims=True)
        vec_ref[0:1, :] += jnp.broadcast_to(jnp.sum(tok, axis=0, keepdims=True), (1, D))
        dy = e * (1.0 / D)
        vec_ref[1:2, :] += jnp.sum(dy * xn, axis=0, keepdims=True)
        dxn = dy * gv
        dx = r * (dxn - xn * jnp.mean(dxn * xn, axis=-1, keepdims=True))
        dx_ref[...] = dx
        vec_ref[2:3, :] += jnp.sum(dx * o_ref[...], axis=0, keepdims=True)
        do_ref[...] = (dx * gt_ref[...]).astype(BF16)

    row = pl.BlockSpec((tm, D), lambda i: (i, 0))
    vec = pl.BlockSpec((1, D), lambda i: (0, 0))
    return pl.pallas_call(
        body, name="loss_head", grid=(S // tm,), in_specs=[row, row, vec, row, vec],
        out_specs=[row, row, pl.BlockSpec((8, D), lambda i: (0, 0))],
        out_shape=[jax.ShapeDtypeStruct((S, D), F32), jax.ShapeDtypeStruct((S, D), BF16), jax.ShapeDtypeStruct((8, D), F32)],
        compiler_params=_cparams(("arbitrary",)))(x3, tgt, gf, o2, gt2)


def _swiglu(ab, F, tf):
    S = ab.shape[0]
    tm = _tile(S, 512, 8)
    nf = F // tf

    def body(a_ref, u_ref, h_ref):
        a = a_ref[...]
        h_ref[...] = (a * _sigmoid(a) * u_ref[...]).astype(BF16)

    return pl.pallas_call(
        body, name="swiglu", grid=(S // tm, nf),
        in_specs=[pl.BlockSpec((tm, tf), lambda i, j: (i, j)), pl.BlockSpec((tm, tf), lambda i, j: (i, j + nf))],
        out_specs=pl.BlockSpec((tm, tf), lambda i, j: (i, j)), out_shape=jax.ShapeDtypeStruct((S, F), BF16),
        compiler_params=_cparams(("parallel", "parallel")))(ab, ab)


def _colsum2(dg2):
    _, S, D = dg2.shape
    tm = _tile(S, 256, 16)

    def body(x_ref, o_ref):
        @pl.when(pl.program_id(0) == 0)
        def _():
            o_ref[...] = jnp.zeros_like(o_ref)

        o_ref[0:1, :] += jnp.sum(x_ref[0].astype(F32), axis=0, keepdims=True)
        o_ref[1:2, :] += jnp.sum(x_ref[1].astype(F32), axis=0, keepdims=True)

    return pl.pallas_call(body, name="gate_bias_grad", grid=(S // tm,), in_specs=[pl.BlockSpec((2, tm, D), lambda i: (0, i, 0))],
                          out_specs=pl.BlockSpec((2, D), lambda i: (0, 0)), out_shape=jax.ShapeDtypeStruct((2, D), F32),
                          compiler_params=_cparams(("arbitrary",)))(dg2)


def _gmlp_common(u_ref, v_ref, lg_ref, lb_ref, ws_ref, bsb_ref, G, T, Dg):
    ug, dug = _gelu_parts(u_ref[...])
    vg, dvg = _gelu_parts(v_ref[...])
    mu = jnp.mean(vg, axis=-1, keepdims=True)
    vc = vg - mu
    rstd = lax.rsqrt(jnp.mean(vc * vc, axis=-1, keepdims=True) + EPS)
    vhat = vc * rstd
    vn = vhat * lg_ref[...] + lb_ref[...]
    row = lax.broadcasted_iota(jnp.int32, (T, T), 0)
    col = lax.broadcasted_iota(jnp.int32, (T, T), 1)
    tril = row >= col
    s = []
    for g in range(G):
        w = jnp.where(tril, ws_ref[g], 0.0)
        s.append(_dot(w, vn[:, g * Dg:(g + 1) * Dg]) + bsb_ref[g])
    return ug, dug, dvg, rstd, vhat, vn, tril, s


def _gmlp_fwd(z, ln_g, ln_b, ws, bsb, GW):
    S = z.shape[0]
    G, T, _ = ws.shape
    Dg = GW // G

    def body(u_ref, v_ref, lg_ref, lb_ref, ws_ref, bsb_ref, ya_ref):
        ug, _, _, _, _, _, _, s = _gmlp_common(u_ref, v_ref, lg_ref, lb_ref, ws_ref, bsb_ref, G, T, Dg)
        for g in range(G):
            sl = slice(g * Dg, (g + 1) * Dg)
            ya_ref[:, sl] = (ug[:, sl] * s[g]).astype(BF16)

    vec = pl.BlockSpec((1, GW), lambda c: (0, 0))
    return pl.pallas_call(
        body, name="gmlp_fwd", grid=(S // T,),
        in_specs=[pl.BlockSpec((T, GW), lambda c: (c, 0)), pl.BlockSpec((T, GW), lambda c: (c, 1)), vec, vec,
                  pl.BlockSpec((G, T, T), lambda c: (0, 0, 0)), pl.BlockSpec((G, T, Dg), lambda c: (0, 0, 0))],
        out_specs=pl.BlockSpec((T, GW), lambda c: (c, 0)), out_shape=jax.ShapeDtypeStruct((S, GW), BF16),
        compiler_params=_cparams(("parallel",)))(z, z, ln_g, ln_b, ws, bsb)


def _gmlp_bwd(z, dya, ln_g, ln_b, ws, bsb, GW):
    S = z.shape[0]
    G, T, _ = ws.shape
    Dg = GW // G
    nc = S // T

    def body(u_ref, v_ref, dya_ref, lg_ref, lb_ref, ws_ref, bsb_ref, dz_ref, dln_ref, dws_ref, dbs_ref, dbs_acc, dvh):
        c = pl.program_id(0)

        @pl.when(c == 0)
        def _():
            dln_ref[...] = jnp.zeros_like(dln_ref)
            dws_ref[...] = jnp.zeros_like(dws_ref)
            dbs_acc[...] = jnp.zeros_like(dbs_acc)

        ug, dug, dvg, rstd, vhat, vn, tril, s = _gmlp_common(u_ref, v_ref, lg_ref, lb_ref, ws_ref, bsb_ref, G, T, Dg)
        dya_v = dya_ref[...]
        for g in range(G):
            sl = slice(g * Dg, (g + 1) * Dg)
            dy_g = dya_v[:, sl]
            dz_ref[:, sl] = (dy_g * s[g] * dug[:, sl]).astype(BF16)
            ds = dy_g * ug[:, sl]
            dbs_acc[g] += ds
            w = jnp.where(tril, ws_ref[g], 0.0)
            dvn_g = _dot(w, ds, _TN)
            dws_ref[g] += jnp.where(tril, _dot(ds, vn[:, sl], _NT), 0.0)
            dln_ref[0:1, sl] += jnp.sum(dvn_g * vhat[:, sl], axis=0, keepdims=True)
            dln_ref[1:2, sl] += jnp.sum(dvn_g, axis=0, keepdims=True)
            dvh[:, sl] = dvn_g * lg_ref[:, sl]
        dvhat = dvh[...]
        m1 = jnp.mean(dvhat, axis=-1, keepdims=True)
        m2 = jnp.mean(dvhat * vhat, axis=-1, keepdims=True)
        dz_ref[:, GW:2 * GW] = (rstd * (dvhat - m1 - vhat * m2) * dvg).astype(BF16)

        @pl.when(c == nc - 1)
        def _():
            for g in range(G):
                dbs_ref[g] = jnp.sum(dbs_acc[g], axis=-1, keepdims=True)

    vec = pl.BlockSpec((1, GW), lambda c: (0, 0))
    return pl.pallas_call(
        body, name="gmlp_bwd", grid=(nc,),
        in_specs=[pl.BlockSpec((T, GW), lambda c: (c, 0)), pl.BlockSpec((T, GW), lambda c: (c, 1)),
                  pl.BlockSpec((T, GW), lambda c: (c, 0)), vec, vec,
                  pl.BlockSpec((G, T, T), lambda c: (0, 0, 0)), pl.BlockSpec((G, T, Dg), lambda c: (0, 0, 0))],
        out_specs=[pl.BlockSpec((T, 2 * GW), lambda c: (c, 0)), pl.BlockSpec((8, GW), lambda c: (0, 0)),
                   pl.BlockSpec((G, T, T), lambda c: (0, 0, 0)), pl.BlockSpec((G, T, 1), lambda c: (0, 0, 0))],
        out_shape=[jax.ShapeDtypeStruct((S, 2 * GW), BF16), jax.ShapeDtypeStruct((8, GW), F32),
                   jax.ShapeDtypeStruct((G, T, T), F32), jax.ShapeDtypeStruct((G, T, 1), F32)],
        scratch_shapes=[pltpu.VMEM((G, T, Dg), F32), pltpu.VMEM((T, GW), F32)],
        compiler_params=_cparams(("arbitrary",)))(z, z, dya, ln_g, ln_b, ws, bsb)


def _hg_common(q_ref, f_ref, hlb_ref):
    C = HG_CHUNK
    a = hlb_ref[...]
    lb = _sigmoid(a[0:1, :] - a[1:2, :])
    sig = _sigmoid(f_ref[...])
    f = lb + (1.0 - lb) * sig
    lf = jnp.log(f)
    kk = 1.0 - f
    q = q_ref[...]
    sq = _sigmoid(q)
    qa = q * sq
    row = lax.broadcasted_iota(jnp.int32, (C, C), 0)
    col = lax.broadcasted_iota(jnp.int32, (C, C), 1)
    tril = row >= col
    b = _ones_dot(tril.astype(BF16), lf)
    bm = b[HG_MID:HG_MID + 1, :]
    bl = b[C - 1:C, :]
    e_b = jnp.exp(b)
    e_qm = jnp.exp(jnp.minimum(b - bm, EXP_CLAMP))
    e_km = jnp.exp(jnp.minimum(bm - b, EXP_CLAMP))
    e_kl = jnp.exp(bl - b)
    return dict(lb=lb, sig=sig, f=f, kk=kk, q=q, sq=sq, qa=qa, tril=tril, e_b=e_b, e_qm=e_qm, e_km=e_km, e_kl=e_kl,
                e_l=jnp.exp(bl), qh=qa * e_b, qt=qa * e_qm, kt=kk * e_km, kh=kk * e_kl)


def _hg_fwd(z, hg_lb, ng, HW):
    S = z.shape[0]
    C, H, dk = HG_CHUNK, HW // HG_DK, HG_DK
    nc = S // C

    def body(q_ref, f_ref, i_ref, og_ref, hlb_ref, ng_ref, yb_ref, o_ref, st_ref, state):
        @pl.when(pl.program_id(0) == 0)
        def _():
            state[...] = jnp.zeros_like(state)

        t = _hg_common(q_ref, f_ref, hlb_ref)
        iv = i_ref[...]
        for h in range(H):
            sl = slice(h * dk, (h + 1) * dk)
            st = state[h]
            st_ref[h] = st
            a = jnp.where(t["tril"], _dot(t["qt"][:, sl], t["kt"][:, sl], _NT), 0.0)
            o_h = _dot(a, iv[:, sl]) + _dot(t["qh"][:, sl], st, _NT)
            state[h] = st * t["e_l"][:, sl] + _dot(iv[:, sl], t["kh"][:, sl], _TN)
            o_ref[:, sl] = o_h
            rr = lax.rsqrt(jnp.mean(o_h * o_h, axis=-1, keepdims=True) + EPS)
            og = og_ref[:, sl]
            yb_ref[:, sl] = (o_h * rr * ng_ref[:, sl] * (og * _sigmoid(og))).astype(BF16)

    def col(k):
        return pl.BlockSpec((C, HW), lambda c: (c, k))

    base = 2
    return pl.pallas_call(
        body, name="hgrn_fwd", grid=(nc,),
        in_specs=[col(base), col(base + 1), col(base + 2), col(base + 3),
                  pl.BlockSpec((2, HW), lambda c: (0, 0)), pl.BlockSpec((1, HW), lambda c: (0, 0))],
        out_specs=[pl.BlockSpec((C, HW), lambda c: (c, 0)), pl.BlockSpec((C, HW), lambda c: (c, 0)),
                   pl.BlockSpec((None, H, dk, dk), lambda c: (c, 0, 0, 0))],
        out_shape=[jax.ShapeDtypeStruct((S, HW), BF16), jax.ShapeDtypeStruct((S, HW), F32),
                   jax.ShapeDtypeStruct((nc, H, dk, dk), F32)],
        scratch_shapes=[pltpu.VMEM((H, dk, dk), F32)],
        compiler_params=_cparams(("arbitrary",)))(z, z, z, z, hg_lb, ng)


def _hg_bwd(z, o, states, dyb, hg_lb, ng, HW):
    S = z.shape[0]
    C, H, dk = HG_CHUNK, HW // HG_DK, HG_DK
    nc = S // C

    def body(q_ref, f_ref, i_ref, og_ref, o_ref, st_ref, stn_ref, dyb_ref, hlb_ref, ng_ref, dz_ref, dng_ref, dhlb_ref,
             dstate, cross, dqa_buf, dkk_buf, db_buf, dlb_acc):
        c = pl.program_id(0)

        @pl.when(c == 0)
        def _():
            dstate[...] = jnp.zeros_like(dstate)
            dlb_acc[...] = jnp.zeros_like(dlb_acc)
            dng_ref[...] = jnp.zeros_like(dng_ref)

        def r16(v):
            return v.astype(BF16).astype(F32)

        t = _hg_common(q_ref, f_ref, hlb_ref)
        iv = i_ref[...]
        for h in range(H):
            sl = slice(h * dk, (h + 1) * dk)
            o_h, og, dyb_h, ng_h = o_ref[:, sl], og_ref[:, sl], dyb_ref[:, sl], ng_ref[:, sl]
            sg = _sigmoid(og)
            silu_og = og * sg
            rr = lax.rsqrt(jnp.mean(o_h * o_h, axis=-1, keepdims=True) + EPS)
            on = o_h * rr
            dng_ref[0:1, sl] += jnp.sum(dyb_h * on * silu_og, axis=0, keepdims=True)
            dz_ref[:, 3 * HW + h * dk:3 * HW + (h + 1) * dk] = (dyb_h * on * ng_h * (sg * (1.0 + og * (1.0 - sg)))).astype(BF16)
            don = dyb_h * ng_h * silu_og
            do_h = rr * (don - on * jnp.mean(don * on, axis=-1, keepdims=True))

            qt, kt, qh, kh, iv_h = t["qt"][:, sl], t["kt"][:, sl], t["qh"][:, sl], t["kh"][:, sl], iv[:, sl]
            a = jnp.where(t["tril"], _dot(qt, kt, _NT), 0.0)
            da = jnp.where(t["tril"], _dot(do_h, iv_h, _NT), 0.0)
            st, dst = st_ref[h], dstate[h]
            cross[:, sl] = jnp.sum(dst * stn_ref[h], axis=0, keepdims=True)
            dqh = _dot(do_h, st)
            dstate[h] = _dot(do_h, qh, _TN) + dst * t["e_l"][:, sl]
            div = _dot(a, do_h, _TN) + _dot(kh, dst, _NT)
            dkh = _dot(iv_h, dst)
            dqt = _dot(da, kt)
            dkt = _dot(da, qt, _TN)
            dz_ref[:, 2 * HW + h * dk:2 * HW + (h + 1) * dk] = div.astype(BF16)
            dqa_buf[:, sl] = dqh * t["e_b"][:, sl] + dqt * t["e_qm"][:, sl]
            dkk_buf[:, sl] = dkt * t["e_km"][:, sl] + dkh * t["e_kl"][:, sl]
            db_buf[:, sl] = r16(qt) * dqt - r16(kt) * dkt + r16(qh) * dqh - r16(kh) * dkh

        dqa, dkk = dqa_buf[...], dkk_buf[...]
        triu = jnp.logical_not(t["tril"]) | (lax.broadcasted_iota(jnp.int32, (C, C), 0) == lax.broadcasted_iota(jnp.int32, (C, C), 1))
        dlf = _ones_dot(triu.astype(BF16), db_buf[...]) + cross[...]
        df = dlf / t["f"] - dkk
        sig, lb = t["sig"], t["lb"]
        dz_ref[:, HW:2 * HW] = (df * (1.0 - lb) * sig * (1.0 - sig)).astype(BF16)
        dlb_acc[...] += jnp.sum(df * (1.0 - sig), axis=0, keepdims=True)
        q, sq = t["q"], t["sq"]
        dz_ref[:, 0:HW] = (dqa * (sq * (1.0 + q * (1.0 - sq)))).astype(BF16)

        @pl.when(c == nc - 1)
        def _():
            da0 = dlb_acc[...] * lb * (1.0 - lb)
            dhlb_ref[0:1, :] = da0
            dhlb_ref[1:2, :] = -da0

    def col(k):
        return pl.BlockSpec((C, HW), lambda c: (nc - 1 - c, k))

    base = 2
    return pl.pallas_call(
        body, name="hgrn_bwd", grid=(nc,),
        in_specs=[col(base), col(base + 1), col(base + 2), col(base + 3), col(0),
                  pl.BlockSpec((None, H, dk, dk), lambda c: (nc - 1 - c, 0, 0, 0)),
                  pl.BlockSpec((None, H, dk, dk), lambda c: (jnp.minimum(nc - c, nc - 1), 0, 0, 0)), col(0),
                  pl.BlockSpec((2, HW), lambda c: (0, 0)), pl.BlockSpec((1, HW), lambda c: (0, 0))],
        out_specs=[pl.BlockSpec((C, 4 * HW), lambda c: (nc - 1 - c, 0)), pl.BlockSpec((8, HW), lambda c: (0, 0)),
                   pl.BlockSpec((2, HW), lambda c: (0, 0))],
        out_shape=[jax.ShapeDtypeStruct((S, 4 * HW), BF16), jax.ShapeDtypeStruct((8, HW), F32), jax.ShapeDtypeStruct((2, HW), F32)],
        scratch_shapes=[pltpu.VMEM((H, dk, dk), F32), pltpu.VMEM((1, HW), F32), pltpu.VMEM((C, HW), F32), pltpu.VMEM((C, HW), F32),
                        pltpu.VMEM((C, HW), F32), pltpu.VMEM((1, HW), F32)],
        compiler_params=_cparams(("arbitrary",)))(z, z, z, z, o, states, states, dyb, hg_lb, ng)


def _position():
    x, y, c = lax.axis_index("x"), lax.axis_index("y"), lax.axis_index("c")
    return x, y, c, 4 * x + 2 * y + c


def _flip(x, y, c, k):
    return (1 - x if k & 4 else x, 1 - y if k & 2 else y, 1 - c if k & 1 else c)


def _allgather_small(name, v):
    R, L = v.shape

    def body(v_ref, out_ref, send_sems, recv_sems):
        x, y, c, me = _position()
        out_ref[me] = v_ref[...]
        copies = []
        for k in range(1, N_DEV):
            cp = pltpu.make_async_remote_copy(src_ref=v_ref, dst_ref=out_ref.at[me], send_sem=send_sems.at[k - 1],
                                              recv_sem=recv_sems.at[k - 1], device_id=_flip(x, y, c, k), device_id_type=MESH)
            cp.start()
            copies.append(cp)
        for cp in copies:
            cp.wait()

    return pl.pallas_call(
        body, name=name, out_shape=jax.ShapeDtypeStruct((N_DEV, R, L), v.dtype),
        in_specs=[pl.BlockSpec(memory_space=pltpu.VMEM)], out_specs=pl.BlockSpec(memory_space=pltpu.VMEM),
        scratch_shapes=[pltpu.SemaphoreType.DMA((N_DEV - 1,)), pltpu.SemaphoreType.DMA((N_DEV - 1,))],
        compiler_params=pltpu.CompilerParams(vmem_limit_bytes=VMEM_LIMIT),
    )(v)


def _allgather_hbm(name, shards):
    n = len(shards)

    def body(*refs):
        ins, outs = refs[:n], refs[n:2 * n]
        send_sems, recv_sems, local_sems = refs[2 * n:]
        x, y, c, me = _position()
        sibling = (x, y, 1 - c)
        chips = [(1 - x, y), (x, 1 - y), (1 - x, 1 - y)]

        def slot(px, py, pc):
            return 4 * px + 2 * py + pc

        def copy(w, k, block, to, src=None):
            dst = outs[w].at[slot(*block)]
            return pltpu.make_async_remote_copy(src_ref=dst if src is None else src, dst_ref=dst, send_sem=send_sems.at[w, k],
                                                recv_sem=recv_sems.at[w, k], device_id=to, device_id_type=MESH)

        mine, first, passed = [], [], []
        for w in range(n):
            cp = pltpu.make_async_copy(ins[w], outs[w].at[me], local_sems.at[w])
            cp.start()
            mine.append(cp)
            for j, chip in enumerate(chips):
                first.append(copy(w, 1 + j, (x, y, c), (*chip, c), src=ins[w]))
            first.append(copy(w, 0, (x, y, c), sibling, src=ins[w]))
        for cp in first:
            cp.start()
        for w in range(n):
            for j, chip in enumerate(chips):
                copy(w, 1 + j, (*chip, c), (x, y, c)).wait_recv()
                cp = copy(w, 4 + j, (*chip, c), sibling)
                cp.start()
                passed.append(cp)
        for w in range(n):
            copy(w, 0, sibling, (x, y, c)).wait_recv()
            for j, chip in enumerate(chips):
                copy(w, 4 + j, (*chip, 1 - c), (x, y, c)).wait_recv()
        for cp in first + passed:
            cp.wait_send()
        for cp in mine:
            cp.wait()

    hbm = pl.BlockSpec(memory_space=pltpu.HBM)
    return pl.pallas_call(
        body, name=name, out_shape=[jax.ShapeDtypeStruct((N_DEV, *s.shape), s.dtype) for s in shards],
        in_specs=[hbm] * n, out_specs=[hbm] * n,
        scratch_shapes=[pltpu.SemaphoreType.DMA((n, 7)), pltpu.SemaphoreType.DMA((n, 7)), pltpu.SemaphoreType.DMA((n,))],
    )(*shards)


def _scatter_partials(name, grads):
    n = len(grads)

    def body(*refs):
        ins, outs = refs[:n], refs[n:2 * n]
        send_sems, recv_sems, local_sems = refs[2 * n:]
        x, y, c, me = _position()
        local, remote = [], []
        for w in range(n):
            cp = pltpu.make_async_copy(ins[w].at[me], outs[w].at[me], local_sems.at[w])
            cp.start()
            local.append(cp)
        for k in range(1, N_DEV):
            px, py, pc = _flip(x, y, c, k)
            peer = 4 * px + 2 * py + pc
            for w in range(n):
                cp = pltpu.make_async_remote_copy(src_ref=ins[w].at[peer], dst_ref=outs[w].at[me], send_sem=send_sems.at[w, k - 1],
                                                  recv_sem=recv_sems.at[w, k - 1], device_id=(px, py, pc), device_id_type=MESH)
                cp.start()
                remote.append(cp)
        for cp in remote:
            cp.wait()
        for cp in local:
            cp.wait()

    hbm = pl.BlockSpec(memory_space=pltpu.HBM)
    return pl.pallas_call(
        body, name=name, out_shape=[jax.ShapeDtypeStruct(g.shape, g.dtype) for g in grads],
        in_specs=[hbm] * n, out_specs=[hbm] * n,
        scratch_shapes=[pltpu.SemaphoreType.DMA((n, 7)), pltpu.SemaphoreType.DMA((n, 7)), pltpu.SemaphoreType.DMA((n,))],
    )(*grads)


def _ada_mod(c16, w):
    _, D = c16.shape
    n = w.shape[1]
    tk = _tile(D, 512)
    nk = D // tk

    def body(c_ref, w_ref, o_ref, ca_ref):
        @pl.when(pl.program_id(0) == 0)
        def _():
            o_ref[...] = jnp.zeros_like(o_ref)

        cv = c_ref[...]
        ca = cv * _sigmoid(cv)
        ca_ref[...] = ca
        o_ref[...] += _dot(ca, w_ref[...])

    return pl.pallas_call(
        body, name="ada_mod", grid=(nk,),
        in_specs=[pl.BlockSpec((16, tk), lambda k: (0, k)), pl.BlockSpec((tk, n), lambda k: (k, 0))],
        out_specs=[pl.BlockSpec((16, n), lambda k: (0, 0)), pl.BlockSpec((16, tk), lambda k: (0, k))],
        out_shape=[jax.ShapeDtypeStruct((16, n), F32), jax.ShapeDtypeStruct((16, D), F32)],
        compiler_params=_cparams(("arbitrary",)))(c16, w)


def _adam_math(w, g, m, v):
    m2 = ADAM_B1 * m + (1.0 - ADAM_B1) * g
    v2 = ADAM_B2 * v + (1.0 - ADAM_B2) * (g * g)
    m_hat = m2 / (1.0 - ADAM_B1 ** ADAM_STEP)
    v_hat = v2 / (1.0 - ADAM_B2 ** ADAM_STEP)
    delta = -ADAM_LR * (m_hat / (jnp.sqrt(v_hat) + ADAM_EPS) + ADAM_WD * w)
    return delta, m2, v2


def _adamw(name, w, m, v, parts):
    R, C = w.shape
    P = parts.shape[0]
    tr = _tile(R, max(16, 131072 // C), 16)

    def body(w_ref, m_ref, v_ref, p_ref, g_ref, d_ref, m2_ref, v2_ref):
        g = p_ref[0].astype(F32)
        for p in range(1, P):
            g = g + p_ref[p].astype(F32)
        delta, m2, v2 = _adam_math(w_ref[...], g, m_ref[...], v_ref[...])
        g_ref[...] = g
        d_ref[...] = delta
        m2_ref[...] = m2
        v2_ref[...] = v2

    blk = pl.BlockSpec((tr, C), lambda i: (i, 0))
    out = jax.ShapeDtypeStruct((R, C), F32)
    return pl.pallas_call(body, name=name, grid=(R // tr,), in_specs=[blk, blk, blk, pl.BlockSpec((P, tr, C), lambda i: (0, i, 0))],
                          out_specs=[blk] * 4, out_shape=[out] * 4, compiler_params=_cparams(("parallel",)))(w, m, v, parts)


def _small_update(gathered, w, m, v):
    _, R, L = gathered.shape
    rs = w.shape[0]

    def body(p_ref, w_ref, m_ref, v_ref, g_ref, d_ref, m2_ref, v2_ref):
        g = p_ref[0]
        for p in range(1, N_DEV):
            g = g + p_ref[p]
        g_ref[...] = g
        delta, m2, v2 = _adam_math(w_ref[...], g[0:rs, :], m_ref[...], v_ref[...])
        d_ref[...] = delta
        m2_ref[...] = m2
        v2_ref[...] = v2

    vm = pl.BlockSpec(memory_space=pltpu.VMEM)
    sm = jax.ShapeDtypeStruct((rs, L), F32)
    return pl.pallas_call(body, name="small_update", in_specs=[vm] * 4, out_specs=[vm] * 4,
                          out_shape=[jax.ShapeDtypeStruct((R, L), F32), sm, sm, sm],
                          compiler_params=pltpu.CompilerParams(vmem_limit_bytes=VMEM_LIMIT))(gathered, w, m, v)


def _local_step(x, tgt, mod, p, w):
    S, D = x.shape
    GW, HW = p["ln_g"].shape[1], p["hg_ng"].shape[1]
    G, T, _ = p["ws"].shape
    F = w["fo"].shape[0]
    in_loc, br_loc, fi_loc = w["in"].shape[2], w["bg"].shape[2], w["fi"].shape[2]
    INW = in_loc * N_DEV
    assert GW == HW and INW == 2 * GW + 4 * HW + 2 * D and fi_loc * N_DEV == 2 * F and F % fi_loc == 0
    sh1, sc1, gt1, sh2, sc2, gt2 = (mod[:, k * D:(k + 1) * D] for k in range(6))
    bsb = jnp.broadcast_to(p["bs"][:, :, None], (G, T, GW // G))

    tm = _tile(S, 1024, 16)
    tmh = _tile(S, 512, 16)
    tn_in = _tile(in_loc, 640)
    tn_d = _tile(D, 512)
    tn_br = _tile(br_loc, 512)
    tk_s = _tile(S, 1024)
    g_off = 2 * GW + 4 * HW

    h1 = _norm_mod("norm1", x, p["norm1_g"], sc1, sh1)
    z = _mm_nn_stacked("proj_in", h1, w["in"], tm=tm, tn=tn_in, tk=D)[0]
    ya = _gmlp_fwd(z, p["ln_g"], p["ln_b"], p["ws"], bsb, GW)
    yb, o_hg, states = _hg_fwd(z, p["hg_lb"], p["hg_ng"], HW)
    pa = _mm_nn_stacked("branch_gmlp", ya, w["bg"], tm=tm, tn=tn_br, tk=GW)[0]

    def gates(ga_ref, gb_ref, ba_ref, bb_ref):
        return _sigmoid(ga_ref[...] + ba_ref[...]), _sigmoid(gb_ref[...] + bb_ref[...])

    def gate_specs(tn_):
        o1, o2 = g_off // tn_, (g_off + D) // tn_
        return [pl.BlockSpec((tm, tn_), lambda i, j, k: (i, o1 + j)), pl.BlockSpec((tm, tn_), lambda i, j, k: (i, o2 + j)),
                pl.BlockSpec((1, tn_), lambda i, j, k: (0, j)), pl.BlockSpec((1, tn_), lambda i, j, k: (0, D // tn_ + j))]

    def merge_ep(acc, ex, outs):
        ga, gb = gates(*ex[1:5])
        outs[0][...] = acc
        outs[1][...] = (ga * ex[0][...] + gb * acc).astype(BF16)

    tile_o = pl.BlockSpec((tm, tn_br), lambda i, j, k: (i, j))
    pb, y = _mm_nn_stacked(
        "branch_hg_merge", yb, w["bh"], tm=tm, tn=tn_br, tk=HW, extras=[pa, z, z, p["b_gate"], p["b_gate"]],
        extra_specs=[tile_o, *gate_specs(tn_br)], out_shapes=[jax.ShapeDtypeStruct((S, D), F32), jax.ShapeDtypeStruct((S, D), BF16)],
        out_specs=[tile_o, tile_o], epilogue=merge_ep)

    def resid_ep(acc, ex, outs):
        outs[0][...] = acc
        outs[1][...] = ex[0][...] + ex[1][...] * acc

    def resid_mm(name, a, b, res, gt, tk):
        K = a.shape[1]
        t_o = pl.BlockSpec((tm, tn_d), lambda i, j, k: (i, j))
        return _matmul(
            name, a, b, dims=_NN, grid_mnk=(S // tm, D // tn_d, K // tk), tiles=(tm, tn_d),
            a_spec=pl.BlockSpec((tm, tk), lambda i, j, k: (i, k)), b_spec=pl.BlockSpec((tk, tn_d), lambda i, j, k: (k, j)),
            extras=[res, gt], extra_specs=[t_o, pl.BlockSpec((1, tn_d), lambda i, j, k: (0, j))],
            out_shapes=[jax.ShapeDtypeStruct((S, D), F32)] * 2, out_specs=[t_o, t_o], epilogue=resid_ep)

    o1, xm = resid_mm("proj_out", y, w["out"], x, gt1, D)
    h2 = _norm_mod("norm2", xm, p["norm2_g"], sc2, sh2)
    ab = _mm_nn_stacked("ffn_in", h2, w["fi"], tm=tm, tn=fi_loc, tk=D)[0]
    hf = _swiglu(ab, F, fi_loc)
    o2, x3 = resid_mm("ffn_out", hf, w["fo"], xm, gt2, _tile(F, 1536))
    dx3, do2, vec_l = _loss_head(x3, tgt, p["final_g"], o2, gt2)

    nf = F // fi_loc

    def dswiglu_ep(acc, ex, outs):
        a, up = ex[0][...], ex[1][...]
        sa = _sigmoid(a)
        outs[0][0] = (acc * up * (sa * (1.0 + a * (1.0 - sa)))).astype(BF16)
        outs[0][1] = (acc * (a * sa)).astype(BF16)

    dab = _matmul(
        "ffn_out_dx", do2, w["fo"], dims=_NT, grid_mnk=(S // tmh, nf, 1), tiles=(tmh, fi_loc),
        a_spec=pl.BlockSpec((tmh, D), lambda i, j, k: (i, 0)), b_spec=pl.BlockSpec((fi_loc, D), lambda i, j, k: (j, 0)),
        extras=[ab, ab], extra_specs=[pl.BlockSpec((tmh, fi_loc), lambda i, j, k: (i, j)), pl.BlockSpec((tmh, fi_loc), lambda i, j, k: (i, j + nf))],
        out_shapes=[jax.ShapeDtypeStruct((2, S, F), BF16)], out_specs=[pl.BlockSpec((2, tmh, fi_loc), lambda i, j, k: (0, i, j))],
        epilogue=dswiglu_ep)[0]
    tm_f = _tile(F, 512)
    g_fo = _mm_tn("ffn_out_dw", hf, do2, pl.BlockSpec((tk_s, tn_d), lambda i, j, k: (k, j)), Mo=F, No=D, S=S, tm=tm_f, tn=tn_d, tk=tk_s)
    dh2 = _mm_nt_stacked("ffn_in_dx", pl.BlockSpec((None, tm, fi_loc), lambda i, j, k: (k // nf, i, k % nf)), dab, w["fi"],
                         M=S, tm=tm, tn=tn_d, tk=fi_loc)
    g_fi = _mm_tn("ffn_in_dw", h2, dab, pl.BlockSpec((None, tk_s, fi_loc), lambda i, j, k: (j // nf, k, j % nf)),
                  Mo=D, No=2 * F, S=S, tm=tn_d, tn=fi_loc, tk=tk_s, stacked_nloc=fi_loc)
    dxm, vec2, do1 = _norm_mod_bwd("norm2_bwd", dh2, xm, p["norm2_g"], sc2, dx3, o1, gt1)

    def dmerge_ep(acc, ex, outs):
        ga, gb = gates(*ex[2:6])
        outs[0][...] = (acc * ga).astype(BF16)
        outs[1][...] = (acc * gb).astype(BF16)
        outs[2][0] = (acc * ex[0][...] * ga * (1.0 - ga)).astype(BF16)
        outs[2][1] = (acc * ex[1][...] * gb * (1.0 - gb)).astype(BF16)

    t_o = pl.BlockSpec((tm, tn_d), lambda i, j, k: (i, j))
    dpa, dpb, dg2 = _matmul(
        "proj_out_dx", do1, w["out"], dims=_NT, grid_mnk=(S // tm, D // tn_d, 1), tiles=(tm, tn_d),
        a_spec=pl.BlockSpec((tm, D), lambda i, j, k: (i, 0)), b_spec=pl.BlockSpec((tn_d, D), lambda i, j, k: (j, 0)),
        extras=[pa, pb, z, z, p["b_gate"], p["b_gate"]], extra_specs=[t_o, t_o, *gate_specs(tn_d)],
        out_shapes=[jax.ShapeDtypeStruct((S, D), BF16), jax.ShapeDtypeStruct((S, D), BF16), jax.ShapeDtypeStruct((2, S, D), BF16)],
        out_specs=[t_o, t_o, pl.BlockSpec((2, tm, tn_d), lambda i, j, k: (0, i, j))], epilogue=dmerge_ep)
    db_gate = _colsum2(dg2)
    g_out = _mm_tn("proj_out_dw", y, do1, pl.BlockSpec((tk_s, tn_d), lambda i, j, k: (k, j)), Mo=D, No=D, S=S, tm=tn_d, tn=tn_d, tk=tk_s)
    a_br = pl.BlockSpec((tm, br_loc), lambda i, j, k: (i, k))
    tn_g = _tile(GW, 512)
    dya = _mm_nt_stacked("branch_gmlp_dx", a_br, dpa, w["bg"], M=S, tm=tm, tn=tn_g, tk=br_loc)
    dyb = _mm_nt_stacked("branch_hg_dx", a_br, dpb, w["bh"], M=S, tm=tm, tn=tn_g, tk=br_loc)
    b_br = pl.BlockSpec((tk_s, br_loc), lambda i, j, k: (k, j))
    g_bg = _mm_tn("branch_gmlp_dw", ya, dpa, b_br, Mo=GW, No=D, S=S, tm=tn_g, tn=br_loc, tk=tk_s, stacked_nloc=br_loc)
    g_bh = _mm_tn("branch_hg_dw", yb, dpb, b_br, Mo=HW, No=D, S=S, tm=tn_g, tn=br_loc, tk=tk_s, stacked_nloc=br_loc)
    dz_gmlp, dln, dws, dbs = _gmlp_bwd(z, dya, p["ln_g"], p["ln_b"], p["ws"], bsb, GW)
    dz_hg, dng, dhlb = _hg_bwd(z, o_hg, states, dyb, p["hg_lb"], p["hg_ng"], HW)
    dz = jnp.concatenate([dz_gmlp, dz_hg, dg2[0], dg2[1]], axis=1)
    dh1 = _mm_nt_stacked("proj_in_dx", pl.BlockSpec((tm, tn_in), lambda i, j, k: (i, k)), dz, w["in"], M=S, tm=tm, tn=tn_d, tk=tn_in)
    g_in = _mm_tn("proj_in_dw", h1, dz, pl.BlockSpec((tk_s, tn_in), lambda i, j, k: (k, j)), Mo=D, No=INW, S=S,
                  tm=tn_d, tn=tn_in, tk=tk_s, stacked_nloc=in_loc)
    dx, vec1 = _norm_mod_bwd("norm1_bwd", dh1, x, p["norm1_g"], sc1, dxm)

    dmod = jnp.concatenate([vec1[0:1], vec1[1:2], vec2[3:4], vec2[0:1], vec2[1:2], vec_l[2:3]], axis=1)
    small = dict(norm1_g=vec1[2:3], b_gate=db_gate.reshape(1, 2 * D), ln_g=dln[0:1], ln_b=dln[1:2], ws=dws, bs=dbs.reshape(G, T),
                 hg_lb=dhlb, hg_ng=dng[0:1], norm2_g=vec2[2:3], final_g=vec_l[1:2], loss=vec_l[0:1, 0:LANES])
    big = dict(w_in=g_in, bg=g_bg, bh=g_bh, out=g_out, fi=g_fi, fo=g_fo)
    return dx, big, small, dmod


_SMALL = ("b_ada", "norm1_g", "b_gate", "ln_g", "ln_b", "ws", "bs", "hg_lb", "hg_ng", "norm2_g", "final_g")


def _pack(parts, rows_mult=8):
    flat = [a.reshape(-1) for a in parts]
    offs, n = [], 0
    for a in flat:
        offs.append(n)
        n += a.shape[0]
    pad = (-n) % (LANES * rows_mult)
    if pad:
        flat.append(jnp.zeros((pad,), F32))
    return jnp.concatenate(flat).reshape(-1, LANES), offs


def kernel(x, c, w_ada, b_ada, norm1_g, w_in, b_gate, gmlp_ln_g, gmlp_ln_b, gmlp_ws, gmlp_bs, hg_lb, hg_norm_g, w_branch_gmlp, w_branch_hg, w_out, norm2_g, w_ffn_in, w_ffn_out, final_norm_g, loss_target, m_w_ada, m_b_ada, m_norm1_g, m_w_in, m_b_gate, m_gmlp_ln_g, m_gmlp_ln_b, m_gmlp_ws, m_gmlp_bs, m_hg_lb, m_hg_norm_g, m_w_branch_gmlp, m_w_branch_hg, m_w_out, m_norm2_g, m_w_ffn_in, m_w_ffn_out, m_final_norm_g, v_w_ada, v_b_ada, v_norm1_g, v_w_in, v_b_gate, v_gmlp_ln_g, v_gmlp_ln_b, v_gmlp_ws, v_gmlp_bs, v_hg_lb, v_hg_norm_g, v_w_branch_gmlp, v_w_branch_hg, v_w_out, v_norm2_g, v_w_ffn_in, v_w_ffn_out, v_final_norm_g):
    S, D = x.shape[1], x.shape[2]
    ada_loc = w_ada.shape[2]
    me = 4 * lax.axis_index("x") + 2 * lax.axis_index("y") + lax.axis_index("c")

    c_all = _allgather_small("gather_c", c.reshape(D // LANES, LANES)).reshape(N_DEV, D)
    mod_cols, c_act = _ada_mod(jnp.pad(c_all, ((0, 16 - N_DEV), (0, 0))), w_ada[0])
    mod_all = _allgather_small("gather_mod", mod_cols[:N_DEV].reshape(-1, LANES)).reshape(N_DEV, N_DEV, ada_loc)
    mod = lax.dynamic_index_in_dim(mod_all, me, axis=1, keepdims=False).reshape(1, N_DEV * ada_loc) + b_ada

    shards = [a[0].astype(BF16) for a in (w_in, w_branch_gmlp, w_branch_hg, w_out, w_ffn_in, w_ffn_out)]
    g_in, g_bg, g_bh, g_out, g_fi, g_fo = _allgather_hbm("gather_weights", shards)
    w = {"in": g_in, "bg": g_bg, "bh": g_bh, "out": g_out.reshape(-1, D), "fi": g_fi, "fo": g_fo.reshape(-1, D)}
    p = dict(norm1_g=norm1_g, b_gate=b_gate, ln_g=gmlp_ln_g, ln_b=gmlp_ln_b, ws=gmlp_ws[0], bs=gmlp_bs[0], hg_lb=hg_lb,
             hg_ng=hg_norm_g, norm2_g=norm2_g, final_g=final_norm_g.reshape(1, D))

    grad_x, big, small, dmod = _local_step(x[0], loss_target[0], mod, p, w)

    small["b_ada"] = dmod
    packed, offs = _pack([small[k] for k in _SMALL] + [small["loss"]])
    gathered = _allgather_small("gather_small", packed)
    wp = dict(p, b_ada=b_ada)
    ms = dict(b_ada=m_b_ada, norm1_g=m_norm1_g, b_gate=m_b_gate, ln_g=m_gmlp_ln_g, ln_b=m_gmlp_ln_b, ws=m_gmlp_ws, bs=m_gmlp_bs,
              hg_lb=m_hg_lb, hg_ng=m_hg_norm_g, norm2_g=m_norm2_g, final_g=m_final_norm_g)
    vs = dict(b_ada=v_b_ada, norm1_g=v_norm1_g, b_gate=v_b_gate, ln_g=v_gmlp_ln_g, ln_b=v_gmlp_ln_b, ws=v_gmlp_ws, bs=v_gmlp_bs,
              hg_lb=v_hg_lb, hg_ng=v_hg_norm_g, norm2_g=v_norm2_g, final_g=v_final_norm_g)
    w_sm, _ = _pack([wp[k] for k in _SMALL])
    m_sm, _ = _pack([ms[k] for k in _SMALL])
    v_sm, _ = _pack([vs[k] for k in _SMALL])
    sm_out = _small_update(gathered, w_sm, m_sm, v_sm)
    shapes = dict(b_ada=b_ada.shape, norm1_g=norm1_g.shape, b_gate=b_gate.shape, ln_g=gmlp_ln_g.shape, ln_b=gmlp_ln_b.shape,
                  ws=gmlp_ws.shape, bs=gmlp_bs.shape, hg_lb=hg_lb.shape, hg_ng=hg_norm_g.shape, norm2_g=norm2_g.shape,
                  final_g=final_norm_g.shape)

    def unpack(arr, k):
        i = _SMALL.index(k)
        n = math.prod(shapes[k])
        return arr.reshape(-1)[offs[i]:offs[i] + n].reshape(shapes[k])

    loss = sm_out[0].reshape(-1)[offs[len(_SMALL)]]

    dmod_all = gathered.reshape(N_DEV, -1)[:, offs[0]:offs[0] + N_DEV * ada_loc]
    dmod_loc = lax.dynamic_slice_in_dim(dmod_all, me * ada_loc, ada_loc, axis=1)
    ca_t = jnp.pad(c_act[:N_DEV].T, ((0, 0), (0, LANES - N_DEV))).astype(BF16)
    dm_p = jnp.pad(dmod_loc, ((0, LANES - N_DEV), (0, 0))).astype(BF16)
    tm_a = _tile(D, 512)
    g_ada = _matmul(
        "ada_dw", ca_t, dm_p, dims=_NN, grid_mnk=(D // tm_a, 1, 1), tiles=(tm_a, ada_loc),
        a_spec=pl.BlockSpec((tm_a, LANES), lambda i, j, k: (i, 0)), b_spec=pl.BlockSpec((LANES, ada_loc), lambda i, j, k: (0, 0)),
        out_shapes=[jax.ShapeDtypeStruct((1, D, ada_loc), F32)], out_specs=[pl.BlockSpec((None, tm_a, ada_loc), lambda i, j, k: (0, i, 0))],
        epilogue=_store(F32))[0]

    g_out3 = big["out"].reshape(N_DEV, -1, D)
    g_fo3 = big["fo"].reshape(N_DEV, -1, D)
    recv = _scatter_partials("scatter_grads", [big["w_in"], big["bg"], big["bh"], g_out3, big["fi"], g_fo3])
    upd = {"w_ada": _adamw("adamw_w_ada", w_ada[0], m_w_ada[0], v_w_ada[0], g_ada)}
    for name, wt, mt, vt, part in (("w_in", w_in, m_w_in, v_w_in, recv[0]), ("w_branch_gmlp", w_branch_gmlp, m_w_branch_gmlp, v_w_branch_gmlp, recv[1]),
                                   ("w_branch_hg", w_branch_hg, m_w_branch_hg, v_w_branch_hg, recv[2]), ("w_out", w_out, m_w_out, v_w_out, recv[3]),
                                   ("w_ffn_in", w_ffn_in, m_w_ffn_in, v_w_ffn_in, recv[4]), ("w_ffn_out", w_ffn_out, m_w_ffn_out, v_w_ffn_out, recv[5])):
        upd[name] = _adamw("adamw_" + name, wt[0], mt[0], vt[0], part)

    order = ("w_ada", "b_ada", "norm1_g", "w_in", "b_gate", "ln_g", "ln_b", "ws", "bs", "hg_lb", "hg_ng", "w_branch_gmlp", "w_branch_hg",
             "w_out", "norm2_g", "w_ffn_in", "w_ffn_out", "final_g")
    outs = [loss, grad_x[None]]
    for idx in range(4):
        for k in order:
            outs.append(upd[k][idx][None] if k in upd else unpack(sm_out[idx], k))
    return tuple(outs)
```

```python
import functools
import math

import jax
import jax.numpy as jnp
from jax import lax
from jax.experimental import pallas as pl
from jax.experimental.pallas import tpu as pltpu

F32 = jnp.float32
BF16 = jnp.bfloat16
N_DEV = 8
EPS = 1e-6
LANES = 128
HG_DK = 128
HG_CHUNK = 64
HG_MID = HG_CHUNK // 2 - 1
EXP_CLAMP = 80.0
VMEM_LIMIT = 48 * 1024 * 1024
ADAM_LR, ADAM_B1, ADAM_B2, ADAM_EPS, ADAM_WD, ADAM_STEP = 0.001, 0.9, 0.999, 1e-08, 0.01, 10
MESH = pl.DeviceIdType.MESH

_NN = (((1,), (0,)), ((), ()))
_NT = (((1,), (1,)), ((), ()))
_TN = (((0,), (0,)), ((), ()))


def _dot(a, b, dims=_NN):
    return lax.dot_general(a.astype(BF16), b.astype(BF16), dims, preferred_element_type=F32)


def _tile(n, target, mult=LANES):
    best = None
    for t in range(mult, min(n, target) + 1, mult):
        if n % t == 0:
            best = t
    return n if best is None else best


def _cparams(sem):
    return pltpu.CompilerParams(dimension_semantics=sem, vmem_limit_bytes=VMEM_LIMIT)


def _sigmoid(x):
    return 1.0 / (1.0 + jnp.exp(-x))


def _gelu_parts(x):
    k0 = math.sqrt(2.0 / math.pi)
    x2 = x * x
    t = jnp.tanh(k0 * (x + 0.044715 * x * x2))
    g = 0.5 * x * (1.0 + t)
    dg = 0.5 * (1.0 + t) + 0.5 * x * (1.0 - t * t) * (k0 * (1.0 + 3.0 * 0.044715 * x2))
    return g, dg


def _split3(x):
    h = x.astype(BF16)
    r = x - h.astype(F32)
    m = r.astype(BF16)
    lo = (r - m.astype(F32)).astype(BF16)
    return h, m, lo


def _ones_dot(mat01, x):
    h, m, lo = _split3(x)
    d = functools.partial(lax.dot_general, dimension_numbers=_NN, preferred_element_type=F32)
    return d(mat01, h) + d(mat01, m) + d(mat01, lo)


def _matmul(name, a, b, *, dims, grid_mnk, tiles, a_spec, b_spec, extras=(), extra_specs=(), out_shapes, out_specs, epilogue):
    gm, gn, nk = grid_mnk
    tm, tn = tiles
    n_ex, n_out = len(extras), len(out_shapes)

    def body(*refs):
        a_ref, b_ref = refs[0], refs[1]
        ex = refs[2:2 + n_ex]
        outs = refs[2 + n_ex:2 + n_ex + n_out]
        if nk == 1:
            epilogue(lax.dot_general(a_ref[...], b_ref[...], dims, preferred_element_type=F32), ex, outs)
            return
        acc = refs[-1]
        k = pl.program_id(2)

        @pl.when(k == 0)
        def _():
            acc[...] = jnp.zeros_like(acc)

        acc[...] += lax.dot_general(a_ref[...], b_ref[...], dims, preferred_element_type=F32)

        @pl.when(k == nk - 1)
        def _():
            epilogue(acc[...], ex, outs)

    return pl.pallas_call(
        body, name=name, grid=(gm, gn, nk), in_specs=[a_spec, b_spec, *extra_specs], out_specs=list(out_specs),
        out_shape=list(out_shapes), scratch_shapes=[] if nk == 1 else [pltpu.VMEM((tm, tn), F32)],
        compiler_params=_cparams(("parallel", "parallel", "arbitrary")),
    )(a, b, *extras)


def _store(dtype):
    def ep(acc, ex, outs):
        outs[0][...] = acc.astype(dtype)
    return ep


def _mm_nn_stacked(name, a, wg, *, tm, tn, tk, out_dtype=F32, extras=(), extra_specs=(), out_shapes=None, out_specs=None, epilogue=None):
    M, K = a.shape
    _, _, nloc = wg.shape
    N = nloc * N_DEV
    q = nloc // tn
    if out_shapes is None:
        out_shapes = [jax.ShapeDtypeStruct((M, N), out_dtype)]
        out_specs = [pl.BlockSpec((tm, tn), lambda i, j, k: (i, j))]
        epilogue = _store(out_dtype)
    return _matmul(
        name, a, wg, dims=_NN, grid_mnk=(M // tm, N // tn, K // tk), tiles=(tm, tn),
        a_spec=pl.BlockSpec((tm, tk), lambda i, j, k: (i, k)),
        b_spec=pl.BlockSpec((None, tk, tn), lambda i, j, k: (j // q, k, j % q)),
        extras=extras, extra_specs=extra_specs, out_shapes=out_shapes, out_specs=out_specs, epilogue=epilogue)


def _mm_nt_stacked(name, a_spec, a, wg, *, M, tm, tn, tk, out_dtype=F32):
    _, Kw, nloc = wg.shape
    q = nloc // tk
    return _matmul(
        name, a, wg, dims=_NT, grid_mnk=(M // tm, Kw // tn, (nloc * N_DEV) // tk), tiles=(tm, tn),
        a_spec=a_spec, b_spec=pl.BlockSpec((None, tn, tk), lambda i, j, k: (k // q, j, k % q)),
        out_shapes=[jax.ShapeDtypeStruct((M, Kw), out_dtype)], out_specs=[pl.BlockSpec((tm, tn), lambda i, j, k: (i, j))],
        epilogue=_store(out_dtype))[0]


def _mm_tn(name, a, b, b_spec, *, Mo, No, S, tm, tn, tk, stacked_nloc=None):
    if stacked_nloc is None:
        out_shape = jax.ShapeDtypeStruct((Mo, No), BF16)
        out_spec = pl.BlockSpec((tm, tn), lambda i, j, k: (i, j))
    else:
        q = stacked_nloc // tn
        out_shape = jax.ShapeDtypeStruct((N_DEV, Mo, stacked_nloc), BF16)
        out_spec = pl.BlockSpec((None, tm, tn), lambda i, j, k: (j // q, i, j % q))
    return _matmul(
        name, a, b, dims=_TN, grid_mnk=(Mo // tm, No // tn, S // tk), tiles=(tm, tn),
        a_spec=pl.BlockSpec((tk, tm), lambda i, j, k: (k, i)), b_spec=b_spec,
        out_shapes=[out_shape], out_specs=[out_spec], epilogue=_store(BF16))[0]


def _norm_mod(name, x, g, sc, sh):
    S, D = x.shape
    tm = _tile(S, 256, 8)

    def body(x_ref, g_ref, sc_ref, sh_ref, h_ref):
        xv = x_ref[...]
        r = lax.rsqrt(jnp.mean(xv * xv, axis=-1, keepdims=True) + EPS)
        h = (xv * r) * g_ref[...]
        h_ref[...] = (h * (1.0 + sc_ref[...]) + sh_ref[...]).astype(BF16)

    row = pl.BlockSpec((tm, D), lambda i: (i, 0))
    vec = pl.BlockSpec((1, D), lambda i: (0, 0))
    return pl.pallas_call(body, name=name, grid=(S // tm,), in_specs=[row, vec, vec, vec], out_specs=row,
                          out_shape=jax.ShapeDtypeStruct((S, D), BF16), compiler_params=_cparams(("parallel",)))(x, g, sc, sh)


def _norm_mod_bwd(name, dh, x, g, sc, dres, o=None, gt=None):
    S, D = x.shape
    tm = _tile(S, 256, 8)
    gated = o is not None

    def body(*refs):
        if gated:
            dh_ref, x_ref, g_ref, sc_ref, dres_ref, o_ref, gt_ref, dx_ref, vec_ref, do_ref = refs
        else:
            dh_ref, x_ref, g_ref, sc_ref, dres_ref, dx_ref, vec_ref = refs
        i = pl.program_id(0)

        @pl.when(i == 0)
        def _():
            vec_ref[...] = jnp.zeros_like(vec_ref)

        xv, dh_v, gv = x_ref[...], dh_ref[...], g_ref[...]
        r = lax.rsqrt(jnp.mean(xv * xv, axis=-1, keepdims=True) + EPS)
        xn = xv * r
        one_sc = 1.0 + sc_ref[...]
        vec_ref[0:1, :] += jnp.sum(dh_v, axis=0, keepdims=True)
        vec_ref[1:2, :] += jnp.sum(dh_v * (xn * gv), axis=0, keepdims=True)
        vec_ref[2:3, :] += jnp.sum(dh_v * one_sc * xn, axis=0, keepdims=True)
        dxn = dh_v * one_sc * gv
        dx = dres_ref[...] + r * (dxn - xn * jnp.mean(dxn * xn, axis=-1, keepdims=True))
        dx_ref[...] = dx
        if gated:
            vec_ref[3:4, :] += jnp.sum(dx * o_ref[...], axis=0, keepdims=True)
            do_ref[...] = (dx * gt_ref[...]).astype(BF16)

    row = pl.BlockSpec((tm, D), lambda i: (i, 0))
    vec = pl.BlockSpec((1, D), lambda i: (0, 0))
    acc = pl.BlockSpec((8, D), lambda i: (0, 0))
    ins = [dh, x, g, sc, dres] + ([o, gt] if gated else [])
    in_specs = [row, row, vec, vec, row] + ([row, vec] if gated else [])
    out_shape = [jax.ShapeDtypeStruct((S, D), F32), jax.ShapeDtypeStruct((8, D), F32)]
    out_specs = [row, acc]
    if gated:
        out_shape.append(jax.ShapeDtypeStruct((S, D), BF16))
        out_specs.append(row)
    return pl.pallas_call(body, name=name, grid=(S // tm,), in_specs=in_specs, out_specs=out_specs, out_shape=out_shape,
                          compiler_params=_cparams(("arbitrary",)))(*ins)


def _loss_head(x3, tgt, gf, o2, gt2):
    S, D = x3.shape
    tm = _tile(S, 256, 8)

    def body(x_ref, t_ref, g_ref, o_ref, gt_ref, dx_ref, do_ref, vec_ref):
        i = pl.program_id(0)

        @pl.when(i == 0)
        def _():
            vec_ref[...] = jnp.zeros_like(vec_ref)

        xv, gv = x_ref[...], g_ref[...]
        r = lax.rsqrt(jnp.mean(xv * xv, axis=-1, keepdims=True) + EPS)
        xn = xv * r
        e = xn * gv - t_ref[...]
        tok = 0.5 * jnp.mean(e * e, axis=-1, keepdims=True)
        vec_ref[0:1, :] += jnp.broadcast_to(jnp.sum(tok, axis=0, keepdims=True), (1, D))
        dy = e * (1.0 / D)
        vec_ref[1:2, :] += jnp.sum(dy * xn, axis=0, keepdims=True)
        dxn = dy * gv
        dx = r * (dxn - xn * jnp.mean(dxn * xn, axis=-1, keepdims=True))
        dx_ref[...] = dx
        vec_ref[2:3, :] += jnp.sum(dx * o_ref[...], axis=0, keepdims=True)
        do_ref[...] = (dx * gt_ref[...]).astype(BF16)

    row = pl.BlockSpec((tm, D), lambda i: (i, 0))
    vec = pl.BlockSpec((1, D), lambda i: (0, 0))
    return pl.pallas_call(
        body, name="loss_head", grid=(S // tm,), in_specs=[row, row, vec, row, vec],
        out_specs=[row, row, pl.BlockSpec((8, D), lambda i: (0, 0))],
        out_shape=[jax.ShapeDtypeStruct((S, D), F32), jax.ShapeDtypeStruct((S, D), BF16), jax.ShapeDtypeStruct((8, D), F32)],
        compiler_params=_cparams(("arbitrary",)))(x3, tgt, gf, o2, gt2)


def _swiglu(ab, F, tf):
    S = ab.shape[0]
    tm = _tile(S, 512, 8)
    nf = F // tf

    def body(a_ref, u_ref, h_ref):
        a = a_ref[...]
        h_ref[...] = (a * _sigmoid(a) * u_ref[...]).astype(BF16)

    return pl.pallas_call(
        body, name="swiglu", grid=(S // tm, nf),
        in_specs=[pl.BlockSpec((tm, tf), lambda i, j: (i, j)), pl.BlockSpec((tm, tf), lambda i, j: (i, j + nf))],
        out_specs=pl.BlockSpec((tm, tf), lambda i, j: (i, j)), out_shape=jax.ShapeDtypeStruct((S, F), BF16),
        compiler_params=_cparams(("parallel", "parallel")))(ab, ab)


def _colsum2(dg2):
    _, S, D = dg2.shape
    tm = _tile(S, 256, 16)

    def body(x_ref, o_ref):
        @pl.when(pl.program_id(0) == 0)
        def _():
            o_ref[...] = jnp.zeros_like(o_ref)

        o_ref[0:1, :] += jnp.sum(x_ref[0].astype(F32), axis=0, keepdims=True)
        o_ref[1:2, :] += jnp.sum(x_ref[1].astype(F32), axis=0, keepdims=True)

    return pl.pallas_call(body, name="gate_bias_grad", grid=(S // tm,), in_specs=[pl.BlockSpec((2, tm, D), lambda i: (0, i, 0))],
                          out_specs=pl.BlockSpec((2, D), lambda i: (0, 0)), out_shape=jax.ShapeDtypeStruct((2, D), F32),
                          compiler_params=_cparams(("arbitrary",)))(dg2)


def _gmlp_common(u_ref, v_ref, lg_ref, lb_ref, ws_ref, bsb_ref, G, T, Dg):
    ug, dug = _gelu_parts(u_ref[...])
    vg, dvg = _gelu_parts(v_ref[...])
    mu = jnp.mean(vg, axis=-1, keepdims=True)
    vc = vg - mu
    rstd = lax.rsqrt(jnp.mean(vc * vc, axis=-1, keepdims=True) + EPS)
    vhat = vc * rstd
    vn = vhat * lg_ref[...] + lb_ref[...]
    row = lax.broadcasted_iota(jnp.int32, (T, T), 0)
    col = lax.broadcasted_iota(jnp.int32, (T, T), 1)
    tril = row >= col
    s = []
    for g in range(G):
        w = jnp.where(tril, ws_ref[g], 0.0)
        s.append(_dot(w, vn[:, g * Dg:(g + 1) * Dg]) + bsb_ref[g])
    return ug, dug, dvg, rstd, vhat, vn, tril, s


def _gmlp_fwd(z, ln_g, ln_b, ws, bsb, GW):
    S = z.shape[0]
    G, T, _ = ws.shape
    Dg = GW // G

    def body(u_ref, v_ref, lg_ref, lb_ref, ws_ref, bsb_ref, ya_ref):
        ug, _, _, _, _, _, _, s = _gmlp_common(u_ref, v_ref, lg_ref, lb_ref, ws_ref, bsb_ref, G, T, Dg)
        for g in range(G):
            sl = slice(g * Dg, (g + 1) * Dg)
            ya_ref[:, sl] = (ug[:, sl] * s[g]).astype(BF16)

    vec = pl.BlockSpec((1, GW), lambda c: (0, 0))
    return pl.pallas_call(
        body, name="gmlp_fwd", grid=(S // T,),
        in_specs=[pl.BlockSpec((T, GW), lambda c: (c, 0)), pl.BlockSpec((T, GW), lambda c: (c, 1)), vec, vec,
                  pl.BlockSpec((G, T, T), lambda c: (0, 0, 0)), pl.BlockSpec((G, T, Dg), lambda c: (0, 0, 0))],
        out_specs=pl.BlockSpec((T, GW), lambda c: (c, 0)), out_shape=jax.ShapeDtypeStruct((S, GW), BF16),
        compiler_params=_cparams(("parallel",)))(z, z, ln_g, ln_b, ws, bsb)


def _gmlp_bwd(z, dya, ln_g, ln_b, ws, bsb, GW):
    S = z.shape[0]
    G, T, _ = ws.shape
    Dg = GW // G
    nc = S // T

    def body(u_ref, v_ref, dya_ref, lg_ref, lb_ref, ws_ref, bsb_ref, dz_ref, dln_ref, dws_ref, dbs_ref, dbs_acc, dvh):
        c = pl.program_id(0)

        @pl.when(c == 0)
        def _():
            dln_ref[...] = jnp.zeros_like(dln_ref)
            dws_ref[...] = jnp.zeros_like(dws_ref)
            dbs_acc[...] = jnp.zeros_like(dbs_acc)

        ug, dug, dvg, rstd, vhat, vn, tril, s = _gmlp_common(u_ref, v_ref, lg_ref, lb_ref, ws_ref, bsb_ref, G, T, Dg)
        dya_v = dya_ref[...]
        for g in range(G):
            sl = slice(g * Dg, (g + 1) * Dg)
            dy_g = dya_v[:, sl]
            dz_ref[:, sl] = (dy_g * s[g] * dug[:, sl]).astype(BF16)
            ds = dy_g * ug[:, sl]
            dbs_acc[g] += ds
            w = jnp.where(tril, ws_ref[g], 0.0)
            dvn_g = _dot(w, ds, _TN)
            dws_ref[g] += jnp.where(tril, _dot(ds, vn[:, sl], _NT), 0.0)
            dln_ref[0:1, sl] += jnp.sum(dvn_g * vhat[:, sl], axis=0, keepdims=True)
            dln_ref[1:2, sl] += jnp.sum(dvn_g, axis=0, keepdims=True)
            dvh[:, sl] = dvn_g * lg_ref[:, sl]
        dvhat = dvh[...]
        m1 = jnp.mean(dvhat, axis=-1, keepdims=True)
        m2 = jnp.mean(dvhat * vhat, axis=-1, keepdims=True)
        dz_ref[:, GW:2 * GW] = (rstd * (dvhat - m1 - vhat * m2) * dvg).astype(BF16)

        @pl.when(c == nc - 1)
        def _():
            for g in range(G):
                dbs_ref[g] = jnp.sum(dbs_acc[g], axis=-1, keepdims=True)

    vec = pl.BlockSpec((1, GW), lambda c: (0, 0))
    return pl.pallas_call(
        body, name="gmlp_bwd", grid=(nc,),
        in_specs=[pl.BlockSpec((T, GW), lambda c: (c, 0)), pl.BlockSpec((T, GW), lambda c: (c, 1)),
                  pl.BlockSpec((T, GW), lambda c: (c, 0)), vec, vec,
                  pl.BlockSpec((G, T, T), lambda c: (0, 0, 0)), pl.BlockSpec((G, T, Dg), lambda c: (0, 0, 0))],
        out_specs=[pl.BlockSpec((T, 2 * GW), lambda c: (c, 0)), pl.BlockSpec((8, GW), lambda c: (0, 0)),
                   pl.BlockSpec((G, T, T), lambda c: (0, 0, 0)), pl.BlockSpec((G, T, 1), lambda c: (0, 0, 0))],
        out_shape=[jax.ShapeDtypeStruct((S, 2 * GW), BF16), jax.ShapeDtypeStruct((8, GW), F32),
                   jax.ShapeDtypeStruct((G, T, T), F32), jax.ShapeDtypeStruct((G, T, 1), F32)],
        scratch_shapes=[pltpu.VMEM((G, T, Dg), F32), pltpu.VMEM((T, GW), F32)],
        compiler_params=_cparams(("arbitrary",)))(z, z, dya, ln_g, ln_b, ws, bsb)


def _hg_common(q_ref, f_ref, hlb_ref):
    C = HG_CHUNK
    a = hlb_ref[...]
    lb = _sigmoid(a[0:1, :] - a[1:2, :])
    sig = _sigmoid(f_ref[...])
    f = lb + (1.0 - lb) * sig
    lf = jnp.log(f)
    kk = 1.0 - f
    q = q_ref[...]
    sq = _sigmoid(q)
    qa = q * sq
    row = lax.broadcasted_iota(jnp.int32, (C, C), 0)
    col = lax.broadcasted_iota(jnp.int32, (C, C), 1)
    tril = row >= col
    b = _ones_dot(tril.astype(BF16), lf)
    bm = b[HG_MID:HG_MID + 1, :]
    bl = b[C - 1:C, :]
    e_b = jnp.exp(b)
    e_qm = jnp.exp(jnp.minimum(b - bm, EXP_CLAMP))
    e_km = jnp.exp(jnp.minimum(bm - b, EXP_CLAMP))
    e_kl = jnp.exp(bl - b)
    return dict(lb=lb, sig=sig, f=f, kk=kk, q=q, sq=sq, qa=qa, tril=tril, e_b=e_b, e_qm=e_qm, e_km=e_km, e_kl=e_kl,
                e_l=jnp.exp(bl), qh=qa * e_b, qt=qa * e_qm, kt=kk * e_km, kh=kk * e_kl)


def _hg_fwd(z, hg_lb, ng, HW):
    S = z.shape[0]
    C, H, dk = HG_CHUNK, HW // HG_DK, HG_DK
    nc = S // C

    def body(q_ref, f_ref, i_ref, og_ref, hlb_ref, ng_ref, yb_ref, o_ref, st_ref, state):
        @pl.when(pl.program_id(0) == 0)
        def _():
            state[...] = jnp.zeros_like(state)

        t = _hg_common(q_ref, f_ref, hlb_ref)
        iv = i_ref[...]
        for h in range(H):
            sl = slice(h * dk, (h + 1) * dk)
            st = state[h]
            st_ref[h] = st
            a = jnp.where(t["tril"], _dot(t["qt"][:, sl], t["kt"][:, sl], _NT), 0.0)
            o_h = _dot(a, iv[:, sl]) + _dot(t["qh"][:, sl], st, _NT)
            state[h] = st * t["e_l"][:, sl] + _dot(iv[:, sl], t["kh"][:, sl], _TN)
            o_ref[:, sl] = o_h
            rr = lax.rsqrt(jnp.mean(o_h * o_h, axis=-1, keepdims=True) + EPS)
            og = og_ref[:, sl]
            yb_ref[:, sl] = (o_h * rr * ng_ref[:, sl] * (og * _sigmoid(og))).astype(BF16)

    def col(k):
        return pl.BlockSpec((C, HW), lambda c: (c, k))

    base = 2
    return pl.pallas_call(
        body, name="hgrn_fwd", grid=(nc,),
        in_specs=[col(base), col(base + 1), col(base + 2), col(base + 3),
                  pl.BlockSpec((2, HW), lambda c: (0, 0)), pl.BlockSpec((1, HW), lambda c: (0, 0))],
        out_specs=[pl.BlockSpec((C, HW), lambda c: (c, 0)), pl.BlockSpec((C, HW), lambda c: (c, 0)),
                   pl.BlockSpec((None, H, dk, dk), lambda c: (c, 0, 0, 0))],
        out_shape=[jax.ShapeDtypeStruct((S, HW), BF16), jax.ShapeDtypeStruct((S, HW), F32),
                   jax.ShapeDtypeStruct((nc, H, dk, dk), F32)],
        scratch_shapes=[pltpu.VMEM((H, dk, dk), F32)],
        compiler_params=_cparams(("arbitrary",)))(z, z, z, z, hg_lb, ng)


def _hg_bwd(z, o, states, dyb, hg_lb, ng, HW):
    S = z.shape[0]
    C, H, dk = HG_CHUNK, HW // HG_DK, HG_DK
    nc = S // C

    def body(q_ref, f_ref, i_ref, og_ref, o_ref, st_ref, stn_ref, dyb_ref, hlb_ref, ng_ref, dz_ref, dng_ref, dhlb_ref,
             dstate, cross, dqa_buf, dkk_buf, db_buf, dlb_acc):
        c = pl.program_id(0)

        @pl.when(c == 0)
        def _():
            dstate[...] = jnp.zeros_like(dstate)
            dlb_acc[...] = jnp.zeros_like(dlb_acc)
            dng_ref[...] = jnp.zeros_like(dng_ref)

        def r16(v):
            return v.astype(BF16).astype(F32)

        t = _hg_common(q_ref, f_ref, hlb_ref)
        iv = i_ref[...]
        for h in range(H):
            sl = slice(h * dk, (h + 1) * dk)
            o_h, og, dyb_h, ng_h = o_ref[:, sl], og_ref[:, sl], dyb_ref[:, sl], ng_ref[:, sl]
            sg = _sigmoid(og)
            silu_og = og * sg
            rr = lax.rsqrt(jnp.mean(o_h * o_h, axis=-1, keepdims=True) + EPS)
            on = o_h * rr
            dng_ref[0:1, sl] += jnp.sum(dyb_h * on * silu_og, axis=0, keepdims=True)
            dz_ref[:, 3 * HW + h * dk:3 * HW + (h + 1) * dk] = (dyb_h * on * ng_h * (sg * (1.0 + og * (1.0 - sg)))).astype(BF16)
            don = dyb_h * ng_h * silu_og
            do_h = rr * (don - on * jnp.mean(don * on, axis=-1, keepdims=True))

            qt, kt, qh, kh, iv_h = t["qt"][:, sl], t["kt"][:, sl], t["qh"][:, sl], t["kh"][:, sl], iv[:, sl]
            a = jnp.where(t["tril"], _dot(qt, kt, _NT), 0.0)
            da = jnp.where(t["tril"], _dot(do_h, iv_h, _NT), 0.0)
            st, dst = st_ref[h], dstate[h]
            cross[:, sl] = jnp.sum(dst * stn_ref[h], axis=0, keepdims=True)
            dqh = _dot(do_h, st)
            dstate[h] = _dot(do_h, qh, _TN) + dst * t["e_l"][:, sl]
            div = _dot(a, do_h, _TN) + _dot(kh, dst, _NT)
            dkh = _dot(iv_h, dst)
            dqt = _dot(da, kt)
            dkt = _dot(da, qt, _TN)
            dz_ref[:, 2 * HW + h * dk:2 * HW + (h + 1) * dk] = div.astype(BF16)
            dqa_buf[:, sl] = dqh * t["e_b"][:, sl] + dqt * t["e_qm"][:, sl]
            dkk_buf[:, sl] = dkt * t["e_km"][:, sl] + dkh * t["e_kl"][:, sl]
            db_buf[:, sl] = r16(qt) * dqt - r16(kt) * dkt + r16(qh) * dqh - r16(kh) * dkh

        dqa, dkk = dqa_buf[...], dkk_buf[...]
        triu = jnp.logical_not(t["tril"]) | (lax.broadcasted_iota(jnp.int32, (C, C), 0) == lax.broadcasted_iota(jnp.int32, (C, C), 1))
        dlf = _ones_dot(triu.astype(BF16), db_buf[...]) + cross[...]
        df = dlf / t["f"] - dkk
        sig, lb = t["sig"], t["lb"]
        dz_ref[:, HW:2 * HW] = (df * (1.0 - lb) * sig * (1.0 - sig)).astype(BF16)
        dlb_acc[...] += jnp.sum(df * (1.0 - sig), axis=0, keepdims=True)
        q, sq = t["q"], t["sq"]
        dz_ref[:, 0:HW] = (dqa * (sq * (1.0 + q * (1.0 - sq)))).astype(BF16)

        @pl.when(c == nc - 1)
        def _():
            da0 = dlb_acc[...] * lb * (1.0 - lb)
            dhlb_ref[0:1, :] = da0
            dhlb_ref[1:2, :] = -da0

    def col(k):
        return pl.BlockSpec((C, HW), lambda c: (nc - 1 - c, k))

    base = 2
    return pl.pallas_call(
        body, name="hgrn_bwd", grid=(nc,),
        in_specs=[col(base), col(base + 1), col(base + 2), col(base + 3), col(0),
                  pl.BlockSpec((None, H, dk, dk), lambda c: (nc - 1 - c, 0, 0, 0)),
                  pl.BlockSpec((None, H, dk, dk), lambda c: (jnp.minimum(nc - c, nc - 1), 0, 0, 0)), col(0),
                  pl.BlockSpec((2, HW), lambda c: (0, 0)), pl.BlockSpec((1, HW), lambda c: (0, 0))],
        out_specs=[pl.BlockSpec((C, 4 * HW), lambda c: (nc - 1 - c, 0)), pl.BlockSpec((8, HW), lambda c: (0, 0)),
                   pl.BlockSpec((2, HW), lambda c: (0, 0))],
        out_shape=[jax.ShapeDtypeStruct((S, 4 * HW), BF16), jax.ShapeDtypeStruct((8, HW), F32), jax.ShapeDtypeStruct((2, HW), F32)],
        scratch_shapes=[pltpu.VMEM((H, dk, dk), F32), pltpu.VMEM((1, HW), F32), pltpu.VMEM((C, HW), F32), pltpu.VMEM((C, HW), F32),
                        pltpu.VMEM((C, HW), F32), pltpu.VMEM((1, HW), F32)],
        compiler_params=_cparams(("arbitrary",)))(z, z, z, z, o, states, states, dyb, hg_lb, ng)


def _position():
    x, y, c = lax.axis_index("x"), lax.axis_index("y"), lax.axis_index("c")
    return x, y, c, 4 * x + 2 * y + c


def _flip(x, y, c, k):
    return (1 - x if k & 4 else x, 1 - y if k & 2 else y, 1 - c if k & 1 else c)


def _allgather_small(name, v):
    R, L = v.shape

    def body(v_ref, out_ref, send_sems, recv_sems):
        x, y, c, me = _position()
        out_ref[me] = v_ref[...]
        copies = []
        for k in range(1, N_DEV):
            cp = pltpu.make_async_remote_copy(src_ref=v_ref, dst_ref=out_ref.at[me], send_sem=send_sems.at[k - 1],
                                              recv_sem=recv_sems.at[k - 1], device_id=_flip(x, y, c, k), device_id_type=MESH)
            cp.start()
            copies.append(cp)
        for cp in copies:
            cp.wait()

    return pl.pallas_call(
        body, name=name, out_shape=jax.ShapeDtypeStruct((N_DEV, R, L), v.dtype),
        in_specs=[pl.BlockSpec(memory_space=pltpu.VMEM)], out_specs=pl.BlockSpec(memory_space=pltpu.VMEM),
        scratch_shapes=[pltpu.SemaphoreType.DMA((N_DEV - 1,)), pltpu.SemaphoreType.DMA((N_DEV - 1,))],
        compiler_params=pltpu.CompilerParams(vmem_limit_bytes=VMEM_LIMIT),
    )(v)


def _allgather_hbm(name, shards):
    n = len(shards)

    def body(*refs):
        ins, outs = refs[:n], refs[n:2 * n]
        send_sems, recv_sems, local_sems = refs[2 * n:]
        x, y, c, me = _position()
        sibling = (x, y, 1 - c)
        chips = [(1 - x, y), (x, 1 - y), (1 - x, 1 - y)]

        def slot(px, py, pc):
            return 4 * px + 2 * py + pc

        def copy(w, k, block, to, src=None):
            dst = outs[w].at[slot(*block)]
            return pltpu.make_async_remote_copy(src_ref=dst if src is None else src, dst_ref=dst, send_sem=send_sems.at[w, k],
                                                recv_sem=recv_sems.at[w, k], device_id=to, device_id_type=MESH)

        mine, first, passed = [], [], []
        for w in range(n):
            cp = pltpu.make_async_copy(ins[w], outs[w].at[me], local_sems.at[w])
            cp.start()
            mine.append(cp)
            for j, chip in enumerate(chips):
                first.append(copy(w, 1 + j, (x, y, c), (*chip, c), src=ins[w]))
            first.append(copy(w, 0, (x, y, c), sibling, src=ins[w]))
        for cp in first:
            cp.start()
        for w in range(n):
            for j, chip in enumerate(chips):
                copy(w, 1 + j, (*chip, c), (x, y, c)).wait_recv()
                cp = copy(w, 4 + j, (*chip, c), sibling)
                cp.start()
                passed.append(cp)
        for w in range(n):
            copy(w, 0, sibling, (x, y, c)).wait_recv()
            for j, chip in enumerate(chips):
                copy(w, 4 + j, (*chip, 1 - c), (x, y, c)).wait_recv()
        for cp in first + passed:
            cp.wait_send()
        for cp in mine:
            cp.wait()

    hbm = pl.BlockSpec(memory_space=pltpu.HBM)
    return pl.pallas_call(
        body, name=name, out_shape=[jax.ShapeDtypeStruct((N_DEV, *s.shape), s.dtype) for s in shards],
        in_specs=[hbm] * n, out_specs=[hbm] * n,
        scratch_shapes=[pltpu.SemaphoreType.DMA((n, 7)), pltpu.SemaphoreType.DMA((n, 7)), pltpu.SemaphoreType.DMA((n,))],
    )(*shards)


_HBM = pl.BlockSpec(memory_space=pltpu.HBM)
_SEM = pl.BlockSpec(memory_space=pltpu.SEMAPHORE)
_EFFECT = pltpu.SideEffectType.DATAFLOW_SIDE_EFFECTING


def _scatter_copies(srcs, lands, send_sems, recv_sems):
    x, y, c, _ = _position()
    copies = []
    for k in range(1, N_DEV):
        px, py, pc = _flip(x, y, c, k)
        for w in range(len(srcs)):
            copies.append(pltpu.make_async_remote_copy(
                src_ref=srcs[w].at[4 * px + 2 * py + pc], dst_ref=lands[w].at[k - 1], send_sem=send_sems.at[w * (N_DEV - 1) + k - 1],
                recv_sem=recv_sems.at[w * (N_DEV - 1) + k - 1], device_id=(px, py, pc), device_id_type=MESH))
    return copies


def _scatter_start(name, grads):
    n = len(grads)
    lands = [pltpu.with_memory_space_constraint(lax.empty((N_DEV - 1, *g.shape[1:]), g.dtype), pltpu.HBM) for g in grads]

    def body(*refs):
        for cp in _scatter_copies(refs[:n], refs[n:2 * n], refs[2 * n], refs[2 * n + 1]):
            cp.start()
        refs[-1][...] = jnp.zeros_like(refs[-1])

    sems = pltpu.SemaphoreType.DMA((n * (N_DEV - 1),))
    res = pl.pallas_call(
        body, name=name,
        out_shape=(sems, sems, *[pltpu.HBM(g.shape, g.dtype) for g in grads], *[pltpu.HBM(a.shape, a.dtype) for a in lands],
                   jax.ShapeDtypeStruct((8, LANES), F32)),
        in_specs=[_HBM] * (2 * n), out_specs=(_SEM, _SEM, *[_HBM] * (2 * n), pl.BlockSpec(memory_space=pltpu.VMEM)),
        input_output_aliases={i: 2 + i for i in range(2 * n)},
        compiler_params=pltpu.CompilerParams(has_side_effects=_EFFECT),
    )(*[pltpu.with_memory_space_constraint(g, pltpu.HBM) for g in grads], *lands)
    return res[0], res[1], list(res[2:2 + n]), list(res[2 + n:2 + 2 * n]), res[-1]


def _scatter_wait(name, send_sems, recv_sems, srcs, lands, after):
    n = len(srcs)

    def body(*refs):
        for cp in _scatter_copies(refs[:n], refs[n:2 * n], refs[2 * n], refs[2 * n + 1]):
            cp.wait_send()
            cp.wait_recv()

    res = pl.pallas_call(
        body, name=name, out_shape=tuple(pltpu.HBM(a.shape, a.dtype) for a in (*srcs, *lands)),
        in_specs=[_HBM] * (2 * n) + [_SEM, _SEM, pl.BlockSpec(memory_space=pl.ANY)], out_specs=tuple([_HBM] * (2 * n)),
        input_output_aliases={i: i for i in range(2 * n)},
        compiler_params=pltpu.CompilerParams(has_side_effects=_EFFECT),
    )(*srcs, *lands, send_sems, recv_sems, after)
    return list(res[:n]), list(res[n:])


def _ada_mod(c16, w):
    _, D = c16.shape
    n = w.shape[1]
    tk = _tile(D, 512)
    nk = D // tk

    def body(c_ref, w_ref, o_ref, ca_ref):
        @pl.when(pl.program_id(0) == 0)
        def _():
            o_ref[...] = jnp.zeros_like(o_ref)

        cv = c_ref[...]
        ca = cv * _sigmoid(cv)
        ca_ref[...] = ca
        o_ref[...] += _dot(ca, w_ref[...])

    return pl.pallas_call(
        body, name="ada_mod", grid=(nk,),
        in_specs=[pl.BlockSpec((16, tk), lambda k: (0, k)), pl.BlockSpec((tk, n), lambda k: (k, 0))],
        out_specs=[pl.BlockSpec((16, n), lambda k: (0, 0)), pl.BlockSpec((16, tk), lambda k: (0, k))],
        out_shape=[jax.ShapeDtypeStruct((16, n), F32), jax.ShapeDtypeStruct((16, D), F32)],
        compiler_params=_cparams(("arbitrary",)))(c16, w)


def _adam_math(w, g, m, v):
    m2 = ADAM_B1 * m + (1.0 - ADAM_B1) * g
    v2 = ADAM_B2 * v + (1.0 - ADAM_B2) * (g * g)
    m_hat = m2 / (1.0 - ADAM_B1 ** ADAM_STEP)
    v_hat = v2 / (1.0 - ADAM_B2 ** ADAM_STEP)
    delta = -ADAM_LR * (m_hat / (jnp.sqrt(v_hat) + ADAM_EPS) + ADAM_WD * w)
    return delta, m2, v2


def _adamw(name, w, m, v, parts):
    R, C = w.shape
    tr = _tile(R, max(16, 131072 // C), 16)
    n_p = len(parts)

    def body(*refs):
        w_ref, m_ref, v_ref = refs[:3]
        g_ref, d_ref, m2_ref, v2_ref = refs[3 + n_p:]
        g = None
        for p_ref in refs[3:3 + n_p]:
            for s in range(p_ref.shape[0]):
                t = p_ref[s].astype(F32)
                g = t if g is None else g + t
        delta, m2, v2 = _adam_math(w_ref[...], g, m_ref[...], v_ref[...])
        g_ref[...] = g
        d_ref[...] = delta
        m2_ref[...] = m2
        v2_ref[...] = v2

    blk = pl.BlockSpec((tr, C), lambda i: (i, 0))
    out = jax.ShapeDtypeStruct((R, C), F32)
    return pl.pallas_call(body, name=name, grid=(R // tr,),
                          in_specs=[blk, blk, blk] + [pl.BlockSpec((a.shape[0], tr, C), lambda i: (0, i, 0)) for a in parts],
                          out_specs=[blk] * 4, out_shape=[out] * 4, compiler_params=_cparams(("parallel",)))(w, m, v, *parts)


def _small_update(gathered, w, m, v):
    _, R, L = gathered.shape
    rs = w.shape[0]

    def body(p_ref, w_ref, m_ref, v_ref, g_ref, d_ref, m2_ref, v2_ref):
        g = p_ref[0]
        for p in range(1, N_DEV):
            g = g + p_ref[p]
        g_ref[...] = g
        delta, m2, v2 = _adam_math(w_ref[...], g[0:rs, :], m_ref[...], v_ref[...])
        d_ref[...] = delta
        m2_ref[...] = m2
        v2_ref[...] = v2

    vm = pl.BlockSpec(memory_space=pltpu.VMEM)
    sm = jax.ShapeDtypeStruct((rs, L), F32)
    return pl.pallas_call(body, name="small_update", in_specs=[vm] * 4, out_specs=[vm] * 4,
                          out_shape=[jax.ShapeDtypeStruct((R, L), F32), sm, sm, sm],
                          compiler_params=pltpu.CompilerParams(vmem_limit_bytes=VMEM_LIMIT))(gathered, w, m, v)


def _local_step(x, tgt, mod, p, w, scatter=None):
    S, D = x.shape
    GW, HW = p["ln_g"].shape[1], p["hg_ng"].shape[1]
    G, T, _ = p["ws"].shape
    F = w["fo"].shape[0]
    in_loc, br_loc, fi_loc = w["in"].shape[2], w["bg"].shape[2], w["fi"].shape[2]
    INW = in_loc * N_DEV
    assert GW == HW and INW == 2 * GW + 4 * HW + 2 * D and fi_loc * N_DEV == 2 * F and F % fi_loc == 0
    sh1, sc1, gt1, sh2, sc2, gt2 = (mod[:, k * D:(k + 1) * D] for k in range(6))
    bsb = jnp.broadcast_to(p["bs"][:, :, None], (G, T, GW // G))

    tm = _tile(S, 1024, 16)
    tmh = _tile(S, 512, 16)
    tn_in = _tile(in_loc, 640)
    tn_d = _tile(D, 512)
    tn_br = _tile(br_loc, 512)
    tk_s = _tile(S, 1024)
    g_off = 2 * GW + 4 * HW

    h1 = _norm_mod("norm1", x, p["norm1_g"], sc1, sh1)
    z = _mm_nn_stacked("proj_in", h1, w["in"], tm=tm, tn=tn_in, tk=D)[0]
    ya = _gmlp_fwd(z, p["ln_g"], p["ln_b"], p["ws"], bsb, GW)
    yb, o_hg, states = _hg_fwd(z, p["hg_lb"], p["hg_ng"], HW)
    pa = _mm_nn_stacked("branch_gmlp", ya, w["bg"], tm=tm, tn=tn_br, tk=GW)[0]

    def gates(ga_ref, gb_ref, ba_ref, bb_ref):
        return _sigmoid(ga_ref[...] + ba_ref[...]), _sigmoid(gb_ref[...] + bb_ref[...])

    def gate_specs(tn_):
        o1, o2 = g_off // tn_, (g_off + D) // tn_
        return [pl.BlockSpec((tm, tn_), lambda i, j, k: (i, o1 + j)), pl.BlockSpec((tm, tn_), lambda i, j, k: (i, o2 + j)),
                pl.BlockSpec((1, tn_), lambda i, j, k: (0, j)), pl.BlockSpec((1, tn_), lambda i, j, k: (0, D // tn_ + j))]

    def merge_ep(acc, ex, outs):
        ga, gb = gates(*ex[1:5])
        outs[0][...] = acc
        outs[1][...] = (ga * ex[0][...] + gb * acc).astype(BF16)

    tile_o = pl.BlockSpec((tm, tn_br), lambda i, j, k: (i, j))
    pb, y = _mm_nn_stacked(
        "branch_hg_merge", yb, w["bh"], tm=tm, tn=tn_br, tk=HW, extras=[pa, z, z, p["b_gate"], p["b_gate"]],
        extra_specs=[tile_o, *gate_specs(tn_br)], out_shapes=[jax.ShapeDtypeStruct((S, D), F32), jax.ShapeDtypeStruct((S, D), BF16)],
        out_specs=[tile_o, tile_o], epilogue=merge_ep)

    def resid_ep(acc, ex, outs):
        outs[0][...] = acc
        outs[1][...] = ex[0][...] + ex[1][...] * acc

    def resid_mm(name, a, b, res, gt, tk):
        K = a.shape[1]
        t_o = pl.BlockSpec((tm, tn_d), lambda i, j, k: (i, j))
        return _matmul(
            name, a, b, dims=_NN, grid_mnk=(S // tm, D // tn_d, K // tk), tiles=(tm, tn_d),
            a_spec=pl.BlockSpec((tm, tk), lambda i, j, k: (i, k)), b_spec=pl.BlockSpec((tk, tn_d), lambda i, j, k: (k, j)),
            extras=[res, gt], extra_specs=[t_o, pl.BlockSpec((1, tn_d), lambda i, j, k: (0, j))],
            out_shapes=[jax.ShapeDtypeStruct((S, D), F32)] * 2, out_specs=[t_o, t_o], epilogue=resid_ep)

    o1, xm = resid_mm("proj_out", y, w["out"], x, gt1, D)
    h2 = _norm_mod("norm2", xm, p["norm2_g"], sc2, sh2)
    ab = _mm_nn_stacked("ffn_in", h2, w["fi"], tm=tm, tn=fi_loc, tk=D)[0]
    hf = _swiglu(ab, F, fi_loc)
    o2, x3 = resid_mm("ffn_out", hf, w["fo"], xm, gt2, _tile(F, 1536))
    dx3, do2, vec_l = _loss_head(x3, tgt, p["final_g"], o2, gt2)

    nf = F // fi_loc

    def dswiglu_ep(acc, ex, outs):
        a, up = ex[0][...], ex[1][...]
        sa = _sigmoid(a)
        outs[0][0] = (acc * up * (sa * (1.0 + a * (1.0 - sa)))).astype(BF16)
        outs[0][1] = (acc * (a * sa)).astype(BF16)

    dab = _matmul(
        "ffn_out_dx", do2, w["fo"], dims=_NT, grid_mnk=(S // tmh, nf, 1), tiles=(tmh, fi_loc),
        a_spec=pl.BlockSpec((tmh, D), lambda i, j, k: (i, 0)), b_spec=pl.BlockSpec((fi_loc, D), lambda i, j, k: (j, 0)),
        extras=[ab, ab], extra_specs=[pl.BlockSpec((tmh, fi_loc), lambda i, j, k: (i, j)), pl.BlockSpec((tmh, fi_loc), lambda i, j, k: (i, j + nf))],
        out_shapes=[jax.ShapeDtypeStruct((2, S, F), BF16)], out_specs=[pl.BlockSpec((2, tmh, fi_loc), lambda i, j, k: (0, i, j))],
        epilogue=dswiglu_ep)[0]
    tm_f = _tile(F, 512)
    g_fo = _mm_tn("ffn_out_dw", hf, do2, pl.BlockSpec((tk_s, tn_d), lambda i, j, k: (k, j)), Mo=F, No=D, S=S, tm=tm_f, tn=tn_d, tk=tk_s)
    dh2 = _mm_nt_stacked("ffn_in_dx", pl.BlockSpec((None, tm, fi_loc), lambda i, j, k: (k // nf, i, k % nf)), dab, w["fi"],
                         M=S, tm=tm, tn=tn_d, tk=fi_loc)
    g_fi = _mm_tn("ffn_in_dw", h2, dab, pl.BlockSpec((None, tk_s, fi_loc), lambda i, j, k: (j // nf, k, j % nf)),
                  Mo=D, No=2 * F, S=S, tm=tn_d, tn=fi_loc, tk=tk_s, stacked_nloc=fi_loc)
    tie = (lambda name, grads: scatter(name, grads)[0:1, 0:1]) if scatter is not None else (lambda name, grads: 0.0)
    dxm, vec2, do1 = _norm_mod_bwd("norm2_bwd", dh2, xm, p["norm2_g"], sc2, dx3, o1, gt1 + tie("scatter_ffn", dict(fo=g_fo, fi=g_fi)))

    def dmerge_ep(acc, ex, outs):
        ga, gb = gates(*ex[2:6])
        outs[0][...] = (acc * ga).astype(BF16)
        outs[1][...] = (acc * gb).astype(BF16)
        outs[2][0] = (acc * ex[0][...] * ga * (1.0 - ga)).astype(BF16)
        outs[2][1] = (acc * ex[1][...] * gb * (1.0 - gb)).astype(BF16)

    t_o = pl.BlockSpec((tm, tn_d), lambda i, j, k: (i, j))
    dpa, dpb, dg2 = _matmul(
        "proj_out_dx", do1, w["out"], dims=_NT, grid_mnk=(S // tm, D // tn_d, 1), tiles=(tm, tn_d),
        a_spec=pl.BlockSpec((tm, D), lambda i, j, k: (i, 0)), b_spec=pl.BlockSpec((tn_d, D), lambda i, j, k: (j, 0)),
        extras=[pa, pb, z, z, p["b_gate"], p["b_gate"]], extra_specs=[t_o, t_o, *gate_specs(tn_d)],
        out_shapes=[jax.ShapeDtypeStruct((S, D), BF16), jax.ShapeDtypeStruct((S, D), BF16), jax.ShapeDtypeStruct((2, S, D), BF16)],
        out_specs=[t_o, t_o, pl.BlockSpec((2, tm, tn_d), lambda i, j, k: (0, i, j))], epilogue=dmerge_ep)
    db_gate = _colsum2(dg2)
    g_out = _mm_tn("proj_out_dw", y, do1, pl.BlockSpec((tk_s, tn_d), lambda i, j, k: (k, j)), Mo=D, No=D, S=S, tm=tn_d, tn=tn_d, tk=tk_s)
    a_br = pl.BlockSpec((tm, br_loc), lambda i, j, k: (i, k))
    tn_g = _tile(GW, 512)
    dya = _mm_nt_stacked("branch_gmlp_dx", a_br, dpa, w["bg"], M=S, tm=tm, tn=tn_g, tk=br_loc)
    dyb = _mm_nt_stacked("branch_hg_dx", a_br, dpb, w["bh"], M=S, tm=tm, tn=tn_g, tk=br_loc)
    b_br = pl.BlockSpec((tk_s, br_loc), lambda i, j, k: (k, j))
    g_bg = _mm_tn("branch_gmlp_dw", ya, dpa, b_br, Mo=GW, No=D, S=S, tm=tn_g, tn=br_loc, tk=tk_s, stacked_nloc=br_loc)
    g_bh = _mm_tn("branch_hg_dw", yb, dpb, b_br, Mo=HW, No=D, S=S, tm=tn_g, tn=br_loc, tk=tk_s, stacked_nloc=br_loc)
    ln_b_tied = p["ln_b"] + tie("scatter_mixer", dict(out=g_out, bg=g_bg, bh=g_bh))
    dz_gmlp, dln, dws, dbs = _gmlp_bwd(z, dya, p["ln_g"], ln_b_tied, p["ws"], bsb, GW)
    dz_hg, dng, dhlb = _hg_bwd(z, o_hg, states, dyb, p["hg_lb"], p["hg_ng"], HW)
    dz = jnp.concatenate([dz_gmlp, dz_hg, dg2[0], dg2[1]], axis=1)
    g_in = _mm_tn("proj_in_dw", h1, dz, pl.BlockSpec((tk_s, tn_in), lambda i, j, k: (k, j)), Mo=D, No=INW, S=S,
                  tm=tn_d, tn=tn_in, tk=tk_s, stacked_nloc=in_loc)
    sc1_tied = sc1 + tie("scatter_proj_in", dict(w_in=g_in))
    dh1 = _mm_nt_stacked("proj_in_dx", pl.BlockSpec((tm, tn_in), lambda i, j, k: (i, k)), dz, w["in"], M=S, tm=tm, tn=tn_d, tk=tn_in)
    dx, vec1 = _norm_mod_bwd("norm1_bwd", dh1, x, p["norm1_g"], sc1_tied, dxm)

    dmod = jnp.concatenate([vec1[0:1], vec1[1:2], vec2[3:4], vec2[0:1], vec2[1:2], vec_l[2:3]], axis=1)
    small = dict(norm1_g=vec1[2:3], b_gate=db_gate.reshape(1, 2 * D), ln_g=dln[0:1], ln_b=dln[1:2], ws=dws, bs=dbs.reshape(G, T),
                 hg_lb=dhlb, hg_ng=dng[0:1], norm2_g=vec2[2:3], final_g=vec_l[1:2], loss=vec_l[0:1, 0:LANES])
    big = dict(w_in=g_in, bg=g_bg, bh=g_bh, out=g_out, fi=g_fi, fo=g_fo)
    return dx, big, small, dmod


_SMALL = ("b_ada", "norm1_g", "b_gate", "ln_g", "ln_b", "ws", "bs", "hg_lb", "hg_ng", "norm2_g", "final_g")


def _pack(parts, rows_mult=8):
    flat = [a.reshape(-1) for a in parts]
    offs, n = [], 0
    for a in flat:
        offs.append(n)
        n += a.shape[0]
    pad = (-n) % (LANES * rows_mult)
    if pad:
        flat.append(jnp.zeros((pad,), F32))
    return jnp.concatenate(flat).reshape(-1, LANES), offs


def kernel(x, c, w_ada, b_ada, norm1_g, w_in, b_gate, gmlp_ln_g, gmlp_ln_b, gmlp_ws, gmlp_bs, hg_lb, hg_norm_g, w_branch_gmlp, w_branch_hg, w_out, norm2_g, w_ffn_in, w_ffn_out, final_norm_g, loss_target, m_w_ada, m_b_ada, m_norm1_g, m_w_in, m_b_gate, m_gmlp_ln_g, m_gmlp_ln_b, m_gmlp_ws, m_gmlp_bs, m_hg_lb, m_hg_norm_g, m_w_branch_gmlp, m_w_branch_hg, m_w_out, m_norm2_g, m_w_ffn_in, m_w_ffn_out, m_final_norm_g, v_w_ada, v_b_ada, v_norm1_g, v_w_in, v_b_gate, v_gmlp_ln_g, v_gmlp_ln_b, v_gmlp_ws, v_gmlp_bs, v_hg_lb, v_hg_norm_g, v_w_branch_gmlp, v_w_branch_hg, v_w_out, v_norm2_g, v_w_ffn_in, v_w_ffn_out, v_final_norm_g):
    S, D = x.shape[1], x.shape[2]
    ada_loc = w_ada.shape[2]
    me = 4 * lax.axis_index("x") + 2 * lax.axis_index("y") + lax.axis_index("c")

    c_all = _allgather_small("gather_c", c.reshape(D // LANES, LANES)).reshape(N_DEV, D)
    mod_cols, c_act = _ada_mod(jnp.pad(c_all, ((0, 16 - N_DEV), (0, 0))), w_ada[0])
    mod_all = _allgather_small("gather_mod", mod_cols[:N_DEV].reshape(-1, LANES)).reshape(N_DEV, N_DEV, ada_loc)
    mod = lax.dynamic_index_in_dim(mod_all, me, axis=1, keepdims=False).reshape(1, N_DEV * ada_loc) + b_ada

    shards = [a[0].astype(BF16) for a in (w_in, w_branch_gmlp, w_branch_hg, w_out, w_ffn_in, w_ffn_out)]
    g_in, g_bg, g_bh, g_out, g_fi, g_fo = _allgather_hbm("gather_weights", shards)
    w = {"in": g_in, "bg": g_bg, "bh": g_bh, "out": g_out.reshape(-1, D), "fi": g_fi, "fo": g_fo.reshape(-1, D)}
    p = dict(norm1_g=norm1_g, b_gate=b_gate, ln_g=gmlp_ln_g, ln_b=gmlp_ln_b, ws=gmlp_ws[0], bs=gmlp_bs[0], hg_lb=hg_lb,
             hg_ng=hg_norm_g, norm2_g=norm2_g, final_g=final_norm_g.reshape(1, D))

    in_flight = {}

    def scatter(name, grads):
        keys = list(grads)
        stacks = [grads[k].reshape(N_DEV, -1, grads[k].shape[-1]) for k in keys]
        send_sems, recv_sems, srcs, lands, token = _scatter_start(name, stacks)
        in_flight[name] = (keys, send_sems, recv_sems, srcs, lands)
        return token

    grad_x, _, small, dmod = _local_step(x[0], loss_target[0], mod, p, w, scatter)

    small["b_ada"] = dmod
    packed, offs = _pack([small[k] for k in _SMALL] + [small["loss"]])
    gathered = _allgather_small("gather_small", packed)
    wp = dict(p, b_ada=b_ada)
    ms = dict(b_ada=m_b_ada, norm1_g=m_norm1_g, b_gate=m_b_gate, ln_g=m_gmlp_ln_g, ln_b=m_gmlp_ln_b, ws=m_gmlp_ws, bs=m_gmlp_bs,
              hg_lb=m_hg_lb, hg_ng=m_hg_norm_g, norm2_g=m_norm2_g, final_g=m_final_norm_g)
    vs = dict(b_ada=v_b_ada, norm1_g=v_norm1_g, b_gate=v_b_gate, ln_g=v_gmlp_ln_g, ln_b=v_gmlp_ln_b, ws=v_gmlp_ws, bs=v_gmlp_bs,
              hg_lb=v_hg_lb, hg_ng=v_hg_norm_g, norm2_g=v_norm2_g, final_g=v_final_norm_g)
    w_sm, _ = _pack([wp[k] for k in _SMALL])
    m_sm, _ = _pack([ms[k] for k in _SMALL])
    v_sm, _ = _pack([vs[k] for k in _SMALL])
    sm_out = _small_update(gathered, w_sm, m_sm, v_sm)
    shapes = dict(b_ada=b_ada.shape, norm1_g=norm1_g.shape, b_gate=b_gate.shape, ln_g=gmlp_ln_g.shape, ln_b=gmlp_ln_b.shape,
                  ws=gmlp_ws.shape, bs=gmlp_bs.shape, hg_lb=hg_lb.shape, hg_ng=hg_norm_g.shape, norm2_g=norm2_g.shape,
                  final_g=final_norm_g.shape)

    def unpack(arr, k):
        i = _SMALL.index(k)
        n = math.prod(shapes[k])
        return arr.reshape(-1)[offs[i]:offs[i] + n].reshape(shapes[k])

    loss = sm_out[0].reshape(-1)[offs[len(_SMALL)]]

    dmod_all = gathered.reshape(N_DEV, -1)[:, offs[0]:offs[0] + N_DEV * ada_loc]
    dmod_loc = lax.dynamic_slice_in_dim(dmod_all, me * ada_loc, ada_loc, axis=1)
    ca_t = jnp.pad(c_act[:N_DEV].T, ((0, 0), (0, LANES - N_DEV))).astype(BF16)
    dm_p = jnp.pad(dmod_loc, ((0, LANES - N_DEV), (0, 0))).astype(BF16)
    tm_a = _tile(D, 512)
    g_ada = _matmul(
        "ada_dw", ca_t, dm_p, dims=_NN, grid_mnk=(D // tm_a, 1, 1), tiles=(tm_a, ada_loc),
        a_spec=pl.BlockSpec((tm_a, LANES), lambda i, j, k: (i, 0)), b_spec=pl.BlockSpec((LANES, ada_loc), lambda i, j, k: (0, 0)),
        out_shapes=[jax.ShapeDtypeStruct((1, D, ada_loc), F32)], out_specs=[pl.BlockSpec((None, tm_a, ada_loc), lambda i, j, k: (0, i, 0))],
        epilogue=_store(F32))[0]

    upd = {"w_ada": _adamw("adamw_w_ada", w_ada[0], m_w_ada[0], v_w_ada[0], [g_ada])}
    big_w = dict(w_in=(w_in, m_w_in, v_w_in, "w_in"), bg=(w_branch_gmlp, m_w_branch_gmlp, v_w_branch_gmlp, "w_branch_gmlp"),
                 bh=(w_branch_hg, m_w_branch_hg, v_w_branch_hg, "w_branch_hg"), out=(w_out, m_w_out, v_w_out, "w_out"),
                 fi=(w_ffn_in, m_w_ffn_in, v_w_ffn_in, "w_ffn_in"), fo=(w_ffn_out, m_w_ffn_out, v_w_ffn_out, "w_ffn_out"))
    after = upd["w_ada"][1]
    for name in ("scatter_ffn", "scatter_mixer", "scatter_proj_in"):
        keys, send_sems, recv_sems, srcs, lands = in_flight[name]
        sent, landed = _scatter_wait(name + "_wait", send_sems, recv_sems, srcs, lands, after)
        for k, sent_k, land_k in zip(keys, sent, landed):
            wt, mt, vt, out_name = big_w[k]
            own_k = lax.dynamic_index_in_dim(sent_k, me, axis=0, keepdims=True)
            upd[out_name] = _adamw("adamw_" + out_name, wt[0], mt[0], vt[0], [own_k, land_k])
            after = upd[out_name][1]

    order = ("w_ada", "b_ada", "norm1_g", "w_in", "b_gate", "ln_g", "ln_b", "ws", "bs", "hg_lb", "hg_ng", "w_branch_gmlp", "w_branch_hg",
             "w_out", "norm2_g", "w_ffn_in", "w_ffn_out", "final_g")
    outs = [loss, grad_x[None]]
    for idx in range(4):
        for k in order:
            outs.append(upd[k][idx][None] if k in upd else unpack(sm_out[idx], k))
    return tuple(outs)
```

```python
import functools
import math

import jax
import jax.numpy as jnp
from jax import lax
from jax.experimental import pallas as pl
from jax.experimental.pallas import tpu as pltpu

F32 = jnp.float32
BF16 = jnp.bfloat16
N_DEV = 8
EPS = 1e-6
LANES = 128
HG_DK = 128
HG_CHUNK = 64
HG_MID = HG_CHUNK // 2 - 1
EXP_CLAMP = 80.0
VMEM_LIMIT = 48 * 1024 * 1024
ADAM_LR, ADAM_B1, ADAM_B2, ADAM_EPS, ADAM_WD, ADAM_STEP = 0.001, 0.9, 0.999, 1e-08, 0.01, 10
MESH = pl.DeviceIdType.MESH

_NN = (((1,), (0,)), ((), ()))
_NT = (((1,), (1,)), ((), ()))
_TN = (((0,), (0,)), ((), ()))


def _dot(a, b, dims=_NN):
    return lax.dot_general(a.astype(BF16), b.astype(BF16), dims, preferred_element_type=F32)


def _tile(n, target, mult=LANES):
    best = None
    for t in range(mult, min(n, target) + 1, mult):
        if n % t == 0:
            best = t
    return n if best is None else best


def _cparams(sem):
    return pltpu.CompilerParams(dimension_semantics=sem, vmem_limit_bytes=VMEM_LIMIT)


def _sigmoid(x):
    return 1.0 / (1.0 + jnp.exp(-x))


def _gelu_parts(x):
    k0 = math.sqrt(2.0 / math.pi)
    x2 = x * x
    t = jnp.tanh(k0 * (x + 0.044715 * x * x2))
    g = 0.5 * x * (1.0 + t)
    dg = 0.5 * (1.0 + t) + 0.5 * x * (1.0 - t * t) * (k0 * (1.0 + 3.0 * 0.044715 * x2))
    return g, dg


def _split3(x):
    h = x.astype(BF16)
    r = x - h.astype(F32)
    m = r.astype(BF16)
    lo = (r - m.astype(F32)).astype(BF16)
    return h, m, lo


def _ones_dot(mat01, x):
    h, m, lo = _split3(x)
    d = functools.partial(lax.dot_general, dimension_numbers=_NN, preferred_element_type=F32)
    return d(mat01, h) + d(mat01, m) + d(mat01, lo)


def _matmul(name, a, b, *, dims, grid_mnk, tiles, a_spec, b_spec, extras=(), extra_specs=(), out_shapes, out_specs, epilogue):
    gm, gn, nk = grid_mnk
    tm, tn = tiles
    n_ex, n_out = len(extras), len(out_shapes)

    def body(*refs):
        a_ref, b_ref = refs[0], refs[1]
        ex = refs[2:2 + n_ex]
        outs = refs[2 + n_ex:2 + n_ex + n_out]
        if nk == 1:
            epilogue(lax.dot_general(a_ref[...], b_ref[...], dims, preferred_element_type=F32), ex, outs)
            return
        acc = refs[-1]
        k = pl.program_id(2)

        @pl.when(k == 0)
        def _():
            acc[...] = jnp.zeros_like(acc)

        acc[...] += lax.dot_general(a_ref[...], b_ref[...], dims, preferred_element_type=F32)

        @pl.when(k == nk - 1)
        def _():
            epilogue(acc[...], ex, outs)

    return pl.pallas_call(
        body, name=name, grid=(gm, gn, nk), in_specs=[a_spec, b_spec, *extra_specs], out_specs=list(out_specs),
        out_shape=list(out_shapes), scratch_shapes=[] if nk == 1 else [pltpu.VMEM((tm, tn), F32)],
        compiler_params=_cparams(("parallel", "parallel", "arbitrary")),
    )(a, b, *extras)


def _store(dtype):
    def ep(acc, ex, outs):
        outs[0][...] = acc.astype(dtype)
    return ep


def _mm_nn_stacked(name, a, wg, *, tm, tn, tk, out_dtype=F32, extras=(), extra_specs=(), out_shapes=None, out_specs=None, epilogue=None):
    M, K = a.shape
    _, _, nloc = wg.shape
    N = nloc * N_DEV
    q = nloc // tn
    if out_shapes is None:
        out_shapes = [jax.ShapeDtypeStruct((M, N), out_dtype)]
        out_specs = [pl.BlockSpec((tm, tn), lambda i, j, k: (i, j))]
        epilogue = _store(out_dtype)
    return _matmul(
        name, a, wg, dims=_NN, grid_mnk=(M // tm, N // tn, K // tk), tiles=(tm, tn),
        a_spec=pl.BlockSpec((tm, tk), lambda i, j, k: (i, k)),
        b_spec=pl.BlockSpec((None, tk, tn), lambda i, j, k: (j // q, k, j % q)),
        extras=extras, extra_specs=extra_specs, out_shapes=out_shapes, out_specs=out_specs, epilogue=epilogue)


def _mm_nt_stacked(name, a_spec, a, wg, *, M, tm, tn, tk, out_dtype=F32):
    _, Kw, nloc = wg.shape
    q = nloc // tk
    return _matmul(
        name, a, wg, dims=_NT, grid_mnk=(M // tm, Kw // tn, (nloc * N_DEV) // tk), tiles=(tm, tn),
        a_spec=a_spec, b_spec=pl.BlockSpec((None, tn, tk), lambda i, j, k: (k // q, j, k % q)),
        out_shapes=[jax.ShapeDtypeStruct((M, Kw), out_dtype)], out_specs=[pl.BlockSpec((tm, tn), lambda i, j, k: (i, j))],
        epilogue=_store(out_dtype))[0]


def _mm_tn(name, a, b, b_spec, *, Mo, No, S, tm, tn, tk, stacked_nloc=None):
    if stacked_nloc is None:
        out_shape = jax.ShapeDtypeStruct((Mo, No), BF16)
        out_spec = pl.BlockSpec((tm, tn), lambda i, j, k: (i, j))
    else:
        q = stacked_nloc // tn
        out_shape = jax.ShapeDtypeStruct((N_DEV, Mo, stacked_nloc), BF16)
        out_spec = pl.BlockSpec((None, tm, tn), lambda i, j, k: (j // q, i, j % q))
    return _matmul(
        name, a, b, dims=_TN, grid_mnk=(Mo // tm, No // tn, S // tk), tiles=(tm, tn),
        a_spec=pl.BlockSpec((tk, tm), lambda i, j, k: (k, i)), b_spec=b_spec,
        out_shapes=[out_shape], out_specs=[out_spec], epilogue=_store(BF16))[0]


def _norm_mod(name, x, g, sc, sh):
    S, D = x.shape
    tm = _tile(S, 256, 8)

    def body(x_ref, g_ref, sc_ref, sh_ref, h_ref):
        xv = x_ref[...]
        r = lax.rsqrt(jnp.mean(xv * xv, axis=-1, keepdims=True) + EPS)
        h = (xv * r) * g_ref[...]
        h_ref[...] = (h * (1.0 + sc_ref[...]) + sh_ref[...]).astype(BF16)

    row = pl.BlockSpec((tm, D), lambda i: (i, 0))
    vec = pl.BlockSpec((1, D), lambda i: (0, 0))
    return pl.pallas_call(body, name=name, grid=(S // tm,), in_specs=[row, vec, vec, vec], out_specs=row,
                          out_shape=jax.ShapeDtypeStruct((S, D), BF16), compiler_params=_cparams(("parallel",)))(x, g, sc, sh)


def _norm_mod_bwd(name, dh, x, g, sc, dres, o=None, gt=None):
    S, D = x.shape
    tm = _tile(S, 256, 8)
    gated = o is not None

    def body(*refs):
        if gated:
            dh_ref, x_ref, g_ref, sc_ref, dres_ref, o_ref, gt_ref, dx_ref, vec_ref, do_ref = refs
        else:
            dh_ref, x_ref, g_ref, sc_ref, dres_ref, dx_ref, vec_ref = refs
        i = pl.program_id(0)

        @pl.when(i == 0)
        def _():
            vec_ref[...] = jnp.zeros_like(vec_ref)

        xv, dh_v, gv = x_ref[...], dh_ref[...], g_ref[...]
        r = lax.rsqrt(jnp.mean(xv * xv, axis=-1, keepdims=True) + EPS)
        xn = xv * r
        one_sc = 1.0 + sc_ref[...]
        vec_ref[0:1, :] += jnp.sum(dh_v, axis=0, keepdims=True)
        vec_ref[1:2, :] += jnp.sum(dh_v * (xn * gv), axis=0, keepdims=True)
        vec_ref[2:3, :] += jnp.sum(dh_v * one_sc * xn, axis=0, keepdims=True)
        dxn = dh_v * one_sc * gv
        dx = dres_ref[...] + r * (dxn - xn * jnp.mean(dxn * xn, axis=-1, keepdims=True))
        dx_ref[...] = dx
        if gated:
            vec_ref[3:4, :] += jnp.sum(dx * o_ref[...], axis=0, keepdims=True)
            do_ref[...] = (dx * gt_ref[...]).astype(BF16)

    row = pl.BlockSpec((tm, D), lambda i: (i, 0))
    vec = pl.BlockSpec((1, D), lambda i: (0, 0))
    acc = pl.BlockSpec((8, D), lambda i: (0, 0))
    ins = [dh, x, g, sc, dres] + ([o, gt] if gated else [])
    in_specs = [row, row, vec, vec, row] + ([row, vec] if gated else [])
    out_shape = [jax.ShapeDtypeStruct((S, D), F32), jax.ShapeDtypeStruct((8, D), F32)]
    out_specs = [row, acc]
    if gated:
        out_shape.append(jax.ShapeDtypeStruct((S, D), BF16))
        out_specs.append(row)
    return pl.pallas_call(body, name=name, grid=(S // tm,), in_specs=in_specs, out_specs=out_specs, out_shape=out_shape,
                          compiler_params=_cparams(("arbitrary",)))(*ins)


def _loss_head(x3, tgt, gf, o2, gt2):
    S, D = x3.shape
    tm = _tile(S, 256, 8)

    def body(x_ref, t_ref, g_ref, o_ref, gt_ref, dx_ref, do_ref, vec_ref):
        i = pl.program_id(0)

        @pl.when(i == 0)
        def _():
            vec_ref[...] = jnp.zeros_like(vec_ref)

        xv, gv = x_ref[...], g_ref[...]
        r = lax.rsqrt(jnp.mean(xv * xv, axis=-1, keepdims=True) + EPS)
        xn = xv * r
        e = xn * gv - t_ref[...]
        tok = 0.5 * jnp.mean(e * e, axis=-1, keepdims=True)
        vec_ref[0:1, :] += jnp.broadcast_to(jnp.sum(tok, axis=0, keepdims=True), (1, D))
        dy = e * (1.0 / D)
        vec_ref[1:2, :] += jnp.sum(dy * xn, axis=0, keepdims=True)
        dxn = dy * gv
        dx = r * (dxn - xn * jnp.mean(dxn * xn, axis=-1, keepdims=True))
        dx_ref[...] = dx
        vec_ref[2:3, :] += jnp.sum(dx * o_ref[...], axis=0, keepdims=True)
        do_ref[...] = (dx * gt_ref[...]).astype(BF16)

    row = pl.BlockSpec((tm, D), lambda i: (i, 0))
    vec = pl.BlockSpec((1, D), lambda i: (0, 0))
    return pl.pallas_call(
        body, name="loss_head", grid=(S // tm,), in_specs=[row, row, vec, row, vec],
        out_specs=[row, row, pl.BlockSpec((8, D), lambda i: (0, 0))],
        out_shape=[jax.ShapeDtypeStruct((S, D), F32), jax.ShapeDtypeStruct((S, D), BF16), jax.ShapeDtypeStruct((8, D), F32)],
        compiler_params=_cparams(("arbitrary",)))(x3, tgt, gf, o2, gt2)


def _swiglu(ab, F, tf):
    S = ab.shape[0]
    tm = _tile(S, 512, 8)
    nf = F // tf

    def body(a_ref, u_ref, h_ref):
        a = a_ref[...]
        h_ref[...] = (a * _sigmoid(a) * u_ref[...]).astype(BF16)

    return pl.pallas_call(
        body, name="swiglu", grid=(S // tm, nf),
        in_specs=[pl.BlockSpec((tm, tf), lambda i, j: (i, j)), pl.BlockSpec((tm, tf), lambda i, j: (i, j + nf))],
        out_specs=pl.BlockSpec((tm, tf), lambda i, j: (i, j)), out_shape=jax.ShapeDtypeStruct((S, F), BF16),
        compiler_params=_cparams(("parallel", "parallel")))(ab, ab)


def _colsum2(dg2):
    _, S, D = dg2.shape
    tm = _tile(S, 256, 16)

    def body(x_ref, o_ref):
        @pl.when(pl.program_id(0) == 0)
        def _():
            o_ref[...] = jnp.zeros_like(o_ref)

        o_ref[0:1, :] += jnp.sum(x_ref[0].astype(F32), axis=0, keepdims=True)
        o_ref[1:2, :] += jnp.sum(x_ref[1].astype(F32), axis=0, keepdims=True)

    return pl.pallas_call(body, name="gate_bias_grad", grid=(S // tm,), in_specs=[pl.BlockSpec((2, tm, D), lambda i: (0, i, 0))],
                          out_specs=pl.BlockSpec((2, D), lambda i: (0, 0)), out_shape=jax.ShapeDtypeStruct((2, D), F32),
                          compiler_params=_cparams(("arbitrary",)))(dg2)


def _gmlp_common(u_ref, v_ref, lg_ref, lb_ref, ws_ref, bsb_ref, G, T, Dg):
    ug, dug = _gelu_parts(u_ref[...])
    vg, dvg = _gelu_parts(v_ref[...])
    mu = jnp.mean(vg, axis=-1, keepdims=True)
    vc = vg - mu
    rstd = lax.rsqrt(jnp.mean(vc * vc, axis=-1, keepdims=True) + EPS)
    vhat = vc * rstd
    vn = vhat * lg_ref[...] + lb_ref[...]
    row = lax.broadcasted_iota(jnp.int32, (T, T), 0)
    col = lax.broadcasted_iota(jnp.int32, (T, T), 1)
    tril = row >= col
    s = []
    for g in range(G):
        w = jnp.where(tril, ws_ref[g], 0.0)
        s.append(_dot(w, vn[:, g * Dg:(g + 1) * Dg]) + bsb_ref[g])
    return ug, dug, dvg, rstd, vhat, vn, tril, s


def _gmlp_fwd(z, ln_g, ln_b, ws, bsb, GW):
    S = z.shape[0]
    G, T, _ = ws.shape
    Dg = GW // G

    def body(u_ref, v_ref, lg_ref, lb_ref, ws_ref, bsb_ref, ya_ref):
        ug, _, _, _, _, _, _, s = _gmlp_common(u_ref, v_ref, lg_ref, lb_ref, ws_ref, bsb_ref, G, T, Dg)
        for g in range(G):
            sl = slice(g * Dg, (g + 1) * Dg)
            ya_ref[:, sl] = (ug[:, sl] * s[g]).astype(BF16)

    vec = pl.BlockSpec((1, GW), lambda c: (0, 0))
    return pl.pallas_call(
        body, name="gmlp_fwd", grid=(S // T,),
        in_specs=[pl.BlockSpec((T, GW), lambda c: (c, 0)), pl.BlockSpec((T, GW), lambda c: (c, 1)), vec, vec,
                  pl.BlockSpec((G, T, T), lambda c: (0, 0, 0)), pl.BlockSpec((G, T, Dg), lambda c: (0, 0, 0))],
        out_specs=pl.BlockSpec((T, GW), lambda c: (c, 0)), out_shape=jax.ShapeDtypeStruct((S, GW), BF16),
        compiler_params=_cparams(("parallel",)))(z, z, ln_g, ln_b, ws, bsb)


def _gmlp_bwd(z, dya, ln_g, ln_b, ws, bsb, GW):
    S = z.shape[0]
    G, T, _ = ws.shape
    Dg = GW // G
    nc = S // T

    def body(u_ref, v_ref, dya_ref, lg_ref, lb_ref, ws_ref, bsb_ref, dz_ref, dln_ref, dws_ref, dbs_ref, dbs_acc, dvh):
        c = pl.program_id(0)

        @pl.when(c == 0)
        def _():
            dln_ref[...] = jnp.zeros_like(dln_ref)
            dws_ref[...] = jnp.zeros_like(dws_ref)
            dbs_acc[...] = jnp.zeros_like(dbs_acc)

        ug, dug, dvg, rstd, vhat, vn, tril, s = _gmlp_common(u_ref, v_ref, lg_ref, lb_ref, ws_ref, bsb_ref, G, T, Dg)
        dya_v = dya_ref[...]
        for g in range(G):
            sl = slice(g * Dg, (g + 1) * Dg)
            dy_g = dya_v[:, sl]
            dz_ref[:, sl] = (dy_g * s[g] * dug[:, sl]).astype(BF16)
            ds = dy_g * ug[:, sl]
            dbs_acc[g] += ds
            w = jnp.where(tril, ws_ref[g], 0.0)
            dvn_g = _dot(w, ds, _TN)
            dws_ref[g] += jnp.where(tril, _dot(ds, vn[:, sl], _NT), 0.0)
            dln_ref[0:1, sl] += jnp.sum(dvn_g * vhat[:, sl], axis=0, keepdims=True)
            dln_ref[1:2, sl] += jnp.sum(dvn_g, axis=0, keepdims=True)
            dvh[:, sl] = dvn_g * lg_ref[:, sl]
        dvhat = dvh[...]
        m1 = jnp.mean(dvhat, axis=-1, keepdims=True)
        m2 = jnp.mean(dvhat * vhat, axis=-1, keepdims=True)
        dz_ref[:, GW:2 * GW] = (rstd * (dvhat - m1 - vhat * m2) * dvg).astype(BF16)

        @pl.when(c == nc - 1)
        def _():
            for g in range(G):
                dbs_ref[g] = jnp.sum(dbs_acc[g], axis=-1, keepdims=True)

    vec = pl.BlockSpec((1, GW), lambda c: (0, 0))
    return pl.pallas_call(
        body, name="gmlp_bwd", grid=(nc,),
        in_specs=[pl.BlockSpec((T, GW), lambda c: (c, 0)), pl.BlockSpec((T, GW), lambda c: (c, 1)),
                  pl.BlockSpec((T, GW), lambda c: (c, 0)), vec, vec,
                  pl.BlockSpec((G, T, T), lambda c: (0, 0, 0)), pl.BlockSpec((G, T, Dg), lambda c: (0, 0, 0))],
        out_specs=[pl.BlockSpec((T, 2 * GW), lambda c: (c, 0)), pl.BlockSpec((8, GW), lambda c: (0, 0)),
                   pl.BlockSpec((G, T, T), lambda c: (0, 0, 0)), pl.BlockSpec((G, T, 1), lambda c: (0, 0, 0))],
        out_shape=[jax.ShapeDtypeStruct((S, 2 * GW), BF16), jax.ShapeDtypeStruct((8, GW), F32),
                   jax.ShapeDtypeStruct((G, T, T), F32), jax.ShapeDtypeStruct((G, T, 1), F32)],
        scratch_shapes=[pltpu.VMEM((G, T, Dg), F32), pltpu.VMEM((T, GW), F32)],
        compiler_params=_cparams(("arbitrary",)))(z, z, dya, ln_g, ln_b, ws, bsb)


def _hg_common(q_ref, f_ref, hlb_ref):
    C = HG_CHUNK
    a = hlb_ref[...]
    lb = _sigmoid(a[0:1, :] - a[1:2, :])
    sig = _sigmoid(f_ref[...])
    f = lb + (1.0 - lb) * sig
    lf = jnp.log(f)
    kk = 1.0 - f
    q = q_ref[...]
    sq = _sigmoid(q)
    qa = q * sq
    row = lax.broadcasted_iota(jnp.int32, (C, C), 0)
    col = lax.broadcasted_iota(jnp.int32, (C, C), 1)
    tril = row >= col
    b = _ones_dot(tril.astype(BF16), lf)
    bm = b[HG_MID:HG_MID + 1, :]
    bl = b[C - 1:C, :]
    e_b = jnp.exp(b)
    e_qm = jnp.exp(jnp.minimum(b - bm, EXP_CLAMP))
    e_km = jnp.exp(jnp.minimum(bm - b, EXP_CLAMP))
    e_kl = jnp.exp(bl - b)
    return dict(lb=lb, sig=sig, f=f, kk=kk, q=q, sq=sq, qa=qa, tril=tril, e_b=e_b, e_qm=e_qm, e_km=e_km, e_kl=e_kl,
                e_l=jnp.exp(bl), qh=qa * e_b, qt=qa * e_qm, kt=kk * e_km, kh=kk * e_kl)


def _hg_fwd(z, hg_lb, ng, HW):
    S = z.shape[0]
    C, H, dk = HG_CHUNK, HW // HG_DK, HG_DK
    nc = S // C

    def body(q_ref, f_ref, i_ref, og_ref, hlb_ref, ng_ref, yb_ref, o_ref, st_ref, state):
        @pl.when(pl.program_id(0) == 0)
        def _():
            state[...] = jnp.zeros_like(state)

        t = _hg_common(q_ref, f_ref, hlb_ref)
        iv = i_ref[...]
        for h in range(H):
            sl = slice(h * dk, (h + 1) * dk)
            st = state[h]
            st_ref[h] = st
            a = jnp.where(t["tril"], _dot(t["qt"][:, sl], t["kt"][:, sl], _NT), 0.0)
            o_h = _dot(a, iv[:, sl]) + _dot(t["qh"][:, sl], st, _NT)
            state[h] = st * t["e_l"][:, sl] + _dot(iv[:, sl], t["kh"][:, sl], _TN)
            o_ref[:, sl] = o_h
            rr = lax.rsqrt(jnp.mean(o_h * o_h, axis=-1, keepdims=True) + EPS)
            og = og_ref[:, sl]
            yb_ref[:, sl] = (o_h * rr * ng_ref[:, sl] * (og * _sigmoid(og))).astype(BF16)

    def col(k):
        return pl.BlockSpec((C, HW), lambda c: (c, k))

    base = 2
    return pl.pallas_call(
        body, name="hgrn_fwd", grid=(nc,),
        in_specs=[col(base), col(base + 1), col(base + 2), col(base + 3),
                  pl.BlockSpec((2, HW), lambda c: (0, 0)), pl.BlockSpec((1, HW), lambda c: (0, 0))],
        out_specs=[pl.BlockSpec((C, HW), lambda c: (c, 0)), pl.BlockSpec((C, HW), lambda c: (c, 0)),
                   pl.BlockSpec((None, H, dk, dk), lambda c: (c, 0, 0, 0))],
        out_shape=[jax.ShapeDtypeStruct((S, HW), BF16), jax.ShapeDtypeStruct((S, HW), F32),
                   jax.ShapeDtypeStruct((nc, H, dk, dk), F32)],
        scratch_shapes=[pltpu.VMEM((H, dk, dk), F32)],
        compiler_params=_cparams(("arbitrary",)))(z, z, z, z, hg_lb, ng)


def _hg_bwd(z, o, states, dyb, hg_lb, ng, HW):
    S = z.shape[0]
    C, H, dk = HG_CHUNK, HW // HG_DK, HG_DK
    nc = S // C

    def body(q_ref, f_ref, i_ref, og_ref, o_ref, st_ref, stn_ref, dyb_ref, hlb_ref, ng_ref, dz_ref, dng_ref, dhlb_ref,
             dstate, cross, dqa_buf, dkk_buf, db_buf, dlb_acc):
        c = pl.program_id(0)

        @pl.when(c == 0)
        def _():
            dstate[...] = jnp.zeros_like(dstate)
            dlb_acc[...] = jnp.zeros_like(dlb_acc)
            dng_ref[...] = jnp.zeros_like(dng_ref)

        def r16(v):
            return v.astype(BF16).astype(F32)

        t = _hg_common(q_ref, f_ref, hlb_ref)
        iv = i_ref[...]
        for h in range(H):
            sl = slice(h * dk, (h + 1) * dk)
            o_h, og, dyb_h, ng_h = o_ref[:, sl], og_ref[:, sl], dyb_ref[:, sl], ng_ref[:, sl]
            sg = _sigmoid(og)
            silu_og = og * sg
            rr = lax.rsqrt(jnp.mean(o_h * o_h, axis=-1, keepdims=True) + EPS)
            on = o_h * rr
            dng_ref[0:1, sl] += jnp.sum(dyb_h * on * silu_og, axis=0, keepdims=True)
            dz_ref[:, 3 * HW + h * dk:3 * HW + (h + 1) * dk] = (dyb_h * on * ng_h * (sg * (1.0 + og * (1.0 - sg)))).astype(BF16)
            don = dyb_h * ng_h * silu_og
            do_h = rr * (don - on * jnp.mean(don * on, axis=-1, keepdims=True))

            qt, kt, qh, kh, iv_h = t["qt"][:, sl], t["kt"][:, sl], t["qh"][:, sl], t["kh"][:, sl], iv[:, sl]
            a = jnp.where(t["tril"], _dot(qt, kt, _NT), 0.0)
            da = jnp.where(t["tril"], _dot(do_h, iv_h, _NT), 0.0)
            st, dst = st_ref[h], dstate[h]
            cross[:, sl] = jnp.sum(dst * stn_ref[h], axis=0, keepdims=True)
            dqh = _dot(do_h, st)
            dstate[h] = _dot(do_h, qh, _TN) + dst * t["e_l"][:, sl]
            div = _dot(a, do_h, _TN) + _dot(kh, dst, _NT)
            dkh = _dot(iv_h, dst)
            dqt = _dot(da, kt)
            dkt = _dot(da, qt, _TN)
            dz_ref[:, 2 * HW + h * dk:2 * HW + (h + 1) * dk] = div.astype(BF16)
            dqa_buf[:, sl] = dqh * t["e_b"][:, sl] + dqt * t["e_qm"][:, sl]
            dkk_buf[:, sl] = dkt * t["e_km"][:, sl] + dkh * t["e_kl"][:, sl]
            db_buf[:, sl] = r16(qt) * dqt - r16(kt) * dkt + r16(qh) * dqh - r16(kh) * dkh

        dqa, dkk = dqa_buf[...], dkk_buf[...]
        triu = jnp.logical_not(t["tril"]) | (lax.broadcasted_iota(jnp.int32, (C, C), 0) == lax.broadcasted_iota(jnp.int32, (C, C), 1))
        dlf = _ones_dot(triu.astype(BF16), db_buf[...]) + cross[...]
        df = dlf / t["f"] - dkk
        sig, lb = t["sig"], t["lb"]
        dz_ref[:, HW:2 * HW] = (df * (1.0 - lb) * sig * (1.0 - sig)).astype(BF16)
        dlb_acc[...] += jnp.sum(df * (1.0 - sig), axis=0, keepdims=True)
        q, sq = t["q"], t["sq"]
        dz_ref[:, 0:HW] = (dqa * (sq * (1.0 + q * (1.0 - sq)))).astype(BF16)

        @pl.when(c == nc - 1)
        def _():
            da0 = dlb_acc[...] * lb * (1.0 - lb)
            dhlb_ref[0:1, :] = da0
            dhlb_ref[1:2, :] = -da0

    def col(k):
        return pl.BlockSpec((C, HW), lambda c: (nc - 1 - c, k))

    base = 2
    return pl.pallas_call(
        body, name="hgrn_bwd", grid=(nc,),
        in_specs=[col(base), col(base + 1), col(base + 2), col(base + 3), col(0),
                  pl.BlockSpec((None, H, dk, dk), lambda c: (nc - 1 - c, 0, 0, 0)),
                  pl.BlockSpec((None, H, dk, dk), lambda c: (jnp.minimum(nc - c, nc - 1), 0, 0, 0)), col(0),
                  pl.BlockSpec((2, HW), lambda c: (0, 0)), pl.BlockSpec((1, HW), lambda c: (0, 0))],
        out_specs=[pl.BlockSpec((C, 4 * HW), lambda c: (nc - 1 - c, 0)), pl.BlockSpec((8, HW), lambda c: (0, 0)),
                   pl.BlockSpec((2, HW), lambda c: (0, 0))],
        out_shape=[jax.ShapeDtypeStruct((S, 4 * HW), BF16), jax.ShapeDtypeStruct((8, HW), F32), jax.ShapeDtypeStruct((2, HW), F32)],
        scratch_shapes=[pltpu.VMEM((H, dk, dk), F32), pltpu.VMEM((1, HW), F32), pltpu.VMEM((C, HW), F32), pltpu.VMEM((C, HW), F32),
                        pltpu.VMEM((C, HW), F32), pltpu.VMEM((1, HW), F32)],
        compiler_params=_cparams(("arbitrary",)))(z, z, z, z, o, states, states, dyb, hg_lb, ng)


def _position():
    x, y, c = lax.axis_index("x"), lax.axis_index("y"), lax.axis_index("c")
    return x, y, c, 4 * x + 2 * y + c


def _flip(x, y, c, k):
    return (1 - x if k & 4 else x, 1 - y if k & 2 else y, 1 - c if k & 1 else c)


def _allgather_small(name, v):
    R, L = v.shape

    def body(v_ref, out_ref, send_sems, recv_sems):
        x, y, c, me = _position()
        out_ref[me] = v_ref[...]
        copies = []
        for k in range(1, N_DEV):
            cp = pltpu.make_async_remote_copy(src_ref=v_ref, dst_ref=out_ref.at[me], send_sem=send_sems.at[k - 1],
                                              recv_sem=recv_sems.at[k - 1], device_id=_flip(x, y, c, k), device_id_type=MESH)
            cp.start()
            copies.append(cp)
        for cp in copies:
            cp.wait()

    return pl.pallas_call(
        body, name=name, out_shape=jax.ShapeDtypeStruct((N_DEV, R, L), v.dtype),
        in_specs=[pl.BlockSpec(memory_space=pltpu.VMEM)], out_specs=pl.BlockSpec(memory_space=pltpu.VMEM),
        scratch_shapes=[pltpu.SemaphoreType.DMA((N_DEV - 1,)), pltpu.SemaphoreType.DMA((N_DEV - 1,))],
        compiler_params=pltpu.CompilerParams(vmem_limit_bytes=VMEM_LIMIT),
    )(v)


def _allgather_hbm(name, shards):
    n = len(shards)

    def body(*refs):
        ins, outs = refs[:n], refs[n:2 * n]
        send_sems, recv_sems, local_sems = refs[2 * n:]
        x, y, c, me = _position()
        sibling = (x, y, 1 - c)
        chips = [(1 - x, y), (x, 1 - y), (1 - x, 1 - y)]

        def slot(px, py, pc):
            return 4 * px + 2 * py + pc

        def copy(w, k, block, to, src=None):
            dst = outs[w].at[slot(*block)]
            return pltpu.make_async_remote_copy(src_ref=dst if src is None else src, dst_ref=dst, send_sem=send_sems.at[w, k],
                                                recv_sem=recv_sems.at[w, k], device_id=to, device_id_type=MESH)

        mine, first, passed = [], [], []
        for w in range(n):
            cp = pltpu.make_async_copy(ins[w], outs[w].at[me], local_sems.at[w])
            cp.start()
            mine.append(cp)
            for j, chip in enumerate(chips):
                first.append(copy(w, 1 + j, (x, y, c), (*chip, c), src=ins[w]))
            first.append(copy(w, 0, (x, y, c), sibling, src=ins[w]))
        for cp in first:
            cp.start()
        for w in range(n):
            for j, chip in enumerate(chips):
                copy(w, 1 + j, (*chip, c), (x, y, c)).wait_recv()
                cp = copy(w, 4 + j, (*chip, c), sibling)
                cp.start()
                passed.append(cp)
        for w in range(n):
            copy(w, 0, sibling, (x, y, c)).wait_recv()
            for j, chip in enumerate(chips):
                copy(w, 4 + j, (*chip, 1 - c), (x, y, c)).wait_recv()
        for cp in first + passed:
            cp.wait_send()
        for cp in mine:
            cp.wait()

    hbm = pl.BlockSpec(memory_space=pltpu.HBM)
    return pl.pallas_call(
        body, name=name, out_shape=[jax.ShapeDtypeStruct((N_DEV, *s.shape), s.dtype) for s in shards],
        in_specs=[hbm] * n, out_specs=[hbm] * n,
        scratch_shapes=[pltpu.SemaphoreType.DMA((n, 7)), pltpu.SemaphoreType.DMA((n, 7)), pltpu.SemaphoreType.DMA((n,))],
    )(*shards)


_HBM = pl.BlockSpec(memory_space=pltpu.HBM)
_SEM = pl.BlockSpec(memory_space=pltpu.SEMAPHORE)
_EFFECT = pltpu.SideEffectType.DATAFLOW_SIDE_EFFECTING


def _split_start(name, bufs, n_sems, copies_fn):
    nb = len(bufs)

    def body(*refs):
        for cp in copies_fn(refs[:nb], refs[nb], refs[nb + 1]):
            cp.start()
        refs[-1][...] = jnp.zeros_like(refs[-1])

    sems = pltpu.SemaphoreType.DMA((n_sems,))
    res = pl.pallas_call(
        body, name=name,
        out_shape=(sems, sems, *[pltpu.HBM(a.shape, a.dtype) for a in bufs], jax.ShapeDtypeStruct((8, LANES), F32)),
        in_specs=[_HBM] * nb, out_specs=(_SEM, _SEM, *[_HBM] * nb, pl.BlockSpec(memory_space=pltpu.VMEM)),
        input_output_aliases={i: 2 + i for i in range(nb)},
        compiler_params=pltpu.CompilerParams(has_side_effects=_EFFECT),
    )(*[pltpu.with_memory_space_constraint(a, pltpu.HBM) for a in bufs])
    return res[0], res[1], list(res[2:2 + nb]), res[-1]


def _split_wait(name, bufs, send_sems, recv_sems, after, copies_fn):
    nb = len(bufs)

    def body(*refs):
        for cp in copies_fn(refs[:nb], refs[nb], refs[nb + 1]):
            cp.wait_send()
            cp.wait_recv()

    res = pl.pallas_call(
        body, name=name, out_shape=tuple(pltpu.HBM(a.shape, a.dtype) for a in bufs),
        in_specs=[_HBM] * nb + [_SEM, _SEM, pl.BlockSpec(memory_space=pl.ANY)], out_specs=tuple([_HBM] * nb),
        input_output_aliases={i: i for i in range(nb)},
        compiler_params=pltpu.CompilerParams(has_side_effects=_EFFECT),
    )(*bufs, send_sems, recv_sems, after)
    return list(res)


N_CHIP = 4


def _to_sibling_copies(n):
    def copies(bufs, send_sems, recv_sems):
        x, y, c, _ = _position()
        out = []
        for w in range(n):
            for q in range(N_CHIP):
                out.append(pltpu.make_async_remote_copy(
                    src_ref=bufs[w].at[2 * q + 1 - c], dst_ref=bufs[n + w].at[q], send_sem=send_sems.at[w * N_CHIP + q],
                    recv_sem=recv_sems.at[w * N_CHIP + q], device_id=(x, y, 1 - c), device_id_type=MESH))
        return out
    return copies


def _to_owner_copies(n):
    def copies(bufs, send_sems, recv_sems):
        x, y, c, _ = _position()
        out = []
        for w in range(n):
            for k in range(1, N_CHIP):
                px, py = (1 - x if k & 2 else x), (1 - y if k & 1 else y)
                out.append(pltpu.make_async_remote_copy(
                    src_ref=bufs[w].at[2 * px + py], dst_ref=bufs[n + w].at[k - 1], send_sem=send_sems.at[w * (N_CHIP - 1) + k - 1],
                    recv_sem=recv_sems.at[w * (N_CHIP - 1) + k - 1], device_id=(px, py, c), device_id_type=MESH))
        return out
    return copies


def _chip_sum(name, stack, landed, c_idx):
    _, R, C = stack.shape
    tr = _tile(R, max(16, 262144 // C), 16)

    def body(c_ref, a_ref, b_ref, o_ref):
        o_ref[...] = (a_ref[...].astype(F32) + b_ref[...].astype(F32)).astype(o_ref.dtype)

    return pl.pallas_call(
        body, name=name,
        grid_spec=pltpu.PrefetchScalarGridSpec(
            num_scalar_prefetch=1, grid=(N_CHIP, R // tr),
            in_specs=[pl.BlockSpec((None, tr, C), lambda q, i, c_ref: (2 * q + c_ref[0], i, 0)),
                      pl.BlockSpec((None, tr, C), lambda q, i, c_ref: (q, i, 0))],
            out_specs=pl.BlockSpec((None, tr, C), lambda q, i, c_ref: (q, i, 0))),
        out_shape=jax.ShapeDtypeStruct((N_CHIP, R, C), stack.dtype),
        compiler_params=_cparams(("parallel", "parallel")))(c_idx, stack, landed)


def _ada_mod(c16, w):
    _, D = c16.shape
    n = w.shape[1]
    tk = _tile(D, 512)
    nk = D // tk

    def body(c_ref, w_ref, o_ref, ca_ref):
        @pl.when(pl.program_id(0) == 0)
        def _():
            o_ref[...] = jnp.zeros_like(o_ref)

        cv = c_ref[...]
        ca = cv * _sigmoid(cv)
        ca_ref[...] = ca
        o_ref[...] += _dot(ca, w_ref[...])

    return pl.pallas_call(
        body, name="ada_mod", grid=(nk,),
        in_specs=[pl.BlockSpec((16, tk), lambda k: (0, k)), pl.BlockSpec((tk, n), lambda k: (k, 0))],
        out_specs=[pl.BlockSpec((16, n), lambda k: (0, 0)), pl.BlockSpec((16, tk), lambda k: (0, k))],
        out_shape=[jax.ShapeDtypeStruct((16, n), F32), jax.ShapeDtypeStruct((16, D), F32)],
        compiler_params=_cparams(("arbitrary",)))(c16, w)


def _adam_math(w, g, m, v):
    m2 = ADAM_B1 * m + (1.0 - ADAM_B1) * g
    v2 = ADAM_B2 * v + (1.0 - ADAM_B2) * (g * g)
    m_hat = m2 / (1.0 - ADAM_B1 ** ADAM_STEP)
    v_hat = v2 / (1.0 - ADAM_B2 ** ADAM_STEP)
    delta = -ADAM_LR * (m_hat / (jnp.sqrt(v_hat) + ADAM_EPS) + ADAM_WD * w)
    return delta, m2, v2


def _adamw(name, w, m, v, parts):
    R, C = w.shape
    tr = _tile(R, max(16, 131072 // C), 16)
    n_p = len(parts)

    def body(*refs):
        w_ref, m_ref, v_ref = refs[:3]
        g_ref, d_ref, m2_ref, v2_ref = refs[3 + n_p:]
        g = None
        for p_ref in refs[3:3 + n_p]:
            for s in range(p_ref.shape[0]):
                t = p_ref[s].astype(F32)
                g = t if g is None else g + t
        delta, m2, v2 = _adam_math(w_ref[...], g, m_ref[...], v_ref[...])
        g_ref[...] = g
        d_ref[...] = delta
        m2_ref[...] = m2
        v2_ref[...] = v2

    blk = pl.BlockSpec((tr, C), lambda i: (i, 0))
    out = jax.ShapeDtypeStruct((R, C), F32)
    return pl.pallas_call(body, name=name, grid=(R // tr,),
                          in_specs=[blk, blk, blk] + [pl.BlockSpec((a.shape[0], tr, C), lambda i: (0, i, 0)) for a in parts],
                          out_specs=[blk] * 4, out_shape=[out] * 4, compiler_params=_cparams(("parallel",)))(w, m, v, *parts)


def _small_update(gathered, w, m, v):
    _, R, L = gathered.shape
    rs = w.shape[0]

    def body(p_ref, w_ref, m_ref, v_ref, g_ref, d_ref, m2_ref, v2_ref):
        g = p_ref[0]
        for p in range(1, N_DEV):
            g = g + p_ref[p]
        g_ref[...] = g
        delta, m2, v2 = _adam_math(w_ref[...], g[0:rs, :], m_ref[...], v_ref[...])
        d_ref[...] = delta
        m2_ref[...] = m2
        v2_ref[...] = v2

    vm = pl.BlockSpec(memory_space=pltpu.VMEM)
    sm = jax.ShapeDtypeStruct((rs, L), F32)
    return pl.pallas_call(body, name="small_update", in_specs=[vm] * 4, out_specs=[vm] * 4,
                          out_shape=[jax.ShapeDtypeStruct((R, L), F32), sm, sm, sm],
                          compiler_params=pltpu.CompilerParams(vmem_limit_bytes=VMEM_LIMIT))(gathered, w, m, v)


def _local_step(x, tgt, mod, p, w, scatter=None):
    S, D = x.shape
    GW, HW = p["ln_g"].shape[1], p["hg_ng"].shape[1]
    G, T, _ = p["ws"].shape
    F = w["fo"].shape[0]
    in_loc, br_loc, fi_loc = w["in"].shape[2], w["bg"].shape[2], w["fi"].shape[2]
    INW = in_loc * N_DEV
    assert GW == HW and INW == 2 * GW + 4 * HW + 2 * D and fi_loc * N_DEV == 2 * F and F % fi_loc == 0
    sh1, sc1, gt1, sh2, sc2, gt2 = (mod[:, k * D:(k + 1) * D] for k in range(6))
    bsb = jnp.broadcast_to(p["bs"][:, :, None], (G, T, GW // G))

    tm = _tile(S, 1024, 16)
    tmh = _tile(S, 512, 16)
    tn_in = _tile(in_loc, 640)
    tn_d = _tile(D, 512)
    tn_br = _tile(br_loc, 512)
    tk_s = _tile(S, 1024)
    g_off = 2 * GW + 4 * HW

    h1 = _norm_mod("norm1", x, p["norm1_g"], sc1, sh1)
    z = _mm_nn_stacked("proj_in", h1, w["in"], tm=tm, tn=tn_in, tk=D)[0]
    ya = _gmlp_fwd(z, p["ln_g"], p["ln_b"], p["ws"], bsb, GW)
    yb, o_hg, states = _hg_fwd(z, p["hg_lb"], p["hg_ng"], HW)
    pa = _mm_nn_stacked("branch_gmlp", ya, w["bg"], tm=tm, tn=tn_br, tk=GW)[0]

    def gates(ga_ref, gb_ref, ba_ref, bb_ref):
        return _sigmoid(ga_ref[...] + ba_ref[...]), _sigmoid(gb_ref[...] + bb_ref[...])

    def gate_specs(tn_):
        o1, o2 = g_off // tn_, (g_off + D) // tn_
        return [pl.BlockSpec((tm, tn_), lambda i, j, k: (i, o1 + j)), pl.BlockSpec((tm, tn_), lambda i, j, k: (i, o2 + j)),
                pl.BlockSpec((1, tn_), lambda i, j, k: (0, j)), pl.BlockSpec((1, tn_), lambda i, j, k: (0, D // tn_ + j))]

    def merge_ep(acc, ex, outs):
        ga, gb = gates(*ex[1:5])
        outs[0][...] = acc
        outs[1][...] = (ga * ex[0][...] + gb * acc).astype(BF16)

    tile_o = pl.BlockSpec((tm, tn_br), lambda i, j, k: (i, j))
    pb, y = _mm_nn_stacked(
        "branch_hg_merge", yb, w["bh"], tm=tm, tn=tn_br, tk=HW, extras=[pa, z, z, p["b_gate"], p["b_gate"]],
        extra_specs=[tile_o, *gate_specs(tn_br)], out_shapes=[jax.ShapeDtypeStruct((S, D), F32), jax.ShapeDtypeStruct((S, D), BF16)],
        out_specs=[tile_o, tile_o], epilogue=merge_ep)

    def resid_ep(acc, ex, outs):
        outs[0][...] = acc
        outs[1][...] = ex[0][...] + ex[1][...] * acc

    def resid_mm(name, a, b, res, gt, tk):
        K = a.shape[1]
        t_o = pl.BlockSpec((tm, tn_d), lambda i, j, k: (i, j))
        return _matmul(
            name, a, b, dims=_NN, grid_mnk=(S // tm, D // tn_d, K // tk), tiles=(tm, tn_d),
            a_spec=pl.BlockSpec((tm, tk), lambda i, j, k: (i, k)), b_spec=pl.BlockSpec((tk, tn_d), lambda i, j, k: (k, j)),
            extras=[res, gt], extra_specs=[t_o, pl.BlockSpec((1, tn_d), lambda i, j, k: (0, j))],
            out_shapes=[jax.ShapeDtypeStruct((S, D), F32)] * 2, out_specs=[t_o, t_o], epilogue=resid_ep)

    o1, xm = resid_mm("proj_out", y, w["out"], x, gt1, D)
    h2 = _norm_mod("norm2", xm, p["norm2_g"], sc2, sh2)
    ab = _mm_nn_stacked("ffn_in", h2, w["fi"], tm=tm, tn=fi_loc, tk=D)[0]
    hf = _swiglu(ab, F, fi_loc)
    o2, x3 = resid_mm("ffn_out", hf, w["fo"], xm, gt2, _tile(F, 1536))
    dx3, do2, vec_l = _loss_head(x3, tgt, p["final_g"], o2, gt2)

    nf = F // fi_loc

    def dswiglu_ep(acc, ex, outs):
        a, up = ex[0][...], ex[1][...]
        sa = _sigmoid(a)
        outs[0][0] = (acc * up * (sa * (1.0 + a * (1.0 - sa)))).astype(BF16)
        outs[0][1] = (acc * (a * sa)).astype(BF16)

    dab = _matmul(
        "ffn_out_dx", do2, w["fo"], dims=_NT, grid_mnk=(S // tmh, nf, 1), tiles=(tmh, fi_loc),
        a_spec=pl.BlockSpec((tmh, D), lambda i, j, k: (i, 0)), b_spec=pl.BlockSpec((fi_loc, D), lambda i, j, k: (j, 0)),
        extras=[ab, ab], extra_specs=[pl.BlockSpec((tmh, fi_loc), lambda i, j, k: (i, j)), pl.BlockSpec((tmh, fi_loc), lambda i, j, k: (i, j + nf))],
        out_shapes=[jax.ShapeDtypeStruct((2, S, F), BF16)], out_specs=[pl.BlockSpec((2, tmh, fi_loc), lambda i, j, k: (0, i, j))],
        epilogue=dswiglu_ep)[0]
    tm_f = _tile(F, 512)
    g_fo = _mm_tn("ffn_out_dw", hf, do2, pl.BlockSpec((tk_s, tn_d), lambda i, j, k: (k, j)), Mo=F, No=D, S=S, tm=tm_f, tn=tn_d, tk=tk_s)
    dh2 = _mm_nt_stacked("ffn_in_dx", pl.BlockSpec((None, tm, fi_loc), lambda i, j, k: (k // nf, i, k % nf)), dab, w["fi"],
                         M=S, tm=tm, tn=tn_d, tk=fi_loc)
    g_fi = _mm_tn("ffn_in_dw", h2, dab, pl.BlockSpec((None, tk_s, fi_loc), lambda i, j, k: (j // nf, k, j % nf)),
                  Mo=D, No=2 * F, S=S, tm=tn_d, tn=fi_loc, tk=tk_s, stacked_nloc=fi_loc)
    tie = (lambda name, grads: scatter[0](name, grads)[0:1, 0:1]) if scatter is not None else (lambda name, grads: 0.0)
    push = (lambda name, after: scatter[1](name, after)[0:1, 0:1]) if scatter is not None else (lambda name, after: 0.0)
    dxm, vec2, do1 = _norm_mod_bwd("norm2_bwd", dh2, xm, p["norm2_g"], sc2, dx3, o1, gt1 + tie("scatter_ffn", dict(fo=g_fo, fi=g_fi)))

    def dmerge_ep(acc, ex, outs):
        ga, gb = gates(*ex[2:6])
        outs[0][...] = (acc * ga).astype(BF16)
        outs[1][...] = (acc * gb).astype(BF16)
        outs[2][0] = (acc * ex[0][...] * ga * (1.0 - ga)).astype(BF16)
        outs[2][1] = (acc * ex[1][...] * gb * (1.0 - gb)).astype(BF16)

    t_o = pl.BlockSpec((tm, tn_d), lambda i, j, k: (i, j))
    dpa, dpb, dg2 = _matmul(
        "proj_out_dx", do1, w["out"], dims=_NT, grid_mnk=(S // tm, D // tn_d, 1), tiles=(tm, tn_d),
        a_spec=pl.BlockSpec((tm, D), lambda i, j, k: (i, 0)), b_spec=pl.BlockSpec((tn_d, D), lambda i, j, k: (j, 0)),
        extras=[pa, pb, z, z, p["b_gate"], p["b_gate"]], extra_specs=[t_o, t_o, *gate_specs(tn_d)],
        out_shapes=[jax.ShapeDtypeStruct((S, D), BF16), jax.ShapeDtypeStruct((S, D), BF16), jax.ShapeDtypeStruct((2, S, D), BF16)],
        out_specs=[t_o, t_o, pl.BlockSpec((2, tm, tn_d), lambda i, j, k: (0, i, j))], epilogue=dmerge_ep)
    db_gate = _colsum2(dg2)
    g_out = _mm_tn("proj_out_dw", y, do1, pl.BlockSpec((tk_s, tn_d), lambda i, j, k: (k, j)), Mo=D, No=D, S=S, tm=tn_d, tn=tn_d, tk=tk_s)
    a_br = pl.BlockSpec((tm, br_loc), lambda i, j, k: (i, k))
    tn_g = _tile(GW, 512)
    dya = _mm_nt_stacked("branch_gmlp_dx", a_br, dpa, w["bg"], M=S, tm=tm, tn=tn_g, tk=br_loc)
    dyb = _mm_nt_stacked("branch_hg_dx", a_br, dpb, w["bh"], M=S, tm=tm, tn=tn_g, tk=br_loc)
    b_br = pl.BlockSpec((tk_s, br_loc), lambda i, j, k: (k, j))
    g_bg = _mm_tn("branch_gmlp_dw", ya, dpa, b_br, Mo=GW, No=D, S=S, tm=tn_g, tn=br_loc, tk=tk_s, stacked_nloc=br_loc)
    g_bh = _mm_tn("branch_hg_dw", yb, dpb, b_br, Mo=HW, No=D, S=S, tm=tn_g, tn=br_loc, tk=tk_s, stacked_nloc=br_loc)
    ln_b_tied = p["ln_b"] + push("scatter_ffn", dpa) + tie("scatter_mixer", dict(out=g_out, bg=g_bg, bh=g_bh))
    dz_gmlp, dln, dws, dbs = _gmlp_bwd(z, dya, p["ln_g"], ln_b_tied, p["ws"], bsb, GW)
    dz_hg, dng, dhlb = _hg_bwd(z, o_hg, states, dyb, p["hg_lb"], p["hg_ng"], HW)
    dz = jnp.concatenate([dz_gmlp, dz_hg, dg2[0], dg2[1]], axis=1)
    g_in = _mm_tn("proj_in_dw", h1, dz, pl.BlockSpec((tk_s, tn_in), lambda i, j, k: (k, j)), Mo=D, No=INW, S=S,
                  tm=tn_d, tn=tn_in, tk=tk_s, stacked_nloc=in_loc)
    sc1_tied = sc1 + push("scatter_mixer", dz_hg) + tie("scatter_proj_in", dict(w_in=g_in))
    dh1 = _mm_nt_stacked("proj_in_dx", pl.BlockSpec((tm, tn_in), lambda i, j, k: (i, k)), dz, w["in"], M=S, tm=tm, tn=tn_d, tk=tn_in)
    dx, vec1 = _norm_mod_bwd("norm1_bwd", dh1, x, p["norm1_g"], sc1_tied, dxm)

    dmod = jnp.concatenate([vec1[0:1], vec1[1:2], vec2[3:4], vec2[0:1], vec2[1:2], vec_l[2:3]], axis=1)
    small = dict(norm1_g=vec1[2:3], b_gate=db_gate.reshape(1, 2 * D), ln_g=dln[0:1], ln_b=dln[1:2], ws=dws, bs=dbs.reshape(G, T),
                 hg_lb=dhlb, hg_ng=dng[0:1], norm2_g=vec2[2:3], final_g=vec_l[1:2], loss=vec_l[0:1, 0:LANES])
    big = dict(w_in=g_in, bg=g_bg, bh=g_bh, out=g_out, fi=g_fi, fo=g_fo)
    return dx, big, small, dmod


_SMALL = ("b_ada", "norm1_g", "b_gate", "ln_g", "ln_b", "ws", "bs", "hg_lb", "hg_ng", "norm2_g", "final_g")


def _pack(parts, rows_mult=8):
    flat = [a.reshape(-1) for a in parts]
    offs, n = [], 0
    for a in flat:
        offs.append(n)
        n += a.shape[0]
    pad = (-n) % (LANES * rows_mult)
    if pad:
        flat.append(jnp.zeros((pad,), F32))
    return jnp.concatenate(flat).reshape(-1, LANES), offs


def kernel(x, c, w_ada, b_ada, norm1_g, w_in, b_gate, gmlp_ln_g, gmlp_ln_b, gmlp_ws, gmlp_bs, hg_lb, hg_norm_g, w_branch_gmlp, w_branch_hg, w_out, norm2_g, w_ffn_in, w_ffn_out, final_norm_g, loss_target, m_w_ada, m_b_ada, m_norm1_g, m_w_in, m_b_gate, m_gmlp_ln_g, m_gmlp_ln_b, m_gmlp_ws, m_gmlp_bs, m_hg_lb, m_hg_norm_g, m_w_branch_gmlp, m_w_branch_hg, m_w_out, m_norm2_g, m_w_ffn_in, m_w_ffn_out, m_final_norm_g, v_w_ada, v_b_ada, v_norm1_g, v_w_in, v_b_gate, v_gmlp_ln_g, v_gmlp_ln_b, v_gmlp_ws, v_gmlp_bs, v_hg_lb, v_hg_norm_g, v_w_branch_gmlp, v_w_branch_hg, v_w_out, v_norm2_g, v_w_ffn_in, v_w_ffn_out, v_final_norm_g):
    S, D = x.shape[1], x.shape[2]
    ada_loc = w_ada.shape[2]
    me = 4 * lax.axis_index("x") + 2 * lax.axis_index("y") + lax.axis_index("c")

    c_all = _allgather_small("gather_c", c.reshape(D // LANES, LANES)).reshape(N_DEV, D)
    mod_cols, c_act = _ada_mod(jnp.pad(c_all, ((0, 16 - N_DEV), (0, 0))), w_ada[0])
    mod_all = _allgather_small("gather_mod", mod_cols[:N_DEV].reshape(-1, LANES)).reshape(N_DEV, N_DEV, ada_loc)
    mod = lax.dynamic_index_in_dim(mod_all, me, axis=1, keepdims=False).reshape(1, N_DEV * ada_loc) + b_ada

    shards = [a[0].astype(BF16) for a in (w_in, w_branch_gmlp, w_branch_hg, w_out, w_ffn_in, w_ffn_out)]
    g_in, g_bg, g_bh, g_out, g_fi, g_fo = _allgather_hbm("gather_weights", shards)
    w = {"in": g_in, "bg": g_bg, "bh": g_bh, "out": g_out.reshape(-1, D), "fi": g_fi, "fo": g_fo.reshape(-1, D)}
    p = dict(norm1_g=norm1_g, b_gate=b_gate, ln_g=gmlp_ln_g, ln_b=gmlp_ln_b, ws=gmlp_ws[0], bs=gmlp_bs[0], hg_lb=hg_lb,
             hg_ng=hg_norm_g, norm2_g=norm2_g, final_g=final_norm_g.reshape(1, D))

    in_flight = {}
    c_idx = lax.axis_index("c").astype(jnp.int32).reshape(1)
    my_chip = 2 * lax.axis_index("x") + lax.axis_index("y")

    def empty_hbm(shape, dtype):
        return pltpu.with_memory_space_constraint(lax.empty(shape, dtype), pltpu.HBM)

    def scatter_start(name, grads):
        keys = list(grads)
        n = len(keys)
        stacks = [grads[k].reshape(N_DEV, -1, grads[k].shape[-1]) for k in keys]
        lands = [empty_hbm((N_CHIP, *g.shape[1:]), g.dtype) for g in stacks]
        send_sems, recv_sems, bufs, token = _split_start(name + "_d2d", stacks + lands, n * N_CHIP, _to_sibling_copies(n))
        in_flight[name] = dict(keys=keys, stage1=(send_sems, recv_sems, bufs))
        return token

    def scatter_push(name, after):
        f = in_flight[name]
        n = len(f["keys"])
        send_sems, recv_sems, bufs = f["stage1"]
        bufs = _split_wait(name + "_d2d_wait", bufs, send_sems, recv_sems, after, _to_sibling_copies(n))
        sums = [_chip_sum(f"{name}_sum_{k}", bufs[i], bufs[n + i], c_idx) for i, k in enumerate(f["keys"])]
        lands = [empty_hbm((N_CHIP - 1, *s.shape[1:]), s.dtype) for s in sums]
        send_sems, recv_sems, bufs, token = _split_start(name + "_ici", sums + lands, n * (N_CHIP - 1), _to_owner_copies(n))
        f["stage2"] = (send_sems, recv_sems, bufs)
        return token

    grad_x, _, small, dmod = _local_step(x[0], loss_target[0], mod, p, w, (scatter_start, scatter_push))
    scatter_push("scatter_proj_in", grad_x)

    small["b_ada"] = dmod
    packed, offs = _pack([small[k] for k in _SMALL] + [small["loss"]])
    gathered = _allgather_small("gather_small", packed)
    wp = dict(p, b_ada=b_ada)
    ms = dict(b_ada=m_b_ada, norm1_g=m_norm1_g, b_gate=m_b_gate, ln_g=m_gmlp_ln_g, ln_b=m_gmlp_ln_b, ws=m_gmlp_ws, bs=m_gmlp_bs,
              hg_lb=m_hg_lb, hg_ng=m_hg_norm_g, norm2_g=m_norm2_g, final_g=m_final_norm_g)
    vs = dict(b_ada=v_b_ada, norm1_g=v_norm1_g, b_gate=v_b_gate, ln_g=v_gmlp_ln_g, ln_b=v_gmlp_ln_b, ws=v_gmlp_ws, bs=v_gmlp_bs,
              hg_lb=v_hg_lb, hg_ng=v_hg_norm_g, norm2_g=v_norm2_g, final_g=v_final_norm_g)
    w_sm, _ = _pack([wp[k] for k in _SMALL])
    m_sm, _ = _pack([ms[k] for k in _SMALL])
    v_sm, _ = _pack([vs[k] for k in _SMALL])
    sm_out = _small_update(gathered, w_sm, m_sm, v_sm)
    shapes = dict(b_ada=b_ada.shape, norm1_g=norm1_g.shape, b_gate=b_gate.shape, ln_g=gmlp_ln_g.shape, ln_b=gmlp_ln_b.shape,
                  ws=gmlp_ws.shape, bs=gmlp_bs.shape, hg_lb=hg_lb.shape, hg_ng=hg_norm_g.shape, norm2_g=norm2_g.shape,
                  final_g=final_norm_g.shape)

    def unpack(arr, k):
        i = _SMALL.index(k)
        n = math.prod(shapes[k])
        return arr.reshape(-1)[offs[i]:offs[i] + n].reshape(shapes[k])

    loss = sm_out[0].reshape(-1)[offs[len(_SMALL)]]

    dmod_all = gathered.reshape(N_DEV, -1)[:, offs[0]:offs[0] + N_DEV * ada_loc]
    dmod_loc = lax.dynamic_slice_in_dim(dmod_all, me * ada_loc, ada_loc, axis=1)
    ca_t = jnp.pad(c_act[:N_DEV].T, ((0, 0), (0, LANES - N_DEV))).astype(BF16)
    dm_p = jnp.pad(dmod_loc, ((0, LANES - N_DEV), (0, 0))).astype(BF16)
    tm_a = _tile(D, 512)
    g_ada = _matmul(
        "ada_dw", ca_t, dm_p, dims=_NN, grid_mnk=(D // tm_a, 1, 1), tiles=(tm_a, ada_loc),
        a_spec=pl.BlockSpec((tm_a, LANES), lambda i, j, k: (i, 0)), b_spec=pl.BlockSpec((LANES, ada_loc), lambda i, j, k: (0, 0)),
        out_shapes=[jax.ShapeDtypeStruct((1, D, ada_loc), F32)], out_specs=[pl.BlockSpec((None, tm_a, ada_loc), lambda i, j, k: (0, i, 0))],
        epilogue=_store(F32))[0]

    upd = {"w_ada": _adamw("adamw_w_ada", w_ada[0], m_w_ada[0], v_w_ada[0], [g_ada])}
    big_w = dict(w_in=(w_in, m_w_in, v_w_in, "w_in"), bg=(w_branch_gmlp, m_w_branch_gmlp, v_w_branch_gmlp, "w_branch_gmlp"),
                 bh=(w_branch_hg, m_w_branch_hg, v_w_branch_hg, "w_branch_hg"), out=(w_out, m_w_out, v_w_out, "w_out"),
                 fi=(w_ffn_in, m_w_ffn_in, v_w_ffn_in, "w_ffn_in"), fo=(w_ffn_out, m_w_ffn_out, v_w_ffn_out, "w_ffn_out"))
    after = upd["w_ada"][1]
    for name in ("scatter_ffn", "scatter_mixer", "scatter_proj_in"):
        keys = in_flight[name]["keys"]
        n = len(keys)
        send_sems, recv_sems, bufs = in_flight[name]["stage2"]
        bufs = _split_wait(name + "_ici_wait", bufs, send_sems, recv_sems, after, _to_owner_copies(n))
        for i, k in enumerate(keys):
            wt, mt, vt, out_name = big_w[k]
            own_k = lax.dynamic_index_in_dim(bufs[i], my_chip, axis=0, keepdims=True)
            upd[out_name] = _adamw("adamw_" + out_name, wt[0], mt[0], vt[0], [own_k, bufs[n + i]])
            after = upd[out_name][1]

    order = ("w_ada", "b_ada", "norm1_g", "w_in", "b_gate", "ln_g", "ln_b", "ws", "bs", "hg_lb", "hg_ng", "w_branch_gmlp", "w_branch_hg",
             "w_out", "norm2_g", "w_ffn_in", "w_ffn_out", "final_g")
    outs = [loss, grad_x[None]]
    for idx in range(4):
        for k in order:
            outs.append(upd[k][idx][None] if k in upd else unpack(sm_out[idx], k))
    return tuple(outs)
```

```python
import functools
import math

import jax
import jax.numpy as jnp
from jax import lax
from jax.experimental import pallas as pl
from jax.experimental.pallas import tpu as pltpu

F32 = jnp.float32
BF16 = jnp.bfloat16
N_DEV = 8
EPS = 1e-6
LANES = 128
HG_DK = 128
HG_CHUNK = 64
HG_MID = HG_CHUNK // 2 - 1
EXP_CLAMP = 80.0
VMEM_LIMIT = 48 * 1024 * 1024
ADAM_LR, ADAM_B1, ADAM_B2, ADAM_EPS, ADAM_WD, ADAM_STEP = 0.001, 0.9, 0.999, 1e-08, 0.01, 10
MESH = pl.DeviceIdType.MESH

_NN = (((1,), (0,)), ((), ()))
_NT = (((1,), (1,)), ((), ()))
_TN = (((0,), (0,)), ((), ()))


def _dot(a, b, dims=_NN):
    return lax.dot_general(a.astype(BF16), b.astype(BF16), dims, preferred_element_type=F32)


def _tile(n, target, mult=LANES):
    best = None
    for t in range(mult, min(n, target) + 1, mult):
        if n % t == 0:
            best = t
    return n if best is None else best


def _cparams(sem):
    return pltpu.CompilerParams(dimension_semantics=sem, vmem_limit_bytes=VMEM_LIMIT)


def _sigmoid(x):
    return 1.0 / (1.0 + jnp.exp(-x))


def _gelu_parts(x):
    k0 = math.sqrt(2.0 / math.pi)
    x2 = x * x
    t = jnp.tanh(k0 * (x + 0.044715 * x * x2))
    g = 0.5 * x * (1.0 + t)
    dg = 0.5 * (1.0 + t) + 0.5 * x * (1.0 - t * t) * (k0 * (1.0 + 3.0 * 0.044715 * x2))
    return g, dg


def _split3(x):
    h = x.astype(BF16)
    r = x - h.astype(F32)
    m = r.astype(BF16)
    lo = (r - m.astype(F32)).astype(BF16)
    return h, m, lo


def _ones_dot(mat01, x):
    h, m, lo = _split3(x)
    d = functools.partial(lax.dot_general, dimension_numbers=_NN, preferred_element_type=F32)
    return d(mat01, h) + d(mat01, m) + d(mat01, lo)


def _matmul(name, a, b, *, dims, grid_mnk, tiles, a_spec, b_spec, extras=(), extra_specs=(), out_shapes, out_specs, epilogue):
    gm, gn, nk = grid_mnk
    tm, tn = tiles
    n_ex, n_out = len(extras), len(out_shapes)

    def body(*refs):
        a_ref, b_ref = refs[0], refs[1]
        ex = refs[2:2 + n_ex]
        outs = refs[2 + n_ex:2 + n_ex + n_out]
        if nk == 1:
            epilogue(lax.dot_general(a_ref[...], b_ref[...], dims, preferred_element_type=F32), ex, outs)
            return
        acc = refs[-1]
        k = pl.program_id(2)

        @pl.when(k == 0)
        def _():
            acc[...] = jnp.zeros_like(acc)

        acc[...] += lax.dot_general(a_ref[...], b_ref[...], dims, preferred_element_type=F32)

        @pl.when(k == nk - 1)
        def _():
            epilogue(acc[...], ex, outs)

    return pl.pallas_call(
        body, name=name, grid=(gm, gn, nk), in_specs=[a_spec, b_spec, *extra_specs], out_specs=list(out_specs),
        out_shape=list(out_shapes), scratch_shapes=[] if nk == 1 else [pltpu.VMEM((tm, tn), F32)],
        compiler_params=_cparams(("parallel", "parallel", "arbitrary")),
    )(a, b, *extras)


def _store(dtype):
    def ep(acc, ex, outs):
        outs[0][...] = acc.astype(dtype)
    return ep


def _mm_nn_stacked(name, a, wg, *, tm, tn, tk, out_dtype=F32, extras=(), extra_specs=(), out_shapes=None, out_specs=None, epilogue=None):
    M, K = a.shape
    _, _, nloc = wg.shape
    N = nloc * N_DEV
    q = nloc // tn
    if out_shapes is None:
        out_shapes = [jax.ShapeDtypeStruct((M, N), out_dtype)]
        out_specs = [pl.BlockSpec((tm, tn), lambda i, j, k: (i, j))]
        epilogue = _store(out_dtype)
    return _matmul(
        name, a, wg, dims=_NN, grid_mnk=(M // tm, N // tn, K // tk), tiles=(tm, tn),
        a_spec=pl.BlockSpec((tm, tk), lambda i, j, k: (i, k)),
        b_spec=pl.BlockSpec((None, tk, tn), lambda i, j, k: (j // q, k, j % q)),
        extras=extras, extra_specs=extra_specs, out_shapes=out_shapes, out_specs=out_specs, epilogue=epilogue)


def _mm_nt_stacked(name, a_spec, a, wg, *, M, tm, tn, tk, out_dtype=F32):
    _, Kw, nloc = wg.shape
    q = nloc // tk
    return _matmul(
        name, a, wg, dims=_NT, grid_mnk=(M // tm, Kw // tn, (nloc * N_DEV) // tk), tiles=(tm, tn),
        a_spec=a_spec, b_spec=pl.BlockSpec((None, tn, tk), lambda i, j, k: (k // q, j, k % q)),
        out_shapes=[jax.ShapeDtypeStruct((M, Kw), out_dtype)], out_specs=[pl.BlockSpec((tm, tn), lambda i, j, k: (i, j))],
        epilogue=_store(out_dtype))[0]


def _mm_tn(name, a, b, b_spec, *, Mo, No, S, tm, tn, tk, stacked_nloc=None):
    if stacked_nloc is None:
        out_shape = jax.ShapeDtypeStruct((Mo, No), BF16)
        out_spec = pl.BlockSpec((tm, tn), lambda i, j, k: (i, j))
    else:
        q = stacked_nloc // tn
        out_shape = jax.ShapeDtypeStruct((N_DEV, Mo, stacked_nloc), BF16)
        out_spec = pl.BlockSpec((None, tm, tn), lambda i, j, k: (j // q, i, j % q))
    return _matmul(
        name, a, b, dims=_TN, grid_mnk=(Mo // tm, No // tn, S // tk), tiles=(tm, tn),
        a_spec=pl.BlockSpec((tk, tm), lambda i, j, k: (k, i)), b_spec=b_spec,
        out_shapes=[out_shape], out_specs=[out_spec], epilogue=_store(BF16))[0]


def _norm_mod(name, x, g, sc, sh):
    S, D = x.shape
    tm = _tile(S, 256, 8)

    def body(x_ref, g_ref, sc_ref, sh_ref, h_ref):
        xv = x_ref[...]
        r = lax.rsqrt(jnp.mean(xv * xv, axis=-1, keepdims=True) + EPS)
        h = (xv * r) * g_ref[...]
        h_ref[...] = (h * (1.0 + sc_ref[...]) + sh_ref[...]).astype(BF16)

    row = pl.BlockSpec((tm, D), lambda i: (i, 0))
    vec = pl.BlockSpec((1, D), lambda i: (0, 0))
    return pl.pallas_call(body, name=name, grid=(S // tm,), in_specs=[row, vec, vec, vec], out_specs=row,
                          out_shape=jax.ShapeDtypeStruct((S, D), BF16), compiler_params=_cparams(("parallel",)))(x, g, sc, sh)


def _norm_mod_bwd(name, dh, x, g, sc, dres, o=None, gt=None):
    S, D = x.shape
    tm = _tile(S, 256, 8)
    gated = o is not None

    def body(*refs):
        if gated:
            dh_ref, x_ref, g_ref, sc_ref, dres_ref, o_ref, gt_ref, dx_ref, vec_ref, do_ref = refs
        else:
            dh_ref, x_ref, g_ref, sc_ref, dres_ref, dx_ref, vec_ref = refs
        i = pl.program_id(0)

        @pl.when(i == 0)
        def _():
            vec_ref[...] = jnp.zeros_like(vec_ref)

        xv, dh_v, gv = x_ref[...], dh_ref[...], g_ref[...]
        r = lax.rsqrt(jnp.mean(xv * xv, axis=-1, keepdims=True) + EPS)
        xn = xv * r
        one_sc = 1.0 + sc_ref[...]
        vec_ref[0:1, :] += jnp.sum(dh_v, axis=0, keepdims=True)
        vec_ref[1:2, :] += jnp.sum(dh_v * (xn * gv), axis=0, keepdims=True)
        vec_ref[2:3, :] += jnp.sum(dh_v * one_sc * xn, axis=0, keepdims=True)
        dxn = dh_v * one_sc * gv
        dx = dres_ref[...] + r * (dxn - xn * jnp.mean(dxn * xn, axis=-1, keepdims=True))
        dx_ref[...] = dx
        if gated:
            vec_ref[3:4, :] += jnp.sum(dx * o_ref[...], axis=0, keepdims=True)
            do_ref[...] = (dx * gt_ref[...]).astype(BF16)

    row = pl.BlockSpec((tm, D), lambda i: (i, 0))
    vec = pl.BlockSpec((1, D), lambda i: (0, 0))
    acc = pl.BlockSpec((8, D), lambda i: (0, 0))
    ins = [dh, x, g, sc, dres] + ([o, gt] if gated else [])
    in_specs = [row, row, vec, vec, row] + ([row, vec] if gated else [])
    out_shape = [jax.ShapeDtypeStruct((S, D), F32), jax.ShapeDtypeStruct((8, D), F32)]
    out_specs = [row, acc]
    if gated:
        out_shape.append(jax.ShapeDtypeStruct((S, D), BF16))
        out_specs.append(row)
    return pl.pallas_call(body, name=name, grid=(S // tm,), in_specs=in_specs, out_specs=out_specs, out_shape=out_shape,
                          compiler_params=_cparams(("arbitrary",)))(*ins)


def _loss_head(x3, tgt, gf, o2, gt2):
    S, D = x3.shape
    tm = _tile(S, 256, 8)

    def body(x_ref, t_ref, g_ref, o_ref, gt_ref, dx_ref, do_ref, vec_ref):
        i = pl.program_id(0)

        @pl.when(i == 0)
        def _():
            vec_ref[...] = jnp.zeros_like(vec_ref)

        xv, gv = x_ref[...], g_ref[...]
        r = lax.rsqrt(jnp.mean(xv * xv, axis=-1, keepdims=True) + EPS)
        xn = xv * r
        e = xn * gv - t_ref[...]
        tok = 0.5 * jnp.mean(e * e, axis=-1, keepdims=True)
        vec_ref[0:1, :] += jnp.broadcast_to(jnp.sum(tok, axis=0, keepdims=True), (1, D))
        dy = e * (1.0 / D)
        vec_ref[1:2, :] += jnp.sum(dy * xn, axis=0, keepdims=True)
        dxn = dy * gv
        dx = r * (dxn - xn * jnp.mean(dxn * xn, axis=-1, keepdims=True))
        dx_ref[...] = dx
        vec_ref[2:3, :] += jnp.sum(dx * o_ref[...], axis=0, keepdims=True)
        do_ref[...] = (dx * gt_ref[...]).astype(BF16)

    row = pl.BlockSpec((tm, D), lambda i: (i, 0))
    vec = pl.BlockSpec((1, D), lambda i: (0, 0))
    return pl.pallas_call(
        body, name="loss_head", grid=(S // tm,), in_specs=[row, row, vec, row, vec],
        out_specs=[row, row, pl.BlockSpec((8, D), lambda i: (0, 0))],
        out_shape=[jax.ShapeDtypeStruct((S, D), F32), jax.ShapeDtypeStruct((S, D), BF16), jax.ShapeDtypeStruct((8, D), F32)],
        compiler_params=_cparams(("arbitrary",)))(x3, tgt, gf, o2, gt2)


def _swiglu(ab, F, tf):
    S = ab.shape[0]
    tm = _tile(S, 512, 8)
    nf = F // tf

    def body(a_ref, u_ref, h_ref):
        a = a_ref[...]
        h_ref[...] = (a * _sigmoid(a) * u_ref[...]).astype(BF16)

    return pl.pallas_call(
        body, name="swiglu", grid=(S // tm, nf),
        in_specs=[pl.BlockSpec((tm, tf), lambda i, j: (i, j)), pl.BlockSpec((tm, tf), lambda i, j: (i, j + nf))],
        out_specs=pl.BlockSpec((tm, tf), lambda i, j: (i, j)), out_shape=jax.ShapeDtypeStruct((S, F), BF16),
        compiler_params=_cparams(("parallel", "parallel")))(ab, ab)


def _colsum2(dg2):
    _, S, D = dg2.shape
    tm = _tile(S, 256, 16)

    def body(x_ref, o_ref):
        @pl.when(pl.program_id(0) == 0)
        def _():
            o_ref[...] = jnp.zeros_like(o_ref)

        o_ref[0:1, :] += jnp.sum(x_ref[0].astype(F32), axis=0, keepdims=True)
        o_ref[1:2, :] += jnp.sum(x_ref[1].astype(F32), axis=0, keepdims=True)

    return pl.pallas_call(body, name="gate_bias_grad", grid=(S // tm,), in_specs=[pl.BlockSpec((2, tm, D), lambda i: (0, i, 0))],
                          out_specs=pl.BlockSpec((2, D), lambda i: (0, 0)), out_shape=jax.ShapeDtypeStruct((2, D), F32),
                          compiler_params=_cparams(("arbitrary",)))(dg2)


def _gmlp_common(u_ref, v_ref, lg_ref, lb_ref, ws_ref, bsb_ref, G, T, Dg):
    ug, dug = _gelu_parts(u_ref[...])
    vg, dvg = _gelu_parts(v_ref[...])
    mu = jnp.mean(vg, axis=-1, keepdims=True)
    vc = vg - mu
    rstd = lax.rsqrt(jnp.mean(vc * vc, axis=-1, keepdims=True) + EPS)
    vhat = vc * rstd
    vn = vhat * lg_ref[...] + lb_ref[...]
    row = lax.broadcasted_iota(jnp.int32, (T, T), 0)
    col = lax.broadcasted_iota(jnp.int32, (T, T), 1)
    tril = row >= col
    s = []
    for g in range(G):
        w = jnp.where(tril, ws_ref[g], 0.0)
        s.append(_dot(w, vn[:, g * Dg:(g + 1) * Dg]) + bsb_ref[g])
    return ug, dug, dvg, rstd, vhat, vn, tril, s


def _gmlp_fwd(z, ln_g, ln_b, ws, bsb, GW):
    S = z.shape[0]
    G, T, _ = ws.shape
    Dg = GW // G

    def body(u_ref, v_ref, lg_ref, lb_ref, ws_ref, bsb_ref, ya_ref):
        ug, _, _, _, _, _, _, s = _gmlp_common(u_ref, v_ref, lg_ref, lb_ref, ws_ref, bsb_ref, G, T, Dg)
        for g in range(G):
            sl = slice(g * Dg, (g + 1) * Dg)
            ya_ref[:, sl] = (ug[:, sl] * s[g]).astype(BF16)

    vec = pl.BlockSpec((1, GW), lambda c: (0, 0))
    return pl.pallas_call(
        body, name="gmlp_fwd", grid=(S // T,),
        in_specs=[pl.BlockSpec((T, GW), lambda c: (c, 0)), pl.BlockSpec((T, GW), lambda c: (c, 1)), vec, vec,
                  pl.BlockSpec((G, T, T), lambda c: (0, 0, 0)), pl.BlockSpec((G, T, Dg), lambda c: (0, 0, 0))],
        out_specs=pl.BlockSpec((T, GW), lambda c: (c, 0)), out_shape=jax.ShapeDtypeStruct((S, GW), BF16),
        compiler_params=_cparams(("parallel",)))(z, z, ln_g, ln_b, ws, bsb)


def _gmlp_bwd(z, dya, ln_g, ln_b, ws, bsb, GW):
    S = z.shape[0]
    G, T, _ = ws.shape
    Dg = GW // G
    nc = S // T

    def body(u_ref, v_ref, dya_ref, lg_ref, lb_ref, ws_ref, bsb_ref, dz_ref, dln_ref, dws_ref, dbs_ref, dbs_acc, dvh):
        c = pl.program_id(0)

        @pl.when(c == 0)
        def _():
            dln_ref[...] = jnp.zeros_like(dln_ref)
            dws_ref[...] = jnp.zeros_like(dws_ref)
            dbs_acc[...] = jnp.zeros_like(dbs_acc)

        ug, dug, dvg, rstd, vhat, vn, tril, s = _gmlp_common(u_ref, v_ref, lg_ref, lb_ref, ws_ref, bsb_ref, G, T, Dg)
        dya_v = dya_ref[...]
        for g in range(G):
            sl = slice(g * Dg, (g + 1) * Dg)
            dy_g = dya_v[:, sl]
            dz_ref[:, sl] = (dy_g * s[g] * dug[:, sl]).astype(BF16)
            ds = dy_g * ug[:, sl]
            dbs_acc[g] += ds
            w = jnp.where(tril, ws_ref[g], 0.0)
            dvn_g = _dot(w, ds, _TN)
            dws_ref[g] += jnp.where(tril, _dot(ds, vn[:, sl], _NT), 0.0)
            dln_ref[0:1, sl] += jnp.sum(dvn_g * vhat[:, sl], axis=0, keepdims=True)
            dln_ref[1:2, sl] += jnp.sum(dvn_g, axis=0, keepdims=True)
            dvh[:, sl] = dvn_g * lg_ref[:, sl]
        dvhat = dvh[...]
        m1 = jnp.mean(dvhat, axis=-1, keepdims=True)
        m2 = jnp.mean(dvhat * vhat, axis=-1, keepdims=True)
        dz_ref[:, GW:2 * GW] = (rstd * (dvhat - m1 - vhat * m2) * dvg).astype(BF16)

        @pl.when(c == nc - 1)
        def _():
            for g in range(G):
                dbs_ref[g] = jnp.sum(dbs_acc[g], axis=-1, keepdims=True)

    vec = pl.BlockSpec((1, GW), lambda c: (0, 0))
    return pl.pallas_call(
        body, name="gmlp_bwd", grid=(nc,),
        in_specs=[pl.BlockSpec((T, GW), lambda c: (c, 0)), pl.BlockSpec((T, GW), lambda c: (c, 1)),
                  pl.BlockSpec((T, GW), lambda c: (c, 0)), vec, vec,
                  pl.BlockSpec((G, T, T), lambda c: (0, 0, 0)), pl.BlockSpec((G, T, Dg), lambda c: (0, 0, 0))],
        out_specs=[pl.BlockSpec((T, 2 * GW), lambda c: (c, 0)), pl.BlockSpec((8, GW), lambda c: (0, 0)),
                   pl.BlockSpec((G, T, T), lambda c: (0, 0, 0)), pl.BlockSpec((G, T, 1), lambda c: (0, 0, 0))],
        out_shape=[jax.ShapeDtypeStruct((S, 2 * GW), BF16), jax.ShapeDtypeStruct((8, GW), F32),
                   jax.ShapeDtypeStruct((G, T, T), F32), jax.ShapeDtypeStruct((G, T, 1), F32)],
        scratch_shapes=[pltpu.VMEM((G, T, Dg), F32), pltpu.VMEM((T, GW), F32)],
        compiler_params=_cparams(("arbitrary",)))(z, z, dya, ln_g, ln_b, ws, bsb)


def _hg_common(q_ref, f_ref, hlb_ref):
    C = HG_CHUNK
    a = hlb_ref[...]
    lb = _sigmoid(a[0:1, :] - a[1:2, :])
    sig = _sigmoid(f_ref[...])
    f = lb + (1.0 - lb) * sig
    lf = jnp.log(f)
    kk = 1.0 - f
    q = q_ref[...]
    sq = _sigmoid(q)
    qa = q * sq
    row = lax.broadcasted_iota(jnp.int32, (C, C), 0)
    col = lax.broadcasted_iota(jnp.int32, (C, C), 1)
    tril = row >= col
    b = _ones_dot(tril.astype(BF16), lf)
    bm = b[HG_MID:HG_MID + 1, :]
    bl = b[C - 1:C, :]
    e_b = jnp.exp(b)
    e_qm = jnp.exp(jnp.minimum(b - bm, EXP_CLAMP))
    e_km = jnp.exp(jnp.minimum(bm - b, EXP_CLAMP))
    e_kl = jnp.exp(bl - b)
    return dict(lb=lb, sig=sig, f=f, kk=kk, q=q, sq=sq, qa=qa, tril=tril, e_b=e_b, e_qm=e_qm, e_km=e_km, e_kl=e_kl,
                e_l=jnp.exp(bl), qh=qa * e_b, qt=qa * e_qm, kt=kk * e_km, kh=kk * e_kl)


def _hg_fwd(z, hg_lb, ng, HW):
    S = z.shape[0]
    C, H, dk = HG_CHUNK, HW // HG_DK, HG_DK
    nc = S // C

    def body(q_ref, f_ref, i_ref, og_ref, hlb_ref, ng_ref, yb_ref, o_ref, st_ref, state):
        @pl.when(pl.program_id(0) == 0)
        def _():
            state[...] = jnp.zeros_like(state)

        t = _hg_common(q_ref, f_ref, hlb_ref)
        iv = i_ref[...]
        for h in range(H):
            sl = slice(h * dk, (h + 1) * dk)
            st = state[h]
            st_ref[h] = st
            a = jnp.where(t["tril"], _dot(t["qt"][:, sl], t["kt"][:, sl], _NT), 0.0)
            o_h = _dot(a, iv[:, sl]) + _dot(t["qh"][:, sl], st, _NT)
            state[h] = st * t["e_l"][:, sl] + _dot(iv[:, sl], t["kh"][:, sl], _TN)
            o_ref[:, sl] = o_h
            rr = lax.rsqrt(jnp.mean(o_h * o_h, axis=-1, keepdims=True) + EPS)
            og = og_ref[:, sl]
            yb_ref[:, sl] = (o_h * rr * ng_ref[:, sl] * (og * _sigmoid(og))).astype(BF16)

    def col(k):
        return pl.BlockSpec((C, HW), lambda c: (c, k))

    base = 2
    return pl.pallas_call(
        body, name="hgrn_fwd", grid=(nc,),
        in_specs=[col(base), col(base + 1), col(base + 2), col(base + 3),
                  pl.BlockSpec((2, HW), lambda c: (0, 0)), pl.BlockSpec((1, HW), lambda c: (0, 0))],
        out_specs=[pl.BlockSpec((C, HW), lambda c: (c, 0)), pl.BlockSpec((C, HW), lambda c: (c, 0)),
                   pl.BlockSpec((None, H, dk, dk), lambda c: (c, 0, 0, 0))],
        out_shape=[jax.ShapeDtypeStruct((S, HW), BF16), jax.ShapeDtypeStruct((S, HW), F32),
                   jax.ShapeDtypeStruct((nc, H, dk, dk), F32)],
        scratch_shapes=[pltpu.VMEM((H, dk, dk), F32)],
        compiler_params=_cparams(("arbitrary",)))(z, z, z, z, hg_lb, ng)


def _hg_bwd(z, o, states, dyb, hg_lb, ng, HW):
    S = z.shape[0]
    C, H, dk = HG_CHUNK, HW // HG_DK, HG_DK
    nc = S // C

    def body(q_ref, f_ref, i_ref, og_ref, o_ref, st_ref, stn_ref, dyb_ref, hlb_ref, ng_ref, dz_ref, dng_ref, dhlb_ref,
             dstate, cross, dqa_buf, dkk_buf, db_buf, dlb_acc):
        c = pl.program_id(0)

        @pl.when(c == 0)
        def _():
            dstate[...] = jnp.zeros_like(dstate)
            dlb_acc[...] = jnp.zeros_like(dlb_acc)
            dng_ref[...] = jnp.zeros_like(dng_ref)

        def r16(v):
            return v.astype(BF16).astype(F32)

        t = _hg_common(q_ref, f_ref, hlb_ref)
        iv = i_ref[...]
        for h in range(H):
            sl = slice(h * dk, (h + 1) * dk)
            o_h, og, dyb_h, ng_h = o_ref[:, sl], og_ref[:, sl], dyb_ref[:, sl], ng_ref[:, sl]
            sg = _sigmoid(og)
            silu_og = og * sg
            rr = lax.rsqrt(jnp.mean(o_h * o_h, axis=-1, keepdims=True) + EPS)
            on = o_h * rr
            dng_ref[0:1, sl] += jnp.sum(dyb_h * on * silu_og, axis=0, keepdims=True)
            dz_ref[:, 3 * HW + h * dk:3 * HW + (h + 1) * dk] = (dyb_h * on * ng_h * (sg * (1.0 + og * (1.0 - sg)))).astype(BF16)
            don = dyb_h * ng_h * silu_og
            do_h = rr * (don - on * jnp.mean(don * on, axis=-1, keepdims=True))

            qt, kt, qh, kh, iv_h = t["qt"][:, sl], t["kt"][:, sl], t["qh"][:, sl], t["kh"][:, sl], iv[:, sl]
            a = jnp.where(t["tril"], _dot(qt, kt, _NT), 0.0)
            da = jnp.where(t["tril"], _dot(do_h, iv_h, _NT), 0.0)
            st, dst = st_ref[h], dstate[h]
            cross[:, sl] = jnp.sum(dst * stn_ref[h], axis=0, keepdims=True)
            dqh = _dot(do_h, st)
            dstate[h] = _dot(do_h, qh, _TN) + dst * t["e_l"][:, sl]
            div = _dot(a, do_h, _TN) + _dot(kh, dst, _NT)
            dkh = _dot(iv_h, dst)
            dqt = _dot(da, kt)
            dkt = _dot(da, qt, _TN)
            dz_ref[:, 2 * HW + h * dk:2 * HW + (h + 1) * dk] = div.astype(BF16)
            dqa_buf[:, sl] = dqh * t["e_b"][:, sl] + dqt * t["e_qm"][:, sl]
            dkk_buf[:, sl] = dkt * t["e_km"][:, sl] + dkh * t["e_kl"][:, sl]
            db_buf[:, sl] = r16(qt) * dqt - r16(kt) * dkt + r16(qh) * dqh - r16(kh) * dkh

        dqa, dkk = dqa_buf[...], dkk_buf[...]
        triu = jnp.logical_not(t["tril"]) | (lax.broadcasted_iota(jnp.int32, (C, C), 0) == lax.broadcasted_iota(jnp.int32, (C, C), 1))
        dlf = _ones_dot(triu.astype(BF16), db_buf[...]) + cross[...]
        df = dlf / t["f"] - dkk
        sig, lb = t["sig"], t["lb"]
        dz_ref[:, HW:2 * HW] = (df * (1.0 - lb) * sig * (1.0 - sig)).astype(BF16)
        dlb_acc[...] += jnp.sum(df * (1.0 - sig), axis=0, keepdims=True)
        q, sq = t["q"], t["sq"]
        dz_ref[:, 0:HW] = (dqa * (sq * (1.0 + q * (1.0 - sq)))).astype(BF16)

        @pl.when(c == nc - 1)
        def _():
            da0 = dlb_acc[...] * lb * (1.0 - lb)
            dhlb_ref[0:1, :] = da0
            dhlb_ref[1:2, :] = -da0

    def col(k):
        return pl.BlockSpec((C, HW), lambda c: (nc - 1 - c, k))

    base = 2
    return pl.pallas_call(
        body, name="hgrn_bwd", grid=(nc,),
        in_specs=[col(base), col(base + 1), col(base + 2), col(base + 3), col(0),
                  pl.BlockSpec((None, H, dk, dk), lambda c: (nc - 1 - c, 0, 0, 0)),
                  pl.BlockSpec((None, H, dk, dk), lambda c: (jnp.minimum(nc - c, nc - 1), 0, 0, 0)), col(0),
                  pl.BlockSpec((2, HW), lambda c: (0, 0)), pl.BlockSpec((1, HW), lambda c: (0, 0))],
        out_specs=[pl.BlockSpec((C, 4 * HW), lambda c: (nc - 1 - c, 0)), pl.BlockSpec((8, HW), lambda c: (0, 0)),
                   pl.BlockSpec((2, HW), lambda c: (0, 0))],
        out_shape=[jax.ShapeDtypeStruct((S, 4 * HW), BF16), jax.ShapeDtypeStruct((8, HW), F32), jax.ShapeDtypeStruct((2, HW), F32)],
        scratch_shapes=[pltpu.VMEM((H, dk, dk), F32), pltpu.VMEM((1, HW), F32), pltpu.VMEM((C, HW), F32), pltpu.VMEM((C, HW), F32),
                        pltpu.VMEM((C, HW), F32), pltpu.VMEM((1, HW), F32)],
        compiler_params=_cparams(("arbitrary",)))(z, z, z, z, o, states, states, dyb, hg_lb, ng)


def _position():
    x, y, c = lax.axis_index("x"), lax.axis_index("y"), lax.axis_index("c")
    return x, y, c, 4 * x + 2 * y + c


def _flip(x, y, c, k):
    return (1 - x if k & 4 else x, 1 - y if k & 2 else y, 1 - c if k & 1 else c)


def _allgather_small(name, v):
    R, L = v.shape

    def body(v_ref, out_ref, send_sems, recv_sems):
        x, y, c, me = _position()
        out_ref[me] = v_ref[...]
        copies = []
        for k in range(1, N_DEV):
            cp = pltpu.make_async_remote_copy(src_ref=v_ref, dst_ref=out_ref.at[me], send_sem=send_sems.at[k - 1],
                                              recv_sem=recv_sems.at[k - 1], device_id=_flip(x, y, c, k), device_id_type=MESH)
            cp.start()
            copies.append(cp)
        for cp in copies:
            cp.wait()

    return pl.pallas_call(
        body, name=name, out_shape=jax.ShapeDtypeStruct((N_DEV, R, L), v.dtype),
        in_specs=[pl.BlockSpec(memory_space=pltpu.VMEM)], out_specs=pl.BlockSpec(memory_space=pltpu.VMEM),
        scratch_shapes=[pltpu.SemaphoreType.DMA((N_DEV - 1,)), pltpu.SemaphoreType.DMA((N_DEV - 1,))],
        compiler_params=pltpu.CompilerParams(vmem_limit_bytes=VMEM_LIMIT),
    )(v)


def _allgather_hbm(name, shards):
    n = len(shards)

    def body(*refs):
        ins, outs = refs[:n], refs[n:2 * n]
        send_sems, recv_sems, local_sems = refs[2 * n:]
        x, y, c, me = _position()
        sibling = (x, y, 1 - c)
        chips = [(1 - x, y), (x, 1 - y), (1 - x, 1 - y)]

        def slot(px, py, pc):
            return 4 * px + 2 * py + pc

        def copy(w, k, block, to, src=None):
            dst = outs[w].at[slot(*block)]
            return pltpu.make_async_remote_copy(src_ref=dst if src is None else src, dst_ref=dst, send_sem=send_sems.at[w, k],
                                                recv_sem=recv_sems.at[w, k], device_id=to, device_id_type=MESH)

        mine, first, passed = [], [], []
        for w in range(n):
            cp = pltpu.make_async_copy(ins[w], outs[w].at[me], local_sems.at[w])
            cp.start()
            mine.append(cp)
            for j, chip in enumerate(chips):
                first.append(copy(w, 1 + j, (x, y, c), (*chip, c), src=ins[w]))
            first.append(copy(w, 0, (x, y, c), sibling, src=ins[w]))
        for cp in first:
            cp.start()
        for w in range(n):
            for j, chip in enumerate(chips):
                copy(w, 1 + j, (*chip, c), (x, y, c)).wait_recv()
                cp = copy(w, 4 + j, (*chip, c), sibling)
                cp.start()
                passed.append(cp)
        for w in range(n):
            copy(w, 0, sibling, (x, y, c)).wait_recv()
            for j, chip in enumerate(chips):
                copy(w, 4 + j, (*chip, 1 - c), (x, y, c)).wait_recv()
        for cp in first + passed:
            cp.wait_send()
        for cp in mine:
            cp.wait()

    hbm = pl.BlockSpec(memory_space=pltpu.HBM)
    return pl.pallas_call(
        body, name=name, out_shape=[jax.ShapeDtypeStruct((N_DEV, *s.shape), s.dtype) for s in shards],
        in_specs=[hbm] * n, out_specs=[hbm] * n,
        scratch_shapes=[pltpu.SemaphoreType.DMA((n, 7)), pltpu.SemaphoreType.DMA((n, 7)), pltpu.SemaphoreType.DMA((n,))],
    )(*shards)


_HBM = pl.BlockSpec(memory_space=pltpu.HBM)
_SEM = pl.BlockSpec(memory_space=pltpu.SEMAPHORE)
_EFFECT = pltpu.SideEffectType.DATAFLOW_SIDE_EFFECTING


def _split_start(name, bufs, n_sems, copies_fn, after=None):
    nb = len(bufs)
    extra = [] if after is None else [after]
    k = nb + len(extra)

    def body(*refs):
        for cp in copies_fn(refs[:nb], refs[k], refs[k + 1]):
            cp.start()
        refs[-1][...] = jnp.zeros_like(refs[-1])

    sems = pltpu.SemaphoreType.DMA((n_sems,))
    res = pl.pallas_call(
        body, name=name,
        out_shape=(sems, sems, *[pltpu.HBM(a.shape, a.dtype) for a in bufs], jax.ShapeDtypeStruct((8, LANES), F32)),
        in_specs=[_HBM] * nb + [pl.BlockSpec(memory_space=pl.ANY)] * len(extra),
        out_specs=(_SEM, _SEM, *[_HBM] * nb, pl.BlockSpec(memory_space=pltpu.VMEM)),
        input_output_aliases={i: 2 + i for i in range(nb)},
        compiler_params=pltpu.CompilerParams(has_side_effects=_EFFECT),
    )(*[pltpu.with_memory_space_constraint(a, pltpu.HBM) for a in bufs], *extra)
    return res[0], res[1], list(res[2:2 + nb]), res[-1]


def _split_wait(name, bufs, send_sems, recv_sems, after, copies_fn):
    nb = len(bufs)

    def body(*refs):
        for cp in copies_fn(refs[:nb], refs[nb], refs[nb + 1]):
            cp.wait_send()
            cp.wait_recv()

    res = pl.pallas_call(
        body, name=name, out_shape=tuple(pltpu.HBM(a.shape, a.dtype) for a in bufs),
        in_specs=[_HBM] * nb + [_SEM, _SEM, pl.BlockSpec(memory_space=pl.ANY)], out_specs=tuple([_HBM] * nb),
        input_output_aliases={i: i for i in range(nb)},
        compiler_params=pltpu.CompilerParams(has_side_effects=_EFFECT),
    )(*bufs, send_sems, recv_sems, after)
    return list(res)


def _split_relay(name, bufs, send_sems, recv_sems, after, wait_fn, n_sems, start_fn):
    nb = len(bufs)

    def body(*refs):
        for cp in wait_fn(refs[:nb], refs[nb], refs[nb + 1]):
            cp.wait_send()
            cp.wait_recv()
        for cp in start_fn(refs[:nb], refs[nb + 3], refs[nb + 4]):
            cp.start()

    sems = pltpu.SemaphoreType.DMA((n_sems,))
    res = pl.pallas_call(
        body, name=name, out_shape=(sems, sems, *[pltpu.HBM(a.shape, a.dtype) for a in bufs]),
        in_specs=[_HBM] * nb + [_SEM, _SEM, pl.BlockSpec(memory_space=pl.ANY)], out_specs=(_SEM, _SEM, *[_HBM] * nb),
        input_output_aliases={i: 2 + i for i in range(nb)},
        compiler_params=pltpu.CompilerParams(has_side_effects=_EFFECT),
    )(*bufs, send_sems, recv_sems, after)
    return res[0], res[1], list(res[2:])


N_CHIP = 4


def _chip_flip(x, y, k):
    return (1 - x if k & 2 else x), (1 - y if k & 1 else y)


def _gather_first_copies(n):
    def copies(bufs, send_sems, recv_sems):
        x, y, c, me = _position()
        out = []
        for w in range(n):
            for k in range(N_CHIP):
                to = (x, y, 1 - c) if k == 0 else (*_chip_flip(x, y, k), c)
                out.append(pltpu.make_async_remote_copy(
                    src_ref=bufs[w], dst_ref=bufs[n + w].at[me], send_sem=send_sems.at[w * N_CHIP + k],
                    recv_sem=recv_sems.at[w * N_CHIP + k], device_id=to, device_id_type=MESH))
        return out
    return copies


def _gather_relay_copies(n):
    def copies(bufs, send_sems, recv_sems):
        x, y, c, _ = _position()
        out = []
        for w in range(n):
            for k in range(1, N_CHIP):
                px, py = _chip_flip(x, y, k)
                blk = bufs[n + w].at[4 * px + 2 * py + c]
                out.append(pltpu.make_async_remote_copy(
                    src_ref=blk, dst_ref=blk, send_sem=send_sems.at[w * (N_CHIP - 1) + k - 1],
                    recv_sem=recv_sems.at[w * (N_CHIP - 1) + k - 1], device_id=(x, y, 1 - c), device_id_type=MESH))
        return out
    return copies


def _to_sibling_copies(n):
    def copies(bufs, send_sems, recv_sems):
        x, y, c, _ = _position()
        out = []
        for w in range(n):
            for q in range(N_CHIP):
                out.append(pltpu.make_async_remote_copy(
                    src_ref=bufs[w].at[2 * q + 1 - c], dst_ref=bufs[n + w].at[q], send_sem=send_sems.at[w * N_CHIP + q],
                    recv_sem=recv_sems.at[w * N_CHIP + q], device_id=(x, y, 1 - c), device_id_type=MESH))
        return out
    return copies


def _to_owner_copies(n):
    def copies(bufs, send_sems, recv_sems):
        x, y, c, _ = _position()
        out = []
        for w in range(n):
            for k in range(1, N_CHIP):
                px, py = (1 - x if k & 2 else x), (1 - y if k & 1 else y)
                out.append(pltpu.make_async_remote_copy(
                    src_ref=bufs[w].at[2 * px + py], dst_ref=bufs[n + w].at[k - 1], send_sem=send_sems.at[w * (N_CHIP - 1) + k - 1],
                    recv_sem=recv_sems.at[w * (N_CHIP - 1) + k - 1], device_id=(px, py, c), device_id_type=MESH))
        return out
    return copies


def _chip_sum(name, stack, landed, c_idx):
    _, R, C = stack.shape
    tr = _tile(R, max(16, 262144 // C), 16)

    def body(c_ref, a_ref, b_ref, o_ref):
        o_ref[...] = (a_ref[...].astype(F32) + b_ref[...].astype(F32)).astype(o_ref.dtype)

    return pl.pallas_call(
        body, name=name,
        grid_spec=pltpu.PrefetchScalarGridSpec(
            num_scalar_prefetch=1, grid=(N_CHIP, R // tr),
            in_specs=[pl.BlockSpec((None, tr, C), lambda q, i, c_ref: (2 * q + c_ref[0], i, 0)),
                      pl.BlockSpec((None, tr, C), lambda q, i, c_ref: (q, i, 0))],
            out_specs=pl.BlockSpec((None, tr, C), lambda q, i, c_ref: (q, i, 0))),
        out_shape=jax.ShapeDtypeStruct((N_CHIP, R, C), stack.dtype),
        compiler_params=_cparams(("parallel", "parallel")))(c_idx, stack, landed)


def _ada_mod(c16, w):
    _, D = c16.shape
    n = w.shape[1]
    tk = _tile(D, 512)
    nk = D // tk

    def body(c_ref, w_ref, o_ref, ca_ref):
        @pl.when(pl.program_id(0) == 0)
        def _():
            o_ref[...] = jnp.zeros_like(o_ref)

        cv = c_ref[...]
        ca = cv * _sigmoid(cv)
        ca_ref[...] = ca
        o_ref[...] += _dot(ca, w_ref[...])

    return pl.pallas_call(
        body, name="ada_mod", grid=(nk,),
        in_specs=[pl.BlockSpec((16, tk), lambda k: (0, k)), pl.BlockSpec((tk, n), lambda k: (k, 0))],
        out_specs=[pl.BlockSpec((16, n), lambda k: (0, 0)), pl.BlockSpec((16, tk), lambda k: (0, k))],
        out_shape=[jax.ShapeDtypeStruct((16, n), F32), jax.ShapeDtypeStruct((16, D), F32)],
        compiler_params=_cparams(("arbitrary",)))(c16, w)


def _adam_math(w, g, m, v):
    m2 = ADAM_B1 * m + (1.0 - ADAM_B1) * g
    v2 = ADAM_B2 * v + (1.0 - ADAM_B2) * (g * g)
    m_hat = m2 / (1.0 - ADAM_B1 ** ADAM_STEP)
    v_hat = v2 / (1.0 - ADAM_B2 ** ADAM_STEP)
    delta = -ADAM_LR * (m_hat / (jnp.sqrt(v_hat) + ADAM_EPS) + ADAM_WD * w)
    return delta, m2, v2


def _adamw(name, w, m, v, parts):
    R, C = w.shape
    tr = _tile(R, max(16, 131072 // C), 16)
    n_p = len(parts)

    def body(*refs):
        w_ref, m_ref, v_ref = refs[:3]
        g_ref, d_ref, m2_ref, v2_ref = refs[3 + n_p:]
        g = None
        for p_ref in refs[3:3 + n_p]:
            for s in range(p_ref.shape[0]):
                t = p_ref[s].astype(F32)
                g = t if g is None else g + t
        delta, m2, v2 = _adam_math(w_ref[...], g, m_ref[...], v_ref[...])
        g_ref[...] = g
        d_ref[...] = delta
        m2_ref[...] = m2
        v2_ref[...] = v2

    blk = pl.BlockSpec((tr, C), lambda i: (i, 0))
    out = jax.ShapeDtypeStruct((R, C), F32)
    return pl.pallas_call(body, name=name, grid=(R // tr,),
                          in_specs=[blk, blk, blk] + [pl.BlockSpec((a.shape[0], tr, C), lambda i: (0, i, 0)) for a in parts],
                          out_specs=[blk] * 4, out_shape=[out] * 4, compiler_params=_cparams(("parallel",)))(w, m, v, *parts)


def _small_update(gathered, w, m, v):
    _, R, L = gathered.shape
    rs = w.shape[0]

    def body(p_ref, w_ref, m_ref, v_ref, g_ref, d_ref, m2_ref, v2_ref):
        g = p_ref[0]
        for p in range(1, N_DEV):
            g = g + p_ref[p]
        g_ref[...] = g
        delta, m2, v2 = _adam_math(w_ref[...], g[0:rs, :], m_ref[...], v_ref[...])
        d_ref[...] = delta
        m2_ref[...] = m2
        v2_ref[...] = v2

    vm = pl.BlockSpec(memory_space=pltpu.VMEM)
    sm = jax.ShapeDtypeStruct((rs, L), F32)
    return pl.pallas_call(body, name="small_update", in_specs=[vm] * 4, out_specs=[vm] * 4,
                          out_shape=[jax.ShapeDtypeStruct((R, L), F32), sm, sm, sm],
                          compiler_params=pltpu.CompilerParams(vmem_limit_bytes=VMEM_LIMIT))(gathered, w, m, v)


class _Fetched(dict):
    def __init__(self, fetch):
        super().__init__()
        self.fetch = fetch

    def first(self, key, after):
        self[key] = self.fetch(key, after)
        return self[key]


def _local_step(x, tgt, mod, p, fetch, F, scatter=None):
    S, D = x.shape
    GW, HW = p["ln_g"].shape[1], p["hg_ng"].shape[1]
    G, T, _ = p["ws"].shape
    w = _Fetched(fetch)
    INW = 2 * GW + 4 * HW + 2 * D
    in_loc, br_loc, fi_loc = INW // N_DEV, D // N_DEV, 2 * F // N_DEV
    assert GW == HW and F % fi_loc == 0
    sh1, sc1, gt1, sh2, sc2, gt2 = (mod[:, k * D:(k + 1) * D] for k in range(6))
    bsb = jnp.broadcast_to(p["bs"][:, :, None], (G, T, GW // G))

    tm = _tile(S, 1024, 16)
    tmh = _tile(S, 512, 16)
    tn_in = _tile(in_loc, 640)
    tn_d = _tile(D, 512)
    tn_br = _tile(br_loc, 512)
    tk_s = _tile(S, 1024)
    g_off = 2 * GW + 4 * HW

    h1 = _norm_mod("norm1", x, p["norm1_g"], sc1, sh1)
    z = _mm_nn_stacked("proj_in", h1, w.first("in", h1), tm=tm, tn=tn_in, tk=D)[0]
    ya = _gmlp_fwd(z, p["ln_g"], p["ln_b"], p["ws"], bsb, GW)
    yb, o_hg, states = _hg_fwd(z, p["hg_lb"], p["hg_ng"], HW)
    pa = _mm_nn_stacked("branch_gmlp", ya, w.first("bg", z), tm=tm, tn=tn_br, tk=GW)[0]

    def gates(ga_ref, gb_ref, ba_ref, bb_ref):
        return _sigmoid(ga_ref[...] + ba_ref[...]), _sigmoid(gb_ref[...] + bb_ref[...])

    def gate_specs(tn_):
        o1, o2 = g_off // tn_, (g_off + D) // tn_
        return [pl.BlockSpec((tm, tn_), lambda i, j, k: (i, o1 + j)), pl.BlockSpec((tm, tn_), lambda i, j, k: (i, o2 + j)),
                pl.BlockSpec((1, tn_), lambda i, j, k: (0, j)), pl.BlockSpec((1, tn_), lambda i, j, k: (0, D // tn_ + j))]

    def merge_ep(acc, ex, outs):
        ga, gb = gates(*ex[1:5])
        outs[0][...] = acc
        outs[1][...] = (ga * ex[0][...] + gb * acc).astype(BF16)

    tile_o = pl.BlockSpec((tm, tn_br), lambda i, j, k: (i, j))
    pb, y = _mm_nn_stacked(
        "branch_hg_merge", yb, w.first("bh", z), tm=tm, tn=tn_br, tk=HW, extras=[pa, z, z, p["b_gate"], p["b_gate"]],
        extra_specs=[tile_o, *gate_specs(tn_br)], out_shapes=[jax.ShapeDtypeStruct((S, D), F32), jax.ShapeDtypeStruct((S, D), BF16)],
        out_specs=[tile_o, tile_o], epilogue=merge_ep)

    def resid_ep(acc, ex, outs):
        outs[0][...] = acc
        outs[1][...] = ex[0][...] + ex[1][...] * acc

    def resid_mm(name, a, b, res, gt, tk):
        K = a.shape[1]
        t_o = pl.BlockSpec((tm, tn_d), lambda i, j, k: (i, j))
        return _matmul(
            name, a, b, dims=_NN, grid_mnk=(S // tm, D // tn_d, K // tk), tiles=(tm, tn_d),
            a_spec=pl.BlockSpec((tm, tk), lambda i, j, k: (i, k)), b_spec=pl.BlockSpec((tk, tn_d), lambda i, j, k: (k, j)),
            extras=[res, gt], extra_specs=[t_o, pl.BlockSpec((1, tn_d), lambda i, j, k: (0, j))],
            out_shapes=[jax.ShapeDtypeStruct((S, D), F32)] * 2, out_specs=[t_o, t_o], epilogue=resid_ep)

    o1, xm = resid_mm("proj_out", y, w.first("out", z), x, gt1, D)
    h2 = _norm_mod("norm2", xm, p["norm2_g"], sc2, sh2)
    ab = _mm_nn_stacked("ffn_in", h2, w.first("fi", y), tm=tm, tn=fi_loc, tk=D)[0]
    hf = _swiglu(ab, F, fi_loc)
    o2, x3 = resid_mm("ffn_out", hf, w.first("fo", ab), xm, gt2, _tile(F, 1536))
    dx3, do2, vec_l = _loss_head(x3, tgt, p["final_g"], o2, gt2)

    nf = F // fi_loc

    def dswiglu_ep(acc, ex, outs):
        a, up = ex[0][...], ex[1][...]
        sa = _sigmoid(a)
        outs[0][0] = (acc * up * (sa * (1.0 + a * (1.0 - sa)))).astype(BF16)
        outs[0][1] = (acc * (a * sa)).astype(BF16)

    dab = _matmul(
        "ffn_out_dx", do2, w["fo"], dims=_NT, grid_mnk=(S // tmh, nf, 1), tiles=(tmh, fi_loc),
        a_spec=pl.BlockSpec((tmh, D), lambda i, j, k: (i, 0)), b_spec=pl.BlockSpec((fi_loc, D), lambda i, j, k: (j, 0)),
        extras=[ab, ab], extra_specs=[pl.BlockSpec((tmh, fi_loc), lambda i, j, k: (i, j)), pl.BlockSpec((tmh, fi_loc), lambda i, j, k: (i, j + nf))],
        out_shapes=[jax.ShapeDtypeStruct((2, S, F), BF16)], out_specs=[pl.BlockSpec((2, tmh, fi_loc), lambda i, j, k: (0, i, j))],
        epilogue=dswiglu_ep)[0]
    tm_f = _tile(F, 512)
    g_fo = _mm_tn("ffn_out_dw", hf, do2, pl.BlockSpec((tk_s, tn_d), lambda i, j, k: (k, j)), Mo=F, No=D, S=S, tm=tm_f, tn=tn_d, tk=tk_s)
    dh2 = _mm_nt_stacked("ffn_in_dx", pl.BlockSpec((None, tm, fi_loc), lambda i, j, k: (k // nf, i, k % nf)), dab, w["fi"],
                         M=S, tm=tm, tn=tn_d, tk=fi_loc)
    g_fi = _mm_tn("ffn_in_dw", h2, dab, pl.BlockSpec((None, tk_s, fi_loc), lambda i, j, k: (j // nf, k, j % nf)),
                  Mo=D, No=2 * F, S=S, tm=tn_d, tn=fi_loc, tk=tk_s, stacked_nloc=fi_loc)
    tie = (lambda name, grads: scatter[0](name, grads)[0:1, 0:1]) if scatter is not None else (lambda name, grads: 0.0)
    push = (lambda name, after: scatter[1](name, after)[0:1, 0:1]) if scatter is not None else (lambda name, after: 0.0)
    dxm, vec2, do1 = _norm_mod_bwd("norm2_bwd", dh2, xm, p["norm2_g"], sc2, dx3, o1, gt1 + tie("scatter_ffn", dict(fo=g_fo, fi=g_fi)))

    def dmerge_ep(acc, ex, outs):
        ga, gb = gates(*ex[2:6])
        outs[0][...] = (acc * ga).astype(BF16)
        outs[1][...] = (acc * gb).astype(BF16)
        outs[2][0] = (acc * ex[0][...] * ga * (1.0 - ga)).astype(BF16)
        outs[2][1] = (acc * ex[1][...] * gb * (1.0 - gb)).astype(BF16)

    t_o = pl.BlockSpec((tm, tn_d), lambda i, j, k: (i, j))
    dpa, dpb, dg2 = _matmul(
        "proj_out_dx", do1, w["out"], dims=_NT, grid_mnk=(S // tm, D // tn_d, 1), tiles=(tm, tn_d),
        a_spec=pl.BlockSpec((tm, D), lambda i, j, k: (i, 0)), b_spec=pl.BlockSpec((tn_d, D), lambda i, j, k: (j, 0)),
        extras=[pa, pb, z, z, p["b_gate"], p["b_gate"]], extra_specs=[t_o, t_o, *gate_specs(tn_d)],
        out_shapes=[jax.ShapeDtypeStruct((S, D), BF16), jax.ShapeDtypeStruct((S, D), BF16), jax.ShapeDtypeStruct((2, S, D), BF16)],
        out_specs=[t_o, t_o, pl.BlockSpec((2, tm, tn_d), lambda i, j, k: (0, i, j))], epilogue=dmerge_ep)
    db_gate = _colsum2(dg2)
    g_out = _mm_tn("proj_out_dw", y, do1, pl.BlockSpec((tk_s, tn_d), lambda i, j, k: (k, j)), Mo=D, No=D, S=S, tm=tn_d, tn=tn_d, tk=tk_s)
    a_br = pl.BlockSpec((tm, br_loc), lambda i, j, k: (i, k))
    tn_g = _tile(GW, 512)
    dya = _mm_nt_stacked("branch_gmlp_dx", a_br, dpa, w["bg"], M=S, tm=tm, tn=tn_g, tk=br_loc)
    dyb = _mm_nt_stacked("branch_hg_dx", a_br, dpb, w["bh"], M=S, tm=tm, tn=tn_g, tk=br_loc)
    b_br = pl.BlockSpec((tk_s, br_loc), lambda i, j, k: (k, j))
    g_bg = _mm_tn("branch_gmlp_dw", ya, dpa, b_br, Mo=GW, No=D, S=S, tm=tn_g, tn=br_loc, tk=tk_s, stacked_nloc=br_loc)
    g_bh = _mm_tn("branch_hg_dw", yb, dpb, b_br, Mo=HW, No=D, S=S, tm=tn_g, tn=br_loc, tk=tk_s, stacked_nloc=br_loc)
    ln_b_tied = p["ln_b"] + push("scatter_ffn", dpa) + tie("scatter_mixer", dict(out=g_out, bg=g_bg, bh=g_bh))
    dz_gmlp, dln, dws, dbs = _gmlp_bwd(z, dya, p["ln_g"], ln_b_tied, p["ws"], bsb, GW)
    dz_hg, dng, dhlb = _hg_bwd(z, o_hg, states, dyb, p["hg_lb"], p["hg_ng"], HW)
    dz = jnp.concatenate([dz_gmlp, dz_hg, dg2[0], dg2[1]], axis=1)
    g_in = _mm_tn("proj_in_dw", h1, dz, pl.BlockSpec((tk_s, tn_in), lambda i, j, k: (k, j)), Mo=D, No=INW, S=S,
                  tm=tn_d, tn=tn_in, tk=tk_s, stacked_nloc=in_loc)
    sc1_tied = sc1 + push("scatter_mixer", dz_hg) + tie("scatter_proj_in", dict(w_in=g_in))
    dh1 = _mm_nt_stacked("proj_in_dx", pl.BlockSpec((tm, tn_in), lambda i, j, k: (i, k)), dz, w["in"], M=S, tm=tm, tn=tn_d, tk=tn_in)
    dx, vec1 = _norm_mod_bwd("norm1_bwd", dh1, x, p["norm1_g"], sc1_tied, dxm)

    dmod = jnp.concatenate([vec1[0:1], vec1[1:2], vec2[3:4], vec2[0:1], vec2[1:2], vec_l[2:3]], axis=1)
    small = dict(norm1_g=vec1[2:3], b_gate=db_gate.reshape(1, 2 * D), ln_g=dln[0:1], ln_b=dln[1:2], ws=dws, bs=dbs.reshape(G, T),
                 hg_lb=dhlb, hg_ng=dng[0:1], norm2_g=vec2[2:3], final_g=vec_l[1:2], loss=vec_l[0:1, 0:LANES])
    big = dict(w_in=g_in, bg=g_bg, bh=g_bh, out=g_out, fi=g_fi, fo=g_fo)
    return dx, big, small, dmod


_SMALL = ("b_ada", "norm1_g", "b_gate", "ln_g", "ln_b", "ws", "bs", "hg_lb", "hg_ng", "norm2_g", "final_g")


def _pack(parts, rows_mult=8):
    flat = [a.reshape(-1) for a in parts]
    offs, n = [], 0
    for a in flat:
        offs.append(n)
        n += a.shape[0]
    pad = (-n) % (LANES * rows_mult)
    if pad:
        flat.append(jnp.zeros((pad,), F32))
    return jnp.concatenate(flat).reshape(-1, LANES), offs


def kernel(x, c, w_ada, b_ada, norm1_g, w_in, b_gate, gmlp_ln_g, gmlp_ln_b, gmlp_ws, gmlp_bs, hg_lb, hg_norm_g, w_branch_gmlp, w_branch_hg, w_out, norm2_g, w_ffn_in, w_ffn_out, final_norm_g, loss_target, m_w_ada, m_b_ada, m_norm1_g, m_w_in, m_b_gate, m_gmlp_ln_g, m_gmlp_ln_b, m_gmlp_ws, m_gmlp_bs, m_hg_lb, m_hg_norm_g, m_w_branch_gmlp, m_w_branch_hg, m_w_out, m_norm2_g, m_w_ffn_in, m_w_ffn_out, m_final_norm_g, v_w_ada, v_b_ada, v_norm1_g, v_w_in, v_b_gate, v_gmlp_ln_g, v_gmlp_ln_b, v_gmlp_ws, v_gmlp_bs, v_hg_lb, v_hg_norm_g, v_w_branch_gmlp, v_w_branch_hg, v_w_out, v_norm2_g, v_w_ffn_in, v_w_ffn_out, v_final_norm_g):
    S, D = x.shape[1], x.shape[2]
    ada_loc = w_ada.shape[2]
    me = 4 * lax.axis_index("x") + 2 * lax.axis_index("y") + lax.axis_index("c")

    c_all = _allgather_small("gather_c", c.reshape(D // LANES, LANES)).reshape(N_DEV, D)
    mod_cols, c_act = _ada_mod(jnp.pad(c_all, ((0, 16 - N_DEV), (0, 0))), w_ada[0])
    mod_all = _allgather_small("gather_mod", mod_cols[:N_DEV].reshape(-1, LANES)).reshape(N_DEV, N_DEV, ada_loc)
    mod = lax.dynamic_index_in_dim(mod_all, me, axis=1, keepdims=False).reshape(1, N_DEV * ada_loc) + b_ada

    def empty_hbm(shape, dtype):
        return pltpu.with_memory_space_constraint(lax.empty(shape, dtype), pltpu.HBM)

    groups = dict(gather_in=dict(keys=["in"], src=[w_in]), gather_mixer=dict(keys=["bg", "bh", "out"], src=[w_branch_gmlp, w_branch_hg, w_out]),
                  gather_ffn_in=dict(keys=["fi"], src=[w_ffn_in]), gather_ffn_out=dict(keys=["fo"], src=[w_ffn_out]))
    group_of = {}
    prev = mod_all
    for gname, g in groups.items():
        n = len(g["keys"])
        shards = [a[0].astype(BF16) for a in g["src"]]
        outs = [lax.dynamic_update_slice(lax.empty((N_DEV, *s.shape), BF16), s[None], (me, 0, 0)) for s in shards]
        *g["hop1"], prev = _split_start(gname + "_hop1", shards + outs, n * N_CHIP, _gather_first_copies(n), after=prev)
        for k in g["keys"]:
            group_of[k] = gname

    def fetch(key, after):
        g = groups[group_of[key]]
        n = len(g["keys"])
        if "done" not in g:
            send_sems, recv_sems, bufs = g["hop1"]
            send_sems, recv_sems, bufs = _split_relay(group_of[key] + "_relay", bufs, send_sems, recv_sems, after,
                                                      _gather_first_copies(n), n * (N_CHIP - 1), _gather_relay_copies(n))
            bufs = _split_wait(group_of[key] + "_hop2_wait", bufs, send_sems, recv_sems, after, _gather_relay_copies(n))
            g["done"] = dict(zip(g["keys"], bufs[n:]))
        arr = g["done"][key]
        return arr.reshape(-1, D) if key in ("out", "fo") else arr

    p = dict(norm1_g=norm1_g, b_gate=b_gate, ln_g=gmlp_ln_g, ln_b=gmlp_ln_b, ws=gmlp_ws[0], bs=gmlp_bs[0], hg_lb=hg_lb,
             hg_ng=hg_norm_g, norm2_g=norm2_g, final_g=final_norm_g.reshape(1, D))

    in_flight = {}
    c_idx = lax.axis_index("c").astype(jnp.int32).reshape(1)
    my_chip = 2 * lax.axis_index("x") + lax.axis_index("y")

    def scatter_start(name, grads):
        keys = list(grads)
        n = len(keys)
        stacks = [grads[k].reshape(N_DEV, -1, grads[k].shape[-1]) for k in keys]
        lands = [empty_hbm((N_CHIP, *g.shape[1:]), g.dtype) for g in stacks]
        send_sems, recv_sems, bufs, token = _split_start(name + "_d2d", stacks + lands, n * N_CHIP, _to_sibling_copies(n))
        in_flight[name] = dict(keys=keys, stage1=(send_sems, recv_sems, bufs))
        return token

    def scatter_push(name, after):
        f = in_flight[name]
        n = len(f["keys"])
        send_sems, recv_sems, bufs = f["stage1"]
        bufs = _split_wait(name + "_d2d_wait", bufs, send_sems, recv_sems, after, _to_sibling_copies(n))
        sums = [_chip_sum(f"{name}_sum_{k}", bufs[i], bufs[n + i], c_idx) for i, k in enumerate(f["keys"])]
        lands = [empty_hbm((N_CHIP - 1, *s.shape[1:]), s.dtype) for s in sums]
        send_sems, recv_sems, bufs, token = _split_start(name + "_ici", sums + lands, n * (N_CHIP - 1), _to_owner_copies(n))
        f["stage2"] = (send_sems, recv_sems, bufs)
        return token

    grad_x, _, small, dmod = _local_step(x[0], loss_target[0], mod, p, fetch, w_ffn_out.shape[1] * N_DEV, (scatter_start, scatter_push))
    scatter_push("scatter_proj_in", grad_x)

    small["b_ada"] = dmod
    packed, offs = _pack([small[k] for k in _SMALL] + [small["loss"]])
    gathered = _allgather_small("gather_small", packed)
    wp = dict(p, b_ada=b_ada)
    ms = dict(b_ada=m_b_ada, norm1_g=m_norm1_g, b_gate=m_b_gate, ln_g=m_gmlp_ln_g, ln_b=m_gmlp_ln_b, ws=m_gmlp_ws, bs=m_gmlp_bs,
              hg_lb=m_hg_lb, hg_ng=m_hg_norm_g, norm2_g=m_norm2_g, final_g=m_final_norm_g)
    vs = dict(b_ada=v_b_ada, norm1_g=v_norm1_g, b_gate=v_b_gate, ln_g=v_gmlp_ln_g, ln_b=v_gmlp_ln_b, ws=v_gmlp_ws, bs=v_gmlp_bs,
              hg_lb=v_hg_lb, hg_ng=v_hg_norm_g, norm2_g=v_norm2_g, final_g=v_final_norm_g)
    w_sm, _ = _pack([wp[k] for k in _SMALL])
    m_sm, _ = _pack([ms[k] for k in _SMALL])
    v_sm, _ = _pack([vs[k] for k in _SMALL])
    sm_out = _small_update(gathered, w_sm, m_sm, v_sm)
    shapes = dict(b_ada=b_ada.shape, norm1_g=norm1_g.shape, b_gate=b_gate.shape, ln_g=gmlp_ln_g.shape, ln_b=gmlp_ln_b.shape,
                  ws=gmlp_ws.shape, bs=gmlp_bs.shape, hg_lb=hg_lb.shape, hg_ng=hg_norm_g.shape, norm2_g=norm2_g.shape,
                  final_g=final_norm_g.shape)

    def unpack(arr, k):
        i = _SMALL.index(k)
        n = math.prod(shapes[k])
        return arr.reshape(-1)[offs[i]:offs[i] + n].reshape(shapes[k])

    loss = sm_out[0].reshape(-1)[offs[len(_SMALL)]]

    dmod_all = gathered.reshape(N_DEV, -1)[:, offs[0]:offs[0] + N_DEV * ada_loc]
    dmod_loc = lax.dynamic_slice_in_dim(dmod_all, me * ada_loc, ada_loc, axis=1)
    ca_t = jnp.pad(c_act[:N_DEV].T, ((0, 0), (0, LANES - N_DEV))).astype(BF16)
    dm_p = jnp.pad(dmod_loc, ((0, LANES - N_DEV), (0, 0))).astype(BF16)
    tm_a = _tile(D, 512)
    g_ada = _matmul(
        "ada_dw", ca_t, dm_p, dims=_NN, grid_mnk=(D // tm_a, 1, 1), tiles=(tm_a, ada_loc),
        a_spec=pl.BlockSpec((tm_a, LANES), lambda i, j, k: (i, 0)), b_spec=pl.BlockSpec((LANES, ada_loc), lambda i, j, k: (0, 0)),
        out_shapes=[jax.ShapeDtypeStruct((1, D, ada_loc), F32)], out_specs=[pl.BlockSpec((None, tm_a, ada_loc), lambda i, j, k: (0, i, 0))],
        epilogue=_store(F32))[0]

    upd = {"w_ada": _adamw("adamw_w_ada", w_ada[0], m_w_ada[0], v_w_ada[0], [g_ada])}
    big_w = dict(w_in=(w_in, m_w_in, v_w_in, "w_in"), bg=(w_branch_gmlp, m_w_branch_gmlp, v_w_branch_gmlp, "w_branch_gmlp"),
                 bh=(w_branch_hg, m_w_branch_hg, v_w_branch_hg, "w_branch_hg"), out=(w_out, m_w_out, v_w_out, "w_out"),
                 fi=(w_ffn_in, m_w_ffn_in, v_w_ffn_in, "w_ffn_in"), fo=(w_ffn_out, m_w_ffn_out, v_w_ffn_out, "w_ffn_out"))
    after = upd["w_ada"][1]
    for name in ("scatter_ffn", "scatter_mixer", "scatter_proj_in"):
        keys = in_flight[name]["keys"]
        n = len(keys)
        send_sems, recv_sems, bufs = in_flight[name]["stage2"]
        bufs = _split_wait(name + "_ici_wait", bufs, send_sems, recv_sems, after, _to_owner_copies(n))
        for i, k in enumerate(keys):
            wt, mt, vt, out_name = big_w[k]
            own_k = lax.dynamic_index_in_dim(bufs[i], my_chip, axis=0, keepdims=True)
            upd[out_name] = _adamw("adamw_" + out_name, wt[0], mt[0], vt[0], [own_k, bufs[n + i]])
            after = upd[out_name][1]

    order = ("w_ada", "b_ada", "norm1_g", "w_in", "b_gate", "ln_g", "ln_b", "ws", "bs", "hg_lb", "hg_ng", "w_branch_gmlp", "w_branch_hg",
             "w_out", "norm2_g", "w_ffn_in", "w_ffn_out", "final_g")
    outs = [loss, grad_x[None]]
    for idx in range(4):
        for k in order:
            outs.append(upd[k][idx][None] if k in upd else unpack(sm_out[idx], k))
    return tuple(outs)
```

```python
import functools
import math

import jax
import jax.numpy as jnp
from jax import lax
from jax.experimental import pallas as pl
from jax.experimental.pallas import tpu as pltpu

F32 = jnp.float32
BF16 = jnp.bfloat16
N_DEV = 8
EPS = 1e-6
LANES = 128
HG_DK = 128
HG_CHUNK = 64
HG_MID = HG_CHUNK // 2 - 1
EXP_CLAMP = 80.0
VMEM_LIMIT = 48 * 1024 * 1024
ADAM_LR, ADAM_B1, ADAM_B2, ADAM_EPS, ADAM_WD, ADAM_STEP = 0.001, 0.9, 0.999, 1e-08, 0.01, 10
MESH = pl.DeviceIdType.MESH

_NN = (((1,), (0,)), ((), ()))
_NT = (((1,), (1,)), ((), ()))
_TN = (((0,), (0,)), ((), ()))


def _dot(a, b, dims=_NN):
    return lax.dot_general(a.astype(BF16), b.astype(BF16), dims, preferred_element_type=F32)


def _tile(n, target, mult=LANES):
    best = None
    for t in range(mult, min(n, target) + 1, mult):
        if n % t == 0:
            best = t
    return n if best is None else best


def _cparams(sem):
    return pltpu.CompilerParams(dimension_semantics=sem, vmem_limit_bytes=VMEM_LIMIT)


def _sigmoid(x):
    return 1.0 / (1.0 + jnp.exp(-x))


def _gelu_parts(x):
    k0 = math.sqrt(2.0 / math.pi)
    x2 = x * x
    t = jnp.tanh(k0 * (x + 0.044715 * x * x2))
    g = 0.5 * x * (1.0 + t)
    dg = 0.5 * (1.0 + t) + 0.5 * x * (1.0 - t * t) * (k0 * (1.0 + 3.0 * 0.044715 * x2))
    return g, dg


def _split3(x):
    h = x.astype(BF16)
    r = x - h.astype(F32)
    m = r.astype(BF16)
    lo = (r - m.astype(F32)).astype(BF16)
    return h, m, lo


def _ones_dot(mat01, x):
    h, m, lo = _split3(x)
    d = functools.partial(lax.dot_general, dimension_numbers=_NN, preferred_element_type=F32)
    return d(mat01, h) + d(mat01, m) + d(mat01, lo)


def _matmul(name, a, b, *, dims, grid_mnk, tiles, a_spec, b_spec, extras=(), extra_specs=(), out_shapes, out_specs, epilogue, after=None):
    gm, gn, nk = grid_mnk
    tm, tn = tiles
    n_ex, n_out = len(extras), len(out_shapes)
    held = [] if after is None else [after]

    def body(*refs):
        a_ref, b_ref = refs[0], refs[1]
        ex = refs[2:2 + n_ex]
        outs = refs[2 + n_ex + len(held):2 + n_ex + len(held) + n_out]
        if nk == 1:
            epilogue(lax.dot_general(a_ref[...], b_ref[...], dims, preferred_element_type=F32), ex, outs)
            return
        acc = refs[-1]
        k = pl.program_id(2)

        @pl.when(k == 0)
        def _():
            acc[...] = jnp.zeros_like(acc)

        acc[...] += lax.dot_general(a_ref[...], b_ref[...], dims, preferred_element_type=F32)

        @pl.when(k == nk - 1)
        def _():
            epilogue(acc[...], ex, outs)

    return pl.pallas_call(
        body, name=name, grid=(gm, gn, nk), in_specs=[a_spec, b_spec, *extra_specs] + [pl.BlockSpec(memory_space=pl.ANY)] * len(held),
        out_specs=list(out_specs), out_shape=list(out_shapes), scratch_shapes=[] if nk == 1 else [pltpu.VMEM((tm, tn), F32)],
        compiler_params=_cparams(("parallel", "parallel", "arbitrary")),
    )(a, b, *extras, *held)


def _store(dtype):
    def ep(acc, ex, outs):
        outs[0][...] = acc.astype(dtype)
    return ep


def _mm_nn_stacked(name, a, wg, *, tm, tn, tk, out_dtype=F32, extras=(), extra_specs=(), out_shapes=None, out_specs=None, epilogue=None):
    M, K = a.shape
    _, _, nloc = wg.shape
    N = nloc * N_DEV
    q = nloc // tn
    if out_shapes is None:
        out_shapes = [jax.ShapeDtypeStruct((M, N), out_dtype)]
        out_specs = [pl.BlockSpec((tm, tn), lambda i, j, k: (i, j))]
        epilogue = _store(out_dtype)
    return _matmul(
        name, a, wg, dims=_NN, grid_mnk=(M // tm, N // tn, K // tk), tiles=(tm, tn),
        a_spec=pl.BlockSpec((tm, tk), lambda i, j, k: (i, k)),
        b_spec=pl.BlockSpec((None, tk, tn), lambda i, j, k: (j // q, k, j % q)),
        extras=extras, extra_specs=extra_specs, out_shapes=out_shapes, out_specs=out_specs, epilogue=epilogue)


def _mm_nt_stacked(name, a_spec, a, wg, *, M, tm, tn, tk, out_dtype=F32, after=None):
    _, Kw, nloc = wg.shape
    q = nloc // tk
    return _matmul(
        name, a, wg, dims=_NT, grid_mnk=(M // tm, Kw // tn, (nloc * N_DEV) // tk), tiles=(tm, tn),
        a_spec=a_spec, b_spec=pl.BlockSpec((None, tn, tk), lambda i, j, k: (k // q, j, k % q)),
        out_shapes=[jax.ShapeDtypeStruct((M, Kw), out_dtype)], out_specs=[pl.BlockSpec((tm, tn), lambda i, j, k: (i, j))],
        epilogue=_store(out_dtype), after=after)[0]


def _mm_tn(name, a, b, b_spec, *, Mo, No, S, tm, tn, tk, stacked_nloc=None, after=None):
    if stacked_nloc is None:
        out_shape = jax.ShapeDtypeStruct((Mo, No), BF16)
        out_spec = pl.BlockSpec((tm, tn), lambda i, j, k: (i, j))
    else:
        q = stacked_nloc // tn
        out_shape = jax.ShapeDtypeStruct((N_DEV, Mo, stacked_nloc), BF16)
        out_spec = pl.BlockSpec((None, tm, tn), lambda i, j, k: (j // q, i, j % q))
    return _matmul(
        name, a, b, dims=_TN, grid_mnk=(Mo // tm, No // tn, S // tk), tiles=(tm, tn),
        a_spec=pl.BlockSpec((tk, tm), lambda i, j, k: (k, i)), b_spec=b_spec,
        out_shapes=[out_shape], out_specs=[out_spec], epilogue=_store(BF16), after=after)[0]


def _norm_mod(name, x, g, sc, sh):
    S, D = x.shape
    tm = _tile(S, 256, 8)

    def body(x_ref, g_ref, sc_ref, sh_ref, h_ref):
        xv = x_ref[...]
        r = lax.rsqrt(jnp.mean(xv * xv, axis=-1, keepdims=True) + EPS)
        h = (xv * r) * g_ref[...]
        h_ref[...] = (h * (1.0 + sc_ref[...]) + sh_ref[...]).astype(BF16)

    row = pl.BlockSpec((tm, D), lambda i: (i, 0))
    vec = pl.BlockSpec((1, D), lambda i: (0, 0))
    return pl.pallas_call(body, name=name, grid=(S // tm,), in_specs=[row, vec, vec, vec], out_specs=row,
                          out_shape=jax.ShapeDtypeStruct((S, D), BF16), compiler_params=_cparams(("parallel",)))(x, g, sc, sh)


def _norm_mod_bwd(name, dh, x, g, sc, dres, o=None, gt=None):
    S, D = x.shape
    tm = _tile(S, 256, 8)
    gated = o is not None

    def body(*refs):
        if gated:
            dh_ref, x_ref, g_ref, sc_ref, dres_ref, o_ref, gt_ref, dx_ref, vec_ref, do_ref = refs
        else:
            dh_ref, x_ref, g_ref, sc_ref, dres_ref, dx_ref, vec_ref = refs
        i = pl.program_id(0)

        @pl.when(i == 0)
        def _():
            vec_ref[...] = jnp.zeros_like(vec_ref)

        xv, dh_v, gv = x_ref[...], dh_ref[...], g_ref[...]
        r = lax.rsqrt(jnp.mean(xv * xv, axis=-1, keepdims=True) + EPS)
        xn = xv * r
        one_sc = 1.0 + sc_ref[...]
        vec_ref[0:1, :] += jnp.sum(dh_v, axis=0, keepdims=True)
        vec_ref[1:2, :] += jnp.sum(dh_v * (xn * gv), axis=0, keepdims=True)
        vec_ref[2:3, :] += jnp.sum(dh_v * one_sc * xn, axis=0, keepdims=True)
        dxn = dh_v * one_sc * gv
        dx = dres_ref[...] + r * (dxn - xn * jnp.mean(dxn * xn, axis=-1, keepdims=True))
        dx_ref[...] = dx
        if gated:
            vec_ref[3:4, :] += jnp.sum(dx * o_ref[...], axis=0, keepdims=True)
            do_ref[...] = (dx * gt_ref[...]).astype(BF16)

    row = pl.BlockSpec((tm, D), lambda i: (i, 0))
    vec = pl.BlockSpec((1, D), lambda i: (0, 0))
    acc = pl.BlockSpec((8, D), lambda i: (0, 0))
    ins = [dh, x, g, sc, dres] + ([o, gt] if gated else [])
    in_specs = [row, row, vec, vec, row] + ([row, vec] if gated else [])
    out_shape = [jax.ShapeDtypeStruct((S, D), F32), jax.ShapeDtypeStruct((8, D), F32)]
    out_specs = [row, acc]
    if gated:
        out_shape.append(jax.ShapeDtypeStruct((S, D), BF16))
        out_specs.append(row)
    return pl.pallas_call(body, name=name, grid=(S // tm,), in_specs=in_specs, out_specs=out_specs, out_shape=out_shape,
                          compiler_params=_cparams(("arbitrary",)))(*ins)


def _loss_head(x3, tgt, gf, o2, gt2):
    S, D = x3.shape
    tm = _tile(S, 256, 8)

    def body(x_ref, t_ref, g_ref, o_ref, gt_ref, dx_ref, do_ref, vec_ref):
        i = pl.program_id(0)

        @pl.when(i == 0)
        def _():
            vec_ref[...] = jnp.zeros_like(vec_ref)

        xv, gv = x_ref[...], g_ref[...]
        r = lax.rsqrt(jnp.mean(xv * xv, axis=-1, keepdims=True) + EPS)
        xn = xv * r
        e = xn * gv - t_ref[...]
        tok = 0.5 * jnp.mean(e * e, axis=-1, keepdims=True)
        vec_ref[0:1, :] += jnp.broadcast_to(jnp.sum(tok, axis=0, keepdims=True), (1, D))
        dy = e * (1.0 / D)
        vec_ref[1:2, :] += jnp.sum(dy * xn, axis=0, keepdims=True)
        dxn = dy * gv
        dx = r * (dxn - xn * jnp.mean(dxn * xn, axis=-1, keepdims=True))
        dx_ref[...] = dx
        vec_ref[2:3, :] += jnp.sum(dx * o_ref[...], axis=0, keepdims=True)
        do_ref[...] = (dx * gt_ref[...]).astype(BF16)

    row = pl.BlockSpec((tm, D), lambda i: (i, 0))
    vec = pl.BlockSpec((1, D), lambda i: (0, 0))
    return pl.pallas_call(
        body, name="loss_head", grid=(S // tm,), in_specs=[row, row, vec, row, vec],
        out_specs=[row, row, pl.BlockSpec((8, D), lambda i: (0, 0))],
        out_shape=[jax.ShapeDtypeStruct((S, D), F32), jax.ShapeDtypeStruct((S, D), BF16), jax.ShapeDtypeStruct((8, D), F32)],
        compiler_params=_cparams(("arbitrary",)))(x3, tgt, gf, o2, gt2)


def _swiglu(ab, F, tf):
    S = ab.shape[0]
    tm = _tile(S, 512, 8)
    nf = F // tf

    def body(a_ref, u_ref, h_ref):
        a = a_ref[...]
        h_ref[...] = (a * _sigmoid(a) * u_ref[...]).astype(BF16)

    return pl.pallas_call(
        body, name="swiglu", grid=(S // tm, nf),
        in_specs=[pl.BlockSpec((tm, tf), lambda i, j: (i, j)), pl.BlockSpec((tm, tf), lambda i, j: (i, j + nf))],
        out_specs=pl.BlockSpec((tm, tf), lambda i, j: (i, j)), out_shape=jax.ShapeDtypeStruct((S, F), BF16),
        compiler_params=_cparams(("parallel", "parallel")))(ab, ab)


def _colsum2(dg2):
    _, S, D = dg2.shape
    tm = _tile(S, 256, 16)

    def body(x_ref, o_ref):
        @pl.when(pl.program_id(0) == 0)
        def _():
            o_ref[...] = jnp.zeros_like(o_ref)

        o_ref[0:1, :] += jnp.sum(x_ref[0].astype(F32), axis=0, keepdims=True)
        o_ref[1:2, :] += jnp.sum(x_ref[1].astype(F32), axis=0, keepdims=True)

    return pl.pallas_call(body, name="gate_bias_grad", grid=(S // tm,), in_specs=[pl.BlockSpec((2, tm, D), lambda i: (0, i, 0))],
                          out_specs=pl.BlockSpec((2, D), lambda i: (0, 0)), out_shape=jax.ShapeDtypeStruct((2, D), F32),
                          compiler_params=_cparams(("arbitrary",)))(dg2)


def _gmlp_common(u_ref, v_ref, lg_ref, lb_ref, ws_ref, bsb_ref, G, T, Dg):
    ug, dug = _gelu_parts(u_ref[...])
    vg, dvg = _gelu_parts(v_ref[...])
    mu = jnp.mean(vg, axis=-1, keepdims=True)
    vc = vg - mu
    rstd = lax.rsqrt(jnp.mean(vc * vc, axis=-1, keepdims=True) + EPS)
    vhat = vc * rstd
    vn = vhat * lg_ref[...] + lb_ref[...]
    row = lax.broadcasted_iota(jnp.int32, (T, T), 0)
    col = lax.broadcasted_iota(jnp.int32, (T, T), 1)
    tril = row >= col
    s = []
    for g in range(G):
        w = jnp.where(tril, ws_ref[g], 0.0)
        s.append(_dot(w, vn[:, g * Dg:(g + 1) * Dg]) + bsb_ref[g])
    return ug, dug, dvg, rstd, vhat, vn, tril, s


def _gmlp_fwd(z, ln_g, ln_b, ws, bsb, GW):
    S = z.shape[0]
    G, T, _ = ws.shape
    Dg = GW // G

    def body(u_ref, v_ref, lg_ref, lb_ref, ws_ref, bsb_ref, ya_ref):
        ug, _, _, _, _, _, _, s = _gmlp_common(u_ref, v_ref, lg_ref, lb_ref, ws_ref, bsb_ref, G, T, Dg)
        for g in range(G):
            sl = slice(g * Dg, (g + 1) * Dg)
            ya_ref[:, sl] = (ug[:, sl] * s[g]).astype(BF16)

    vec = pl.BlockSpec((1, GW), lambda c: (0, 0))
    return pl.pallas_call(
        body, name="gmlp_fwd", grid=(S // T,),
        in_specs=[pl.BlockSpec((T, GW), lambda c: (c, 0)), pl.BlockSpec((T, GW), lambda c: (c, 1)), vec, vec,
                  pl.BlockSpec((G, T, T), lambda c: (0, 0, 0)), pl.BlockSpec((G, T, Dg), lambda c: (0, 0, 0))],
        out_specs=pl.BlockSpec((T, GW), lambda c: (c, 0)), out_shape=jax.ShapeDtypeStruct((S, GW), BF16),
        compiler_params=_cparams(("parallel",)))(z, z, ln_g, ln_b, ws, bsb)


def _gmlp_bwd(z, dya, ln_g, ln_b, ws, bsb, GW):
    S = z.shape[0]
    G, T, _ = ws.shape
    Dg = GW // G
    nc = S // T

    def body(u_ref, v_ref, dya_ref, lg_ref, lb_ref, ws_ref, bsb_ref, dz_ref, dln_ref, dws_ref, dbs_ref, dbs_acc, dvh):
        c = pl.program_id(0)

        @pl.when(c == 0)
        def _():
            dln_ref[...] = jnp.zeros_like(dln_ref)
            dws_ref[...] = jnp.zeros_like(dws_ref)
            dbs_acc[...] = jnp.zeros_like(dbs_acc)

        ug, dug, dvg, rstd, vhat, vn, tril, s = _gmlp_common(u_ref, v_ref, lg_ref, lb_ref, ws_ref, bsb_ref, G, T, Dg)
        dya_v = dya_ref[...]
        for g in range(G):
            sl = slice(g * Dg, (g + 1) * Dg)
            dy_g = dya_v[:, sl]
            dz_ref[:, sl] = (dy_g * s[g] * dug[:, sl]).astype(BF16)
            ds = dy_g * ug[:, sl]
            dbs_acc[g] += ds
            w = jnp.where(tril, ws_ref[g], 0.0)
            dvn_g = _dot(w, ds, _TN)
            dws_ref[g] += jnp.where(tril, _dot(ds, vn[:, sl], _NT), 0.0)
            dln_ref[0:1, sl] += jnp.sum(dvn_g * vhat[:, sl], axis=0, keepdims=True)
            dln_ref[1:2, sl] += jnp.sum(dvn_g, axis=0, keepdims=True)
            dvh[:, sl] = dvn_g * lg_ref[:, sl]
        dvhat = dvh[...]
        m1 = jnp.mean(dvhat, axis=-1, keepdims=True)
        m2 = jnp.mean(dvhat * vhat, axis=-1, keepdims=True)
        dz_ref[:, GW:2 * GW] = (rstd * (dvhat - m1 - vhat * m2) * dvg).astype(BF16)

        @pl.when(c == nc - 1)
        def _():
            for g in range(G):
                dbs_ref[g] = jnp.sum(dbs_acc[g], axis=-1, keepdims=True)

    vec = pl.BlockSpec((1, GW), lambda c: (0, 0))
    return pl.pallas_call(
        body, name="gmlp_bwd", grid=(nc,),
        in_specs=[pl.BlockSpec((T, GW), lambda c: (c, 0)), pl.BlockSpec((T, GW), lambda c: (c, 1)),
                  pl.BlockSpec((T, GW), lambda c: (c, 0)), vec, vec,
                  pl.BlockSpec((G, T, T), lambda c: (0, 0, 0)), pl.BlockSpec((G, T, Dg), lambda c: (0, 0, 0))],
        out_specs=[pl.BlockSpec((T, 2 * GW), lambda c: (c, 0)), pl.BlockSpec((8, GW), lambda c: (0, 0)),
                   pl.BlockSpec((G, T, T), lambda c: (0, 0, 0)), pl.BlockSpec((G, T, 1), lambda c: (0, 0, 0))],
        out_shape=[jax.ShapeDtypeStruct((S, 2 * GW), BF16), jax.ShapeDtypeStruct((8, GW), F32),
                   jax.ShapeDtypeStruct((G, T, T), F32), jax.ShapeDtypeStruct((G, T, 1), F32)],
        scratch_shapes=[pltpu.VMEM((G, T, Dg), F32), pltpu.VMEM((T, GW), F32)],
        compiler_params=_cparams(("arbitrary",)))(z, z, dya, ln_g, ln_b, ws, bsb)


def _hg_common(q_ref, f_ref, hlb_ref):
    C = HG_CHUNK
    a = hlb_ref[...]
    lb = _sigmoid(a[0:1, :] - a[1:2, :])
    sig = _sigmoid(f_ref[...])
    f = lb + (1.0 - lb) * sig
    lf = jnp.log(f)
    kk = 1.0 - f
    q = q_ref[...]
    sq = _sigmoid(q)
    qa = q * sq
    row = lax.broadcasted_iota(jnp.int32, (C, C), 0)
    col = lax.broadcasted_iota(jnp.int32, (C, C), 1)
    tril = row >= col
    b = _ones_dot(tril.astype(BF16), lf)
    bm = b[HG_MID:HG_MID + 1, :]
    bl = b[C - 1:C, :]
    e_b = jnp.exp(b)
    e_qm = jnp.exp(jnp.minimum(b - bm, EXP_CLAMP))
    e_km = jnp.exp(jnp.minimum(bm - b, EXP_CLAMP))
    e_kl = jnp.exp(bl - b)
    return dict(lb=lb, sig=sig, f=f, kk=kk, q=q, sq=sq, qa=qa, tril=tril, e_b=e_b, e_qm=e_qm, e_km=e_km, e_kl=e_kl,
                e_l=jnp.exp(bl), qh=qa * e_b, qt=qa * e_qm, kt=kk * e_km, kh=kk * e_kl)


def _hg_fwd(z, hg_lb, ng, HW):
    S = z.shape[0]
    C, H, dk = HG_CHUNK, HW // HG_DK, HG_DK
    nc = S // C

    def body(q_ref, f_ref, i_ref, og_ref, hlb_ref, ng_ref, yb_ref, o_ref, st_ref, state):
        @pl.when(pl.program_id(0) == 0)
        def _():
            state[...] = jnp.zeros_like(state)

        t = _hg_common(q_ref, f_ref, hlb_ref)
        iv = i_ref[...]
        for h in range(H):
            sl = slice(h * dk, (h + 1) * dk)
            st = state[h]
            st_ref[h] = st
            a = jnp.where(t["tril"], _dot(t["qt"][:, sl], t["kt"][:, sl], _NT), 0.0)
            o_h = _dot(a, iv[:, sl]) + _dot(t["qh"][:, sl], st, _NT)
            state[h] = st * t["e_l"][:, sl] + _dot(iv[:, sl], t["kh"][:, sl], _TN)
            o_ref[:, sl] = o_h
            rr = lax.rsqrt(jnp.mean(o_h * o_h, axis=-1, keepdims=True) + EPS)
            og = og_ref[:, sl]
            yb_ref[:, sl] = (o_h * rr * ng_ref[:, sl] * (og * _sigmoid(og))).astype(BF16)

    def col(k):
        return pl.BlockSpec((C, HW), lambda c: (c, k))

    base = 2
    return pl.pallas_call(
        body, name="hgrn_fwd", grid=(nc,),
        in_specs=[col(base), col(base + 1), col(base + 2), col(base + 3),
                  pl.BlockSpec((2, HW), lambda c: (0, 0)), pl.BlockSpec((1, HW), lambda c: (0, 0))],
        out_specs=[pl.BlockSpec((C, HW), lambda c: (c, 0)), pl.BlockSpec((C, HW), lambda c: (c, 0)),
                   pl.BlockSpec((None, H, dk, dk), lambda c: (c, 0, 0, 0))],
        out_shape=[jax.ShapeDtypeStruct((S, HW), BF16), jax.ShapeDtypeStruct((S, HW), F32),
                   jax.ShapeDtypeStruct((nc, H, dk, dk), F32)],
        scratch_shapes=[pltpu.VMEM((H, dk, dk), F32)],
        compiler_params=_cparams(("arbitrary",)))(z, z, z, z, hg_lb, ng)


def _hg_bwd(z, o, states, dyb, hg_lb, ng, HW):
    S = z.shape[0]
    C, H, dk = HG_CHUNK, HW // HG_DK, HG_DK
    nc = S // C

    def body(q_ref, f_ref, i_ref, og_ref, o_ref, st_ref, stn_ref, dyb_ref, hlb_ref, ng_ref, dz_ref, dng_ref, dhlb_ref,
             dstate, cross, dqa_buf, dkk_buf, db_buf, dlb_acc):
        c = pl.program_id(0)

        @pl.when(c == 0)
        def _():
            dstate[...] = jnp.zeros_like(dstate)
            dlb_acc[...] = jnp.zeros_like(dlb_acc)
            dng_ref[...] = jnp.zeros_like(dng_ref)

        def r16(v):
            return v.astype(BF16).astype(F32)

        t = _hg_common(q_ref, f_ref, hlb_ref)
        iv = i_ref[...]
        for h in range(H):
            sl = slice(h * dk, (h + 1) * dk)
            o_h, og, dyb_h, ng_h = o_ref[:, sl], og_ref[:, sl], dyb_ref[:, sl], ng_ref[:, sl]
            sg = _sigmoid(og)
            silu_og = og * sg
            rr = lax.rsqrt(jnp.mean(o_h * o_h, axis=-1, keepdims=True) + EPS)
            on = o_h * rr
            dng_ref[0:1, sl] += jnp.sum(dyb_h * on * silu_og, axis=0, keepdims=True)
            dz_ref[:, 3 * HW + h * dk:3 * HW + (h + 1) * dk] = (dyb_h * on * ng_h * (sg * (1.0 + og * (1.0 - sg)))).astype(BF16)
            don = dyb_h * ng_h * silu_og
            do_h = rr * (don - on * jnp.mean(don * on, axis=-1, keepdims=True))

            qt, kt, qh, kh, iv_h = t["qt"][:, sl], t["kt"][:, sl], t["qh"][:, sl], t["kh"][:, sl], iv[:, sl]
            a = jnp.where(t["tril"], _dot(qt, kt, _NT), 0.0)
            da = jnp.where(t["tril"], _dot(do_h, iv_h, _NT), 0.0)
            st, dst = st_ref[h], dstate[h]
            cross[:, sl] = jnp.sum(dst * stn_ref[h], axis=0, keepdims=True)
            dqh = _dot(do_h, st)
            dstate[h] = _dot(do_h, qh, _TN) + dst * t["e_l"][:, sl]
            div = _dot(a, do_h, _TN) + _dot(kh, dst, _NT)
            dkh = _dot(iv_h, dst)
            dqt = _dot(da, kt)
            dkt = _dot(da, qt, _TN)
            dz_ref[:, 2 * HW + h * dk:2 * HW + (h + 1) * dk] = div.astype(BF16)
            dqa_buf[:, sl] = dqh * t["e_b"][:, sl] + dqt * t["e_qm"][:, sl]
            dkk_buf[:, sl] = dkt * t["e_km"][:, sl] + dkh * t["e_kl"][:, sl]
            db_buf[:, sl] = r16(qt) * dqt - r16(kt) * dkt + r16(qh) * dqh - r16(kh) * dkh

        dqa, dkk = dqa_buf[...], dkk_buf[...]
        triu = jnp.logical_not(t["tril"]) | (lax.broadcasted_iota(jnp.int32, (C, C), 0) == lax.broadcasted_iota(jnp.int32, (C, C), 1))
        dlf = _ones_dot(triu.astype(BF16), db_buf[...]) + cross[...]
        df = dlf / t["f"] - dkk
        sig, lb = t["sig"], t["lb"]
        dz_ref[:, HW:2 * HW] = (df * (1.0 - lb) * sig * (1.0 - sig)).astype(BF16)
        dlb_acc[...] += jnp.sum(df * (1.0 - sig), axis=0, keepdims=True)
        q, sq = t["q"], t["sq"]
        dz_ref[:, 0:HW] = (dqa * (sq * (1.0 + q * (1.0 - sq)))).astype(BF16)

        @pl.when(c == nc - 1)
        def _():
            da0 = dlb_acc[...] * lb * (1.0 - lb)
            dhlb_ref[0:1, :] = da0
            dhlb_ref[1:2, :] = -da0

    def col(k):
        return pl.BlockSpec((C, HW), lambda c: (nc - 1 - c, k))

    base = 2
    return pl.pallas_call(
        body, name="hgrn_bwd", grid=(nc,),
        in_specs=[col(base), col(base + 1), col(base + 2), col(base + 3), col(0),
                  pl.BlockSpec((None, H, dk, dk), lambda c: (nc - 1 - c, 0, 0, 0)),
                  pl.BlockSpec((None, H, dk, dk), lambda c: (jnp.minimum(nc - c, nc - 1), 0, 0, 0)), col(0),
                  pl.BlockSpec((2, HW), lambda c: (0, 0)), pl.BlockSpec((1, HW), lambda c: (0, 0))],
        out_specs=[pl.BlockSpec((C, 4 * HW), lambda c: (nc - 1 - c, 0)), pl.BlockSpec((8, HW), lambda c: (0, 0)),
                   pl.BlockSpec((2, HW), lambda c: (0, 0))],
        out_shape=[jax.ShapeDtypeStruct((S, 4 * HW), BF16), jax.ShapeDtypeStruct((8, HW), F32), jax.ShapeDtypeStruct((2, HW), F32)],
        scratch_shapes=[pltpu.VMEM((H, dk, dk), F32), pltpu.VMEM((1, HW), F32), pltpu.VMEM((C, HW), F32), pltpu.VMEM((C, HW), F32),
                        pltpu.VMEM((C, HW), F32), pltpu.VMEM((1, HW), F32)],
        compiler_params=_cparams(("arbitrary",)))(z, z, z, z, o, states, states, dyb, hg_lb, ng)


def _position():
    x, y, c = lax.axis_index("x"), lax.axis_index("y"), lax.axis_index("c")
    return x, y, c, 4 * x + 2 * y + c


def _flip(x, y, c, k):
    return (1 - x if k & 4 else x, 1 - y if k & 2 else y, 1 - c if k & 1 else c)


def _allgather_small(name, v):
    R, L = v.shape

    def body(v_ref, out_ref, send_sems, recv_sems):
        x, y, c, me = _position()
        out_ref[me] = v_ref[...]
        copies = []
        for k in range(1, N_DEV):
            cp = pltpu.make_async_remote_copy(src_ref=v_ref, dst_ref=out_ref.at[me], send_sem=send_sems.at[k - 1],
                                              recv_sem=recv_sems.at[k - 1], device_id=_flip(x, y, c, k), device_id_type=MESH)
            cp.start()
            copies.append(cp)
        for cp in copies:
            cp.wait()

    return pl.pallas_call(
        body, name=name, out_shape=jax.ShapeDtypeStruct((N_DEV, R, L), v.dtype),
        in_specs=[pl.BlockSpec(memory_space=pltpu.VMEM)], out_specs=pl.BlockSpec(memory_space=pltpu.VMEM),
        scratch_shapes=[pltpu.SemaphoreType.DMA((N_DEV - 1,)), pltpu.SemaphoreType.DMA((N_DEV - 1,))],
        compiler_params=pltpu.CompilerParams(vmem_limit_bytes=VMEM_LIMIT),
    )(v)


def _allgather_hbm(name, shards):
    n = len(shards)

    def body(*refs):
        ins, outs = refs[:n], refs[n:2 * n]
        send_sems, recv_sems, local_sems = refs[2 * n:]
        x, y, c, me = _position()
        sibling = (x, y, 1 - c)
        chips = [(1 - x, y), (x, 1 - y), (1 - x, 1 - y)]

        def slot(px, py, pc):
            return 4 * px + 2 * py + pc

        def copy(w, k, block, to, src=None):
            dst = outs[w].at[slot(*block)]
            return pltpu.make_async_remote_copy(src_ref=dst if src is None else src, dst_ref=dst, send_sem=send_sems.at[w, k],
                                                recv_sem=recv_sems.at[w, k], device_id=to, device_id_type=MESH)

        mine, first, passed = [], [], []
        for w in range(n):
            cp = pltpu.make_async_copy(ins[w], outs[w].at[me], local_sems.at[w])
            cp.start()
            mine.append(cp)
            for j, chip in enumerate(chips):
                first.append(copy(w, 1 + j, (x, y, c), (*chip, c), src=ins[w]))
            first.append(copy(w, 0, (x, y, c), sibling, src=ins[w]))
        for cp in first:
            cp.start()
        for w in range(n):
            for j, chip in enumerate(chips):
                copy(w, 1 + j, (*chip, c), (x, y, c)).wait_recv()
                cp = copy(w, 4 + j, (*chip, c), sibling)
                cp.start()
                passed.append(cp)
        for w in range(n):
            copy(w, 0, sibling, (x, y, c)).wait_recv()
            for j, chip in enumerate(chips):
                copy(w, 4 + j, (*chip, 1 - c), (x, y, c)).wait_recv()
        for cp in first + passed:
            cp.wait_send()
        for cp in mine:
            cp.wait()

    hbm = pl.BlockSpec(memory_space=pltpu.HBM)
    return pl.pallas_call(
        body, name=name, out_shape=[jax.ShapeDtypeStruct((N_DEV, *s.shape), s.dtype) for s in shards],
        in_specs=[hbm] * n, out_specs=[hbm] * n,
        scratch_shapes=[pltpu.SemaphoreType.DMA((n, 7)), pltpu.SemaphoreType.DMA((n, 7)), pltpu.SemaphoreType.DMA((n,))],
    )(*shards)


_HBM = pl.BlockSpec(memory_space=pltpu.HBM)
_SEM = pl.BlockSpec(memory_space=pltpu.SEMAPHORE)
_EFFECT = pltpu.SideEffectType.DATAFLOW_SIDE_EFFECTING


def _split_start(name, bufs, n_sems, copies_fn, after=None):
    nb = len(bufs)
    extra = [] if after is None else [after]
    k = nb + len(extra)

    def body(*refs):
        for cp in copies_fn(refs[:nb], refs[k], refs[k + 1]):
            cp.start()
        refs[-1][...] = jnp.zeros_like(refs[-1])

    sems = pltpu.SemaphoreType.DMA((n_sems,))
    res = pl.pallas_call(
        body, name=name,
        out_shape=(sems, sems, *[pltpu.HBM(a.shape, a.dtype) for a in bufs], jax.ShapeDtypeStruct((8, LANES), F32)),
        in_specs=[_HBM] * nb + [pl.BlockSpec(memory_space=pl.ANY)] * len(extra),
        out_specs=(_SEM, _SEM, *[_HBM] * nb, pl.BlockSpec(memory_space=pltpu.VMEM)),
        input_output_aliases={i: 2 + i for i in range(nb)},
        compiler_params=pltpu.CompilerParams(has_side_effects=_EFFECT),
    )(*[pltpu.with_memory_space_constraint(a, pltpu.HBM) for a in bufs], *extra)
    return res[0], res[1], list(res[2:2 + nb]), res[-1]


def _split_wait(name, bufs, send_sems, recv_sems, after, copies_fn):
    nb = len(bufs)

    def body(*refs):
        for cp in copies_fn(refs[:nb], refs[nb], refs[nb + 1]):
            cp.wait_send()
            cp.wait_recv()

    res = pl.pallas_call(
        body, name=name, out_shape=tuple(pltpu.HBM(a.shape, a.dtype) for a in bufs),
        in_specs=[_HBM] * nb + [_SEM, _SEM, pl.BlockSpec(memory_space=pl.ANY)], out_specs=tuple([_HBM] * nb),
        input_output_aliases={i: i for i in range(nb)},
        compiler_params=pltpu.CompilerParams(has_side_effects=_EFFECT),
    )(*bufs, send_sems, recv_sems, after)
    return list(res)


def _split_relay(name, bufs, send_sems, recv_sems, after, wait_fn, n_sems, start_fn):
    nb = len(bufs)

    def body(*refs):
        for cp in wait_fn(refs[:nb], refs[nb], refs[nb + 1]):
            cp.wait_send()
            cp.wait_recv()
        for cp in start_fn(refs[:nb], refs[nb + 3], refs[nb + 4]):
            cp.start()

    sems = pltpu.SemaphoreType.DMA((n_sems,))
    res = pl.pallas_call(
        body, name=name, out_shape=(sems, sems, *[pltpu.HBM(a.shape, a.dtype) for a in bufs]),
        in_specs=[_HBM] * nb + [_SEM, _SEM, pl.BlockSpec(memory_space=pl.ANY)], out_specs=(_SEM, _SEM, *[_HBM] * nb),
        input_output_aliases={i: 2 + i for i in range(nb)},
        compiler_params=pltpu.CompilerParams(has_side_effects=_EFFECT),
    )(*bufs, send_sems, recv_sems, after)
    return res[0], res[1], list(res[2:])


N_CHIP = 4


def _chip_flip(x, y, k):
    return (1 - x if k & 2 else x), (1 - y if k & 1 else y)


def _gather_first_copies(n):
    def copies(bufs, send_sems, recv_sems):
        x, y, c, me = _position()
        out = []
        for w in range(n):
            for k in range(N_CHIP):
                to = (x, y, 1 - c) if k == 0 else (*_chip_flip(x, y, k), c)
                out.append(pltpu.make_async_remote_copy(
                    src_ref=bufs[w], dst_ref=bufs[n + w].at[me], send_sem=send_sems.at[w * N_CHIP + k],
                    recv_sem=recv_sems.at[w * N_CHIP + k], device_id=to, device_id_type=MESH))
        return out
    return copies


def _gather_relay_copies(n):
    def copies(bufs, send_sems, recv_sems):
        x, y, c, _ = _position()
        out = []
        for w in range(n):
            for k in range(1, N_CHIP):
                px, py = _chip_flip(x, y, k)
                blk = bufs[n + w].at[4 * px + 2 * py + c]
                out.append(pltpu.make_async_remote_copy(
                    src_ref=blk, dst_ref=blk, send_sem=send_sems.at[w * (N_CHIP - 1) + k - 1],
                    recv_sem=recv_sems.at[w * (N_CHIP - 1) + k - 1], device_id=(x, y, 1 - c), device_id_type=MESH))
        return out
    return copies


def _to_sibling_copies(n):
    def copies(bufs, send_sems, recv_sems):
        x, y, c, _ = _position()
        out = []
        for w in range(n):
            for q in range(N_CHIP):
                out.append(pltpu.make_async_remote_copy(
                    src_ref=bufs[w].at[2 * q + 1 - c], dst_ref=bufs[n + w].at[q], send_sem=send_sems.at[w * N_CHIP + q],
                    recv_sem=recv_sems.at[w * N_CHIP + q], device_id=(x, y, 1 - c), device_id_type=MESH))
        return out
    return copies


def _to_owner_copies(n):
    def copies(bufs, send_sems, recv_sems):
        x, y, c, _ = _position()
        out = []
        for w in range(n):
            for k in range(1, N_CHIP):
                px, py = (1 - x if k & 2 else x), (1 - y if k & 1 else y)
                out.append(pltpu.make_async_remote_copy(
                    src_ref=bufs[w].at[2 * px + py], dst_ref=bufs[n + w].at[k - 1], send_sem=send_sems.at[w * (N_CHIP - 1) + k - 1],
                    recv_sem=recv_sems.at[w * (N_CHIP - 1) + k - 1], device_id=(px, py, c), device_id_type=MESH))
        return out
    return copies


def _chip_sum(name, stack, landed, c_idx):
    _, R, C = stack.shape
    tr = _tile(R, max(16, 262144 // C), 16)

    def body(c_ref, a_ref, b_ref, o_ref):
        o_ref[...] = (a_ref[...].astype(F32) + b_ref[...].astype(F32)).astype(o_ref.dtype)

    return pl.pallas_call(
        body, name=name,
        grid_spec=pltpu.PrefetchScalarGridSpec(
            num_scalar_prefetch=1, grid=(N_CHIP, R // tr),
            in_specs=[pl.BlockSpec((None, tr, C), lambda q, i, c_ref: (2 * q + c_ref[0], i, 0)),
                      pl.BlockSpec((None, tr, C), lambda q, i, c_ref: (q, i, 0))],
            out_specs=pl.BlockSpec((None, tr, C), lambda q, i, c_ref: (q, i, 0))),
        out_shape=jax.ShapeDtypeStruct((N_CHIP, R, C), stack.dtype),
        compiler_params=_cparams(("parallel", "parallel")))(c_idx, stack, landed)


def _ada_mod(c16, w):
    _, D = c16.shape
    n = w.shape[1]
    tk = _tile(D, 512)
    nk = D // tk

    def body(c_ref, w_ref, o_ref, ca_ref):
        @pl.when(pl.program_id(0) == 0)
        def _():
            o_ref[...] = jnp.zeros_like(o_ref)

        cv = c_ref[...]
        ca = cv * _sigmoid(cv)
        ca_ref[...] = ca
        o_ref[...] += _dot(ca, w_ref[...])

    return pl.pallas_call(
        body, name="ada_mod", grid=(nk,),
        in_specs=[pl.BlockSpec((16, tk), lambda k: (0, k)), pl.BlockSpec((tk, n), lambda k: (k, 0))],
        out_specs=[pl.BlockSpec((16, n), lambda k: (0, 0)), pl.BlockSpec((16, tk), lambda k: (0, k))],
        out_shape=[jax.ShapeDtypeStruct((16, n), F32), jax.ShapeDtypeStruct((16, D), F32)],
        compiler_params=_cparams(("arbitrary",)))(c16, w)


def _adam_math(w, g, m, v):
    m2 = ADAM_B1 * m + (1.0 - ADAM_B1) * g
    v2 = ADAM_B2 * v + (1.0 - ADAM_B2) * (g * g)
    m_hat = m2 / (1.0 - ADAM_B1 ** ADAM_STEP)
    v_hat = v2 / (1.0 - ADAM_B2 ** ADAM_STEP)
    delta = -ADAM_LR * (m_hat / (jnp.sqrt(v_hat) + ADAM_EPS) + ADAM_WD * w)
    return delta, m2, v2


def _adamw(name, w, m, v, parts):
    R, C = w.shape
    tr = _tile(R, max(16, 131072 // C), 16)
    n_p = len(parts)

    def body(*refs):
        w_ref, m_ref, v_ref = refs[:3]
        g_ref, d_ref, m2_ref, v2_ref = refs[3 + n_p:]
        g = None
        for p_ref in refs[3:3 + n_p]:
            for s in range(p_ref.shape[0]):
                t = p_ref[s].astype(F32)
                g = t if g is None else g + t
        delta, m2, v2 = _adam_math(w_ref[...], g, m_ref[...], v_ref[...])
        g_ref[...] = g
        d_ref[...] = delta
        m2_ref[...] = m2
        v2_ref[...] = v2

    blk = pl.BlockSpec((tr, C), lambda i: (i, 0))
    out = jax.ShapeDtypeStruct((R, C), F32)
    return pl.pallas_call(body, name=name, grid=(R // tr,),
                          in_specs=[blk, blk, blk] + [pl.BlockSpec((a.shape[0], tr, C), lambda i: (0, i, 0)) for a in parts],
                          out_specs=[blk] * 4, out_shape=[out] * 4, compiler_params=_cparams(("parallel",)))(w, m, v, *parts)


def _small_update(gathered, w, m, v, after):
    _, R, L = gathered.shape
    rs = w.shape[0]

    def body(p_ref, w_ref, m_ref, v_ref, after_ref, g_ref, d_ref, m2_ref, v2_ref):
        g = p_ref[0]
        for p in range(1, N_DEV):
            g = g + p_ref[p]
        g_ref[...] = g
        delta, m2, v2 = _adam_math(w_ref[...], g[0:rs, :], m_ref[...], v_ref[...])
        d_ref[...] = delta
        m2_ref[...] = m2
        v2_ref[...] = v2

    vm = pl.BlockSpec(memory_space=pltpu.VMEM)
    sm = jax.ShapeDtypeStruct((rs, L), F32)
    return pl.pallas_call(body, name="small_update", in_specs=[vm] * 4 + [pl.BlockSpec(memory_space=pl.ANY)], out_specs=[vm] * 4,
                          out_shape=[jax.ShapeDtypeStruct((R, L), F32), sm, sm, sm],
                          compiler_params=pltpu.CompilerParams(vmem_limit_bytes=VMEM_LIMIT))(gathered, w, m, v, after)


class _Fetched(dict):
    def __init__(self, fetch):
        super().__init__()
        self.fetch = fetch

    def first(self, key, after):
        self[key] = self.fetch(key, after)
        return self[key]


def _local_step(x, tgt, mod, p, fetch, F, scatter=None):
    S, D = x.shape
    GW, HW = p["ln_g"].shape[1], p["hg_ng"].shape[1]
    G, T, _ = p["ws"].shape
    w = _Fetched(fetch)
    INW = 2 * GW + 4 * HW + 2 * D
    in_loc, br_loc, fi_loc = INW // N_DEV, D // N_DEV, 2 * F // N_DEV
    assert GW == HW and F % fi_loc == 0
    sh1, sc1, gt1, sh2, sc2, gt2 = (mod[:, k * D:(k + 1) * D] for k in range(6))
    bsb = jnp.broadcast_to(p["bs"][:, :, None], (G, T, GW // G))

    tm = _tile(S, 1024, 16)
    tmh = _tile(S, 512, 16)
    tn_in = _tile(in_loc, 640)
    tn_d = _tile(D, 512)
    tn_br = _tile(br_loc, 512)
    tk_s = _tile(S, 1024)
    g_off = 2 * GW + 4 * HW

    h1 = _norm_mod("norm1", x, p["norm1_g"], sc1, sh1)
    z = _mm_nn_stacked("proj_in", h1, w.first("in", h1), tm=tm, tn=tn_in, tk=D)[0]
    ya = _gmlp_fwd(z, p["ln_g"], p["ln_b"], p["ws"], bsb, GW)
    yb, o_hg, states = _hg_fwd(z, p["hg_lb"], p["hg_ng"], HW)
    pa = _mm_nn_stacked("branch_gmlp", ya, w.first("bg", z), tm=tm, tn=tn_br, tk=GW)[0]

    def gates(ga_ref, gb_ref, ba_ref, bb_ref):
        return _sigmoid(ga_ref[...] + ba_ref[...]), _sigmoid(gb_ref[...] + bb_ref[...])

    def gate_specs(tn_):
        o1, o2 = g_off // tn_, (g_off + D) // tn_
        return [pl.BlockSpec((tm, tn_), lambda i, j, k: (i, o1 + j)), pl.BlockSpec((tm, tn_), lambda i, j, k: (i, o2 + j)),
                pl.BlockSpec((1, tn_), lambda i, j, k: (0, j)), pl.BlockSpec((1, tn_), lambda i, j, k: (0, D // tn_ + j))]

    def merge_ep(acc, ex, outs):
        ga, gb = gates(*ex[1:5])
        outs[0][...] = acc
        outs[1][...] = (ga * ex[0][...] + gb * acc).astype(BF16)

    tile_o = pl.BlockSpec((tm, tn_br), lambda i, j, k: (i, j))
    pb, y = _mm_nn_stacked(
        "branch_hg_merge", yb, w.first("bh", z), tm=tm, tn=tn_br, tk=HW, extras=[pa, z, z, p["b_gate"], p["b_gate"]],
        extra_specs=[tile_o, *gate_specs(tn_br)], out_shapes=[jax.ShapeDtypeStruct((S, D), F32), jax.ShapeDtypeStruct((S, D), BF16)],
        out_specs=[tile_o, tile_o], epilogue=merge_ep)

    def resid_ep(acc, ex, outs):
        outs[0][...] = acc
        outs[1][...] = ex[0][...] + ex[1][...] * acc

    def resid_mm(name, a, b, res, gt, tk):
        K = a.shape[1]
        t_o = pl.BlockSpec((tm, tn_d), lambda i, j, k: (i, j))
        return _matmul(
            name, a, b, dims=_NN, grid_mnk=(S // tm, D // tn_d, K // tk), tiles=(tm, tn_d),
            a_spec=pl.BlockSpec((tm, tk), lambda i, j, k: (i, k)), b_spec=pl.BlockSpec((tk, tn_d), lambda i, j, k: (k, j)),
            extras=[res, gt], extra_specs=[t_o, pl.BlockSpec((1, tn_d), lambda i, j, k: (0, j))],
            out_shapes=[jax.ShapeDtypeStruct((S, D), F32)] * 2, out_specs=[t_o, t_o], epilogue=resid_ep)

    o1, xm = resid_mm("proj_out", y, w.first("out", z), x, gt1, D)
    h2 = _norm_mod("norm2", xm, p["norm2_g"], sc2, sh2)
    ab = _mm_nn_stacked("ffn_in", h2, w.first("fi", y), tm=tm, tn=fi_loc, tk=D)[0]
    hf = _swiglu(ab, F, fi_loc)
    o2, x3 = resid_mm("ffn_out", hf, w.first("fo", ab), xm, gt2, _tile(F, 1536))
    dx3, do2, vec_l = _loss_head(x3, tgt, p["final_g"], o2, gt2)

    nf = F // fi_loc

    def dswiglu_ep(acc, ex, outs):
        a, up = ex[0][...], ex[1][...]
        sa = _sigmoid(a)
        outs[0][0] = (acc * up * (sa * (1.0 + a * (1.0 - sa)))).astype(BF16)
        outs[0][1] = (acc * (a * sa)).astype(BF16)

    dab = _matmul(
        "ffn_out_dx", do2, w["fo"], dims=_NT, grid_mnk=(S // tmh, nf, 1), tiles=(tmh, fi_loc),
        a_spec=pl.BlockSpec((tmh, D), lambda i, j, k: (i, 0)), b_spec=pl.BlockSpec((fi_loc, D), lambda i, j, k: (j, 0)),
        extras=[ab, ab], extra_specs=[pl.BlockSpec((tmh, fi_loc), lambda i, j, k: (i, j)), pl.BlockSpec((tmh, fi_loc), lambda i, j, k: (i, j + nf))],
        out_shapes=[jax.ShapeDtypeStruct((2, S, F), BF16)], out_specs=[pl.BlockSpec((2, tmh, fi_loc), lambda i, j, k: (0, i, j))],
        epilogue=dswiglu_ep)[0]
    start = (lambda name, grads: scatter[0](name, grads)) if scatter is not None else (lambda name, grads: None)
    push = (lambda name, after: scatter[1](name, after)) if scatter is not None else (lambda name, after: None)

    def zero(token):
        return 0.0 if token is None else token[0:1, 0:1]

    tm_f = _tile(F, 512)
    g_fo = _mm_tn("ffn_out_dw", hf, do2, pl.BlockSpec((tk_s, tn_d), lambda i, j, k: (k, j)), Mo=F, No=D, S=S, tm=tm_f, tn=tn_d, tk=tk_s)
    g_fi = _mm_tn("ffn_in_dw", h2, dab, pl.BlockSpec((None, tk_s, fi_loc), lambda i, j, k: (j // nf, k, j % nf)),
                  Mo=D, No=2 * F, S=S, tm=tn_d, tn=fi_loc, tk=tk_s, stacked_nloc=fi_loc, after=g_fo)
    t_ffn = start("scatter_ffn", dict(fo=g_fo, fi=g_fi))
    dh2 = _mm_nt_stacked("ffn_in_dx", pl.BlockSpec((None, tm, fi_loc), lambda i, j, k: (k // nf, i, k % nf)), dab, w["fi"],
                         M=S, tm=tm, tn=tn_d, tk=fi_loc, after=t_ffn)
    dxm, vec2, do1 = _norm_mod_bwd("norm2_bwd", dh2, xm, p["norm2_g"], sc2, dx3, o1, gt1)
    t_ffn = push("scatter_ffn", dxm)

    def dmerge_ep(acc, ex, outs):
        ga, gb = gates(*ex[2:6])
        outs[0][...] = (acc * ga).astype(BF16)
        outs[1][...] = (acc * gb).astype(BF16)
        outs[2][0] = (acc * ex[0][...] * ga * (1.0 - ga)).astype(BF16)
        outs[2][1] = (acc * ex[1][...] * gb * (1.0 - gb)).astype(BF16)

    t_o = pl.BlockSpec((tm, tn_d), lambda i, j, k: (i, j))
    dpa, dpb, dg2 = _matmul(
        "proj_out_dx", do1, w["out"], dims=_NT, grid_mnk=(S // tm, D // tn_d, 1), tiles=(tm, tn_d),
        a_spec=pl.BlockSpec((tm, D), lambda i, j, k: (i, 0)), b_spec=pl.BlockSpec((tn_d, D), lambda i, j, k: (j, 0)),
        extras=[pa, pb, z, z, p["b_gate"], p["b_gate"]], extra_specs=[t_o, t_o, *gate_specs(tn_d)],
        out_shapes=[jax.ShapeDtypeStruct((S, D), BF16), jax.ShapeDtypeStruct((S, D), BF16), jax.ShapeDtypeStruct((2, S, D), BF16)],
        out_specs=[t_o, t_o, pl.BlockSpec((2, tm, tn_d), lambda i, j, k: (0, i, j))], epilogue=dmerge_ep, after=t_ffn)
    g_out = _mm_tn("proj_out_dw", y, do1, pl.BlockSpec((tk_s, tn_d), lambda i, j, k: (k, j)), Mo=D, No=D, S=S, tm=tn_d, tn=tn_d, tk=tk_s)
    tn_g = _tile(GW, 512)
    b_br = pl.BlockSpec((tk_s, br_loc), lambda i, j, k: (k, j))
    g_bg = _mm_tn("branch_gmlp_dw", ya, dpa, b_br, Mo=GW, No=D, S=S, tm=tn_g, tn=br_loc, tk=tk_s, stacked_nloc=br_loc)
    g_bh = _mm_tn("branch_hg_dw", yb, dpb, b_br, Mo=HW, No=D, S=S, tm=tn_g, tn=br_loc, tk=tk_s, stacked_nloc=br_loc)
    t_mix = start("scatter_mixer", dict(out=g_out, bg=g_bg, bh=g_bh))
    a_br = pl.BlockSpec((tm, br_loc), lambda i, j, k: (i, k))
    dya = _mm_nt_stacked("branch_gmlp_dx", a_br, dpa, w["bg"], M=S, tm=tm, tn=tn_g, tk=br_loc, after=t_mix)
    dyb = _mm_nt_stacked("branch_hg_dx", a_br, dpb, w["bh"], M=S, tm=tm, tn=tn_g, tk=br_loc, after=t_mix)
    db_gate = _colsum2(dg2)
    dz_gmlp, dln, dws, dbs = _gmlp_bwd(z, dya, p["ln_g"], p["ln_b"], p["ws"], bsb, GW)
    t_mix = push("scatter_mixer", dz_gmlp)
    dz_hg, dng, dhlb = _hg_bwd(z, o_hg, states, dyb, p["hg_lb"], p["hg_ng"] + zero(t_mix), HW)
    dz = jnp.concatenate([dz_gmlp, dz_hg, dg2[0], dg2[1]], axis=1)
    g_in = _mm_tn("proj_in_dw", h1, dz, pl.BlockSpec((tk_s, tn_in), lambda i, j, k: (k, j)), Mo=D, No=INW, S=S,
                  tm=tn_d, tn=tn_in, tk=tk_s, stacked_nloc=in_loc)
    t_in = start("scatter_proj_in", dict(w_in=g_in))
    dh1 = _mm_nt_stacked("proj_in_dx", pl.BlockSpec((tm, tn_in), lambda i, j, k: (i, k)), dz, w["in"], M=S, tm=tm, tn=tn_d, tk=tn_in,
                         after=t_in)
    dx, vec1 = _norm_mod_bwd("norm1_bwd", dh1, x, p["norm1_g"], sc1, dxm)

    dmod = jnp.concatenate([vec1[0:1], vec1[1:2], vec2[3:4], vec2[0:1], vec2[1:2], vec_l[2:3]], axis=1)
    small = dict(norm1_g=vec1[2:3], b_gate=db_gate.reshape(1, 2 * D), ln_g=dln[0:1], ln_b=dln[1:2], ws=dws, bs=dbs.reshape(G, T),
                 hg_lb=dhlb, hg_ng=dng[0:1], norm2_g=vec2[2:3], final_g=vec_l[1:2], loss=vec_l[0:1, 0:LANES])
    big = dict(w_in=g_in, bg=g_bg, bh=g_bh, out=g_out, fi=g_fi, fo=g_fo)
    return dx, big, small, dmod


_SMALL = ("b_ada", "norm1_g", "b_gate", "ln_g", "ln_b", "ws", "bs", "hg_lb", "hg_ng", "norm2_g", "final_g")


def _pack(parts, rows_mult=8):
    flat = [a.reshape(-1) for a in parts]
    offs, n = [], 0
    for a in flat:
        offs.append(n)
        n += a.shape[0]
    pad = (-n) % (LANES * rows_mult)
    if pad:
        flat.append(jnp.zeros((pad,), F32))
    return jnp.concatenate(flat).reshape(-1, LANES), offs


def kernel(x, c, w_ada, b_ada, norm1_g, w_in, b_gate, gmlp_ln_g, gmlp_ln_b, gmlp_ws, gmlp_bs, hg_lb, hg_norm_g, w_branch_gmlp, w_branch_hg, w_out, norm2_g, w_ffn_in, w_ffn_out, final_norm_g, loss_target, m_w_ada, m_b_ada, m_norm1_g, m_w_in, m_b_gate, m_gmlp_ln_g, m_gmlp_ln_b, m_gmlp_ws, m_gmlp_bs, m_hg_lb, m_hg_norm_g, m_w_branch_gmlp, m_w_branch_hg, m_w_out, m_norm2_g, m_w_ffn_in, m_w_ffn_out, m_final_norm_g, v_w_ada, v_b_ada, v_norm1_g, v_w_in, v_b_gate, v_gmlp_ln_g, v_gmlp_ln_b, v_gmlp_ws, v_gmlp_bs, v_hg_lb, v_hg_norm_g, v_w_branch_gmlp, v_w_branch_hg, v_w_out, v_norm2_g, v_w_ffn_in, v_w_ffn_out, v_final_norm_g):
    S, D = x.shape[1], x.shape[2]
    ada_loc = w_ada.shape[2]
    me = 4 * lax.axis_index("x") + 2 * lax.axis_index("y") + lax.axis_index("c")

    c_all = _allgather_small("gather_c", c.reshape(D // LANES, LANES)).reshape(N_DEV, D)
    mod_cols, c_act = _ada_mod(jnp.pad(c_all, ((0, 16 - N_DEV), (0, 0))), w_ada[0])
    mod_all = _allgather_small("gather_mod", mod_cols[:N_DEV].reshape(-1, LANES)).reshape(N_DEV, N_DEV, ada_loc)
    mod = lax.dynamic_index_in_dim(mod_all, me, axis=1, keepdims=False).reshape(1, N_DEV * ada_loc) + b_ada

    def empty_hbm(shape, dtype):
        return pltpu.with_memory_space_constraint(lax.empty(shape, dtype), pltpu.HBM)

    groups = dict(gather_in=dict(keys=["in"], src=[w_in]), gather_mixer=dict(keys=["bg", "bh", "out"], src=[w_branch_gmlp, w_branch_hg, w_out]),
                  gather_ffn_in=dict(keys=["fi"], src=[w_ffn_in]), gather_ffn_out=dict(keys=["fo"], src=[w_ffn_out]))
    group_of = {}
    prev = mod_all
    for gname, g in groups.items():
        n = len(g["keys"])
        shards = [a[0].astype(BF16) for a in g["src"]]
        outs = [lax.dynamic_update_slice(lax.empty((N_DEV, *s.shape), BF16), s[None], (me, 0, 0)) for s in shards]
        *g["hop1"], prev = _split_start(gname + "_hop1", shards + outs, n * N_CHIP, _gather_first_copies(n), after=prev)
        for k in g["keys"]:
            group_of[k] = gname
    mod = mod + prev[0:1, 0:1]

    def fetch(key, after):
        g = groups[group_of[key]]
        n = len(g["keys"])
        if "done" not in g:
            send_sems, recv_sems, bufs = g["hop1"]
            send_sems, recv_sems, bufs = _split_relay(group_of[key] + "_relay", bufs, send_sems, recv_sems, after,
                                                      _gather_first_copies(n), n * (N_CHIP - 1), _gather_relay_copies(n))
            bufs = _split_wait(group_of[key] + "_hop2_wait", bufs, send_sems, recv_sems, after, _gather_relay_copies(n))
            g["done"] = dict(zip(g["keys"], bufs[n:]))
        arr = g["done"][key]
        return arr.reshape(-1, D) if key in ("out", "fo") else arr

    p = dict(norm1_g=norm1_g, b_gate=b_gate, ln_g=gmlp_ln_g, ln_b=gmlp_ln_b, ws=gmlp_ws[0], bs=gmlp_bs[0], hg_lb=hg_lb,
             hg_ng=hg_norm_g, norm2_g=norm2_g, final_g=final_norm_g.reshape(1, D))

    in_flight = {}
    c_idx = lax.axis_index("c").astype(jnp.int32).reshape(1)
    my_chip = 2 * lax.axis_index("x") + lax.axis_index("y")

    def scatter_start(name, grads):
        keys = list(grads)
        n = len(keys)
        stacks = [grads[k].reshape(N_DEV, -1, grads[k].shape[-1]) for k in keys]
        lands = [empty_hbm((N_CHIP, *g.shape[1:]), g.dtype) for g in stacks]
        send_sems, recv_sems, bufs, token = _split_start(name + "_d2d", stacks + lands, n * N_CHIP, _to_sibling_copies(n))
        in_flight[name] = dict(keys=keys, stage1=(send_sems, recv_sems, bufs))
        return token

    def scatter_push(name, after):
        f = in_flight[name]
        n = len(f["keys"])
        send_sems, recv_sems, bufs = f["stage1"]
        bufs = _split_wait(name + "_d2d_wait", bufs, send_sems, recv_sems, after, _to_sibling_copies(n))
        sums = [_chip_sum(f"{name}_sum_{k}", bufs[i], bufs[n + i], c_idx) for i, k in enumerate(f["keys"])]
        lands = [empty_hbm((N_CHIP - 1, *s.shape[1:]), s.dtype) for s in sums]
        send_sems, recv_sems, bufs, token = _split_start(name + "_ici", sums + lands, n * (N_CHIP - 1), _to_owner_copies(n))
        f["stage2"] = (send_sems, recv_sems, bufs)
        return token

    grad_x, _, small, dmod = _local_step(x[0], loss_target[0], mod, p, fetch, w_ffn_out.shape[1] * N_DEV, (scatter_start, scatter_push))

    small["b_ada"] = dmod
    packed, offs = _pack([small[k] for k in _SMALL] + [small["loss"]])
    gathered = _allgather_small("gather_small", packed)
    wp = dict(p, b_ada=b_ada)
    ms = dict(b_ada=m_b_ada, norm1_g=m_norm1_g, b_gate=m_b_gate, ln_g=m_gmlp_ln_g, ln_b=m_gmlp_ln_b, ws=m_gmlp_ws, bs=m_gmlp_bs,
              hg_lb=m_hg_lb, hg_ng=m_hg_norm_g, norm2_g=m_norm2_g, final_g=m_final_norm_g)
    vs = dict(b_ada=v_b_ada, norm1_g=v_norm1_g, b_gate=v_b_gate, ln_g=v_gmlp_ln_g, ln_b=v_gmlp_ln_b, ws=v_gmlp_ws, bs=v_gmlp_bs,
              hg_lb=v_hg_lb, hg_ng=v_hg_norm_g, norm2_g=v_norm2_g, final_g=v_final_norm_g)
    w_sm, _ = _pack([wp[k] for k in _SMALL])
    m_sm, _ = _pack([ms[k] for k in _SMALL])
    v_sm, _ = _pack([vs[k] for k in _SMALL])
    sm_out = _small_update(gathered, w_sm, m_sm, v_sm, scatter_push("scatter_proj_in", gathered))
    shapes = dict(b_ada=b_ada.shape, norm1_g=norm1_g.shape, b_gate=b_gate.shape, ln_g=gmlp_ln_g.shape, ln_b=gmlp_ln_b.shape,
                  ws=gmlp_ws.shape, bs=gmlp_bs.shape, hg_lb=hg_lb.shape, hg_ng=hg_norm_g.shape, norm2_g=norm2_g.shape,
                  final_g=final_norm_g.shape)

    def unpack(arr, k):
        i = _SMALL.index(k)
        n = math.prod(shapes[k])
        return arr.reshape(-1)[offs[i]:offs[i] + n].reshape(shapes[k])

    loss = sm_out[0].reshape(-1)[offs[len(_SMALL)]]

    dmod_all = gathered.reshape(N_DEV, -1)[:, offs[0]:offs[0] + N_DEV * ada_loc]
    dmod_loc = lax.dynamic_slice_in_dim(dmod_all, me * ada_loc, ada_loc, axis=1)
    ca_t = jnp.pad(c_act[:N_DEV].T, ((0, 0), (0, LANES - N_DEV))).astype(BF16)
    dm_p = jnp.pad(dmod_loc, ((0, LANES - N_DEV), (0, 0))).astype(BF16)
    tm_a = _tile(D, 512)
    g_ada = _matmul(
        "ada_dw", ca_t, dm_p, dims=_NN, grid_mnk=(D // tm_a, 1, 1), tiles=(tm_a, ada_loc),
        a_spec=pl.BlockSpec((tm_a, LANES), lambda i, j, k: (i, 0)), b_spec=pl.BlockSpec((LANES, ada_loc), lambda i, j, k: (0, 0)),
        out_shapes=[jax.ShapeDtypeStruct((1, D, ada_loc), F32)], out_specs=[pl.BlockSpec((None, tm_a, ada_loc), lambda i, j, k: (0, i, 0))],
        epilogue=_store(F32))[0]

    upd = {"w_ada": _adamw("adamw_w_ada", w_ada[0], m_w_ada[0], v_w_ada[0], [g_ada])}
    big_w = dict(w_in=(w_in, m_w_in, v_w_in, "w_in"), bg=(w_branch_gmlp, m_w_branch_gmlp, v_w_branch_gmlp, "w_branch_gmlp"),
                 bh=(w_branch_hg, m_w_branch_hg, v_w_branch_hg, "w_branch_hg"), out=(w_out, m_w_out, v_w_out, "w_out"),
                 fi=(w_ffn_in, m_w_ffn_in, v_w_ffn_in, "w_ffn_in"), fo=(w_ffn_out, m_w_ffn_out, v_w_ffn_out, "w_ffn_out"))
    after = upd["w_ada"][1]
    for name in ("scatter_ffn", "scatter_mixer", "scatter_proj_in"):
        keys = in_flight[name]["keys"]
        n = len(keys)
        send_sems, recv_sems, bufs = in_flight[name]["stage2"]
        bufs = _split_wait(name + "_ici_wait", bufs, send_sems, recv_sems, after, _to_owner_copies(n))
        for i, k in enumerate(keys):
            wt, mt, vt, out_name = big_w[k]
            own_k = lax.dynamic_index_in_dim(bufs[i], my_chip, axis=0, keepdims=True)
            upd[out_name] = _adamw("adamw_" + out_name, wt[0], mt[0], vt[0], [own_k, bufs[n + i]])
            after = upd[out_name][1]

    order = ("w_ada", "b_ada", "norm1_g", "w_in", "b_gate", "ln_g", "ln_b", "ws", "bs", "hg_lb", "hg_ng", "w_branch_gmlp", "w_branch_hg",
             "w_out", "norm2_g", "w_ffn_in", "w_ffn_out", "final_g")
    outs = [loss, grad_x[None]]
    for idx in range(4):
        for k in order:
            outs.append(upd[k][idx][None] if k in upd else unpack(sm_out[idx], k))
    return tuple(outs)
```

```python
import functools
import math

import jax
import jax.numpy as jnp
from jax import lax
from jax.experimental import pallas as pl
from jax.experimental.pallas import tpu as pltpu

F32 = jnp.float32
BF16 = jnp.bfloat16
N_DEV = 8
EPS = 1e-6
LANES = 128
HG_DK = 128
HG_CHUNK = 64
HG_MID = HG_CHUNK // 2 - 1
EXP_CLAMP = 80.0
VMEM_LIMIT = 48 * 1024 * 1024
ADAM_LR, ADAM_B1, ADAM_B2, ADAM_EPS, ADAM_WD, ADAM_STEP = 0.001, 0.9, 0.999, 1e-08, 0.01, 10
MESH = pl.DeviceIdType.MESH

_NN = (((1,), (0,)), ((), ()))
_NT = (((1,), (1,)), ((), ()))
_TN = (((0,), (0,)), ((), ()))


def _dot(a, b, dims=_NN):
    return lax.dot_general(a.astype(BF16), b.astype(BF16), dims, preferred_element_type=F32)


def _tile(n, target, mult=LANES):
    best = None
    for t in range(mult, min(n, target) + 1, mult):
        if n % t == 0:
            best = t
    return n if best is None else best


def _cparams(sem):
    return pltpu.CompilerParams(dimension_semantics=sem, vmem_limit_bytes=VMEM_LIMIT)


def _sigmoid(x):
    return 1.0 / (1.0 + jnp.exp(-x))


def _gelu_parts(x):
    k0 = math.sqrt(2.0 / math.pi)
    x2 = x * x
    t = jnp.tanh(k0 * (x + 0.044715 * x * x2))
    g = 0.5 * x * (1.0 + t)
    dg = 0.5 * (1.0 + t) + 0.5 * x * (1.0 - t * t) * (k0 * (1.0 + 3.0 * 0.044715 * x2))
    return g, dg


def _split3(x):
    h = x.astype(BF16)
    r = x - h.astype(F32)
    m = r.astype(BF16)
    lo = (r - m.astype(F32)).astype(BF16)
    return h, m, lo


def _ones_dot(mat01, x):
    h, m, lo = _split3(x)
    d = functools.partial(lax.dot_general, dimension_numbers=_NN, preferred_element_type=F32)
    return d(mat01, h) + d(mat01, m) + d(mat01, lo)


def _matmul(name, a, b, *, dims, grid_mnk, tiles, a_spec, b_spec, extras=(), extra_specs=(), out_shapes, out_specs, epilogue, after=None):
    gm, gn, nk = grid_mnk
    tm, tn = tiles
    n_ex, n_out = len(extras), len(out_shapes)
    held = [] if after is None else [after]

    def body(*refs):
        a_ref, b_ref = refs[0], refs[1]
        ex = refs[2:2 + n_ex]
        outs = refs[2 + n_ex + len(held):2 + n_ex + len(held) + n_out]
        if nk == 1:
            epilogue(lax.dot_general(a_ref[...], b_ref[...], dims, preferred_element_type=F32), ex, outs)
            return
        acc = refs[-1]
        k = pl.program_id(2)

        @pl.when(k == 0)
        def _():
            acc[...] = jnp.zeros_like(acc)

        acc[...] += lax.dot_general(a_ref[...], b_ref[...], dims, preferred_element_type=F32)

        @pl.when(k == nk - 1)
        def _():
            epilogue(acc[...], ex, outs)

    return pl.pallas_call(
        body, name=name, grid=(gm, gn, nk), in_specs=[a_spec, b_spec, *extra_specs] + [pl.BlockSpec(memory_space=pl.ANY)] * len(held),
        out_specs=list(out_specs), out_shape=list(out_shapes), scratch_shapes=[] if nk == 1 else [pltpu.VMEM((tm, tn), F32)],
        compiler_params=_cparams(("parallel", "parallel", "arbitrary")),
    )(a, b, *extras, *held)


def _store(dtype):
    def ep(acc, ex, outs):
        outs[0][...] = acc.astype(dtype)
    return ep


def _mm_nn_stacked(name, a, wg, *, tm, tn, tk, out_dtype=F32, extras=(), extra_specs=(), out_shapes=None, out_specs=None, epilogue=None):
    M, K = a.shape
    _, _, nloc = wg.shape
    N = nloc * N_DEV
    q = nloc // tn
    if out_shapes is None:
        out_shapes = [jax.ShapeDtypeStruct((M, N), out_dtype)]
        out_specs = [pl.BlockSpec((tm, tn), lambda i, j, k: (i, j))]
        epilogue = _store(out_dtype)
    return _matmul(
        name, a, wg, dims=_NN, grid_mnk=(M // tm, N // tn, K // tk), tiles=(tm, tn),
        a_spec=pl.BlockSpec((tm, tk), lambda i, j, k: (i, k)),
        b_spec=pl.BlockSpec((None, tk, tn), lambda i, j, k: (j // q, k, j % q)),
        extras=extras, extra_specs=extra_specs, out_shapes=out_shapes, out_specs=out_specs, epilogue=epilogue)


def _mm_nt_stacked(name, a_spec, a, wg, *, M, tm, tn, tk, out_dtype=F32, after=None):
    _, Kw, nloc = wg.shape
    q = nloc // tk
    return _matmul(
        name, a, wg, dims=_NT, grid_mnk=(M // tm, Kw // tn, (nloc * N_DEV) // tk), tiles=(tm, tn),
        a_spec=a_spec, b_spec=pl.BlockSpec((None, tn, tk), lambda i, j, k: (k // q, j, k % q)),
        out_shapes=[jax.ShapeDtypeStruct((M, Kw), out_dtype)], out_specs=[pl.BlockSpec((tm, tn), lambda i, j, k: (i, j))],
        epilogue=_store(out_dtype), after=after)[0]


def _mm_tn(name, a, b, b_spec, *, Mo, No, S, tm, tn, tk, stacked_nloc=None, after=None):
    if stacked_nloc is None:
        out_shape = jax.ShapeDtypeStruct((Mo, No), BF16)
        out_spec = pl.BlockSpec((tm, tn), lambda i, j, k: (i, j))
    else:
        q = stacked_nloc // tn
        out_shape = jax.ShapeDtypeStruct((N_DEV, Mo, stacked_nloc), BF16)
        out_spec = pl.BlockSpec((None, tm, tn), lambda i, j, k: (j // q, i, j % q))
    return _matmul(
        name, a, b, dims=_TN, grid_mnk=(Mo // tm, No // tn, S // tk), tiles=(tm, tn),
        a_spec=pl.BlockSpec((tk, tm), lambda i, j, k: (k, i)), b_spec=b_spec,
        out_shapes=[out_shape], out_specs=[out_spec], epilogue=_store(BF16), after=after)[0]


def _norm_mod(name, x, g, sc, sh):
    S, D = x.shape
    tm = _tile(S, 256, 8)

    def body(x_ref, g_ref, sc_ref, sh_ref, h_ref):
        xv = x_ref[...]
        r = lax.rsqrt(jnp.mean(xv * xv, axis=-1, keepdims=True) + EPS)
        h = (xv * r) * g_ref[...]
        h_ref[...] = (h * (1.0 + sc_ref[...]) + sh_ref[...]).astype(BF16)

    row = pl.BlockSpec((tm, D), lambda i: (i, 0))
    vec = pl.BlockSpec((1, D), lambda i: (0, 0))
    return pl.pallas_call(body, name=name, grid=(S // tm,), in_specs=[row, vec, vec, vec], out_specs=row,
                          out_shape=jax.ShapeDtypeStruct((S, D), BF16), compiler_params=_cparams(("parallel",)))(x, g, sc, sh)


def _norm_mod_bwd(name, dh, x, g, sc, dres, o=None, gt=None):
    S, D = x.shape
    tm = _tile(S, 256, 8)
    gated = o is not None

    def body(*refs):
        if gated:
            dh_ref, x_ref, g_ref, sc_ref, dres_ref, o_ref, gt_ref, dx_ref, vec_ref, do_ref = refs
        else:
            dh_ref, x_ref, g_ref, sc_ref, dres_ref, dx_ref, vec_ref = refs
        i = pl.program_id(0)

        @pl.when(i == 0)
        def _():
            vec_ref[...] = jnp.zeros_like(vec_ref)

        xv, dh_v, gv = x_ref[...], dh_ref[...], g_ref[...]
        r = lax.rsqrt(jnp.mean(xv * xv, axis=-1, keepdims=True) + EPS)
        xn = xv * r
        one_sc = 1.0 + sc_ref[...]
        vec_ref[0:1, :] += jnp.sum(dh_v, axis=0, keepdims=True)
        vec_ref[1:2, :] += jnp.sum(dh_v * (xn * gv), axis=0, keepdims=True)
        vec_ref[2:3, :] += jnp.sum(dh_v * one_sc * xn, axis=0, keepdims=True)
        dxn = dh_v * one_sc * gv
        dx = dres_ref[...] + r * (dxn - xn * jnp.mean(dxn * xn, axis=-1, keepdims=True))
        dx_ref[...] = dx
        if gated:
            vec_ref[3:4, :] += jnp.sum(dx * o_ref[...], axis=0, keepdims=True)
            do_ref[...] = (dx * gt_ref[...]).astype(BF16)

    row = pl.BlockSpec((tm, D), lambda i: (i, 0))
    vec = pl.BlockSpec((1, D), lambda i: (0, 0))
    acc = pl.BlockSpec((8, D), lambda i: (0, 0))
    ins = [dh, x, g, sc, dres] + ([o, gt] if gated else [])
    in_specs = [row, row, vec, vec, row] + ([row, vec] if gated else [])
    out_shape = [jax.ShapeDtypeStruct((S, D), F32), jax.ShapeDtypeStruct((8, D), F32)]
    out_specs = [row, acc]
    if gated:
        out_shape.append(jax.ShapeDtypeStruct((S, D), BF16))
        out_specs.append(row)
    return pl.pallas_call(body, name=name, grid=(S // tm,), in_specs=in_specs, out_specs=out_specs, out_shape=out_shape,
                          compiler_params=_cparams(("arbitrary",)))(*ins)


def _loss_head(x3, tgt, gf, o2, gt2):
    S, D = x3.shape
    tm = _tile(S, 256, 8)

    def body(x_ref, t_ref, g_ref, o_ref, gt_ref, dx_ref, do_ref, vec_ref):
        i = pl.program_id(0)

        @pl.when(i == 0)
        def _():
            vec_ref[...] = jnp.zeros_like(vec_ref)

        xv, gv = x_ref[...], g_ref[...]
        r = lax.rsqrt(jnp.mean(xv * xv, axis=-1, keepdims=True) + EPS)
        xn = xv * r
        e = xn * gv - t_ref[...]
        tok = 0.5 * jnp.mean(e * e, axis=-1, keepdims=True)
        vec_ref[0:1, :] += jnp.broadcast_to(jnp.sum(tok, axis=0, keepdims=True), (1, D))
        dy = e * (1.0 / D)
        vec_ref[1:2, :] += jnp.sum(dy * xn, axis=0, keepdims=True)
        dxn = dy * gv
        dx = r * (dxn - xn * jnp.mean(dxn * xn, axis=-1, keepdims=True))
        dx_ref[...] = dx
        vec_ref[2:3, :] += jnp.sum(dx * o_ref[...], axis=0, keepdims=True)
        do_ref[...] = (dx * gt_ref[...]).astype(BF16)

    row = pl.BlockSpec((tm, D), lambda i: (i, 0))
    vec = pl.BlockSpec((1, D), lambda i: (0, 0))
    return pl.pallas_call(
        body, name="loss_head", grid=(S // tm,), in_specs=[row, row, vec, row, vec],
        out_specs=[row, row, pl.BlockSpec((8, D), lambda i: (0, 0))],
        out_shape=[jax.ShapeDtypeStruct((S, D), F32), jax.ShapeDtypeStruct((S, D), BF16), jax.ShapeDtypeStruct((8, D), F32)],
        compiler_params=_cparams(("arbitrary",)))(x3, tgt, gf, o2, gt2)


def _swiglu(ab, F, tf):
    S = ab.shape[0]
    tm = _tile(S, 512, 8)
    nf = F // tf

    def body(a_ref, u_ref, h_ref):
        a = a_ref[...]
        h_ref[...] = (a * _sigmoid(a) * u_ref[...]).astype(BF16)

    return pl.pallas_call(
        body, name="swiglu", grid=(S // tm, nf),
        in_specs=[pl.BlockSpec((tm, tf), lambda i, j: (i, j)), pl.BlockSpec((tm, tf), lambda i, j: (i, j + nf))],
        out_specs=pl.BlockSpec((tm, tf), lambda i, j: (i, j)), out_shape=jax.ShapeDtypeStruct((S, F), BF16),
        compiler_params=_cparams(("parallel", "parallel")))(ab, ab)


def _colsum2(dg2):
    _, S, D = dg2.shape
    tm = _tile(S, 256, 16)

    def body(x_ref, o_ref):
        @pl.when(pl.program_id(0) == 0)
        def _():
            o_ref[...] = jnp.zeros_like(o_ref)

        o_ref[0:1, :] += jnp.sum(x_ref[0].astype(F32), axis=0, keepdims=True)
        o_ref[1:2, :] += jnp.sum(x_ref[1].astype(F32), axis=0, keepdims=True)

    return pl.pallas_call(body, name="gate_bias_grad", grid=(S // tm,), in_specs=[pl.BlockSpec((2, tm, D), lambda i: (0, i, 0))],
                          out_specs=pl.BlockSpec((2, D), lambda i: (0, 0)), out_shape=jax.ShapeDtypeStruct((2, D), F32),
                          compiler_params=_cparams(("arbitrary",)))(dg2)


def _gmlp_common(u_ref, v_ref, lg_ref, lb_ref, ws_ref, bsb_ref, G, T, Dg):
    ug, dug = _gelu_parts(u_ref[...])
    vg, dvg = _gelu_parts(v_ref[...])
    mu = jnp.mean(vg, axis=-1, keepdims=True)
    vc = vg - mu
    rstd = lax.rsqrt(jnp.mean(vc * vc, axis=-1, keepdims=True) + EPS)
    vhat = vc * rstd
    vn = vhat * lg_ref[...] + lb_ref[...]
    row = lax.broadcasted_iota(jnp.int32, (T, T), 0)
    col = lax.broadcasted_iota(jnp.int32, (T, T), 1)
    tril = row >= col
    s = []
    for g in range(G):
        w = jnp.where(tril, ws_ref[g], 0.0)
        s.append(_dot(w, vn[:, g * Dg:(g + 1) * Dg]) + bsb_ref[g])
    return ug, dug, dvg, rstd, vhat, vn, tril, s


def _gmlp_fwd(z, ln_g, ln_b, ws, bsb, GW):
    S = z.shape[0]
    G, T, _ = ws.shape
    Dg = GW // G

    def body(u_ref, v_ref, lg_ref, lb_ref, ws_ref, bsb_ref, ya_ref):
        ug, _, _, _, _, _, _, s = _gmlp_common(u_ref, v_ref, lg_ref, lb_ref, ws_ref, bsb_ref, G, T, Dg)
        for g in range(G):
            sl = slice(g * Dg, (g + 1) * Dg)
            ya_ref[:, sl] = (ug[:, sl] * s[g]).astype(BF16)

    vec = pl.BlockSpec((1, GW), lambda c: (0, 0))
    return pl.pallas_call(
        body, name="gmlp_fwd", grid=(S // T,),
        in_specs=[pl.BlockSpec((T, GW), lambda c: (c, 0)), pl.BlockSpec((T, GW), lambda c: (c, 1)), vec, vec,
                  pl.BlockSpec((G, T, T), lambda c: (0, 0, 0)), pl.BlockSpec((G, T, Dg), lambda c: (0, 0, 0))],
        out_specs=pl.BlockSpec((T, GW), lambda c: (c, 0)), out_shape=jax.ShapeDtypeStruct((S, GW), BF16),
        compiler_params=_cparams(("parallel",)))(z, z, ln_g, ln_b, ws, bsb)


def _gmlp_bwd(z, dya, ln_g, ln_b, ws, bsb, GW):
    S = z.shape[0]
    G, T, _ = ws.shape
    Dg = GW // G
    nc = S // T

    def body(u_ref, v_ref, dya_ref, lg_ref, lb_ref, ws_ref, bsb_ref, dz_ref, dln_ref, dws_ref, dbs_ref, dbs_acc, dvh):
        c = pl.program_id(0)

        @pl.when(c == 0)
        def _():
            dln_ref[...] = jnp.zeros_like(dln_ref)
            dws_ref[...] = jnp.zeros_like(dws_ref)
            dbs_acc[...] = jnp.zeros_like(dbs_acc)

        ug, dug, dvg, rstd, vhat, vn, tril, s = _gmlp_common(u_ref, v_ref, lg_ref, lb_ref, ws_ref, bsb_ref, G, T, Dg)
        dya_v = dya_ref[...]
        for g in range(G):
            sl = slice(g * Dg, (g + 1) * Dg)
            dy_g = dya_v[:, sl]
            dz_ref[:, sl] = (dy_g * s[g] * dug[:, sl]).astype(BF16)
            ds = dy_g * ug[:, sl]
            dbs_acc[g] += ds
            w = jnp.where(tril, ws_ref[g], 0.0)
            dvn_g = _dot(w, ds, _TN)
            dws_ref[g] += jnp.where(tril, _dot(ds, vn[:, sl], _NT), 0.0)
            dln_ref[0:1, sl] += jnp.sum(dvn_g * vhat[:, sl], axis=0, keepdims=True)
            dln_ref[1:2, sl] += jnp.sum(dvn_g, axis=0, keepdims=True)
            dvh[:, sl] = dvn_g * lg_ref[:, sl]
        dvhat = dvh[...]
        m1 = jnp.mean(dvhat, axis=-1, keepdims=True)
        m2 = jnp.mean(dvhat * vhat, axis=-1, keepdims=True)
        dz_ref[:, GW:2 * GW] = (rstd * (dvhat - m1 - vhat * m2) * dvg).astype(BF16)

        @pl.when(c == nc - 1)
        def _():
            for g in range(G):
                dbs_ref[g] = jnp.sum(dbs_acc[g], axis=-1, keepdims=True)

    vec = pl.BlockSpec((1, GW), lambda c: (0, 0))
    return pl.pallas_call(
        body, name="gmlp_bwd", grid=(nc,),
        in_specs=[pl.BlockSpec((T, GW), lambda c: (c, 0)), pl.BlockSpec((T, GW), lambda c: (c, 1)),
                  pl.BlockSpec((T, GW), lambda c: (c, 0)), vec, vec,
                  pl.BlockSpec((G, T, T), lambda c: (0, 0, 0)), pl.BlockSpec((G, T, Dg), lambda c: (0, 0, 0))],
        out_specs=[pl.BlockSpec((T, 2 * GW), lambda c: (c, 0)), pl.BlockSpec((8, GW), lambda c: (0, 0)),
                   pl.BlockSpec((G, T, T), lambda c: (0, 0, 0)), pl.BlockSpec((G, T, 1), lambda c: (0, 0, 0))],
        out_shape=[jax.ShapeDtypeStruct((S, 2 * GW), BF16), jax.ShapeDtypeStruct((8, GW), F32),
                   jax.ShapeDtypeStruct((G, T, T), F32), jax.ShapeDtypeStruct((G, T, 1), F32)],
        scratch_shapes=[pltpu.VMEM((G, T, Dg), F32), pltpu.VMEM((T, GW), F32)],
        compiler_params=_cparams(("arbitrary",)))(z, z, dya, ln_g, ln_b, ws, bsb)


def _hg_common(q_ref, f_ref, hlb_ref):
    C = HG_CHUNK
    a = hlb_ref[...]
    lb = _sigmoid(a[0:1, :] - a[1:2, :])
    sig = _sigmoid(f_ref[...])
    f = lb + (1.0 - lb) * sig
    lf = jnp.log(f)
    kk = 1.0 - f
    q = q_ref[...]
    sq = _sigmoid(q)
    qa = q * sq
    row = lax.broadcasted_iota(jnp.int32, (C, C), 0)
    col = lax.broadcasted_iota(jnp.int32, (C, C), 1)
    tril = row >= col
    b = _ones_dot(tril.astype(BF16), lf)
    bm = b[HG_MID:HG_MID + 1, :]
    bl = b[C - 1:C, :]
    e_b = jnp.exp(b)
    e_qm = jnp.exp(jnp.minimum(b - bm, EXP_CLAMP))
    e_km = jnp.exp(jnp.minimum(bm - b, EXP_CLAMP))
    e_kl = jnp.exp(bl - b)
    return dict(lb=lb, sig=sig, f=f, kk=kk, q=q, sq=sq, qa=qa, tril=tril, e_b=e_b, e_qm=e_qm, e_km=e_km, e_kl=e_kl,
                e_l=jnp.exp(bl), qh=qa * e_b, qt=qa * e_qm, kt=kk * e_km, kh=kk * e_kl)


def _hg_fwd(z, hg_lb, ng, HW):
    S = z.shape[0]
    C, H, dk = HG_CHUNK, HW // HG_DK, HG_DK
    nc = S // C

    def body(q_ref, f_ref, i_ref, og_ref, hlb_ref, ng_ref, yb_ref, o_ref, st_ref, state):
        @pl.when(pl.program_id(0) == 0)
        def _():
            state[...] = jnp.zeros_like(state)

        t = _hg_common(q_ref, f_ref, hlb_ref)
        iv = i_ref[...]
        for h in range(H):
            sl = slice(h * dk, (h + 1) * dk)
            st = state[h]
            st_ref[h] = st
            a = jnp.where(t["tril"], _dot(t["qt"][:, sl], t["kt"][:, sl], _NT), 0.0)
            o_h = _dot(a, iv[:, sl]) + _dot(t["qh"][:, sl], st, _NT)
            state[h] = st * t["e_l"][:, sl] + _dot(iv[:, sl], t["kh"][:, sl], _TN)
            o_ref[:, sl] = o_h
            rr = lax.rsqrt(jnp.mean(o_h * o_h, axis=-1, keepdims=True) + EPS)
            og = og_ref[:, sl]
            yb_ref[:, sl] = (o_h * rr * ng_ref[:, sl] * (og * _sigmoid(og))).astype(BF16)

    def col(k):
        return pl.BlockSpec((C, HW), lambda c: (c, k))

    base = 2
    return pl.pallas_call(
        body, name="hgrn_fwd", grid=(nc,),
        in_specs=[col(base), col(base + 1), col(base + 2), col(base + 3),
                  pl.BlockSpec((2, HW), lambda c: (0, 0)), pl.BlockSpec((1, HW), lambda c: (0, 0))],
        out_specs=[pl.BlockSpec((C, HW), lambda c: (c, 0)), pl.BlockSpec((C, HW), lambda c: (c, 0)),
                   pl.BlockSpec((None, H, dk, dk), lambda c: (c, 0, 0, 0))],
        out_shape=[jax.ShapeDtypeStruct((S, HW), BF16), jax.ShapeDtypeStruct((S, HW), F32),
                   jax.ShapeDtypeStruct((nc, H, dk, dk), F32)],
        scratch_shapes=[pltpu.VMEM((H, dk, dk), F32)],
        compiler_params=_cparams(("arbitrary",)))(z, z, z, z, hg_lb, ng)


def _hg_bwd(z, o, states, dyb, hg_lb, ng, HW):
    S = z.shape[0]
    C, H, dk = HG_CHUNK, HW // HG_DK, HG_DK
    nc = S // C

    def body(q_ref, f_ref, i_ref, og_ref, o_ref, st_ref, stn_ref, dyb_ref, hlb_ref, ng_ref, dz_ref, dng_ref, dhlb_ref,
             dstate, cross, dqa_buf, dkk_buf, db_buf, dlb_acc):
        c = pl.program_id(0)

        @pl.when(c == 0)
        def _():
            dstate[...] = jnp.zeros_like(dstate)
            dlb_acc[...] = jnp.zeros_like(dlb_acc)
            dng_ref[...] = jnp.zeros_like(dng_ref)

        def r16(v):
            return v.astype(BF16).astype(F32)

        t = _hg_common(q_ref, f_ref, hlb_ref)
        iv = i_ref[...]
        for h in range(H):
            sl = slice(h * dk, (h + 1) * dk)
            o_h, og, dyb_h, ng_h = o_ref[:, sl], og_ref[:, sl], dyb_ref[:, sl], ng_ref[:, sl]
            sg = _sigmoid(og)
            silu_og = og * sg
            rr = lax.rsqrt(jnp.mean(o_h * o_h, axis=-1, keepdims=True) + EPS)
            on = o_h * rr
            dng_ref[0:1, sl] += jnp.sum(dyb_h * on * silu_og, axis=0, keepdims=True)
            dz_ref[:, 3 * HW + h * dk:3 * HW + (h + 1) * dk] = (dyb_h * on * ng_h * (sg * (1.0 + og * (1.0 - sg)))).astype(BF16)
            don = dyb_h * ng_h * silu_og
            do_h = rr * (don - on * jnp.mean(don * on, axis=-1, keepdims=True))

            qt, kt, qh, kh, iv_h = t["qt"][:, sl], t["kt"][:, sl], t["qh"][:, sl], t["kh"][:, sl], iv[:, sl]
            a = jnp.where(t["tril"], _dot(qt, kt, _NT), 0.0)
            da = jnp.where(t["tril"], _dot(do_h, iv_h, _NT), 0.0)
            st, dst = st_ref[h], dstate[h]
            cross[:, sl] = jnp.sum(dst * stn_ref[h], axis=0, keepdims=True)
            dqh = _dot(do_h, st)
            dstate[h] = _dot(do_h, qh, _TN) + dst * t["e_l"][:, sl]
            div = _dot(a, do_h, _TN) + _dot(kh, dst, _NT)
            dkh = _dot(iv_h, dst)
            dqt = _dot(da, kt)
            dkt = _dot(da, qt, _TN)
            dz_ref[:, 2 * HW + h * dk:2 * HW + (h + 1) * dk] = div.astype(BF16)
            dqa_buf[:, sl] = dqh * t["e_b"][:, sl] + dqt * t["e_qm"][:, sl]
            dkk_buf[:, sl] = dkt * t["e_km"][:, sl] + dkh * t["e_kl"][:, sl]
            db_buf[:, sl] = r16(qt) * dqt - r16(kt) * dkt + r16(qh) * dqh - r16(kh) * dkh

        dqa, dkk = dqa_buf[...], dkk_buf[...]
        triu = jnp.logical_not(t["tril"]) | (lax.broadcasted_iota(jnp.int32, (C, C), 0) == lax.broadcasted_iota(jnp.int32, (C, C), 1))
        dlf = _ones_dot(triu.astype(BF16), db_buf[...]) + cross[...]
        df = dlf / t["f"] - dkk
        sig, lb = t["sig"], t["lb"]
        dz_ref[:, HW:2 * HW] = (df * (1.0 - lb) * sig * (1.0 - sig)).astype(BF16)
        dlb_acc[...] += jnp.sum(df * (1.0 - sig), axis=0, keepdims=True)
        q, sq = t["q"], t["sq"]
        dz_ref[:, 0:HW] = (dqa * (sq * (1.0 + q * (1.0 - sq)))).astype(BF16)

        @pl.when(c == nc - 1)
        def _():
            da0 = dlb_acc[...] * lb * (1.0 - lb)
            dhlb_ref[0:1, :] = da0
            dhlb_ref[1:2, :] = -da0

    def col(k):
        return pl.BlockSpec((C, HW), lambda c: (nc - 1 - c, k))

    base = 2
    return pl.pallas_call(
        body, name="hgrn_bwd", grid=(nc,),
        in_specs=[col(base), col(base + 1), col(base + 2), col(base + 3), col(0),
                  pl.BlockSpec((None, H, dk, dk), lambda c: (nc - 1 - c, 0, 0, 0)),
                  pl.BlockSpec((None, H, dk, dk), lambda c: (jnp.minimum(nc - c, nc - 1), 0, 0, 0)), col(0),
                  pl.BlockSpec((2, HW), lambda c: (0, 0)), pl.BlockSpec((1, HW), lambda c: (0, 0))],
        out_specs=[pl.BlockSpec((C, 4 * HW), lambda c: (nc - 1 - c, 0)), pl.BlockSpec((8, HW), lambda c: (0, 0)),
                   pl.BlockSpec((2, HW), lambda c: (0, 0))],
        out_shape=[jax.ShapeDtypeStruct((S, 4 * HW), BF16), jax.ShapeDtypeStruct((8, HW), F32), jax.ShapeDtypeStruct((2, HW), F32)],
        scratch_shapes=[pltpu.VMEM((H, dk, dk), F32), pltpu.VMEM((1, HW), F32), pltpu.VMEM((C, HW), F32), pltpu.VMEM((C, HW), F32),
                        pltpu.VMEM((C, HW), F32), pltpu.VMEM((1, HW), F32)],
        compiler_params=_cparams(("arbitrary",)))(z, z, z, z, o, states, states, dyb, hg_lb, ng)


def _position():
    x, y, c = lax.axis_index("x"), lax.axis_index("y"), lax.axis_index("c")
    return x, y, c, 4 * x + 2 * y + c


def _flip(x, y, c, k):
    return (1 - x if k & 4 else x, 1 - y if k & 2 else y, 1 - c if k & 1 else c)


def _allgather_small(name, v):
    R, L = v.shape

    def body(v_ref, out_ref, send_sems, recv_sems):
        x, y, c, me = _position()
        out_ref[me] = v_ref[...]
        copies = []
        for k in range(1, N_DEV):
            cp = pltpu.make_async_remote_copy(src_ref=v_ref, dst_ref=out_ref.at[me], send_sem=send_sems.at[k - 1],
                                              recv_sem=recv_sems.at[k - 1], device_id=_flip(x, y, c, k), device_id_type=MESH)
            cp.start()
            copies.append(cp)
        for cp in copies:
            cp.wait()

    return pl.pallas_call(
        body, name=name, out_shape=jax.ShapeDtypeStruct((N_DEV, R, L), v.dtype),
        in_specs=[pl.BlockSpec(memory_space=pltpu.VMEM)], out_specs=pl.BlockSpec(memory_space=pltpu.VMEM),
        scratch_shapes=[pltpu.SemaphoreType.DMA((N_DEV - 1,)), pltpu.SemaphoreType.DMA((N_DEV - 1,))],
        compiler_params=pltpu.CompilerParams(vmem_limit_bytes=VMEM_LIMIT),
    )(v)


def _allgather_hbm(name, shards):
    n = len(shards)

    def body(*refs):
        ins, outs = refs[:n], refs[n:2 * n]
        send_sems, recv_sems, local_sems = refs[2 * n:]
        x, y, c, me = _position()
        sibling = (x, y, 1 - c)
        chips = [(1 - x, y), (x, 1 - y), (1 - x, 1 - y)]

        def slot(px, py, pc):
            return 4 * px + 2 * py + pc

        def copy(w, k, block, to, src=None):
            dst = outs[w].at[slot(*block)]
            return pltpu.make_async_remote_copy(src_ref=dst if src is None else src, dst_ref=dst, send_sem=send_sems.at[w, k],
                                                recv_sem=recv_sems.at[w, k], device_id=to, device_id_type=MESH)

        mine, first, passed = [], [], []
        for w in range(n):
            cp = pltpu.make_async_copy(ins[w], outs[w].at[me], local_sems.at[w])
            cp.start()
            mine.append(cp)
            for j, chip in enumerate(chips):
                first.append(copy(w, 1 + j, (x, y, c), (*chip, c), src=ins[w]))
            first.append(copy(w, 0, (x, y, c), sibling, src=ins[w]))
        for cp in first:
            cp.start()
        for w in range(n):
            for j, chip in enumerate(chips):
                copy(w, 1 + j, (*chip, c), (x, y, c)).wait_recv()
                cp = copy(w, 4 + j, (*chip, c), sibling)
                cp.start()
                passed.append(cp)
        for w in range(n):
            copy(w, 0, sibling, (x, y, c)).wait_recv()
            for j, chip in enumerate(chips):
                copy(w, 4 + j, (*chip, 1 - c), (x, y, c)).wait_recv()
        for cp in first + passed:
            cp.wait_send()
        for cp in mine:
            cp.wait()

    hbm = pl.BlockSpec(memory_space=pltpu.HBM)
    return pl.pallas_call(
        body, name=name, out_shape=[jax.ShapeDtypeStruct((N_DEV, *s.shape), s.dtype) for s in shards],
        in_specs=[hbm] * n, out_specs=[hbm] * n,
        scratch_shapes=[pltpu.SemaphoreType.DMA((n, 7)), pltpu.SemaphoreType.DMA((n, 7)), pltpu.SemaphoreType.DMA((n,))],
    )(*shards)


_HBM = pl.BlockSpec(memory_space=pltpu.HBM)
_SEM = pl.BlockSpec(memory_space=pltpu.SEMAPHORE)
_EFFECT = pltpu.SideEffectType.DATAFLOW_SIDE_EFFECTING


def _split_start(name, bufs, n_sems, copies_fn, after=None):
    nb = len(bufs)
    extra = [] if after is None else [after]
    k = nb + len(extra)

    def body(*refs):
        for cp in copies_fn(refs[:nb], refs[k], refs[k + 1]):
            cp.start()
        refs[-1][...] = jnp.zeros_like(refs[-1])

    sems = pltpu.SemaphoreType.DMA((n_sems,))
    res = pl.pallas_call(
        body, name=name,
        out_shape=(sems, sems, *[pltpu.HBM(a.shape, a.dtype) for a in bufs], jax.ShapeDtypeStruct((8, LANES), F32)),
        in_specs=[_HBM] * nb + [pl.BlockSpec(memory_space=pl.ANY)] * len(extra),
        out_specs=(_SEM, _SEM, *[_HBM] * nb, pl.BlockSpec(memory_space=pltpu.VMEM)),
        input_output_aliases={i: 2 + i for i in range(nb)},
        compiler_params=pltpu.CompilerParams(has_side_effects=_EFFECT),
    )(*[pltpu.with_memory_space_constraint(a, pltpu.HBM) for a in bufs], *extra)
    return res[0], res[1], list(res[2:2 + nb]), res[-1]


def _split_wait(name, bufs, send_sems, recv_sems, after, copies_fn):
    nb = len(bufs)

    def body(*refs):
        for cp in copies_fn(refs[:nb], refs[nb], refs[nb + 1]):
            cp.wait_send()
            cp.wait_recv()

    res = pl.pallas_call(
        body, name=name, out_shape=tuple(pltpu.HBM(a.shape, a.dtype) for a in bufs),
        in_specs=[_HBM] * nb + [_SEM, _SEM, pl.BlockSpec(memory_space=pl.ANY)], out_specs=tuple([_HBM] * nb),
        input_output_aliases={i: i for i in range(nb)},
        compiler_params=pltpu.CompilerParams(has_side_effects=_EFFECT),
    )(*bufs, send_sems, recv_sems, after)
    return list(res)


def _split_relay(name, bufs, send_sems, recv_sems, after, wait_fn, n_sems, start_fn):
    nb = len(bufs)

    def body(*refs):
        for cp in wait_fn(refs[:nb], refs[nb], refs[nb + 1]):
            cp.wait_send()
            cp.wait_recv()
        for cp in start_fn(refs[:nb], refs[nb + 3], refs[nb + 4]):
            cp.start()

    sems = pltpu.SemaphoreType.DMA((n_sems,))
    res = pl.pallas_call(
        body, name=name, out_shape=(sems, sems, *[pltpu.HBM(a.shape, a.dtype) for a in bufs]),
        in_specs=[_HBM] * nb + [_SEM, _SEM, pl.BlockSpec(memory_space=pl.ANY)], out_specs=(_SEM, _SEM, *[_HBM] * nb),
        input_output_aliases={i: 2 + i for i in range(nb)},
        compiler_params=pltpu.CompilerParams(has_side_effects=_EFFECT),
    )(*bufs, send_sems, recv_sems, after)
    return res[0], res[1], list(res[2:])


N_CHIP = 4


def _chip_flip(x, y, k):
    return (1 - x if k & 2 else x), (1 - y if k & 1 else y)


def _gather_first_copies(n):
    def copies(bufs, send_sems, recv_sems):
        x, y, c, me = _position()
        out = []
        for w in range(n):
            for k in range(N_CHIP):
                to = (x, y, 1 - c) if k == 0 else (*_chip_flip(x, y, k), c)
                out.append(pltpu.make_async_remote_copy(
                    src_ref=bufs[w], dst_ref=bufs[n + w].at[me], send_sem=send_sems.at[w * N_CHIP + k],
                    recv_sem=recv_sems.at[w * N_CHIP + k], device_id=to, device_id_type=MESH))
        return out
    return copies


def _gather_relay_copies(n):
    def copies(bufs, send_sems, recv_sems):
        x, y, c, _ = _position()
        out = []
        for w in range(n):
            for k in range(1, N_CHIP):
                px, py = _chip_flip(x, y, k)
                blk = bufs[n + w].at[4 * px + 2 * py + c]
                out.append(pltpu.make_async_remote_copy(
                    src_ref=blk, dst_ref=blk, send_sem=send_sems.at[w * (N_CHIP - 1) + k - 1],
                    recv_sem=recv_sems.at[w * (N_CHIP - 1) + k - 1], device_id=(x, y, 1 - c), device_id_type=MESH))
        return out
    return copies


def _to_sibling_copies(n):
    def copies(bufs, send_sems, recv_sems):
        x, y, c, _ = _position()
        out = []
        for w in range(n):
            for q in range(N_CHIP):
                out.append(pltpu.make_async_remote_copy(
                    src_ref=bufs[w].at[2 * q + 1 - c], dst_ref=bufs[n + w].at[q], send_sem=send_sems.at[w * N_CHIP + q],
                    recv_sem=recv_sems.at[w * N_CHIP + q], device_id=(x, y, 1 - c), device_id_type=MESH))
        return out
    return copies


def _to_owner_copies(n):
    def copies(bufs, send_sems, recv_sems):
        x, y, c, _ = _position()
        out = []
        for w in range(n):
            for k in range(1, N_CHIP):
                px, py = (1 - x if k & 2 else x), (1 - y if k & 1 else y)
                out.append(pltpu.make_async_remote_copy(
                    src_ref=bufs[w].at[2 * px + py], dst_ref=bufs[n + w].at[k - 1], send_sem=send_sems.at[w * (N_CHIP - 1) + k - 1],
                    recv_sem=recv_sems.at[w * (N_CHIP - 1) + k - 1], device_id=(px, py, c), device_id_type=MESH))
        return out
    return copies


def _chip_sum(name, stack, landed, c_idx):
    _, R, C = stack.shape
    tr = _tile(R, max(16, 262144 // C), 16)

    def body(c_ref, a_ref, b_ref, o_ref):
        o_ref[...] = (a_ref[...].astype(F32) + b_ref[...].astype(F32)).astype(o_ref.dtype)

    return pl.pallas_call(
        body, name=name,
        grid_spec=pltpu.PrefetchScalarGridSpec(
            num_scalar_prefetch=1, grid=(N_CHIP, R // tr),
            in_specs=[pl.BlockSpec((None, tr, C), lambda q, i, c_ref: (2 * q + c_ref[0], i, 0)),
                      pl.BlockSpec((None, tr, C), lambda q, i, c_ref: (q, i, 0))],
            out_specs=pl.BlockSpec((None, tr, C), lambda q, i, c_ref: (q, i, 0))),
        out_shape=jax.ShapeDtypeStruct((N_CHIP, R, C), stack.dtype),
        compiler_params=_cparams(("parallel", "parallel")))(c_idx, stack, landed)


def _ada_mod(c16, w):
    _, D = c16.shape
    n = w.shape[1]
    tk = _tile(D, 512)
    nk = D // tk

    def body(c_ref, w_ref, o_ref, ca_ref):
        @pl.when(pl.program_id(0) == 0)
        def _():
            o_ref[...] = jnp.zeros_like(o_ref)

        cv = c_ref[...]
        ca = cv * _sigmoid(cv)
        ca_ref[...] = ca
        o_ref[...] += _dot(ca, w_ref[...])

    return pl.pallas_call(
        body, name="ada_mod", grid=(nk,),
        in_specs=[pl.BlockSpec((16, tk), lambda k: (0, k)), pl.BlockSpec((tk, n), lambda k: (k, 0))],
        out_specs=[pl.BlockSpec((16, n), lambda k: (0, 0)), pl.BlockSpec((16, tk), lambda k: (0, k))],
        out_shape=[jax.ShapeDtypeStruct((16, n), F32), jax.ShapeDtypeStruct((16, D), F32)],
        compiler_params=_cparams(("arbitrary",)))(c16, w)


def _adam_math(w, g, m, v):
    m2 = ADAM_B1 * m + (1.0 - ADAM_B1) * g
    v2 = ADAM_B2 * v + (1.0 - ADAM_B2) * (g * g)
    m_hat = m2 / (1.0 - ADAM_B1 ** ADAM_STEP)
    v_hat = v2 / (1.0 - ADAM_B2 ** ADAM_STEP)
    delta = -ADAM_LR * (m_hat / (jnp.sqrt(v_hat) + ADAM_EPS) + ADAM_WD * w)
    return delta, m2, v2


def _adamw(name, w, m, v, parts):
    R, C = w.shape
    tr = _tile(R, max(16, 131072 // C), 16)
    n_p = len(parts)

    def body(*refs):
        w_ref, m_ref, v_ref = refs[:3]
        g_ref, d_ref, m2_ref, v2_ref = refs[3 + n_p:]
        g = None
        for p_ref in refs[3:3 + n_p]:
            for s in range(p_ref.shape[0]):
                t = p_ref[s].astype(F32)
                g = t if g is None else g + t
        delta, m2, v2 = _adam_math(w_ref[...], g, m_ref[...], v_ref[...])
        g_ref[...] = g
        d_ref[...] = delta
        m2_ref[...] = m2
        v2_ref[...] = v2

    blk = pl.BlockSpec((tr, C), lambda i: (i, 0))
    out = jax.ShapeDtypeStruct((R, C), F32)
    return pl.pallas_call(body, name=name, grid=(R // tr,),
                          in_specs=[blk, blk, blk] + [pl.BlockSpec((a.shape[0], tr, C), lambda i: (0, i, 0)) for a in parts],
                          out_specs=[blk] * 4, out_shape=[out] * 4, compiler_params=_cparams(("parallel",)))(w, m, v, *parts)


def _small_update(gathered, w, m, v, after):
    _, R, L = gathered.shape
    rs = w.shape[0]

    def body(p_ref, w_ref, m_ref, v_ref, after_ref, g_ref, d_ref, m2_ref, v2_ref):
        g = p_ref[0]
        for p in range(1, N_DEV):
            g = g + p_ref[p]
        g_ref[...] = g
        delta, m2, v2 = _adam_math(w_ref[...], g[0:rs, :], m_ref[...], v_ref[...])
        d_ref[...] = delta
        m2_ref[...] = m2
        v2_ref[...] = v2

    vm = pl.BlockSpec(memory_space=pltpu.VMEM)
    sm = jax.ShapeDtypeStruct((rs, L), F32)
    return pl.pallas_call(body, name="small_update", in_specs=[vm] * 4 + [pl.BlockSpec(memory_space=pl.ANY)], out_specs=[vm] * 4,
                          out_shape=[jax.ShapeDtypeStruct((R, L), F32), sm, sm, sm],
                          compiler_params=pltpu.CompilerParams(vmem_limit_bytes=VMEM_LIMIT))(gathered, w, m, v, after)


class _Fetched(dict):
    def __init__(self, fetch):
        super().__init__()
        self.fetch = fetch

    def first(self, key, after):
        self[key] = self.fetch(key, after)
        return self[key]


def _local_step(x, tgt, mod, p, fetch, F, scatter=None):
    S, D = x.shape
    GW, HW = p["ln_g"].shape[1], p["hg_ng"].shape[1]
    G, T, _ = p["ws"].shape
    w = _Fetched(fetch)
    INW = 2 * GW + 4 * HW + 2 * D
    in_loc, br_loc, fi_loc = INW // N_DEV, D // N_DEV, 2 * F // N_DEV
    assert GW == HW and F % fi_loc == 0
    sh1, sc1, gt1, sh2, sc2, gt2 = (mod[:, k * D:(k + 1) * D] for k in range(6))
    bsb = jnp.broadcast_to(p["bs"][:, :, None], (G, T, GW // G))

    tm = _tile(S, 1024, 16)
    tmh = _tile(S, 512, 16)
    tn_in = _tile(in_loc, 640)
    tn_d = _tile(D, 512)
    tn_br = _tile(br_loc, 512)
    tk_s = S
    tm_w = _tile(D, 1024)
    g_off = 2 * GW + 4 * HW

    h1 = _norm_mod("norm1", x, p["norm1_g"], sc1, sh1)
    z = _mm_nn_stacked("proj_in", h1, w.first("in", h1), tm=tm, tn=tn_in, tk=D)[0]
    ya = _gmlp_fwd(z, p["ln_g"], p["ln_b"], p["ws"], bsb, GW)
    yb, o_hg, states = _hg_fwd(z, p["hg_lb"], p["hg_ng"], HW)
    pa = _mm_nn_stacked("branch_gmlp", ya, w.first("bg", z), tm=tm, tn=tn_br, tk=GW)[0]

    def gates(ga_ref, gb_ref, ba_ref, bb_ref):
        return _sigmoid(ga_ref[...] + ba_ref[...]), _sigmoid(gb_ref[...] + bb_ref[...])

    def gate_specs(tn_):
        o1, o2 = g_off // tn_, (g_off + D) // tn_
        return [pl.BlockSpec((tm, tn_), lambda i, j, k: (i, o1 + j)), pl.BlockSpec((tm, tn_), lambda i, j, k: (i, o2 + j)),
                pl.BlockSpec((1, tn_), lambda i, j, k: (0, j)), pl.BlockSpec((1, tn_), lambda i, j, k: (0, D // tn_ + j))]

    def merge_ep(acc, ex, outs):
        ga, gb = gates(*ex[1:5])
        outs[0][...] = acc
        outs[1][...] = (ga * ex[0][...] + gb * acc).astype(BF16)

    tile_o = pl.BlockSpec((tm, tn_br), lambda i, j, k: (i, j))
    pb, y = _mm_nn_stacked(
        "branch_hg_merge", yb, w.first("bh", z), tm=tm, tn=tn_br, tk=HW, extras=[pa, z, z, p["b_gate"], p["b_gate"]],
        extra_specs=[tile_o, *gate_specs(tn_br)], out_shapes=[jax.ShapeDtypeStruct((S, D), F32), jax.ShapeDtypeStruct((S, D), BF16)],
        out_specs=[tile_o, tile_o], epilogue=merge_ep)

    def resid_ep(acc, ex, outs):
        outs[0][...] = acc
        outs[1][...] = ex[0][...] + ex[1][...] * acc

    def resid_mm(name, a, b, res, gt, tk):
        K = a.shape[1]
        t_o = pl.BlockSpec((tm, tn_d), lambda i, j, k: (i, j))
        return _matmul(
            name, a, b, dims=_NN, grid_mnk=(S // tm, D // tn_d, K // tk), tiles=(tm, tn_d),
            a_spec=pl.BlockSpec((tm, tk), lambda i, j, k: (i, k)), b_spec=pl.BlockSpec((tk, tn_d), lambda i, j, k: (k, j)),
            extras=[res, gt], extra_specs=[t_o, pl.BlockSpec((1, tn_d), lambda i, j, k: (0, j))],
            out_shapes=[jax.ShapeDtypeStruct((S, D), F32)] * 2, out_specs=[t_o, t_o], epilogue=resid_ep)

    o1, xm = resid_mm("proj_out", y, w.first("out", z), x, gt1, D)
    h2 = _norm_mod("norm2", xm, p["norm2_g"], sc2, sh2)
    ab = _mm_nn_stacked("ffn_in", h2, w.first("fi", y), tm=tm, tn=fi_loc, tk=D)[0]
    hf = _swiglu(ab, F, fi_loc)
    o2, x3 = resid_mm("ffn_out", hf, w.first("fo", ab), xm, gt2, _tile(F, 1536))
    dx3, do2, vec_l = _loss_head(x3, tgt, p["final_g"], o2, gt2)

    nf = F // fi_loc

    def dswiglu_ep(acc, ex, outs):
        a, up = ex[0][...], ex[1][...]
        sa = _sigmoid(a)
        outs[0][0] = (acc * up * (sa * (1.0 + a * (1.0 - sa)))).astype(BF16)
        outs[0][1] = (acc * (a * sa)).astype(BF16)

    dab = _matmul(
        "ffn_out_dx", do2, w["fo"], dims=_NT, grid_mnk=(S // tmh, nf, 1), tiles=(tmh, fi_loc),
        a_spec=pl.BlockSpec((tmh, D), lambda i, j, k: (i, 0)), b_spec=pl.BlockSpec((fi_loc, D), lambda i, j, k: (j, 0)),
        extras=[ab, ab], extra_specs=[pl.BlockSpec((tmh, fi_loc), lambda i, j, k: (i, j)), pl.BlockSpec((tmh, fi_loc), lambda i, j, k: (i, j + nf))],
        out_shapes=[jax.ShapeDtypeStruct((2, S, F), BF16)], out_specs=[pl.BlockSpec((2, tmh, fi_loc), lambda i, j, k: (0, i, j))],
        epilogue=dswiglu_ep)[0]
    start = (lambda name, grads: scatter[0](name, grads)) if scatter is not None else (lambda name, grads: None)
    push = (lambda name, after: scatter[1](name, after)) if scatter is not None else (lambda name, after: None)

    def zero(token):
        return 0.0 if token is None else token[0:1, 0:1]

    tm_f = _tile(F, 512)
    g_fo = _mm_tn("ffn_out_dw", hf, do2, pl.BlockSpec((tk_s, D), lambda i, j, k: (k, j)), Mo=F, No=D, S=S, tm=tm_f, tn=D, tk=tk_s)
    g_fi = _mm_tn("ffn_in_dw", h2, dab, pl.BlockSpec((None, tk_s, fi_loc), lambda i, j, k: (j // nf, k, j % nf)),
                  Mo=D, No=2 * F, S=S, tm=tm_w, tn=fi_loc, tk=tk_s, stacked_nloc=fi_loc, after=g_fo)
    t_ffn = start("scatter_ffn", dict(fo=g_fo, fi=g_fi))
    dh2 = _mm_nt_stacked("ffn_in_dx", pl.BlockSpec((None, tmh, fi_loc), lambda i, j, k: (k // nf, i, k % nf)), dab, w["fi"],
                         M=S, tm=tmh, tn=D, tk=fi_loc, after=t_ffn)
    dxm, vec2, do1 = _norm_mod_bwd("norm2_bwd", dh2, xm, p["norm2_g"], sc2, dx3, o1, gt1)
    t_ffn = push("scatter_ffn", dxm)

    def dmerge_ep(acc, ex, outs):
        ga, gb = gates(*ex[2:6])
        outs[0][...] = (acc * ga).astype(BF16)
        outs[1][...] = (acc * gb).astype(BF16)
        outs[2][0] = (acc * ex[0][...] * ga * (1.0 - ga)).astype(BF16)
        outs[2][1] = (acc * ex[1][...] * gb * (1.0 - gb)).astype(BF16)

    t_o = pl.BlockSpec((tm, tn_d), lambda i, j, k: (i, j))
    dpa, dpb, dg2 = _matmul(
        "proj_out_dx", do1, w["out"], dims=_NT, grid_mnk=(S // tm, D // tn_d, 1), tiles=(tm, tn_d),
        a_spec=pl.BlockSpec((tm, D), lambda i, j, k: (i, 0)), b_spec=pl.BlockSpec((tn_d, D), lambda i, j, k: (j, 0)),
        extras=[pa, pb, z, z, p["b_gate"], p["b_gate"]], extra_specs=[t_o, t_o, *gate_specs(tn_d)],
        out_shapes=[jax.ShapeDtypeStruct((S, D), BF16), jax.ShapeDtypeStruct((S, D), BF16), jax.ShapeDtypeStruct((2, S, D), BF16)],
        out_specs=[t_o, t_o, pl.BlockSpec((2, tm, tn_d), lambda i, j, k: (0, i, j))], epilogue=dmerge_ep, after=t_ffn)
    g_out = _mm_tn("proj_out_dw", y, do1, pl.BlockSpec((tk_s, D), lambda i, j, k: (k, j)), Mo=D, No=D, S=S, tm=tn_d, tn=D, tk=tk_s)
    tn_g = _tile(GW, 512)
    b_br = pl.BlockSpec((tk_s, br_loc), lambda i, j, k: (k, j))
    g_bg = _mm_tn("branch_gmlp_dw", ya, dpa, b_br, Mo=GW, No=D, S=S, tm=tn_g, tn=br_loc, tk=tk_s, stacked_nloc=br_loc)
    g_bh = _mm_tn("branch_hg_dw", yb, dpb, b_br, Mo=HW, No=D, S=S, tm=tn_g, tn=br_loc, tk=tk_s, stacked_nloc=br_loc)
    t_mix = start("scatter_mixer", dict(out=g_out, bg=g_bg, bh=g_bh))
    def branch_dx(name, dp, wg):
        flat = jnp.swapaxes(wg, 0, 1).reshape(wg.shape[1], D)
        return _matmul(
            name, dp, flat, dims=_NT, grid_mnk=(S // tm, GW // tn_g, 1), tiles=(tm, tn_g),
            a_spec=pl.BlockSpec((tm, D), lambda i, j, k: (i, 0)), b_spec=pl.BlockSpec((tn_g, D), lambda i, j, k: (j, 0)),
            out_shapes=[jax.ShapeDtypeStruct((S, GW), F32)], out_specs=[pl.BlockSpec((tm, tn_g), lambda i, j, k: (i, j))],
            epilogue=_store(F32), after=t_mix)[0]

    dya = branch_dx("branch_gmlp_dx", dpa, w["bg"])
    dyb = branch_dx("branch_hg_dx", dpb, w["bh"])
    db_gate = _colsum2(dg2)
    dz_gmlp, dln, dws, dbs = _gmlp_bwd(z, dya, p["ln_g"], p["ln_b"], p["ws"], bsb, GW)
    t_mix = push("scatter_mixer", dz_gmlp)
    dz_hg, dng, dhlb = _hg_bwd(z, o_hg, states, dyb, p["hg_lb"], p["hg_ng"] + zero(t_mix), HW)
    dz = jnp.concatenate([dz_gmlp, dz_hg, dg2[0], dg2[1]], axis=1)
    g_in = _mm_tn("proj_in_dw", h1, dz, pl.BlockSpec((tk_s, in_loc), lambda i, j, k: (k, j)), Mo=D, No=INW, S=S,
                  tm=tm_w, tn=in_loc, tk=tk_s, stacked_nloc=in_loc)
    t_in = start("scatter_proj_in", dict(w_in=g_in))
    dh1 = _mm_nt_stacked("proj_in_dx", pl.BlockSpec((tmh, in_loc), lambda i, j, k: (i, k)), dz, w["in"], M=S, tm=tmh, tn=D, tk=in_loc,
                         after=t_in)
    dx, vec1 = _norm_mod_bwd("norm1_bwd", dh1, x, p["norm1_g"], sc1, dxm)

    dmod = jnp.concatenate([vec1[0:1], vec1[1:2], vec2[3:4], vec2[0:1], vec2[1:2], vec_l[2:3]], axis=1)
    small = dict(norm1_g=vec1[2:3], b_gate=db_gate.reshape(1, 2 * D), ln_g=dln[0:1], ln_b=dln[1:2], ws=dws, bs=dbs.reshape(G, T),
                 hg_lb=dhlb, hg_ng=dng[0:1], norm2_g=vec2[2:3], final_g=vec_l[1:2], loss=vec_l[0:1, 0:LANES])
    big = dict(w_in=g_in, bg=g_bg, bh=g_bh, out=g_out, fi=g_fi, fo=g_fo)
    return dx, big, small, dmod


_SMALL = ("b_ada", "norm1_g", "b_gate", "ln_g", "ln_b", "ws", "bs", "hg_lb", "hg_ng", "norm2_g", "final_g")


def _pack(parts, rows_mult=8):
    flat = [a.reshape(-1) for a in parts]
    offs, n = [], 0
    for a in flat:
        offs.append(n)
        n += a.shape[0]
    pad = (-n) % (LANES * rows_mult)
    if pad:
        flat.append(jnp.zeros((pad,), F32))
    return jnp.concatenate(flat).reshape(-1, LANES), offs


def kernel(x, c, w_ada, b_ada, norm1_g, w_in, b_gate, gmlp_ln_g, gmlp_ln_b, gmlp_ws, gmlp_bs, hg_lb, hg_norm_g, w_branch_gmlp, w_branch_hg, w_out, norm2_g, w_ffn_in, w_ffn_out, final_norm_g, loss_target, m_w_ada, m_b_ada, m_norm1_g, m_w_in, m_b_gate, m_gmlp_ln_g, m_gmlp_ln_b, m_gmlp_ws, m_gmlp_bs, m_hg_lb, m_hg_norm_g, m_w_branch_gmlp, m_w_branch_hg, m_w_out, m_norm2_g, m_w_ffn_in, m_w_ffn_out, m_final_norm_g, v_w_ada, v_b_ada, v_norm1_g, v_w_in, v_b_gate, v_gmlp_ln_g, v_gmlp_ln_b, v_gmlp_ws, v_gmlp_bs, v_hg_lb, v_hg_norm_g, v_w_branch_gmlp, v_w_branch_hg, v_w_out, v_norm2_g, v_w_ffn_in, v_w_ffn_out, v_final_norm_g):
    S, D = x.shape[1], x.shape[2]
    ada_loc = w_ada.shape[2]
    me = 4 * lax.axis_index("x") + 2 * lax.axis_index("y") + lax.axis_index("c")

    c_all = _allgather_small("gather_c", c.reshape(D // LANES, LANES)).reshape(N_DEV, D)
    mod_cols, c_act = _ada_mod(jnp.pad(c_all, ((0, 16 - N_DEV), (0, 0))), w_ada[0])
    mod_all = _allgather_small("gather_mod", mod_cols[:N_DEV].reshape(-1, LANES)).reshape(N_DEV, N_DEV, ada_loc)
    mod = lax.dynamic_index_in_dim(mod_all, me, axis=1, keepdims=False).reshape(1, N_DEV * ada_loc) + b_ada

    def empty_hbm(shape, dtype):
        return pltpu.with_memory_space_constraint(lax.empty(shape, dtype), pltpu.HBM)

    groups = dict(gather_in=dict(keys=["in"], src=[w_in]), gather_mixer=dict(keys=["bg", "bh", "out"], src=[w_branch_gmlp, w_branch_hg, w_out]),
                  gather_ffn_in=dict(keys=["fi"], src=[w_ffn_in]), gather_ffn_out=dict(keys=["fo"], src=[w_ffn_out]))
    group_of = {}
    prev = mod_all
    for gname, g in groups.items():
        n = len(g["keys"])
        shards = [a[0].astype(BF16) for a in g["src"]]
        outs = [lax.dynamic_update_slice(lax.empty((N_DEV, *s.shape), BF16), s[None], (me, 0, 0)) for s in shards]
        *g["hop1"], prev = _split_start(gname + "_hop1", shards + outs, n * N_CHIP, _gather_first_copies(n), after=prev)
        for k in g["keys"]:
            group_of[k] = gname
    mod = mod + prev[0:1, 0:1]

    def fetch(key, after):
        g = groups[group_of[key]]
        n = len(g["keys"])
        if "done" not in g:
            send_sems, recv_sems, bufs = g["hop1"]
            send_sems, recv_sems, bufs = _split_relay(group_of[key] + "_relay", bufs, send_sems, recv_sems, after,
                                                      _gather_first_copies(n), n * (N_CHIP - 1), _gather_relay_copies(n))
            bufs = _split_wait(group_of[key] + "_hop2_wait", bufs, send_sems, recv_sems, after, _gather_relay_copies(n))
            g["done"] = dict(zip(g["keys"], bufs[n:]))
        arr = g["done"][key]
        return arr.reshape(-1, D) if key in ("out", "fo") else arr

    p = dict(norm1_g=norm1_g, b_gate=b_gate, ln_g=gmlp_ln_g, ln_b=gmlp_ln_b, ws=gmlp_ws[0], bs=gmlp_bs[0], hg_lb=hg_lb,
             hg_ng=hg_norm_g, norm2_g=norm2_g, final_g=final_norm_g.reshape(1, D))

    in_flight = {}
    c_idx = lax.axis_index("c").astype(jnp.int32).reshape(1)
    my_chip = 2 * lax.axis_index("x") + lax.axis_index("y")

    def scatter_start(name, grads):
        keys = list(grads)
        n = len(keys)
        stacks = [grads[k].reshape(N_DEV, -1, grads[k].shape[-1]) for k in keys]
        lands = [empty_hbm((N_CHIP, *g.shape[1:]), g.dtype) for g in stacks]
        send_sems, recv_sems, bufs, token = _split_start(name + "_d2d", stacks + lands, n * N_CHIP, _to_sibling_copies(n))
        in_flight[name] = dict(keys=keys, stage1=(send_sems, recv_sems, bufs))
        return token

    def scatter_push(name, after):
        f = in_flight[name]
        n = len(f["keys"])
        send_sems, recv_sems, bufs = f["stage1"]
        bufs = _split_wait(name + "_d2d_wait", bufs, send_sems, recv_sems, after, _to_sibling_copies(n))
        sums = [_chip_sum(f"{name}_sum_{k}", bufs[i], bufs[n + i], c_idx) for i, k in enumerate(f["keys"])]
        lands = [empty_hbm((N_CHIP - 1, *s.shape[1:]), s.dtype) for s in sums]
        send_sems, recv_sems, bufs, token = _split_start(name + "_ici", sums + lands, n * (N_CHIP - 1), _to_owner_copies(n))
        f["stage2"] = (send_sems, recv_sems, bufs)
        return token

    grad_x, _, small, dmod = _local_step(x[0], loss_target[0], mod, p, fetch, w_ffn_out.shape[1] * N_DEV, (scatter_start, scatter_push))

    small["b_ada"] = dmod
    packed, offs = _pack([small[k] for k in _SMALL] + [small["loss"]])
    gathered = _allgather_small("gather_small", packed)
    wp = dict(p, b_ada=b_ada)
    ms = dict(b_ada=m_b_ada, norm1_g=m_norm1_g, b_gate=m_b_gate, ln_g=m_gmlp_ln_g, ln_b=m_gmlp_ln_b, ws=m_gmlp_ws, bs=m_gmlp_bs,
              hg_lb=m_hg_lb, hg_ng=m_hg_norm_g, norm2_g=m_norm2_g, final_g=m_final_norm_g)
    vs = dict(b_ada=v_b_ada, norm1_g=v_norm1_g, b_gate=v_b_gate, ln_g=v_gmlp_ln_g, ln_b=v_gmlp_ln_b, ws=v_gmlp_ws, bs=v_gmlp_bs,
              hg_lb=v_hg_lb, hg_ng=v_hg_norm_g, norm2_g=v_norm2_g, final_g=v_final_norm_g)
    w_sm, _ = _pack([wp[k] for k in _SMALL])
    m_sm, _ = _pack([ms[k] for k in _SMALL])
    v_sm, _ = _pack([vs[k] for k in _SMALL])
    t_tail = scatter_push("scatter_proj_in", gathered)
    sm_out = _small_update(gathered, w_sm, m_sm, v_sm, t_tail)
    shapes = dict(b_ada=b_ada.shape, norm1_g=norm1_g.shape, b_gate=b_gate.shape, ln_g=gmlp_ln_g.shape, ln_b=gmlp_ln_b.shape,
                  ws=gmlp_ws.shape, bs=gmlp_bs.shape, hg_lb=hg_lb.shape, hg_ng=hg_norm_g.shape, norm2_g=norm2_g.shape,
                  final_g=final_norm_g.shape)

    def unpack(arr, k):
        i = _SMALL.index(k)
        n = math.prod(shapes[k])
        return arr.reshape(-1)[offs[i]:offs[i] + n].reshape(shapes[k])

    loss = sm_out[0].reshape(-1)[offs[len(_SMALL)]]

    dmod_all = gathered.reshape(N_DEV, -1)[:, offs[0]:offs[0] + N_DEV * ada_loc]
    dmod_loc = lax.dynamic_slice_in_dim(dmod_all, me * ada_loc, ada_loc, axis=1)
    ca_t = jnp.pad(c_act[:N_DEV].T, ((0, 0), (0, LANES - N_DEV))).astype(BF16)
    dm_p = jnp.pad(dmod_loc, ((0, LANES - N_DEV), (0, 0))).astype(BF16)
    tm_a = _tile(D, 512)
    g_ada = _matmul(
        "ada_dw", ca_t, dm_p, dims=_NN, grid_mnk=(D // tm_a, 1, 1), tiles=(tm_a, ada_loc),
        a_spec=pl.BlockSpec((tm_a, LANES), lambda i, j, k: (i, 0)), b_spec=pl.BlockSpec((LANES, ada_loc), lambda i, j, k: (0, 0)),
        out_shapes=[jax.ShapeDtypeStruct((1, D, ada_loc), F32)], out_specs=[pl.BlockSpec((None, tm_a, ada_loc), lambda i, j, k: (0, i, 0))],
        epilogue=_store(F32), after=t_tail)[0]

    upd = {"w_ada": _adamw("adamw_w_ada", w_ada[0], m_w_ada[0], v_w_ada[0], [g_ada])}
    big_w = dict(w_in=(w_in, m_w_in, v_w_in, "w_in"), bg=(w_branch_gmlp, m_w_branch_gmlp, v_w_branch_gmlp, "w_branch_gmlp"),
                 bh=(w_branch_hg, m_w_branch_hg, v_w_branch_hg, "w_branch_hg"), out=(w_out, m_w_out, v_w_out, "w_out"),
                 fi=(w_ffn_in, m_w_ffn_in, v_w_ffn_in, "w_ffn_in"), fo=(w_ffn_out, m_w_ffn_out, v_w_ffn_out, "w_ffn_out"))
    after = upd["w_ada"][1]
    for name in ("scatter_ffn", "scatter_mixer", "scatter_proj_in"):
        keys = in_flight[name]["keys"]
        n = len(keys)
        send_sems, recv_sems, bufs = in_flight[name]["stage2"]
        bufs = _split_wait(name + "_ici_wait", bufs, send_sems, recv_sems, after, _to_owner_copies(n))
        for i, k in enumerate(keys):
            wt, mt, vt, out_name = big_w[k]
            own_k = lax.dynamic_index_in_dim(bufs[i], my_chip, axis=0, keepdims=True)
            upd[out_name] = _adamw("adamw_" + out_name, wt[0], mt[0], vt[0], [own_k, bufs[n + i]])
            after = upd[out_name][1]

    order = ("w_ada", "b_ada", "norm1_g", "w_in", "b_gate", "ln_g", "ln_b", "ws", "bs", "hg_lb", "hg_ng", "w_branch_gmlp", "w_branch_hg",
             "w_out", "norm2_g", "w_ffn_in", "w_ffn_out", "final_g")
    outs = [loss, grad_x[None]]
    for idx in range(4):
        for k in order:
            outs.append(upd[k][idx][None] if k in upd else unpack(sm_out[idx], k))
    return tuple(outs)
```

```python
import functools
import math

import jax
import jax.numpy as jnp
from jax import lax
from jax.experimental import pallas as pl
from jax.experimental.pallas import tpu as pltpu

F32 = jnp.float32
BF16 = jnp.bfloat16
N_DEV = 8
EPS = 1e-6
LANES = 128
HG_DK = 128
HG_CHUNK = 64
HG_MID = HG_CHUNK // 2 - 1
EXP_CLAMP = 80.0
VMEM_LIMIT = 48 * 1024 * 1024
ADAM_LR, ADAM_B1, ADAM_B2, ADAM_EPS, ADAM_WD, ADAM_STEP = 0.001, 0.9, 0.999, 1e-08, 0.01, 10
MESH = pl.DeviceIdType.MESH

_NN = (((1,), (0,)), ((), ()))
_NT = (((1,), (1,)), ((), ()))
_TN = (((0,), (0,)), ((), ()))


def _dot(a, b, dims=_NN):
    return lax.dot_general(a.astype(BF16), b.astype(BF16), dims, preferred_element_type=F32)


def _tile(n, target, mult=LANES):
    best = None
    for t in range(mult, min(n, target) + 1, mult):
        if n % t == 0:
            best = t
    return n if best is None else best


def _cparams(sem):
    return pltpu.CompilerParams(dimension_semantics=sem, vmem_limit_bytes=VMEM_LIMIT)


def _sigmoid(x):
    return 1.0 / (1.0 + jnp.exp(-x))


def _gelu_parts(x):
    k0 = math.sqrt(2.0 / math.pi)
    x2 = x * x
    t = jnp.tanh(k0 * (x + 0.044715 * x * x2))
    g = 0.5 * x * (1.0 + t)
    dg = 0.5 * (1.0 + t) + 0.5 * x * (1.0 - t * t) * (k0 * (1.0 + 3.0 * 0.044715 * x2))
    return g, dg


def _split3(x):
    h = x.astype(BF16)
    r = x - h.astype(F32)
    m = r.astype(BF16)
    lo = (r - m.astype(F32)).astype(BF16)
    return h, m, lo


def _ones_dot(mat01, x):
    h, m, lo = _split3(x)
    d = functools.partial(lax.dot_general, dimension_numbers=_NN, preferred_element_type=F32)
    return d(mat01, h) + d(mat01, m) + d(mat01, lo)


def _matmul(name, a, b, *, dims, grid_mnk, tiles, a_spec, b_spec, extras=(), extra_specs=(), out_shapes, out_specs, epilogue, after=None):
    gm, gn, nk = grid_mnk
    tm, tn = tiles
    n_ex, n_out = len(extras), len(out_shapes)
    held = [] if after is None else [after]

    def body(*refs):
        a_ref, b_ref = refs[0], refs[1]
        ex = refs[2:2 + n_ex]
        outs = refs[2 + n_ex + len(held):2 + n_ex + len(held) + n_out]
        if nk == 1:
            epilogue(lax.dot_general(a_ref[...], b_ref[...], dims, preferred_element_type=F32), ex, outs)
            return
        acc = refs[-1]
        k = pl.program_id(2)

        @pl.when(k == 0)
        def _():
            acc[...] = jnp.zeros_like(acc)

        acc[...] += lax.dot_general(a_ref[...], b_ref[...], dims, preferred_element_type=F32)

        @pl.when(k == nk - 1)
        def _():
            epilogue(acc[...], ex, outs)

    return pl.pallas_call(
        body, name=name, grid=(gm, gn, nk), in_specs=[a_spec, b_spec, *extra_specs] + [pl.BlockSpec(memory_space=pl.ANY)] * len(held),
        out_specs=list(out_specs), out_shape=list(out_shapes), scratch_shapes=[] if nk == 1 else [pltpu.VMEM((tm, tn), F32)],
        compiler_params=_cparams(("parallel", "parallel", "arbitrary")),
    )(a, b, *extras, *held)


def _store(dtype):
    def ep(acc, ex, outs):
        outs[0][...] = acc.astype(dtype)
    return ep


def _mm_nn_stacked(name, a, wg, *, tm, tn, tk, out_dtype=F32, extras=(), extra_specs=(), out_shapes=None, out_specs=None, epilogue=None):
    M, K = a.shape
    _, _, nloc = wg.shape
    N = nloc * N_DEV
    q = nloc // tn
    if out_shapes is None:
        out_shapes = [jax.ShapeDtypeStruct((M, N), out_dtype)]
        out_specs = [pl.BlockSpec((tm, tn), lambda i, j, k: (i, j))]
        epilogue = _store(out_dtype)
    return _matmul(
        name, a, wg, dims=_NN, grid_mnk=(M // tm, N // tn, K // tk), tiles=(tm, tn),
        a_spec=pl.BlockSpec((tm, tk), lambda i, j, k: (i, k)),
        b_spec=pl.BlockSpec((None, tk, tn), lambda i, j, k: (j // q, k, j % q)),
        extras=extras, extra_specs=extra_specs, out_shapes=out_shapes, out_specs=out_specs, epilogue=epilogue)


def _mm_nt_stacked(name, a_spec, a, wg, *, M, tm, tn, tk, out_dtype=F32, after=None):
    _, Kw, nloc = wg.shape
    q = nloc // tk
    return _matmul(
        name, a, wg, dims=_NT, grid_mnk=(M // tm, Kw // tn, (nloc * N_DEV) // tk), tiles=(tm, tn),
        a_spec=a_spec, b_spec=pl.BlockSpec((None, tn, tk), lambda i, j, k: (k // q, j, k % q)),
        out_shapes=[jax.ShapeDtypeStruct((M, Kw), out_dtype)], out_specs=[pl.BlockSpec((tm, tn), lambda i, j, k: (i, j))],
        epilogue=_store(out_dtype), after=after)[0]


def _mm_tn(name, a, b, b_spec, *, Mo, No, S, tm, tn, tk, stacked_nloc=None, after=None):
    if stacked_nloc is None:
        out_shape = jax.ShapeDtypeStruct((Mo, No), BF16)
        out_spec = pl.BlockSpec((tm, tn), lambda i, j, k: (i, j))
    else:
        q = stacked_nloc // tn
        out_shape = jax.ShapeDtypeStruct((N_DEV, Mo, stacked_nloc), BF16)
        out_spec = pl.BlockSpec((None, tm, tn), lambda i, j, k: (j // q, i, j % q))
    return _matmul(
        name, a, b, dims=_TN, grid_mnk=(Mo // tm, No // tn, S // tk), tiles=(tm, tn),
        a_spec=pl.BlockSpec((tk, tm), lambda i, j, k: (k, i)), b_spec=b_spec,
        out_shapes=[out_shape], out_specs=[out_spec], epilogue=_store(BF16), after=after)[0]


def _norm_mod(name, x, g, sc, sh):
    S, D = x.shape
    tm = _tile(S, 256, 8)

    def body(x_ref, g_ref, sc_ref, sh_ref, h_ref):
        xv = x_ref[...]
        r = lax.rsqrt(jnp.mean(xv * xv, axis=-1, keepdims=True) + EPS)
        h = (xv * r) * g_ref[...]
        h_ref[...] = (h * (1.0 + sc_ref[...]) + sh_ref[...]).astype(BF16)

    row = pl.BlockSpec((tm, D), lambda i: (i, 0))
    vec = pl.BlockSpec((1, D), lambda i: (0, 0))
    return pl.pallas_call(body, name=name, grid=(S // tm,), in_specs=[row, vec, vec, vec], out_specs=row,
                          out_shape=jax.ShapeDtypeStruct((S, D), BF16), compiler_params=_cparams(("parallel",)))(x, g, sc, sh)


def _norm_mod_bwd(name, dh, x, g, sc, dres, o=None, gt=None):
    S, D = x.shape
    tm = _tile(S, 256, 8)
    gated = o is not None

    def body(*refs):
        if gated:
            dh_ref, x_ref, g_ref, sc_ref, dres_ref, o_ref, gt_ref, dx_ref, vec_ref, do_ref = refs
        else:
            dh_ref, x_ref, g_ref, sc_ref, dres_ref, dx_ref, vec_ref = refs
        i = pl.program_id(0)

        @pl.when(i == 0)
        def _():
            vec_ref[...] = jnp.zeros_like(vec_ref)

        xv, dh_v, gv = x_ref[...], dh_ref[...], g_ref[...]
        r = lax.rsqrt(jnp.mean(xv * xv, axis=-1, keepdims=True) + EPS)
        xn = xv * r
        one_sc = 1.0 + sc_ref[...]
        vec_ref[0:1, :] += jnp.sum(dh_v, axis=0, keepdims=True)
        vec_ref[1:2, :] += jnp.sum(dh_v * (xn * gv), axis=0, keepdims=True)
        vec_ref[2:3, :] += jnp.sum(dh_v * one_sc * xn, axis=0, keepdims=True)
        dxn = dh_v * one_sc * gv
        dx = dres_ref[...] + r * (dxn - xn * jnp.mean(dxn * xn, axis=-1, keepdims=True))
        dx_ref[...] = dx
        if gated:
            vec_ref[3:4, :] += jnp.sum(dx * o_ref[...], axis=0, keepdims=True)
            do_ref[...] = (dx * gt_ref[...]).astype(BF16)

    row = pl.BlockSpec((tm, D), lambda i: (i, 0))
    vec = pl.BlockSpec((1, D), lambda i: (0, 0))
    acc = pl.BlockSpec((8, D), lambda i: (0, 0))
    ins = [dh, x, g, sc, dres] + ([o, gt] if gated else [])
    in_specs = [row, row, vec, vec, row] + ([row, vec] if gated else [])
    out_shape = [jax.ShapeDtypeStruct((S, D), F32), jax.ShapeDtypeStruct((8, D), F32)]
    out_specs = [row, acc]
    if gated:
        out_shape.append(jax.ShapeDtypeStruct((S, D), BF16))
        out_specs.append(row)
    return pl.pallas_call(body, name=name, grid=(S // tm,), in_specs=in_specs, out_specs=out_specs, out_shape=out_shape,
                          compiler_params=_cparams(("arbitrary",)))(*ins)


def _loss_head(x3, tgt, gf, o2, gt2):
    S, D = x3.shape
    tm = _tile(S, 256, 8)

    def body(x_ref, t_ref, g_ref, o_ref, gt_ref, dx_ref, do_ref, vec_ref):
        i = pl.program_id(0)

        @pl.when(i == 0)
        def _():
            vec_ref[...] = jnp.zeros_like(vec_ref)

        xv, gv = x_ref[...], g_ref[...]
        r = lax.rsqrt(jnp.mean(xv * xv, axis=-1, keepdims=True) + EPS)
        xn = xv * r
        e = xn * gv - t_ref[...]
        tok = 0.5 * jnp.mean(e * e, axis=-1, keepdims=True)
        vec_ref[0:1, :] += jnp.broadcast_to(jnp.sum(tok, axis=0, keepdims=True), (1, D))
        dy = e * (1.0 / D)
        vec_ref[1:2, :] += jnp.sum(dy * xn, axis=0, keepdims=True)
        dxn = dy * gv
        dx = r * (dxn - xn * jnp.mean(dxn * xn, axis=-1, keepdims=True))
        dx_ref[...] = dx
        vec_ref[2:3, :] += jnp.sum(dx * o_ref[...], axis=0, keepdims=True)
        do_ref[...] = (dx * gt_ref[...]).astype(BF16)

    row = pl.BlockSpec((tm, D), lambda i: (i, 0))
    vec = pl.BlockSpec((1, D), lambda i: (0, 0))
    return pl.pallas_call(
        body, name="loss_head", grid=(S // tm,), in_specs=[row, row, vec, row, vec],
        out_specs=[row, row, pl.BlockSpec((8, D), lambda i: (0, 0))],
        out_shape=[jax.ShapeDtypeStruct((S, D), F32), jax.ShapeDtypeStruct((S, D), BF16), jax.ShapeDtypeStruct((8, D), F32)],
        compiler_params=_cparams(("arbitrary",)))(x3, tgt, gf, o2, gt2)


def _swiglu(ab, F, tf):
    S = ab.shape[0]
    tm = _tile(S, 512, 8)
    nf = F // tf

    def body(a_ref, u_ref, h_ref):
        a = a_ref[...]
        h_ref[...] = (a * _sigmoid(a) * u_ref[...]).astype(BF16)

    return pl.pallas_call(
        body, name="swiglu", grid=(S // tm, nf),
        in_specs=[pl.BlockSpec((tm, tf), lambda i, j: (i, j)), pl.BlockSpec((tm, tf), lambda i, j: (i, j + nf))],
        out_specs=pl.BlockSpec((tm, tf), lambda i, j: (i, j)), out_shape=jax.ShapeDtypeStruct((S, F), BF16),
        compiler_params=_cparams(("parallel", "parallel")))(ab, ab)


def _colsum2(dg2):
    _, S, D = dg2.shape
    tm = _tile(S, 256, 16)

    def body(x_ref, o_ref):
        @pl.when(pl.program_id(0) == 0)
        def _():
            o_ref[...] = jnp.zeros_like(o_ref)

        o_ref[0:1, :] += jnp.sum(x_ref[0].astype(F32), axis=0, keepdims=True)
        o_ref[1:2, :] += jnp.sum(x_ref[1].astype(F32), axis=0, keepdims=True)

    return pl.pallas_call(body, name="gate_bias_grad", grid=(S // tm,), in_specs=[pl.BlockSpec((2, tm, D), lambda i: (0, i, 0))],
                          out_specs=pl.BlockSpec((2, D), lambda i: (0, 0)), out_shape=jax.ShapeDtypeStruct((2, D), F32),
                          compiler_params=_cparams(("arbitrary",)))(dg2)


def _gmlp_common(u_ref, v_ref, lg_ref, lb_ref, ws_ref, bsb_ref, G, T, Dg):
    ug, dug = _gelu_parts(u_ref[...])
    vg, dvg = _gelu_parts(v_ref[...])
    mu = jnp.mean(vg, axis=-1, keepdims=True)
    vc = vg - mu
    rstd = lax.rsqrt(jnp.mean(vc * vc, axis=-1, keepdims=True) + EPS)
    vhat = vc * rstd
    vn = vhat * lg_ref[...] + lb_ref[...]
    row = lax.broadcasted_iota(jnp.int32, (T, T), 0)
    col = lax.broadcasted_iota(jnp.int32, (T, T), 1)
    tril = row >= col
    s = []
    for g in range(G):
        w = jnp.where(tril, ws_ref[g], 0.0)
        s.append(_dot(w, vn[:, g * Dg:(g + 1) * Dg]) + bsb_ref[g])
    return ug, dug, dvg, rstd, vhat, vn, tril, s


def _gmlp_fwd(z, ln_g, ln_b, ws, bsb, GW):
    S = z.shape[0]
    G, T, _ = ws.shape
    Dg = GW // G

    def body(u_ref, v_ref, lg_ref, lb_ref, ws_ref, bsb_ref, ya_ref):
        ug, _, _, _, _, _, _, s = _gmlp_common(u_ref, v_ref, lg_ref, lb_ref, ws_ref, bsb_ref, G, T, Dg)
        for g in range(G):
            sl = slice(g * Dg, (g + 1) * Dg)
            ya_ref[:, sl] = (ug[:, sl] * s[g]).astype(BF16)

    vec = pl.BlockSpec((1, GW), lambda c: (0, 0))
    return pl.pallas_call(
        body, name="gmlp_fwd", grid=(S // T,),
        in_specs=[pl.BlockSpec((T, GW), lambda c: (c, 0)), pl.BlockSpec((T, GW), lambda c: (c, 1)), vec, vec,
                  pl.BlockSpec((G, T, T), lambda c: (0, 0, 0)), pl.BlockSpec((G, T, Dg), lambda c: (0, 0, 0))],
        out_specs=pl.BlockSpec((T, GW), lambda c: (c, 0)), out_shape=jax.ShapeDtypeStruct((S, GW), BF16),
        compiler_params=_cparams(("parallel",)))(z, z, ln_g, ln_b, ws, bsb)


def _gmlp_bwd(z, dya, ln_g, ln_b, ws, bsb, GW):
    S = z.shape[0]
    G, T, _ = ws.shape
    Dg = GW // G
    nc = S // T

    def body(u_ref, v_ref, dya_ref, lg_ref, lb_ref, ws_ref, bsb_ref, dz_ref, dln_ref, dws_ref, dbs_ref, dbs_acc, dvh):
        c = pl.program_id(0)

        @pl.when(c == 0)
        def _():
            dln_ref[...] = jnp.zeros_like(dln_ref)
            dws_ref[...] = jnp.zeros_like(dws_ref)
            dbs_acc[...] = jnp.zeros_like(dbs_acc)

        ug, dug, dvg, rstd, vhat, vn, tril, s = _gmlp_common(u_ref, v_ref, lg_ref, lb_ref, ws_ref, bsb_ref, G, T, Dg)
        dya_v = dya_ref[...]
        for g in range(G):
            sl = slice(g * Dg, (g + 1) * Dg)
            dy_g = dya_v[:, sl]
            dz_ref[:, sl] = (dy_g * s[g] * dug[:, sl]).astype(BF16)
            ds = dy_g * ug[:, sl]
            dbs_acc[g] += ds
            w = jnp.where(tril, ws_ref[g], 0.0)
            dvn_g = _dot(w, ds, _TN)
            dws_ref[g] += jnp.where(tril, _dot(ds, vn[:, sl], _NT), 0.0)
            dln_ref[0:1, sl] += jnp.sum(dvn_g * vhat[:, sl], axis=0, keepdims=True)
            dln_ref[1:2, sl] += jnp.sum(dvn_g, axis=0, keepdims=True)
            dvh[:, sl] = dvn_g * lg_ref[:, sl]
        dvhat = dvh[...]
        m1 = jnp.mean(dvhat, axis=-1, keepdims=True)
        m2 = jnp.mean(dvhat * vhat, axis=-1, keepdims=True)
        dz_ref[:, GW:2 * GW] = (rstd * (dvhat - m1 - vhat * m2) * dvg).astype(BF16)

        @pl.when(c == nc - 1)
        def _():
            for g in range(G):
                dbs_ref[g] = jnp.sum(dbs_acc[g], axis=-1, keepdims=True)

    vec = pl.BlockSpec((1, GW), lambda c: (0, 0))
    return pl.pallas_call(
        body, name="gmlp_bwd", grid=(nc,),
        in_specs=[pl.BlockSpec((T, GW), lambda c: (c, 0)), pl.BlockSpec((T, GW), lambda c: (c, 1)),
                  pl.BlockSpec((T, GW), lambda c: (c, 0)), vec, vec,
                  pl.BlockSpec((G, T, T), lambda c: (0, 0, 0)), pl.BlockSpec((G, T, Dg), lambda c: (0, 0, 0))],
        out_specs=[pl.BlockSpec((T, 2 * GW), lambda c: (c, 0)), pl.BlockSpec((8, GW), lambda c: (0, 0)),
                   pl.BlockSpec((G, T, T), lambda c: (0, 0, 0)), pl.BlockSpec((G, T, 1), lambda c: (0, 0, 0))],
        out_shape=[jax.ShapeDtypeStruct((S, 2 * GW), BF16), jax.ShapeDtypeStruct((8, GW), F32),
                   jax.ShapeDtypeStruct((G, T, T), F32), jax.ShapeDtypeStruct((G, T, 1), F32)],
        scratch_shapes=[pltpu.VMEM((G, T, Dg), F32), pltpu.VMEM((T, GW), F32)],
        compiler_params=_cparams(("arbitrary",)))(z, z, dya, ln_g, ln_b, ws, bsb)


def _hg_common(q_ref, f_ref, hlb_ref):
    C = HG_CHUNK
    a = hlb_ref[...]
    lb = _sigmoid(a[0:1, :] - a[1:2, :])
    sig = _sigmoid(f_ref[...])
    f = lb + (1.0 - lb) * sig
    lf = jnp.log(f)
    kk = 1.0 - f
    q = q_ref[...]
    sq = _sigmoid(q)
    qa = q * sq
    row = lax.broadcasted_iota(jnp.int32, (C, C), 0)
    col = lax.broadcasted_iota(jnp.int32, (C, C), 1)
    tril = row >= col
    b = _ones_dot(tril.astype(BF16), lf)
    bm = b[HG_MID:HG_MID + 1, :]
    bl = b[C - 1:C, :]
    e_b = jnp.exp(b)
    e_qm = jnp.exp(jnp.minimum(b - bm, EXP_CLAMP))
    e_km = jnp.exp(jnp.minimum(bm - b, EXP_CLAMP))
    e_kl = jnp.exp(bl - b)
    return dict(lb=lb, sig=sig, f=f, kk=kk, q=q, sq=sq, qa=qa, tril=tril, e_b=e_b, e_qm=e_qm, e_km=e_km, e_kl=e_kl,
                e_l=jnp.exp(bl), qh=qa * e_b, qt=qa * e_qm, kt=kk * e_km, kh=kk * e_kl)


def _hg_fwd(z, hg_lb, ng, HW):
    S = z.shape[0]
    C, H, dk = HG_CHUNK, HW // HG_DK, HG_DK
    nc = S // C

    def body(q_ref, f_ref, i_ref, og_ref, hlb_ref, ng_ref, yb_ref, o_ref, st_ref, state):
        @pl.when(pl.program_id(0) == 0)
        def _():
            state[...] = jnp.zeros_like(state)

        t = _hg_common(q_ref, f_ref, hlb_ref)
        iv = i_ref[...]
        for h in range(H):
            sl = slice(h * dk, (h + 1) * dk)
            st = state[h]
            st_ref[h] = st
            a = jnp.where(t["tril"], _dot(t["qt"][:, sl], t["kt"][:, sl], _NT), 0.0)
            o_h = _dot(a, iv[:, sl]) + _dot(t["qh"][:, sl], st, _NT)
            state[h] = st * t["e_l"][:, sl] + _dot(iv[:, sl], t["kh"][:, sl], _TN)
            o_ref[:, sl] = o_h
            rr = lax.rsqrt(jnp.mean(o_h * o_h, axis=-1, keepdims=True) + EPS)
            og = og_ref[:, sl]
            yb_ref[:, sl] = (o_h * rr * ng_ref[:, sl] * (og * _sigmoid(og))).astype(BF16)

    def col(k):
        return pl.BlockSpec((C, HW), lambda c: (c, k))

    base = 2
    return pl.pallas_call(
        body, name="hgrn_fwd", grid=(nc,),
        in_specs=[col(base), col(base + 1), col(base + 2), col(base + 3),
                  pl.BlockSpec((2, HW), lambda c: (0, 0)), pl.BlockSpec((1, HW), lambda c: (0, 0))],
        out_specs=[pl.BlockSpec((C, HW), lambda c: (c, 0)), pl.BlockSpec((C, HW), lambda c: (c, 0)),
                   pl.BlockSpec((None, H, dk, dk), lambda c: (c, 0, 0, 0))],
        out_shape=[jax.ShapeDtypeStruct((S, HW), BF16), jax.ShapeDtypeStruct((S, HW), F32),
                   jax.ShapeDtypeStruct((nc, H, dk, dk), F32)],
        scratch_shapes=[pltpu.VMEM((H, dk, dk), F32)],
        compiler_params=_cparams(("arbitrary",)))(z, z, z, z, hg_lb, ng)


def _hg_bwd(z, o, states, dyb, hg_lb, ng, HW, dz_head, dz_tail):
    S = z.shape[0]
    C, H, dk = HG_CHUNK, HW // HG_DK, HG_DK
    nc = S // C
    B0 = dz_head.shape[1]
    DT = dz_tail.shape[2]
    INW = B0 + 4 * HW + 2 * DT

    def body(q_ref, f_ref, i_ref, og_ref, o_ref, st_ref, stn_ref, dyb_ref, hlb_ref, ng_ref, head_ref, tail_ref,
             dzf_ref, dng_ref, dhlb_ref, dstate, cross, dqa_buf, dkk_buf, db_buf, dlb_acc):
        c = pl.program_id(0)
        dzf_ref[:, 0:B0] = head_ref[...]
        dzf_ref[:, B0 + 4 * HW:B0 + 4 * HW + DT] = tail_ref[0]
        dzf_ref[:, B0 + 4 * HW + DT:INW] = tail_ref[1]
        dz_ref = dzf_ref.at[:, B0:B0 + 4 * HW]

        @pl.when(c == 0)
        def _():
            dstate[...] = jnp.zeros_like(dstate)
            dlb_acc[...] = jnp.zeros_like(dlb_acc)
            dng_ref[...] = jnp.zeros_like(dng_ref)

        def r16(v):
            return v.astype(BF16).astype(F32)

        t = _hg_common(q_ref, f_ref, hlb_ref)
        iv = i_ref[...]
        for h in range(H):
            sl = slice(h * dk, (h + 1) * dk)
            o_h, og, dyb_h, ng_h = o_ref[:, sl], og_ref[:, sl], dyb_ref[:, sl], ng_ref[:, sl]
            sg = _sigmoid(og)
            silu_og = og * sg
            rr = lax.rsqrt(jnp.mean(o_h * o_h, axis=-1, keepdims=True) + EPS)
            on = o_h * rr
            dng_ref[0:1, sl] += jnp.sum(dyb_h * on * silu_og, axis=0, keepdims=True)
            dz_ref[:, 3 * HW + h * dk:3 * HW + (h + 1) * dk] = (dyb_h * on * ng_h * (sg * (1.0 + og * (1.0 - sg)))).astype(BF16)
            don = dyb_h * ng_h * silu_og
            do_h = rr * (don - on * jnp.mean(don * on, axis=-1, keepdims=True))

            qt, kt, qh, kh, iv_h = t["qt"][:, sl], t["kt"][:, sl], t["qh"][:, sl], t["kh"][:, sl], iv[:, sl]
            a = jnp.where(t["tril"], _dot(qt, kt, _NT), 0.0)
            da = jnp.where(t["tril"], _dot(do_h, iv_h, _NT), 0.0)
            st, dst = st_ref[h], dstate[h]
            cross[:, sl] = jnp.sum(dst * stn_ref[h], axis=0, keepdims=True)
            dqh = _dot(do_h, st)
            dstate[h] = _dot(do_h, qh, _TN) + dst * t["e_l"][:, sl]
            div = _dot(a, do_h, _TN) + _dot(kh, dst, _NT)
            dkh = _dot(iv_h, dst)
            dqt = _dot(da, kt)
            dkt = _dot(da, qt, _TN)
            dz_ref[:, 2 * HW + h * dk:2 * HW + (h + 1) * dk] = div.astype(BF16)
            dqa_buf[:, sl] = dqh * t["e_b"][:, sl] + dqt * t["e_qm"][:, sl]
            dkk_buf[:, sl] = dkt * t["e_km"][:, sl] + dkh * t["e_kl"][:, sl]
            db_buf[:, sl] = r16(qt) * dqt - r16(kt) * dkt + r16(qh) * dqh - r16(kh) * dkh

        dqa, dkk = dqa_buf[...], dkk_buf[...]
        triu = jnp.logical_not(t["tril"]) | (lax.broadcasted_iota(jnp.int32, (C, C), 0) == lax.broadcasted_iota(jnp.int32, (C, C), 1))
        dlf = _ones_dot(triu.astype(BF16), db_buf[...]) + cross[...]
        df = dlf / t["f"] - dkk
        sig, lb = t["sig"], t["lb"]
        dz_ref[:, HW:2 * HW] = (df * (1.0 - lb) * sig * (1.0 - sig)).astype(BF16)
        dlb_acc[...] += jnp.sum(df * (1.0 - sig), axis=0, keepdims=True)
        q, sq = t["q"], t["sq"]
        dz_ref[:, 0:HW] = (dqa * (sq * (1.0 + q * (1.0 - sq)))).astype(BF16)

        @pl.when(c == nc - 1)
        def _():
            da0 = dlb_acc[...] * lb * (1.0 - lb)
            dhlb_ref[0:1, :] = da0
            dhlb_ref[1:2, :] = -da0

    def col(k):
        return pl.BlockSpec((C, HW), lambda c: (nc - 1 - c, k))

    base = 2
    return pl.pallas_call(
        body, name="hgrn_bwd", grid=(nc,),
        in_specs=[col(base), col(base + 1), col(base + 2), col(base + 3), col(0),
                  pl.BlockSpec((None, H, dk, dk), lambda c: (nc - 1 - c, 0, 0, 0)),
                  pl.BlockSpec((None, H, dk, dk), lambda c: (jnp.minimum(nc - c, nc - 1), 0, 0, 0)), col(0),
                  pl.BlockSpec((2, HW), lambda c: (0, 0)), pl.BlockSpec((1, HW), lambda c: (0, 0)),
                  pl.BlockSpec((C, B0), lambda c: (nc - 1 - c, 0)), pl.BlockSpec((2, C, DT), lambda c: (0, nc - 1 - c, 0))],
        out_specs=[pl.BlockSpec((C, INW), lambda c: (nc - 1 - c, 0)), pl.BlockSpec((8, HW), lambda c: (0, 0)),
                   pl.BlockSpec((2, HW), lambda c: (0, 0))],
        out_shape=[jax.ShapeDtypeStruct((S, INW), BF16), jax.ShapeDtypeStruct((8, HW), F32), jax.ShapeDtypeStruct((2, HW), F32)],
        scratch_shapes=[pltpu.VMEM((H, dk, dk), F32), pltpu.VMEM((1, HW), F32), pltpu.VMEM((C, HW), F32), pltpu.VMEM((C, HW), F32),
                        pltpu.VMEM((C, HW), F32), pltpu.VMEM((1, HW), F32)],
        compiler_params=_cparams(("arbitrary",)))(z, z, z, z, o, states, states, dyb, hg_lb, ng, dz_head, dz_tail)


def _position():
    x, y, c = lax.axis_index("x"), lax.axis_index("y"), lax.axis_index("c")
    return x, y, c, 4 * x + 2 * y + c


def _flip(x, y, c, k):
    return (1 - x if k & 4 else x, 1 - y if k & 2 else y, 1 - c if k & 1 else c)


def _allgather_small(name, v):
    R, L = v.shape

    def body(v_ref, out_ref, send_sems, recv_sems):
        x, y, c, me = _position()
        out_ref[me] = v_ref[...]
        copies = []
        for k in range(1, N_DEV):
            cp = pltpu.make_async_remote_copy(src_ref=v_ref, dst_ref=out_ref.at[me], send_sem=send_sems.at[k - 1],
                                              recv_sem=recv_sems.at[k - 1], device_id=_flip(x, y, c, k), device_id_type=MESH)
            cp.start()
            copies.append(cp)
        for cp in copies:
            cp.wait()

    return pl.pallas_call(
        body, name=name, out_shape=jax.ShapeDtypeStruct((N_DEV, R, L), v.dtype),
        in_specs=[pl.BlockSpec(memory_space=pltpu.VMEM)], out_specs=pl.BlockSpec(memory_space=pltpu.VMEM),
        scratch_shapes=[pltpu.SemaphoreType.DMA((N_DEV - 1,)), pltpu.SemaphoreType.DMA((N_DEV - 1,))],
        compiler_params=pltpu.CompilerParams(vmem_limit_bytes=VMEM_LIMIT),
    )(v)


def _allgather_hbm(name, shards):
    n = len(shards)

    def body(*refs):
        ins, outs = refs[:n], refs[n:2 * n]
        send_sems, recv_sems, local_sems = refs[2 * n:]
        x, y, c, me = _position()
        sibling = (x, y, 1 - c)
        chips = [(1 - x, y), (x, 1 - y), (1 - x, 1 - y)]

        def slot(px, py, pc):
            return 4 * px + 2 * py + pc

        def copy(w, k, block, to, src=None):
            dst = outs[w].at[slot(*block)]
            return pltpu.make_async_remote_copy(src_ref=dst if src is None else src, dst_ref=dst, send_sem=send_sems.at[w, k],
                                                recv_sem=recv_sems.at[w, k], device_id=to, device_id_type=MESH)

        mine, first, passed = [], [], []
        for w in range(n):
            cp = pltpu.make_async_copy(ins[w], outs[w].at[me], local_sems.at[w])
            cp.start()
            mine.append(cp)
            for j, chip in enumerate(chips):
                first.append(copy(w, 1 + j, (x, y, c), (*chip, c), src=ins[w]))
            first.append(copy(w, 0, (x, y, c), sibling, src=ins[w]))
        for cp in first:
            cp.start()
        for w in range(n):
            for j, chip in enumerate(chips):
                copy(w, 1 + j, (*chip, c), (x, y, c)).wait_recv()
                cp = copy(w, 4 + j, (*chip, c), sibling)
                cp.start()
                passed.append(cp)
        for w in range(n):
            copy(w, 0, sibling, (x, y, c)).wait_recv()
            for j, chip in enumerate(chips):
                copy(w, 4 + j, (*chip, 1 - c), (x, y, c)).wait_recv()
        for cp in first + passed:
            cp.wait_send()
        for cp in mine:
            cp.wait()

    hbm = pl.BlockSpec(memory_space=pltpu.HBM)
    return pl.pallas_call(
        body, name=name, out_shape=[jax.ShapeDtypeStruct((N_DEV, *s.shape), s.dtype) for s in shards],
        in_specs=[hbm] * n, out_specs=[hbm] * n,
        scratch_shapes=[pltpu.SemaphoreType.DMA((n, 7)), pltpu.SemaphoreType.DMA((n, 7)), pltpu.SemaphoreType.DMA((n,))],
    )(*shards)


_HBM = pl.BlockSpec(memory_space=pltpu.HBM)
_SEM = pl.BlockSpec(memory_space=pltpu.SEMAPHORE)
_EFFECT = pltpu.SideEffectType.DATAFLOW_SIDE_EFFECTING


def _split_start(name, bufs, n_sems, copies_fn, after=None):
    nb = len(bufs)
    extra = [] if after is None else [after]
    k = nb + len(extra)

    def body(*refs):
        for cp in copies_fn(refs[:nb], refs[k], refs[k + 1]):
            cp.start()
        refs[-1][...] = jnp.zeros_like(refs[-1])

    sems = pltpu.SemaphoreType.DMA((n_sems,))
    res = pl.pallas_call(
        body, name=name,
        out_shape=(sems, sems, *[pltpu.HBM(a.shape, a.dtype) for a in bufs], jax.ShapeDtypeStruct((8, LANES), F32)),
        in_specs=[_HBM] * nb + [pl.BlockSpec(memory_space=pl.ANY)] * len(extra),
        out_specs=(_SEM, _SEM, *[_HBM] * nb, pl.BlockSpec(memory_space=pltpu.VMEM)),
        input_output_aliases={i: 2 + i for i in range(nb)},
        compiler_params=pltpu.CompilerParams(has_side_effects=_EFFECT),
    )(*[pltpu.with_memory_space_constraint(a, pltpu.HBM) for a in bufs], *extra)
    return res[0], res[1], list(res[2:2 + nb]), res[-1]


def _split_wait(name, bufs, send_sems, recv_sems, after, copies_fn):
    nb = len(bufs)

    def body(*refs):
        for cp in copies_fn(refs[:nb], refs[nb], refs[nb + 1]):
            cp.wait_send()
            cp.wait_recv()

    res = pl.pallas_call(
        body, name=name, out_shape=tuple(pltpu.HBM(a.shape, a.dtype) for a in bufs),
        in_specs=[_HBM] * nb + [_SEM, _SEM, pl.BlockSpec(memory_space=pl.ANY)], out_specs=tuple([_HBM] * nb),
        input_output_aliases={i: i for i in range(nb)},
        compiler_params=pltpu.CompilerParams(has_side_effects=_EFFECT),
    )(*bufs, send_sems, recv_sems, after)
    return list(res)


def _split_relay(name, bufs, send_sems, recv_sems, after, wait_fn, n_sems, start_fn):
    nb = len(bufs)

    def body(*refs):
        for cp in wait_fn(refs[:nb], refs[nb], refs[nb + 1]):
            cp.wait_send()
            cp.wait_recv()
        for cp in start_fn(refs[:nb], refs[nb + 3], refs[nb + 4]):
            cp.start()

    sems = pltpu.SemaphoreType.DMA((n_sems,))
    res = pl.pallas_call(
        body, name=name, out_shape=(sems, sems, *[pltpu.HBM(a.shape, a.dtype) for a in bufs]),
        in_specs=[_HBM] * nb + [_SEM, _SEM, pl.BlockSpec(memory_space=pl.ANY)], out_specs=(_SEM, _SEM, *[_HBM] * nb),
        input_output_aliases={i: 2 + i for i in range(nb)},
        compiler_params=pltpu.CompilerParams(has_side_effects=_EFFECT),
    )(*bufs, send_sems, recv_sems, after)
    return res[0], res[1], list(res[2:])


N_CHIP = 4


def _chip_flip(x, y, k):
    return (1 - x if k & 2 else x), (1 - y if k & 1 else y)


def _gather_first_copies(n):
    def copies(bufs, send_sems, recv_sems):
        x, y, c, me = _position()
        out = []
        for w in range(n):
            for k in range(N_CHIP):
                to = (x, y, 1 - c) if k == 0 else (*_chip_flip(x, y, k), c)
                out.append(pltpu.make_async_remote_copy(
                    src_ref=bufs[w], dst_ref=bufs[n + w].at[me], send_sem=send_sems.at[w * N_CHIP + k],
                    recv_sem=recv_sems.at[w * N_CHIP + k], device_id=to, device_id_type=MESH))
        return out
    return copies


def _gather_relay_copies(n):
    def copies(bufs, send_sems, recv_sems):
        x, y, c, _ = _position()
        out = []
        for w in range(n):
            for k in range(1, N_CHIP):
                px, py = _chip_flip(x, y, k)
                blk = bufs[n + w].at[4 * px + 2 * py + c]
                out.append(pltpu.make_async_remote_copy(
                    src_ref=blk, dst_ref=blk, send_sem=send_sems.at[w * (N_CHIP - 1) + k - 1],
                    recv_sem=recv_sems.at[w * (N_CHIP - 1) + k - 1], device_id=(x, y, 1 - c), device_id_type=MESH))
        return out
    return copies


def _to_sibling_copies(n):
    def copies(bufs, send_sems, recv_sems):
        x, y, c, _ = _position()
        out = []
        for w in range(n):
            for q in range(N_CHIP):
                out.append(pltpu.make_async_remote_copy(
                    src_ref=bufs[w].at[2 * q + 1 - c], dst_ref=bufs[n + w].at[q], send_sem=send_sems.at[w * N_CHIP + q],
                    recv_sem=recv_sems.at[w * N_CHIP + q], device_id=(x, y, 1 - c), device_id_type=MESH))
        return out
    return copies


def _to_owner_copies(n):
    def copies(bufs, send_sems, recv_sems):
        x, y, c, _ = _position()
        out = []
        for w in range(n):
            for k in range(1, N_CHIP):
                px, py = (1 - x if k & 2 else x), (1 - y if k & 1 else y)
                out.append(pltpu.make_async_remote_copy(
                    src_ref=bufs[w].at[2 * px + py], dst_ref=bufs[n + w].at[k - 1], send_sem=send_sems.at[w * (N_CHIP - 1) + k - 1],
                    recv_sem=recv_sems.at[w * (N_CHIP - 1) + k - 1], device_id=(px, py, c), device_id_type=MESH))
        return out
    return copies


def _chip_sum(name, stack, landed, c_idx):
    _, R, C = stack.shape
    tr = _tile(R, max(16, 1048576 // C), 16)

    def body(c_ref, a_ref, b_ref, o_ref):
        o_ref[...] = (a_ref[...].astype(F32) + b_ref[...].astype(F32)).astype(o_ref.dtype)

    return pl.pallas_call(
        body, name=name,
        grid_spec=pltpu.PrefetchScalarGridSpec(
            num_scalar_prefetch=1, grid=(N_CHIP, R // tr),
            in_specs=[pl.BlockSpec((None, tr, C), lambda q, i, c_ref: (2 * q + c_ref[0], i, 0)),
                      pl.BlockSpec((None, tr, C), lambda q, i, c_ref: (q, i, 0))],
            out_specs=pl.BlockSpec((None, tr, C), lambda q, i, c_ref: (q, i, 0))),
        out_shape=jax.ShapeDtypeStruct((N_CHIP, R, C), stack.dtype),
        compiler_params=_cparams(("parallel", "parallel")))(c_idx, stack, landed)


def _ada_mod(c16, w):
    _, D = c16.shape
    n = w.shape[1]
    tk = _tile(D, 512)
    nk = D // tk

    def body(c_ref, w_ref, o_ref, ca_ref):
        @pl.when(pl.program_id(0) == 0)
        def _():
            o_ref[...] = jnp.zeros_like(o_ref)

        cv = c_ref[...]
        ca = cv * _sigmoid(cv)
        ca_ref[...] = ca
        o_ref[...] += _dot(ca, w_ref[...])

    return pl.pallas_call(
        body, name="ada_mod", grid=(nk,),
        in_specs=[pl.BlockSpec((16, tk), lambda k: (0, k)), pl.BlockSpec((tk, n), lambda k: (k, 0))],
        out_specs=[pl.BlockSpec((16, n), lambda k: (0, 0)), pl.BlockSpec((16, tk), lambda k: (0, k))],
        out_shape=[jax.ShapeDtypeStruct((16, n), F32), jax.ShapeDtypeStruct((16, D), F32)],
        compiler_params=_cparams(("arbitrary",)))(c16, w)


def _adam_math(w, g, m, v):
    m2 = ADAM_B1 * m + (1.0 - ADAM_B1) * g
    v2 = ADAM_B2 * v + (1.0 - ADAM_B2) * (g * g)
    m_hat = m2 / (1.0 - ADAM_B1 ** ADAM_STEP)
    v_hat = v2 / (1.0 - ADAM_B2 ** ADAM_STEP)
    delta = -ADAM_LR * (m_hat / (jnp.sqrt(v_hat) + ADAM_EPS) + ADAM_WD * w)
    return delta, m2, v2


def _adamw(name, w, m, v, parts):
    R, C = w.shape
    tr = _tile(R, max(16, 393216 // C), 16)
    n_p = len(parts)

    def body(*refs):
        w_ref, m_ref, v_ref = refs[:3]
        g_ref, d_ref, m2_ref, v2_ref = refs[3 + n_p:]
        g = None
        for p_ref in refs[3:3 + n_p]:
            for s in range(p_ref.shape[0]):
                t = p_ref[s].astype(F32)
                g = t if g is None else g + t
        delta, m2, v2 = _adam_math(w_ref[...], g, m_ref[...], v_ref[...])
        g_ref[...] = g
        d_ref[...] = delta
        m2_ref[...] = m2
        v2_ref[...] = v2

    blk = pl.BlockSpec((tr, C), lambda i: (i, 0))
    out = jax.ShapeDtypeStruct((R, C), F32)
    return pl.pallas_call(body, name=name, grid=(R // tr,),
                          in_specs=[blk, blk, blk] + [pl.BlockSpec((a.shape[0], tr, C), lambda i: (0, i, 0)) for a in parts],
                          out_specs=[blk] * 4, out_shape=[out] * 4, compiler_params=_cparams(("parallel",)))(w, m, v, *parts)


def _small_update(gathered, w, m, v, after):
    _, R, L = gathered.shape
    rs = w.shape[0]

    def body(p_ref, w_ref, m_ref, v_ref, after_ref, g_ref, d_ref, m2_ref, v2_ref):
        g = p_ref[0]
        for p in range(1, N_DEV):
            g = g + p_ref[p]
        g_ref[...] = g
        delta, m2, v2 = _adam_math(w_ref[...], g[0:rs, :], m_ref[...], v_ref[...])
        d_ref[...] = delta
        m2_ref[...] = m2
        v2_ref[...] = v2

    vm = pl.BlockSpec(memory_space=pltpu.VMEM)
    sm = jax.ShapeDtypeStruct((rs, L), F32)
    return pl.pallas_call(body, name="small_update", in_specs=[vm] * 4 + [pl.BlockSpec(memory_space=pl.ANY)], out_specs=[vm] * 4,
                          out_shape=[jax.ShapeDtypeStruct((R, L), F32), sm, sm, sm],
                          compiler_params=pltpu.CompilerParams(vmem_limit_bytes=VMEM_LIMIT))(gathered, w, m, v, after)


class _Fetched(dict):
    def __init__(self, fetch):
        super().__init__()
        self.fetch = fetch

    def first(self, key, after):
        self[key] = self.fetch(key, after)
        return self[key]


def _local_step(x, tgt, mod, p, fetch, F, scatter=None):
    S, D = x.shape
    GW, HW = p["ln_g"].shape[1], p["hg_ng"].shape[1]
    G, T, _ = p["ws"].shape
    w = _Fetched(fetch)
    INW = 2 * GW + 4 * HW + 2 * D
    in_loc, br_loc, fi_loc = INW // N_DEV, D // N_DEV, 2 * F // N_DEV
    assert GW == HW and F % fi_loc == 0
    sh1, sc1, gt1, sh2, sc2, gt2 = (mod[:, k * D:(k + 1) * D] for k in range(6))
    bsb = jnp.broadcast_to(p["bs"][:, :, None], (G, T, GW // G))

    tm = _tile(S, 1024, 16)
    tmh = _tile(S, 512, 16)
    tn_in = _tile(in_loc, 1280)
    tn_d = _tile(D, 512)
    tn_br = _tile(br_loc, 512)
    tk_s = S
    tm_w = _tile(D, 1024)
    g_off = 2 * GW + 4 * HW

    h1 = _norm_mod("norm1", x, p["norm1_g"], sc1, sh1)
    z = _mm_nn_stacked("proj_in", h1, w.first("in", h1), tm=tm, tn=tn_in, tk=D)[0]
    ya = _gmlp_fwd(z, p["ln_g"], p["ln_b"], p["ws"], bsb, GW)
    yb, o_hg, states = _hg_fwd(z, p["hg_lb"], p["hg_ng"], HW)
    pa = _mm_nn_stacked("branch_gmlp", ya, w.first("bg", z), tm=tm, tn=tn_br, tk=GW)[0]

    def gates(ga_ref, gb_ref, ba_ref, bb_ref):
        return _sigmoid(ga_ref[...] + ba_ref[...]), _sigmoid(gb_ref[...] + bb_ref[...])

    def gate_specs(tn_):
        o1, o2 = g_off // tn_, (g_off + D) // tn_
        return [pl.BlockSpec((tm, tn_), lambda i, j, k: (i, o1 + j)), pl.BlockSpec((tm, tn_), lambda i, j, k: (i, o2 + j)),
                pl.BlockSpec((1, tn_), lambda i, j, k: (0, j)), pl.BlockSpec((1, tn_), lambda i, j, k: (0, D // tn_ + j))]

    def merge_ep(acc, ex, outs):
        ga, gb = gates(*ex[1:5])
        outs[0][...] = acc
        outs[1][...] = (ga * ex[0][...] + gb * acc).astype(BF16)

    tile_o = pl.BlockSpec((tm, tn_br), lambda i, j, k: (i, j))
    pb, y = _mm_nn_stacked(
        "branch_hg_merge", yb, w.first("bh", z), tm=tm, tn=tn_br, tk=HW, extras=[pa, z, z, p["b_gate"], p["b_gate"]],
        extra_specs=[tile_o, *gate_specs(tn_br)], out_shapes=[jax.ShapeDtypeStruct((S, D), F32), jax.ShapeDtypeStruct((S, D), BF16)],
        out_specs=[tile_o, tile_o], epilogue=merge_ep)

    def resid_ep(acc, ex, outs):
        outs[0][...] = acc
        outs[1][...] = ex[0][...] + ex[1][...] * acc

    def resid_mm(name, a, b, res, gt, tm_):
        K = a.shape[1]
        t_o = pl.BlockSpec((tm_, tn_d), lambda i, j, k: (i, j))
        return _matmul(
            name, a, b, dims=_NN, grid_mnk=(S // tm_, D // tn_d, 1), tiles=(tm_, tn_d),
            a_spec=pl.BlockSpec((tm_, K), lambda i, j, k: (i, 0)), b_spec=pl.BlockSpec((K, tn_d), lambda i, j, k: (0, j)),
            extras=[res, gt], extra_specs=[t_o, pl.BlockSpec((1, tn_d), lambda i, j, k: (0, j))],
            out_shapes=[jax.ShapeDtypeStruct((S, D), F32)] * 2, out_specs=[t_o, t_o], epilogue=resid_ep)

    o1, xm = resid_mm("proj_out", y, w.first("out", z), x, gt1, tm)
    h2 = _norm_mod("norm2", xm, p["norm2_g"], sc2, sh2)
    ab = _mm_nn_stacked("ffn_in", h2, w.first("fi", y), tm=tm, tn=fi_loc, tk=D)[0]
    hf = _swiglu(ab, F, fi_loc)
    o2, x3 = resid_mm("ffn_out", hf, w.first("fo", ab), xm, gt2, tmh)
    dx3, do2, vec_l = _loss_head(x3, tgt, p["final_g"], o2, gt2)

    nf = F // fi_loc

    def dswiglu_ep(acc, ex, outs):
        a, up = ex[0][...], ex[1][...]
        sa = _sigmoid(a)
        outs[0][0] = (acc * up * (sa * (1.0 + a * (1.0 - sa)))).astype(BF16)
        outs[0][1] = (acc * (a * sa)).astype(BF16)

    dab = _matmul(
        "ffn_out_dx", do2, w["fo"], dims=_NT, grid_mnk=(S // tmh, nf, 1), tiles=(tmh, fi_loc),
        a_spec=pl.BlockSpec((tmh, D), lambda i, j, k: (i, 0)), b_spec=pl.BlockSpec((fi_loc, D), lambda i, j, k: (j, 0)),
        extras=[ab, ab], extra_specs=[pl.BlockSpec((tmh, fi_loc), lambda i, j, k: (i, j)), pl.BlockSpec((tmh, fi_loc), lambda i, j, k: (i, j + nf))],
        out_shapes=[jax.ShapeDtypeStruct((2, S, F), BF16)], out_specs=[pl.BlockSpec((2, tmh, fi_loc), lambda i, j, k: (0, i, j))],
        epilogue=dswiglu_ep)[0]
    start = (lambda name, grads: scatter[0](name, grads)) if scatter is not None else (lambda name, grads: None)
    push = (lambda name, after: scatter[1](name, after)) if scatter is not None else (lambda name, after: None)

    def zero(token):
        return 0.0 if token is None else token[0:1, 0:1]

    tm_f = _tile(F, 512)
    g_fo = _mm_tn("ffn_out_dw", hf, do2, pl.BlockSpec((tk_s, D), lambda i, j, k: (k, j)), Mo=F, No=D, S=S, tm=tm_f, tn=D, tk=tk_s)
    g_fi = _mm_tn("ffn_in_dw", h2, dab, pl.BlockSpec((None, tk_s, fi_loc), lambda i, j, k: (j // nf, k, j % nf)),
                  Mo=D, No=2 * F, S=S, tm=tm_w, tn=fi_loc, tk=tk_s, stacked_nloc=fi_loc, after=g_fo)
    t_ffn = start("scatter_ffn", dict(fo=g_fo, fi=g_fi))
    dh2 = _mm_nt_stacked("ffn_in_dx", pl.BlockSpec((None, tmh, fi_loc), lambda i, j, k: (k // nf, i, k % nf)), dab, w["fi"],
                         M=S, tm=tmh, tn=D, tk=fi_loc, after=t_ffn)
    dxm, vec2, do1 = _norm_mod_bwd("norm2_bwd", dh2, xm, p["norm2_g"], sc2, dx3, o1, gt1)
    t_ffn = push("scatter_ffn", dxm)

    def dmerge_ep(acc, ex, outs):
        ga, gb = gates(*ex[2:6])
        outs[0][...] = (acc * ga).astype(BF16)
        outs[1][...] = (acc * gb).astype(BF16)
        outs[2][0] = (acc * ex[0][...] * ga * (1.0 - ga)).astype(BF16)
        outs[2][1] = (acc * ex[1][...] * gb * (1.0 - gb)).astype(BF16)

    t_o = pl.BlockSpec((tm, tn_d), lambda i, j, k: (i, j))
    dpa, dpb, dg2 = _matmul(
        "proj_out_dx", do1, w["out"], dims=_NT, grid_mnk=(S // tm, D // tn_d, 1), tiles=(tm, tn_d),
        a_spec=pl.BlockSpec((tm, D), lambda i, j, k: (i, 0)), b_spec=pl.BlockSpec((tn_d, D), lambda i, j, k: (j, 0)),
        extras=[pa, pb, z, z, p["b_gate"], p["b_gate"]], extra_specs=[t_o, t_o, *gate_specs(tn_d)],
        out_shapes=[jax.ShapeDtypeStruct((S, D), BF16), jax.ShapeDtypeStruct((S, D), BF16), jax.ShapeDtypeStruct((2, S, D), BF16)],
        out_specs=[t_o, t_o, pl.BlockSpec((2, tm, tn_d), lambda i, j, k: (0, i, j))], epilogue=dmerge_ep, after=t_ffn)
    g_out = _mm_tn("proj_out_dw", y, do1, pl.BlockSpec((tk_s, D), lambda i, j, k: (k, j)), Mo=D, No=D, S=S, tm=tn_d, tn=D, tk=tk_s)
    tn_g = _tile(GW, 512)
    b_br = pl.BlockSpec((tk_s, br_loc), lambda i, j, k: (k, j))
    g_bg = _mm_tn("branch_gmlp_dw", ya, dpa, b_br, Mo=GW, No=D, S=S, tm=tn_g, tn=br_loc, tk=tk_s, stacked_nloc=br_loc)
    g_bh = _mm_tn("branch_hg_dw", yb, dpb, b_br, Mo=HW, No=D, S=S, tm=tn_g, tn=br_loc, tk=tk_s, stacked_nloc=br_loc)
    t_mix = start("scatter_mixer", dict(out=g_out, bg=g_bg, bh=g_bh))
    def branch_dx(name, dp, wg):
        flat = jnp.swapaxes(wg, 0, 1).reshape(wg.shape[1], D)
        return _matmul(
            name, dp, flat, dims=_NT, grid_mnk=(S // tm, GW // tn_g, 1), tiles=(tm, tn_g),
            a_spec=pl.BlockSpec((tm, D), lambda i, j, k: (i, 0)), b_spec=pl.BlockSpec((tn_g, D), lambda i, j, k: (j, 0)),
            out_shapes=[jax.ShapeDtypeStruct((S, GW), F32)], out_specs=[pl.BlockSpec((tm, tn_g), lambda i, j, k: (i, j))],
            epilogue=_store(F32), after=t_mix)[0]

    dya = branch_dx("branch_gmlp_dx", dpa, w["bg"])
    dyb = branch_dx("branch_hg_dx", dpb, w["bh"])
    db_gate = _colsum2(dg2)
    dz_gmlp, dln, dws, dbs = _gmlp_bwd(z, dya, p["ln_g"], p["ln_b"], p["ws"], bsb, GW)
    t_mix = push("scatter_mixer", dz_gmlp)
    dz, dng, dhlb = _hg_bwd(z, o_hg, states, dyb, p["hg_lb"], p["hg_ng"] + zero(t_mix), HW, dz_gmlp, dg2)
    g_in = _mm_tn("proj_in_dw", h1, dz, pl.BlockSpec((tk_s, in_loc), lambda i, j, k: (k, j)), Mo=D, No=INW, S=S,
                  tm=tm_w, tn=in_loc, tk=tk_s, stacked_nloc=in_loc)
    t_in = start("scatter_proj_in", dict(w_in=g_in))
    dh1 = _mm_nt_stacked("proj_in_dx", pl.BlockSpec((tmh, in_loc), lambda i, j, k: (i, k)), dz, w["in"], M=S, tm=tmh, tn=D, tk=in_loc,
                         after=t_in)
    dx, vec1 = _norm_mod_bwd("norm1_bwd", dh1, x, p["norm1_g"], sc1, dxm)

    dmod = jnp.concatenate([vec1[0:1], vec1[1:2], vec2[3:4], vec2[0:1], vec2[1:2], vec_l[2:3]], axis=1)
    small = dict(norm1_g=vec1[2:3], b_gate=db_gate.reshape(1, 2 * D), ln_g=dln[0:1], ln_b=dln[1:2], ws=dws, bs=dbs.reshape(G, T),
                 hg_lb=dhlb, hg_ng=dng[0:1], norm2_g=vec2[2:3], final_g=vec_l[1:2], loss=vec_l[0:1, 0:LANES])
    big = dict(w_in=g_in, bg=g_bg, bh=g_bh, out=g_out, fi=g_fi, fo=g_fo)
    return dx, big, small, dmod


_SMALL = ("b_ada", "norm1_g", "b_gate", "ln_g", "ln_b", "ws", "bs", "hg_lb", "hg_ng", "norm2_g", "final_g")


def _pack(parts, rows_mult=8):
    flat = [a.reshape(-1) for a in parts]
    offs, n = [], 0
    for a in flat:
        offs.append(n)
        n += a.shape[0]
    pad = (-n) % (LANES * rows_mult)
    if pad:
        flat.append(jnp.zeros((pad,), F32))
    return jnp.concatenate(flat).reshape(-1, LANES), offs


def kernel(x, c, w_ada, b_ada, norm1_g, w_in, b_gate, gmlp_ln_g, gmlp_ln_b, gmlp_ws, gmlp_bs, hg_lb, hg_norm_g, w_branch_gmlp, w_branch_hg, w_out, norm2_g, w_ffn_in, w_ffn_out, final_norm_g, loss_target, m_w_ada, m_b_ada, m_norm1_g, m_w_in, m_b_gate, m_gmlp_ln_g, m_gmlp_ln_b, m_gmlp_ws, m_gmlp_bs, m_hg_lb, m_hg_norm_g, m_w_branch_gmlp, m_w_branch_hg, m_w_out, m_norm2_g, m_w_ffn_in, m_w_ffn_out, m_final_norm_g, v_w_ada, v_b_ada, v_norm1_g, v_w_in, v_b_gate, v_gmlp_ln_g, v_gmlp_ln_b, v_gmlp_ws, v_gmlp_bs, v_hg_lb, v_hg_norm_g, v_w_branch_gmlp, v_w_branch_hg, v_w_out, v_norm2_g, v_w_ffn_in, v_w_ffn_out, v_final_norm_g):
    S, D = x.shape[1], x.shape[2]
    ada_loc = w_ada.shape[2]
    me = 4 * lax.axis_index("x") + 2 * lax.axis_index("y") + lax.axis_index("c")

    c_all = _allgather_small("gather_c", c.reshape(D // LANES, LANES)).reshape(N_DEV, D)
    mod_cols, c_act = _ada_mod(jnp.pad(c_all, ((0, 16 - N_DEV), (0, 0))), w_ada[0])
    mod_all = _allgather_small("gather_mod", mod_cols[:N_DEV].reshape(-1, LANES)).reshape(N_DEV, N_DEV, ada_loc)
    mod = lax.dynamic_index_in_dim(mod_all, me, axis=1, keepdims=False).reshape(1, N_DEV * ada_loc) + b_ada

    def empty_hbm(shape, dtype):
        return pltpu.with_memory_space_constraint(lax.empty(shape, dtype), pltpu.HBM)

    groups = dict(gather_in=dict(keys=["in"], src=[w_in]), gather_mixer=dict(keys=["bg", "bh", "out"], src=[w_branch_gmlp, w_branch_hg, w_out]),
                  gather_ffn_in=dict(keys=["fi"], src=[w_ffn_in]), gather_ffn_out=dict(keys=["fo"], src=[w_ffn_out]))
    group_of = {}
    prev = mod_all
    for gname, g in groups.items():
        n = len(g["keys"])
        shards = [a[0].astype(BF16) for a in g["src"]]
        outs = [lax.dynamic_update_slice(lax.empty((N_DEV, *s.shape), BF16), s[None], (me, 0, 0)) for s in shards]
        *g["hop1"], prev = _split_start(gname + "_hop1", shards + outs, n * N_CHIP, _gather_first_copies(n), after=prev)
        for k in g["keys"]:
            group_of[k] = gname
    mod = mod + prev[0:1, 0:1]

    def fetch(key, after):
        g = groups[group_of[key]]
        n = len(g["keys"])
        if "done" not in g:
            send_sems, recv_sems, bufs = g["hop1"]
            send_sems, recv_sems, bufs = _split_relay(group_of[key] + "_relay", bufs, send_sems, recv_sems, after,
                                                      _gather_first_copies(n), n * (N_CHIP - 1), _gather_relay_copies(n))
            bufs = _split_wait(group_of[key] + "_hop2_wait", bufs, send_sems, recv_sems, after, _gather_relay_copies(n))
            g["done"] = dict(zip(g["keys"], bufs[n:]))
        arr = g["done"][key]
        return arr.reshape(-1, D) if key in ("out", "fo") else arr

    p = dict(norm1_g=norm1_g, b_gate=b_gate, ln_g=gmlp_ln_g, ln_b=gmlp_ln_b, ws=gmlp_ws[0], bs=gmlp_bs[0], hg_lb=hg_lb,
             hg_ng=hg_norm_g, norm2_g=norm2_g, final_g=final_norm_g.reshape(1, D))

    in_flight = {}
    c_idx = lax.axis_index("c").astype(jnp.int32).reshape(1)
    my_chip = 2 * lax.axis_index("x") + lax.axis_index("y")

    def scatter_start(name, grads):
        keys = list(grads)
        n = len(keys)
        stacks = [grads[k].reshape(N_DEV, -1, grads[k].shape[-1]) for k in keys]
        lands = [empty_hbm((N_CHIP, *g.shape[1:]), g.dtype) for g in stacks]
        send_sems, recv_sems, bufs, token = _split_start(name + "_d2d", stacks + lands, n * N_CHIP, _to_sibling_copies(n))
        in_flight[name] = dict(keys=keys, stage1=(send_sems, recv_sems, bufs))
        return token

    def scatter_push(name, after):
        f = in_flight[name]
        n = len(f["keys"])
        send_sems, recv_sems, bufs = f["stage1"]
        bufs = _split_wait(name + "_d2d_wait", bufs, send_sems, recv_sems, after, _to_sibling_copies(n))
        sums = [_chip_sum(f"{name}_sum_{k}", bufs[i], bufs[n + i], c_idx) for i, k in enumerate(f["keys"])]
        lands = [empty_hbm((N_CHIP - 1, *s.shape[1:]), s.dtype) for s in sums]
        send_sems, recv_sems, bufs, token = _split_start(name + "_ici", sums + lands, n * (N_CHIP - 1), _to_owner_copies(n))
        f["stage2"] = (send_sems, recv_sems, bufs)
        return token

    grad_x, _, small, dmod = _local_step(x[0], loss_target[0], mod, p, fetch, w_ffn_out.shape[1] * N_DEV, (scatter_start, scatter_push))

    small["b_ada"] = dmod
    packed, offs = _pack([small[k] for k in _SMALL] + [small["loss"]])
    gathered = _allgather_small("gather_small", packed)
    wp = dict(p, b_ada=b_ada)
    ms = dict(b_ada=m_b_ada, norm1_g=m_norm1_g, b_gate=m_b_gate, ln_g=m_gmlp_ln_g, ln_b=m_gmlp_ln_b, ws=m_gmlp_ws, bs=m_gmlp_bs,
              hg_lb=m_hg_lb, hg_ng=m_hg_norm_g, norm2_g=m_norm2_g, final_g=m_final_norm_g)
    vs = dict(b_ada=v_b_ada, norm1_g=v_norm1_g, b_gate=v_b_gate, ln_g=v_gmlp_ln_g, ln_b=v_gmlp_ln_b, ws=v_gmlp_ws, bs=v_gmlp_bs,
              hg_lb=v_hg_lb, hg_ng=v_hg_norm_g, norm2_g=v_norm2_g, final_g=v_final_norm_g)
    w_sm, _ = _pack([wp[k] for k in _SMALL])
    m_sm, _ = _pack([ms[k] for k in _SMALL])
    v_sm, _ = _pack([vs[k] for k in _SMALL])
    t_tail = scatter_push("scatter_proj_in", gathered)
    sm_out = _small_update(gathered, w_sm, m_sm, v_sm, t_tail)
    shapes = dict(b_ada=b_ada.shape, norm1_g=norm1_g.shape, b_gate=b_gate.shape, ln_g=gmlp_ln_g.shape, ln_b=gmlp_ln_b.shape,
                  ws=gmlp_ws.shape, bs=gmlp_bs.shape, hg_lb=hg_lb.shape, hg_ng=hg_norm_g.shape, norm2_g=norm2_g.shape,
                  final_g=final_norm_g.shape)

    def unpack(arr, k):
        i = _SMALL.index(k)
        n = math.prod(shapes[k])
        return arr.reshape(-1)[offs[i]:offs[i] + n].reshape(shapes[k])

    loss = sm_out[0].reshape(-1)[offs[len(_SMALL)]]

    dmod_all = gathered.reshape(N_DEV, -1)[:, offs[0]:offs[0] + N_DEV * ada_loc]
    dmod_loc = lax.dynamic_slice_in_dim(dmod_all, me * ada_loc, ada_loc, axis=1)
    ca_t = jnp.pad(c_act[:N_DEV].T, ((0, 0), (0, LANES - N_DEV))).astype(BF16)
    dm_p = jnp.pad(dmod_loc, ((0, LANES - N_DEV), (0, 0))).astype(BF16)
    tm_a = _tile(D, 512)
    g_ada = _matmul(
        "ada_dw", ca_t, dm_p, dims=_NN, grid_mnk=(D // tm_a, 1, 1), tiles=(tm_a, ada_loc),
        a_spec=pl.BlockSpec((tm_a, LANES), lambda i, j, k: (i, 0)), b_spec=pl.BlockSpec((LANES, ada_loc), lambda i, j, k: (0, 0)),
        out_shapes=[jax.ShapeDtypeStruct((1, D, ada_loc), F32)], out_specs=[pl.BlockSpec((None, tm_a, ada_loc), lambda i, j, k: (0, i, 0))],
        epilogue=_store(F32), after=t_tail)[0]

    upd = {"w_ada": _adamw("adamw_w_ada", w_ada[0], m_w_ada[0], v_w_ada[0], [g_ada])}
    big_w = dict(w_in=(w_in, m_w_in, v_w_in, "w_in"), bg=(w_branch_gmlp, m_w_branch_gmlp, v_w_branch_gmlp, "w_branch_gmlp"),
                 bh=(w_branch_hg, m_w_branch_hg, v_w_branch_hg, "w_branch_hg"), out=(w_out, m_w_out, v_w_out, "w_out"),
                 fi=(w_ffn_in, m_w_ffn_in, v_w_ffn_in, "w_ffn_in"), fo=(w_ffn_out, m_w_ffn_out, v_w_ffn_out, "w_ffn_out"))
    after = upd["w_ada"][1]
    for name in ("scatter_ffn", "scatter_mixer", "scatter_proj_in"):
        keys = in_flight[name]["keys"]
        n = len(keys)
        send_sems, recv_sems, bufs = in_flight[name]["stage2"]
        bufs = _split_wait(name + "_ici_wait", bufs, send_sems, recv_sems, after, _to_owner_copies(n))
        for i, k in enumerate(keys):
            wt, mt, vt, out_name = big_w[k]
            own_k = lax.dynamic_index_in_dim(bufs[i], my_chip, axis=0, keepdims=True)
            upd[out_name] = _adamw("adamw_" + out_name, wt[0], mt[0], vt[0], [own_k, bufs[n + i]])
            after = upd[out_name][1]

    order = ("w_ada", "b_ada", "norm1_g", "w_in", "b_gate", "ln_g", "ln_b", "ws", "bs", "hg_lb", "hg_ng", "w_branch_gmlp", "w_branch_hg",
             "w_out", "norm2_g", "w_ffn_in", "w_ffn_out", "final_g")
    outs = [loss, grad_x[None]]
    for idx in range(4):
        for k in order:
            outs.append(upd[k][idx][None] if k in upd else unpack(sm_out[idx], k))
    return tuple(outs)
```

```python
import functools
import math

import jax
import jax.numpy as jnp
from jax import lax
from jax.experimental import pallas as pl
from jax.experimental.pallas import tpu as pltpu

F32 = jnp.float32
BF16 = jnp.bfloat16
N_DEV = 8
EPS = 1e-6
LANES = 128
HG_DK = 128
HG_CHUNK = 64
HG_MID = HG_CHUNK // 2 - 1
EXP_CLAMP = 80.0
VMEM_LIMIT = 48 * 1024 * 1024
ADAM_LR, ADAM_B1, ADAM_B2, ADAM_EPS, ADAM_WD, ADAM_STEP = 0.001, 0.9, 0.999, 1e-08, 0.01, 10
MESH = pl.DeviceIdType.MESH

_NN = (((1,), (0,)), ((), ()))
_NT = (((1,), (1,)), ((), ()))
_TN = (((0,), (0,)), ((), ()))


def _dot(a, b, dims=_NN):
    return lax.dot_general(a.astype(BF16), b.astype(BF16), dims, preferred_element_type=F32)


def _tile(n, target, mult=LANES):
    best = None
    for t in range(mult, min(n, target) + 1, mult):
        if n % t == 0:
            best = t
    return n if best is None else best


def _cparams(sem):
    return pltpu.CompilerParams(dimension_semantics=sem, vmem_limit_bytes=VMEM_LIMIT)


def _sigmoid(x):
    return 1.0 / (1.0 + jnp.exp(-x))


def _gelu_parts(x):
    k0 = math.sqrt(2.0 / math.pi)
    x2 = x * x
    t = jnp.tanh(k0 * (x + 0.044715 * x * x2))
    g = 0.5 * x * (1.0 + t)
    dg = 0.5 * (1.0 + t) + 0.5 * x * (1.0 - t * t) * (k0 * (1.0 + 3.0 * 0.044715 * x2))
    return g, dg


def _split3(x):
    h = x.astype(BF16)
    r = x - h.astype(F32)
    m = r.astype(BF16)
    lo = (r - m.astype(F32)).astype(BF16)
    return h, m, lo


def _ones_dot(mat01, x):
    h, m, lo = _split3(x)
    d = functools.partial(lax.dot_general, dimension_numbers=_NN, preferred_element_type=F32)
    return d(mat01, h) + d(mat01, m) + d(mat01, lo)


def _matmul(name, a, b, *, dims, grid_mnk, tiles, a_spec, b_spec, extras=(), extra_specs=(), out_shapes, out_specs, epilogue, after=None):
    gm, gn, nk = grid_mnk
    tm, tn = tiles
    n_ex, n_out = len(extras), len(out_shapes)
    held = [] if after is None else [after]

    def body(*refs):
        a_ref, b_ref = refs[0], refs[1]
        ex = refs[2:2 + n_ex]
        outs = refs[2 + n_ex + len(held):2 + n_ex + len(held) + n_out]
        if nk == 1:
            epilogue(lax.dot_general(a_ref[...], b_ref[...], dims, preferred_element_type=F32), ex, outs)
            return
        acc = refs[-1]
        k = pl.program_id(2)

        @pl.when(k == 0)
        def _():
            acc[...] = jnp.zeros_like(acc)

        acc[...] += lax.dot_general(a_ref[...], b_ref[...], dims, preferred_element_type=F32)

        @pl.when(k == nk - 1)
        def _():
            epilogue(acc[...], ex, outs)

    return pl.pallas_call(
        body, name=name, grid=(gm, gn, nk), in_specs=[a_spec, b_spec, *extra_specs] + [pl.BlockSpec(memory_space=pl.ANY)] * len(held),
        out_specs=list(out_specs), out_shape=list(out_shapes), scratch_shapes=[] if nk == 1 else [pltpu.VMEM((tm, tn), F32)],
        compiler_params=_cparams(("parallel", "parallel", "arbitrary")),
    )(a, b, *extras, *held)


def _store(dtype):
    def ep(acc, ex, outs):
        outs[0][...] = acc.astype(dtype)
    return ep


def _mm_nn_stacked(name, a, wg, *, tm, tn, tk, out_dtype=F32, extras=(), extra_specs=(), out_shapes=None, out_specs=None, epilogue=None):
    M, K = a.shape
    _, _, nloc = wg.shape
    N = nloc * N_DEV
    q = nloc // tn
    if out_shapes is None:
        out_shapes = [jax.ShapeDtypeStruct((M, N), out_dtype)]
        out_specs = [pl.BlockSpec((tm, tn), lambda i, j, k: (i, j))]
        epilogue = _store(out_dtype)
    return _matmul(
        name, a, wg, dims=_NN, grid_mnk=(M // tm, N // tn, K // tk), tiles=(tm, tn),
        a_spec=pl.BlockSpec((tm, tk), lambda i, j, k: (i, k)),
        b_spec=pl.BlockSpec((None, tk, tn), lambda i, j, k: (j // q, k, j % q)),
        extras=extras, extra_specs=extra_specs, out_shapes=out_shapes, out_specs=out_specs, epilogue=epilogue)


def _mm_nt_stacked(name, a_spec, a, wg, *, M, tm, tn, tk, out_dtype=F32, after=None):
    _, Kw, nloc = wg.shape
    q = nloc // tk
    return _matmul(
        name, a, wg, dims=_NT, grid_mnk=(M // tm, Kw // tn, (nloc * N_DEV) // tk), tiles=(tm, tn),
        a_spec=a_spec, b_spec=pl.BlockSpec((None, tn, tk), lambda i, j, k: (k // q, j, k % q)),
        out_shapes=[jax.ShapeDtypeStruct((M, Kw), out_dtype)], out_specs=[pl.BlockSpec((tm, tn), lambda i, j, k: (i, j))],
        epilogue=_store(out_dtype), after=after)[0]


def _mm_tn(name, a, b, b_spec, *, Mo, No, S, tm, tn, tk, stacked_nloc=None, after=None):
    if stacked_nloc is None:
        out_shape = jax.ShapeDtypeStruct((Mo, No), BF16)
        out_spec = pl.BlockSpec((tm, tn), lambda i, j, k: (i, j))
    else:
        q = stacked_nloc // tn
        out_shape = jax.ShapeDtypeStruct((N_DEV, Mo, stacked_nloc), BF16)
        out_spec = pl.BlockSpec((None, tm, tn), lambda i, j, k: (j // q, i, j % q))
    return _matmul(
        name, a, b, dims=_TN, grid_mnk=(Mo // tm, No // tn, S // tk), tiles=(tm, tn),
        a_spec=pl.BlockSpec((tk, tm), lambda i, j, k: (k, i)), b_spec=b_spec,
        out_shapes=[out_shape], out_specs=[out_spec], epilogue=_store(BF16), after=after)[0]


def _norm_mod(name, x, g, sc, sh):
    S, D = x.shape
    tm = _tile(S, 256, 8)

    def body(x_ref, g_ref, sc_ref, sh_ref, h_ref):
        xv = x_ref[...]
        r = lax.rsqrt(jnp.mean(xv * xv, axis=-1, keepdims=True) + EPS)
        h = (xv * r) * g_ref[...]
        h_ref[...] = (h * (1.0 + sc_ref[...]) + sh_ref[...]).astype(BF16)

    row = pl.BlockSpec((tm, D), lambda i: (i, 0))
    vec = pl.BlockSpec((1, D), lambda i: (0, 0))
    return pl.pallas_call(body, name=name, grid=(S // tm,), in_specs=[row, vec, vec, vec], out_specs=row,
                          out_shape=jax.ShapeDtypeStruct((S, D), BF16), compiler_params=_cparams(("parallel",)))(x, g, sc, sh)


def _norm_mod_bwd(name, dh, x, g, sc, dres, o=None, gt=None):
    S, D = x.shape
    tm = _tile(S, 256, 8)
    gated = o is not None

    def body(*refs):
        if gated:
            dh_ref, x_ref, g_ref, sc_ref, dres_ref, o_ref, gt_ref, dx_ref, vec_ref, do_ref = refs
        else:
            dh_ref, x_ref, g_ref, sc_ref, dres_ref, dx_ref, vec_ref = refs
        i = pl.program_id(0)

        @pl.when(i == 0)
        def _():
            vec_ref[...] = jnp.zeros_like(vec_ref)

        xv, dh_v, gv = x_ref[...], dh_ref[...], g_ref[...]
        r = lax.rsqrt(jnp.mean(xv * xv, axis=-1, keepdims=True) + EPS)
        xn = xv * r
        one_sc = 1.0 + sc_ref[...]
        vec_ref[0:1, :] += jnp.sum(dh_v, axis=0, keepdims=True)
        vec_ref[1:2, :] += jnp.sum(dh_v * (xn * gv), axis=0, keepdims=True)
        vec_ref[2:3, :] += jnp.sum(dh_v * one_sc * xn, axis=0, keepdims=True)
        dxn = dh_v * one_sc * gv
        dx = dres_ref[...] + r * (dxn - xn * jnp.mean(dxn * xn, axis=-1, keepdims=True))
        dx_ref[...] = dx
        if gated:
            vec_ref[3:4, :] += jnp.sum(dx * o_ref[...], axis=0, keepdims=True)
            do_ref[...] = (dx * gt_ref[...]).astype(BF16)

    row = pl.BlockSpec((tm, D), lambda i: (i, 0))
    vec = pl.BlockSpec((1, D), lambda i: (0, 0))
    acc = pl.BlockSpec((8, D), lambda i: (0, 0))
    ins = [dh, x, g, sc, dres] + ([o, gt] if gated else [])
    in_specs = [row, row, vec, vec, row] + ([row, vec] if gated else [])
    out_shape = [jax.ShapeDtypeStruct((S, D), F32), jax.ShapeDtypeStruct((8, D), F32)]
    out_specs = [row, acc]
    if gated:
        out_shape.append(jax.ShapeDtypeStruct((S, D), BF16))
        out_specs.append(row)
    return pl.pallas_call(body, name=name, grid=(S // tm,), in_specs=in_specs, out_specs=out_specs, out_shape=out_shape,
                          compiler_params=_cparams(("arbitrary",)))(*ins)


def _loss_head(x3, tgt, gf, o2, gt2):
    S, D = x3.shape
    tm = _tile(S, 256, 8)

    def body(x_ref, t_ref, g_ref, o_ref, gt_ref, dx_ref, do_ref, vec_ref):
        i = pl.program_id(0)

        @pl.when(i == 0)
        def _():
            vec_ref[...] = jnp.zeros_like(vec_ref)

        xv, gv = x_ref[...], g_ref[...]
        r = lax.rsqrt(jnp.mean(xv * xv, axis=-1, keepdims=True) + EPS)
        xn = xv * r
        e = xn * gv - t_ref[...]
        tok = 0.5 * jnp.mean(e * e, axis=-1, keepdims=True)
        vec_ref[0:1, :] += jnp.broadcast_to(jnp.sum(tok, axis=0, keepdims=True), (1, D))
        dy = e * (1.0 / D)
        vec_ref[1:2, :] += jnp.sum(dy * xn, axis=0, keepdims=True)
        dxn = dy * gv
        dx = r * (dxn - xn * jnp.mean(dxn * xn, axis=-1, keepdims=True))
        dx_ref[...] = dx
        vec_ref[2:3, :] += jnp.sum(dx * o_ref[...], axis=0, keepdims=True)
        do_ref[...] = (dx * gt_ref[...]).astype(BF16)

    row = pl.BlockSpec((tm, D), lambda i: (i, 0))
    vec = pl.BlockSpec((1, D), lambda i: (0, 0))
    return pl.pallas_call(
        body, name="loss_head", grid=(S // tm,), in_specs=[row, row, vec, row, vec],
        out_specs=[row, row, pl.BlockSpec((8, D), lambda i: (0, 0))],
        out_shape=[jax.ShapeDtypeStruct((S, D), F32), jax.ShapeDtypeStruct((S, D), BF16), jax.ShapeDtypeStruct((8, D), F32)],
        compiler_params=_cparams(("arbitrary",)))(x3, tgt, gf, o2, gt2)


def _swiglu(ab, F, tf):
    S = ab.shape[0]
    tm = _tile(S, 512, 8)
    nf = F // tf

    def body(a_ref, u_ref, h_ref):
        a = a_ref[...]
        h_ref[...] = (a * _sigmoid(a) * u_ref[...]).astype(BF16)

    return pl.pallas_call(
        body, name="swiglu", grid=(S // tm, nf),
        in_specs=[pl.BlockSpec((tm, tf), lambda i, j: (i, j)), pl.BlockSpec((tm, tf), lambda i, j: (i, j + nf))],
        out_specs=pl.BlockSpec((tm, tf), lambda i, j: (i, j)), out_shape=jax.ShapeDtypeStruct((S, F), BF16),
        compiler_params=_cparams(("parallel", "parallel")))(ab, ab)


def _colsum2(dg2):
    _, S, D = dg2.shape
    tm = _tile(S, 256, 16)

    def body(x_ref, o_ref):
        @pl.when(pl.program_id(0) == 0)
        def _():
            o_ref[...] = jnp.zeros_like(o_ref)

        o_ref[0:1, :] += jnp.sum(x_ref[0].astype(F32), axis=0, keepdims=True)
        o_ref[1:2, :] += jnp.sum(x_ref[1].astype(F32), axis=0, keepdims=True)

    return pl.pallas_call(body, name="gate_bias_grad", grid=(S // tm,), in_specs=[pl.BlockSpec((2, tm, D), lambda i: (0, i, 0))],
                          out_specs=pl.BlockSpec((2, D), lambda i: (0, 0)), out_shape=jax.ShapeDtypeStruct((2, D), F32),
                          compiler_params=_cparams(("arbitrary",)))(dg2)


def _gmlp_common(u_ref, v_ref, lg_ref, lb_ref, ws_ref, bsb_ref, G, T, Dg):
    ug, dug = _gelu_parts(u_ref[...])
    vg, dvg = _gelu_parts(v_ref[...])
    mu = jnp.mean(vg, axis=-1, keepdims=True)
    vc = vg - mu
    rstd = lax.rsqrt(jnp.mean(vc * vc, axis=-1, keepdims=True) + EPS)
    vhat = vc * rstd
    vn = vhat * lg_ref[...] + lb_ref[...]
    row = lax.broadcasted_iota(jnp.int32, (T, T), 0)
    col = lax.broadcasted_iota(jnp.int32, (T, T), 1)
    tril = row >= col
    s = []
    for g in range(G):
        w = jnp.where(tril, ws_ref[g], 0.0)
        s.append(_dot(w, vn[:, g * Dg:(g + 1) * Dg]) + bsb_ref[g])
    return ug, dug, dvg, rstd, vhat, vn, tril, s


def _gmlp_fwd(z, ln_g, ln_b, ws, bsb, GW):
    S = z.shape[0]
    G, T, _ = ws.shape
    Dg = GW // G

    def body(u_ref, v_ref, lg_ref, lb_ref, ws_ref, bsb_ref, ya_ref):
        ug, _, _, _, _, _, _, s = _gmlp_common(u_ref, v_ref, lg_ref, lb_ref, ws_ref, bsb_ref, G, T, Dg)
        for g in range(G):
            sl = slice(g * Dg, (g + 1) * Dg)
            ya_ref[:, sl] = (ug[:, sl] * s[g]).astype(BF16)

    vec = pl.BlockSpec((1, GW), lambda c: (0, 0))
    return pl.pallas_call(
        body, name="gmlp_fwd", grid=(S // T,),
        in_specs=[pl.BlockSpec((T, GW), lambda c: (c, 0)), pl.BlockSpec((T, GW), lambda c: (c, 1)), vec, vec,
                  pl.BlockSpec((G, T, T), lambda c: (0, 0, 0)), pl.BlockSpec((G, T, Dg), lambda c: (0, 0, 0))],
        out_specs=pl.BlockSpec((T, GW), lambda c: (c, 0)), out_shape=jax.ShapeDtypeStruct((S, GW), BF16),
        compiler_params=_cparams(("parallel",)))(z, z, ln_g, ln_b, ws, bsb)


def _gmlp_bwd(z, dya, ln_g, ln_b, ws, bsb, GW):
    S = z.shape[0]
    G, T, _ = ws.shape
    Dg = GW // G
    nc = S // T

    def body(u_ref, v_ref, dya_ref, lg_ref, lb_ref, ws_ref, bsb_ref, dz_ref, dln_ref, dws_ref, dbs_ref, dbs_acc, dvh):
        c = pl.program_id(0)

        @pl.when(c == 0)
        def _():
            dln_ref[...] = jnp.zeros_like(dln_ref)
            dws_ref[...] = jnp.zeros_like(dws_ref)
            dbs_acc[...] = jnp.zeros_like(dbs_acc)

        ug, dug, dvg, rstd, vhat, vn, tril, s = _gmlp_common(u_ref, v_ref, lg_ref, lb_ref, ws_ref, bsb_ref, G, T, Dg)
        dya_v = dya_ref[...]
        for g in range(G):
            sl = slice(g * Dg, (g + 1) * Dg)
            dy_g = dya_v[:, sl]
            dz_ref[:, sl] = (dy_g * s[g] * dug[:, sl]).astype(BF16)
            ds = dy_g * ug[:, sl]
            dbs_acc[g] += ds
            w = jnp.where(tril, ws_ref[g], 0.0)
            dvn_g = _dot(w, ds, _TN)
            dws_ref[g] += jnp.where(tril, _dot(ds, vn[:, sl], _NT), 0.0)
            dln_ref[0:1, sl] += jnp.sum(dvn_g * vhat[:, sl], axis=0, keepdims=True)
            dln_ref[1:2, sl] += jnp.sum(dvn_g, axis=0, keepdims=True)
            dvh[:, sl] = dvn_g * lg_ref[:, sl]
        dvhat = dvh[...]
        m1 = jnp.mean(dvhat, axis=-1, keepdims=True)
        m2 = jnp.mean(dvhat * vhat, axis=-1, keepdims=True)
        dz_ref[:, GW:2 * GW] = (rstd * (dvhat - m1 - vhat * m2) * dvg).astype(BF16)

        @pl.when(c == nc - 1)
        def _():
            for g in range(G):
                dbs_ref[g] = jnp.sum(dbs_acc[g], axis=-1, keepdims=True)

    vec = pl.BlockSpec((1, GW), lambda c: (0, 0))
    return pl.pallas_call(
        body, name="gmlp_bwd", grid=(nc,),
        in_specs=[pl.BlockSpec((T, GW), lambda c: (c, 0)), pl.BlockSpec((T, GW), lambda c: (c, 1)),
                  pl.BlockSpec((T, GW), lambda c: (c, 0)), vec, vec,
                  pl.BlockSpec((G, T, T), lambda c: (0, 0, 0)), pl.BlockSpec((G, T, Dg), lambda c: (0, 0, 0))],
        out_specs=[pl.BlockSpec((T, 2 * GW), lambda c: (c, 0)), pl.BlockSpec((8, GW), lambda c: (0, 0)),
                   pl.BlockSpec((G, T, T), lambda c: (0, 0, 0)), pl.BlockSpec((G, T, 1), lambda c: (0, 0, 0))],
        out_shape=[jax.ShapeDtypeStruct((S, 2 * GW), BF16), jax.ShapeDtypeStruct((8, GW), F32),
                   jax.ShapeDtypeStruct((G, T, T), F32), jax.ShapeDtypeStruct((G, T, 1), F32)],
        scratch_shapes=[pltpu.VMEM((G, T, Dg), F32), pltpu.VMEM((T, GW), F32)],
        compiler_params=_cparams(("arbitrary",)))(z, z, dya, ln_g, ln_b, ws, bsb)


def _hg_common(q_ref, f_ref, hlb_ref):
    C = HG_CHUNK
    a = hlb_ref[...]
    lb = _sigmoid(a[0:1, :] - a[1:2, :])
    sig = _sigmoid(f_ref[...])
    f = lb + (1.0 - lb) * sig
    lf = jnp.log(f)
    kk = 1.0 - f
    q = q_ref[...]
    sq = _sigmoid(q)
    qa = q * sq
    row = lax.broadcasted_iota(jnp.int32, (C, C), 0)
    col = lax.broadcasted_iota(jnp.int32, (C, C), 1)
    tril = row >= col
    b = _ones_dot(tril.astype(BF16), lf)
    bm = b[HG_MID:HG_MID + 1, :]
    bl = b[C - 1:C, :]
    e_b = jnp.exp(b)
    e_qm = jnp.exp(jnp.minimum(b - bm, EXP_CLAMP))
    e_km = jnp.exp(jnp.minimum(bm - b, EXP_CLAMP))
    e_kl = jnp.exp(bl - b)
    return dict(lb=lb, sig=sig, f=f, kk=kk, q=q, sq=sq, qa=qa, tril=tril, e_b=e_b, e_qm=e_qm, e_km=e_km, e_kl=e_kl,
                e_l=jnp.exp(bl), qh=qa * e_b, qt=qa * e_qm, kt=kk * e_km, kh=kk * e_kl)


def _hg_fwd(z, hg_lb, ng, HW):
    S = z.shape[0]
    C, H, dk = HG_CHUNK, HW // HG_DK, HG_DK
    nc = S // C

    def body(q_ref, f_ref, i_ref, og_ref, hlb_ref, ng_ref, yb_ref, o_ref, st_ref, state):
        @pl.when(pl.program_id(0) == 0)
        def _():
            state[...] = jnp.zeros_like(state)

        t = _hg_common(q_ref, f_ref, hlb_ref)
        iv = i_ref[...]
        for h in range(H):
            sl = slice(h * dk, (h + 1) * dk)
            st = state[h]
            st_ref[h] = st
            a = jnp.where(t["tril"], _dot(t["qt"][:, sl], t["kt"][:, sl], _NT), 0.0)
            o_h = _dot(a, iv[:, sl]) + _dot(t["qh"][:, sl], st, _NT)
            state[h] = st * t["e_l"][:, sl] + _dot(iv[:, sl], t["kh"][:, sl], _TN)
            o_ref[:, sl] = o_h
            rr = lax.rsqrt(jnp.mean(o_h * o_h, axis=-1, keepdims=True) + EPS)
            og = og_ref[:, sl]
            yb_ref[:, sl] = (o_h * rr * ng_ref[:, sl] * (og * _sigmoid(og))).astype(BF16)

    def col(k):
        return pl.BlockSpec((C, HW), lambda c: (c, k))

    base = 2
    return pl.pallas_call(
        body, name="hgrn_fwd", grid=(nc,),
        in_specs=[col(base), col(base + 1), col(base + 2), col(base + 3),
                  pl.BlockSpec((2, HW), lambda c: (0, 0)), pl.BlockSpec((1, HW), lambda c: (0, 0))],
        out_specs=[pl.BlockSpec((C, HW), lambda c: (c, 0)), pl.BlockSpec((C, HW), lambda c: (c, 0)),
                   pl.BlockSpec((None, H, dk, dk), lambda c: (c, 0, 0, 0))],
        out_shape=[jax.ShapeDtypeStruct((S, HW), BF16), jax.ShapeDtypeStruct((S, HW), F32),
                   jax.ShapeDtypeStruct((nc, H, dk, dk), F32)],
        scratch_shapes=[pltpu.VMEM((H, dk, dk), F32)],
        compiler_params=_cparams(("arbitrary",)))(z, z, z, z, hg_lb, ng)


def _hg_bwd(z, o, states, dyb, hg_lb, ng, HW, dz_head, dz_tail):
    S = z.shape[0]
    C, H, dk = HG_CHUNK, HW // HG_DK, HG_DK
    nc = S // C
    B0 = dz_head.shape[1]
    DT = dz_tail.shape[2]
    INW = B0 + 4 * HW + 2 * DT

    def body(q_ref, f_ref, i_ref, og_ref, o_ref, st_ref, stn_ref, dyb_ref, hlb_ref, ng_ref, head_ref, tail_ref,
             dzf_ref, dng_ref, dhlb_ref, dstate, cross, dqa_buf, dkk_buf, db_buf, dlb_acc):
        c = pl.program_id(0)
        dzf_ref[:, 0:B0] = head_ref[...]
        dzf_ref[:, B0 + 4 * HW:B0 + 4 * HW + DT] = tail_ref[0]
        dzf_ref[:, B0 + 4 * HW + DT:INW] = tail_ref[1]
        dz_ref = dzf_ref.at[:, B0:B0 + 4 * HW]

        @pl.when(c == 0)
        def _():
            dstate[...] = jnp.zeros_like(dstate)
            dlb_acc[...] = jnp.zeros_like(dlb_acc)
            dng_ref[...] = jnp.zeros_like(dng_ref)

        def r16(v):
            return v.astype(BF16).astype(F32)

        t = _hg_common(q_ref, f_ref, hlb_ref)
        iv = i_ref[...]
        for h in range(H):
            sl = slice(h * dk, (h + 1) * dk)
            o_h, og, dyb_h, ng_h = o_ref[:, sl], og_ref[:, sl], dyb_ref[:, sl], ng_ref[:, sl]
            sg = _sigmoid(og)
            silu_og = og * sg
            rr = lax.rsqrt(jnp.mean(o_h * o_h, axis=-1, keepdims=True) + EPS)
            on = o_h * rr
            dng_ref[0:1, sl] += jnp.sum(dyb_h * on * silu_og, axis=0, keepdims=True)
            dz_ref[:, 3 * HW + h * dk:3 * HW + (h + 1) * dk] = (dyb_h * on * ng_h * (sg * (1.0 + og * (1.0 - sg)))).astype(BF16)
            don = dyb_h * ng_h * silu_og
            do_h = rr * (don - on * jnp.mean(don * on, axis=-1, keepdims=True))

            qt, kt, qh, kh, iv_h = t["qt"][:, sl], t["kt"][:, sl], t["qh"][:, sl], t["kh"][:, sl], iv[:, sl]
            a = jnp.where(t["tril"], _dot(qt, kt, _NT), 0.0)
            da = jnp.where(t["tril"], _dot(do_h, iv_h, _NT), 0.0)
            st, dst = st_ref[h], dstate[h]
            cross[:, sl] = jnp.sum(dst * stn_ref[h], axis=0, keepdims=True)
            dqh = _dot(do_h, st)
            dstate[h] = _dot(do_h, qh, _TN) + dst * t["e_l"][:, sl]
            div = _dot(a, do_h, _TN) + _dot(kh, dst, _NT)
            dkh = _dot(iv_h, dst)
            dqt = _dot(da, kt)
            dkt = _dot(da, qt, _TN)
            dz_ref[:, 2 * HW + h * dk:2 * HW + (h + 1) * dk] = div.astype(BF16)
            dqa_buf[:, sl] = dqh * t["e_b"][:, sl] + dqt * t["e_qm"][:, sl]
            dkk_buf[:, sl] = dkt * t["e_km"][:, sl] + dkh * t["e_kl"][:, sl]
            db_buf[:, sl] = r16(qt) * dqt - r16(kt) * dkt + r16(qh) * dqh - r16(kh) * dkh

        dqa, dkk = dqa_buf[...], dkk_buf[...]
        triu = jnp.logical_not(t["tril"]) | (lax.broadcasted_iota(jnp.int32, (C, C), 0) == lax.broadcasted_iota(jnp.int32, (C, C), 1))
        dlf = _ones_dot(triu.astype(BF16), db_buf[...]) + cross[...]
        df = dlf / t["f"] - dkk
        sig, lb = t["sig"], t["lb"]
        dz_ref[:, HW:2 * HW] = (df * (1.0 - lb) * sig * (1.0 - sig)).astype(BF16)
        dlb_acc[...] += jnp.sum(df * (1.0 - sig), axis=0, keepdims=True)
        q, sq = t["q"], t["sq"]
        dz_ref[:, 0:HW] = (dqa * (sq * (1.0 + q * (1.0 - sq)))).astype(BF16)

        @pl.when(c == nc - 1)
        def _():
            da0 = dlb_acc[...] * lb * (1.0 - lb)
            dhlb_ref[0:1, :] = da0
            dhlb_ref[1:2, :] = -da0

    def col(k):
        return pl.BlockSpec((C, HW), lambda c: (nc - 1 - c, k))

    base = 2
    return pl.pallas_call(
        body, name="hgrn_bwd", grid=(nc,),
        in_specs=[col(base), col(base + 1), col(base + 2), col(base + 3), col(0),
                  pl.BlockSpec((None, H, dk, dk), lambda c: (nc - 1 - c, 0, 0, 0)),
                  pl.BlockSpec((None, H, dk, dk), lambda c: (jnp.minimum(nc - c, nc - 1), 0, 0, 0)), col(0),
                  pl.BlockSpec((2, HW), lambda c: (0, 0)), pl.BlockSpec((1, HW), lambda c: (0, 0)),
                  pl.BlockSpec((C, B0), lambda c: (nc - 1 - c, 0)), pl.BlockSpec((2, C, DT), lambda c: (0, nc - 1 - c, 0))],
        out_specs=[pl.BlockSpec((C, INW), lambda c: (nc - 1 - c, 0)), pl.BlockSpec((8, HW), lambda c: (0, 0)),
                   pl.BlockSpec((2, HW), lambda c: (0, 0))],
        out_shape=[jax.ShapeDtypeStruct((S, INW), BF16), jax.ShapeDtypeStruct((8, HW), F32), jax.ShapeDtypeStruct((2, HW), F32)],
        scratch_shapes=[pltpu.VMEM((H, dk, dk), F32), pltpu.VMEM((1, HW), F32), pltpu.VMEM((C, HW), F32), pltpu.VMEM((C, HW), F32),
                        pltpu.VMEM((C, HW), F32), pltpu.VMEM((1, HW), F32)],
        compiler_params=_cparams(("arbitrary",)))(z, z, z, z, o, states, states, dyb, hg_lb, ng, dz_head, dz_tail)


def _position():
    x, y, c = lax.axis_index("x"), lax.axis_index("y"), lax.axis_index("c")
    return x, y, c, 4 * x + 2 * y + c


def _flip(x, y, c, k):
    return (1 - x if k & 4 else x, 1 - y if k & 2 else y, 1 - c if k & 1 else c)


def _allgather_small(name, v):
    R, L = v.shape

    def body(v_ref, out_ref, send_sems, recv_sems):
        x, y, c, me = _position()
        out_ref[me] = v_ref[...]
        copies = []
        for k in range(1, N_DEV):
            cp = pltpu.make_async_remote_copy(src_ref=v_ref, dst_ref=out_ref.at[me], send_sem=send_sems.at[k - 1],
                                              recv_sem=recv_sems.at[k - 1], device_id=_flip(x, y, c, k), device_id_type=MESH)
            cp.start()
            copies.append(cp)
        for cp in copies:
            cp.wait()

    return pl.pallas_call(
        body, name=name, out_shape=jax.ShapeDtypeStruct((N_DEV, R, L), v.dtype),
        in_specs=[pl.BlockSpec(memory_space=pltpu.VMEM)], out_specs=pl.BlockSpec(memory_space=pltpu.VMEM),
        scratch_shapes=[pltpu.SemaphoreType.DMA((N_DEV - 1,)), pltpu.SemaphoreType.DMA((N_DEV - 1,))],
        compiler_params=pltpu.CompilerParams(vmem_limit_bytes=VMEM_LIMIT),
    )(v)


def _allgather_hbm(name, shards):
    n = len(shards)

    def body(*refs):
        ins, outs = refs[:n], refs[n:2 * n]
        send_sems, recv_sems, local_sems = refs[2 * n:]
        x, y, c, me = _position()
        sibling = (x, y, 1 - c)
        chips = [(1 - x, y), (x, 1 - y), (1 - x, 1 - y)]

        def slot(px, py, pc):
            return 4 * px + 2 * py + pc

        def copy(w, k, block, to, src=None):
            dst = outs[w].at[slot(*block)]
            return pltpu.make_async_remote_copy(src_ref=dst if src is None else src, dst_ref=dst, send_sem=send_sems.at[w, k],
                                                recv_sem=recv_sems.at[w, k], device_id=to, device_id_type=MESH)

        mine, first, passed = [], [], []
        for w in range(n):
            cp = pltpu.make_async_copy(ins[w], outs[w].at[me], local_sems.at[w])
            cp.start()
            mine.append(cp)
            for j, chip in enumerate(chips):
                first.append(copy(w, 1 + j, (x, y, c), (*chip, c), src=ins[w]))
            first.append(copy(w, 0, (x, y, c), sibling, src=ins[w]))
        for cp in first:
            cp.start()
        for w in range(n):
            for j, chip in enumerate(chips):
                copy(w, 1 + j, (*chip, c), (x, y, c)).wait_recv()
                cp = copy(w, 4 + j, (*chip, c), sibling)
                cp.start()
                passed.append(cp)
        for w in range(n):
            copy(w, 0, sibling, (x, y, c)).wait_recv()
            for j, chip in enumerate(chips):
                copy(w, 4 + j, (*chip, 1 - c), (x, y, c)).wait_recv()
        for cp in first + passed:
            cp.wait_send()
        for cp in mine:
            cp.wait()

    hbm = pl.BlockSpec(memory_space=pltpu.HBM)
    return pl.pallas_call(
        body, name=name, out_shape=[jax.ShapeDtypeStruct((N_DEV, *s.shape), s.dtype) for s in shards],
        in_specs=[hbm] * n, out_specs=[hbm] * n,
        scratch_shapes=[pltpu.SemaphoreType.DMA((n, 7)), pltpu.SemaphoreType.DMA((n, 7)), pltpu.SemaphoreType.DMA((n,))],
    )(*shards)


_HBM = pl.BlockSpec(memory_space=pltpu.HBM)
_SEM = pl.BlockSpec(memory_space=pltpu.SEMAPHORE)
_EFFECT = pltpu.SideEffectType.DATAFLOW_SIDE_EFFECTING


def _split_start(name, bufs, n_sems, copies_fn, after=None):
    nb = len(bufs)
    extra = [] if after is None else [after]
    k = nb + len(extra)

    def body(*refs):
        for cp in copies_fn(refs[:nb], refs[k], refs[k + 1]):
            cp.start()
        refs[-1][...] = jnp.zeros_like(refs[-1])

    sems = pltpu.SemaphoreType.DMA((n_sems,))
    res = pl.pallas_call(
        body, name=name,
        out_shape=(sems, sems, *[pltpu.HBM(a.shape, a.dtype) for a in bufs], jax.ShapeDtypeStruct((8, LANES), F32)),
        in_specs=[_HBM] * nb + [pl.BlockSpec(memory_space=pl.ANY)] * len(extra),
        out_specs=(_SEM, _SEM, *[_HBM] * nb, pl.BlockSpec(memory_space=pltpu.VMEM)),
        input_output_aliases={i: 2 + i for i in range(nb)},
        compiler_params=pltpu.CompilerParams(has_side_effects=_EFFECT),
    )(*[pltpu.with_memory_space_constraint(a, pltpu.HBM) for a in bufs], *extra)
    return res[0], res[1], list(res[2:2 + nb]), res[-1]


def _split_wait(name, bufs, send_sems, recv_sems, after, copies_fn):
    nb = len(bufs)

    def body(*refs):
        for cp in copies_fn(refs[:nb], refs[nb], refs[nb + 1]):
            cp.wait_send()
            cp.wait_recv()

    res = pl.pallas_call(
        body, name=name, out_shape=tuple(pltpu.HBM(a.shape, a.dtype) for a in bufs),
        in_specs=[_HBM] * nb + [_SEM, _SEM, pl.BlockSpec(memory_space=pl.ANY)], out_specs=tuple([_HBM] * nb),
        input_output_aliases={i: i for i in range(nb)},
        compiler_params=pltpu.CompilerParams(has_side_effects=_EFFECT),
    )(*bufs, send_sems, recv_sems, after)
    return list(res)


def _split_relay(name, bufs, send_sems, recv_sems, after, wait_fn, n_sems, start_fn):
    nb = len(bufs)

    def body(*refs):
        for cp in wait_fn(refs[:nb], refs[nb], refs[nb + 1]):
            cp.wait_send()
            cp.wait_recv()
        for cp in start_fn(refs[:nb], refs[nb + 3], refs[nb + 4]):
            cp.start()

    sems = pltpu.SemaphoreType.DMA((n_sems,))
    res = pl.pallas_call(
        body, name=name, out_shape=(sems, sems, *[pltpu.HBM(a.shape, a.dtype) for a in bufs]),
        in_specs=[_HBM] * nb + [_SEM, _SEM, pl.BlockSpec(memory_space=pl.ANY)], out_specs=(_SEM, _SEM, *[_HBM] * nb),
        input_output_aliases={i: 2 + i for i in range(nb)},
        compiler_params=pltpu.CompilerParams(has_side_effects=_EFFECT),
    )(*bufs, send_sems, recv_sems, after)
    return res[0], res[1], list(res[2:])


N_CHIP = 4


def _chip_flip(x, y, k):
    return (1 - x if k & 2 else x), (1 - y if k & 1 else y)


def _gather_first_copies(n):
    def copies(bufs, send_sems, recv_sems):
        x, y, c, me = _position()
        out = []
        for w in range(n):
            for k in range(N_CHIP):
                to = (x, y, 1 - c) if k == 0 else (*_chip_flip(x, y, k), c)
                out.append(pltpu.make_async_remote_copy(
                    src_ref=bufs[w], dst_ref=bufs[n + w].at[me], send_sem=send_sems.at[w * N_CHIP + k],
                    recv_sem=recv_sems.at[w * N_CHIP + k], device_id=to, device_id_type=MESH))
        return out
    return copies


def _gather_relay_copies(n):
    def copies(bufs, send_sems, recv_sems):
        x, y, c, _ = _position()
        out = []
        for w in range(n):
            for k in range(1, N_CHIP):
                px, py = _chip_flip(x, y, k)
                blk = bufs[n + w].at[4 * px + 2 * py + c]
                out.append(pltpu.make_async_remote_copy(
                    src_ref=blk, dst_ref=blk, send_sem=send_sems.at[w * (N_CHIP - 1) + k - 1],
                    recv_sem=recv_sems.at[w * (N_CHIP - 1) + k - 1], device_id=(x, y, 1 - c), device_id_type=MESH))
        return out
    return copies


def _to_sibling_copies(n):
    def copies(bufs, send_sems, recv_sems):
        x, y, c, _ = _position()
        out = []
        for w in range(n):
            for q in range(N_CHIP):
                out.append(pltpu.make_async_remote_copy(
                    src_ref=bufs[w].at[2 * q + 1 - c], dst_ref=bufs[n + w].at[q], send_sem=send_sems.at[w * N_CHIP + q],
                    recv_sem=recv_sems.at[w * N_CHIP + q], device_id=(x, y, 1 - c), device_id_type=MESH))
        return out
    return copies


def _to_owner_copies(n):
    def copies(bufs, send_sems, recv_sems):
        x, y, c, _ = _position()
        out = []
        for w in range(n):
            for k in range(1, N_CHIP):
                px, py = (1 - x if k & 2 else x), (1 - y if k & 1 else y)
                out.append(pltpu.make_async_remote_copy(
                    src_ref=bufs[w].at[2 * px + py], dst_ref=bufs[n + w].at[k - 1], send_sem=send_sems.at[w * (N_CHIP - 1) + k - 1],
                    recv_sem=recv_sems.at[w * (N_CHIP - 1) + k - 1], device_id=(px, py, c), device_id_type=MESH))
        return out
    return copies


def _chip_sum(name, stack, landed, c_idx):
    _, R, C = stack.shape
    tr = _tile(R, max(16, 1048576 // C), 16)

    def body(c_ref, a_ref, b_ref, o_ref):
        o_ref[...] = (a_ref[...].astype(F32) + b_ref[...].astype(F32)).astype(o_ref.dtype)

    return pl.pallas_call(
        body, name=name,
        grid_spec=pltpu.PrefetchScalarGridSpec(
            num_scalar_prefetch=1, grid=(N_CHIP, R // tr),
            in_specs=[pl.BlockSpec((None, tr, C), lambda q, i, c_ref: (2 * q + c_ref[0], i, 0)),
                      pl.BlockSpec((None, tr, C), lambda q, i, c_ref: (q, i, 0))],
            out_specs=pl.BlockSpec((None, tr, C), lambda q, i, c_ref: (q, i, 0))),
        out_shape=jax.ShapeDtypeStruct((N_CHIP, R, C), stack.dtype),
        compiler_params=_cparams(("parallel", "parallel")))(c_idx, stack, landed)


def _ada_mod(c16, w):
    _, D = c16.shape
    n = w.shape[1]
    tk = _tile(D, 512)
    nk = D // tk

    def body(c_ref, w_ref, o_ref, ca_ref):
        @pl.when(pl.program_id(0) == 0)
        def _():
            o_ref[...] = jnp.zeros_like(o_ref)

        cv = c_ref[...]
        ca = cv * _sigmoid(cv)
        ca_ref[...] = ca
        o_ref[...] += _dot(ca, w_ref[...])

    return pl.pallas_call(
        body, name="ada_mod", grid=(nk,),
        in_specs=[pl.BlockSpec((16, tk), lambda k: (0, k)), pl.BlockSpec((tk, n), lambda k: (k, 0))],
        out_specs=[pl.BlockSpec((16, n), lambda k: (0, 0)), pl.BlockSpec((16, tk), lambda k: (0, k))],
        out_shape=[jax.ShapeDtypeStruct((16, n), F32), jax.ShapeDtypeStruct((16, D), F32)],
        compiler_params=_cparams(("arbitrary",)))(c16, w)


def _adam_math(w, g, m, v):
    m2 = ADAM_B1 * m + (1.0 - ADAM_B1) * g
    v2 = ADAM_B2 * v + (1.0 - ADAM_B2) * (g * g)
    m_hat = m2 / (1.0 - ADAM_B1 ** ADAM_STEP)
    v_hat = v2 / (1.0 - ADAM_B2 ** ADAM_STEP)
    delta = -ADAM_LR * (m_hat / (jnp.sqrt(v_hat) + ADAM_EPS) + ADAM_WD * w)
    return delta, m2, v2


def _adamw(name, w, m, v, parts):
    R, C = w.shape
    tr = _tile(R, max(16, 393216 // C), 16)
    sizes = [len(grp) for grp in parts]
    parts = [a for grp in parts for a in grp]
    n_p = len(parts)

    def body(*refs):
        w_ref, m_ref, v_ref = refs[:3]
        g_ref, d_ref, m2_ref, v2_ref = refs[3 + n_p:]
        cols, at = [], 3
        for n_grp in sizes:
            g = None
            for p_ref in refs[at:at + n_grp]:
                for s in range(p_ref.shape[0]):
                    t = p_ref[s].astype(F32)
                    g = t if g is None else g + t
            cols.append(g)
            at += n_grp
        g = cols[0] if len(cols) == 1 else jnp.concatenate(cols, axis=-1)
        delta, m2, v2 = _adam_math(w_ref[...], g, m_ref[...], v_ref[...])
        g_ref[...] = g
        d_ref[...] = delta
        m2_ref[...] = m2
        v2_ref[...] = v2

    blk = pl.BlockSpec((tr, C), lambda i: (i, 0))
    out = jax.ShapeDtypeStruct((R, C), F32)
    return pl.pallas_call(body, name=name, grid=(R // tr,),
                          in_specs=[blk, blk, blk] + [pl.BlockSpec((a.shape[0], tr, a.shape[2]), lambda i: (0, i, 0)) for a in parts],
                          out_specs=[blk] * 4, out_shape=[out] * 4, compiler_params=_cparams(("parallel",)))(w, m, v, *parts)


def _small_update(gathered, w, m, v, after):
    _, R, L = gathered.shape
    rs = w.shape[0]

    def body(p_ref, w_ref, m_ref, v_ref, after_ref, g_ref, d_ref, m2_ref, v2_ref):
        g = p_ref[0]
        for p in range(1, N_DEV):
            g = g + p_ref[p]
        g_ref[...] = g
        delta, m2, v2 = _adam_math(w_ref[...], g[0:rs, :], m_ref[...], v_ref[...])
        d_ref[...] = delta
        m2_ref[...] = m2
        v2_ref[...] = v2

    vm = pl.BlockSpec(memory_space=pltpu.VMEM)
    sm = jax.ShapeDtypeStruct((rs, L), F32)
    return pl.pallas_call(body, name="small_update", in_specs=[vm] * 4 + [pl.BlockSpec(memory_space=pl.ANY)], out_specs=[vm] * 4,
                          out_shape=[jax.ShapeDtypeStruct((R, L), F32), sm, sm, sm],
                          compiler_params=pltpu.CompilerParams(vmem_limit_bytes=VMEM_LIMIT))(gathered, w, m, v, after)


class _Fetched(dict):
    def __init__(self, fetch):
        super().__init__()
        self.fetch = fetch

    def first(self, key, after):
        self[key] = self.fetch(key, after)
        return self[key]


def _local_step(x, tgt, mod, p, fetch, F, scatter=None):
    S, D = x.shape
    GW, HW = p["ln_g"].shape[1], p["hg_ng"].shape[1]
    G, T, _ = p["ws"].shape
    w = _Fetched(fetch)
    INW = 2 * GW + 4 * HW + 2 * D
    in_loc, br_loc, fi_loc = INW // N_DEV, D // N_DEV, 2 * F // N_DEV
    assert GW == HW and F % fi_loc == 0
    sh1, sc1, gt1, sh2, sc2, gt2 = (mod[:, k * D:(k + 1) * D] for k in range(6))
    bsb = jnp.broadcast_to(p["bs"][:, :, None], (G, T, GW // G))

    tm = _tile(S, 1024, 16)
    tmh = _tile(S, 512, 16)
    tn_in = _tile(in_loc, 1280)
    tn_d = _tile(D, 512)
    tn_br = _tile(br_loc, 512)
    tk_s = S
    tm_w = _tile(D, 1024)
    g_off = 2 * GW + 4 * HW

    h1 = _norm_mod("norm1", x, p["norm1_g"], sc1, sh1)
    z = _mm_nn_stacked("proj_in", h1, w.first("in", h1), tm=tm, tn=tn_in, tk=D)[0]
    ya = _gmlp_fwd(z, p["ln_g"], p["ln_b"], p["ws"], bsb, GW)
    yb, o_hg, states = _hg_fwd(z, p["hg_lb"], p["hg_ng"], HW)
    pa = _mm_nn_stacked("branch_gmlp", ya, w.first("bg", z), tm=tm, tn=tn_br, tk=GW)[0]

    def gates(ga_ref, gb_ref, ba_ref, bb_ref):
        return _sigmoid(ga_ref[...] + ba_ref[...]), _sigmoid(gb_ref[...] + bb_ref[...])

    def gate_specs(tn_):
        o1, o2 = g_off // tn_, (g_off + D) // tn_
        return [pl.BlockSpec((tm, tn_), lambda i, j, k: (i, o1 + j)), pl.BlockSpec((tm, tn_), lambda i, j, k: (i, o2 + j)),
                pl.BlockSpec((1, tn_), lambda i, j, k: (0, j)), pl.BlockSpec((1, tn_), lambda i, j, k: (0, D // tn_ + j))]

    def merge_ep(acc, ex, outs):
        ga, gb = gates(*ex[1:5])
        outs[0][...] = acc
        outs[1][...] = (ga * ex[0][...] + gb * acc).astype(BF16)

    tile_o = pl.BlockSpec((tm, tn_br), lambda i, j, k: (i, j))
    pb, y = _mm_nn_stacked(
        "branch_hg_merge", yb, w.first("bh", z), tm=tm, tn=tn_br, tk=HW, extras=[pa, z, z, p["b_gate"], p["b_gate"]],
        extra_specs=[tile_o, *gate_specs(tn_br)], out_shapes=[jax.ShapeDtypeStruct((S, D), F32), jax.ShapeDtypeStruct((S, D), BF16)],
        out_specs=[tile_o, tile_o], epilogue=merge_ep)

    def resid_ep(acc, ex, outs):
        outs[0][...] = acc
        outs[1][...] = ex[0][...] + ex[1][...] * acc

    def resid_mm(name, a, b, res, gt, tm_):
        K = a.shape[1]
        t_o = pl.BlockSpec((tm_, tn_d), lambda i, j, k: (i, j))
        return _matmul(
            name, a, b, dims=_NN, grid_mnk=(S // tm_, D // tn_d, 1), tiles=(tm_, tn_d),
            a_spec=pl.BlockSpec((tm_, K), lambda i, j, k: (i, 0)), b_spec=pl.BlockSpec((K, tn_d), lambda i, j, k: (0, j)),
            extras=[res, gt], extra_specs=[t_o, pl.BlockSpec((1, tn_d), lambda i, j, k: (0, j))],
            out_shapes=[jax.ShapeDtypeStruct((S, D), F32)] * 2, out_specs=[t_o, t_o], epilogue=resid_ep)

    o1, xm = resid_mm("proj_out", y, w.first("out", z), x, gt1, tm)
    h2 = _norm_mod("norm2", xm, p["norm2_g"], sc2, sh2)
    ab = _mm_nn_stacked("ffn_in", h2, w.first("fi", y), tm=tm, tn=fi_loc, tk=D)[0]
    hf = _swiglu(ab, F, fi_loc)
    o2, x3 = resid_mm("ffn_out", hf, w.first("fo", ab), xm, gt2, tmh)
    dx3, do2, vec_l = _loss_head(x3, tgt, p["final_g"], o2, gt2)

    nf = F // fi_loc

    def dswiglu_ep(acc, ex, outs):
        a, up = ex[0][...], ex[1][...]
        sa = _sigmoid(a)
        outs[0][0] = (acc * up * (sa * (1.0 + a * (1.0 - sa)))).astype(BF16)
        outs[0][1] = (acc * (a * sa)).astype(BF16)

    dab = _matmul(
        "ffn_out_dx", do2, w["fo"], dims=_NT, grid_mnk=(S // tmh, nf, 1), tiles=(tmh, fi_loc),
        a_spec=pl.BlockSpec((tmh, D), lambda i, j, k: (i, 0)), b_spec=pl.BlockSpec((fi_loc, D), lambda i, j, k: (j, 0)),
        extras=[ab, ab], extra_specs=[pl.BlockSpec((tmh, fi_loc), lambda i, j, k: (i, j)), pl.BlockSpec((tmh, fi_loc), lambda i, j, k: (i, j + nf))],
        out_shapes=[jax.ShapeDtypeStruct((2, S, F), BF16)], out_specs=[pl.BlockSpec((2, tmh, fi_loc), lambda i, j, k: (0, i, j))],
        epilogue=dswiglu_ep)[0]
    start = (lambda name, grads: scatter[0](name, grads)) if scatter is not None else (lambda name, grads: None)
    push = (lambda name, after: scatter[1](name, after)) if scatter is not None else (lambda name, after: None)

    def zero(token):
        return 0.0 if token is None else token[0:1, 0:1]

    tm_f = _tile(F, 512)
    g_fo = _mm_tn("ffn_out_dw", hf, do2, pl.BlockSpec((tk_s, D), lambda i, j, k: (k, j)), Mo=F, No=D, S=S, tm=tm_f, tn=D, tk=tk_s)
    g_fi = _mm_tn("ffn_in_dw", h2, dab, pl.BlockSpec((None, tk_s, fi_loc), lambda i, j, k: (j // nf, k, j % nf)),
                  Mo=D, No=2 * F, S=S, tm=tm_w, tn=fi_loc, tk=tk_s, stacked_nloc=fi_loc, after=g_fo)
    t_ffn = start("scatter_ffn", dict(fo=g_fo, fi=g_fi))
    dh2 = _mm_nt_stacked("ffn_in_dx", pl.BlockSpec((None, tmh, fi_loc), lambda i, j, k: (k // nf, i, k % nf)), dab, w["fi"],
                         M=S, tm=tmh, tn=D, tk=fi_loc, after=t_ffn)
    dxm, vec2, do1 = _norm_mod_bwd("norm2_bwd", dh2, xm, p["norm2_g"], sc2, dx3, o1, gt1)
    t_ffn = push("scatter_ffn", dxm)

    def dmerge_ep(acc, ex, outs):
        ga, gb = gates(*ex[2:6])
        outs[0][...] = (acc * ga).astype(BF16)
        outs[1][...] = (acc * gb).astype(BF16)
        outs[2][0] = (acc * ex[0][...] * ga * (1.0 - ga)).astype(BF16)
        outs[2][1] = (acc * ex[1][...] * gb * (1.0 - gb)).astype(BF16)

    t_o = pl.BlockSpec((tm, tn_d), lambda i, j, k: (i, j))
    dpa, dpb, dg2 = _matmul(
        "proj_out_dx", do1, w["out"], dims=_NT, grid_mnk=(S // tm, D // tn_d, 1), tiles=(tm, tn_d),
        a_spec=pl.BlockSpec((tm, D), lambda i, j, k: (i, 0)), b_spec=pl.BlockSpec((tn_d, D), lambda i, j, k: (j, 0)),
        extras=[pa, pb, z, z, p["b_gate"], p["b_gate"]], extra_specs=[t_o, t_o, *gate_specs(tn_d)],
        out_shapes=[jax.ShapeDtypeStruct((S, D), BF16), jax.ShapeDtypeStruct((S, D), BF16), jax.ShapeDtypeStruct((2, S, D), BF16)],
        out_specs=[t_o, t_o, pl.BlockSpec((2, tm, tn_d), lambda i, j, k: (0, i, j))], epilogue=dmerge_ep, after=t_ffn)
    g_out = _mm_tn("proj_out_dw", y, do1, pl.BlockSpec((tk_s, D), lambda i, j, k: (k, j)), Mo=D, No=D, S=S, tm=tn_d, tn=D, tk=tk_s)
    tn_g = _tile(GW, 512)
    b_br = pl.BlockSpec((tk_s, br_loc), lambda i, j, k: (k, j))
    g_bg = _mm_tn("branch_gmlp_dw", ya, dpa, b_br, Mo=GW, No=D, S=S, tm=tn_g, tn=br_loc, tk=tk_s, stacked_nloc=br_loc)
    g_bh = _mm_tn("branch_hg_dw", yb, dpb, b_br, Mo=HW, No=D, S=S, tm=tn_g, tn=br_loc, tk=tk_s, stacked_nloc=br_loc)
    t_mix = start("scatter_mixer", dict(out=g_out, bg=g_bg, bh=g_bh))
    def branch_dx(name, dp, wg):
        flat = jnp.swapaxes(wg, 0, 1).reshape(wg.shape[1], D)
        return _matmul(
            name, dp, flat, dims=_NT, grid_mnk=(S // tm, GW // tn_g, 1), tiles=(tm, tn_g),
            a_spec=pl.BlockSpec((tm, D), lambda i, j, k: (i, 0)), b_spec=pl.BlockSpec((tn_g, D), lambda i, j, k: (j, 0)),
            out_shapes=[jax.ShapeDtypeStruct((S, GW), F32)], out_specs=[pl.BlockSpec((tm, tn_g), lambda i, j, k: (i, j))],
            epilogue=_store(F32), after=t_mix)[0]

    dya = branch_dx("branch_gmlp_dx", dpa, w["bg"])
    dyb = branch_dx("branch_hg_dx", dpb, w["bh"])
    db_gate = _colsum2(dg2)
    dz_gmlp, dln, dws, dbs = _gmlp_bwd(z, dya, p["ln_g"], p["ln_b"], p["ws"], bsb, GW)
    t_mix = push("scatter_mixer", dz_gmlp)
    dz, dng, dhlb = _hg_bwd(z, o_hg, states, dyb, p["hg_lb"], p["hg_ng"] + zero(t_mix), HW, dz_gmlp, dg2)
    half = in_loc // 2
    g_in = []
    t_in = None
    for hname, h in (("a", 0), ("b", 1)):
        g_in.append(_mm_tn("proj_in_dw_" + hname, h1, dz, pl.BlockSpec((tk_s, half), lambda i, j, k, h=h: (k, 2 * j + h)), Mo=D, No=INW // 2,
                           S=S, tm=tm_w, tn=half, tk=tk_s, stacked_nloc=half, after=t_in))
        t_in = start("scatter_proj_in_" + hname, {"w_in_" + hname: g_in[-1]})
    t_in = push("scatter_proj_in_a", t_in)
    dh1 = _mm_nt_stacked("proj_in_dx", pl.BlockSpec((tmh, in_loc), lambda i, j, k: (i, k)), dz, w["in"], M=S, tm=tmh, tn=D, tk=in_loc,
                         after=t_in)
    dx, vec1 = _norm_mod_bwd("norm1_bwd", dh1, x, p["norm1_g"], sc1, dxm)

    dmod = jnp.concatenate([vec1[0:1], vec1[1:2], vec2[3:4], vec2[0:1], vec2[1:2], vec_l[2:3]], axis=1)
    small = dict(norm1_g=vec1[2:3], b_gate=db_gate.reshape(1, 2 * D), ln_g=dln[0:1], ln_b=dln[1:2], ws=dws, bs=dbs.reshape(G, T),
                 hg_lb=dhlb, hg_ng=dng[0:1], norm2_g=vec2[2:3], final_g=vec_l[1:2], loss=vec_l[0:1, 0:LANES])
    big = dict(w_in_a=g_in[0], w_in_b=g_in[1], bg=g_bg, bh=g_bh, out=g_out, fi=g_fi, fo=g_fo)
    return dx, big, small, dmod


_SMALL = ("b_ada", "norm1_g", "b_gate", "ln_g", "ln_b", "ws", "bs", "hg_lb", "hg_ng", "norm2_g", "final_g")


def _pack(parts, rows_mult=8):
    flat = [a.reshape(-1) for a in parts]
    offs, n = [], 0
    for a in flat:
        offs.append(n)
        n += a.shape[0]
    pad = (-n) % (LANES * rows_mult)
    if pad:
        flat.append(jnp.zeros((pad,), F32))
    return jnp.concatenate(flat).reshape(-1, LANES), offs


def kernel(x, c, w_ada, b_ada, norm1_g, w_in, b_gate, gmlp_ln_g, gmlp_ln_b, gmlp_ws, gmlp_bs, hg_lb, hg_norm_g, w_branch_gmlp, w_branch_hg, w_out, norm2_g, w_ffn_in, w_ffn_out, final_norm_g, loss_target, m_w_ada, m_b_ada, m_norm1_g, m_w_in, m_b_gate, m_gmlp_ln_g, m_gmlp_ln_b, m_gmlp_ws, m_gmlp_bs, m_hg_lb, m_hg_norm_g, m_w_branch_gmlp, m_w_branch_hg, m_w_out, m_norm2_g, m_w_ffn_in, m_w_ffn_out, m_final_norm_g, v_w_ada, v_b_ada, v_norm1_g, v_w_in, v_b_gate, v_gmlp_ln_g, v_gmlp_ln_b, v_gmlp_ws, v_gmlp_bs, v_hg_lb, v_hg_norm_g, v_w_branch_gmlp, v_w_branch_hg, v_w_out, v_norm2_g, v_w_ffn_in, v_w_ffn_out, v_final_norm_g):
    S, D = x.shape[1], x.shape[2]
    ada_loc = w_ada.shape[2]
    me = 4 * lax.axis_index("x") + 2 * lax.axis_index("y") + lax.axis_index("c")

    c_all = _allgather_small("gather_c", c.reshape(D // LANES, LANES)).reshape(N_DEV, D)
    mod_cols, c_act = _ada_mod(jnp.pad(c_all, ((0, 16 - N_DEV), (0, 0))), w_ada[0])
    mod_all = _allgather_small("gather_mod", mod_cols[:N_DEV].reshape(-1, LANES)).reshape(N_DEV, N_DEV, ada_loc)
    mod = lax.dynamic_index_in_dim(mod_all, me, axis=1, keepdims=False).reshape(1, N_DEV * ada_loc) + b_ada

    def empty_hbm(shape, dtype):
        return pltpu.with_memory_space_constraint(lax.empty(shape, dtype), pltpu.HBM)

    groups = dict(gather_in=dict(keys=["in"], src=[w_in]), gather_mixer=dict(keys=["bg", "bh", "out"], src=[w_branch_gmlp, w_branch_hg, w_out]),
                  gather_ffn_in=dict(keys=["fi"], src=[w_ffn_in]), gather_ffn_out=dict(keys=["fo"], src=[w_ffn_out]))
    group_of = {}
    prev = mod_all
    for gname, g in groups.items():
        n = len(g["keys"])
        shards = [a[0].astype(BF16) for a in g["src"]]
        outs = [lax.dynamic_update_slice(lax.empty((N_DEV, *s.shape), BF16), s[None], (me, 0, 0)) for s in shards]
        *g["hop1"], prev = _split_start(gname + "_hop1", shards + outs, n * N_CHIP, _gather_first_copies(n), after=prev)
        for k in g["keys"]:
            group_of[k] = gname
    mod = mod + prev[0:1, 0:1]

    def fetch(key, after):
        g = groups[group_of[key]]
        n = len(g["keys"])
        if "done" not in g:
            send_sems, recv_sems, bufs = g["hop1"]
            send_sems, recv_sems, bufs = _split_relay(group_of[key] + "_relay", bufs, send_sems, recv_sems, after,
                                                      _gather_first_copies(n), n * (N_CHIP - 1), _gather_relay_copies(n))
            bufs = _split_wait(group_of[key] + "_hop2_wait", bufs, send_sems, recv_sems, after, _gather_relay_copies(n))
            g["done"] = dict(zip(g["keys"], bufs[n:]))
        arr = g["done"][key]
        return arr.reshape(-1, D) if key in ("out", "fo") else arr

    p = dict(norm1_g=norm1_g, b_gate=b_gate, ln_g=gmlp_ln_g, ln_b=gmlp_ln_b, ws=gmlp_ws[0], bs=gmlp_bs[0], hg_lb=hg_lb,
             hg_ng=hg_norm_g, norm2_g=norm2_g, final_g=final_norm_g.reshape(1, D))

    in_flight = {}
    c_idx = lax.axis_index("c").astype(jnp.int32).reshape(1)
    my_chip = 2 * lax.axis_index("x") + lax.axis_index("y")

    def scatter_start(name, grads):
        keys = list(grads)
        n = len(keys)
        stacks = [grads[k].reshape(N_DEV, -1, grads[k].shape[-1]) for k in keys]
        lands = [empty_hbm((N_CHIP, *g.shape[1:]), g.dtype) for g in stacks]
        send_sems, recv_sems, bufs, token = _split_start(name + "_d2d", stacks + lands, n * N_CHIP, _to_sibling_copies(n))
        in_flight[name] = dict(keys=keys, stage1=(send_sems, recv_sems, bufs))
        return token

    def scatter_push(name, after):
        f = in_flight[name]
        n = len(f["keys"])
        send_sems, recv_sems, bufs = f["stage1"]
        bufs = _split_wait(name + "_d2d_wait", bufs, send_sems, recv_sems, after, _to_sibling_copies(n))
        sums = [_chip_sum(f"{name}_sum_{k}", bufs[i], bufs[n + i], c_idx) for i, k in enumerate(f["keys"])]
        lands = [empty_hbm((N_CHIP - 1, *s.shape[1:]), s.dtype) for s in sums]
        send_sems, recv_sems, bufs, token = _split_start(name + "_ici", sums + lands, n * (N_CHIP - 1), _to_owner_copies(n))
        f["stage2"] = (send_sems, recv_sems, bufs)
        return token

    grad_x, _, small, dmod = _local_step(x[0], loss_target[0], mod, p, fetch, w_ffn_out.shape[1] * N_DEV, (scatter_start, scatter_push))

    small["b_ada"] = dmod
    packed, offs = _pack([small[k] for k in _SMALL] + [small["loss"]])
    gathered = _allgather_small("gather_small", packed)
    wp = dict(p, b_ada=b_ada)
    ms = dict(b_ada=m_b_ada, norm1_g=m_norm1_g, b_gate=m_b_gate, ln_g=m_gmlp_ln_g, ln_b=m_gmlp_ln_b, ws=m_gmlp_ws, bs=m_gmlp_bs,
              hg_lb=m_hg_lb, hg_ng=m_hg_norm_g, norm2_g=m_norm2_g, final_g=m_final_norm_g)
    vs = dict(b_ada=v_b_ada, norm1_g=v_norm1_g, b_gate=v_b_gate, ln_g=v_gmlp_ln_g, ln_b=v_gmlp_ln_b, ws=v_gmlp_ws, bs=v_gmlp_bs,
              hg_lb=v_hg_lb, hg_ng=v_hg_norm_g, norm2_g=v_norm2_g, final_g=v_final_norm_g)
    w_sm, _ = _pack([wp[k] for k in _SMALL])
    m_sm, _ = _pack([ms[k] for k in _SMALL])
    v_sm, _ = _pack([vs[k] for k in _SMALL])
    t_tail = scatter_push("scatter_proj_in_b", gathered)
    sm_out = _small_update(gathered, w_sm, m_sm, v_sm, t_tail)
    shapes = dict(b_ada=b_ada.shape, norm1_g=norm1_g.shape, b_gate=b_gate.shape, ln_g=gmlp_ln_g.shape, ln_b=gmlp_ln_b.shape,
                  ws=gmlp_ws.shape, bs=gmlp_bs.shape, hg_lb=hg_lb.shape, hg_ng=hg_norm_g.shape, norm2_g=norm2_g.shape,
                  final_g=final_norm_g.shape)

    def unpack(arr, k):
        i = _SMALL.index(k)
        n = math.prod(shapes[k])
        return arr.reshape(-1)[offs[i]:offs[i] + n].reshape(shapes[k])

    loss = sm_out[0].reshape(-1)[offs[len(_SMALL)]]

    dmod_all = gathered.reshape(N_DEV, -1)[:, offs[0]:offs[0] + N_DEV * ada_loc]
    dmod_loc = lax.dynamic_slice_in_dim(dmod_all, me * ada_loc, ada_loc, axis=1)
    ca_t = jnp.pad(c_act[:N_DEV].T, ((0, 0), (0, LANES - N_DEV))).astype(BF16)
    dm_p = jnp.pad(dmod_loc, ((0, LANES - N_DEV), (0, 0))).astype(BF16)
    tm_a = _tile(D, 512)
    g_ada = _matmul(
        "ada_dw", ca_t, dm_p, dims=_NN, grid_mnk=(D // tm_a, 1, 1), tiles=(tm_a, ada_loc),
        a_spec=pl.BlockSpec((tm_a, LANES), lambda i, j, k: (i, 0)), b_spec=pl.BlockSpec((LANES, ada_loc), lambda i, j, k: (0, 0)),
        out_shapes=[jax.ShapeDtypeStruct((1, D, ada_loc), F32)], out_specs=[pl.BlockSpec((None, tm_a, ada_loc), lambda i, j, k: (0, i, 0))],
        epilogue=_store(F32), after=t_tail)[0]

    upd = {"w_ada": _adamw("adamw_w_ada", w_ada[0], m_w_ada[0], v_w_ada[0], [[g_ada]])}
    big_w = dict(w_in=(w_in, m_w_in, v_w_in, "w_in"), bg=(w_branch_gmlp, m_w_branch_gmlp, v_w_branch_gmlp, "w_branch_gmlp"),
                 bh=(w_branch_hg, m_w_branch_hg, v_w_branch_hg, "w_branch_hg"), out=(w_out, m_w_out, v_w_out, "w_out"),
                 fi=(w_ffn_in, m_w_ffn_in, v_w_ffn_in, "w_ffn_in"), fo=(w_ffn_out, m_w_ffn_out, v_w_ffn_out, "w_ffn_out"))
    after = upd["w_ada"][1]
    arrived = {}
    for name in ("scatter_ffn", "scatter_mixer", "scatter_proj_in_a", "scatter_proj_in_b"):
        keys = in_flight[name]["keys"]
        n = len(keys)
        send_sems, recv_sems, bufs = in_flight[name]["stage2"]
        bufs = _split_wait(name + "_ici_wait", bufs, send_sems, recv_sems, after, _to_owner_copies(n))
        for i, k in enumerate(keys):
            arrived[k] = [lax.dynamic_index_in_dim(bufs[i], my_chip, axis=0, keepdims=True), bufs[n + i]]
            cols = [k] if k in big_w else (["w_in_a", "w_in_b"] if k == "w_in_b" else None)
            if cols is None:
                continue
            wt, mt, vt, out_name = big_w[k if k in big_w else "w_in"]
            upd[out_name] = _adamw("adamw_" + out_name, wt[0], mt[0], vt[0], [arrived[c] for c in cols])
            after = upd[out_name][1]

    order = ("w_ada", "b_ada", "norm1_g", "w_in", "b_gate", "ln_g", "ln_b", "ws", "bs", "hg_lb", "hg_ng", "w_branch_gmlp", "w_branch_hg",
             "w_out", "norm2_g", "w_ffn_in", "w_ffn_out", "final_g")
    outs = [loss, grad_x[None]]
    for idx in range(4):
        for k in order:
            outs.append(upd[k][idx][None] if k in upd else unpack(sm_out[idx], k))
    return tuple(outs)
```

```python
import functools
import math

import jax
import jax.numpy as jnp
from jax import lax
from jax.experimental import pallas as pl
from jax.experimental.pallas import tpu as pltpu

F32 = jnp.float32
BF16 = jnp.bfloat16
N_DEV = 8
EPS = 1e-6
LANES = 128
HG_DK = 128
HG_CHUNK = 64
HG_MID = HG_CHUNK // 2 - 1
EXP_CLAMP = 80.0
VMEM_LIMIT = 48 * 1024 * 1024
ADAM_LR, ADAM_B1, ADAM_B2, ADAM_EPS, ADAM_WD, ADAM_STEP = 0.001, 0.9, 0.999, 1e-08, 0.01, 10
MESH = pl.DeviceIdType.MESH

_NN = (((1,), (0,)), ((), ()))
_NT = (((1,), (1,)), ((), ()))
_TN = (((0,), (0,)), ((), ()))


def _dot(a, b, dims=_NN):
    return lax.dot_general(a.astype(BF16), b.astype(BF16), dims, preferred_element_type=F32)


def _tile(n, target, mult=LANES):
    best = None
    for t in range(mult, min(n, target) + 1, mult):
        if n % t == 0:
            best = t
    return n if best is None else best


def _cparams(sem):
    return pltpu.CompilerParams(dimension_semantics=sem, vmem_limit_bytes=VMEM_LIMIT)


def _sigmoid(x):
    return 1.0 / (1.0 + jnp.exp(-x))


def _gelu_parts(x):
    k0 = math.sqrt(2.0 / math.pi)
    x2 = x * x
    t = jnp.tanh(k0 * (x + 0.044715 * x * x2))
    g = 0.5 * x * (1.0 + t)
    dg = 0.5 * (1.0 + t) + 0.5 * x * (1.0 - t * t) * (k0 * (1.0 + 3.0 * 0.044715 * x2))
    return g, dg


def _split3(x):
    h = x.astype(BF16)
    r = x - h.astype(F32)
    m = r.astype(BF16)
    lo = (r - m.astype(F32)).astype(BF16)
    return h, m, lo


def _ones_dot(mat01, x):
    h, m, lo = _split3(x)
    d = functools.partial(lax.dot_general, dimension_numbers=_NN, preferred_element_type=F32)
    return d(mat01, h) + d(mat01, m) + d(mat01, lo)


def _matmul(name, a, b, *, dims, grid_mnk, tiles, a_spec, b_spec, extras=(), extra_specs=(), out_shapes, out_specs, epilogue, after=None):
    gm, gn, nk = grid_mnk
    tm, tn = tiles
    n_ex, n_out = len(extras), len(out_shapes)
    held = [] if after is None else [after]

    def body(*refs):
        a_ref, b_ref = refs[0], refs[1]
        ex = refs[2:2 + n_ex]
        outs = refs[2 + n_ex + len(held):2 + n_ex + len(held) + n_out]
        if nk == 1:
            epilogue(lax.dot_general(a_ref[...], b_ref[...], dims, preferred_element_type=F32), ex, outs)
            return
        acc = refs[-1]
        k = pl.program_id(2)

        @pl.when(k == 0)
        def _():
            acc[...] = jnp.zeros_like(acc)

        acc[...] += lax.dot_general(a_ref[...], b_ref[...], dims, preferred_element_type=F32)

        @pl.when(k == nk - 1)
        def _():
            epilogue(acc[...], ex, outs)

    return pl.pallas_call(
        body, name=name, grid=(gm, gn, nk), in_specs=[a_spec, b_spec, *extra_specs] + [pl.BlockSpec(memory_space=pl.ANY)] * len(held),
        out_specs=list(out_specs), out_shape=list(out_shapes), scratch_shapes=[] if nk == 1 else [pltpu.VMEM((tm, tn), F32)],
        compiler_params=_cparams(("parallel", "parallel", "arbitrary")),
    )(a, b, *extras, *held)


def _store(dtype):
    def ep(acc, ex, outs):
        outs[0][...] = acc.astype(dtype)
    return ep


def _mm_nn_stacked(name, a, wg, *, tm, tn, tk, out_dtype=F32, extras=(), extra_specs=(), out_shapes=None, out_specs=None, epilogue=None):
    M, K = a.shape
    _, _, nloc = wg.shape
    N = nloc * N_DEV
    q = nloc // tn
    if out_shapes is None:
        out_shapes = [jax.ShapeDtypeStruct((M, N), out_dtype)]
        out_specs = [pl.BlockSpec((tm, tn), lambda i, j, k: (i, j))]
        epilogue = _store(out_dtype)
    return _matmul(
        name, a, wg, dims=_NN, grid_mnk=(M // tm, N // tn, K // tk), tiles=(tm, tn),
        a_spec=pl.BlockSpec((tm, tk), lambda i, j, k: (i, k)),
        b_spec=pl.BlockSpec((None, tk, tn), lambda i, j, k: (j // q, k, j % q)),
        extras=extras, extra_specs=extra_specs, out_shapes=out_shapes, out_specs=out_specs, epilogue=epilogue)


def _mm_nt_stacked(name, a_spec, a, wg, *, M, tm, tn, tk, out_dtype=F32, after=None):
    _, Kw, nloc = wg.shape
    q = nloc // tk
    return _matmul(
        name, a, wg, dims=_NT, grid_mnk=(M // tm, Kw // tn, (nloc * N_DEV) // tk), tiles=(tm, tn),
        a_spec=a_spec, b_spec=pl.BlockSpec((None, tn, tk), lambda i, j, k: (k // q, j, k % q)),
        out_shapes=[jax.ShapeDtypeStruct((M, Kw), out_dtype)], out_specs=[pl.BlockSpec((tm, tn), lambda i, j, k: (i, j))],
        epilogue=_store(out_dtype), after=after)[0]


def _mm_tn(name, a, b, b_spec, *, Mo, No, S, tm, tn, tk, stacked_nloc=None, after=None):
    if stacked_nloc is None:
        out_shape = jax.ShapeDtypeStruct((Mo, No), BF16)
        out_spec = pl.BlockSpec((tm, tn), lambda i, j, k: (i, j))
    else:
        q = stacked_nloc // tn
        out_shape = jax.ShapeDtypeStruct((N_DEV, Mo, stacked_nloc), BF16)
        out_spec = pl.BlockSpec((None, tm, tn), lambda i, j, k: (j // q, i, j % q))
    return _matmul(
        name, a, b, dims=_TN, grid_mnk=(Mo // tm, No // tn, S // tk), tiles=(tm, tn),
        a_spec=pl.BlockSpec((tk, tm), lambda i, j, k: (k, i)), b_spec=b_spec,
        out_shapes=[out_shape], out_specs=[out_spec], epilogue=_store(BF16), after=after)[0]


def _norm_mod(name, x, g, sc, sh):
    S, D = x.shape
    tm = _tile(S, 256, 8)

    def body(x_ref, g_ref, sc_ref, sh_ref, h_ref):
        xv = x_ref[...]
        r = lax.rsqrt(jnp.mean(xv * xv, axis=-1, keepdims=True) + EPS)
        h = (xv * r) * g_ref[...]
        h_ref[...] = (h * (1.0 + sc_ref[...]) + sh_ref[...]).astype(BF16)

    row = pl.BlockSpec((tm, D), lambda i: (i, 0))
    vec = pl.BlockSpec((1, D), lambda i: (0, 0))
    return pl.pallas_call(body, name=name, grid=(S // tm,), in_specs=[row, vec, vec, vec], out_specs=row,
                          out_shape=jax.ShapeDtypeStruct((S, D), BF16), compiler_params=_cparams(("parallel",)))(x, g, sc, sh)


def _norm_mod_bwd(name, dh, x, g, sc, dres, o=None, gt=None):
    S, D = x.shape
    tm = _tile(S, 256, 8)
    gated = o is not None

    def body(*refs):
        if gated:
            dh_ref, x_ref, g_ref, sc_ref, dres_ref, o_ref, gt_ref, dx_ref, vec_ref, do_ref = refs
        else:
            dh_ref, x_ref, g_ref, sc_ref, dres_ref, dx_ref, vec_ref = refs
        i = pl.program_id(0)

        @pl.when(i == 0)
        def _():
            vec_ref[...] = jnp.zeros_like(vec_ref)

        xv, dh_v, gv = x_ref[...], dh_ref[...], g_ref[...]
        r = lax.rsqrt(jnp.mean(xv * xv, axis=-1, keepdims=True) + EPS)
        xn = xv * r
        one_sc = 1.0 + sc_ref[...]
        vec_ref[0:1, :] += jnp.sum(dh_v, axis=0, keepdims=True)
        vec_ref[1:2, :] += jnp.sum(dh_v * (xn * gv), axis=0, keepdims=True)
        vec_ref[2:3, :] += jnp.sum(dh_v * one_sc * xn, axis=0, keepdims=True)
        dxn = dh_v * one_sc * gv
        dx = dres_ref[...] + r * (dxn - xn * jnp.mean(dxn * xn, axis=-1, keepdims=True))
        dx_ref[...] = dx
        if gated:
            vec_ref[3:4, :] += jnp.sum(dx * o_ref[...], axis=0, keepdims=True)
            do_ref[...] = (dx * gt_ref[...]).astype(BF16)

    row = pl.BlockSpec((tm, D), lambda i: (i, 0))
    vec = pl.BlockSpec((1, D), lambda i: (0, 0))
    acc = pl.BlockSpec((8, D), lambda i: (0, 0))
    ins = [dh, x, g, sc, dres] + ([o, gt] if gated else [])
    in_specs = [row, row, vec, vec, row] + ([row, vec] if gated else [])
    out_shape = [jax.ShapeDtypeStruct((S, D), F32), jax.ShapeDtypeStruct((8, D), F32)]
    out_specs = [row, acc]
    if gated:
        out_shape.append(jax.ShapeDtypeStruct((S, D), BF16))
        out_specs.append(row)
    return pl.pallas_call(body, name=name, grid=(S // tm,), in_specs=in_specs, out_specs=out_specs, out_shape=out_shape,
                          compiler_params=_cparams(("arbitrary",)))(*ins)


def _loss_head(x3, tgt, gf, o2, gt2):
    S, D = x3.shape
    tm = _tile(S, 256, 8)

    def body(x_ref, t_ref, g_ref, o_ref, gt_ref, dx_ref, do_ref, vec_ref):
        i = pl.program_id(0)

        @pl.when(i == 0)
        def _():
            vec_ref[...] = jnp.zeros_like(vec_ref)

        xv, gv = x_ref[...], g_ref[...]
        r = lax.rsqrt(jnp.mean(xv * xv, axis=-1, keepdims=True) + EPS)
        xn = xv * r
        e = xn * gv - t_ref[...]
        tok = 0.5 * jnp.mean(e * e, axis=-1, keepdims=True)
        vec_ref[0:1, :] += jnp.broadcast_to(jnp.sum(tok, axis=0, keepdims=True), (1, D))
        dy = e * (1.0 / D)
        vec_ref[1:2, :] += jnp.sum(dy * xn, axis=0, keepdims=True)
        dxn = dy * gv
        dx = r * (dxn - xn * jnp.mean(dxn * xn, axis=-1, keepdims=True))
        dx_ref[...] = dx
        vec_ref[2:3, :] += jnp.sum(dx * o_ref[...], axis=0, keepdims=True)
        do_ref[...] = (dx * gt_ref[...]).astype(BF16)

    row = pl.BlockSpec((tm, D), lambda i: (i, 0))
    vec = pl.BlockSpec((1, D), lambda i: (0, 0))
    return pl.pallas_call(
        body, name="loss_head", grid=(S // tm,), in_specs=[row, row, vec, row, vec],
        out_specs=[row, row, pl.BlockSpec((8, D), lambda i: (0, 0))],
        out_shape=[jax.ShapeDtypeStruct((S, D), F32), jax.ShapeDtypeStruct((S, D), BF16), jax.ShapeDtypeStruct((8, D), F32)],
        compiler_params=_cparams(("arbitrary",)))(x3, tgt, gf, o2, gt2)


def _ffn_in_swiglu(h, wg):
    S, D = h.shape
    _, _, tf = wg.shape
    nf = N_DEV // 2
    F = nf * tf
    tm = _tile(S, 256, 16)

    def body(h_ref, wa_ref, wu_ref, hf_ref, fac_ref):
        hv = h_ref[...]
        a = lax.dot_general(hv, wa_ref[...], _NN, preferred_element_type=F32)
        up = lax.dot_general(hv, wu_ref[...], _NN, preferred_element_type=F32)
        sa = _sigmoid(a)
        silu = a * sa
        hf_ref[...] = (silu * up).astype(BF16)
        fac_ref[0] = (up * (sa * (1.0 + a * (1.0 - sa)))).astype(BF16)
        fac_ref[1] = silu.astype(BF16)

    return pl.pallas_call(
        body, name="ffn_in_swiglu", grid=(nf, S // tm),
        in_specs=[pl.BlockSpec((tm, D), lambda j, i: (i, 0)), pl.BlockSpec((None, D, tf), lambda j, i: (j, 0, 0)),
                  pl.BlockSpec((None, D, tf), lambda j, i: (j + nf, 0, 0))],
        out_specs=[pl.BlockSpec((tm, tf), lambda j, i: (i, j)), pl.BlockSpec((2, tm, tf), lambda j, i: (0, i, j))],
        out_shape=[jax.ShapeDtypeStruct((S, F), BF16), jax.ShapeDtypeStruct((2, S, F), BF16)],
        compiler_params=_cparams(("parallel", "parallel")))(h, wg, wg)


def _colsum2(dg2):
    _, S, D = dg2.shape
    tm = _tile(S, 256, 16)

    def body(x_ref, o_ref):
        @pl.when(pl.program_id(0) == 0)
        def _():
            o_ref[...] = jnp.zeros_like(o_ref)

        o_ref[0:1, :] += jnp.sum(x_ref[0].astype(F32), axis=0, keepdims=True)
        o_ref[1:2, :] += jnp.sum(x_ref[1].astype(F32), axis=0, keepdims=True)

    return pl.pallas_call(body, name="gate_bias_grad", grid=(S // tm,), in_specs=[pl.BlockSpec((2, tm, D), lambda i: (0, i, 0))],
                          out_specs=pl.BlockSpec((2, D), lambda i: (0, 0)), out_shape=jax.ShapeDtypeStruct((2, D), F32),
                          compiler_params=_cparams(("arbitrary",)))(dg2)


def _gmlp_common(u_ref, v_ref, lg_ref, lb_ref, ws_ref, bsb_ref, G, T, Dg):
    ug, dug = _gelu_parts(u_ref[...])
    vg, dvg = _gelu_parts(v_ref[...])
    mu = jnp.mean(vg, axis=-1, keepdims=True)
    vc = vg - mu
    rstd = lax.rsqrt(jnp.mean(vc * vc, axis=-1, keepdims=True) + EPS)
    vhat = vc * rstd
    vn = vhat * lg_ref[...] + lb_ref[...]
    row = lax.broadcasted_iota(jnp.int32, (T, T), 0)
    col = lax.broadcasted_iota(jnp.int32, (T, T), 1)
    tril = row >= col
    s = []
    for g in range(G):
        w = jnp.where(tril, ws_ref[g], 0.0)
        s.append(_dot(w, vn[:, g * Dg:(g + 1) * Dg]) + bsb_ref[g])
    return ug, dug, dvg, rstd, vhat, vn, tril, s


def _gmlp_fwd(z, ln_g, ln_b, ws, bsb, GW):
    S = z.shape[0]
    G, T, _ = ws.shape
    Dg = GW // G

    def body(u_ref, v_ref, lg_ref, lb_ref, ws_ref, bsb_ref, ya_ref):
        ug, _, _, _, _, _, _, s = _gmlp_common(u_ref, v_ref, lg_ref, lb_ref, ws_ref, bsb_ref, G, T, Dg)
        for g in range(G):
            sl = slice(g * Dg, (g + 1) * Dg)
            ya_ref[:, sl] = (ug[:, sl] * s[g]).astype(BF16)

    vec = pl.BlockSpec((1, GW), lambda c: (0, 0))
    return pl.pallas_call(
        body, name="gmlp_fwd", grid=(S // T,),
        in_specs=[pl.BlockSpec((T, GW), lambda c: (c, 0)), pl.BlockSpec((T, GW), lambda c: (c, 1)), vec, vec,
                  pl.BlockSpec((G, T, T), lambda c: (0, 0, 0)), pl.BlockSpec((G, T, Dg), lambda c: (0, 0, 0))],
        out_specs=pl.BlockSpec((T, GW), lambda c: (c, 0)), out_shape=jax.ShapeDtypeStruct((S, GW), BF16),
        compiler_params=_cparams(("parallel",)))(z, z, ln_g, ln_b, ws, bsb)


def _gmlp_bwd(z, dya, ln_g, ln_b, ws, bsb, GW):
    S = z.shape[0]
    G, T, _ = ws.shape
    Dg = GW // G
    nc = S // T

    def body(u_ref, v_ref, dya_ref, lg_ref, lb_ref, ws_ref, bsb_ref, dz_ref, dln_ref, dws_ref, dbs_ref, dbs_acc, dvh):
        c = pl.program_id(0)

        @pl.when(c == 0)
        def _():
            dln_ref[...] = jnp.zeros_like(dln_ref)
            dws_ref[...] = jnp.zeros_like(dws_ref)
            dbs_acc[...] = jnp.zeros_like(dbs_acc)

        ug, dug, dvg, rstd, vhat, vn, tril, s = _gmlp_common(u_ref, v_ref, lg_ref, lb_ref, ws_ref, bsb_ref, G, T, Dg)
        dya_v = dya_ref[...]
        for g in range(G):
            sl = slice(g * Dg, (g + 1) * Dg)
            dy_g = dya_v[:, sl]
            dz_ref[:, sl] = (dy_g * s[g] * dug[:, sl]).astype(BF16)
            ds = dy_g * ug[:, sl]
            dbs_acc[g] += ds
            w = jnp.where(tril, ws_ref[g], 0.0)
            dvn_g = _dot(w, ds, _TN)
            dws_ref[g] += jnp.where(tril, _dot(ds, vn[:, sl], _NT), 0.0)
            dln_ref[0:1, sl] += jnp.sum(dvn_g * vhat[:, sl], axis=0, keepdims=True)
            dln_ref[1:2, sl] += jnp.sum(dvn_g, axis=0, keepdims=True)
            dvh[:, sl] = dvn_g * lg_ref[:, sl]
        dvhat = dvh[...]
        m1 = jnp.mean(dvhat, axis=-1, keepdims=True)
        m2 = jnp.mean(dvhat * vhat, axis=-1, keepdims=True)
        dz_ref[:, GW:2 * GW] = (rstd * (dvhat - m1 - vhat * m2) * dvg).astype(BF16)

        @pl.when(c == nc - 1)
        def _():
            for g in range(G):
                dbs_ref[g] = jnp.sum(dbs_acc[g], axis=-1, keepdims=True)

    vec = pl.BlockSpec((1, GW), lambda c: (0, 0))
    return pl.pallas_call(
        body, name="gmlp_bwd", grid=(nc,),
        in_specs=[pl.BlockSpec((T, GW), lambda c: (c, 0)), pl.BlockSpec((T, GW), lambda c: (c, 1)),
                  pl.BlockSpec((T, GW), lambda c: (c, 0)), vec, vec,
                  pl.BlockSpec((G, T, T), lambda c: (0, 0, 0)), pl.BlockSpec((G, T, Dg), lambda c: (0, 0, 0))],
        out_specs=[pl.BlockSpec((T, 2 * GW), lambda c: (c, 0)), pl.BlockSpec((8, GW), lambda c: (0, 0)),
                   pl.BlockSpec((G, T, T), lambda c: (0, 0, 0)), pl.BlockSpec((G, T, 1), lambda c: (0, 0, 0))],
        out_shape=[jax.ShapeDtypeStruct((S, 2 * GW), BF16), jax.ShapeDtypeStruct((8, GW), F32),
                   jax.ShapeDtypeStruct((G, T, T), F32), jax.ShapeDtypeStruct((G, T, 1), F32)],
        scratch_shapes=[pltpu.VMEM((G, T, Dg), F32), pltpu.VMEM((T, GW), F32)],
        compiler_params=_cparams(("arbitrary",)))(z, z, dya, ln_g, ln_b, ws, bsb)


def _hg_common(q_ref, f_ref, hlb_ref):
    C = HG_CHUNK
    a = hlb_ref[...]
    lb = _sigmoid(a[0:1, :] - a[1:2, :])
    sig = _sigmoid(f_ref[...])
    f = lb + (1.0 - lb) * sig
    lf = jnp.log(f)
    kk = 1.0 - f
    q = q_ref[...]
    sq = _sigmoid(q)
    qa = q * sq
    row = lax.broadcasted_iota(jnp.int32, (C, C), 0)
    col = lax.broadcasted_iota(jnp.int32, (C, C), 1)
    tril = row >= col
    b = _ones_dot(tril.astype(BF16), lf)
    bm = b[HG_MID:HG_MID + 1, :]
    bl = b[C - 1:C, :]
    e_b = jnp.exp(b)
    e_qm = jnp.exp(jnp.minimum(b - bm, EXP_CLAMP))
    e_km = jnp.exp(jnp.minimum(bm - b, EXP_CLAMP))
    e_kl = jnp.exp(bl - b)
    return dict(lb=lb, sig=sig, f=f, kk=kk, q=q, sq=sq, qa=qa, tril=tril, e_b=e_b, e_qm=e_qm, e_km=e_km, e_kl=e_kl,
                e_l=jnp.exp(bl), qh=qa * e_b, qt=qa * e_qm, kt=kk * e_km, kh=kk * e_kl)


def _hg_fwd(z, hg_lb, ng, HW):
    S = z.shape[0]
    C, H, dk = HG_CHUNK, HW // HG_DK, HG_DK
    nc = S // C

    def body(q_ref, f_ref, i_ref, og_ref, hlb_ref, ng_ref, yb_ref, o_ref, st_ref, state):
        @pl.when(pl.program_id(0) == 0)
        def _():
            state[...] = jnp.zeros_like(state)

        t = _hg_common(q_ref, f_ref, hlb_ref)
        iv = i_ref[...]
        for h in range(H):
            sl = slice(h * dk, (h + 1) * dk)
            st = state[h]
            st_ref[h] = st
            a = jnp.where(t["tril"], _dot(t["qt"][:, sl], t["kt"][:, sl], _NT), 0.0)
            o_h = _dot(a, iv[:, sl]) + _dot(t["qh"][:, sl], st, _NT)
            state[h] = st * t["e_l"][:, sl] + _dot(iv[:, sl], t["kh"][:, sl], _TN)
            o_ref[:, sl] = o_h
            rr = lax.rsqrt(jnp.mean(o_h * o_h, axis=-1, keepdims=True) + EPS)
            og = og_ref[:, sl]
            yb_ref[:, sl] = (o_h * rr * ng_ref[:, sl] * (og * _sigmoid(og))).astype(BF16)

    def col(k):
        return pl.BlockSpec((C, HW), lambda c: (c, k))

    base = 2
    return pl.pallas_call(
        body, name="hgrn_fwd", grid=(nc,),
        in_specs=[col(base), col(base + 1), col(base + 2), col(base + 3),
                  pl.BlockSpec((2, HW), lambda c: (0, 0)), pl.BlockSpec((1, HW), lambda c: (0, 0))],
        out_specs=[pl.BlockSpec((C, HW), lambda c: (c, 0)), pl.BlockSpec((C, HW), lambda c: (c, 0)),
                   pl.BlockSpec((None, H, dk, dk), lambda c: (c, 0, 0, 0))],
        out_shape=[jax.ShapeDtypeStruct((S, HW), BF16), jax.ShapeDtypeStruct((S, HW), F32),
                   jax.ShapeDtypeStruct((nc, H, dk, dk), F32)],
        scratch_shapes=[pltpu.VMEM((H, dk, dk), F32)],
        compiler_params=_cparams(("arbitrary",)))(z, z, z, z, hg_lb, ng)


def _hg_bwd(z, o, states, dyb, hg_lb, ng, HW, dz_head, dz_tail):
    S = z.shape[0]
    C, H, dk = HG_CHUNK, HW // HG_DK, HG_DK
    nc = S // C
    B0 = dz_head.shape[1]
    DT = dz_tail.shape[2]
    INW = B0 + 4 * HW + 2 * DT

    def body(q_ref, f_ref, i_ref, og_ref, o_ref, st_ref, stn_ref, dyb_ref, hlb_ref, ng_ref, head_ref, tail_ref,
             dzf_ref, dng_ref, dhlb_ref, dstate, cross, dqa_buf, dkk_buf, db_buf, dlb_acc):
        c = pl.program_id(0)
        dzf_ref[:, 0:B0] = head_ref[...]
        dzf_ref[:, B0 + 4 * HW:B0 + 4 * HW + DT] = tail_ref[0]
        dzf_ref[:, B0 + 4 * HW + DT:INW] = tail_ref[1]
        dz_ref = dzf_ref.at[:, B0:B0 + 4 * HW]

        @pl.when(c == 0)
        def _():
            dstate[...] = jnp.zeros_like(dstate)
            dlb_acc[...] = jnp.zeros_like(dlb_acc)
            dng_ref[...] = jnp.zeros_like(dng_ref)

        def r16(v):
            return v.astype(BF16).astype(F32)

        t = _hg_common(q_ref, f_ref, hlb_ref)
        iv = i_ref[...]
        for h in range(H):
            sl = slice(h * dk, (h + 1) * dk)
            o_h, og, dyb_h, ng_h = o_ref[:, sl], og_ref[:, sl], dyb_ref[:, sl], ng_ref[:, sl]
            sg = _sigmoid(og)
            silu_og = og * sg
            rr = lax.rsqrt(jnp.mean(o_h * o_h, axis=-1, keepdims=True) + EPS)
            on = o_h * rr
            dng_ref[0:1, sl] += jnp.sum(dyb_h * on * silu_og, axis=0, keepdims=True)
            dz_ref[:, 3 * HW + h * dk:3 * HW + (h + 1) * dk] = (dyb_h * on * ng_h * (sg * (1.0 + og * (1.0 - sg)))).astype(BF16)
            don = dyb_h * ng_h * silu_og
            do_h = rr * (don - on * jnp.mean(don * on, axis=-1, keepdims=True))

            qt, kt, qh, kh, iv_h = t["qt"][:, sl], t["kt"][:, sl], t["qh"][:, sl], t["kh"][:, sl], iv[:, sl]
            a = jnp.where(t["tril"], _dot(qt, kt, _NT), 0.0)
            da = jnp.where(t["tril"], _dot(do_h, iv_h, _NT), 0.0)
            st, dst = st_ref[h], dstate[h]
            cross[:, sl] = jnp.sum(dst * stn_ref[h], axis=0, keepdims=True)
            dqh = _dot(do_h, st)
            dstate[h] = _dot(do_h, qh, _TN) + dst * t["e_l"][:, sl]
            div = _dot(a, do_h, _TN) + _dot(kh, dst, _NT)
            dkh = _dot(iv_h, dst)
            dqt = _dot(da, kt)
            dkt = _dot(da, qt, _TN)
            dz_ref[:, 2 * HW + h * dk:2 * HW + (h + 1) * dk] = div.astype(BF16)
            dqa_buf[:, sl] = dqh * t["e_b"][:, sl] + dqt * t["e_qm"][:, sl]
            dkk_buf[:, sl] = dkt * t["e_km"][:, sl] + dkh * t["e_kl"][:, sl]
            db_buf[:, sl] = r16(qt) * dqt - r16(kt) * dkt + r16(qh) * dqh - r16(kh) * dkh

        dqa, dkk = dqa_buf[...], dkk_buf[...]
        triu = jnp.logical_not(t["tril"]) | (lax.broadcasted_iota(jnp.int32, (C, C), 0) == lax.broadcasted_iota(jnp.int32, (C, C), 1))
        dlf = _ones_dot(triu.astype(BF16), db_buf[...]) + cross[...]
        df = dlf / t["f"] - dkk
        sig, lb = t["sig"], t["lb"]
        dz_ref[:, HW:2 * HW] = (df * (1.0 - lb) * sig * (1.0 - sig)).astype(BF16)
        dlb_acc[...] += jnp.sum(df * (1.0 - sig), axis=0, keepdims=True)
        q, sq = t["q"], t["sq"]
        dz_ref[:, 0:HW] = (dqa * (sq * (1.0 + q * (1.0 - sq)))).astype(BF16)

        @pl.when(c == nc - 1)
        def _():
            da0 = dlb_acc[...] * lb * (1.0 - lb)
            dhlb_ref[0:1, :] = da0
            dhlb_ref[1:2, :] = -da0

    def col(k):
        return pl.BlockSpec((C, HW), lambda c: (nc - 1 - c, k))

    base = 2
    return pl.pallas_call(
        body, name="hgrn_bwd", grid=(nc,),
        in_specs=[col(base), col(base + 1), col(base + 2), col(base + 3), col(0),
                  pl.BlockSpec((None, H, dk, dk), lambda c: (nc - 1 - c, 0, 0, 0)),
                  pl.BlockSpec((None, H, dk, dk), lambda c: (jnp.minimum(nc - c, nc - 1), 0, 0, 0)), col(0),
                  pl.BlockSpec((2, HW), lambda c: (0, 0)), pl.BlockSpec((1, HW), lambda c: (0, 0)),
                  pl.BlockSpec((C, B0), lambda c: (nc - 1 - c, 0)), pl.BlockSpec((2, C, DT), lambda c: (0, nc - 1 - c, 0))],
        out_specs=[pl.BlockSpec((C, INW), lambda c: (nc - 1 - c, 0)), pl.BlockSpec((8, HW), lambda c: (0, 0)),
                   pl.BlockSpec((2, HW), lambda c: (0, 0))],
        out_shape=[jax.ShapeDtypeStruct((S, INW), BF16), jax.ShapeDtypeStruct((8, HW), F32), jax.ShapeDtypeStruct((2, HW), F32)],
        scratch_shapes=[pltpu.VMEM((H, dk, dk), F32), pltpu.VMEM((1, HW), F32), pltpu.VMEM((C, HW), F32), pltpu.VMEM((C, HW), F32),
                        pltpu.VMEM((C, HW), F32), pltpu.VMEM((1, HW), F32)],
        compiler_params=_cparams(("arbitrary",)))(z, z, z, z, o, states, states, dyb, hg_lb, ng, dz_head, dz_tail)


def _position():
    x, y, c = lax.axis_index("x"), lax.axis_index("y"), lax.axis_index("c")
    return x, y, c, 4 * x + 2 * y + c


def _flip(x, y, c, k):
    return (1 - x if k & 4 else x, 1 - y if k & 2 else y, 1 - c if k & 1 else c)


def _allgather_small(name, v):
    R, L = v.shape

    def body(v_ref, out_ref, send_sems, recv_sems):
        x, y, c, me = _position()
        out_ref[me] = v_ref[...]
        copies = []
        for k in range(1, N_DEV):
            cp = pltpu.make_async_remote_copy(src_ref=v_ref, dst_ref=out_ref.at[me], send_sem=send_sems.at[k - 1],
                                              recv_sem=recv_sems.at[k - 1], device_id=_flip(x, y, c, k), device_id_type=MESH)
            cp.start()
            copies.append(cp)
        for cp in copies:
            cp.wait()

    return pl.pallas_call(
        body, name=name, out_shape=jax.ShapeDtypeStruct((N_DEV, R, L), v.dtype),
        in_specs=[pl.BlockSpec(memory_space=pltpu.VMEM)], out_specs=pl.BlockSpec(memory_space=pltpu.VMEM),
        scratch_shapes=[pltpu.SemaphoreType.DMA((N_DEV - 1,)), pltpu.SemaphoreType.DMA((N_DEV - 1,))],
        compiler_params=pltpu.CompilerParams(vmem_limit_bytes=VMEM_LIMIT),
    )(v)


def _allgather_hbm(name, shards):
    n = len(shards)

    def body(*refs):
        ins, outs = refs[:n], refs[n:2 * n]
        send_sems, recv_sems, local_sems = refs[2 * n:]
        x, y, c, me = _position()
        sibling = (x, y, 1 - c)
        chips = [(1 - x, y), (x, 1 - y), (1 - x, 1 - y)]

        def slot(px, py, pc):
            return 4 * px + 2 * py + pc

        def copy(w, k, block, to, src=None):
            dst = outs[w].at[slot(*block)]
            return pltpu.make_async_remote_copy(src_ref=dst if src is None else src, dst_ref=dst, send_sem=send_sems.at[w, k],
                                                recv_sem=recv_sems.at[w, k], device_id=to, device_id_type=MESH)

        mine, first, passed = [], [], []
        for w in range(n):
            cp = pltpu.make_async_copy(ins[w], outs[w].at[me], local_sems.at[w])
            cp.start()
            mine.append(cp)
            for j, chip in enumerate(chips):
                first.append(copy(w, 1 + j, (x, y, c), (*chip, c), src=ins[w]))
            first.append(copy(w, 0, (x, y, c), sibling, src=ins[w]))
        for cp in first:
            cp.start()
        for w in range(n):
            for j, chip in enumerate(chips):
                copy(w, 1 + j, (*chip, c), (x, y, c)).wait_recv()
                cp = copy(w, 4 + j, (*chip, c), sibling)
                cp.start()
                passed.append(cp)
        for w in range(n):
            copy(w, 0, sibling, (x, y, c)).wait_recv()
            for j, chip in enumerate(chips):
                copy(w, 4 + j, (*chip, 1 - c), (x, y, c)).wait_recv()
        for cp in first + passed:
            cp.wait_send()
        for cp in mine:
            cp.wait()

    hbm = pl.BlockSpec(memory_space=pltpu.HBM)
    return pl.pallas_call(
        body, name=name, out_shape=[jax.ShapeDtypeStruct((N_DEV, *s.shape), s.dtype) for s in shards],
        in_specs=[hbm] * n, out_specs=[hbm] * n,
        scratch_shapes=[pltpu.SemaphoreType.DMA((n, 7)), pltpu.SemaphoreType.DMA((n, 7)), pltpu.SemaphoreType.DMA((n,))],
    )(*shards)


_HBM = pl.BlockSpec(memory_space=pltpu.HBM)
_SEM = pl.BlockSpec(memory_space=pltpu.SEMAPHORE)
_EFFECT = pltpu.SideEffectType.DATAFLOW_SIDE_EFFECTING


def _split_start(name, bufs, n_sems, copies_fn, after=None):
    nb = len(bufs)
    extra = [] if after is None else [after]
    k = nb + len(extra)

    def body(*refs):
        for cp in copies_fn(refs[:nb], refs[k], refs[k + 1]):
            cp.start()
        refs[-1][...] = jnp.zeros_like(refs[-1])

    sems = pltpu.SemaphoreType.DMA((n_sems,))
    res = pl.pallas_call(
        body, name=name,
        out_shape=(sems, sems, *[pltpu.HBM(a.shape, a.dtype) for a in bufs], jax.ShapeDtypeStruct((8, LANES), F32)),
        in_specs=[_HBM] * nb + [pl.BlockSpec(memory_space=pl.ANY)] * len(extra),
        out_specs=(_SEM, _SEM, *[_HBM] * nb, pl.BlockSpec(memory_space=pltpu.VMEM)),
        input_output_aliases={i: 2 + i for i in range(nb)},
        compiler_params=pltpu.CompilerParams(has_side_effects=_EFFECT),
    )(*[pltpu.with_memory_space_constraint(a, pltpu.HBM) for a in bufs], *extra)
    return res[0], res[1], list(res[2:2 + nb]), res[-1]


def _split_wait(name, bufs, send_sems, recv_sems, after, copies_fn):
    nb = len(bufs)

    def body(*refs):
        for cp in copies_fn(refs[:nb], refs[nb], refs[nb + 1]):
            cp.wait_send()
            cp.wait_recv()

    res = pl.pallas_call(
        body, name=name, out_shape=tuple(pltpu.HBM(a.shape, a.dtype) for a in bufs),
        in_specs=[_HBM] * nb + [_SEM, _SEM, pl.BlockSpec(memory_space=pl.ANY)], out_specs=tuple([_HBM] * nb),
        input_output_aliases={i: i for i in range(nb)},
        compiler_params=pltpu.CompilerParams(has_side_effects=_EFFECT),
    )(*bufs, send_sems, recv_sems, after)
    return list(res)


def _split_relay(name, bufs, send_sems, recv_sems, after, wait_fn, n_sems, start_fn):
    nb = len(bufs)

    def body(*refs):
        for cp in wait_fn(refs[:nb], refs[nb], refs[nb + 1]):
            cp.wait_send()
            cp.wait_recv()
        for cp in start_fn(refs[:nb], refs[nb + 3], refs[nb + 4]):
            cp.start()

    sems = pltpu.SemaphoreType.DMA((n_sems,))
    res = pl.pallas_call(
        body, name=name, out_shape=(sems, sems, *[pltpu.HBM(a.shape, a.dtype) for a in bufs]),
        in_specs=[_HBM] * nb + [_SEM, _SEM, pl.BlockSpec(memory_space=pl.ANY)], out_specs=(_SEM, _SEM, *[_HBM] * nb),
        input_output_aliases={i: 2 + i for i in range(nb)},
        compiler_params=pltpu.CompilerParams(has_side_effects=_EFFECT),
    )(*bufs, send_sems, recv_sems, after)
    return res[0], res[1], list(res[2:])


N_CHIP = 4


def _chip_flip(x, y, k):
    return (1 - x if k & 2 else x), (1 - y if k & 1 else y)


def _gather_first_copies(n):
    def copies(bufs, send_sems, recv_sems):
        x, y, c, me = _position()
        out = []
        for w in range(n):
            for k in range(N_CHIP):
                to = (x, y, 1 - c) if k == 0 else (*_chip_flip(x, y, k), c)
                out.append(pltpu.make_async_remote_copy(
                    src_ref=bufs[w], dst_ref=bufs[n + w].at[me], send_sem=send_sems.at[w * N_CHIP + k],
                    recv_sem=recv_sems.at[w * N_CHIP + k], device_id=to, device_id_type=MESH))
        return out
    return copies


def _gather_relay_copies(n):
    def copies(bufs, send_sems, recv_sems):
        x, y, c, _ = _position()
        out = []
        for w in range(n):
            for k in range(1, N_CHIP):
                px, py = _chip_flip(x, y, k)
                blk = bufs[n + w].at[4 * px + 2 * py + c]
                out.append(pltpu.make_async_remote_copy(
                    src_ref=blk, dst_ref=blk, send_sem=send_sems.at[w * (N_CHIP - 1) + k - 1],
                    recv_sem=recv_sems.at[w * (N_CHIP - 1) + k - 1], device_id=(x, y, 1 - c), device_id_type=MESH))
        return out
    return copies


def _to_sibling_copies(n):
    def copies(bufs, send_sems, recv_sems):
        x, y, c, _ = _position()
        out = []
        for w in range(n):
            for q in range(N_CHIP):
                out.append(pltpu.make_async_remote_copy(
                    src_ref=bufs[w].at[2 * q + 1 - c], dst_ref=bufs[n + w].at[q], send_sem=send_sems.at[w * N_CHIP + q],
                    recv_sem=recv_sems.at[w * N_CHIP + q], device_id=(x, y, 1 - c), device_id_type=MESH))
        return out
    return copies


def _to_owner_copies(n):
    def copies(bufs, send_sems, recv_sems):
        x, y, c, _ = _position()
        out = []
        for w in range(n):
            for k in range(1, N_CHIP):
                px, py = (1 - x if k & 2 else x), (1 - y if k & 1 else y)
                out.append(pltpu.make_async_remote_copy(
                    src_ref=bufs[w].at[2 * px + py], dst_ref=bufs[n + w].at[k - 1], send_sem=send_sems.at[w * (N_CHIP - 1) + k - 1],
                    recv_sem=recv_sems.at[w * (N_CHIP - 1) + k - 1], device_id=(px, py, c), device_id_type=MESH))
        return out
    return copies


def _chip_sum(name, stack, landed, c_idx):
    _, R, C = stack.shape
    tr = _tile(R, max(16, 1048576 // C), 16)

    def body(c_ref, a_ref, b_ref, o_ref):
        o_ref[...] = (a_ref[...].astype(F32) + b_ref[...].astype(F32)).astype(o_ref.dtype)

    return pl.pallas_call(
        body, name=name,
        grid_spec=pltpu.PrefetchScalarGridSpec(
            num_scalar_prefetch=1, grid=(N_CHIP, R // tr),
            in_specs=[pl.BlockSpec((None, tr, C), lambda q, i, c_ref: (2 * q + c_ref[0], i, 0)),
                      pl.BlockSpec((None, tr, C), lambda q, i, c_ref: (q, i, 0))],
            out_specs=pl.BlockSpec((None, tr, C), lambda q, i, c_ref: (q, i, 0))),
        out_shape=jax.ShapeDtypeStruct((N_CHIP, R, C), stack.dtype),
        compiler_params=_cparams(("parallel", "parallel")))(c_idx, stack, landed)


def _ada_mod(c16, w):
    _, D = c16.shape
    n = w.shape[1]
    tk = _tile(D, 512)
    nk = D // tk

    def body(c_ref, w_ref, o_ref, ca_ref):
        @pl.when(pl.program_id(0) == 0)
        def _():
            o_ref[...] = jnp.zeros_like(o_ref)

        cv = c_ref[...]
        ca = cv * _sigmoid(cv)
        ca_ref[...] = ca
        o_ref[...] += _dot(ca, w_ref[...])

    return pl.pallas_call(
        body, name="ada_mod", grid=(nk,),
        in_specs=[pl.BlockSpec((16, tk), lambda k: (0, k)), pl.BlockSpec((tk, n), lambda k: (k, 0))],
        out_specs=[pl.BlockSpec((16, n), lambda k: (0, 0)), pl.BlockSpec((16, tk), lambda k: (0, k))],
        out_shape=[jax.ShapeDtypeStruct((16, n), F32), jax.ShapeDtypeStruct((16, D), F32)],
        compiler_params=_cparams(("arbitrary",)))(c16, w)


def _adam_math(w, g, m, v):
    m2 = ADAM_B1 * m + (1.0 - ADAM_B1) * g
    v2 = ADAM_B2 * v + (1.0 - ADAM_B2) * (g * g)
    m_hat = m2 / (1.0 - ADAM_B1 ** ADAM_STEP)
    v_hat = v2 / (1.0 - ADAM_B2 ** ADAM_STEP)
    delta = -ADAM_LR * (m_hat / (jnp.sqrt(v_hat) + ADAM_EPS) + ADAM_WD * w)
    return delta, m2, v2


def _adamw(name, w, m, v, parts):
    R, C = w.shape
    tr = _tile(R, max(16, 393216 // C), 16)
    sizes = [len(grp) for grp in parts]
    parts = [a for grp in parts for a in grp]
    n_p = len(parts)

    def body(*refs):
        w_ref, m_ref, v_ref = refs[:3]
        g_ref, d_ref, m2_ref, v2_ref = refs[3 + n_p:]
        cols, at = [], 3
        for n_grp in sizes:
            g = None
            for p_ref in refs[at:at + n_grp]:
                for s in range(p_ref.shape[0]):
                    t = p_ref[s].astype(F32)
                    g = t if g is None else g + t
            cols.append(g)
            at += n_grp
        g = cols[0] if len(cols) == 1 else jnp.concatenate(cols, axis=-1)
        delta, m2, v2 = _adam_math(w_ref[...], g, m_ref[...], v_ref[...])
        g_ref[...] = g
        d_ref[...] = delta
        m2_ref[...] = m2
        v2_ref[...] = v2

    blk = pl.BlockSpec((tr, C), lambda i: (i, 0))
    out = jax.ShapeDtypeStruct((R, C), F32)
    return pl.pallas_call(body, name=name, grid=(R // tr,),
                          in_specs=[blk, blk, blk] + [pl.BlockSpec((a.shape[0], tr, a.shape[2]), lambda i: (0, i, 0)) for a in parts],
                          out_specs=[blk] * 4, out_shape=[out] * 4, compiler_params=_cparams(("parallel",)))(w, m, v, *parts)


def _small_update(gathered, w, m, v, after):
    _, R, L = gathered.shape
    rs = w.shape[0]

    def body(p_ref, w_ref, m_ref, v_ref, after_ref, g_ref, d_ref, m2_ref, v2_ref):
        g = p_ref[0]
        for p in range(1, N_DEV):
            g = g + p_ref[p]
        g_ref[...] = g
        delta, m2, v2 = _adam_math(w_ref[...], g[0:rs, :], m_ref[...], v_ref[...])
        d_ref[...] = delta
        m2_ref[...] = m2
        v2_ref[...] = v2

    vm = pl.BlockSpec(memory_space=pltpu.VMEM)
    sm = jax.ShapeDtypeStruct((rs, L), F32)
    return pl.pallas_call(body, name="small_update", in_specs=[vm] * 4 + [pl.BlockSpec(memory_space=pl.ANY)], out_specs=[vm] * 4,
                          out_shape=[jax.ShapeDtypeStruct((R, L), F32), sm, sm, sm],
                          compiler_params=pltpu.CompilerParams(vmem_limit_bytes=VMEM_LIMIT))(gathered, w, m, v, after)


class _Fetched(dict):
    def __init__(self, fetch):
        super().__init__()
        self.fetch = fetch

    def first(self, key, after):
        self[key] = self.fetch(key, after)
        return self[key]


def _local_step(x, tgt, mod, p, fetch, F, scatter=None):
    S, D = x.shape
    GW, HW = p["ln_g"].shape[1], p["hg_ng"].shape[1]
    G, T, _ = p["ws"].shape
    w = _Fetched(fetch)
    INW = 2 * GW + 4 * HW + 2 * D
    in_loc, br_loc, fi_loc = INW // N_DEV, D // N_DEV, 2 * F // N_DEV
    assert GW == HW and F % fi_loc == 0
    sh1, sc1, gt1, sh2, sc2, gt2 = (mod[:, k * D:(k + 1) * D] for k in range(6))
    bsb = jnp.broadcast_to(p["bs"][:, :, None], (G, T, GW // G))

    tm = _tile(S, 1024, 16)
    tmh = _tile(S, 512, 16)
    tn_in = _tile(in_loc, 1280)
    tn_d = _tile(D, 512)
    tn_br = _tile(br_loc, 512)
    tk_s = S
    tm_w = _tile(D, 1024)
    g_off = 2 * GW + 4 * HW

    h1 = _norm_mod("norm1", x, p["norm1_g"], sc1, sh1)
    z = _mm_nn_stacked("proj_in", h1, w.first("in", h1), tm=tm, tn=tn_in, tk=D)[0]
    ya = _gmlp_fwd(z, p["ln_g"], p["ln_b"], p["ws"], bsb, GW)
    yb, o_hg, states = _hg_fwd(z, p["hg_lb"], p["hg_ng"], HW)
    pa = _mm_nn_stacked("branch_gmlp", ya, w.first("bg", z), tm=tm, tn=tn_br, tk=GW)[0]

    def gates(ga_ref, gb_ref, ba_ref, bb_ref):
        return _sigmoid(ga_ref[...] + ba_ref[...]), _sigmoid(gb_ref[...] + bb_ref[...])

    def gate_specs(tn_):
        o1, o2 = g_off // tn_, (g_off + D) // tn_
        return [pl.BlockSpec((tm, tn_), lambda i, j, k: (i, o1 + j)), pl.BlockSpec((tm, tn_), lambda i, j, k: (i, o2 + j)),
                pl.BlockSpec((1, tn_), lambda i, j, k: (0, j)), pl.BlockSpec((1, tn_), lambda i, j, k: (0, D // tn_ + j))]

    def merge_ep(acc, ex, outs):
        ga, gb = gates(*ex[1:5])
        outs[0][...] = acc
        outs[1][...] = (ga * ex[0][...] + gb * acc).astype(BF16)

    tile_o = pl.BlockSpec((tm, tn_br), lambda i, j, k: (i, j))
    pb, y = _mm_nn_stacked(
        "branch_hg_merge", yb, w.first("bh", z), tm=tm, tn=tn_br, tk=HW, extras=[pa, z, z, p["b_gate"], p["b_gate"]],
        extra_specs=[tile_o, *gate_specs(tn_br)], out_shapes=[jax.ShapeDtypeStruct((S, D), F32), jax.ShapeDtypeStruct((S, D), BF16)],
        out_specs=[tile_o, tile_o], epilogue=merge_ep)

    def resid_ep(acc, ex, outs):
        outs[0][...] = acc
        outs[1][...] = ex[0][...] + ex[1][...] * acc

    def resid_mm(name, a, b, res, gt, tm_):
        K = a.shape[1]
        t_o = pl.BlockSpec((tm_, tn_d), lambda i, j, k: (i, j))
        return _matmul(
            name, a, b, dims=_NN, grid_mnk=(S // tm_, D // tn_d, 1), tiles=(tm_, tn_d),
            a_spec=pl.BlockSpec((tm_, K), lambda i, j, k: (i, 0)), b_spec=pl.BlockSpec((K, tn_d), lambda i, j, k: (0, j)),
            extras=[res, gt], extra_specs=[t_o, pl.BlockSpec((1, tn_d), lambda i, j, k: (0, j))],
            out_shapes=[jax.ShapeDtypeStruct((S, D), F32)] * 2, out_specs=[t_o, t_o], epilogue=resid_ep)

    o1, xm = resid_mm("proj_out", y, w.first("out", z), x, gt1, tm)
    h2 = _norm_mod("norm2", xm, p["norm2_g"], sc2, sh2)
    hf, hf_fac = _ffn_in_swiglu(h2, w.first("fi", y))
    o2, x3 = resid_mm("ffn_out", hf, w.first("fo", hf), xm, gt2, tmh)
    dx3, do2, vec_l = _loss_head(x3, tgt, p["final_g"], o2, gt2)

    nf = F // fi_loc

    def dswiglu_ep(acc, ex, outs):
        outs[0][0] = (acc * ex[0][0].astype(F32)).astype(BF16)
        outs[0][1] = (acc * ex[0][1].astype(F32)).astype(BF16)

    pair = pl.BlockSpec((2, tmh, fi_loc), lambda i, j, k: (0, i, j))
    dab = _matmul(
        "ffn_out_dx", do2, w["fo"], dims=_NT, grid_mnk=(S // tmh, nf, 1), tiles=(tmh, fi_loc),
        a_spec=pl.BlockSpec((tmh, D), lambda i, j, k: (i, 0)), b_spec=pl.BlockSpec((fi_loc, D), lambda i, j, k: (j, 0)),
        extras=[hf_fac], extra_specs=[pair], out_shapes=[jax.ShapeDtypeStruct((2, S, F), BF16)], out_specs=[pair],
        epilogue=dswiglu_ep)[0]
    start = (lambda name, grads: scatter[0](name, grads)) if scatter is not None else (lambda name, grads: None)
    push = (lambda name, after: scatter[1](name, after)) if scatter is not None else (lambda name, after: None)

    def zero(token):
        return 0.0 if token is None else token[0:1, 0:1]

    tm_f = _tile(F, 512)
    g_fo = _mm_tn("ffn_out_dw", hf, do2, pl.BlockSpec((tk_s, D), lambda i, j, k: (k, j)), Mo=F, No=D, S=S, tm=tm_f, tn=D, tk=tk_s)
    g_fi = _mm_tn("ffn_in_dw", h2, dab, pl.BlockSpec((None, tk_s, fi_loc), lambda i, j, k: (j // nf, k, j % nf)),
                  Mo=D, No=2 * F, S=S, tm=tm_w, tn=fi_loc, tk=tk_s, stacked_nloc=fi_loc, after=g_fo)
    t_ffn = start("scatter_ffn", dict(fo=g_fo, fi=g_fi))
    dh2 = _mm_nt_stacked("ffn_in_dx", pl.BlockSpec((None, tmh, fi_loc), lambda i, j, k: (k // nf, i, k % nf)), dab, w["fi"],
                         M=S, tm=tmh, tn=D, tk=fi_loc, after=t_ffn)
    dxm, vec2, do1 = _norm_mod_bwd("norm2_bwd", dh2, xm, p["norm2_g"], sc2, dx3, o1, gt1)
    t_ffn = push("scatter_ffn", dxm)

    def dmerge_ep(acc, ex, outs):
        ga, gb = gates(*ex[2:6])
        outs[0][...] = (acc * ga).astype(BF16)
        outs[1][...] = (acc * gb).astype(BF16)
        outs[2][0] = (acc * ex[0][...] * ga * (1.0 - ga)).astype(BF16)
        outs[2][1] = (acc * ex[1][...] * gb * (1.0 - gb)).astype(BF16)

    t_o = pl.BlockSpec((tm, tn_d), lambda i, j, k: (i, j))
    dpa, dpb, dg2 = _matmul(
        "proj_out_dx", do1, w["out"], dims=_NT, grid_mnk=(S // tm, D // tn_d, 1), tiles=(tm, tn_d),
        a_spec=pl.BlockSpec((tm, D), lambda i, j, k: (i, 0)), b_spec=pl.BlockSpec((tn_d, D), lambda i, j, k: (j, 0)),
        extras=[pa, pb, z, z, p["b_gate"], p["b_gate"]], extra_specs=[t_o, t_o, *gate_specs(tn_d)],
        out_shapes=[jax.ShapeDtypeStruct((S, D), BF16), jax.ShapeDtypeStruct((S, D), BF16), jax.ShapeDtypeStruct((2, S, D), BF16)],
        out_specs=[t_o, t_o, pl.BlockSpec((2, tm, tn_d), lambda i, j, k: (0, i, j))], epilogue=dmerge_ep, after=t_ffn)
    g_out = _mm_tn("proj_out_dw", y, do1, pl.BlockSpec((tk_s, D), lambda i, j, k: (k, j)), Mo=D, No=D, S=S, tm=tn_d, tn=D, tk=tk_s)
    tn_g = _tile(GW, 512)
    b_br = pl.BlockSpec((tk_s, br_loc), lambda i, j, k: (k, j))
    g_bg = _mm_tn("branch_gmlp_dw", ya, dpa, b_br, Mo=GW, No=D, S=S, tm=tn_g, tn=br_loc, tk=tk_s, stacked_nloc=br_loc)
    g_bh = _mm_tn("branch_hg_dw", yb, dpb, b_br, Mo=HW, No=D, S=S, tm=tn_g, tn=br_loc, tk=tk_s, stacked_nloc=br_loc)
    t_mix = start("scatter_mixer", dict(out=g_out, bg=g_bg, bh=g_bh))
    def branch_dx(name, dp, wg):
        flat = jnp.swapaxes(wg, 0, 1).reshape(wg.shape[1], D)
        return _matmul(
            name, dp, flat, dims=_NT, grid_mnk=(S // tm, GW // tn_g, 1), tiles=(tm, tn_g),
            a_spec=pl.BlockSpec((tm, D), lambda i, j, k: (i, 0)), b_spec=pl.BlockSpec((tn_g, D), lambda i, j, k: (j, 0)),
            out_shapes=[jax.ShapeDtypeStruct((S, GW), F32)], out_specs=[pl.BlockSpec((tm, tn_g), lambda i, j, k: (i, j))],
            epilogue=_store(F32), after=t_mix)[0]

    dya = branch_dx("branch_gmlp_dx", dpa, w["bg"])
    dyb = branch_dx("branch_hg_dx", dpb, w["bh"])
    db_gate = _colsum2(dg2)
    dz_gmlp, dln, dws, dbs = _gmlp_bwd(z, dya, p["ln_g"], p["ln_b"], p["ws"], bsb, GW)
    t_mix = push("scatter_mixer", dz_gmlp)
    dz, dng, dhlb = _hg_bwd(z, o_hg, states, dyb, p["hg_lb"], p["hg_ng"] + zero(t_mix), HW, dz_gmlp, dg2)
    half = in_loc // 2
    g_in = []
    t_in = None
    for hname, h in (("a", 0), ("b", 1)):
        g_in.append(_mm_tn("proj_in_dw_" + hname, h1, dz, pl.BlockSpec((tk_s, half), lambda i, j, k, h=h: (k, 2 * j + h)), Mo=D, No=INW // 2,
                           S=S, tm=tm_w, tn=half, tk=tk_s, stacked_nloc=half, after=t_in))
        t_in = start("scatter_proj_in_" + hname, {"w_in_" + hname: g_in[-1]})
    t_in = push("scatter_proj_in_a", t_in)
    dh1 = _mm_nt_stacked("proj_in_dx", pl.BlockSpec((tmh, in_loc), lambda i, j, k: (i, k)), dz, w["in"], M=S, tm=tmh, tn=D, tk=in_loc,
                         after=t_in)
    dx, vec1 = _norm_mod_bwd("norm1_bwd", dh1, x, p["norm1_g"], sc1, dxm)

    dmod = jnp.concatenate([vec1[0:1], vec1[1:2], vec2[3:4], vec2[0:1], vec2[1:2], vec_l[2:3]], axis=1)
    small = dict(norm1_g=vec1[2:3], b_gate=db_gate.reshape(1, 2 * D), ln_g=dln[0:1], ln_b=dln[1:2], ws=dws, bs=dbs.reshape(G, T),
                 hg_lb=dhlb, hg_ng=dng[0:1], norm2_g=vec2[2:3], final_g=vec_l[1:2], loss=vec_l[0:1, 0:LANES])
    big = dict(w_in_a=g_in[0], w_in_b=g_in[1], bg=g_bg, bh=g_bh, out=g_out, fi=g_fi, fo=g_fo)
    return dx, big, small, dmod


_SMALL = ("b_ada", "norm1_g", "b_gate", "ln_g", "ln_b", "ws", "bs", "hg_lb", "hg_ng", "norm2_g", "final_g")


def _pack(parts, rows_mult=8):
    flat = [a.reshape(-1) for a in parts]
    offs, n = [], 0
    for a in flat:
        offs.append(n)
        n += a.shape[0]
    pad = (-n) % (LANES * rows_mult)
    if pad:
        flat.append(jnp.zeros((pad,), F32))
    return jnp.concatenate(flat).reshape(-1, LANES), offs


def kernel(x, c, w_ada, b_ada, norm1_g, w_in, b_gate, gmlp_ln_g, gmlp_ln_b, gmlp_ws, gmlp_bs, hg_lb, hg_norm_g, w_branch_gmlp, w_branch_hg, w_out, norm2_g, w_ffn_in, w_ffn_out, final_norm_g, loss_target, m_w_ada, m_b_ada, m_norm1_g, m_w_in, m_b_gate, m_gmlp_ln_g, m_gmlp_ln_b, m_gmlp_ws, m_gmlp_bs, m_hg_lb, m_hg_norm_g, m_w_branch_gmlp, m_w_branch_hg, m_w_out, m_norm2_g, m_w_ffn_in, m_w_ffn_out, m_final_norm_g, v_w_ada, v_b_ada, v_norm1_g, v_w_in, v_b_gate, v_gmlp_ln_g, v_gmlp_ln_b, v_gmlp_ws, v_gmlp_bs, v_hg_lb, v_hg_norm_g, v_w_branch_gmlp, v_w_branch_hg, v_w_out, v_norm2_g, v_w_ffn_in, v_w_ffn_out, v_final_norm_g):
    S, D = x.shape[1], x.shape[2]
    ada_loc = w_ada.shape[2]
    me = 4 * lax.axis_index("x") + 2 * lax.axis_index("y") + lax.axis_index("c")

    c_all = _allgather_small("gather_c", c.reshape(D // LANES, LANES)).reshape(N_DEV, D)
    mod_cols, c_act = _ada_mod(jnp.pad(c_all, ((0, 16 - N_DEV), (0, 0))), w_ada[0])
    mod_all = _allgather_small("gather_mod", mod_cols[:N_DEV].reshape(-1, LANES)).reshape(N_DEV, N_DEV, ada_loc)
    mod = lax.dynamic_index_in_dim(mod_all, me, axis=1, keepdims=False).reshape(1, N_DEV * ada_loc) + b_ada

    def empty_hbm(shape, dtype):
        return pltpu.with_memory_space_constraint(lax.empty(shape, dtype), pltpu.HBM)

    groups = dict(gather_in=dict(keys=["in"], src=[w_in]), gather_mixer=dict(keys=["bg", "bh", "out"], src=[w_branch_gmlp, w_branch_hg, w_out]),
                  gather_ffn_in=dict(keys=["fi"], src=[w_ffn_in]), gather_ffn_out=dict(keys=["fo"], src=[w_ffn_out]))
    group_of = {}
    prev = mod_all
    for gname, g in groups.items():
        n = len(g["keys"])
        shards = [a[0].astype(BF16) for a in g["src"]]
        outs = [lax.dynamic_update_slice(lax.empty((N_DEV, *s.shape), BF16), s[None], (me, 0, 0)) for s in shards]
        *g["hop1"], prev = _split_start(gname + "_hop1", shards + outs, n * N_CHIP, _gather_first_copies(n), after=prev)
        for k in g["keys"]:
            group_of[k] = gname
    mod = mod + prev[0:1, 0:1]

    def fetch(key, after):
        g = groups[group_of[key]]
        n = len(g["keys"])
        if "done" not in g:
            send_sems, recv_sems, bufs = g["hop1"]
            send_sems, recv_sems, bufs = _split_relay(group_of[key] + "_relay", bufs, send_sems, recv_sems, after,
                                                      _gather_first_copies(n), n * (N_CHIP - 1), _gather_relay_copies(n))
            bufs = _split_wait(group_of[key] + "_hop2_wait", bufs, send_sems, recv_sems, after, _gather_relay_copies(n))
            g["done"] = dict(zip(g["keys"], bufs[n:]))
        arr = g["done"][key]
        return arr.reshape(-1, D) if key in ("out", "fo") else arr

    p = dict(norm1_g=norm1_g, b_gate=b_gate, ln_g=gmlp_ln_g, ln_b=gmlp_ln_b, ws=gmlp_ws[0], bs=gmlp_bs[0], hg_lb=hg_lb,
             hg_ng=hg_norm_g, norm2_g=norm2_g, final_g=final_norm_g.reshape(1, D))

    in_flight = {}
    c_idx = lax.axis_index("c").astype(jnp.int32).reshape(1)
    my_chip = 2 * lax.axis_index("x") + lax.axis_index("y")

    def scatter_start(name, grads):
        keys = list(grads)
        n = len(keys)
        stacks = [grads[k].reshape(N_DEV, -1, grads[k].shape[-1]) for k in keys]
        lands = [empty_hbm((N_CHIP, *g.shape[1:]), g.dtype) for g in stacks]
        send_sems, recv_sems, bufs, token = _split_start(name + "_d2d", stacks + lands, n * N_CHIP, _to_sibling_copies(n))
        in_flight[name] = dict(keys=keys, stage1=(send_sems, recv_sems, bufs))
        return token

    def scatter_push(name, after):
        f = in_flight[name]
        n = len(f["keys"])
        send_sems, recv_sems, bufs = f["stage1"]
        bufs = _split_wait(name + "_d2d_wait", bufs, send_sems, recv_sems, after, _to_sibling_copies(n))
        sums = [_chip_sum(f"{name}_sum_{k}", bufs[i], bufs[n + i], c_idx) for i, k in enumerate(f["keys"])]
        lands = [empty_hbm((N_CHIP - 1, *s.shape[1:]), s.dtype) for s in sums]
        send_sems, recv_sems, bufs, token = _split_start(name + "_ici", sums + lands, n * (N_CHIP - 1), _to_owner_copies(n))
        f["stage2"] = (send_sems, recv_sems, bufs)
        return token

    grad_x, _, small, dmod = _local_step(x[0], loss_target[0], mod, p, fetch, w_ffn_out.shape[1] * N_DEV, (scatter_start, scatter_push))

    small["b_ada"] = dmod
    packed, offs = _pack([small[k] for k in _SMALL] + [small["loss"]])
    gathered = _allgather_small("gather_small", packed)
    wp = dict(p, b_ada=b_ada)
    ms = dict(b_ada=m_b_ada, norm1_g=m_norm1_g, b_gate=m_b_gate, ln_g=m_gmlp_ln_g, ln_b=m_gmlp_ln_b, ws=m_gmlp_ws, bs=m_gmlp_bs,
              hg_lb=m_hg_lb, hg_ng=m_hg_norm_g, norm2_g=m_norm2_g, final_g=m_final_norm_g)
    vs = dict(b_ada=v_b_ada, norm1_g=v_norm1_g, b_gate=v_b_gate, ln_g=v_gmlp_ln_g, ln_b=v_gmlp_ln_b, ws=v_gmlp_ws, bs=v_gmlp_bs,
              hg_lb=v_hg_lb, hg_ng=v_hg_norm_g, norm2_g=v_norm2_g, final_g=v_final_norm_g)
    w_sm, _ = _pack([wp[k] for k in _SMALL])
    m_sm, _ = _pack([ms[k] for k in _SMALL])
    v_sm, _ = _pack([vs[k] for k in _SMALL])
    t_tail = scatter_push("scatter_proj_in_b", gathered)
    sm_out = _small_update(gathered, w_sm, m_sm, v_sm, t_tail)
    shapes = dict(b_ada=b_ada.shape, norm1_g=norm1_g.shape, b_gate=b_gate.shape, ln_g=gmlp_ln_g.shape, ln_b=gmlp_ln_b.shape,
                  ws=gmlp_ws.shape, bs=gmlp_bs.shape, hg_lb=hg_lb.shape, hg_ng=hg_norm_g.shape, norm2_g=norm2_g.shape,
                  final_g=final_norm_g.shape)

    def unpack(arr, k):
        i = _SMALL.index(k)
        n = math.prod(shapes[k])
        return arr.reshape(-1)[offs[i]:offs[i] + n].reshape(shapes[k])

    loss = sm_out[0].reshape(-1)[offs[len(_SMALL)]]

    dmod_all = gathered.reshape(N_DEV, -1)[:, offs[0]:offs[0] + N_DEV * ada_loc]
    dmod_loc = lax.dynamic_slice_in_dim(dmod_all, me * ada_loc, ada_loc, axis=1)
    ca_t = jnp.pad(c_act[:N_DEV].T, ((0, 0), (0, LANES - N_DEV))).astype(BF16)
    dm_p = jnp.pad(dmod_loc, ((0, LANES - N_DEV), (0, 0))).astype(BF16)
    tm_a = _tile(D, 512)
    g_ada = _matmul(
        "ada_dw", ca_t, dm_p, dims=_NN, grid_mnk=(D // tm_a, 1, 1), tiles=(tm_a, ada_loc),
        a_spec=pl.BlockSpec((tm_a, LANES), lambda i, j, k: (i, 0)), b_spec=pl.BlockSpec((LANES, ada_loc), lambda i, j, k: (0, 0)),
        out_shapes=[jax.ShapeDtypeStruct((1, D, ada_loc), F32)], out_specs=[pl.BlockSpec((None, tm_a, ada_loc), lambda i, j, k: (0, i, 0))],
        epilogue=_store(F32), after=t_tail)[0]

    upd = {"w_ada": _adamw("adamw_w_ada", w_ada[0], m_w_ada[0], v_w_ada[0], [[g_ada]])}
    big_w = dict(w_in=(w_in, m_w_in, v_w_in, "w_in"), bg=(w_branch_gmlp, m_w_branch_gmlp, v_w_branch_gmlp, "w_branch_gmlp"),
                 bh=(w_branch_hg, m_w_branch_hg, v_w_branch_hg, "w_branch_hg"), out=(w_out, m_w_out, v_w_out, "w_out"),
                 fi=(w_ffn_in, m_w_ffn_in, v_w_ffn_in, "w_ffn_in"), fo=(w_ffn_out, m_w_ffn_out, v_w_ffn_out, "w_ffn_out"))
    after = upd["w_ada"][1]
    arrived = {}
    for name in ("scatter_ffn", "scatter_mixer", "scatter_proj_in_a", "scatter_proj_in_b"):
        keys = in_flight[name]["keys"]
        n = len(keys)
        send_sems, recv_sems, bufs = in_flight[name]["stage2"]
        bufs = _split_wait(name + "_ici_wait", bufs, send_sems, recv_sems, after, _to_owner_copies(n))
        for i, k in enumerate(keys):
            arrived[k] = [lax.dynamic_index_in_dim(bufs[i], my_chip, axis=0, keepdims=True), bufs[n + i]]
            cols = [k] if k in big_w else (["w_in_a", "w_in_b"] if k == "w_in_b" else None)
            if cols is None:
                continue
            wt, mt, vt, out_name = big_w[k if k in big_w else "w_in"]
            upd[out_name] = _adamw("adamw_" + out_name, wt[0], mt[0], vt[0], [arrived[c] for c in cols])
            after = upd[out_name][1]

    order = ("w_ada", "b_ada", "norm1_g", "w_in", "b_gate", "ln_g", "ln_b", "ws", "bs", "hg_lb", "hg_ng", "w_branch_gmlp", "w_branch_hg",
             "w_out", "norm2_g", "w_ffn_in", "w_ffn_out", "final_g")
    outs = [loss, grad_x[None]]
    for idx in range(4):
        for k in order:
            outs.append(upd[k][idx][None] if k in upd else unpack(sm_out[idx], k))
    return tuple(outs)
```

```python
import functools
import math

import jax
import jax.numpy as jnp
from jax import lax
from jax.experimental import pallas as pl
from jax.experimental.pallas import tpu as pltpu

F32 = jnp.float32
BF16 = jnp.bfloat16
N_DEV = 8
EPS = 1e-6
LANES = 128
HG_DK = 128
HG_CHUNK = 64
HG_MID = HG_CHUNK // 2 - 1
EXP_CLAMP = 80.0
VMEM_LIMIT = 48 * 1024 * 1024
ADAM_LR, ADAM_B1, ADAM_B2, ADAM_EPS, ADAM_WD, ADAM_STEP = 0.001, 0.9, 0.999, 1e-08, 0.01, 10
MESH = pl.DeviceIdType.MESH

_NN = (((1,), (0,)), ((), ()))
_NT = (((1,), (1,)), ((), ()))
_TN = (((0,), (0,)), ((), ()))


def _dot(a, b, dims=_NN):
    return lax.dot_general(a.astype(BF16), b.astype(BF16), dims, preferred_element_type=F32)


def _tile(n, target, mult=LANES):
    best = None
    for t in range(mult, min(n, target) + 1, mult):
        if n % t == 0:
            best = t
    return n if best is None else best


def _cparams(sem):
    return pltpu.CompilerParams(dimension_semantics=sem, vmem_limit_bytes=VMEM_LIMIT)


def _sigmoid(x):
    return 1.0 / (1.0 + jnp.exp(-x))


def _gelu_parts(x):
    k0 = math.sqrt(2.0 / math.pi)
    x2 = x * x
    t = jnp.tanh(k0 * (x + 0.044715 * x * x2))
    g = 0.5 * x * (1.0 + t)
    dg = 0.5 * (1.0 + t) + 0.5 * x * (1.0 - t * t) * (k0 * (1.0 + 3.0 * 0.044715 * x2))
    return g, dg


def _split3(x):
    h = x.astype(BF16)
    r = x - h.astype(F32)
    m = r.astype(BF16)
    lo = (r - m.astype(F32)).astype(BF16)
    return h, m, lo


def _ones_dot(mat01, x):
    h, m, lo = _split3(x)
    d = functools.partial(lax.dot_general, dimension_numbers=_NN, preferred_element_type=F32)
    return d(mat01, h) + d(mat01, m) + d(mat01, lo)


def _matmul(name, a, b, *, dims, grid_mnk, tiles, a_spec, b_spec, extras=(), extra_specs=(), out_shapes, out_specs, epilogue, after=None):
    gm, gn, nk = grid_mnk
    tm, tn = tiles
    n_ex, n_out = len(extras), len(out_shapes)
    held = [] if after is None else [after]

    def body(*refs):
        a_ref, b_ref = refs[0], refs[1]
        ex = refs[2:2 + n_ex]
        outs = refs[2 + n_ex + len(held):2 + n_ex + len(held) + n_out]
        if nk == 1:
            epilogue(lax.dot_general(a_ref[...], b_ref[...], dims, preferred_element_type=F32), ex, outs)
            return
        acc = refs[-1]
        k = pl.program_id(2)

        @pl.when(k == 0)
        def _():
            acc[...] = jnp.zeros_like(acc)

        acc[...] += lax.dot_general(a_ref[...], b_ref[...], dims, preferred_element_type=F32)

        @pl.when(k == nk - 1)
        def _():
            epilogue(acc[...], ex, outs)

    return pl.pallas_call(
        body, name=name, grid=(gm, gn, nk), in_specs=[a_spec, b_spec, *extra_specs] + [pl.BlockSpec(memory_space=pl.ANY)] * len(held),
        out_specs=list(out_specs), out_shape=list(out_shapes), scratch_shapes=[] if nk == 1 else [pltpu.VMEM((tm, tn), F32)],
        compiler_params=_cparams(("parallel", "parallel", "arbitrary")),
    )(a, b, *extras, *held)


def _store(dtype):
    def ep(acc, ex, outs):
        outs[0][...] = acc.astype(dtype)
    return ep


def _mm_nn_stacked(name, a, wg, *, tm, tn, tk, out_dtype=F32, extras=(), extra_specs=(), out_shapes=None, out_specs=None, epilogue=None):
    M, K = a.shape
    _, _, nloc = wg.shape
    N = nloc * N_DEV
    q = nloc // tn
    if out_shapes is None:
        out_shapes = [jax.ShapeDtypeStruct((M, N), out_dtype)]
        out_specs = [pl.BlockSpec((tm, tn), lambda i, j, k: (i, j))]
        epilogue = _store(out_dtype)
    return _matmul(
        name, a, wg, dims=_NN, grid_mnk=(M // tm, N // tn, K // tk), tiles=(tm, tn),
        a_spec=pl.BlockSpec((tm, tk), lambda i, j, k: (i, k)),
        b_spec=pl.BlockSpec((None, tk, tn), lambda i, j, k: (j // q, k, j % q)),
        extras=extras, extra_specs=extra_specs, out_shapes=out_shapes, out_specs=out_specs, epilogue=epilogue)


def _mm_nt_stacked(name, a_spec, a, wg, *, M, tm, tn, tk, out_dtype=F32, after=None):
    _, Kw, nloc = wg.shape
    q = nloc // tk
    return _matmul(
        name, a, wg, dims=_NT, grid_mnk=(M // tm, Kw // tn, (nloc * N_DEV) // tk), tiles=(tm, tn),
        a_spec=a_spec, b_spec=pl.BlockSpec((None, tn, tk), lambda i, j, k: (k // q, j, k % q)),
        out_shapes=[jax.ShapeDtypeStruct((M, Kw), out_dtype)], out_specs=[pl.BlockSpec((tm, tn), lambda i, j, k: (i, j))],
        epilogue=_store(out_dtype), after=after)[0]


def _mm_tn(name, a, b, b_spec, *, Mo, No, S, tm, tn, tk, stacked_nloc=None, after=None, a_off=0):
    if stacked_nloc is None:
        out_shape = jax.ShapeDtypeStruct((Mo, No), BF16)
        out_spec = pl.BlockSpec((tm, tn), lambda i, j, k: (i, j))
    else:
        q = stacked_nloc // tn
        out_shape = jax.ShapeDtypeStruct((N_DEV, Mo, stacked_nloc), BF16)
        out_spec = pl.BlockSpec((None, tm, tn), lambda i, j, k: (j // q, i, j % q))
    return _matmul(
        name, a, b, dims=_TN, grid_mnk=(Mo // tm, No // tn, S // tk), tiles=(tm, tn),
        a_spec=pl.BlockSpec((tk, tm), lambda i, j, k: (k, i + a_off)), b_spec=b_spec,
        out_shapes=[out_shape], out_specs=[out_spec], epilogue=_store(BF16), after=after)[0]


def _norm_mod(name, x, g, sc, sh):
    S, D = x.shape
    tm = _tile(S, 256, 8)

    def body(x_ref, g_ref, sc_ref, sh_ref, h_ref):
        xv = x_ref[...]
        r = lax.rsqrt(jnp.mean(xv * xv, axis=-1, keepdims=True) + EPS)
        h = (xv * r) * g_ref[...]
        h_ref[...] = (h * (1.0 + sc_ref[...]) + sh_ref[...]).astype(BF16)

    row = pl.BlockSpec((tm, D), lambda i: (i, 0))
    vec = pl.BlockSpec((1, D), lambda i: (0, 0))
    return pl.pallas_call(body, name=name, grid=(S // tm,), in_specs=[row, vec, vec, vec], out_specs=row,
                          out_shape=jax.ShapeDtypeStruct((S, D), BF16), compiler_params=_cparams(("parallel",)))(x, g, sc, sh)


def _norm_mod_bwd(name, dh, x, g, sc, dres, o=None, gt=None):
    S, D = x.shape
    tm = _tile(S, 256, 8)
    gated = o is not None

    def body(*refs):
        if gated:
            dh_ref, x_ref, g_ref, sc_ref, dres_ref, o_ref, gt_ref, dx_ref, vec_ref, do_ref = refs
        else:
            dh_ref, x_ref, g_ref, sc_ref, dres_ref, dx_ref, vec_ref = refs
        i = pl.program_id(0)

        @pl.when(i == 0)
        def _():
            vec_ref[...] = jnp.zeros_like(vec_ref)

        xv, dh_v, gv = x_ref[...], dh_ref[...], g_ref[...]
        r = lax.rsqrt(jnp.mean(xv * xv, axis=-1, keepdims=True) + EPS)
        xn = xv * r
        one_sc = 1.0 + sc_ref[...]
        vec_ref[0:1, :] += jnp.sum(dh_v, axis=0, keepdims=True)
        vec_ref[1:2, :] += jnp.sum(dh_v * (xn * gv), axis=0, keepdims=True)
        vec_ref[2:3, :] += jnp.sum(dh_v * one_sc * xn, axis=0, keepdims=True)
        dxn = dh_v * one_sc * gv
        dx = dres_ref[...] + r * (dxn - xn * jnp.mean(dxn * xn, axis=-1, keepdims=True))
        dx_ref[...] = dx
        if gated:
            vec_ref[3:4, :] += jnp.sum(dx * o_ref[...], axis=0, keepdims=True)
            do_ref[...] = (dx * gt_ref[...]).astype(BF16)

    row = pl.BlockSpec((tm, D), lambda i: (i, 0))
    vec = pl.BlockSpec((1, D), lambda i: (0, 0))
    acc = pl.BlockSpec((8, D), lambda i: (0, 0))
    ins = [dh, x, g, sc, dres] + ([o, gt] if gated else [])
    in_specs = [row, row, vec, vec, row] + ([row, vec] if gated else [])
    out_shape = [jax.ShapeDtypeStruct((S, D), F32), jax.ShapeDtypeStruct((8, D), F32)]
    out_specs = [row, acc]
    if gated:
        out_shape.append(jax.ShapeDtypeStruct((S, D), BF16))
        out_specs.append(row)
    return pl.pallas_call(body, name=name, grid=(S // tm,), in_specs=in_specs, out_specs=out_specs, out_shape=out_shape,
                          compiler_params=_cparams(("arbitrary",)))(*ins)


def _loss_head(x3, tgt, gf, o2, gt2):
    S, D = x3.shape
    tm = _tile(S, 256, 8)

    def body(x_ref, t_ref, g_ref, o_ref, gt_ref, dx_ref, do_ref, vec_ref):
        i = pl.program_id(0)

        @pl.when(i == 0)
        def _():
            vec_ref[...] = jnp.zeros_like(vec_ref)

        xv, gv = x_ref[...], g_ref[...]
        r = lax.rsqrt(jnp.mean(xv * xv, axis=-1, keepdims=True) + EPS)
        xn = xv * r
        e = xn * gv - t_ref[...]
        tok = 0.5 * jnp.mean(e * e, axis=-1, keepdims=True)
        vec_ref[0:1, :] += jnp.broadcast_to(jnp.sum(tok, axis=0, keepdims=True), (1, D))
        dy = e * (1.0 / D)
        vec_ref[1:2, :] += jnp.sum(dy * xn, axis=0, keepdims=True)
        dxn = dy * gv
        dx = r * (dxn - xn * jnp.mean(dxn * xn, axis=-1, keepdims=True))
        dx_ref[...] = dx
        vec_ref[2:3, :] += jnp.sum(dx * o_ref[...], axis=0, keepdims=True)
        do_ref[...] = (dx * gt_ref[...]).astype(BF16)

    row = pl.BlockSpec((tm, D), lambda i: (i, 0))
    vec = pl.BlockSpec((1, D), lambda i: (0, 0))
    return pl.pallas_call(
        body, name="loss_head", grid=(S // tm,), in_specs=[row, row, vec, row, vec],
        out_specs=[row, row, pl.BlockSpec((8, D), lambda i: (0, 0))],
        out_shape=[jax.ShapeDtypeStruct((S, D), F32), jax.ShapeDtypeStruct((S, D), BF16), jax.ShapeDtypeStruct((8, D), F32)],
        compiler_params=_cparams(("arbitrary",)))(x3, tgt, gf, o2, gt2)


def _ffn_in_swiglu(h, wg):
    S, D = h.shape
    _, _, tf = wg.shape
    nf = N_DEV // 2
    F = nf * tf
    tm = _tile(S, 256, 16)

    def body(h_ref, wa_ref, wu_ref, hf_ref, fac_ref):
        hv = h_ref[...]
        a = lax.dot_general(hv, wa_ref[...], _NN, preferred_element_type=F32)
        up = lax.dot_general(hv, wu_ref[...], _NN, preferred_element_type=F32)
        sa = _sigmoid(a)
        silu = a * sa
        hf_ref[...] = (silu * up).astype(BF16)
        fac_ref[0] = (up * (sa * (1.0 + a * (1.0 - sa)))).astype(BF16)
        fac_ref[1] = silu.astype(BF16)

    return pl.pallas_call(
        body, name="ffn_in_swiglu", grid=(nf, S // tm),
        in_specs=[pl.BlockSpec((tm, D), lambda j, i: (i, 0)), pl.BlockSpec((None, D, tf), lambda j, i: (j, 0, 0)),
                  pl.BlockSpec((None, D, tf), lambda j, i: (j + nf, 0, 0))],
        out_specs=[pl.BlockSpec((tm, tf), lambda j, i: (i, j)), pl.BlockSpec((2, tm, tf), lambda j, i: (0, i, j))],
        out_shape=[jax.ShapeDtypeStruct((S, F), BF16), jax.ShapeDtypeStruct((2, S, F), BF16)],
        compiler_params=_cparams(("parallel", "parallel")))(h, wg, wg)


def _colsum2(dg2):
    _, S, D = dg2.shape
    tm = _tile(S, 256, 16)

    def body(x_ref, o_ref):
        @pl.when(pl.program_id(0) == 0)
        def _():
            o_ref[...] = jnp.zeros_like(o_ref)

        o_ref[0:1, :] += jnp.sum(x_ref[0].astype(F32), axis=0, keepdims=True)
        o_ref[1:2, :] += jnp.sum(x_ref[1].astype(F32), axis=0, keepdims=True)

    return pl.pallas_call(body, name="gate_bias_grad", grid=(S // tm,), in_specs=[pl.BlockSpec((2, tm, D), lambda i: (0, i, 0))],
                          out_specs=pl.BlockSpec((2, D), lambda i: (0, 0)), out_shape=jax.ShapeDtypeStruct((2, D), F32),
                          compiler_params=_cparams(("arbitrary",)))(dg2)


def _gmlp_common(u_ref, v_ref, lg_ref, lb_ref, ws_ref, bsb_ref, G, T, Dg):
    ug, dug = _gelu_parts(u_ref[...])
    vg, dvg = _gelu_parts(v_ref[...])
    mu = jnp.mean(vg, axis=-1, keepdims=True)
    vc = vg - mu
    rstd = lax.rsqrt(jnp.mean(vc * vc, axis=-1, keepdims=True) + EPS)
    vhat = vc * rstd
    vn = vhat * lg_ref[...] + lb_ref[...]
    row = lax.broadcasted_iota(jnp.int32, (T, T), 0)
    col = lax.broadcasted_iota(jnp.int32, (T, T), 1)
    tril = row >= col
    s = []
    for g in range(G):
        w = jnp.where(tril, ws_ref[g], 0.0)
        s.append(_dot(w, vn[:, g * Dg:(g + 1) * Dg]) + bsb_ref[g])
    return ug, dug, dvg, rstd, vhat, vn, tril, s


def _gmlp_fwd(z, ln_g, ln_b, ws, bsb, GW):
    S = z.shape[0]
    G, T, _ = ws.shape
    Dg = GW // G

    def body(u_ref, v_ref, lg_ref, lb_ref, ws_ref, bsb_ref, ya_ref):
        ug, _, _, _, _, _, _, s = _gmlp_common(u_ref, v_ref, lg_ref, lb_ref, ws_ref, bsb_ref, G, T, Dg)
        for g in range(G):
            sl = slice(g * Dg, (g + 1) * Dg)
            ya_ref[:, sl] = (ug[:, sl] * s[g]).astype(BF16)

    vec = pl.BlockSpec((1, GW), lambda c: (0, 0))
    return pl.pallas_call(
        body, name="gmlp_fwd", grid=(S // T,),
        in_specs=[pl.BlockSpec((T, GW), lambda c: (c, 0)), pl.BlockSpec((T, GW), lambda c: (c, 1)), vec, vec,
                  pl.BlockSpec((G, T, T), lambda c: (0, 0, 0)), pl.BlockSpec((G, T, Dg), lambda c: (0, 0, 0))],
        out_specs=pl.BlockSpec((T, GW), lambda c: (c, 0)), out_shape=jax.ShapeDtypeStruct((S, GW), BF16),
        compiler_params=_cparams(("parallel",)))(z, z, ln_g, ln_b, ws, bsb)


def _gmlp_bwd(z, dya, ln_g, ln_b, ws, bsb, GW):
    S = z.shape[0]
    G, T, _ = ws.shape
    Dg = GW // G
    nc = S // T

    def body(u_ref, v_ref, dya_ref, lg_ref, lb_ref, ws_ref, bsb_ref, dz_ref, dln_ref, dws_ref, dbs_ref, dbs_acc, dvh):
        c = pl.program_id(0)

        @pl.when(c == 0)
        def _():
            dln_ref[...] = jnp.zeros_like(dln_ref)
            dws_ref[...] = jnp.zeros_like(dws_ref)
            dbs_acc[...] = jnp.zeros_like(dbs_acc)

        ug, dug, dvg, rstd, vhat, vn, tril, s = _gmlp_common(u_ref, v_ref, lg_ref, lb_ref, ws_ref, bsb_ref, G, T, Dg)
        dya_v = dya_ref[...]
        for g in range(G):
            sl = slice(g * Dg, (g + 1) * Dg)
            dy_g = dya_v[:, sl]
            dz_ref[:, sl] = (dy_g * s[g] * dug[:, sl]).astype(BF16)
            ds = dy_g * ug[:, sl]
            dbs_acc[g] += ds
            w = jnp.where(tril, ws_ref[g], 0.0)
            dvn_g = _dot(w, ds, _TN)
            dws_ref[g] += jnp.where(tril, _dot(ds, vn[:, sl], _NT), 0.0)
            dln_ref[0:1, sl] += jnp.sum(dvn_g * vhat[:, sl], axis=0, keepdims=True)
            dln_ref[1:2, sl] += jnp.sum(dvn_g, axis=0, keepdims=True)
            dvh[:, sl] = dvn_g * lg_ref[:, sl]
        dvhat = dvh[...]
        m1 = jnp.mean(dvhat, axis=-1, keepdims=True)
        m2 = jnp.mean(dvhat * vhat, axis=-1, keepdims=True)
        dz_ref[:, GW:2 * GW] = (rstd * (dvhat - m1 - vhat * m2) * dvg).astype(BF16)

        @pl.when(c == nc - 1)
        def _():
            for g in range(G):
                dbs_ref[g] = jnp.sum(dbs_acc[g], axis=-1, keepdims=True)

    vec = pl.BlockSpec((1, GW), lambda c: (0, 0))
    return pl.pallas_call(
        body, name="gmlp_bwd", grid=(nc,),
        in_specs=[pl.BlockSpec((T, GW), lambda c: (c, 0)), pl.BlockSpec((T, GW), lambda c: (c, 1)),
                  pl.BlockSpec((T, GW), lambda c: (c, 0)), vec, vec,
                  pl.BlockSpec((G, T, T), lambda c: (0, 0, 0)), pl.BlockSpec((G, T, Dg), lambda c: (0, 0, 0))],
        out_specs=[pl.BlockSpec((T, 2 * GW), lambda c: (c, 0)), pl.BlockSpec((8, GW), lambda c: (0, 0)),
                   pl.BlockSpec((G, T, T), lambda c: (0, 0, 0)), pl.BlockSpec((G, T, 1), lambda c: (0, 0, 0))],
        out_shape=[jax.ShapeDtypeStruct((S, 2 * GW), BF16), jax.ShapeDtypeStruct((8, GW), F32),
                   jax.ShapeDtypeStruct((G, T, T), F32), jax.ShapeDtypeStruct((G, T, 1), F32)],
        scratch_shapes=[pltpu.VMEM((G, T, Dg), F32), pltpu.VMEM((T, GW), F32)],
        compiler_params=_cparams(("arbitrary",)))(z, z, dya, ln_g, ln_b, ws, bsb)


def _hg_common(q_ref, f_ref, hlb_ref):
    C = HG_CHUNK
    a = hlb_ref[...]
    lb = _sigmoid(a[0:1, :] - a[1:2, :])
    sig = _sigmoid(f_ref[...])
    f = lb + (1.0 - lb) * sig
    lf = jnp.log(f)
    kk = 1.0 - f
    q = q_ref[...]
    sq = _sigmoid(q)
    qa = q * sq
    row = lax.broadcasted_iota(jnp.int32, (C, C), 0)
    col = lax.broadcasted_iota(jnp.int32, (C, C), 1)
    tril = row >= col
    b = _ones_dot(tril.astype(BF16), lf)
    bm = b[HG_MID:HG_MID + 1, :]
    bl = b[C - 1:C, :]
    e_b = jnp.exp(b)
    e_qm = jnp.exp(jnp.minimum(b - bm, EXP_CLAMP))
    e_km = jnp.exp(jnp.minimum(bm - b, EXP_CLAMP))
    e_kl = jnp.exp(bl - b)
    return dict(lb=lb, sig=sig, f=f, kk=kk, q=q, sq=sq, qa=qa, tril=tril, e_b=e_b, e_qm=e_qm, e_km=e_km, e_kl=e_kl,
                e_l=jnp.exp(bl), qh=qa * e_b, qt=qa * e_qm, kt=kk * e_km, kh=kk * e_kl)


def _hg_fwd(z, hg_lb, ng, HW):
    S = z.shape[0]
    C, H, dk = HG_CHUNK, HW // HG_DK, HG_DK
    nc = S // C

    def body(q_ref, f_ref, i_ref, og_ref, hlb_ref, ng_ref, yb_ref, o_ref, st_ref, state):
        @pl.when(pl.program_id(0) == 0)
        def _():
            state[...] = jnp.zeros_like(state)

        t = _hg_common(q_ref, f_ref, hlb_ref)
        iv = i_ref[...]
        for h in range(H):
            sl = slice(h * dk, (h + 1) * dk)
            st = state[h]
            st_ref[h] = st
            a = jnp.where(t["tril"], _dot(t["qt"][:, sl], t["kt"][:, sl], _NT), 0.0)
            o_h = _dot(a, iv[:, sl]) + _dot(t["qh"][:, sl], st, _NT)
            state[h] = st * t["e_l"][:, sl] + _dot(iv[:, sl], t["kh"][:, sl], _TN)
            o_ref[:, sl] = o_h
            rr = lax.rsqrt(jnp.mean(o_h * o_h, axis=-1, keepdims=True) + EPS)
            og = og_ref[:, sl]
            yb_ref[:, sl] = (o_h * rr * ng_ref[:, sl] * (og * _sigmoid(og))).astype(BF16)

    def col(k):
        return pl.BlockSpec((C, HW), lambda c: (c, k))

    base = 2
    return pl.pallas_call(
        body, name="hgrn_fwd", grid=(nc,),
        in_specs=[col(base), col(base + 1), col(base + 2), col(base + 3),
                  pl.BlockSpec((2, HW), lambda c: (0, 0)), pl.BlockSpec((1, HW), lambda c: (0, 0))],
        out_specs=[pl.BlockSpec((C, HW), lambda c: (c, 0)), pl.BlockSpec((C, HW), lambda c: (c, 0)),
                   pl.BlockSpec((None, H, dk, dk), lambda c: (c, 0, 0, 0))],
        out_shape=[jax.ShapeDtypeStruct((S, HW), BF16), jax.ShapeDtypeStruct((S, HW), F32),
                   jax.ShapeDtypeStruct((nc, H, dk, dk), F32)],
        scratch_shapes=[pltpu.VMEM((H, dk, dk), F32)],
        compiler_params=_cparams(("arbitrary",)))(z, z, z, z, hg_lb, ng)


def _hg_bwd(z, o, states, dyb, hg_lb, ng, HW, dz_head, dz_tail):
    S = z.shape[0]
    C, H, dk = HG_CHUNK, HW // HG_DK, HG_DK
    nc = S // C
    B0 = dz_head.shape[1]
    DT = dz_tail.shape[2]
    INW = B0 + 4 * HW + 2 * DT

    def body(q_ref, f_ref, i_ref, og_ref, o_ref, st_ref, stn_ref, dyb_ref, hlb_ref, ng_ref, head_ref, tail_ref,
             dzf_ref, dng_ref, dhlb_ref, dstate, cross, dqa_buf, dkk_buf, db_buf, dlb_acc):
        c = pl.program_id(0)
        dzf_ref[:, 0:B0] = head_ref[...]
        dzf_ref[:, B0 + 4 * HW:B0 + 4 * HW + DT] = tail_ref[0]
        dzf_ref[:, B0 + 4 * HW + DT:INW] = tail_ref[1]
        dz_ref = dzf_ref.at[:, B0:B0 + 4 * HW]

        @pl.when(c == 0)
        def _():
            dstate[...] = jnp.zeros_like(dstate)
            dlb_acc[...] = jnp.zeros_like(dlb_acc)
            dng_ref[...] = jnp.zeros_like(dng_ref)

        def r16(v):
            return v.astype(BF16).astype(F32)

        t = _hg_common(q_ref, f_ref, hlb_ref)
        iv = i_ref[...]
        for h in range(H):
            sl = slice(h * dk, (h + 1) * dk)
            o_h, og, dyb_h, ng_h = o_ref[:, sl], og_ref[:, sl], dyb_ref[:, sl], ng_ref[:, sl]
            sg = _sigmoid(og)
            silu_og = og * sg
            rr = lax.rsqrt(jnp.mean(o_h * o_h, axis=-1, keepdims=True) + EPS)
            on = o_h * rr
            dng_ref[0:1, sl] += jnp.sum(dyb_h * on * silu_og, axis=0, keepdims=True)
            dz_ref[:, 3 * HW + h * dk:3 * HW + (h + 1) * dk] = (dyb_h * on * ng_h * (sg * (1.0 + og * (1.0 - sg)))).astype(BF16)
            don = dyb_h * ng_h * silu_og
            do_h = rr * (don - on * jnp.mean(don * on, axis=-1, keepdims=True))

            qt, kt, qh, kh, iv_h = t["qt"][:, sl], t["kt"][:, sl], t["qh"][:, sl], t["kh"][:, sl], iv[:, sl]
            a = jnp.where(t["tril"], _dot(qt, kt, _NT), 0.0)
            da = jnp.where(t["tril"], _dot(do_h, iv_h, _NT), 0.0)
            st, dst = st_ref[h], dstate[h]
            cross[:, sl] = jnp.sum(dst * stn_ref[h], axis=0, keepdims=True)
            dqh = _dot(do_h, st)
            dstate[h] = _dot(do_h, qh, _TN) + dst * t["e_l"][:, sl]
            div = _dot(a, do_h, _TN) + _dot(kh, dst, _NT)
            dkh = _dot(iv_h, dst)
            dqt = _dot(da, kt)
            dkt = _dot(da, qt, _TN)
            dz_ref[:, 2 * HW + h * dk:2 * HW + (h + 1) * dk] = div.astype(BF16)
            dqa_buf[:, sl] = dqh * t["e_b"][:, sl] + dqt * t["e_qm"][:, sl]
            dkk_buf[:, sl] = dkt * t["e_km"][:, sl] + dkh * t["e_kl"][:, sl]
            db_buf[:, sl] = r16(qt) * dqt - r16(kt) * dkt + r16(qh) * dqh - r16(kh) * dkh

        dqa, dkk = dqa_buf[...], dkk_buf[...]
        triu = jnp.logical_not(t["tril"]) | (lax.broadcasted_iota(jnp.int32, (C, C), 0) == lax.broadcasted_iota(jnp.int32, (C, C), 1))
        dlf = _ones_dot(triu.astype(BF16), db_buf[...]) + cross[...]
        df = dlf / t["f"] - dkk
        sig, lb = t["sig"], t["lb"]
        dz_ref[:, HW:2 * HW] = (df * (1.0 - lb) * sig * (1.0 - sig)).astype(BF16)
        dlb_acc[...] += jnp.sum(df * (1.0 - sig), axis=0, keepdims=True)
        q, sq = t["q"], t["sq"]
        dz_ref[:, 0:HW] = (dqa * (sq * (1.0 + q * (1.0 - sq)))).astype(BF16)

        @pl.when(c == nc - 1)
        def _():
            da0 = dlb_acc[...] * lb * (1.0 - lb)
            dhlb_ref[0:1, :] = da0
            dhlb_ref[1:2, :] = -da0

    def col(k):
        return pl.BlockSpec((C, HW), lambda c: (nc - 1 - c, k))

    base = 2
    return pl.pallas_call(
        body, name="hgrn_bwd", grid=(nc,),
        in_specs=[col(base), col(base + 1), col(base + 2), col(base + 3), col(0),
                  pl.BlockSpec((None, H, dk, dk), lambda c: (nc - 1 - c, 0, 0, 0)),
                  pl.BlockSpec((None, H, dk, dk), lambda c: (jnp.minimum(nc - c, nc - 1), 0, 0, 0)), col(0),
                  pl.BlockSpec((2, HW), lambda c: (0, 0)), pl.BlockSpec((1, HW), lambda c: (0, 0)),
                  pl.BlockSpec((C, B0), lambda c: (nc - 1 - c, 0)), pl.BlockSpec((2, C, DT), lambda c: (0, nc - 1 - c, 0))],
        out_specs=[pl.BlockSpec((C, INW), lambda c: (nc - 1 - c, 0)), pl.BlockSpec((8, HW), lambda c: (0, 0)),
                   pl.BlockSpec((2, HW), lambda c: (0, 0))],
        out_shape=[jax.ShapeDtypeStruct((S, INW), BF16), jax.ShapeDtypeStruct((8, HW), F32), jax.ShapeDtypeStruct((2, HW), F32)],
        scratch_shapes=[pltpu.VMEM((H, dk, dk), F32), pltpu.VMEM((1, HW), F32), pltpu.VMEM((C, HW), F32), pltpu.VMEM((C, HW), F32),
                        pltpu.VMEM((C, HW), F32), pltpu.VMEM((1, HW), F32)],
        compiler_params=_cparams(("arbitrary",)))(z, z, z, z, o, states, states, dyb, hg_lb, ng, dz_head, dz_tail)


def _position():
    x, y, c = lax.axis_index("x"), lax.axis_index("y"), lax.axis_index("c")
    return x, y, c, 4 * x + 2 * y + c


def _flip(x, y, c, k):
    return (1 - x if k & 4 else x, 1 - y if k & 2 else y, 1 - c if k & 1 else c)


def _allgather_small(name, v):
    R, L = v.shape

    def body(v_ref, out_ref, send_sems, recv_sems):
        x, y, c, me = _position()
        out_ref[me] = v_ref[...]
        copies = []
        for k in range(1, N_DEV):
            cp = pltpu.make_async_remote_copy(src_ref=v_ref, dst_ref=out_ref.at[me], send_sem=send_sems.at[k - 1],
                                              recv_sem=recv_sems.at[k - 1], device_id=_flip(x, y, c, k), device_id_type=MESH)
            cp.start()
            copies.append(cp)
        for cp in copies:
            cp.wait()

    return pl.pallas_call(
        body, name=name, out_shape=jax.ShapeDtypeStruct((N_DEV, R, L), v.dtype),
        in_specs=[pl.BlockSpec(memory_space=pltpu.VMEM)], out_specs=pl.BlockSpec(memory_space=pltpu.VMEM),
        scratch_shapes=[pltpu.SemaphoreType.DMA((N_DEV - 1,)), pltpu.SemaphoreType.DMA((N_DEV - 1,))],
        compiler_params=pltpu.CompilerParams(vmem_limit_bytes=VMEM_LIMIT),
    )(v)


def _allgather_hbm(name, shards):
    n = len(shards)

    def body(*refs):
        ins, outs = refs[:n], refs[n:2 * n]
        send_sems, recv_sems, local_sems = refs[2 * n:]
        x, y, c, me = _position()
        sibling = (x, y, 1 - c)
        chips = [(1 - x, y), (x, 1 - y), (1 - x, 1 - y)]

        def slot(px, py, pc):
            return 4 * px + 2 * py + pc

        def copy(w, k, block, to, src=None):
            dst = outs[w].at[slot(*block)]
            return pltpu.make_async_remote_copy(src_ref=dst if src is None else src, dst_ref=dst, send_sem=send_sems.at[w, k],
                                                recv_sem=recv_sems.at[w, k], device_id=to, device_id_type=MESH)

        mine, first, passed = [], [], []
        for w in range(n):
            cp = pltpu.make_async_copy(ins[w], outs[w].at[me], local_sems.at[w])
            cp.start()
            mine.append(cp)
            for j, chip in enumerate(chips):
                first.append(copy(w, 1 + j, (x, y, c), (*chip, c), src=ins[w]))
            first.append(copy(w, 0, (x, y, c), sibling, src=ins[w]))
        for cp in first:
            cp.start()
        for w in range(n):
            for j, chip in enumerate(chips):
                copy(w, 1 + j, (*chip, c), (x, y, c)).wait_recv()
                cp = copy(w, 4 + j, (*chip, c), sibling)
                cp.start()
                passed.append(cp)
        for w in range(n):
            copy(w, 0, sibling, (x, y, c)).wait_recv()
            for j, chip in enumerate(chips):
                copy(w, 4 + j, (*chip, 1 - c), (x, y, c)).wait_recv()
        for cp in first + passed:
            cp.wait_send()
        for cp in mine:
            cp.wait()

    hbm = pl.BlockSpec(memory_space=pltpu.HBM)
    return pl.pallas_call(
        body, name=name, out_shape=[jax.ShapeDtypeStruct((N_DEV, *s.shape), s.dtype) for s in shards],
        in_specs=[hbm] * n, out_specs=[hbm] * n,
        scratch_shapes=[pltpu.SemaphoreType.DMA((n, 7)), pltpu.SemaphoreType.DMA((n, 7)), pltpu.SemaphoreType.DMA((n,))],
    )(*shards)


_HBM = pl.BlockSpec(memory_space=pltpu.HBM)
_SEM = pl.BlockSpec(memory_space=pltpu.SEMAPHORE)
_EFFECT = pltpu.SideEffectType.DATAFLOW_SIDE_EFFECTING


def _split_start(name, bufs, n_sems, copies_fn, after=None):
    nb = len(bufs)
    extra = [] if after is None else [after]
    k = nb + len(extra)

    def body(*refs):
        for cp in copies_fn(refs[:nb], refs[k], refs[k + 1]):
            cp.start()
        refs[-1][...] = jnp.zeros_like(refs[-1])

    sems = pltpu.SemaphoreType.DMA((n_sems,))
    res = pl.pallas_call(
        body, name=name,
        out_shape=(sems, sems, *[pltpu.HBM(a.shape, a.dtype) for a in bufs], jax.ShapeDtypeStruct((8, LANES), F32)),
        in_specs=[_HBM] * nb + [pl.BlockSpec(memory_space=pl.ANY)] * len(extra),
        out_specs=(_SEM, _SEM, *[_HBM] * nb, pl.BlockSpec(memory_space=pltpu.VMEM)),
        input_output_aliases={i: 2 + i for i in range(nb)},
        compiler_params=pltpu.CompilerParams(has_side_effects=_EFFECT),
    )(*[pltpu.with_memory_space_constraint(a, pltpu.HBM) for a in bufs], *extra)
    return res[0], res[1], list(res[2:2 + nb]), res[-1]


def _split_wait(name, bufs, send_sems, recv_sems, after, copies_fn):
    nb = len(bufs)

    def body(*refs):
        for cp in copies_fn(refs[:nb], refs[nb], refs[nb + 1]):
            cp.wait_send()
            cp.wait_recv()

    res = pl.pallas_call(
        body, name=name, out_shape=tuple(pltpu.HBM(a.shape, a.dtype) for a in bufs),
        in_specs=[_HBM] * nb + [_SEM, _SEM, pl.BlockSpec(memory_space=pl.ANY)], out_specs=tuple([_HBM] * nb),
        input_output_aliases={i: i for i in range(nb)},
        compiler_params=pltpu.CompilerParams(has_side_effects=_EFFECT),
    )(*bufs, send_sems, recv_sems, after)
    return list(res)


def _split_relay(name, bufs, send_sems, recv_sems, after, wait_fn, n_sems, start_fn):
    nb = len(bufs)

    def body(*refs):
        for cp in wait_fn(refs[:nb], refs[nb], refs[nb + 1]):
            cp.wait_send()
            cp.wait_recv()
        for cp in start_fn(refs[:nb], refs[nb + 3], refs[nb + 4]):
            cp.start()

    sems = pltpu.SemaphoreType.DMA((n_sems,))
    res = pl.pallas_call(
        body, name=name, out_shape=(sems, sems, *[pltpu.HBM(a.shape, a.dtype) for a in bufs]),
        in_specs=[_HBM] * nb + [_SEM, _SEM, pl.BlockSpec(memory_space=pl.ANY)], out_specs=(_SEM, _SEM, *[_HBM] * nb),
        input_output_aliases={i: 2 + i for i in range(nb)},
        compiler_params=pltpu.CompilerParams(has_side_effects=_EFFECT),
    )(*bufs, send_sems, recv_sems, after)
    return res[0], res[1], list(res[2:])


N_CHIP = 4


def _chip_flip(x, y, k):
    return (1 - x if k & 2 else x), (1 - y if k & 1 else y)


def _gather_first_copies(n):
    def copies(bufs, send_sems, recv_sems):
        x, y, c, me = _position()
        out = []
        for w in range(n):
            for k in range(N_CHIP):
                to = (x, y, 1 - c) if k == 0 else (*_chip_flip(x, y, k), c)
                out.append(pltpu.make_async_remote_copy(
                    src_ref=bufs[w], dst_ref=bufs[n + w].at[me], send_sem=send_sems.at[w * N_CHIP + k],
                    recv_sem=recv_sems.at[w * N_CHIP + k], device_id=to, device_id_type=MESH))
        return out
    return copies


def _gather_relay_copies(n):
    def copies(bufs, send_sems, recv_sems):
        x, y, c, _ = _position()
        out = []
        for w in range(n):
            for k in range(1, N_CHIP):
                px, py = _chip_flip(x, y, k)
                blk = bufs[n + w].at[4 * px + 2 * py + c]
                out.append(pltpu.make_async_remote_copy(
                    src_ref=blk, dst_ref=blk, send_sem=send_sems.at[w * (N_CHIP - 1) + k - 1],
                    recv_sem=recv_sems.at[w * (N_CHIP - 1) + k - 1], device_id=(x, y, 1 - c), device_id_type=MESH))
        return out
    return copies


def _to_sibling_copies(n):
    def copies(bufs, send_sems, recv_sems):
        x, y, c, _ = _position()
        out = []
        for w in range(n):
            for q in range(N_CHIP):
                out.append(pltpu.make_async_remote_copy(
                    src_ref=bufs[w].at[2 * q + 1 - c], dst_ref=bufs[n + w].at[q], send_sem=send_sems.at[w * N_CHIP + q],
                    recv_sem=recv_sems.at[w * N_CHIP + q], device_id=(x, y, 1 - c), device_id_type=MESH))
        return out
    return copies


def _to_owner_copies(n):
    def copies(bufs, send_sems, recv_sems):
        x, y, c, _ = _position()
        out = []
        for w in range(n):
            for k in range(1, N_CHIP):
                px, py = (1 - x if k & 2 else x), (1 - y if k & 1 else y)
                out.append(pltpu.make_async_remote_copy(
                    src_ref=bufs[w].at[2 * px + py], dst_ref=bufs[n + w].at[k - 1], send_sem=send_sems.at[w * (N_CHIP - 1) + k - 1],
                    recv_sem=recv_sems.at[w * (N_CHIP - 1) + k - 1], device_id=(px, py, c), device_id_type=MESH))
        return out
    return copies


def _chip_sum(name, stack, landed, c_idx):
    _, R, C = stack.shape
    tr = _tile(R, max(16, 1048576 // C), 16)

    def body(c_ref, a_ref, b_ref, o_ref):
        o_ref[...] = (a_ref[...].astype(F32) + b_ref[...].astype(F32)).astype(o_ref.dtype)

    return pl.pallas_call(
        body, name=name,
        grid_spec=pltpu.PrefetchScalarGridSpec(
            num_scalar_prefetch=1, grid=(N_CHIP, R // tr),
            in_specs=[pl.BlockSpec((None, tr, C), lambda q, i, c_ref: (2 * q + c_ref[0], i, 0)),
                      pl.BlockSpec((None, tr, C), lambda q, i, c_ref: (q, i, 0))],
            out_specs=pl.BlockSpec((None, tr, C), lambda q, i, c_ref: (q, i, 0))),
        out_shape=jax.ShapeDtypeStruct((N_CHIP, R, C), stack.dtype),
        compiler_params=_cparams(("parallel", "parallel")))(c_idx, stack, landed)


def _ada_mod(c16, w):
    _, D = c16.shape
    n = w.shape[1]
    tk = _tile(D, 512)
    nk = D // tk

    def body(c_ref, w_ref, o_ref, ca_ref):
        @pl.when(pl.program_id(0) == 0)
        def _():
            o_ref[...] = jnp.zeros_like(o_ref)

        cv = c_ref[...]
        ca = cv * _sigmoid(cv)
        ca_ref[...] = ca
        o_ref[...] += _dot(ca, w_ref[...])

    return pl.pallas_call(
        body, name="ada_mod", grid=(nk,),
        in_specs=[pl.BlockSpec((16, tk), lambda k: (0, k)), pl.BlockSpec((tk, n), lambda k: (k, 0))],
        out_specs=[pl.BlockSpec((16, n), lambda k: (0, 0)), pl.BlockSpec((16, tk), lambda k: (0, k))],
        out_shape=[jax.ShapeDtypeStruct((16, n), F32), jax.ShapeDtypeStruct((16, D), F32)],
        compiler_params=_cparams(("arbitrary",)))(c16, w)


def _adam_math(w, g, m, v):
    m2 = ADAM_B1 * m + (1.0 - ADAM_B1) * g
    v2 = ADAM_B2 * v + (1.0 - ADAM_B2) * (g * g)
    m_hat = m2 / (1.0 - ADAM_B1 ** ADAM_STEP)
    v_hat = v2 / (1.0 - ADAM_B2 ** ADAM_STEP)
    delta = -ADAM_LR * (m_hat / (jnp.sqrt(v_hat) + ADAM_EPS) + ADAM_WD * w)
    return delta, m2, v2


def _adamw(name, w, m, v, parts, row0=0, into=None):
    R, C = w.shape
    Rp = parts[0].shape[1]
    tr = _tile(Rp, max(16, 393216 // C), 16)
    off = row0 // tr
    n_p = len(parts)
    held = [] if into is None else list(into)

    def body(*refs):
        w_ref, m_ref, v_ref = refs[:3]
        g_ref, d_ref, m2_ref, v2_ref = refs[3 + n_p + len(held):]
        g = None
        for p_ref in refs[3:3 + n_p]:
            for s in range(p_ref.shape[0]):
                t = p_ref[s].astype(F32)
                g = t if g is None else g + t
        delta, m2, v2 = _adam_math(w_ref[...], g, m_ref[...], v_ref[...])
        g_ref[...] = g
        d_ref[...] = delta
        m2_ref[...] = m2
        v2_ref[...] = v2

    blk = pl.BlockSpec((tr, C), lambda i: (i + off, 0))
    out = jax.ShapeDtypeStruct((R, C), F32)
    return pl.pallas_call(
        body, name=name, grid=(Rp // tr,),
        in_specs=[blk, blk, blk] + [pl.BlockSpec((a.shape[0], tr, C), lambda i: (0, i, 0)) for a in parts]
        + [pl.BlockSpec(memory_space=pl.ANY)] * len(held),
        out_specs=[blk] * 4, out_shape=[out] * 4, input_output_aliases={3 + n_p + i: i for i in range(len(held))},
        compiler_params=_cparams(("parallel",)))(w, m, v, *parts, *held)


def _small_update(gathered, w, m, v, after):
    _, R, L = gathered.shape
    rs = w.shape[0]

    def body(p_ref, w_ref, m_ref, v_ref, after_ref, g_ref, d_ref, m2_ref, v2_ref):
        g = p_ref[0]
        for p in range(1, N_DEV):
            g = g + p_ref[p]
        g_ref[...] = g
        delta, m2, v2 = _adam_math(w_ref[...], g[0:rs, :], m_ref[...], v_ref[...])
        d_ref[...] = delta
        m2_ref[...] = m2
        v2_ref[...] = v2

    vm = pl.BlockSpec(memory_space=pltpu.VMEM)
    sm = jax.ShapeDtypeStruct((rs, L), F32)
    return pl.pallas_call(body, name="small_update", in_specs=[vm] * 4 + [pl.BlockSpec(memory_space=pl.ANY)], out_specs=[vm] * 4,
                          out_shape=[jax.ShapeDtypeStruct((R, L), F32), sm, sm, sm],
                          compiler_params=pltpu.CompilerParams(vmem_limit_bytes=VMEM_LIMIT))(gathered, w, m, v, after)


class _Fetched(dict):
    def __init__(self, fetch):
        super().__init__()
        self.fetch = fetch

    def first(self, key, after):
        self[key] = self.fetch(key, after)
        return self[key]


def _local_step(x, tgt, mod, p, fetch, F, scatter=None):
    S, D = x.shape
    GW, HW = p["ln_g"].shape[1], p["hg_ng"].shape[1]
    G, T, _ = p["ws"].shape
    w = _Fetched(fetch)
    INW = 2 * GW + 4 * HW + 2 * D
    in_loc, br_loc, fi_loc = INW // N_DEV, D // N_DEV, 2 * F // N_DEV
    assert GW == HW and F % fi_loc == 0
    sh1, sc1, gt1, sh2, sc2, gt2 = (mod[:, k * D:(k + 1) * D] for k in range(6))
    bsb = jnp.broadcast_to(p["bs"][:, :, None], (G, T, GW // G))

    tm = _tile(S, 1024, 16)
    tmh = _tile(S, 512, 16)
    tn_in = _tile(in_loc, 1280)
    tn_d = _tile(D, 512)
    tn_br = _tile(br_loc, 512)
    tk_s = S
    tm_w = _tile(D, 1024)
    g_off = 2 * GW + 4 * HW

    h1 = _norm_mod("norm1", x, p["norm1_g"], sc1, sh1)
    z = _mm_nn_stacked("proj_in", h1, w.first("in", h1), tm=tm, tn=tn_in, tk=D)[0]
    ya = _gmlp_fwd(z, p["ln_g"], p["ln_b"], p["ws"], bsb, GW)
    yb, o_hg, states = _hg_fwd(z, p["hg_lb"], p["hg_ng"], HW)
    pa = _mm_nn_stacked("branch_gmlp", ya, w.first("bg", z), tm=tm, tn=tn_br, tk=GW)[0]

    def gates(ga_ref, gb_ref, ba_ref, bb_ref):
        return _sigmoid(ga_ref[...] + ba_ref[...]), _sigmoid(gb_ref[...] + bb_ref[...])

    def gate_specs(tn_):
        o1, o2 = g_off // tn_, (g_off + D) // tn_
        return [pl.BlockSpec((tm, tn_), lambda i, j, k: (i, o1 + j)), pl.BlockSpec((tm, tn_), lambda i, j, k: (i, o2 + j)),
                pl.BlockSpec((1, tn_), lambda i, j, k: (0, j)), pl.BlockSpec((1, tn_), lambda i, j, k: (0, D // tn_ + j))]

    def merge_ep(acc, ex, outs):
        ga, gb = gates(*ex[1:5])
        outs[0][...] = acc
        outs[1][...] = (ga * ex[0][...] + gb * acc).astype(BF16)

    tile_o = pl.BlockSpec((tm, tn_br), lambda i, j, k: (i, j))
    pb, y = _mm_nn_stacked(
        "branch_hg_merge", yb, w.first("bh", z), tm=tm, tn=tn_br, tk=HW, extras=[pa, z, z, p["b_gate"], p["b_gate"]],
        extra_specs=[tile_o, *gate_specs(tn_br)], out_shapes=[jax.ShapeDtypeStruct((S, D), F32), jax.ShapeDtypeStruct((S, D), BF16)],
        out_specs=[tile_o, tile_o], epilogue=merge_ep)

    def resid_ep(acc, ex, outs):
        outs[0][...] = acc
        outs[1][...] = ex[0][...] + ex[1][...] * acc

    def resid_mm(name, a, b, res, gt, tm_):
        K = a.shape[1]
        t_o = pl.BlockSpec((tm_, tn_d), lambda i, j, k: (i, j))
        return _matmul(
            name, a, b, dims=_NN, grid_mnk=(S // tm_, D // tn_d, 1), tiles=(tm_, tn_d),
            a_spec=pl.BlockSpec((tm_, K), lambda i, j, k: (i, 0)), b_spec=pl.BlockSpec((K, tn_d), lambda i, j, k: (0, j)),
            extras=[res, gt], extra_specs=[t_o, pl.BlockSpec((1, tn_d), lambda i, j, k: (0, j))],
            out_shapes=[jax.ShapeDtypeStruct((S, D), F32)] * 2, out_specs=[t_o, t_o], epilogue=resid_ep)

    o1, xm = resid_mm("proj_out", y, w.first("out", z), x, gt1, tm)
    h2 = _norm_mod("norm2", xm, p["norm2_g"], sc2, sh2)
    hf, hf_fac = _ffn_in_swiglu(h2, w.first("fi", h2))
    o2, x3 = resid_mm("ffn_out", hf, w.first("fo", hf), xm, gt2, tmh)
    dx3, do2, vec_l = _loss_head(x3, tgt, p["final_g"], o2, gt2)

    nf = F // fi_loc

    def dswiglu_ep(acc, ex, outs):
        outs[0][0] = (acc * ex[0][0].astype(F32)).astype(BF16)
        outs[0][1] = (acc * ex[0][1].astype(F32)).astype(BF16)

    pair = pl.BlockSpec((2, tmh, fi_loc), lambda i, j, k: (0, i, j))
    dab = _matmul(
        "ffn_out_dx", do2, w["fo"], dims=_NT, grid_mnk=(S // tmh, nf, 1), tiles=(tmh, fi_loc),
        a_spec=pl.BlockSpec((tmh, D), lambda i, j, k: (i, 0)), b_spec=pl.BlockSpec((fi_loc, D), lambda i, j, k: (j, 0)),
        extras=[hf_fac], extra_specs=[pair], out_shapes=[jax.ShapeDtypeStruct((2, S, F), BF16)], out_specs=[pair],
        epilogue=dswiglu_ep)[0]
    start = (lambda name, grads: scatter[0](name, grads)) if scatter is not None else (lambda name, grads: None)
    push = (lambda name, after: scatter[1](name, after)) if scatter is not None else (lambda name, after: None)

    def zero(token):
        return 0.0 if token is None else token[0:1, 0:1]

    tm_f = _tile(F, 512)
    g_fo = _mm_tn("ffn_out_dw", hf, do2, pl.BlockSpec((tk_s, D), lambda i, j, k: (k, j)), Mo=F, No=D, S=S, tm=tm_f, tn=D, tk=tk_s)
    g_fi = _mm_tn("ffn_in_dw", h2, dab, pl.BlockSpec((None, tk_s, fi_loc), lambda i, j, k: (j // nf, k, j % nf)),
                  Mo=D, No=2 * F, S=S, tm=tm_w, tn=fi_loc, tk=tk_s, stacked_nloc=fi_loc, after=g_fo)
    t_ffn = start("scatter_ffn", dict(fo=g_fo, fi=g_fi))
    dh2 = _mm_nt_stacked("ffn_in_dx", pl.BlockSpec((None, tmh, fi_loc), lambda i, j, k: (k // nf, i, k % nf)), dab, w["fi"],
                         M=S, tm=tmh, tn=D, tk=fi_loc, after=t_ffn)
    dxm, vec2, do1 = _norm_mod_bwd("norm2_bwd", dh2, xm, p["norm2_g"], sc2, dx3, o1, gt1)
    t_ffn = push("scatter_ffn", dxm)

    def dmerge_ep(acc, ex, outs):
        ga, gb = gates(*ex[2:6])
        outs[0][...] = (acc * ga).astype(BF16)
        outs[1][...] = (acc * gb).astype(BF16)
        outs[2][0] = (acc * ex[0][...] * ga * (1.0 - ga)).astype(BF16)
        outs[2][1] = (acc * ex[1][...] * gb * (1.0 - gb)).astype(BF16)

    t_o = pl.BlockSpec((tm, tn_d), lambda i, j, k: (i, j))
    dpa, dpb, dg2 = _matmul(
        "proj_out_dx", do1, w["out"], dims=_NT, grid_mnk=(S // tm, D // tn_d, 1), tiles=(tm, tn_d),
        a_spec=pl.BlockSpec((tm, D), lambda i, j, k: (i, 0)), b_spec=pl.BlockSpec((tn_d, D), lambda i, j, k: (j, 0)),
        extras=[pa, pb, z, z, p["b_gate"], p["b_gate"]], extra_specs=[t_o, t_o, *gate_specs(tn_d)],
        out_shapes=[jax.ShapeDtypeStruct((S, D), BF16), jax.ShapeDtypeStruct((S, D), BF16), jax.ShapeDtypeStruct((2, S, D), BF16)],
        out_specs=[t_o, t_o, pl.BlockSpec((2, tm, tn_d), lambda i, j, k: (0, i, j))], epilogue=dmerge_ep, after=t_ffn)
    g_out = _mm_tn("proj_out_dw", y, do1, pl.BlockSpec((tk_s, D), lambda i, j, k: (k, j)), Mo=D, No=D, S=S, tm=tn_d, tn=D, tk=tk_s)
    tn_g = _tile(GW, 512)
    b_br = pl.BlockSpec((tk_s, br_loc), lambda i, j, k: (k, j))
    g_bg = _mm_tn("branch_gmlp_dw", ya, dpa, b_br, Mo=GW, No=D, S=S, tm=tn_g, tn=br_loc, tk=tk_s, stacked_nloc=br_loc)
    g_bh = _mm_tn("branch_hg_dw", yb, dpb, b_br, Mo=HW, No=D, S=S, tm=tn_g, tn=br_loc, tk=tk_s, stacked_nloc=br_loc)
    t_mix = start("scatter_mixer", dict(out=g_out, bg=g_bg, bh=g_bh))
    def branch_dx(name, dp, wg):
        flat = jnp.swapaxes(wg, 0, 1).reshape(wg.shape[1], D)
        return _matmul(
            name, dp, flat, dims=_NT, grid_mnk=(S // tm, GW // tn_g, 1), tiles=(tm, tn_g),
            a_spec=pl.BlockSpec((tm, D), lambda i, j, k: (i, 0)), b_spec=pl.BlockSpec((tn_g, D), lambda i, j, k: (j, 0)),
            out_shapes=[jax.ShapeDtypeStruct((S, GW), F32)], out_specs=[pl.BlockSpec((tm, tn_g), lambda i, j, k: (i, j))],
            epilogue=_store(F32), after=t_mix)[0]

    dya = branch_dx("branch_gmlp_dx", dpa, w["bg"])
    dyb = branch_dx("branch_hg_dx", dpb, w["bh"])
    db_gate = _colsum2(dg2)
    dz_gmlp, dln, dws, dbs = _gmlp_bwd(z, dya, p["ln_g"], p["ln_b"], p["ws"], bsb, GW)
    t_mix = push("scatter_mixer", dz_gmlp)
    dz, dng, dhlb = _hg_bwd(z, o_hg, states, dyb, p["hg_lb"], p["hg_ng"] + zero(t_mix), HW, dz_gmlp, dg2)
    half = D // 2
    tm_h = _tile(half, 1024)
    g_in = []
    t_in = None
    for hname, h in (("a", 0), ("b", 1)):
        g_in.append(_mm_tn("proj_in_dw_" + hname, h1, dz, pl.BlockSpec((tk_s, in_loc), lambda i, j, k: (k, j)), Mo=half, No=INW, S=S,
                           tm=tm_h, tn=in_loc, tk=tk_s, stacked_nloc=in_loc, after=t_in, a_off=h * (half // tm_h)))
        t_in = start("scatter_proj_in_" + hname, {"w_in_" + hname: g_in[-1]})
    t_in = push("scatter_proj_in_a", t_in)
    dh1 = _mm_nt_stacked("proj_in_dx", pl.BlockSpec((tmh, in_loc), lambda i, j, k: (i, k)), dz, w["in"], M=S, tm=tmh, tn=D, tk=in_loc,
                         after=t_in)
    dx, vec1 = _norm_mod_bwd("norm1_bwd", dh1, x, p["norm1_g"], sc1, dxm)

    dmod = jnp.concatenate([vec1[0:1], vec1[1:2], vec2[3:4], vec2[0:1], vec2[1:2], vec_l[2:3]], axis=1)
    small = dict(norm1_g=vec1[2:3], b_gate=db_gate.reshape(1, 2 * D), ln_g=dln[0:1], ln_b=dln[1:2], ws=dws, bs=dbs.reshape(G, T),
                 hg_lb=dhlb, hg_ng=dng[0:1], norm2_g=vec2[2:3], final_g=vec_l[1:2], loss=vec_l[0:1, 0:LANES])
    big = dict(w_in_a=g_in[0], w_in_b=g_in[1], bg=g_bg, bh=g_bh, out=g_out, fi=g_fi, fo=g_fo)
    return dx, big, small, dmod


_SMALL = ("b_ada", "norm1_g", "b_gate", "ln_g", "ln_b", "ws", "bs", "hg_lb", "hg_ng", "norm2_g", "final_g")


def _pack(parts, rows_mult=8):
    flat = [a.reshape(-1) for a in parts]
    offs, n = [], 0
    for a in flat:
        offs.append(n)
        n += a.shape[0]
    pad = (-n) % (LANES * rows_mult)
    if pad:
        flat.append(jnp.zeros((pad,), F32))
    return jnp.concatenate(flat).reshape(-1, LANES), offs


def kernel(x, c, w_ada, b_ada, norm1_g, w_in, b_gate, gmlp_ln_g, gmlp_ln_b, gmlp_ws, gmlp_bs, hg_lb, hg_norm_g, w_branch_gmlp, w_branch_hg, w_out, norm2_g, w_ffn_in, w_ffn_out, final_norm_g, loss_target, m_w_ada, m_b_ada, m_norm1_g, m_w_in, m_b_gate, m_gmlp_ln_g, m_gmlp_ln_b, m_gmlp_ws, m_gmlp_bs, m_hg_lb, m_hg_norm_g, m_w_branch_gmlp, m_w_branch_hg, m_w_out, m_norm2_g, m_w_ffn_in, m_w_ffn_out, m_final_norm_g, v_w_ada, v_b_ada, v_norm1_g, v_w_in, v_b_gate, v_gmlp_ln_g, v_gmlp_ln_b, v_gmlp_ws, v_gmlp_bs, v_hg_lb, v_hg_norm_g, v_w_branch_gmlp, v_w_branch_hg, v_w_out, v_norm2_g, v_w_ffn_in, v_w_ffn_out, v_final_norm_g):
    S, D = x.shape[1], x.shape[2]
    ada_loc = w_ada.shape[2]
    me = 4 * lax.axis_index("x") + 2 * lax.axis_index("y") + lax.axis_index("c")

    c_all = _allgather_small("gather_c", c.reshape(D // LANES, LANES)).reshape(N_DEV, D)
    mod_cols, c_act = _ada_mod(jnp.pad(c_all, ((0, 16 - N_DEV), (0, 0))), w_ada[0])
    mod_all = _allgather_small("gather_mod", mod_cols[:N_DEV].reshape(-1, LANES)).reshape(N_DEV, N_DEV, ada_loc)
    mod = lax.dynamic_index_in_dim(mod_all, me, axis=1, keepdims=False).reshape(1, N_DEV * ada_loc) + b_ada

    def empty_hbm(shape, dtype):
        return pltpu.with_memory_space_constraint(lax.empty(shape, dtype), pltpu.HBM)

    groups = dict(gather_in=dict(keys=["in"], src=[w_in]), gather_mixer=dict(keys=["bg", "bh", "out"], src=[w_branch_gmlp, w_branch_hg, w_out]),
                  gather_ffn_in=dict(keys=["fi"], src=[w_ffn_in]), gather_ffn_out=dict(keys=["fo"], src=[w_ffn_out]))
    group_of = {}
    prev = mod_all
    for gname, g in groups.items():
        n = len(g["keys"])
        shards = [a[0].astype(BF16) for a in g["src"]]
        outs = [lax.dynamic_update_slice(lax.empty((N_DEV, *s.shape), BF16), s[None], (me, 0, 0)) for s in shards]
        *g["hop1"], prev = _split_start(gname + "_hop1", shards + outs, n * N_CHIP, _gather_first_copies(n), after=prev)
        for k in g["keys"]:
            group_of[k] = gname
    mod = mod + prev[0:1, 0:1]

    def fetch(key, after):
        g = groups[group_of[key]]
        n = len(g["keys"])
        if "done" not in g:
            send_sems, recv_sems, bufs = g["hop1"]
            send_sems, recv_sems, bufs = _split_relay(group_of[key] + "_relay", bufs, send_sems, recv_sems, after,
                                                      _gather_first_copies(n), n * (N_CHIP - 1), _gather_relay_copies(n))
            bufs = _split_wait(group_of[key] + "_hop2_wait", bufs, send_sems, recv_sems, after, _gather_relay_copies(n))
            g["done"] = dict(zip(g["keys"], bufs[n:]))
        arr = g["done"][key]
        return arr.reshape(-1, D) if key in ("out", "fo") else arr

    p = dict(norm1_g=norm1_g, b_gate=b_gate, ln_g=gmlp_ln_g, ln_b=gmlp_ln_b, ws=gmlp_ws[0], bs=gmlp_bs[0], hg_lb=hg_lb,
             hg_ng=hg_norm_g, norm2_g=norm2_g, final_g=final_norm_g.reshape(1, D))

    in_flight = {}
    c_idx = lax.axis_index("c").astype(jnp.int32).reshape(1)
    my_chip = 2 * lax.axis_index("x") + lax.axis_index("y")

    def scatter_start(name, grads):
        keys = list(grads)
        n = len(keys)
        stacks = [grads[k].reshape(N_DEV, -1, grads[k].shape[-1]) for k in keys]
        lands = [empty_hbm((N_CHIP, *g.shape[1:]), g.dtype) for g in stacks]
        send_sems, recv_sems, bufs, token = _split_start(name + "_d2d", stacks + lands, n * N_CHIP, _to_sibling_copies(n))
        in_flight[name] = dict(keys=keys, stage1=(send_sems, recv_sems, bufs))
        return token

    def scatter_push(name, after):
        f = in_flight[name]
        n = len(f["keys"])
        send_sems, recv_sems, bufs = f["stage1"]
        bufs = _split_wait(name + "_d2d_wait", bufs, send_sems, recv_sems, after, _to_sibling_copies(n))
        sums = [_chip_sum(f"{name}_sum_{k}", bufs[i], bufs[n + i], c_idx) for i, k in enumerate(f["keys"])]
        lands = [empty_hbm((N_CHIP - 1, *s.shape[1:]), s.dtype) for s in sums]
        send_sems, recv_sems, bufs, token = _split_start(name + "_ici", sums + lands, n * (N_CHIP - 1), _to_owner_copies(n))
        f["stage2"] = (send_sems, recv_sems, bufs)
        return token

    grad_x, _, small, dmod = _local_step(x[0], loss_target[0], mod, p, fetch, w_ffn_out.shape[1] * N_DEV, (scatter_start, scatter_push))

    small["b_ada"] = dmod
    packed, offs = _pack([small[k] for k in _SMALL] + [small["loss"]])
    gathered = _allgather_small("gather_small", packed)
    wp = dict(p, b_ada=b_ada)
    ms = dict(b_ada=m_b_ada, norm1_g=m_norm1_g, b_gate=m_b_gate, ln_g=m_gmlp_ln_g, ln_b=m_gmlp_ln_b, ws=m_gmlp_ws, bs=m_gmlp_bs,
              hg_lb=m_hg_lb, hg_ng=m_hg_norm_g, norm2_g=m_norm2_g, final_g=m_final_norm_g)
    vs = dict(b_ada=v_b_ada, norm1_g=v_norm1_g, b_gate=v_b_gate, ln_g=v_gmlp_ln_g, ln_b=v_gmlp_ln_b, ws=v_gmlp_ws, bs=v_gmlp_bs,
              hg_lb=v_hg_lb, hg_ng=v_hg_norm_g, norm2_g=v_norm2_g, final_g=v_final_norm_g)
    w_sm, _ = _pack([wp[k] for k in _SMALL])
    m_sm, _ = _pack([ms[k] for k in _SMALL])
    v_sm, _ = _pack([vs[k] for k in _SMALL])
    t_tail = scatter_push("scatter_proj_in_b", gathered)
    sm_out = _small_update(gathered, w_sm, m_sm, v_sm, t_tail)
    shapes = dict(b_ada=b_ada.shape, norm1_g=norm1_g.shape, b_gate=b_gate.shape, ln_g=gmlp_ln_g.shape, ln_b=gmlp_ln_b.shape,
                  ws=gmlp_ws.shape, bs=gmlp_bs.shape, hg_lb=hg_lb.shape, hg_ng=hg_norm_g.shape, norm2_g=norm2_g.shape,
                  final_g=final_norm_g.shape)

    def unpack(arr, k):
        i = _SMALL.index(k)
        n = math.prod(shapes[k])
        return arr.reshape(-1)[offs[i]:offs[i] + n].reshape(shapes[k])

    loss = sm_out[0].reshape(-1)[offs[len(_SMALL)]]

    dmod_all = gathered.reshape(N_DEV, -1)[:, offs[0]:offs[0] + N_DEV * ada_loc]
    dmod_loc = lax.dynamic_slice_in_dim(dmod_all, me * ada_loc, ada_loc, axis=1)
    ca_t = jnp.pad(c_act[:N_DEV].T, ((0, 0), (0, LANES - N_DEV))).astype(BF16)
    dm_p = jnp.pad(dmod_loc, ((0, LANES - N_DEV), (0, 0))).astype(BF16)
    tm_a = _tile(D, 512)
    g_ada = _matmul(
        "ada_dw", ca_t, dm_p, dims=_NN, grid_mnk=(D // tm_a, 1, 1), tiles=(tm_a, ada_loc),
        a_spec=pl.BlockSpec((tm_a, LANES), lambda i, j, k: (i, 0)), b_spec=pl.BlockSpec((LANES, ada_loc), lambda i, j, k: (0, 0)),
        out_shapes=[jax.ShapeDtypeStruct((1, D, ada_loc), F32)], out_specs=[pl.BlockSpec((None, tm_a, ada_loc), lambda i, j, k: (0, i, 0))],
        epilogue=_store(F32), after=t_tail)[0]

    upd = {"w_ada": _adamw("adamw_w_ada", w_ada[0], m_w_ada[0], v_w_ada[0], [g_ada])}
    big_w = dict(w_in=(w_in, m_w_in, v_w_in, "w_in"), bg=(w_branch_gmlp, m_w_branch_gmlp, v_w_branch_gmlp, "w_branch_gmlp"),
                 bh=(w_branch_hg, m_w_branch_hg, v_w_branch_hg, "w_branch_hg"), out=(w_out, m_w_out, v_w_out, "w_out"),
                 fi=(w_ffn_in, m_w_ffn_in, v_w_ffn_in, "w_ffn_in"), fo=(w_ffn_out, m_w_ffn_out, v_w_ffn_out, "w_ffn_out"))
    after = upd["w_ada"][1]
    for name in ("scatter_ffn", "scatter_mixer", "scatter_proj_in_a", "scatter_proj_in_b"):
        keys = in_flight[name]["keys"]
        n = len(keys)
        send_sems, recv_sems, bufs = in_flight[name]["stage2"]
        bufs = _split_wait(name + "_ici_wait", bufs, send_sems, recv_sems, after, _to_owner_copies(n))
        for i, k in enumerate(keys):
            parts = [lax.dynamic_index_in_dim(bufs[i], my_chip, axis=0, keepdims=True), bufs[n + i]]
            if k in big_w:
                wt, mt, vt, out_name = big_w[k]
                upd[out_name] = _adamw("adamw_" + out_name, wt[0], mt[0], vt[0], parts)
            else:
                wt, mt, vt, out_name = big_w["w_in"]
                upd[out_name] = _adamw("adamw_" + k, wt[0], mt[0], vt[0], parts, row0=0 if k == "w_in_a" else parts[0].shape[1],
                                       into=upd.get(out_name))
            after = upd[out_name][1]

    order = ("w_ada", "b_ada", "norm1_g", "w_in", "b_gate", "ln_g", "ln_b", "ws", "bs", "hg_lb", "hg_ng", "w_branch_gmlp", "w_branch_hg",
             "w_out", "norm2_g", "w_ffn_in", "w_ffn_out", "final_g")
    outs = [loss, grad_x[None]]
    for idx in range(4):
        for k in order:
            outs.append(upd[k][idx][None] if k in upd else unpack(sm_out[idx], k))
    return tuple(outs)
```

```python
import functools
import math

import jax
import jax.numpy as jnp
from jax import lax
from jax.experimental import pallas as pl
from jax.experimental.pallas import tpu as pltpu

F32 = jnp.float32
BF16 = jnp.bfloat16
N_DEV = 8
EPS = 1e-6
LANES = 128
HG_DK = 128
HG_CHUNK = 64
HG_MID = HG_CHUNK // 2 - 1
EXP_CLAMP = 80.0
VMEM_LIMIT = 48 * 1024 * 1024
ADAM_LR, ADAM_B1, ADAM_B2, ADAM_EPS, ADAM_WD, ADAM_STEP = 0.001, 0.9, 0.999, 1e-08, 0.01, 10
MESH = pl.DeviceIdType.MESH

_NN = (((1,), (0,)), ((), ()))
_NT = (((1,), (1,)), ((), ()))
_TN = (((0,), (0,)), ((), ()))


def _dot(a, b, dims=_NN):
    return lax.dot_general(a.astype(BF16), b.astype(BF16), dims, preferred_element_type=F32)


def _tile(n, target, mult=LANES):
    best = None
    for t in range(mult, min(n, target) + 1, mult):
        if n % t == 0:
            best = t
    return n if best is None else best


def _cparams(sem):
    return pltpu.CompilerParams(dimension_semantics=sem, vmem_limit_bytes=VMEM_LIMIT)


def _sigmoid(x):
    return 1.0 / (1.0 + jnp.exp(-x))


def _gelu_parts(x):
    k0 = math.sqrt(2.0 / math.pi)
    x2 = x * x
    t = jnp.tanh(k0 * (x + 0.044715 * x * x2))
    g = 0.5 * x * (1.0 + t)
    dg = 0.5 * (1.0 + t) + 0.5 * x * (1.0 - t * t) * (k0 * (1.0 + 3.0 * 0.044715 * x2))
    return g, dg


def _split3(x):
    h = x.astype(BF16)
    r = x - h.astype(F32)
    m = r.astype(BF16)
    lo = (r - m.astype(F32)).astype(BF16)
    return h, m, lo


def _ones_dot(mat01, x):
    h, m, lo = _split3(x)
    d = functools.partial(lax.dot_general, dimension_numbers=_NN, preferred_element_type=F32)
    return d(mat01, h) + d(mat01, m) + d(mat01, lo)


def _matmul(name, a, b, *, dims, grid_mnk, tiles, a_spec, b_spec, extras=(), extra_specs=(), out_shapes, out_specs, epilogue, after=None):
    gm, gn, nk = grid_mnk
    tm, tn = tiles
    n_ex, n_out = len(extras), len(out_shapes)
    held = [] if after is None else [after]

    def body(*refs):
        a_ref, b_ref = refs[0], refs[1]
        ex = refs[2:2 + n_ex]
        outs = refs[2 + n_ex + len(held):2 + n_ex + len(held) + n_out]
        if nk == 1:
            epilogue(lax.dot_general(a_ref[...], b_ref[...], dims, preferred_element_type=F32), ex, outs)
            return
        acc = refs[-1]
        k = pl.program_id(2)

        @pl.when(k == 0)
        def _():
            acc[...] = jnp.zeros_like(acc)

        acc[...] += lax.dot_general(a_ref[...], b_ref[...], dims, preferred_element_type=F32)

        @pl.when(k == nk - 1)
        def _():
            epilogue(acc[...], ex, outs)

    return pl.pallas_call(
        body, name=name, grid=(gm, gn, nk), in_specs=[a_spec, b_spec, *extra_specs] + [pl.BlockSpec(memory_space=pl.ANY)] * len(held),
        out_specs=list(out_specs), out_shape=list(out_shapes), scratch_shapes=[] if nk == 1 else [pltpu.VMEM((tm, tn), F32)],
        compiler_params=_cparams(("parallel", "parallel", "arbitrary")),
    )(a, b, *extras, *held)


def _store(dtype):
    def ep(acc, ex, outs):
        outs[0][...] = acc.astype(dtype)
    return ep


def _mm_nn_stacked(name, a, wg, *, tm, tn, tk, out_dtype=F32, extras=(), extra_specs=(), out_shapes=None, out_specs=None, epilogue=None):
    M, K = a.shape
    _, _, nloc = wg.shape
    N = nloc * N_DEV
    q = nloc // tn
    if out_shapes is None:
        out_shapes = [jax.ShapeDtypeStruct((M, N), out_dtype)]
        out_specs = [pl.BlockSpec((tm, tn), lambda i, j, k: (i, j))]
        epilogue = _store(out_dtype)
    return _matmul(
        name, a, wg, dims=_NN, grid_mnk=(M // tm, N // tn, K // tk), tiles=(tm, tn),
        a_spec=pl.BlockSpec((tm, tk), lambda i, j, k: (i, k)),
        b_spec=pl.BlockSpec((None, tk, tn), lambda i, j, k: (j // q, k, j % q)),
        extras=extras, extra_specs=extra_specs, out_shapes=out_shapes, out_specs=out_specs, epilogue=epilogue)


def _mm_nt_stacked(name, a_spec, a, wg, *, M, tm, tn, tk, out_dtype=F32, after=None):
    _, Kw, nloc = wg.shape
    q = nloc // tk
    return _matmul(
        name, a, wg, dims=_NT, grid_mnk=(M // tm, Kw // tn, (nloc * N_DEV) // tk), tiles=(tm, tn),
        a_spec=a_spec, b_spec=pl.BlockSpec((None, tn, tk), lambda i, j, k: (k // q, j, k % q)),
        out_shapes=[jax.ShapeDtypeStruct((M, Kw), out_dtype)], out_specs=[pl.BlockSpec((tm, tn), lambda i, j, k: (i, j))],
        epilogue=_store(out_dtype), after=after)[0]


def _mm_tn(name, a, b, b_spec, *, Mo, No, S, tm, tn, tk, stacked_nloc=None, after=None, a_off=0):
    if stacked_nloc is None:
        out_shape = jax.ShapeDtypeStruct((Mo, No), BF16)
        out_spec = pl.BlockSpec((tm, tn), lambda i, j, k: (i, j))
    else:
        q = stacked_nloc // tn
        out_shape = jax.ShapeDtypeStruct((N_DEV, Mo, stacked_nloc), BF16)
        out_spec = pl.BlockSpec((None, tm, tn), lambda i, j, k: (j // q, i, j % q))
    return _matmul(
        name, a, b, dims=_TN, grid_mnk=(Mo // tm, No // tn, S // tk), tiles=(tm, tn),
        a_spec=pl.BlockSpec((tk, tm), lambda i, j, k: (k, i + a_off)), b_spec=b_spec,
        out_shapes=[out_shape], out_specs=[out_spec], epilogue=_store(BF16), after=after)[0]


def _norm_mod(name, x, g, sc, sh):
    S, D = x.shape
    tm = _tile(S, 256, 8)

    def body(x_ref, g_ref, sc_ref, sh_ref, h_ref):
        xv = x_ref[...]
        r = lax.rsqrt(jnp.mean(xv * xv, axis=-1, keepdims=True) + EPS)
        h = (xv * r) * g_ref[...]
        h_ref[...] = (h * (1.0 + sc_ref[...]) + sh_ref[...]).astype(BF16)

    row = pl.BlockSpec((tm, D), lambda i: (i, 0))
    vec = pl.BlockSpec((1, D), lambda i: (0, 0))
    return pl.pallas_call(body, name=name, grid=(S // tm,), in_specs=[row, vec, vec, vec], out_specs=row,
                          out_shape=jax.ShapeDtypeStruct((S, D), BF16), compiler_params=_cparams(("parallel",)))(x, g, sc, sh)


def _norm_mod_bwd(name, dh, x, g, sc, dres, o=None, gt=None):
    S, D = x.shape
    tm = _tile(S, 256, 8)
    gated = o is not None

    def body(*refs):
        if gated:
            dh_ref, x_ref, g_ref, sc_ref, dres_ref, o_ref, gt_ref, dx_ref, vec_ref, do_ref = refs
        else:
            dh_ref, x_ref, g_ref, sc_ref, dres_ref, dx_ref, vec_ref = refs
        i = pl.program_id(0)

        @pl.when(i == 0)
        def _():
            vec_ref[...] = jnp.zeros_like(vec_ref)

        xv, dh_v, gv = x_ref[...], dh_ref[...], g_ref[...]
        r = lax.rsqrt(jnp.mean(xv * xv, axis=-1, keepdims=True) + EPS)
        xn = xv * r
        one_sc = 1.0 + sc_ref[...]
        vec_ref[0:1, :] += jnp.sum(dh_v, axis=0, keepdims=True)
        vec_ref[1:2, :] += jnp.sum(dh_v * (xn * gv), axis=0, keepdims=True)
        vec_ref[2:3, :] += jnp.sum(dh_v * one_sc * xn, axis=0, keepdims=True)
        dxn = dh_v * one_sc * gv
        dx = dres_ref[...] + r * (dxn - xn * jnp.mean(dxn * xn, axis=-1, keepdims=True))
        dx_ref[...] = dx
        if gated:
            vec_ref[3:4, :] += jnp.sum(dx * o_ref[...], axis=0, keepdims=True)
            do_ref[...] = (dx * gt_ref[...]).astype(BF16)

    row = pl.BlockSpec((tm, D), lambda i: (i, 0))
    vec = pl.BlockSpec((1, D), lambda i: (0, 0))
    acc = pl.BlockSpec((8, D), lambda i: (0, 0))
    ins = [dh, x, g, sc, dres] + ([o, gt] if gated else [])
    in_specs = [row, row, vec, vec, row] + ([row, vec] if gated else [])
    out_shape = [jax.ShapeDtypeStruct((S, D), F32), jax.ShapeDtypeStruct((8, D), F32)]
    out_specs = [row, acc]
    if gated:
        out_shape.append(jax.ShapeDtypeStruct((S, D), BF16))
        out_specs.append(row)
    return pl.pallas_call(body, name=name, grid=(S // tm,), in_specs=in_specs, out_specs=out_specs, out_shape=out_shape,
                          compiler_params=_cparams(("arbitrary",)))(*ins)


def _loss_head(x3, tgt, gf, o2, gt2):
    S, D = x3.shape
    tm = _tile(S, 256, 8)

    def body(x_ref, t_ref, g_ref, o_ref, gt_ref, dx_ref, do_ref, vec_ref):
        i = pl.program_id(0)

        @pl.when(i == 0)
        def _():
            vec_ref[...] = jnp.zeros_like(vec_ref)

        xv, gv = x_ref[...], g_ref[...]
        r = lax.rsqrt(jnp.mean(xv * xv, axis=-1, keepdims=True) + EPS)
        xn = xv * r
        e = xn * gv - t_ref[...]
        tok = 0.5 * jnp.mean(e * e, axis=-1, keepdims=True)
        vec_ref[0:1, :] += jnp.broadcast_to(jnp.sum(tok, axis=0, keepdims=True), (1, D))
        dy = e * (1.0 / D)
        vec_ref[1:2, :] += jnp.sum(dy * xn, axis=0, keepdims=True)
        dxn = dy * gv
        dx = r * (dxn - xn * jnp.mean(dxn * xn, axis=-1, keepdims=True))
        dx_ref[...] = dx
        vec_ref[2:3, :] += jnp.sum(dx * o_ref[...], axis=0, keepdims=True)
        do_ref[...] = (dx * gt_ref[...]).astype(BF16)

    row = pl.BlockSpec((tm, D), lambda i: (i, 0))
    vec = pl.BlockSpec((1, D), lambda i: (0, 0))
    return pl.pallas_call(
        body, name="loss_head", grid=(S // tm,), in_specs=[row, row, vec, row, vec],
        out_specs=[row, row, pl.BlockSpec((8, D), lambda i: (0, 0))],
        out_shape=[jax.ShapeDtypeStruct((S, D), F32), jax.ShapeDtypeStruct((S, D), BF16), jax.ShapeDtypeStruct((8, D), F32)],
        compiler_params=_cparams(("arbitrary",)))(x3, tgt, gf, o2, gt2)


def _ffn_in_swiglu(h, wg):
    S, D = h.shape
    _, _, tf = wg.shape
    nf = N_DEV // 2
    F = nf * tf
    tm = _tile(S, 256, 16)

    def body(h_ref, wa_ref, wu_ref, hf_ref, fac_ref):
        hv = h_ref[...]
        a = lax.dot_general(hv, wa_ref[...], _NN, preferred_element_type=F32)
        up = lax.dot_general(hv, wu_ref[...], _NN, preferred_element_type=F32)
        sa = _sigmoid(a)
        silu = a * sa
        hf_ref[...] = (silu * up).astype(BF16)
        fac_ref[0] = (up * (sa * (1.0 + a * (1.0 - sa)))).astype(BF16)
        fac_ref[1] = silu.astype(BF16)

    return pl.pallas_call(
        body, name="ffn_in_swiglu", grid=(nf, S // tm),
        in_specs=[pl.BlockSpec((tm, D), lambda j, i: (i, 0)), pl.BlockSpec((None, D, tf), lambda j, i: (j, 0, 0)),
                  pl.BlockSpec((None, D, tf), lambda j, i: (j + nf, 0, 0))],
        out_specs=[pl.BlockSpec((tm, tf), lambda j, i: (i, j)), pl.BlockSpec((2, tm, tf), lambda j, i: (0, i, j))],
        out_shape=[jax.ShapeDtypeStruct((S, F), BF16), jax.ShapeDtypeStruct((2, S, F), BF16)],
        compiler_params=_cparams(("parallel", "parallel")))(h, wg, wg)


def _colsum2(dg2):
    _, S, D = dg2.shape
    tm = _tile(S, 256, 16)

    def body(x_ref, o_ref):
        @pl.when(pl.program_id(0) == 0)
        def _():
            o_ref[...] = jnp.zeros_like(o_ref)

        o_ref[0:1, :] += jnp.sum(x_ref[0].astype(F32), axis=0, keepdims=True)
        o_ref[1:2, :] += jnp.sum(x_ref[1].astype(F32), axis=0, keepdims=True)

    return pl.pallas_call(body, name="gate_bias_grad", grid=(S // tm,), in_specs=[pl.BlockSpec((2, tm, D), lambda i: (0, i, 0))],
                          out_specs=pl.BlockSpec((2, D), lambda i: (0, 0)), out_shape=jax.ShapeDtypeStruct((2, D), F32),
                          compiler_params=_cparams(("arbitrary",)))(dg2)


def _gmlp_common(u_ref, v_ref, lg_ref, lb_ref, ws_ref, bsb_ref, G, T, Dg):
    ug, dug = _gelu_parts(u_ref[...])
    vg, dvg = _gelu_parts(v_ref[...])
    mu = jnp.mean(vg, axis=-1, keepdims=True)
    vc = vg - mu
    rstd = lax.rsqrt(jnp.mean(vc * vc, axis=-1, keepdims=True) + EPS)
    vhat = vc * rstd
    vn = vhat * lg_ref[...] + lb_ref[...]
    row = lax.broadcasted_iota(jnp.int32, (T, T), 0)
    col = lax.broadcasted_iota(jnp.int32, (T, T), 1)
    tril = row >= col
    s = []
    for g in range(G):
        w = jnp.where(tril, ws_ref[g], 0.0)
        s.append(_dot(w, vn[:, g * Dg:(g + 1) * Dg]) + bsb_ref[g])
    return ug, dug, dvg, rstd, vhat, vn, tril, s


def _gmlp_fwd(z, ln_g, ln_b, ws, bsb, GW):
    S = z.shape[0]
    G, T, _ = ws.shape
    Dg = GW // G

    def body(u_ref, v_ref, lg_ref, lb_ref, ws_ref, bsb_ref, ya_ref):
        ug, _, _, _, _, _, _, s = _gmlp_common(u_ref, v_ref, lg_ref, lb_ref, ws_ref, bsb_ref, G, T, Dg)
        for g in range(G):
            sl = slice(g * Dg, (g + 1) * Dg)
            ya_ref[:, sl] = (ug[:, sl] * s[g]).astype(BF16)

    vec = pl.BlockSpec((1, GW), lambda c: (0, 0))
    return pl.pallas_call(
        body, name="gmlp_fwd", grid=(S // T,),
        in_specs=[pl.BlockSpec((T, GW), lambda c: (c, 0)), pl.BlockSpec((T, GW), lambda c: (c, 1)), vec, vec,
                  pl.BlockSpec((G, T, T), lambda c: (0, 0, 0)), pl.BlockSpec((G, T, Dg), lambda c: (0, 0, 0))],
        out_specs=pl.BlockSpec((T, GW), lambda c: (c, 0)), out_shape=jax.ShapeDtypeStruct((S, GW), BF16),
        compiler_params=_cparams(("parallel",)))(z, z, ln_g, ln_b, ws, bsb)


def _gmlp_bwd(z, dya, ln_g, ln_b, ws, bsb, GW):
    S = z.shape[0]
    G, T, _ = ws.shape
    Dg = GW // G
    nc = S // T

    def body(u_ref, v_ref, dya_ref, lg_ref, lb_ref, ws_ref, bsb_ref, dz_ref, dln_ref, dws_ref, dbs_ref, dbs_acc, dvh):
        c = pl.program_id(0)

        @pl.when(c == 0)
        def _():
            dln_ref[...] = jnp.zeros_like(dln_ref)
            dws_ref[...] = jnp.zeros_like(dws_ref)
            dbs_acc[...] = jnp.zeros_like(dbs_acc)

        ug, dug, dvg, rstd, vhat, vn, tril, s = _gmlp_common(u_ref, v_ref, lg_ref, lb_ref, ws_ref, bsb_ref, G, T, Dg)
        dya_v = dya_ref[...]
        for g in range(G):
            sl = slice(g * Dg, (g + 1) * Dg)
            dy_g = dya_v[:, sl]
            dz_ref[:, sl] = (dy_g * s[g] * dug[:, sl]).astype(BF16)
            ds = dy_g * ug[:, sl]
            dbs_acc[g] += ds
            w = jnp.where(tril, ws_ref[g], 0.0)
            dvn_g = _dot(w, ds, _TN)
            dws_ref[g] += jnp.where(tril, _dot(ds, vn[:, sl], _NT), 0.0)
            dln_ref[0:1, sl] += jnp.sum(dvn_g * vhat[:, sl], axis=0, keepdims=True)
            dln_ref[1:2, sl] += jnp.sum(dvn_g, axis=0, keepdims=True)
            dvh[:, sl] = dvn_g * lg_ref[:, sl]
        dvhat = dvh[...]
        m1 = jnp.mean(dvhat, axis=-1, keepdims=True)
        m2 = jnp.mean(dvhat * vhat, axis=-1, keepdims=True)
        dz_ref[:, GW:2 * GW] = (rstd * (dvhat - m1 - vhat * m2) * dvg).astype(BF16)

        @pl.when(c == nc - 1)
        def _():
            for g in range(G):
                dbs_ref[g] = jnp.sum(dbs_acc[g], axis=-1, keepdims=True)

    vec = pl.BlockSpec((1, GW), lambda c: (0, 0))
    return pl.pallas_call(
        body, name="gmlp_bwd", grid=(nc,),
        in_specs=[pl.BlockSpec((T, GW), lambda c: (c, 0)), pl.BlockSpec((T, GW), lambda c: (c, 1)),
                  pl.BlockSpec((T, GW), lambda c: (c, 0)), vec, vec,
                  pl.BlockSpec((G, T, T), lambda c: (0, 0, 0)), pl.BlockSpec((G, T, Dg), lambda c: (0, 0, 0))],
        out_specs=[pl.BlockSpec((T, 2 * GW), lambda c: (c, 0)), pl.BlockSpec((8, GW), lambda c: (0, 0)),
                   pl.BlockSpec((G, T, T), lambda c: (0, 0, 0)), pl.BlockSpec((G, T, 1), lambda c: (0, 0, 0))],
        out_shape=[jax.ShapeDtypeStruct((S, 2 * GW), BF16), jax.ShapeDtypeStruct((8, GW), F32),
                   jax.ShapeDtypeStruct((G, T, T), F32), jax.ShapeDtypeStruct((G, T, 1), F32)],
        scratch_shapes=[pltpu.VMEM((G, T, Dg), F32), pltpu.VMEM((T, GW), F32)],
        compiler_params=_cparams(("arbitrary",)))(z, z, dya, ln_g, ln_b, ws, bsb)


def _hg_common(q_ref, f_ref, hlb_ref):
    C = HG_CHUNK
    a = hlb_ref[...]
    lb = _sigmoid(a[0:1, :] - a[1:2, :])
    sig = _sigmoid(f_ref[...])
    f = lb + (1.0 - lb) * sig
    lf = jnp.log(f)
    kk = 1.0 - f
    q = q_ref[...]
    sq = _sigmoid(q)
    qa = q * sq
    row = lax.broadcasted_iota(jnp.int32, (C, C), 0)
    col = lax.broadcasted_iota(jnp.int32, (C, C), 1)
    tril = row >= col
    b = _ones_dot(tril.astype(BF16), lf)
    bm = b[HG_MID:HG_MID + 1, :]
    bl = b[C - 1:C, :]
    e_b = jnp.exp(b)
    e_qm = jnp.exp(jnp.minimum(b - bm, EXP_CLAMP))
    e_km = jnp.exp(jnp.minimum(bm - b, EXP_CLAMP))
    e_kl = jnp.exp(bl - b)
    return dict(lb=lb, sig=sig, f=f, kk=kk, q=q, sq=sq, qa=qa, tril=tril, e_b=e_b, e_qm=e_qm, e_km=e_km, e_kl=e_kl,
                e_l=jnp.exp(bl), qh=qa * e_b, qt=qa * e_qm, kt=kk * e_km, kh=kk * e_kl)


def _hg_fwd(z, hg_lb, ng, HW):
    S = z.shape[0]
    C, H, dk = HG_CHUNK, HW // HG_DK, HG_DK
    nc = S // C

    def body(q_ref, f_ref, i_ref, og_ref, hlb_ref, ng_ref, yb_ref, o_ref, st_ref, state):
        @pl.when(pl.program_id(0) == 0)
        def _():
            state[...] = jnp.zeros_like(state)

        t = _hg_common(q_ref, f_ref, hlb_ref)
        iv = i_ref[...]
        for h in range(H):
            sl = slice(h * dk, (h + 1) * dk)
            st = state[h]
            st_ref[h] = st
            a = jnp.where(t["tril"], _dot(t["qt"][:, sl], t["kt"][:, sl], _NT), 0.0)
            o_h = _dot(a, iv[:, sl]) + _dot(t["qh"][:, sl], st, _NT)
            state[h] = st * t["e_l"][:, sl] + _dot(iv[:, sl], t["kh"][:, sl], _TN)
            o_ref[:, sl] = o_h
            rr = lax.rsqrt(jnp.mean(o_h * o_h, axis=-1, keepdims=True) + EPS)
            og = og_ref[:, sl]
            yb_ref[:, sl] = (o_h * rr * ng_ref[:, sl] * (og * _sigmoid(og))).astype(BF16)

    def col(k):
        return pl.BlockSpec((C, HW), lambda c: (c, k))

    base = 2
    return pl.pallas_call(
        body, name="hgrn_fwd", grid=(nc,),
        in_specs=[col(base), col(base + 1), col(base + 2), col(base + 3),
                  pl.BlockSpec((2, HW), lambda c: (0, 0)), pl.BlockSpec((1, HW), lambda c: (0, 0))],
        out_specs=[pl.BlockSpec((C, HW), lambda c: (c, 0)), pl.BlockSpec((C, HW), lambda c: (c, 0)),
                   pl.BlockSpec((None, H, dk, dk), lambda c: (c, 0, 0, 0))],
        out_shape=[jax.ShapeDtypeStruct((S, HW), BF16), jax.ShapeDtypeStruct((S, HW), F32),
                   jax.ShapeDtypeStruct((nc, H, dk, dk), F32)],
        scratch_shapes=[pltpu.VMEM((H, dk, dk), F32)],
        compiler_params=_cparams(("arbitrary",)))(z, z, z, z, hg_lb, ng)


def _hg_bwd(z, o, states, dyb, hg_lb, ng, HW, dz_head, dz_tail):
    S = z.shape[0]
    C, H, dk = HG_CHUNK, HW // HG_DK, HG_DK
    nc = S // C
    B0 = dz_head.shape[1]
    DT = dz_tail.shape[2]
    INW = B0 + 4 * HW + 2 * DT

    def body(q_ref, f_ref, i_ref, og_ref, o_ref, st_ref, stn_ref, dyb_ref, hlb_ref, ng_ref, head_ref, tail_ref,
             dzf_ref, dng_ref, dhlb_ref, dstate, cross, dqa_buf, dkk_buf, db_buf, dlb_acc):
        c = pl.program_id(0)
        dzf_ref[:, 0:B0] = head_ref[...]
        dzf_ref[:, B0 + 4 * HW:B0 + 4 * HW + DT] = tail_ref[0]
        dzf_ref[:, B0 + 4 * HW + DT:INW] = tail_ref[1]
        dz_ref = dzf_ref.at[:, B0:B0 + 4 * HW]

        @pl.when(c == 0)
        def _():
            dstate[...] = jnp.zeros_like(dstate)
            dlb_acc[...] = jnp.zeros_like(dlb_acc)
            dng_ref[...] = jnp.zeros_like(dng_ref)

        def r16(v):
            return v.astype(BF16).astype(F32)

        t = _hg_common(q_ref, f_ref, hlb_ref)
        iv = i_ref[...]
        for h in range(H):
            sl = slice(h * dk, (h + 1) * dk)
            o_h, og, dyb_h, ng_h = o_ref[:, sl], og_ref[:, sl], dyb_ref[:, sl], ng_ref[:, sl]
            sg = _sigmoid(og)
            silu_og = og * sg
            rr = lax.rsqrt(jnp.mean(o_h * o_h, axis=-1, keepdims=True) + EPS)
            on = o_h * rr
            dng_ref[0:1, sl] += jnp.sum(dyb_h * on * silu_og, axis=0, keepdims=True)
            dz_ref[:, 3 * HW + h * dk:3 * HW + (h + 1) * dk] = (dyb_h * on * ng_h * (sg * (1.0 + og * (1.0 - sg)))).astype(BF16)
            don = dyb_h * ng_h * silu_og
            do_h = rr * (don - on * jnp.mean(don * on, axis=-1, keepdims=True))

            qt, kt, qh, kh, iv_h = t["qt"][:, sl], t["kt"][:, sl], t["qh"][:, sl], t["kh"][:, sl], iv[:, sl]
            a = jnp.where(t["tril"], _dot(qt, kt, _NT), 0.0)
            da = jnp.where(t["tril"], _dot(do_h, iv_h, _NT), 0.0)
            st, dst = st_ref[h], dstate[h]
            cross[:, sl] = jnp.sum(dst * stn_ref[h], axis=0, keepdims=True)
            dqh = _dot(do_h, st)
            dstate[h] = _dot(do_h, qh, _TN) + dst * t["e_l"][:, sl]
            div = _dot(a, do_h, _TN) + _dot(kh, dst, _NT)
            dkh = _dot(iv_h, dst)
            dqt = _dot(da, kt)
            dkt = _dot(da, qt, _TN)
            dz_ref[:, 2 * HW + h * dk:2 * HW + (h + 1) * dk] = div.astype(BF16)
            dqa_buf[:, sl] = dqh * t["e_b"][:, sl] + dqt * t["e_qm"][:, sl]
            dkk_buf[:, sl] = dkt * t["e_km"][:, sl] + dkh * t["e_kl"][:, sl]
            db_buf[:, sl] = r16(qt) * dqt - r16(kt) * dkt + r16(qh) * dqh - r16(kh) * dkh

        dqa, dkk = dqa_buf[...], dkk_buf[...]
        triu = jnp.logical_not(t["tril"]) | (lax.broadcasted_iota(jnp.int32, (C, C), 0) == lax.broadcasted_iota(jnp.int32, (C, C), 1))
        dlf = _ones_dot(triu.astype(BF16), db_buf[...]) + cross[...]
        df = dlf / t["f"] - dkk
        sig, lb = t["sig"], t["lb"]
        dz_ref[:, HW:2 * HW] = (df * (1.0 - lb) * sig * (1.0 - sig)).astype(BF16)
        dlb_acc[...] += jnp.sum(df * (1.0 - sig), axis=0, keepdims=True)
        q, sq = t["q"], t["sq"]
        dz_ref[:, 0:HW] = (dqa * (sq * (1.0 + q * (1.0 - sq)))).astype(BF16)

        @pl.when(c == nc - 1)
        def _():
            da0 = dlb_acc[...] * lb * (1.0 - lb)
            dhlb_ref[0:1, :] = da0
            dhlb_ref[1:2, :] = -da0

    def col(k):
        return pl.BlockSpec((C, HW), lambda c: (nc - 1 - c, k))

    base = 2
    return pl.pallas_call(
        body, name="hgrn_bwd", grid=(nc,),
        in_specs=[col(base), col(base + 1), col(base + 2), col(base + 3), col(0),
                  pl.BlockSpec((None, H, dk, dk), lambda c: (nc - 1 - c, 0, 0, 0)),
                  pl.BlockSpec((None, H, dk, dk), lambda c: (jnp.minimum(nc - c, nc - 1), 0, 0, 0)), col(0),
                  pl.BlockSpec((2, HW), lambda c: (0, 0)), pl.BlockSpec((1, HW), lambda c: (0, 0)),
                  pl.BlockSpec((C, B0), lambda c: (nc - 1 - c, 0)), pl.BlockSpec((2, C, DT), lambda c: (0, nc - 1 - c, 0))],
        out_specs=[pl.BlockSpec((C, INW), lambda c: (nc - 1 - c, 0)), pl.BlockSpec((8, HW), lambda c: (0, 0)),
                   pl.BlockSpec((2, HW), lambda c: (0, 0))],
        out_shape=[jax.ShapeDtypeStruct((S, INW), BF16), jax.ShapeDtypeStruct((8, HW), F32), jax.ShapeDtypeStruct((2, HW), F32)],
        scratch_shapes=[pltpu.VMEM((H, dk, dk), F32), pltpu.VMEM((1, HW), F32), pltpu.VMEM((C, HW), F32), pltpu.VMEM((C, HW), F32),
                        pltpu.VMEM((C, HW), F32), pltpu.VMEM((1, HW), F32)],
        compiler_params=_cparams(("arbitrary",)))(z, z, z, z, o, states, states, dyb, hg_lb, ng, dz_head, dz_tail)


def _position():
    x, y, c = lax.axis_index("x"), lax.axis_index("y"), lax.axis_index("c")
    return x, y, c, 4 * x + 2 * y + c


def _flip(x, y, c, k):
    return (1 - x if k & 4 else x, 1 - y if k & 2 else y, 1 - c if k & 1 else c)


def _allgather_small(name, v):
    R, L = v.shape

    def body(v_ref, out_ref, send_sems, recv_sems):
        x, y, c, me = _position()
        out_ref[me] = v_ref[...]
        copies = []
        for k in range(1, N_DEV):
            cp = pltpu.make_async_remote_copy(src_ref=v_ref, dst_ref=out_ref.at[me], send_sem=send_sems.at[k - 1],
                                              recv_sem=recv_sems.at[k - 1], device_id=_flip(x, y, c, k), device_id_type=MESH)
            cp.start()
            copies.append(cp)
        for cp in copies:
            cp.wait()

    return pl.pallas_call(
        body, name=name, out_shape=jax.ShapeDtypeStruct((N_DEV, R, L), v.dtype),
        in_specs=[pl.BlockSpec(memory_space=pltpu.VMEM)], out_specs=pl.BlockSpec(memory_space=pltpu.VMEM),
        scratch_shapes=[pltpu.SemaphoreType.DMA((N_DEV - 1,)), pltpu.SemaphoreType.DMA((N_DEV - 1,))],
        compiler_params=pltpu.CompilerParams(vmem_limit_bytes=VMEM_LIMIT),
    )(v)


def _allgather_hbm(name, shards):
    n = len(shards)

    def body(*refs):
        ins, outs = refs[:n], refs[n:2 * n]
        send_sems, recv_sems, local_sems = refs[2 * n:]
        x, y, c, me = _position()
        sibling = (x, y, 1 - c)
        chips = [(1 - x, y), (x, 1 - y), (1 - x, 1 - y)]

        def slot(px, py, pc):
            return 4 * px + 2 * py + pc

        def copy(w, k, block, to, src=None):
            dst = outs[w].at[slot(*block)]
            return pltpu.make_async_remote_copy(src_ref=dst if src is None else src, dst_ref=dst, send_sem=send_sems.at[w, k],
                                                recv_sem=recv_sems.at[w, k], device_id=to, device_id_type=MESH)

        mine, first, passed = [], [], []
        for w in range(n):
            cp = pltpu.make_async_copy(ins[w], outs[w].at[me], local_sems.at[w])
            cp.start()
            mine.append(cp)
            for j, chip in enumerate(chips):
                first.append(copy(w, 1 + j, (x, y, c), (*chip, c), src=ins[w]))
            first.append(copy(w, 0, (x, y, c), sibling, src=ins[w]))
        for cp in first:
            cp.start()
        for w in range(n):
            for j, chip in enumerate(chips):
                copy(w, 1 + j, (*chip, c), (x, y, c)).wait_recv()
                cp = copy(w, 4 + j, (*chip, c), sibling)
                cp.start()
                passed.append(cp)
        for w in range(n):
            copy(w, 0, sibling, (x, y, c)).wait_recv()
            for j, chip in enumerate(chips):
                copy(w, 4 + j, (*chip, 1 - c), (x, y, c)).wait_recv()
        for cp in first + passed:
            cp.wait_send()
        for cp in mine:
            cp.wait()

    hbm = pl.BlockSpec(memory_space=pltpu.HBM)
    return pl.pallas_call(
        body, name=name, out_shape=[jax.ShapeDtypeStruct((N_DEV, *s.shape), s.dtype) for s in shards],
        in_specs=[hbm] * n, out_specs=[hbm] * n,
        scratch_shapes=[pltpu.SemaphoreType.DMA((n, 7)), pltpu.SemaphoreType.DMA((n, 7)), pltpu.SemaphoreType.DMA((n,))],
    )(*shards)


_HBM = pl.BlockSpec(memory_space=pltpu.HBM)
_SEM = pl.BlockSpec(memory_space=pltpu.SEMAPHORE)
_EFFECT = pltpu.SideEffectType.DATAFLOW_SIDE_EFFECTING


def _split_start(name, bufs, n_sems, copies_fn, after=None):
    nb = len(bufs)
    extra = [] if after is None else [after]
    k = nb + len(extra)

    def body(*refs):
        for cp in copies_fn(refs[:nb], refs[k], refs[k + 1]):
            cp.start()
        refs[-1][...] = jnp.zeros_like(refs[-1])

    sems = pltpu.SemaphoreType.DMA((n_sems,))
    res = pl.pallas_call(
        body, name=name,
        out_shape=(sems, sems, *[pltpu.HBM(a.shape, a.dtype) for a in bufs], jax.ShapeDtypeStruct((8, LANES), F32)),
        in_specs=[_HBM] * nb + [pl.BlockSpec(memory_space=pl.ANY)] * len(extra),
        out_specs=(_SEM, _SEM, *[_HBM] * nb, pl.BlockSpec(memory_space=pltpu.VMEM)),
        input_output_aliases={i: 2 + i for i in range(nb)},
        compiler_params=pltpu.CompilerParams(has_side_effects=_EFFECT),
    )(*[pltpu.with_memory_space_constraint(a, pltpu.HBM) for a in bufs], *extra)
    return res[0], res[1], list(res[2:2 + nb]), res[-1]


def _split_wait(name, bufs, send_sems, recv_sems, after, copies_fn):
    nb = len(bufs)

    def body(*refs):
        for cp in copies_fn(refs[:nb], refs[nb], refs[nb + 1]):
            cp.wait_send()
            cp.wait_recv()

    res = pl.pallas_call(
        body, name=name, out_shape=tuple(pltpu.HBM(a.shape, a.dtype) for a in bufs),
        in_specs=[_HBM] * nb + [_SEM, _SEM, pl.BlockSpec(memory_space=pl.ANY)], out_specs=tuple([_HBM] * nb),
        input_output_aliases={i: i for i in range(nb)},
        compiler_params=pltpu.CompilerParams(has_side_effects=_EFFECT),
    )(*bufs, send_sems, recv_sems, after)
    return list(res)


def _split_relay(name, bufs, send_sems, recv_sems, after, wait_fn, n_sems, start_fn):
    nb = len(bufs)

    def body(*refs):
        for cp in wait_fn(refs[:nb], refs[nb], refs[nb + 1]):
            cp.wait_send()
            cp.wait_recv()
        for cp in start_fn(refs[:nb], refs[nb + 3], refs[nb + 4]):
            cp.start()
        refs[-1][...] = jnp.zeros_like(refs[-1])

    sems = pltpu.SemaphoreType.DMA((n_sems,))
    res = pl.pallas_call(
        body, name=name, out_shape=(sems, sems, *[pltpu.HBM(a.shape, a.dtype) for a in bufs], jax.ShapeDtypeStruct((8, LANES), F32)),
        in_specs=[_HBM] * nb + [_SEM, _SEM, pl.BlockSpec(memory_space=pl.ANY)],
        out_specs=(_SEM, _SEM, *[_HBM] * nb, pl.BlockSpec(memory_space=pltpu.VMEM)),
        input_output_aliases={i: 2 + i for i in range(nb)},
        compiler_params=pltpu.CompilerParams(has_side_effects=_EFFECT),
    )(*bufs, send_sems, recv_sems, after)
    return res[0], res[1], list(res[2:2 + nb]), res[-1]


N_CHIP = 4


def _chip_flip(x, y, k):
    return (1 - x if k & 2 else x), (1 - y if k & 1 else y)


def _gather_first_copies(n):
    def copies(bufs, send_sems, recv_sems):
        x, y, c, me = _position()
        out = []
        for w in range(n):
            for k in range(N_CHIP):
                to = (x, y, 1 - c) if k == 0 else (*_chip_flip(x, y, k), c)
                out.append(pltpu.make_async_remote_copy(
                    src_ref=bufs[w], dst_ref=bufs[n + w].at[me], send_sem=send_sems.at[w * N_CHIP + k],
                    recv_sem=recv_sems.at[w * N_CHIP + k], device_id=to, device_id_type=MESH))
        return out
    return copies


def _gather_relay_copies(n):
    def copies(bufs, send_sems, recv_sems):
        x, y, c, _ = _position()
        out = []
        for w in range(n):
            for k in range(1, N_CHIP):
                px, py = _chip_flip(x, y, k)
                blk = bufs[n + w].at[4 * px + 2 * py + c]
                out.append(pltpu.make_async_remote_copy(
                    src_ref=blk, dst_ref=blk, send_sem=send_sems.at[w * (N_CHIP - 1) + k - 1],
                    recv_sem=recv_sems.at[w * (N_CHIP - 1) + k - 1], device_id=(x, y, 1 - c), device_id_type=MESH))
        return out
    return copies


def _xor(a, b):
    return a + b - 2 * a * b


def _forward_first_copies(n):
    def copies(bufs, send_sems, recv_sems):
        x, y, c, me = _position()
        out = []
        for w in range(n):
            for k, to in enumerate([(x, y, 1 - c), (1 - x, y, c), (x, 1 - y, c)]):
                out.append(pltpu.make_async_remote_copy(
                    src_ref=bufs[w], dst_ref=bufs[n + w].at[me], send_sem=send_sems.at[w * 3 + k],
                    recv_sem=recv_sems.at[w * 3 + k], device_id=to, device_id_type=MESH))
        return out
    return copies


def _forward_second_copies(n):
    def copies(bufs, send_sems, recv_sems):
        x, y, c, _ = _position()
        to = (_xor(x, c), _xor(y, 1 - c), c)
        from_chip = 2 * _xor(x, 1 - c) + _xor(y, c)
        out = []
        for w in range(n):
            blk = bufs[n + w].at[2 * from_chip + c]
            out.append(pltpu.make_async_remote_copy(src_ref=blk, dst_ref=blk, send_sem=send_sems.at[w * 3], recv_sem=recv_sems.at[w * 3],
                                                    device_id=to, device_id_type=MESH))
            for k, (px, py) in enumerate([(1 - x, y), (x, 1 - y)]):
                blk = bufs[n + w].at[4 * px + 2 * py + c]
                out.append(pltpu.make_async_remote_copy(src_ref=blk, dst_ref=blk, send_sem=send_sems.at[w * 3 + 1 + k],
                                                        recv_sem=recv_sems.at[w * 3 + 1 + k], device_id=(x, y, 1 - c), device_id_type=MESH))
        return out
    return copies


def _forward_third_copies(n):
    def copies(bufs, send_sems, recv_sems):
        x, y, c, _ = _position()
        out = []
        for w in range(n):
            blk = bufs[n + w].at[4 * (1 - x) + 2 * (1 - y) + c]
            out.append(pltpu.make_async_remote_copy(src_ref=blk, dst_ref=blk, send_sem=send_sems.at[w], recv_sem=recv_sems.at[w],
                                                    device_id=(x, y, 1 - c), device_id_type=MESH))
        return out
    return copies


def _to_sibling_copies(n):
    def copies(bufs, send_sems, recv_sems):
        x, y, c, _ = _position()
        out = []
        for w in range(n):
            for q in range(N_CHIP):
                out.append(pltpu.make_async_remote_copy(
                    src_ref=bufs[w].at[2 * q + 1 - c], dst_ref=bufs[n + w].at[q], send_sem=send_sems.at[w * N_CHIP + q],
                    recv_sem=recv_sems.at[w * N_CHIP + q], device_id=(x, y, 1 - c), device_id_type=MESH))
        return out
    return copies


def _to_owner_copies(n):
    def copies(bufs, send_sems, recv_sems):
        x, y, c, _ = _position()
        out = []
        for w in range(n):
            for k in range(1, N_CHIP):
                px, py = (1 - x if k & 2 else x), (1 - y if k & 1 else y)
                out.append(pltpu.make_async_remote_copy(
                    src_ref=bufs[w].at[2 * px + py], dst_ref=bufs[n + w].at[k - 1], send_sem=send_sems.at[w * (N_CHIP - 1) + k - 1],
                    recv_sem=recv_sems.at[w * (N_CHIP - 1) + k - 1], device_id=(px, py, c), device_id_type=MESH))
        return out
    return copies


def _chip_sum(name, stack, landed, c_idx):
    _, R, C = stack.shape
    tr = _tile(R, max(16, 1048576 // C), 16)

    def body(c_ref, a_ref, b_ref, o_ref):
        o_ref[...] = (a_ref[...].astype(F32) + b_ref[...].astype(F32)).astype(o_ref.dtype)

    return pl.pallas_call(
        body, name=name,
        grid_spec=pltpu.PrefetchScalarGridSpec(
            num_scalar_prefetch=1, grid=(N_CHIP, R // tr),
            in_specs=[pl.BlockSpec((None, tr, C), lambda q, i, c_ref: (2 * q + c_ref[0], i, 0)),
                      pl.BlockSpec((None, tr, C), lambda q, i, c_ref: (q, i, 0))],
            out_specs=pl.BlockSpec((None, tr, C), lambda q, i, c_ref: (q, i, 0))),
        out_shape=jax.ShapeDtypeStruct((N_CHIP, R, C), stack.dtype),
        compiler_params=_cparams(("parallel", "parallel")))(c_idx, stack, landed)


def _ada_mod(c16, w):
    _, D = c16.shape
    n = w.shape[1]
    tk = _tile(D, 512)
    nk = D // tk

    def body(c_ref, w_ref, o_ref, ca_ref):
        @pl.when(pl.program_id(0) == 0)
        def _():
            o_ref[...] = jnp.zeros_like(o_ref)

        cv = c_ref[...]
        ca = cv * _sigmoid(cv)
        ca_ref[...] = ca
        o_ref[...] += _dot(ca, w_ref[...])

    return pl.pallas_call(
        body, name="ada_mod", grid=(nk,),
        in_specs=[pl.BlockSpec((16, tk), lambda k: (0, k)), pl.BlockSpec((tk, n), lambda k: (k, 0))],
        out_specs=[pl.BlockSpec((16, n), lambda k: (0, 0)), pl.BlockSpec((16, tk), lambda k: (0, k))],
        out_shape=[jax.ShapeDtypeStruct((16, n), F32), jax.ShapeDtypeStruct((16, D), F32)],
        compiler_params=_cparams(("arbitrary",)))(c16, w)


def _adam_math(w, g, m, v):
    m2 = ADAM_B1 * m + (1.0 - ADAM_B1) * g
    v2 = ADAM_B2 * v + (1.0 - ADAM_B2) * (g * g)
    m_hat = m2 / (1.0 - ADAM_B1 ** ADAM_STEP)
    v_hat = v2 / (1.0 - ADAM_B2 ** ADAM_STEP)
    delta = -ADAM_LR * (m_hat / (jnp.sqrt(v_hat) + ADAM_EPS) + ADAM_WD * w)
    return delta, m2, v2


def _adamw(name, w, m, v, parts, row0=0, into=None):
    R, C = w.shape
    Rp = parts[0].shape[1]
    tr = _tile(Rp, max(16, 393216 // C), 16)
    off = row0 // tr
    n_p = len(parts)
    held = [] if into is None else list(into)

    def body(*refs):
        w_ref, m_ref, v_ref = refs[:3]
        g_ref, d_ref, m2_ref, v2_ref = refs[3 + n_p + len(held):]
        g = None
        for p_ref in refs[3:3 + n_p]:
            for s in range(p_ref.shape[0]):
                t = p_ref[s].astype(F32)
                g = t if g is None else g + t
        delta, m2, v2 = _adam_math(w_ref[...], g, m_ref[...], v_ref[...])
        g_ref[...] = g
        d_ref[...] = delta
        m2_ref[...] = m2
        v2_ref[...] = v2

    blk = pl.BlockSpec((tr, C), lambda i: (i + off, 0))
    out = jax.ShapeDtypeStruct((R, C), F32)
    return pl.pallas_call(
        body, name=name, grid=(Rp // tr,),
        in_specs=[blk, blk, blk] + [pl.BlockSpec((a.shape[0], tr, C), lambda i: (0, i, 0)) for a in parts]
        + [pl.BlockSpec(memory_space=pl.ANY)] * len(held),
        out_specs=[blk] * 4, out_shape=[out] * 4, input_output_aliases={3 + n_p + i: i for i in range(len(held))},
        compiler_params=_cparams(("parallel",)))(w, m, v, *parts, *held)


def _small_update(gathered, w, m, v, after):
    _, R, L = gathered.shape
    rs = w.shape[0]

    def body(p_ref, w_ref, m_ref, v_ref, after_ref, g_ref, d_ref, m2_ref, v2_ref):
        g = p_ref[0]
        for p in range(1, N_DEV):
            g = g + p_ref[p]
        g_ref[...] = g
        delta, m2, v2 = _adam_math(w_ref[...], g[0:rs, :], m_ref[...], v_ref[...])
        d_ref[...] = delta
        m2_ref[...] = m2
        v2_ref[...] = v2

    vm = pl.BlockSpec(memory_space=pltpu.VMEM)
    sm = jax.ShapeDtypeStruct((rs, L), F32)
    return pl.pallas_call(body, name="small_update", in_specs=[vm] * 4 + [pl.BlockSpec(memory_space=pl.ANY)], out_specs=[vm] * 4,
                          out_shape=[jax.ShapeDtypeStruct((R, L), F32), sm, sm, sm],
                          compiler_params=pltpu.CompilerParams(vmem_limit_bytes=VMEM_LIMIT))(gathered, w, m, v, after)


class _Fetched(dict):
    def __init__(self, fetch):
        super().__init__()
        self.fetch = fetch

    def first(self, key, after):
        self[key] = self.fetch(key, after)
        return self[key]


def _local_step(x, tgt, mod, p, fetch, F, scatter=None):
    S, D = x.shape
    GW, HW = p["ln_g"].shape[1], p["hg_ng"].shape[1]
    G, T, _ = p["ws"].shape
    w = _Fetched(fetch)
    INW = 2 * GW + 4 * HW + 2 * D
    in_loc, br_loc, fi_loc = INW // N_DEV, D // N_DEV, 2 * F // N_DEV
    assert GW == HW and F % fi_loc == 0
    sh1, sc1, gt1, sh2, sc2, gt2 = (mod[:, k * D:(k + 1) * D] for k in range(6))
    bsb = jnp.broadcast_to(p["bs"][:, :, None], (G, T, GW // G))

    tm = _tile(S, 1024, 16)
    tmh = _tile(S, 512, 16)
    tn_in = _tile(in_loc, 1280)
    tn_d = _tile(D, 512)
    tn_br = _tile(br_loc, 512)
    tk_s = S
    tm_w = _tile(D, 1024)
    g_off = 2 * GW + 4 * HW

    h1 = _norm_mod("norm1", x, p["norm1_g"], sc1, sh1)
    z = _mm_nn_stacked("proj_in", h1, w.first("in", h1), tm=tm, tn=tn_in, tk=D)[0]
    ya = _gmlp_fwd(z, p["ln_g"], p["ln_b"], p["ws"], bsb, GW)
    yb, o_hg, states = _hg_fwd(z, p["hg_lb"], p["hg_ng"], HW)
    w.first("fi_early", yb)
    pa = _mm_nn_stacked("branch_gmlp", ya, w.first("bg", z), tm=tm, tn=tn_br, tk=GW)[0]

    def gates(ga_ref, gb_ref, ba_ref, bb_ref):
        return _sigmoid(ga_ref[...] + ba_ref[...]), _sigmoid(gb_ref[...] + bb_ref[...])

    def gate_specs(tn_):
        o1, o2 = g_off // tn_, (g_off + D) // tn_
        return [pl.BlockSpec((tm, tn_), lambda i, j, k: (i, o1 + j)), pl.BlockSpec((tm, tn_), lambda i, j, k: (i, o2 + j)),
                pl.BlockSpec((1, tn_), lambda i, j, k: (0, j)), pl.BlockSpec((1, tn_), lambda i, j, k: (0, D // tn_ + j))]

    def merge_ep(acc, ex, outs):
        ga, gb = gates(*ex[1:5])
        outs[0][...] = acc
        outs[1][...] = (ga * ex[0][...] + gb * acc).astype(BF16)

    tile_o = pl.BlockSpec((tm, tn_br), lambda i, j, k: (i, j))
    pb, y = _mm_nn_stacked(
        "branch_hg_merge", yb, w.first("bh", z), tm=tm, tn=tn_br, tk=HW, extras=[pa, z, z, p["b_gate"], p["b_gate"]],
        extra_specs=[tile_o, *gate_specs(tn_br)], out_shapes=[jax.ShapeDtypeStruct((S, D), F32), jax.ShapeDtypeStruct((S, D), BF16)],
        out_specs=[tile_o, tile_o], epilogue=merge_ep)

    def resid_ep(acc, ex, outs):
        outs[0][...] = acc
        outs[1][...] = ex[0][...] + ex[1][...] * acc

    def resid_mm(name, a, b, res, gt, tm_):
        K = a.shape[1]
        t_o = pl.BlockSpec((tm_, tn_d), lambda i, j, k: (i, j))
        return _matmul(
            name, a, b, dims=_NN, grid_mnk=(S // tm_, D // tn_d, 1), tiles=(tm_, tn_d),
            a_spec=pl.BlockSpec((tm_, K), lambda i, j, k: (i, 0)), b_spec=pl.BlockSpec((K, tn_d), lambda i, j, k: (0, j)),
            extras=[res, gt], extra_specs=[t_o, pl.BlockSpec((1, tn_d), lambda i, j, k: (0, j))],
            out_shapes=[jax.ShapeDtypeStruct((S, D), F32)] * 2, out_specs=[t_o, t_o], epilogue=resid_ep)

    o1, xm = resid_mm("proj_out", y, w.first("out", z), x, gt1, tm)
    h2 = _norm_mod("norm2", xm, p["norm2_g"], sc2, sh2)
    hf, hf_fac = _ffn_in_swiglu(h2, w.first("fi", h2))
    o2, x3 = resid_mm("ffn_out", hf, w.first("fo", hf), xm, gt2, tmh)
    dx3, do2, vec_l = _loss_head(x3, tgt, p["final_g"], o2, gt2)

    nf = F // fi_loc

    def dswiglu_ep(acc, ex, outs):
        outs[0][0] = (acc * ex[0][0].astype(F32)).astype(BF16)
        outs[0][1] = (acc * ex[0][1].astype(F32)).astype(BF16)

    pair = pl.BlockSpec((2, tmh, fi_loc), lambda i, j, k: (0, i, j))
    dab = _matmul(
        "ffn_out_dx", do2, w["fo"], dims=_NT, grid_mnk=(S // tmh, nf, 1), tiles=(tmh, fi_loc),
        a_spec=pl.BlockSpec((tmh, D), lambda i, j, k: (i, 0)), b_spec=pl.BlockSpec((fi_loc, D), lambda i, j, k: (j, 0)),
        extras=[hf_fac], extra_specs=[pair], out_shapes=[jax.ShapeDtypeStruct((2, S, F), BF16)], out_specs=[pair],
        epilogue=dswiglu_ep)[0]
    start = (lambda name, grads: scatter[0](name, grads)) if scatter is not None else (lambda name, grads: None)
    push = (lambda name, after: scatter[1](name, after)) if scatter is not None else (lambda name, after: None)

    def zero(token):
        return 0.0 if token is None else token[0:1, 0:1]

    tm_f = _tile(F, 512)
    g_fo = _mm_tn("ffn_out_dw", hf, do2, pl.BlockSpec((tk_s, D), lambda i, j, k: (k, j)), Mo=F, No=D, S=S, tm=tm_f, tn=D, tk=tk_s)
    g_fi = _mm_tn("ffn_in_dw", h2, dab, pl.BlockSpec((None, tk_s, fi_loc), lambda i, j, k: (j // nf, k, j % nf)),
                  Mo=D, No=2 * F, S=S, tm=tm_w, tn=fi_loc, tk=tk_s, stacked_nloc=fi_loc, after=g_fo)
    t_ffn = start("scatter_ffn", dict(fo=g_fo, fi=g_fi))
    dh2 = _mm_nt_stacked("ffn_in_dx", pl.BlockSpec((None, tmh, fi_loc), lambda i, j, k: (k // nf, i, k % nf)), dab, w["fi"],
                         M=S, tm=tmh, tn=D, tk=fi_loc, after=t_ffn)
    dxm, vec2, do1 = _norm_mod_bwd("norm2_bwd", dh2, xm, p["norm2_g"], sc2, dx3, o1, gt1)
    t_ffn = push("scatter_ffn", dxm)

    def dmerge_ep(acc, ex, outs):
        ga, gb = gates(*ex[2:6])
        outs[0][...] = (acc * ga).astype(BF16)
        outs[1][...] = (acc * gb).astype(BF16)
        outs[2][0] = (acc * ex[0][...] * ga * (1.0 - ga)).astype(BF16)
        outs[2][1] = (acc * ex[1][...] * gb * (1.0 - gb)).astype(BF16)

    t_o = pl.BlockSpec((tm, tn_d), lambda i, j, k: (i, j))
    dpa, dpb, dg2 = _matmul(
        "proj_out_dx", do1, w["out"], dims=_NT, grid_mnk=(S // tm, D // tn_d, 1), tiles=(tm, tn_d),
        a_spec=pl.BlockSpec((tm, D), lambda i, j, k: (i, 0)), b_spec=pl.BlockSpec((tn_d, D), lambda i, j, k: (j, 0)),
        extras=[pa, pb, z, z, p["b_gate"], p["b_gate"]], extra_specs=[t_o, t_o, *gate_specs(tn_d)],
        out_shapes=[jax.ShapeDtypeStruct((S, D), BF16), jax.ShapeDtypeStruct((S, D), BF16), jax.ShapeDtypeStruct((2, S, D), BF16)],
        out_specs=[t_o, t_o, pl.BlockSpec((2, tm, tn_d), lambda i, j, k: (0, i, j))], epilogue=dmerge_ep, after=t_ffn)
    g_out = _mm_tn("proj_out_dw", y, do1, pl.BlockSpec((tk_s, D), lambda i, j, k: (k, j)), Mo=D, No=D, S=S, tm=tn_d, tn=D, tk=tk_s)
    tn_g = _tile(GW, 512)
    b_br = pl.BlockSpec((tk_s, br_loc), lambda i, j, k: (k, j))
    g_bg = _mm_tn("branch_gmlp_dw", ya, dpa, b_br, Mo=GW, No=D, S=S, tm=tn_g, tn=br_loc, tk=tk_s, stacked_nloc=br_loc)
    g_bh = _mm_tn("branch_hg_dw", yb, dpb, b_br, Mo=HW, No=D, S=S, tm=tn_g, tn=br_loc, tk=tk_s, stacked_nloc=br_loc)
    t_mix = start("scatter_mixer", dict(out=g_out, bg=g_bg, bh=g_bh))
    def branch_dx(name, dp, wg):
        flat = jnp.swapaxes(wg, 0, 1).reshape(wg.shape[1], D)
        return _matmul(
            name, dp, flat, dims=_NT, grid_mnk=(S // tm, GW // tn_g, 1), tiles=(tm, tn_g),
            a_spec=pl.BlockSpec((tm, D), lambda i, j, k: (i, 0)), b_spec=pl.BlockSpec((tn_g, D), lambda i, j, k: (j, 0)),
            out_shapes=[jax.ShapeDtypeStruct((S, GW), F32)], out_specs=[pl.BlockSpec((tm, tn_g), lambda i, j, k: (i, j))],
            epilogue=_store(F32), after=t_mix)[0]

    dya = branch_dx("branch_gmlp_dx", dpa, w["bg"])
    dyb = branch_dx("branch_hg_dx", dpb, w["bh"])
    db_gate = _colsum2(dg2)
    dz_gmlp, dln, dws, dbs = _gmlp_bwd(z, dya, p["ln_g"], p["ln_b"], p["ws"], bsb, GW)
    t_mix = push("scatter_mixer", dz_gmlp)
    dz, dng, dhlb = _hg_bwd(z, o_hg, states, dyb, p["hg_lb"], p["hg_ng"] + zero(t_mix), HW, dz_gmlp, dg2)
    half = D // 2
    tm_h = _tile(half, 1024)
    g_in = []
    t_in = None
    for hname, h in (("a", 0), ("b", 1)):
        g_in.append(_mm_tn("proj_in_dw_" + hname, h1, dz, pl.BlockSpec((tk_s, in_loc), lambda i, j, k: (k, j)), Mo=half, No=INW, S=S,
                           tm=tm_h, tn=in_loc, tk=tk_s, stacked_nloc=in_loc, after=t_in, a_off=h * (half // tm_h)))
        t_in = start("scatter_proj_in_" + hname, {"w_in_" + hname: g_in[-1]})
    t_in = push("scatter_proj_in_a", t_in)
    dh1 = _mm_nt_stacked("proj_in_dx", pl.BlockSpec((tmh, in_loc), lambda i, j, k: (i, k)), dz, w["in"], M=S, tm=tmh, tn=D, tk=in_loc,
                         after=t_in)
    dx, vec1 = _norm_mod_bwd("norm1_bwd", dh1, x, p["norm1_g"], sc1, dxm)

    dmod = jnp.concatenate([vec1[0:1], vec1[1:2], vec2[3:4], vec2[0:1], vec2[1:2], vec_l[2:3]], axis=1)
    small = dict(norm1_g=vec1[2:3], b_gate=db_gate.reshape(1, 2 * D), ln_g=dln[0:1], ln_b=dln[1:2], ws=dws, bs=dbs.reshape(G, T),
                 hg_lb=dhlb, hg_ng=dng[0:1], norm2_g=vec2[2:3], final_g=vec_l[1:2], loss=vec_l[0:1, 0:LANES])
    big = dict(w_in_a=g_in[0], w_in_b=g_in[1], bg=g_bg, bh=g_bh, out=g_out, fi=g_fi, fo=g_fo)
    return dx, big, small, dmod


_SMALL = ("b_ada", "norm1_g", "b_gate", "ln_g", "ln_b", "ws", "bs", "hg_lb", "hg_ng", "norm2_g", "final_g")


def _pack(parts, rows_mult=8):
    flat = [a.reshape(-1) for a in parts]
    offs, n = [], 0
    for a in flat:
        offs.append(n)
        n += a.shape[0]
    pad = (-n) % (LANES * rows_mult)
    if pad:
        flat.append(jnp.zeros((pad,), F32))
    return jnp.concatenate(flat).reshape(-1, LANES), offs


def kernel(x, c, w_ada, b_ada, norm1_g, w_in, b_gate, gmlp_ln_g, gmlp_ln_b, gmlp_ws, gmlp_bs, hg_lb, hg_norm_g, w_branch_gmlp, w_branch_hg, w_out, norm2_g, w_ffn_in, w_ffn_out, final_norm_g, loss_target, m_w_ada, m_b_ada, m_norm1_g, m_w_in, m_b_gate, m_gmlp_ln_g, m_gmlp_ln_b, m_gmlp_ws, m_gmlp_bs, m_hg_lb, m_hg_norm_g, m_w_branch_gmlp, m_w_branch_hg, m_w_out, m_norm2_g, m_w_ffn_in, m_w_ffn_out, m_final_norm_g, v_w_ada, v_b_ada, v_norm1_g, v_w_in, v_b_gate, v_gmlp_ln_g, v_gmlp_ln_b, v_gmlp_ws, v_gmlp_bs, v_hg_lb, v_hg_norm_g, v_w_branch_gmlp, v_w_branch_hg, v_w_out, v_norm2_g, v_w_ffn_in, v_w_ffn_out, v_final_norm_g):
    S, D = x.shape[1], x.shape[2]
    ada_loc = w_ada.shape[2]
    me = 4 * lax.axis_index("x") + 2 * lax.axis_index("y") + lax.axis_index("c")

    c_all = _allgather_small("gather_c", c.reshape(D // LANES, LANES)).reshape(N_DEV, D)
    mod_cols, c_act = _ada_mod(jnp.pad(c_all, ((0, 16 - N_DEV), (0, 0))), w_ada[0])
    mod_all = _allgather_small("gather_mod", mod_cols[:N_DEV].reshape(-1, LANES)).reshape(N_DEV, N_DEV, ada_loc)
    mod = lax.dynamic_index_in_dim(mod_all, me, axis=1, keepdims=False).reshape(1, N_DEV * ada_loc) + b_ada

    def empty_hbm(shape, dtype):
        return pltpu.with_memory_space_constraint(lax.empty(shape, dtype), pltpu.HBM)

    groups = dict(gather_in=dict(keys=["in"], src=[w_in], forward=True),
                  gather_mixer=dict(keys=["bg", "bh", "out"], src=[w_branch_gmlp, w_branch_hg, w_out], forward=False),
                  gather_ffn_in=dict(keys=["fi"], src=[w_ffn_in], forward=True),
                  gather_ffn_out=dict(keys=["fo"], src=[w_ffn_out], forward=False))
    group_of = {k: gname for gname, g in groups.items() for k in g["keys"]}

    def first_hop(gname, after):
        g = groups[gname]
        n = len(g["keys"])
        shards = [a[0].astype(BF16) for a in g["src"]]
        outs = [lax.dynamic_update_slice(lax.empty((N_DEV, *s.shape), BF16), s[None], (me, 0, 0)) for s in shards]
        if g["forward"]:
            *g["hop"], token = _split_start(gname + "_hop1", shards + outs, n * 3, _forward_first_copies(n), after=after)
        else:
            *g["hop"], token = _split_start(gname + "_hop1", shards + outs, n * N_CHIP, _gather_first_copies(n), after=after)
        return token

    def second_hop(gname, after):
        g = groups[gname]
        n = len(g["keys"])
        *g["hop"], token = _split_relay(gname + "_hop2", g["hop"][2], g["hop"][0], g["hop"][1], after,
                                        _forward_first_copies(n), n * 3, _forward_second_copies(n))
        return token

    def finish(gname, after):
        g = groups[gname]
        n = len(g["keys"])
        send_sems, recv_sems, bufs = g["hop"]
        if g["forward"]:
            send_sems, recv_sems, bufs, _ = _split_relay(gname + "_hop3", bufs, send_sems, recv_sems, after,
                                                         _forward_second_copies(n), n, _forward_third_copies(n))
            bufs = _split_wait(gname + "_wait", bufs, send_sems, recv_sems, after, _forward_third_copies(n))
        else:
            send_sems, recv_sems, bufs, _ = _split_relay(gname + "_relay", bufs, send_sems, recv_sems, after,
                                                         _gather_first_copies(n), n * (N_CHIP - 1), _gather_relay_copies(n))
            bufs = _split_wait(gname + "_wait", bufs, send_sems, recv_sems, after, _gather_relay_copies(n))
        g["done"] = dict(zip(g["keys"], bufs[n:]))

    token = first_hop("gather_in", mod_all)
    mod = mod + token[0:1, 0:1]

    def fetch(key, after):
        if key == "in":
            t = second_hop("gather_in", after)
            t = first_hop("gather_mixer", t)
            t = first_hop("gather_ffn_in", t)
            finish("gather_in", t)
        elif key == "fi_early":
            first_hop("gather_ffn_out", second_hop("gather_ffn_in", after))
            return None
        elif "done" not in groups[group_of[key]]:
            finish(group_of[key], after)
        arr = groups[group_of[key]]["done"][key]
        return arr.reshape(-1, D) if key in ("out", "fo") else arr

    p = dict(norm1_g=norm1_g, b_gate=b_gate, ln_g=gmlp_ln_g, ln_b=gmlp_ln_b, ws=gmlp_ws[0], bs=gmlp_bs[0], hg_lb=hg_lb,
             hg_ng=hg_norm_g, norm2_g=norm2_g, final_g=final_norm_g.reshape(1, D))

    in_flight = {}
    c_idx = lax.axis_index("c").astype(jnp.int32).reshape(1)
    my_chip = 2 * lax.axis_index("x") + lax.axis_index("y")

    def scatter_start(name, grads):
        keys = list(grads)
        n = len(keys)
        stacks = [grads[k].reshape(N_DEV, -1, grads[k].shape[-1]) for k in keys]
        lands = [empty_hbm((N_CHIP, *g.shape[1:]), g.dtype) for g in stacks]
        send_sems, recv_sems, bufs, token = _split_start(name + "_d2d", stacks + lands, n * N_CHIP, _to_sibling_copies(n))
        in_flight[name] = dict(keys=keys, stage1=(send_sems, recv_sems, bufs))
        return token

    def scatter_push(name, after):
        f = in_flight[name]
        n = len(f["keys"])
        send_sems, recv_sems, bufs = f["stage1"]
        bufs = _split_wait(name + "_d2d_wait", bufs, send_sems, recv_sems, after, _to_sibling_copies(n))
        sums = [_chip_sum(f"{name}_sum_{k}", bufs[i], bufs[n + i], c_idx) for i, k in enumerate(f["keys"])]
        lands = [empty_hbm((N_CHIP - 1, *s.shape[1:]), s.dtype) for s in sums]
        send_sems, recv_sems, bufs, token = _split_start(name + "_ici", sums + lands, n * (N_CHIP - 1), _to_owner_copies(n))
        f["stage2"] = (send_sems, recv_sems, bufs)
        return token

    grad_x, _, small, dmod = _local_step(x[0], loss_target[0], mod, p, fetch, w_ffn_out.shape[1] * N_DEV, (scatter_start, scatter_push))

    small["b_ada"] = dmod
    packed, offs = _pack([small[k] for k in _SMALL] + [small["loss"]])
    gathered = _allgather_small("gather_small", packed)
    wp = dict(p, b_ada=b_ada)
    ms = dict(b_ada=m_b_ada, norm1_g=m_norm1_g, b_gate=m_b_gate, ln_g=m_gmlp_ln_g, ln_b=m_gmlp_ln_b, ws=m_gmlp_ws, bs=m_gmlp_bs,
              hg_lb=m_hg_lb, hg_ng=m_hg_norm_g, norm2_g=m_norm2_g, final_g=m_final_norm_g)
    vs = dict(b_ada=v_b_ada, norm1_g=v_norm1_g, b_gate=v_b_gate, ln_g=v_gmlp_ln_g, ln_b=v_gmlp_ln_b, ws=v_gmlp_ws, bs=v_gmlp_bs,
              hg_lb=v_hg_lb, hg_ng=v_hg_norm_g, norm2_g=v_norm2_g, final_g=v_final_norm_g)
    w_sm, _ = _pack([wp[k] for k in _SMALL])
    m_sm, _ = _pack([ms[k] for k in _SMALL])
    v_sm, _ = _pack([vs[k] for k in _SMALL])
    t_tail = scatter_push("scatter_proj_in_b", gathered)
    sm_out = _small_update(gathered, w_sm, m_sm, v_sm, t_tail)
    shapes = dict(b_ada=b_ada.shape, norm1_g=norm1_g.shape, b_gate=b_gate.shape, ln_g=gmlp_ln_g.shape, ln_b=gmlp_ln_b.shape,
                  ws=gmlp_ws.shape, bs=gmlp_bs.shape, hg_lb=hg_lb.shape, hg_ng=hg_norm_g.shape, norm2_g=norm2_g.shape,
                  final_g=final_norm_g.shape)

    def unpack(arr, k):
        i = _SMALL.index(k)
        n = math.prod(shapes[k])
        return arr.reshape(-1)[offs[i]:offs[i] + n].reshape(shapes[k])

    loss = sm_out[0].reshape(-1)[offs[len(_SMALL)]]

    dmod_all = gathered.reshape(N_DEV, -1)[:, offs[0]:offs[0] + N_DEV * ada_loc]
    dmod_loc = lax.dynamic_slice_in_dim(dmod_all, me * ada_loc, ada_loc, axis=1)
    ca_t = jnp.pad(c_act[:N_DEV].T, ((0, 0), (0, LANES - N_DEV))).astype(BF16)
    dm_p = jnp.pad(dmod_loc, ((0, LANES - N_DEV), (0, 0))).astype(BF16)
    tm_a = _tile(D, 512)
    g_ada = _matmul(
        "ada_dw", ca_t, dm_p, dims=_NN, grid_mnk=(D // tm_a, 1, 1), tiles=(tm_a, ada_loc),
        a_spec=pl.BlockSpec((tm_a, LANES), lambda i, j, k: (i, 0)), b_spec=pl.BlockSpec((LANES, ada_loc), lambda i, j, k: (0, 0)),
        out_shapes=[jax.ShapeDtypeStruct((1, D, ada_loc), F32)], out_specs=[pl.BlockSpec((None, tm_a, ada_loc), lambda i, j, k: (0, i, 0))],
        epilogue=_store(F32), after=t_tail)[0]

    upd = {"w_ada": _adamw("adamw_w_ada", w_ada[0], m_w_ada[0], v_w_ada[0], [g_ada])}
    big_w = dict(w_in=(w_in, m_w_in, v_w_in, "w_in"), bg=(w_branch_gmlp, m_w_branch_gmlp, v_w_branch_gmlp, "w_branch_gmlp"),
                 bh=(w_branch_hg, m_w_branch_hg, v_w_branch_hg, "w_branch_hg"), out=(w_out, m_w_out, v_w_out, "w_out"),
                 fi=(w_ffn_in, m_w_ffn_in, v_w_ffn_in, "w_ffn_in"), fo=(w_ffn_out, m_w_ffn_out, v_w_ffn_out, "w_ffn_out"))
    after = upd["w_ada"][1]
    for name in ("scatter_ffn", "scatter_mixer", "scatter_proj_in_a", "scatter_proj_in_b"):
        keys = in_flight[name]["keys"]
        n = len(keys)
        send_sems, recv_sems, bufs = in_flight[name]["stage2"]
        bufs = _split_wait(name + "_ici_wait", bufs, send_sems, recv_sems, after, _to_owner_copies(n))
        for i, k in enumerate(keys):
            parts = [lax.dynamic_index_in_dim(bufs[i], my_chip, axis=0, keepdims=True), bufs[n + i]]
            if k in big_w:
                wt, mt, vt, out_name = big_w[k]
                upd[out_name] = _adamw("adamw_" + out_name, wt[0], mt[0], vt[0], parts)
            else:
                wt, mt, vt, out_name = big_w["w_in"]
                upd[out_name] = _adamw("adamw_" + k, wt[0], mt[0], vt[0], parts, row0=0 if k == "w_in_a" else parts[0].shape[1],
                                       into=upd.get(out_name))
            after = upd[out_name][1]

    order = ("w_ada", "b_ada", "norm1_g", "w_in", "b_gate", "ln_g", "ln_b", "ws", "bs", "hg_lb", "hg_ng", "w_branch_gmlp", "w_branch_hg",
             "w_out", "norm2_g", "w_ffn_in", "w_ffn_out", "final_g")
    outs = [loss, grad_x[None]]
    for idx in range(4):
        for k in order:
            outs.append(upd[k][idx][None] if k in upd else unpack(sm_out[idx], k))
    return tuple(outs)
```

```python
import functools
import math

import jax
import jax.numpy as jnp
from jax import lax
from jax.experimental import pallas as pl
from jax.experimental.pallas import tpu as pltpu

F32 = jnp.float32
BF16 = jnp.bfloat16
N_DEV = 8
EPS = 1e-6
LANES = 128
HG_DK = 128
HG_CHUNK = 64
HG_MID = HG_CHUNK // 2 - 1
EXP_CLAMP = 80.0
VMEM_LIMIT = 48 * 1024 * 1024
ADAM_LR, ADAM_B1, ADAM_B2, ADAM_EPS, ADAM_WD, ADAM_STEP = 0.001, 0.9, 0.999, 1e-08, 0.01, 10
MESH = pl.DeviceIdType.MESH

_NN = (((1,), (0,)), ((), ()))
_NT = (((1,), (1,)), ((), ()))
_TN = (((0,), (0,)), ((), ()))


def _dot(a, b, dims=_NN):
    return lax.dot_general(a.astype(BF16), b.astype(BF16), dims, preferred_element_type=F32)


def _tile(n, target, mult=LANES):
    best = None
    for t in range(mult, min(n, target) + 1, mult):
        if n % t == 0:
            best = t
    return n if best is None else best


def _cparams(sem):
    return pltpu.CompilerParams(dimension_semantics=sem, vmem_limit_bytes=VMEM_LIMIT)


def _sigmoid(x):
    return 1.0 / (1.0 + jnp.exp(-x))


def _gelu_parts(x):
    k0 = math.sqrt(2.0 / math.pi)
    x2 = x * x
    t = jnp.tanh(k0 * (x + 0.044715 * x * x2))
    g = 0.5 * x * (1.0 + t)
    dg = 0.5 * (1.0 + t) + 0.5 * x * (1.0 - t * t) * (k0 * (1.0 + 3.0 * 0.044715 * x2))
    return g, dg


def _split3(x):
    h = x.astype(BF16)
    r = x - h.astype(F32)
    m = r.astype(BF16)
    lo = (r - m.astype(F32)).astype(BF16)
    return h, m, lo


def _ones_dot(mat01, x):
    h, m, lo = _split3(x)
    d = functools.partial(lax.dot_general, dimension_numbers=_NN, preferred_element_type=F32)
    return d(mat01, h) + d(mat01, m) + d(mat01, lo)


def _matmul(name, a, b, *, dims, grid_mnk, tiles, a_spec, b_spec, extras=(), extra_specs=(), out_shapes, out_specs, epilogue, after=None):
    gm, gn, nk = grid_mnk
    tm, tn = tiles
    n_ex, n_out = len(extras), len(out_shapes)
    held = [] if after is None else [after]

    def body(*refs):
        a_ref, b_ref = refs[0], refs[1]
        ex = refs[2:2 + n_ex]
        outs = refs[2 + n_ex + len(held):2 + n_ex + len(held) + n_out]
        if nk == 1:
            epilogue(lax.dot_general(a_ref[...], b_ref[...], dims, preferred_element_type=F32), ex, outs)
            return
        acc = refs[-1]
        k = pl.program_id(2)

        @pl.when(k == 0)
        def _():
            acc[...] = jnp.zeros_like(acc)

        acc[...] += lax.dot_general(a_ref[...], b_ref[...], dims, preferred_element_type=F32)

        @pl.when(k == nk - 1)
        def _():
            epilogue(acc[...], ex, outs)

    return pl.pallas_call(
        body, name=name, grid=(gm, gn, nk), in_specs=[a_spec, b_spec, *extra_specs] + [pl.BlockSpec(memory_space=pl.ANY)] * len(held),
        out_specs=list(out_specs), out_shape=list(out_shapes), scratch_shapes=[] if nk == 1 else [pltpu.VMEM((tm, tn), F32)],
        compiler_params=_cparams(("parallel", "parallel", "arbitrary")),
    )(a, b, *extras, *held)


def _store(dtype):
    def ep(acc, ex, outs):
        outs[0][...] = acc.astype(dtype)
    return ep


def _mm_nn_stacked(name, a, wg, *, tm, tn, tk, out_dtype=F32, extras=(), extra_specs=(), out_shapes=None, out_specs=None, epilogue=None,
                   after=None):
    M, K = a.shape
    _, _, nloc = wg.shape
    N = nloc * N_DEV
    q = nloc // tn
    if out_shapes is None:
        out_shapes = [jax.ShapeDtypeStruct((M, N), out_dtype)]
        out_specs = [pl.BlockSpec((tm, tn), lambda i, j, k: (i, j))]
        epilogue = _store(out_dtype)
    return _matmul(
        name, a, wg, dims=_NN, grid_mnk=(M // tm, N // tn, K // tk), tiles=(tm, tn),
        a_spec=pl.BlockSpec((tm, tk), lambda i, j, k: (i, k)),
        b_spec=pl.BlockSpec((None, tk, tn), lambda i, j, k: (j // q, k, j % q)),
        extras=extras, extra_specs=extra_specs, out_shapes=out_shapes, out_specs=out_specs, epilogue=epilogue, after=after)


def _mm_nt_stacked(name, a_spec, a, wg, *, M, tm, tn, tk, out_dtype=F32, after=None):
    _, Kw, nloc = wg.shape
    q = nloc // tk
    return _matmul(
        name, a, wg, dims=_NT, grid_mnk=(M // tm, Kw // tn, (nloc * N_DEV) // tk), tiles=(tm, tn),
        a_spec=a_spec, b_spec=pl.BlockSpec((None, tn, tk), lambda i, j, k: (k // q, j, k % q)),
        out_shapes=[jax.ShapeDtypeStruct((M, Kw), out_dtype)], out_specs=[pl.BlockSpec((tm, tn), lambda i, j, k: (i, j))],
        epilogue=_store(out_dtype), after=after)[0]


def _mm_tn(name, a, b, b_spec, *, Mo, No, S, tm, tn, tk, stacked_nloc=None, after=None, a_off=0):
    if stacked_nloc is None:
        out_shape = jax.ShapeDtypeStruct((Mo, No), BF16)
        out_spec = pl.BlockSpec((tm, tn), lambda i, j, k: (i, j))
    else:
        q = stacked_nloc // tn
        out_shape = jax.ShapeDtypeStruct((N_DEV, Mo, stacked_nloc), BF16)
        out_spec = pl.BlockSpec((None, tm, tn), lambda i, j, k: (j // q, i, j % q))
    return _matmul(
        name, a, b, dims=_TN, grid_mnk=(Mo // tm, No // tn, S // tk), tiles=(tm, tn),
        a_spec=pl.BlockSpec((tk, tm), lambda i, j, k: (k, i + a_off)), b_spec=b_spec,
        out_shapes=[out_shape], out_specs=[out_spec], epilogue=_store(BF16), after=after)[0]


def _norm_mod(name, x, g, sc, sh):
    S, D = x.shape
    tm = _tile(S, 256, 8)

    def body(x_ref, g_ref, sc_ref, sh_ref, h_ref):
        xv = x_ref[...]
        r = lax.rsqrt(jnp.mean(xv * xv, axis=-1, keepdims=True) + EPS)
        h = (xv * r) * g_ref[...]
        h_ref[...] = (h * (1.0 + sc_ref[...]) + sh_ref[...]).astype(BF16)

    row = pl.BlockSpec((tm, D), lambda i: (i, 0))
    vec = pl.BlockSpec((1, D), lambda i: (0, 0))
    return pl.pallas_call(body, name=name, grid=(S // tm,), in_specs=[row, vec, vec, vec], out_specs=row,
                          out_shape=jax.ShapeDtypeStruct((S, D), BF16), compiler_params=_cparams(("parallel",)))(x, g, sc, sh)


def _norm_mod_bwd(name, dh, x, g, sc, dres, o=None, gt=None):
    S, D = x.shape
    tm = _tile(S, 256, 8)
    gated = o is not None

    def body(*refs):
        if gated:
            dh_ref, x_ref, g_ref, sc_ref, dres_ref, o_ref, gt_ref, dx_ref, vec_ref, do_ref = refs
        else:
            dh_ref, x_ref, g_ref, sc_ref, dres_ref, dx_ref, vec_ref = refs
        i = pl.program_id(0)

        @pl.when(i == 0)
        def _():
            vec_ref[...] = jnp.zeros_like(vec_ref)

        xv, dh_v, gv = x_ref[...], dh_ref[...], g_ref[...]
        r = lax.rsqrt(jnp.mean(xv * xv, axis=-1, keepdims=True) + EPS)
        xn = xv * r
        one_sc = 1.0 + sc_ref[...]
        vec_ref[0:1, :] += jnp.sum(dh_v, axis=0, keepdims=True)
        vec_ref[1:2, :] += jnp.sum(dh_v * (xn * gv), axis=0, keepdims=True)
        vec_ref[2:3, :] += jnp.sum(dh_v * one_sc * xn, axis=0, keepdims=True)
        dxn = dh_v * one_sc * gv
        dx = dres_ref[...] + r * (dxn - xn * jnp.mean(dxn * xn, axis=-1, keepdims=True))
        dx_ref[...] = dx
        if gated:
            vec_ref[3:4, :] += jnp.sum(dx * o_ref[...], axis=0, keepdims=True)
            do_ref[...] = (dx * gt_ref[...]).astype(BF16)

    row = pl.BlockSpec((tm, D), lambda i: (i, 0))
    vec = pl.BlockSpec((1, D), lambda i: (0, 0))
    acc = pl.BlockSpec((8, D), lambda i: (0, 0))
    ins = [dh, x, g, sc, dres] + ([o, gt] if gated else [])
    in_specs = [row, row, vec, vec, row] + ([row, vec] if gated else [])
    out_shape = [jax.ShapeDtypeStruct((S, D), F32), jax.ShapeDtypeStruct((8, D), F32)]
    out_specs = [row, acc]
    if gated:
        out_shape.append(jax.ShapeDtypeStruct((S, D), BF16))
        out_specs.append(row)
    return pl.pallas_call(body, name=name, grid=(S // tm,), in_specs=in_specs, out_specs=out_specs, out_shape=out_shape,
                          compiler_params=_cparams(("arbitrary",)))(*ins)


def _loss_head(x3, tgt, gf, o2, gt2):
    S, D = x3.shape
    tm = _tile(S, 256, 8)

    def body(x_ref, t_ref, g_ref, o_ref, gt_ref, dx_ref, do_ref, vec_ref):
        i = pl.program_id(0)

        @pl.when(i == 0)
        def _():
            vec_ref[...] = jnp.zeros_like(vec_ref)

        xv, gv = x_ref[...], g_ref[...]
        r = lax.rsqrt(jnp.mean(xv * xv, axis=-1, keepdims=True) + EPS)
        xn = xv * r
        e = xn * gv - t_ref[...]
        tok = 0.5 * jnp.mean(e * e, axis=-1, keepdims=True)
        vec_ref[0:1, :] += jnp.broadcast_to(jnp.sum(tok, axis=0, keepdims=True), (1, D))
        dy = e * (1.0 / D)
        vec_ref[1:2, :] += jnp.sum(dy * xn, axis=0, keepdims=True)
        dxn = dy * gv
        dx = r * (dxn - xn * jnp.mean(dxn * xn, axis=-1, keepdims=True))
        dx_ref[...] = dx
        vec_ref[2:3, :] += jnp.sum(dx * o_ref[...], axis=0, keepdims=True)
        do_ref[...] = (dx * gt_ref[...]).astype(BF16)

    row = pl.BlockSpec((tm, D), lambda i: (i, 0))
    vec = pl.BlockSpec((1, D), lambda i: (0, 0))
    return pl.pallas_call(
        body, name="loss_head", grid=(S // tm,), in_specs=[row, row, vec, row, vec],
        out_specs=[row, row, pl.BlockSpec((8, D), lambda i: (0, 0))],
        out_shape=[jax.ShapeDtypeStruct((S, D), F32), jax.ShapeDtypeStruct((S, D), BF16), jax.ShapeDtypeStruct((8, D), F32)],
        compiler_params=_cparams(("arbitrary",)))(x3, tgt, gf, o2, gt2)


def _ffn_in_swiglu(h, wg):
    S, D = h.shape
    _, _, tf = wg.shape
    nf = N_DEV // 2
    F = nf * tf
    tm = _tile(S, 256, 16)

    def body(h_ref, wa_ref, wu_ref, hf_ref, fac_ref):
        hv = h_ref[...]
        a = lax.dot_general(hv, wa_ref[...], _NN, preferred_element_type=F32)
        up = lax.dot_general(hv, wu_ref[...], _NN, preferred_element_type=F32)
        sa = _sigmoid(a)
        silu = a * sa
        hf_ref[...] = (silu * up).astype(BF16)
        fac_ref[0] = (up * (sa * (1.0 + a * (1.0 - sa)))).astype(BF16)
        fac_ref[1] = silu.astype(BF16)

    return pl.pallas_call(
        body, name="ffn_in_swiglu", grid=(nf, S // tm),
        in_specs=[pl.BlockSpec((tm, D), lambda j, i: (i, 0)), pl.BlockSpec((None, D, tf), lambda j, i: (j, 0, 0)),
                  pl.BlockSpec((None, D, tf), lambda j, i: (j + nf, 0, 0))],
        out_specs=[pl.BlockSpec((tm, tf), lambda j, i: (i, j)), pl.BlockSpec((2, tm, tf), lambda j, i: (0, i, j))],
        out_shape=[jax.ShapeDtypeStruct((S, F), BF16), jax.ShapeDtypeStruct((2, S, F), BF16)],
        compiler_params=_cparams(("parallel", "parallel")))(h, wg, wg)


def _colsum2(dg2):
    _, S, D = dg2.shape
    tm = _tile(S, 256, 16)

    def body(x_ref, o_ref):
        @pl.when(pl.program_id(0) == 0)
        def _():
            o_ref[...] = jnp.zeros_like(o_ref)

        o_ref[0:1, :] += jnp.sum(x_ref[0].astype(F32), axis=0, keepdims=True)
        o_ref[1:2, :] += jnp.sum(x_ref[1].astype(F32), axis=0, keepdims=True)

    return pl.pallas_call(body, name="gate_bias_grad", grid=(S // tm,), in_specs=[pl.BlockSpec((2, tm, D), lambda i: (0, i, 0))],
                          out_specs=pl.BlockSpec((2, D), lambda i: (0, 0)), out_shape=jax.ShapeDtypeStruct((2, D), F32),
                          compiler_params=_cparams(("arbitrary",)))(dg2)


def _gmlp_common(u_ref, v_ref, lg_ref, lb_ref, ws_ref, bsb_ref, G, T, Dg):
    ug, dug = _gelu_parts(u_ref[...])
    vg, dvg = _gelu_parts(v_ref[...])
    mu = jnp.mean(vg, axis=-1, keepdims=True)
    vc = vg - mu
    rstd = lax.rsqrt(jnp.mean(vc * vc, axis=-1, keepdims=True) + EPS)
    vhat = vc * rstd
    vn = vhat * lg_ref[...] + lb_ref[...]
    row = lax.broadcasted_iota(jnp.int32, (T, T), 0)
    col = lax.broadcasted_iota(jnp.int32, (T, T), 1)
    tril = row >= col
    s = []
    for g in range(G):
        w = jnp.where(tril, ws_ref[g], 0.0)
        s.append(_dot(w, vn[:, g * Dg:(g + 1) * Dg]) + bsb_ref[g])
    return ug, dug, dvg, rstd, vhat, vn, tril, s


def _gmlp_fwd(z, ln_g, ln_b, ws, bsb, GW):
    S = z.shape[0]
    G, T, _ = ws.shape
    Dg = GW // G

    def body(u_ref, v_ref, lg_ref, lb_ref, ws_ref, bsb_ref, ya_ref):
        ug, _, _, _, _, _, _, s = _gmlp_common(u_ref, v_ref, lg_ref, lb_ref, ws_ref, bsb_ref, G, T, Dg)
        for g in range(G):
            sl = slice(g * Dg, (g + 1) * Dg)
            ya_ref[:, sl] = (ug[:, sl] * s[g]).astype(BF16)

    vec = pl.BlockSpec((1, GW), lambda c: (0, 0))
    return pl.pallas_call(
        body, name="gmlp_fwd", grid=(S // T,),
        in_specs=[pl.BlockSpec((T, GW), lambda c: (c, 0)), pl.BlockSpec((T, GW), lambda c: (c, 1)), vec, vec,
                  pl.BlockSpec((G, T, T), lambda c: (0, 0, 0)), pl.BlockSpec((G, T, Dg), lambda c: (0, 0, 0))],
        out_specs=pl.BlockSpec((T, GW), lambda c: (c, 0)), out_shape=jax.ShapeDtypeStruct((S, GW), BF16),
        compiler_params=_cparams(("parallel",)))(z, z, ln_g, ln_b, ws, bsb)


def _gmlp_bwd(z, dya, ln_g, ln_b, ws, bsb, GW):
    S = z.shape[0]
    G, T, _ = ws.shape
    Dg = GW // G
    nc = S // T

    def body(u_ref, v_ref, dya_ref, lg_ref, lb_ref, ws_ref, bsb_ref, dz_ref, dln_ref, dws_ref, dbs_ref, dbs_acc, dvh):
        c = pl.program_id(0)

        @pl.when(c == 0)
        def _():
            dln_ref[...] = jnp.zeros_like(dln_ref)
            dws_ref[...] = jnp.zeros_like(dws_ref)
            dbs_acc[...] = jnp.zeros_like(dbs_acc)

        ug, dug, dvg, rstd, vhat, vn, tril, s = _gmlp_common(u_ref, v_ref, lg_ref, lb_ref, ws_ref, bsb_ref, G, T, Dg)
        dya_v = dya_ref[...]
        for g in range(G):
            sl = slice(g * Dg, (g + 1) * Dg)
            dy_g = dya_v[:, sl]
            dz_ref[:, sl] = (dy_g * s[g] * dug[:, sl]).astype(BF16)
            ds = dy_g * ug[:, sl]
            dbs_acc[g] += ds
            w = jnp.where(tril, ws_ref[g], 0.0)
            dvn_g = _dot(w, ds, _TN)
            dws_ref[g] += jnp.where(tril, _dot(ds, vn[:, sl], _NT), 0.0)
            dln_ref[0:1, sl] += jnp.sum(dvn_g * vhat[:, sl], axis=0, keepdims=True)
            dln_ref[1:2, sl] += jnp.sum(dvn_g, axis=0, keepdims=True)
            dvh[:, sl] = dvn_g * lg_ref[:, sl]
        dvhat = dvh[...]
        m1 = jnp.mean(dvhat, axis=-1, keepdims=True)
        m2 = jnp.mean(dvhat * vhat, axis=-1, keepdims=True)
        dz_ref[:, GW:2 * GW] = (rstd * (dvhat - m1 - vhat * m2) * dvg).astype(BF16)

        @pl.when(c == nc - 1)
        def _():
            for g in range(G):
                dbs_ref[g] = jnp.sum(dbs_acc[g], axis=-1, keepdims=True)

    vec = pl.BlockSpec((1, GW), lambda c: (0, 0))
    return pl.pallas_call(
        body, name="gmlp_bwd", grid=(nc,),
        in_specs=[pl.BlockSpec((T, GW), lambda c: (c, 0)), pl.BlockSpec((T, GW), lambda c: (c, 1)),
                  pl.BlockSpec((T, GW), lambda c: (c, 0)), vec, vec,
                  pl.BlockSpec((G, T, T), lambda c: (0, 0, 0)), pl.BlockSpec((G, T, Dg), lambda c: (0, 0, 0))],
        out_specs=[pl.BlockSpec((T, 2 * GW), lambda c: (c, 0)), pl.BlockSpec((8, GW), lambda c: (0, 0)),
                   pl.BlockSpec((G, T, T), lambda c: (0, 0, 0)), pl.BlockSpec((G, T, 1), lambda c: (0, 0, 0))],
        out_shape=[jax.ShapeDtypeStruct((S, 2 * GW), BF16), jax.ShapeDtypeStruct((8, GW), F32),
                   jax.ShapeDtypeStruct((G, T, T), F32), jax.ShapeDtypeStruct((G, T, 1), F32)],
        scratch_shapes=[pltpu.VMEM((G, T, Dg), F32), pltpu.VMEM((T, GW), F32)],
        compiler_params=_cparams(("arbitrary",)))(z, z, dya, ln_g, ln_b, ws, bsb)


def _hg_common(q_ref, f_ref, hlb_ref):
    C = HG_CHUNK
    a = hlb_ref[...]
    lb = _sigmoid(a[0:1, :] - a[1:2, :])
    sig = _sigmoid(f_ref[...])
    f = lb + (1.0 - lb) * sig
    lf = jnp.log(f)
    kk = 1.0 - f
    q = q_ref[...]
    sq = _sigmoid(q)
    qa = q * sq
    row = lax.broadcasted_iota(jnp.int32, (C, C), 0)
    col = lax.broadcasted_iota(jnp.int32, (C, C), 1)
    tril = row >= col
    b = _ones_dot(tril.astype(BF16), lf)
    bm = b[HG_MID:HG_MID + 1, :]
    bl = b[C - 1:C, :]
    e_b = jnp.exp(b)
    e_qm = jnp.exp(jnp.minimum(b - bm, EXP_CLAMP))
    e_km = jnp.exp(jnp.minimum(bm - b, EXP_CLAMP))
    e_kl = jnp.exp(bl - b)
    return dict(lb=lb, sig=sig, f=f, kk=kk, q=q, sq=sq, qa=qa, tril=tril, e_b=e_b, e_qm=e_qm, e_km=e_km, e_kl=e_kl,
                e_l=jnp.exp(bl), qh=qa * e_b, qt=qa * e_qm, kt=kk * e_km, kh=kk * e_kl)


def _hg_fwd(z, hg_lb, ng, HW):
    S = z.shape[0]
    C, H, dk = HG_CHUNK, HW // HG_DK, HG_DK
    nc = S // C

    def body(q_ref, f_ref, i_ref, og_ref, hlb_ref, ng_ref, yb_ref, o_ref, st_ref, state):
        @pl.when(pl.program_id(0) == 0)
        def _():
            state[...] = jnp.zeros_like(state)

        t = _hg_common(q_ref, f_ref, hlb_ref)
        iv = i_ref[...]
        for h in range(H):
            sl = slice(h * dk, (h + 1) * dk)
            st = state[h]
            st_ref[h] = st
            a = jnp.where(t["tril"], _dot(t["qt"][:, sl], t["kt"][:, sl], _NT), 0.0)
            o_h = _dot(a, iv[:, sl]) + _dot(t["qh"][:, sl], st, _NT)
            state[h] = st * t["e_l"][:, sl] + _dot(iv[:, sl], t["kh"][:, sl], _TN)
            o_ref[:, sl] = o_h
            rr = lax.rsqrt(jnp.mean(o_h * o_h, axis=-1, keepdims=True) + EPS)
            og = og_ref[:, sl]
            yb_ref[:, sl] = (o_h * rr * ng_ref[:, sl] * (og * _sigmoid(og))).astype(BF16)

    def col(k):
        return pl.BlockSpec((C, HW), lambda c: (c, k))

    base = 2
    return pl.pallas_call(
        body, name="hgrn_fwd", grid=(nc,),
        in_specs=[col(base), col(base + 1), col(base + 2), col(base + 3),
                  pl.BlockSpec((2, HW), lambda c: (0, 0)), pl.BlockSpec((1, HW), lambda c: (0, 0))],
        out_specs=[pl.BlockSpec((C, HW), lambda c: (c, 0)), pl.BlockSpec((C, HW), lambda c: (c, 0)),
                   pl.BlockSpec((None, H, dk, dk), lambda c: (c, 0, 0, 0))],
        out_shape=[jax.ShapeDtypeStruct((S, HW), BF16), jax.ShapeDtypeStruct((S, HW), F32),
                   jax.ShapeDtypeStruct((nc, H, dk, dk), F32)],
        scratch_shapes=[pltpu.VMEM((H, dk, dk), F32)],
        compiler_params=_cparams(("arbitrary",)))(z, z, z, z, hg_lb, ng)


def _hg_bwd(z, o, states, dyb, hg_lb, ng, HW, dz_head, dz_tail):
    S = z.shape[0]
    C, H, dk = HG_CHUNK, HW // HG_DK, HG_DK
    nc = S // C
    B0 = dz_head.shape[1]
    DT = dz_tail.shape[2]
    INW = B0 + 4 * HW + 2 * DT

    def body(q_ref, f_ref, i_ref, og_ref, o_ref, st_ref, stn_ref, dyb_ref, hlb_ref, ng_ref, head_ref, tail_ref,
             dzf_ref, dng_ref, dhlb_ref, dstate, cross, dqa_buf, dkk_buf, db_buf, dlb_acc):
        c = pl.program_id(0)
        dzf_ref[:, 0:B0] = head_ref[...]
        dzf_ref[:, B0 + 4 * HW:B0 + 4 * HW + DT] = tail_ref[0]
        dzf_ref[:, B0 + 4 * HW + DT:INW] = tail_ref[1]
        dz_ref = dzf_ref.at[:, B0:B0 + 4 * HW]

        @pl.when(c == 0)
        def _():
            dstate[...] = jnp.zeros_like(dstate)
            dlb_acc[...] = jnp.zeros_like(dlb_acc)
            dng_ref[...] = jnp.zeros_like(dng_ref)

        def r16(v):
            return v.astype(BF16).astype(F32)

        t = _hg_common(q_ref, f_ref, hlb_ref)
        iv = i_ref[...]
        for h in range(H):
            sl = slice(h * dk, (h + 1) * dk)
            o_h, og, dyb_h, ng_h = o_ref[:, sl], og_ref[:, sl], dyb_ref[:, sl], ng_ref[:, sl]
            sg = _sigmoid(og)
            silu_og = og * sg
            rr = lax.rsqrt(jnp.mean(o_h * o_h, axis=-1, keepdims=True) + EPS)
            on = o_h * rr
            dng_ref[0:1, sl] += jnp.sum(dyb_h * on * silu_og, axis=0, keepdims=True)
            dz_ref[:, 3 * HW + h * dk:3 * HW + (h + 1) * dk] = (dyb_h * on * ng_h * (sg * (1.0 + og * (1.0 - sg)))).astype(BF16)
            don = dyb_h * ng_h * silu_og
            do_h = rr * (don - on * jnp.mean(don * on, axis=-1, keepdims=True))

            qt, kt, qh, kh, iv_h = t["qt"][:, sl], t["kt"][:, sl], t["qh"][:, sl], t["kh"][:, sl], iv[:, sl]
            a = jnp.where(t["tril"], _dot(qt, kt, _NT), 0.0)
            da = jnp.where(t["tril"], _dot(do_h, iv_h, _NT), 0.0)
            st, dst = st_ref[h], dstate[h]
            cross[:, sl] = jnp.sum(dst * stn_ref[h], axis=0, keepdims=True)
            dqh = _dot(do_h, st)
            dstate[h] = _dot(do_h, qh, _TN) + dst * t["e_l"][:, sl]
            div = _dot(a, do_h, _TN) + _dot(kh, dst, _NT)
            dkh = _dot(iv_h, dst)
            dqt = _dot(da, kt)
            dkt = _dot(da, qt, _TN)
            dz_ref[:, 2 * HW + h * dk:2 * HW + (h + 1) * dk] = div.astype(BF16)
            dqa_buf[:, sl] = dqh * t["e_b"][:, sl] + dqt * t["e_qm"][:, sl]
            dkk_buf[:, sl] = dkt * t["e_km"][:, sl] + dkh * t["e_kl"][:, sl]
            db_buf[:, sl] = r16(qt) * dqt - r16(kt) * dkt + r16(qh) * dqh - r16(kh) * dkh

        dqa, dkk = dqa_buf[...], dkk_buf[...]
        triu = jnp.logical_not(t["tril"]) | (lax.broadcasted_iota(jnp.int32, (C, C), 0) == lax.broadcasted_iota(jnp.int32, (C, C), 1))
        dlf = _ones_dot(triu.astype(BF16), db_buf[...]) + cross[...]
        df = dlf / t["f"] - dkk
        sig, lb = t["sig"], t["lb"]
        dz_ref[:, HW:2 * HW] = (df * (1.0 - lb) * sig * (1.0 - sig)).astype(BF16)
        dlb_acc[...] += jnp.sum(df * (1.0 - sig), axis=0, keepdims=True)
        q, sq = t["q"], t["sq"]
        dz_ref[:, 0:HW] = (dqa * (sq * (1.0 + q * (1.0 - sq)))).astype(BF16)

        @pl.when(c == nc - 1)
        def _():
            da0 = dlb_acc[...] * lb * (1.0 - lb)
            dhlb_ref[0:1, :] = da0
            dhlb_ref[1:2, :] = -da0

    def col(k):
        return pl.BlockSpec((C, HW), lambda c: (nc - 1 - c, k))

    base = 2
    return pl.pallas_call(
        body, name="hgrn_bwd", grid=(nc,),
        in_specs=[col(base), col(base + 1), col(base + 2), col(base + 3), col(0),
                  pl.BlockSpec((None, H, dk, dk), lambda c: (nc - 1 - c, 0, 0, 0)),
                  pl.BlockSpec((None, H, dk, dk), lambda c: (jnp.minimum(nc - c, nc - 1), 0, 0, 0)), col(0),
                  pl.BlockSpec((2, HW), lambda c: (0, 0)), pl.BlockSpec((1, HW), lambda c: (0, 0)),
                  pl.BlockSpec((C, B0), lambda c: (nc - 1 - c, 0)), pl.BlockSpec((2, C, DT), lambda c: (0, nc - 1 - c, 0))],
        out_specs=[pl.BlockSpec((C, INW), lambda c: (nc - 1 - c, 0)), pl.BlockSpec((8, HW), lambda c: (0, 0)),
                   pl.BlockSpec((2, HW), lambda c: (0, 0))],
        out_shape=[jax.ShapeDtypeStruct((S, INW), BF16), jax.ShapeDtypeStruct((8, HW), F32), jax.ShapeDtypeStruct((2, HW), F32)],
        scratch_shapes=[pltpu.VMEM((H, dk, dk), F32), pltpu.VMEM((1, HW), F32), pltpu.VMEM((C, HW), F32), pltpu.VMEM((C, HW), F32),
                        pltpu.VMEM((C, HW), F32), pltpu.VMEM((1, HW), F32)],
        compiler_params=_cparams(("arbitrary",)))(z, z, z, z, o, states, states, dyb, hg_lb, ng, dz_head, dz_tail)


def _position():
    x, y, c = lax.axis_index("x"), lax.axis_index("y"), lax.axis_index("c")
    return x, y, c, 4 * x + 2 * y + c


def _flip(x, y, c, k):
    return (1 - x if k & 4 else x, 1 - y if k & 2 else y, 1 - c if k & 1 else c)


def _allgather_small(name, v):
    R, L = v.shape

    def body(v_ref, out_ref, send_sems, recv_sems):
        x, y, c, me = _position()
        out_ref[me] = v_ref[...]
        copies = []
        for k in range(1, N_DEV):
            cp = pltpu.make_async_remote_copy(src_ref=v_ref, dst_ref=out_ref.at[me], send_sem=send_sems.at[k - 1],
                                              recv_sem=recv_sems.at[k - 1], device_id=_flip(x, y, c, k), device_id_type=MESH)
            cp.start()
            copies.append(cp)
        for cp in copies:
            cp.wait()

    return pl.pallas_call(
        body, name=name, out_shape=jax.ShapeDtypeStruct((N_DEV, R, L), v.dtype),
        in_specs=[pl.BlockSpec(memory_space=pltpu.VMEM)], out_specs=pl.BlockSpec(memory_space=pltpu.VMEM),
        scratch_shapes=[pltpu.SemaphoreType.DMA((N_DEV - 1,)), pltpu.SemaphoreType.DMA((N_DEV - 1,))],
        compiler_params=pltpu.CompilerParams(vmem_limit_bytes=VMEM_LIMIT),
    )(v)


def _allgather_hbm(name, shards):
    n = len(shards)

    def body(*refs):
        ins, outs = refs[:n], refs[n:2 * n]
        send_sems, recv_sems, local_sems = refs[2 * n:]
        x, y, c, me = _position()
        sibling = (x, y, 1 - c)
        chips = [(1 - x, y), (x, 1 - y), (1 - x, 1 - y)]

        def slot(px, py, pc):
            return 4 * px + 2 * py + pc

        def copy(w, k, block, to, src=None):
            dst = outs[w].at[slot(*block)]
            return pltpu.make_async_remote_copy(src_ref=dst if src is None else src, dst_ref=dst, send_sem=send_sems.at[w, k],
                                                recv_sem=recv_sems.at[w, k], device_id=to, device_id_type=MESH)

        mine, first, passed = [], [], []
        for w in range(n):
            cp = pltpu.make_async_copy(ins[w], outs[w].at[me], local_sems.at[w])
            cp.start()
            mine.append(cp)
            for j, chip in enumerate(chips):
                first.append(copy(w, 1 + j, (x, y, c), (*chip, c), src=ins[w]))
            first.append(copy(w, 0, (x, y, c), sibling, src=ins[w]))
        for cp in first:
            cp.start()
        for w in range(n):
            for j, chip in enumerate(chips):
                copy(w, 1 + j, (*chip, c), (x, y, c)).wait_recv()
                cp = copy(w, 4 + j, (*chip, c), sibling)
                cp.start()
                passed.append(cp)
        for w in range(n):
            copy(w, 0, sibling, (x, y, c)).wait_recv()
            for j, chip in enumerate(chips):
                copy(w, 4 + j, (*chip, 1 - c), (x, y, c)).wait_recv()
        for cp in first + passed:
            cp.wait_send()
        for cp in mine:
            cp.wait()

    hbm = pl.BlockSpec(memory_space=pltpu.HBM)
    return pl.pallas_call(
        body, name=name, out_shape=[jax.ShapeDtypeStruct((N_DEV, *s.shape), s.dtype) for s in shards],
        in_specs=[hbm] * n, out_specs=[hbm] * n,
        scratch_shapes=[pltpu.SemaphoreType.DMA((n, 7)), pltpu.SemaphoreType.DMA((n, 7)), pltpu.SemaphoreType.DMA((n,))],
    )(*shards)


_HBM = pl.BlockSpec(memory_space=pltpu.HBM)
_SEM = pl.BlockSpec(memory_space=pltpu.SEMAPHORE)
_EFFECT = pltpu.SideEffectType.DATAFLOW_SIDE_EFFECTING


def _split_start(name, bufs, n_sems, copies_fn, after=None):
    nb = len(bufs)
    extra = [] if after is None else [after]
    k = nb + len(extra)

    def body(*refs):
        for cp in copies_fn(refs[:nb], refs[k], refs[k + 1]):
            cp.start()
        refs[-1][...] = jnp.zeros_like(refs[-1])

    sems = pltpu.SemaphoreType.DMA((n_sems,))
    res = pl.pallas_call(
        body, name=name,
        out_shape=(sems, sems, *[pltpu.HBM(a.shape, a.dtype) for a in bufs], jax.ShapeDtypeStruct((8, LANES), F32)),
        in_specs=[_HBM] * nb + [pl.BlockSpec(memory_space=pl.ANY)] * len(extra),
        out_specs=(_SEM, _SEM, *[_HBM] * nb, pl.BlockSpec(memory_space=pltpu.VMEM)),
        input_output_aliases={i: 2 + i for i in range(nb)},
        compiler_params=pltpu.CompilerParams(has_side_effects=_EFFECT),
    )(*[pltpu.with_memory_space_constraint(a, pltpu.HBM) for a in bufs], *extra)
    return res[0], res[1], list(res[2:2 + nb]), res[-1]


def _split_wait(name, bufs, send_sems, recv_sems, after, copies_fn):
    nb = len(bufs)

    def body(*refs):
        for cp in copies_fn(refs[:nb], refs[nb], refs[nb + 1]):
            cp.wait_send()
            cp.wait_recv()

    res = pl.pallas_call(
        body, name=name, out_shape=tuple(pltpu.HBM(a.shape, a.dtype) for a in bufs),
        in_specs=[_HBM] * nb + [_SEM, _SEM, pl.BlockSpec(memory_space=pl.ANY)], out_specs=tuple([_HBM] * nb),
        input_output_aliases={i: i for i in range(nb)},
        compiler_params=pltpu.CompilerParams(has_side_effects=_EFFECT),
    )(*bufs, send_sems, recv_sems, after)
    return list(res)


def _split_relay(name, bufs, send_sems, recv_sems, after, wait_fn, n_sems, start_fn):
    nb = len(bufs)

    def body(*refs):
        for cp in wait_fn(refs[:nb], refs[nb], refs[nb + 1]):
            cp.wait_send()
            cp.wait_recv()
        for cp in start_fn(refs[:nb], refs[nb + 3], refs[nb + 4]):
            cp.start()
        refs[-1][...] = jnp.zeros_like(refs[-1])

    sems = pltpu.SemaphoreType.DMA((n_sems,))
    res = pl.pallas_call(
        body, name=name, out_shape=(sems, sems, *[pltpu.HBM(a.shape, a.dtype) for a in bufs], jax.ShapeDtypeStruct((8, LANES), F32)),
        in_specs=[_HBM] * nb + [_SEM, _SEM, pl.BlockSpec(memory_space=pl.ANY)],
        out_specs=(_SEM, _SEM, *[_HBM] * nb, pl.BlockSpec(memory_space=pltpu.VMEM)),
        input_output_aliases={i: 2 + i for i in range(nb)},
        compiler_params=pltpu.CompilerParams(has_side_effects=_EFFECT),
    )(*bufs, send_sems, recv_sems, after)
    return res[0], res[1], list(res[2:2 + nb]), res[-1]


N_CHIP = 4


def _chip_flip(x, y, k):
    return (1 - x if k & 2 else x), (1 - y if k & 1 else y)


def _gather_first_copies(n):
    def copies(bufs, send_sems, recv_sems):
        x, y, c, me = _position()
        out = []
        for w in range(n):
            for k in range(N_CHIP):
                to = (x, y, 1 - c) if k == 0 else (*_chip_flip(x, y, k), c)
                out.append(pltpu.make_async_remote_copy(
                    src_ref=bufs[w], dst_ref=bufs[n + w].at[me], send_sem=send_sems.at[w * N_CHIP + k],
                    recv_sem=recv_sems.at[w * N_CHIP + k], device_id=to, device_id_type=MESH))
        return out
    return copies


def _gather_relay_copies(n):
    def copies(bufs, send_sems, recv_sems):
        x, y, c, _ = _position()
        out = []
        for w in range(n):
            for k in range(1, N_CHIP):
                px, py = _chip_flip(x, y, k)
                blk = bufs[n + w].at[4 * px + 2 * py + c]
                out.append(pltpu.make_async_remote_copy(
                    src_ref=blk, dst_ref=blk, send_sem=send_sems.at[w * (N_CHIP - 1) + k - 1],
                    recv_sem=recv_sems.at[w * (N_CHIP - 1) + k - 1], device_id=(x, y, 1 - c), device_id_type=MESH))
        return out
    return copies


def _xor(a, b):
    return a + b - 2 * a * b


def _forward_first_copies(n):
    def copies(bufs, send_sems, recv_sems):
        x, y, c, me = _position()
        out = []
        for w in range(n):
            for k, to in enumerate([(x, y, 1 - c), (1 - x, y, c), (x, 1 - y, c)]):
                out.append(pltpu.make_async_remote_copy(
                    src_ref=bufs[w], dst_ref=bufs[n + w].at[me], send_sem=send_sems.at[w * 3 + k],
                    recv_sem=recv_sems.at[w * 3 + k], device_id=to, device_id_type=MESH))
        return out
    return copies


def _forward_second_copies(n):
    def copies(bufs, send_sems, recv_sems):
        x, y, c, _ = _position()
        out = []
        for w in range(n):
            half = bufs[n + w].shape[1] // 2
            for k, (src_chip, rows, to) in enumerate([((1 - x, y), pl.ds(0, half), (x, 1 - y, c)), ((x, 1 - y), pl.ds(half, half), (1 - x, y, c))]):
                blk = bufs[n + w].at[4 * src_chip[0] + 2 * src_chip[1] + c, rows]
                out.append(pltpu.make_async_remote_copy(src_ref=blk, dst_ref=blk, send_sem=send_sems.at[w * 4 + k],
                                                        recv_sem=recv_sems.at[w * 4 + k], device_id=to, device_id_type=MESH))
            for k, (px, py) in enumerate([(1 - x, y), (x, 1 - y)]):
                blk = bufs[n + w].at[4 * px + 2 * py + c]
                out.append(pltpu.make_async_remote_copy(src_ref=blk, dst_ref=blk, send_sem=send_sems.at[w * 4 + 2 + k],
                                                        recv_sem=recv_sems.at[w * 4 + 2 + k], device_id=(x, y, 1 - c), device_id_type=MESH))
        return out
    return copies


def _forward_third_copies(n):
    def copies(bufs, send_sems, recv_sems):
        x, y, c, _ = _position()
        out = []
        for w in range(n):
            blk = bufs[n + w].at[4 * (1 - x) + 2 * (1 - y) + c]
            out.append(pltpu.make_async_remote_copy(src_ref=blk, dst_ref=blk, send_sem=send_sems.at[w], recv_sem=recv_sems.at[w],
                                                    device_id=(x, y, 1 - c), device_id_type=MESH))
        return out
    return copies


def _to_sibling_copies(n):
    def copies(bufs, send_sems, recv_sems):
        x, y, c, _ = _position()
        out = []
        for w in range(n):
            for q in range(N_CHIP):
                out.append(pltpu.make_async_remote_copy(
                    src_ref=bufs[w].at[2 * q + 1 - c], dst_ref=bufs[n + w].at[q], send_sem=send_sems.at[w * N_CHIP + q],
                    recv_sem=recv_sems.at[w * N_CHIP + q], device_id=(x, y, 1 - c), device_id_type=MESH))
        return out
    return copies


def _to_owner_copies(n):
    def copies(bufs, send_sems, recv_sems):
        x, y, c, _ = _position()
        out = []
        for w in range(n):
            for k in range(1, N_CHIP):
                px, py = (1 - x if k & 2 else x), (1 - y if k & 1 else y)
                out.append(pltpu.make_async_remote_copy(
                    src_ref=bufs[w].at[2 * px + py], dst_ref=bufs[n + w].at[k - 1], send_sem=send_sems.at[w * (N_CHIP - 1) + k - 1],
                    recv_sem=recv_sems.at[w * (N_CHIP - 1) + k - 1], device_id=(px, py, c), device_id_type=MESH))
        return out
    return copies


def _chip_sum(name, stack, landed, c_idx):
    _, R, C = stack.shape
    tr = _tile(R, max(16, 1048576 // C), 16)

    def body(c_ref, a_ref, b_ref, o_ref):
        o_ref[...] = (a_ref[...].astype(F32) + b_ref[...].astype(F32)).astype(o_ref.dtype)

    return pl.pallas_call(
        body, name=name,
        grid_spec=pltpu.PrefetchScalarGridSpec(
            num_scalar_prefetch=1, grid=(N_CHIP, R // tr),
            in_specs=[pl.BlockSpec((None, tr, C), lambda q, i, c_ref: (2 * q + c_ref[0], i, 0)),
                      pl.BlockSpec((None, tr, C), lambda q, i, c_ref: (q, i, 0))],
            out_specs=pl.BlockSpec((None, tr, C), lambda q, i, c_ref: (q, i, 0))),
        out_shape=jax.ShapeDtypeStruct((N_CHIP, R, C), stack.dtype),
        compiler_params=_cparams(("parallel", "parallel")))(c_idx, stack, landed)


def _ada_mod(c16, w):
    _, D = c16.shape
    n = w.shape[1]
    tk = _tile(D, 512)
    nk = D // tk

    def body(c_ref, w_ref, o_ref, ca_ref):
        @pl.when(pl.program_id(0) == 0)
        def _():
            o_ref[...] = jnp.zeros_like(o_ref)

        cv = c_ref[...]
        ca = cv * _sigmoid(cv)
        ca_ref[...] = ca
        o_ref[...] += _dot(ca, w_ref[...])

    return pl.pallas_call(
        body, name="ada_mod", grid=(nk,),
        in_specs=[pl.BlockSpec((16, tk), lambda k: (0, k)), pl.BlockSpec((tk, n), lambda k: (k, 0))],
        out_specs=[pl.BlockSpec((16, n), lambda k: (0, 0)), pl.BlockSpec((16, tk), lambda k: (0, k))],
        out_shape=[jax.ShapeDtypeStruct((16, n), F32), jax.ShapeDtypeStruct((16, D), F32)],
        compiler_params=_cparams(("arbitrary",)))(c16, w)


def _adam_math(w, g, m, v):
    m2 = ADAM_B1 * m + (1.0 - ADAM_B1) * g
    v2 = ADAM_B2 * v + (1.0 - ADAM_B2) * (g * g)
    m_hat = m2 / (1.0 - ADAM_B1 ** ADAM_STEP)
    v_hat = v2 / (1.0 - ADAM_B2 ** ADAM_STEP)
    delta = -ADAM_LR * (m_hat / (jnp.sqrt(v_hat) + ADAM_EPS) + ADAM_WD * w)
    return delta, m2, v2


def _adamw(name, w, m, v, parts, row0=0, into=None):
    R, C = w.shape
    Rp = parts[0].shape[1]
    tr = _tile(Rp, max(16, 393216 // C), 16)
    off = row0 // tr
    n_p = len(parts)
    held = [] if into is None else list(into)

    def body(*refs):
        w_ref, m_ref, v_ref = refs[:3]
        g_ref, d_ref, m2_ref, v2_ref = refs[3 + n_p + len(held):]
        g = None
        for p_ref in refs[3:3 + n_p]:
            for s in range(p_ref.shape[0]):
                t = p_ref[s].astype(F32)
                g = t if g is None else g + t
        delta, m2, v2 = _adam_math(w_ref[...], g, m_ref[...], v_ref[...])
        g_ref[...] = g
        d_ref[...] = delta
        m2_ref[...] = m2
        v2_ref[...] = v2

    blk = pl.BlockSpec((tr, C), lambda i: (i + off, 0))
    out = jax.ShapeDtypeStruct((R, C), F32)
    return pl.pallas_call(
        body, name=name, grid=(Rp // tr,),
        in_specs=[blk, blk, blk] + [pl.BlockSpec((a.shape[0], tr, C), lambda i: (0, i, 0)) for a in parts]
        + [pl.BlockSpec(memory_space=pl.ANY)] * len(held),
        out_specs=[blk] * 4, out_shape=[out] * 4, input_output_aliases={3 + n_p + i: i for i in range(len(held))},
        compiler_params=_cparams(("parallel",)))(w, m, v, *parts, *held)


def _small_update(gathered, w, m, v, after):
    _, R, L = gathered.shape
    rs = w.shape[0]

    def body(p_ref, w_ref, m_ref, v_ref, after_ref, g_ref, d_ref, m2_ref, v2_ref):
        g = p_ref[0]
        for p in range(1, N_DEV):
            g = g + p_ref[p]
        g_ref[...] = g
        delta, m2, v2 = _adam_math(w_ref[...], g[0:rs, :], m_ref[...], v_ref[...])
        d_ref[...] = delta
        m2_ref[...] = m2
        v2_ref[...] = v2

    vm = pl.BlockSpec(memory_space=pltpu.VMEM)
    sm = jax.ShapeDtypeStruct((rs, L), F32)
    return pl.pallas_call(body, name="small_update", in_specs=[vm] * 4 + [pl.BlockSpec(memory_space=pl.ANY)], out_specs=[vm] * 4,
                          out_shape=[jax.ShapeDtypeStruct((R, L), F32), sm, sm, sm],
                          compiler_params=pltpu.CompilerParams(vmem_limit_bytes=VMEM_LIMIT))(gathered, w, m, v, after)


class _Fetched(dict):
    def __init__(self, fetch):
        super().__init__()
        self.fetch = fetch

    def first(self, key, after):
        self[key] = self.fetch(key, after)
        return self[key]


def _local_step(x, tgt, mod, p, fetch, F, scatter=None):
    S, D = x.shape
    GW, HW = p["ln_g"].shape[1], p["hg_ng"].shape[1]
    G, T, _ = p["ws"].shape
    w = _Fetched(fetch)
    INW = 2 * GW + 4 * HW + 2 * D
    in_loc, br_loc, fi_loc = INW // N_DEV, D // N_DEV, 2 * F // N_DEV
    assert GW == HW and F % fi_loc == 0
    sh1, sc1, gt1, sh2, sc2, gt2 = (mod[:, k * D:(k + 1) * D] for k in range(6))
    bsb = jnp.broadcast_to(p["bs"][:, :, None], (G, T, GW // G))

    tm = _tile(S, 1024, 16)
    tmh = _tile(S, 512, 16)
    tn_in = _tile(in_loc, 1280)
    tn_d = _tile(D, 512)
    tn_br = _tile(br_loc, 512)
    tk_s = S
    tm_w = _tile(D, 1024)
    g_off = 2 * GW + 4 * HW

    h1 = _norm_mod("norm1", x, p["norm1_g"], sc1, sh1)
    z = _mm_nn_stacked("proj_in", h1, w.first("in", h1), tm=tm, tn=tn_in, tk=D)[0]
    ya = _gmlp_fwd(z, p["ln_g"], p["ln_b"], p["ws"], bsb, GW)
    yb, o_hg, states = _hg_fwd(z, p["hg_lb"], p["hg_ng"], HW)
    t_fi = w.first("fi_early", yb)
    pa = _mm_nn_stacked("branch_gmlp", ya, w.first("bg", z), tm=tm, tn=tn_br, tk=GW, after=t_fi)[0]

    def gates(ga_ref, gb_ref, ba_ref, bb_ref):
        return _sigmoid(ga_ref[...] + ba_ref[...]), _sigmoid(gb_ref[...] + bb_ref[...])

    def gate_specs(tn_):
        o1, o2 = g_off // tn_, (g_off + D) // tn_
        return [pl.BlockSpec((tm, tn_), lambda i, j, k: (i, o1 + j)), pl.BlockSpec((tm, tn_), lambda i, j, k: (i, o2 + j)),
                pl.BlockSpec((1, tn_), lambda i, j, k: (0, j)), pl.BlockSpec((1, tn_), lambda i, j, k: (0, D // tn_ + j))]

    def merge_ep(acc, ex, outs):
        ga, gb = gates(*ex[1:5])
        outs[0][...] = acc
        outs[1][...] = (ga * ex[0][...] + gb * acc).astype(BF16)

    tile_o = pl.BlockSpec((tm, tn_br), lambda i, j, k: (i, j))
    pb, y = _mm_nn_stacked(
        "branch_hg_merge", yb, w.first("bh", z), tm=tm, tn=tn_br, tk=HW, extras=[pa, z, z, p["b_gate"], p["b_gate"]],
        extra_specs=[tile_o, *gate_specs(tn_br)], out_shapes=[jax.ShapeDtypeStruct((S, D), F32), jax.ShapeDtypeStruct((S, D), BF16)],
        out_specs=[tile_o, tile_o], epilogue=merge_ep)

    def resid_ep(acc, ex, outs):
        outs[0][...] = acc
        outs[1][...] = ex[0][...] + ex[1][...] * acc

    def resid_mm(name, a, b, res, gt, tm_):
        K = a.shape[1]
        t_o = pl.BlockSpec((tm_, tn_d), lambda i, j, k: (i, j))
        return _matmul(
            name, a, b, dims=_NN, grid_mnk=(S // tm_, D // tn_d, 1), tiles=(tm_, tn_d),
            a_spec=pl.BlockSpec((tm_, K), lambda i, j, k: (i, 0)), b_spec=pl.BlockSpec((K, tn_d), lambda i, j, k: (0, j)),
            extras=[res, gt], extra_specs=[t_o, pl.BlockSpec((1, tn_d), lambda i, j, k: (0, j))],
            out_shapes=[jax.ShapeDtypeStruct((S, D), F32)] * 2, out_specs=[t_o, t_o], epilogue=resid_ep)

    o1, xm = resid_mm("proj_out", y, w.first("out", z), x, gt1, tm)
    h2 = _norm_mod("norm2", xm, p["norm2_g"], sc2, sh2)
    hf, hf_fac = _ffn_in_swiglu(h2, w.first("fi", h2))
    o2, x3 = resid_mm("ffn_out", hf, w.first("fo", hf), xm, gt2, tmh)
    dx3, do2, vec_l = _loss_head(x3, tgt, p["final_g"], o2, gt2)

    nf = F // fi_loc

    def dswiglu_ep(acc, ex, outs):
        outs[0][0] = (acc * ex[0][0].astype(F32)).astype(BF16)
        outs[0][1] = (acc * ex[0][1].astype(F32)).astype(BF16)

    pair = pl.BlockSpec((2, tmh, fi_loc), lambda i, j, k: (0, i, j))
    dab = _matmul(
        "ffn_out_dx", do2, w["fo"], dims=_NT, grid_mnk=(S // tmh, nf, 1), tiles=(tmh, fi_loc),
        a_spec=pl.BlockSpec((tmh, D), lambda i, j, k: (i, 0)), b_spec=pl.BlockSpec((fi_loc, D), lambda i, j, k: (j, 0)),
        extras=[hf_fac], extra_specs=[pair], out_shapes=[jax.ShapeDtypeStruct((2, S, F), BF16)], out_specs=[pair],
        epilogue=dswiglu_ep)[0]
    start = (lambda name, grads: scatter[0](name, grads)) if scatter is not None else (lambda name, grads: None)
    push = (lambda name, after: scatter[1](name, after)) if scatter is not None else (lambda name, after: None)

    def zero(token):
        return 0.0 if token is None else token[0:1, 0:1]

    tm_f = _tile(F, 512)
    g_fo = _mm_tn("ffn_out_dw", hf, do2, pl.BlockSpec((tk_s, D), lambda i, j, k: (k, j)), Mo=F, No=D, S=S, tm=tm_f, tn=D, tk=tk_s)
    g_fi = _mm_tn("ffn_in_dw", h2, dab, pl.BlockSpec((None, tk_s, fi_loc), lambda i, j, k: (j // nf, k, j % nf)),
                  Mo=D, No=2 * F, S=S, tm=tm_w, tn=fi_loc, tk=tk_s, stacked_nloc=fi_loc, after=g_fo)
    t_ffn = start("scatter_ffn", dict(fo=g_fo, fi=g_fi))
    dh2 = _mm_nt_stacked("ffn_in_dx", pl.BlockSpec((None, tmh, fi_loc), lambda i, j, k: (k // nf, i, k % nf)), dab, w["fi"],
                         M=S, tm=tmh, tn=D, tk=fi_loc, after=t_ffn)
    dxm, vec2, do1 = _norm_mod_bwd("norm2_bwd", dh2, xm, p["norm2_g"], sc2, dx3, o1, gt1)
    t_ffn = push("scatter_ffn", dxm)

    def dmerge_ep(acc, ex, outs):
        ga, gb = gates(*ex[2:6])
        outs[0][...] = (acc * ga).astype(BF16)
        outs[1][...] = (acc * gb).astype(BF16)
        outs[2][0] = (acc * ex[0][...] * ga * (1.0 - ga)).astype(BF16)
        outs[2][1] = (acc * ex[1][...] * gb * (1.0 - gb)).astype(BF16)

    t_o = pl.BlockSpec((tm, tn_d), lambda i, j, k: (i, j))
    dpa, dpb, dg2 = _matmul(
        "proj_out_dx", do1, w["out"], dims=_NT, grid_mnk=(S // tm, D // tn_d, 1), tiles=(tm, tn_d),
        a_spec=pl.BlockSpec((tm, D), lambda i, j, k: (i, 0)), b_spec=pl.BlockSpec((tn_d, D), lambda i, j, k: (j, 0)),
        extras=[pa, pb, z, z, p["b_gate"], p["b_gate"]], extra_specs=[t_o, t_o, *gate_specs(tn_d)],
        out_shapes=[jax.ShapeDtypeStruct((S, D), BF16), jax.ShapeDtypeStruct((S, D), BF16), jax.ShapeDtypeStruct((2, S, D), BF16)],
        out_specs=[t_o, t_o, pl.BlockSpec((2, tm, tn_d), lambda i, j, k: (0, i, j))], epilogue=dmerge_ep, after=t_ffn)
    g_out = _mm_tn("proj_out_dw", y, do1, pl.BlockSpec((tk_s, D), lambda i, j, k: (k, j)), Mo=D, No=D, S=S, tm=tn_d, tn=D, tk=tk_s)
    tn_g = _tile(GW, 512)
    b_br = pl.BlockSpec((tk_s, br_loc), lambda i, j, k: (k, j))
    g_bg = _mm_tn("branch_gmlp_dw", ya, dpa, b_br, Mo=GW, No=D, S=S, tm=tn_g, tn=br_loc, tk=tk_s, stacked_nloc=br_loc)
    g_bh = _mm_tn("branch_hg_dw", yb, dpb, b_br, Mo=HW, No=D, S=S, tm=tn_g, tn=br_loc, tk=tk_s, stacked_nloc=br_loc)
    t_mix = start("scatter_mixer", dict(out=g_out, bg=g_bg, bh=g_bh))
    def branch_dx(name, dp, wg):
        flat = jnp.swapaxes(wg, 0, 1).reshape(wg.shape[1], D)
        return _matmul(
            name, dp, flat, dims=_NT, grid_mnk=(S // tm, GW // tn_g, 1), tiles=(tm, tn_g),
            a_spec=pl.BlockSpec((tm, D), lambda i, j, k: (i, 0)), b_spec=pl.BlockSpec((tn_g, D), lambda i, j, k: (j, 0)),
            out_shapes=[jax.ShapeDtypeStruct((S, GW), F32)], out_specs=[pl.BlockSpec((tm, tn_g), lambda i, j, k: (i, j))],
            epilogue=_store(F32), after=t_mix)[0]

    dya = branch_dx("branch_gmlp_dx", dpa, w["bg"])
    dyb = branch_dx("branch_hg_dx", dpb, w["bh"])
    db_gate = _colsum2(dg2)
    dz_gmlp, dln, dws, dbs = _gmlp_bwd(z, dya, p["ln_g"], p["ln_b"], p["ws"], bsb, GW)
    t_mix = push("scatter_mixer", dz_gmlp)
    dz, dng, dhlb = _hg_bwd(z, o_hg, states, dyb, p["hg_lb"], p["hg_ng"] + zero(t_mix), HW, dz_gmlp, dg2)
    half = D // 2
    tm_h = _tile(half, 1024)
    g_in = []
    t_in = None
    for hname, h in (("a", 0), ("b", 1)):
        g_in.append(_mm_tn("proj_in_dw_" + hname, h1, dz, pl.BlockSpec((tk_s, in_loc), lambda i, j, k: (k, j)), Mo=half, No=INW, S=S,
                           tm=tm_h, tn=in_loc, tk=tk_s, stacked_nloc=in_loc, after=t_in, a_off=h * (half // tm_h)))
        t_in = start("scatter_proj_in_" + hname, {"w_in_" + hname: g_in[-1]})
    t_in = push("scatter_proj_in_a", t_in)
    dh1 = _mm_nt_stacked("proj_in_dx", pl.BlockSpec((tmh, in_loc), lambda i, j, k: (i, k)), dz, w["in"], M=S, tm=tmh, tn=D, tk=in_loc,
                         after=t_in)
    dx, vec1 = _norm_mod_bwd("norm1_bwd", dh1, x, p["norm1_g"], sc1, dxm)

    dmod = jnp.concatenate([vec1[0:1], vec1[1:2], vec2[3:4], vec2[0:1], vec2[1:2], vec_l[2:3]], axis=1)
    small = dict(norm1_g=vec1[2:3], b_gate=db_gate.reshape(1, 2 * D), ln_g=dln[0:1], ln_b=dln[1:2], ws=dws, bs=dbs.reshape(G, T),
                 hg_lb=dhlb, hg_ng=dng[0:1], norm2_g=vec2[2:3], final_g=vec_l[1:2], loss=vec_l[0:1, 0:LANES])
    big = dict(w_in_a=g_in[0], w_in_b=g_in[1], bg=g_bg, bh=g_bh, out=g_out, fi=g_fi, fo=g_fo)
    return dx, big, small, dmod


_SMALL = ("b_ada", "norm1_g", "b_gate", "ln_g", "ln_b", "ws", "bs", "hg_lb", "hg_ng", "norm2_g", "final_g")


def _pack(parts, rows_mult=8):
    flat = [a.reshape(-1) for a in parts]
    offs, n = [], 0
    for a in flat:
        offs.append(n)
        n += a.shape[0]
    pad = (-n) % (LANES * rows_mult)
    if pad:
        flat.append(jnp.zeros((pad,), F32))
    return jnp.concatenate(flat).reshape(-1, LANES), offs


def kernel(x, c, w_ada, b_ada, norm1_g, w_in, b_gate, gmlp_ln_g, gmlp_ln_b, gmlp_ws, gmlp_bs, hg_lb, hg_norm_g, w_branch_gmlp, w_branch_hg, w_out, norm2_g, w_ffn_in, w_ffn_out, final_norm_g, loss_target, m_w_ada, m_b_ada, m_norm1_g, m_w_in, m_b_gate, m_gmlp_ln_g, m_gmlp_ln_b, m_gmlp_ws, m_gmlp_bs, m_hg_lb, m_hg_norm_g, m_w_branch_gmlp, m_w_branch_hg, m_w_out, m_norm2_g, m_w_ffn_in, m_w_ffn_out, m_final_norm_g, v_w_ada, v_b_ada, v_norm1_g, v_w_in, v_b_gate, v_gmlp_ln_g, v_gmlp_ln_b, v_gmlp_ws, v_gmlp_bs, v_hg_lb, v_hg_norm_g, v_w_branch_gmlp, v_w_branch_hg, v_w_out, v_norm2_g, v_w_ffn_in, v_w_ffn_out, v_final_norm_g):
    S, D = x.shape[1], x.shape[2]
    ada_loc = w_ada.shape[2]
    me = 4 * lax.axis_index("x") + 2 * lax.axis_index("y") + lax.axis_index("c")

    c_all = _allgather_small("gather_c", c.reshape(D // LANES, LANES)).reshape(N_DEV, D)
    mod_cols, c_act = _ada_mod(jnp.pad(c_all, ((0, 16 - N_DEV), (0, 0))), w_ada[0])
    mod_all = _allgather_small("gather_mod", mod_cols[:N_DEV].reshape(-1, LANES)).reshape(N_DEV, N_DEV, ada_loc)
    mod = lax.dynamic_index_in_dim(mod_all, me, axis=1, keepdims=False).reshape(1, N_DEV * ada_loc) + b_ada

    def empty_hbm(shape, dtype):
        return pltpu.with_memory_space_constraint(lax.empty(shape, dtype), pltpu.HBM)

    groups = dict(gather_in=dict(keys=["in"], src=[w_in], forward=True),
                  gather_mixer=dict(keys=["bg", "bh", "out"], src=[w_branch_gmlp, w_branch_hg, w_out], forward=False),
                  gather_ffn_in=dict(keys=["fi"], src=[w_ffn_in], forward=True),
                  gather_ffn_out=dict(keys=["fo"], src=[w_ffn_out], forward=False))
    group_of = {k: gname for gname, g in groups.items() for k in g["keys"]}

    def first_hop(gname, after):
        g = groups[gname]
        n = len(g["keys"])
        shards = [a[0].astype(BF16) for a in g["src"]]
        outs = [lax.dynamic_update_slice(lax.empty((N_DEV, *s.shape), BF16), s[None], (me, 0, 0)) for s in shards]
        if g["forward"]:
            *g["hop"], token = _split_start(gname + "_hop1", shards + outs, n * 3, _forward_first_copies(n), after=after)
        else:
            *g["hop"], token = _split_start(gname + "_hop1", shards + outs, n * N_CHIP, _gather_first_copies(n), after=after)
        return token

    def second_hop(gname, after):
        g = groups[gname]
        n = len(g["keys"])
        *g["hop"], token = _split_relay(gname + "_hop2", g["hop"][2], g["hop"][0], g["hop"][1], after,
                                        _forward_first_copies(n), n * 4, _forward_second_copies(n))
        return token

    def finish(gname, after):
        g = groups[gname]
        n = len(g["keys"])
        send_sems, recv_sems, bufs = g["hop"]
        if g["forward"]:
            send_sems, recv_sems, bufs, _ = _split_relay(gname + "_hop3", bufs, send_sems, recv_sems, after,
                                                         _forward_second_copies(n), n, _forward_third_copies(n))
            bufs = _split_wait(gname + "_wait", bufs, send_sems, recv_sems, after, _forward_third_copies(n))
        else:
            send_sems, recv_sems, bufs, _ = _split_relay(gname + "_relay", bufs, send_sems, recv_sems, after,
                                                         _gather_first_copies(n), n * (N_CHIP - 1), _gather_relay_copies(n))
            bufs = _split_wait(gname + "_wait", bufs, send_sems, recv_sems, after, _gather_relay_copies(n))
        g["done"] = dict(zip(g["keys"], bufs[n:]))

    token = first_hop("gather_in", mod_all)
    mod = mod + token[0:1, 0:1]

    def fetch(key, after):
        if key == "in":
            t = second_hop("gather_in", after)
            t = first_hop("gather_mixer", t)
            t = first_hop("gather_ffn_in", t)
            finish("gather_in", t)
        elif key == "fi_early":
            return first_hop("gather_ffn_out", second_hop("gather_ffn_in", after))
        elif "done" not in groups[group_of[key]]:
            finish(group_of[key], after)
        arr = groups[group_of[key]]["done"][key]
        return arr.reshape(-1, D) if key in ("out", "fo") else arr

    p = dict(norm1_g=norm1_g, b_gate=b_gate, ln_g=gmlp_ln_g, ln_b=gmlp_ln_b, ws=gmlp_ws[0], bs=gmlp_bs[0], hg_lb=hg_lb,
             hg_ng=hg_norm_g, norm2_g=norm2_g, final_g=final_norm_g.reshape(1, D))

    in_flight = {}
    c_idx = lax.axis_index("c").astype(jnp.int32).reshape(1)
    my_chip = 2 * lax.axis_index("x") + lax.axis_index("y")

    def scatter_start(name, grads):
        keys = list(grads)
        n = len(keys)
        stacks = [grads[k].reshape(N_DEV, -1, grads[k].shape[-1]) for k in keys]
        lands = [empty_hbm((N_CHIP, *g.shape[1:]), g.dtype) for g in stacks]
        send_sems, recv_sems, bufs, token = _split_start(name + "_d2d", stacks + lands, n * N_CHIP, _to_sibling_copies(n))
        in_flight[name] = dict(keys=keys, stage1=(send_sems, recv_sems, bufs))
        return token

    def scatter_push(name, after):
        f = in_flight[name]
        n = len(f["keys"])
        send_sems, recv_sems, bufs = f["stage1"]
        bufs = _split_wait(name + "_d2d_wait", bufs, send_sems, recv_sems, after, _to_sibling_copies(n))
        sums = [_chip_sum(f"{name}_sum_{k}", bufs[i], bufs[n + i], c_idx) for i, k in enumerate(f["keys"])]
        lands = [empty_hbm((N_CHIP - 1, *s.shape[1:]), s.dtype) for s in sums]
        send_sems, recv_sems, bufs, token = _split_start(name + "_ici", sums + lands, n * (N_CHIP - 1), _to_owner_copies(n))
        f["stage2"] = (send_sems, recv_sems, bufs)
        return token

    grad_x, _, small, dmod = _local_step(x[0], loss_target[0], mod, p, fetch, w_ffn_out.shape[1] * N_DEV, (scatter_start, scatter_push))

    small["b_ada"] = dmod
    packed, offs = _pack([small[k] for k in _SMALL] + [small["loss"]])
    gathered = _allgather_small("gather_small", packed)
    wp = dict(p, b_ada=b_ada)
    ms = dict(b_ada=m_b_ada, norm1_g=m_norm1_g, b_gate=m_b_gate, ln_g=m_gmlp_ln_g, ln_b=m_gmlp_ln_b, ws=m_gmlp_ws, bs=m_gmlp_bs,
              hg_lb=m_hg_lb, hg_ng=m_hg_norm_g, norm2_g=m_norm2_g, final_g=m_final_norm_g)
    vs = dict(b_ada=v_b_ada, norm1_g=v_norm1_g, b_gate=v_b_gate, ln_g=v_gmlp_ln_g, ln_b=v_gmlp_ln_b, ws=v_gmlp_ws, bs=v_gmlp_bs,
              hg_lb=v_hg_lb, hg_ng=v_hg_norm_g, norm2_g=v_norm2_g, final_g=v_final_norm_g)
    w_sm, _ = _pack([wp[k] for k in _SMALL])
    m_sm, _ = _pack([ms[k] for k in _SMALL])
    v_sm, _ = _pack([vs[k] for k in _SMALL])
    t_tail = scatter_push("scatter_proj_in_b", gathered)
    sm_out = _small_update(gathered, w_sm, m_sm, v_sm, t_tail)
    shapes = dict(b_ada=b_ada.shape, norm1_g=norm1_g.shape, b_gate=b_gate.shape, ln_g=gmlp_ln_g.shape, ln_b=gmlp_ln_b.shape,
                  ws=gmlp_ws.shape, bs=gmlp_bs.shape, hg_lb=hg_lb.shape, hg_ng=hg_norm_g.shape, norm2_g=norm2_g.shape,
                  final_g=final_norm_g.shape)

    def unpack(arr, k):
        i = _SMALL.index(k)
        n = math.prod(shapes[k])
        return arr.reshape(-1)[offs[i]:offs[i] + n].reshape(shapes[k])

    loss = sm_out[0].reshape(-1)[offs[len(_SMALL)]]

    dmod_all = gathered.reshape(N_DEV, -1)[:, offs[0]:offs[0] + N_DEV * ada_loc]
    dmod_loc = lax.dynamic_slice_in_dim(dmod_all, me * ada_loc, ada_loc, axis=1)
    ca_t = jnp.pad(c_act[:N_DEV].T, ((0, 0), (0, LANES - N_DEV))).astype(BF16)
    dm_p = jnp.pad(dmod_loc, ((0, LANES - N_DEV), (0, 0))).astype(BF16)
    tm_a = _tile(D, 512)
    g_ada = _matmul(
        "ada_dw", ca_t, dm_p, dims=_NN, grid_mnk=(D // tm_a, 1, 1), tiles=(tm_a, ada_loc),
        a_spec=pl.BlockSpec((tm_a, LANES), lambda i, j, k: (i, 0)), b_spec=pl.BlockSpec((LANES, ada_loc), lambda i, j, k: (0, 0)),
        out_shapes=[jax.ShapeDtypeStruct((1, D, ada_loc), F32)], out_specs=[pl.BlockSpec((None, tm_a, ada_loc), lambda i, j, k: (0, i, 0))],
        epilogue=_store(F32), after=t_tail)[0]

    upd = {"w_ada": _adamw("adamw_w_ada", w_ada[0], m_w_ada[0], v_w_ada[0], [g_ada])}
    big_w = dict(w_in=(w_in, m_w_in, v_w_in, "w_in"), bg=(w_branch_gmlp, m_w_branch_gmlp, v_w_branch_gmlp, "w_branch_gmlp"),
                 bh=(w_branch_hg, m_w_branch_hg, v_w_branch_hg, "w_branch_hg"), out=(w_out, m_w_out, v_w_out, "w_out"),
                 fi=(w_ffn_in, m_w_ffn_in, v_w_ffn_in, "w_ffn_in"), fo=(w_ffn_out, m_w_ffn_out, v_w_ffn_out, "w_ffn_out"))
    after = upd["w_ada"][1]
    for name in ("scatter_ffn", "scatter_mixer", "scatter_proj_in_a", "scatter_proj_in_b"):
        keys = in_flight[name]["keys"]
        n = len(keys)
        send_sems, recv_sems, bufs = in_flight[name]["stage2"]
        bufs = _split_wait(name + "_ici_wait", bufs, send_sems, recv_sems, after, _to_owner_copies(n))
        for i, k in enumerate(keys):
            parts = [lax.dynamic_index_in_dim(bufs[i], my_chip, axis=0, keepdims=True), bufs[n + i]]
            if k in big_w:
                wt, mt, vt, out_name = big_w[k]
                upd[out_name] = _adamw("adamw_" + out_name, wt[0], mt[0], vt[0], parts)
            else:
                wt, mt, vt, out_name = big_w["w_in"]
                upd[out_name] = _adamw("adamw_" + k, wt[0], mt[0], vt[0], parts, row0=0 if k == "w_in_a" else parts[0].shape[1],
                                       into=upd.get(out_name))
            after = upd[out_name][1]

    order = ("w_ada", "b_ada", "norm1_g", "w_in", "b_gate", "ln_g", "ln_b", "ws", "bs", "hg_lb", "hg_ng", "w_branch_gmlp", "w_branch_hg",
             "w_out", "norm2_g", "w_ffn_in", "w_ffn_out", "final_g")
    outs = [loss, grad_x[None]]
    for idx in range(4):
        for k in order:
            outs.append(upd[k][idx][None] if k in upd else unpack(sm_out[idx], k))
    return tuple(outs)
```

```python
import functools
import math

import jax
import jax.numpy as jnp
from jax import lax
from jax.experimental import pallas as pl
from jax.experimental.pallas import tpu as pltpu

F32 = jnp.float32
BF16 = jnp.bfloat16
N_DEV = 8
EPS = 1e-6
LANES = 128
HG_DK = 128
HG_CHUNK = 64
HG_MID = HG_CHUNK // 2 - 1
EXP_CLAMP = 80.0
VMEM_LIMIT = 48 * 1024 * 1024
ADAM_LR, ADAM_B1, ADAM_B2, ADAM_EPS, ADAM_WD, ADAM_STEP = 0.001, 0.9, 0.999, 1e-08, 0.01, 10
MESH = pl.DeviceIdType.MESH

_NN = (((1,), (0,)), ((), ()))
_NT = (((1,), (1,)), ((), ()))
_TN = (((0,), (0,)), ((), ()))


def _dot(a, b, dims=_NN):
    return lax.dot_general(a.astype(BF16), b.astype(BF16), dims, preferred_element_type=F32)


def _tile(n, target, mult=LANES):
    best = None
    for t in range(mult, min(n, target) + 1, mult):
        if n % t == 0:
            best = t
    return n if best is None else best


def _cparams(sem):
    return pltpu.CompilerParams(dimension_semantics=sem, vmem_limit_bytes=VMEM_LIMIT)


def _sigmoid(x):
    return 1.0 / (1.0 + jnp.exp(-x))


def _gelu_parts(x):
    k0 = math.sqrt(2.0 / math.pi)
    x2 = x * x
    t = jnp.tanh(k0 * (x + 0.044715 * x * x2))
    g = 0.5 * x * (1.0 + t)
    dg = 0.5 * (1.0 + t) + 0.5 * x * (1.0 - t * t) * (k0 * (1.0 + 3.0 * 0.044715 * x2))
    return g, dg


def _split3(x):
    h = x.astype(BF16)
    r = x - h.astype(F32)
    m = r.astype(BF16)
    lo = (r - m.astype(F32)).astype(BF16)
    return h, m, lo


def _ones_dot(mat01, x):
    h, m, lo = _split3(x)
    d = functools.partial(lax.dot_general, dimension_numbers=_NN, preferred_element_type=F32)
    return d(mat01, h) + d(mat01, m) + d(mat01, lo)


def _matmul(name, a, b, *, dims, grid_mnk, tiles, a_spec, b_spec, extras=(), extra_specs=(), out_shapes, out_specs, epilogue, after=None,
            sem=None):
    gm, gn, nk = grid_mnk
    tm, tn = tiles
    n_ex, n_out = len(extras), len(out_shapes)
    held = [] if after is None else [after]

    def body(*refs):
        a_ref, b_ref = refs[0], refs[1]
        ex = refs[2:2 + n_ex]
        outs = refs[2 + n_ex + len(held):2 + n_ex + len(held) + n_out]
        more = () if sem is None else (pl.program_id(0) == 0,)
        if nk == 1:
            epilogue(lax.dot_general(a_ref[...], b_ref[...], dims, preferred_element_type=F32), ex, outs, *more)
            return
        acc = refs[-1]
        k = pl.program_id(2)

        @pl.when(k == 0)
        def _():
            acc[...] = jnp.zeros_like(acc)

        acc[...] += lax.dot_general(a_ref[...], b_ref[...], dims, preferred_element_type=F32)

        @pl.when(k == nk - 1)
        def _():
            epilogue(acc[...], ex, outs, *more)

    return pl.pallas_call(
        body, name=name, grid=(gm, gn, nk), in_specs=[a_spec, b_spec, *extra_specs] + [pl.BlockSpec(memory_space=pl.ANY)] * len(held),
        out_specs=list(out_specs), out_shape=list(out_shapes), scratch_shapes=[] if nk == 1 else [pltpu.VMEM((tm, tn), F32)],
        compiler_params=_cparams(sem or ("parallel", "parallel", "arbitrary")),
    )(a, b, *extras, *held)


def _store(dtype):
    def ep(acc, ex, outs):
        outs[0][...] = acc.astype(dtype)
    return ep


def _mm_nn_stacked(name, a, wg, *, tm, tn, tk, out_dtype=F32, extras=(), extra_specs=(), out_shapes=None, out_specs=None, epilogue=None,
                   after=None):
    M, K = a.shape
    _, _, nloc = wg.shape
    N = nloc * N_DEV
    q = nloc // tn
    if out_shapes is None:
        out_shapes = [jax.ShapeDtypeStruct((M, N), out_dtype)]
        out_specs = [pl.BlockSpec((tm, tn), lambda i, j, k: (i, j))]
        epilogue = _store(out_dtype)
    return _matmul(
        name, a, wg, dims=_NN, grid_mnk=(M // tm, N // tn, K // tk), tiles=(tm, tn),
        a_spec=pl.BlockSpec((tm, tk), lambda i, j, k: (i, k)),
        b_spec=pl.BlockSpec((None, tk, tn), lambda i, j, k: (j // q, k, j % q)),
        extras=extras, extra_specs=extra_specs, out_shapes=out_shapes, out_specs=out_specs, epilogue=epilogue, after=after)


def _mm_nt_stacked(name, a_spec, a, wg, *, M, tm, tn, tk, out_dtype=F32, after=None, extras=(), extra_specs=(), out_shapes=None,
                   out_specs=None, epilogue=None, sem=None):
    _, Kw, nloc = wg.shape
    q = nloc // tk
    single = out_shapes is None
    if single:
        out_shapes = [jax.ShapeDtypeStruct((M, Kw), out_dtype)]
        out_specs = [pl.BlockSpec((tm, tn), lambda i, j, k: (i, j))]
        epilogue = _store(out_dtype)
    res = _matmul(
        name, a, wg, dims=_NT, grid_mnk=(M // tm, Kw // tn, (nloc * N_DEV) // tk), tiles=(tm, tn),
        a_spec=a_spec, b_spec=pl.BlockSpec((None, tn, tk), lambda i, j, k: (k // q, j, k % q)),
        extras=extras, extra_specs=extra_specs, out_shapes=out_shapes, out_specs=out_specs, epilogue=epilogue, after=after, sem=sem)
    return res[0] if single else res


def _mm_tn(name, a, b, b_spec, *, Mo, No, S, tm, tn, tk, stacked_nloc=None, after=None, a_off=0):
    if stacked_nloc is None:
        out_shape = jax.ShapeDtypeStruct((Mo, No), BF16)
        out_spec = pl.BlockSpec((tm, tn), lambda i, j, k: (i, j))
    else:
        q = stacked_nloc // tn
        out_shape = jax.ShapeDtypeStruct((N_DEV, Mo, stacked_nloc), BF16)
        out_spec = pl.BlockSpec((None, tm, tn), lambda i, j, k: (j // q, i, j % q))
    return _matmul(
        name, a, b, dims=_TN, grid_mnk=(Mo // tm, No // tn, S // tk), tiles=(tm, tn),
        a_spec=pl.BlockSpec((tk, tm), lambda i, j, k: (k, i + a_off)), b_spec=b_spec,
        out_shapes=[out_shape], out_specs=[out_spec], epilogue=_store(BF16), after=after)[0]


def _norm_mod(name, x, g, sc, sh):
    S, D = x.shape
    tm = _tile(S, 256, 8)

    def body(x_ref, g_ref, sc_ref, sh_ref, h_ref):
        xv = x_ref[...]
        r = lax.rsqrt(jnp.mean(xv * xv, axis=-1, keepdims=True) + EPS)
        h = (xv * r) * g_ref[...]
        h_ref[...] = (h * (1.0 + sc_ref[...]) + sh_ref[...]).astype(BF16)

    row = pl.BlockSpec((tm, D), lambda i: (i, 0))
    vec = pl.BlockSpec((1, D), lambda i: (0, 0))
    return pl.pallas_call(body, name=name, grid=(S // tm,), in_specs=[row, vec, vec, vec], out_specs=row,
                          out_shape=jax.ShapeDtypeStruct((S, D), BF16), compiler_params=_cparams(("parallel",)))(x, g, sc, sh)


def _norm_mod_bwd_rows(first, dh_v, x_ref, g_ref, sc_ref, dres_ref, dx_ref, vec_ref, o_ref=None, gt_ref=None, do_ref=None):
    @pl.when(first)
    def _():
        vec_ref[...] = jnp.zeros_like(vec_ref)

    xv, gv = x_ref[...], g_ref[...]
    r = lax.rsqrt(jnp.mean(xv * xv, axis=-1, keepdims=True) + EPS)
    xn = xv * r
    one_sc = 1.0 + sc_ref[...]
    vec_ref[0:1, :] += jnp.sum(dh_v, axis=0, keepdims=True)
    vec_ref[1:2, :] += jnp.sum(dh_v * (xn * gv), axis=0, keepdims=True)
    vec_ref[2:3, :] += jnp.sum(dh_v * one_sc * xn, axis=0, keepdims=True)
    dxn = dh_v * one_sc * gv
    dx = dres_ref[...] + r * (dxn - xn * jnp.mean(dxn * xn, axis=-1, keepdims=True))
    dx_ref[...] = dx
    if o_ref is not None:
        vec_ref[3:4, :] += jnp.sum(dx * o_ref[...], axis=0, keepdims=True)
        do_ref[...] = (dx * gt_ref[...]).astype(BF16)


def _norm_mod_bwd(name, dh, x, g, sc, dres, o=None, gt=None):
    S, D = x.shape
    tm = _tile(S, 256, 8)
    gated = o is not None

    def body(*refs):
        if gated:
            dh_ref, x_ref, g_ref, sc_ref, dres_ref, o_ref, gt_ref, dx_ref, vec_ref, do_ref = refs
        else:
            dh_ref, x_ref, g_ref, sc_ref, dres_ref, dx_ref, vec_ref = refs
            o_ref = gt_ref = do_ref = None
        _norm_mod_bwd_rows(pl.program_id(0) == 0, dh_ref[...], x_ref, g_ref, sc_ref, dres_ref, dx_ref, vec_ref, o_ref, gt_ref, do_ref)

    row = pl.BlockSpec((tm, D), lambda i: (i, 0))
    vec = pl.BlockSpec((1, D), lambda i: (0, 0))
    acc = pl.BlockSpec((8, D), lambda i: (0, 0))
    ins = [dh, x, g, sc, dres] + ([o, gt] if gated else [])
    in_specs = [row, row, vec, vec, row] + ([row, vec] if gated else [])
    out_shape = [jax.ShapeDtypeStruct((S, D), F32), jax.ShapeDtypeStruct((8, D), F32)]
    out_specs = [row, acc]
    if gated:
        out_shape.append(jax.ShapeDtypeStruct((S, D), BF16))
        out_specs.append(row)
    return pl.pallas_call(body, name=name, grid=(S // tm,), in_specs=in_specs, out_specs=out_specs, out_shape=out_shape,
                          compiler_params=_cparams(("arbitrary",)))(*ins)


def _loss_head(x3, tgt, gf, o2, gt2):
    S, D = x3.shape
    tm = _tile(S, 256, 8)

    def body(x_ref, t_ref, g_ref, o_ref, gt_ref, dx_ref, do_ref, vec_ref):
        i = pl.program_id(0)

        @pl.when(i == 0)
        def _():
            vec_ref[...] = jnp.zeros_like(vec_ref)

        xv, gv = x_ref[...], g_ref[...]
        r = lax.rsqrt(jnp.mean(xv * xv, axis=-1, keepdims=True) + EPS)
        xn = xv * r
        e = xn * gv - t_ref[...]
        tok = 0.5 * jnp.mean(e * e, axis=-1, keepdims=True)
        vec_ref[0:1, :] += jnp.broadcast_to(jnp.sum(tok, axis=0, keepdims=True), (1, D))
        dy = e * (1.0 / D)
        vec_ref[1:2, :] += jnp.sum(dy * xn, axis=0, keepdims=True)
        dxn = dy * gv
        dx = r * (dxn - xn * jnp.mean(dxn * xn, axis=-1, keepdims=True))
        dx_ref[...] = dx
        vec_ref[2:3, :] += jnp.sum(dx * o_ref[...], axis=0, keepdims=True)
        do_ref[...] = (dx * gt_ref[...]).astype(BF16)

    row = pl.BlockSpec((tm, D), lambda i: (i, 0))
    vec = pl.BlockSpec((1, D), lambda i: (0, 0))
    return pl.pallas_call(
        body, name="loss_head", grid=(S // tm,), in_specs=[row, row, vec, row, vec],
        out_specs=[row, row, pl.BlockSpec((8, D), lambda i: (0, 0))],
        out_shape=[jax.ShapeDtypeStruct((S, D), F32), jax.ShapeDtypeStruct((S, D), BF16), jax.ShapeDtypeStruct((8, D), F32)],
        compiler_params=_cparams(("arbitrary",)))(x3, tgt, gf, o2, gt2)


def _ffn_in_swiglu(h, wg):
    S, D = h.shape
    _, _, tf = wg.shape
    nf = N_DEV // 2
    F = nf * tf
    tm = _tile(S, 256, 16)

    def body(h_ref, wa_ref, wu_ref, hf_ref, fac_ref):
        hv = h_ref[...]
        a = lax.dot_general(hv, wa_ref[...], _NN, preferred_element_type=F32)
        up = lax.dot_general(hv, wu_ref[...], _NN, preferred_element_type=F32)
        sa = _sigmoid(a)
        silu = a * sa
        hf_ref[...] = (silu * up).astype(BF16)
        fac_ref[0] = (up * (sa * (1.0 + a * (1.0 - sa)))).astype(BF16)
        fac_ref[1] = silu.astype(BF16)

    return pl.pallas_call(
        body, name="ffn_in_swiglu", grid=(nf, S // tm),
        in_specs=[pl.BlockSpec((tm, D), lambda j, i: (i, 0)), pl.BlockSpec((None, D, tf), lambda j, i: (j, 0, 0)),
                  pl.BlockSpec((None, D, tf), lambda j, i: (j + nf, 0, 0))],
        out_specs=[pl.BlockSpec((tm, tf), lambda j, i: (i, j)), pl.BlockSpec((2, tm, tf), lambda j, i: (0, i, j))],
        out_shape=[jax.ShapeDtypeStruct((S, F), BF16), jax.ShapeDtypeStruct((2, S, F), BF16)],
        compiler_params=_cparams(("parallel", "parallel")))(h, wg, wg)


def _colsum2(dg2):
    _, S, D = dg2.shape
    tm = _tile(S, 256, 16)

    def body(x_ref, o_ref):
        @pl.when(pl.program_id(0) == 0)
        def _():
            o_ref[...] = jnp.zeros_like(o_ref)

        o_ref[0:1, :] += jnp.sum(x_ref[0].astype(F32), axis=0, keepdims=True)
        o_ref[1:2, :] += jnp.sum(x_ref[1].astype(F32), axis=0, keepdims=True)

    return pl.pallas_call(body, name="gate_bias_grad", grid=(S // tm,), in_specs=[pl.BlockSpec((2, tm, D), lambda i: (0, i, 0))],
                          out_specs=pl.BlockSpec((2, D), lambda i: (0, 0)), out_shape=jax.ShapeDtypeStruct((2, D), F32),
                          compiler_params=_cparams(("arbitrary",)))(dg2)


def _gmlp_common(u_ref, v_ref, lg_ref, lb_ref, ws_ref, bsb_ref, G, T, Dg):
    ug, dug = _gelu_parts(u_ref[...])
    vg, dvg = _gelu_parts(v_ref[...])
    mu = jnp.mean(vg, axis=-1, keepdims=True)
    vc = vg - mu
    rstd = lax.rsqrt(jnp.mean(vc * vc, axis=-1, keepdims=True) + EPS)
    vhat = vc * rstd
    vn = vhat * lg_ref[...] + lb_ref[...]
    row = lax.broadcasted_iota(jnp.int32, (T, T), 0)
    col = lax.broadcasted_iota(jnp.int32, (T, T), 1)
    tril = row >= col
    s = []
    for g in range(G):
        w = jnp.where(tril, ws_ref[g], 0.0)
        s.append(_dot(w, vn[:, g * Dg:(g + 1) * Dg]) + bsb_ref[g])
    return ug, dug, dvg, rstd, vhat, vn, tril, s


def _gmlp_fwd(z, ln_g, ln_b, ws, bsb, GW):
    S = z.shape[0]
    G, T, _ = ws.shape
    Dg = GW // G

    def body(u_ref, v_ref, lg_ref, lb_ref, ws_ref, bsb_ref, ya_ref):
        ug, _, _, _, _, _, _, s = _gmlp_common(u_ref, v_ref, lg_ref, lb_ref, ws_ref, bsb_ref, G, T, Dg)
        for g in range(G):
            sl = slice(g * Dg, (g + 1) * Dg)
            ya_ref[:, sl] = (ug[:, sl] * s[g]).astype(BF16)

    vec = pl.BlockSpec((1, GW), lambda c: (0, 0))
    return pl.pallas_call(
        body, name="gmlp_fwd", grid=(S // T,),
        in_specs=[pl.BlockSpec((T, GW), lambda c: (c, 0)), pl.BlockSpec((T, GW), lambda c: (c, 1)), vec, vec,
                  pl.BlockSpec((G, T, T), lambda c: (0, 0, 0)), pl.BlockSpec((G, T, Dg), lambda c: (0, 0, 0))],
        out_specs=pl.BlockSpec((T, GW), lambda c: (c, 0)), out_shape=jax.ShapeDtypeStruct((S, GW), BF16),
        compiler_params=_cparams(("parallel",)))(z, z, ln_g, ln_b, ws, bsb)


def _gmlp_bwd(z, dya, ln_g, ln_b, ws, bsb, GW):
    S = z.shape[0]
    G, T, _ = ws.shape
    Dg = GW // G
    nc = S // T

    def body(u_ref, v_ref, dya_ref, lg_ref, lb_ref, ws_ref, bsb_ref, dz_ref, dln_ref, dws_ref, dbs_ref, dbs_acc, dvh):
        c = pl.program_id(0)

        @pl.when(c == 0)
        def _():
            dln_ref[...] = jnp.zeros_like(dln_ref)
            dws_ref[...] = jnp.zeros_like(dws_ref)
            dbs_acc[...] = jnp.zeros_like(dbs_acc)

        ug, dug, dvg, rstd, vhat, vn, tril, s = _gmlp_common(u_ref, v_ref, lg_ref, lb_ref, ws_ref, bsb_ref, G, T, Dg)
        dya_v = dya_ref[...]
        for g in range(G):
            sl = slice(g * Dg, (g + 1) * Dg)
            dy_g = dya_v[:, sl]
            dz_ref[:, sl] = (dy_g * s[g] * dug[:, sl]).astype(BF16)
            ds = dy_g * ug[:, sl]
            dbs_acc[g] += ds
            w = jnp.where(tril, ws_ref[g], 0.0)
            dvn_g = _dot(w, ds, _TN)
            dws_ref[g] += jnp.where(tril, _dot(ds, vn[:, sl], _NT), 0.0)
            dln_ref[0:1, sl] += jnp.sum(dvn_g * vhat[:, sl], axis=0, keepdims=True)
            dln_ref[1:2, sl] += jnp.sum(dvn_g, axis=0, keepdims=True)
            dvh[:, sl] = dvn_g * lg_ref[:, sl]
        dvhat = dvh[...]
        m1 = jnp.mean(dvhat, axis=-1, keepdims=True)
        m2 = jnp.mean(dvhat * vhat, axis=-1, keepdims=True)
        dz_ref[:, GW:2 * GW] = (rstd * (dvhat - m1 - vhat * m2) * dvg).astype(BF16)

        @pl.when(c == nc - 1)
        def _():
            for g in range(G):
                dbs_ref[g] = jnp.sum(dbs_acc[g], axis=-1, keepdims=True)

    vec = pl.BlockSpec((1, GW), lambda c: (0, 0))
    return pl.pallas_call(
        body, name="gmlp_bwd", grid=(nc,),
        in_specs=[pl.BlockSpec((T, GW), lambda c: (c, 0)), pl.BlockSpec((T, GW), lambda c: (c, 1)),
                  pl.BlockSpec((T, GW), lambda c: (c, 0)), vec, vec,
                  pl.BlockSpec((G, T, T), lambda c: (0, 0, 0)), pl.BlockSpec((G, T, Dg), lambda c: (0, 0, 0))],
        out_specs=[pl.BlockSpec((T, 2 * GW), lambda c: (c, 0)), pl.BlockSpec((8, GW), lambda c: (0, 0)),
                   pl.BlockSpec((G, T, T), lambda c: (0, 0, 0)), pl.BlockSpec((G, T, 1), lambda c: (0, 0, 0))],
        out_shape=[jax.ShapeDtypeStruct((S, 2 * GW), BF16), jax.ShapeDtypeStruct((8, GW), F32),
                   jax.ShapeDtypeStruct((G, T, T), F32), jax.ShapeDtypeStruct((G, T, 1), F32)],
        scratch_shapes=[pltpu.VMEM((G, T, Dg), F32), pltpu.VMEM((T, GW), F32)],
        compiler_params=_cparams(("arbitrary",)))(z, z, dya, ln_g, ln_b, ws, bsb)


def _hg_common(q_ref, f_ref, hlb_ref):
    C = HG_CHUNK
    a = hlb_ref[...]
    lb = _sigmoid(a[0:1, :] - a[1:2, :])
    sig = _sigmoid(f_ref[...])
    f = lb + (1.0 - lb) * sig
    lf = jnp.log(f)
    kk = 1.0 - f
    q = q_ref[...]
    sq = _sigmoid(q)
    qa = q * sq
    row = lax.broadcasted_iota(jnp.int32, (C, C), 0)
    col = lax.broadcasted_iota(jnp.int32, (C, C), 1)
    tril = row >= col
    b = _ones_dot(tril.astype(BF16), lf)
    bm = b[HG_MID:HG_MID + 1, :]
    bl = b[C - 1:C, :]
    e_b = jnp.exp(b)
    e_qm = jnp.exp(jnp.minimum(b - bm, EXP_CLAMP))
    e_km = jnp.exp(jnp.minimum(bm - b, EXP_CLAMP))
    e_kl = jnp.exp(bl - b)
    return dict(lb=lb, sig=sig, f=f, kk=kk, q=q, sq=sq, qa=qa, tril=tril, e_b=e_b, e_qm=e_qm, e_km=e_km, e_kl=e_kl,
                e_l=jnp.exp(bl), qh=qa * e_b, qt=qa * e_qm, kt=kk * e_km, kh=kk * e_kl)


def _hg_fwd(z, hg_lb, ng, HW):
    S = z.shape[0]
    C, H, dk = HG_CHUNK, HW // HG_DK, HG_DK
    nc = S // C

    def body(q_ref, f_ref, i_ref, og_ref, hlb_ref, ng_ref, yb_ref, o_ref, st_ref, state):
        @pl.when(pl.program_id(0) == 0)
        def _():
            state[...] = jnp.zeros_like(state)

        t = _hg_common(q_ref, f_ref, hlb_ref)
        iv = i_ref[...]
        for h in range(H):
            sl = slice(h * dk, (h + 1) * dk)
            st = state[h]
            st_ref[h] = st
            a = jnp.where(t["tril"], _dot(t["qt"][:, sl], t["kt"][:, sl], _NT), 0.0)
            o_h = _dot(a, iv[:, sl]) + _dot(t["qh"][:, sl], st, _NT)
            state[h] = st * t["e_l"][:, sl] + _dot(iv[:, sl], t["kh"][:, sl], _TN)
            o_ref[:, sl] = o_h
            rr = lax.rsqrt(jnp.mean(o_h * o_h, axis=-1, keepdims=True) + EPS)
            og = og_ref[:, sl]
            yb_ref[:, sl] = (o_h * rr * ng_ref[:, sl] * (og * _sigmoid(og))).astype(BF16)

    def col(k):
        return pl.BlockSpec((C, HW), lambda c: (c, k))

    base = 2
    return pl.pallas_call(
        body, name="hgrn_fwd", grid=(nc,),
        in_specs=[col(base), col(base + 1), col(base + 2), col(base + 3),
                  pl.BlockSpec((2, HW), lambda c: (0, 0)), pl.BlockSpec((1, HW), lambda c: (0, 0))],
        out_specs=[pl.BlockSpec((C, HW), lambda c: (c, 0)), pl.BlockSpec((C, HW), lambda c: (c, 0)),
                   pl.BlockSpec((None, H, dk, dk), lambda c: (c, 0, 0, 0))],
        out_shape=[jax.ShapeDtypeStruct((S, HW), BF16), jax.ShapeDtypeStruct((S, HW), F32),
                   jax.ShapeDtypeStruct((nc, H, dk, dk), F32)],
        scratch_shapes=[pltpu.VMEM((H, dk, dk), F32)],
        compiler_params=_cparams(("arbitrary",)))(z, z, z, z, hg_lb, ng)


def _hg_bwd(z, o, states, dyb, hg_lb, ng, HW, dz_head, dz_tail):
    S = z.shape[0]
    C, H, dk = HG_CHUNK, HW // HG_DK, HG_DK
    nc = S // C
    B0 = dz_head.shape[1]
    DT = dz_tail.shape[2]
    INW = B0 + 4 * HW + 2 * DT

    def body(q_ref, f_ref, i_ref, og_ref, o_ref, st_ref, stn_ref, dyb_ref, hlb_ref, ng_ref, head_ref, tail_ref,
             dzf_ref, dng_ref, dhlb_ref, dstate, cross, dqa_buf, dkk_buf, db_buf, dlb_acc):
        c = pl.program_id(0)
        dzf_ref[:, 0:B0] = head_ref[...]
        dzf_ref[:, B0 + 4 * HW:B0 + 4 * HW + DT] = tail_ref[0]
        dzf_ref[:, B0 + 4 * HW + DT:INW] = tail_ref[1]
        dz_ref = dzf_ref.at[:, B0:B0 + 4 * HW]

        @pl.when(c == 0)
        def _():
            dstate[...] = jnp.zeros_like(dstate)
            dlb_acc[...] = jnp.zeros_like(dlb_acc)
            dng_ref[...] = jnp.zeros_like(dng_ref)

        def r16(v):
            return v.astype(BF16).astype(F32)

        t = _hg_common(q_ref, f_ref, hlb_ref)
        iv = i_ref[...]
        for h in range(H):
            sl = slice(h * dk, (h + 1) * dk)
            o_h, og, dyb_h, ng_h = o_ref[:, sl], og_ref[:, sl], dyb_ref[:, sl], ng_ref[:, sl]
            sg = _sigmoid(og)
            silu_og = og * sg
            rr = lax.rsqrt(jnp.mean(o_h * o_h, axis=-1, keepdims=True) + EPS)
            on = o_h * rr
            dng_ref[0:1, sl] += jnp.sum(dyb_h * on * silu_og, axis=0, keepdims=True)
            dz_ref[:, 3 * HW + h * dk:3 * HW + (h + 1) * dk] = (dyb_h * on * ng_h * (sg * (1.0 + og * (1.0 - sg)))).astype(BF16)
            don = dyb_h * ng_h * silu_og
            do_h = rr * (don - on * jnp.mean(don * on, axis=-1, keepdims=True))

            qt, kt, qh, kh, iv_h = t["qt"][:, sl], t["kt"][:, sl], t["qh"][:, sl], t["kh"][:, sl], iv[:, sl]
            a = jnp.where(t["tril"], _dot(qt, kt, _NT), 0.0)
            da = jnp.where(t["tril"], _dot(do_h, iv_h, _NT), 0.0)
            st, dst = st_ref[h], dstate[h]
            cross[:, sl] = jnp.sum(dst * stn_ref[h], axis=0, keepdims=True)
            dqh = _dot(do_h, st)
            dstate[h] = _dot(do_h, qh, _TN) + dst * t["e_l"][:, sl]
            div = _dot(a, do_h, _TN) + _dot(kh, dst, _NT)
            dkh = _dot(iv_h, dst)
            dqt = _dot(da, kt)
            dkt = _dot(da, qt, _TN)
            dz_ref[:, 2 * HW + h * dk:2 * HW + (h + 1) * dk] = div.astype(BF16)
            dqa_buf[:, sl] = dqh * t["e_b"][:, sl] + dqt * t["e_qm"][:, sl]
            dkk_buf[:, sl] = dkt * t["e_km"][:, sl] + dkh * t["e_kl"][:, sl]
            db_buf[:, sl] = r16(qt) * dqt - r16(kt) * dkt + r16(qh) * dqh - r16(kh) * dkh

        dqa, dkk = dqa_buf[...], dkk_buf[...]
        triu = jnp.logical_not(t["tril"]) | (lax.broadcasted_iota(jnp.int32, (C, C), 0) == lax.broadcasted_iota(jnp.int32, (C, C), 1))
        dlf = _ones_dot(triu.astype(BF16), db_buf[...]) + cross[...]
        df = dlf / t["f"] - dkk
        sig, lb = t["sig"], t["lb"]
        dz_ref[:, HW:2 * HW] = (df * (1.0 - lb) * sig * (1.0 - sig)).astype(BF16)
        dlb_acc[...] += jnp.sum(df * (1.0 - sig), axis=0, keepdims=True)
        q, sq = t["q"], t["sq"]
        dz_ref[:, 0:HW] = (dqa * (sq * (1.0 + q * (1.0 - sq)))).astype(BF16)

        @pl.when(c == nc - 1)
        def _():
            da0 = dlb_acc[...] * lb * (1.0 - lb)
            dhlb_ref[0:1, :] = da0
            dhlb_ref[1:2, :] = -da0

    def col(k):
        return pl.BlockSpec((C, HW), lambda c: (nc - 1 - c, k))

    base = 2
    return pl.pallas_call(
        body, name="hgrn_bwd", grid=(nc,),
        in_specs=[col(base), col(base + 1), col(base + 2), col(base + 3), col(0),
                  pl.BlockSpec((None, H, dk, dk), lambda c: (nc - 1 - c, 0, 0, 0)),
                  pl.BlockSpec((None, H, dk, dk), lambda c: (jnp.minimum(nc - c, nc - 1), 0, 0, 0)), col(0),
                  pl.BlockSpec((2, HW), lambda c: (0, 0)), pl.BlockSpec((1, HW), lambda c: (0, 0)),
                  pl.BlockSpec((C, B0), lambda c: (nc - 1 - c, 0)), pl.BlockSpec((2, C, DT), lambda c: (0, nc - 1 - c, 0))],
        out_specs=[pl.BlockSpec((C, INW), lambda c: (nc - 1 - c, 0)), pl.BlockSpec((8, HW), lambda c: (0, 0)),
                   pl.BlockSpec((2, HW), lambda c: (0, 0))],
        out_shape=[jax.ShapeDtypeStruct((S, INW), BF16), jax.ShapeDtypeStruct((8, HW), F32), jax.ShapeDtypeStruct((2, HW), F32)],
        scratch_shapes=[pltpu.VMEM((H, dk, dk), F32), pltpu.VMEM((1, HW), F32), pltpu.VMEM((C, HW), F32), pltpu.VMEM((C, HW), F32),
                        pltpu.VMEM((C, HW), F32), pltpu.VMEM((1, HW), F32)],
        compiler_params=_cparams(("arbitrary",)))(z, z, z, z, o, states, states, dyb, hg_lb, ng, dz_head, dz_tail)


def _position():
    x, y, c = lax.axis_index("x"), lax.axis_index("y"), lax.axis_index("c")
    return x, y, c, 4 * x + 2 * y + c


def _flip(x, y, c, k):
    return (1 - x if k & 4 else x, 1 - y if k & 2 else y, 1 - c if k & 1 else c)


def _allgather_small(name, v):
    R, L = v.shape

    def body(v_ref, out_ref, send_sems, recv_sems):
        x, y, c, me = _position()
        out_ref[me] = v_ref[...]
        copies = []
        for k in range(1, N_DEV):
            cp = pltpu.make_async_remote_copy(src_ref=v_ref, dst_ref=out_ref.at[me], send_sem=send_sems.at[k - 1],
                                              recv_sem=recv_sems.at[k - 1], device_id=_flip(x, y, c, k), device_id_type=MESH)
            cp.start()
            copies.append(cp)
        for cp in copies:
            cp.wait()

    return pl.pallas_call(
        body, name=name, out_shape=jax.ShapeDtypeStruct((N_DEV, R, L), v.dtype),
        in_specs=[pl.BlockSpec(memory_space=pltpu.VMEM)], out_specs=pl.BlockSpec(memory_space=pltpu.VMEM),
        scratch_shapes=[pltpu.SemaphoreType.DMA((N_DEV - 1,)), pltpu.SemaphoreType.DMA((N_DEV - 1,))],
        compiler_params=pltpu.CompilerParams(vmem_limit_bytes=VMEM_LIMIT),
    )(v)


def _allgather_hbm(name, shards):
    n = len(shards)

    def body(*refs):
        ins, outs = refs[:n], refs[n:2 * n]
        send_sems, recv_sems, local_sems = refs[2 * n:]
        x, y, c, me = _position()
        sibling = (x, y, 1 - c)
        chips = [(1 - x, y), (x, 1 - y), (1 - x, 1 - y)]

        def slot(px, py, pc):
            return 4 * px + 2 * py + pc

        def copy(w, k, block, to, src=None):
            dst = outs[w].at[slot(*block)]
            return pltpu.make_async_remote_copy(src_ref=dst if src is None else src, dst_ref=dst, send_sem=send_sems.at[w, k],
                                                recv_sem=recv_sems.at[w, k], device_id=to, device_id_type=MESH)

        mine, first, passed = [], [], []
        for w in range(n):
            cp = pltpu.make_async_copy(ins[w], outs[w].at[me], local_sems.at[w])
            cp.start()
            mine.append(cp)
            for j, chip in enumerate(chips):
                first.append(copy(w, 1 + j, (x, y, c), (*chip, c), src=ins[w]))
            first.append(copy(w, 0, (x, y, c), sibling, src=ins[w]))
        for cp in first:
            cp.start()
        for w in range(n):
            for j, chip in enumerate(chips):
                copy(w, 1 + j, (*chip, c), (x, y, c)).wait_recv()
                cp = copy(w, 4 + j, (*chip, c), sibling)
                cp.start()
                passed.append(cp)
        for w in range(n):
            copy(w, 0, sibling, (x, y, c)).wait_recv()
            for j, chip in enumerate(chips):
                copy(w, 4 + j, (*chip, 1 - c), (x, y, c)).wait_recv()
        for cp in first + passed:
            cp.wait_send()
        for cp in mine:
            cp.wait()

    hbm = pl.BlockSpec(memory_space=pltpu.HBM)
    return pl.pallas_call(
        body, name=name, out_shape=[jax.ShapeDtypeStruct((N_DEV, *s.shape), s.dtype) for s in shards],
        in_specs=[hbm] * n, out_specs=[hbm] * n,
        scratch_shapes=[pltpu.SemaphoreType.DMA((n, 7)), pltpu.SemaphoreType.DMA((n, 7)), pltpu.SemaphoreType.DMA((n,))],
    )(*shards)


_HBM = pl.BlockSpec(memory_space=pltpu.HBM)
_SEM = pl.BlockSpec(memory_space=pltpu.SEMAPHORE)
_EFFECT = pltpu.SideEffectType.DATAFLOW_SIDE_EFFECTING


def _split_start(name, bufs, n_sems, copies_fn, after=None):
    nb = len(bufs)
    extra = [] if after is None else [after]
    k = nb + len(extra)

    def body(*refs):
        for cp in copies_fn(refs[:nb], refs[k], refs[k + 1]):
            cp.start()
        refs[-1][...] = jnp.zeros_like(refs[-1])

    sems = pltpu.SemaphoreType.DMA((n_sems,))
    res = pl.pallas_call(
        body, name=name,
        out_shape=(sems, sems, *[pltpu.HBM(a.shape, a.dtype) for a in bufs], jax.ShapeDtypeStruct((8, LANES), F32)),
        in_specs=[_HBM] * nb + [pl.BlockSpec(memory_space=pl.ANY)] * len(extra),
        out_specs=(_SEM, _SEM, *[_HBM] * nb, pl.BlockSpec(memory_space=pltpu.VMEM)),
        input_output_aliases={i: 2 + i for i in range(nb)},
        compiler_params=pltpu.CompilerParams(has_side_effects=_EFFECT),
    )(*[pltpu.with_memory_space_constraint(a, pltpu.HBM) for a in bufs], *extra)
    return res[0], res[1], list(res[2:2 + nb]), res[-1]


def _split_wait(name, bufs, send_sems, recv_sems, after, copies_fn):
    nb = len(bufs)

    def body(*refs):
        for cp in copies_fn(refs[:nb], refs[nb], refs[nb + 1]):
            cp.wait_send()
            cp.wait_recv()

    res = pl.pallas_call(
        body, name=name, out_shape=tuple(pltpu.HBM(a.shape, a.dtype) for a in bufs),
        in_specs=[_HBM] * nb + [_SEM, _SEM, pl.BlockSpec(memory_space=pl.ANY)], out_specs=tuple([_HBM] * nb),
        input_output_aliases={i: i for i in range(nb)},
        compiler_params=pltpu.CompilerParams(has_side_effects=_EFFECT),
    )(*bufs, send_sems, recv_sems, after)
    return list(res)


def _split_relay(name, bufs, send_sems, recv_sems, after, wait_fn, n_sems, start_fn):
    nb = len(bufs)

    def body(*refs):
        for cp in wait_fn(refs[:nb], refs[nb], refs[nb + 1]):
            cp.wait_send()
            cp.wait_recv()
        for cp in start_fn(refs[:nb], refs[nb + 3], refs[nb + 4]):
            cp.start()
        refs[-1][...] = jnp.zeros_like(refs[-1])

    sems = pltpu.SemaphoreType.DMA((n_sems,))
    res = pl.pallas_call(
        body, name=name, out_shape=(sems, sems, *[pltpu.HBM(a.shape, a.dtype) for a in bufs], jax.ShapeDtypeStruct((8, LANES), F32)),
        in_specs=[_HBM] * nb + [_SEM, _SEM, pl.BlockSpec(memory_space=pl.ANY)],
        out_specs=(_SEM, _SEM, *[_HBM] * nb, pl.BlockSpec(memory_space=pltpu.VMEM)),
        input_output_aliases={i: 2 + i for i in range(nb)},
        compiler_params=pltpu.CompilerParams(has_side_effects=_EFFECT),
    )(*bufs, send_sems, recv_sems, after)
    return res[0], res[1], list(res[2:2 + nb]), res[-1]


N_CHIP = 4


def _chip_flip(x, y, k):
    return (1 - x if k & 2 else x), (1 - y if k & 1 else y)


def _gather_first_copies(n):
    def copies(bufs, send_sems, recv_sems):
        x, y, c, me = _position()
        out = []
        for w in range(n):
            for k in range(N_CHIP):
                to = (x, y, 1 - c) if k == 0 else (*_chip_flip(x, y, k), c)
                out.append(pltpu.make_async_remote_copy(
                    src_ref=bufs[w], dst_ref=bufs[n + w].at[me], send_sem=send_sems.at[w * N_CHIP + k],
                    recv_sem=recv_sems.at[w * N_CHIP + k], device_id=to, device_id_type=MESH))
        return out
    return copies


def _gather_relay_copies(n):
    def copies(bufs, send_sems, recv_sems):
        x, y, c, _ = _position()
        out = []
        for w in range(n):
            for k in range(1, N_CHIP):
                px, py = _chip_flip(x, y, k)
                blk = bufs[n + w].at[4 * px + 2 * py + c]
                out.append(pltpu.make_async_remote_copy(
                    src_ref=blk, dst_ref=blk, send_sem=send_sems.at[w * (N_CHIP - 1) + k - 1],
                    recv_sem=recv_sems.at[w * (N_CHIP - 1) + k - 1], device_id=(x, y, 1 - c), device_id_type=MESH))
        return out
    return copies


def _xor(a, b):
    return a + b - 2 * a * b


def _forward_first_copies(n):
    def copies(bufs, send_sems, recv_sems):
        x, y, c, me = _position()
        out = []
        for w in range(n):
            for k, to in enumerate([(x, y, 1 - c), (1 - x, y, c), (x, 1 - y, c)]):
                out.append(pltpu.make_async_remote_copy(
                    src_ref=bufs[w], dst_ref=bufs[n + w].at[me], send_sem=send_sems.at[w * 3 + k],
                    recv_sem=recv_sems.at[w * 3 + k], device_id=to, device_id_type=MESH))
        return out
    return copies


def _forward_second_copies(n):
    def copies(bufs, send_sems, recv_sems):
        x, y, c, _ = _position()
        out = []
        for w in range(n):
            half = bufs[n + w].shape[1] // 2
            for k, (src_chip, rows, to) in enumerate([((1 - x, y), pl.ds(0, half), (x, 1 - y, c)), ((x, 1 - y), pl.ds(half, half), (1 - x, y, c))]):
                blk = bufs[n + w].at[4 * src_chip[0] + 2 * src_chip[1] + c, rows]
                out.append(pltpu.make_async_remote_copy(src_ref=blk, dst_ref=blk, send_sem=send_sems.at[w * 4 + k],
                                                        recv_sem=recv_sems.at[w * 4 + k], device_id=to, device_id_type=MESH))
            for k, (px, py) in enumerate([(1 - x, y), (x, 1 - y)]):
                blk = bufs[n + w].at[4 * px + 2 * py + c]
                out.append(pltpu.make_async_remote_copy(src_ref=blk, dst_ref=blk, send_sem=send_sems.at[w * 4 + 2 + k],
                                                        recv_sem=recv_sems.at[w * 4 + 2 + k], device_id=(x, y, 1 - c), device_id_type=MESH))
        return out
    return copies


def _forward_third_copies(n):
    def copies(bufs, send_sems, recv_sems):
        x, y, c, _ = _position()
        out = []
        for w in range(n):
            blk = bufs[n + w].at[4 * (1 - x) + 2 * (1 - y) + c]
            out.append(pltpu.make_async_remote_copy(src_ref=blk, dst_ref=blk, send_sem=send_sems.at[w], recv_sem=recv_sems.at[w],
                                                    device_id=(x, y, 1 - c), device_id_type=MESH))
        return out
    return copies


def _to_sibling_copies(n):
    def copies(bufs, send_sems, recv_sems):
        x, y, c, _ = _position()
        out = []
        for w in range(n):
            for q in range(N_CHIP):
                out.append(pltpu.make_async_remote_copy(
                    src_ref=bufs[w].at[2 * q + 1 - c], dst_ref=bufs[n + w].at[q], send_sem=send_sems.at[w * N_CHIP + q],
                    recv_sem=recv_sems.at[w * N_CHIP + q], device_id=(x, y, 1 - c), device_id_type=MESH))
        return out
    return copies


def _to_owner_copies(n):
    def copies(bufs, send_sems, recv_sems):
        x, y, c, _ = _position()
        out = []
        for w in range(n):
            for k in range(1, N_CHIP):
                px, py = (1 - x if k & 2 else x), (1 - y if k & 1 else y)
                out.append(pltpu.make_async_remote_copy(
                    src_ref=bufs[w].at[2 * px + py], dst_ref=bufs[n + w].at[k - 1], send_sem=send_sems.at[w * (N_CHIP - 1) + k - 1],
                    recv_sem=recv_sems.at[w * (N_CHIP - 1) + k - 1], device_id=(px, py, c), device_id_type=MESH))
        return out
    return copies


def _chip_sum(name, stack, landed, c_idx):
    _, R, C = stack.shape
    tr = _tile(R, max(16, 1048576 // C), 16)

    def body(c_ref, a_ref, b_ref, o_ref):
        o_ref[...] = (a_ref[...].astype(F32) + b_ref[...].astype(F32)).astype(o_ref.dtype)

    return pl.pallas_call(
        body, name=name,
        grid_spec=pltpu.PrefetchScalarGridSpec(
            num_scalar_prefetch=1, grid=(N_CHIP, R // tr),
            in_specs=[pl.BlockSpec((None, tr, C), lambda q, i, c_ref: (2 * q + c_ref[0], i, 0)),
                      pl.BlockSpec((None, tr, C), lambda q, i, c_ref: (q, i, 0))],
            out_specs=pl.BlockSpec((None, tr, C), lambda q, i, c_ref: (q, i, 0))),
        out_shape=jax.ShapeDtypeStruct((N_CHIP, R, C), stack.dtype),
        compiler_params=_cparams(("parallel", "parallel")))(c_idx, stack, landed)


def _ada_mod(c16, w):
    _, D = c16.shape
    n = w.shape[1]
    tk = _tile(D, 512)
    nk = D // tk

    def body(c_ref, w_ref, o_ref, ca_ref):
        @pl.when(pl.program_id(0) == 0)
        def _():
            o_ref[...] = jnp.zeros_like(o_ref)

        cv = c_ref[...]
        ca = cv * _sigmoid(cv)
        ca_ref[...] = ca
        o_ref[...] += _dot(ca, w_ref[...])

    return pl.pallas_call(
        body, name="ada_mod", grid=(nk,),
        in_specs=[pl.BlockSpec((16, tk), lambda k: (0, k)), pl.BlockSpec((tk, n), lambda k: (k, 0))],
        out_specs=[pl.BlockSpec((16, n), lambda k: (0, 0)), pl.BlockSpec((16, tk), lambda k: (0, k))],
        out_shape=[jax.ShapeDtypeStruct((16, n), F32), jax.ShapeDtypeStruct((16, D), F32)],
        compiler_params=_cparams(("arbitrary",)))(c16, w)


def _adam_math(w, g, m, v):
    m2 = ADAM_B1 * m + (1.0 - ADAM_B1) * g
    v2 = ADAM_B2 * v + (1.0 - ADAM_B2) * (g * g)
    m_hat = m2 / (1.0 - ADAM_B1 ** ADAM_STEP)
    v_hat = v2 / (1.0 - ADAM_B2 ** ADAM_STEP)
    delta = -ADAM_LR * (m_hat / (jnp.sqrt(v_hat) + ADAM_EPS) + ADAM_WD * w)
    return delta, m2, v2


def _adamw(name, w, m, v, parts, row0=0, into=None):
    R, C = w.shape
    Rp = parts[0].shape[1]
    tr = _tile(Rp, max(16, 393216 // C), 16)
    off = row0 // tr
    n_p = len(parts)
    held = [] if into is None else list(into)

    def body(*refs):
        w_ref, m_ref, v_ref = refs[:3]
        g_ref, d_ref, m2_ref, v2_ref = refs[3 + n_p + len(held):]
        g = None
        for p_ref in refs[3:3 + n_p]:
            for s in range(p_ref.shape[0]):
                t = p_ref[s].astype(F32)
                g = t if g is None else g + t
        delta, m2, v2 = _adam_math(w_ref[...], g, m_ref[...], v_ref[...])
        g_ref[...] = g
        d_ref[...] = delta
        m2_ref[...] = m2
        v2_ref[...] = v2

    blk = pl.BlockSpec((tr, C), lambda i: (i + off, 0))
    out = jax.ShapeDtypeStruct((R, C), F32)
    return pl.pallas_call(
        body, name=name, grid=(Rp // tr,),
        in_specs=[blk, blk, blk] + [pl.BlockSpec((a.shape[0], tr, C), lambda i: (0, i, 0)) for a in parts]
        + [pl.BlockSpec(memory_space=pl.ANY)] * len(held),
        out_specs=[blk] * 4, out_shape=[out] * 4, input_output_aliases={3 + n_p + i: i for i in range(len(held))},
        compiler_params=_cparams(("parallel",)))(w, m, v, *parts, *held)


def _small_update(gathered, w, m, v, after):
    _, R, L = gathered.shape
    rs = w.shape[0]

    def body(p_ref, w_ref, m_ref, v_ref, after_ref, g_ref, d_ref, m2_ref, v2_ref):
        g = p_ref[0]
        for p in range(1, N_DEV):
            g = g + p_ref[p]
        g_ref[...] = g
        delta, m2, v2 = _adam_math(w_ref[...], g[0:rs, :], m_ref[...], v_ref[...])
        d_ref[...] = delta
        m2_ref[...] = m2
        v2_ref[...] = v2

    vm = pl.BlockSpec(memory_space=pltpu.VMEM)
    sm = jax.ShapeDtypeStruct((rs, L), F32)
    return pl.pallas_call(body, name="small_update", in_specs=[vm] * 4 + [pl.BlockSpec(memory_space=pl.ANY)], out_specs=[vm] * 4,
                          out_shape=[jax.ShapeDtypeStruct((R, L), F32), sm, sm, sm],
                          compiler_params=pltpu.CompilerParams(vmem_limit_bytes=VMEM_LIMIT))(gathered, w, m, v, after)


class _Fetched(dict):
    def __init__(self, fetch):
        super().__init__()
        self.fetch = fetch

    def first(self, key, after):
        self[key] = self.fetch(key, after)
        return self[key]


def _local_step(x, tgt, mod, p, fetch, F, scatter=None):
    S, D = x.shape
    GW, HW = p["ln_g"].shape[1], p["hg_ng"].shape[1]
    G, T, _ = p["ws"].shape
    w = _Fetched(fetch)
    INW = 2 * GW + 4 * HW + 2 * D
    in_loc, br_loc, fi_loc = INW // N_DEV, D // N_DEV, 2 * F // N_DEV
    assert GW == HW and F % fi_loc == 0
    sh1, sc1, gt1, sh2, sc2, gt2 = (mod[:, k * D:(k + 1) * D] for k in range(6))
    bsb = jnp.broadcast_to(p["bs"][:, :, None], (G, T, GW // G))

    tm = _tile(S, 1024, 16)
    tmh = _tile(S, 512, 16)
    tn_in = _tile(in_loc, 1280)
    tn_d = _tile(D, 512)
    tn_br = _tile(br_loc, 512)
    tk_s = S
    tm_w = _tile(D, 1024)
    g_off = 2 * GW + 4 * HW

    h1 = _norm_mod("norm1", x, p["norm1_g"], sc1, sh1)
    z = _mm_nn_stacked("proj_in", h1, w.first("in", h1), tm=tm, tn=tn_in, tk=D)[0]
    ya = _gmlp_fwd(z, p["ln_g"], p["ln_b"], p["ws"], bsb, GW)
    yb, o_hg, states = _hg_fwd(z, p["hg_lb"], p["hg_ng"], HW)
    t_fi = w.first("fi_early", yb)
    pa = _mm_nn_stacked("branch_gmlp", ya, w.first("bg", z), tm=tm, tn=tn_br, tk=GW, after=t_fi)[0]

    def gates(ga_ref, gb_ref, ba_ref, bb_ref):
        return _sigmoid(ga_ref[...] + ba_ref[...]), _sigmoid(gb_ref[...] + bb_ref[...])

    def gate_specs(tn_):
        o1, o2 = g_off // tn_, (g_off + D) // tn_
        return [pl.BlockSpec((tm, tn_), lambda i, j, k: (i, o1 + j)), pl.BlockSpec((tm, tn_), lambda i, j, k: (i, o2 + j)),
                pl.BlockSpec((1, tn_), lambda i, j, k: (0, j)), pl.BlockSpec((1, tn_), lambda i, j, k: (0, D // tn_ + j))]

    def merge_ep(acc, ex, outs):
        ga, gb = gates(*ex[1:5])
        outs[0][...] = acc
        outs[1][...] = (ga * ex[0][...] + gb * acc).astype(BF16)

    tile_o = pl.BlockSpec((tm, tn_br), lambda i, j, k: (i, j))
    pb, y = _mm_nn_stacked(
        "branch_hg_merge", yb, w.first("bh", z), tm=tm, tn=tn_br, tk=HW, extras=[pa, z, z, p["b_gate"], p["b_gate"]],
        extra_specs=[tile_o, *gate_specs(tn_br)], out_shapes=[jax.ShapeDtypeStruct((S, D), F32), jax.ShapeDtypeStruct((S, D), BF16)],
        out_specs=[tile_o, tile_o], epilogue=merge_ep)

    def resid_ep(acc, ex, outs):
        outs[0][...] = acc
        outs[1][...] = ex[0][...] + ex[1][...] * acc

    def resid_mm(name, a, b, res, gt, tm_):
        K = a.shape[1]
        t_o = pl.BlockSpec((tm_, tn_d), lambda i, j, k: (i, j))
        return _matmul(
            name, a, b, dims=_NN, grid_mnk=(S // tm_, D // tn_d, 1), tiles=(tm_, tn_d),
            a_spec=pl.BlockSpec((tm_, K), lambda i, j, k: (i, 0)), b_spec=pl.BlockSpec((K, tn_d), lambda i, j, k: (0, j)),
            extras=[res, gt], extra_specs=[t_o, pl.BlockSpec((1, tn_d), lambda i, j, k: (0, j))],
            out_shapes=[jax.ShapeDtypeStruct((S, D), F32)] * 2, out_specs=[t_o, t_o], epilogue=resid_ep)

    o1, xm = resid_mm("proj_out", y, w.first("out", z), x, gt1, tm)
    h2 = _norm_mod("norm2", xm, p["norm2_g"], sc2, sh2)
    hf, hf_fac = _ffn_in_swiglu(h2, w.first("fi", h2))
    o2, x3 = resid_mm("ffn_out", hf, w.first("fo", hf), xm, gt2, tmh)
    dx3, do2, vec_l = _loss_head(x3, tgt, p["final_g"], o2, gt2)

    nf = F // fi_loc

    def dswiglu_ep(acc, ex, outs):
        outs[0][0] = (acc * ex[0][0].astype(F32)).astype(BF16)
        outs[0][1] = (acc * ex[0][1].astype(F32)).astype(BF16)

    pair = pl.BlockSpec((2, tmh, fi_loc), lambda i, j, k: (0, i, j))
    dab = _matmul(
        "ffn_out_dx", do2, w["fo"], dims=_NT, grid_mnk=(S // tmh, nf, 1), tiles=(tmh, fi_loc),
        a_spec=pl.BlockSpec((tmh, D), lambda i, j, k: (i, 0)), b_spec=pl.BlockSpec((fi_loc, D), lambda i, j, k: (j, 0)),
        extras=[hf_fac], extra_specs=[pair], out_shapes=[jax.ShapeDtypeStruct((2, S, F), BF16)], out_specs=[pair],
        epilogue=dswiglu_ep)[0]
    start = (lambda name, grads: scatter[0](name, grads)) if scatter is not None else (lambda name, grads: None)
    push = (lambda name, after: scatter[1](name, after)) if scatter is not None else (lambda name, after: None)

    def zero(token):
        return 0.0 if token is None else token[0:1, 0:1]

    tm_f = _tile(F, 512)
    g_fo = _mm_tn("ffn_out_dw", hf, do2, pl.BlockSpec((tk_s, D), lambda i, j, k: (k, j)), Mo=F, No=D, S=S, tm=tm_f, tn=D, tk=tk_s)
    g_fi = _mm_tn("ffn_in_dw", h2, dab, pl.BlockSpec((None, tk_s, fi_loc), lambda i, j, k: (j // nf, k, j % nf)),
                  Mo=D, No=2 * F, S=S, tm=tm_w, tn=fi_loc, tk=tk_s, stacked_nloc=fi_loc, after=g_fo)
    t_ffn = start("scatter_ffn", dict(fo=g_fo, fi=g_fi))
    dh2 = _mm_nt_stacked("ffn_in_dx", pl.BlockSpec((None, tmh, fi_loc), lambda i, j, k: (k // nf, i, k % nf)), dab, w["fi"],
                         M=S, tm=tmh, tn=D, tk=fi_loc, after=t_ffn)
    dxm, vec2, do1 = _norm_mod_bwd("norm2_bwd", dh2, xm, p["norm2_g"], sc2, dx3, o1, gt1)
    t_ffn = push("scatter_ffn", dxm)

    def dmerge_ep(acc, ex, outs):
        ga, gb = gates(*ex[2:6])
        outs[0][...] = (acc * ga).astype(BF16)
        outs[1][...] = (acc * gb).astype(BF16)
        outs[2][0] = (acc * ex[0][...] * ga * (1.0 - ga)).astype(BF16)
        outs[2][1] = (acc * ex[1][...] * gb * (1.0 - gb)).astype(BF16)

    t_o = pl.BlockSpec((tm, tn_d), lambda i, j, k: (i, j))
    dpa, dpb, dg2 = _matmul(
        "proj_out_dx", do1, w["out"], dims=_NT, grid_mnk=(S // tm, D // tn_d, 1), tiles=(tm, tn_d),
        a_spec=pl.BlockSpec((tm, D), lambda i, j, k: (i, 0)), b_spec=pl.BlockSpec((tn_d, D), lambda i, j, k: (j, 0)),
        extras=[pa, pb, z, z, p["b_gate"], p["b_gate"]], extra_specs=[t_o, t_o, *gate_specs(tn_d)],
        out_shapes=[jax.ShapeDtypeStruct((S, D), BF16), jax.ShapeDtypeStruct((S, D), BF16), jax.ShapeDtypeStruct((2, S, D), BF16)],
        out_specs=[t_o, t_o, pl.BlockSpec((2, tm, tn_d), lambda i, j, k: (0, i, j))], epilogue=dmerge_ep, after=t_ffn)
    g_out = _mm_tn("proj_out_dw", y, do1, pl.BlockSpec((tk_s, D), lambda i, j, k: (k, j)), Mo=D, No=D, S=S, tm=tn_d, tn=D, tk=tk_s)
    tn_g = _tile(GW, 512)
    b_br = pl.BlockSpec((tk_s, br_loc), lambda i, j, k: (k, j))
    g_bg = _mm_tn("branch_gmlp_dw", ya, dpa, b_br, Mo=GW, No=D, S=S, tm=tn_g, tn=br_loc, tk=tk_s, stacked_nloc=br_loc)
    g_bh = _mm_tn("branch_hg_dw", yb, dpb, b_br, Mo=HW, No=D, S=S, tm=tn_g, tn=br_loc, tk=tk_s, stacked_nloc=br_loc)
    t_mix = start("scatter_mixer", dict(out=g_out, bg=g_bg, bh=g_bh))
    def branch_dx(name, dp, wg):
        flat = jnp.swapaxes(wg, 0, 1).reshape(wg.shape[1], D)
        return _matmul(
            name, dp, flat, dims=_NT, grid_mnk=(S // tm, GW // tn_g, 1), tiles=(tm, tn_g),
            a_spec=pl.BlockSpec((tm, D), lambda i, j, k: (i, 0)), b_spec=pl.BlockSpec((tn_g, D), lambda i, j, k: (j, 0)),
            out_shapes=[jax.ShapeDtypeStruct((S, GW), F32)], out_specs=[pl.BlockSpec((tm, tn_g), lambda i, j, k: (i, j))],
            epilogue=_store(F32), after=t_mix)[0]

    dya = branch_dx("branch_gmlp_dx", dpa, w["bg"])
    dyb = branch_dx("branch_hg_dx", dpb, w["bh"])
    db_gate = _colsum2(dg2)
    dz_gmlp, dln, dws, dbs = _gmlp_bwd(z, dya, p["ln_g"], p["ln_b"], p["ws"], bsb, GW)
    t_mix = push("scatter_mixer", dz_gmlp)
    dz, dng, dhlb = _hg_bwd(z, o_hg, states, dyb, p["hg_lb"], p["hg_ng"] + zero(t_mix), HW, dz_gmlp, dg2)
    half = D // 2
    tm_h = _tile(half, 1024)
    g_in = []
    t_in = None
    for hname, h in (("a", 0), ("b", 1)):
        g_in.append(_mm_tn("proj_in_dw_" + hname, h1, dz, pl.BlockSpec((tk_s, in_loc), lambda i, j, k: (k, j)), Mo=half, No=INW, S=S,
                           tm=tm_h, tn=in_loc, tk=tk_s, stacked_nloc=in_loc, after=t_in, a_off=h * (half // tm_h)))
        t_in = start("scatter_proj_in_" + hname, {"w_in_" + hname: g_in[-1]})
    t_in = push("scatter_proj_in_a", t_in)
    def norm1_bwd_ep(acc, ex, outs, first):
        _norm_mod_bwd_rows(first, acc, ex[0], ex[1], ex[2], ex[3], outs[0], outs[1])

    tmq = _tile(S, 256, 16)
    row_d = pl.BlockSpec((tmq, D), lambda i, j, k: (i, 0))
    vec_d = pl.BlockSpec((1, D), lambda i, j, k: (0, 0))
    dx, vec1 = _mm_nt_stacked(
        "proj_in_dx", pl.BlockSpec((tmq, in_loc), lambda i, j, k: (i, k)), dz, w["in"], M=S, tm=tmq, tn=D, tk=in_loc, after=t_in,
        extras=[x, p["norm1_g"], sc1, dxm], extra_specs=[row_d, vec_d, vec_d, row_d],
        out_shapes=[jax.ShapeDtypeStruct((S, D), F32), jax.ShapeDtypeStruct((8, D), F32)],
        out_specs=[row_d, pl.BlockSpec((8, D), lambda i, j, k: (0, 0))], epilogue=norm1_bwd_ep, sem=("arbitrary", "arbitrary", "arbitrary"))

    dmod = jnp.concatenate([vec1[0:1], vec1[1:2], vec2[3:4], vec2[0:1], vec2[1:2], vec_l[2:3]], axis=1)
    small = dict(norm1_g=vec1[2:3], b_gate=db_gate.reshape(1, 2 * D), ln_g=dln[0:1], ln_b=dln[1:2], ws=dws, bs=dbs.reshape(G, T),
                 hg_lb=dhlb, hg_ng=dng[0:1], norm2_g=vec2[2:3], final_g=vec_l[1:2], loss=vec_l[0:1, 0:LANES])
    big = dict(w_in_a=g_in[0], w_in_b=g_in[1], bg=g_bg, bh=g_bh, out=g_out, fi=g_fi, fo=g_fo)
    return dx, big, small, dmod


_SMALL = ("b_ada", "norm1_g", "b_gate", "ln_g", "ln_b", "ws", "bs", "hg_lb", "hg_ng", "norm2_g", "final_g")


def _pack(parts, rows_mult=8):
    flat = [a.reshape(-1) for a in parts]
    offs, n = [], 0
    for a in flat:
        offs.append(n)
        n += a.shape[0]
    pad = (-n) % (LANES * rows_mult)
    if pad:
        flat.append(jnp.zeros((pad,), F32))
    return jnp.concatenate(flat).reshape(-1, LANES), offs


def kernel(x, c, w_ada, b_ada, norm1_g, w_in, b_gate, gmlp_ln_g, gmlp_ln_b, gmlp_ws, gmlp_bs, hg_lb, hg_norm_g, w_branch_gmlp, w_branch_hg, w_out, norm2_g, w_ffn_in, w_ffn_out, final_norm_g, loss_target, m_w_ada, m_b_ada, m_norm1_g, m_w_in, m_b_gate, m_gmlp_ln_g, m_gmlp_ln_b, m_gmlp_ws, m_gmlp_bs, m_hg_lb, m_hg_norm_g, m_w_branch_gmlp, m_w_branch_hg, m_w_out, m_norm2_g, m_w_ffn_in, m_w_ffn_out, m_final_norm_g, v_w_ada, v_b_ada, v_norm1_g, v_w_in, v_b_gate, v_gmlp_ln_g, v_gmlp_ln_b, v_gmlp_ws, v_gmlp_bs, v_hg_lb, v_hg_norm_g, v_w_branch_gmlp, v_w_branch_hg, v_w_out, v_norm2_g, v_w_ffn_in, v_w_ffn_out, v_final_norm_g):
    S, D = x.shape[1], x.shape[2]
    ada_loc = w_ada.shape[2]
    me = 4 * lax.axis_index("x") + 2 * lax.axis_index("y") + lax.axis_index("c")

    c_all = _allgather_small("gather_c", c.reshape(D // LANES, LANES)).reshape(N_DEV, D)
    mod_cols, c_act = _ada_mod(jnp.pad(c_all, ((0, 16 - N_DEV), (0, 0))), w_ada[0])
    mod_all = _allgather_small("gather_mod", mod_cols[:N_DEV].reshape(-1, LANES)).reshape(N_DEV, N_DEV, ada_loc)
    mod = lax.dynamic_index_in_dim(mod_all, me, axis=1, keepdims=False).reshape(1, N_DEV * ada_loc) + b_ada

    def empty_hbm(shape, dtype):
        return pltpu.with_memory_space_constraint(lax.empty(shape, dtype), pltpu.HBM)

    groups = dict(gather_in=dict(keys=["in"], src=[w_in], forward=True),
                  gather_mixer=dict(keys=["bg", "bh", "out"], src=[w_branch_gmlp, w_branch_hg, w_out], forward=False),
                  gather_ffn_in=dict(keys=["fi"], src=[w_ffn_in], forward=True),
                  gather_ffn_out=dict(keys=["fo"], src=[w_ffn_out], forward=False))
    group_of = {k: gname for gname, g in groups.items() for k in g["keys"]}

    def first_hop(gname, after):
        g = groups[gname]
        n = len(g["keys"])
        shards = [a[0].astype(BF16) for a in g["src"]]
        outs = [lax.dynamic_update_slice(lax.empty((N_DEV, *s.shape), BF16), s[None], (me, 0, 0)) for s in shards]
        if g["forward"]:
            *g["hop"], token = _split_start(gname + "_hop1", shards + outs, n * 3, _forward_first_copies(n), after=after)
        else:
            *g["hop"], token = _split_start(gname + "_hop1", shards + outs, n * N_CHIP, _gather_first_copies(n), after=after)
        return token

    def second_hop(gname, after):
        g = groups[gname]
        n = len(g["keys"])
        *g["hop"], token = _split_relay(gname + "_hop2", g["hop"][2], g["hop"][0], g["hop"][1], after,
                                        _forward_first_copies(n), n * 4, _forward_second_copies(n))
        return token

    def finish(gname, after):
        g = groups[gname]
        n = len(g["keys"])
        send_sems, recv_sems, bufs = g["hop"]
        if g["forward"]:
            send_sems, recv_sems, bufs, _ = _split_relay(gname + "_hop3", bufs, send_sems, recv_sems, after,
                                                         _forward_second_copies(n), n, _forward_third_copies(n))
            bufs = _split_wait(gname + "_wait", bufs, send_sems, recv_sems, after, _forward_third_copies(n))
        else:
            send_sems, recv_sems, bufs, _ = _split_relay(gname + "_relay", bufs, send_sems, recv_sems, after,
                                                         _gather_first_copies(n), n * (N_CHIP - 1), _gather_relay_copies(n))
            bufs = _split_wait(gname + "_wait", bufs, send_sems, recv_sems, after, _gather_relay_copies(n))
        g["done"] = dict(zip(g["keys"], bufs[n:]))

    token = first_hop("gather_in", mod_all)
    mod = mod + token[0:1, 0:1]

    def fetch(key, after):
        if key == "in":
            t = second_hop("gather_in", after)
            t = first_hop("gather_mixer", t)
            t = first_hop("gather_ffn_in", t)
            finish("gather_in", t)
        elif key == "fi_early":
            return first_hop("gather_ffn_out", second_hop("gather_ffn_in", after))
        elif "done" not in groups[group_of[key]]:
            finish(group_of[key], after)
        arr = groups[group_of[key]]["done"][key]
        return arr.reshape(-1, D) if key in ("out", "fo") else arr

    p = dict(norm1_g=norm1_g, b_gate=b_gate, ln_g=gmlp_ln_g, ln_b=gmlp_ln_b, ws=gmlp_ws[0], bs=gmlp_bs[0], hg_lb=hg_lb,
             hg_ng=hg_norm_g, norm2_g=norm2_g, final_g=final_norm_g.reshape(1, D))

    in_flight = {}
    c_idx = lax.axis_index("c").astype(jnp.int32).reshape(1)
    my_chip = 2 * lax.axis_index("x") + lax.axis_index("y")

    def scatter_start(name, grads):
        keys = list(grads)
        n = len(keys)
        stacks = [grads[k].reshape(N_DEV, -1, grads[k].shape[-1]) for k in keys]
        lands = [empty_hbm((N_CHIP, *g.shape[1:]), g.dtype) for g in stacks]
        send_sems, recv_sems, bufs, token = _split_start(name + "_d2d", stacks + lands, n * N_CHIP, _to_sibling_copies(n))
        in_flight[name] = dict(keys=keys, stage1=(send_sems, recv_sems, bufs))
        return token

    def scatter_push(name, after):
        f = in_flight[name]
        n = len(f["keys"])
        send_sems, recv_sems, bufs = f["stage1"]
        bufs = _split_wait(name + "_d2d_wait", bufs, send_sems, recv_sems, after, _to_sibling_copies(n))
        sums = [_chip_sum(f"{name}_sum_{k}", bufs[i], bufs[n + i], c_idx) for i, k in enumerate(f["keys"])]
        lands = [empty_hbm((N_CHIP - 1, *s.shape[1:]), s.dtype) for s in sums]
        send_sems, recv_sems, bufs, token = _split_start(name + "_ici", sums + lands, n * (N_CHIP - 1), _to_owner_copies(n))
        f["stage2"] = (send_sems, recv_sems, bufs)
        return token

    grad_x, _, small, dmod = _local_step(x[0], loss_target[0], mod, p, fetch, w_ffn_out.shape[1] * N_DEV, (scatter_start, scatter_push))

    small["b_ada"] = dmod
    packed, offs = _pack([small[k] for k in _SMALL] + [small["loss"]])
    gathered = _allgather_small("gather_small", packed)
    wp = dict(p, b_ada=b_ada)
    ms = dict(b_ada=m_b_ada, norm1_g=m_norm1_g, b_gate=m_b_gate, ln_g=m_gmlp_ln_g, ln_b=m_gmlp_ln_b, ws=m_gmlp_ws, bs=m_gmlp_bs,
              hg_lb=m_hg_lb, hg_ng=m_hg_norm_g, norm2_g=m_norm2_g, final_g=m_final_norm_g)
    vs = dict(b_ada=v_b_ada, norm1_g=v_norm1_g, b_gate=v_b_gate, ln_g=v_gmlp_ln_g, ln_b=v_gmlp_ln_b, ws=v_gmlp_ws, bs=v_gmlp_bs,
              hg_lb=v_hg_lb, hg_ng=v_hg_norm_g, norm2_g=v_norm2_g, final_g=v_final_norm_g)
    w_sm, _ = _pack([wp[k] for k in _SMALL])
    m_sm, _ = _pack([ms[k] for k in _SMALL])
    v_sm, _ = _pack([vs[k] for k in _SMALL])
    t_tail = scatter_push("scatter_proj_in_b", gathered)
    sm_out = _small_update(gathered, w_sm, m_sm, v_sm, t_tail)
    shapes = dict(b_ada=b_ada.shape, norm1_g=norm1_g.shape, b_gate=b_gate.shape, ln_g=gmlp_ln_g.shape, ln_b=gmlp_ln_b.shape,
                  ws=gmlp_ws.shape, bs=gmlp_bs.shape, hg_lb=hg_lb.shape, hg_ng=hg_norm_g.shape, norm2_g=norm2_g.shape,
                  final_g=final_norm_g.shape)

    def unpack(arr, k):
        i = _SMALL.index(k)
        n = math.prod(shapes[k])
        return arr.reshape(-1)[offs[i]:offs[i] + n].reshape(shapes[k])

    loss = sm_out[0].reshape(-1)[offs[len(_SMALL)]]

    dmod_all = gathered.reshape(N_DEV, -1)[:, offs[0]:offs[0] + N_DEV * ada_loc]
    dmod_loc = lax.dynamic_slice_in_dim(dmod_all, me * ada_loc, ada_loc, axis=1)
    ca_t = jnp.pad(c_act[:N_DEV].T, ((0, 0), (0, LANES - N_DEV))).astype(BF16)
    dm_p = jnp.pad(dmod_loc, ((0, LANES - N_DEV), (0, 0))).astype(BF16)
    tm_a = _tile(D, 512)
    g_ada = _matmul(
        "ada_dw", ca_t, dm_p, dims=_NN, grid_mnk=(D // tm_a, 1, 1), tiles=(tm_a, ada_loc),
        a_spec=pl.BlockSpec((tm_a, LANES), lambda i, j, k: (i, 0)), b_spec=pl.BlockSpec((LANES, ada_loc), lambda i, j, k: (0, 0)),
        out_shapes=[jax.ShapeDtypeStruct((1, D, ada_loc), F32)], out_specs=[pl.BlockSpec((None, tm_a, ada_loc), lambda i, j, k: (0, i, 0))],
        epilogue=_store(F32), after=t_tail)[0]

    upd = {"w_ada": _adamw("adamw_w_ada", w_ada[0], m_w_ada[0], v_w_ada[0], [g_ada])}
    big_w = dict(w_in=(w_in, m_w_in, v_w_in, "w_in"), bg=(w_branch_gmlp, m_w_branch_gmlp, v_w_branch_gmlp, "w_branch_gmlp"),
                 bh=(w_branch_hg, m_w_branch_hg, v_w_branch_hg, "w_branch_hg"), out=(w_out, m_w_out, v_w_out, "w_out"),
                 fi=(w_ffn_in, m_w_ffn_in, v_w_ffn_in, "w_ffn_in"), fo=(w_ffn_out, m_w_ffn_out, v_w_ffn_out, "w_ffn_out"))
    after = upd["w_ada"][1]
    for name in ("scatter_ffn", "scatter_mixer", "scatter_proj_in_a", "scatter_proj_in_b"):
        keys = in_flight[name]["keys"]
        n = len(keys)
        send_sems, recv_sems, bufs = in_flight[name]["stage2"]
        bufs = _split_wait(name + "_ici_wait", bufs, send_sems, recv_sems, after, _to_owner_copies(n))
        for i, k in enumerate(keys):
            parts = [lax.dynamic_index_in_dim(bufs[i], my_chip, axis=0, keepdims=True), bufs[n + i]]
            if k in big_w:
                wt, mt, vt, out_name = big_w[k]
                upd[out_name] = _adamw("adamw_" + out_name, wt[0], mt[0], vt[0], parts)
            else:
                wt, mt, vt, out_name = big_w["w_in"]
                upd[out_name] = _adamw("adamw_" + k, wt[0], mt[0], vt[0], parts, row0=0 if k == "w_in_a" else parts[0].shape[1],
                                       into=upd.get(out_name))
            after = upd[out_name][1]

    order = ("w_ada", "b_ada", "norm1_g", "w_in", "b_gate", "ln_g", "ln_b", "ws", "bs", "hg_lb", "hg_ng", "w_branch_gmlp", "w_branch_hg",
             "w_out", "norm2_g", "w_ffn_in", "w_ffn_out", "final_g")
    outs = [loss, grad_x[None]]
    for idx in range(4):
        for k in order:
            outs.append(upd[k][idx][None] if k in upd else unpack(sm_out[idx], k))
    return tuple(outs)
```

```python
import functools
import math

import jax
import jax.numpy as jnp
from jax import lax
from jax.experimental import pallas as pl
from jax.experimental.pallas import tpu as pltpu

F32 = jnp.float32
BF16 = jnp.bfloat16
N_DEV = 8
EPS = 1e-6
LANES = 128
HG_DK = 128
HG_CHUNK = 64
HG_MID = HG_CHUNK // 2 - 1
EXP_CLAMP = 80.0
VMEM_LIMIT = 48 * 1024 * 1024
ADAM_LR, ADAM_B1, ADAM_B2, ADAM_EPS, ADAM_WD, ADAM_STEP = 0.001, 0.9, 0.999, 1e-08, 0.01, 10
MESH = pl.DeviceIdType.MESH

_NN = (((1,), (0,)), ((), ()))
_NT = (((1,), (1,)), ((), ()))
_TN = (((0,), (0,)), ((), ()))


def _dot(a, b, dims=_NN):
    return lax.dot_general(a.astype(BF16), b.astype(BF16), dims, preferred_element_type=F32)


def _tile(n, target, mult=LANES):
    best = None
    for t in range(mult, min(n, target) + 1, mult):
        if n % t == 0:
            best = t
    return n if best is None else best


def _cparams(sem):
    return pltpu.CompilerParams(dimension_semantics=sem, vmem_limit_bytes=VMEM_LIMIT)


def _sigmoid(x):
    return 1.0 / (1.0 + jnp.exp(-x))


def _gelu_parts(x):
    k0 = math.sqrt(2.0 / math.pi)
    x2 = x * x
    t = jnp.tanh(k0 * (x + 0.044715 * x * x2))
    g = 0.5 * x * (1.0 + t)
    dg = 0.5 * (1.0 + t) + 0.5 * x * (1.0 - t * t) * (k0 * (1.0 + 3.0 * 0.044715 * x2))
    return g, dg


def _split3(x):
    h = x.astype(BF16)
    r = x - h.astype(F32)
    m = r.astype(BF16)
    lo = (r - m.astype(F32)).astype(BF16)
    return h, m, lo


def _ones_dot(mat01, x):
    h, m, lo = _split3(x)
    d = functools.partial(lax.dot_general, dimension_numbers=_NN, preferred_element_type=F32)
    return d(mat01, h) + d(mat01, m) + d(mat01, lo)


def _matmul(name, a, b, *, dims, grid_mnk, tiles, a_spec, b_spec, extras=(), extra_specs=(), out_shapes, out_specs, epilogue, after=None,
            sem=None):
    gm, gn, nk = grid_mnk
    tm, tn = tiles
    n_ex, n_out = len(extras), len(out_shapes)
    held = [] if after is None else [after]

    def body(*refs):
        a_ref, b_ref = refs[0], refs[1]
        ex = refs[2:2 + n_ex]
        outs = refs[2 + n_ex + len(held):2 + n_ex + len(held) + n_out]
        more = () if sem is None else (pl.program_id(0) == 0,)
        if nk == 1:
            epilogue(lax.dot_general(a_ref[...], b_ref[...], dims, preferred_element_type=F32), ex, outs, *more)
            return
        acc = refs[-1]
        k = pl.program_id(2)

        @pl.when(k == 0)
        def _():
            acc[...] = jnp.zeros_like(acc)

        acc[...] += lax.dot_general(a_ref[...], b_ref[...], dims, preferred_element_type=F32)

        @pl.when(k == nk - 1)
        def _():
            epilogue(acc[...], ex, outs, *more)

    return pl.pallas_call(
        body, name=name, grid=(gm, gn, nk), in_specs=[a_spec, b_spec, *extra_specs] + [pl.BlockSpec(memory_space=pl.ANY)] * len(held),
        out_specs=list(out_specs), out_shape=list(out_shapes), scratch_shapes=[] if nk == 1 else [pltpu.VMEM((tm, tn), F32)],
        compiler_params=_cparams(sem or ("parallel", "parallel", "arbitrary")),
    )(a, b, *extras, *held)


def _store(dtype):
    def ep(acc, ex, outs):
        outs[0][...] = acc.astype(dtype)
    return ep


def _mm_nn_stacked(name, a, wg, *, tm, tn, tk, out_dtype=F32, extras=(), extra_specs=(), out_shapes=None, out_specs=None, epilogue=None,
                   after=None):
    M, K = a.shape
    _, _, nloc = wg.shape
    N = nloc * N_DEV
    q = nloc // tn
    if out_shapes is None:
        out_shapes = [jax.ShapeDtypeStruct((M, N), out_dtype)]
        out_specs = [pl.BlockSpec((tm, tn), lambda i, j, k: (i, j))]
        epilogue = _store(out_dtype)
    return _matmul(
        name, a, wg, dims=_NN, grid_mnk=(M // tm, N // tn, K // tk), tiles=(tm, tn),
        a_spec=pl.BlockSpec((tm, tk), lambda i, j, k: (i, k)),
        b_spec=pl.BlockSpec((None, tk, tn), lambda i, j, k: (j // q, k, j % q)),
        extras=extras, extra_specs=extra_specs, out_shapes=out_shapes, out_specs=out_specs, epilogue=epilogue, after=after)


def _mm_nt_stacked(name, a_spec, a, wg, *, M, tm, tn, tk, out_dtype=F32, after=None, extras=(), extra_specs=(), out_shapes=None,
                   out_specs=None, epilogue=None, sem=None):
    _, Kw, nloc = wg.shape
    q = nloc // tk
    single = out_shapes is None
    if single:
        out_shapes = [jax.ShapeDtypeStruct((M, Kw), out_dtype)]
        out_specs = [pl.BlockSpec((tm, tn), lambda i, j, k: (i, j))]
        epilogue = _store(out_dtype)
    res = _matmul(
        name, a, wg, dims=_NT, grid_mnk=(M // tm, Kw // tn, (nloc * N_DEV) // tk), tiles=(tm, tn),
        a_spec=a_spec, b_spec=pl.BlockSpec((None, tn, tk), lambda i, j, k: (k // q, j, k % q)),
        extras=extras, extra_specs=extra_specs, out_shapes=out_shapes, out_specs=out_specs, epilogue=epilogue, after=after, sem=sem)
    return res[0] if single else res


def _mm_tn(name, a, b, b_spec, *, Mo, No, S, tm, tn, tk, stacked_nloc=None, after=None, a_off=0):
    if stacked_nloc is None:
        out_shape = jax.ShapeDtypeStruct((Mo, No), BF16)
        out_spec = pl.BlockSpec((tm, tn), lambda i, j, k: (i, j))
    else:
        q = stacked_nloc // tn
        out_shape = jax.ShapeDtypeStruct((N_DEV, Mo, stacked_nloc), BF16)
        out_spec = pl.BlockSpec((None, tm, tn), lambda i, j, k: (j // q, i, j % q))
    return _matmul(
        name, a, b, dims=_TN, grid_mnk=(Mo // tm, No // tn, S // tk), tiles=(tm, tn),
        a_spec=pl.BlockSpec((tk, tm), lambda i, j, k: (k, i + a_off)), b_spec=b_spec,
        out_shapes=[out_shape], out_specs=[out_spec], epilogue=_store(BF16), after=after)[0]


def _norm_mod(name, x, g, sc, sh):
    S, D = x.shape
    tm = _tile(S, 256, 8)

    def body(x_ref, g_ref, sc_ref, sh_ref, h_ref):
        xv = x_ref[...]
        r = lax.rsqrt(jnp.mean(xv * xv, axis=-1, keepdims=True) + EPS)
        h = (xv * r) * g_ref[...]
        h_ref[...] = (h * (1.0 + sc_ref[...]) + sh_ref[...]).astype(BF16)

    row = pl.BlockSpec((tm, D), lambda i: (i, 0))
    vec = pl.BlockSpec((1, D), lambda i: (0, 0))
    return pl.pallas_call(body, name=name, grid=(S // tm,), in_specs=[row, vec, vec, vec], out_specs=row,
                          out_shape=jax.ShapeDtypeStruct((S, D), BF16), compiler_params=_cparams(("parallel",)))(x, g, sc, sh)


def _norm_mod_bwd_rows(first, dh_v, x_ref, g_ref, sc_ref, dres_ref, dx_ref, vec_ref, o_ref=None, gt_ref=None, do_ref=None):
    @pl.when(first)
    def _():
        vec_ref[...] = jnp.zeros_like(vec_ref)

    xv, gv = x_ref[...], g_ref[...]
    r = lax.rsqrt(jnp.mean(xv * xv, axis=-1, keepdims=True) + EPS)
    xn = xv * r
    one_sc = 1.0 + sc_ref[...]
    vec_ref[0:1, :] += jnp.sum(dh_v, axis=0, keepdims=True)
    vec_ref[1:2, :] += jnp.sum(dh_v * (xn * gv), axis=0, keepdims=True)
    vec_ref[2:3, :] += jnp.sum(dh_v * one_sc * xn, axis=0, keepdims=True)
    dxn = dh_v * one_sc * gv
    dx = dres_ref[...] + r * (dxn - xn * jnp.mean(dxn * xn, axis=-1, keepdims=True))
    dx_ref[...] = dx
    if o_ref is not None:
        vec_ref[3:4, :] += jnp.sum(dx * o_ref[...], axis=0, keepdims=True)
        do_ref[...] = (dx * gt_ref[...]).astype(BF16)


def _norm_mod_bwd(name, dh, x, g, sc, dres, o=None, gt=None):
    S, D = x.shape
    tm = _tile(S, 256, 8)
    gated = o is not None

    def body(*refs):
        if gated:
            dh_ref, x_ref, g_ref, sc_ref, dres_ref, o_ref, gt_ref, dx_ref, vec_ref, do_ref = refs
        else:
            dh_ref, x_ref, g_ref, sc_ref, dres_ref, dx_ref, vec_ref = refs
            o_ref = gt_ref = do_ref = None
        _norm_mod_bwd_rows(pl.program_id(0) == 0, dh_ref[...], x_ref, g_ref, sc_ref, dres_ref, dx_ref, vec_ref, o_ref, gt_ref, do_ref)

    row = pl.BlockSpec((tm, D), lambda i: (i, 0))
    vec = pl.BlockSpec((1, D), lambda i: (0, 0))
    acc = pl.BlockSpec((8, D), lambda i: (0, 0))
    ins = [dh, x, g, sc, dres] + ([o, gt] if gated else [])
    in_specs = [row, row, vec, vec, row] + ([row, vec] if gated else [])
    out_shape = [jax.ShapeDtypeStruct((S, D), F32), jax.ShapeDtypeStruct((8, D), F32)]
    out_specs = [row, acc]
    if gated:
        out_shape.append(jax.ShapeDtypeStruct((S, D), BF16))
        out_specs.append(row)
    return pl.pallas_call(body, name=name, grid=(S // tm,), in_specs=in_specs, out_specs=out_specs, out_shape=out_shape,
                          compiler_params=_cparams(("arbitrary",)))(*ins)


def _loss_head(x3, tgt, gf, o2, gt2):
    S, D = x3.shape
    tm = _tile(S, 256, 8)

    def body(x_ref, t_ref, g_ref, o_ref, gt_ref, dx_ref, do_ref, vec_ref):
        i = pl.program_id(0)

        @pl.when(i == 0)
        def _():
            vec_ref[...] = jnp.zeros_like(vec_ref)

        xv, gv = x_ref[...], g_ref[...]
        r = lax.rsqrt(jnp.mean(xv * xv, axis=-1, keepdims=True) + EPS)
        xn = xv * r
        e = xn * gv - t_ref[...]
        tok = 0.5 * jnp.mean(e * e, axis=-1, keepdims=True)
        vec_ref[0:1, :] += jnp.broadcast_to(jnp.sum(tok, axis=0, keepdims=True), (1, D))
        dy = e * (1.0 / D)
        vec_ref[1:2, :] += jnp.sum(dy * xn, axis=0, keepdims=True)
        dxn = dy * gv
        dx = r * (dxn - xn * jnp.mean(dxn * xn, axis=-1, keepdims=True))
        dx_ref[...] = dx
        vec_ref[2:3, :] += jnp.sum(dx * o_ref[...], axis=0, keepdims=True)
        do_ref[...] = (dx * gt_ref[...]).astype(BF16)

    row = pl.BlockSpec((tm, D), lambda i: (i, 0))
    vec = pl.BlockSpec((1, D), lambda i: (0, 0))
    return pl.pallas_call(
        body, name="loss_head", grid=(S // tm,), in_specs=[row, row, vec, row, vec],
        out_specs=[row, row, pl.BlockSpec((8, D), lambda i: (0, 0))],
        out_shape=[jax.ShapeDtypeStruct((S, D), F32), jax.ShapeDtypeStruct((S, D), BF16), jax.ShapeDtypeStruct((8, D), F32)],
        compiler_params=_cparams(("arbitrary",)))(x3, tgt, gf, o2, gt2)


def _ffn_in_swiglu(h, wg):
    S, D = h.shape
    _, _, tf = wg.shape
    nf = N_DEV // 2
    F = nf * tf
    tm = _tile(S, 256, 16)

    def body(h_ref, wa_ref, wu_ref, hf_ref, fac_ref):
        hv = h_ref[...]
        a = lax.dot_general(hv, wa_ref[...], _NN, preferred_element_type=F32)
        up = lax.dot_general(hv, wu_ref[...], _NN, preferred_element_type=F32)
        sa = _sigmoid(a)
        silu = a * sa
        hf_ref[...] = (silu * up).astype(BF16)
        fac_ref[0] = (up * (sa * (1.0 + a * (1.0 - sa)))).astype(BF16)
        fac_ref[1] = silu.astype(BF16)

    return pl.pallas_call(
        body, name="ffn_in_swiglu", grid=(nf, S // tm),
        in_specs=[pl.BlockSpec((tm, D), lambda j, i: (i, 0)), pl.BlockSpec((None, D, tf), lambda j, i: (j, 0, 0)),
                  pl.BlockSpec((None, D, tf), lambda j, i: (j + nf, 0, 0))],
        out_specs=[pl.BlockSpec((tm, tf), lambda j, i: (i, j)), pl.BlockSpec((2, tm, tf), lambda j, i: (0, i, j))],
        out_shape=[jax.ShapeDtypeStruct((S, F), BF16), jax.ShapeDtypeStruct((2, S, F), BF16)],
        compiler_params=_cparams(("parallel", "parallel")))(h, wg, wg)


def _colsum2(dg2):
    _, S, D = dg2.shape
    tm = _tile(S, 256, 16)

    def body(x_ref, o_ref):
        @pl.when(pl.program_id(0) == 0)
        def _():
            o_ref[...] = jnp.zeros_like(o_ref)

        o_ref[0:1, :] += jnp.sum(x_ref[0].astype(F32), axis=0, keepdims=True)
        o_ref[1:2, :] += jnp.sum(x_ref[1].astype(F32), axis=0, keepdims=True)

    return pl.pallas_call(body, name="gate_bias_grad", grid=(S // tm,), in_specs=[pl.BlockSpec((2, tm, D), lambda i: (0, i, 0))],
                          out_specs=pl.BlockSpec((2, D), lambda i: (0, 0)), out_shape=jax.ShapeDtypeStruct((2, D), F32),
                          compiler_params=_cparams(("arbitrary",)))(dg2)


def _gmlp_common(u_ref, v_ref, lg_ref, lb_ref, ws_ref, bsb_ref, G, T, Dg):
    ug, dug = _gelu_parts(u_ref[...])
    vg, dvg = _gelu_parts(v_ref[...])
    mu = jnp.mean(vg, axis=-1, keepdims=True)
    vc = vg - mu
    rstd = lax.rsqrt(jnp.mean(vc * vc, axis=-1, keepdims=True) + EPS)
    vhat = vc * rstd
    vn = vhat * lg_ref[...] + lb_ref[...]
    row = lax.broadcasted_iota(jnp.int32, (T, T), 0)
    col = lax.broadcasted_iota(jnp.int32, (T, T), 1)
    tril = row >= col
    s = []
    for g in range(G):
        w = jnp.where(tril, ws_ref[g], 0.0)
        s.append(_dot(w, vn[:, g * Dg:(g + 1) * Dg]) + bsb_ref[g])
    return ug, dug, dvg, rstd, vhat, vn, tril, s


def _gmlp_fwd(z, ln_g, ln_b, ws, bsb, GW):
    S = z.shape[0]
    G, T, _ = ws.shape
    Dg = GW // G

    def body(u_ref, v_ref, lg_ref, lb_ref, ws_ref, bsb_ref, ya_ref):
        ug, _, _, _, _, _, _, s = _gmlp_common(u_ref, v_ref, lg_ref, lb_ref, ws_ref, bsb_ref, G, T, Dg)
        for g in range(G):
            sl = slice(g * Dg, (g + 1) * Dg)
            ya_ref[:, sl] = (ug[:, sl] * s[g]).astype(BF16)

    vec = pl.BlockSpec((1, GW), lambda c: (0, 0))
    return pl.pallas_call(
        body, name="gmlp_fwd", grid=(S // T,),
        in_specs=[pl.BlockSpec((T, GW), lambda c: (c, 0)), pl.BlockSpec((T, GW), lambda c: (c, 1)), vec, vec,
                  pl.BlockSpec((G, T, T), lambda c: (0, 0, 0)), pl.BlockSpec((G, T, Dg), lambda c: (0, 0, 0))],
        out_specs=pl.BlockSpec((T, GW), lambda c: (c, 0)), out_shape=jax.ShapeDtypeStruct((S, GW), BF16),
        compiler_params=_cparams(("parallel",)))(z, z, ln_g, ln_b, ws, bsb)


def _gmlp_bwd(z, dya, ln_g, ln_b, ws, bsb, GW):
    S = z.shape[0]
    G, T, _ = ws.shape
    Dg = GW // G
    nc = S // T

    def body(u_ref, v_ref, dya_ref, lg_ref, lb_ref, ws_ref, bsb_ref, dz_ref, dln_ref, dws_ref, dbs_ref, dbs_acc, dvh):
        c = pl.program_id(0)

        @pl.when(c == 0)
        def _():
            dln_ref[...] = jnp.zeros_like(dln_ref)
            dws_ref[...] = jnp.zeros_like(dws_ref)
            dbs_acc[...] = jnp.zeros_like(dbs_acc)

        ug, dug, dvg, rstd, vhat, vn, tril, s = _gmlp_common(u_ref, v_ref, lg_ref, lb_ref, ws_ref, bsb_ref, G, T, Dg)
        dya_v = dya_ref[...]
        for g in range(G):
            sl = slice(g * Dg, (g + 1) * Dg)
            dy_g = dya_v[:, sl]
            dz_ref[:, sl] = (dy_g * s[g] * dug[:, sl]).astype(BF16)
            ds = dy_g * ug[:, sl]
            dbs_acc[g] += ds
            w = jnp.where(tril, ws_ref[g], 0.0)
            dvn_g = _dot(w, ds, _TN)
            dws_ref[g] += jnp.where(tril, _dot(ds, vn[:, sl], _NT), 0.0)
            dln_ref[0:1, sl] += jnp.sum(dvn_g * vhat[:, sl], axis=0, keepdims=True)
            dln_ref[1:2, sl] += jnp.sum(dvn_g, axis=0, keepdims=True)
            dvh[:, sl] = dvn_g * lg_ref[:, sl]
        dvhat = dvh[...]
        m1 = jnp.mean(dvhat, axis=-1, keepdims=True)
        m2 = jnp.mean(dvhat * vhat, axis=-1, keepdims=True)
        dz_ref[:, GW:2 * GW] = (rstd * (dvhat - m1 - vhat * m2) * dvg).astype(BF16)

        @pl.when(c == nc - 1)
        def _():
            for g in range(G):
                dbs_ref[g] = jnp.sum(dbs_acc[g], axis=-1, keepdims=True)

    vec = pl.BlockSpec((1, GW), lambda c: (0, 0))
    return pl.pallas_call(
        body, name="gmlp_bwd", grid=(nc,),
        in_specs=[pl.BlockSpec((T, GW), lambda c: (c, 0)), pl.BlockSpec((T, GW), lambda c: (c, 1)),
                  pl.BlockSpec((T, GW), lambda c: (c, 0)), vec, vec,
                  pl.BlockSpec((G, T, T), lambda c: (0, 0, 0)), pl.BlockSpec((G, T, Dg), lambda c: (0, 0, 0))],
        out_specs=[pl.BlockSpec((T, 2 * GW), lambda c: (c, 0)), pl.BlockSpec((8, GW), lambda c: (0, 0)),
                   pl.BlockSpec((G, T, T), lambda c: (0, 0, 0)), pl.BlockSpec((G, T, 1), lambda c: (0, 0, 0))],
        out_shape=[jax.ShapeDtypeStruct((S, 2 * GW), BF16), jax.ShapeDtypeStruct((8, GW), F32),
                   jax.ShapeDtypeStruct((G, T, T), F32), jax.ShapeDtypeStruct((G, T, 1), F32)],
        scratch_shapes=[pltpu.VMEM((G, T, Dg), F32), pltpu.VMEM((T, GW), F32)],
        compiler_params=_cparams(("arbitrary",)))(z, z, dya, ln_g, ln_b, ws, bsb)


def _hg_common(q_ref, f_ref, hlb_ref):
    C = HG_CHUNK
    a = hlb_ref[...]
    lb = _sigmoid(a[0:1, :] - a[1:2, :])
    sig = _sigmoid(f_ref[...])
    f = lb + (1.0 - lb) * sig
    lf = jnp.log(f)
    kk = 1.0 - f
    q = q_ref[...]
    sq = _sigmoid(q)
    qa = q * sq
    row = lax.broadcasted_iota(jnp.int32, (C, C), 0)
    col = lax.broadcasted_iota(jnp.int32, (C, C), 1)
    tril = row >= col
    b = _ones_dot(tril.astype(BF16), lf)
    bm = b[HG_MID:HG_MID + 1, :]
    bl = b[C - 1:C, :]
    e_b = jnp.exp(b)
    e_qm = jnp.exp(jnp.minimum(b - bm, EXP_CLAMP))
    e_km = jnp.exp(jnp.minimum(bm - b, EXP_CLAMP))
    e_kl = jnp.exp(bl - b)
    return dict(lb=lb, sig=sig, f=f, kk=kk, q=q, sq=sq, qa=qa, tril=tril, e_b=e_b, e_qm=e_qm, e_km=e_km, e_kl=e_kl,
                e_l=jnp.exp(bl), qh=qa * e_b, qt=qa * e_qm, kt=kk * e_km, kh=kk * e_kl)


def _hg_fwd(z, hg_lb, ng, HW):
    S = z.shape[0]
    C, H, dk = HG_CHUNK, HW // HG_DK, HG_DK
    nc = S // C

    def body(q_ref, f_ref, i_ref, og_ref, hlb_ref, ng_ref, yb_ref, o_ref, st_ref, state):
        @pl.when(pl.program_id(0) == 0)
        def _():
            state[...] = jnp.zeros_like(state)

        t = _hg_common(q_ref, f_ref, hlb_ref)
        iv = i_ref[...]
        for h in range(H):
            sl = slice(h * dk, (h + 1) * dk)
            st = state[h]
            st_ref[h] = st
            a = jnp.where(t["tril"], _dot(t["qt"][:, sl], t["kt"][:, sl], _NT), 0.0)
            o_h = _dot(a, iv[:, sl]) + _dot(t["qh"][:, sl], st, _NT)
            state[h] = st * t["e_l"][:, sl] + _dot(iv[:, sl], t["kh"][:, sl], _TN)
            o_ref[:, sl] = o_h
            rr = lax.rsqrt(jnp.mean(o_h * o_h, axis=-1, keepdims=True) + EPS)
            og = og_ref[:, sl]
            yb_ref[:, sl] = (o_h * rr * ng_ref[:, sl] * (og * _sigmoid(og))).astype(BF16)

    def col(k):
        return pl.BlockSpec((C, HW), lambda c: (c, k))

    base = 2
    return pl.pallas_call(
        body, name="hgrn_fwd", grid=(nc,),
        in_specs=[col(base), col(base + 1), col(base + 2), col(base + 3),
                  pl.BlockSpec((2, HW), lambda c: (0, 0)), pl.BlockSpec((1, HW), lambda c: (0, 0))],
        out_specs=[pl.BlockSpec((C, HW), lambda c: (c, 0)), pl.BlockSpec((C, HW), lambda c: (c, 0)),
                   pl.BlockSpec((None, H, dk, dk), lambda c: (c, 0, 0, 0))],
        out_shape=[jax.ShapeDtypeStruct((S, HW), BF16), jax.ShapeDtypeStruct((S, HW), F32),
                   jax.ShapeDtypeStruct((nc, H, dk, dk), F32)],
        scratch_shapes=[pltpu.VMEM((H, dk, dk), F32)],
        compiler_params=_cparams(("arbitrary",)))(z, z, z, z, hg_lb, ng)


def _hg_bwd(z, o, states, dyb, hg_lb, ng, HW, dz_head, dz_tail):
    S = z.shape[0]
    C, H, dk = HG_CHUNK, HW // HG_DK, HG_DK
    nc = S // C
    B0 = dz_head.shape[1]
    DT = dz_tail.shape[2]
    INW = B0 + 4 * HW + 2 * DT

    def body(q_ref, f_ref, i_ref, og_ref, o_ref, st_ref, stn_ref, dyb_ref, hlb_ref, ng_ref, head_ref, tail_ref,
             dzf_ref, dng_ref, dhlb_ref, dstate, cross, dqa_buf, dkk_buf, db_buf, dlb_acc):
        c = pl.program_id(0)
        dzf_ref[:, 0:B0] = head_ref[...]
        dzf_ref[:, B0 + 4 * HW:B0 + 4 * HW + DT] = tail_ref[0]
        dzf_ref[:, B0 + 4 * HW + DT:INW] = tail_ref[1]
        dz_ref = dzf_ref.at[:, B0:B0 + 4 * HW]

        @pl.when(c == 0)
        def _():
            dstate[...] = jnp.zeros_like(dstate)
            dlb_acc[...] = jnp.zeros_like(dlb_acc)
            dng_ref[...] = jnp.zeros_like(dng_ref)

        def r16(v):
            return v.astype(BF16).astype(F32)

        t = _hg_common(q_ref, f_ref, hlb_ref)
        iv = i_ref[...]
        for h in range(H):
            sl = slice(h * dk, (h + 1) * dk)
            o_h, og, dyb_h, ng_h = o_ref[:, sl], og_ref[:, sl], dyb_ref[:, sl], ng_ref[:, sl]
            sg = _sigmoid(og)
            silu_og = og * sg
            rr = lax.rsqrt(jnp.mean(o_h * o_h, axis=-1, keepdims=True) + EPS)
            on = o_h * rr
            dng_ref[0:1, sl] += jnp.sum(dyb_h * on * silu_og, axis=0, keepdims=True)
            dz_ref[:, 3 * HW + h * dk:3 * HW + (h + 1) * dk] = (dyb_h * on * ng_h * (sg * (1.0 + og * (1.0 - sg)))).astype(BF16)
            don = dyb_h * ng_h * silu_og
            do_h = rr * (don - on * jnp.mean(don * on, axis=-1, keepdims=True))

            qt, kt, qh, kh, iv_h = t["qt"][:, sl], t["kt"][:, sl], t["qh"][:, sl], t["kh"][:, sl], iv[:, sl]
            a = jnp.where(t["tril"], _dot(qt, kt, _NT), 0.0)
            da = jnp.where(t["tril"], _dot(do_h, iv_h, _NT), 0.0)
            st, dst = st_ref[h], dstate[h]
            cross[:, sl] = jnp.sum(dst * stn_ref[h], axis=0, keepdims=True)
            dqh = _dot(do_h, st)
            dstate[h] = _dot(do_h, qh, _TN) + dst * t["e_l"][:, sl]
            div = _dot(a, do_h, _TN) + _dot(kh, dst, _NT)
            dkh = _dot(iv_h, dst)
            dqt = _dot(da, kt)
            dkt = _dot(da, qt, _TN)
            dz_ref[:, 2 * HW + h * dk:2 * HW + (h + 1) * dk] = div.astype(BF16)
            dqa_buf[:, sl] = dqh * t["e_b"][:, sl] + dqt * t["e_qm"][:, sl]
            dkk_buf[:, sl] = dkt * t["e_km"][:, sl] + dkh * t["e_kl"][:, sl]
            db_buf[:, sl] = r16(qt) * dqt - r16(kt) * dkt + r16(qh) * dqh - r16(kh) * dkh

        dqa, dkk = dqa_buf[...], dkk_buf[...]
        triu = jnp.logical_not(t["tril"]) | (lax.broadcasted_iota(jnp.int32, (C, C), 0) == lax.broadcasted_iota(jnp.int32, (C, C), 1))
        dlf = _ones_dot(triu.astype(BF16), db_buf[...]) + cross[...]
        df = dlf / t["f"] - dkk
        sig, lb = t["sig"], t["lb"]
        dz_ref[:, HW:2 * HW] = (df * (1.0 - lb) * sig * (1.0 - sig)).astype(BF16)
        dlb_acc[...] += jnp.sum(df * (1.0 - sig), axis=0, keepdims=True)
        q, sq = t["q"], t["sq"]
        dz_ref[:, 0:HW] = (dqa * (sq * (1.0 + q * (1.0 - sq)))).astype(BF16)

        @pl.when(c == nc - 1)
        def _():
            da0 = dlb_acc[...] * lb * (1.0 - lb)
            dhlb_ref[0:1, :] = da0
            dhlb_ref[1:2, :] = -da0

    def col(k):
        return pl.BlockSpec((C, HW), lambda c: (nc - 1 - c, k))

    base = 2
    return pl.pallas_call(
        body, name="hgrn_bwd", grid=(nc,),
        in_specs=[col(base), col(base + 1), col(base + 2), col(base + 3), col(0),
                  pl.BlockSpec((None, H, dk, dk), lambda c: (nc - 1 - c, 0, 0, 0)),
                  pl.BlockSpec((None, H, dk, dk), lambda c: (jnp.minimum(nc - c, nc - 1), 0, 0, 0)), col(0),
                  pl.BlockSpec((2, HW), lambda c: (0, 0)), pl.BlockSpec((1, HW), lambda c: (0, 0)),
                  pl.BlockSpec((C, B0), lambda c: (nc - 1 - c, 0)), pl.BlockSpec((2, C, DT), lambda c: (0, nc - 1 - c, 0))],
        out_specs=[pl.BlockSpec((C, INW), lambda c: (nc - 1 - c, 0)), pl.BlockSpec((8, HW), lambda c: (0, 0)),
                   pl.BlockSpec((2, HW), lambda c: (0, 0))],
        out_shape=[jax.ShapeDtypeStruct((S, INW), BF16), jax.ShapeDtypeStruct((8, HW), F32), jax.ShapeDtypeStruct((2, HW), F32)],
        scratch_shapes=[pltpu.VMEM((H, dk, dk), F32), pltpu.VMEM((1, HW), F32), pltpu.VMEM((C, HW), F32), pltpu.VMEM((C, HW), F32),
                        pltpu.VMEM((C, HW), F32), pltpu.VMEM((1, HW), F32)],
        compiler_params=_cparams(("arbitrary",)))(z, z, z, z, o, states, states, dyb, hg_lb, ng, dz_head, dz_tail)


def _position():
    x, y, c = lax.axis_index("x"), lax.axis_index("y"), lax.axis_index("c")
    return x, y, c, 4 * x + 2 * y + c


def _flip(x, y, c, k):
    return (1 - x if k & 4 else x, 1 - y if k & 2 else y, 1 - c if k & 1 else c)


def _allgather_small(name, v):
    R, L = v.shape

    def body(v_ref, out_ref, send_sems, recv_sems):
        x, y, c, me = _position()
        out_ref[me] = v_ref[...]
        copies = []
        for k in range(1, N_DEV):
            cp = pltpu.make_async_remote_copy(src_ref=v_ref, dst_ref=out_ref.at[me], send_sem=send_sems.at[k - 1],
                                              recv_sem=recv_sems.at[k - 1], device_id=_flip(x, y, c, k), device_id_type=MESH)
            cp.start()
            copies.append(cp)
        for cp in copies:
            cp.wait()

    return pl.pallas_call(
        body, name=name, out_shape=jax.ShapeDtypeStruct((N_DEV, R, L), v.dtype),
        in_specs=[pl.BlockSpec(memory_space=pltpu.VMEM)], out_specs=pl.BlockSpec(memory_space=pltpu.VMEM),
        scratch_shapes=[pltpu.SemaphoreType.DMA((N_DEV - 1,)), pltpu.SemaphoreType.DMA((N_DEV - 1,))],
        compiler_params=pltpu.CompilerParams(vmem_limit_bytes=VMEM_LIMIT),
    )(v)


def _allgather_hbm(name, shards):
    n = len(shards)

    def body(*refs):
        ins, outs = refs[:n], refs[n:2 * n]
        send_sems, recv_sems, local_sems = refs[2 * n:]
        x, y, c, me = _position()
        sibling = (x, y, 1 - c)
        chips = [(1 - x, y), (x, 1 - y), (1 - x, 1 - y)]

        def slot(px, py, pc):
            return 4 * px + 2 * py + pc

        def copy(w, k, block, to, src=None):
            dst = outs[w].at[slot(*block)]
            return pltpu.make_async_remote_copy(src_ref=dst if src is None else src, dst_ref=dst, send_sem=send_sems.at[w, k],
                                                recv_sem=recv_sems.at[w, k], device_id=to, device_id_type=MESH)

        mine, first, passed = [], [], []
        for w in range(n):
            cp = pltpu.make_async_copy(ins[w], outs[w].at[me], local_sems.at[w])
            cp.start()
            mine.append(cp)
            for j, chip in enumerate(chips):
                first.append(copy(w, 1 + j, (x, y, c), (*chip, c), src=ins[w]))
            first.append(copy(w, 0, (x, y, c), sibling, src=ins[w]))
        for cp in first:
            cp.start()
        for w in range(n):
            for j, chip in enumerate(chips):
                copy(w, 1 + j, (*chip, c), (x, y, c)).wait_recv()
                cp = copy(w, 4 + j, (*chip, c), sibling)
                cp.start()
                passed.append(cp)
        for w in range(n):
            copy(w, 0, sibling, (x, y, c)).wait_recv()
            for j, chip in enumerate(chips):
                copy(w, 4 + j, (*chip, 1 - c), (x, y, c)).wait_recv()
        for cp in first + passed:
            cp.wait_send()
        for cp in mine:
            cp.wait()

    hbm = pl.BlockSpec(memory_space=pltpu.HBM)
    return pl.pallas_call(
        body, name=name, out_shape=[jax.ShapeDtypeStruct((N_DEV, *s.shape), s.dtype) for s in shards],
        in_specs=[hbm] * n, out_specs=[hbm] * n,
        scratch_shapes=[pltpu.SemaphoreType.DMA((n, 7)), pltpu.SemaphoreType.DMA((n, 7)), pltpu.SemaphoreType.DMA((n,))],
    )(*shards)


_HBM = pl.BlockSpec(memory_space=pltpu.HBM)
_SEM = pl.BlockSpec(memory_space=pltpu.SEMAPHORE)
_EFFECT = pltpu.SideEffectType.DATAFLOW_SIDE_EFFECTING


def _split_start(name, bufs, n_sems, copies_fn, after=None):
    nb = len(bufs)
    extra = [] if after is None else [after]
    k = nb + len(extra)

    def body(*refs):
        for cp in copies_fn(refs[:nb], refs[k], refs[k + 1]):
            cp.start()
        refs[-1][...] = jnp.zeros_like(refs[-1])

    sems = pltpu.SemaphoreType.DMA((n_sems,))
    res = pl.pallas_call(
        body, name=name,
        out_shape=(sems, sems, *[pltpu.HBM(a.shape, a.dtype) for a in bufs], jax.ShapeDtypeStruct((8, LANES), F32)),
        in_specs=[_HBM] * nb + [pl.BlockSpec(memory_space=pl.ANY)] * len(extra),
        out_specs=(_SEM, _SEM, *[_HBM] * nb, pl.BlockSpec(memory_space=pltpu.VMEM)),
        input_output_aliases={i: 2 + i for i in range(nb)},
        compiler_params=pltpu.CompilerParams(has_side_effects=_EFFECT),
    )(*[pltpu.with_memory_space_constraint(a, pltpu.HBM) for a in bufs], *extra)
    return res[0], res[1], list(res[2:2 + nb]), res[-1]


def _split_wait(name, bufs, send_sems, recv_sems, after, copies_fn):
    nb = len(bufs)

    def body(*refs):
        for cp in copies_fn(refs[:nb], refs[nb], refs[nb + 1]):
            cp.wait_send()
            cp.wait_recv()

    res = pl.pallas_call(
        body, name=name, out_shape=tuple(pltpu.HBM(a.shape, a.dtype) for a in bufs),
        in_specs=[_HBM] * nb + [_SEM, _SEM, pl.BlockSpec(memory_space=pl.ANY)], out_specs=tuple([_HBM] * nb),
        input_output_aliases={i: i for i in range(nb)},
        compiler_params=pltpu.CompilerParams(has_side_effects=_EFFECT),
    )(*bufs, send_sems, recv_sems, after)
    return list(res)


def _split_relay(name, bufs, send_sems, recv_sems, after, wait_fn, n_sems, start_fn):
    nb = len(bufs)

    def body(*refs):
        for cp in wait_fn(refs[:nb], refs[nb], refs[nb + 1]):
            cp.wait_send()
            cp.wait_recv()
        for cp in start_fn(refs[:nb], refs[nb + 3], refs[nb + 4]):
            cp.start()
        refs[-1][...] = jnp.zeros_like(refs[-1])

    sems = pltpu.SemaphoreType.DMA((n_sems,))
    res = pl.pallas_call(
        body, name=name, out_shape=(sems, sems, *[pltpu.HBM(a.shape, a.dtype) for a in bufs], jax.ShapeDtypeStruct((8, LANES), F32)),
        in_specs=[_HBM] * nb + [_SEM, _SEM, pl.BlockSpec(memory_space=pl.ANY)],
        out_specs=(_SEM, _SEM, *[_HBM] * nb, pl.BlockSpec(memory_space=pltpu.VMEM)),
        input_output_aliases={i: 2 + i for i in range(nb)},
        compiler_params=pltpu.CompilerParams(has_side_effects=_EFFECT),
    )(*bufs, send_sems, recv_sems, after)
    return res[0], res[1], list(res[2:2 + nb]), res[-1]


N_CHIP = 4


def _chip_flip(x, y, k):
    return (1 - x if k & 2 else x), (1 - y if k & 1 else y)


def _gather_first_copies(n):
    def copies(bufs, send_sems, recv_sems):
        x, y, c, me = _position()
        out = []
        for w in range(n):
            for k in range(N_CHIP):
                to = (x, y, 1 - c) if k == 0 else (*_chip_flip(x, y, k), c)
                out.append(pltpu.make_async_remote_copy(
                    src_ref=bufs[w], dst_ref=bufs[n + w].at[me], send_sem=send_sems.at[w * N_CHIP + k],
                    recv_sem=recv_sems.at[w * N_CHIP + k], device_id=to, device_id_type=MESH))
        return out
    return copies


def _gather_relay_copies(n):
    def copies(bufs, send_sems, recv_sems):
        x, y, c, _ = _position()
        out = []
        for w in range(n):
            for k in range(1, N_CHIP):
                px, py = _chip_flip(x, y, k)
                blk = bufs[n + w].at[4 * px + 2 * py + c]
                out.append(pltpu.make_async_remote_copy(
                    src_ref=blk, dst_ref=blk, send_sem=send_sems.at[w * (N_CHIP - 1) + k - 1],
                    recv_sem=recv_sems.at[w * (N_CHIP - 1) + k - 1], device_id=(x, y, 1 - c), device_id_type=MESH))
        return out
    return copies


def _small_gather_copies(bufs, send_sems, recv_sems):
    x, y, c, me = _position()
    return [pltpu.make_async_remote_copy(src_ref=bufs[0], dst_ref=bufs[1].at[me], send_sem=send_sems.at[k - 1], recv_sem=recv_sems.at[k - 1],
                                         device_id=_flip(x, y, c, k), device_id_type=MESH) for k in range(1, N_DEV)]


def _xor(a, b):
    return a + b - 2 * a * b


def _forward_first_copies(n):
    def copies(bufs, send_sems, recv_sems):
        x, y, c, me = _position()
        out = []
        for w in range(n):
            for k, to in enumerate([(x, y, 1 - c), (1 - x, y, c), (x, 1 - y, c)]):
                out.append(pltpu.make_async_remote_copy(
                    src_ref=bufs[w], dst_ref=bufs[n + w].at[me], send_sem=send_sems.at[w * 3 + k],
                    recv_sem=recv_sems.at[w * 3 + k], device_id=to, device_id_type=MESH))
        return out
    return copies


def _forward_second_copies(n):
    def copies(bufs, send_sems, recv_sems):
        x, y, c, _ = _position()
        out = []
        for w in range(n):
            half = bufs[n + w].shape[1] // 2
            for k, (src_chip, rows, to) in enumerate([((1 - x, y), pl.ds(0, half), (x, 1 - y, c)), ((x, 1 - y), pl.ds(half, half), (1 - x, y, c))]):
                blk = bufs[n + w].at[4 * src_chip[0] + 2 * src_chip[1] + c, rows]
                out.append(pltpu.make_async_remote_copy(src_ref=blk, dst_ref=blk, send_sem=send_sems.at[w * 4 + k],
                                                        recv_sem=recv_sems.at[w * 4 + k], device_id=to, device_id_type=MESH))
            for k, (px, py) in enumerate([(1 - x, y), (x, 1 - y)]):
                blk = bufs[n + w].at[4 * px + 2 * py + c]
                out.append(pltpu.make_async_remote_copy(src_ref=blk, dst_ref=blk, send_sem=send_sems.at[w * 4 + 2 + k],
                                                        recv_sem=recv_sems.at[w * 4 + 2 + k], device_id=(x, y, 1 - c), device_id_type=MESH))
        return out
    return copies


def _forward_third_copies(n):
    def copies(bufs, send_sems, recv_sems):
        x, y, c, _ = _position()
        out = []
        for w in range(n):
            blk = bufs[n + w].at[4 * (1 - x) + 2 * (1 - y) + c]
            out.append(pltpu.make_async_remote_copy(src_ref=blk, dst_ref=blk, send_sem=send_sems.at[w], recv_sem=recv_sems.at[w],
                                                    device_id=(x, y, 1 - c), device_id_type=MESH))
        return out
    return copies


def _to_sibling_copies(n):
    def copies(bufs, send_sems, recv_sems):
        x, y, c, _ = _position()
        out = []
        for w in range(n):
            for q in range(N_CHIP):
                out.append(pltpu.make_async_remote_copy(
                    src_ref=bufs[w].at[2 * q + 1 - c], dst_ref=bufs[n + w].at[q], send_sem=send_sems.at[w * N_CHIP + q],
                    recv_sem=recv_sems.at[w * N_CHIP + q], device_id=(x, y, 1 - c), device_id_type=MESH))
        return out
    return copies


def _to_owner_copies(n):
    def copies(bufs, send_sems, recv_sems):
        x, y, c, _ = _position()
        out = []
        for w in range(n):
            for k in range(1, N_CHIP):
                px, py = (1 - x if k & 2 else x), (1 - y if k & 1 else y)
                out.append(pltpu.make_async_remote_copy(
                    src_ref=bufs[w].at[2 * px + py], dst_ref=bufs[n + w].at[k - 1], send_sem=send_sems.at[w * (N_CHIP - 1) + k - 1],
                    recv_sem=recv_sems.at[w * (N_CHIP - 1) + k - 1], device_id=(px, py, c), device_id_type=MESH))
        return out
    return copies


def _chip_sum(name, stack, landed, c_idx):
    _, R, C = stack.shape
    tr = _tile(R, max(16, 1048576 // C), 16)

    def body(c_ref, a_ref, b_ref, o_ref):
        o_ref[...] = (a_ref[...].astype(F32) + b_ref[...].astype(F32)).astype(o_ref.dtype)

    return pl.pallas_call(
        body, name=name,
        grid_spec=pltpu.PrefetchScalarGridSpec(
            num_scalar_prefetch=1, grid=(N_CHIP, R // tr),
            in_specs=[pl.BlockSpec((None, tr, C), lambda q, i, c_ref: (2 * q + c_ref[0], i, 0)),
                      pl.BlockSpec((None, tr, C), lambda q, i, c_ref: (q, i, 0))],
            out_specs=pl.BlockSpec((None, tr, C), lambda q, i, c_ref: (q, i, 0))),
        out_shape=jax.ShapeDtypeStruct((N_CHIP, R, C), stack.dtype),
        compiler_params=_cparams(("parallel", "parallel")))(c_idx, stack, landed)


def _ada_mod(c16, w):
    _, D = c16.shape
    n = w.shape[1]
    tk = _tile(D, 512)
    nk = D // tk

    def body(c_ref, w_ref, o_ref, ca_ref):
        @pl.when(pl.program_id(0) == 0)
        def _():
            o_ref[...] = jnp.zeros_like(o_ref)

        cv = c_ref[...]
        ca = cv * _sigmoid(cv)
        ca_ref[...] = ca
        o_ref[...] += _dot(ca, w_ref[...])

    return pl.pallas_call(
        body, name="ada_mod", grid=(nk,),
        in_specs=[pl.BlockSpec((16, tk), lambda k: (0, k)), pl.BlockSpec((tk, n), lambda k: (k, 0))],
        out_specs=[pl.BlockSpec((16, n), lambda k: (0, 0)), pl.BlockSpec((16, tk), lambda k: (0, k))],
        out_shape=[jax.ShapeDtypeStruct((16, n), F32), jax.ShapeDtypeStruct((16, D), F32)],
        compiler_params=_cparams(("arbitrary",)))(c16, w)


def _adam_math(w, g, m, v):
    m2 = ADAM_B1 * m + (1.0 - ADAM_B1) * g
    v2 = ADAM_B2 * v + (1.0 - ADAM_B2) * (g * g)
    m_hat = m2 / (1.0 - ADAM_B1 ** ADAM_STEP)
    v_hat = v2 / (1.0 - ADAM_B2 ** ADAM_STEP)
    delta = -ADAM_LR * (m_hat / (jnp.sqrt(v_hat) + ADAM_EPS) + ADAM_WD * w)
    return delta, m2, v2


def _adamw(name, w, m, v, parts, row0=0, into=None):
    R, C = w.shape
    Rp = parts[0].shape[1]
    tr = _tile(Rp, max(16, 393216 // C), 16)
    off = row0 // tr
    n_p = len(parts)
    held = [] if into is None else list(into)

    def body(*refs):
        w_ref, m_ref, v_ref = refs[:3]
        g_ref, d_ref, m2_ref, v2_ref = refs[3 + n_p + len(held):]
        g = None
        for p_ref in refs[3:3 + n_p]:
            for s in range(p_ref.shape[0]):
                t = p_ref[s].astype(F32)
                g = t if g is None else g + t
        delta, m2, v2 = _adam_math(w_ref[...], g, m_ref[...], v_ref[...])
        g_ref[...] = g
        d_ref[...] = delta
        m2_ref[...] = m2
        v2_ref[...] = v2

    blk = pl.BlockSpec((tr, C), lambda i: (i + off, 0))
    out = jax.ShapeDtypeStruct((R, C), F32)
    return pl.pallas_call(
        body, name=name, grid=(Rp // tr,),
        in_specs=[blk, blk, blk] + [pl.BlockSpec((a.shape[0], tr, C), lambda i: (0, i, 0)) for a in parts]
        + [pl.BlockSpec(memory_space=pl.ANY)] * len(held),
        out_specs=[blk] * 4, out_shape=[out] * 4, input_output_aliases={3 + n_p + i: i for i in range(len(held))},
        compiler_params=_cparams(("parallel",)))(w, m, v, *parts, *held)


def _small_update(gathered, w, m, v, after):
    _, R, L = gathered.shape
    rs = w.shape[0]

    def body(p_ref, w_ref, m_ref, v_ref, after_ref, g_ref, d_ref, m2_ref, v2_ref):
        g = p_ref[0]
        for p in range(1, N_DEV):
            g = g + p_ref[p]
        g_ref[...] = g
        delta, m2, v2 = _adam_math(w_ref[...], g[0:rs, :], m_ref[...], v_ref[...])
        d_ref[...] = delta
        m2_ref[...] = m2
        v2_ref[...] = v2

    vm = pl.BlockSpec(memory_space=pltpu.VMEM)
    sm = jax.ShapeDtypeStruct((rs, L), F32)
    return pl.pallas_call(body, name="small_update", in_specs=[vm] * 4 + [pl.BlockSpec(memory_space=pl.ANY)], out_specs=[vm] * 4,
                          out_shape=[jax.ShapeDtypeStruct((R, L), F32), sm, sm, sm],
                          compiler_params=pltpu.CompilerParams(vmem_limit_bytes=VMEM_LIMIT))(gathered, w, m, v, after)


class _Fetched(dict):
    def __init__(self, fetch):
        super().__init__()
        self.fetch = fetch

    def first(self, key, after):
        self[key] = self.fetch(key, after)
        return self[key]


def _local_step(x, tgt, mod, p, fetch, F, scatter=None):
    S, D = x.shape
    GW, HW = p["ln_g"].shape[1], p["hg_ng"].shape[1]
    G, T, _ = p["ws"].shape
    w = _Fetched(fetch)
    INW = 2 * GW + 4 * HW + 2 * D
    in_loc, br_loc, fi_loc = INW // N_DEV, D // N_DEV, 2 * F // N_DEV
    assert GW == HW and F % fi_loc == 0
    sh1, sc1, gt1, sh2, sc2, gt2 = (mod[:, k * D:(k + 1) * D] for k in range(6))
    bsb = jnp.broadcast_to(p["bs"][:, :, None], (G, T, GW // G))

    tm = _tile(S, 1024, 16)
    tmh = _tile(S, 512, 16)
    tn_in = _tile(in_loc, 1280)
    tn_d = _tile(D, 512)
    tn_br = _tile(br_loc, 512)
    tk_s = S
    tm_w = _tile(D, 1024)
    g_off = 2 * GW + 4 * HW

    h1 = _norm_mod("norm1", x, p["norm1_g"], sc1, sh1)
    z = _mm_nn_stacked("proj_in", h1, w.first("in", h1), tm=tm, tn=tn_in, tk=D)[0]
    ya = _gmlp_fwd(z, p["ln_g"], p["ln_b"], p["ws"], bsb, GW)
    yb, o_hg, states = _hg_fwd(z, p["hg_lb"], p["hg_ng"], HW)
    t_fi = w.first("fi_early", yb)
    pa = _mm_nn_stacked("branch_gmlp", ya, w.first("bg", z), tm=tm, tn=tn_br, tk=GW, after=t_fi)[0]

    def gates(ga_ref, gb_ref, ba_ref, bb_ref):
        return _sigmoid(ga_ref[...] + ba_ref[...]), _sigmoid(gb_ref[...] + bb_ref[...])

    def gate_specs(tn_):
        o1, o2 = g_off // tn_, (g_off + D) // tn_
        return [pl.BlockSpec((tm, tn_), lambda i, j, k: (i, o1 + j)), pl.BlockSpec((tm, tn_), lambda i, j, k: (i, o2 + j)),
                pl.BlockSpec((1, tn_), lambda i, j, k: (0, j)), pl.BlockSpec((1, tn_), lambda i, j, k: (0, D // tn_ + j))]

    def merge_ep(acc, ex, outs):
        ga, gb = gates(*ex[1:5])
        outs[0][...] = acc
        outs[1][...] = (ga * ex[0][...] + gb * acc).astype(BF16)

    tile_o = pl.BlockSpec((tm, tn_br), lambda i, j, k: (i, j))
    pb, y = _mm_nn_stacked(
        "branch_hg_merge", yb, w.first("bh", z), tm=tm, tn=tn_br, tk=HW, extras=[pa, z, z, p["b_gate"], p["b_gate"]],
        extra_specs=[tile_o, *gate_specs(tn_br)], out_shapes=[jax.ShapeDtypeStruct((S, D), F32), jax.ShapeDtypeStruct((S, D), BF16)],
        out_specs=[tile_o, tile_o], epilogue=merge_ep)

    def resid_ep(acc, ex, outs):
        outs[0][...] = acc
        outs[1][...] = ex[0][...] + ex[1][...] * acc

    def resid_mm(name, a, b, res, gt, tm_):
        K = a.shape[1]
        t_o = pl.BlockSpec((tm_, tn_d), lambda i, j, k: (i, j))
        return _matmul(
            name, a, b, dims=_NN, grid_mnk=(S // tm_, D // tn_d, 1), tiles=(tm_, tn_d),
            a_spec=pl.BlockSpec((tm_, K), lambda i, j, k: (i, 0)), b_spec=pl.BlockSpec((K, tn_d), lambda i, j, k: (0, j)),
            extras=[res, gt], extra_specs=[t_o, pl.BlockSpec((1, tn_d), lambda i, j, k: (0, j))],
            out_shapes=[jax.ShapeDtypeStruct((S, D), F32)] * 2, out_specs=[t_o, t_o], epilogue=resid_ep)

    o1, xm = resid_mm("proj_out", y, w.first("out", z), x, gt1, tm)
    h2 = _norm_mod("norm2", xm, p["norm2_g"], sc2, sh2)
    hf, hf_fac = _ffn_in_swiglu(h2, w.first("fi", h2))
    o2, x3 = resid_mm("ffn_out", hf, w.first("fo", hf), xm, gt2, tmh)
    dx3, do2, vec_l = _loss_head(x3, tgt, p["final_g"], o2, gt2)

    nf = F // fi_loc

    def dswiglu_ep(acc, ex, outs):
        outs[0][0] = (acc * ex[0][0].astype(F32)).astype(BF16)
        outs[0][1] = (acc * ex[0][1].astype(F32)).astype(BF16)

    pair = pl.BlockSpec((2, tmh, fi_loc), lambda i, j, k: (0, i, j))
    dab = _matmul(
        "ffn_out_dx", do2, w["fo"], dims=_NT, grid_mnk=(S // tmh, nf, 1), tiles=(tmh, fi_loc),
        a_spec=pl.BlockSpec((tmh, D), lambda i, j, k: (i, 0)), b_spec=pl.BlockSpec((fi_loc, D), lambda i, j, k: (j, 0)),
        extras=[hf_fac], extra_specs=[pair], out_shapes=[jax.ShapeDtypeStruct((2, S, F), BF16)], out_specs=[pair],
        epilogue=dswiglu_ep)[0]
    start = (lambda name, grads: scatter[0](name, grads)) if scatter is not None else (lambda name, grads: None)
    push = (lambda name, after: scatter[1](name, after)) if scatter is not None else (lambda name, after: None)

    def zero(token):
        return 0.0 if token is None else token[0:1, 0:1]

    tm_f = _tile(F, 512)
    g_fo = _mm_tn("ffn_out_dw", hf, do2, pl.BlockSpec((tk_s, D), lambda i, j, k: (k, j)), Mo=F, No=D, S=S, tm=tm_f, tn=D, tk=tk_s)
    g_fi = _mm_tn("ffn_in_dw", h2, dab, pl.BlockSpec((None, tk_s, fi_loc), lambda i, j, k: (j // nf, k, j % nf)),
                  Mo=D, No=2 * F, S=S, tm=tm_w, tn=fi_loc, tk=tk_s, stacked_nloc=fi_loc, after=g_fo)
    t_ffn = start("scatter_ffn", dict(fo=g_fo, fi=g_fi))
    dh2 = _mm_nt_stacked("ffn_in_dx", pl.BlockSpec((None, tmh, fi_loc), lambda i, j, k: (k // nf, i, k % nf)), dab, w["fi"],
                         M=S, tm=tmh, tn=D, tk=fi_loc, after=t_ffn)
    dxm, vec2, do1 = _norm_mod_bwd("norm2_bwd", dh2, xm, p["norm2_g"], sc2, dx3, o1, gt1)
    t_ffn = push("scatter_ffn", dxm)

    def dmerge_ep(acc, ex, outs):
        ga, gb = gates(*ex[2:6])
        outs[0][...] = (acc * ga).astype(BF16)
        outs[1][...] = (acc * gb).astype(BF16)
        outs[2][0] = (acc * ex[0][...] * ga * (1.0 - ga)).astype(BF16)
        outs[2][1] = (acc * ex[1][...] * gb * (1.0 - gb)).astype(BF16)

    t_o = pl.BlockSpec((tm, tn_d), lambda i, j, k: (i, j))
    dpa, dpb, dg2 = _matmul(
        "proj_out_dx", do1, w["out"], dims=_NT, grid_mnk=(S // tm, D // tn_d, 1), tiles=(tm, tn_d),
        a_spec=pl.BlockSpec((tm, D), lambda i, j, k: (i, 0)), b_spec=pl.BlockSpec((tn_d, D), lambda i, j, k: (j, 0)),
        extras=[pa, pb, z, z, p["b_gate"], p["b_gate"]], extra_specs=[t_o, t_o, *gate_specs(tn_d)],
        out_shapes=[jax.ShapeDtypeStruct((S, D), BF16), jax.ShapeDtypeStruct((S, D), BF16), jax.ShapeDtypeStruct((2, S, D), BF16)],
        out_specs=[t_o, t_o, pl.BlockSpec((2, tm, tn_d), lambda i, j, k: (0, i, j))], epilogue=dmerge_ep, after=t_ffn)
    g_out = _mm_tn("proj_out_dw", y, do1, pl.BlockSpec((tk_s, D), lambda i, j, k: (k, j)), Mo=D, No=D, S=S, tm=tn_d, tn=D, tk=tk_s)
    tn_g = _tile(GW, 512)
    b_br = pl.BlockSpec((tk_s, br_loc), lambda i, j, k: (k, j))
    g_bg = _mm_tn("branch_gmlp_dw", ya, dpa, b_br, Mo=GW, No=D, S=S, tm=tn_g, tn=br_loc, tk=tk_s, stacked_nloc=br_loc)
    g_bh = _mm_tn("branch_hg_dw", yb, dpb, b_br, Mo=HW, No=D, S=S, tm=tn_g, tn=br_loc, tk=tk_s, stacked_nloc=br_loc)
    t_mix = start("scatter_mixer", dict(out=g_out, bg=g_bg, bh=g_bh))
    def branch_dx(name, dp, wg):
        flat = jnp.swapaxes(wg, 0, 1).reshape(wg.shape[1], D)
        return _matmul(
            name, dp, flat, dims=_NT, grid_mnk=(S // tm, GW // tn_g, 1), tiles=(tm, tn_g),
            a_spec=pl.BlockSpec((tm, D), lambda i, j, k: (i, 0)), b_spec=pl.BlockSpec((tn_g, D), lambda i, j, k: (j, 0)),
            out_shapes=[jax.ShapeDtypeStruct((S, GW), F32)], out_specs=[pl.BlockSpec((tm, tn_g), lambda i, j, k: (i, j))],
            epilogue=_store(F32), after=t_mix)[0]

    dya = branch_dx("branch_gmlp_dx", dpa, w["bg"])
    dyb = branch_dx("branch_hg_dx", dpb, w["bh"])
    db_gate = _colsum2(dg2)
    dz_gmlp, dln, dws, dbs = _gmlp_bwd(z, dya, p["ln_g"], p["ln_b"], p["ws"], bsb, GW)
    t_mix = push("scatter_mixer", dz_gmlp)
    dz, dng, dhlb = _hg_bwd(z, o_hg, states, dyb, p["hg_lb"], p["hg_ng"] + zero(t_mix), HW, dz_gmlp, dg2)
    half = D // 2
    tm_h = _tile(half, 1024)
    g_in = []
    t_in = None
    for hname, h in (("a", 0), ("b", 1)):
        g_in.append(_mm_tn("proj_in_dw_" + hname, h1, dz, pl.BlockSpec((tk_s, in_loc), lambda i, j, k: (k, j)), Mo=half, No=INW, S=S,
                           tm=tm_h, tn=in_loc, tk=tk_s, stacked_nloc=in_loc, after=t_in, a_off=h * (half // tm_h)))
        t_in = start("scatter_proj_in_" + hname, {"w_in_" + hname: g_in[-1]})
    t_in = push("scatter_proj_in_a", t_in)
    dh1 = _mm_nt_stacked("proj_in_dx", pl.BlockSpec((tmh, in_loc), lambda i, j, k: (i, k)), dz, w["in"], M=S, tm=tmh, tn=D, tk=in_loc,
                         after=t_in)
    dx, vec1 = _norm_mod_bwd("norm1_bwd", dh1, x, p["norm1_g"], sc1, dxm)

    dmod = jnp.concatenate([vec1[0:1], vec1[1:2], vec2[3:4], vec2[0:1], vec2[1:2], vec_l[2:3]], axis=1)
    small = dict(norm1_g=vec1[2:3], b_gate=db_gate.reshape(1, 2 * D), ln_g=dln[0:1], ln_b=dln[1:2], ws=dws, bs=dbs.reshape(G, T),
                 hg_lb=dhlb, hg_ng=dng[0:1], norm2_g=vec2[2:3], final_g=vec_l[1:2], loss=vec_l[0:1, 0:LANES])
    big = dict(w_in_a=g_in[0], w_in_b=g_in[1], bg=g_bg, bh=g_bh, out=g_out, fi=g_fi, fo=g_fo)
    return dx, big, small, dmod


_SMALL = ("b_ada", "norm1_g", "b_gate", "ln_g", "ln_b", "ws", "bs", "hg_lb", "hg_ng", "norm2_g", "final_g")


def _pack(parts, rows_mult=8):
    flat = [a.reshape(-1) for a in parts]
    offs, n = [], 0
    for a in flat:
        offs.append(n)
        n += a.shape[0]
    pad = (-n) % (LANES * rows_mult)
    if pad:
        flat.append(jnp.zeros((pad,), F32))
    return jnp.concatenate(flat).reshape(-1, LANES), offs


def kernel(x, c, w_ada, b_ada, norm1_g, w_in, b_gate, gmlp_ln_g, gmlp_ln_b, gmlp_ws, gmlp_bs, hg_lb, hg_norm_g, w_branch_gmlp, w_branch_hg, w_out, norm2_g, w_ffn_in, w_ffn_out, final_norm_g, loss_target, m_w_ada, m_b_ada, m_norm1_g, m_w_in, m_b_gate, m_gmlp_ln_g, m_gmlp_ln_b, m_gmlp_ws, m_gmlp_bs, m_hg_lb, m_hg_norm_g, m_w_branch_gmlp, m_w_branch_hg, m_w_out, m_norm2_g, m_w_ffn_in, m_w_ffn_out, m_final_norm_g, v_w_ada, v_b_ada, v_norm1_g, v_w_in, v_b_gate, v_gmlp_ln_g, v_gmlp_ln_b, v_gmlp_ws, v_gmlp_bs, v_hg_lb, v_hg_norm_g, v_w_branch_gmlp, v_w_branch_hg, v_w_out, v_norm2_g, v_w_ffn_in, v_w_ffn_out, v_final_norm_g):
    S, D = x.shape[1], x.shape[2]
    ada_loc = w_ada.shape[2]
    me = 4 * lax.axis_index("x") + 2 * lax.axis_index("y") + lax.axis_index("c")

    c_all = _allgather_small("gather_c", c.reshape(D // LANES, LANES)).reshape(N_DEV, D)
    mod_cols, c_act = _ada_mod(jnp.pad(c_all, ((0, 16 - N_DEV), (0, 0))), w_ada[0])
    mod_all = _allgather_small("gather_mod", mod_cols[:N_DEV].reshape(-1, LANES)).reshape(N_DEV, N_DEV, ada_loc)
    mod = lax.dynamic_index_in_dim(mod_all, me, axis=1, keepdims=False).reshape(1, N_DEV * ada_loc) + b_ada

    def empty_hbm(shape, dtype):
        return pltpu.with_memory_space_constraint(lax.empty(shape, dtype), pltpu.HBM)

    groups = dict(gather_in=dict(keys=["in"], src=[w_in], forward=True),
                  gather_mixer=dict(keys=["bg", "bh", "out"], src=[w_branch_gmlp, w_branch_hg, w_out], forward=False),
                  gather_ffn_in=dict(keys=["fi"], src=[w_ffn_in], forward=True),
                  gather_ffn_out=dict(keys=["fo"], src=[w_ffn_out], forward=False))
    group_of = {k: gname for gname, g in groups.items() for k in g["keys"]}

    def first_hop(gname, after):
        g = groups[gname]
        n = len(g["keys"])
        shards = [a[0].astype(BF16) for a in g["src"]]
        outs = [lax.dynamic_update_slice(lax.empty((N_DEV, *s.shape), BF16), s[None], (me, 0, 0)) for s in shards]
        if g["forward"]:
            *g["hop"], token = _split_start(gname + "_hop1", shards + outs, n * 3, _forward_first_copies(n), after=after)
        else:
            *g["hop"], token = _split_start(gname + "_hop1", shards + outs, n * N_CHIP, _gather_first_copies(n), after=after)
        return token

    def second_hop(gname, after):
        g = groups[gname]
        n = len(g["keys"])
        *g["hop"], token = _split_relay(gname + "_hop2", g["hop"][2], g["hop"][0], g["hop"][1], after,
                                        _forward_first_copies(n), n * 4, _forward_second_copies(n))
        return token

    def finish(gname, after):
        g = groups[gname]
        n = len(g["keys"])
        send_sems, recv_sems, bufs = g["hop"]
        if g["forward"]:
            send_sems, recv_sems, bufs, _ = _split_relay(gname + "_hop3", bufs, send_sems, recv_sems, after,
                                                         _forward_second_copies(n), n, _forward_third_copies(n))
            bufs = _split_wait(gname + "_wait", bufs, send_sems, recv_sems, after, _forward_third_copies(n))
        else:
            send_sems, recv_sems, bufs, _ = _split_relay(gname + "_relay", bufs, send_sems, recv_sems, after,
                                                         _gather_first_copies(n), n * (N_CHIP - 1), _gather_relay_copies(n))
            bufs = _split_wait(gname + "_wait", bufs, send_sems, recv_sems, after, _gather_relay_copies(n))
        g["done"] = dict(zip(g["keys"], bufs[n:]))

    token = first_hop("gather_in", mod_all)
    mod = mod + token[0:1, 0:1]

    def fetch(key, after):
        if key == "in":
            t = second_hop("gather_in", after)
            t = first_hop("gather_mixer", t)
            t = first_hop("gather_ffn_in", t)
            finish("gather_in", t)
        elif key == "fi_early":
            return first_hop("gather_ffn_out", second_hop("gather_ffn_in", after))
        elif "done" not in groups[group_of[key]]:
            finish(group_of[key], after)
        arr = groups[group_of[key]]["done"][key]
        return arr.reshape(-1, D) if key in ("out", "fo") else arr

    p = dict(norm1_g=norm1_g, b_gate=b_gate, ln_g=gmlp_ln_g, ln_b=gmlp_ln_b, ws=gmlp_ws[0], bs=gmlp_bs[0], hg_lb=hg_lb,
             hg_ng=hg_norm_g, norm2_g=norm2_g, final_g=final_norm_g.reshape(1, D))

    in_flight = {}
    c_idx = lax.axis_index("c").astype(jnp.int32).reshape(1)
    my_chip = 2 * lax.axis_index("x") + lax.axis_index("y")

    def scatter_start(name, grads):
        keys = list(grads)
        n = len(keys)
        stacks = [grads[k].reshape(N_DEV, -1, grads[k].shape[-1]) for k in keys]
        lands = [empty_hbm((N_CHIP, *g.shape[1:]), g.dtype) for g in stacks]
        send_sems, recv_sems, bufs, token = _split_start(name + "_d2d", stacks + lands, n * N_CHIP, _to_sibling_copies(n))
        in_flight[name] = dict(keys=keys, stage1=(send_sems, recv_sems, bufs))
        return token

    def scatter_push(name, after):
        f = in_flight[name]
        n = len(f["keys"])
        send_sems, recv_sems, bufs = f["stage1"]
        bufs = _split_wait(name + "_d2d_wait", bufs, send_sems, recv_sems, after, _to_sibling_copies(n))
        sums = [_chip_sum(f"{name}_sum_{k}", bufs[i], bufs[n + i], c_idx) for i, k in enumerate(f["keys"])]
        lands = [empty_hbm((N_CHIP - 1, *s.shape[1:]), s.dtype) for s in sums]
        send_sems, recv_sems, bufs, token = _split_start(name + "_ici", sums + lands, n * (N_CHIP - 1), _to_owner_copies(n))
        f["stage2"] = (send_sems, recv_sems, bufs)
        return token

    grad_x, _, small, dmod = _local_step(x[0], loss_target[0], mod, p, fetch, w_ffn_out.shape[1] * N_DEV, (scatter_start, scatter_push))

    small["b_ada"] = dmod
    packed, offs = _pack([small[k] for k in _SMALL] + [small["loss"]])
    sg_send, sg_recv, sg_bufs, t_tail = _split_start(
        "gather_small_start", [packed, lax.dynamic_update_slice(lax.empty((N_DEV, *packed.shape), F32), packed[None], (me, 0, 0))],
        N_DEV - 1, _small_gather_copies)
    t_tail = scatter_push("scatter_proj_in_b", t_tail)
    big_w = dict(w_in=(w_in, m_w_in, v_w_in, "w_in"), bg=(w_branch_gmlp, m_w_branch_gmlp, v_w_branch_gmlp, "w_branch_gmlp"),
                 bh=(w_branch_hg, m_w_branch_hg, v_w_branch_hg, "w_branch_hg"), out=(w_out, m_w_out, v_w_out, "w_out"),
                 fi=(w_ffn_in, m_w_ffn_in, v_w_ffn_in, "w_ffn_in"), fo=(w_ffn_out, m_w_ffn_out, v_w_ffn_out, "w_ffn_out"))
    upd = {}

    def land_and_update(name, after):
        keys = in_flight[name]["keys"]
        n = len(keys)
        send_sems, recv_sems, bufs = in_flight[name]["stage2"]
        bufs = _split_wait(name + "_ici_wait", bufs, send_sems, recv_sems, after, _to_owner_copies(n))
        for i, k in enumerate(keys):
            parts = [lax.dynamic_index_in_dim(bufs[i], my_chip, axis=0, keepdims=True), bufs[n + i]]
            if k in big_w:
                wt, mt, vt, out_name = big_w[k]
                upd[out_name] = _adamw("adamw_" + out_name, wt[0], mt[0], vt[0], parts)
            else:
                wt, mt, vt, out_name = big_w["w_in"]
                upd[out_name] = _adamw("adamw_" + k, wt[0], mt[0], vt[0], parts, row0=0 if k == "w_in_a" else parts[0].shape[1],
                                       into=upd.get(out_name))
            after = upd[out_name][1]
        return after

    after = land_and_update("scatter_mixer", land_and_update("scatter_ffn", t_tail))
    gathered = _split_wait("gather_small_wait", sg_bufs, sg_send, sg_recv, after, _small_gather_copies)[1]
    wp = dict(p, b_ada=b_ada)
    ms = dict(b_ada=m_b_ada, norm1_g=m_norm1_g, b_gate=m_b_gate, ln_g=m_gmlp_ln_g, ln_b=m_gmlp_ln_b, ws=m_gmlp_ws, bs=m_gmlp_bs,
              hg_lb=m_hg_lb, hg_ng=m_hg_norm_g, norm2_g=m_norm2_g, final_g=m_final_norm_g)
    vs = dict(b_ada=v_b_ada, norm1_g=v_norm1_g, b_gate=v_b_gate, ln_g=v_gmlp_ln_g, ln_b=v_gmlp_ln_b, ws=v_gmlp_ws, bs=v_gmlp_bs,
              hg_lb=v_hg_lb, hg_ng=v_hg_norm_g, norm2_g=v_norm2_g, final_g=v_final_norm_g)
    w_sm, _ = _pack([wp[k] for k in _SMALL])
    m_sm, _ = _pack([ms[k] for k in _SMALL])
    v_sm, _ = _pack([vs[k] for k in _SMALL])
    sm_out = _small_update(gathered, w_sm, m_sm, v_sm, after)
    shapes = dict(b_ada=b_ada.shape, norm1_g=norm1_g.shape, b_gate=b_gate.shape, ln_g=gmlp_ln_g.shape, ln_b=gmlp_ln_b.shape,
                  ws=gmlp_ws.shape, bs=gmlp_bs.shape, hg_lb=hg_lb.shape, hg_ng=hg_norm_g.shape, norm2_g=norm2_g.shape,
                  final_g=final_norm_g.shape)

    def unpack(arr, k):
        i = _SMALL.index(k)
        n = math.prod(shapes[k])
        return arr.reshape(-1)[offs[i]:offs[i] + n].reshape(shapes[k])

    loss = sm_out[0].reshape(-1)[offs[len(_SMALL)]]

    dmod_all = gathered.reshape(N_DEV, -1)[:, offs[0]:offs[0] + N_DEV * ada_loc]
    dmod_loc = lax.dynamic_slice_in_dim(dmod_all, me * ada_loc, ada_loc, axis=1)
    ca_t = jnp.pad(c_act[:N_DEV].T, ((0, 0), (0, LANES - N_DEV))).astype(BF16)
    dm_p = jnp.pad(dmod_loc, ((0, LANES - N_DEV), (0, 0))).astype(BF16)
    tm_a = _tile(D, 512)
    g_ada = _matmul(
        "ada_dw", ca_t, dm_p, dims=_NN, grid_mnk=(D // tm_a, 1, 1), tiles=(tm_a, ada_loc),
        a_spec=pl.BlockSpec((tm_a, LANES), lambda i, j, k: (i, 0)), b_spec=pl.BlockSpec((LANES, ada_loc), lambda i, j, k: (0, 0)),
        out_shapes=[jax.ShapeDtypeStruct((1, D, ada_loc), F32)], out_specs=[pl.BlockSpec((None, tm_a, ada_loc), lambda i, j, k: (0, i, 0))],
        epilogue=_store(F32))[0]
    upd["w_ada"] = _adamw("adamw_w_ada", w_ada[0], m_w_ada[0], v_w_ada[0], [g_ada])
    land_and_update("scatter_proj_in_b", land_and_update("scatter_proj_in_a", upd["w_ada"][1]))

    order = ("w_ada", "b_ada", "norm1_g", "w_in", "b_gate", "ln_g", "ln_b", "ws", "bs", "hg_lb", "hg_ng", "w_branch_gmlp", "w_branch_hg",
             "w_out", "norm2_g", "w_ffn_in", "w_ffn_out", "final_g")
    outs = [loss, grad_x[None]]
    for idx in range(4):
        for k in order:
            outs.append(upd[k][idx][None] if k in upd else unpack(sm_out[idx], k))
    return tuple(outs)
```

```python
import functools
import math

import jax
import jax.numpy as jnp
from jax import lax
from jax.experimental import pallas as pl
from jax.experimental.pallas import tpu as pltpu

F32 = jnp.float32
BF16 = jnp.bfloat16
N_DEV = 8
EPS = 1e-6
LANES = 128
HG_DK = 128
HG_CHUNK = 64
HG_MID = HG_CHUNK // 2 - 1
EXP_CLAMP = 80.0
VMEM_LIMIT = 48 * 1024 * 1024
ADAM_LR, ADAM_B1, ADAM_B2, ADAM_EPS, ADAM_WD, ADAM_STEP = 0.001, 0.9, 0.999, 1e-08, 0.01, 10
MESH = pl.DeviceIdType.MESH

_NN = (((1,), (0,)), ((), ()))
_NT = (((1,), (1,)), ((), ()))
_TN = (((0,), (0,)), ((), ()))


def _dot(a, b, dims=_NN):
    return lax.dot_general(a.astype(BF16), b.astype(BF16), dims, preferred_element_type=F32)


def _tile(n, target, mult=LANES):
    best = None
    for t in range(mult, min(n, target) + 1, mult):
        if n % t == 0:
            best = t
    return n if best is None else best


def _cparams(sem):
    return pltpu.CompilerParams(dimension_semantics=sem, vmem_limit_bytes=VMEM_LIMIT)


def _sigmoid(x):
    return 1.0 / (1.0 + jnp.exp(-x))


def _gelu_parts(x):
    k0 = math.sqrt(2.0 / math.pi)
    x2 = x * x
    t = jnp.tanh(k0 * (x + 0.044715 * x * x2))
    g = 0.5 * x * (1.0 + t)
    dg = 0.5 * (1.0 + t) + 0.5 * x * (1.0 - t * t) * (k0 * (1.0 + 3.0 * 0.044715 * x2))
    return g, dg


def _split3(x):
    h = x.astype(BF16)
    r = x - h.astype(F32)
    m = r.astype(BF16)
    lo = (r - m.astype(F32)).astype(BF16)
    return h, m, lo


def _ones_dot(mat01, x):
    h, m, lo = _split3(x)
    d = functools.partial(lax.dot_general, dimension_numbers=_NN, preferred_element_type=F32)
    return d(mat01, h) + d(mat01, m) + d(mat01, lo)


def _matmul(name, a, b, *, dims, grid_mnk, tiles, a_spec, b_spec, extras=(), extra_specs=(), out_shapes, out_specs, epilogue, after=None,
            sem=None):
    gm, gn, nk = grid_mnk
    tm, tn = tiles
    n_ex, n_out = len(extras), len(out_shapes)
    held = [] if after is None else [after]

    def body(*refs):
        a_ref, b_ref = refs[0], refs[1]
        ex = refs[2:2 + n_ex]
        outs = refs[2 + n_ex + len(held):2 + n_ex + len(held) + n_out]
        more = () if sem is None else (pl.program_id(0) == 0,)
        if nk == 1:
            epilogue(lax.dot_general(a_ref[...], b_ref[...], dims, preferred_element_type=F32), ex, outs, *more)
            return
        acc = refs[-1]
        k = pl.program_id(2)

        @pl.when(k == 0)
        def _():
            acc[...] = jnp.zeros_like(acc)

        acc[...] += lax.dot_general(a_ref[...], b_ref[...], dims, preferred_element_type=F32)

        @pl.when(k == nk - 1)
        def _():
            epilogue(acc[...], ex, outs, *more)

    return pl.pallas_call(
        body, name=name, grid=(gm, gn, nk), in_specs=[a_spec, b_spec, *extra_specs] + [pl.BlockSpec(memory_space=pl.ANY)] * len(held),
        out_specs=list(out_specs), out_shape=list(out_shapes), scratch_shapes=[] if nk == 1 else [pltpu.VMEM((tm, tn), F32)],
        compiler_params=_cparams(sem or ("parallel", "parallel", "arbitrary")),
    )(a, b, *extras, *held)


def _store(dtype):
    def ep(acc, ex, outs):
        outs[0][...] = acc.astype(dtype)
    return ep


def _mm_nn_stacked(name, a, wg, *, tm, tn, tk, out_dtype=F32, extras=(), extra_specs=(), out_shapes=None, out_specs=None, epilogue=None,
                   after=None):
    M, K = a.shape
    _, _, nloc = wg.shape
    N = nloc * N_DEV
    q = nloc // tn
    if out_shapes is None:
        out_shapes = [jax.ShapeDtypeStruct((M, N), out_dtype)]
        out_specs = [pl.BlockSpec((tm, tn), lambda i, j, k: (i, j))]
        epilogue = _store(out_dtype)
    return _matmul(
        name, a, wg, dims=_NN, grid_mnk=(M // tm, N // tn, K // tk), tiles=(tm, tn),
        a_spec=pl.BlockSpec((tm, tk), lambda i, j, k: (i, k)),
        b_spec=pl.BlockSpec((None, tk, tn), lambda i, j, k: (j // q, k, j % q)),
        extras=extras, extra_specs=extra_specs, out_shapes=out_shapes, out_specs=out_specs, epilogue=epilogue, after=after)


def _mm_nt_stacked(name, a_spec, a, wg, *, M, tm, tn, tk, out_dtype=F32, after=None, extras=(), extra_specs=(), out_shapes=None,
                   out_specs=None, epilogue=None, sem=None):
    _, Kw, nloc = wg.shape
    q = nloc // tk
    single = out_shapes is None
    if single:
        out_shapes = [jax.ShapeDtypeStruct((M, Kw), out_dtype)]
        out_specs = [pl.BlockSpec((tm, tn), lambda i, j, k: (i, j))]
        epilogue = _store(out_dtype)
    res = _matmul(
        name, a, wg, dims=_NT, grid_mnk=(M // tm, Kw // tn, (nloc * N_DEV) // tk), tiles=(tm, tn),
        a_spec=a_spec, b_spec=pl.BlockSpec((None, tn, tk), lambda i, j, k: (k // q, j, k % q)),
        extras=extras, extra_specs=extra_specs, out_shapes=out_shapes, out_specs=out_specs, epilogue=epilogue, after=after, sem=sem)
    return res[0] if single else res


def _mm_tn(name, a, b, b_spec, *, Mo, No, S, tm, tn, tk, stacked_nloc=None, after=None, a_off=0):
    if stacked_nloc is None:
        out_shape = jax.ShapeDtypeStruct((Mo, No), BF16)
        out_spec = pl.BlockSpec((tm, tn), lambda i, j, k: (i, j))
    else:
        q = stacked_nloc // tn
        out_shape = jax.ShapeDtypeStruct((N_DEV, Mo, stacked_nloc), BF16)
        out_spec = pl.BlockSpec((None, tm, tn), lambda i, j, k: (j // q, i, j % q))
    return _matmul(
        name, a, b, dims=_TN, grid_mnk=(Mo // tm, No // tn, S // tk), tiles=(tm, tn),
        a_spec=pl.BlockSpec((tk, tm), lambda i, j, k: (k, i + a_off)), b_spec=b_spec,
        out_shapes=[out_shape], out_specs=[out_spec], epilogue=_store(BF16), after=after)[0]


def _norm_mod(name, x, g, sc, sh):
    S, D = x.shape
    tm = _tile(S, 256, 8)

    def body(x_ref, g_ref, sc_ref, sh_ref, h_ref):
        xv = x_ref[...]
        r = lax.rsqrt(jnp.mean(xv * xv, axis=-1, keepdims=True) + EPS)
        h = (xv * r) * g_ref[...]
        h_ref[...] = (h * (1.0 + sc_ref[...]) + sh_ref[...]).astype(BF16)

    row = pl.BlockSpec((tm, D), lambda i: (i, 0))
    vec = pl.BlockSpec((1, D), lambda i: (0, 0))
    return pl.pallas_call(body, name=name, grid=(S // tm,), in_specs=[row, vec, vec, vec], out_specs=row,
                          out_shape=jax.ShapeDtypeStruct((S, D), BF16), compiler_params=_cparams(("parallel",)))(x, g, sc, sh)


def _norm_mod_bwd_rows(first, dh_v, x_ref, g_ref, sc_ref, dres_ref, dx_ref, vec_ref, o_ref=None, gt_ref=None, do_ref=None):
    @pl.when(first)
    def _():
        vec_ref[...] = jnp.zeros_like(vec_ref)

    xv, gv = x_ref[...], g_ref[...]
    r = lax.rsqrt(jnp.mean(xv * xv, axis=-1, keepdims=True) + EPS)
    xn = xv * r
    one_sc = 1.0 + sc_ref[...]
    vec_ref[0:1, :] += jnp.sum(dh_v, axis=0, keepdims=True)
    vec_ref[1:2, :] += jnp.sum(dh_v * (xn * gv), axis=0, keepdims=True)
    vec_ref[2:3, :] += jnp.sum(dh_v * one_sc * xn, axis=0, keepdims=True)
    dxn = dh_v * one_sc * gv
    dx = dres_ref[...] + r * (dxn - xn * jnp.mean(dxn * xn, axis=-1, keepdims=True))
    dx_ref[...] = dx
    if o_ref is not None:
        vec_ref[3:4, :] += jnp.sum(dx * o_ref[...], axis=0, keepdims=True)
        do_ref[...] = (dx * gt_ref[...]).astype(BF16)


def _norm_mod_bwd(name, dh, x, g, sc, dres, o=None, gt=None):
    S, D = x.shape
    tm = _tile(S, 256, 8)
    gated = o is not None

    def body(*refs):
        if gated:
            dh_ref, x_ref, g_ref, sc_ref, dres_ref, o_ref, gt_ref, dx_ref, vec_ref, do_ref = refs
        else:
            dh_ref, x_ref, g_ref, sc_ref, dres_ref, dx_ref, vec_ref = refs
            o_ref = gt_ref = do_ref = None
        _norm_mod_bwd_rows(pl.program_id(0) == 0, dh_ref[...], x_ref, g_ref, sc_ref, dres_ref, dx_ref, vec_ref, o_ref, gt_ref, do_ref)

    row = pl.BlockSpec((tm, D), lambda i: (i, 0))
    vec = pl.BlockSpec((1, D), lambda i: (0, 0))
    acc = pl.BlockSpec((8, D), lambda i: (0, 0))
    ins = [dh, x, g, sc, dres] + ([o, gt] if gated else [])
    in_specs = [row, row, vec, vec, row] + ([row, vec] if gated else [])
    out_shape = [jax.ShapeDtypeStruct((S, D), F32), jax.ShapeDtypeStruct((8, D), F32)]
    out_specs = [row, acc]
    if gated:
        out_shape.append(jax.ShapeDtypeStruct((S, D), BF16))
        out_specs.append(row)
    return pl.pallas_call(body, name=name, grid=(S // tm,), in_specs=in_specs, out_specs=out_specs, out_shape=out_shape,
                          compiler_params=_cparams(("arbitrary",)))(*ins)


def _loss_head(x3, tgt, gf, o2, gt2):
    S, D = x3.shape
    tm = _tile(S, 256, 8)

    def body(x_ref, t_ref, g_ref, o_ref, gt_ref, dx_ref, do_ref, vec_ref):
        i = pl.program_id(0)

        @pl.when(i == 0)
        def _():
            vec_ref[...] = jnp.zeros_like(vec_ref)

        xv, gv = x_ref[...], g_ref[...]
        r = lax.rsqrt(jnp.mean(xv * xv, axis=-1, keepdims=True) + EPS)
        xn = xv * r
        e = xn * gv - t_ref[...]
        tok = 0.5 * jnp.mean(e * e, axis=-1, keepdims=True)
        vec_ref[0:1, :] += jnp.broadcast_to(jnp.sum(tok, axis=0, keepdims=True), (1, D))
        dy = e * (1.0 / D)
        vec_ref[1:2, :] += jnp.sum(dy * xn, axis=0, keepdims=True)
        dxn = dy * gv
        dx = r * (dxn - xn * jnp.mean(dxn * xn, axis=-1, keepdims=True))
        dx_ref[...] = dx
        vec_ref[2:3, :] += jnp.sum(dx * o_ref[...], axis=0, keepdims=True)
        do_ref[...] = (dx * gt_ref[...]).astype(BF16)

    row = pl.BlockSpec((tm, D), lambda i: (i, 0))
    vec = pl.BlockSpec((1, D), lambda i: (0, 0))
    return pl.pallas_call(
        body, name="loss_head", grid=(S // tm,), in_specs=[row, row, vec, row, vec],
        out_specs=[row, row, pl.BlockSpec((8, D), lambda i: (0, 0))],
        out_shape=[jax.ShapeDtypeStruct((S, D), F32), jax.ShapeDtypeStruct((S, D), BF16), jax.ShapeDtypeStruct((8, D), F32)],
        compiler_params=_cparams(("arbitrary",)))(x3, tgt, gf, o2, gt2)


def _ffn_in_swiglu(h, wg):
    S, D = h.shape
    _, _, tf = wg.shape
    nf = N_DEV // 2
    F = nf * tf
    tm = _tile(S, 256, 16)

    def body(h_ref, wa_ref, wu_ref, hf_ref, fac_ref):
        hv = h_ref[...]
        a = lax.dot_general(hv, wa_ref[...], _NN, preferred_element_type=F32)
        up = lax.dot_general(hv, wu_ref[...], _NN, preferred_element_type=F32)
        sa = _sigmoid(a)
        silu = a * sa
        hf_ref[...] = (silu * up).astype(BF16)
        fac_ref[0] = (up * (sa * (1.0 + a * (1.0 - sa)))).astype(BF16)
        fac_ref[1] = silu.astype(BF16)

    return pl.pallas_call(
        body, name="ffn_in_swiglu", grid=(nf, S // tm),
        in_specs=[pl.BlockSpec((tm, D), lambda j, i: (i, 0)), pl.BlockSpec((None, D, tf), lambda j, i: (j, 0, 0)),
                  pl.BlockSpec((None, D, tf), lambda j, i: (j + nf, 0, 0))],
        out_specs=[pl.BlockSpec((tm, tf), lambda j, i: (i, j)), pl.BlockSpec((2, tm, tf), lambda j, i: (0, i, j))],
        out_shape=[jax.ShapeDtypeStruct((S, F), BF16), jax.ShapeDtypeStruct((2, S, F), BF16)],
        compiler_params=_cparams(("parallel", "parallel")))(h, wg, wg)


def _colsum2(dg2):
    _, S, D = dg2.shape
    tm = _tile(S, 256, 16)

    def body(x_ref, o_ref):
        @pl.when(pl.program_id(0) == 0)
        def _():
            o_ref[...] = jnp.zeros_like(o_ref)

        o_ref[0:1, :] += jnp.sum(x_ref[0].astype(F32), axis=0, keepdims=True)
        o_ref[1:2, :] += jnp.sum(x_ref[1].astype(F32), axis=0, keepdims=True)

    return pl.pallas_call(body, name="gate_bias_grad", grid=(S // tm,), in_specs=[pl.BlockSpec((2, tm, D), lambda i: (0, i, 0))],
                          out_specs=pl.BlockSpec((2, D), lambda i: (0, 0)), out_shape=jax.ShapeDtypeStruct((2, D), F32),
                          compiler_params=_cparams(("arbitrary",)))(dg2)


def _gmlp_common(u_ref, v_ref, lg_ref, lb_ref, ws_ref, bsb_ref, G, T, Dg):
    ug, dug = _gelu_parts(u_ref[...])
    vg, dvg = _gelu_parts(v_ref[...])
    mu = jnp.mean(vg, axis=-1, keepdims=True)
    vc = vg - mu
    rstd = lax.rsqrt(jnp.mean(vc * vc, axis=-1, keepdims=True) + EPS)
    vhat = vc * rstd
    vn = vhat * lg_ref[...] + lb_ref[...]
    row = lax.broadcasted_iota(jnp.int32, (T, T), 0)
    col = lax.broadcasted_iota(jnp.int32, (T, T), 1)
    tril = row >= col
    s = []
    for g in range(G):
        w = jnp.where(tril, ws_ref[g], 0.0)
        s.append(_dot(w, vn[:, g * Dg:(g + 1) * Dg]) + bsb_ref[g])
    return ug, dug, dvg, rstd, vhat, vn, tril, s


def _gmlp_fwd(z, ln_g, ln_b, ws, bsb, GW):
    S = z.shape[0]
    G, T, _ = ws.shape
    Dg = GW // G

    def body(u_ref, v_ref, lg_ref, lb_ref, ws_ref, bsb_ref, ya_ref):
        ug, _, _, _, _, _, _, s = _gmlp_common(u_ref, v_ref, lg_ref, lb_ref, ws_ref, bsb_ref, G, T, Dg)
        for g in range(G):
            sl = slice(g * Dg, (g + 1) * Dg)
            ya_ref[:, sl] = (ug[:, sl] * s[g]).astype(BF16)

    vec = pl.BlockSpec((1, GW), lambda c: (0, 0))
    return pl.pallas_call(
        body, name="gmlp_fwd", grid=(S // T,),
        in_specs=[pl.BlockSpec((T, GW), lambda c: (c, 0)), pl.BlockSpec((T, GW), lambda c: (c, 1)), vec, vec,
                  pl.BlockSpec((G, T, T), lambda c: (0, 0, 0)), pl.BlockSpec((G, T, Dg), lambda c: (0, 0, 0))],
        out_specs=pl.BlockSpec((T, GW), lambda c: (c, 0)), out_shape=jax.ShapeDtypeStruct((S, GW), BF16),
        compiler_params=_cparams(("parallel",)))(z, z, ln_g, ln_b, ws, bsb)


def _gmlp_bwd(z, dya, ln_g, ln_b, ws, bsb, GW):
    S = z.shape[0]
    G, T, _ = ws.shape
    Dg = GW // G
    nc = S // T

    def body(u_ref, v_ref, dya_ref, lg_ref, lb_ref, ws_ref, bsb_ref, dz_ref, dln_ref, dws_ref, dbs_ref, dbs_acc, dvh):
        c = pl.program_id(0)

        @pl.when(c == 0)
        def _():
            dln_ref[...] = jnp.zeros_like(dln_ref)
            dws_ref[...] = jnp.zeros_like(dws_ref)
            dbs_acc[...] = jnp.zeros_like(dbs_acc)

        ug, dug, dvg, rstd, vhat, vn, tril, s = _gmlp_common(u_ref, v_ref, lg_ref, lb_ref, ws_ref, bsb_ref, G, T, Dg)
        dya_v = dya_ref[...]
        for g in range(G):
            sl = slice(g * Dg, (g + 1) * Dg)
            dy_g = dya_v[:, sl]
            dz_ref[:, sl] = (dy_g * s[g] * dug[:, sl]).astype(BF16)
            ds = dy_g * ug[:, sl]
            dbs_acc[g] += ds
            w = jnp.where(tril, ws_ref[g], 0.0)
            dvn_g = _dot(w, ds, _TN)
            dws_ref[g] += jnp.where(tril, _dot(ds, vn[:, sl], _NT), 0.0)
            dln_ref[0:1, sl] += jnp.sum(dvn_g * vhat[:, sl], axis=0, keepdims=True)
            dln_ref[1:2, sl] += jnp.sum(dvn_g, axis=0, keepdims=True)
            dvh[:, sl] = dvn_g * lg_ref[:, sl]
        dvhat = dvh[...]
        m1 = jnp.mean(dvhat, axis=-1, keepdims=True)
        m2 = jnp.mean(dvhat * vhat, axis=-1, keepdims=True)
        dz_ref[:, GW:2 * GW] = (rstd * (dvhat - m1 - vhat * m2) * dvg).astype(BF16)

        @pl.when(c == nc - 1)
        def _():
            for g in range(G):
                dbs_ref[g] = jnp.sum(dbs_acc[g], axis=-1, keepdims=True)

    vec = pl.BlockSpec((1, GW), lambda c: (0, 0))
    return pl.pallas_call(
        body, name="gmlp_bwd", grid=(nc,),
        in_specs=[pl.BlockSpec((T, GW), lambda c: (c, 0)), pl.BlockSpec((T, GW), lambda c: (c, 1)),
                  pl.BlockSpec((T, GW), lambda c: (c, 0)), vec, vec,
                  pl.BlockSpec((G, T, T), lambda c: (0, 0, 0)), pl.BlockSpec((G, T, Dg), lambda c: (0, 0, 0))],
        out_specs=[pl.BlockSpec((T, 2 * GW), lambda c: (c, 0)), pl.BlockSpec((8, GW), lambda c: (0, 0)),
                   pl.BlockSpec((G, T, T), lambda c: (0, 0, 0)), pl.BlockSpec((G, T, 1), lambda c: (0, 0, 0))],
        out_shape=[jax.ShapeDtypeStruct((S, 2 * GW), BF16), jax.ShapeDtypeStruct((8, GW), F32),
                   jax.ShapeDtypeStruct((G, T, T), F32), jax.ShapeDtypeStruct((G, T, 1), F32)],
        scratch_shapes=[pltpu.VMEM((G, T, Dg), F32), pltpu.VMEM((T, GW), F32)],
        compiler_params=_cparams(("arbitrary",)))(z, z, dya, ln_g, ln_b, ws, bsb)


def _hg_common(q_ref, f_ref, hlb_ref):
    C = HG_CHUNK
    a = hlb_ref[...]
    lb = _sigmoid(a[0:1, :] - a[1:2, :])
    sig = _sigmoid(f_ref[...])
    f = lb + (1.0 - lb) * sig
    lf = jnp.log(f)
    kk = 1.0 - f
    q = q_ref[...]
    sq = _sigmoid(q)
    qa = q * sq
    row = lax.broadcasted_iota(jnp.int32, (C, C), 0)
    col = lax.broadcasted_iota(jnp.int32, (C, C), 1)
    tril = row >= col
    b = _ones_dot(tril.astype(BF16), lf)
    bm = b[HG_MID:HG_MID + 1, :]
    bl = b[C - 1:C, :]
    e_b = jnp.exp(b)
    e_qm = jnp.exp(jnp.minimum(b - bm, EXP_CLAMP))
    e_km = jnp.exp(jnp.minimum(bm - b, EXP_CLAMP))
    e_kl = jnp.exp(bl - b)
    return dict(lb=lb, sig=sig, f=f, kk=kk, q=q, sq=sq, qa=qa, tril=tril, e_b=e_b, e_qm=e_qm, e_km=e_km, e_kl=e_kl,
                e_l=jnp.exp(bl), qh=qa * e_b, qt=qa * e_qm, kt=kk * e_km, kh=kk * e_kl)


def _hg_fwd(z, hg_lb, ng, HW):
    S = z.shape[0]
    C, H, dk = HG_CHUNK, HW // HG_DK, HG_DK
    nc = S // C

    def body(q_ref, f_ref, i_ref, og_ref, hlb_ref, ng_ref, yb_ref, o_ref, st_ref, state):
        @pl.when(pl.program_id(0) == 0)
        def _():
            state[...] = jnp.zeros_like(state)

        t = _hg_common(q_ref, f_ref, hlb_ref)
        iv = i_ref[...]
        for h in range(H):
            sl = slice(h * dk, (h + 1) * dk)
            st = state[h]
            st_ref[h] = st
            a = jnp.where(t["tril"], _dot(t["qt"][:, sl], t["kt"][:, sl], _NT), 0.0)
            o_h = _dot(a, iv[:, sl]) + _dot(t["qh"][:, sl], st, _NT)
            state[h] = st * t["e_l"][:, sl] + _dot(iv[:, sl], t["kh"][:, sl], _TN)
            o_ref[:, sl] = o_h
            rr = lax.rsqrt(jnp.mean(o_h * o_h, axis=-1, keepdims=True) + EPS)
            og = og_ref[:, sl]
            yb_ref[:, sl] = (o_h * rr * ng_ref[:, sl] * (og * _sigmoid(og))).astype(BF16)

    def col(k):
        return pl.BlockSpec((C, HW), lambda c: (c, k))

    base = 2
    return pl.pallas_call(
        body, name="hgrn_fwd", grid=(nc,),
        in_specs=[col(base), col(base + 1), col(base + 2), col(base + 3),
                  pl.BlockSpec((2, HW), lambda c: (0, 0)), pl.BlockSpec((1, HW), lambda c: (0, 0))],
        out_specs=[pl.BlockSpec((C, HW), lambda c: (c, 0)), pl.BlockSpec((C, HW), lambda c: (c, 0)),
                   pl.BlockSpec((None, H, dk, dk), lambda c: (c, 0, 0, 0))],
        out_shape=[jax.ShapeDtypeStruct((S, HW), BF16), jax.ShapeDtypeStruct((S, HW), F32),
                   jax.ShapeDtypeStruct((nc, H, dk, dk), F32)],
        scratch_shapes=[pltpu.VMEM((H, dk, dk), F32)],
        compiler_params=_cparams(("arbitrary",)))(z, z, z, z, hg_lb, ng)


def _hg_bwd(z, o, states, dyb, hg_lb, ng, HW, dz_head, dz_tail):
    S = z.shape[0]
    C, H, dk = HG_CHUNK, HW // HG_DK, HG_DK
    nc = S // C
    B0 = dz_head.shape[1]
    DT = dz_tail.shape[2]
    INW = B0 + 4 * HW + 2 * DT

    def body(q_ref, f_ref, i_ref, og_ref, o_ref, st_ref, stn_ref, dyb_ref, hlb_ref, ng_ref, head_ref, tail_ref,
             dzf_ref, dng_ref, dhlb_ref, dstate, cross, dqa_buf, dkk_buf, db_buf, dlb_acc):
        c = pl.program_id(0)
        dzf_ref[:, 0:B0] = head_ref[...]
        dzf_ref[:, B0 + 4 * HW:B0 + 4 * HW + DT] = tail_ref[0]
        dzf_ref[:, B0 + 4 * HW + DT:INW] = tail_ref[1]
        dz_ref = dzf_ref.at[:, B0:B0 + 4 * HW]

        @pl.when(c == 0)
        def _():
            dstate[...] = jnp.zeros_like(dstate)
            dlb_acc[...] = jnp.zeros_like(dlb_acc)
            dng_ref[...] = jnp.zeros_like(dng_ref)

        def r16(v):
            return v.astype(BF16).astype(F32)

        t = _hg_common(q_ref, f_ref, hlb_ref)
        iv = i_ref[...]
        for h in range(H):
            sl = slice(h * dk, (h + 1) * dk)
            o_h, og, dyb_h, ng_h = o_ref[:, sl], og_ref[:, sl], dyb_ref[:, sl], ng_ref[:, sl]
            sg = _sigmoid(og)
            silu_og = og * sg
            rr = lax.rsqrt(jnp.mean(o_h * o_h, axis=-1, keepdims=True) + EPS)
            on = o_h * rr
            dng_ref[0:1, sl] += jnp.sum(dyb_h * on * silu_og, axis=0, keepdims=True)
            dz_ref[:, 3 * HW + h * dk:3 * HW + (h + 1) * dk] = (dyb_h * on * ng_h * (sg * (1.0 + og * (1.0 - sg)))).astype(BF16)
            don = dyb_h * ng_h * silu_og
            do_h = rr * (don - on * jnp.mean(don * on, axis=-1, keepdims=True))

            qt, kt, qh, kh, iv_h = t["qt"][:, sl], t["kt"][:, sl], t["qh"][:, sl], t["kh"][:, sl], iv[:, sl]
            a = jnp.where(t["tril"], _dot(qt, kt, _NT), 0.0)
            da = jnp.where(t["tril"], _dot(do_h, iv_h, _NT), 0.0)
            st, dst = st_ref[h], dstate[h]
            cross[:, sl] = jnp.sum(dst * stn_ref[h], axis=0, keepdims=True)
            dqh = _dot(do_h, st)
            dstate[h] = _dot(do_h, qh, _TN) + dst * t["e_l"][:, sl]
            div = _dot(a, do_h, _TN) + _dot(kh, dst, _NT)
            dkh = _dot(iv_h, dst)
            dqt = _dot(da, kt)
            dkt = _dot(da, qt, _TN)
            dz_ref[:, 2 * HW + h * dk:2 * HW + (h + 1) * dk] = div.astype(BF16)
            dqa_buf[:, sl] = dqh * t["e_b"][:, sl] + dqt * t["e_qm"][:, sl]
            dkk_buf[:, sl] = dkt * t["e_km"][:, sl] + dkh * t["e_kl"][:, sl]
            db_buf[:, sl] = r16(qt) * dqt - r16(kt) * dkt + r16(qh) * dqh - r16(kh) * dkh

        dqa, dkk = dqa_buf[...], dkk_buf[...]
        triu = jnp.logical_not(t["tril"]) | (lax.broadcasted_iota(jnp.int32, (C, C), 0) == lax.broadcasted_iota(jnp.int32, (C, C), 1))
        dlf = _ones_dot(triu.astype(BF16), db_buf[...]) + cross[...]
        df = dlf / t["f"] - dkk
        sig, lb = t["sig"], t["lb"]
        dz_ref[:, HW:2 * HW] = (df * (1.0 - lb) * sig * (1.0 - sig)).astype(BF16)
        dlb_acc[...] += jnp.sum(df * (1.0 - sig), axis=0, keepdims=True)
        q, sq = t["q"], t["sq"]
        dz_ref[:, 0:HW] = (dqa * (sq * (1.0 + q * (1.0 - sq)))).astype(BF16)

        @pl.when(c == nc - 1)
        def _():
            da0 = dlb_acc[...] * lb * (1.0 - lb)
            dhlb_ref[0:1, :] = da0
            dhlb_ref[1:2, :] = -da0

    def col(k):
        return pl.BlockSpec((C, HW), lambda c: (nc - 1 - c, k))

    base = 2
    return pl.pallas_call(
        body, name="hgrn_bwd", grid=(nc,),
        in_specs=[col(base), col(base + 1), col(base + 2), col(base + 3), col(0),
                  pl.BlockSpec((None, H, dk, dk), lambda c: (nc - 1 - c, 0, 0, 0)),
                  pl.BlockSpec((None, H, dk, dk), lambda c: (jnp.minimum(nc - c, nc - 1), 0, 0, 0)), col(0),
                  pl.BlockSpec((2, HW), lambda c: (0, 0)), pl.BlockSpec((1, HW), lambda c: (0, 0)),
                  pl.BlockSpec((C, B0), lambda c: (nc - 1 - c, 0)), pl.BlockSpec((2, C, DT), lambda c: (0, nc - 1 - c, 0))],
        out_specs=[pl.BlockSpec((C, INW), lambda c: (nc - 1 - c, 0)), pl.BlockSpec((8, HW), lambda c: (0, 0)),
                   pl.BlockSpec((2, HW), lambda c: (0, 0))],
        out_shape=[jax.ShapeDtypeStruct((S, INW), BF16), jax.ShapeDtypeStruct((8, HW), F32), jax.ShapeDtypeStruct((2, HW), F32)],
        scratch_shapes=[pltpu.VMEM((H, dk, dk), F32), pltpu.VMEM((1, HW), F32), pltpu.VMEM((C, HW), F32), pltpu.VMEM((C, HW), F32),
                        pltpu.VMEM((C, HW), F32), pltpu.VMEM((1, HW), F32)],
        compiler_params=_cparams(("arbitrary",)))(z, z, z, z, o, states, states, dyb, hg_lb, ng, dz_head, dz_tail)


def _position():
    x, y, c = lax.axis_index("x"), lax.axis_index("y"), lax.axis_index("c")
    return x, y, c, 4 * x + 2 * y + c


def _flip(x, y, c, k):
    return (1 - x if k & 4 else x, 1 - y if k & 2 else y, 1 - c if k & 1 else c)


def _allgather_small(name, v):
    R, L = v.shape

    def body(v_ref, out_ref, send_sems, recv_sems):
        x, y, c, me = _position()
        out_ref[me] = v_ref[...]
        copies = []
        for k in range(1, N_DEV):
            cp = pltpu.make_async_remote_copy(src_ref=v_ref, dst_ref=out_ref.at[me], send_sem=send_sems.at[k - 1],
                                              recv_sem=recv_sems.at[k - 1], device_id=_flip(x, y, c, k), device_id_type=MESH)
            cp.start()
            copies.append(cp)
        for cp in copies:
            cp.wait()

    return pl.pallas_call(
        body, name=name, out_shape=jax.ShapeDtypeStruct((N_DEV, R, L), v.dtype),
        in_specs=[pl.BlockSpec(memory_space=pltpu.VMEM)], out_specs=pl.BlockSpec(memory_space=pltpu.VMEM),
        scratch_shapes=[pltpu.SemaphoreType.DMA((N_DEV - 1,)), pltpu.SemaphoreType.DMA((N_DEV - 1,))],
        compiler_params=pltpu.CompilerParams(vmem_limit_bytes=VMEM_LIMIT),
    )(v)


def _allgather_hbm(name, shards):
    n = len(shards)

    def body(*refs):
        ins, outs = refs[:n], refs[n:2 * n]
        send_sems, recv_sems, local_sems = refs[2 * n:]
        x, y, c, me = _position()
        sibling = (x, y, 1 - c)
        chips = [(1 - x, y), (x, 1 - y), (1 - x, 1 - y)]

        def slot(px, py, pc):
            return 4 * px + 2 * py + pc

        def copy(w, k, block, to, src=None):
            dst = outs[w].at[slot(*block)]
            return pltpu.make_async_remote_copy(src_ref=dst if src is None else src, dst_ref=dst, send_sem=send_sems.at[w, k],
                                                recv_sem=recv_sems.at[w, k], device_id=to, device_id_type=MESH)

        mine, first, passed = [], [], []
        for w in range(n):
            cp = pltpu.make_async_copy(ins[w], outs[w].at[me], local_sems.at[w])
            cp.start()
            mine.append(cp)
            for j, chip in enumerate(chips):
                first.append(copy(w, 1 + j, (x, y, c), (*chip, c), src=ins[w]))
            first.append(copy(w, 0, (x, y, c), sibling, src=ins[w]))
        for cp in first:
            cp.start()
        for w in range(n):
            for j, chip in enumerate(chips):
                copy(w, 1 + j, (*chip, c), (x, y, c)).wait_recv()
                cp = copy(w, 4 + j, (*chip, c), sibling)
                cp.start()
                passed.append(cp)
        for w in range(n):
            copy(w, 0, sibling, (x, y, c)).wait_recv()
            for j, chip in enumerate(chips):
                copy(w, 4 + j, (*chip, 1 - c), (x, y, c)).wait_recv()
        for cp in first + passed:
            cp.wait_send()
        for cp in mine:
            cp.wait()

    hbm = pl.BlockSpec(memory_space=pltpu.HBM)
    return pl.pallas_call(
        body, name=name, out_shape=[jax.ShapeDtypeStruct((N_DEV, *s.shape), s.dtype) for s in shards],
        in_specs=[hbm] * n, out_specs=[hbm] * n,
        scratch_shapes=[pltpu.SemaphoreType.DMA((n, 7)), pltpu.SemaphoreType.DMA((n, 7)), pltpu.SemaphoreType.DMA((n,))],
    )(*shards)


_HBM = pl.BlockSpec(memory_space=pltpu.HBM)
_SEM = pl.BlockSpec(memory_space=pltpu.SEMAPHORE)
_EFFECT = pltpu.SideEffectType.DATAFLOW_SIDE_EFFECTING


def _split_start(name, bufs, n_sems, copies_fn, after=None):
    nb = len(bufs)
    extra = [] if after is None else [after]
    k = nb + len(extra)

    def body(*refs):
        for cp in copies_fn(refs[:nb], refs[k], refs[k + 1]):
            cp.start()
        refs[-1][...] = jnp.zeros_like(refs[-1])

    sems = pltpu.SemaphoreType.DMA((n_sems,))
    res = pl.pallas_call(
        body, name=name,
        out_shape=(sems, sems, *[pltpu.HBM(a.shape, a.dtype) for a in bufs], jax.ShapeDtypeStruct((8, LANES), F32)),
        in_specs=[_HBM] * nb + [pl.BlockSpec(memory_space=pl.ANY)] * len(extra),
        out_specs=(_SEM, _SEM, *[_HBM] * nb, pl.BlockSpec(memory_space=pltpu.VMEM)),
        input_output_aliases={i: 2 + i for i in range(nb)},
        compiler_params=pltpu.CompilerParams(has_side_effects=_EFFECT),
    )(*[pltpu.with_memory_space_constraint(a, pltpu.HBM) for a in bufs], *extra)
    return res[0], res[1], list(res[2:2 + nb]), res[-1]


def _split_wait(name, bufs, send_sems, recv_sems, after, copies_fn):
    nb = len(bufs)

    def body(*refs):
        for cp in copies_fn(refs[:nb], refs[nb], refs[nb + 1]):
            cp.wait_send()
            cp.wait_recv()

    res = pl.pallas_call(
        body, name=name, out_shape=tuple(pltpu.HBM(a.shape, a.dtype) for a in bufs),
        in_specs=[_HBM] * nb + [_SEM, _SEM, pl.BlockSpec(memory_space=pl.ANY)], out_specs=tuple([_HBM] * nb),
        input_output_aliases={i: i for i in range(nb)},
        compiler_params=pltpu.CompilerParams(has_side_effects=_EFFECT),
    )(*bufs, send_sems, recv_sems, after)
    return list(res)


def _split_relay(name, bufs, send_sems, recv_sems, after, wait_fn, n_sems, start_fn):
    nb = len(bufs)

    def body(*refs):
        for cp in wait_fn(refs[:nb], refs[nb], refs[nb + 1]):
            cp.wait_send()
            cp.wait_recv()
        for cp in start_fn(refs[:nb], refs[nb + 3], refs[nb + 4]):
            cp.start()
        refs[-1][...] = jnp.zeros_like(refs[-1])

    sems = pltpu.SemaphoreType.DMA((n_sems,))
    res = pl.pallas_call(
        body, name=name, out_shape=(sems, sems, *[pltpu.HBM(a.shape, a.dtype) for a in bufs], jax.ShapeDtypeStruct((8, LANES), F32)),
        in_specs=[_HBM] * nb + [_SEM, _SEM, pl.BlockSpec(memory_space=pl.ANY)],
        out_specs=(_SEM, _SEM, *[_HBM] * nb, pl.BlockSpec(memory_space=pltpu.VMEM)),
        input_output_aliases={i: 2 + i for i in range(nb)},
        compiler_params=pltpu.CompilerParams(has_side_effects=_EFFECT),
    )(*bufs, send_sems, recv_sems, after)
    return res[0], res[1], list(res[2:2 + nb]), res[-1]


N_CHIP = 4


def _chip_flip(x, y, k):
    return (1 - x if k & 2 else x), (1 - y if k & 1 else y)


def _gather_first_copies(n):
    def copies(bufs, send_sems, recv_sems):
        x, y, c, me = _position()
        out = []
        for w in range(n):
            for k in range(N_CHIP):
                to = (x, y, 1 - c) if k == 0 else (*_chip_flip(x, y, k), c)
                out.append(pltpu.make_async_remote_copy(
                    src_ref=bufs[w], dst_ref=bufs[n + w].at[me], send_sem=send_sems.at[w * N_CHIP + k],
                    recv_sem=recv_sems.at[w * N_CHIP + k], device_id=to, device_id_type=MESH))
        return out
    return copies


def _gather_relay_copies(n):
    def copies(bufs, send_sems, recv_sems):
        x, y, c, _ = _position()
        out = []
        for w in range(n):
            for k in range(1, N_CHIP):
                px, py = _chip_flip(x, y, k)
                blk = bufs[n + w].at[4 * px + 2 * py + c]
                out.append(pltpu.make_async_remote_copy(
                    src_ref=blk, dst_ref=blk, send_sem=send_sems.at[w * (N_CHIP - 1) + k - 1],
                    recv_sem=recv_sems.at[w * (N_CHIP - 1) + k - 1], device_id=(x, y, 1 - c), device_id_type=MESH))
        return out
    return copies


def _small_gather_copies(bufs, send_sems, recv_sems):
    x, y, c, me = _position()
    return [pltpu.make_async_remote_copy(src_ref=bufs[0], dst_ref=bufs[1].at[me], send_sem=send_sems.at[k - 1], recv_sem=recv_sems.at[k - 1],
                                         device_id=_flip(x, y, c, k), device_id_type=MESH) for k in range(1, N_DEV)]


def _xor(a, b):
    return a + b - 2 * a * b


def _forward_first_copies(n):
    def copies(bufs, send_sems, recv_sems):
        x, y, c, me = _position()
        out = []
        for w in range(n):
            for k, to in enumerate([(x, y, 1 - c), (1 - x, y, c), (x, 1 - y, c)]):
                out.append(pltpu.make_async_remote_copy(
                    src_ref=bufs[w], dst_ref=bufs[n + w].at[me], send_sem=send_sems.at[w * 3 + k],
                    recv_sem=recv_sems.at[w * 3 + k], device_id=to, device_id_type=MESH))
        return out
    return copies


def _forward_second_copies(n):
    def copies(bufs, send_sems, recv_sems):
        x, y, c, _ = _position()
        out = []
        for w in range(n):
            half = bufs[n + w].shape[1] // 2
            for k, (src_chip, rows, to) in enumerate([((1 - x, y), pl.ds(0, half), (x, 1 - y, c)), ((x, 1 - y), pl.ds(half, half), (1 - x, y, c))]):
                blk = bufs[n + w].at[4 * src_chip[0] + 2 * src_chip[1] + c, rows]
                out.append(pltpu.make_async_remote_copy(src_ref=blk, dst_ref=blk, send_sem=send_sems.at[w * 4 + k],
                                                        recv_sem=recv_sems.at[w * 4 + k], device_id=to, device_id_type=MESH))
            for k, (px, py) in enumerate([(1 - x, y), (x, 1 - y)]):
                blk = bufs[n + w].at[4 * px + 2 * py + c]
                out.append(pltpu.make_async_remote_copy(src_ref=blk, dst_ref=blk, send_sem=send_sems.at[w * 4 + 2 + k],
                                                        recv_sem=recv_sems.at[w * 4 + 2 + k], device_id=(x, y, 1 - c), device_id_type=MESH))
        return out
    return copies


def _forward_third_copies(n):
    def copies(bufs, send_sems, recv_sems):
        x, y, c, _ = _position()
        out = []
        for w in range(n):
            blk = bufs[n + w].at[4 * (1 - x) + 2 * (1 - y) + c]
            out.append(pltpu.make_async_remote_copy(src_ref=blk, dst_ref=blk, send_sem=send_sems.at[w], recv_sem=recv_sems.at[w],
                                                    device_id=(x, y, 1 - c), device_id_type=MESH))
        return out
    return copies


def _to_sibling_copies(n):
    def copies(bufs, send_sems, recv_sems):
        x, y, c, _ = _position()
        out = []
        for w in range(n):
            for q in range(N_CHIP):
                out.append(pltpu.make_async_remote_copy(
                    src_ref=bufs[w].at[2 * q + 1 - c], dst_ref=bufs[n + w].at[q], send_sem=send_sems.at[w * N_CHIP + q],
                    recv_sem=recv_sems.at[w * N_CHIP + q], device_id=(x, y, 1 - c), device_id_type=MESH))
        return out
    return copies


def _to_owner_copies(n):
    def copies(bufs, send_sems, recv_sems):
        x, y, c, _ = _position()
        out = []
        for w in range(n):
            for k in range(1, N_CHIP):
                px, py = (1 - x if k & 2 else x), (1 - y if k & 1 else y)
                out.append(pltpu.make_async_remote_copy(
                    src_ref=bufs[w].at[2 * px + py], dst_ref=bufs[n + w].at[k - 1], send_sem=send_sems.at[w * (N_CHIP - 1) + k - 1],
                    recv_sem=recv_sems.at[w * (N_CHIP - 1) + k - 1], device_id=(px, py, c), device_id_type=MESH))
        return out
    return copies


def _chip_sum(name, stack, landed, c_idx):
    _, R, C = stack.shape
    tr = _tile(R, max(16, 1048576 // C), 16)

    def body(c_ref, a_ref, b_ref, o_ref):
        o_ref[...] = (a_ref[...].astype(F32) + b_ref[...].astype(F32)).astype(o_ref.dtype)

    return pl.pallas_call(
        body, name=name,
        grid_spec=pltpu.PrefetchScalarGridSpec(
            num_scalar_prefetch=1, grid=(N_CHIP, R // tr),
            in_specs=[pl.BlockSpec((None, tr, C), lambda q, i, c_ref: (2 * q + c_ref[0], i, 0)),
                      pl.BlockSpec((None, tr, C), lambda q, i, c_ref: (q, i, 0))],
            out_specs=pl.BlockSpec((None, tr, C), lambda q, i, c_ref: (q, i, 0))),
        out_shape=jax.ShapeDtypeStruct((N_CHIP, R, C), stack.dtype),
        compiler_params=_cparams(("parallel", "parallel")))(c_idx, stack, landed)


def _ada_mod(c16, w):
    _, D = c16.shape
    n = w.shape[1]
    tk = _tile(D, 512)
    nk = D // tk

    def body(c_ref, w_ref, o_ref, ca_ref):
        @pl.when(pl.program_id(0) == 0)
        def _():
            o_ref[...] = jnp.zeros_like(o_ref)

        cv = c_ref[...]
        ca = cv * _sigmoid(cv)
        ca_ref[...] = ca
        o_ref[...] += _dot(ca, w_ref[...])

    return pl.pallas_call(
        body, name="ada_mod", grid=(nk,),
        in_specs=[pl.BlockSpec((16, tk), lambda k: (0, k)), pl.BlockSpec((tk, n), lambda k: (k, 0))],
        out_specs=[pl.BlockSpec((16, n), lambda k: (0, 0)), pl.BlockSpec((16, tk), lambda k: (0, k))],
        out_shape=[jax.ShapeDtypeStruct((16, n), F32), jax.ShapeDtypeStruct((16, D), F32)],
        compiler_params=_cparams(("arbitrary",)))(c16, w)


def _adam_math(w, g, m, v):
    m2 = ADAM_B1 * m + (1.0 - ADAM_B1) * g
    v2 = ADAM_B2 * v + (1.0 - ADAM_B2) * (g * g)
    m_hat = m2 / (1.0 - ADAM_B1 ** ADAM_STEP)
    v_hat = v2 / (1.0 - ADAM_B2 ** ADAM_STEP)
    delta = -ADAM_LR * (m_hat / (jnp.sqrt(v_hat) + ADAM_EPS) + ADAM_WD * w)
    return delta, m2, v2


def _adamw(name, w, m, v, parts, row0=0, into=None):
    R, C = w.shape
    Rp = parts[0].shape[1]
    tr = _tile(Rp, max(16, 393216 // C), 16)
    off = row0 // tr
    n_p = len(parts)
    held = [] if into is None else list(into)

    def body(*refs):
        w_ref, m_ref, v_ref = refs[:3]
        g_ref, d_ref, m2_ref, v2_ref = refs[3 + n_p + len(held):]
        g = None
        for p_ref in refs[3:3 + n_p]:
            for s in range(p_ref.shape[0]):
                t = p_ref[s].astype(F32)
                g = t if g is None else g + t
        delta, m2, v2 = _adam_math(w_ref[...], g, m_ref[...], v_ref[...])
        g_ref[...] = g
        d_ref[...] = delta
        m2_ref[...] = m2
        v2_ref[...] = v2

    blk = pl.BlockSpec((tr, C), lambda i: (i + off, 0))
    out = jax.ShapeDtypeStruct((R, C), F32)
    return pl.pallas_call(
        body, name=name, grid=(Rp // tr,),
        in_specs=[blk, blk, blk] + [pl.BlockSpec((a.shape[0], tr, C), lambda i: (0, i, 0)) for a in parts]
        + [pl.BlockSpec(memory_space=pl.ANY)] * len(held),
        out_specs=[blk] * 4, out_shape=[out] * 4, input_output_aliases={3 + n_p + i: i for i in range(len(held))},
        compiler_params=_cparams(("parallel",)))(w, m, v, *parts, *held)


def _small_update(gathered, w, m, v, after):
    _, R, L = gathered.shape
    rs = w.shape[0]

    def body(p_ref, w_ref, m_ref, v_ref, after_ref, g_ref, d_ref, m2_ref, v2_ref):
        g = p_ref[0]
        for p in range(1, N_DEV):
            g = g + p_ref[p]
        g_ref[...] = g
        delta, m2, v2 = _adam_math(w_ref[...], g[0:rs, :], m_ref[...], v_ref[...])
        d_ref[...] = delta
        m2_ref[...] = m2
        v2_ref[...] = v2

    vm = pl.BlockSpec(memory_space=pltpu.VMEM)
    sm = jax.ShapeDtypeStruct((rs, L), F32)
    return pl.pallas_call(body, name="small_update", in_specs=[vm] * 4 + [pl.BlockSpec(memory_space=pl.ANY)], out_specs=[vm] * 4,
                          out_shape=[jax.ShapeDtypeStruct((R, L), F32), sm, sm, sm],
                          compiler_params=pltpu.CompilerParams(vmem_limit_bytes=VMEM_LIMIT))(gathered, w, m, v, after)


class _Fetched(dict):
    def __init__(self, fetch):
        super().__init__()
        self.fetch = fetch

    def first(self, key, after):
        self[key] = self.fetch(key, after)
        return self[key]


def _local_step(x, tgt, mod, p, fetch, F, scatter=None):
    S, D = x.shape
    GW, HW = p["ln_g"].shape[1], p["hg_ng"].shape[1]
    G, T, _ = p["ws"].shape
    w = _Fetched(fetch)
    INW = 2 * GW + 4 * HW + 2 * D
    in_loc, br_loc, fi_loc = INW // N_DEV, D // N_DEV, 2 * F // N_DEV
    assert GW == HW and F % fi_loc == 0
    sh1, sc1, gt1, sh2, sc2, gt2 = (mod[:, k * D:(k + 1) * D] for k in range(6))
    bsb = jnp.broadcast_to(p["bs"][:, :, None], (G, T, GW // G))

    tm = _tile(S, 1024, 16)
    tmh = _tile(S, 512, 16)
    tn_in = _tile(in_loc, 1280)
    tn_d = _tile(D, 512)
    tn_br = _tile(br_loc, 512)
    tk_s = S
    tm_w = _tile(D, 1024)
    g_off = 2 * GW + 4 * HW

    h1 = _norm_mod("norm1", x, p["norm1_g"], sc1, sh1)
    z = _mm_nn_stacked("proj_in", h1, w.first("in", h1), tm=tm, tn=tn_in, tk=D)[0]
    ya = _gmlp_fwd(z, p["ln_g"], p["ln_b"], p["ws"], bsb, GW)
    yb, o_hg, states = _hg_fwd(z, p["hg_lb"], p["hg_ng"], HW)
    pa = _mm_nn_stacked("branch_gmlp", ya, w.first("bg", yb), tm=tm, tn=tn_br, tk=GW)[0]
    t_fi = w.first("fi_early", pa)

    def gates(ga_ref, gb_ref, ba_ref, bb_ref):
        return _sigmoid(ga_ref[...] + ba_ref[...]), _sigmoid(gb_ref[...] + bb_ref[...])

    def gate_specs(tn_):
        o1, o2 = g_off // tn_, (g_off + D) // tn_
        return [pl.BlockSpec((tm, tn_), lambda i, j, k: (i, o1 + j)), pl.BlockSpec((tm, tn_), lambda i, j, k: (i, o2 + j)),
                pl.BlockSpec((1, tn_), lambda i, j, k: (0, j)), pl.BlockSpec((1, tn_), lambda i, j, k: (0, D // tn_ + j))]

    def merge_ep(acc, ex, outs):
        ga, gb = gates(*ex[1:5])
        outs[0][...] = acc
        outs[1][...] = (ga * ex[0][...] + gb * acc).astype(BF16)

    tile_o = pl.BlockSpec((tm, tn_br), lambda i, j, k: (i, j))
    pb, y = _mm_nn_stacked(
        "branch_hg_merge", yb, w.first("bh", z), tm=tm, tn=tn_br, tk=HW, extras=[pa, z, z, p["b_gate"], p["b_gate"]],
        extra_specs=[tile_o, *gate_specs(tn_br)], out_shapes=[jax.ShapeDtypeStruct((S, D), F32), jax.ShapeDtypeStruct((S, D), BF16)],
        out_specs=[tile_o, tile_o], epilogue=merge_ep, after=t_fi)

    def resid_ep(acc, ex, outs):
        outs[0][...] = acc
        outs[1][...] = ex[0][...] + ex[1][...] * acc

    def resid_mm(name, a, b, res, gt, tm_):
        K = a.shape[1]
        t_o = pl.BlockSpec((tm_, tn_d), lambda i, j, k: (i, j))
        return _matmul(
            name, a, b, dims=_NN, grid_mnk=(S // tm_, D // tn_d, 1), tiles=(tm_, tn_d),
            a_spec=pl.BlockSpec((tm_, K), lambda i, j, k: (i, 0)), b_spec=pl.BlockSpec((K, tn_d), lambda i, j, k: (0, j)),
            extras=[res, gt], extra_specs=[t_o, pl.BlockSpec((1, tn_d), lambda i, j, k: (0, j))],
            out_shapes=[jax.ShapeDtypeStruct((S, D), F32)] * 2, out_specs=[t_o, t_o], epilogue=resid_ep)

    o1, xm = resid_mm("proj_out", y, w.first("out", z), x, gt1, tm)
    h2 = _norm_mod("norm2", xm, p["norm2_g"], sc2, sh2)
    hf, hf_fac = _ffn_in_swiglu(h2, w.first("fi", h2))
    o2, x3 = resid_mm("ffn_out", hf, w.first("fo", hf), xm, gt2, tmh)
    dx3, do2, vec_l = _loss_head(x3, tgt, p["final_g"], o2, gt2)

    nf = F // fi_loc

    def dswiglu_ep(acc, ex, outs):
        outs[0][0] = (acc * ex[0][0].astype(F32)).astype(BF16)
        outs[0][1] = (acc * ex[0][1].astype(F32)).astype(BF16)

    pair = pl.BlockSpec((2, tmh, fi_loc), lambda i, j, k: (0, i, j))
    dab = _matmul(
        "ffn_out_dx", do2, w["fo"], dims=_NT, grid_mnk=(S // tmh, nf, 1), tiles=(tmh, fi_loc),
        a_spec=pl.BlockSpec((tmh, D), lambda i, j, k: (i, 0)), b_spec=pl.BlockSpec((fi_loc, D), lambda i, j, k: (j, 0)),
        extras=[hf_fac], extra_specs=[pair], out_shapes=[jax.ShapeDtypeStruct((2, S, F), BF16)], out_specs=[pair],
        epilogue=dswiglu_ep)[0]
    start = (lambda name, grads: scatter[0](name, grads)) if scatter is not None else (lambda name, grads: None)
    push = (lambda name, after: scatter[1](name, after)) if scatter is not None else (lambda name, after: None)

    def zero(token):
        return 0.0 if token is None else token[0:1, 0:1]

    tm_f = _tile(F, 512)
    g_fo = _mm_tn("ffn_out_dw", hf, do2, pl.BlockSpec((tk_s, D), lambda i, j, k: (k, j)), Mo=F, No=D, S=S, tm=tm_f, tn=D, tk=tk_s)
    g_fi = _mm_tn("ffn_in_dw", h2, dab, pl.BlockSpec((None, tk_s, fi_loc), lambda i, j, k: (j // nf, k, j % nf)),
                  Mo=D, No=2 * F, S=S, tm=tm_w, tn=fi_loc, tk=tk_s, stacked_nloc=fi_loc, after=g_fo)
    t_ffn = start("scatter_ffn", dict(fo=g_fo, fi=g_fi))
    dh2 = _mm_nt_stacked("ffn_in_dx", pl.BlockSpec((None, tm, fi_loc), lambda i, j, k: (k // nf, i, k % nf)), dab, w["fi"],
                         M=S, tm=tm, tn=tm_w, tk=fi_loc, after=t_ffn)
    dxm, vec2, do1 = _norm_mod_bwd("norm2_bwd", dh2, xm, p["norm2_g"], sc2, dx3, o1, gt1)
    t_ffn = push("scatter_ffn", dxm)

    def dmerge_ep(acc, ex, outs):
        ga, gb = gates(*ex[2:6])
        outs[0][...] = (acc * ga).astype(BF16)
        outs[1][...] = (acc * gb).astype(BF16)
        outs[2][0] = (acc * ex[0][...] * ga * (1.0 - ga)).astype(BF16)
        outs[2][1] = (acc * ex[1][...] * gb * (1.0 - gb)).astype(BF16)

    t_o = pl.BlockSpec((tm, tn_d), lambda i, j, k: (i, j))
    dpa, dpb, dg2 = _matmul(
        "proj_out_dx", do1, w["out"], dims=_NT, grid_mnk=(S // tm, D // tn_d, 1), tiles=(tm, tn_d),
        a_spec=pl.BlockSpec((tm, D), lambda i, j, k: (i, 0)), b_spec=pl.BlockSpec((tn_d, D), lambda i, j, k: (j, 0)),
        extras=[pa, pb, z, z, p["b_gate"], p["b_gate"]], extra_specs=[t_o, t_o, *gate_specs(tn_d)],
        out_shapes=[jax.ShapeDtypeStruct((S, D), BF16), jax.ShapeDtypeStruct((S, D), BF16), jax.ShapeDtypeStruct((2, S, D), BF16)],
        out_specs=[t_o, t_o, pl.BlockSpec((2, tm, tn_d), lambda i, j, k: (0, i, j))], epilogue=dmerge_ep, after=t_ffn)
    g_out = _mm_tn("proj_out_dw", y, do1, pl.BlockSpec((tk_s, D), lambda i, j, k: (k, j)), Mo=D, No=D, S=S, tm=tn_d, tn=D, tk=tk_s)
    tn_g = _tile(GW, 512)
    b_br = pl.BlockSpec((tk_s, br_loc), lambda i, j, k: (k, j))
    g_bg = _mm_tn("branch_gmlp_dw", ya, dpa, b_br, Mo=GW, No=D, S=S, tm=tn_g, tn=br_loc, tk=tk_s, stacked_nloc=br_loc)
    g_bh = _mm_tn("branch_hg_dw", yb, dpb, b_br, Mo=HW, No=D, S=S, tm=tn_g, tn=br_loc, tk=tk_s, stacked_nloc=br_loc)
    t_mix = start("scatter_mixer", dict(out=g_out, bg=g_bg, bh=g_bh))
    def branch_dx(name, dp, wg):
        flat = jnp.swapaxes(wg, 0, 1).reshape(wg.shape[1], D)
        return _matmul(
            name, dp, flat, dims=_NT, grid_mnk=(S // tm, GW // tn_g, 1), tiles=(tm, tn_g),
            a_spec=pl.BlockSpec((tm, D), lambda i, j, k: (i, 0)), b_spec=pl.BlockSpec((tn_g, D), lambda i, j, k: (j, 0)),
            out_shapes=[jax.ShapeDtypeStruct((S, GW), F32)], out_specs=[pl.BlockSpec((tm, tn_g), lambda i, j, k: (i, j))],
            epilogue=_store(F32), after=t_mix)[0]

    dya = branch_dx("branch_gmlp_dx", dpa, w["bg"])
    dyb = branch_dx("branch_hg_dx", dpb, w["bh"])
    db_gate = _colsum2(dg2)
    dz_gmlp, dln, dws, dbs = _gmlp_bwd(z, dya, p["ln_g"], p["ln_b"], p["ws"], bsb, GW)
    t_mix = push("scatter_mixer", dz_gmlp)
    dz, dng, dhlb = _hg_bwd(z, o_hg, states, dyb, p["hg_lb"], p["hg_ng"] + zero(t_mix), HW, dz_gmlp, dg2)
    half = D // 2
    tm_h = _tile(half, 1024)
    g_in = []
    t_in = None
    for hname, h in (("a", 0), ("b", 1)):
        g_in.append(_mm_tn("proj_in_dw_" + hname, h1, dz, pl.BlockSpec((tk_s, in_loc), lambda i, j, k: (k, j)), Mo=half, No=INW, S=S,
                           tm=tm_h, tn=in_loc, tk=tk_s, stacked_nloc=in_loc, after=t_in, a_off=h * (half // tm_h)))
        t_in = start("scatter_proj_in_" + hname, {"w_in_" + hname: g_in[-1]})
    t_in = push("scatter_proj_in_a", t_in)
    dh1 = _mm_nt_stacked("proj_in_dx", pl.BlockSpec((tm, in_loc), lambda i, j, k: (i, k)), dz, w["in"], M=S, tm=tm, tn=tm_w, tk=in_loc,
                         after=t_in)
    dx, vec1 = _norm_mod_bwd("norm1_bwd", dh1, x, p["norm1_g"], sc1, dxm)

    dmod = jnp.concatenate([vec1[0:1], vec1[1:2], vec2[3:4], vec2[0:1], vec2[1:2], vec_l[2:3]], axis=1)
    small = dict(norm1_g=vec1[2:3], b_gate=db_gate.reshape(1, 2 * D), ln_g=dln[0:1], ln_b=dln[1:2], ws=dws, bs=dbs.reshape(G, T),
                 hg_lb=dhlb, hg_ng=dng[0:1], norm2_g=vec2[2:3], final_g=vec_l[1:2], loss=vec_l[0:1, 0:LANES])
    big = dict(w_in_a=g_in[0], w_in_b=g_in[1], bg=g_bg, bh=g_bh, out=g_out, fi=g_fi, fo=g_fo)
    return dx, big, small, dmod


_SMALL = ("b_ada", "norm1_g", "b_gate", "ln_g", "ln_b", "ws", "bs", "hg_lb", "hg_ng", "norm2_g", "final_g")


def _pack(parts, rows_mult=8):
    flat = [a.reshape(-1) for a in parts]
    offs, n = [], 0
    for a in flat:
        offs.append(n)
        n += a.shape[0]
    pad = (-n) % (LANES * rows_mult)
    if pad:
        flat.append(jnp.zeros((pad,), F32))
    return jnp.concatenate(flat).reshape(-1, LANES), offs


def kernel(x, c, w_ada, b_ada, norm1_g, w_in, b_gate, gmlp_ln_g, gmlp_ln_b, gmlp_ws, gmlp_bs, hg_lb, hg_norm_g, w_branch_gmlp, w_branch_hg, w_out, norm2_g, w_ffn_in, w_ffn_out, final_norm_g, loss_target, m_w_ada, m_b_ada, m_norm1_g, m_w_in, m_b_gate, m_gmlp_ln_g, m_gmlp_ln_b, m_gmlp_ws, m_gmlp_bs, m_hg_lb, m_hg_norm_g, m_w_branch_gmlp, m_w_branch_hg, m_w_out, m_norm2_g, m_w_ffn_in, m_w_ffn_out, m_final_norm_g, v_w_ada, v_b_ada, v_norm1_g, v_w_in, v_b_gate, v_gmlp_ln_g, v_gmlp_ln_b, v_gmlp_ws, v_gmlp_bs, v_hg_lb, v_hg_norm_g, v_w_branch_gmlp, v_w_branch_hg, v_w_out, v_norm2_g, v_w_ffn_in, v_w_ffn_out, v_final_norm_g):
    S, D = x.shape[1], x.shape[2]
    ada_loc = w_ada.shape[2]
    me = 4 * lax.axis_index("x") + 2 * lax.axis_index("y") + lax.axis_index("c")

    c_all = _allgather_small("gather_c", c.reshape(D // LANES, LANES)).reshape(N_DEV, D)
    mod_cols, c_act = _ada_mod(jnp.pad(c_all, ((0, 16 - N_DEV), (0, 0))), w_ada[0])
    mod_all = _allgather_small("gather_mod", mod_cols[:N_DEV].reshape(-1, LANES)).reshape(N_DEV, N_DEV, ada_loc)
    mod = lax.dynamic_index_in_dim(mod_all, me, axis=1, keepdims=False).reshape(1, N_DEV * ada_loc) + b_ada

    def empty_hbm(shape, dtype):
        return pltpu.with_memory_space_constraint(lax.empty(shape, dtype), pltpu.HBM)

    groups = dict(gather_in=dict(keys=["in"], src=[w_in], forward=True),
                  gather_mixer=dict(keys=["bg", "bh", "out"], src=[w_branch_gmlp, w_branch_hg, w_out], forward=False),
                  gather_ffn_in=dict(keys=["fi"], src=[w_ffn_in], forward=True),
                  gather_ffn_out=dict(keys=["fo"], src=[w_ffn_out], forward=False))
    group_of = {k: gname for gname, g in groups.items() for k in g["keys"]}

    def first_hop(gname, after):
        g = groups[gname]
        n = len(g["keys"])
        shards = [a[0].astype(BF16) for a in g["src"]]
        outs = [lax.dynamic_update_slice(lax.empty((N_DEV, *s.shape), BF16), s[None], (me, 0, 0)) for s in shards]
        if g["forward"]:
            *g["hop"], token = _split_start(gname + "_hop1", shards + outs, n * 3, _forward_first_copies(n), after=after)
        else:
            *g["hop"], token = _split_start(gname + "_hop1", shards + outs, n * N_CHIP, _gather_first_copies(n), after=after)
        return token

    def second_hop(gname, after):
        g = groups[gname]
        n = len(g["keys"])
        *g["hop"], token = _split_relay(gname + "_hop2", g["hop"][2], g["hop"][0], g["hop"][1], after,
                                        _forward_first_copies(n), n * 4, _forward_second_copies(n))
        return token

    def finish(gname, after):
        g = groups[gname]
        n = len(g["keys"])
        send_sems, recv_sems, bufs = g["hop"]
        if g["forward"]:
            send_sems, recv_sems, bufs, _ = _split_relay(gname + "_hop3", bufs, send_sems, recv_sems, after,
                                                         _forward_second_copies(n), n, _forward_third_copies(n))
            bufs = _split_wait(gname + "_wait", bufs, send_sems, recv_sems, after, _forward_third_copies(n))
        else:
            send_sems, recv_sems, bufs, _ = _split_relay(gname + "_relay", bufs, send_sems, recv_sems, after,
                                                         _gather_first_copies(n), n * (N_CHIP - 1), _gather_relay_copies(n))
            bufs = _split_wait(gname + "_wait", bufs, send_sems, recv_sems, after, _gather_relay_copies(n))
        g["done"] = dict(zip(g["keys"], bufs[n:]))

    token = first_hop("gather_in", mod_all)
    mod = mod + token[0:1, 0:1]

    def fetch(key, after):
        if key == "in":
            t = second_hop("gather_in", after)
            t = first_hop("gather_mixer", t)
            t = first_hop("gather_ffn_in", t)
            finish("gather_in", t)
        elif key == "fi_early":
            return first_hop("gather_ffn_out", second_hop("gather_ffn_in", after))
        elif "done" not in groups[group_of[key]]:
            finish(group_of[key], after)
        arr = groups[group_of[key]]["done"][key]
        return arr.reshape(-1, D) if key in ("out", "fo") else arr

    p = dict(norm1_g=norm1_g, b_gate=b_gate, ln_g=gmlp_ln_g, ln_b=gmlp_ln_b, ws=gmlp_ws[0], bs=gmlp_bs[0], hg_lb=hg_lb,
             hg_ng=hg_norm_g, norm2_g=norm2_g, final_g=final_norm_g.reshape(1, D))

    in_flight = {}
    c_idx = lax.axis_index("c").astype(jnp.int32).reshape(1)
    my_chip = 2 * lax.axis_index("x") + lax.axis_index("y")

    def scatter_start(name, grads):
        keys = list(grads)
        n = len(keys)
        stacks = [grads[k].reshape(N_DEV, -1, grads[k].shape[-1]) for k in keys]
        lands = [empty_hbm((N_CHIP, *g.shape[1:]), g.dtype) for g in stacks]
        send_sems, recv_sems, bufs, token = _split_start(name + "_d2d", stacks + lands, n * N_CHIP, _to_sibling_copies(n))
        in_flight[name] = dict(keys=keys, stage1=(send_sems, recv_sems, bufs))
        return token

    def scatter_push(name, after):
        f = in_flight[name]
        n = len(f["keys"])
        send_sems, recv_sems, bufs = f["stage1"]
        bufs = _split_wait(name + "_d2d_wait", bufs, send_sems, recv_sems, after, _to_sibling_copies(n))
        sums = [_chip_sum(f"{name}_sum_{k}", bufs[i], bufs[n + i], c_idx) for i, k in enumerate(f["keys"])]
        lands = [empty_hbm((N_CHIP - 1, *s.shape[1:]), s.dtype) for s in sums]
        send_sems, recv_sems, bufs, token = _split_start(name + "_ici", sums + lands, n * (N_CHIP - 1), _to_owner_copies(n))
        f["stage2"] = (send_sems, recv_sems, bufs)
        return token

    grad_x, _, small, dmod = _local_step(x[0], loss_target[0], mod, p, fetch, w_ffn_out.shape[1] * N_DEV, (scatter_start, scatter_push))

    small["b_ada"] = dmod
    packed, offs = _pack([small[k] for k in _SMALL] + [small["loss"]])
    sg_send, sg_recv, sg_bufs, t_tail = _split_start(
        "gather_small_start", [packed, lax.dynamic_update_slice(lax.empty((N_DEV, *packed.shape), F32), packed[None], (me, 0, 0))],
        N_DEV - 1, _small_gather_copies)
    t_tail = scatter_push("scatter_proj_in_b", t_tail)
    big_w = dict(w_in=(w_in, m_w_in, v_w_in, "w_in"), bg=(w_branch_gmlp, m_w_branch_gmlp, v_w_branch_gmlp, "w_branch_gmlp"),
                 bh=(w_branch_hg, m_w_branch_hg, v_w_branch_hg, "w_branch_hg"), out=(w_out, m_w_out, v_w_out, "w_out"),
                 fi=(w_ffn_in, m_w_ffn_in, v_w_ffn_in, "w_ffn_in"), fo=(w_ffn_out, m_w_ffn_out, v_w_ffn_out, "w_ffn_out"))
    upd = {}

    def land_and_update(name, after):
        keys = in_flight[name]["keys"]
        n = len(keys)
        send_sems, recv_sems, bufs = in_flight[name]["stage2"]
        bufs = _split_wait(name + "_ici_wait", bufs, send_sems, recv_sems, after, _to_owner_copies(n))
        for i, k in enumerate(keys):
            parts = [lax.dynamic_index_in_dim(bufs[i], my_chip, axis=0, keepdims=True), bufs[n + i]]
            if k in big_w:
                wt, mt, vt, out_name = big_w[k]
                upd[out_name] = _adamw("adamw_" + out_name, wt[0], mt[0], vt[0], parts)
            else:
                wt, mt, vt, out_name = big_w["w_in"]
                upd[out_name] = _adamw("adamw_" + k, wt[0], mt[0], vt[0], parts, row0=0 if k == "w_in_a" else parts[0].shape[1],
                                       into=upd.get(out_name))
            after = upd[out_name][1]
        return after

    after = land_and_update("scatter_mixer", land_and_update("scatter_ffn", t_tail))
    gathered = _split_wait("gather_small_wait", sg_bufs, sg_send, sg_recv, after, _small_gather_copies)[1]
    wp = dict(p, b_ada=b_ada)
    ms = dict(b_ada=m_b_ada, norm1_g=m_norm1_g, b_gate=m_b_gate, ln_g=m_gmlp_ln_g, ln_b=m_gmlp_ln_b, ws=m_gmlp_ws, bs=m_gmlp_bs,
              hg_lb=m_hg_lb, hg_ng=m_hg_norm_g, norm2_g=m_norm2_g, final_g=m_final_norm_g)
    vs = dict(b_ada=v_b_ada, norm1_g=v_norm1_g, b_gate=v_b_gate, ln_g=v_gmlp_ln_g, ln_b=v_gmlp_ln_b, ws=v_gmlp_ws, bs=v_gmlp_bs,
              hg_lb=v_hg_lb, hg_ng=v_hg_norm_g, norm2_g=v_norm2_g, final_g=v_final_norm_g)
    w_sm, _ = _pack([wp[k] for k in _SMALL])
    m_sm, _ = _pack([ms[k] for k in _SMALL])
    v_sm, _ = _pack([vs[k] for k in _SMALL])
    sm_out = _small_update(gathered, w_sm, m_sm, v_sm, after)
    shapes = dict(b_ada=b_ada.shape, norm1_g=norm1_g.shape, b_gate=b_gate.shape, ln_g=gmlp_ln_g.shape, ln_b=gmlp_ln_b.shape,
                  ws=gmlp_ws.shape, bs=gmlp_bs.shape, hg_lb=hg_lb.shape, hg_ng=hg_norm_g.shape, norm2_g=norm2_g.shape,
                  final_g=final_norm_g.shape)

    def unpack(arr, k):
        i = _SMALL.index(k)
        n = math.prod(shapes[k])
        return arr.reshape(-1)[offs[i]:offs[i] + n].reshape(shapes[k])

    loss = sm_out[0].reshape(-1)[offs[len(_SMALL)]]

    dmod_all = gathered.reshape(N_DEV, -1)[:, offs[0]:offs[0] + N_DEV * ada_loc]
    dmod_loc = lax.dynamic_slice_in_dim(dmod_all, me * ada_loc, ada_loc, axis=1)
    ca_t = jnp.pad(c_act[:N_DEV].T, ((0, 0), (0, LANES - N_DEV))).astype(BF16)
    dm_p = jnp.pad(dmod_loc, ((0, LANES - N_DEV), (0, 0))).astype(BF16)
    tm_a = _tile(D, 512)
    g_ada = _matmul(
        "ada_dw", ca_t, dm_p, dims=_NN, grid_mnk=(D // tm_a, 1, 1), tiles=(tm_a, ada_loc),
        a_spec=pl.BlockSpec((tm_a, LANES), lambda i, j, k: (i, 0)), b_spec=pl.BlockSpec((LANES, ada_loc), lambda i, j, k: (0, 0)),
        out_shapes=[jax.ShapeDtypeStruct((1, D, ada_loc), F32)], out_specs=[pl.BlockSpec((None, tm_a, ada_loc), lambda i, j, k: (0, i, 0))],
        epilogue=_store(F32))[0]
    upd["w_ada"] = _adamw("adamw_w_ada", w_ada[0], m_w_ada[0], v_w_ada[0], [g_ada])
    land_and_update("scatter_proj_in_b", land_and_update("scatter_proj_in_a", upd["w_ada"][1]))

    order = ("w_ada", "b_ada", "norm1_g", "w_in", "b_gate", "ln_g", "ln_b", "ws", "bs", "hg_lb", "hg_ng", "w_branch_gmlp", "w_branch_hg",
             "w_out", "norm2_g", "w_ffn_in", "w_ffn_out", "final_g")
    outs = [loss, grad_x[None]]
    for idx in range(4):
        for k in order:
            outs.append(upd[k][idx][None] if k in upd else unpack(sm_out[idx], k))
    return tuple(outs)
```

```python
import functools
import math

import jax
import jax.numpy as jnp
from jax import lax
from jax.experimental import pallas as pl
from jax.experimental.pallas import tpu as pltpu

F32 = jnp.float32
BF16 = jnp.bfloat16
N_DEV = 8
EPS = 1e-6
LANES = 128
HG_DK = 128
HG_CHUNK = 64
HG_MID = HG_CHUNK // 2 - 1
EXP_CLAMP = 80.0
VMEM_LIMIT = 48 * 1024 * 1024
ADAM_LR, ADAM_B1, ADAM_B2, ADAM_EPS, ADAM_WD, ADAM_STEP = 0.001, 0.9, 0.999, 1e-08, 0.01, 10
MESH = pl.DeviceIdType.MESH

_NN = (((1,), (0,)), ((), ()))
_NT = (((1,), (1,)), ((), ()))
_TN = (((0,), (0,)), ((), ()))


def _dot(a, b, dims=_NN):
    return lax.dot_general(a.astype(BF16), b.astype(BF16), dims, preferred_element_type=F32)


def _tile(n, target, mult=LANES):
    best = None
    for t in range(mult, min(n, target) + 1, mult):
        if n % t == 0:
            best = t
    return n if best is None else best


def _cparams(sem):
    return pltpu.CompilerParams(dimension_semantics=sem, vmem_limit_bytes=VMEM_LIMIT)


def _sigmoid(x):
    return 1.0 / (1.0 + jnp.exp(-x))


def _gelu_parts(x):
    k0 = math.sqrt(2.0 / math.pi)
    x2 = x * x
    t = jnp.tanh(k0 * (x + 0.044715 * x * x2))
    g = 0.5 * x * (1.0 + t)
    dg = 0.5 * (1.0 + t) + 0.5 * x * (1.0 - t * t) * (k0 * (1.0 + 3.0 * 0.044715 * x2))
    return g, dg


def _split3(x):
    h = x.astype(BF16)
    r = x - h.astype(F32)
    m = r.astype(BF16)
    lo = (r - m.astype(F32)).astype(BF16)
    return h, m, lo


def _ones_dot(mat01, x):
    h, m, lo = _split3(x)
    d = functools.partial(lax.dot_general, dimension_numbers=_NN, preferred_element_type=F32)
    return d(mat01, h) + d(mat01, m) + d(mat01, lo)


def _matmul(name, a, b, *, dims, grid_mnk, tiles, a_spec, b_spec, extras=(), extra_specs=(), out_shapes, out_specs, epilogue, after=None,
            sem=None):
    gm, gn, nk = grid_mnk
    tm, tn = tiles
    n_ex, n_out = len(extras), len(out_shapes)
    held = [] if after is None else [after]

    def body(*refs):
        a_ref, b_ref = refs[0], refs[1]
        ex = refs[2:2 + n_ex]
        outs = refs[2 + n_ex + len(held):2 + n_ex + len(held) + n_out]
        more = () if sem is None else (pl.program_id(0) == 0,)
        if nk == 1:
            epilogue(lax.dot_general(a_ref[...], b_ref[...], dims, preferred_element_type=F32), ex, outs, *more)
            return
        acc = refs[-1]
        k = pl.program_id(2)

        @pl.when(k == 0)
        def _():
            acc[...] = jnp.zeros_like(acc)

        acc[...] += lax.dot_general(a_ref[...], b_ref[...], dims, preferred_element_type=F32)

        @pl.when(k == nk - 1)
        def _():
            epilogue(acc[...], ex, outs, *more)

    return pl.pallas_call(
        body, name=name, grid=(gm, gn, nk), in_specs=[a_spec, b_spec, *extra_specs] + [pl.BlockSpec(memory_space=pl.ANY)] * len(held),
        out_specs=list(out_specs), out_shape=list(out_shapes), scratch_shapes=[] if nk == 1 else [pltpu.VMEM((tm, tn), F32)],
        compiler_params=_cparams(sem or ("parallel", "parallel", "arbitrary")),
    )(a, b, *extras, *held)


def _store(dtype):
    def ep(acc, ex, outs):
        outs[0][...] = acc.astype(dtype)
    return ep


def _mm_nn_stacked(name, a, wg, *, tm, tn, tk, out_dtype=F32, extras=(), extra_specs=(), out_shapes=None, out_specs=None, epilogue=None,
                   after=None):
    M, K = a.shape
    _, _, nloc = wg.shape
    N = nloc * N_DEV
    q = nloc // tn
    if out_shapes is None:
        out_shapes = [jax.ShapeDtypeStruct((M, N), out_dtype)]
        out_specs = [pl.BlockSpec((tm, tn), lambda i, j, k: (i, j))]
        epilogue = _store(out_dtype)
    return _matmul(
        name, a, wg, dims=_NN, grid_mnk=(M // tm, N // tn, K // tk), tiles=(tm, tn),
        a_spec=pl.BlockSpec((tm, tk), lambda i, j, k: (i, k)),
        b_spec=pl.BlockSpec((None, tk, tn), lambda i, j, k: (j // q, k, j % q)),
        extras=extras, extra_specs=extra_specs, out_shapes=out_shapes, out_specs=out_specs, epilogue=epilogue, after=after)


def _mm_nt_stacked(name, a_spec, a, wg, *, M, tm, tn, tk, out_dtype=F32, after=None, extras=(), extra_specs=(), out_shapes=None,
                   out_specs=None, epilogue=None, sem=None):
    _, Kw, nloc = wg.shape
    q = nloc // tk
    single = out_shapes is None
    if single:
        out_shapes = [jax.ShapeDtypeStruct((M, Kw), out_dtype)]
        out_specs = [pl.BlockSpec((tm, tn), lambda i, j, k: (i, j))]
        epilogue = _store(out_dtype)
    res = _matmul(
        name, a, wg, dims=_NT, grid_mnk=(M // tm, Kw // tn, (nloc * N_DEV) // tk), tiles=(tm, tn),
        a_spec=a_spec, b_spec=pl.BlockSpec((None, tn, tk), lambda i, j, k: (k // q, j, k % q)),
        extras=extras, extra_specs=extra_specs, out_shapes=out_shapes, out_specs=out_specs, epilogue=epilogue, after=after, sem=sem)
    return res[0] if single else res


def _mm_tn(name, a, b, b_spec, *, Mo, No, S, tm, tn, tk, stacked_nloc=None, after=None, a_off=0):
    if stacked_nloc is None:
        out_shape = jax.ShapeDtypeStruct((Mo, No), BF16)
        out_spec = pl.BlockSpec((tm, tn), lambda i, j, k: (i, j))
    else:
        q = stacked_nloc // tn
        out_shape = jax.ShapeDtypeStruct((N_DEV, Mo, stacked_nloc), BF16)
        out_spec = pl.BlockSpec((None, tm, tn), lambda i, j, k: (j // q, i, j % q))
    return _matmul(
        name, a, b, dims=_TN, grid_mnk=(Mo // tm, No // tn, S // tk), tiles=(tm, tn),
        a_spec=pl.BlockSpec((tk, tm), lambda i, j, k: (k, i + a_off)), b_spec=b_spec,
        out_shapes=[out_shape], out_specs=[out_spec], epilogue=_store(BF16), after=after)[0]


def _norm_mod(name, x, g, sc, sh):
    S, D = x.shape
    tm = _tile(S, 256, 8)

    def body(x_ref, g_ref, sc_ref, sh_ref, h_ref):
        xv = x_ref[...]
        r = lax.rsqrt(jnp.mean(xv * xv, axis=-1, keepdims=True) + EPS)
        h = (xv * r) * g_ref[...]
        h_ref[...] = (h * (1.0 + sc_ref[...]) + sh_ref[...]).astype(BF16)

    row = pl.BlockSpec((tm, D), lambda i: (i, 0))
    vec = pl.BlockSpec((1, D), lambda i: (0, 0))
    return pl.pallas_call(body, name=name, grid=(S // tm,), in_specs=[row, vec, vec, vec], out_specs=row,
                          out_shape=jax.ShapeDtypeStruct((S, D), BF16), compiler_params=_cparams(("parallel",)))(x, g, sc, sh)


def _norm_mod_bwd_rows(first, dh_v, x_ref, g_ref, sc_ref, dres_ref, dx_ref, vec_ref, o_ref=None, gt_ref=None, do_ref=None):
    @pl.when(first)
    def _():
        vec_ref[...] = jnp.zeros_like(vec_ref)

    xv, gv = x_ref[...], g_ref[...]
    r = lax.rsqrt(jnp.mean(xv * xv, axis=-1, keepdims=True) + EPS)
    xn = xv * r
    one_sc = 1.0 + sc_ref[...]
    vec_ref[0:1, :] += jnp.sum(dh_v, axis=0, keepdims=True)
    vec_ref[1:2, :] += jnp.sum(dh_v * (xn * gv), axis=0, keepdims=True)
    vec_ref[2:3, :] += jnp.sum(dh_v * one_sc * xn, axis=0, keepdims=True)
    dxn = dh_v * one_sc * gv
    dx = dres_ref[...] + r * (dxn - xn * jnp.mean(dxn * xn, axis=-1, keepdims=True))
    dx_ref[...] = dx
    if o_ref is not None:
        vec_ref[3:4, :] += jnp.sum(dx * o_ref[...], axis=0, keepdims=True)
        do_ref[...] = (dx * gt_ref[...]).astype(BF16)


def _norm_mod_bwd(name, dh, x, g, sc, dres, o=None, gt=None):
    S, D = x.shape
    tm = _tile(S, 256, 8)
    gated = o is not None

    def body(*refs):
        if gated:
            dh_ref, x_ref, g_ref, sc_ref, dres_ref, o_ref, gt_ref, dx_ref, vec_ref, do_ref = refs
        else:
            dh_ref, x_ref, g_ref, sc_ref, dres_ref, dx_ref, vec_ref = refs
            o_ref = gt_ref = do_ref = None
        _norm_mod_bwd_rows(pl.program_id(0) == 0, dh_ref[...], x_ref, g_ref, sc_ref, dres_ref, dx_ref, vec_ref, o_ref, gt_ref, do_ref)

    row = pl.BlockSpec((tm, D), lambda i: (i, 0))
    vec = pl.BlockSpec((1, D), lambda i: (0, 0))
    acc = pl.BlockSpec((8, D), lambda i: (0, 0))
    ins = [dh, x, g, sc, dres] + ([o, gt] if gated else [])
    in_specs = [row, row, vec, vec, row] + ([row, vec] if gated else [])
    out_shape = [jax.ShapeDtypeStruct((S, D), F32), jax.ShapeDtypeStruct((8, D), F32)]
    out_specs = [row, acc]
    if gated:
        out_shape.append(jax.ShapeDtypeStruct((S, D), BF16))
        out_specs.append(row)
    return pl.pallas_call(body, name=name, grid=(S // tm,), in_specs=in_specs, out_specs=out_specs, out_shape=out_shape,
                          compiler_params=_cparams(("arbitrary",)))(*ins)


def _loss_head(x3, tgt, gf, o2, gt2):
    S, D = x3.shape
    tm = _tile(S, 256, 8)

    def body(x_ref, t_ref, g_ref, o_ref, gt_ref, dx_ref, do_ref, vec_ref):
        i = pl.program_id(0)

        @pl.when(i == 0)
        def _():
            vec_ref[...] = jnp.zeros_like(vec_ref)

        xv, gv = x_ref[...], g_ref[...]
        r = lax.rsqrt(jnp.mean(xv * xv, axis=-1, keepdims=True) + EPS)
        xn = xv * r
        e = xn * gv - t_ref[...]
        tok = 0.5 * jnp.mean(e * e, axis=-1, keepdims=True)
        vec_ref[0:1, :] += jnp.broadcast_to(jnp.sum(tok, axis=0, keepdims=True), (1, D))
        dy = e * (1.0 / D)
        vec_ref[1:2, :] += jnp.sum(dy * xn, axis=0, keepdims=True)
        dxn = dy * gv
        dx = r * (dxn - xn * jnp.mean(dxn * xn, axis=-1, keepdims=True))
        dx_ref[...] = dx
        vec_ref[2:3, :] += jnp.sum(dx * o_ref[...], axis=0, keepdims=True)
        do_ref[...] = (dx * gt_ref[...]).astype(BF16)

    row = pl.BlockSpec((tm, D), lambda i: (i, 0))
    vec = pl.BlockSpec((1, D), lambda i: (0, 0))
    return pl.pallas_call(
        body, name="loss_head", grid=(S // tm,), in_specs=[row, row, vec, row, vec],
        out_specs=[row, row, pl.BlockSpec((8, D), lambda i: (0, 0))],
        out_shape=[jax.ShapeDtypeStruct((S, D), F32), jax.ShapeDtypeStruct((S, D), BF16), jax.ShapeDtypeStruct((8, D), F32)],
        compiler_params=_cparams(("arbitrary",)))(x3, tgt, gf, o2, gt2)


def _ffn_in_swiglu(h, wg):
    S, D = h.shape
    _, _, tf = wg.shape
    nf = N_DEV // 2
    F = nf * tf
    tm = _tile(S, 256, 16)

    def body(h_ref, wa_ref, wu_ref, hf_ref, fac_ref):
        hv = h_ref[...]
        a = lax.dot_general(hv, wa_ref[...], _NN, preferred_element_type=F32)
        up = lax.dot_general(hv, wu_ref[...], _NN, preferred_element_type=F32)
        sa = _sigmoid(a)
        silu = a * sa
        hf_ref[...] = (silu * up).astype(BF16)
        fac_ref[0] = (up * (sa * (1.0 + a * (1.0 - sa)))).astype(BF16)
        fac_ref[1] = silu.astype(BF16)

    return pl.pallas_call(
        body, name="ffn_in_swiglu", grid=(nf, S // tm),
        in_specs=[pl.BlockSpec((tm, D), lambda j, i: (i, 0)), pl.BlockSpec((None, D, tf), lambda j, i: (j, 0, 0)),
                  pl.BlockSpec((None, D, tf), lambda j, i: (j + nf, 0, 0))],
        out_specs=[pl.BlockSpec((tm, tf), lambda j, i: (i, j)), pl.BlockSpec((2, tm, tf), lambda j, i: (0, i, j))],
        out_shape=[jax.ShapeDtypeStruct((S, F), BF16), jax.ShapeDtypeStruct((2, S, F), BF16)],
        compiler_params=_cparams(("parallel", "parallel")))(h, wg, wg)


def _colsum2(dg2):
    _, S, D = dg2.shape
    tm = _tile(S, 256, 16)

    def body(x_ref, o_ref):
        @pl.when(pl.program_id(0) == 0)
        def _():
            o_ref[...] = jnp.zeros_like(o_ref)

        o_ref[0:1, :] += jnp.sum(x_ref[0].astype(F32), axis=0, keepdims=True)
        o_ref[1:2, :] += jnp.sum(x_ref[1].astype(F32), axis=0, keepdims=True)

    return pl.pallas_call(body, name="gate_bias_grad", grid=(S // tm,), in_specs=[pl.BlockSpec((2, tm, D), lambda i: (0, i, 0))],
                          out_specs=pl.BlockSpec((2, D), lambda i: (0, 0)), out_shape=jax.ShapeDtypeStruct((2, D), F32),
                          compiler_params=_cparams(("arbitrary",)))(dg2)


def _gmlp_common(u_ref, v_ref, lg_ref, lb_ref, ws_ref, bsb_ref, G, T, Dg):
    ug, dug = _gelu_parts(u_ref[...])
    vg, dvg = _gelu_parts(v_ref[...])
    mu = jnp.mean(vg, axis=-1, keepdims=True)
    vc = vg - mu
    rstd = lax.rsqrt(jnp.mean(vc * vc, axis=-1, keepdims=True) + EPS)
    vhat = vc * rstd
    vn = vhat * lg_ref[...] + lb_ref[...]
    row = lax.broadcasted_iota(jnp.int32, (T, T), 0)
    col = lax.broadcasted_iota(jnp.int32, (T, T), 1)
    tril = row >= col
    s = []
    for g in range(G):
        w = jnp.where(tril, ws_ref[g], 0.0)
        s.append(_dot(w, vn[:, g * Dg:(g + 1) * Dg]) + bsb_ref[g])
    return ug, dug, dvg, rstd, vhat, vn, tril, s


def _gmlp_fwd(z, ln_g, ln_b, ws, bsb, GW):
    S = z.shape[0]
    G, T, _ = ws.shape
    Dg = GW // G

    def body(u_ref, v_ref, lg_ref, lb_ref, ws_ref, bsb_ref, ya_ref):
        ug, _, _, _, _, _, _, s = _gmlp_common(u_ref, v_ref, lg_ref, lb_ref, ws_ref, bsb_ref, G, T, Dg)
        for g in range(G):
            sl = slice(g * Dg, (g + 1) * Dg)
            ya_ref[:, sl] = (ug[:, sl] * s[g]).astype(BF16)

    vec = pl.BlockSpec((1, GW), lambda c: (0, 0))
    return pl.pallas_call(
        body, name="gmlp_fwd", grid=(S // T,),
        in_specs=[pl.BlockSpec((T, GW), lambda c: (c, 0)), pl.BlockSpec((T, GW), lambda c: (c, 1)), vec, vec,
                  pl.BlockSpec((G, T, T), lambda c: (0, 0, 0)), pl.BlockSpec((G, T, Dg), lambda c: (0, 0, 0))],
        out_specs=pl.BlockSpec((T, GW), lambda c: (c, 0)), out_shape=jax.ShapeDtypeStruct((S, GW), BF16),
        compiler_params=_cparams(("parallel",)))(z, z, ln_g, ln_b, ws, bsb)


def _gmlp_bwd(z, dya, ln_g, ln_b, ws, bsb, GW):
    S = z.shape[0]
    G, T, _ = ws.shape
    Dg = GW // G
    nc = S // T

    def body(u_ref, v_ref, dya_ref, lg_ref, lb_ref, ws_ref, bsb_ref, dz_ref, dln_ref, dws_ref, dbs_ref, dbs_acc, dvh):
        c = pl.program_id(0)

        @pl.when(c == 0)
        def _():
            dln_ref[...] = jnp.zeros_like(dln_ref)
            dws_ref[...] = jnp.zeros_like(dws_ref)
            dbs_acc[...] = jnp.zeros_like(dbs_acc)

        ug, dug, dvg, rstd, vhat, vn, tril, s = _gmlp_common(u_ref, v_ref, lg_ref, lb_ref, ws_ref, bsb_ref, G, T, Dg)
        dya_v = dya_ref[...]
        for g in range(G):
            sl = slice(g * Dg, (g + 1) * Dg)
            dy_g = dya_v[:, sl]
            dz_ref[:, sl] = (dy_g * s[g] * dug[:, sl]).astype(BF16)
            ds = dy_g * ug[:, sl]
            dbs_acc[g] += ds
            w = jnp.where(tril, ws_ref[g], 0.0)
            dvn_g = _dot(w, ds, _TN)
            dws_ref[g] += jnp.where(tril, _dot(ds, vn[:, sl], _NT), 0.0)
            dln_ref[0:1, sl] += jnp.sum(dvn_g * vhat[:, sl], axis=0, keepdims=True)
            dln_ref[1:2, sl] += jnp.sum(dvn_g, axis=0, keepdims=True)
            dvh[:, sl] = dvn_g * lg_ref[:, sl]
        dvhat = dvh[...]
        m1 = jnp.mean(dvhat, axis=-1, keepdims=True)
        m2 = jnp.mean(dvhat * vhat, axis=-1, keepdims=True)
        dz_ref[:, GW:2 * GW] = (rstd * (dvhat - m1 - vhat * m2) * dvg).astype(BF16)

        @pl.when(c == nc - 1)
        def _():
            for g in range(G):
                dbs_ref[g] = jnp.sum(dbs_acc[g], axis=-1, keepdims=True)

    vec = pl.BlockSpec((1, GW), lambda c: (0, 0))
    return pl.pallas_call(
        body, name="gmlp_bwd", grid=(nc,),
        in_specs=[pl.BlockSpec((T, GW), lambda c: (c, 0)), pl.BlockSpec((T, GW), lambda c: (c, 1)),
                  pl.BlockSpec((T, GW), lambda c: (c, 0)), vec, vec,
                  pl.BlockSpec((G, T, T), lambda c: (0, 0, 0)), pl.BlockSpec((G, T, Dg), lambda c: (0, 0, 0))],
        out_specs=[pl.BlockSpec((T, 2 * GW), lambda c: (c, 0)), pl.BlockSpec((8, GW), lambda c: (0, 0)),
                   pl.BlockSpec((G, T, T), lambda c: (0, 0, 0)), pl.BlockSpec((G, T, 1), lambda c: (0, 0, 0))],
        out_shape=[jax.ShapeDtypeStruct((S, 2 * GW), BF16), jax.ShapeDtypeStruct((8, GW), F32),
                   jax.ShapeDtypeStruct((G, T, T), F32), jax.ShapeDtypeStruct((G, T, 1), F32)],
        scratch_shapes=[pltpu.VMEM((G, T, Dg), F32), pltpu.VMEM((T, GW), F32)],
        compiler_params=_cparams(("arbitrary",)))(z, z, dya, ln_g, ln_b, ws, bsb)


def _hg_common(q_ref, f_ref, hlb_ref):
    C = HG_CHUNK
    a = hlb_ref[...]
    lb = _sigmoid(a[0:1, :] - a[1:2, :])
    sig = _sigmoid(f_ref[...])
    f = lb + (1.0 - lb) * sig
    lf = jnp.log(f)
    kk = 1.0 - f
    q = q_ref[...]
    sq = _sigmoid(q)
    qa = q * sq
    row = lax.broadcasted_iota(jnp.int32, (C, C), 0)
    col = lax.broadcasted_iota(jnp.int32, (C, C), 1)
    tril = row >= col
    b = _ones_dot(tril.astype(BF16), lf)
    bm = b[HG_MID:HG_MID + 1, :]
    bl = b[C - 1:C, :]
    e_b = jnp.exp(b)
    e_qm = jnp.exp(jnp.minimum(b - bm, EXP_CLAMP))
    e_km = jnp.exp(jnp.minimum(bm - b, EXP_CLAMP))
    e_kl = jnp.exp(bl - b)
    return dict(lb=lb, sig=sig, f=f, kk=kk, q=q, sq=sq, qa=qa, tril=tril, e_b=e_b, e_qm=e_qm, e_km=e_km, e_kl=e_kl,
                e_l=jnp.exp(bl), qh=qa * e_b, qt=qa * e_qm, kt=kk * e_km, kh=kk * e_kl)


def _hg_fwd(z, hg_lb, ng, HW):
    S = z.shape[0]
    C, H, dk = HG_CHUNK, HW // HG_DK, HG_DK
    nc = S // C

    def body(q_ref, f_ref, i_ref, og_ref, hlb_ref, ng_ref, yb_ref, o_ref, st_ref, state):
        @pl.when(pl.program_id(0) == 0)
        def _():
            state[...] = jnp.zeros_like(state)

        t = _hg_common(q_ref, f_ref, hlb_ref)
        iv = i_ref[...]
        for h in range(H):
            sl = slice(h * dk, (h + 1) * dk)
            st = state[h]
            st_ref[h] = st
            a = jnp.where(t["tril"], _dot(t["qt"][:, sl], t["kt"][:, sl], _NT), 0.0)
            o_h = _dot(a, iv[:, sl]) + _dot(t["qh"][:, sl], st, _NT)
            state[h] = st * t["e_l"][:, sl] + _dot(iv[:, sl], t["kh"][:, sl], _TN)
            o_ref[:, sl] = o_h
            rr = lax.rsqrt(jnp.mean(o_h * o_h, axis=-1, keepdims=True) + EPS)
            og = og_ref[:, sl]
            yb_ref[:, sl] = (o_h * rr * ng_ref[:, sl] * (og * _sigmoid(og))).astype(BF16)

    def col(k):
        return pl.BlockSpec((C, HW), lambda c: (c, k))

    base = 2
    return pl.pallas_call(
        body, name="hgrn_fwd", grid=(nc,),
        in_specs=[col(base), col(base + 1), col(base + 2), col(base + 3),
                  pl.BlockSpec((2, HW), lambda c: (0, 0)), pl.BlockSpec((1, HW), lambda c: (0, 0))],
        out_specs=[pl.BlockSpec((C, HW), lambda c: (c, 0)), pl.BlockSpec((C, HW), lambda c: (c, 0)),
                   pl.BlockSpec((None, H, dk, dk), lambda c: (c, 0, 0, 0))],
        out_shape=[jax.ShapeDtypeStruct((S, HW), BF16), jax.ShapeDtypeStruct((S, HW), F32),
                   jax.ShapeDtypeStruct((nc, H, dk, dk), F32)],
        scratch_shapes=[pltpu.VMEM((H, dk, dk), F32)],
        compiler_params=_cparams(("arbitrary",)))(z, z, z, z, hg_lb, ng)


def _hg_bwd(z, o, states, dyb, hg_lb, ng, HW, dz_head, dz_tail):
    S = z.shape[0]
    C, H, dk = HG_CHUNK, HW // HG_DK, HG_DK
    nc = S // C
    B0 = dz_head.shape[1]
    DT = dz_tail.shape[2]
    INW = B0 + 4 * HW + 2 * DT

    def body(q_ref, f_ref, i_ref, og_ref, o_ref, st_ref, stn_ref, dyb_ref, hlb_ref, ng_ref, head_ref, tail_ref,
             dzf_ref, dng_ref, dhlb_ref, dstate, cross, dqa_buf, dkk_buf, db_buf, dlb_acc):
        c = pl.program_id(0)
        dzf_ref[:, 0:B0] = head_ref[...]
        dzf_ref[:, B0 + 4 * HW:B0 + 4 * HW + DT] = tail_ref[0]
        dzf_ref[:, B0 + 4 * HW + DT:INW] = tail_ref[1]
        dz_ref = dzf_ref.at[:, B0:B0 + 4 * HW]

        @pl.when(c == 0)
        def _():
            dstate[...] = jnp.zeros_like(dstate)
            dlb_acc[...] = jnp.zeros_like(dlb_acc)
            dng_ref[...] = jnp.zeros_like(dng_ref)

        def r16(v):
            return v.astype(BF16).astype(F32)

        t = _hg_common(q_ref, f_ref, hlb_ref)
        iv = i_ref[...]
        for h in range(H):
            sl = slice(h * dk, (h + 1) * dk)
            o_h, og, dyb_h, ng_h = o_ref[:, sl], og_ref[:, sl], dyb_ref[:, sl], ng_ref[:, sl]
            sg = _sigmoid(og)
            silu_og = og * sg
            rr = lax.rsqrt(jnp.mean(o_h * o_h, axis=-1, keepdims=True) + EPS)
            on = o_h * rr
            dng_ref[0:1, sl] += jnp.sum(dyb_h * on * silu_og, axis=0, keepdims=True)
            dz_ref[:, 3 * HW + h * dk:3 * HW + (h + 1) * dk] = (dyb_h * on * ng_h * (sg * (1.0 + og * (1.0 - sg)))).astype(BF16)
            don = dyb_h * ng_h * silu_og
            do_h = rr * (don - on * jnp.mean(don * on, axis=-1, keepdims=True))

            qt, kt, qh, kh, iv_h = t["qt"][:, sl], t["kt"][:, sl], t["qh"][:, sl], t["kh"][:, sl], iv[:, sl]
            a = jnp.where(t["tril"], _dot(qt, kt, _NT), 0.0)
            da = jnp.where(t["tril"], _dot(do_h, iv_h, _NT), 0.0)
            st, dst = st_ref[h], dstate[h]
            cross[:, sl] = jnp.sum(dst * stn_ref[h], axis=0, keepdims=True)
            dqh = _dot(do_h, st)
            dstate[h] = _dot(do_h, qh, _TN) + dst * t["e_l"][:, sl]
            div = _dot(a, do_h, _TN) + _dot(kh, dst, _NT)
            dkh = _dot(iv_h, dst)
            dqt = _dot(da, kt)
            dkt = _dot(da, qt, _TN)
            dz_ref[:, 2 * HW + h * dk:2 * HW + (h + 1) * dk] = div.astype(BF16)
            dqa_buf[:, sl] = dqh * t["e_b"][:, sl] + dqt * t["e_qm"][:, sl]
            dkk_buf[:, sl] = dkt * t["e_km"][:, sl] + dkh * t["e_kl"][:, sl]
            db_buf[:, sl] = r16(qt) * dqt - r16(kt) * dkt + r16(qh) * dqh - r16(kh) * dkh

        dqa, dkk = dqa_buf[...], dkk_buf[...]
        triu = jnp.logical_not(t["tril"]) | (lax.broadcasted_iota(jnp.int32, (C, C), 0) == lax.broadcasted_iota(jnp.int32, (C, C), 1))
        dlf = _ones_dot(triu.astype(BF16), db_buf[...]) + cross[...]
        df = dlf / t["f"] - dkk
        sig, lb = t["sig"], t["lb"]
        dz_ref[:, HW:2 * HW] = (df * (1.0 - lb) * sig * (1.0 - sig)).astype(BF16)
        dlb_acc[...] += jnp.sum(df * (1.0 - sig), axis=0, keepdims=True)
        q, sq = t["q"], t["sq"]
        dz_ref[:, 0:HW] = (dqa * (sq * (1.0 + q * (1.0 - sq)))).astype(BF16)

        @pl.when(c == nc - 1)
        def _():
            da0 = dlb_acc[...] * lb * (1.0 - lb)
            dhlb_ref[0:1, :] = da0
            dhlb_ref[1:2, :] = -da0

    def col(k):
        return pl.BlockSpec((C, HW), lambda c: (nc - 1 - c, k))

    base = 2
    return pl.pallas_call(
        body, name="hgrn_bwd", grid=(nc,),
        in_specs=[col(base), col(base + 1), col(base + 2), col(base + 3), col(0),
                  pl.BlockSpec((None, H, dk, dk), lambda c: (nc - 1 - c, 0, 0, 0)),
                  pl.BlockSpec((None, H, dk, dk), lambda c: (jnp.minimum(nc - c, nc - 1), 0, 0, 0)), col(0),
                  pl.BlockSpec((2, HW), lambda c: (0, 0)), pl.BlockSpec((1, HW), lambda c: (0, 0)),
                  pl.BlockSpec((C, B0), lambda c: (nc - 1 - c, 0)), pl.BlockSpec((2, C, DT), lambda c: (0, nc - 1 - c, 0))],
        out_specs=[pl.BlockSpec((C, INW), lambda c: (nc - 1 - c, 0)), pl.BlockSpec((8, HW), lambda c: (0, 0)),
                   pl.BlockSpec((2, HW), lambda c: (0, 0))],
        out_shape=[jax.ShapeDtypeStruct((S, INW), BF16), jax.ShapeDtypeStruct((8, HW), F32), jax.ShapeDtypeStruct((2, HW), F32)],
        scratch_shapes=[pltpu.VMEM((H, dk, dk), F32), pltpu.VMEM((1, HW), F32), pltpu.VMEM((C, HW), F32), pltpu.VMEM((C, HW), F32),
                        pltpu.VMEM((C, HW), F32), pltpu.VMEM((1, HW), F32)],
        compiler_params=_cparams(("arbitrary",)))(z, z, z, z, o, states, states, dyb, hg_lb, ng, dz_head, dz_tail)


def _position():
    x, y, c = lax.axis_index("x"), lax.axis_index("y"), lax.axis_index("c")
    return x, y, c, 4 * x + 2 * y + c


def _flip(x, y, c, k):
    return (1 - x if k & 4 else x, 1 - y if k & 2 else y, 1 - c if k & 1 else c)


def _allgather_small(name, v):
    R, L = v.shape

    def body(v_ref, out_ref, send_sems, recv_sems):
        x, y, c, me = _position()
        out_ref[me] = v_ref[...]
        copies = []
        for k in range(1, N_DEV):
            cp = pltpu.make_async_remote_copy(src_ref=v_ref, dst_ref=out_ref.at[me], send_sem=send_sems.at[k - 1],
                                              recv_sem=recv_sems.at[k - 1], device_id=_flip(x, y, c, k), device_id_type=MESH)
            cp.start()
            copies.append(cp)
        for cp in copies:
            cp.wait()

    return pl.pallas_call(
        body, name=name, out_shape=jax.ShapeDtypeStruct((N_DEV, R, L), v.dtype),
        in_specs=[pl.BlockSpec(memory_space=pltpu.VMEM)], out_specs=pl.BlockSpec(memory_space=pltpu.VMEM),
        scratch_shapes=[pltpu.SemaphoreType.DMA((N_DEV - 1,)), pltpu.SemaphoreType.DMA((N_DEV - 1,))],
        compiler_params=pltpu.CompilerParams(vmem_limit_bytes=VMEM_LIMIT),
    )(v)


def _allgather_hbm(name, shards):
    n = len(shards)

    def body(*refs):
        ins, outs = refs[:n], refs[n:2 * n]
        send_sems, recv_sems, local_sems = refs[2 * n:]
        x, y, c, me = _position()
        sibling = (x, y, 1 - c)
        chips = [(1 - x, y), (x, 1 - y), (1 - x, 1 - y)]

        def slot(px, py, pc):
            return 4 * px + 2 * py + pc

        def copy(w, k, block, to, src=None):
            dst = outs[w].at[slot(*block)]
            return pltpu.make_async_remote_copy(src_ref=dst if src is None else src, dst_ref=dst, send_sem=send_sems.at[w, k],
                                                recv_sem=recv_sems.at[w, k], device_id=to, device_id_type=MESH)

        mine, first, passed = [], [], []
        for w in range(n):
            cp = pltpu.make_async_copy(ins[w], outs[w].at[me], local_sems.at[w])
            cp.start()
            mine.append(cp)
            for j, chip in enumerate(chips):
                first.append(copy(w, 1 + j, (x, y, c), (*chip, c), src=ins[w]))
            first.append(copy(w, 0, (x, y, c), sibling, src=ins[w]))
        for cp in first:
            cp.start()
        for w in range(n):
            for j, chip in enumerate(chips):
                copy(w, 1 + j, (*chip, c), (x, y, c)).wait_recv()
                cp = copy(w, 4 + j, (*chip, c), sibling)
                cp.start()
                passed.append(cp)
        for w in range(n):
            copy(w, 0, sibling, (x, y, c)).wait_recv()
            for j, chip in enumerate(chips):
                copy(w, 4 + j, (*chip, 1 - c), (x, y, c)).wait_recv()
        for cp in first + passed:
            cp.wait_send()
        for cp in mine:
            cp.wait()

    hbm = pl.BlockSpec(memory_space=pltpu.HBM)
    return pl.pallas_call(
        body, name=name, out_shape=[jax.ShapeDtypeStruct((N_DEV, *s.shape), s.dtype) for s in shards],
        in_specs=[hbm] * n, out_specs=[hbm] * n,
        scratch_shapes=[pltpu.SemaphoreType.DMA((n, 7)), pltpu.SemaphoreType.DMA((n, 7)), pltpu.SemaphoreType.DMA((n,))],
    )(*shards)


_HBM = pl.BlockSpec(memory_space=pltpu.HBM)
_SEM = pl.BlockSpec(memory_space=pltpu.SEMAPHORE)
_EFFECT = pltpu.SideEffectType.DATAFLOW_SIDE_EFFECTING


def _split_start(name, bufs, n_sems, copies_fn, after=None):
    nb = len(bufs)
    extra = [] if after is None else [after]
    k = nb + len(extra)

    def body(*refs):
        for cp in copies_fn(refs[:nb], refs[k], refs[k + 1]):
            cp.start()
        refs[-1][...] = jnp.zeros_like(refs[-1])

    sems = pltpu.SemaphoreType.DMA((n_sems,))
    res = pl.pallas_call(
        body, name=name,
        out_shape=(sems, sems, *[pltpu.HBM(a.shape, a.dtype) for a in bufs], jax.ShapeDtypeStruct((8, LANES), F32)),
        in_specs=[_HBM] * nb + [pl.BlockSpec(memory_space=pl.ANY)] * len(extra),
        out_specs=(_SEM, _SEM, *[_HBM] * nb, pl.BlockSpec(memory_space=pltpu.VMEM)),
        input_output_aliases={i: 2 + i for i in range(nb)},
        compiler_params=pltpu.CompilerParams(has_side_effects=_EFFECT),
    )(*[pltpu.with_memory_space_constraint(a, pltpu.HBM) for a in bufs], *extra)
    return res[0], res[1], list(res[2:2 + nb]), res[-1]


def _split_wait(name, bufs, send_sems, recv_sems, after, copies_fn):
    nb = len(bufs)

    def body(*refs):
        for cp in copies_fn(refs[:nb], refs[nb], refs[nb + 1]):
            cp.wait_send()
            cp.wait_recv()

    res = pl.pallas_call(
        body, name=name, out_shape=tuple(pltpu.HBM(a.shape, a.dtype) for a in bufs),
        in_specs=[_HBM] * nb + [_SEM, _SEM, pl.BlockSpec(memory_space=pl.ANY)], out_specs=tuple([_HBM] * nb),
        input_output_aliases={i: i for i in range(nb)},
        compiler_params=pltpu.CompilerParams(has_side_effects=_EFFECT),
    )(*bufs, send_sems, recv_sems, after)
    return list(res)


def _split_relay(name, bufs, send_sems, recv_sems, after, wait_fn, n_sems, start_fn):
    nb = len(bufs)

    def body(*refs):
        for cp in wait_fn(refs[:nb], refs[nb], refs[nb + 1]):
            cp.wait_send()
            cp.wait_recv()
        for cp in start_fn(refs[:nb], refs[nb + 3], refs[nb + 4]):
            cp.start()
        refs[-1][...] = jnp.zeros_like(refs[-1])

    sems = pltpu.SemaphoreType.DMA((n_sems,))
    res = pl.pallas_call(
        body, name=name, out_shape=(sems, sems, *[pltpu.HBM(a.shape, a.dtype) for a in bufs], jax.ShapeDtypeStruct((8, LANES), F32)),
        in_specs=[_HBM] * nb + [_SEM, _SEM, pl.BlockSpec(memory_space=pl.ANY)],
        out_specs=(_SEM, _SEM, *[_HBM] * nb, pl.BlockSpec(memory_space=pltpu.VMEM)),
        input_output_aliases={i: 2 + i for i in range(nb)},
        compiler_params=pltpu.CompilerParams(has_side_effects=_EFFECT),
    )(*bufs, send_sems, recv_sems, after)
    return res[0], res[1], list(res[2:2 + nb]), res[-1]


N_CHIP = 4


def _chip_flip(x, y, k):
    return (1 - x if k & 2 else x), (1 - y if k & 1 else y)


def _gather_first_copies(n):
    def copies(bufs, send_sems, recv_sems):
        x, y, c, me = _position()
        out = []
        for w in range(n):
            for k in range(N_CHIP):
                to = (x, y, 1 - c) if k == 0 else (*_chip_flip(x, y, k), c)
                out.append(pltpu.make_async_remote_copy(
                    src_ref=bufs[w], dst_ref=bufs[n + w].at[me], send_sem=send_sems.at[w * N_CHIP + k],
                    recv_sem=recv_sems.at[w * N_CHIP + k], device_id=to, device_id_type=MESH))
        return out
    return copies


def _gather_relay_copies(n):
    def copies(bufs, send_sems, recv_sems):
        x, y, c, _ = _position()
        out = []
        for w in range(n):
            for k in range(1, N_CHIP):
                px, py = _chip_flip(x, y, k)
                blk = bufs[n + w].at[4 * px + 2 * py + c]
                out.append(pltpu.make_async_remote_copy(
                    src_ref=blk, dst_ref=blk, send_sem=send_sems.at[w * (N_CHIP - 1) + k - 1],
                    recv_sem=recv_sems.at[w * (N_CHIP - 1) + k - 1], device_id=(x, y, 1 - c), device_id_type=MESH))
        return out
    return copies


def _small_gather_copies(bufs, send_sems, recv_sems):
    x, y, c, me = _position()
    return [pltpu.make_async_remote_copy(src_ref=bufs[0], dst_ref=bufs[1].at[me], send_sem=send_sems.at[k - 1], recv_sem=recv_sems.at[k - 1],
                                         device_id=_flip(x, y, c, k), device_id_type=MESH) for k in range(1, N_DEV)]


def _xor(a, b):
    return a + b - 2 * a * b


def _forward_first_copies(n):
    def copies(bufs, send_sems, recv_sems):
        x, y, c, me = _position()
        out = []
        for w in range(n):
            for k, to in enumerate([(x, y, 1 - c), (1 - x, y, c), (x, 1 - y, c)]):
                out.append(pltpu.make_async_remote_copy(
                    src_ref=bufs[w], dst_ref=bufs[n + w].at[me], send_sem=send_sems.at[w * 3 + k],
                    recv_sem=recv_sems.at[w * 3 + k], device_id=to, device_id_type=MESH))
        return out
    return copies


def _forward_second_copies(n):
    def copies(bufs, send_sems, recv_sems):
        x, y, c, _ = _position()
        out = []
        for w in range(n):
            half = bufs[n + w].shape[1] // 2
            for k, (src_chip, rows, to) in enumerate([((1 - x, y), pl.ds(0, half), (x, 1 - y, c)), ((x, 1 - y), pl.ds(half, half), (1 - x, y, c))]):
                blk = bufs[n + w].at[4 * src_chip[0] + 2 * src_chip[1] + c, rows]
                out.append(pltpu.make_async_remote_copy(src_ref=blk, dst_ref=blk, send_sem=send_sems.at[w * 4 + k],
                                                        recv_sem=recv_sems.at[w * 4 + k], device_id=to, device_id_type=MESH))
            for k, (px, py) in enumerate([(1 - x, y), (x, 1 - y)]):
                blk = bufs[n + w].at[4 * px + 2 * py + c]
                out.append(pltpu.make_async_remote_copy(src_ref=blk, dst_ref=blk, send_sem=send_sems.at[w * 4 + 2 + k],
                                                        recv_sem=recv_sems.at[w * 4 + 2 + k], device_id=(x, y, 1 - c), device_id_type=MESH))
        return out
    return copies


def _forward_third_copies(n):
    def copies(bufs, send_sems, recv_sems):
        x, y, c, _ = _position()
        out = []
        for w in range(n):
            blk = bufs[n + w].at[4 * (1 - x) + 2 * (1 - y) + c]
            out.append(pltpu.make_async_remote_copy(src_ref=blk, dst_ref=blk, send_sem=send_sems.at[w], recv_sem=recv_sems.at[w],
                                                    device_id=(x, y, 1 - c), device_id_type=MESH))
        return out
    return copies


def _to_sibling_copies(n):
    def copies(bufs, send_sems, recv_sems):
        x, y, c, _ = _position()
        out = []
        for w in range(n):
            for q in range(N_CHIP):
                out.append(pltpu.make_async_remote_copy(
                    src_ref=bufs[w].at[2 * q + 1 - c], dst_ref=bufs[n + w].at[q], send_sem=send_sems.at[w * N_CHIP + q],
                    recv_sem=recv_sems.at[w * N_CHIP + q], device_id=(x, y, 1 - c), device_id_type=MESH))
        return out
    return copies


def _to_owner_copies(n):
    def copies(bufs, send_sems, recv_sems):
        x, y, c, _ = _position()
        out = []
        for w in range(n):
            for k in range(1, N_CHIP):
                px, py = (1 - x if k & 2 else x), (1 - y if k & 1 else y)
                out.append(pltpu.make_async_remote_copy(
                    src_ref=bufs[w].at[2 * px + py], dst_ref=bufs[n + w].at[k - 1], send_sem=send_sems.at[w * (N_CHIP - 1) + k - 1],
                    recv_sem=recv_sems.at[w * (N_CHIP - 1) + k - 1], device_id=(px, py, c), device_id_type=MESH))
        return out
    return copies


def _chip_sum(name, stack, landed, c_idx):
    _, R, C = stack.shape
    tr = _tile(R, max(16, 1048576 // C), 16)

    def body(c_ref, a_ref, b_ref, o_ref):
        o_ref[...] = (a_ref[...].astype(F32) + b_ref[...].astype(F32)).astype(o_ref.dtype)

    return pl.pallas_call(
        body, name=name,
        grid_spec=pltpu.PrefetchScalarGridSpec(
            num_scalar_prefetch=1, grid=(N_CHIP, R // tr),
            in_specs=[pl.BlockSpec((None, tr, C), lambda q, i, c_ref: (2 * q + c_ref[0], i, 0)),
                      pl.BlockSpec((None, tr, C), lambda q, i, c_ref: (q, i, 0))],
            out_specs=pl.BlockSpec((None, tr, C), lambda q, i, c_ref: (q, i, 0))),
        out_shape=jax.ShapeDtypeStruct((N_CHIP, R, C), stack.dtype),
        compiler_params=_cparams(("parallel", "parallel")))(c_idx, stack, landed)


def _ada_mod(c16, w):
    _, D = c16.shape
    n = w.shape[1]
    tk = _tile(D, 512)
    nk = D // tk

    def body(c_ref, w_ref, o_ref, ca_ref):
        @pl.when(pl.program_id(0) == 0)
        def _():
            o_ref[...] = jnp.zeros_like(o_ref)

        cv = c_ref[...]
        ca = cv * _sigmoid(cv)
        ca_ref[...] = ca
        o_ref[...] += _dot(ca, w_ref[...])

    return pl.pallas_call(
        body, name="ada_mod", grid=(nk,),
        in_specs=[pl.BlockSpec((16, tk), lambda k: (0, k)), pl.BlockSpec((tk, n), lambda k: (k, 0))],
        out_specs=[pl.BlockSpec((16, n), lambda k: (0, 0)), pl.BlockSpec((16, tk), lambda k: (0, k))],
        out_shape=[jax.ShapeDtypeStruct((16, n), F32), jax.ShapeDtypeStruct((16, D), F32)],
        compiler_params=_cparams(("arbitrary",)))(c16, w)


def _adam_math(w, g, m, v):
    m2 = ADAM_B1 * m + (1.0 - ADAM_B1) * g
    v2 = ADAM_B2 * v + (1.0 - ADAM_B2) * (g * g)
    m_hat = m2 / (1.0 - ADAM_B1 ** ADAM_STEP)
    v_hat = v2 / (1.0 - ADAM_B2 ** ADAM_STEP)
    delta = -ADAM_LR * (m_hat / (jnp.sqrt(v_hat) + ADAM_EPS) + ADAM_WD * w)
    return delta, m2, v2


def _adamw(name, w, m, v, parts, row0=0, into=None):
    R, C = w.shape
    Rp = parts[0].shape[1]
    tr = _tile(Rp, max(16, 393216 // C), 16)
    off = row0 // tr
    n_p = len(parts)
    held = [] if into is None else list(into)

    def body(*refs):
        w_ref, m_ref, v_ref = refs[:3]
        g_ref, d_ref, m2_ref, v2_ref = refs[3 + n_p + len(held):]
        g = None
        for p_ref in refs[3:3 + n_p]:
            for s in range(p_ref.shape[0]):
                t = p_ref[s].astype(F32)
                g = t if g is None else g + t
        delta, m2, v2 = _adam_math(w_ref[...], g, m_ref[...], v_ref[...])
        g_ref[...] = g
        d_ref[...] = delta
        m2_ref[...] = m2
        v2_ref[...] = v2

    blk = pl.BlockSpec((tr, C), lambda i: (i + off, 0))
    out = jax.ShapeDtypeStruct((R, C), F32)
    return pl.pallas_call(
        body, name=name, grid=(Rp // tr,),
        in_specs=[blk, blk, blk] + [pl.BlockSpec((a.shape[0], tr, C), lambda i: (0, i, 0)) for a in parts]
        + [pl.BlockSpec(memory_space=pl.ANY)] * len(held),
        out_specs=[blk] * 4, out_shape=[out] * 4, input_output_aliases={3 + n_p + i: i for i in range(len(held))},
        compiler_params=_cparams(("parallel",)))(w, m, v, *parts, *held)


def _small_update(gathered, w, m, v, after):
    _, R, L = gathered.shape
    rs = w.shape[0]

    def body(p_ref, w_ref, m_ref, v_ref, after_ref, g_ref, d_ref, m2_ref, v2_ref):
        g = p_ref[0]
        for p in range(1, N_DEV):
            g = g + p_ref[p]
        g_ref[...] = g
        delta, m2, v2 = _adam_math(w_ref[...], g[0:rs, :], m_ref[...], v_ref[...])
        d_ref[...] = delta
        m2_ref[...] = m2
        v2_ref[...] = v2

    vm = pl.BlockSpec(memory_space=pltpu.VMEM)
    sm = jax.ShapeDtypeStruct((rs, L), F32)
    return pl.pallas_call(body, name="small_update", in_specs=[vm] * 4 + [pl.BlockSpec(memory_space=pl.ANY)], out_specs=[vm] * 4,
                          out_shape=[jax.ShapeDtypeStruct((R, L), F32), sm, sm, sm],
                          compiler_params=pltpu.CompilerParams(vmem_limit_bytes=VMEM_LIMIT))(gathered, w, m, v, after)


class _Fetched(dict):
    def __init__(self, fetch):
        super().__init__()
        self.fetch = fetch

    def first(self, key, after):
        self[key] = self.fetch(key, after)
        return self[key]


def _local_step(x, tgt, mod, p, fetch, F, scatter=None):
    S, D = x.shape
    GW, HW = p["ln_g"].shape[1], p["hg_ng"].shape[1]
    G, T, _ = p["ws"].shape
    w = _Fetched(fetch)
    INW = 2 * GW + 4 * HW + 2 * D
    in_loc, br_loc, fi_loc = INW // N_DEV, D // N_DEV, 2 * F // N_DEV
    assert GW == HW and F % fi_loc == 0
    sh1, sc1, gt1, sh2, sc2, gt2 = (mod[:, k * D:(k + 1) * D] for k in range(6))
    bsb = jnp.broadcast_to(p["bs"][:, :, None], (G, T, GW // G))

    tm = _tile(S, 1024, 16)
    tmh = _tile(S, 512, 16)
    tn_in = _tile(in_loc, 1280)
    tn_d = _tile(D, 512)
    tn_br = _tile(br_loc, 512)
    tk_s = S
    tm_w = _tile(D, 1024)
    g_off = 2 * GW + 4 * HW

    h1 = _norm_mod("norm1", x, p["norm1_g"], sc1, sh1)
    z = _mm_nn_stacked("proj_in", h1, w.first("in", h1), tm=tm, tn=tn_in, tk=D)[0]
    ya = _gmlp_fwd(z, p["ln_g"], p["ln_b"], p["ws"], bsb, GW)
    yb, o_hg, states = _hg_fwd(z, p["hg_lb"], p["hg_ng"], HW)
    flat = {k: jnp.swapaxes(w.first(k, yb), 0, 1).reshape(GW, D) for k in ("bg", "bh")}
    tn_f = _tile(D, 1024)
    pa = _matmul(
        "branch_gmlp", ya, flat["bg"], dims=_NN, grid_mnk=(S // tm, D // tn_f, 1), tiles=(tm, tn_f),
        a_spec=pl.BlockSpec((tm, GW), lambda i, j, k: (i, 0)), b_spec=pl.BlockSpec((GW, tn_f), lambda i, j, k: (0, j)),
        out_shapes=[jax.ShapeDtypeStruct((S, D), F32)], out_specs=[pl.BlockSpec((tm, tn_f), lambda i, j, k: (i, j))], epilogue=_store(F32))[0]
    t_fi = w.first("fi_early", pa)

    def gates(ga_ref, gb_ref, ba_ref, bb_ref):
        return _sigmoid(ga_ref[...] + ba_ref[...]), _sigmoid(gb_ref[...] + bb_ref[...])

    def gate_specs(tn_, tm_=tm):
        o1, o2 = g_off // tn_, (g_off + D) // tn_
        return [pl.BlockSpec((tm_, tn_), lambda i, j, k: (i, o1 + j)), pl.BlockSpec((tm_, tn_), lambda i, j, k: (i, o2 + j)),
                pl.BlockSpec((1, tn_), lambda i, j, k: (0, j)), pl.BlockSpec((1, tn_), lambda i, j, k: (0, D // tn_ + j))]

    def merge_ep(acc, ex, outs):
        ga, gb = gates(*ex[1:5])
        outs[0][...] = acc
        outs[1][...] = (ga * ex[0][...] + gb * acc).astype(BF16)

    tile_o = pl.BlockSpec((tmh, tn_f), lambda i, j, k: (i, j))
    pb, y = _matmul(
        "branch_hg_merge", yb, flat["bh"], dims=_NN, grid_mnk=(S // tmh, D // tn_f, 1), tiles=(tmh, tn_f),
        a_spec=pl.BlockSpec((tmh, HW), lambda i, j, k: (i, 0)), b_spec=pl.BlockSpec((HW, tn_f), lambda i, j, k: (0, j)),
        extras=[pa, z, z, p["b_gate"], p["b_gate"]], extra_specs=[tile_o, *gate_specs(tn_f, tmh)],
        out_shapes=[jax.ShapeDtypeStruct((S, D), F32), jax.ShapeDtypeStruct((S, D), BF16)], out_specs=[tile_o, tile_o],
        epilogue=merge_ep, after=t_fi)

    def resid_ep(acc, ex, outs):
        outs[0][...] = acc
        outs[1][...] = ex[0][...] + ex[1][...] * acc

    def resid_mm(name, a, b, res, gt, tm_):
        K = a.shape[1]
        t_o = pl.BlockSpec((tm_, tn_d), lambda i, j, k: (i, j))
        return _matmul(
            name, a, b, dims=_NN, grid_mnk=(S // tm_, D // tn_d, 1), tiles=(tm_, tn_d),
            a_spec=pl.BlockSpec((tm_, K), lambda i, j, k: (i, 0)), b_spec=pl.BlockSpec((K, tn_d), lambda i, j, k: (0, j)),
            extras=[res, gt], extra_specs=[t_o, pl.BlockSpec((1, tn_d), lambda i, j, k: (0, j))],
            out_shapes=[jax.ShapeDtypeStruct((S, D), F32)] * 2, out_specs=[t_o, t_o], epilogue=resid_ep)

    o1, xm = resid_mm("proj_out", y, w.first("out", z), x, gt1, tm)
    h2 = _norm_mod("norm2", xm, p["norm2_g"], sc2, sh2)
    hf, hf_fac = _ffn_in_swiglu(h2, w.first("fi", h2))
    o2, x3 = resid_mm("ffn_out", hf, w.first("fo", hf), xm, gt2, tmh)
    dx3, do2, vec_l = _loss_head(x3, tgt, p["final_g"], o2, gt2)

    nf = F // fi_loc

    def dswiglu_ep(acc, ex, outs):
        outs[0][0] = (acc * ex[0][0].astype(F32)).astype(BF16)
        outs[0][1] = (acc * ex[0][1].astype(F32)).astype(BF16)

    pair = pl.BlockSpec((2, tmh, fi_loc), lambda i, j, k: (0, i, j))
    dab = _matmul(
        "ffn_out_dx", do2, w["fo"], dims=_NT, grid_mnk=(S // tmh, nf, 1), tiles=(tmh, fi_loc),
        a_spec=pl.BlockSpec((tmh, D), lambda i, j, k: (i, 0)), b_spec=pl.BlockSpec((fi_loc, D), lambda i, j, k: (j, 0)),
        extras=[hf_fac], extra_specs=[pair], out_shapes=[jax.ShapeDtypeStruct((2, S, F), BF16)], out_specs=[pair],
        epilogue=dswiglu_ep)[0]
    start = (lambda name, grads: scatter[0](name, grads)) if scatter is not None else (lambda name, grads: None)
    push = (lambda name, after: scatter[1](name, after)) if scatter is not None else (lambda name, after: None)

    def zero(token):
        return 0.0 if token is None else token[0:1, 0:1]

    tm_f = _tile(F, 512)
    g_fo = _mm_tn("ffn_out_dw", hf, do2, pl.BlockSpec((tk_s, D), lambda i, j, k: (k, j)), Mo=F, No=D, S=S, tm=tm_f, tn=D, tk=tk_s)
    g_fi = _mm_tn("ffn_in_dw", h2, dab, pl.BlockSpec((None, tk_s, fi_loc), lambda i, j, k: (j // nf, k, j % nf)),
                  Mo=D, No=2 * F, S=S, tm=tm_w, tn=fi_loc, tk=tk_s, stacked_nloc=fi_loc, after=g_fo)
    t_ffn = start("scatter_ffn", dict(fo=g_fo, fi=g_fi))
    dh2 = _mm_nt_stacked("ffn_in_dx", pl.BlockSpec((None, tm, fi_loc), lambda i, j, k: (k // nf, i, k % nf)), dab, w["fi"],
                         M=S, tm=tm, tn=tm_w, tk=fi_loc, after=t_ffn)
    dxm, vec2, do1 = _norm_mod_bwd("norm2_bwd", dh2, xm, p["norm2_g"], sc2, dx3, o1, gt1)
    t_ffn = push("scatter_ffn", dxm)

    def dmerge_ep(acc, ex, outs):
        ga, gb = gates(*ex[2:6])
        outs[0][...] = (acc * ga).astype(BF16)
        outs[1][...] = (acc * gb).astype(BF16)
        outs[2][0] = (acc * ex[0][...] * ga * (1.0 - ga)).astype(BF16)
        outs[2][1] = (acc * ex[1][...] * gb * (1.0 - gb)).astype(BF16)

    t_o = pl.BlockSpec((tm, tn_d), lambda i, j, k: (i, j))
    dpa, dpb, dg2 = _matmul(
        "proj_out_dx", do1, w["out"], dims=_NT, grid_mnk=(S // tm, D // tn_d, 1), tiles=(tm, tn_d),
        a_spec=pl.BlockSpec((tm, D), lambda i, j, k: (i, 0)), b_spec=pl.BlockSpec((tn_d, D), lambda i, j, k: (j, 0)),
        extras=[pa, pb, z, z, p["b_gate"], p["b_gate"]], extra_specs=[t_o, t_o, *gate_specs(tn_d)],
        out_shapes=[jax.ShapeDtypeStruct((S, D), BF16), jax.ShapeDtypeStruct((S, D), BF16), jax.ShapeDtypeStruct((2, S, D), BF16)],
        out_specs=[t_o, t_o, pl.BlockSpec((2, tm, tn_d), lambda i, j, k: (0, i, j))], epilogue=dmerge_ep, after=t_ffn)
    g_out = _mm_tn("proj_out_dw", y, do1, pl.BlockSpec((tk_s, D), lambda i, j, k: (k, j)), Mo=D, No=D, S=S, tm=tn_d, tn=D, tk=tk_s)
    tn_g = _tile(GW, 512)
    b_br = pl.BlockSpec((tk_s, br_loc), lambda i, j, k: (k, j))
    g_bg = _mm_tn("branch_gmlp_dw", ya, dpa, b_br, Mo=GW, No=D, S=S, tm=tn_g, tn=br_loc, tk=tk_s, stacked_nloc=br_loc)
    g_bh = _mm_tn("branch_hg_dw", yb, dpb, b_br, Mo=HW, No=D, S=S, tm=tn_g, tn=br_loc, tk=tk_s, stacked_nloc=br_loc)
    t_mix = start("scatter_mixer", dict(out=g_out, bg=g_bg, bh=g_bh))
    def branch_dx(name, dp, w_flat):
        return _matmul(
            name, dp, w_flat, dims=_NT, grid_mnk=(S // tm, GW // tn_g, 1), tiles=(tm, tn_g),
            a_spec=pl.BlockSpec((tm, D), lambda i, j, k: (i, 0)), b_spec=pl.BlockSpec((tn_g, D), lambda i, j, k: (j, 0)),
            out_shapes=[jax.ShapeDtypeStruct((S, GW), F32)], out_specs=[pl.BlockSpec((tm, tn_g), lambda i, j, k: (i, j))],
            epilogue=_store(F32), after=t_mix)[0]

    dya = branch_dx("branch_gmlp_dx", dpa, flat["bg"])
    dyb = branch_dx("branch_hg_dx", dpb, flat["bh"])
    db_gate = _colsum2(dg2)
    dz_gmlp, dln, dws, dbs = _gmlp_bwd(z, dya, p["ln_g"], p["ln_b"], p["ws"], bsb, GW)
    t_mix = push("scatter_mixer", dz_gmlp)
    dz, dng, dhlb = _hg_bwd(z, o_hg, states, dyb, p["hg_lb"], p["hg_ng"] + zero(t_mix), HW, dz_gmlp, dg2)
    half = D // 2
    tm_h = _tile(half, 1024)
    g_in = []
    t_in = None
    for hname, h in (("a", 0), ("b", 1)):
        g_in.append(_mm_tn("proj_in_dw_" + hname, h1, dz, pl.BlockSpec((tk_s, in_loc), lambda i, j, k: (k, j)), Mo=half, No=INW, S=S,
                           tm=tm_h, tn=in_loc, tk=tk_s, stacked_nloc=in_loc, after=t_in, a_off=h * (half // tm_h)))
        t_in = start("scatter_proj_in_" + hname, {"w_in_" + hname: g_in[-1]})
    t_in = push("scatter_proj_in_a", t_in)
    dh1 = _mm_nt_stacked("proj_in_dx", pl.BlockSpec((tm, in_loc), lambda i, j, k: (i, k)), dz, w["in"], M=S, tm=tm, tn=tm_w, tk=in_loc,
                         after=t_in)
    dx, vec1 = _norm_mod_bwd("norm1_bwd", dh1, x, p["norm1_g"], sc1, dxm)

    dmod = jnp.concatenate([vec1[0:1], vec1[1:2], vec2[3:4], vec2[0:1], vec2[1:2], vec_l[2:3]], axis=1)
    small = dict(norm1_g=vec1[2:3], b_gate=db_gate.reshape(1, 2 * D), ln_g=dln[0:1], ln_b=dln[1:2], ws=dws, bs=dbs.reshape(G, T),
                 hg_lb=dhlb, hg_ng=dng[0:1], norm2_g=vec2[2:3], final_g=vec_l[1:2], loss=vec_l[0:1, 0:LANES])
    big = dict(w_in_a=g_in[0], w_in_b=g_in[1], bg=g_bg, bh=g_bh, out=g_out, fi=g_fi, fo=g_fo)
    return dx, big, small, dmod


_SMALL = ("b_ada", "norm1_g", "b_gate", "ln_g", "ln_b", "ws", "bs", "hg_lb", "hg_ng", "norm2_g", "final_g")


def _pack(parts, rows_mult=8):
    flat = [a.reshape(-1) for a in parts]
    offs, n = [], 0
    for a in flat:
        offs.append(n)
        n += a.shape[0]
    pad = (-n) % (LANES * rows_mult)
    if pad:
        flat.append(jnp.zeros((pad,), F32))
    return jnp.concatenate(flat).reshape(-1, LANES), offs


def kernel(x, c, w_ada, b_ada, norm1_g, w_in, b_gate, gmlp_ln_g, gmlp_ln_b, gmlp_ws, gmlp_bs, hg_lb, hg_norm_g, w_branch_gmlp, w_branch_hg, w_out, norm2_g, w_ffn_in, w_ffn_out, final_norm_g, loss_target, m_w_ada, m_b_ada, m_norm1_g, m_w_in, m_b_gate, m_gmlp_ln_g, m_gmlp_ln_b, m_gmlp_ws, m_gmlp_bs, m_hg_lb, m_hg_norm_g, m_w_branch_gmlp, m_w_branch_hg, m_w_out, m_norm2_g, m_w_ffn_in, m_w_ffn_out, m_final_norm_g, v_w_ada, v_b_ada, v_norm1_g, v_w_in, v_b_gate, v_gmlp_ln_g, v_gmlp_ln_b, v_gmlp_ws, v_gmlp_bs, v_hg_lb, v_hg_norm_g, v_w_branch_gmlp, v_w_branch_hg, v_w_out, v_norm2_g, v_w_ffn_in, v_w_ffn_out, v_final_norm_g):
    S, D = x.shape[1], x.shape[2]
    ada_loc = w_ada.shape[2]
    me = 4 * lax.axis_index("x") + 2 * lax.axis_index("y") + lax.axis_index("c")

    c_all = _allgather_small("gather_c", c.reshape(D // LANES, LANES)).reshape(N_DEV, D)
    mod_cols, c_act = _ada_mod(jnp.pad(c_all, ((0, 16 - N_DEV), (0, 0))), w_ada[0])
    mod_all = _allgather_small("gather_mod", mod_cols[:N_DEV].reshape(-1, LANES)).reshape(N_DEV, N_DEV, ada_loc)
    mod = lax.dynamic_index_in_dim(mod_all, me, axis=1, keepdims=False).reshape(1, N_DEV * ada_loc) + b_ada

    def empty_hbm(shape, dtype):
        return pltpu.with_memory_space_constraint(lax.empty(shape, dtype), pltpu.HBM)

    groups = dict(gather_in=dict(keys=["in"], src=[w_in], forward=True),
                  gather_mixer=dict(keys=["bg", "bh", "out"], src=[w_branch_gmlp, w_branch_hg, w_out], forward=False),
                  gather_ffn_in=dict(keys=["fi"], src=[w_ffn_in], forward=True),
                  gather_ffn_out=dict(keys=["fo"], src=[w_ffn_out], forward=False))
    group_of = {k: gname for gname, g in groups.items() for k in g["keys"]}

    def first_hop(gname, after):
        g = groups[gname]
        n = len(g["keys"])
        shards = [a[0].astype(BF16) for a in g["src"]]
        outs = [lax.dynamic_update_slice(lax.empty((N_DEV, *s.shape), BF16), s[None], (me, 0, 0)) for s in shards]
        if g["forward"]:
            *g["hop"], token = _split_start(gname + "_hop1", shards + outs, n * 3, _forward_first_copies(n), after=after)
        else:
            *g["hop"], token = _split_start(gname + "_hop1", shards + outs, n * N_CHIP, _gather_first_copies(n), after=after)
        return token

    def second_hop(gname, after):
        g = groups[gname]
        n = len(g["keys"])
        *g["hop"], token = _split_relay(gname + "_hop2", g["hop"][2], g["hop"][0], g["hop"][1], after,
                                        _forward_first_copies(n), n * 4, _forward_second_copies(n))
        return token

    def finish(gname, after):
        g = groups[gname]
        n = len(g["keys"])
        send_sems, recv_sems, bufs = g["hop"]
        if g["forward"]:
            send_sems, recv_sems, bufs, _ = _split_relay(gname + "_hop3", bufs, send_sems, recv_sems, after,
                                                         _forward_second_copies(n), n, _forward_third_copies(n))
            bufs = _split_wait(gname + "_wait", bufs, send_sems, recv_sems, after, _forward_third_copies(n))
        else:
            send_sems, recv_sems, bufs, _ = _split_relay(gname + "_relay", bufs, send_sems, recv_sems, after,
                                                         _gather_first_copies(n), n * (N_CHIP - 1), _gather_relay_copies(n))
            bufs = _split_wait(gname + "_wait", bufs, send_sems, recv_sems, after, _gather_relay_copies(n))
        g["done"] = dict(zip(g["keys"], bufs[n:]))

    token = first_hop("gather_in", mod_all)
    mod = mod + token[0:1, 0:1]

    def fetch(key, after):
        if key == "in":
            t = second_hop("gather_in", after)
            t = first_hop("gather_mixer", t)
            t = first_hop("gather_ffn_in", t)
            finish("gather_in", t)
        elif key == "fi_early":
            return first_hop("gather_ffn_out", second_hop("gather_ffn_in", after))
        elif "done" not in groups[group_of[key]]:
            finish(group_of[key], after)
        arr = groups[group_of[key]]["done"][key]
        return arr.reshape(-1, D) if key in ("out", "fo") else arr

    p = dict(norm1_g=norm1_g, b_gate=b_gate, ln_g=gmlp_ln_g, ln_b=gmlp_ln_b, ws=gmlp_ws[0], bs=gmlp_bs[0], hg_lb=hg_lb,
             hg_ng=hg_norm_g, norm2_g=norm2_g, final_g=final_norm_g.reshape(1, D))

    in_flight = {}
    c_idx = lax.axis_index("c").astype(jnp.int32).reshape(1)
    my_chip = 2 * lax.axis_index("x") + lax.axis_index("y")

    def scatter_start(name, grads):
        keys = list(grads)
        n = len(keys)
        stacks = [grads[k].reshape(N_DEV, -1, grads[k].shape[-1]) for k in keys]
        lands = [empty_hbm((N_CHIP, *g.shape[1:]), g.dtype) for g in stacks]
        send_sems, recv_sems, bufs, token = _split_start(name + "_d2d", stacks + lands, n * N_CHIP, _to_sibling_copies(n))
        in_flight[name] = dict(keys=keys, stage1=(send_sems, recv_sems, bufs))
        return token

    def scatter_push(name, after):
        f = in_flight[name]
        n = len(f["keys"])
        send_sems, recv_sems, bufs = f["stage1"]
        bufs = _split_wait(name + "_d2d_wait", bufs, send_sems, recv_sems, after, _to_sibling_copies(n))
        sums = [_chip_sum(f"{name}_sum_{k}", bufs[i], bufs[n + i], c_idx) for i, k in enumerate(f["keys"])]
        lands = [empty_hbm((N_CHIP - 1, *s.shape[1:]), s.dtype) for s in sums]
        send_sems, recv_sems, bufs, token = _split_start(name + "_ici", sums + lands, n * (N_CHIP - 1), _to_owner_copies(n))
        f["stage2"] = (send_sems, recv_sems, bufs)
        return token

    grad_x, _, small, dmod = _local_step(x[0], loss_target[0], mod, p, fetch, w_ffn_out.shape[1] * N_DEV, (scatter_start, scatter_push))

    small["b_ada"] = dmod
    packed, offs = _pack([small[k] for k in _SMALL] + [small["loss"]])
    sg_send, sg_recv, sg_bufs, t_tail = _split_start(
        "gather_small_start", [packed, lax.dynamic_update_slice(lax.empty((N_DEV, *packed.shape), F32), packed[None], (me, 0, 0))],
        N_DEV - 1, _small_gather_copies)
    t_tail = scatter_push("scatter_proj_in_b", t_tail)
    big_w = dict(w_in=(w_in, m_w_in, v_w_in, "w_in"), bg=(w_branch_gmlp, m_w_branch_gmlp, v_w_branch_gmlp, "w_branch_gmlp"),
                 bh=(w_branch_hg, m_w_branch_hg, v_w_branch_hg, "w_branch_hg"), out=(w_out, m_w_out, v_w_out, "w_out"),
                 fi=(w_ffn_in, m_w_ffn_in, v_w_ffn_in, "w_ffn_in"), fo=(w_ffn_out, m_w_ffn_out, v_w_ffn_out, "w_ffn_out"))
    upd = {}

    def land_and_update(name, after):
        keys = in_flight[name]["keys"]
        n = len(keys)
        send_sems, recv_sems, bufs = in_flight[name]["stage2"]
        bufs = _split_wait(name + "_ici_wait", bufs, send_sems, recv_sems, after, _to_owner_copies(n))
        for i, k in enumerate(keys):
            parts = [lax.dynamic_index_in_dim(bufs[i], my_chip, axis=0, keepdims=True), bufs[n + i]]
            if k in big_w:
                wt, mt, vt, out_name = big_w[k]
                upd[out_name] = _adamw("adamw_" + out_name, wt[0], mt[0], vt[0], parts)
            else:
                wt, mt, vt, out_name = big_w["w_in"]
                upd[out_name] = _adamw("adamw_" + k, wt[0], mt[0], vt[0], parts, row0=0 if k == "w_in_a" else parts[0].shape[1],
                                       into=upd.get(out_name))
            after = upd[out_name][1]
        return after

    after = land_and_update("scatter_mixer", land_and_update("scatter_ffn", t_tail))
    gathered = _split_wait("gather_small_wait", sg_bufs, sg_send, sg_recv, after, _small_gather_copies)[1]
    wp = dict(p, b_ada=b_ada)
    ms = dict(b_ada=m_b_ada, norm1_g=m_norm1_g, b_gate=m_b_gate, ln_g=m_gmlp_ln_g, ln_b=m_gmlp_ln_b, ws=m_gmlp_ws, bs=m_gmlp_bs,
              hg_lb=m_hg_lb, hg_ng=m_hg_norm_g, norm2_g=m_norm2_g, final_g=m_final_norm_g)
    vs = dict(b_ada=v_b_ada, norm1_g=v_norm1_g, b_gate=v_b_gate, ln_g=v_gmlp_ln_g, ln_b=v_gmlp_ln_b, ws=v_gmlp_ws, bs=v_gmlp_bs,
              hg_lb=v_hg_lb, hg_ng=v_hg_norm_g, norm2_g=v_norm2_g, final_g=v_final_norm_g)
    w_sm, _ = _pack([wp[k] for k in _SMALL])
    m_sm, _ = _pack([ms[k] for k in _SMALL])
    v_sm, _ = _pack([vs[k] for k in _SMALL])
    sm_out = _small_update(gathered, w_sm, m_sm, v_sm, after)
    shapes = dict(b_ada=b_ada.shape, norm1_g=norm1_g.shape, b_gate=b_gate.shape, ln_g=gmlp_ln_g.shape, ln_b=gmlp_ln_b.shape,
                  ws=gmlp_ws.shape, bs=gmlp_bs.shape, hg_lb=hg_lb.shape, hg_ng=hg_norm_g.shape, norm2_g=norm2_g.shape,
                  final_g=final_norm_g.shape)

    def unpack(arr, k):
        i = _SMALL.index(k)
        n = math.prod(shapes[k])
        return arr.reshape(-1)[offs[i]:offs[i] + n].reshape(shapes[k])

    loss = sm_out[0].reshape(-1)[offs[len(_SMALL)]]

    dmod_all = gathered.reshape(N_DEV, -1)[:, offs[0]:offs[0] + N_DEV * ada_loc]
    dmod_loc = lax.dynamic_slice_in_dim(dmod_all, me * ada_loc, ada_loc, axis=1)
    ca_t = jnp.pad(c_act[:N_DEV].T, ((0, 0), (0, LANES - N_DEV))).astype(BF16)
    dm_p = jnp.pad(dmod_loc, ((0, LANES - N_DEV), (0, 0))).astype(BF16)
    tm_a = _tile(D, 512)
    g_ada = _matmul(
        "ada_dw", ca_t, dm_p, dims=_NN, grid_mnk=(D // tm_a, 1, 1), tiles=(tm_a, ada_loc),
        a_spec=pl.BlockSpec((tm_a, LANES), lambda i, j, k: (i, 0)), b_spec=pl.BlockSpec((LANES, ada_loc), lambda i, j, k: (0, 0)),
        out_shapes=[jax.ShapeDtypeStruct((1, D, ada_loc), F32)], out_specs=[pl.BlockSpec((None, tm_a, ada_loc), lambda i, j, k: (0, i, 0))],
        epilogue=_store(F32))[0]
    upd["w_ada"] = _adamw("adamw_w_ada", w_ada[0], m_w_ada[0], v_w_ada[0], [g_ada])
    land_and_update("scatter_proj_in_b", land_and_update("scatter_proj_in_a", upd["w_ada"][1]))

    order = ("w_ada", "b_ada", "norm1_g", "w_in", "b_gate", "ln_g", "ln_b", "ws", "bs", "hg_lb", "hg_ng", "w_branch_gmlp", "w_branch_hg",
             "w_out", "norm2_g", "w_ffn_in", "w_ffn_out", "final_g")
    outs = [loss, grad_x[None]]
    for idx in range(4):
        for k in order:
            outs.append(upd[k][idx][None] if k in upd else unpack(sm_out[idx], k))
    return tuple(outs)
```

```python
import functools
import math

import jax
import jax.numpy as jnp
from jax import lax
from jax.experimental import pallas as pl
from jax.experimental.pallas import tpu as pltpu

F32 = jnp.float32
BF16 = jnp.bfloat16
N_DEV = 8
EPS = 1e-6
LANES = 128
HG_DK = 128
HG_CHUNK = 64
HG_MID = HG_CHUNK // 2 - 1
EXP_CLAMP = 80.0
VMEM_LIMIT = 48 * 1024 * 1024
ADAM_LR, ADAM_B1, ADAM_B2, ADAM_EPS, ADAM_WD, ADAM_STEP = 0.001, 0.9, 0.999, 1e-08, 0.01, 10
MESH = pl.DeviceIdType.MESH

_NN = (((1,), (0,)), ((), ()))
_NT = (((1,), (1,)), ((), ()))
_TN = (((0,), (0,)), ((), ()))


def _dot(a, b, dims=_NN):
    return lax.dot_general(a.astype(BF16), b.astype(BF16), dims, preferred_element_type=F32)


def _tile(n, target, mult=LANES):
    best = None
    for t in range(mult, min(n, target) + 1, mult):
        if n % t == 0:
            best = t
    return n if best is None else best


def _cparams(sem):
    return pltpu.CompilerParams(dimension_semantics=sem, vmem_limit_bytes=VMEM_LIMIT)


def _sigmoid(x):
    return 1.0 / (1.0 + jnp.exp(-x))


def _sigmoid_abs(x):
    return 0.5 * jnp.tanh(0.5 * x) + 0.5


def _gelu_parts(x):
    k0 = math.sqrt(2.0 / math.pi)
    x2 = x * x
    t = jnp.tanh(k0 * (x + 0.044715 * x * x2))
    g = 0.5 * x * (1.0 + t)
    dg = 0.5 * (1.0 + t) + 0.5 * x * (1.0 - t * t) * (k0 * (1.0 + 3.0 * 0.044715 * x2))
    return g, dg


def _split3(x):
    h = x.astype(BF16)
    r = x - h.astype(F32)
    m = r.astype(BF16)
    lo = (r - m.astype(F32)).astype(BF16)
    return h, m, lo


def _ones_dot(mat01, x):
    h, m, lo = _split3(x)
    d = functools.partial(lax.dot_general, dimension_numbers=_NN, preferred_element_type=F32)
    return d(mat01, h) + d(mat01, m) + d(mat01, lo)


def _matmul(name, a, b, *, dims, grid_mnk, tiles, a_spec, b_spec, extras=(), extra_specs=(), out_shapes, out_specs, epilogue, after=None,
            sem=None):
    gm, gn, nk = grid_mnk
    tm, tn = tiles
    n_ex, n_out = len(extras), len(out_shapes)
    held = [] if after is None else [after]

    def body(*refs):
        a_ref, b_ref = refs[0], refs[1]
        ex = refs[2:2 + n_ex]
        outs = refs[2 + n_ex + len(held):2 + n_ex + len(held) + n_out]
        more = () if sem is None else (pl.program_id(0) == 0,)
        if nk == 1:
            epilogue(lax.dot_general(a_ref[...], b_ref[...], dims, preferred_element_type=F32), ex, outs, *more)
            return
        acc = refs[-1]
        k = pl.program_id(2)

        @pl.when(k == 0)
        def _():
            acc[...] = jnp.zeros_like(acc)

        acc[...] += lax.dot_general(a_ref[...], b_ref[...], dims, preferred_element_type=F32)

        @pl.when(k == nk - 1)
        def _():
            epilogue(acc[...], ex, outs, *more)

    return pl.pallas_call(
        body, name=name, grid=(gm, gn, nk), in_specs=[a_spec, b_spec, *extra_specs] + [pl.BlockSpec(memory_space=pl.ANY)] * len(held),
        out_specs=list(out_specs), out_shape=list(out_shapes), scratch_shapes=[] if nk == 1 else [pltpu.VMEM((tm, tn), F32)],
        compiler_params=_cparams(sem or ("parallel", "parallel", "arbitrary")),
    )(a, b, *extras, *held)


def _store(dtype):
    def ep(acc, ex, outs):
        outs[0][...] = acc.astype(dtype)
    return ep


def _mm_nn_stacked(name, a, wg, *, tm, tn, tk, out_dtype=F32, extras=(), extra_specs=(), out_shapes=None, out_specs=None, epilogue=None,
                   after=None):
    M, K = a.shape
    _, _, nloc = wg.shape
    N = nloc * N_DEV
    q = nloc // tn
    if out_shapes is None:
        out_shapes = [jax.ShapeDtypeStruct((M, N), out_dtype)]
        out_specs = [pl.BlockSpec((tm, tn), lambda i, j, k: (i, j))]
        epilogue = _store(out_dtype)
    return _matmul(
        name, a, wg, dims=_NN, grid_mnk=(M // tm, N // tn, K // tk), tiles=(tm, tn),
        a_spec=pl.BlockSpec((tm, tk), lambda i, j, k: (i, k)),
        b_spec=pl.BlockSpec((None, tk, tn), lambda i, j, k: (j // q, k, j % q)),
        extras=extras, extra_specs=extra_specs, out_shapes=out_shapes, out_specs=out_specs, epilogue=epilogue, after=after)


def _mm_nt_stacked(name, a_spec, a, wg, *, M, tm, tn, tk, out_dtype=F32, after=None, extras=(), extra_specs=(), out_shapes=None,
                   out_specs=None, epilogue=None, sem=None):
    _, Kw, nloc = wg.shape
    q = nloc // tk
    single = out_shapes is None
    if single:
        out_shapes = [jax.ShapeDtypeStruct((M, Kw), out_dtype)]
        out_specs = [pl.BlockSpec((tm, tn), lambda i, j, k: (i, j))]
        epilogue = _store(out_dtype)
    res = _matmul(
        name, a, wg, dims=_NT, grid_mnk=(M // tm, Kw // tn, (nloc * N_DEV) // tk), tiles=(tm, tn),
        a_spec=a_spec, b_spec=pl.BlockSpec((None, tn, tk), lambda i, j, k: (k // q, j, k % q)),
        extras=extras, extra_specs=extra_specs, out_shapes=out_shapes, out_specs=out_specs, epilogue=epilogue, after=after, sem=sem)
    return res[0] if single else res


def _mm_tn(name, a, b, b_spec, *, Mo, No, S, tm, tn, tk, stacked_nloc=None, after=None, a_off=0):
    if stacked_nloc is None:
        out_shape = jax.ShapeDtypeStruct((Mo, No), BF16)
        out_spec = pl.BlockSpec((tm, tn), lambda i, j, k: (i, j))
    else:
        q = stacked_nloc // tn
        out_shape = jax.ShapeDtypeStruct((N_DEV, Mo, stacked_nloc), BF16)
        out_spec = pl.BlockSpec((None, tm, tn), lambda i, j, k: (j // q, i, j % q))
    return _matmul(
        name, a, b, dims=_TN, grid_mnk=(Mo // tm, No // tn, S // tk), tiles=(tm, tn),
        a_spec=pl.BlockSpec((tk, tm), lambda i, j, k: (k, i + a_off)), b_spec=b_spec,
        out_shapes=[out_shape], out_specs=[out_spec], epilogue=_store(BF16), after=after)[0]


def _norm_mod(name, x, g, sc, sh):
    S, D = x.shape
    tm = _tile(S, 256, 8)

    def body(x_ref, g_ref, sc_ref, sh_ref, h_ref):
        xv = x_ref[...]
        r = lax.rsqrt(jnp.mean(xv * xv, axis=-1, keepdims=True) + EPS)
        h = (xv * r) * g_ref[...]
        h_ref[...] = (h * (1.0 + sc_ref[...]) + sh_ref[...]).astype(BF16)

    row = pl.BlockSpec((tm, D), lambda i: (i, 0))
    vec = pl.BlockSpec((1, D), lambda i: (0, 0))
    return pl.pallas_call(body, name=name, grid=(S // tm,), in_specs=[row, vec, vec, vec], out_specs=row,
                          out_shape=jax.ShapeDtypeStruct((S, D), BF16), compiler_params=_cparams(("parallel",)))(x, g, sc, sh)


def _norm_mod_bwd_rows(first, dh_v, x_ref, g_ref, sc_ref, dres_ref, dx_ref, vec_ref, o_ref=None, gt_ref=None, do_ref=None):
    @pl.when(first)
    def _():
        vec_ref[...] = jnp.zeros_like(vec_ref)

    xv, gv = x_ref[...], g_ref[...]
    r = lax.rsqrt(jnp.mean(xv * xv, axis=-1, keepdims=True) + EPS)
    xn = xv * r
    one_sc = 1.0 + sc_ref[...]
    vec_ref[0:1, :] += jnp.sum(dh_v, axis=0, keepdims=True)
    vec_ref[1:2, :] += jnp.sum(dh_v * (xn * gv), axis=0, keepdims=True)
    vec_ref[2:3, :] += jnp.sum(dh_v * one_sc * xn, axis=0, keepdims=True)
    dxn = dh_v * one_sc * gv
    dx = dres_ref[...] + r * (dxn - xn * jnp.mean(dxn * xn, axis=-1, keepdims=True))
    dx_ref[...] = dx
    if o_ref is not None:
        vec_ref[3:4, :] += jnp.sum(dx * o_ref[...], axis=0, keepdims=True)
        do_ref[...] = (dx * gt_ref[...]).astype(BF16)


def _norm_mod_bwd(name, dh, x, g, sc, dres, o=None, gt=None):
    S, D = x.shape
    tm = _tile(S, 256, 8)
    gated = o is not None

    def body(*refs):
        if gated:
            dh_ref, x_ref, g_ref, sc_ref, dres_ref, o_ref, gt_ref, dx_ref, vec_ref, do_ref = refs
        else:
            dh_ref, x_ref, g_ref, sc_ref, dres_ref, dx_ref, vec_ref = refs
            o_ref = gt_ref = do_ref = None
        _norm_mod_bwd_rows(pl.program_id(0) == 0, dh_ref[...], x_ref, g_ref, sc_ref, dres_ref, dx_ref, vec_ref, o_ref, gt_ref, do_ref)

    row = pl.BlockSpec((tm, D), lambda i: (i, 0))
    vec = pl.BlockSpec((1, D), lambda i: (0, 0))
    acc = pl.BlockSpec((8, D), lambda i: (0, 0))
    ins = [dh, x, g, sc, dres] + ([o, gt] if gated else [])
    in_specs = [row, row, vec, vec, row] + ([row, vec] if gated else [])
    out_shape = [jax.ShapeDtypeStruct((S, D), F32), jax.ShapeDtypeStruct((8, D), F32)]
    out_specs = [row, acc]
    if gated:
        out_shape.append(jax.ShapeDtypeStruct((S, D), BF16))
        out_specs.append(row)
    return pl.pallas_call(body, name=name, grid=(S // tm,), in_specs=in_specs, out_specs=out_specs, out_shape=out_shape,
                          compiler_params=_cparams(("arbitrary",)))(*ins)


def _loss_head(x3, tgt, gf, o2, gt2):
    S, D = x3.shape
    tm = _tile(S, 256, 8)

    def body(x_ref, t_ref, g_ref, o_ref, gt_ref, dx_ref, do_ref, vec_ref):
        i = pl.program_id(0)

        @pl.when(i == 0)
        def _():
            vec_ref[...] = jnp.zeros_like(vec_ref)

        xv, gv = x_ref[...], g_ref[...]
        r = lax.rsqrt(jnp.mean(xv * xv, axis=-1, keepdims=True) + EPS)
        xn = xv * r
        e = xn * gv - t_ref[...]
        tok = 0.5 * jnp.mean(e * e, axis=-1, keepdims=True)
        vec_ref[0:1, :] += jnp.broadcast_to(jnp.sum(tok, axis=0, keepdims=True), (1, D))
        dy = e * (1.0 / D)
        vec_ref[1:2, :] += jnp.sum(dy * xn, axis=0, keepdims=True)
        dxn = dy * gv
        dx = r * (dxn - xn * jnp.mean(dxn * xn, axis=-1, keepdims=True))
        dx_ref[...] = dx
        vec_ref[2:3, :] += jnp.sum(dx * o_ref[...], axis=0, keepdims=True)
        do_ref[...] = (dx * gt_ref[...]).astype(BF16)

    row = pl.BlockSpec((tm, D), lambda i: (i, 0))
    vec = pl.BlockSpec((1, D), lambda i: (0, 0))
    return pl.pallas_call(
        body, name="loss_head", grid=(S // tm,), in_specs=[row, row, vec, row, vec],
        out_specs=[row, row, pl.BlockSpec((8, D), lambda i: (0, 0))],
        out_shape=[jax.ShapeDtypeStruct((S, D), F32), jax.ShapeDtypeStruct((S, D), BF16), jax.ShapeDtypeStruct((8, D), F32)],
        compiler_params=_cparams(("arbitrary",)))(x3, tgt, gf, o2, gt2)


def _ffn_in_swiglu(h, wg):
    S, D = h.shape
    _, _, tf = wg.shape
    nf = N_DEV // 2
    F = nf * tf
    tm = _tile(S, 256, 16)

    def body(h_ref, wa_ref, wu_ref, hf_ref, fac_ref):
        hv = h_ref[...]
        a = lax.dot_general(hv, wa_ref[...], _NN, preferred_element_type=F32)
        up = lax.dot_general(hv, wu_ref[...], _NN, preferred_element_type=F32)
        sa = _sigmoid_abs(a)
        silu = a * sa
        hf_ref[...] = (silu * up).astype(BF16)
        fac_ref[0] = (up * (sa * (1.0 + a * (1.0 - sa)))).astype(BF16)
        fac_ref[1] = silu.astype(BF16)

    return pl.pallas_call(
        body, name="ffn_in_swiglu", grid=(nf, S // tm),
        in_specs=[pl.BlockSpec((tm, D), lambda j, i: (i, 0)), pl.BlockSpec((None, D, tf), lambda j, i: (j, 0, 0)),
                  pl.BlockSpec((None, D, tf), lambda j, i: (j + nf, 0, 0))],
        out_specs=[pl.BlockSpec((tm, tf), lambda j, i: (i, j)), pl.BlockSpec((2, tm, tf), lambda j, i: (0, i, j))],
        out_shape=[jax.ShapeDtypeStruct((S, F), BF16), jax.ShapeDtypeStruct((2, S, F), BF16)],
        compiler_params=_cparams(("parallel", "parallel")))(h, wg, wg)


def _colsum2(dg2):
    _, S, D = dg2.shape
    tm = _tile(S, 256, 16)

    def body(x_ref, o_ref):
        @pl.when(pl.program_id(0) == 0)
        def _():
            o_ref[...] = jnp.zeros_like(o_ref)

        o_ref[0:1, :] += jnp.sum(x_ref[0].astype(F32), axis=0, keepdims=True)
        o_ref[1:2, :] += jnp.sum(x_ref[1].astype(F32), axis=0, keepdims=True)

    return pl.pallas_call(body, name="gate_bias_grad", grid=(S // tm,), in_specs=[pl.BlockSpec((2, tm, D), lambda i: (0, i, 0))],
                          out_specs=pl.BlockSpec((2, D), lambda i: (0, 0)), out_shape=jax.ShapeDtypeStruct((2, D), F32),
                          compiler_params=_cparams(("arbitrary",)))(dg2)


def _gmlp_common(u_ref, v_ref, lg_ref, lb_ref, ws_ref, bsb_ref, G, T, Dg):
    ug, dug = _gelu_parts(u_ref[...])
    vg, dvg = _gelu_parts(v_ref[...])
    mu = jnp.mean(vg, axis=-1, keepdims=True)
    vc = vg - mu
    rstd = lax.rsqrt(jnp.mean(vc * vc, axis=-1, keepdims=True) + EPS)
    vhat = vc * rstd
    vn = vhat * lg_ref[...] + lb_ref[...]
    row = lax.broadcasted_iota(jnp.int32, (T, T), 0)
    col = lax.broadcasted_iota(jnp.int32, (T, T), 1)
    tril = row >= col
    s = []
    for g in range(G):
        w = jnp.where(tril, ws_ref[g], 0.0)
        s.append(_dot(w, vn[:, g * Dg:(g + 1) * Dg]) + bsb_ref[g])
    return ug, dug, dvg, rstd, vhat, vn, tril, s


def _gmlp_fwd(z, ln_g, ln_b, ws, bsb, GW):
    S = z.shape[0]
    G, T, _ = ws.shape
    Dg = GW // G

    def body(u_ref, v_ref, lg_ref, lb_ref, ws_ref, bsb_ref, ya_ref):
        ug, _, _, _, _, _, _, s = _gmlp_common(u_ref, v_ref, lg_ref, lb_ref, ws_ref, bsb_ref, G, T, Dg)
        for g in range(G):
            sl = slice(g * Dg, (g + 1) * Dg)
            ya_ref[:, sl] = (ug[:, sl] * s[g]).astype(BF16)

    vec = pl.BlockSpec((1, GW), lambda c: (0, 0))
    return pl.pallas_call(
        body, name="gmlp_fwd", grid=(S // T,),
        in_specs=[pl.BlockSpec((T, GW), lambda c: (c, 0)), pl.BlockSpec((T, GW), lambda c: (c, 1)), vec, vec,
                  pl.BlockSpec((G, T, T), lambda c: (0, 0, 0)), pl.BlockSpec((G, T, Dg), lambda c: (0, 0, 0))],
        out_specs=pl.BlockSpec((T, GW), lambda c: (c, 0)), out_shape=jax.ShapeDtypeStruct((S, GW), BF16),
        compiler_params=_cparams(("parallel",)))(z, z, ln_g, ln_b, ws, bsb)


def _gmlp_bwd(z, dya, ln_g, ln_b, ws, bsb, GW):
    S = z.shape[0]
    G, T, _ = ws.shape
    Dg = GW // G
    nc = S // T

    def body(u_ref, v_ref, dya_ref, lg_ref, lb_ref, ws_ref, bsb_ref, dz_ref, dln_ref, dws_ref, dbs_ref, dbs_acc, dvh):
        c = pl.program_id(0)

        @pl.when(c == 0)
        def _():
            dln_ref[...] = jnp.zeros_like(dln_ref)
            dws_ref[...] = jnp.zeros_like(dws_ref)
            dbs_acc[...] = jnp.zeros_like(dbs_acc)

        ug, dug, dvg, rstd, vhat, vn, tril, s = _gmlp_common(u_ref, v_ref, lg_ref, lb_ref, ws_ref, bsb_ref, G, T, Dg)
        dya_v = dya_ref[...]
        for g in range(G):
            sl = slice(g * Dg, (g + 1) * Dg)
            dy_g = dya_v[:, sl]
            dz_ref[:, sl] = (dy_g * s[g] * dug[:, sl]).astype(BF16)
            ds = dy_g * ug[:, sl]
            dbs_acc[g] += ds
            w = jnp.where(tril, ws_ref[g], 0.0)
            dvn_g = _dot(w, ds, _TN)
            dws_ref[g] += jnp.where(tril, _dot(ds, vn[:, sl], _NT), 0.0)
            dln_ref[0:1, sl] += jnp.sum(dvn_g * vhat[:, sl], axis=0, keepdims=True)
            dln_ref[1:2, sl] += jnp.sum(dvn_g, axis=0, keepdims=True)
            dvh[:, sl] = dvn_g * lg_ref[:, sl]
        dvhat = dvh[...]
        m1 = jnp.mean(dvhat, axis=-1, keepdims=True)
        m2 = jnp.mean(dvhat * vhat, axis=-1, keepdims=True)
        dz_ref[:, GW:2 * GW] = (rstd * (dvhat - m1 - vhat * m2) * dvg).astype(BF16)

        @pl.when(c == nc - 1)
        def _():
            for g in range(G):
                dbs_ref[g] = jnp.sum(dbs_acc[g], axis=-1, keepdims=True)

    vec = pl.BlockSpec((1, GW), lambda c: (0, 0))
    return pl.pallas_call(
        body, name="gmlp_bwd", grid=(nc,),
        in_specs=[pl.BlockSpec((T, GW), lambda c: (c, 0)), pl.BlockSpec((T, GW), lambda c: (c, 1)),
                  pl.BlockSpec((T, GW), lambda c: (c, 0)), vec, vec,
                  pl.BlockSpec((G, T, T), lambda c: (0, 0, 0)), pl.BlockSpec((G, T, Dg), lambda c: (0, 0, 0))],
        out_specs=[pl.BlockSpec((T, 2 * GW), lambda c: (c, 0)), pl.BlockSpec((8, GW), lambda c: (0, 0)),
                   pl.BlockSpec((G, T, T), lambda c: (0, 0, 0)), pl.BlockSpec((G, T, 1), lambda c: (0, 0, 0))],
        out_shape=[jax.ShapeDtypeStruct((S, 2 * GW), BF16), jax.ShapeDtypeStruct((8, GW), F32),
                   jax.ShapeDtypeStruct((G, T, T), F32), jax.ShapeDtypeStruct((G, T, 1), F32)],
        scratch_shapes=[pltpu.VMEM((G, T, Dg), F32), pltpu.VMEM((T, GW), F32)],
        compiler_params=_cparams(("arbitrary",)))(z, z, dya, ln_g, ln_b, ws, bsb)


def _hg_common(q_ref, f_ref, hlb_ref):
    C = HG_CHUNK
    a = hlb_ref[...]
    lb = _sigmoid(a[0:1, :] - a[1:2, :])
    sig = _sigmoid(f_ref[...])
    f = lb + (1.0 - lb) * sig
    lf = jnp.log(f)
    kk = 1.0 - f
    q = q_ref[...]
    sq = _sigmoid_abs(q)
    qa = q * sq
    row = lax.broadcasted_iota(jnp.int32, (C, C), 0)
    col = lax.broadcasted_iota(jnp.int32, (C, C), 1)
    tril = row >= col
    b = _ones_dot(tril.astype(BF16), lf)
    bm = b[HG_MID:HG_MID + 1, :]
    bl = b[C - 1:C, :]
    e_b = jnp.exp(b)
    e_qm = jnp.exp(jnp.minimum(b - bm, EXP_CLAMP))
    e_km = jnp.exp(jnp.minimum(bm - b, EXP_CLAMP))
    e_kl = jnp.exp(bl - b)
    return dict(lb=lb, sig=sig, f=f, kk=kk, q=q, sq=sq, qa=qa, tril=tril, e_b=e_b, e_qm=e_qm, e_km=e_km, e_kl=e_kl,
                e_l=jnp.exp(bl), qh=qa * e_b, qt=qa * e_qm, kt=kk * e_km, kh=kk * e_kl)


def _hg_fwd(z, hg_lb, ng, HW):
    S = z.shape[0]
    C, H, dk = HG_CHUNK, HW // HG_DK, HG_DK
    nc = S // C

    def body(q_ref, f_ref, i_ref, og_ref, hlb_ref, ng_ref, yb_ref, o_ref, st_ref, state):
        @pl.when(pl.program_id(0) == 0)
        def _():
            state[...] = jnp.zeros_like(state)

        t = _hg_common(q_ref, f_ref, hlb_ref)
        iv = i_ref[...]
        for h in range(H):
            sl = slice(h * dk, (h + 1) * dk)
            st = state[h]
            st_ref[h] = st
            a = jnp.where(t["tril"], _dot(t["qt"][:, sl], t["kt"][:, sl], _NT), 0.0)
            o_h = _dot(a, iv[:, sl]) + _dot(t["qh"][:, sl], st, _NT)
            state[h] = st * t["e_l"][:, sl] + _dot(iv[:, sl], t["kh"][:, sl], _TN)
            o_ref[:, sl] = o_h
            rr = lax.rsqrt(jnp.mean(o_h * o_h, axis=-1, keepdims=True) + EPS)
            og = og_ref[:, sl]
            yb_ref[:, sl] = (o_h * rr * ng_ref[:, sl] * (og * _sigmoid_abs(og))).astype(BF16)

    def col(k):
        return pl.BlockSpec((C, HW), lambda c: (c, k))

    base = 2
    return pl.pallas_call(
        body, name="hgrn_fwd", grid=(nc,),
        in_specs=[col(base), col(base + 1), col(base + 2), col(base + 3),
                  pl.BlockSpec((2, HW), lambda c: (0, 0)), pl.BlockSpec((1, HW), lambda c: (0, 0))],
        out_specs=[pl.BlockSpec((C, HW), lambda c: (c, 0)), pl.BlockSpec((C, HW), lambda c: (c, 0)),
                   pl.BlockSpec((None, H, dk, dk), lambda c: (c, 0, 0, 0))],
        out_shape=[jax.ShapeDtypeStruct((S, HW), BF16), jax.ShapeDtypeStruct((S, HW), F32),
                   jax.ShapeDtypeStruct((nc, H, dk, dk), F32)],
        scratch_shapes=[pltpu.VMEM((H, dk, dk), F32)],
        compiler_params=_cparams(("arbitrary",)))(z, z, z, z, hg_lb, ng)


def _hg_bwd(z, o, states, dyb, hg_lb, ng, HW, dz_head, dz_tail):
    S = z.shape[0]
    C, H, dk = HG_CHUNK, HW // HG_DK, HG_DK
    nc = S // C
    B0 = dz_head.shape[1]
    DT = dz_tail.shape[2]
    INW = B0 + 4 * HW + 2 * DT

    def body(q_ref, f_ref, i_ref, og_ref, o_ref, st_ref, stn_ref, dyb_ref, hlb_ref, ng_ref, head_ref, tail_ref,
             dzf_ref, dng_ref, dhlb_ref, dstate, cross, dqa_buf, dkk_buf, db_buf, dlb_acc):
        c = pl.program_id(0)
        dzf_ref[:, 0:B0] = head_ref[...]
        dzf_ref[:, B0 + 4 * HW:B0 + 4 * HW + DT] = tail_ref[0]
        dzf_ref[:, B0 + 4 * HW + DT:INW] = tail_ref[1]
        dz_ref = dzf_ref.at[:, B0:B0 + 4 * HW]

        @pl.when(c == 0)
        def _():
            dstate[...] = jnp.zeros_like(dstate)
            dlb_acc[...] = jnp.zeros_like(dlb_acc)
            dng_ref[...] = jnp.zeros_like(dng_ref)

        def r16(v):
            return v.astype(BF16).astype(F32)

        t = _hg_common(q_ref, f_ref, hlb_ref)
        iv = i_ref[...]
        for h in range(H):
            sl = slice(h * dk, (h + 1) * dk)
            o_h, og, dyb_h, ng_h = o_ref[:, sl], og_ref[:, sl], dyb_ref[:, sl], ng_ref[:, sl]
            sg = _sigmoid_abs(og)
            silu_og = og * sg
            rr = lax.rsqrt(jnp.mean(o_h * o_h, axis=-1, keepdims=True) + EPS)
            on = o_h * rr
            dng_ref[0:1, sl] += jnp.sum(dyb_h * on * silu_og, axis=0, keepdims=True)
            dz_ref[:, 3 * HW + h * dk:3 * HW + (h + 1) * dk] = (dyb_h * on * ng_h * (sg * (1.0 + og * (1.0 - sg)))).astype(BF16)
            don = dyb_h * ng_h * silu_og
            do_h = rr * (don - on * jnp.mean(don * on, axis=-1, keepdims=True))

            qt, kt, qh, kh, iv_h = t["qt"][:, sl], t["kt"][:, sl], t["qh"][:, sl], t["kh"][:, sl], iv[:, sl]
            a = jnp.where(t["tril"], _dot(qt, kt, _NT), 0.0)
            da = jnp.where(t["tril"], _dot(do_h, iv_h, _NT), 0.0)
            st, dst = st_ref[h], dstate[h]
            cross[:, sl] = jnp.sum(dst * stn_ref[h], axis=0, keepdims=True)
            dqh = _dot(do_h, st)
            dstate[h] = _dot(do_h, qh, _TN) + dst * t["e_l"][:, sl]
            div = _dot(a, do_h, _TN) + _dot(kh, dst, _NT)
            dkh = _dot(iv_h, dst)
            dqt = _dot(da, kt)
            dkt = _dot(da, qt, _TN)
            dz_ref[:, 2 * HW + h * dk:2 * HW + (h + 1) * dk] = div.astype(BF16)
            dqa_buf[:, sl] = dqh * t["e_b"][:, sl] + dqt * t["e_qm"][:, sl]
            dkk_buf[:, sl] = dkt * t["e_km"][:, sl] + dkh * t["e_kl"][:, sl]
            db_buf[:, sl] = r16(qt) * dqt - r16(kt) * dkt + r16(qh) * dqh - r16(kh) * dkh

        dqa, dkk = dqa_buf[...], dkk_buf[...]
        triu = jnp.logical_not(t["tril"]) | (lax.broadcasted_iota(jnp.int32, (C, C), 0) == lax.broadcasted_iota(jnp.int32, (C, C), 1))
        dlf = _ones_dot(triu.astype(BF16), db_buf[...]) + cross[...]
        df = dlf / t["f"] - dkk
        sig, lb = t["sig"], t["lb"]
        dz_ref[:, HW:2 * HW] = (df * (1.0 - lb) * sig * (1.0 - sig)).astype(BF16)
        dlb_acc[...] += jnp.sum(df * (1.0 - sig), axis=0, keepdims=True)
        q, sq = t["q"], t["sq"]
        dz_ref[:, 0:HW] = (dqa * (sq * (1.0 + q * (1.0 - sq)))).astype(BF16)

        @pl.when(c == nc - 1)
        def _():
            da0 = dlb_acc[...] * lb * (1.0 - lb)
            dhlb_ref[0:1, :] = da0
            dhlb_ref[1:2, :] = -da0

    def col(k):
        return pl.BlockSpec((C, HW), lambda c: (nc - 1 - c, k))

    base = 2
    return pl.pallas_call(
        body, name="hgrn_bwd", grid=(nc,),
        in_specs=[col(base), col(base + 1), col(base + 2), col(base + 3), col(0),
                  pl.BlockSpec((None, H, dk, dk), lambda c: (nc - 1 - c, 0, 0, 0)),
                  pl.BlockSpec((None, H, dk, dk), lambda c: (jnp.minimum(nc - c, nc - 1), 0, 0, 0)), col(0),
                  pl.BlockSpec((2, HW), lambda c: (0, 0)), pl.BlockSpec((1, HW), lambda c: (0, 0)),
                  pl.BlockSpec((C, B0), lambda c: (nc - 1 - c, 0)), pl.BlockSpec((2, C, DT), lambda c: (0, nc - 1 - c, 0))],
        out_specs=[pl.BlockSpec((C, INW), lambda c: (nc - 1 - c, 0)), pl.BlockSpec((8, HW), lambda c: (0, 0)),
                   pl.BlockSpec((2, HW), lambda c: (0, 0))],
        out_shape=[jax.ShapeDtypeStruct((S, INW), BF16), jax.ShapeDtypeStruct((8, HW), F32), jax.ShapeDtypeStruct((2, HW), F32)],
        scratch_shapes=[pltpu.VMEM((H, dk, dk), F32), pltpu.VMEM((1, HW), F32), pltpu.VMEM((C, HW), F32), pltpu.VMEM((C, HW), F32),
                        pltpu.VMEM((C, HW), F32), pltpu.VMEM((1, HW), F32)],
        compiler_params=_cparams(("arbitrary",)))(z, z, z, z, o, states, states, dyb, hg_lb, ng, dz_head, dz_tail)


def _position():
    x, y, c = lax.axis_index("x"), lax.axis_index("y"), lax.axis_index("c")
    return x, y, c, 4 * x + 2 * y + c


def _flip(x, y, c, k):
    return (1 - x if k & 4 else x, 1 - y if k & 2 else y, 1 - c if k & 1 else c)


def _allgather_small(name, v):
    R, L = v.shape

    def body(v_ref, out_ref, send_sems, recv_sems):
        x, y, c, me = _position()
        out_ref[me] = v_ref[...]
        copies = []
        for k in range(1, N_DEV):
            cp = pltpu.make_async_remote_copy(src_ref=v_ref, dst_ref=out_ref.at[me], send_sem=send_sems.at[k - 1],
                                              recv_sem=recv_sems.at[k - 1], device_id=_flip(x, y, c, k), device_id_type=MESH)
            cp.start()
            copies.append(cp)
        for cp in copies:
            cp.wait()

    return pl.pallas_call(
        body, name=name, out_shape=jax.ShapeDtypeStruct((N_DEV, R, L), v.dtype),
        in_specs=[pl.BlockSpec(memory_space=pltpu.VMEM)], out_specs=pl.BlockSpec(memory_space=pltpu.VMEM),
        scratch_shapes=[pltpu.SemaphoreType.DMA((N_DEV - 1,)), pltpu.SemaphoreType.DMA((N_DEV - 1,))],
        compiler_params=pltpu.CompilerParams(vmem_limit_bytes=VMEM_LIMIT),
    )(v)


def _allgather_hbm(name, shards):
    n = len(shards)

    def body(*refs):
        ins, outs = refs[:n], refs[n:2 * n]
        send_sems, recv_sems, local_sems = refs[2 * n:]
        x, y, c, me = _position()
        sibling = (x, y, 1 - c)
        chips = [(1 - x, y), (x, 1 - y), (1 - x, 1 - y)]

        def slot(px, py, pc):
            return 4 * px + 2 * py + pc

        def copy(w, k, block, to, src=None):
            dst = outs[w].at[slot(*block)]
            return pltpu.make_async_remote_copy(src_ref=dst if src is None else src, dst_ref=dst, send_sem=send_sems.at[w, k],
                                                recv_sem=recv_sems.at[w, k], device_id=to, device_id_type=MESH)

        mine, first, passed = [], [], []
        for w in range(n):
            cp = pltpu.make_async_copy(ins[w], outs[w].at[me], local_sems.at[w])
            cp.start()
            mine.append(cp)
            for j, chip in enumerate(chips):
                first.append(copy(w, 1 + j, (x, y, c), (*chip, c), src=ins[w]))
            first.append(copy(w, 0, (x, y, c), sibling, src=ins[w]))
        for cp in first:
            cp.start()
        for w in range(n):
            for j, chip in enumerate(chips):
                copy(w, 1 + j, (*chip, c), (x, y, c)).wait_recv()
                cp = copy(w, 4 + j, (*chip, c), sibling)
                cp.start()
                passed.append(cp)
        for w in range(n):
            copy(w, 0, sibling, (x, y, c)).wait_recv()
            for j, chip in enumerate(chips):
                copy(w, 4 + j, (*chip, 1 - c), (x, y, c)).wait_recv()
        for cp in first + passed:
            cp.wait_send()
        for cp in mine:
            cp.wait()

    hbm = pl.BlockSpec(memory_space=pltpu.HBM)
    return pl.pallas_call(
        body, name=name, out_shape=[jax.ShapeDtypeStruct((N_DEV, *s.shape), s.dtype) for s in shards],
        in_specs=[hbm] * n, out_specs=[hbm] * n,
        scratch_shapes=[pltpu.SemaphoreType.DMA((n, 7)), pltpu.SemaphoreType.DMA((n, 7)), pltpu.SemaphoreType.DMA((n,))],
    )(*shards)


_HBM = pl.BlockSpec(memory_space=pltpu.HBM)
_SEM = pl.BlockSpec(memory_space=pltpu.SEMAPHORE)
_EFFECT = pltpu.SideEffectType.DATAFLOW_SIDE_EFFECTING


def _split_start(name, bufs, n_sems, copies_fn, after=None):
    nb = len(bufs)
    extra = [] if after is None else (list(after) if isinstance(after, (list, tuple)) else [after])
    k = nb + len(extra)

    def body(*refs):
        for cp in copies_fn(refs[:nb], refs[k], refs[k + 1]):
            cp.start()
        refs[-1][...] = jnp.zeros_like(refs[-1])

    sems = pltpu.SemaphoreType.DMA((n_sems,))
    res = pl.pallas_call(
        body, name=name,
        out_shape=(sems, sems, *[pltpu.HBM(a.shape, a.dtype) for a in bufs], jax.ShapeDtypeStruct((8, LANES), F32)),
        in_specs=[_HBM] * nb + [pl.BlockSpec(memory_space=pl.ANY)] * len(extra),
        out_specs=(_SEM, _SEM, *[_HBM] * nb, pl.BlockSpec(memory_space=pltpu.VMEM)),
        input_output_aliases={i: 2 + i for i in range(nb)},
        compiler_params=pltpu.CompilerParams(has_side_effects=_EFFECT),
    )(*[pltpu.with_memory_space_constraint(a, pltpu.HBM) for a in bufs], *extra)
    return res[0], res[1], list(res[2:2 + nb]), res[-1]


def _split_wait(name, bufs, send_sems, recv_sems, after, copies_fn):
    nb = len(bufs)

    def body(*refs):
        for cp in copies_fn(refs[:nb], refs[nb], refs[nb + 1]):
            cp.wait_send()
            cp.wait_recv()

    res = pl.pallas_call(
        body, name=name, out_shape=tuple(pltpu.HBM(a.shape, a.dtype) for a in bufs),
        in_specs=[_HBM] * nb + [_SEM, _SEM, pl.BlockSpec(memory_space=pl.ANY)], out_specs=tuple([_HBM] * nb),
        input_output_aliases={i: i for i in range(nb)},
        compiler_params=pltpu.CompilerParams(has_side_effects=_EFFECT),
    )(*bufs, send_sems, recv_sems, after)
    return list(res)


def _split_relay(name, bufs, send_sems, recv_sems, after, wait_fn, n_sems, start_fn):
    nb = len(bufs)

    def body(*refs):
        for cp in wait_fn(refs[:nb], refs[nb], refs[nb + 1]):
            cp.wait_send()
            cp.wait_recv()
        for cp in start_fn(refs[:nb], refs[nb + 3], refs[nb + 4]):
            cp.start()
        refs[-1][...] = jnp.zeros_like(refs[-1])

    sems = pltpu.SemaphoreType.DMA((n_sems,))
    res = pl.pallas_call(
        body, name=name, out_shape=(sems, sems, *[pltpu.HBM(a.shape, a.dtype) for a in bufs], jax.ShapeDtypeStruct((8, LANES), F32)),
        in_specs=[_HBM] * nb + [_SEM, _SEM, pl.BlockSpec(memory_space=pl.ANY)],
        out_specs=(_SEM, _SEM, *[_HBM] * nb, pl.BlockSpec(memory_space=pltpu.VMEM)),
        input_output_aliases={i: 2 + i for i in range(nb)},
        compiler_params=pltpu.CompilerParams(has_side_effects=_EFFECT),
    )(*bufs, send_sems, recv_sems, after)
    return res[0], res[1], list(res[2:2 + nb]), res[-1]


N_CHIP = 4


def _chip_flip(x, y, k):
    return (1 - x if k & 2 else x), (1 - y if k & 1 else y)


def _gather_first_copies(n):
    def copies(bufs, send_sems, recv_sems):
        x, y, c, me = _position()
        out = []
        for w in range(n):
            for k in range(N_CHIP):
                to = (x, y, 1 - c) if k == 0 else (*_chip_flip(x, y, k), c)
                out.append(pltpu.make_async_remote_copy(
                    src_ref=bufs[w], dst_ref=bufs[n + w].at[me], send_sem=send_sems.at[w * N_CHIP + k],
                    recv_sem=recv_sems.at[w * N_CHIP + k], device_id=to, device_id_type=MESH))
        return out
    return copies


def _gather_relay_copies(n):
    def copies(bufs, send_sems, recv_sems):
        x, y, c, _ = _position()
        out = []
        for w in range(n):
            for k in range(1, N_CHIP):
                px, py = _chip_flip(x, y, k)
                blk = bufs[n + w].at[4 * px + 2 * py + c]
                out.append(pltpu.make_async_remote_copy(
                    src_ref=blk, dst_ref=blk, send_sem=send_sems.at[w * (N_CHIP - 1) + k - 1],
                    recv_sem=recv_sems.at[w * (N_CHIP - 1) + k - 1], device_id=(x, y, 1 - c), device_id_type=MESH))
        return out
    return copies


def _small_gather_copies(bufs, send_sems, recv_sems):
    x, y, c, me = _position()
    return [pltpu.make_async_remote_copy(src_ref=bufs[0], dst_ref=bufs[1].at[me], send_sem=send_sems.at[k - 1], recv_sem=recv_sems.at[k - 1],
                                         device_id=_flip(x, y, c, k), device_id_type=MESH) for k in range(1, N_DEV)]


def _xor(a, b):
    return a + b - 2 * a * b


def _forward_first_copies(n):
    def copies(bufs, send_sems, recv_sems):
        x, y, c, me = _position()
        out = []
        for w in range(n):
            for k, to in enumerate([(x, y, 1 - c), (1 - x, y, c), (x, 1 - y, c)]):
                out.append(pltpu.make_async_remote_copy(
                    src_ref=bufs[w], dst_ref=bufs[n + w].at[me], send_sem=send_sems.at[w * 3 + k],
                    recv_sem=recv_sems.at[w * 3 + k], device_id=to, device_id_type=MESH))
        return out
    return copies


def _forward_second_copies(n):
    def copies(bufs, send_sems, recv_sems):
        x, y, c, _ = _position()
        out = []
        for w in range(n):
            half = bufs[n + w].shape[1] // 2
            for k, (src_chip, rows, to) in enumerate([((1 - x, y), pl.ds(0, half), (x, 1 - y, c)), ((x, 1 - y), pl.ds(half, half), (1 - x, y, c))]):
                blk = bufs[n + w].at[4 * src_chip[0] + 2 * src_chip[1] + c, rows]
                out.append(pltpu.make_async_remote_copy(src_ref=blk, dst_ref=blk, send_sem=send_sems.at[w * 4 + k],
                                                        recv_sem=recv_sems.at[w * 4 + k], device_id=to, device_id_type=MESH))
            for k, (px, py) in enumerate([(1 - x, y), (x, 1 - y)]):
                blk = bufs[n + w].at[4 * px + 2 * py + c]
                out.append(pltpu.make_async_remote_copy(src_ref=blk, dst_ref=blk, send_sem=send_sems.at[w * 4 + 2 + k],
                                                        recv_sem=recv_sems.at[w * 4 + 2 + k], device_id=(x, y, 1 - c), device_id_type=MESH))
        return out
    return copies


def _forward_third_copies(n):
    def copies(bufs, send_sems, recv_sems):
        x, y, c, _ = _position()
        out = []
        for w in range(n):
            blk = bufs[n + w].at[4 * (1 - x) + 2 * (1 - y) + c]
            out.append(pltpu.make_async_remote_copy(src_ref=blk, dst_ref=blk, send_sem=send_sems.at[w], recv_sem=recv_sems.at[w],
                                                    device_id=(x, y, 1 - c), device_id_type=MESH))
        return out
    return copies


def _to_sibling_copies(n):
    def copies(bufs, send_sems, recv_sems):
        x, y, c, _ = _position()
        out = []
        for w in range(n):
            for q in range(N_CHIP):
                out.append(pltpu.make_async_remote_copy(
                    src_ref=bufs[w].at[2 * q + 1 - c], dst_ref=bufs[n + w].at[q], send_sem=send_sems.at[w * N_CHIP + q],
                    recv_sem=recv_sems.at[w * N_CHIP + q], device_id=(x, y, 1 - c), device_id_type=MESH))
        return out
    return copies


def _to_owner_copies(n):
    def copies(bufs, send_sems, recv_sems):
        x, y, c, _ = _position()
        out = []
        for w in range(n):
            for k in range(1, N_CHIP):
                px, py = (1 - x if k & 2 else x), (1 - y if k & 1 else y)
                out.append(pltpu.make_async_remote_copy(
                    src_ref=bufs[w].at[2 * px + py], dst_ref=bufs[n + w].at[k - 1], send_sem=send_sems.at[w * (N_CHIP - 1) + k - 1],
                    recv_sem=recv_sems.at[w * (N_CHIP - 1) + k - 1], device_id=(px, py, c), device_id_type=MESH))
        return out
    return copies


def _chip_sum(name, stack, landed, c_idx):
    _, R, C = stack.shape
    tr = _tile(R, max(16, 1048576 // C), 16)

    def body(c_ref, a_ref, b_ref, o_ref):
        o_ref[...] = (a_ref[...].astype(F32) + b_ref[...].astype(F32)).astype(o_ref.dtype)

    return pl.pallas_call(
        body, name=name,
        grid_spec=pltpu.PrefetchScalarGridSpec(
            num_scalar_prefetch=1, grid=(N_CHIP, R // tr),
            in_specs=[pl.BlockSpec((None, tr, C), lambda q, i, c_ref: (2 * q + c_ref[0], i, 0)),
                      pl.BlockSpec((None, tr, C), lambda q, i, c_ref: (q, i, 0))],
            out_specs=pl.BlockSpec((None, tr, C), lambda q, i, c_ref: (q, i, 0))),
        out_shape=jax.ShapeDtypeStruct((N_CHIP, R, C), stack.dtype),
        compiler_params=_cparams(("parallel", "parallel")))(c_idx, stack, landed)


def _ada_mod(c16, w):
    _, D = c16.shape
    n = w.shape[1]
    tk = _tile(D, 512)
    nk = D // tk

    def body(c_ref, w_ref, o_ref, ca_ref):
        @pl.when(pl.program_id(0) == 0)
        def _():
            o_ref[...] = jnp.zeros_like(o_ref)

        cv = c_ref[...]
        ca = cv * _sigmoid(cv)
        ca_ref[...] = ca
        o_ref[...] += _dot(ca, w_ref[...])

    return pl.pallas_call(
        body, name="ada_mod", grid=(nk,),
        in_specs=[pl.BlockSpec((16, tk), lambda k: (0, k)), pl.BlockSpec((tk, n), lambda k: (k, 0))],
        out_specs=[pl.BlockSpec((16, n), lambda k: (0, 0)), pl.BlockSpec((16, tk), lambda k: (0, k))],
        out_shape=[jax.ShapeDtypeStruct((16, n), F32), jax.ShapeDtypeStruct((16, D), F32)],
        compiler_params=_cparams(("arbitrary",)))(c16, w)


def _adam_math(w, g, m, v):
    m2 = ADAM_B1 * m + (1.0 - ADAM_B1) * g
    v2 = ADAM_B2 * v + (1.0 - ADAM_B2) * (g * g)
    m_hat = m2 / (1.0 - ADAM_B1 ** ADAM_STEP)
    v_hat = v2 / (1.0 - ADAM_B2 ** ADAM_STEP)
    delta = -ADAM_LR * (m_hat / (jnp.sqrt(v_hat) + ADAM_EPS) + ADAM_WD * w)
    return delta, m2, v2


def _adamw(name, w, m, v, parts, row0=0, into=None):
    R, C = w.shape
    Rp = parts[0].shape[1]
    tr = _tile(Rp, max(16, 393216 // C), 16)
    off = row0 // tr
    n_p = len(parts)
    held = [] if into is None else list(into)

    def body(*refs):
        w_ref, m_ref, v_ref = refs[:3]
        g_ref, d_ref, m2_ref, v2_ref = refs[3 + n_p + len(held):]
        g = None
        for p_ref in refs[3:3 + n_p]:
            for s in range(p_ref.shape[0]):
                t = p_ref[s].astype(F32)
                g = t if g is None else g + t
        delta, m2, v2 = _adam_math(w_ref[...], g, m_ref[...], v_ref[...])
        g_ref[...] = g
        d_ref[...] = delta
        m2_ref[...] = m2
        v2_ref[...] = v2

    blk = pl.BlockSpec((tr, C), lambda i: (i + off, 0))
    out = jax.ShapeDtypeStruct((R, C), F32)
    return pl.pallas_call(
        body, name=name, grid=(Rp // tr,),
        in_specs=[blk, blk, blk] + [pl.BlockSpec((a.shape[0], tr, C), lambda i: (0, i, 0)) for a in parts]
        + [pl.BlockSpec(memory_space=pl.ANY)] * len(held),
        out_specs=[blk] * 4, out_shape=[out] * 4, input_output_aliases={3 + n_p + i: i for i in range(len(held))},
        compiler_params=_cparams(("parallel",)))(w, m, v, *parts, *held)


def _small_update(gathered, w, m, v, after):
    _, R, L = gathered.shape
    rs = w.shape[0]

    def body(p_ref, w_ref, m_ref, v_ref, after_ref, g_ref, d_ref, m2_ref, v2_ref):
        g = p_ref[0]
        for p in range(1, N_DEV):
            g = g + p_ref[p]
        g_ref[...] = g
        delta, m2, v2 = _adam_math(w_ref[...], g[0:rs, :], m_ref[...], v_ref[...])
        d_ref[...] = delta
        m2_ref[...] = m2
        v2_ref[...] = v2

    vm = pl.BlockSpec(memory_space=pltpu.VMEM)
    sm = jax.ShapeDtypeStruct((rs, L), F32)
    return pl.pallas_call(body, name="small_update", in_specs=[vm] * 4 + [pl.BlockSpec(memory_space=pl.ANY)], out_specs=[vm] * 4,
                          out_shape=[jax.ShapeDtypeStruct((R, L), F32), sm, sm, sm],
                          compiler_params=pltpu.CompilerParams(vmem_limit_bytes=VMEM_LIMIT))(gathered, w, m, v, after)


class _Fetched(dict):
    def __init__(self, fetch):
        super().__init__()
        self.fetch = fetch

    def first(self, key, after):
        self[key] = self.fetch(key, after)
        return self[key]


def _local_step(x, tgt, mod, p, fetch, F, scatter=None):
    S, D = x.shape
    GW, HW = p["ln_g"].shape[1], p["hg_ng"].shape[1]
    G, T, _ = p["ws"].shape
    w = _Fetched(fetch)
    INW = 2 * GW + 4 * HW + 2 * D
    in_loc, br_loc, fi_loc = INW // N_DEV, D // N_DEV, 2 * F // N_DEV
    assert GW == HW and F % fi_loc == 0
    sh1, sc1, gt1, sh2, sc2, gt2 = (mod[:, k * D:(k + 1) * D] for k in range(6))
    bsb = jnp.broadcast_to(p["bs"][:, :, None], (G, T, GW // G))

    tm = _tile(S, 1024, 16)
    tmh = _tile(S, 512, 16)
    tn_in = _tile(in_loc, 1280)
    tn_d = _tile(D, 512)
    tn_br = _tile(br_loc, 512)
    tk_s = S
    tm_w = _tile(D, 1024)
    g_off = 2 * GW + 4 * HW

    h1 = _norm_mod("norm1", x, p["norm1_g"], sc1, sh1)
    z = _mm_nn_stacked("proj_in", h1, w.first("in", h1), tm=tm, tn=tn_in, tk=D)[0]
    ya = _gmlp_fwd(z, p["ln_g"], p["ln_b"], p["ws"], bsb, GW)
    yb, o_hg, states = _hg_fwd(z, p["hg_lb"], p["hg_ng"], HW)
    flat = {k: jnp.swapaxes(w.first(k, yb), 0, 1).reshape(GW, D) for k in ("bg", "bh")}
    tn_f = _tile(D, 1024)
    pa = _matmul(
        "branch_gmlp", ya, flat["bg"], dims=_NN, grid_mnk=(S // tm, D // tn_f, 1), tiles=(tm, tn_f),
        a_spec=pl.BlockSpec((tm, GW), lambda i, j, k: (i, 0)), b_spec=pl.BlockSpec((GW, tn_f), lambda i, j, k: (0, j)),
        out_shapes=[jax.ShapeDtypeStruct((S, D), F32)], out_specs=[pl.BlockSpec((tm, tn_f), lambda i, j, k: (i, j))], epilogue=_store(F32))[0]
    t_fi = w.first("fi_early", pa)

    def gates(ga_ref, gb_ref, ba_ref, bb_ref):
        return _sigmoid_abs(ga_ref[...] + ba_ref[...]), _sigmoid_abs(gb_ref[...] + bb_ref[...])

    def gate_specs(tn_, tm_=tm):
        o1, o2 = g_off // tn_, (g_off + D) // tn_
        return [pl.BlockSpec((tm_, tn_), lambda i, j, k: (i, o1 + j)), pl.BlockSpec((tm_, tn_), lambda i, j, k: (i, o2 + j)),
                pl.BlockSpec((1, tn_), lambda i, j, k: (0, j)), pl.BlockSpec((1, tn_), lambda i, j, k: (0, D // tn_ + j))]

    def merge_ep(acc, ex, outs):
        ga, gb = gates(*ex[1:5])
        outs[0][...] = acc
        outs[1][...] = (ga * ex[0][...] + gb * acc).astype(BF16)

    tile_o = pl.BlockSpec((tmh, tn_f), lambda i, j, k: (i, j))
    pb, y = _matmul(
        "branch_hg_merge", yb, flat["bh"], dims=_NN, grid_mnk=(S // tmh, D // tn_f, 1), tiles=(tmh, tn_f),
        a_spec=pl.BlockSpec((tmh, HW), lambda i, j, k: (i, 0)), b_spec=pl.BlockSpec((HW, tn_f), lambda i, j, k: (0, j)),
        extras=[pa, z, z, p["b_gate"], p["b_gate"]], extra_specs=[tile_o, *gate_specs(tn_f, tmh)],
        out_shapes=[jax.ShapeDtypeStruct((S, D), F32), jax.ShapeDtypeStruct((S, D), BF16)], out_specs=[tile_o, tile_o],
        epilogue=merge_ep, after=t_fi)

    def resid_ep(acc, ex, outs):
        outs[0][...] = acc
        outs[1][...] = ex[0][...] + ex[1][...] * acc

    def resid_mm(name, a, b, res, gt, tm_):
        K = a.shape[1]
        t_o = pl.BlockSpec((tm_, tn_d), lambda i, j, k: (i, j))
        return _matmul(
            name, a, b, dims=_NN, grid_mnk=(S // tm_, D // tn_d, 1), tiles=(tm_, tn_d),
            a_spec=pl.BlockSpec((tm_, K), lambda i, j, k: (i, 0)), b_spec=pl.BlockSpec((K, tn_d), lambda i, j, k: (0, j)),
            extras=[res, gt], extra_specs=[t_o, pl.BlockSpec((1, tn_d), lambda i, j, k: (0, j))],
            out_shapes=[jax.ShapeDtypeStruct((S, D), F32)] * 2, out_specs=[t_o, t_o], epilogue=resid_ep)

    o1, xm = resid_mm("proj_out", y, w.first("out", z), x, gt1, tm)
    h2 = _norm_mod("norm2", xm, p["norm2_g"], sc2, sh2)
    hf, hf_fac = _ffn_in_swiglu(h2, w.first("fi", h2))
    o2, x3 = resid_mm("ffn_out", hf, w.first("fo", hf), xm, gt2, tmh)
    dx3, do2, vec_l = _loss_head(x3, tgt, p["final_g"], o2, gt2)

    nf = F // fi_loc

    def dswiglu_ep(acc, ex, outs):
        outs[0][0] = (acc * ex[0][0].astype(F32)).astype(BF16)
        outs[0][1] = (acc * ex[0][1].astype(F32)).astype(BF16)

    pair = pl.BlockSpec((2, tmh, fi_loc), lambda i, j, k: (0, i, j))
    dab = _matmul(
        "ffn_out_dx", do2, w["fo"], dims=_NT, grid_mnk=(S // tmh, nf, 1), tiles=(tmh, fi_loc),
        a_spec=pl.BlockSpec((tmh, D), lambda i, j, k: (i, 0)), b_spec=pl.BlockSpec((fi_loc, D), lambda i, j, k: (j, 0)),
        extras=[hf_fac], extra_specs=[pair], out_shapes=[jax.ShapeDtypeStruct((2, S, F), BF16)], out_specs=[pair],
        epilogue=dswiglu_ep)[0]
    start = (lambda name, grads: scatter[0](name, grads)) if scatter is not None else (lambda name, grads: None)
    push = (lambda name, after: scatter[1](name, after)) if scatter is not None else (lambda name, after: None)

    def zero(token):
        return 0.0 if token is None else token[0:1, 0:1]

    tm_f = _tile(F, 512)
    g_fo = _mm_tn("ffn_out_dw", hf, do2, pl.BlockSpec((tk_s, D), lambda i, j, k: (k, j)), Mo=F, No=D, S=S, tm=tm_f, tn=D, tk=tk_s)
    g_fi = _mm_tn("ffn_in_dw", h2, dab, pl.BlockSpec((None, tk_s, fi_loc), lambda i, j, k: (j // nf, k, j % nf)),
                  Mo=D, No=2 * F, S=S, tm=tm_w, tn=fi_loc, tk=tk_s, stacked_nloc=fi_loc, after=g_fo)
    t_ffn = start("scatter_ffn", dict(fo=g_fo, fi=g_fi))
    dh2 = _mm_nt_stacked("ffn_in_dx", pl.BlockSpec((None, tm, fi_loc), lambda i, j, k: (k // nf, i, k % nf)), dab, w["fi"],
                         M=S, tm=tm, tn=tm_w, tk=fi_loc, after=t_ffn)
    dxm, vec2, do1 = _norm_mod_bwd("norm2_bwd", dh2, xm, p["norm2_g"], sc2, dx3, o1, gt1)
    t_ffn = push("scatter_ffn", dxm)

    def dmerge_ep(acc, ex, outs):
        ga, gb = gates(*ex[2:6])
        outs[0][...] = (acc * ga).astype(BF16)
        outs[1][...] = (acc * gb).astype(BF16)
        outs[2][0] = (acc * ex[0][...] * ga * (1.0 - ga)).astype(BF16)
        outs[2][1] = (acc * ex[1][...] * gb * (1.0 - gb)).astype(BF16)

    t_o = pl.BlockSpec((tm, tn_d), lambda i, j, k: (i, j))
    dpa, dpb, dg2 = _matmul(
        "proj_out_dx", do1, w["out"], dims=_NT, grid_mnk=(S // tm, D // tn_d, 1), tiles=(tm, tn_d),
        a_spec=pl.BlockSpec((tm, D), lambda i, j, k: (i, 0)), b_spec=pl.BlockSpec((tn_d, D), lambda i, j, k: (j, 0)),
        extras=[pa, pb, z, z, p["b_gate"], p["b_gate"]], extra_specs=[t_o, t_o, *gate_specs(tn_d)],
        out_shapes=[jax.ShapeDtypeStruct((S, D), BF16), jax.ShapeDtypeStruct((S, D), BF16), jax.ShapeDtypeStruct((2, S, D), BF16)],
        out_specs=[t_o, t_o, pl.BlockSpec((2, tm, tn_d), lambda i, j, k: (0, i, j))], epilogue=dmerge_ep, after=t_ffn)
    g_out = _mm_tn("proj_out_dw", y, do1, pl.BlockSpec((tk_s, D), lambda i, j, k: (k, j)), Mo=D, No=D, S=S, tm=tn_d, tn=D, tk=tk_s)
    tn_g = _tile(GW, 512)
    b_br = pl.BlockSpec((tk_s, br_loc), lambda i, j, k: (k, j))
    g_bg = _mm_tn("branch_gmlp_dw", ya, dpa, b_br, Mo=GW, No=D, S=S, tm=tn_g, tn=br_loc, tk=tk_s, stacked_nloc=br_loc)
    g_bh = _mm_tn("branch_hg_dw", yb, dpb, b_br, Mo=HW, No=D, S=S, tm=tn_g, tn=br_loc, tk=tk_s, stacked_nloc=br_loc)
    t_mix = start("scatter_mixer", dict(out=g_out, bg=g_bg, bh=g_bh))
    def branch_dx(name, dp, w_flat):
        return _matmul(
            name, dp, w_flat, dims=_NT, grid_mnk=(S // tm, GW // tn_g, 1), tiles=(tm, tn_g),
            a_spec=pl.BlockSpec((tm, D), lambda i, j, k: (i, 0)), b_spec=pl.BlockSpec((tn_g, D), lambda i, j, k: (j, 0)),
            out_shapes=[jax.ShapeDtypeStruct((S, GW), F32)], out_specs=[pl.BlockSpec((tm, tn_g), lambda i, j, k: (i, j))],
            epilogue=_store(F32), after=t_mix)[0]

    dya = branch_dx("branch_gmlp_dx", dpa, flat["bg"])
    dyb = branch_dx("branch_hg_dx", dpb, flat["bh"])
    db_gate = _colsum2(dg2)
    dz_gmlp, dln, dws, dbs = _gmlp_bwd(z, dya, p["ln_g"], p["ln_b"], p["ws"], bsb, GW)
    t_mix = push("scatter_mixer", dz_gmlp)
    dz, dng, dhlb = _hg_bwd(z, o_hg, states, dyb, p["hg_lb"], p["hg_ng"] + zero(t_mix), HW, dz_gmlp, dg2)
    half = D // 2
    tm_h = _tile(half, 1024)
    g_in = []
    t_in = None
    for hname, h in (("a", 0), ("b", 1)):
        g_in.append(_mm_tn("proj_in_dw_" + hname, h1, dz, pl.BlockSpec((tk_s, in_loc), lambda i, j, k: (k, j)), Mo=half, No=INW, S=S,
                           tm=tm_h, tn=in_loc, tk=tk_s, stacked_nloc=in_loc, after=t_in, a_off=h * (half // tm_h)))
        t_in = start("scatter_proj_in_" + hname, {"w_in_" + hname: g_in[-1]})
    t_in = push("scatter_proj_in_a", t_in)
    dh1 = _mm_nt_stacked("proj_in_dx", pl.BlockSpec((tm, in_loc), lambda i, j, k: (i, k)), dz, w["in"], M=S, tm=tm, tn=tm_w, tk=in_loc,
                         after=t_in)
    dx, vec1 = _norm_mod_bwd("norm1_bwd", dh1, x, p["norm1_g"], sc1, dxm)

    dmod = jnp.concatenate([vec1[0:1], vec1[1:2], vec2[3:4], vec2[0:1], vec2[1:2], vec_l[2:3]], axis=1)
    small = dict(norm1_g=vec1[2:3], b_gate=db_gate.reshape(1, 2 * D), ln_g=dln[0:1], ln_b=dln[1:2], ws=dws, bs=dbs.reshape(G, T),
                 hg_lb=dhlb, hg_ng=dng[0:1], norm2_g=vec2[2:3], final_g=vec_l[1:2], loss=vec_l[0:1, 0:LANES])
    big = dict(w_in_a=g_in[0], w_in_b=g_in[1], bg=g_bg, bh=g_bh, out=g_out, fi=g_fi, fo=g_fo)
    return dx, big, small, dmod


_SMALL = ("b_ada", "norm1_g", "b_gate", "ln_g", "ln_b", "ws", "bs", "hg_lb", "hg_ng", "norm2_g", "final_g")


def _pack(parts, rows_mult=8):
    flat = [a.reshape(-1) for a in parts]
    offs, n = [], 0
    for a in flat:
        offs.append(n)
        n += a.shape[0]
    pad = (-n) % (LANES * rows_mult)
    if pad:
        flat.append(jnp.zeros((pad,), F32))
    return jnp.concatenate(flat).reshape(-1, LANES), offs


def kernel(x, c, w_ada, b_ada, norm1_g, w_in, b_gate, gmlp_ln_g, gmlp_ln_b, gmlp_ws, gmlp_bs, hg_lb, hg_norm_g, w_branch_gmlp, w_branch_hg, w_out, norm2_g, w_ffn_in, w_ffn_out, final_norm_g, loss_target, m_w_ada, m_b_ada, m_norm1_g, m_w_in, m_b_gate, m_gmlp_ln_g, m_gmlp_ln_b, m_gmlp_ws, m_gmlp_bs, m_hg_lb, m_hg_norm_g, m_w_branch_gmlp, m_w_branch_hg, m_w_out, m_norm2_g, m_w_ffn_in, m_w_ffn_out, m_final_norm_g, v_w_ada, v_b_ada, v_norm1_g, v_w_in, v_b_gate, v_gmlp_ln_g, v_gmlp_ln_b, v_gmlp_ws, v_gmlp_bs, v_hg_lb, v_hg_norm_g, v_w_branch_gmlp, v_w_branch_hg, v_w_out, v_norm2_g, v_w_ffn_in, v_w_ffn_out, v_final_norm_g):
    S, D = x.shape[1], x.shape[2]
    ada_loc = w_ada.shape[2]
    me = 4 * lax.axis_index("x") + 2 * lax.axis_index("y") + lax.axis_index("c")

    c_all = _allgather_small("gather_c", c.reshape(D // LANES, LANES)).reshape(N_DEV, D)
    mod_cols, c_act = _ada_mod(jnp.pad(c_all, ((0, 16 - N_DEV), (0, 0))), w_ada[0])
    mod_all = _allgather_small("gather_mod", mod_cols[:N_DEV].reshape(-1, LANES)).reshape(N_DEV, N_DEV, ada_loc)
    mod = lax.dynamic_index_in_dim(mod_all, me, axis=1, keepdims=False).reshape(1, N_DEV * ada_loc) + b_ada

    def empty_hbm(shape, dtype):
        return pltpu.with_memory_space_constraint(lax.empty(shape, dtype), pltpu.HBM)

    groups = dict(gather_in=dict(keys=["in"], src=[w_in], forward=True),
                  gather_mixer=dict(keys=["bg", "bh", "out"], src=[w_branch_gmlp, w_branch_hg, w_out], forward=False),
                  gather_ffn_in=dict(keys=["fi"], src=[w_ffn_in], forward=True),
                  gather_ffn_out=dict(keys=["fo"], src=[w_ffn_out], forward=False))
    group_of = {k: gname for gname, g in groups.items() for k in g["keys"]}

    def first_hop(gname, after):
        g = groups[gname]
        n = len(g["keys"])
        shards = [a[0].astype(BF16) for a in g["src"]]
        outs = [lax.dynamic_update_slice(lax.empty((N_DEV, *s.shape), BF16), s[None], (me, 0, 0)) for s in shards]
        if g["forward"]:
            *g["hop"], token = _split_start(gname + "_hop1", shards + outs, n * 3, _forward_first_copies(n), after=after)
        else:
            *g["hop"], token = _split_start(gname + "_hop1", shards + outs, n * N_CHIP, _gather_first_copies(n), after=after)
        return token

    def second_hop(gname, after):
        g = groups[gname]
        n = len(g["keys"])
        *g["hop"], token = _split_relay(gname + "_hop2", g["hop"][2], g["hop"][0], g["hop"][1], after,
                                        _forward_first_copies(n), n * 4, _forward_second_copies(n))
        return token

    def finish(gname, after):
        g = groups[gname]
        n = len(g["keys"])
        send_sems, recv_sems, bufs = g["hop"]
        if g["forward"]:
            send_sems, recv_sems, bufs, _ = _split_relay(gname + "_hop3", bufs, send_sems, recv_sems, after,
                                                         _forward_second_copies(n), n, _forward_third_copies(n))
            bufs = _split_wait(gname + "_wait", bufs, send_sems, recv_sems, after, _forward_third_copies(n))
        else:
            send_sems, recv_sems, bufs, _ = _split_relay(gname + "_relay", bufs, send_sems, recv_sems, after,
                                                         _gather_first_copies(n), n * (N_CHIP - 1), _gather_relay_copies(n))
            bufs = _split_wait(gname + "_wait", bufs, send_sems, recv_sems, after, _gather_relay_copies(n))
        g["done"] = dict(zip(g["keys"], bufs[n:]))

    token = first_hop("gather_in", mod_all)
    mod = mod + token[0:1, 0:1]

    def fetch(key, after):
        if key == "in":
            t = second_hop("gather_in", after)
            t = first_hop("gather_mixer", t)
            t = first_hop("gather_ffn_in", t)
            finish("gather_in", t)
        elif key == "fi_early":
            return first_hop("gather_ffn_out", second_hop("gather_ffn_in", after))
        elif "done" not in groups[group_of[key]]:
            finish(group_of[key], after)
        arr = groups[group_of[key]]["done"][key]
        return arr.reshape(-1, D) if key in ("out", "fo") else arr

    p = dict(norm1_g=norm1_g, b_gate=b_gate, ln_g=gmlp_ln_g, ln_b=gmlp_ln_b, ws=gmlp_ws[0], bs=gmlp_bs[0], hg_lb=hg_lb,
             hg_ng=hg_norm_g, norm2_g=norm2_g, final_g=final_norm_g.reshape(1, D))

    in_flight = {}
    c_idx = lax.axis_index("c").astype(jnp.int32).reshape(1)
    my_chip = 2 * lax.axis_index("x") + lax.axis_index("y")

    def scatter_start(name, grads):
        keys = list(grads)
        n = len(keys)
        stacks = [grads[k].reshape(N_DEV, -1, grads[k].shape[-1]) for k in keys]
        lands = [empty_hbm((N_CHIP, *g.shape[1:]), g.dtype) for g in stacks]
        send_sems, recv_sems, bufs, token = _split_start(name + "_d2d", stacks + lands, n * N_CHIP, _to_sibling_copies(n))
        in_flight[name] = dict(keys=keys, stage1=(send_sems, recv_sems, bufs))
        return token

    def scatter_push(name, after):
        f = in_flight[name]
        n = len(f["keys"])
        send_sems, recv_sems, bufs = f["stage1"]
        bufs = _split_wait(name + "_d2d_wait", bufs, send_sems, recv_sems, after, _to_sibling_copies(n))
        sums = [_chip_sum(f"{name}_sum_{k}", bufs[i], bufs[n + i], c_idx) for i, k in enumerate(f["keys"])]
        lands = [empty_hbm((N_CHIP - 1, *s.shape[1:]), s.dtype) for s in sums]
        send_sems, recv_sems, bufs, token = _split_start(name + "_ici", sums + lands, n * (N_CHIP - 1), _to_owner_copies(n))
        f["stage2"] = (send_sems, recv_sems, bufs)
        return token

    grad_x, _, small, dmod = _local_step(x[0], loss_target[0], mod, p, fetch, w_ffn_out.shape[1] * N_DEV, (scatter_start, scatter_push))

    small["b_ada"] = dmod
    packed, offs = _pack([small[k] for k in _SMALL] + [small["loss"]])
    sg_send, sg_recv, sg_bufs, t_tail = _split_start(
        "gather_small_start", [packed, lax.dynamic_update_slice(lax.empty((N_DEV, *packed.shape), F32), packed[None], (me, 0, 0))],
        N_DEV - 1, _small_gather_copies)
    t_tail = scatter_push("scatter_proj_in_b", t_tail)
    big_w = dict(w_in=(w_in, m_w_in, v_w_in, "w_in"), bg=(w_branch_gmlp, m_w_branch_gmlp, v_w_branch_gmlp, "w_branch_gmlp"),
                 bh=(w_branch_hg, m_w_branch_hg, v_w_branch_hg, "w_branch_hg"), out=(w_out, m_w_out, v_w_out, "w_out"),
                 fi=(w_ffn_in, m_w_ffn_in, v_w_ffn_in, "w_ffn_in"), fo=(w_ffn_out, m_w_ffn_out, v_w_ffn_out, "w_ffn_out"))
    upd = {}

    def land_and_update(name, after):
        keys = in_flight[name]["keys"]
        n = len(keys)
        send_sems, recv_sems, bufs = in_flight[name]["stage2"]
        bufs = _split_wait(name + "_ici_wait", bufs, send_sems, recv_sems, after, _to_owner_copies(n))
        for i, k in enumerate(keys):
            parts = [lax.dynamic_index_in_dim(bufs[i], my_chip, axis=0, keepdims=True), bufs[n + i]]
            if k in big_w:
                wt, mt, vt, out_name = big_w[k]
                upd[out_name] = _adamw("adamw_" + out_name, wt[0], mt[0], vt[0], parts)
            else:
                wt, mt, vt, out_name = big_w["w_in"]
                upd[out_name] = _adamw("adamw_" + k, wt[0], mt[0], vt[0], parts, row0=0 if k == "w_in_a" else parts[0].shape[1],
                                       into=upd.get(out_name))
            after = upd[out_name][1]
        return after

    after = land_and_update("scatter_mixer", land_and_update("scatter_ffn", t_tail))
    gathered = _split_wait("gather_small_wait", sg_bufs, sg_send, sg_recv, after, _small_gather_copies)[1]
    wp = dict(p, b_ada=b_ada)
    ms = dict(b_ada=m_b_ada, norm1_g=m_norm1_g, b_gate=m_b_gate, ln_g=m_gmlp_ln_g, ln_b=m_gmlp_ln_b, ws=m_gmlp_ws, bs=m_gmlp_bs,
              hg_lb=m_hg_lb, hg_ng=m_hg_norm_g, norm2_g=m_norm2_g, final_g=m_final_norm_g)
    vs = dict(b_ada=v_b_ada, norm1_g=v_norm1_g, b_gate=v_b_gate, ln_g=v_gmlp_ln_g, ln_b=v_gmlp_ln_b, ws=v_gmlp_ws, bs=v_gmlp_bs,
              hg_lb=v_hg_lb, hg_ng=v_hg_norm_g, norm2_g=v_norm2_g, final_g=v_final_norm_g)
    w_sm, _ = _pack([wp[k] for k in _SMALL])
    m_sm, _ = _pack([ms[k] for k in _SMALL])
    v_sm, _ = _pack([vs[k] for k in _SMALL])
    sm_out = _small_update(gathered, w_sm, m_sm, v_sm, after)
    shapes = dict(b_ada=b_ada.shape, norm1_g=norm1_g.shape, b_gate=b_gate.shape, ln_g=gmlp_ln_g.shape, ln_b=gmlp_ln_b.shape,
                  ws=gmlp_ws.shape, bs=gmlp_bs.shape, hg_lb=hg_lb.shape, hg_ng=hg_norm_g.shape, norm2_g=norm2_g.shape,
                  final_g=final_norm_g.shape)

    def unpack(arr, k):
        i = _SMALL.index(k)
        n = math.prod(shapes[k])
        return arr.reshape(-1)[offs[i]:offs[i] + n].reshape(shapes[k])

    loss = sm_out[0].reshape(-1)[offs[len(_SMALL)]]

    dmod_all = gathered.reshape(N_DEV, -1)[:, offs[0]:offs[0] + N_DEV * ada_loc]
    dmod_loc = lax.dynamic_slice_in_dim(dmod_all, me * ada_loc, ada_loc, axis=1)
    ca_t = jnp.pad(c_act[:N_DEV].T, ((0, 0), (0, LANES - N_DEV))).astype(BF16)
    dm_p = jnp.pad(dmod_loc, ((0, LANES - N_DEV), (0, 0))).astype(BF16)
    tm_a = _tile(D, 512)
    g_ada = _matmul(
        "ada_dw", ca_t, dm_p, dims=_NN, grid_mnk=(D // tm_a, 1, 1), tiles=(tm_a, ada_loc),
        a_spec=pl.BlockSpec((tm_a, LANES), lambda i, j, k: (i, 0)), b_spec=pl.BlockSpec((LANES, ada_loc), lambda i, j, k: (0, 0)),
        out_shapes=[jax.ShapeDtypeStruct((1, D, ada_loc), F32)], out_specs=[pl.BlockSpec((None, tm_a, ada_loc), lambda i, j, k: (0, i, 0))],
        epilogue=_store(F32))[0]
    upd["w_ada"] = _adamw("adamw_w_ada", w_ada[0], m_w_ada[0], v_w_ada[0], [g_ada])
    land_and_update("scatter_proj_in_b", land_and_update("scatter_proj_in_a", upd["w_ada"][1]))

    order = ("w_ada", "b_ada", "norm1_g", "w_in", "b_gate", "ln_g", "ln_b", "ws", "bs", "hg_lb", "hg_ng", "w_branch_gmlp", "w_branch_hg",
             "w_out", "norm2_g", "w_ffn_in", "w_ffn_out", "final_g")
    outs = [loss, grad_x[None]]
    for idx in range(4):
        for k in order:
            outs.append(upd[k][idx][None] if k in upd else unpack(sm_out[idx], k))
    return tuple(outs)
```

```python
import functools
import math

import jax
import jax.numpy as jnp
from jax import lax
from jax.experimental import pallas as pl
from jax.experimental.pallas import tpu as pltpu

F32 = jnp.float32
BF16 = jnp.bfloat16
N_DEV = 8
EPS = 1e-6
LANES = 128
HG_DK = 128
HG_CHUNK = 64
HG_MID = HG_CHUNK // 2 - 1
EXP_CLAMP = 80.0
VMEM_LIMIT = 48 * 1024 * 1024
ADAM_LR, ADAM_B1, ADAM_B2, ADAM_EPS, ADAM_WD, ADAM_STEP = 0.001, 0.9, 0.999, 1e-08, 0.01, 10
MESH = pl.DeviceIdType.MESH

_NN = (((1,), (0,)), ((), ()))
_NT = (((1,), (1,)), ((), ()))
_TN = (((0,), (0,)), ((), ()))


def _dot(a, b, dims=_NN):
    return lax.dot_general(a.astype(BF16), b.astype(BF16), dims, preferred_element_type=F32)


def _tile(n, target, mult=LANES):
    best = None
    for t in range(mult, min(n, target) + 1, mult):
        if n % t == 0:
            best = t
    return n if best is None else best


def _cparams(sem):
    return pltpu.CompilerParams(dimension_semantics=sem, vmem_limit_bytes=VMEM_LIMIT)


def _sigmoid(x):
    return 1.0 / (1.0 + jnp.exp(-x))


def _gelu_parts(x):
    k0 = math.sqrt(2.0 / math.pi)
    x2 = x * x
    t = jnp.tanh(k0 * (x + 0.044715 * x * x2))
    g = 0.5 * x * (1.0 + t)
    dg = 0.5 * (1.0 + t) + 0.5 * x * (1.0 - t * t) * (k0 * (1.0 + 3.0 * 0.044715 * x2))
    return g, dg


def _split3(x):
    h = x.astype(BF16)
    r = x - h.astype(F32)
    m = r.astype(BF16)
    lo = (r - m.astype(F32)).astype(BF16)
    return h, m, lo


def _ones_dot(mat01, x):
    h, m, lo = _split3(x)
    d = functools.partial(lax.dot_general, dimension_numbers=_NN, preferred_element_type=F32)
    return d(mat01, h) + d(mat01, m) + d(mat01, lo)


def _matmul(name, a, b, *, dims, grid_mnk, tiles, a_spec, b_spec, extras=(), extra_specs=(), out_shapes, out_specs, epilogue, after=None,
            sem=None):
    gm, gn, nk = grid_mnk
    tm, tn = tiles
    n_ex, n_out = len(extras), len(out_shapes)
    held = [] if after is None else [after]

    def body(*refs):
        a_ref, b_ref = refs[0], refs[1]
        ex = refs[2:2 + n_ex]
        outs = refs[2 + n_ex + len(held):2 + n_ex + len(held) + n_out]
        more = () if sem is None else (pl.program_id(0) == 0,)
        if nk == 1:
            epilogue(lax.dot_general(a_ref[...], b_ref[...], dims, preferred_element_type=F32), ex, outs, *more)
            return
        acc = refs[-1]
        k = pl.program_id(2)

        @pl.when(k == 0)
        def _():
            acc[...] = jnp.zeros_like(acc)

        acc[...] += lax.dot_general(a_ref[...], b_ref[...], dims, preferred_element_type=F32)

        @pl.when(k == nk - 1)
        def _():
            epilogue(acc[...], ex, outs, *more)

    return pl.pallas_call(
        body, name=name, grid=(gm, gn, nk), in_specs=[a_spec, b_spec, *extra_specs] + [pl.BlockSpec(memory_space=pl.ANY)] * len(held),
        out_specs=list(out_specs), out_shape=list(out_shapes), scratch_shapes=[] if nk == 1 else [pltpu.VMEM((tm, tn), F32)],
        compiler_params=_cparams(sem or ("parallel", "parallel", "arbitrary")),
    )(a, b, *extras, *held)


def _store(dtype):
    def ep(acc, ex, outs):
        outs[0][...] = acc.astype(dtype)
    return ep


def _mm_nn_stacked(name, a, wg, *, tm, tn, tk, out_dtype=F32, extras=(), extra_specs=(), out_shapes=None, out_specs=None, epilogue=None,
                   after=None):
    M, K = a.shape
    _, _, nloc = wg.shape
    N = nloc * N_DEV
    q = nloc // tn
    if out_shapes is None:
        out_shapes = [jax.ShapeDtypeStruct((M, N), out_dtype)]
        out_specs = [pl.BlockSpec((tm, tn), lambda i, j, k: (i, j))]
        epilogue = _store(out_dtype)
    return _matmul(
        name, a, wg, dims=_NN, grid_mnk=(M // tm, N // tn, K // tk), tiles=(tm, tn),
        a_spec=pl.BlockSpec((tm, tk), lambda i, j, k: (i, k)),
        b_spec=pl.BlockSpec((None, tk, tn), lambda i, j, k: (j // q, k, j % q)),
        extras=extras, extra_specs=extra_specs, out_shapes=out_shapes, out_specs=out_specs, epilogue=epilogue, after=after)


def _mm_nt_stacked(name, a_spec, a, wg, *, M, tm, tn, tk, out_dtype=F32, after=None, extras=(), extra_specs=(), out_shapes=None,
                   out_specs=None, epilogue=None, sem=None):
    _, Kw, nloc = wg.shape
    q = nloc // tk
    single = out_shapes is None
    if single:
        out_shapes = [jax.ShapeDtypeStruct((M, Kw), out_dtype)]
        out_specs = [pl.BlockSpec((tm, tn), lambda i, j, k: (i, j))]
        epilogue = _store(out_dtype)
    res = _matmul(
        name, a, wg, dims=_NT, grid_mnk=(M // tm, Kw // tn, (nloc * N_DEV) // tk), tiles=(tm, tn),
        a_spec=a_spec, b_spec=pl.BlockSpec((None, tn, tk), lambda i, j, k: (k // q, j, k % q)),
        extras=extras, extra_specs=extra_specs, out_shapes=out_shapes, out_specs=out_specs, epilogue=epilogue, after=after, sem=sem)
    return res[0] if single else res


def _mm_tn(name, a, b, b_spec, *, Mo, No, S, tm, tn, tk, stacked_nloc=None, after=None, a_off=0):
    if stacked_nloc is None:
        out_shape = jax.ShapeDtypeStruct((Mo, No), BF16)
        out_spec = pl.BlockSpec((tm, tn), lambda i, j, k: (i, j))
    else:
        q = stacked_nloc // tn
        out_shape = jax.ShapeDtypeStruct((N_DEV, Mo, stacked_nloc), BF16)
        out_spec = pl.BlockSpec((None, tm, tn), lambda i, j, k: (j // q, i, j % q))
    return _matmul(
        name, a, b, dims=_TN, grid_mnk=(Mo // tm, No // tn, S // tk), tiles=(tm, tn),
        a_spec=pl.BlockSpec((tk, tm), lambda i, j, k: (k, i + a_off)), b_spec=b_spec,
        out_shapes=[out_shape], out_specs=[out_spec], epilogue=_store(BF16), after=after)[0]


def _norm_mod(name, x, g, sc, sh):
    S, D = x.shape
    tm = _tile(S, 256, 8)

    def body(x_ref, g_ref, sc_ref, sh_ref, h_ref):
        xv = x_ref[...]
        r = lax.rsqrt(jnp.mean(xv * xv, axis=-1, keepdims=True) + EPS)
        h = (xv * r) * g_ref[...]
        h_ref[...] = (h * (1.0 + sc_ref[...]) + sh_ref[...]).astype(BF16)

    row = pl.BlockSpec((tm, D), lambda i: (i, 0))
    vec = pl.BlockSpec((1, D), lambda i: (0, 0))
    return pl.pallas_call(body, name=name, grid=(S // tm,), in_specs=[row, vec, vec, vec], out_specs=row,
                          out_shape=jax.ShapeDtypeStruct((S, D), BF16), compiler_params=_cparams(("parallel",)))(x, g, sc, sh)


def _norm_mod_bwd_rows(first, dh_v, x_ref, g_ref, sc_ref, dres_ref, dx_ref, vec_ref, o_ref=None, gt_ref=None, do_ref=None):
    @pl.when(first)
    def _():
        vec_ref[...] = jnp.zeros_like(vec_ref)

    xv, gv = x_ref[...], g_ref[...]
    r = lax.rsqrt(jnp.mean(xv * xv, axis=-1, keepdims=True) + EPS)
    xn = xv * r
    one_sc = 1.0 + sc_ref[...]
    vec_ref[0:1, :] += jnp.sum(dh_v, axis=0, keepdims=True)
    vec_ref[1:2, :] += jnp.sum(dh_v * (xn * gv), axis=0, keepdims=True)
    vec_ref[2:3, :] += jnp.sum(dh_v * one_sc * xn, axis=0, keepdims=True)
    dxn = dh_v * one_sc * gv
    dx = dres_ref[...] + r * (dxn - xn * jnp.mean(dxn * xn, axis=-1, keepdims=True))
    dx_ref[...] = dx
    if o_ref is not None:
        vec_ref[3:4, :] += jnp.sum(dx * o_ref[...], axis=0, keepdims=True)
        do_ref[...] = (dx * gt_ref[...]).astype(BF16)


def _norm_mod_bwd(name, dh, x, g, sc, dres, o=None, gt=None):
    S, D = x.shape
    tm = _tile(S, 256, 8)
    gated = o is not None

    def body(*refs):
        if gated:
            dh_ref, x_ref, g_ref, sc_ref, dres_ref, o_ref, gt_ref, dx_ref, vec_ref, do_ref = refs
        else:
            dh_ref, x_ref, g_ref, sc_ref, dres_ref, dx_ref, vec_ref = refs
            o_ref = gt_ref = do_ref = None
        _norm_mod_bwd_rows(pl.program_id(0) == 0, dh_ref[...], x_ref, g_ref, sc_ref, dres_ref, dx_ref, vec_ref, o_ref, gt_ref, do_ref)

    row = pl.BlockSpec((tm, D), lambda i: (i, 0))
    vec = pl.BlockSpec((1, D), lambda i: (0, 0))
    acc = pl.BlockSpec((8, D), lambda i: (0, 0))
    ins = [dh, x, g, sc, dres] + ([o, gt] if gated else [])
    in_specs = [row, row, vec, vec, row] + ([row, vec] if gated else [])
    out_shape = [jax.ShapeDtypeStruct((S, D), F32), jax.ShapeDtypeStruct((8, D), F32)]
    out_specs = [row, acc]
    if gated:
        out_shape.append(jax.ShapeDtypeStruct((S, D), BF16))
        out_specs.append(row)
    return pl.pallas_call(body, name=name, grid=(S // tm,), in_specs=in_specs, out_specs=out_specs, out_shape=out_shape,
                          compiler_params=_cparams(("arbitrary",)))(*ins)


def _loss_head(x3, tgt, gf, o2, gt2):
    S, D = x3.shape
    tm = _tile(S, 256, 8)

    def body(x_ref, t_ref, g_ref, o_ref, gt_ref, dx_ref, do_ref, vec_ref):
        i = pl.program_id(0)

        @pl.when(i == 0)
        def _():
            vec_ref[...] = jnp.zeros_like(vec_ref)

        xv, gv = x_ref[...], g_ref[...]
        r = lax.rsqrt(jnp.mean(xv * xv, axis=-1, keepdims=True) + EPS)
        xn = xv * r
        e = xn * gv - t_ref[...]
        tok = 0.5 * jnp.mean(e * e, axis=-1, keepdims=True)
        vec_ref[0:1, :] += jnp.broadcast_to(jnp.sum(tok, axis=0, keepdims=True), (1, D))
        dy = e * (1.0 / D)
        vec_ref[1:2, :] += jnp.sum(dy * xn, axis=0, keepdims=True)
        dxn = dy * gv
        dx = r * (dxn - xn * jnp.mean(dxn * xn, axis=-1, keepdims=True))
        dx_ref[...] = dx
        vec_ref[2:3, :] += jnp.sum(dx * o_ref[...], axis=0, keepdims=True)
        do_ref[...] = (dx * gt_ref[...]).astype(BF16)

    row = pl.BlockSpec((tm, D), lambda i: (i, 0))
    vec = pl.BlockSpec((1, D), lambda i: (0, 0))
    return pl.pallas_call(
        body, name="loss_head", grid=(S // tm,), in_specs=[row, row, vec, row, vec],
        out_specs=[row, row, pl.BlockSpec((8, D), lambda i: (0, 0))],
        out_shape=[jax.ShapeDtypeStruct((S, D), F32), jax.ShapeDtypeStruct((S, D), BF16), jax.ShapeDtypeStruct((8, D), F32)],
        compiler_params=_cparams(("arbitrary",)))(x3, tgt, gf, o2, gt2)


def _ffn_in_swiglu(h, wg):
    S, D = h.shape
    _, _, tf = wg.shape
    nf = N_DEV // 2
    F = nf * tf
    tm = _tile(S, 256, 16)

    def body(h_ref, wa_ref, wu_ref, hf_ref, fac_ref):
        hv = h_ref[...]
        a = lax.dot_general(hv, wa_ref[...], _NN, preferred_element_type=F32)
        up = lax.dot_general(hv, wu_ref[...], _NN, preferred_element_type=F32)
        sa = _sigmoid(a)
        silu = a * sa
        hf_ref[...] = (silu * up).astype(BF16)
        fac_ref[0] = (up * (sa * (1.0 + a * (1.0 - sa)))).astype(BF16)
        fac_ref[1] = silu.astype(BF16)

    return pl.pallas_call(
        body, name="ffn_in_swiglu", grid=(nf, S // tm),
        in_specs=[pl.BlockSpec((tm, D), lambda j, i: (i, 0)), pl.BlockSpec((None, D, tf), lambda j, i: (j, 0, 0)),
                  pl.BlockSpec((None, D, tf), lambda j, i: (j + nf, 0, 0))],
        out_specs=[pl.BlockSpec((tm, tf), lambda j, i: (i, j)), pl.BlockSpec((2, tm, tf), lambda j, i: (0, i, j))],
        out_shape=[jax.ShapeDtypeStruct((S, F), BF16), jax.ShapeDtypeStruct((2, S, F), BF16)],
        compiler_params=_cparams(("parallel", "parallel")))(h, wg, wg)


def _colsum2(dg2):
    _, S, D = dg2.shape
    tm = _tile(S, 256, 16)

    def body(x_ref, o_ref):
        @pl.when(pl.program_id(0) == 0)
        def _():
            o_ref[...] = jnp.zeros_like(o_ref)

        o_ref[0:1, :] += jnp.sum(x_ref[0].astype(F32), axis=0, keepdims=True)
        o_ref[1:2, :] += jnp.sum(x_ref[1].astype(F32), axis=0, keepdims=True)

    return pl.pallas_call(body, name="gate_bias_grad", grid=(S // tm,), in_specs=[pl.BlockSpec((2, tm, D), lambda i: (0, i, 0))],
                          out_specs=pl.BlockSpec((2, D), lambda i: (0, 0)), out_shape=jax.ShapeDtypeStruct((2, D), F32),
                          compiler_params=_cparams(("arbitrary",)))(dg2)


def _gmlp_common(u_ref, v_ref, lg_ref, lb_ref, ws_ref, bsb_ref, G, T, Dg):
    ug, dug = _gelu_parts(u_ref[...])
    vg, dvg = _gelu_parts(v_ref[...])
    mu = jnp.mean(vg, axis=-1, keepdims=True)
    vc = vg - mu
    rstd = lax.rsqrt(jnp.mean(vc * vc, axis=-1, keepdims=True) + EPS)
    vhat = vc * rstd
    vn = vhat * lg_ref[...] + lb_ref[...]
    row = lax.broadcasted_iota(jnp.int32, (T, T), 0)
    col = lax.broadcasted_iota(jnp.int32, (T, T), 1)
    tril = row >= col
    s = []
    for g in range(G):
        w = jnp.where(tril, ws_ref[g], 0.0)
        s.append(_dot(w, vn[:, g * Dg:(g + 1) * Dg]) + bsb_ref[g])
    return ug, dug, dvg, rstd, vhat, vn, tril, s


def _gmlp_fwd(z, ln_g, ln_b, ws, bsb, GW):
    S = z.shape[0]
    G, T, _ = ws.shape
    Dg = GW // G

    def body(u_ref, v_ref, lg_ref, lb_ref, ws_ref, bsb_ref, ya_ref):
        ug, _, _, _, _, _, _, s = _gmlp_common(u_ref, v_ref, lg_ref, lb_ref, ws_ref, bsb_ref, G, T, Dg)
        for g in range(G):
            sl = slice(g * Dg, (g + 1) * Dg)
            ya_ref[:, sl] = (ug[:, sl] * s[g]).astype(BF16)

    vec = pl.BlockSpec((1, GW), lambda c: (0, 0))
    return pl.pallas_call(
        body, name="gmlp_fwd", grid=(S // T,),
        in_specs=[pl.BlockSpec((T, GW), lambda c: (c, 0)), pl.BlockSpec((T, GW), lambda c: (c, 1)), vec, vec,
                  pl.BlockSpec((G, T, T), lambda c: (0, 0, 0)), pl.BlockSpec((G, T, Dg), lambda c: (0, 0, 0))],
        out_specs=pl.BlockSpec((T, GW), lambda c: (c, 0)), out_shape=jax.ShapeDtypeStruct((S, GW), BF16),
        compiler_params=_cparams(("parallel",)))(z, z, ln_g, ln_b, ws, bsb)


def _gmlp_bwd(z, dya, ln_g, ln_b, ws, bsb, GW):
    S = z.shape[0]
    G, T, _ = ws.shape
    Dg = GW // G
    nc = S // T

    def body(u_ref, v_ref, dya_ref, lg_ref, lb_ref, ws_ref, bsb_ref, dz_ref, dln_ref, dws_ref, dbs_ref, dbs_acc, dvh):
        c = pl.program_id(0)

        @pl.when(c == 0)
        def _():
            dln_ref[...] = jnp.zeros_like(dln_ref)
            dws_ref[...] = jnp.zeros_like(dws_ref)
            dbs_acc[...] = jnp.zeros_like(dbs_acc)

        ug, dug, dvg, rstd, vhat, vn, tril, s = _gmlp_common(u_ref, v_ref, lg_ref, lb_ref, ws_ref, bsb_ref, G, T, Dg)
        dya_v = dya_ref[...]
        for g in range(G):
            sl = slice(g * Dg, (g + 1) * Dg)
            dy_g = dya_v[:, sl]
            dz_ref[:, sl] = (dy_g * s[g] * dug[:, sl]).astype(BF16)
            ds = dy_g * ug[:, sl]
            dbs_acc[g] += ds
            w = jnp.where(tril, ws_ref[g], 0.0)
            dvn_g = _dot(w, ds, _TN)
            dws_ref[g] += jnp.where(tril, _dot(ds, vn[:, sl], _NT), 0.0)
            dln_ref[0:1, sl] += jnp.sum(dvn_g * vhat[:, sl], axis=0, keepdims=True)
            dln_ref[1:2, sl] += jnp.sum(dvn_g, axis=0, keepdims=True)
            dvh[:, sl] = dvn_g * lg_ref[:, sl]
        dvhat = dvh[...]
        m1 = jnp.mean(dvhat, axis=-1, keepdims=True)
        m2 = jnp.mean(dvhat * vhat, axis=-1, keepdims=True)
        dz_ref[:, GW:2 * GW] = (rstd * (dvhat - m1 - vhat * m2) * dvg).astype(BF16)

        @pl.when(c == nc - 1)
        def _():
            for g in range(G):
                dbs_ref[g] = jnp.sum(dbs_acc[g], axis=-1, keepdims=True)

    vec = pl.BlockSpec((1, GW), lambda c: (0, 0))
    return pl.pallas_call(
        body, name="gmlp_bwd", grid=(nc,),
        in_specs=[pl.BlockSpec((T, GW), lambda c: (c, 0)), pl.BlockSpec((T, GW), lambda c: (c, 1)),
                  pl.BlockSpec((T, GW), lambda c: (c, 0)), vec, vec,
                  pl.BlockSpec((G, T, T), lambda c: (0, 0, 0)), pl.BlockSpec((G, T, Dg), lambda c: (0, 0, 0))],
        out_specs=[pl.BlockSpec((T, 2 * GW), lambda c: (c, 0)), pl.BlockSpec((8, GW), lambda c: (0, 0)),
                   pl.BlockSpec((G, T, T), lambda c: (0, 0, 0)), pl.BlockSpec((G, T, 1), lambda c: (0, 0, 0))],
        out_shape=[jax.ShapeDtypeStruct((S, 2 * GW), BF16), jax.ShapeDtypeStruct((8, GW), F32),
                   jax.ShapeDtypeStruct((G, T, T), F32), jax.ShapeDtypeStruct((G, T, 1), F32)],
        scratch_shapes=[pltpu.VMEM((G, T, Dg), F32), pltpu.VMEM((T, GW), F32)],
        compiler_params=_cparams(("arbitrary",)))(z, z, dya, ln_g, ln_b, ws, bsb)


def _hg_common(q_ref, f_ref, hlb_ref):
    C = HG_CHUNK
    a = hlb_ref[...]
    lb = _sigmoid(a[0:1, :] - a[1:2, :])
    sig = _sigmoid(f_ref[...])
    f = lb + (1.0 - lb) * sig
    lf = jnp.log(f)
    kk = 1.0 - f
    q = q_ref[...]
    sq = _sigmoid(q)
    qa = q * sq
    row = lax.broadcasted_iota(jnp.int32, (C, C), 0)
    col = lax.broadcasted_iota(jnp.int32, (C, C), 1)
    tril = row >= col
    b = _ones_dot(tril.astype(BF16), lf)
    bm = b[HG_MID:HG_MID + 1, :]
    bl = b[C - 1:C, :]
    e_b = jnp.exp(b)
    e_qm = jnp.exp(jnp.minimum(b - bm, EXP_CLAMP))
    e_km = jnp.exp(jnp.minimum(bm - b, EXP_CLAMP))
    e_kl = jnp.exp(bl - b)
    return dict(lb=lb, sig=sig, f=f, kk=kk, q=q, sq=sq, qa=qa, tril=tril, e_b=e_b, e_qm=e_qm, e_km=e_km, e_kl=e_kl,
                e_l=jnp.exp(bl), qh=qa * e_b, qt=qa * e_qm, kt=kk * e_km, kh=kk * e_kl)


def _hg_fwd(z, hg_lb, ng, HW):
    S = z.shape[0]
    C, H, dk = HG_CHUNK, HW // HG_DK, HG_DK
    nc = S // C

    def body(q_ref, f_ref, i_ref, og_ref, hlb_ref, ng_ref, yb_ref, o_ref, st_ref, state):
        @pl.when(pl.program_id(0) == 0)
        def _():
            state[...] = jnp.zeros_like(state)

        t = _hg_common(q_ref, f_ref, hlb_ref)
        iv = i_ref[...]
        for h in range(H):
            sl = slice(h * dk, (h + 1) * dk)
            st = state[h]
            st_ref[h] = st
            a = jnp.where(t["tril"], _dot(t["qt"][:, sl], t["kt"][:, sl], _NT), 0.0)
            o_h = _dot(a, iv[:, sl]) + _dot(t["qh"][:, sl], st, _NT)
            state[h] = st * t["e_l"][:, sl] + _dot(iv[:, sl], t["kh"][:, sl], _TN)
            o_ref[:, sl] = o_h
            rr = lax.rsqrt(jnp.mean(o_h * o_h, axis=-1, keepdims=True) + EPS)
            og = og_ref[:, sl]
            yb_ref[:, sl] = (o_h * rr * ng_ref[:, sl] * (og * _sigmoid(og))).astype(BF16)

    def col(k):
        return pl.BlockSpec((C, HW), lambda c: (c, k))

    base = 2
    return pl.pallas_call(
        body, name="hgrn_fwd", grid=(nc,),
        in_specs=[col(base), col(base + 1), col(base + 2), col(base + 3),
                  pl.BlockSpec((2, HW), lambda c: (0, 0)), pl.BlockSpec((1, HW), lambda c: (0, 0))],
        out_specs=[pl.BlockSpec((C, HW), lambda c: (c, 0)), pl.BlockSpec((C, HW), lambda c: (c, 0)),
                   pl.BlockSpec((None, H, dk, dk), lambda c: (c, 0, 0, 0))],
        out_shape=[jax.ShapeDtypeStruct((S, HW), BF16), jax.ShapeDtypeStruct((S, HW), F32),
                   jax.ShapeDtypeStruct((nc, H, dk, dk), F32)],
        scratch_shapes=[pltpu.VMEM((H, dk, dk), F32)],
        compiler_params=_cparams(("arbitrary",)))(z, z, z, z, hg_lb, ng)


def _hg_bwd(z, o, states, dyb, hg_lb, ng, HW, dz_head, dz_tail):
    S = z.shape[0]
    C, H, dk = HG_CHUNK, HW // HG_DK, HG_DK
    nc = S // C
    B0 = dz_head.shape[1]
    DT = dz_tail.shape[2]
    INW = B0 + 4 * HW + 2 * DT

    def body(q_ref, f_ref, i_ref, og_ref, o_ref, st_ref, stn_ref, dyb_ref, hlb_ref, ng_ref, head_ref, tail_ref,
             dzf_ref, dng_ref, dhlb_ref, dstate, cross, dqa_buf, dkk_buf, db_buf, dlb_acc):
        c = pl.program_id(0)
        dzf_ref[:, 0:B0] = head_ref[...]
        dzf_ref[:, B0 + 4 * HW:B0 + 4 * HW + DT] = tail_ref[0]
        dzf_ref[:, B0 + 4 * HW + DT:INW] = tail_ref[1]
        dz_ref = dzf_ref.at[:, B0:B0 + 4 * HW]

        @pl.when(c == 0)
        def _():
            dstate[...] = jnp.zeros_like(dstate)
            dlb_acc[...] = jnp.zeros_like(dlb_acc)
            dng_ref[...] = jnp.zeros_like(dng_ref)

        def r16(v):
            return v.astype(BF16).astype(F32)

        t = _hg_common(q_ref, f_ref, hlb_ref)
        iv = i_ref[...]
        for h in range(H):
            sl = slice(h * dk, (h + 1) * dk)
            o_h, og, dyb_h, ng_h = o_ref[:, sl], og_ref[:, sl], dyb_ref[:, sl], ng_ref[:, sl]
            sg = _sigmoid(og)
            silu_og = og * sg
            rr = lax.rsqrt(jnp.mean(o_h * o_h, axis=-1, keepdims=True) + EPS)
            on = o_h * rr
            dng_ref[0:1, sl] += jnp.sum(dyb_h * on * silu_og, axis=0, keepdims=True)
            dz_ref[:, 3 * HW + h * dk:3 * HW + (h + 1) * dk] = (dyb_h * on * ng_h * (sg * (1.0 + og * (1.0 - sg)))).astype(BF16)
            don = dyb_h * ng_h * silu_og
            do_h = rr * (don - on * jnp.mean(don * on, axis=-1, keepdims=True))

            qt, kt, qh, kh, iv_h = t["qt"][:, sl], t["kt"][:, sl], t["qh"][:, sl], t["kh"][:, sl], iv[:, sl]
            a = jnp.where(t["tril"], _dot(qt, kt, _NT), 0.0)
            da = jnp.where(t["tril"], _dot(do_h, iv_h, _NT), 0.0)
            st, dst = st_ref[h], dstate[h]
            cross[:, sl] = jnp.sum(dst * stn_ref[h], axis=0, keepdims=True)
            dqh = _dot(do_h, st)
            dstate[h] = _dot(do_h, qh, _TN) + dst * t["e_l"][:, sl]
            div = _dot(a, do_h, _TN) + _dot(kh, dst, _NT)
            dkh = _dot(iv_h, dst)
            dqt = _dot(da, kt)
            dkt = _dot(da, qt, _TN)
            dz_ref[:, 2 * HW + h * dk:2 * HW + (h + 1) * dk] = div.astype(BF16)
            dqa_buf[:, sl] = dqh * t["e_b"][:, sl] + dqt * t["e_qm"][:, sl]
            dkk_buf[:, sl] = dkt * t["e_km"][:, sl] + dkh * t["e_kl"][:, sl]
            db_buf[:, sl] = r16(qt) * dqt - r16(kt) * dkt + r16(qh) * dqh - r16(kh) * dkh

        dqa, dkk = dqa_buf[...], dkk_buf[...]
        triu = jnp.logical_not(t["tril"]) | (lax.broadcasted_iota(jnp.int32, (C, C), 0) == lax.broadcasted_iota(jnp.int32, (C, C), 1))
        dlf = _ones_dot(triu.astype(BF16), db_buf[...]) + cross[...]
        df = dlf / t["f"] - dkk
        sig, lb = t["sig"], t["lb"]
        dz_ref[:, HW:2 * HW] = (df * (1.0 - lb) * sig * (1.0 - sig)).astype(BF16)
        dlb_acc[...] += jnp.sum(df * (1.0 - sig), axis=0, keepdims=True)
        q, sq = t["q"], t["sq"]
        dz_ref[:, 0:HW] = (dqa * (sq * (1.0 + q * (1.0 - sq)))).astype(BF16)

        @pl.when(c == nc - 1)
        def _():
            da0 = dlb_acc[...] * lb * (1.0 - lb)
            dhlb_ref[0:1, :] = da0
            dhlb_ref[1:2, :] = -da0

    def col(k):
        return pl.BlockSpec((C, HW), lambda c: (nc - 1 - c, k))

    base = 2
    return pl.pallas_call(
        body, name="hgrn_bwd", grid=(nc,),
        in_specs=[col(base), col(base + 1), col(base + 2), col(base + 3), col(0),
                  pl.BlockSpec((None, H, dk, dk), lambda c: (nc - 1 - c, 0, 0, 0)),
                  pl.BlockSpec((None, H, dk, dk), lambda c: (jnp.minimum(nc - c, nc - 1), 0, 0, 0)), col(0),
                  pl.BlockSpec((2, HW), lambda c: (0, 0)), pl.BlockSpec((1, HW), lambda c: (0, 0)),
                  pl.BlockSpec((C, B0), lambda c: (nc - 1 - c, 0)), pl.BlockSpec((2, C, DT), lambda c: (0, nc - 1 - c, 0))],
        out_specs=[pl.BlockSpec((C, INW), lambda c: (nc - 1 - c, 0)), pl.BlockSpec((8, HW), lambda c: (0, 0)),
                   pl.BlockSpec((2, HW), lambda c: (0, 0))],
        out_shape=[jax.ShapeDtypeStruct((S, INW), BF16), jax.ShapeDtypeStruct((8, HW), F32), jax.ShapeDtypeStruct((2, HW), F32)],
        scratch_shapes=[pltpu.VMEM((H, dk, dk), F32), pltpu.VMEM((1, HW), F32), pltpu.VMEM((C, HW), F32), pltpu.VMEM((C, HW), F32),
                        pltpu.VMEM((C, HW), F32), pltpu.VMEM((1, HW), F32)],
        compiler_params=_cparams(("arbitrary",)))(z, z, z, z, o, states, states, dyb, hg_lb, ng, dz_head, dz_tail)


def _position():
    x, y, c = lax.axis_index("x"), lax.axis_index("y"), lax.axis_index("c")
    return x, y, c, 4 * x + 2 * y + c


def _flip(x, y, c, k):
    return (1 - x if k & 4 else x, 1 - y if k & 2 else y, 1 - c if k & 1 else c)


def _allgather_small(name, v):
    R, L = v.shape

    def body(v_ref, out_ref, send_sems, recv_sems):
        x, y, c, me = _position()
        out_ref[me] = v_ref[...]
        copies = []
        for k in range(1, N_DEV):
            cp = pltpu.make_async_remote_copy(src_ref=v_ref, dst_ref=out_ref.at[me], send_sem=send_sems.at[k - 1],
                                              recv_sem=recv_sems.at[k - 1], device_id=_flip(x, y, c, k), device_id_type=MESH)
            cp.start()
            copies.append(cp)
        for cp in copies:
            cp.wait()

    return pl.pallas_call(
        body, name=name, out_shape=jax.ShapeDtypeStruct((N_DEV, R, L), v.dtype),
        in_specs=[pl.BlockSpec(memory_space=pltpu.VMEM)], out_specs=pl.BlockSpec(memory_space=pltpu.VMEM),
        scratch_shapes=[pltpu.SemaphoreType.DMA((N_DEV - 1,)), pltpu.SemaphoreType.DMA((N_DEV - 1,))],
        compiler_params=pltpu.CompilerParams(vmem_limit_bytes=VMEM_LIMIT),
    )(v)


def _allgather_hbm(name, shards):
    n = len(shards)

    def body(*refs):
        ins, outs = refs[:n], refs[n:2 * n]
        send_sems, recv_sems, local_sems = refs[2 * n:]
        x, y, c, me = _position()
        sibling = (x, y, 1 - c)
        chips = [(1 - x, y), (x, 1 - y), (1 - x, 1 - y)]

        def slot(px, py, pc):
            return 4 * px + 2 * py + pc

        def copy(w, k, block, to, src=None):
            dst = outs[w].at[slot(*block)]
            return pltpu.make_async_remote_copy(src_ref=dst if src is None else src, dst_ref=dst, send_sem=send_sems.at[w, k],
                                                recv_sem=recv_sems.at[w, k], device_id=to, device_id_type=MESH)

        mine, first, passed = [], [], []
        for w in range(n):
            cp = pltpu.make_async_copy(ins[w], outs[w].at[me], local_sems.at[w])
            cp.start()
            mine.append(cp)
            for j, chip in enumerate(chips):
                first.append(copy(w, 1 + j, (x, y, c), (*chip, c), src=ins[w]))
            first.append(copy(w, 0, (x, y, c), sibling, src=ins[w]))
        for cp in first:
            cp.start()
        for w in range(n):
            for j, chip in enumerate(chips):
                copy(w, 1 + j, (*chip, c), (x, y, c)).wait_recv()
                cp = copy(w, 4 + j, (*chip, c), sibling)
                cp.start()
                passed.append(cp)
        for w in range(n):
            copy(w, 0, sibling, (x, y, c)).wait_recv()
            for j, chip in enumerate(chips):
                copy(w, 4 + j, (*chip, 1 - c), (x, y, c)).wait_recv()
        for cp in first + passed:
            cp.wait_send()
        for cp in mine:
            cp.wait()

    hbm = pl.BlockSpec(memory_space=pltpu.HBM)
    return pl.pallas_call(
        body, name=name, out_shape=[jax.ShapeDtypeStruct((N_DEV, *s.shape), s.dtype) for s in shards],
        in_specs=[hbm] * n, out_specs=[hbm] * n,
        scratch_shapes=[pltpu.SemaphoreType.DMA((n, 7)), pltpu.SemaphoreType.DMA((n, 7)), pltpu.SemaphoreType.DMA((n,))],
    )(*shards)


_HBM = pl.BlockSpec(memory_space=pltpu.HBM)
_SEM = pl.BlockSpec(memory_space=pltpu.SEMAPHORE)
_EFFECT = pltpu.SideEffectType.DATAFLOW_SIDE_EFFECTING


def _split_start(name, bufs, n_sems, copies_fn, after=None):
    nb = len(bufs)
    extra = [] if after is None else [after]
    k = nb + len(extra)

    def body(*refs):
        for cp in copies_fn(refs[:nb], refs[k], refs[k + 1]):
            cp.start()
        refs[-1][...] = jnp.zeros_like(refs[-1])

    sems = pltpu.SemaphoreType.DMA((n_sems,))
    res = pl.pallas_call(
        body, name=name,
        out_shape=(sems, sems, *[pltpu.HBM(a.shape, a.dtype) for a in bufs], jax.ShapeDtypeStruct((8, LANES), F32)),
        in_specs=[_HBM] * nb + [pl.BlockSpec(memory_space=pl.ANY)] * len(extra),
        out_specs=(_SEM, _SEM, *[_HBM] * nb, pl.BlockSpec(memory_space=pltpu.VMEM)),
        input_output_aliases={i: 2 + i for i in range(nb)},
        compiler_params=pltpu.CompilerParams(has_side_effects=_EFFECT),
    )(*[pltpu.with_memory_space_constraint(a, pltpu.HBM) for a in bufs], *extra)
    return res[0], res[1], list(res[2:2 + nb]), res[-1]


def _split_wait(name, bufs, send_sems, recv_sems, after, copies_fn):
    nb = len(bufs)

    def body(*refs):
        for cp in copies_fn(refs[:nb], refs[nb], refs[nb + 1]):
            cp.wait_send()
            cp.wait_recv()

    res = pl.pallas_call(
        body, name=name, out_shape=tuple(pltpu.HBM(a.shape, a.dtype) for a in bufs),
        in_specs=[_HBM] * nb + [_SEM, _SEM, pl.BlockSpec(memory_space=pl.ANY)], out_specs=tuple([_HBM] * nb),
        input_output_aliases={i: i for i in range(nb)},
        compiler_params=pltpu.CompilerParams(has_side_effects=_EFFECT),
    )(*bufs, send_sems, recv_sems, after)
    return list(res)


def _split_relay(name, bufs, send_sems, recv_sems, after, wait_fn, n_sems, start_fn):
    nb = len(bufs)

    def body(*refs):
        for cp in wait_fn(refs[:nb], refs[nb], refs[nb + 1]):
            cp.wait_send()
            cp.wait_recv()
        for cp in start_fn(refs[:nb], refs[nb + 3], refs[nb + 4]):
            cp.start()
        refs[-1][...] = jnp.zeros_like(refs[-1])

    sems = pltpu.SemaphoreType.DMA((n_sems,))
    res = pl.pallas_call(
        body, name=name, out_shape=(sems, sems, *[pltpu.HBM(a.shape, a.dtype) for a in bufs], jax.ShapeDtypeStruct((8, LANES), F32)),
        in_specs=[_HBM] * nb + [_SEM, _SEM, pl.BlockSpec(memory_space=pl.ANY)],
        out_specs=(_SEM, _SEM, *[_HBM] * nb, pl.BlockSpec(memory_space=pltpu.VMEM)),
        input_output_aliases={i: 2 + i for i in range(nb)},
        compiler_params=pltpu.CompilerParams(has_side_effects=_EFFECT),
    )(*bufs, send_sems, recv_sems, after)
    return res[0], res[1], list(res[2:2 + nb]), res[-1]


N_CHIP = 4


def _chip_flip(x, y, k):
    return (1 - x if k & 2 else x), (1 - y if k & 1 else y)


def _gather_first_copies(n):
    def copies(bufs, send_sems, recv_sems):
        x, y, c, me = _position()
        out = []
        for w in range(n):
            for k in range(N_CHIP):
                to = (x, y, 1 - c) if k == 0 else (*_chip_flip(x, y, k), c)
                out.append(pltpu.make_async_remote_copy(
                    src_ref=bufs[w], dst_ref=bufs[n + w].at[me], send_sem=send_sems.at[w * N_CHIP + k],
                    recv_sem=recv_sems.at[w * N_CHIP + k], device_id=to, device_id_type=MESH))
        return out
    return copies


def _gather_relay_copies(n):
    def copies(bufs, send_sems, recv_sems):
        x, y, c, _ = _position()
        out = []
        for w in range(n):
            for k in range(1, N_CHIP):
                px, py = _chip_flip(x, y, k)
                blk = bufs[n + w].at[4 * px + 2 * py + c]
                out.append(pltpu.make_async_remote_copy(
                    src_ref=blk, dst_ref=blk, send_sem=send_sems.at[w * (N_CHIP - 1) + k - 1],
                    recv_sem=recv_sems.at[w * (N_CHIP - 1) + k - 1], device_id=(x, y, 1 - c), device_id_type=MESH))
        return out
    return copies


def _small_gather_copies(bufs, send_sems, recv_sems):
    x, y, c, me = _position()
    return [pltpu.make_async_remote_copy(src_ref=bufs[0], dst_ref=bufs[1].at[me], send_sem=send_sems.at[k - 1], recv_sem=recv_sems.at[k - 1],
                                         device_id=_flip(x, y, c, k), device_id_type=MESH) for k in range(1, N_DEV)]


def _xor(a, b):
    return a + b - 2 * a * b


def _forward_first_copies(n):
    def copies(bufs, send_sems, recv_sems):
        x, y, c, me = _position()
        out = []
        for w in range(n):
            for k, to in enumerate([(x, y, 1 - c), (1 - x, y, c), (x, 1 - y, c)]):
                out.append(pltpu.make_async_remote_copy(
                    src_ref=bufs[w], dst_ref=bufs[n + w].at[me], send_sem=send_sems.at[w * 3 + k],
                    recv_sem=recv_sems.at[w * 3 + k], device_id=to, device_id_type=MESH))
        return out
    return copies


def _forward_second_copies(n):
    def copies(bufs, send_sems, recv_sems):
        x, y, c, _ = _position()
        out = []
        for w in range(n):
            half = bufs[n + w].shape[1] // 2
            for k, (src_chip, rows, to) in enumerate([((1 - x, y), pl.ds(0, half), (x, 1 - y, c)), ((x, 1 - y), pl.ds(half, half), (1 - x, y, c))]):
                blk = bufs[n + w].at[4 * src_chip[0] + 2 * src_chip[1] + c, rows]
                out.append(pltpu.make_async_remote_copy(src_ref=blk, dst_ref=blk, send_sem=send_sems.at[w * 4 + k],
                                                        recv_sem=recv_sems.at[w * 4 + k], device_id=to, device_id_type=MESH))
            for k, (px, py) in enumerate([(1 - x, y), (x, 1 - y)]):
                blk = bufs[n + w].at[4 * px + 2 * py + c]
                out.append(pltpu.make_async_remote_copy(src_ref=blk, dst_ref=blk, send_sem=send_sems.at[w * 4 + 2 + k],
                                                        recv_sem=recv_sems.at[w * 4 + 2 + k], device_id=(x, y, 1 - c), device_id_type=MESH))
        return out
    return copies


def _forward_third_copies(n):
    def copies(bufs, send_sems, recv_sems):
        x, y, c, _ = _position()
        out = []
        for w in range(n):
            blk = bufs[n + w].at[4 * (1 - x) + 2 * (1 - y) + c]
            out.append(pltpu.make_async_remote_copy(src_ref=blk, dst_ref=blk, send_sem=send_sems.at[w], recv_sem=recv_sems.at[w],
                                                    device_id=(x, y, 1 - c), device_id_type=MESH))
        return out
    return copies


def _to_sibling_copies(n):
    def copies(bufs, send_sems, recv_sems):
        x, y, c, _ = _position()
        out = []
        for w in range(n):
            for q in range(N_CHIP):
                out.append(pltpu.make_async_remote_copy(
                    src_ref=bufs[w].at[2 * q + 1 - c], dst_ref=bufs[n + w].at[q], send_sem=send_sems.at[w * N_CHIP + q],
                    recv_sem=recv_sems.at[w * N_CHIP + q], device_id=(x, y, 1 - c), device_id_type=MESH))
        return out
    return copies


def _to_owner_copies(n):
    def copies(bufs, send_sems, recv_sems):
        x, y, c, _ = _position()
        out = []
        for w in range(n):
            for k in range(1, N_CHIP):
                px, py = (1 - x if k & 2 else x), (1 - y if k & 1 else y)
                out.append(pltpu.make_async_remote_copy(
                    src_ref=bufs[w].at[2 * px + py], dst_ref=bufs[n + w].at[k - 1], send_sem=send_sems.at[w * (N_CHIP - 1) + k - 1],
                    recv_sem=recv_sems.at[w * (N_CHIP - 1) + k - 1], device_id=(px, py, c), device_id_type=MESH))
        return out
    return copies


def _chip_sum(name, stack, landed, c_idx):
    _, R, C = stack.shape
    tr = _tile(R, max(16, 1048576 // C), 16)

    def body(c_ref, a_ref, b_ref, o_ref):
        o_ref[...] = (a_ref[...].astype(F32) + b_ref[...].astype(F32)).astype(o_ref.dtype)

    return pl.pallas_call(
        body, name=name,
        grid_spec=pltpu.PrefetchScalarGridSpec(
            num_scalar_prefetch=1, grid=(N_CHIP, R // tr),
            in_specs=[pl.BlockSpec((None, tr, C), lambda q, i, c_ref: (2 * q + c_ref[0], i, 0)),
                      pl.BlockSpec((None, tr, C), lambda q, i, c_ref: (q, i, 0))],
            out_specs=pl.BlockSpec((None, tr, C), lambda q, i, c_ref: (q, i, 0))),
        out_shape=jax.ShapeDtypeStruct((N_CHIP, R, C), stack.dtype),
        compiler_params=_cparams(("parallel", "parallel")))(c_idx, stack, landed)


def _ada_mod(c16, w):
    _, D = c16.shape
    n = w.shape[1]
    tk = _tile(D, 512)
    nk = D // tk

    def body(c_ref, w_ref, o_ref, ca_ref):
        @pl.when(pl.program_id(0) == 0)
        def _():
            o_ref[...] = jnp.zeros_like(o_ref)

        cv = c_ref[...]
        ca = cv * _sigmoid(cv)
        ca_ref[...] = ca
        o_ref[...] += _dot(ca, w_ref[...])

    return pl.pallas_call(
        body, name="ada_mod", grid=(nk,),
        in_specs=[pl.BlockSpec((16, tk), lambda k: (0, k)), pl.BlockSpec((tk, n), lambda k: (k, 0))],
        out_specs=[pl.BlockSpec((16, n), lambda k: (0, 0)), pl.BlockSpec((16, tk), lambda k: (0, k))],
        out_shape=[jax.ShapeDtypeStruct((16, n), F32), jax.ShapeDtypeStruct((16, D), F32)],
        compiler_params=_cparams(("arbitrary",)))(c16, w)


def _cast_shard(name, wf, slot):
    r, c = wf.shape
    tr = _tile(r, max(16, 1048576 // c), 16)

    def body(slot_ref, w_ref, s_ref, g_ref):
        v = w_ref[...].astype(BF16)
        s_ref[...] = v
        g_ref[...] = v

    return pl.pallas_call(
        body, name=name,
        grid_spec=pltpu.PrefetchScalarGridSpec(
            num_scalar_prefetch=1, grid=(r // tr,), in_specs=[pl.BlockSpec((tr, c), lambda i, s: (i, 0))],
            out_specs=[pl.BlockSpec((tr, c), lambda i, s: (i, 0)), pl.BlockSpec((None, tr, c), lambda i, s: (s[0], i, 0))]),
        out_shape=[jax.ShapeDtypeStruct((r, c), BF16), jax.ShapeDtypeStruct((N_DEV, r, c), BF16)],
        compiler_params=_cparams(("parallel",)))(slot, wf)


def _adam_math(w, g, m, v):
    m2 = ADAM_B1 * m + (1.0 - ADAM_B1) * g
    v2 = ADAM_B2 * v + (1.0 - ADAM_B2) * (g * g)
    m_hat = m2 / (1.0 - ADAM_B1 ** ADAM_STEP)
    v_hat = v2 / (1.0 - ADAM_B2 ** ADAM_STEP)
    delta = -ADAM_LR * (m_hat / (jnp.sqrt(v_hat) + ADAM_EPS) + ADAM_WD * w)
    return delta, m2, v2


def _adamw(name, w, m, v, own, own_slot, parts=(), row0=0, into=None):
    R, C = w.shape
    Rp = own.shape[1]
    tr = _tile(Rp, max(16, 393216 // C), 16)
    off = row0 // tr
    n_p = len(parts)
    held = [] if into is None else list(into)

    def body(slot_ref, *refs):
        w_ref, m_ref, v_ref, own_ref = refs[:4]
        g_ref, d_ref, m2_ref, v2_ref = refs[4 + n_p + len(held):]
        g = own_ref[...].astype(F32)
        for p_ref in refs[4:4 + n_p]:
            for s in range(p_ref.shape[0]):
                g = g + p_ref[s].astype(F32)
        delta, m2, v2 = _adam_math(w_ref[...], g, m_ref[...], v_ref[...])
        g_ref[...] = g
        d_ref[...] = delta
        m2_ref[...] = m2
        v2_ref[...] = v2

    blk = pl.BlockSpec((tr, C), lambda i, s: (i + off, 0))
    out = jax.ShapeDtypeStruct((R, C), F32)
    return pl.pallas_call(
        body, name=name,
        grid_spec=pltpu.PrefetchScalarGridSpec(
            num_scalar_prefetch=1, grid=(Rp // tr,),
            in_specs=[blk, blk, blk, pl.BlockSpec((None, tr, C), lambda i, s: (s[0], i, 0))]
            + [pl.BlockSpec((a.shape[0], tr, C), lambda i, s: (0, i, 0)) for a in parts]
            + [pl.BlockSpec(memory_space=pl.ANY)] * len(held),
            out_specs=[blk] * 4),
        out_shape=[out] * 4, input_output_aliases={5 + n_p + i: i for i in range(len(held))},
        compiler_params=_cparams(("parallel",)))(own_slot, w, m, v, own, *parts, *held)


def _small_update(gathered, w, m, v, after, rows):
    _, R, L = gathered.shape
    rs = w.shape[0]
    n = len(rows)
    assert all(r % 8 == 0 for r in rows) and sum(rows) <= rs and rs + 8 <= R

    def body(p_ref, w_ref, m_ref, v_ref, after_ref, *outs):
        g = p_ref[0]
        for p in range(1, N_DEV):
            g = g + p_ref[p]
        kinds = (g,) + _adam_math(w_ref[...], g[0:rs, :], m_ref[...], v_ref[...])
        at = 0
        for k, r in enumerate(rows):
            for idx, val in enumerate(kinds):
                outs[idx * n + k][...] = val[at:at + r, :]
            at += r
        outs[4 * n][...] = g[at:at + 8, :]

    vm = pl.BlockSpec(memory_space=pltpu.VMEM)
    shapes = [jax.ShapeDtypeStruct((r, L), F32) for _ in range(4) for r in rows] + [jax.ShapeDtypeStruct((8, L), F32)]
    return pl.pallas_call(body, name="small_update", in_specs=[vm] * 4 + [pl.BlockSpec(memory_space=pl.ANY)], out_specs=[vm] * len(shapes),
                          out_shape=shapes, compiler_params=pltpu.CompilerParams(vmem_limit_bytes=VMEM_LIMIT))(gathered, w, m, v, after)


class _Fetched(dict):
    def __init__(self, fetch):
        super().__init__()
        self.fetch = fetch

    def first(self, key, after):
        self[key] = self.fetch(key, after)
        return self[key]


def _local_step(x, tgt, mod, p, fetch, F, scatter=None):
    S, D = x.shape
    GW, HW = p["ln_g"].shape[1], p["hg_ng"].shape[1]
    G, T, _ = p["ws"].shape
    w = _Fetched(fetch)
    INW = 2 * GW + 4 * HW + 2 * D
    in_loc, br_loc, fi_loc = INW // N_DEV, D // N_DEV, 2 * F // N_DEV
    assert GW == HW and F % fi_loc == 0
    sh1, sc1, gt1, sh2, sc2, gt2 = (mod[:, k * D:(k + 1) * D] for k in range(6))
    bsb = jnp.broadcast_to(p["bs"][:, :, None], (G, T, GW // G))

    tm = _tile(S, 1024, 16)
    tmh = _tile(S, 512, 16)
    tn_in = _tile(in_loc, 1280)
    tn_d = _tile(D, 512)
    tn_br = _tile(br_loc, 512)
    tk_s = S
    tm_w = _tile(D, 1024)
    g_off = 2 * GW + 4 * HW

    h1 = _norm_mod("norm1", x, p["norm1_g"], sc1, sh1)
    z = _mm_nn_stacked("proj_in", h1, w.first("in", h1), tm=tm, tn=tn_in, tk=D)[0]
    ya = _gmlp_fwd(z, p["ln_g"], p["ln_b"], p["ws"], bsb, GW)
    yb, o_hg, states = _hg_fwd(z, p["hg_lb"], p["hg_ng"], HW)
    flat = {k: jnp.swapaxes(w.first(k, yb), 0, 1).reshape(GW, D) for k in ("bg", "bh")}
    tn_f = _tile(D, 1024)
    pa = _matmul(
        "branch_gmlp", ya, flat["bg"], dims=_NN, grid_mnk=(S // tm, D // tn_f, 1), tiles=(tm, tn_f),
        a_spec=pl.BlockSpec((tm, GW), lambda i, j, k: (i, 0)), b_spec=pl.BlockSpec((GW, tn_f), lambda i, j, k: (0, j)),
        out_shapes=[jax.ShapeDtypeStruct((S, D), F32)], out_specs=[pl.BlockSpec((tm, tn_f), lambda i, j, k: (i, j))], epilogue=_store(F32))[0]
    t_fi = w.first("fi_early", pa)

    def gates(ga_ref, gb_ref, ba_ref, bb_ref):
        return _sigmoid(ga_ref[...] + ba_ref[...]), _sigmoid(gb_ref[...] + bb_ref[...])

    def gate_specs(tn_, tm_=tm):
        o1, o2 = g_off // tn_, (g_off + D) // tn_
        return [pl.BlockSpec((tm_, tn_), lambda i, j, k: (i, o1 + j)), pl.BlockSpec((tm_, tn_), lambda i, j, k: (i, o2 + j)),
                pl.BlockSpec((1, tn_), lambda i, j, k: (0, j)), pl.BlockSpec((1, tn_), lambda i, j, k: (0, D // tn_ + j))]

    def merge_ep(acc, ex, outs):
        ga, gb = gates(*ex[1:5])
        outs[0][...] = acc
        outs[1][...] = (ga * ex[0][...] + gb * acc).astype(BF16)

    tile_o = pl.BlockSpec((tmh, tn_f), lambda i, j, k: (i, j))
    pb, y = _matmul(
        "branch_hg_merge", yb, flat["bh"], dims=_NN, grid_mnk=(S // tmh, D // tn_f, 1), tiles=(tmh, tn_f),
        a_spec=pl.BlockSpec((tmh, HW), lambda i, j, k: (i, 0)), b_spec=pl.BlockSpec((HW, tn_f), lambda i, j, k: (0, j)),
        extras=[pa, z, z, p["b_gate"], p["b_gate"]], extra_specs=[tile_o, *gate_specs(tn_f, tmh)],
        out_shapes=[jax.ShapeDtypeStruct((S, D), F32), jax.ShapeDtypeStruct((S, D), BF16)], out_specs=[tile_o, tile_o],
        epilogue=merge_ep, after=t_fi)

    def resid_ep(acc, ex, outs):
        outs[0][...] = acc
        outs[1][...] = ex[0][...] + ex[1][...] * acc

    def resid_mm(name, a, b, res, gt, tm_):
        K = a.shape[1]
        t_o = pl.BlockSpec((tm_, tn_d), lambda i, j, k: (i, j))
        return _matmul(
            name, a, b, dims=_NN, grid_mnk=(S // tm_, D // tn_d, 1), tiles=(tm_, tn_d),
            a_spec=pl.BlockSpec((tm_, K), lambda i, j, k: (i, 0)), b_spec=pl.BlockSpec((K, tn_d), lambda i, j, k: (0, j)),
            extras=[res, gt], extra_specs=[t_o, pl.BlockSpec((1, tn_d), lambda i, j, k: (0, j))],
            out_shapes=[jax.ShapeDtypeStruct((S, D), F32)] * 2, out_specs=[t_o, t_o], epilogue=resid_ep)

    o1, xm = resid_mm("proj_out", y, w.first("out", z), x, gt1, tm)
    h2 = _norm_mod("norm2", xm, p["norm2_g"], sc2, sh2)
    hf, hf_fac = _ffn_in_swiglu(h2, w.first("fi", h2))
    o2, x3 = resid_mm("ffn_out", hf, w.first("fo", hf), xm, gt2, tmh)
    dx3, do2, vec_l = _loss_head(x3, tgt, p["final_g"], o2, gt2)

    nf = F // fi_loc

    def dswiglu_ep(acc, ex, outs):
        outs[0][0] = (acc * ex[0][0].astype(F32)).astype(BF16)
        outs[0][1] = (acc * ex[0][1].astype(F32)).astype(BF16)

    pair = pl.BlockSpec((2, tmh, fi_loc), lambda i, j, k: (0, i, j))
    dab = _matmul(
        "ffn_out_dx", do2, w["fo"], dims=_NT, grid_mnk=(S // tmh, nf, 1), tiles=(tmh, fi_loc),
        a_spec=pl.BlockSpec((tmh, D), lambda i, j, k: (i, 0)), b_spec=pl.BlockSpec((fi_loc, D), lambda i, j, k: (j, 0)),
        extras=[hf_fac], extra_specs=[pair], out_shapes=[jax.ShapeDtypeStruct((2, S, F), BF16)], out_specs=[pair],
        epilogue=dswiglu_ep)[0]
    start = (lambda name, grads: scatter[0](name, grads)) if scatter is not None else (lambda name, grads: None)
    push = (lambda name, after: scatter[1](name, after)) if scatter is not None else (lambda name, after: None)

    def zero(token):
        return 0.0 if token is None else token[0:1, 0:1]

    tm_f = _tile(F, 512)
    g_fo = _mm_tn("ffn_out_dw", hf, do2, pl.BlockSpec((tk_s, D), lambda i, j, k: (k, j)), Mo=F, No=D, S=S, tm=tm_f, tn=D, tk=tk_s)
    g_fi = _mm_tn("ffn_in_dw", h2, dab, pl.BlockSpec((None, tk_s, fi_loc), lambda i, j, k: (j // nf, k, j % nf)),
                  Mo=D, No=2 * F, S=S, tm=tm_w, tn=fi_loc, tk=tk_s, stacked_nloc=fi_loc, after=g_fo)
    t_ffn = start("scatter_ffn", dict(fo=g_fo, fi=g_fi))
    dh2 = _mm_nt_stacked("ffn_in_dx", pl.BlockSpec((None, tm, fi_loc), lambda i, j, k: (k // nf, i, k % nf)), dab, w["fi"],
                         M=S, tm=tm, tn=tm_w, tk=fi_loc, after=t_ffn)
    dxm, vec2, do1 = _norm_mod_bwd("norm2_bwd", dh2, xm, p["norm2_g"], sc2, dx3, o1, gt1)
    t_ffn = push("scatter_ffn", dxm)

    def dmerge_ep(acc, ex, outs):
        ga, gb = gates(*ex[2:6])
        outs[0][...] = (acc * ga).astype(BF16)
        outs[1][...] = (acc * gb).astype(BF16)
        outs[2][0] = (acc * ex[0][...] * ga * (1.0 - ga)).astype(BF16)
        outs[2][1] = (acc * ex[1][...] * gb * (1.0 - gb)).astype(BF16)

    t_o = pl.BlockSpec((tm, tn_d), lambda i, j, k: (i, j))
    dpa, dpb, dg2 = _matmul(
        "proj_out_dx", do1, w["out"], dims=_NT, grid_mnk=(S // tm, D // tn_d, 1), tiles=(tm, tn_d),
        a_spec=pl.BlockSpec((tm, D), lambda i, j, k: (i, 0)), b_spec=pl.BlockSpec((tn_d, D), lambda i, j, k: (j, 0)),
        extras=[pa, pb, z, z, p["b_gate"], p["b_gate"]], extra_specs=[t_o, t_o, *gate_specs(tn_d)],
        out_shapes=[jax.ShapeDtypeStruct((S, D), BF16), jax.ShapeDtypeStruct((S, D), BF16), jax.ShapeDtypeStruct((2, S, D), BF16)],
        out_specs=[t_o, t_o, pl.BlockSpec((2, tm, tn_d), lambda i, j, k: (0, i, j))], epilogue=dmerge_ep, after=t_ffn)
    g_out = _mm_tn("proj_out_dw", y, do1, pl.BlockSpec((tk_s, D), lambda i, j, k: (k, j)), Mo=D, No=D, S=S, tm=tn_d, tn=D, tk=tk_s)
    tn_g = _tile(GW, 512)
    b_br = pl.BlockSpec((tk_s, br_loc), lambda i, j, k: (k, j))
    g_bg = _mm_tn("branch_gmlp_dw", ya, dpa, b_br, Mo=GW, No=D, S=S, tm=tn_g, tn=br_loc, tk=tk_s, stacked_nloc=br_loc)
    g_bh = _mm_tn("branch_hg_dw", yb, dpb, b_br, Mo=HW, No=D, S=S, tm=tn_g, tn=br_loc, tk=tk_s, stacked_nloc=br_loc)
    t_mix = start("scatter_mixer", dict(out=g_out, bg=g_bg, bh=g_bh))
    def branch_dx(name, dp, w_flat):
        return _matmul(
            name, dp, w_flat, dims=_NT, grid_mnk=(S // tm, GW // tn_g, 1), tiles=(tm, tn_g),
            a_spec=pl.BlockSpec((tm, D), lambda i, j, k: (i, 0)), b_spec=pl.BlockSpec((tn_g, D), lambda i, j, k: (j, 0)),
            out_shapes=[jax.ShapeDtypeStruct((S, GW), F32)], out_specs=[pl.BlockSpec((tm, tn_g), lambda i, j, k: (i, j))],
            epilogue=_store(F32), after=t_mix)[0]

    dya = branch_dx("branch_gmlp_dx", dpa, flat["bg"])
    dyb = branch_dx("branch_hg_dx", dpb, flat["bh"])
    db_gate = _colsum2(dg2)
    dz_gmlp, dln, dws, dbs = _gmlp_bwd(z, dya, p["ln_g"], p["ln_b"], p["ws"], bsb, GW)
    t_mix = push("scatter_mixer", dz_gmlp)
    dz, dng, dhlb = _hg_bwd(z, o_hg, states, dyb, p["hg_lb"], p["hg_ng"] + zero(t_mix), HW, dz_gmlp, dg2)
    half = D // 2
    tm_h = _tile(half, 1024)
    g_in = []
    t_in = None
    for hname, h in (("a", 0), ("b", 1)):
        g_in.append(_mm_tn("proj_in_dw_" + hname, h1, dz, pl.BlockSpec((tk_s, in_loc), lambda i, j, k: (k, j)), Mo=half, No=INW, S=S,
                           tm=tm_h, tn=in_loc, tk=tk_s, stacked_nloc=in_loc, after=t_in, a_off=h * (half // tm_h)))
        t_in = start("scatter_proj_in_" + hname, {"w_in_" + hname: g_in[-1]})
    t_in = push("scatter_proj_in_a", t_in)
    dh1 = _mm_nt_stacked("proj_in_dx", pl.BlockSpec((tm, in_loc), lambda i, j, k: (i, k)), dz, w["in"], M=S, tm=tm, tn=tm_w, tk=in_loc,
                         after=t_in)
    dx, vec1 = _norm_mod_bwd("norm1_bwd", dh1, x, p["norm1_g"], sc1, dxm)

    dmod = jnp.concatenate([vec1[0:1], vec1[1:2], vec2[3:4], vec2[0:1], vec2[1:2], vec_l[2:3]], axis=1)
    small = dict(norm1_g=vec1[2:3], b_gate=db_gate.reshape(1, 2 * D), ln_g=dln[0:1], ln_b=dln[1:2], ws=dws, bs=dbs.reshape(G, T),
                 hg_lb=dhlb, hg_ng=dng[0:1], norm2_g=vec2[2:3], final_g=vec_l[1:2], loss=vec_l[0:1, 0:LANES])
    big = dict(w_in_a=g_in[0], w_in_b=g_in[1], bg=g_bg, bh=g_bh, out=g_out, fi=g_fi, fo=g_fo)
    return dx, big, small, dmod


_SMALL = ("b_ada", "norm1_g", "b_gate", "ln_g", "ln_b", "ws", "bs", "hg_lb", "hg_ng", "norm2_g", "final_g")


def _pack(parts, rows_mult=8):
    flat = [a.reshape(-1) for a in parts]
    offs, n = [], 0
    for a in flat:
        offs.append(n)
        n += a.shape[0]
    pad = (-n) % (LANES * rows_mult)
    if pad:
        flat.append(jnp.zeros((pad,), F32))
    return jnp.concatenate(flat).reshape(-1, LANES), offs


def kernel(x, c, w_ada, b_ada, norm1_g, w_in, b_gate, gmlp_ln_g, gmlp_ln_b, gmlp_ws, gmlp_bs, hg_lb, hg_norm_g, w_branch_gmlp, w_branch_hg, w_out, norm2_g, w_ffn_in, w_ffn_out, final_norm_g, loss_target, m_w_ada, m_b_ada, m_norm1_g, m_w_in, m_b_gate, m_gmlp_ln_g, m_gmlp_ln_b, m_gmlp_ws, m_gmlp_bs, m_hg_lb, m_hg_norm_g, m_w_branch_gmlp, m_w_branch_hg, m_w_out, m_norm2_g, m_w_ffn_in, m_w_ffn_out, m_final_norm_g, v_w_ada, v_b_ada, v_norm1_g, v_w_in, v_b_gate, v_gmlp_ln_g, v_gmlp_ln_b, v_gmlp_ws, v_gmlp_bs, v_hg_lb, v_hg_norm_g, v_w_branch_gmlp, v_w_branch_hg, v_w_out, v_norm2_g, v_w_ffn_in, v_w_ffn_out, v_final_norm_g):
    S, D = x.shape[1], x.shape[2]
    ada_loc = w_ada.shape[2]
    me = 4 * lax.axis_index("x") + 2 * lax.axis_index("y") + lax.axis_index("c")
    me_idx = me.astype(jnp.int32).reshape(1)

    c_all = _allgather_small("gather_c", c.reshape(D // LANES, LANES)).reshape(N_DEV, D)
    mod_cols, c_act = _ada_mod(jnp.pad(c_all, ((0, 16 - N_DEV), (0, 0))), w_ada[0])
    mod_all = _allgather_small("gather_mod", mod_cols[:N_DEV].reshape(-1, LANES)).reshape(N_DEV, N_DEV, ada_loc)
    mod = lax.dynamic_index_in_dim(mod_all, me, axis=1, keepdims=False).reshape(1, N_DEV * ada_loc) + b_ada

    def empty_hbm(shape, dtype):
        return pltpu.with_memory_space_constraint(lax.empty(shape, dtype), pltpu.HBM)

    groups = dict(gather_in=dict(keys=["in"], src=[w_in], forward=True),
                  gather_mixer=dict(keys=["bg", "bh", "out"], src=[w_branch_gmlp, w_branch_hg, w_out], forward=False),
                  gather_ffn_in=dict(keys=["fi"], src=[w_ffn_in], forward=True),
                  gather_ffn_out=dict(keys=["fo"], src=[w_ffn_out], forward=False))
    group_of = {k: gname for gname, g in groups.items() for k in g["keys"]}

    def first_hop(gname, after):
        g = groups[gname]
        n = len(g["keys"])
        cast = [_cast_shard(f"{gname}_cast_{k}", a[0], me_idx) for k, a in zip(g["keys"], g["src"])]
        shards, outs = [s for s, _ in cast], [o for _, o in cast]
        if g["forward"]:
            *g["hop"], token = _split_start(gname + "_hop1", shards + outs, n * 3, _forward_first_copies(n), after=after)
        else:
            *g["hop"], token = _split_start(gname + "_hop1", shards + outs, n * N_CHIP, _gather_first_copies(n), after=after)
        return token

    def second_hop(gname, after):
        g = groups[gname]
        n = len(g["keys"])
        *g["hop"], token = _split_relay(gname + "_hop2", g["hop"][2], g["hop"][0], g["hop"][1], after,
                                        _forward_first_copies(n), n * 4, _forward_second_copies(n))
        return token

    def finish(gname, after):
        g = groups[gname]
        n = len(g["keys"])
        send_sems, recv_sems, bufs = g["hop"]
        if g["forward"]:
            send_sems, recv_sems, bufs, _ = _split_relay(gname + "_hop3", bufs, send_sems, recv_sems, after,
                                                         _forward_second_copies(n), n, _forward_third_copies(n))
            bufs = _split_wait(gname + "_wait", bufs, send_sems, recv_sems, after, _forward_third_copies(n))
        else:
            send_sems, recv_sems, bufs, _ = _split_relay(gname + "_relay", bufs, send_sems, recv_sems, after,
                                                         _gather_first_copies(n), n * (N_CHIP - 1), _gather_relay_copies(n))
            bufs = _split_wait(gname + "_wait", bufs, send_sems, recv_sems, after, _gather_relay_copies(n))
        g["done"] = dict(zip(g["keys"], bufs[n:]))

    token = first_hop("gather_in", mod_all)
    mod = mod + token[0:1, 0:1]

    def fetch(key, after):
        if key == "in":
            t = second_hop("gather_in", after)
            t = first_hop("gather_mixer", t)
            t = first_hop("gather_ffn_in", t)
            finish("gather_in", t)
        elif key == "fi_early":
            return first_hop("gather_ffn_out", second_hop("gather_ffn_in", after))
        elif "done" not in groups[group_of[key]]:
            finish(group_of[key], after)
        arr = groups[group_of[key]]["done"][key]
        return arr.reshape(-1, D) if key in ("out", "fo") else arr

    p = dict(norm1_g=norm1_g, b_gate=b_gate, ln_g=gmlp_ln_g, ln_b=gmlp_ln_b, ws=gmlp_ws[0], bs=gmlp_bs[0], hg_lb=hg_lb,
             hg_ng=hg_norm_g, norm2_g=norm2_g, final_g=final_norm_g.reshape(1, D))

    in_flight = {}
    c_idx = lax.axis_index("c").astype(jnp.int32).reshape(1)
    chip_idx = (2 * lax.axis_index("x") + lax.axis_index("y")).astype(jnp.int32).reshape(1)

    def scatter_start(name, grads):
        keys = list(grads)
        n = len(keys)
        stacks = [grads[k].reshape(N_DEV, -1, grads[k].shape[-1]) for k in keys]
        lands = [empty_hbm((N_CHIP, *g.shape[1:]), g.dtype) for g in stacks]
        send_sems, recv_sems, bufs, token = _split_start(name + "_d2d", stacks + lands, n * N_CHIP, _to_sibling_copies(n))
        in_flight[name] = dict(keys=keys, stage1=(send_sems, recv_sems, bufs))
        return token

    def scatter_push(name, after):
        f = in_flight[name]
        n = len(f["keys"])
        send_sems, recv_sems, bufs = f["stage1"]
        bufs = _split_wait(name + "_d2d_wait", bufs, send_sems, recv_sems, after, _to_sibling_copies(n))
        sums = [_chip_sum(f"{name}_sum_{k}", bufs[i], bufs[n + i], c_idx) for i, k in enumerate(f["keys"])]
        lands = [empty_hbm((N_CHIP - 1, *s.shape[1:]), s.dtype) for s in sums]
        send_sems, recv_sems, bufs, token = _split_start(name + "_ici", sums + lands, n * (N_CHIP - 1), _to_owner_copies(n))
        f["stage2"] = (send_sems, recv_sems, bufs)
        return token

    grad_x, _, small, dmod = _local_step(x[0], loss_target[0], mod, p, fetch, w_ffn_out.shape[1] * N_DEV, (scatter_start, scatter_push))

    small["b_ada"] = dmod
    packed, offs = _pack([small[k] for k in _SMALL] + [small["loss"]])
    sg_send, sg_recv, sg_bufs, t_tail = _split_start(
        "gather_small_start", [packed, lax.dynamic_update_slice(lax.empty((N_DEV, *packed.shape), F32), packed[None], (me, 0, 0))],
        N_DEV - 1, _small_gather_copies)
    t_tail = scatter_push("scatter_proj_in_b", t_tail)
    big_w = dict(w_in=(w_in, m_w_in, v_w_in, "w_in"), bg=(w_branch_gmlp, m_w_branch_gmlp, v_w_branch_gmlp, "w_branch_gmlp"),
                 bh=(w_branch_hg, m_w_branch_hg, v_w_branch_hg, "w_branch_hg"), out=(w_out, m_w_out, v_w_out, "w_out"),
                 fi=(w_ffn_in, m_w_ffn_in, v_w_ffn_in, "w_ffn_in"), fo=(w_ffn_out, m_w_ffn_out, v_w_ffn_out, "w_ffn_out"))
    upd = {}

    def land_and_update(name, after):
        keys = in_flight[name]["keys"]
        n = len(keys)
        send_sems, recv_sems, bufs = in_flight[name]["stage2"]
        bufs = _split_wait(name + "_ici_wait", bufs, send_sems, recv_sems, after, _to_owner_copies(n))
        for i, k in enumerate(keys):
            if k in big_w:
                wt, mt, vt, out_name = big_w[k]
                upd[out_name] = _adamw("adamw_" + out_name, wt[0], mt[0], vt[0], bufs[i], chip_idx, [bufs[n + i]])
            else:
                wt, mt, vt, out_name = big_w["w_in"]
                upd[out_name] = _adamw("adamw_" + k, wt[0], mt[0], vt[0], bufs[i], chip_idx, [bufs[n + i]],
                                       row0=0 if k == "w_in_a" else bufs[i].shape[1], into=upd.get(out_name))
            after = upd[out_name][1]
        return after

    after = land_and_update("scatter_mixer", land_and_update("scatter_ffn", t_tail))
    gathered = _split_wait("gather_small_wait", sg_bufs, sg_send, sg_recv, after, _small_gather_copies)[1]
    wp = dict(p, b_ada=b_ada)
    ms = dict(b_ada=m_b_ada, norm1_g=m_norm1_g, b_gate=m_b_gate, ln_g=m_gmlp_ln_g, ln_b=m_gmlp_ln_b, ws=m_gmlp_ws, bs=m_gmlp_bs,
              hg_lb=m_hg_lb, hg_ng=m_hg_norm_g, norm2_g=m_norm2_g, final_g=m_final_norm_g)
    vs = dict(b_ada=v_b_ada, norm1_g=v_norm1_g, b_gate=v_b_gate, ln_g=v_gmlp_ln_g, ln_b=v_gmlp_ln_b, ws=v_gmlp_ws, bs=v_gmlp_bs,
              hg_lb=v_hg_lb, hg_ng=v_hg_norm_g, norm2_g=v_norm2_g, final_g=v_final_norm_g)
    w_sm, _ = _pack([wp[k] for k in _SMALL])
    m_sm, _ = _pack([ms[k] for k in _SMALL])
    v_sm, _ = _pack([vs[k] for k in _SMALL])
    shapes = dict(b_ada=b_ada.shape, norm1_g=norm1_g.shape, b_gate=b_gate.shape, ln_g=gmlp_ln_g.shape, ln_b=gmlp_ln_b.shape,
                  ws=gmlp_ws.shape, bs=gmlp_bs.shape, hg_lb=hg_lb.shape, hg_ng=hg_norm_g.shape, norm2_g=norm2_g.shape,
                  final_g=final_norm_g.shape)
    sm_out = _small_update(gathered, w_sm, m_sm, v_sm, after, [math.prod(shapes[k]) // LANES for k in _SMALL])

    def unpack(idx, k):
        return sm_out[idx * len(_SMALL) + _SMALL.index(k)].reshape(shapes[k])

    loss = sm_out[-1][0, 0]

    dmod_all = gathered.reshape(N_DEV, -1)[:, offs[0]:offs[0] + N_DEV * ada_loc]
    dmod_loc = lax.dynamic_slice_in_dim(dmod_all, me * ada_loc, ada_loc, axis=1)
    ca_t = jnp.pad(c_act[:N_DEV].T, ((0, 0), (0, LANES - N_DEV))).astype(BF16)
    dm_p = jnp.pad(dmod_loc, ((0, LANES - N_DEV), (0, 0))).astype(BF16)
    tm_a = _tile(D, 512)
    g_ada = _matmul(
        "ada_dw", ca_t, dm_p, dims=_NN, grid_mnk=(D // tm_a, 1, 1), tiles=(tm_a, ada_loc),
        a_spec=pl.BlockSpec((tm_a, LANES), lambda i, j, k: (i, 0)), b_spec=pl.BlockSpec((LANES, ada_loc), lambda i, j, k: (0, 0)),
        out_shapes=[jax.ShapeDtypeStruct((1, D, ada_loc), F32)], out_specs=[pl.BlockSpec((None, tm_a, ada_loc), lambda i, j, k: (0, i, 0))],
        epilogue=_store(F32))[0]
    upd["w_ada"] = _adamw("adamw_w_ada", w_ada[0], m_w_ada[0], v_w_ada[0], g_ada, jnp.zeros((1,), jnp.int32))
    land_and_update("scatter_proj_in_b", land_and_update("scatter_proj_in_a", upd["w_ada"][1]))

    order = ("w_ada", "b_ada", "norm1_g", "w_in", "b_gate", "ln_g", "ln_b", "ws", "bs", "hg_lb", "hg_ng", "w_branch_gmlp", "w_branch_hg",
             "w_out", "norm2_g", "w_ffn_in", "w_ffn_out", "final_g")
    outs = [loss, grad_x[None]]
    for idx in range(4):
        for k in order:
            outs.append(upd[k][idx][None] if k in upd else unpack(idx, k))
    return tuple(outs)
```

```python
import functools
import math

import jax
import jax.numpy as jnp
from jax import lax
from jax.experimental import pallas as pl
from jax.experimental.pallas import tpu as pltpu

F32 = jnp.float32
BF16 = jnp.bfloat16
N_DEV = 8
EPS = 1e-6
LANES = 128
HG_DK = 128
HG_CHUNK = 64
HG_MID = HG_CHUNK // 2 - 1
EXP_CLAMP = 80.0
VMEM_LIMIT = 48 * 1024 * 1024
ADAM_LR, ADAM_B1, ADAM_B2, ADAM_EPS, ADAM_WD, ADAM_STEP = 0.001, 0.9, 0.999, 1e-08, 0.01, 10
MESH = pl.DeviceIdType.MESH

_NN = (((1,), (0,)), ((), ()))
_NT = (((1,), (1,)), ((), ()))
_TN = (((0,), (0,)), ((), ()))


def _dot(a, b, dims=_NN):
    return lax.dot_general(a.astype(BF16), b.astype(BF16), dims, preferred_element_type=F32)


def _tile(n, target, mult=LANES):
    best = None
    for t in range(mult, min(n, target) + 1, mult):
        if n % t == 0:
            best = t
    return n if best is None else best


def _cparams(sem):
    return pltpu.CompilerParams(dimension_semantics=sem, vmem_limit_bytes=VMEM_LIMIT)


def _sigmoid(x):
    return 1.0 / (1.0 + jnp.exp(-x))


def _gelu_parts(x):
    k0 = math.sqrt(2.0 / math.pi)
    x2 = x * x
    t = jnp.tanh(k0 * (x + 0.044715 * x * x2))
    g = 0.5 * x * (1.0 + t)
    dg = 0.5 * (1.0 + t) + 0.5 * x * (1.0 - t * t) * (k0 * (1.0 + 3.0 * 0.044715 * x2))
    return g, dg


def _split3(x):
    h = x.astype(BF16)
    r = x - h.astype(F32)
    m = r.astype(BF16)
    lo = (r - m.astype(F32)).astype(BF16)
    return h, m, lo


def _ones_dot(mat01, x):
    h, m, lo = _split3(x)
    d = functools.partial(lax.dot_general, dimension_numbers=_NN, preferred_element_type=F32)
    return d(mat01, h) + d(mat01, m) + d(mat01, lo)


def _matmul(name, a, b, *, dims, grid_mnk, tiles, a_spec, b_spec, extras=(), extra_specs=(), out_shapes, out_specs, epilogue, after=None,
            sem=None):
    gm, gn, nk = grid_mnk
    tm, tn = tiles
    n_ex, n_out = len(extras), len(out_shapes)
    held = [] if after is None else [after]

    def body(*refs):
        a_ref, b_ref = refs[0], refs[1]
        ex = refs[2:2 + n_ex]
        outs = refs[2 + n_ex + len(held):2 + n_ex + len(held) + n_out]
        more = () if sem is None else (pl.program_id(0) == 0,)
        if nk == 1:
            epilogue(lax.dot_general(a_ref[...], b_ref[...], dims, preferred_element_type=F32), ex, outs, *more)
            return
        acc = refs[-1]
        k = pl.program_id(2)

        @pl.when(k == 0)
        def _():
            acc[...] = jnp.zeros_like(acc)

        acc[...] += lax.dot_general(a_ref[...], b_ref[...], dims, preferred_element_type=F32)

        @pl.when(k == nk - 1)
        def _():
            epilogue(acc[...], ex, outs, *more)

    return pl.pallas_call(
        body, name=name, grid=(gm, gn, nk), in_specs=[a_spec, b_spec, *extra_specs] + [pl.BlockSpec(memory_space=pl.ANY)] * len(held),
        out_specs=list(out_specs), out_shape=list(out_shapes), scratch_shapes=[] if nk == 1 else [pltpu.VMEM((tm, tn), F32)],
        compiler_params=_cparams(sem or ("parallel", "parallel", "arbitrary")),
    )(a, b, *extras, *held)


def _store(dtype):
    def ep(acc, ex, outs):
        outs[0][...] = acc.astype(dtype)
    return ep


def _mm_nn_stacked(name, a, wg, *, tm, tn, tk, out_dtype=F32, extras=(), extra_specs=(), out_shapes=None, out_specs=None, epilogue=None,
                   after=None):
    M, K = a.shape
    _, _, nloc = wg.shape
    N = nloc * N_DEV
    q = nloc // tn
    if out_shapes is None:
        out_shapes = [jax.ShapeDtypeStruct((M, N), out_dtype)]
        out_specs = [pl.BlockSpec((tm, tn), lambda i, j, k: (i, j))]
        epilogue = _store(out_dtype)
    return _matmul(
        name, a, wg, dims=_NN, grid_mnk=(M // tm, N // tn, K // tk), tiles=(tm, tn),
        a_spec=pl.BlockSpec((tm, tk), lambda i, j, k: (i, k)),
        b_spec=pl.BlockSpec((None, tk, tn), lambda i, j, k: (j // q, k, j % q)),
        extras=extras, extra_specs=extra_specs, out_shapes=out_shapes, out_specs=out_specs, epilogue=epilogue, after=after)


def _mm_nn_shards(name, a, wg, slots, into=None, after=None):
    M, K = a.shape
    _, _, nloc = wg.shape
    tm = _tile(M, 1024, 16)
    held = ([] if into is None else [into]) + ([] if after is None else [after])

    def body(slots_ref, a_ref, w_ref, *rest):
        rest[-1][...] = lax.dot_general(a_ref[...], w_ref[...], _NN, preferred_element_type=F32)

    return pl.pallas_call(
        body, name=name,
        grid_spec=pltpu.PrefetchScalarGridSpec(
            num_scalar_prefetch=1, grid=(slots.shape[0], M // tm),
            in_specs=[pl.BlockSpec((tm, K), lambda b, i, s: (i, 0)), pl.BlockSpec((None, K, nloc), lambda b, i, s: (s[b], 0, 0))]
            + [pl.BlockSpec(memory_space=pl.ANY)] * len(held),
            out_specs=pl.BlockSpec((tm, nloc), lambda b, i, s: (i, s[b]))),
        out_shape=jax.ShapeDtypeStruct((M, nloc * N_DEV), F32), input_output_aliases={} if into is None else {3: 0},
        compiler_params=_cparams(("parallel", "parallel")))(slots, a, wg, *held)


def _mm_nt_stacked(name, a_spec, a, wg, *, M, tm, tn, tk, out_dtype=F32, after=None, extras=(), extra_specs=(), out_shapes=None,
                   out_specs=None, epilogue=None, sem=None):
    _, Kw, nloc = wg.shape
    q = nloc // tk
    single = out_shapes is None
    if single:
        out_shapes = [jax.ShapeDtypeStruct((M, Kw), out_dtype)]
        out_specs = [pl.BlockSpec((tm, tn), lambda i, j, k: (i, j))]
        epilogue = _store(out_dtype)
    res = _matmul(
        name, a, wg, dims=_NT, grid_mnk=(M // tm, Kw // tn, (nloc * N_DEV) // tk), tiles=(tm, tn),
        a_spec=a_spec, b_spec=pl.BlockSpec((None, tn, tk), lambda i, j, k: (k // q, j, k % q)),
        extras=extras, extra_specs=extra_specs, out_shapes=out_shapes, out_specs=out_specs, epilogue=epilogue, after=after, sem=sem)
    return res[0] if single else res


def _mm_tn(name, a, b, b_spec, *, Mo, No, S, tm, tn, tk, stacked_nloc=None, after=None, a_off=0):
    if stacked_nloc is None:
        out_shape = jax.ShapeDtypeStruct((Mo, No), BF16)
        out_spec = pl.BlockSpec((tm, tn), lambda i, j, k: (i, j))
    else:
        q = stacked_nloc // tn
        out_shape = jax.ShapeDtypeStruct((N_DEV, Mo, stacked_nloc), BF16)
        out_spec = pl.BlockSpec((None, tm, tn), lambda i, j, k: (j // q, i, j % q))
    return _matmul(
        name, a, b, dims=_TN, grid_mnk=(Mo // tm, No // tn, S // tk), tiles=(tm, tn),
        a_spec=pl.BlockSpec((tk, tm), lambda i, j, k: (k, i + a_off)), b_spec=b_spec,
        out_shapes=[out_shape], out_specs=[out_spec], epilogue=_store(BF16), after=after)[0]


def _norm_mod(name, x, g, sc, sh):
    S, D = x.shape
    tm = _tile(S, 256, 8)

    def body(x_ref, g_ref, sc_ref, sh_ref, h_ref):
        xv = x_ref[...]
        r = lax.rsqrt(jnp.mean(xv * xv, axis=-1, keepdims=True) + EPS)
        h = (xv * r) * g_ref[...]
        h_ref[...] = (h * (1.0 + sc_ref[...]) + sh_ref[...]).astype(BF16)

    row = pl.BlockSpec((tm, D), lambda i: (i, 0))
    vec = pl.BlockSpec((1, D), lambda i: (0, 0))
    return pl.pallas_call(body, name=name, grid=(S // tm,), in_specs=[row, vec, vec, vec], out_specs=row,
                          out_shape=jax.ShapeDtypeStruct((S, D), BF16), compiler_params=_cparams(("parallel",)))(x, g, sc, sh)


def _norm_mod_bwd_rows(first, dh_v, x_ref, g_ref, sc_ref, dres_ref, dx_ref, vec_ref, o_ref=None, gt_ref=None, do_ref=None):
    @pl.when(first)
    def _():
        vec_ref[...] = jnp.zeros_like(vec_ref)

    xv, gv = x_ref[...], g_ref[...]
    r = lax.rsqrt(jnp.mean(xv * xv, axis=-1, keepdims=True) + EPS)
    xn = xv * r
    one_sc = 1.0 + sc_ref[...]
    vec_ref[0:1, :] += jnp.sum(dh_v, axis=0, keepdims=True)
    vec_ref[1:2, :] += jnp.sum(dh_v * (xn * gv), axis=0, keepdims=True)
    vec_ref[2:3, :] += jnp.sum(dh_v * one_sc * xn, axis=0, keepdims=True)
    dxn = dh_v * one_sc * gv
    dx = dres_ref[...] + r * (dxn - xn * jnp.mean(dxn * xn, axis=-1, keepdims=True))
    dx_ref[...] = dx
    if o_ref is not None:
        vec_ref[3:4, :] += jnp.sum(dx * o_ref[...], axis=0, keepdims=True)
        do_ref[...] = (dx * gt_ref[...]).astype(BF16)


def _norm_mod_bwd(name, dh, x, g, sc, dres, o=None, gt=None):
    S, D = x.shape
    tm = _tile(S, 256, 8)
    gated = o is not None

    def body(*refs):
        if gated:
            dh_ref, x_ref, g_ref, sc_ref, dres_ref, o_ref, gt_ref, dx_ref, vec_ref, do_ref = refs
        else:
            dh_ref, x_ref, g_ref, sc_ref, dres_ref, dx_ref, vec_ref = refs
            o_ref = gt_ref = do_ref = None
        _norm_mod_bwd_rows(pl.program_id(0) == 0, dh_ref[...], x_ref, g_ref, sc_ref, dres_ref, dx_ref, vec_ref, o_ref, gt_ref, do_ref)

    row = pl.BlockSpec((tm, D), lambda i: (i, 0))
    vec = pl.BlockSpec((1, D), lambda i: (0, 0))
    acc = pl.BlockSpec((8, D), lambda i: (0, 0))
    ins = [dh, x, g, sc, dres] + ([o, gt] if gated else [])
    in_specs = [row, row, vec, vec, row] + ([row, vec] if gated else [])
    out_shape = [jax.ShapeDtypeStruct((S, D), F32), jax.ShapeDtypeStruct((8, D), F32)]
    out_specs = [row, acc]
    if gated:
        out_shape.append(jax.ShapeDtypeStruct((S, D), BF16))
        out_specs.append(row)
    return pl.pallas_call(body, name=name, grid=(S // tm,), in_specs=in_specs, out_specs=out_specs, out_shape=out_shape,
                          compiler_params=_cparams(("arbitrary",)))(*ins)


def _loss_head(x3, tgt, gf, o2, gt2):
    S, D = x3.shape
    tm = _tile(S, 256, 8)

    def body(x_ref, t_ref, g_ref, o_ref, gt_ref, dx_ref, do_ref, vec_ref):
        i = pl.program_id(0)

        @pl.when(i == 0)
        def _():
            vec_ref[...] = jnp.zeros_like(vec_ref)

        xv, gv = x_ref[...], g_ref[...]
        r = lax.rsqrt(jnp.mean(xv * xv, axis=-1, keepdims=True) + EPS)
        xn = xv * r
        e = xn * gv - t_ref[...]
        tok = 0.5 * jnp.mean(e * e, axis=-1, keepdims=True)
        vec_ref[0:1, :] += jnp.broadcast_to(jnp.sum(tok, axis=0, keepdims=True), (1, D))
        dy = e * (1.0 / D)
        vec_ref[1:2, :] += jnp.sum(dy * xn, axis=0, keepdims=True)
        dxn = dy * gv
        dx = r * (dxn - xn * jnp.mean(dxn * xn, axis=-1, keepdims=True))
        dx_ref[...] = dx
        vec_ref[2:3, :] += jnp.sum(dx * o_ref[...], axis=0, keepdims=True)
        do_ref[...] = (dx * gt_ref[...]).astype(BF16)

    row = pl.BlockSpec((tm, D), lambda i: (i, 0))
    vec = pl.BlockSpec((1, D), lambda i: (0, 0))
    return pl.pallas_call(
        body, name="loss_head", grid=(S // tm,), in_specs=[row, row, vec, row, vec],
        out_specs=[row, row, pl.BlockSpec((8, D), lambda i: (0, 0))],
        out_shape=[jax.ShapeDtypeStruct((S, D), F32), jax.ShapeDtypeStruct((S, D), BF16), jax.ShapeDtypeStruct((8, D), F32)],
        compiler_params=_cparams(("arbitrary",)))(x3, tgt, gf, o2, gt2)


def _ffn_in_swiglu(h, wg):
    S, D = h.shape
    _, _, tf = wg.shape
    nf = N_DEV // 2
    F = nf * tf
    tm = _tile(S, 256, 16)

    def body(h_ref, wa_ref, wu_ref, hf_ref, fac_ref):
        hv = h_ref[...]
        a = lax.dot_general(hv, wa_ref[...], _NN, preferred_element_type=F32)
        up = lax.dot_general(hv, wu_ref[...], _NN, preferred_element_type=F32)
        sa = _sigmoid(a)
        silu = a * sa
        hf_ref[...] = (silu * up).astype(BF16)
        fac_ref[0] = (up * (sa * (1.0 + a * (1.0 - sa)))).astype(BF16)
        fac_ref[1] = silu.astype(BF16)

    return pl.pallas_call(
        body, name="ffn_in_swiglu", grid=(nf, S // tm),
        in_specs=[pl.BlockSpec((tm, D), lambda j, i: (i, 0)), pl.BlockSpec((None, D, tf), lambda j, i: (j, 0, 0)),
                  pl.BlockSpec((None, D, tf), lambda j, i: (j + nf, 0, 0))],
        out_specs=[pl.BlockSpec((tm, tf), lambda j, i: (i, j)), pl.BlockSpec((2, tm, tf), lambda j, i: (0, i, j))],
        out_shape=[jax.ShapeDtypeStruct((S, F), BF16), jax.ShapeDtypeStruct((2, S, F), BF16)],
        compiler_params=_cparams(("parallel", "parallel")))(h, wg, wg)


def _colsum2(dg2):
    _, S, D = dg2.shape
    tm = _tile(S, 256, 16)

    def body(x_ref, o_ref):
        @pl.when(pl.program_id(0) == 0)
        def _():
            o_ref[...] = jnp.zeros_like(o_ref)

        o_ref[0:1, :] += jnp.sum(x_ref[0].astype(F32), axis=0, keepdims=True)
        o_ref[1:2, :] += jnp.sum(x_ref[1].astype(F32), axis=0, keepdims=True)

    return pl.pallas_call(body, name="gate_bias_grad", grid=(S // tm,), in_specs=[pl.BlockSpec((2, tm, D), lambda i: (0, i, 0))],
                          out_specs=pl.BlockSpec((2, D), lambda i: (0, 0)), out_shape=jax.ShapeDtypeStruct((2, D), F32),
                          compiler_params=_cparams(("arbitrary",)))(dg2)


def _gmlp_common(u_ref, v_ref, lg_ref, lb_ref, ws_ref, bsb_ref, G, T, Dg):
    ug, dug = _gelu_parts(u_ref[...])
    vg, dvg = _gelu_parts(v_ref[...])
    mu = jnp.mean(vg, axis=-1, keepdims=True)
    vc = vg - mu
    rstd = lax.rsqrt(jnp.mean(vc * vc, axis=-1, keepdims=True) + EPS)
    vhat = vc * rstd
    vn = vhat * lg_ref[...] + lb_ref[...]
    row = lax.broadcasted_iota(jnp.int32, (T, T), 0)
    col = lax.broadcasted_iota(jnp.int32, (T, T), 1)
    tril = row >= col
    s = []
    for g in range(G):
        w = jnp.where(tril, ws_ref[g], 0.0)
        s.append(_dot(w, vn[:, g * Dg:(g + 1) * Dg]) + bsb_ref[g])
    return ug, dug, dvg, rstd, vhat, vn, tril, s


def _gmlp_fwd(z, ln_g, ln_b, ws, bsb, GW):
    S = z.shape[0]
    G, T, _ = ws.shape
    Dg = GW // G

    def body(u_ref, v_ref, lg_ref, lb_ref, ws_ref, bsb_ref, ya_ref):
        ug, _, _, _, _, _, _, s = _gmlp_common(u_ref, v_ref, lg_ref, lb_ref, ws_ref, bsb_ref, G, T, Dg)
        for g in range(G):
            sl = slice(g * Dg, (g + 1) * Dg)
            ya_ref[:, sl] = (ug[:, sl] * s[g]).astype(BF16)

    vec = pl.BlockSpec((1, GW), lambda c: (0, 0))
    return pl.pallas_call(
        body, name="gmlp_fwd", grid=(S // T,),
        in_specs=[pl.BlockSpec((T, GW), lambda c: (c, 0)), pl.BlockSpec((T, GW), lambda c: (c, 1)), vec, vec,
                  pl.BlockSpec((G, T, T), lambda c: (0, 0, 0)), pl.BlockSpec((G, T, Dg), lambda c: (0, 0, 0))],
        out_specs=pl.BlockSpec((T, GW), lambda c: (c, 0)), out_shape=jax.ShapeDtypeStruct((S, GW), BF16),
        compiler_params=_cparams(("parallel",)))(z, z, ln_g, ln_b, ws, bsb)


def _gmlp_bwd(z, dya, ln_g, ln_b, ws, bsb, GW):
    S = z.shape[0]
    G, T, _ = ws.shape
    Dg = GW // G
    nc = S // T

    def body(u_ref, v_ref, dya_ref, lg_ref, lb_ref, ws_ref, bsb_ref, dz_ref, dln_ref, dws_ref, dbs_ref, dbs_acc, dvh):
        c = pl.program_id(0)

        @pl.when(c == 0)
        def _():
            dln_ref[...] = jnp.zeros_like(dln_ref)
            dws_ref[...] = jnp.zeros_like(dws_ref)
            dbs_acc[...] = jnp.zeros_like(dbs_acc)

        ug, dug, dvg, rstd, vhat, vn, tril, s = _gmlp_common(u_ref, v_ref, lg_ref, lb_ref, ws_ref, bsb_ref, G, T, Dg)
        dya_v = dya_ref[...]
        for g in range(G):
            sl = slice(g * Dg, (g + 1) * Dg)
            dy_g = dya_v[:, sl]
            dz_ref[:, sl] = (dy_g * s[g] * dug[:, sl]).astype(BF16)
            ds = dy_g * ug[:, sl]
            dbs_acc[g] += ds
            w = jnp.where(tril, ws_ref[g], 0.0)
            dvn_g = _dot(w, ds, _TN)
            dws_ref[g] += jnp.where(tril, _dot(ds, vn[:, sl], _NT), 0.0)
            dln_ref[0:1, sl] += jnp.sum(dvn_g * vhat[:, sl], axis=0, keepdims=True)
            dln_ref[1:2, sl] += jnp.sum(dvn_g, axis=0, keepdims=True)
            dvh[:, sl] = dvn_g * lg_ref[:, sl]
        dvhat = dvh[...]
        m1 = jnp.mean(dvhat, axis=-1, keepdims=True)
        m2 = jnp.mean(dvhat * vhat, axis=-1, keepdims=True)
        dz_ref[:, GW:2 * GW] = (rstd * (dvhat - m1 - vhat * m2) * dvg).astype(BF16)

        @pl.when(c == nc - 1)
        def _():
            for g in range(G):
                dbs_ref[g] = jnp.sum(dbs_acc[g], axis=-1, keepdims=True)

    vec = pl.BlockSpec((1, GW), lambda c: (0, 0))
    return pl.pallas_call(
        body, name="gmlp_bwd", grid=(nc,),
        in_specs=[pl.BlockSpec((T, GW), lambda c: (c, 0)), pl.BlockSpec((T, GW), lambda c: (c, 1)),
                  pl.BlockSpec((T, GW), lambda c: (c, 0)), vec, vec,
                  pl.BlockSpec((G, T, T), lambda c: (0, 0, 0)), pl.BlockSpec((G, T, Dg), lambda c: (0, 0, 0))],
        out_specs=[pl.BlockSpec((T, 2 * GW), lambda c: (c, 0)), pl.BlockSpec((8, GW), lambda c: (0, 0)),
                   pl.BlockSpec((G, T, T), lambda c: (0, 0, 0)), pl.BlockSpec((G, T, 1), lambda c: (0, 0, 0))],
        out_shape=[jax.ShapeDtypeStruct((S, 2 * GW), BF16), jax.ShapeDtypeStruct((8, GW), F32),
                   jax.ShapeDtypeStruct((G, T, T), F32), jax.ShapeDtypeStruct((G, T, 1), F32)],
        scratch_shapes=[pltpu.VMEM((G, T, Dg), F32), pltpu.VMEM((T, GW), F32)],
        compiler_params=_cparams(("arbitrary",)))(z, z, dya, ln_g, ln_b, ws, bsb)


def _hg_common(q_ref, f_ref, hlb_ref):
    C = HG_CHUNK
    a = hlb_ref[...]
    lb = _sigmoid(a[0:1, :] - a[1:2, :])
    sig = _sigmoid(f_ref[...])
    f = lb + (1.0 - lb) * sig
    lf = jnp.log(f)
    kk = 1.0 - f
    q = q_ref[...]
    sq = _sigmoid(q)
    qa = q * sq
    row = lax.broadcasted_iota(jnp.int32, (C, C), 0)
    col = lax.broadcasted_iota(jnp.int32, (C, C), 1)
    tril = row >= col
    b = _ones_dot(tril.astype(BF16), lf)
    bm = b[HG_MID:HG_MID + 1, :]
    bl = b[C - 1:C, :]
    e_b = jnp.exp(b)
    e_qm = jnp.exp(jnp.minimum(b - bm, EXP_CLAMP))
    e_km = jnp.exp(jnp.minimum(bm - b, EXP_CLAMP))
    e_kl = jnp.exp(bl - b)
    return dict(lb=lb, sig=sig, f=f, kk=kk, q=q, sq=sq, qa=qa, tril=tril, e_b=e_b, e_qm=e_qm, e_km=e_km, e_kl=e_kl,
                e_l=jnp.exp(bl), qh=qa * e_b, qt=qa * e_qm, kt=kk * e_km, kh=kk * e_kl)


def _hg_fwd(z, hg_lb, ng, HW):
    S = z.shape[0]
    C, H, dk = HG_CHUNK, HW // HG_DK, HG_DK
    nc = S // C

    def body(q_ref, f_ref, i_ref, og_ref, hlb_ref, ng_ref, yb_ref, o_ref, st_ref, state):
        @pl.when(pl.program_id(0) == 0)
        def _():
            state[...] = jnp.zeros_like(state)

        t = _hg_common(q_ref, f_ref, hlb_ref)
        iv = i_ref[...]
        for h in range(H):
            sl = slice(h * dk, (h + 1) * dk)
            st = state[h]
            st_ref[h] = st
            a = jnp.where(t["tril"], _dot(t["qt"][:, sl], t["kt"][:, sl], _NT), 0.0)
            o_h = _dot(a, iv[:, sl]) + _dot(t["qh"][:, sl], st, _NT)
            state[h] = st * t["e_l"][:, sl] + _dot(iv[:, sl], t["kh"][:, sl], _TN)
            o_ref[:, sl] = o_h
            rr = lax.rsqrt(jnp.mean(o_h * o_h, axis=-1, keepdims=True) + EPS)
            og = og_ref[:, sl]
            yb_ref[:, sl] = (o_h * rr * ng_ref[:, sl] * (og * _sigmoid(og))).astype(BF16)

    def col(k):
        return pl.BlockSpec((C, HW), lambda c: (c, k))

    base = 2
    return pl.pallas_call(
        body, name="hgrn_fwd", grid=(nc,),
        in_specs=[col(base), col(base + 1), col(base + 2), col(base + 3),
                  pl.BlockSpec((2, HW), lambda c: (0, 0)), pl.BlockSpec((1, HW), lambda c: (0, 0))],
        out_specs=[pl.BlockSpec((C, HW), lambda c: (c, 0)), pl.BlockSpec((C, HW), lambda c: (c, 0)),
                   pl.BlockSpec((None, H, dk, dk), lambda c: (c, 0, 0, 0))],
        out_shape=[jax.ShapeDtypeStruct((S, HW), BF16), jax.ShapeDtypeStruct((S, HW), F32),
                   jax.ShapeDtypeStruct((nc, H, dk, dk), F32)],
        scratch_shapes=[pltpu.VMEM((H, dk, dk), F32)],
        compiler_params=_cparams(("arbitrary",)))(z, z, z, z, hg_lb, ng)


def _hg_bwd(z, o, states, dyb, hg_lb, ng, HW, dz_head, dz_tail):
    S = z.shape[0]
    C, H, dk = HG_CHUNK, HW // HG_DK, HG_DK
    nc = S // C
    B0 = dz_head.shape[1]
    DT = dz_tail.shape[2]
    INW = B0 + 4 * HW + 2 * DT

    def body(q_ref, f_ref, i_ref, og_ref, o_ref, st_ref, stn_ref, dyb_ref, hlb_ref, ng_ref, head_ref, tail_ref,
             dzf_ref, dng_ref, dhlb_ref, dstate, cross, dqa_buf, dkk_buf, db_buf, dlb_acc):
        c = pl.program_id(0)
        dzf_ref[:, 0:B0] = head_ref[...]
        dzf_ref[:, B0 + 4 * HW:B0 + 4 * HW + DT] = tail_ref[0]
        dzf_ref[:, B0 + 4 * HW + DT:INW] = tail_ref[1]
        dz_ref = dzf_ref.at[:, B0:B0 + 4 * HW]

        @pl.when(c == 0)
        def _():
            dstate[...] = jnp.zeros_like(dstate)
            dlb_acc[...] = jnp.zeros_like(dlb_acc)
            dng_ref[...] = jnp.zeros_like(dng_ref)

        def r16(v):
            return v.astype(BF16).astype(F32)

        t = _hg_common(q_ref, f_ref, hlb_ref)
        iv = i_ref[...]
        for h in range(H):
            sl = slice(h * dk, (h + 1) * dk)
            o_h, og, dyb_h, ng_h = o_ref[:, sl], og_ref[:, sl], dyb_ref[:, sl], ng_ref[:, sl]
            sg = _sigmoid(og)
            silu_og = og * sg
            rr = lax.rsqrt(jnp.mean(o_h * o_h, axis=-1, keepdims=True) + EPS)
            on = o_h * rr
            dng_ref[0:1, sl] += jnp.sum(dyb_h * on * silu_og, axis=0, keepdims=True)
            dz_ref[:, 3 * HW + h * dk:3 * HW + (h + 1) * dk] = (dyb_h * on * ng_h * (sg * (1.0 + og * (1.0 - sg)))).astype(BF16)
            don = dyb_h * ng_h * silu_og
            do_h = rr * (don - on * jnp.mean(don * on, axis=-1, keepdims=True))

            qt, kt, qh, kh, iv_h = t["qt"][:, sl], t["kt"][:, sl], t["qh"][:, sl], t["kh"][:, sl], iv[:, sl]
            a = jnp.where(t["tril"], _dot(qt, kt, _NT), 0.0)
            da = jnp.where(t["tril"], _dot(do_h, iv_h, _NT), 0.0)
            st, dst = st_ref[h], dstate[h]
            cross[:, sl] = jnp.sum(dst * stn_ref[h], axis=0, keepdims=True)
            dqh = _dot(do_h, st)
            dstate[h] = _dot(do_h, qh, _TN) + dst * t["e_l"][:, sl]
            div = _dot(a, do_h, _TN) + _dot(kh, dst, _NT)
            dkh = _dot(iv_h, dst)
            dqt = _dot(da, kt)
            dkt = _dot(da, qt, _TN)
            dz_ref[:, 2 * HW + h * dk:2 * HW + (h + 1) * dk] = div.astype(BF16)
            dqa_buf[:, sl] = dqh * t["e_b"][:, sl] + dqt * t["e_qm"][:, sl]
            dkk_buf[:, sl] = dkt * t["e_km"][:, sl] + dkh * t["e_kl"][:, sl]
            db_buf[:, sl] = r16(qt) * dqt - r16(kt) * dkt + r16(qh) * dqh - r16(kh) * dkh

        dqa, dkk = dqa_buf[...], dkk_buf[...]
        triu = jnp.logical_not(t["tril"]) | (lax.broadcasted_iota(jnp.int32, (C, C), 0) == lax.broadcasted_iota(jnp.int32, (C, C), 1))
        dlf = _ones_dot(triu.astype(BF16), db_buf[...]) + cross[...]
        df = dlf / t["f"] - dkk
        sig, lb = t["sig"], t["lb"]
        dz_ref[:, HW:2 * HW] = (df * (1.0 - lb) * sig * (1.0 - sig)).astype(BF16)
        dlb_acc[...] += jnp.sum(df * (1.0 - sig), axis=0, keepdims=True)
        q, sq = t["q"], t["sq"]
        dz_ref[:, 0:HW] = (dqa * (sq * (1.0 + q * (1.0 - sq)))).astype(BF16)

        @pl.when(c == nc - 1)
        def _():
            da0 = dlb_acc[...] * lb * (1.0 - lb)
            dhlb_ref[0:1, :] = da0
            dhlb_ref[1:2, :] = -da0

    def col(k):
        return pl.BlockSpec((C, HW), lambda c: (nc - 1 - c, k))

    base = 2
    return pl.pallas_call(
        body, name="hgrn_bwd", grid=(nc,),
        in_specs=[col(base), col(base + 1), col(base + 2), col(base + 3), col(0),
                  pl.BlockSpec((None, H, dk, dk), lambda c: (nc - 1 - c, 0, 0, 0)),
                  pl.BlockSpec((None, H, dk, dk), lambda c: (jnp.minimum(nc - c, nc - 1), 0, 0, 0)), col(0),
                  pl.BlockSpec((2, HW), lambda c: (0, 0)), pl.BlockSpec((1, HW), lambda c: (0, 0)),
                  pl.BlockSpec((C, B0), lambda c: (nc - 1 - c, 0)), pl.BlockSpec((2, C, DT), lambda c: (0, nc - 1 - c, 0))],
        out_specs=[pl.BlockSpec((C, INW), lambda c: (nc - 1 - c, 0)), pl.BlockSpec((8, HW), lambda c: (0, 0)),
                   pl.BlockSpec((2, HW), lambda c: (0, 0))],
        out_shape=[jax.ShapeDtypeStruct((S, INW), BF16), jax.ShapeDtypeStruct((8, HW), F32), jax.ShapeDtypeStruct((2, HW), F32)],
        scratch_shapes=[pltpu.VMEM((H, dk, dk), F32), pltpu.VMEM((1, HW), F32), pltpu.VMEM((C, HW), F32), pltpu.VMEM((C, HW), F32),
                        pltpu.VMEM((C, HW), F32), pltpu.VMEM((1, HW), F32)],
        compiler_params=_cparams(("arbitrary",)))(z, z, z, z, o, states, states, dyb, hg_lb, ng, dz_head, dz_tail)


def _position():
    x, y, c = lax.axis_index("x"), lax.axis_index("y"), lax.axis_index("c")
    return x, y, c, 4 * x + 2 * y + c


def _flip(x, y, c, k):
    return (1 - x if k & 4 else x, 1 - y if k & 2 else y, 1 - c if k & 1 else c)


def _allgather_small(name, v):
    R, L = v.shape

    def body(v_ref, out_ref, send_sems, recv_sems):
        x, y, c, me = _position()
        out_ref[me] = v_ref[...]
        copies = []
        for k in range(1, N_DEV):
            cp = pltpu.make_async_remote_copy(src_ref=v_ref, dst_ref=out_ref.at[me], send_sem=send_sems.at[k - 1],
                                              recv_sem=recv_sems.at[k - 1], device_id=_flip(x, y, c, k), device_id_type=MESH)
            cp.start()
            copies.append(cp)
        for cp in copies:
            cp.wait()

    return pl.pallas_call(
        body, name=name, out_shape=jax.ShapeDtypeStruct((N_DEV, R, L), v.dtype),
        in_specs=[pl.BlockSpec(memory_space=pltpu.VMEM)], out_specs=pl.BlockSpec(memory_space=pltpu.VMEM),
        scratch_shapes=[pltpu.SemaphoreType.DMA((N_DEV - 1,)), pltpu.SemaphoreType.DMA((N_DEV - 1,))],
        compiler_params=pltpu.CompilerParams(vmem_limit_bytes=VMEM_LIMIT),
    )(v)


def _allgather_hbm(name, shards):
    n = len(shards)

    def body(*refs):
        ins, outs = refs[:n], refs[n:2 * n]
        send_sems, recv_sems, local_sems = refs[2 * n:]
        x, y, c, me = _position()
        sibling = (x, y, 1 - c)
        chips = [(1 - x, y), (x, 1 - y), (1 - x, 1 - y)]

        def slot(px, py, pc):
            return 4 * px + 2 * py + pc

        def copy(w, k, block, to, src=None):
            dst = outs[w].at[slot(*block)]
            return pltpu.make_async_remote_copy(src_ref=dst if src is None else src, dst_ref=dst, send_sem=send_sems.at[w, k],
                                                recv_sem=recv_sems.at[w, k], device_id=to, device_id_type=MESH)

        mine, first, passed = [], [], []
        for w in range(n):
            cp = pltpu.make_async_copy(ins[w], outs[w].at[me], local_sems.at[w])
            cp.start()
            mine.append(cp)
            for j, chip in enumerate(chips):
                first.append(copy(w, 1 + j, (x, y, c), (*chip, c), src=ins[w]))
            first.append(copy(w, 0, (x, y, c), sibling, src=ins[w]))
        for cp in first:
            cp.start()
        for w in range(n):
            for j, chip in enumerate(chips):
                copy(w, 1 + j, (*chip, c), (x, y, c)).wait_recv()
                cp = copy(w, 4 + j, (*chip, c), sibling)
                cp.start()
                passed.append(cp)
        for w in range(n):
            copy(w, 0, sibling, (x, y, c)).wait_recv()
            for j, chip in enumerate(chips):
                copy(w, 4 + j, (*chip, 1 - c), (x, y, c)).wait_recv()
        for cp in first + passed:
            cp.wait_send()
        for cp in mine:
            cp.wait()

    hbm = pl.BlockSpec(memory_space=pltpu.HBM)
    return pl.pallas_call(
        body, name=name, out_shape=[jax.ShapeDtypeStruct((N_DEV, *s.shape), s.dtype) for s in shards],
        in_specs=[hbm] * n, out_specs=[hbm] * n,
        scratch_shapes=[pltpu.SemaphoreType.DMA((n, 7)), pltpu.SemaphoreType.DMA((n, 7)), pltpu.SemaphoreType.DMA((n,))],
    )(*shards)


_HBM = pl.BlockSpec(memory_space=pltpu.HBM)
_SEM = pl.BlockSpec(memory_space=pltpu.SEMAPHORE)
_EFFECT = pltpu.SideEffectType.DATAFLOW_SIDE_EFFECTING


def _split_start(name, bufs, n_sems, copies_fn, after=None):
    nb = len(bufs)
    extra = [] if after is None else [after]
    k = nb + len(extra)

    def body(*refs):
        for cp in copies_fn(refs[:nb], refs[k], refs[k + 1]):
            cp.start()
        refs[-1][...] = jnp.zeros_like(refs[-1])

    sems = pltpu.SemaphoreType.DMA((n_sems,))
    res = pl.pallas_call(
        body, name=name,
        out_shape=(sems, sems, *[pltpu.HBM(a.shape, a.dtype) for a in bufs], jax.ShapeDtypeStruct((8, LANES), F32)),
        in_specs=[_HBM] * nb + [pl.BlockSpec(memory_space=pl.ANY)] * len(extra),
        out_specs=(_SEM, _SEM, *[_HBM] * nb, pl.BlockSpec(memory_space=pltpu.VMEM)),
        input_output_aliases={i: 2 + i for i in range(nb)},
        compiler_params=pltpu.CompilerParams(has_side_effects=_EFFECT),
    )(*[pltpu.with_memory_space_constraint(a, pltpu.HBM) for a in bufs], *extra)
    return res[0], res[1], list(res[2:2 + nb]), res[-1]


def _split_wait(name, bufs, send_sems, recv_sems, after, copies_fn):
    nb = len(bufs)

    def body(*refs):
        for cp in copies_fn(refs[:nb], refs[nb], refs[nb + 1]):
            cp.wait_send()
            cp.wait_recv()

    res = pl.pallas_call(
        body, name=name, out_shape=tuple(pltpu.HBM(a.shape, a.dtype) for a in bufs),
        in_specs=[_HBM] * nb + [_SEM, _SEM, pl.BlockSpec(memory_space=pl.ANY)], out_specs=tuple([_HBM] * nb),
        input_output_aliases={i: i for i in range(nb)},
        compiler_params=pltpu.CompilerParams(has_side_effects=_EFFECT),
    )(*bufs, send_sems, recv_sems, after)
    return list(res)


def _split_relay(name, bufs, send_sems, recv_sems, after, wait_fn, n_sems, start_fn):
    nb = len(bufs)

    def body(*refs):
        for cp in wait_fn(refs[:nb], refs[nb], refs[nb + 1]):
            cp.wait_send()
            cp.wait_recv()
        for cp in start_fn(refs[:nb], refs[nb + 3], refs[nb + 4]):
            cp.start()
        refs[-1][...] = jnp.zeros_like(refs[-1])

    sems = pltpu.SemaphoreType.DMA((n_sems,))
    res = pl.pallas_call(
        body, name=name, out_shape=(sems, sems, *[pltpu.HBM(a.shape, a.dtype) for a in bufs], jax.ShapeDtypeStruct((8, LANES), F32)),
        in_specs=[_HBM] * nb + [_SEM, _SEM, pl.BlockSpec(memory_space=pl.ANY)],
        out_specs=(_SEM, _SEM, *[_HBM] * nb, pl.BlockSpec(memory_space=pltpu.VMEM)),
        input_output_aliases={i: 2 + i for i in range(nb)},
        compiler_params=pltpu.CompilerParams(has_side_effects=_EFFECT),
    )(*bufs, send_sems, recv_sems, after)
    return res[0], res[1], list(res[2:2 + nb]), res[-1]


N_CHIP = 4


def _chip_flip(x, y, k):
    return (1 - x if k & 2 else x), (1 - y if k & 1 else y)


def _gather_first_copies(n):
    def copies(bufs, send_sems, recv_sems):
        x, y, c, me = _position()
        out = []
        for w in range(n):
            for k in range(N_CHIP):
                to = (x, y, 1 - c) if k == 0 else (*_chip_flip(x, y, k), c)
                out.append(pltpu.make_async_remote_copy(
                    src_ref=bufs[w], dst_ref=bufs[n + w].at[me], send_sem=send_sems.at[w * N_CHIP + k],
                    recv_sem=recv_sems.at[w * N_CHIP + k], device_id=to, device_id_type=MESH))
        return out
    return copies


def _gather_relay_copies(n):
    def copies(bufs, send_sems, recv_sems):
        x, y, c, _ = _position()
        out = []
        for w in range(n):
            for k in range(1, N_CHIP):
                px, py = _chip_flip(x, y, k)
                blk = bufs[n + w].at[4 * px + 2 * py + c]
                out.append(pltpu.make_async_remote_copy(
                    src_ref=blk, dst_ref=blk, send_sem=send_sems.at[w * (N_CHIP - 1) + k - 1],
                    recv_sem=recv_sems.at[w * (N_CHIP - 1) + k - 1], device_id=(x, y, 1 - c), device_id_type=MESH))
        return out
    return copies


def _small_gather_copies(bufs, send_sems, recv_sems):
    x, y, c, me = _position()
    return [pltpu.make_async_remote_copy(src_ref=bufs[0], dst_ref=bufs[1].at[me], send_sem=send_sems.at[k - 1], recv_sem=recv_sems.at[k - 1],
                                         device_id=_flip(x, y, c, k), device_id_type=MESH) for k in range(1, N_DEV)]


def _xor(a, b):
    return a + b - 2 * a * b


def _forward_first_copies(n):
    def copies(bufs, send_sems, recv_sems):
        x, y, c, me = _position()
        out = []
        for w in range(n):
            for k, to in enumerate([(x, y, 1 - c), (1 - x, y, c), (x, 1 - y, c)]):
                out.append(pltpu.make_async_remote_copy(
                    src_ref=bufs[w], dst_ref=bufs[n + w].at[me], send_sem=send_sems.at[w * 3 + k],
                    recv_sem=recv_sems.at[w * 3 + k], device_id=to, device_id_type=MESH))
        return out
    return copies


def _forward_second_copies(n):
    def copies(bufs, send_sems, recv_sems):
        x, y, c, _ = _position()
        out = []
        for w in range(n):
            half = bufs[n + w].shape[1] // 2
            for k, (src_chip, rows, to) in enumerate([((1 - x, y), pl.ds(0, half), (x, 1 - y, c)), ((x, 1 - y), pl.ds(half, half), (1 - x, y, c))]):
                blk = bufs[n + w].at[4 * src_chip[0] + 2 * src_chip[1] + c, rows]
                out.append(pltpu.make_async_remote_copy(src_ref=blk, dst_ref=blk, send_sem=send_sems.at[w * 4 + k],
                                                        recv_sem=recv_sems.at[w * 4 + k], device_id=to, device_id_type=MESH))
            for k, (px, py) in enumerate([(1 - x, y), (x, 1 - y)]):
                blk = bufs[n + w].at[4 * px + 2 * py + c]
                out.append(pltpu.make_async_remote_copy(src_ref=blk, dst_ref=blk, send_sem=send_sems.at[w * 4 + 2 + k],
                                                        recv_sem=recv_sems.at[w * 4 + 2 + k], device_id=(x, y, 1 - c), device_id_type=MESH))
        return out
    return copies


def _forward_third_copies(n):
    def copies(bufs, send_sems, recv_sems):
        x, y, c, _ = _position()
        out = []
        for w in range(n):
            blk = bufs[n + w].at[4 * (1 - x) + 2 * (1 - y) + c]
            out.append(pltpu.make_async_remote_copy(src_ref=blk, dst_ref=blk, send_sem=send_sems.at[w], recv_sem=recv_sems.at[w],
                                                    device_id=(x, y, 1 - c), device_id_type=MESH))
        return out
    return copies


def _to_sibling_copies(n):
    def copies(bufs, send_sems, recv_sems):
        x, y, c, _ = _position()
        out = []
        for w in range(n):
            for q in range(N_CHIP):
                out.append(pltpu.make_async_remote_copy(
                    src_ref=bufs[w].at[2 * q + 1 - c], dst_ref=bufs[n + w].at[q], send_sem=send_sems.at[w * N_CHIP + q],
                    recv_sem=recv_sems.at[w * N_CHIP + q], device_id=(x, y, 1 - c), device_id_type=MESH))
        return out
    return copies


def _to_owner_copies(n):
    def copies(bufs, send_sems, recv_sems):
        x, y, c, _ = _position()
        out = []
        for w in range(n):
            for k in range(1, N_CHIP):
                px, py = (1 - x if k & 2 else x), (1 - y if k & 1 else y)
                out.append(pltpu.make_async_remote_copy(
                    src_ref=bufs[w].at[2 * px + py], dst_ref=bufs[n + w].at[k - 1], send_sem=send_sems.at[w * (N_CHIP - 1) + k - 1],
                    recv_sem=recv_sems.at[w * (N_CHIP - 1) + k - 1], device_id=(px, py, c), device_id_type=MESH))
        return out
    return copies


def _chip_sum(name, stack, landed, c_idx):
    _, R, C = stack.shape
    tr = _tile(R, max(16, 1048576 // C), 16)

    def body(c_ref, a_ref, b_ref, o_ref):
        o_ref[...] = (a_ref[...].astype(F32) + b_ref[...].astype(F32)).astype(o_ref.dtype)

    return pl.pallas_call(
        body, name=name,
        grid_spec=pltpu.PrefetchScalarGridSpec(
            num_scalar_prefetch=1, grid=(N_CHIP, R // tr),
            in_specs=[pl.BlockSpec((None, tr, C), lambda q, i, c_ref: (2 * q + c_ref[0], i, 0)),
                      pl.BlockSpec((None, tr, C), lambda q, i, c_ref: (q, i, 0))],
            out_specs=pl.BlockSpec((None, tr, C), lambda q, i, c_ref: (q, i, 0))),
        out_shape=jax.ShapeDtypeStruct((N_CHIP, R, C), stack.dtype),
        compiler_params=_cparams(("parallel", "parallel")))(c_idx, stack, landed)


def _ada_mod(c16, w):
    _, D = c16.shape
    n = w.shape[1]
    tk = _tile(D, 512)
    nk = D // tk

    def body(c_ref, w_ref, o_ref, ca_ref):
        @pl.when(pl.program_id(0) == 0)
        def _():
            o_ref[...] = jnp.zeros_like(o_ref)

        cv = c_ref[...]
        ca = cv * _sigmoid(cv)
        ca_ref[...] = ca
        o_ref[...] += _dot(ca, w_ref[...])

    return pl.pallas_call(
        body, name="ada_mod", grid=(nk,),
        in_specs=[pl.BlockSpec((16, tk), lambda k: (0, k)), pl.BlockSpec((tk, n), lambda k: (k, 0))],
        out_specs=[pl.BlockSpec((16, n), lambda k: (0, 0)), pl.BlockSpec((16, tk), lambda k: (0, k))],
        out_shape=[jax.ShapeDtypeStruct((16, n), F32), jax.ShapeDtypeStruct((16, D), F32)],
        compiler_params=_cparams(("arbitrary",)))(c16, w)


def _cast_shard(name, wf, slot):
    r, c = wf.shape
    tr = _tile(r, max(16, 1048576 // c), 16)

    def body(slot_ref, w_ref, s_ref, g_ref):
        v = w_ref[...].astype(BF16)
        s_ref[...] = v
        g_ref[...] = v

    return pl.pallas_call(
        body, name=name,
        grid_spec=pltpu.PrefetchScalarGridSpec(
            num_scalar_prefetch=1, grid=(r // tr,), in_specs=[pl.BlockSpec((tr, c), lambda i, s: (i, 0))],
            out_specs=[pl.BlockSpec((tr, c), lambda i, s: (i, 0)), pl.BlockSpec((None, tr, c), lambda i, s: (s[0], i, 0))]),
        out_shape=[jax.ShapeDtypeStruct((r, c), BF16), jax.ShapeDtypeStruct((N_DEV, r, c), BF16)],
        compiler_params=_cparams(("parallel",)))(slot, wf)


def _adam_math(w, g, m, v):
    m2 = ADAM_B1 * m + (1.0 - ADAM_B1) * g
    v2 = ADAM_B2 * v + (1.0 - ADAM_B2) * (g * g)
    m_hat = m2 / (1.0 - ADAM_B1 ** ADAM_STEP)
    v_hat = v2 / (1.0 - ADAM_B2 ** ADAM_STEP)
    delta = -ADAM_LR * (m_hat / (jnp.sqrt(v_hat) + ADAM_EPS) + ADAM_WD * w)
    return delta, m2, v2


def _adamw(name, w, m, v, own, own_slot, parts=(), row0=0, into=None):
    R, C = w.shape
    Rp = own.shape[1]
    tr = _tile(Rp, max(16, 393216 // C), 16)
    off = row0 // tr
    n_p = len(parts)
    held = [] if into is None else list(into)

    def body(slot_ref, *refs):
        w_ref, m_ref, v_ref, own_ref = refs[:4]
        g_ref, d_ref, m2_ref, v2_ref = refs[4 + n_p + len(held):]
        g = own_ref[...].astype(F32)
        for p_ref in refs[4:4 + n_p]:
            for s in range(p_ref.shape[0]):
                g = g + p_ref[s].astype(F32)
        delta, m2, v2 = _adam_math(w_ref[...], g, m_ref[...], v_ref[...])
        g_ref[...] = g
        d_ref[...] = delta
        m2_ref[...] = m2
        v2_ref[...] = v2

    blk = pl.BlockSpec((tr, C), lambda i, s: (i + off, 0))
    out = jax.ShapeDtypeStruct((R, C), F32)
    return pl.pallas_call(
        body, name=name,
        grid_spec=pltpu.PrefetchScalarGridSpec(
            num_scalar_prefetch=1, grid=(Rp // tr,),
            in_specs=[blk, blk, blk, pl.BlockSpec((None, tr, C), lambda i, s: (s[0], i, 0))]
            + [pl.BlockSpec((a.shape[0], tr, C), lambda i, s: (0, i, 0)) for a in parts]
            + [pl.BlockSpec(memory_space=pl.ANY)] * len(held),
            out_specs=[blk] * 4),
        out_shape=[out] * 4, input_output_aliases={5 + n_p + i: i for i in range(len(held))},
        compiler_params=_cparams(("parallel",)))(own_slot, w, m, v, own, *parts, *held)


def _small_update(gathered, w, m, v, after, rows):
    _, R, L = gathered.shape
    rs = w.shape[0]
    n = len(rows)
    assert all(r % 8 == 0 for r in rows) and sum(rows) <= rs and rs + 8 <= R

    def body(p_ref, w_ref, m_ref, v_ref, after_ref, *outs):
        g = p_ref[0]
        for p in range(1, N_DEV):
            g = g + p_ref[p]
        kinds = (g,) + _adam_math(w_ref[...], g[0:rs, :], m_ref[...], v_ref[...])
        at = 0
        for k, r in enumerate(rows):
            for idx, val in enumerate(kinds):
                outs[idx * n + k][...] = val[at:at + r, :]
            at += r
        outs[4 * n][...] = g[at:at + 8, :]

    vm = pl.BlockSpec(memory_space=pltpu.VMEM)
    shapes = [jax.ShapeDtypeStruct((r, L), F32) for _ in range(4) for r in rows] + [jax.ShapeDtypeStruct((8, L), F32)]
    return pl.pallas_call(body, name="small_update", in_specs=[vm] * 4 + [pl.BlockSpec(memory_space=pl.ANY)], out_specs=[vm] * len(shapes),
                          out_shape=shapes, compiler_params=pltpu.CompilerParams(vmem_limit_bytes=VMEM_LIMIT))(gathered, w, m, v, after)


class _Fetched(dict):
    def __init__(self, fetch):
        super().__init__()
        self.fetch = fetch

    def first(self, key, after):
        self[key] = self.fetch(key, after)
        return self[key]


def _local_step(x, tgt, mod, p, fetch, F, scatter=None):
    S, D = x.shape
    GW, HW = p["ln_g"].shape[1], p["hg_ng"].shape[1]
    G, T, _ = p["ws"].shape
    w = _Fetched(fetch)
    INW = 2 * GW + 4 * HW + 2 * D
    in_loc, br_loc, fi_loc = INW // N_DEV, D // N_DEV, 2 * F // N_DEV
    assert GW == HW and F % fi_loc == 0
    sh1, sc1, gt1, sh2, sc2, gt2 = (mod[:, k * D:(k + 1) * D] for k in range(6))
    bsb = jnp.broadcast_to(p["bs"][:, :, None], (G, T, GW // G))

    tm = _tile(S, 1024, 16)
    tmh = _tile(S, 512, 16)
    tn_in = _tile(in_loc, 1280)
    tn_d = _tile(D, 512)
    tn_br = _tile(br_loc, 512)
    tk_s = S
    tm_w = _tile(D, 1024)
    g_off = 2 * GW + 4 * HW

    h1 = _norm_mod("norm1", x, p["norm1_g"], sc1, sh1)
    early = w.first("in_early", h1)
    if early is None:
        z = _mm_nn_stacked("proj_in", h1, w.first("in", h1), tm=tm, tn=tn_in, tk=D)[0]
    else:
        partial, landed, rest, t_hops = early
        z = _mm_nn_shards("proj_in_landed", h1, partial, landed, after=t_hops)
        z = _mm_nn_shards("proj_in_rest", h1, w.first("in", z), rest, into=z)
    ya = _gmlp_fwd(z, p["ln_g"], p["ln_b"], p["ws"], bsb, GW)
    yb, o_hg, states = _hg_fwd(z, p["hg_lb"], p["hg_ng"], HW)
    flat = {k: jnp.swapaxes(w.first(k, yb), 0, 1).reshape(GW, D) for k in ("bg", "bh")}
    tn_f = _tile(D, 1024)
    pa = _matmul(
        "branch_gmlp", ya, flat["bg"], dims=_NN, grid_mnk=(S // tm, D // tn_f, 1), tiles=(tm, tn_f),
        a_spec=pl.BlockSpec((tm, GW), lambda i, j, k: (i, 0)), b_spec=pl.BlockSpec((GW, tn_f), lambda i, j, k: (0, j)),
        out_shapes=[jax.ShapeDtypeStruct((S, D), F32)], out_specs=[pl.BlockSpec((tm, tn_f), lambda i, j, k: (i, j))], epilogue=_store(F32))[0]
    t_fi = w.first("fi_early", pa)

    def gates(ga_ref, gb_ref, ba_ref, bb_ref):
        return _sigmoid(ga_ref[...] + ba_ref[...]), _sigmoid(gb_ref[...] + bb_ref[...])

    def gate_specs(tn_, tm_=tm):
        o1, o2 = g_off // tn_, (g_off + D) // tn_
        return [pl.BlockSpec((tm_, tn_), lambda i, j, k: (i, o1 + j)), pl.BlockSpec((tm_, tn_), lambda i, j, k: (i, o2 + j)),
                pl.BlockSpec((1, tn_), lambda i, j, k: (0, j)), pl.BlockSpec((1, tn_), lambda i, j, k: (0, D // tn_ + j))]

    def merge_ep(acc, ex, outs):
        ga, gb = gates(*ex[1:5])
        outs[0][...] = acc
        outs[1][...] = (ga * ex[0][...] + gb * acc).astype(BF16)

    tile_o = pl.BlockSpec((tmh, tn_f), lambda i, j, k: (i, j))
    pb, y = _matmul(
        "branch_hg_merge", yb, flat["bh"], dims=_NN, grid_mnk=(S // tmh, D // tn_f, 1), tiles=(tmh, tn_f),
        a_spec=pl.BlockSpec((tmh, HW), lambda i, j, k: (i, 0)), b_spec=pl.BlockSpec((HW, tn_f), lambda i, j, k: (0, j)),
        extras=[pa, z, z, p["b_gate"], p["b_gate"]], extra_specs=[tile_o, *gate_specs(tn_f, tmh)],
        out_shapes=[jax.ShapeDtypeStruct((S, D), F32), jax.ShapeDtypeStruct((S, D), BF16)], out_specs=[tile_o, tile_o],
        epilogue=merge_ep, after=t_fi)

    def resid_ep(acc, ex, outs):
        outs[0][...] = acc
        outs[1][...] = ex[0][...] + ex[1][...] * acc

    def resid_mm(name, a, b, res, gt, tm_):
        K = a.shape[1]
        t_o = pl.BlockSpec((tm_, tn_d), lambda i, j, k: (i, j))
        return _matmul(
            name, a, b, dims=_NN, grid_mnk=(S // tm_, D // tn_d, 1), tiles=(tm_, tn_d),
            a_spec=pl.BlockSpec((tm_, K), lambda i, j, k: (i, 0)), b_spec=pl.BlockSpec((K, tn_d), lambda i, j, k: (0, j)),
            extras=[res, gt], extra_specs=[t_o, pl.BlockSpec((1, tn_d), lambda i, j, k: (0, j))],
            out_shapes=[jax.ShapeDtypeStruct((S, D), F32)] * 2, out_specs=[t_o, t_o], epilogue=resid_ep)

    o1, xm = resid_mm("proj_out", y, w.first("out", z), x, gt1, tm)
    h2 = _norm_mod("norm2", xm, p["norm2_g"], sc2, sh2)
    hf, hf_fac = _ffn_in_swiglu(h2, w.first("fi", h2))
    o2, x3 = resid_mm("ffn_out", hf, w.first("fo", hf), xm, gt2, tmh)
    dx3, do2, vec_l = _loss_head(x3, tgt, p["final_g"], o2, gt2)

    nf = F // fi_loc

    def dswiglu_ep(acc, ex, outs):
        outs[0][0] = (acc * ex[0][0].astype(F32)).astype(BF16)
        outs[0][1] = (acc * ex[0][1].astype(F32)).astype(BF16)

    pair = pl.BlockSpec((2, tmh, fi_loc), lambda i, j, k: (0, i, j))
    dab = _matmul(
        "ffn_out_dx", do2, w["fo"], dims=_NT, grid_mnk=(S // tmh, nf, 1), tiles=(tmh, fi_loc),
        a_spec=pl.BlockSpec((tmh, D), lambda i, j, k: (i, 0)), b_spec=pl.BlockSpec((fi_loc, D), lambda i, j, k: (j, 0)),
        extras=[hf_fac], extra_specs=[pair], out_shapes=[jax.ShapeDtypeStruct((2, S, F), BF16)], out_specs=[pair],
        epilogue=dswiglu_ep)[0]
    start = (lambda name, grads: scatter[0](name, grads)) if scatter is not None else (lambda name, grads: None)
    push = (lambda name, after: scatter[1](name, after)) if scatter is not None else (lambda name, after: None)

    def zero(token):
        return 0.0 if token is None else token[0:1, 0:1]

    tm_f = _tile(F, 512)
    g_fo = _mm_tn("ffn_out_dw", hf, do2, pl.BlockSpec((tk_s, D), lambda i, j, k: (k, j)), Mo=F, No=D, S=S, tm=tm_f, tn=D, tk=tk_s)
    g_fi = _mm_tn("ffn_in_dw", h2, dab, pl.BlockSpec((None, tk_s, fi_loc), lambda i, j, k: (j // nf, k, j % nf)),
                  Mo=D, No=2 * F, S=S, tm=tm_w, tn=fi_loc, tk=tk_s, stacked_nloc=fi_loc, after=g_fo)
    t_ffn = start("scatter_ffn", dict(fo=g_fo, fi=g_fi))
    dh2 = _mm_nt_stacked("ffn_in_dx", pl.BlockSpec((None, tm, fi_loc), lambda i, j, k: (k // nf, i, k % nf)), dab, w["fi"],
                         M=S, tm=tm, tn=tm_w, tk=fi_loc, after=t_ffn)
    dxm, vec2, do1 = _norm_mod_bwd("norm2_bwd", dh2, xm, p["norm2_g"], sc2, dx3, o1, gt1)
    t_ffn = push("scatter_ffn", dxm)

    def dmerge_ep(acc, ex, outs):
        ga, gb = gates(*ex[2:6])
        outs[0][...] = (acc * ga).astype(BF16)
        outs[1][...] = (acc * gb).astype(BF16)
        outs[2][0] = (acc * ex[0][...] * ga * (1.0 - ga)).astype(BF16)
        outs[2][1] = (acc * ex[1][...] * gb * (1.0 - gb)).astype(BF16)

    t_o = pl.BlockSpec((tm, tn_d), lambda i, j, k: (i, j))
    dpa, dpb, dg2 = _matmul(
        "proj_out_dx", do1, w["out"], dims=_NT, grid_mnk=(S // tm, D // tn_d, 1), tiles=(tm, tn_d),
        a_spec=pl.BlockSpec((tm, D), lambda i, j, k: (i, 0)), b_spec=pl.BlockSpec((tn_d, D), lambda i, j, k: (j, 0)),
        extras=[pa, pb, z, z, p["b_gate"], p["b_gate"]], extra_specs=[t_o, t_o, *gate_specs(tn_d)],
        out_shapes=[jax.ShapeDtypeStruct((S, D), BF16), jax.ShapeDtypeStruct((S, D), BF16), jax.ShapeDtypeStruct((2, S, D), BF16)],
        out_specs=[t_o, t_o, pl.BlockSpec((2, tm, tn_d), lambda i, j, k: (0, i, j))], epilogue=dmerge_ep, after=t_ffn)
    g_out = _mm_tn("proj_out_dw", y, do1, pl.BlockSpec((tk_s, D), lambda i, j, k: (k, j)), Mo=D, No=D, S=S, tm=tn_d, tn=D, tk=tk_s)
    tn_g = _tile(GW, 512)
    b_br = pl.BlockSpec((tk_s, br_loc), lambda i, j, k: (k, j))
    g_bg = _mm_tn("branch_gmlp_dw", ya, dpa, b_br, Mo=GW, No=D, S=S, tm=tn_g, tn=br_loc, tk=tk_s, stacked_nloc=br_loc)
    g_bh = _mm_tn("branch_hg_dw", yb, dpb, b_br, Mo=HW, No=D, S=S, tm=tn_g, tn=br_loc, tk=tk_s, stacked_nloc=br_loc)
    t_mix = start("scatter_mixer", dict(out=g_out, bg=g_bg, bh=g_bh))
    def branch_dx(name, dp, w_flat):
        return _matmul(
            name, dp, w_flat, dims=_NT, grid_mnk=(S // tm, GW // tn_g, 1), tiles=(tm, tn_g),
            a_spec=pl.BlockSpec((tm, D), lambda i, j, k: (i, 0)), b_spec=pl.BlockSpec((tn_g, D), lambda i, j, k: (j, 0)),
            out_shapes=[jax.ShapeDtypeStruct((S, GW), F32)], out_specs=[pl.BlockSpec((tm, tn_g), lambda i, j, k: (i, j))],
            epilogue=_store(F32), after=t_mix)[0]

    dya = branch_dx("branch_gmlp_dx", dpa, flat["bg"])
    dyb = branch_dx("branch_hg_dx", dpb, flat["bh"])
    db_gate = _colsum2(dg2)
    dz_gmlp, dln, dws, dbs = _gmlp_bwd(z, dya, p["ln_g"], p["ln_b"], p["ws"], bsb, GW)
    t_mix = push("scatter_mixer", dz_gmlp)
    dz, dng, dhlb = _hg_bwd(z, o_hg, states, dyb, p["hg_lb"], p["hg_ng"] + zero(t_mix), HW, dz_gmlp, dg2)
    half = D // 2
    tm_h = _tile(half, 1024)
    g_in = []
    t_in = None
    for hname, h in (("a", 0), ("b", 1)):
        g_in.append(_mm_tn("proj_in_dw_" + hname, h1, dz, pl.BlockSpec((tk_s, in_loc), lambda i, j, k: (k, j)), Mo=half, No=INW, S=S,
                           tm=tm_h, tn=in_loc, tk=tk_s, stacked_nloc=in_loc, after=t_in, a_off=h * (half // tm_h)))
        t_in = start("scatter_proj_in_" + hname, {"w_in_" + hname: g_in[-1]})
    t_in = push("scatter_proj_in_a", t_in)
    dh1 = _mm_nt_stacked("proj_in_dx", pl.BlockSpec((tm, in_loc), lambda i, j, k: (i, k)), dz, w["in"], M=S, tm=tm, tn=tm_w, tk=in_loc,
                         after=t_in)
    dx, vec1 = _norm_mod_bwd("norm1_bwd", dh1, x, p["norm1_g"], sc1, dxm)

    dmod = jnp.concatenate([vec1[0:1], vec1[1:2], vec2[3:4], vec2[0:1], vec2[1:2], vec_l[2:3]], axis=1)
    small = dict(norm1_g=vec1[2:3], b_gate=db_gate.reshape(1, 2 * D), ln_g=dln[0:1], ln_b=dln[1:2], ws=dws, bs=dbs.reshape(G, T),
                 hg_lb=dhlb, hg_ng=dng[0:1], norm2_g=vec2[2:3], final_g=vec_l[1:2], loss=vec_l[0:1, 0:LANES])
    big = dict(w_in_a=g_in[0], w_in_b=g_in[1], bg=g_bg, bh=g_bh, out=g_out, fi=g_fi, fo=g_fo)
    return dx, big, small, dmod


_SMALL = ("b_ada", "norm1_g", "b_gate", "ln_g", "ln_b", "ws", "bs", "hg_lb", "hg_ng", "norm2_g", "final_g")


def _pack(parts, rows_mult=8):
    flat = [a.reshape(-1) for a in parts]
    offs, n = [], 0
    for a in flat:
        offs.append(n)
        n += a.shape[0]
    pad = (-n) % (LANES * rows_mult)
    if pad:
        flat.append(jnp.zeros((pad,), F32))
    return jnp.concatenate(flat).reshape(-1, LANES), offs


def kernel(x, c, w_ada, b_ada, norm1_g, w_in, b_gate, gmlp_ln_g, gmlp_ln_b, gmlp_ws, gmlp_bs, hg_lb, hg_norm_g, w_branch_gmlp, w_branch_hg, w_out, norm2_g, w_ffn_in, w_ffn_out, final_norm_g, loss_target, m_w_ada, m_b_ada, m_norm1_g, m_w_in, m_b_gate, m_gmlp_ln_g, m_gmlp_ln_b, m_gmlp_ws, m_gmlp_bs, m_hg_lb, m_hg_norm_g, m_w_branch_gmlp, m_w_branch_hg, m_w_out, m_norm2_g, m_w_ffn_in, m_w_ffn_out, m_final_norm_g, v_w_ada, v_b_ada, v_norm1_g, v_w_in, v_b_gate, v_gmlp_ln_g, v_gmlp_ln_b, v_gmlp_ws, v_gmlp_bs, v_hg_lb, v_hg_norm_g, v_w_branch_gmlp, v_w_branch_hg, v_w_out, v_norm2_g, v_w_ffn_in, v_w_ffn_out, v_final_norm_g):
    S, D = x.shape[1], x.shape[2]
    ada_loc = w_ada.shape[2]
    me = 4 * lax.axis_index("x") + 2 * lax.axis_index("y") + lax.axis_index("c")
    me_idx = me.astype(jnp.int32).reshape(1)

    c_all = _allgather_small("gather_c", c.reshape(D // LANES, LANES)).reshape(N_DEV, D)
    mod_cols, c_act = _ada_mod(jnp.pad(c_all, ((0, 16 - N_DEV), (0, 0))), w_ada[0])
    mod_all = _allgather_small("gather_mod", mod_cols[:N_DEV].reshape(-1, LANES)).reshape(N_DEV, N_DEV, ada_loc)
    mod = lax.dynamic_index_in_dim(mod_all, me, axis=1, keepdims=False).reshape(1, N_DEV * ada_loc) + b_ada

    def empty_hbm(shape, dtype):
        return pltpu.with_memory_space_constraint(lax.empty(shape, dtype), pltpu.HBM)

    groups = dict(gather_in=dict(keys=["in"], src=[w_in], forward=True),
                  gather_mixer=dict(keys=["bg", "bh", "out"], src=[w_branch_gmlp, w_branch_hg, w_out], forward=False),
                  gather_ffn_in=dict(keys=["fi"], src=[w_ffn_in], forward=True),
                  gather_ffn_out=dict(keys=["fo"], src=[w_ffn_out], forward=False))
    group_of = {k: gname for gname, g in groups.items() for k in g["keys"]}

    def first_hop(gname, after):
        g = groups[gname]
        n = len(g["keys"])
        cast = [_cast_shard(f"{gname}_cast_{k}", a[0], me_idx) for k, a in zip(g["keys"], g["src"])]
        shards, outs = [s for s, _ in cast], [o for _, o in cast]
        if g["forward"]:
            *g["hop"], token = _split_start(gname + "_hop1", shards + outs, n * 3, _forward_first_copies(n), after=after)
        else:
            *g["hop"], token = _split_start(gname + "_hop1", shards + outs, n * N_CHIP, _gather_first_copies(n), after=after)
        return token

    def second_hop(gname, after):
        g = groups[gname]
        n = len(g["keys"])
        *g["hop"], token = _split_relay(gname + "_hop2", g["hop"][2], g["hop"][0], g["hop"][1], after,
                                        _forward_first_copies(n), n * 4, _forward_second_copies(n))
        return token

    def finish(gname, after):
        g = groups[gname]
        n = len(g["keys"])
        send_sems, recv_sems, bufs = g["hop"]
        if g["forward"]:
            send_sems, recv_sems, bufs, _ = _split_relay(gname + "_hop3", bufs, send_sems, recv_sems, after,
                                                         _forward_second_copies(n), n, _forward_third_copies(n))
            bufs = _split_wait(gname + "_wait", bufs, send_sems, recv_sems, after, _forward_third_copies(n))
        else:
            send_sems, recv_sems, bufs, _ = _split_relay(gname + "_relay", bufs, send_sems, recv_sems, after,
                                                         _gather_first_copies(n), n * (N_CHIP - 1), _gather_relay_copies(n))
            bufs = _split_wait(gname + "_wait", bufs, send_sems, recv_sems, after, _gather_relay_copies(n))
        g["done"] = dict(zip(g["keys"], bufs[n:]))

    token = first_hop("gather_in", mod_all)
    mod = mod + token[0:1, 0:1]

    def fetch(key, after):
        if key == "in_early":
            t = second_hop("gather_in", after)
            t = first_hop("gather_mixer", t)
            t = first_hop("gather_ffn_in", t)
            x_, y_, c_ = lax.axis_index("x"), lax.axis_index("y"), lax.axis_index("c")
            landed = jnp.stack([me, me + 1 - 2 * c_, 4 * (1 - x_) + 2 * y_ + c_, 4 * x_ + 2 * (1 - y_) + c_]).astype(jnp.int32)
            rest = jnp.stack([4 * (1 - x_) + 2 * y_ + 1 - c_, 4 * x_ + 2 * (1 - y_) + 1 - c_, 4 * (1 - x_) + 2 * (1 - y_) + c_,
                              4 * (1 - x_) + 2 * (1 - y_) + 1 - c_]).astype(jnp.int32)
            return groups["gather_in"]["hop"][2][1], landed, rest, t
        if key == "in":
            finish("gather_in", after)
        elif key == "fi_early":
            return first_hop("gather_ffn_out", second_hop("gather_ffn_in", after))
        elif "done" not in groups[group_of[key]]:
            finish(group_of[key], after)
        arr = groups[group_of[key]]["done"][key]
        return arr.reshape(-1, D) if key in ("out", "fo") else arr

    p = dict(norm1_g=norm1_g, b_gate=b_gate, ln_g=gmlp_ln_g, ln_b=gmlp_ln_b, ws=gmlp_ws[0], bs=gmlp_bs[0], hg_lb=hg_lb,
             hg_ng=hg_norm_g, norm2_g=norm2_g, final_g=final_norm_g.reshape(1, D))

    in_flight = {}
    c_idx = lax.axis_index("c").astype(jnp.int32).reshape(1)
    chip_idx = (2 * lax.axis_index("x") + lax.axis_index("y")).astype(jnp.int32).reshape(1)

    def scatter_start(name, grads):
        keys = list(grads)
        n = len(keys)
        stacks = [grads[k].reshape(N_DEV, -1, grads[k].shape[-1]) for k in keys]
        lands = [empty_hbm((N_CHIP, *g.shape[1:]), g.dtype) for g in stacks]
        send_sems, recv_sems, bufs, token = _split_start(name + "_d2d", stacks + lands, n * N_CHIP, _to_sibling_copies(n))
        in_flight[name] = dict(keys=keys, stage1=(send_sems, recv_sems, bufs))
        return token

    def scatter_push(name, after):
        f = in_flight[name]
        n = len(f["keys"])
        send_sems, recv_sems, bufs = f["stage1"]
        bufs = _split_wait(name + "_d2d_wait", bufs, send_sems, recv_sems, after, _to_sibling_copies(n))
        sums = [_chip_sum(f"{name}_sum_{k}", bufs[i], bufs[n + i], c_idx) for i, k in enumerate(f["keys"])]
        lands = [empty_hbm((N_CHIP - 1, *s.shape[1:]), s.dtype) for s in sums]
        send_sems, recv_sems, bufs, token = _split_start(name + "_ici", sums + lands, n * (N_CHIP - 1), _to_owner_copies(n))
        f["stage2"] = (send_sems, recv_sems, bufs)
        return token

    grad_x, _, small, dmod = _local_step(x[0], loss_target[0], mod, p, fetch, w_ffn_out.shape[1] * N_DEV, (scatter_start, scatter_push))

    small["b_ada"] = dmod
    packed, offs = _pack([small[k] for k in _SMALL] + [small["loss"]])
    sg_send, sg_recv, sg_bufs, t_tail = _split_start(
        "gather_small_start", [packed, lax.dynamic_update_slice(lax.empty((N_DEV, *packed.shape), F32), packed[None], (me, 0, 0))],
        N_DEV - 1, _small_gather_copies)
    t_tail = scatter_push("scatter_proj_in_b", t_tail)
    big_w = dict(w_in=(w_in, m_w_in, v_w_in, "w_in"), bg=(w_branch_gmlp, m_w_branch_gmlp, v_w_branch_gmlp, "w_branch_gmlp"),
                 bh=(w_branch_hg, m_w_branch_hg, v_w_branch_hg, "w_branch_hg"), out=(w_out, m_w_out, v_w_out, "w_out"),
                 fi=(w_ffn_in, m_w_ffn_in, v_w_ffn_in, "w_ffn_in"), fo=(w_ffn_out, m_w_ffn_out, v_w_ffn_out, "w_ffn_out"))
    upd = {}

    def land_and_update(name, after):
        keys = in_flight[name]["keys"]
        n = len(keys)
        send_sems, recv_sems, bufs = in_flight[name]["stage2"]
        bufs = _split_wait(name + "_ici_wait", bufs, send_sems, recv_sems, after, _to_owner_copies(n))
        for i, k in enumerate(keys):
            if k in big_w:
                wt, mt, vt, out_name = big_w[k]
                upd[out_name] = _adamw("adamw_" + out_name, wt[0], mt[0], vt[0], bufs[i], chip_idx, [bufs[n + i]])
            else:
                wt, mt, vt, out_name = big_w["w_in"]
                upd[out_name] = _adamw("adamw_" + k, wt[0], mt[0], vt[0], bufs[i], chip_idx, [bufs[n + i]],
                                       row0=0 if k == "w_in_a" else bufs[i].shape[1], into=upd.get(out_name))
            after = upd[out_name][1]
        return after

    after = land_and_update("scatter_mixer", land_and_update("scatter_ffn", t_tail))
    gathered = _split_wait("gather_small_wait", sg_bufs, sg_send, sg_recv, after, _small_gather_copies)[1]
    wp = dict(p, b_ada=b_ada)
    ms = dict(b_ada=m_b_ada, norm1_g=m_norm1_g, b_gate=m_b_gate, ln_g=m_gmlp_ln_g, ln_b=m_gmlp_ln_b, ws=m_gmlp_ws, bs=m_gmlp_bs,
              hg_lb=m_hg_lb, hg_ng=m_hg_norm_g, norm2_g=m_norm2_g, final_g=m_final_norm_g)
    vs = dict(b_ada=v_b_ada, norm1_g=v_norm1_g, b_gate=v_b_gate, ln_g=v_gmlp_ln_g, ln_b=v_gmlp_ln_b, ws=v_gmlp_ws, bs=v_gmlp_bs,
              hg_lb=v_hg_lb, hg_ng=v_hg_norm_g, norm2_g=v_norm2_g, final_g=v_final_norm_g)
    w_sm, _ = _pack([wp[k] for k in _SMALL])
    m_sm, _ = _pack([ms[k] for k in _SMALL])
    v_sm, _ = _pack([vs[k] for k in _SMALL])
    shapes = dict(b_ada=b_ada.shape, norm1_g=norm1_g.shape, b_gate=b_gate.shape, ln_g=gmlp_ln_g.shape, ln_b=gmlp_ln_b.shape,
                  ws=gmlp_ws.shape, bs=gmlp_bs.shape, hg_lb=hg_lb.shape, hg_ng=hg_norm_g.shape, norm2_g=norm2_g.shape,
                  final_g=final_norm_g.shape)
    sm_out = _small_update(gathered, w_sm, m_sm, v_sm, after, [math.prod(shapes[k]) // LANES for k in _SMALL])

    def unpack(idx, k):
        return sm_out[idx * len(_SMALL) + _SMALL.index(k)].reshape(shapes[k])

    loss = sm_out[-1][0, 0]

    dmod_all = gathered.reshape(N_DEV, -1)[:, offs[0]:offs[0] + N_DEV * ada_loc]
    dmod_loc = lax.dynamic_slice_in_dim(dmod_all, me * ada_loc, ada_loc, axis=1)
    ca_t = jnp.pad(c_act[:N_DEV].T, ((0, 0), (0, LANES - N_DEV))).astype(BF16)
    dm_p = jnp.pad(dmod_loc, ((0, LANES - N_DEV), (0, 0))).astype(BF16)
    tm_a = _tile(D, 512)
    g_ada = _matmul(
        "ada_dw", ca_t, dm_p, dims=_NN, grid_mnk=(D // tm_a, 1, 1), tiles=(tm_a, ada_loc),
        a_spec=pl.BlockSpec((tm_a, LANES), lambda i, j, k: (i, 0)), b_spec=pl.BlockSpec((LANES, ada_loc), lambda i, j, k: (0, 0)),
        out_shapes=[jax.ShapeDtypeStruct((1, D, ada_loc), F32)], out_specs=[pl.BlockSpec((None, tm_a, ada_loc), lambda i, j, k: (0, i, 0))],
        epilogue=_store(F32))[0]
    upd["w_ada"] = _adamw("adamw_w_ada", w_ada[0], m_w_ada[0], v_w_ada[0], g_ada, jnp.zeros((1,), jnp.int32))
    land_and_update("scatter_proj_in_b", land_and_update("scatter_proj_in_a", upd["w_ada"][1]))

    order = ("w_ada", "b_ada", "norm1_g", "w_in", "b_gate", "ln_g", "ln_b", "ws", "bs", "hg_lb", "hg_ng", "w_branch_gmlp", "w_branch_hg",
             "w_out", "norm2_g", "w_ffn_in", "w_ffn_out", "final_g")
    outs = [loss, grad_x[None]]
    for idx in range(4):
        for k in order:
            outs.append(upd[k][idx][None] if k in upd else unpack(idx, k))
    return tuple(outs)
```

```python
import functools
import math

import jax
import jax.numpy as jnp
from jax import lax
from jax.experimental import pallas as pl
from jax.experimental.pallas import tpu as pltpu

F32 = jnp.float32
BF16 = jnp.bfloat16
N_DEV = 8
EPS = 1e-6
LANES = 128
HG_DK = 128
HG_CHUNK = 64
HG_MID = HG_CHUNK // 2 - 1
EXP_CLAMP = 80.0
VMEM_LIMIT = 48 * 1024 * 1024
ADAM_LR, ADAM_B1, ADAM_B2, ADAM_EPS, ADAM_WD, ADAM_STEP = 0.001, 0.9, 0.999, 1e-08, 0.01, 10
MESH = pl.DeviceIdType.MESH

_NN = (((1,), (0,)), ((), ()))
_NT = (((1,), (1,)), ((), ()))
_TN = (((0,), (0,)), ((), ()))


def _dot(a, b, dims=_NN):
    return lax.dot_general(a.astype(BF16), b.astype(BF16), dims, preferred_element_type=F32)


def _tile(n, target, mult=LANES):
    best = None
    for t in range(mult, min(n, target) + 1, mult):
        if n % t == 0:
            best = t
    return n if best is None else best


def _cparams(sem):
    return pltpu.CompilerParams(dimension_semantics=sem, vmem_limit_bytes=VMEM_LIMIT)


def _sigmoid(x):
    return 1.0 / (1.0 + jnp.exp(-x))


def _gelu_parts(x):
    k0 = math.sqrt(2.0 / math.pi)
    x2 = x * x
    t = jnp.tanh(k0 * (x + 0.044715 * x * x2))
    g = 0.5 * x * (1.0 + t)
    dg = 0.5 * (1.0 + t) + 0.5 * x * (1.0 - t * t) * (k0 * (1.0 + 3.0 * 0.044715 * x2))
    return g, dg


def _split3(x):
    h = x.astype(BF16)
    r = x - h.astype(F32)
    m = r.astype(BF16)
    lo = (r - m.astype(F32)).astype(BF16)
    return h, m, lo


def _ones_dot(mat01, x):
    h, m, lo = _split3(x)
    d = functools.partial(lax.dot_general, dimension_numbers=_NN, preferred_element_type=F32)
    return d(mat01, h) + d(mat01, m) + d(mat01, lo)


def _matmul(name, a, b, *, dims, grid_mnk, tiles, a_spec, b_spec, extras=(), extra_specs=(), out_shapes, out_specs, epilogue, after=None,
            sem=None):
    gm, gn, nk = grid_mnk
    tm, tn = tiles
    n_ex, n_out = len(extras), len(out_shapes)
    held = [] if after is None else [after]

    def body(*refs):
        a_ref, b_ref = refs[0], refs[1]
        ex = refs[2:2 + n_ex]
        outs = refs[2 + n_ex + len(held):2 + n_ex + len(held) + n_out]
        more = () if sem is None else (pl.program_id(0) == 0,)
        if nk == 1:
            epilogue(lax.dot_general(a_ref[...], b_ref[...], dims, preferred_element_type=F32), ex, outs, *more)
            return
        acc = refs[-1]
        k = pl.program_id(2)

        @pl.when(k == 0)
        def _():
            acc[...] = jnp.zeros_like(acc)

        acc[...] += lax.dot_general(a_ref[...], b_ref[...], dims, preferred_element_type=F32)

        @pl.when(k == nk - 1)
        def _():
            epilogue(acc[...], ex, outs, *more)

    return pl.pallas_call(
        body, name=name, grid=(gm, gn, nk), in_specs=[a_spec, b_spec, *extra_specs] + [pl.BlockSpec(memory_space=pl.ANY)] * len(held),
        out_specs=list(out_specs), out_shape=list(out_shapes), scratch_shapes=[] if nk == 1 else [pltpu.VMEM((tm, tn), F32)],
        compiler_params=_cparams(sem or ("parallel", "parallel", "arbitrary")),
    )(a, b, *extras, *held)


def _store(dtype):
    def ep(acc, ex, outs):
        outs[0][...] = acc.astype(dtype)
    return ep


def _mm_nn_stacked(name, a, wg, *, tm, tn, tk, out_dtype=F32, extras=(), extra_specs=(), out_shapes=None, out_specs=None, epilogue=None,
                   after=None):
    M, K = a.shape
    _, _, nloc = wg.shape
    N = nloc * N_DEV
    q = nloc // tn
    if out_shapes is None:
        out_shapes = [jax.ShapeDtypeStruct((M, N), out_dtype)]
        out_specs = [pl.BlockSpec((tm, tn), lambda i, j, k: (i, j))]
        epilogue = _store(out_dtype)
    return _matmul(
        name, a, wg, dims=_NN, grid_mnk=(M // tm, N // tn, K // tk), tiles=(tm, tn),
        a_spec=pl.BlockSpec((tm, tk), lambda i, j, k: (i, k)),
        b_spec=pl.BlockSpec((None, tk, tn), lambda i, j, k: (j // q, k, j % q)),
        extras=extras, extra_specs=extra_specs, out_shapes=out_shapes, out_specs=out_specs, epilogue=epilogue, after=after)


def _mm_nt_stacked(name, a_spec, a, wg, *, M, tm, tn, tk, out_dtype=F32, after=None, extras=(), extra_specs=(), out_shapes=None,
                   out_specs=None, epilogue=None, sem=None):
    _, Kw, nloc = wg.shape
    q = nloc // tk
    single = out_shapes is None
    if single:
        out_shapes = [jax.ShapeDtypeStruct((M, Kw), out_dtype)]
        out_specs = [pl.BlockSpec((tm, tn), lambda i, j, k: (i, j))]
        epilogue = _store(out_dtype)
    res = _matmul(
        name, a, wg, dims=_NT, grid_mnk=(M // tm, Kw // tn, (nloc * N_DEV) // tk), tiles=(tm, tn),
        a_spec=a_spec, b_spec=pl.BlockSpec((None, tn, tk), lambda i, j, k: (k // q, j, k % q)),
        extras=extras, extra_specs=extra_specs, out_shapes=out_shapes, out_specs=out_specs, epilogue=epilogue, after=after, sem=sem)
    return res[0] if single else res


def _mm_tn(name, a, b, b_spec, *, Mo, No, S, tm, tn, tk, stacked_nloc=None, after=None, a_off=0):
    if stacked_nloc is None:
        out_shape = jax.ShapeDtypeStruct((Mo, No), BF16)
        out_spec = pl.BlockSpec((tm, tn), lambda i, j, k: (i, j))
    else:
        q = stacked_nloc // tn
        out_shape = jax.ShapeDtypeStruct((N_DEV, Mo, stacked_nloc), BF16)
        out_spec = pl.BlockSpec((None, tm, tn), lambda i, j, k: (j // q, i, j % q))
    return _matmul(
        name, a, b, dims=_TN, grid_mnk=(Mo // tm, No // tn, S // tk), tiles=(tm, tn),
        a_spec=pl.BlockSpec((tk, tm), lambda i, j, k: (k, i + a_off)), b_spec=b_spec,
        out_shapes=[out_shape], out_specs=[out_spec], epilogue=_store(BF16), after=after)[0]


def _norm_mod(name, x, g, sc, sh):
    S, D = x.shape
    tm = _tile(S, 256, 8)

    def body(x_ref, g_ref, sc_ref, sh_ref, h_ref):
        xv = x_ref[...]
        r = lax.rsqrt(jnp.mean(xv * xv, axis=-1, keepdims=True) + EPS)
        h = (xv * r) * g_ref[...]
        h_ref[...] = (h * (1.0 + sc_ref[...]) + sh_ref[...]).astype(BF16)

    row = pl.BlockSpec((tm, D), lambda i: (i, 0))
    vec = pl.BlockSpec((1, D), lambda i: (0, 0))
    return pl.pallas_call(body, name=name, grid=(S // tm,), in_specs=[row, vec, vec, vec], out_specs=row,
                          out_shape=jax.ShapeDtypeStruct((S, D), BF16), compiler_params=_cparams(("parallel",)))(x, g, sc, sh)


def _norm_mod_bwd_rows(first, dh_v, x_ref, g_ref, sc_ref, dres_ref, dx_ref, vec_ref, o_ref=None, gt_ref=None, do_ref=None):
    @pl.when(first)
    def _():
        vec_ref[...] = jnp.zeros_like(vec_ref)

    xv, gv = x_ref[...], g_ref[...]
    r = lax.rsqrt(jnp.mean(xv * xv, axis=-1, keepdims=True) + EPS)
    xn = xv * r
    one_sc = 1.0 + sc_ref[...]
    vec_ref[0:1, :] += jnp.sum(dh_v, axis=0, keepdims=True)
    vec_ref[1:2, :] += jnp.sum(dh_v * (xn * gv), axis=0, keepdims=True)
    vec_ref[2:3, :] += jnp.sum(dh_v * one_sc * xn, axis=0, keepdims=True)
    dxn = dh_v * one_sc * gv
    dx = dres_ref[...] + r * (dxn - xn * jnp.mean(dxn * xn, axis=-1, keepdims=True))
    dx_ref[...] = dx
    if o_ref is not None:
        vec_ref[3:4, :] += jnp.sum(dx * o_ref[...], axis=0, keepdims=True)
        do_ref[...] = (dx * gt_ref[...]).astype(BF16)


def _norm_mod_bwd(name, dh, x, g, sc, dres, o=None, gt=None):
    S, D = x.shape
    tm = _tile(S, 256, 8)
    gated = o is not None

    def body(*refs):
        if gated:
            dh_ref, x_ref, g_ref, sc_ref, dres_ref, o_ref, gt_ref, dx_ref, vec_ref, do_ref = refs
        else:
            dh_ref, x_ref, g_ref, sc_ref, dres_ref, dx_ref, vec_ref = refs
            o_ref = gt_ref = do_ref = None
        _norm_mod_bwd_rows(pl.program_id(0) == 0, dh_ref[...], x_ref, g_ref, sc_ref, dres_ref, dx_ref, vec_ref, o_ref, gt_ref, do_ref)

    row = pl.BlockSpec((tm, D), lambda i: (i, 0))
    vec = pl.BlockSpec((1, D), lambda i: (0, 0))
    acc = pl.BlockSpec((8, D), lambda i: (0, 0))
    ins = [dh, x, g, sc, dres] + ([o, gt] if gated else [])
    in_specs = [row, row, vec, vec, row] + ([row, vec] if gated else [])
    out_shape = [jax.ShapeDtypeStruct((S, D), F32), jax.ShapeDtypeStruct((8, D), F32)]
    out_specs = [row, acc]
    if gated:
        out_shape.append(jax.ShapeDtypeStruct((S, D), BF16))
        out_specs.append(row)
    return pl.pallas_call(body, name=name, grid=(S // tm,), in_specs=in_specs, out_specs=out_specs, out_shape=out_shape,
                          compiler_params=_cparams(("arbitrary",)))(*ins)


def _loss_head(x3, tgt, gf, o2, gt2):
    S, D = x3.shape
    tm = _tile(S, 256, 8)

    def body(x_ref, t_ref, g_ref, o_ref, gt_ref, dx_ref, do_ref, vec_ref):
        i = pl.program_id(0)

        @pl.when(i == 0)
        def _():
            vec_ref[...] = jnp.zeros_like(vec_ref)

        xv, gv = x_ref[...], g_ref[...]
        r = lax.rsqrt(jnp.mean(xv * xv, axis=-1, keepdims=True) + EPS)
        xn = xv * r
        e = xn * gv - t_ref[...]
        tok = 0.5 * jnp.mean(e * e, axis=-1, keepdims=True)
        vec_ref[0:1, :] += jnp.broadcast_to(jnp.sum(tok, axis=0, keepdims=True), (1, D))
        dy = e * (1.0 / D)
        vec_ref[1:2, :] += jnp.sum(dy * xn, axis=0, keepdims=True)
        dxn = dy * gv
        dx = r * (dxn - xn * jnp.mean(dxn * xn, axis=-1, keepdims=True))
        dx_ref[...] = dx
        vec_ref[2:3, :] += jnp.sum(dx * o_ref[...], axis=0, keepdims=True)
        do_ref[...] = (dx * gt_ref[...]).astype(BF16)

    row = pl.BlockSpec((tm, D), lambda i: (i, 0))
    vec = pl.BlockSpec((1, D), lambda i: (0, 0))
    return pl.pallas_call(
        body, name="loss_head", grid=(S // tm,), in_specs=[row, row, vec, row, vec],
        out_specs=[row, row, pl.BlockSpec((8, D), lambda i: (0, 0))],
        out_shape=[jax.ShapeDtypeStruct((S, D), F32), jax.ShapeDtypeStruct((S, D), BF16), jax.ShapeDtypeStruct((8, D), F32)],
        compiler_params=_cparams(("arbitrary",)))(x3, tgt, gf, o2, gt2)


def _ffn_in_swiglu(h, wg):
    S, D = h.shape
    _, _, tf = wg.shape
    nf = N_DEV // 2
    F = nf * tf
    tm = _tile(S, 256, 16)

    def body(h_ref, wa_ref, wu_ref, hf_ref, fac_ref):
        hv = h_ref[...]
        a = lax.dot_general(hv, wa_ref[...], _NN, preferred_element_type=F32)
        up = lax.dot_general(hv, wu_ref[...], _NN, preferred_element_type=F32)
        sa = _sigmoid(a)
        silu = a * sa
        hf_ref[...] = (silu * up).astype(BF16)
        fac_ref[0] = (up * (sa * (1.0 + a * (1.0 - sa)))).astype(BF16)
        fac_ref[1] = silu.astype(BF16)

    return pl.pallas_call(
        body, name="ffn_in_swiglu", grid=(nf, S // tm),
        in_specs=[pl.BlockSpec((tm, D), lambda j, i: (i, 0)), pl.BlockSpec((None, D, tf), lambda j, i: (j, 0, 0)),
                  pl.BlockSpec((None, D, tf), lambda j, i: (j + nf, 0, 0))],
        out_specs=[pl.BlockSpec((tm, tf), lambda j, i: (i, j)), pl.BlockSpec((2, tm, tf), lambda j, i: (0, i, j))],
        out_shape=[jax.ShapeDtypeStruct((S, F), BF16), jax.ShapeDtypeStruct((2, S, F), BF16)],
        compiler_params=_cparams(("parallel", "parallel")))(h, wg, wg)


def _colsum2(dg2):
    _, S, D = dg2.shape
    tm = _tile(S, 256, 16)

    def body(x_ref, o_ref):
        @pl.when(pl.program_id(0) == 0)
        def _():
            o_ref[...] = jnp.zeros_like(o_ref)

        o_ref[0:1, :] += jnp.sum(x_ref[0].astype(F32), axis=0, keepdims=True)
        o_ref[1:2, :] += jnp.sum(x_ref[1].astype(F32), axis=0, keepdims=True)

    return pl.pallas_call(body, name="gate_bias_grad", grid=(S // tm,), in_specs=[pl.BlockSpec((2, tm, D), lambda i: (0, i, 0))],
                          out_specs=pl.BlockSpec((2, D), lambda i: (0, 0)), out_shape=jax.ShapeDtypeStruct((2, D), F32),
                          compiler_params=_cparams(("arbitrary",)))(dg2)


def _gmlp_common(u_ref, v_ref, lg_ref, lb_ref, ws_ref, bsb_ref, G, T, Dg):
    ug, dug = _gelu_parts(u_ref[...])
    vg, dvg = _gelu_parts(v_ref[...])
    mu = jnp.mean(vg, axis=-1, keepdims=True)
    vc = vg - mu
    rstd = lax.rsqrt(jnp.mean(vc * vc, axis=-1, keepdims=True) + EPS)
    vhat = vc * rstd
    vn = vhat * lg_ref[...] + lb_ref[...]
    row = lax.broadcasted_iota(jnp.int32, (T, T), 0)
    col = lax.broadcasted_iota(jnp.int32, (T, T), 1)
    tril = row >= col
    s = []
    for g in range(G):
        w = jnp.where(tril, ws_ref[g], 0.0)
        s.append(_dot(w, vn[:, g * Dg:(g + 1) * Dg]) + bsb_ref[g])
    return ug, dug, dvg, rstd, vhat, vn, tril, s


def _gmlp_fwd(z, ln_g, ln_b, ws, bsb, GW):
    S = z.shape[0]
    G, T, _ = ws.shape
    Dg = GW // G

    def body(u_ref, v_ref, lg_ref, lb_ref, ws_ref, bsb_ref, ya_ref):
        ug, _, _, _, _, _, _, s = _gmlp_common(u_ref, v_ref, lg_ref, lb_ref, ws_ref, bsb_ref, G, T, Dg)
        for g in range(G):
            sl = slice(g * Dg, (g + 1) * Dg)
            ya_ref[:, sl] = (ug[:, sl] * s[g]).astype(BF16)

    vec = pl.BlockSpec((1, GW), lambda c: (0, 0))
    return pl.pallas_call(
        body, name="gmlp_fwd", grid=(S // T,),
        in_specs=[pl.BlockSpec((T, GW), lambda c: (c, 0)), pl.BlockSpec((T, GW), lambda c: (c, 1)), vec, vec,
                  pl.BlockSpec((G, T, T), lambda c: (0, 0, 0)), pl.BlockSpec((G, T, Dg), lambda c: (0, 0, 0))],
        out_specs=pl.BlockSpec((T, GW), lambda c: (c, 0)), out_shape=jax.ShapeDtypeStruct((S, GW), BF16),
        compiler_params=_cparams(("parallel",)))(z, z, ln_g, ln_b, ws, bsb)


def _gmlp_bwd(z, dya, ln_g, ln_b, ws, bsb, GW):
    S = z.shape[0]
    G, T, _ = ws.shape
    Dg = GW // G
    nc = S // T

    def body(u_ref, v_ref, dya_ref, lg_ref, lb_ref, ws_ref, bsb_ref, dz_ref, dln_ref, dws_ref, dbs_ref, dbs_acc, dvh):
        c = pl.program_id(0)

        @pl.when(c == 0)
        def _():
            dln_ref[...] = jnp.zeros_like(dln_ref)
            dws_ref[...] = jnp.zeros_like(dws_ref)
            dbs_acc[...] = jnp.zeros_like(dbs_acc)

        ug, dug, dvg, rstd, vhat, vn, tril, s = _gmlp_common(u_ref, v_ref, lg_ref, lb_ref, ws_ref, bsb_ref, G, T, Dg)
        dya_v = dya_ref[...]
        for g in range(G):
            sl = slice(g * Dg, (g + 1) * Dg)
            dy_g = dya_v[:, sl]
            dz_ref[:, sl] = (dy_g * s[g] * dug[:, sl]).astype(BF16)
            ds = dy_g * ug[:, sl]
            dbs_acc[g] += ds
            w = jnp.where(tril, ws_ref[g], 0.0)
            dvn_g = _dot(w, ds, _TN)
            dws_ref[g] += jnp.where(tril, _dot(ds, vn[:, sl], _NT), 0.0)
            dln_ref[0:1, sl] += jnp.sum(dvn_g * vhat[:, sl], axis=0, keepdims=True)
            dln_ref[1:2, sl] += jnp.sum(dvn_g, axis=0, keepdims=True)
            dvh[:, sl] = dvn_g * lg_ref[:, sl]
        dvhat = dvh[...]
        m1 = jnp.mean(dvhat, axis=-1, keepdims=True)
        m2 = jnp.mean(dvhat * vhat, axis=-1, keepdims=True)
        dz_ref[:, GW:2 * GW] = (rstd * (dvhat - m1 - vhat * m2) * dvg).astype(BF16)

        @pl.when(c == nc - 1)
        def _():
            for g in range(G):
                dbs_ref[g] = jnp.sum(dbs_acc[g], axis=-1, keepdims=True)

    vec = pl.BlockSpec((1, GW), lambda c: (0, 0))
    return pl.pallas_call(
        body, name="gmlp_bwd", grid=(nc,),
        in_specs=[pl.BlockSpec((T, GW), lambda c: (c, 0)), pl.BlockSpec((T, GW), lambda c: (c, 1)),
                  pl.BlockSpec((T, GW), lambda c: (c, 0)), vec, vec,
                  pl.BlockSpec((G, T, T), lambda c: (0, 0, 0)), pl.BlockSpec((G, T, Dg), lambda c: (0, 0, 0))],
        out_specs=[pl.BlockSpec((T, 2 * GW), lambda c: (c, 0)), pl.BlockSpec((8, GW), lambda c: (0, 0)),
                   pl.BlockSpec((G, T, T), lambda c: (0, 0, 0)), pl.BlockSpec((G, T, 1), lambda c: (0, 0, 0))],
        out_shape=[jax.ShapeDtypeStruct((S, 2 * GW), BF16), jax.ShapeDtypeStruct((8, GW), F32),
                   jax.ShapeDtypeStruct((G, T, T), F32), jax.ShapeDtypeStruct((G, T, 1), F32)],
        scratch_shapes=[pltpu.VMEM((G, T, Dg), F32), pltpu.VMEM((T, GW), F32)],
        compiler_params=_cparams(("arbitrary",)))(z, z, dya, ln_g, ln_b, ws, bsb)


def _hg_common(q_ref, f_ref, hlb_ref):
    C = HG_CHUNK
    a = hlb_ref[...]
    lb = _sigmoid(a[0:1, :] - a[1:2, :])
    sig = _sigmoid(f_ref[...])
    f = lb + (1.0 - lb) * sig
    lf = jnp.log(f)
    kk = 1.0 - f
    q = q_ref[...]
    sq = _sigmoid(q)
    qa = q * sq
    row = lax.broadcasted_iota(jnp.int32, (C, C), 0)
    col = lax.broadcasted_iota(jnp.int32, (C, C), 1)
    tril = row >= col
    b = _ones_dot(tril.astype(BF16), lf)
    bm = b[HG_MID:HG_MID + 1, :]
    bl = b[C - 1:C, :]
    e_b = jnp.exp(b)
    e_qm = jnp.exp(jnp.minimum(b - bm, EXP_CLAMP))
    e_km = jnp.exp(jnp.minimum(bm - b, EXP_CLAMP))
    e_kl = jnp.exp(bl - b)
    return dict(lb=lb, sig=sig, f=f, kk=kk, q=q, sq=sq, qa=qa, tril=tril, e_b=e_b, e_qm=e_qm, e_km=e_km, e_kl=e_kl,
                e_l=jnp.exp(bl), qh=qa * e_b, qt=qa * e_qm, kt=kk * e_km, kh=kk * e_kl)


def _hg_fwd(z, hg_lb, ng, HW):
    S = z.shape[0]
    C, H, dk = HG_CHUNK, HW // HG_DK, HG_DK
    nc = S // C

    def body(q_ref, f_ref, i_ref, og_ref, hlb_ref, ng_ref, yb_ref, o_ref, st_ref, state):
        @pl.when(pl.program_id(0) == 0)
        def _():
            state[...] = jnp.zeros_like(state)

        t = _hg_common(q_ref, f_ref, hlb_ref)
        iv = i_ref[...]
        for h in range(H):
            sl = slice(h * dk, (h + 1) * dk)
            st = state[h]
            st_ref[h] = st
            a = jnp.where(t["tril"], _dot(t["qt"][:, sl], t["kt"][:, sl], _NT), 0.0)
            o_h = _dot(a, iv[:, sl]) + _dot(t["qh"][:, sl], st, _NT)
            state[h] = st * t["e_l"][:, sl] + _dot(iv[:, sl], t["kh"][:, sl], _TN)
            o_ref[:, sl] = o_h
            rr = lax.rsqrt(jnp.mean(o_h * o_h, axis=-1, keepdims=True) + EPS)
            og = og_ref[:, sl]
            yb_ref[:, sl] = (o_h * rr * ng_ref[:, sl] * (og * _sigmoid(og))).astype(BF16)

    def col(k):
        return pl.BlockSpec((C, HW), lambda c: (c, k))

    base = 2
    return pl.pallas_call(
        body, name="hgrn_fwd", grid=(nc,),
        in_specs=[col(base), col(base + 1), col(base + 2), col(base + 3),
                  pl.BlockSpec((2, HW), lambda c: (0, 0)), pl.BlockSpec((1, HW), lambda c: (0, 0))],
        out_specs=[pl.BlockSpec((C, HW), lambda c: (c, 0)), pl.BlockSpec((C, HW), lambda c: (c, 0)),
                   pl.BlockSpec((None, H, dk, dk), lambda c: (c, 0, 0, 0))],
        out_shape=[jax.ShapeDtypeStruct((S, HW), BF16), jax.ShapeDtypeStruct((S, HW), F32),
                   jax.ShapeDtypeStruct((nc, H, dk, dk), F32)],
        scratch_shapes=[pltpu.VMEM((H, dk, dk), F32)],
        compiler_params=_cparams(("arbitrary",)))(z, z, z, z, hg_lb, ng)


def _hg_bwd(z, o, states, dyb, hg_lb, ng, HW, dz_head, dz_tail):
    S = z.shape[0]
    C, H, dk = HG_CHUNK, HW // HG_DK, HG_DK
    nc = S // C
    B0 = dz_head.shape[1]
    DT = dz_tail.shape[2]
    INW = B0 + 4 * HW + 2 * DT

    def body(q_ref, f_ref, i_ref, og_ref, o_ref, st_ref, stn_ref, dyb_ref, hlb_ref, ng_ref, head_ref, tail_ref,
             dzf_ref, dng_ref, dhlb_ref, dstate, cross, dqa_buf, dkk_buf, db_buf, dlb_acc):
        c = pl.program_id(0)
        dzf_ref[:, 0:B0] = head_ref[...]
        dzf_ref[:, B0 + 4 * HW:B0 + 4 * HW + DT] = tail_ref[0]
        dzf_ref[:, B0 + 4 * HW + DT:INW] = tail_ref[1]
        dz_ref = dzf_ref.at[:, B0:B0 + 4 * HW]

        @pl.when(c == 0)
        def _():
            dstate[...] = jnp.zeros_like(dstate)
            dlb_acc[...] = jnp.zeros_like(dlb_acc)
            dng_ref[...] = jnp.zeros_like(dng_ref)

        def r16(v):
            return v.astype(BF16).astype(F32)

        t = _hg_common(q_ref, f_ref, hlb_ref)
        iv = i_ref[...]
        for h in range(H):
            sl = slice(h * dk, (h + 1) * dk)
            o_h, og, dyb_h, ng_h = o_ref[:, sl], og_ref[:, sl], dyb_ref[:, sl], ng_ref[:, sl]
            sg = _sigmoid(og)
            silu_og = og * sg
            rr = lax.rsqrt(jnp.mean(o_h * o_h, axis=-1, keepdims=True) + EPS)
            on = o_h * rr
            dng_ref[0:1, sl] += jnp.sum(dyb_h * on * silu_og, axis=0, keepdims=True)
            dz_ref[:, 3 * HW + h * dk:3 * HW + (h + 1) * dk] = (dyb_h * on * ng_h * (sg * (1.0 + og * (1.0 - sg)))).astype(BF16)
            don = dyb_h * ng_h * silu_og
            do_h = rr * (don - on * jnp.mean(don * on, axis=-1, keepdims=True))

            qt, kt, qh, kh, iv_h = t["qt"][:, sl], t["kt"][:, sl], t["qh"][:, sl], t["kh"][:, sl], iv[:, sl]
            a = jnp.where(t["tril"], _dot(qt, kt, _NT), 0.0)
            da = jnp.where(t["tril"], _dot(do_h, iv_h, _NT), 0.0)
            st, dst = st_ref[h], dstate[h]
            cross[:, sl] = jnp.sum(dst * stn_ref[h], axis=0, keepdims=True)
            dqh = _dot(do_h, st)
            dstate[h] = _dot(do_h, qh, _TN) + dst * t["e_l"][:, sl]
            div = _dot(a, do_h, _TN) + _dot(kh, dst, _NT)
            dkh = _dot(iv_h, dst)
            dqt = _dot(da, kt)
            dkt = _dot(da, qt, _TN)
            dz_ref[:, 2 * HW + h * dk:2 * HW + (h + 1) * dk] = div.astype(BF16)
            dqa_buf[:, sl] = dqh * t["e_b"][:, sl] + dqt * t["e_qm"][:, sl]
            dkk_buf[:, sl] = dkt * t["e_km"][:, sl] + dkh * t["e_kl"][:, sl]
            db_buf[:, sl] = r16(qt) * dqt - r16(kt) * dkt + r16(qh) * dqh - r16(kh) * dkh

        dqa, dkk = dqa_buf[...], dkk_buf[...]
        triu = jnp.logical_not(t["tril"]) | (lax.broadcasted_iota(jnp.int32, (C, C), 0) == lax.broadcasted_iota(jnp.int32, (C, C), 1))
        dlf = _ones_dot(triu.astype(BF16), db_buf[...]) + cross[...]
        df = dlf / t["f"] - dkk
        sig, lb = t["sig"], t["lb"]
        dz_ref[:, HW:2 * HW] = (df * (1.0 - lb) * sig * (1.0 - sig)).astype(BF16)
        dlb_acc[...] += jnp.sum(df * (1.0 - sig), axis=0, keepdims=True)
        q, sq = t["q"], t["sq"]
        dz_ref[:, 0:HW] = (dqa * (sq * (1.0 + q * (1.0 - sq)))).astype(BF16)

        @pl.when(c == nc - 1)
        def _():
            da0 = dlb_acc[...] * lb * (1.0 - lb)
            dhlb_ref[0:1, :] = da0
            dhlb_ref[1:2, :] = -da0

    def col(k):
        return pl.BlockSpec((C, HW), lambda c: (nc - 1 - c, k))

    base = 2
    return pl.pallas_call(
        body, name="hgrn_bwd", grid=(nc,),
        in_specs=[col(base), col(base + 1), col(base + 2), col(base + 3), col(0),
                  pl.BlockSpec((None, H, dk, dk), lambda c: (nc - 1 - c, 0, 0, 0)),
                  pl.BlockSpec((None, H, dk, dk), lambda c: (jnp.minimum(nc - c, nc - 1), 0, 0, 0)), col(0),
                  pl.BlockSpec((2, HW), lambda c: (0, 0)), pl.BlockSpec((1, HW), lambda c: (0, 0)),
                  pl.BlockSpec((C, B0), lambda c: (nc - 1 - c, 0)), pl.BlockSpec((2, C, DT), lambda c: (0, nc - 1 - c, 0))],
        out_specs=[pl.BlockSpec((C, INW), lambda c: (nc - 1 - c, 0)), pl.BlockSpec((8, HW), lambda c: (0, 0)),
                   pl.BlockSpec((2, HW), lambda c: (0, 0))],
        out_shape=[jax.ShapeDtypeStruct((S, INW), BF16), jax.ShapeDtypeStruct((8, HW), F32), jax.ShapeDtypeStruct((2, HW), F32)],
        scratch_shapes=[pltpu.VMEM((H, dk, dk), F32), pltpu.VMEM((1, HW), F32), pltpu.VMEM((C, HW), F32), pltpu.VMEM((C, HW), F32),
                        pltpu.VMEM((C, HW), F32), pltpu.VMEM((1, HW), F32)],
        compiler_params=_cparams(("arbitrary",)))(z, z, z, z, o, states, states, dyb, hg_lb, ng, dz_head, dz_tail)


def _position():
    x, y, c = lax.axis_index("x"), lax.axis_index("y"), lax.axis_index("c")
    return x, y, c, 4 * x + 2 * y + c


def _flip(x, y, c, k):
    return (1 - x if k & 4 else x, 1 - y if k & 2 else y, 1 - c if k & 1 else c)


def _allgather_small(name, v):
    R, L = v.shape

    def body(v_ref, out_ref, send_sems, recv_sems):
        x, y, c, me = _position()
        out_ref[me] = v_ref[...]
        copies = []
        for k in range(1, N_DEV):
            cp = pltpu.make_async_remote_copy(src_ref=v_ref, dst_ref=out_ref.at[me], send_sem=send_sems.at[k - 1],
                                              recv_sem=recv_sems.at[k - 1], device_id=_flip(x, y, c, k), device_id_type=MESH)
            cp.start()
            copies.append(cp)
        for cp in copies:
            cp.wait()

    return pl.pallas_call(
        body, name=name, out_shape=jax.ShapeDtypeStruct((N_DEV, R, L), v.dtype),
        in_specs=[pl.BlockSpec(memory_space=pltpu.VMEM)], out_specs=pl.BlockSpec(memory_space=pltpu.VMEM),
        scratch_shapes=[pltpu.SemaphoreType.DMA((N_DEV - 1,)), pltpu.SemaphoreType.DMA((N_DEV - 1,))],
        compiler_params=pltpu.CompilerParams(vmem_limit_bytes=VMEM_LIMIT),
    )(v)


def _allgather_hbm(name, shards):
    n = len(shards)

    def body(*refs):
        ins, outs = refs[:n], refs[n:2 * n]
        send_sems, recv_sems, local_sems = refs[2 * n:]
        x, y, c, me = _position()
        sibling = (x, y, 1 - c)
        chips = [(1 - x, y), (x, 1 - y), (1 - x, 1 - y)]

        def slot(px, py, pc):
            return 4 * px + 2 * py + pc

        def copy(w, k, block, to, src=None):
            dst = outs[w].at[slot(*block)]
            return pltpu.make_async_remote_copy(src_ref=dst if src is None else src, dst_ref=dst, send_sem=send_sems.at[w, k],
                                                recv_sem=recv_sems.at[w, k], device_id=to, device_id_type=MESH)

        mine, first, passed = [], [], []
        for w in range(n):
            cp = pltpu.make_async_copy(ins[w], outs[w].at[me], local_sems.at[w])
            cp.start()
            mine.append(cp)
            for j, chip in enumerate(chips):
                first.append(copy(w, 1 + j, (x, y, c), (*chip, c), src=ins[w]))
            first.append(copy(w, 0, (x, y, c), sibling, src=ins[w]))
        for cp in first:
            cp.start()
        for w in range(n):
            for j, chip in enumerate(chips):
                copy(w, 1 + j, (*chip, c), (x, y, c)).wait_recv()
                cp = copy(w, 4 + j, (*chip, c), sibling)
                cp.start()
                passed.append(cp)
        for w in range(n):
            copy(w, 0, sibling, (x, y, c)).wait_recv()
            for j, chip in enumerate(chips):
                copy(w, 4 + j, (*chip, 1 - c), (x, y, c)).wait_recv()
        for cp in first + passed:
            cp.wait_send()
        for cp in mine:
            cp.wait()

    hbm = pl.BlockSpec(memory_space=pltpu.HBM)
    return pl.pallas_call(
        body, name=name, out_shape=[jax.ShapeDtypeStruct((N_DEV, *s.shape), s.dtype) for s in shards],
        in_specs=[hbm] * n, out_specs=[hbm] * n,
        scratch_shapes=[pltpu.SemaphoreType.DMA((n, 7)), pltpu.SemaphoreType.DMA((n, 7)), pltpu.SemaphoreType.DMA((n,))],
    )(*shards)


_HBM = pl.BlockSpec(memory_space=pltpu.HBM)
_SEM = pl.BlockSpec(memory_space=pltpu.SEMAPHORE)
_EFFECT = pltpu.SideEffectType.DATAFLOW_SIDE_EFFECTING


def _split_start(name, bufs, n_sems, copies_fn, after=None):
    nb = len(bufs)
    extra = [] if after is None else [after]
    k = nb + len(extra)

    def body(*refs):
        for cp in copies_fn(refs[:nb], refs[k], refs[k + 1]):
            cp.start()
        refs[-1][...] = jnp.zeros_like(refs[-1])

    sems = pltpu.SemaphoreType.DMA((n_sems,))
    res = pl.pallas_call(
        body, name=name,
        out_shape=(sems, sems, *[pltpu.HBM(a.shape, a.dtype) for a in bufs], jax.ShapeDtypeStruct((8, LANES), F32)),
        in_specs=[_HBM] * nb + [pl.BlockSpec(memory_space=pl.ANY)] * len(extra),
        out_specs=(_SEM, _SEM, *[_HBM] * nb, pl.BlockSpec(memory_space=pltpu.VMEM)),
        input_output_aliases={i: 2 + i for i in range(nb)},
        compiler_params=pltpu.CompilerParams(has_side_effects=_EFFECT),
    )(*[pltpu.with_memory_space_constraint(a, pltpu.HBM) for a in bufs], *extra)
    return res[0], res[1], list(res[2:2 + nb]), res[-1]


def _split_wait(name, bufs, send_sems, recv_sems, after, copies_fn):
    nb = len(bufs)

    def body(*refs):
        for cp in copies_fn(refs[:nb], refs[nb], refs[nb + 1]):
            cp.wait_send()
            cp.wait_recv()

    res = pl.pallas_call(
        body, name=name, out_shape=tuple(pltpu.HBM(a.shape, a.dtype) for a in bufs),
        in_specs=[_HBM] * nb + [_SEM, _SEM, pl.BlockSpec(memory_space=pl.ANY)], out_specs=tuple([_HBM] * nb),
        input_output_aliases={i: i for i in range(nb)},
        compiler_params=pltpu.CompilerParams(has_side_effects=_EFFECT),
    )(*bufs, send_sems, recv_sems, after)
    return list(res)


def _split_relay(name, bufs, send_sems, recv_sems, after, wait_fn, n_sems, start_fn):
    nb = len(bufs)

    def body(*refs):
        for cp in wait_fn(refs[:nb], refs[nb], refs[nb + 1]):
            cp.wait_send()
            cp.wait_recv()
        for cp in start_fn(refs[:nb], refs[nb + 3], refs[nb + 4]):
            cp.start()
        refs[-1][...] = jnp.zeros_like(refs[-1])

    sems = pltpu.SemaphoreType.DMA((n_sems,))
    res = pl.pallas_call(
        body, name=name, out_shape=(sems, sems, *[pltpu.HBM(a.shape, a.dtype) for a in bufs], jax.ShapeDtypeStruct((8, LANES), F32)),
        in_specs=[_HBM] * nb + [_SEM, _SEM, pl.BlockSpec(memory_space=pl.ANY)],
        out_specs=(_SEM, _SEM, *[_HBM] * nb, pl.BlockSpec(memory_space=pltpu.VMEM)),
        input_output_aliases={i: 2 + i for i in range(nb)},
        compiler_params=pltpu.CompilerParams(has_side_effects=_EFFECT),
    )(*bufs, send_sems, recv_sems, after)
    return res[0], res[1], list(res[2:2 + nb]), res[-1]


N_CHIP = 4


def _chip_flip(x, y, k):
    return (1 - x if k & 2 else x), (1 - y if k & 1 else y)


def _gather_first_copies(n):
    def copies(bufs, send_sems, recv_sems):
        x, y, c, me = _position()
        out = []
        for w in range(n):
            for k in range(N_CHIP):
                to = (x, y, 1 - c) if k == 0 else (*_chip_flip(x, y, k), c)
                out.append(pltpu.make_async_remote_copy(
                    src_ref=bufs[w], dst_ref=bufs[n + w].at[me], send_sem=send_sems.at[w * N_CHIP + k],
                    recv_sem=recv_sems.at[w * N_CHIP + k], device_id=to, device_id_type=MESH))
        return out
    return copies


def _gather_relay_copies(n):
    def copies(bufs, send_sems, recv_sems):
        x, y, c, _ = _position()
        out = []
        for w in range(n):
            for k in range(1, N_CHIP):
                px, py = _chip_flip(x, y, k)
                blk = bufs[n + w].at[4 * px + 2 * py + c]
                out.append(pltpu.make_async_remote_copy(
                    src_ref=blk, dst_ref=blk, send_sem=send_sems.at[w * (N_CHIP - 1) + k - 1],
                    recv_sem=recv_sems.at[w * (N_CHIP - 1) + k - 1], device_id=(x, y, 1 - c), device_id_type=MESH))
        return out
    return copies


def _small_gather_copies(bufs, send_sems, recv_sems):
    x, y, c, me = _position()
    return [pltpu.make_async_remote_copy(src_ref=bufs[0], dst_ref=bufs[1].at[me], send_sem=send_sems.at[k - 1], recv_sem=recv_sems.at[k - 1],
                                         device_id=_flip(x, y, c, k), device_id_type=MESH) for k in range(1, N_DEV)]


def _xor(a, b):
    return a + b - 2 * a * b


def _forward_first_copies(n):
    def copies(bufs, send_sems, recv_sems):
        x, y, c, me = _position()
        out = []
        for w in range(n):
            for k, to in enumerate([(x, y, 1 - c), (1 - x, y, c), (x, 1 - y, c)]):
                out.append(pltpu.make_async_remote_copy(
                    src_ref=bufs[w], dst_ref=bufs[n + w].at[me], send_sem=send_sems.at[w * 3 + k],
                    recv_sem=recv_sems.at[w * 3 + k], device_id=to, device_id_type=MESH))
        return out
    return copies


def _forward_second_copies(n):
    def copies(bufs, send_sems, recv_sems):
        x, y, c, _ = _position()
        out = []
        for w in range(n):
            half = bufs[n + w].shape[1] // 2
            for k, (src_chip, rows, to) in enumerate([((1 - x, y), pl.ds(0, half), (x, 1 - y, c)), ((x, 1 - y), pl.ds(half, half), (1 - x, y, c))]):
                blk = bufs[n + w].at[4 * src_chip[0] + 2 * src_chip[1] + c, rows]
                out.append(pltpu.make_async_remote_copy(src_ref=blk, dst_ref=blk, send_sem=send_sems.at[w * 4 + k],
                                                        recv_sem=recv_sems.at[w * 4 + k], device_id=to, device_id_type=MESH))
            for k, (px, py) in enumerate([(1 - x, y), (x, 1 - y)]):
                blk = bufs[n + w].at[4 * px + 2 * py + c]
                out.append(pltpu.make_async_remote_copy(src_ref=blk, dst_ref=blk, send_sem=send_sems.at[w * 4 + 2 + k],
                                                        recv_sem=recv_sems.at[w * 4 + 2 + k], device_id=(x, y, 1 - c), device_id_type=MESH))
        return out
    return copies


def _forward_third_copies(n):
    def copies(bufs, send_sems, recv_sems):
        x, y, c, _ = _position()
        out = []
        for w in range(n):
            blk = bufs[n + w].at[4 * (1 - x) + 2 * (1 - y) + c]
            out.append(pltpu.make_async_remote_copy(src_ref=blk, dst_ref=blk, send_sem=send_sems.at[w], recv_sem=recv_sems.at[w],
                                                    device_id=(x, y, 1 - c), device_id_type=MESH))
        return out
    return copies


def _to_sibling_copies(n):
    def copies(bufs, send_sems, recv_sems):
        x, y, c, _ = _position()
        out = []
        for w in range(n):
            for q in range(N_CHIP):
                out.append(pltpu.make_async_remote_copy(
                    src_ref=bufs[w].at[2 * q + 1 - c], dst_ref=bufs[n + w].at[q], send_sem=send_sems.at[w * N_CHIP + q],
                    recv_sem=recv_sems.at[w * N_CHIP + q], device_id=(x, y, 1 - c), device_id_type=MESH))
        return out
    return copies


def _to_owner_copies(n):
    def copies(bufs, send_sems, recv_sems):
        x, y, c, _ = _position()
        out = []
        for w in range(n):
            for k in range(1, N_CHIP):
                px, py = (1 - x if k & 2 else x), (1 - y if k & 1 else y)
                out.append(pltpu.make_async_remote_copy(
                    src_ref=bufs[w].at[2 * px + py], dst_ref=bufs[n + w].at[k - 1], send_sem=send_sems.at[w * (N_CHIP - 1) + k - 1],
                    recv_sem=recv_sems.at[w * (N_CHIP - 1) + k - 1], device_id=(px, py, c), device_id_type=MESH))
        return out
    return copies


def _chip_sum(name, stack, landed, c_idx):
    _, R, C = stack.shape
    tr = _tile(R, max(16, 1048576 // C), 16)

    def body(c_ref, a_ref, b_ref, o_ref):
        o_ref[...] = (a_ref[...].astype(F32) + b_ref[...].astype(F32)).astype(o_ref.dtype)

    return pl.pallas_call(
        body, name=name,
        grid_spec=pltpu.PrefetchScalarGridSpec(
            num_scalar_prefetch=1, grid=(N_CHIP, R // tr),
            in_specs=[pl.BlockSpec((None, tr, C), lambda q, i, c_ref: (2 * q + c_ref[0], i, 0)),
                      pl.BlockSpec((None, tr, C), lambda q, i, c_ref: (q, i, 0))],
            out_specs=pl.BlockSpec((None, tr, C), lambda q, i, c_ref: (q, i, 0))),
        out_shape=jax.ShapeDtypeStruct((N_CHIP, R, C), stack.dtype),
        compiler_params=_cparams(("parallel", "parallel")))(c_idx, stack, landed)


def _ada_mod(c16, w):
    _, D = c16.shape
    n = w.shape[1]
    tk = _tile(D, 512)
    nk = D // tk

    def body(c_ref, w_ref, o_ref, ca_ref):
        @pl.when(pl.program_id(0) == 0)
        def _():
            o_ref[...] = jnp.zeros_like(o_ref)

        cv = c_ref[...]
        ca = cv * _sigmoid(cv)
        ca_ref[...] = ca
        o_ref[...] += _dot(ca, w_ref[...])

    return pl.pallas_call(
        body, name="ada_mod", grid=(nk,),
        in_specs=[pl.BlockSpec((16, tk), lambda k: (0, k)), pl.BlockSpec((tk, n), lambda k: (k, 0))],
        out_specs=[pl.BlockSpec((16, n), lambda k: (0, 0)), pl.BlockSpec((16, tk), lambda k: (0, k))],
        out_shape=[jax.ShapeDtypeStruct((16, n), F32), jax.ShapeDtypeStruct((16, D), F32)],
        compiler_params=_cparams(("arbitrary",)))(c16, w)


def _cast_shard(name, wf, slot):
    r, c = wf.shape
    tr = _tile(r, max(16, 1048576 // c), 16)

    def body(slot_ref, w_ref, s_ref, g_ref):
        v = w_ref[...].astype(BF16)
        s_ref[...] = v
        g_ref[...] = v

    return pl.pallas_call(
        body, name=name,
        grid_spec=pltpu.PrefetchScalarGridSpec(
            num_scalar_prefetch=1, grid=(r // tr,), in_specs=[pl.BlockSpec((tr, c), lambda i, s: (i, 0))],
            out_specs=[pl.BlockSpec((tr, c), lambda i, s: (i, 0)), pl.BlockSpec((None, tr, c), lambda i, s: (s[0], i, 0))]),
        out_shape=[jax.ShapeDtypeStruct((r, c), BF16), jax.ShapeDtypeStruct((N_DEV, r, c), BF16)],
        compiler_params=_cparams(("parallel",)))(slot, wf)


def _adam_math(w, g, m, v):
    m2 = ADAM_B1 * m + (1.0 - ADAM_B1) * g
    v2 = ADAM_B2 * v + (1.0 - ADAM_B2) * (g * g)
    m_hat = m2 / (1.0 - ADAM_B1 ** ADAM_STEP)
    v_hat = v2 / (1.0 - ADAM_B2 ** ADAM_STEP)
    delta = -ADAM_LR * (m_hat / (jnp.sqrt(v_hat) + ADAM_EPS) + ADAM_WD * w)
    return delta, m2, v2


def _adamw(name, w, m, v, own, own_slot, parts=(), row0=0, into=None):
    R, C = w.shape
    Rp = own.shape[1]
    tr = _tile(Rp, max(16, 393216 // C), 16)
    off = row0 // tr
    n_p = len(parts)
    held = [] if into is None else list(into)

    def body(slot_ref, *refs):
        w_ref, m_ref, v_ref, own_ref = refs[:4]
        g_ref, d_ref, m2_ref, v2_ref = refs[4 + n_p + len(held):]
        g = own_ref[...].astype(F32)
        for p_ref in refs[4:4 + n_p]:
            for s in range(p_ref.shape[0]):
                g = g + p_ref[s].astype(F32)
        delta, m2, v2 = _adam_math(w_ref[...], g, m_ref[...], v_ref[...])
        g_ref[...] = g
        d_ref[...] = delta
        m2_ref[...] = m2
        v2_ref[...] = v2

    blk = pl.BlockSpec((tr, C), lambda i, s: (i + off, 0))
    out = jax.ShapeDtypeStruct((R, C), F32)
    return pl.pallas_call(
        body, name=name,
        grid_spec=pltpu.PrefetchScalarGridSpec(
            num_scalar_prefetch=1, grid=(Rp // tr,),
            in_specs=[blk, blk, blk, pl.BlockSpec((None, tr, C), lambda i, s: (s[0], i, 0))]
            + [pl.BlockSpec((a.shape[0], tr, C), lambda i, s: (0, i, 0)) for a in parts]
            + [pl.BlockSpec(memory_space=pl.ANY)] * len(held),
            out_specs=[blk] * 4),
        out_shape=[out] * 4, input_output_aliases={5 + n_p + i: i for i in range(len(held))},
        compiler_params=_cparams(("parallel",)))(own_slot, w, m, v, own, *parts, *held)


def _small_update(gathered, w, m, v, after, rows):
    _, R, L = gathered.shape
    rs = w.shape[0]
    n = len(rows)
    assert all(r % 8 == 0 for r in rows) and sum(rows) <= rs and rs + 8 <= R

    def body(p_ref, w_ref, m_ref, v_ref, after_ref, *outs):
        g = p_ref[0]
        for p in range(1, N_DEV):
            g = g + p_ref[p]
        kinds = (g,) + _adam_math(w_ref[...], g[0:rs, :], m_ref[...], v_ref[...])
        at = 0
        for k, r in enumerate(rows):
            for idx, val in enumerate(kinds):
                outs[idx * n + k][...] = val[at:at + r, :]
            at += r
        outs[4 * n][...] = g[at:at + 8, :]

    vm = pl.BlockSpec(memory_space=pltpu.VMEM)
    shapes = [jax.ShapeDtypeStruct((r, L), F32) for _ in range(4) for r in rows] + [jax.ShapeDtypeStruct((8, L), F32)]
    return pl.pallas_call(body, name="small_update", in_specs=[vm] * 4 + [pl.BlockSpec(memory_space=pl.ANY)], out_specs=[vm] * len(shapes),
                          out_shape=shapes, compiler_params=pltpu.CompilerParams(vmem_limit_bytes=VMEM_LIMIT))(gathered, w, m, v, after)


class _Fetched(dict):
    def __init__(self, fetch):
        super().__init__()
        self.fetch = fetch

    def first(self, key, after):
        self[key] = self.fetch(key, after)
        return self[key]


def _local_step(x, tgt, mod, p, fetch, F, scatter=None):
    S, D = x.shape
    GW, HW = p["ln_g"].shape[1], p["hg_ng"].shape[1]
    G, T, _ = p["ws"].shape
    w = _Fetched(fetch)
    INW = 2 * GW + 4 * HW + 2 * D
    in_loc, br_loc, fi_loc = INW // N_DEV, D // N_DEV, 2 * F // N_DEV
    assert GW == HW and F % fi_loc == 0
    sh1, sc1, gt1, sh2, sc2, gt2 = (mod[:, k * D:(k + 1) * D] for k in range(6))
    bsb = jnp.broadcast_to(p["bs"][:, :, None], (G, T, GW // G))

    tm = _tile(S, 1024, 16)
    tmh = _tile(S, 512, 16)
    tn_in = _tile(in_loc, 1280)
    tn_d = _tile(D, 512)
    tn_br = _tile(br_loc, 512)
    tk_s = S
    tm_w = _tile(D, 1024)
    g_off = 2 * GW + 4 * HW

    h1 = _norm_mod("norm1", x, p["norm1_g"], sc1, sh1)
    z = _mm_nn_stacked("proj_in", h1, w.first("in", h1), tm=tm, tn=tn_in, tk=D)[0]
    ya = _gmlp_fwd(z, p["ln_g"], p["ln_b"], p["ws"], bsb, GW)
    yb, o_hg, states = _hg_fwd(z, p["hg_lb"], p["hg_ng"], HW)
    flat = {k: jnp.swapaxes(w.first(k, yb), 0, 1).reshape(GW, D) for k in ("bg", "bh")}
    tn_f = _tile(D, 1024)
    pa = _matmul(
        "branch_gmlp", ya, flat["bg"], dims=_NN, grid_mnk=(S // tm, D // tn_f, 1), tiles=(tm, tn_f),
        a_spec=pl.BlockSpec((tm, GW), lambda i, j, k: (i, 0)), b_spec=pl.BlockSpec((GW, tn_f), lambda i, j, k: (0, j)),
        out_shapes=[jax.ShapeDtypeStruct((S, D), F32)], out_specs=[pl.BlockSpec((tm, tn_f), lambda i, j, k: (i, j))], epilogue=_store(F32))[0]
    t_fi = w.first("fi_early", pa)

    def gates(ga_ref, gb_ref, ba_ref, bb_ref):
        return _sigmoid(ga_ref[...] + ba_ref[...]), _sigmoid(gb_ref[...] + bb_ref[...])

    def gate_specs(tn_, tm_=tm):
        o1, o2 = g_off // tn_, (g_off + D) // tn_
        return [pl.BlockSpec((tm_, tn_), lambda i, j, k: (i, o1 + j)), pl.BlockSpec((tm_, tn_), lambda i, j, k: (i, o2 + j)),
                pl.BlockSpec((1, tn_), lambda i, j, k: (0, j)), pl.BlockSpec((1, tn_), lambda i, j, k: (0, D // tn_ + j))]

    def merge_ep(acc, ex, outs):
        ga, gb = gates(*ex[1:5])
        outs[0][...] = acc
        outs[1][...] = (ga * ex[0][...] + gb * acc).astype(BF16)

    tile_o = pl.BlockSpec((tmh, tn_f), lambda i, j, k: (i, j))
    pb, y = _matmul(
        "branch_hg_merge", yb, flat["bh"], dims=_NN, grid_mnk=(S // tmh, D // tn_f, 1), tiles=(tmh, tn_f),
        a_spec=pl.BlockSpec((tmh, HW), lambda i, j, k: (i, 0)), b_spec=pl.BlockSpec((HW, tn_f), lambda i, j, k: (0, j)),
        extras=[pa, z, z, p["b_gate"], p["b_gate"]], extra_specs=[tile_o, *gate_specs(tn_f, tmh)],
        out_shapes=[jax.ShapeDtypeStruct((S, D), F32), jax.ShapeDtypeStruct((S, D), BF16)], out_specs=[tile_o, tile_o],
        epilogue=merge_ep, after=t_fi)

    def resid_ep(acc, ex, outs):
        outs[0][...] = acc
        outs[1][...] = ex[0][...] + ex[1][...] * acc

    def resid_mm(name, a, b, res, gt, tm_):
        K = a.shape[1]
        t_o = pl.BlockSpec((tm_, tn_d), lambda i, j, k: (i, j))
        return _matmul(
            name, a, b, dims=_NN, grid_mnk=(S // tm_, D // tn_d, 1), tiles=(tm_, tn_d),
            a_spec=pl.BlockSpec((tm_, K), lambda i, j, k: (i, 0)), b_spec=pl.BlockSpec((K, tn_d), lambda i, j, k: (0, j)),
            extras=[res, gt], extra_specs=[t_o, pl.BlockSpec((1, tn_d), lambda i, j, k: (0, j))],
            out_shapes=[jax.ShapeDtypeStruct((S, D), F32)] * 2, out_specs=[t_o, t_o], epilogue=resid_ep)

    o1, xm = resid_mm("proj_out", y, w.first("out", z), x, gt1, tm)
    h2 = _norm_mod("norm2", xm, p["norm2_g"], sc2, sh2)
    hf, hf_fac = _ffn_in_swiglu(h2, w.first("fi", h2))
    o2, x3 = resid_mm("ffn_out", hf, w.first("fo", hf), xm, gt2, tmh)
    dx3, do2, vec_l = _loss_head(x3, tgt, p["final_g"], o2, gt2)

    nf = F // fi_loc

    def dswiglu_ep(acc, ex, outs):
        outs[0][0] = (acc * ex[0][0].astype(F32)).astype(BF16)
        outs[0][1] = (acc * ex[0][1].astype(F32)).astype(BF16)

    pair = pl.BlockSpec((2, tmh, fi_loc), lambda i, j, k: (0, i, j))
    dab = _matmul(
        "ffn_out_dx", do2, w["fo"], dims=_NT, grid_mnk=(S // tmh, nf, 1), tiles=(tmh, fi_loc),
        a_spec=pl.BlockSpec((tmh, D), lambda i, j, k: (i, 0)), b_spec=pl.BlockSpec((fi_loc, D), lambda i, j, k: (j, 0)),
        extras=[hf_fac], extra_specs=[pair], out_shapes=[jax.ShapeDtypeStruct((2, S, F), BF16)], out_specs=[pair],
        epilogue=dswiglu_ep)[0]
    start = (lambda name, grads: scatter[0](name, grads)) if scatter is not None else (lambda name, grads: None)
    push = (lambda name, after: scatter[1](name, after)) if scatter is not None else (lambda name, after: None)

    def zero(token):
        return 0.0 if token is None else token[0:1, 0:1]

    tm_f = _tile(F, 512)
    g_fo = _mm_tn("ffn_out_dw", hf, do2, pl.BlockSpec((tk_s, D), lambda i, j, k: (k, j)), Mo=F, No=D, S=S, tm=tm_f, tn=D, tk=tk_s)
    g_fi = _mm_tn("ffn_in_dw", h2, dab, pl.BlockSpec((None, tk_s, fi_loc), lambda i, j, k: (j // nf, k, j % nf)),
                  Mo=D, No=2 * F, S=S, tm=tm_w, tn=fi_loc, tk=tk_s, stacked_nloc=fi_loc, after=g_fo)
    t_ffn = start("scatter_ffn", dict(fo=g_fo, fi=g_fi))
    dh2 = _mm_nt_stacked("ffn_in_dx", pl.BlockSpec((None, tm, fi_loc), lambda i, j, k: (k // nf, i, k % nf)), dab, w["fi"],
                         M=S, tm=tm, tn=tm_w, tk=fi_loc, after=t_ffn)
    dxm, vec2, do1 = _norm_mod_bwd("norm2_bwd", dh2, xm, p["norm2_g"], sc2, dx3, o1, gt1)
    t_ffn = push("scatter_ffn", dxm)

    def dmerge_ep(acc, ex, outs):
        ga, gb = gates(*ex[2:6])
        outs[0][...] = (acc * ga).astype(BF16)
        outs[1][...] = (acc * gb).astype(BF16)
        outs[2][0] = (acc * ex[0][...] * ga * (1.0 - ga)).astype(BF16)
        outs[2][1] = (acc * ex[1][...] * gb * (1.0 - gb)).astype(BF16)

    t_o = pl.BlockSpec((tm, tn_d), lambda i, j, k: (i, j))
    dpa, dpb, dg2 = _matmul(
        "proj_out_dx", do1, w["out"], dims=_NT, grid_mnk=(S // tm, D // tn_d, 1), tiles=(tm, tn_d),
        a_spec=pl.BlockSpec((tm, D), lambda i, j, k: (i, 0)), b_spec=pl.BlockSpec((tn_d, D), lambda i, j, k: (j, 0)),
        extras=[pa, pb, z, z, p["b_gate"], p["b_gate"]], extra_specs=[t_o, t_o, *gate_specs(tn_d)],
        out_shapes=[jax.ShapeDtypeStruct((S, D), BF16), jax.ShapeDtypeStruct((S, D), BF16), jax.ShapeDtypeStruct((2, S, D), BF16)],
        out_specs=[t_o, t_o, pl.BlockSpec((2, tm, tn_d), lambda i, j, k: (0, i, j))], epilogue=dmerge_ep, after=t_ffn)
    g_out = _mm_tn("proj_out_dw", y, do1, pl.BlockSpec((tk_s, D), lambda i, j, k: (k, j)), Mo=D, No=D, S=S, tm=tn_d, tn=D, tk=tk_s)
    tn_g = _tile(GW, 512)
    b_br = pl.BlockSpec((tk_s, br_loc), lambda i, j, k: (k, j))
    g_bg = _mm_tn("branch_gmlp_dw", ya, dpa, b_br, Mo=GW, No=D, S=S, tm=tn_g, tn=br_loc, tk=tk_s, stacked_nloc=br_loc)
    g_bh = _mm_tn("branch_hg_dw", yb, dpb, b_br, Mo=HW, No=D, S=S, tm=tn_g, tn=br_loc, tk=tk_s, stacked_nloc=br_loc)
    t_mix = start("scatter_mixer", dict(out=g_out, bg=g_bg, bh=g_bh))
    def branch_dx(name, dp, w_flat):
        return _matmul(
            name, dp, w_flat, dims=_NT, grid_mnk=(S // tm, GW // tn_g, 1), tiles=(tm, tn_g),
            a_spec=pl.BlockSpec((tm, D), lambda i, j, k: (i, 0)), b_spec=pl.BlockSpec((tn_g, D), lambda i, j, k: (j, 0)),
            out_shapes=[jax.ShapeDtypeStruct((S, GW), F32)], out_specs=[pl.BlockSpec((tm, tn_g), lambda i, j, k: (i, j))],
            epilogue=_store(F32), after=t_mix)[0]

    dya = branch_dx("branch_gmlp_dx", dpa, flat["bg"])
    dyb = branch_dx("branch_hg_dx", dpb, flat["bh"])
    db_gate = _colsum2(dg2)
    dz_gmlp, dln, dws, dbs = _gmlp_bwd(z, dya, p["ln_g"], p["ln_b"], p["ws"], bsb, GW)
    t_mix = push("scatter_mixer", dz_gmlp)
    dz, dng, dhlb = _hg_bwd(z, o_hg, states, dyb, p["hg_lb"], p["hg_ng"] + zero(t_mix), HW, dz_gmlp, dg2)
    half = D // 2
    tm_h = _tile(half, 1024)
    g_in = []
    t_in = None
    for hname, h in (("a", 0), ("b", 1)):
        g_in.append(_mm_tn("proj_in_dw_" + hname, h1, dz, pl.BlockSpec((tk_s, in_loc), lambda i, j, k: (k, j)), Mo=half, No=INW, S=S,
                           tm=tm_h, tn=in_loc, tk=tk_s, stacked_nloc=in_loc, after=t_in, a_off=h * (half // tm_h)))
        t_in = start("scatter_proj_in_" + hname, {"w_in_" + hname: g_in[-1]})
    t_in = push("scatter_proj_in_a", t_in)
    dh1 = _mm_nt_stacked("proj_in_dx", pl.BlockSpec((tm, in_loc), lambda i, j, k: (i, k)), dz, w["in"], M=S, tm=tm, tn=tm_w, tk=in_loc,
                         after=t_in)
    dx, vec1 = _norm_mod_bwd("norm1_bwd", dh1, x, p["norm1_g"], sc1, dxm)

    dmod = jnp.concatenate([vec1[0:1], vec1[1:2], vec2[3:4], vec2[0:1], vec2[1:2], vec_l[2:3]], axis=1)
    small = dict(norm1_g=vec1[2:3], b_gate=db_gate.reshape(1, 2 * D), ln_g=dln[0:1], ln_b=dln[1:2], ws=dws, bs=dbs.reshape(G, T),
                 hg_lb=dhlb, hg_ng=dng[0:1], norm2_g=vec2[2:3], final_g=vec_l[1:2], loss=vec_l[0:1, 0:LANES])
    big = dict(w_in_a=g_in[0], w_in_b=g_in[1], bg=g_bg, bh=g_bh, out=g_out, fi=g_fi, fo=g_fo)
    return dx, big, small, dmod


_SMALL = ("b_ada", "norm1_g", "b_gate", "ln_g", "ln_b", "ws", "bs", "hg_lb", "hg_ng", "norm2_g", "final_g")


def _pack(parts, rows_mult=8):
    flat = [a.reshape(-1) for a in parts]
    offs, n = [], 0
    for a in flat:
        offs.append(n)
        n += a.shape[0]
    pad = (-n) % (LANES * rows_mult)
    if pad:
        flat.append(jnp.zeros((pad,), F32))
    return jnp.concatenate(flat).reshape(-1, LANES), offs


def kernel(x, c, w_ada, b_ada, norm1_g, w_in, b_gate, gmlp_ln_g, gmlp_ln_b, gmlp_ws, gmlp_bs, hg_lb, hg_norm_g, w_branch_gmlp, w_branch_hg, w_out, norm2_g, w_ffn_in, w_ffn_out, final_norm_g, loss_target, m_w_ada, m_b_ada, m_norm1_g, m_w_in, m_b_gate, m_gmlp_ln_g, m_gmlp_ln_b, m_gmlp_ws, m_gmlp_bs, m_hg_lb, m_hg_norm_g, m_w_branch_gmlp, m_w_branch_hg, m_w_out, m_norm2_g, m_w_ffn_in, m_w_ffn_out, m_final_norm_g, v_w_ada, v_b_ada, v_norm1_g, v_w_in, v_b_gate, v_gmlp_ln_g, v_gmlp_ln_b, v_gmlp_ws, v_gmlp_bs, v_hg_lb, v_hg_norm_g, v_w_branch_gmlp, v_w_branch_hg, v_w_out, v_norm2_g, v_w_ffn_in, v_w_ffn_out, v_final_norm_g):
    S, D = x.shape[1], x.shape[2]
    ada_loc = w_ada.shape[2]
    me = 4 * lax.axis_index("x") + 2 * lax.axis_index("y") + lax.axis_index("c")
    me_idx = me.astype(jnp.int32).reshape(1)

    def empty_hbm(shape, dtype):
        return pltpu.with_memory_space_constraint(lax.empty(shape, dtype), pltpu.HBM)

    groups = dict(gather_in=dict(keys=["in"], src=[w_in], forward=True),
                  gather_mixer=dict(keys=["bg", "bh", "out"], src=[w_branch_gmlp, w_branch_hg, w_out], forward=False),
                  gather_ffn_in=dict(keys=["fi"], src=[w_ffn_in], forward=True),
                  gather_ffn_out=dict(keys=["fo"], src=[w_ffn_out], forward=False))
    group_of = {k: gname for gname, g in groups.items() for k in g["keys"]}

    def first_hop(gname, after):
        g = groups[gname]
        n = len(g["keys"])
        cast = [_cast_shard(f"{gname}_cast_{k}", a[0], me_idx) for k, a in zip(g["keys"], g["src"])]
        shards, outs = [s for s, _ in cast], [o for _, o in cast]
        if g["forward"]:
            *g["hop"], token = _split_start(gname + "_hop1", shards + outs, n * 3, _forward_first_copies(n), after=after)
        else:
            *g["hop"], token = _split_start(gname + "_hop1", shards + outs, n * N_CHIP, _gather_first_copies(n), after=after)
        return token

    def second_hop(gname, after):
        g = groups[gname]
        n = len(g["keys"])
        *g["hop"], token = _split_relay(gname + "_hop2", g["hop"][2], g["hop"][0], g["hop"][1], after,
                                        _forward_first_copies(n), n * 4, _forward_second_copies(n))
        return token

    def finish(gname, after):
        g = groups[gname]
        n = len(g["keys"])
        send_sems, recv_sems, bufs = g["hop"]
        if g["forward"]:
            send_sems, recv_sems, bufs, _ = _split_relay(gname + "_hop3", bufs, send_sems, recv_sems, after,
                                                         _forward_second_copies(n), n, _forward_third_copies(n))
            bufs = _split_wait(gname + "_wait", bufs, send_sems, recv_sems, after, _forward_third_copies(n))
        else:
            send_sems, recv_sems, bufs, _ = _split_relay(gname + "_relay", bufs, send_sems, recv_sems, after,
                                                         _gather_first_copies(n), n * (N_CHIP - 1), _gather_relay_copies(n))
            bufs = _split_wait(gname + "_wait", bufs, send_sems, recv_sems, after, _gather_relay_copies(n))
        g["done"] = dict(zip(g["keys"], bufs[n:]))

    c_all = _allgather_small("gather_c", c.reshape(D // LANES, LANES)).reshape(N_DEV, D)
    token = first_hop("gather_in", c_all)
    mod_cols, c_act = _ada_mod(jnp.pad(c_all, ((0, 16 - N_DEV), (0, 0))) + token[0:1, 0:1], w_ada[0])
    mod_vec = mod_cols[:N_DEV].reshape(-1, LANES)
    mg_send, mg_recv, mg_bufs, token = _split_start(
        "gather_mod_start", [mod_vec, lax.dynamic_update_slice(lax.empty((N_DEV, *mod_vec.shape), F32), mod_vec[None], (me, 0, 0))],
        N_DEV - 1, _small_gather_copies)
    token = second_hop("gather_in", token)
    token = first_hop("gather_ffn_in", first_hop("gather_mixer", token))
    mod_all = _split_wait("gather_mod_wait", mg_bufs, mg_send, mg_recv, token, _small_gather_copies)[1].reshape(N_DEV, N_DEV, ada_loc)
    mod = lax.dynamic_index_in_dim(mod_all, me, axis=1, keepdims=False).reshape(1, N_DEV * ada_loc) + b_ada

    def fetch(key, after):
        if key == "in":
            finish("gather_in", after)
        elif key == "fi_early":
            return first_hop("gather_ffn_out", second_hop("gather_ffn_in", after))
        elif "done" not in groups[group_of[key]]:
            finish(group_of[key], after)
        arr = groups[group_of[key]]["done"][key]
        return arr.reshape(-1, D) if key in ("out", "fo") else arr

    p = dict(norm1_g=norm1_g, b_gate=b_gate, ln_g=gmlp_ln_g, ln_b=gmlp_ln_b, ws=gmlp_ws[0], bs=gmlp_bs[0], hg_lb=hg_lb,
             hg_ng=hg_norm_g, norm2_g=norm2_g, final_g=final_norm_g.reshape(1, D))

    in_flight = {}
    c_idx = lax.axis_index("c").astype(jnp.int32).reshape(1)
    chip_idx = (2 * lax.axis_index("x") + lax.axis_index("y")).astype(jnp.int32).reshape(1)

    def scatter_start(name, grads):
        keys = list(grads)
        n = len(keys)
        stacks = [grads[k].reshape(N_DEV, -1, grads[k].shape[-1]) for k in keys]
        lands = [empty_hbm((N_CHIP, *g.shape[1:]), g.dtype) for g in stacks]
        send_sems, recv_sems, bufs, token = _split_start(name + "_d2d", stacks + lands, n * N_CHIP, _to_sibling_copies(n))
        in_flight[name] = dict(keys=keys, stage1=(send_sems, recv_sems, bufs))
        return token

    def scatter_push(name, after):
        f = in_flight[name]
        n = len(f["keys"])
        send_sems, recv_sems, bufs = f["stage1"]
        bufs = _split_wait(name + "_d2d_wait", bufs, send_sems, recv_sems, after, _to_sibling_copies(n))
        sums = [_chip_sum(f"{name}_sum_{k}", bufs[i], bufs[n + i], c_idx) for i, k in enumerate(f["keys"])]
        lands = [empty_hbm((N_CHIP - 1, *s.shape[1:]), s.dtype) for s in sums]
        send_sems, recv_sems, bufs, token = _split_start(name + "_ici", sums + lands, n * (N_CHIP - 1), _to_owner_copies(n))
        f["stage2"] = (send_sems, recv_sems, bufs)
        return token

    grad_x, _, small, dmod = _local_step(x[0], loss_target[0], mod, p, fetch, w_ffn_out.shape[1] * N_DEV, (scatter_start, scatter_push))

    small["b_ada"] = dmod
    packed, offs = _pack([small[k] for k in _SMALL] + [small["loss"]])
    sg_send, sg_recv, sg_bufs, t_tail = _split_start(
        "gather_small_start", [packed, lax.dynamic_update_slice(lax.empty((N_DEV, *packed.shape), F32), packed[None], (me, 0, 0))],
        N_DEV - 1, _small_gather_copies)
    t_tail = scatter_push("scatter_proj_in_b", t_tail)
    big_w = dict(w_in=(w_in, m_w_in, v_w_in, "w_in"), bg=(w_branch_gmlp, m_w_branch_gmlp, v_w_branch_gmlp, "w_branch_gmlp"),
                 bh=(w_branch_hg, m_w_branch_hg, v_w_branch_hg, "w_branch_hg"), out=(w_out, m_w_out, v_w_out, "w_out"),
                 fi=(w_ffn_in, m_w_ffn_in, v_w_ffn_in, "w_ffn_in"), fo=(w_ffn_out, m_w_ffn_out, v_w_ffn_out, "w_ffn_out"))
    upd = {}

    def land_and_update(name, after):
        keys = in_flight[name]["keys"]
        n = len(keys)
        send_sems, recv_sems, bufs = in_flight[name]["stage2"]
        bufs = _split_wait(name + "_ici_wait", bufs, send_sems, recv_sems, after, _to_owner_copies(n))
        for i, k in enumerate(keys):
            if k in big_w:
                wt, mt, vt, out_name = big_w[k]
                upd[out_name] = _adamw("adamw_" + out_name, wt[0], mt[0], vt[0], bufs[i], chip_idx, [bufs[n + i]])
            else:
                wt, mt, vt, out_name = big_w["w_in"]
                upd[out_name] = _adamw("adamw_" + k, wt[0], mt[0], vt[0], bufs[i], chip_idx, [bufs[n + i]],
                                       row0=0 if k == "w_in_a" else bufs[i].shape[1], into=upd.get(out_name))
            after = upd[out_name][1]
        return after

    after = land_and_update("scatter_mixer", land_and_update("scatter_ffn", t_tail))
    gathered = _split_wait("gather_small_wait", sg_bufs, sg_send, sg_recv, after, _small_gather_copies)[1]
    wp = dict(p, b_ada=b_ada)
    ms = dict(b_ada=m_b_ada, norm1_g=m_norm1_g, b_gate=m_b_gate, ln_g=m_gmlp_ln_g, ln_b=m_gmlp_ln_b, ws=m_gmlp_ws, bs=m_gmlp_bs,
              hg_lb=m_hg_lb, hg_ng=m_hg_norm_g, norm2_g=m_norm2_g, final_g=m_final_norm_g)
    vs = dict(b_ada=v_b_ada, norm1_g=v_norm1_g, b_gate=v_b_gate, ln_g=v_gmlp_ln_g, ln_b=v_gmlp_ln_b, ws=v_gmlp_ws, bs=v_gmlp_bs,
              hg_lb=v_hg_lb, hg_ng=v_hg_norm_g, norm2_g=v_norm2_g, final_g=v_final_norm_g)
    w_sm, _ = _pack([wp[k] for k in _SMALL])
    m_sm, _ = _pack([ms[k] for k in _SMALL])
    v_sm, _ = _pack([vs[k] for k in _SMALL])
    shapes = dict(b_ada=b_ada.shape, norm1_g=norm1_g.shape, b_gate=b_gate.shape, ln_g=gmlp_ln_g.shape, ln_b=gmlp_ln_b.shape,
                  ws=gmlp_ws.shape, bs=gmlp_bs.shape, hg_lb=hg_lb.shape, hg_ng=hg_norm_g.shape, norm2_g=norm2_g.shape,
                  final_g=final_norm_g.shape)
    sm_out = _small_update(gathered, w_sm, m_sm, v_sm, after, [math.prod(shapes[k]) // LANES for k in _SMALL])

    def unpack(idx, k):
        return sm_out[idx * len(_SMALL) + _SMALL.index(k)].reshape(shapes[k])

    loss = sm_out[-1][0, 0]

    dmod_all = gathered.reshape(N_DEV, -1)[:, offs[0]:offs[0] + N_DEV * ada_loc]
    dmod_loc = lax.dynamic_slice_in_dim(dmod_all, me * ada_loc, ada_loc, axis=1)
    ca_t = jnp.pad(c_act[:N_DEV].T, ((0, 0), (0, LANES - N_DEV))).astype(BF16)
    dm_p = jnp.pad(dmod_loc, ((0, LANES - N_DEV), (0, 0))).astype(BF16)
    tm_a = _tile(D, 512)
    g_ada = _matmul(
        "ada_dw", ca_t, dm_p, dims=_NN, grid_mnk=(D // tm_a, 1, 1), tiles=(tm_a, ada_loc),
        a_spec=pl.BlockSpec((tm_a, LANES), lambda i, j, k: (i, 0)), b_spec=pl.BlockSpec((LANES, ada_loc), lambda i, j, k: (0, 0)),
        out_shapes=[jax.ShapeDtypeStruct((1, D, ada_loc), F32)], out_specs=[pl.BlockSpec((None, tm_a, ada_loc), lambda i, j, k: (0, i, 0))],
        epilogue=_store(F32))[0]
    upd["w_ada"] = _adamw("adamw_w_ada", w_ada[0], m_w_ada[0], v_w_ada[0], g_ada, jnp.zeros((1,), jnp.int32))
    land_and_update("scatter_proj_in_b", land_and_update("scatter_proj_in_a", upd["w_ada"][1]))

    order = ("w_ada", "b_ada", "norm1_g", "w_in", "b_gate", "ln_g", "ln_b", "ws", "bs", "hg_lb", "hg_ng", "w_branch_gmlp", "w_branch_hg",
             "w_out", "norm2_g", "w_ffn_in", "w_ffn_out", "final_g")
    outs = [loss, grad_x[None]]
    for idx in range(4):
        for k in order:
            outs.append(upd[k][idx][None] if k in upd else unpack(idx, k))
    return tuple(outs)
```

```python
import functools
import math

import jax
import jax.numpy as jnp
from jax import lax
from jax.experimental import pallas as pl
from jax.experimental.pallas import tpu as pltpu

F32 = jnp.float32
BF16 = jnp.bfloat16
N_DEV = 8
EPS = 1e-6
LANES = 128
HG_DK = 128
HG_CHUNK = 64
HG_MID = HG_CHUNK // 2 - 1
EXP_CLAMP = 80.0
VMEM_LIMIT = 48 * 1024 * 1024
ADAM_LR, ADAM_B1, ADAM_B2, ADAM_EPS, ADAM_WD, ADAM_STEP = 0.001, 0.9, 0.999, 1e-08, 0.01, 10
MESH = pl.DeviceIdType.MESH

_NN = (((1,), (0,)), ((), ()))
_NT = (((1,), (1,)), ((), ()))
_TN = (((0,), (0,)), ((), ()))


def _dot(a, b, dims=_NN):
    return lax.dot_general(a.astype(BF16), b.astype(BF16), dims, preferred_element_type=F32)


def _tile(n, target, mult=LANES):
    best = None
    for t in range(mult, min(n, target) + 1, mult):
        if n % t == 0:
            best = t
    return n if best is None else best


def _cparams(sem):
    return pltpu.CompilerParams(dimension_semantics=sem, vmem_limit_bytes=VMEM_LIMIT)


def _sigmoid(x):
    return 1.0 / (1.0 + jnp.exp(-x))


def _gelu_parts(x):
    k0 = math.sqrt(2.0 / math.pi)
    x2 = x * x
    t = jnp.tanh(k0 * (x + 0.044715 * x * x2))
    g = 0.5 * x * (1.0 + t)
    dg = 0.5 * (1.0 + t) + 0.5 * x * (1.0 - t * t) * (k0 * (1.0 + 3.0 * 0.044715 * x2))
    return g, dg


def _split3(x):
    h = x.astype(BF16)
    r = x - h.astype(F32)
    m = r.astype(BF16)
    lo = (r - m.astype(F32)).astype(BF16)
    return h, m, lo


def _ones_dot(mat01, x):
    h, m, lo = _split3(x)
    d = functools.partial(lax.dot_general, dimension_numbers=_NN, preferred_element_type=F32)
    return d(mat01, h) + d(mat01, m) + d(mat01, lo)


def _matmul(name, a, b, *, dims, grid_mnk, tiles, a_spec, b_spec, extras=(), extra_specs=(), out_shapes, out_specs, epilogue, after=None,
            sem=None):
    gm, gn, nk = grid_mnk
    tm, tn = tiles
    n_ex, n_out = len(extras), len(out_shapes)
    held = [] if after is None else [after]

    def body(*refs):
        a_ref, b_ref = refs[0], refs[1]
        ex = refs[2:2 + n_ex]
        outs = refs[2 + n_ex + len(held):2 + n_ex + len(held) + n_out]
        more = () if sem is None else (pl.program_id(0) == 0,)
        if nk == 1:
            epilogue(lax.dot_general(a_ref[...], b_ref[...], dims, preferred_element_type=F32), ex, outs, *more)
            return
        acc = refs[-1]
        k = pl.program_id(2)

        @pl.when(k == 0)
        def _():
            acc[...] = jnp.zeros_like(acc)

        acc[...] += lax.dot_general(a_ref[...], b_ref[...], dims, preferred_element_type=F32)

        @pl.when(k == nk - 1)
        def _():
            epilogue(acc[...], ex, outs, *more)

    return pl.pallas_call(
        body, name=name, grid=(gm, gn, nk), in_specs=[a_spec, b_spec, *extra_specs] + [pl.BlockSpec(memory_space=pl.ANY)] * len(held),
        out_specs=list(out_specs), out_shape=list(out_shapes), scratch_shapes=[] if nk == 1 else [pltpu.VMEM((tm, tn), F32)],
        compiler_params=_cparams(sem or ("parallel", "parallel", "arbitrary")),
    )(a, b, *extras, *held)


def _store(dtype):
    def ep(acc, ex, outs):
        outs[0][...] = acc.astype(dtype)
    return ep


def _mm_nn_stacked(name, a, wg, *, tm, tn, tk, out_dtype=F32, extras=(), extra_specs=(), out_shapes=None, out_specs=None, epilogue=None,
                   after=None):
    M, K = a.shape
    _, _, nloc = wg.shape
    N = nloc * N_DEV
    q = nloc // tn
    if out_shapes is None:
        out_shapes = [jax.ShapeDtypeStruct((M, N), out_dtype)]
        out_specs = [pl.BlockSpec((tm, tn), lambda i, j, k: (i, j))]
        epilogue = _store(out_dtype)
    return _matmul(
        name, a, wg, dims=_NN, grid_mnk=(M // tm, N // tn, K // tk), tiles=(tm, tn),
        a_spec=pl.BlockSpec((tm, tk), lambda i, j, k: (i, k)),
        b_spec=pl.BlockSpec((None, tk, tn), lambda i, j, k: (j // q, k, j % q)),
        extras=extras, extra_specs=extra_specs, out_shapes=out_shapes, out_specs=out_specs, epilogue=epilogue, after=after)


def _mm_nt_stacked(name, a_spec, a, wg, *, M, tm, tn, tk, out_dtype=F32, after=None, extras=(), extra_specs=(), out_shapes=None,
                   out_specs=None, epilogue=None, sem=None):
    _, Kw, nloc = wg.shape
    q = nloc // tk
    single = out_shapes is None
    if single:
        out_shapes = [jax.ShapeDtypeStruct((M, Kw), out_dtype)]
        out_specs = [pl.BlockSpec((tm, tn), lambda i, j, k: (i, j))]
        epilogue = _store(out_dtype)
    res = _matmul(
        name, a, wg, dims=_NT, grid_mnk=(M // tm, Kw // tn, (nloc * N_DEV) // tk), tiles=(tm, tn),
        a_spec=a_spec, b_spec=pl.BlockSpec((None, tn, tk), lambda i, j, k: (k // q, j, k % q)),
        extras=extras, extra_specs=extra_specs, out_shapes=out_shapes, out_specs=out_specs, epilogue=epilogue, after=after, sem=sem)
    return res[0] if single else res


def _mm_tn(name, a, b, b_spec, *, Mo, No, S, tm, tn, tk, stacked_nloc=None, after=None, a_off=0):
    if stacked_nloc is None:
        out_shape = jax.ShapeDtypeStruct((Mo, No), BF16)
        out_spec = pl.BlockSpec((tm, tn), lambda i, j, k: (i, j))
    else:
        q = stacked_nloc // tn
        out_shape = jax.ShapeDtypeStruct((N_DEV, Mo, stacked_nloc), BF16)
        out_spec = pl.BlockSpec((None, tm, tn), lambda i, j, k: (j // q, i, j % q))
    return _matmul(
        name, a, b, dims=_TN, grid_mnk=(Mo // tm, No // tn, S // tk), tiles=(tm, tn),
        a_spec=pl.BlockSpec((tk, tm), lambda i, j, k: (k, i + a_off)), b_spec=b_spec,
        out_shapes=[out_shape], out_specs=[out_spec], epilogue=_store(BF16), after=after)[0]


def _norm_mod(name, x, g, sc, sh):
    S, D = x.shape
    tm = _tile(S, 256, 8)

    def body(x_ref, g_ref, sc_ref, sh_ref, h_ref):
        xv = x_ref[...]
        r = lax.rsqrt(jnp.mean(xv * xv, axis=-1, keepdims=True) + EPS)
        h = (xv * r) * g_ref[...]
        h_ref[...] = (h * (1.0 + sc_ref[...]) + sh_ref[...]).astype(BF16)

    row = pl.BlockSpec((tm, D), lambda i: (i, 0))
    vec = pl.BlockSpec((1, D), lambda i: (0, 0))
    return pl.pallas_call(body, name=name, grid=(S // tm,), in_specs=[row, vec, vec, vec], out_specs=row,
                          out_shape=jax.ShapeDtypeStruct((S, D), BF16), compiler_params=_cparams(("parallel",)))(x, g, sc, sh)


def _norm_mod_bwd_rows(first, dh_v, x_ref, g_ref, sc_ref, dres_ref, dx_ref, vec_ref, o_ref=None, gt_ref=None, do_ref=None):
    @pl.when(first)
    def _():
        vec_ref[...] = jnp.zeros_like(vec_ref)

    xv, gv = x_ref[...], g_ref[...]
    r = lax.rsqrt(jnp.mean(xv * xv, axis=-1, keepdims=True) + EPS)
    xn = xv * r
    one_sc = 1.0 + sc_ref[...]
    vec_ref[0:1, :] += jnp.sum(dh_v, axis=0, keepdims=True)
    vec_ref[1:2, :] += jnp.sum(dh_v * (xn * gv), axis=0, keepdims=True)
    vec_ref[2:3, :] += jnp.sum(dh_v * one_sc * xn, axis=0, keepdims=True)
    dxn = dh_v * one_sc * gv
    dx = dres_ref[...] + r * (dxn - xn * jnp.mean(dxn * xn, axis=-1, keepdims=True))
    dx_ref[...] = dx
    if o_ref is not None:
        vec_ref[3:4, :] += jnp.sum(dx * o_ref[...], axis=0, keepdims=True)
        do_ref[...] = (dx * gt_ref[...]).astype(BF16)


def _norm_mod_bwd(name, dh, x, g, sc, dres, o=None, gt=None):
    S, D = x.shape
    tm = _tile(S, 256, 8)
    gated = o is not None

    def body(*refs):
        if gated:
            dh_ref, x_ref, g_ref, sc_ref, dres_ref, o_ref, gt_ref, dx_ref, vec_ref, do_ref = refs
        else:
            dh_ref, x_ref, g_ref, sc_ref, dres_ref, dx_ref, vec_ref = refs
            o_ref = gt_ref = do_ref = None
        _norm_mod_bwd_rows(pl.program_id(0) == 0, dh_ref[...], x_ref, g_ref, sc_ref, dres_ref, dx_ref, vec_ref, o_ref, gt_ref, do_ref)

    row = pl.BlockSpec((tm, D), lambda i: (i, 0))
    vec = pl.BlockSpec((1, D), lambda i: (0, 0))
    acc = pl.BlockSpec((8, D), lambda i: (0, 0))
    ins = [dh, x, g, sc, dres] + ([o, gt] if gated else [])
    in_specs = [row, row, vec, vec, row] + ([row, vec] if gated else [])
    out_shape = [jax.ShapeDtypeStruct((S, D), F32), jax.ShapeDtypeStruct((8, D), F32)]
    out_specs = [row, acc]
    if gated:
        out_shape.append(jax.ShapeDtypeStruct((S, D), BF16))
        out_specs.append(row)
    return pl.pallas_call(body, name=name, grid=(S // tm,), in_specs=in_specs, out_specs=out_specs, out_shape=out_shape,
                          compiler_params=_cparams(("arbitrary",)))(*ins)


def _loss_head(x3, tgt, gf, o2, gt2):
    S, D = x3.shape
    tm = _tile(S, 256, 8)

    def body(x_ref, t_ref, g_ref, o_ref, gt_ref, dx_ref, do_ref, vec_ref):
        i = pl.program_id(0)

        @pl.when(i == 0)
        def _():
            vec_ref[...] = jnp.zeros_like(vec_ref)

        xv, gv = x_ref[...], g_ref[...]
        r = lax.rsqrt(jnp.mean(xv * xv, axis=-1, keepdims=True) + EPS)
        xn = xv * r
        e = xn * gv - t_ref[...]
        tok = 0.5 * jnp.mean(e * e, axis=-1, keepdims=True)
        vec_ref[0:1, :] += jnp.broadcast_to(jnp.sum(tok, axis=0, keepdims=True), (1, D))
        dy = e * (1.0 / D)
        vec_ref[1:2, :] += jnp.sum(dy * xn, axis=0, keepdims=True)
        dxn = dy * gv
        dx = r * (dxn - xn * jnp.mean(dxn * xn, axis=-1, keepdims=True))
        dx_ref[...] = dx
        vec_ref[2:3, :] += jnp.sum(dx * o_ref[...], axis=0, keepdims=True)
        do_ref[...] = (dx * gt_ref[...]).astype(BF16)

    row = pl.BlockSpec((tm, D), lambda i: (i, 0))
    vec = pl.BlockSpec((1, D), lambda i: (0, 0))
    return pl.pallas_call(
        body, name="loss_head", grid=(S // tm,), in_specs=[row, row, vec, row, vec],
        out_specs=[row, row, pl.BlockSpec((8, D), lambda i: (0, 0))],
        out_shape=[jax.ShapeDtypeStruct((S, D), F32), jax.ShapeDtypeStruct((S, D), BF16), jax.ShapeDtypeStruct((8, D), F32)],
        compiler_params=_cparams(("arbitrary",)))(x3, tgt, gf, o2, gt2)


def _ffn_in_swiglu(h, wg):
    S, D = h.shape
    _, _, tf = wg.shape
    nf = N_DEV // 2
    F = nf * tf
    tm = _tile(S, 256, 16)

    def body(h_ref, wa_ref, wu_ref, hf_ref, fac_ref):
        hv = h_ref[...]
        a = lax.dot_general(hv, wa_ref[...], _NN, preferred_element_type=F32)
        up = lax.dot_general(hv, wu_ref[...], _NN, preferred_element_type=F32)
        sa = _sigmoid(a)
        silu = a * sa
        hf_ref[...] = (silu * up).astype(BF16)
        fac_ref[0] = (up * (sa * (1.0 + a * (1.0 - sa)))).astype(BF16)
        fac_ref[1] = silu.astype(BF16)

    return pl.pallas_call(
        body, name="ffn_in_swiglu", grid=(nf, S // tm),
        in_specs=[pl.BlockSpec((tm, D), lambda j, i: (i, 0)), pl.BlockSpec((None, D, tf), lambda j, i: (j, 0, 0)),
                  pl.BlockSpec((None, D, tf), lambda j, i: (j + nf, 0, 0))],
        out_specs=[pl.BlockSpec((tm, tf), lambda j, i: (i, j)), pl.BlockSpec((2, tm, tf), lambda j, i: (0, i, j))],
        out_shape=[jax.ShapeDtypeStruct((S, F), BF16), jax.ShapeDtypeStruct((2, S, F), BF16)],
        compiler_params=_cparams(("parallel", "parallel")))(h, wg, wg)


def _colsum2(dg2):
    _, S, D = dg2.shape
    tm = _tile(S, 256, 16)

    def body(x_ref, o_ref):
        @pl.when(pl.program_id(0) == 0)
        def _():
            o_ref[...] = jnp.zeros_like(o_ref)

        o_ref[0:1, :] += jnp.sum(x_ref[0].astype(F32), axis=0, keepdims=True)
        o_ref[1:2, :] += jnp.sum(x_ref[1].astype(F32), axis=0, keepdims=True)

    return pl.pallas_call(body, name="gate_bias_grad", grid=(S // tm,), in_specs=[pl.BlockSpec((2, tm, D), lambda i: (0, i, 0))],
                          out_specs=pl.BlockSpec((2, D), lambda i: (0, 0)), out_shape=jax.ShapeDtypeStruct((2, D), F32),
                          compiler_params=_cparams(("arbitrary",)))(dg2)


def _gmlp_common(u_ref, v_ref, lg_ref, lb_ref, ws_ref, bsb_ref, G, T, Dg):
    ug, dug = _gelu_parts(u_ref[...])
    vg, dvg = _gelu_parts(v_ref[...])
    mu = jnp.mean(vg, axis=-1, keepdims=True)
    vc = vg - mu
    rstd = lax.rsqrt(jnp.mean(vc * vc, axis=-1, keepdims=True) + EPS)
    vhat = vc * rstd
    vn = vhat * lg_ref[...] + lb_ref[...]
    row = lax.broadcasted_iota(jnp.int32, (T, T), 0)
    col = lax.broadcasted_iota(jnp.int32, (T, T), 1)
    tril = row >= col
    s = []
    for g in range(G):
        w = jnp.where(tril, ws_ref[g], 0.0)
        s.append(_dot(w, vn[:, g * Dg:(g + 1) * Dg]) + bsb_ref[g])
    return ug, dug, dvg, rstd, vhat, vn, tril, s


def _gmlp_fwd(z, ln_g, ln_b, ws, bsb, GW):
    S = z.shape[0]
    G, T, _ = ws.shape
    Dg = GW // G

    def body(u_ref, v_ref, lg_ref, lb_ref, ws_ref, bsb_ref, ya_ref):
        ug, _, _, _, _, _, _, s = _gmlp_common(u_ref, v_ref, lg_ref, lb_ref, ws_ref, bsb_ref, G, T, Dg)
        for g in range(G):
            sl = slice(g * Dg, (g + 1) * Dg)
            ya_ref[:, sl] = (ug[:, sl] * s[g]).astype(BF16)

    vec = pl.BlockSpec((1, GW), lambda c: (0, 0))
    return pl.pallas_call(
        body, name="gmlp_fwd", grid=(S // T,),
        in_specs=[pl.BlockSpec((T, GW), lambda c: (c, 0)), pl.BlockSpec((T, GW), lambda c: (c, 1)), vec, vec,
                  pl.BlockSpec((G, T, T), lambda c: (0, 0, 0)), pl.BlockSpec((G, T, Dg), lambda c: (0, 0, 0))],
        out_specs=pl.BlockSpec((T, GW), lambda c: (c, 0)), out_shape=jax.ShapeDtypeStruct((S, GW), BF16),
        compiler_params=_cparams(("parallel",)))(z, z, ln_g, ln_b, ws, bsb)


def _gmlp_bwd(z, dya, ln_g, ln_b, ws, bsb, GW):
    S = z.shape[0]
    G, T, _ = ws.shape
    Dg = GW // G
    nc = S // T

    def body(u_ref, v_ref, dya_ref, lg_ref, lb_ref, ws_ref, bsb_ref, dz_ref, dln_ref, dws_ref, dbs_ref, dbs_acc, dvh):
        c = pl.program_id(0)

        @pl.when(c == 0)
        def _():
            dln_ref[...] = jnp.zeros_like(dln_ref)
            dws_ref[...] = jnp.zeros_like(dws_ref)
            dbs_acc[...] = jnp.zeros_like(dbs_acc)

        ug, dug, dvg, rstd, vhat, vn, tril, s = _gmlp_common(u_ref, v_ref, lg_ref, lb_ref, ws_ref, bsb_ref, G, T, Dg)
        dya_v = dya_ref[...]
        for g in range(G):
            sl = slice(g * Dg, (g + 1) * Dg)
            dy_g = dya_v[:, sl]
            dz_ref[:, sl] = (dy_g * s[g] * dug[:, sl]).astype(BF16)
            ds = dy_g * ug[:, sl]
            dbs_acc[g] += ds
            w = jnp.where(tril, ws_ref[g], 0.0)
            dvn_g = _dot(w, ds, _TN)
            dws_ref[g] += jnp.where(tril, _dot(ds, vn[:, sl], _NT), 0.0)
            dln_ref[0:1, sl] += jnp.sum(dvn_g * vhat[:, sl], axis=0, keepdims=True)
            dln_ref[1:2, sl] += jnp.sum(dvn_g, axis=0, keepdims=True)
            dvh[:, sl] = dvn_g * lg_ref[:, sl]
        dvhat = dvh[...]
        m1 = jnp.mean(dvhat, axis=-1, keepdims=True)
        m2 = jnp.mean(dvhat * vhat, axis=-1, keepdims=True)
        dz_ref[:, GW:2 * GW] = (rstd * (dvhat - m1 - vhat * m2) * dvg).astype(BF16)

        @pl.when(c == nc - 1)
        def _():
            for g in range(G):
                dbs_ref[g] = jnp.sum(dbs_acc[g], axis=-1, keepdims=True)

    vec = pl.BlockSpec((1, GW), lambda c: (0, 0))
    return pl.pallas_call(
        body, name="gmlp_bwd", grid=(nc,),
        in_specs=[pl.BlockSpec((T, GW), lambda c: (c, 0)), pl.BlockSpec((T, GW), lambda c: (c, 1)),
                  pl.BlockSpec((T, GW), lambda c: (c, 0)), vec, vec,
                  pl.BlockSpec((G, T, T), lambda c: (0, 0, 0)), pl.BlockSpec((G, T, Dg), lambda c: (0, 0, 0))],
        out_specs=[pl.BlockSpec((T, 2 * GW), lambda c: (c, 0)), pl.BlockSpec((8, GW), lambda c: (0, 0)),
                   pl.BlockSpec((G, T, T), lambda c: (0, 0, 0)), pl.BlockSpec((G, T, 1), lambda c: (0, 0, 0))],
        out_shape=[jax.ShapeDtypeStruct((S, 2 * GW), BF16), jax.ShapeDtypeStruct((8, GW), F32),
                   jax.ShapeDtypeStruct((G, T, T), F32), jax.ShapeDtypeStruct((G, T, 1), F32)],
        scratch_shapes=[pltpu.VMEM((G, T, Dg), F32), pltpu.VMEM((T, GW), F32)],
        compiler_params=_cparams(("arbitrary",)))(z, z, dya, ln_g, ln_b, ws, bsb)


def _hg_common(q_ref, f_ref, hlb_ref):
    C = HG_CHUNK
    a = hlb_ref[...]
    lb = _sigmoid(a[0:1, :] - a[1:2, :])
    sig = _sigmoid(f_ref[...])
    f = lb + (1.0 - lb) * sig
    lf = jnp.log(f)
    kk = 1.0 - f
    q = q_ref[...]
    sq = _sigmoid(q)
    qa = q * sq
    row = lax.broadcasted_iota(jnp.int32, (C, C), 0)
    col = lax.broadcasted_iota(jnp.int32, (C, C), 1)
    tril = row >= col
    b = _ones_dot(tril.astype(BF16), lf)
    bm = b[HG_MID:HG_MID + 1, :]
    bl = b[C - 1:C, :]
    e_b = jnp.exp(b)
    e_qm = jnp.exp(jnp.minimum(b - bm, EXP_CLAMP))
    e_km = jnp.exp(jnp.minimum(bm - b, EXP_CLAMP))
    e_kl = jnp.exp(bl - b)
    return dict(lb=lb, sig=sig, f=f, kk=kk, q=q, sq=sq, qa=qa, tril=tril, e_b=e_b, e_qm=e_qm, e_km=e_km, e_kl=e_kl,
                e_l=jnp.exp(bl), qh=qa * e_b, qt=qa * e_qm, kt=kk * e_km, kh=kk * e_kl)


def _hg_fwd(z, hg_lb, ng, HW):
    S = z.shape[0]
    C, H, dk = HG_CHUNK, HW // HG_DK, HG_DK
    nc = S // C

    def body(q_ref, f_ref, i_ref, og_ref, hlb_ref, ng_ref, yb_ref, o_ref, st_ref, state):
        @pl.when(pl.program_id(0) == 0)
        def _():
            state[...] = jnp.zeros_like(state)

        t = _hg_common(q_ref, f_ref, hlb_ref)
        iv = i_ref[...]
        for h in range(H):
            sl = slice(h * dk, (h + 1) * dk)
            st = state[h]
            st_ref[h] = st
            a = jnp.where(t["tril"], _dot(t["qt"][:, sl], t["kt"][:, sl], _NT), 0.0)
            o_h = _dot(a, iv[:, sl]) + _dot(t["qh"][:, sl], st, _NT)
            state[h] = st * t["e_l"][:, sl] + _dot(iv[:, sl], t["kh"][:, sl], _TN)
            o_ref[:, sl] = o_h
            rr = lax.rsqrt(jnp.mean(o_h * o_h, axis=-1, keepdims=True) + EPS)
            og = og_ref[:, sl]
            yb_ref[:, sl] = (o_h * rr * ng_ref[:, sl] * (og * _sigmoid(og))).astype(BF16)

    def col(k):
        return pl.BlockSpec((C, HW), lambda c: (c, k))

    base = 2
    return pl.pallas_call(
        body, name="hgrn_fwd", grid=(nc,),
        in_specs=[col(base), col(base + 1), col(base + 2), col(base + 3),
                  pl.BlockSpec((2, HW), lambda c: (0, 0)), pl.BlockSpec((1, HW), lambda c: (0, 0))],
        out_specs=[pl.BlockSpec((C, HW), lambda c: (c, 0)), pl.BlockSpec((C, HW), lambda c: (c, 0)),
                   pl.BlockSpec((None, H, dk, dk), lambda c: (c, 0, 0, 0))],
        out_shape=[jax.ShapeDtypeStruct((S, HW), BF16), jax.ShapeDtypeStruct((S, HW), F32),
                   jax.ShapeDtypeStruct((nc, H, dk, dk), F32)],
        scratch_shapes=[pltpu.VMEM((H, dk, dk), F32)],
        compiler_params=_cparams(("arbitrary",)))(z, z, z, z, hg_lb, ng)


def _hg_bwd(z, o, states, dyb, hg_lb, ng, HW, dz_head, dz_tail):
    S = z.shape[0]
    C, H, dk = HG_CHUNK, HW // HG_DK, HG_DK
    nc = S // C
    B0 = dz_head.shape[1]
    DT = dz_tail.shape[2]
    INW = B0 + 4 * HW + 2 * DT

    def body(q_ref, f_ref, i_ref, og_ref, o_ref, st_ref, stn_ref, dyb_ref, hlb_ref, ng_ref, head_ref, tail_ref,
             dzf_ref, dng_ref, dhlb_ref, dstate, cross, dqa_buf, dkk_buf, db_buf, dlb_acc):
        c = pl.program_id(0)
        dzf_ref[:, 0:B0] = head_ref[...]
        dzf_ref[:, B0 + 4 * HW:B0 + 4 * HW + DT] = tail_ref[0]
        dzf_ref[:, B0 + 4 * HW + DT:INW] = tail_ref[1]
        dz_ref = dzf_ref.at[:, B0:B0 + 4 * HW]

        @pl.when(c == 0)
        def _():
            dstate[...] = jnp.zeros_like(dstate)
            dlb_acc[...] = jnp.zeros_like(dlb_acc)
            dng_ref[...] = jnp.zeros_like(dng_ref)

        def r16(v):
            return v.astype(BF16).astype(F32)

        t = _hg_common(q_ref, f_ref, hlb_ref)
        iv = i_ref[...]
        for h in range(H):
            sl = slice(h * dk, (h + 1) * dk)
            o_h, og, dyb_h, ng_h = o_ref[:, sl], og_ref[:, sl], dyb_ref[:, sl], ng_ref[:, sl]
            sg = _sigmoid(og)
            silu_og = og * sg
            rr = lax.rsqrt(jnp.mean(o_h * o_h, axis=-1, keepdims=True) + EPS)
            on = o_h * rr
            dng_ref[0:1, sl] += jnp.sum(dyb_h * on * silu_og, axis=0, keepdims=True)
            dz_ref[:, 3 * HW + h * dk:3 * HW + (h + 1) * dk] = (dyb_h * on * ng_h * (sg * (1.0 + og * (1.0 - sg)))).astype(BF16)
            don = dyb_h * ng_h * silu_og
            do_h = rr * (don - on * jnp.mean(don * on, axis=-1, keepdims=True))

            qt, kt, qh, kh, iv_h = t["qt"][:, sl], t["kt"][:, sl], t["qh"][:, sl], t["kh"][:, sl], iv[:, sl]
            a = jnp.where(t["tril"], _dot(qt, kt, _NT), 0.0)
            da = jnp.where(t["tril"], _dot(do_h, iv_h, _NT), 0.0)
            st, dst = st_ref[h], dstate[h]
            cross[:, sl] = jnp.sum(dst * stn_ref[h], axis=0, keepdims=True)
            dqh = _dot(do_h, st)
            dstate[h] = _dot(do_h, qh, _TN) + dst * t["e_l"][:, sl]
            div = _dot(a, do_h, _TN) + _dot(kh, dst, _NT)
            dkh = _dot(iv_h, dst)
            dqt = _dot(da, kt)
            dkt = _dot(da, qt, _TN)
            dz_ref[:, 2 * HW + h * dk:2 * HW + (h + 1) * dk] = div.astype(BF16)
            dqa_buf[:, sl] = dqh * t["e_b"][:, sl] + dqt * t["e_qm"][:, sl]
            dkk_buf[:, sl] = dkt * t["e_km"][:, sl] + dkh * t["e_kl"][:, sl]
            db_buf[:, sl] = r16(qt) * dqt - r16(kt) * dkt + r16(qh) * dqh - r16(kh) * dkh

        dqa, dkk = dqa_buf[...], dkk_buf[...]
        triu = jnp.logical_not(t["tril"]) | (lax.broadcasted_iota(jnp.int32, (C, C), 0) == lax.broadcasted_iota(jnp.int32, (C, C), 1))
        dlf = _ones_dot(triu.astype(BF16), db_buf[...]) + cross[...]
        df = dlf / t["f"] - dkk
        sig, lb = t["sig"], t["lb"]
        dz_ref[:, HW:2 * HW] = (df * (1.0 - lb) * sig * (1.0 - sig)).astype(BF16)
        dlb_acc[...] += jnp.sum(df * (1.0 - sig), axis=0, keepdims=True)
        q, sq = t["q"], t["sq"]
        dz_ref[:, 0:HW] = (dqa * (sq * (1.0 + q * (1.0 - sq)))).astype(BF16)

        @pl.when(c == nc - 1)
        def _():
            da0 = dlb_acc[...] * lb * (1.0 - lb)
            dhlb_ref[0:1, :] = da0
            dhlb_ref[1:2, :] = -da0

    def col(k):
        return pl.BlockSpec((C, HW), lambda c: (nc - 1 - c, k))

    base = 2
    return pl.pallas_call(
        body, name="hgrn_bwd", grid=(nc,),
        in_specs=[col(base), col(base + 1), col(base + 2), col(base + 3), col(0),
                  pl.BlockSpec((None, H, dk, dk), lambda c: (nc - 1 - c, 0, 0, 0)),
                  pl.BlockSpec((None, H, dk, dk), lambda c: (jnp.minimum(nc - c, nc - 1), 0, 0, 0)), col(0),
                  pl.BlockSpec((2, HW), lambda c: (0, 0)), pl.BlockSpec((1, HW), lambda c: (0, 0)),
                  pl.BlockSpec((C, B0), lambda c: (nc - 1 - c, 0)), pl.BlockSpec((2, C, DT), lambda c: (0, nc - 1 - c, 0))],
        out_specs=[pl.BlockSpec((C, INW), lambda c: (nc - 1 - c, 0)), pl.BlockSpec((8, HW), lambda c: (0, 0)),
                   pl.BlockSpec((2, HW), lambda c: (0, 0))],
        out_shape=[jax.ShapeDtypeStruct((S, INW), BF16), jax.ShapeDtypeStruct((8, HW), F32), jax.ShapeDtypeStruct((2, HW), F32)],
        scratch_shapes=[pltpu.VMEM((H, dk, dk), F32), pltpu.VMEM((1, HW), F32), pltpu.VMEM((C, HW), F32), pltpu.VMEM((C, HW), F32),
                        pltpu.VMEM((C, HW), F32), pltpu.VMEM((1, HW), F32)],
        compiler_params=_cparams(("arbitrary",)))(z, z, z, z, o, states, states, dyb, hg_lb, ng, dz_head, dz_tail)


def _position():
    x, y, c = lax.axis_index("x"), lax.axis_index("y"), lax.axis_index("c")
    return x, y, c, 4 * x + 2 * y + c


def _flip(x, y, c, k):
    return (1 - x if k & 4 else x, 1 - y if k & 2 else y, 1 - c if k & 1 else c)


def _allgather_small(name, v):
    R, L = v.shape

    def body(v_ref, out_ref, send_sems, recv_sems):
        x, y, c, me = _position()
        out_ref[me] = v_ref[...]
        copies = []
        for k in range(1, N_DEV):
            cp = pltpu.make_async_remote_copy(src_ref=v_ref, dst_ref=out_ref.at[me], send_sem=send_sems.at[k - 1],
                                              recv_sem=recv_sems.at[k - 1], device_id=_flip(x, y, c, k), device_id_type=MESH)
            cp.start()
            copies.append(cp)
        for cp in copies:
            cp.wait()

    return pl.pallas_call(
        body, name=name, out_shape=jax.ShapeDtypeStruct((N_DEV, R, L), v.dtype),
        in_specs=[pl.BlockSpec(memory_space=pltpu.VMEM)], out_specs=pl.BlockSpec(memory_space=pltpu.VMEM),
        scratch_shapes=[pltpu.SemaphoreType.DMA((N_DEV - 1,)), pltpu.SemaphoreType.DMA((N_DEV - 1,))],
        compiler_params=pltpu.CompilerParams(vmem_limit_bytes=VMEM_LIMIT),
    )(v)


def _allgather_hbm(name, shards):
    n = len(shards)

    def body(*refs):
        ins, outs = refs[:n], refs[n:2 * n]
        send_sems, recv_sems, local_sems = refs[2 * n:]
        x, y, c, me = _position()
        sibling = (x, y, 1 - c)
        chips = [(1 - x, y), (x, 1 - y), (1 - x, 1 - y)]

        def slot(px, py, pc):
            return 4 * px + 2 * py + pc

        def copy(w, k, block, to, src=None):
            dst = outs[w].at[slot(*block)]
            return pltpu.make_async_remote_copy(src_ref=dst if src is None else src, dst_ref=dst, send_sem=send_sems.at[w, k],
                                                recv_sem=recv_sems.at[w, k], device_id=to, device_id_type=MESH)

        mine, first, passed = [], [], []
        for w in range(n):
            cp = pltpu.make_async_copy(ins[w], outs[w].at[me], local_sems.at[w])
            cp.start()
            mine.append(cp)
            for j, chip in enumerate(chips):
                first.append(copy(w, 1 + j, (x, y, c), (*chip, c), src=ins[w]))
            first.append(copy(w, 0, (x, y, c), sibling, src=ins[w]))
        for cp in first:
            cp.start()
        for w in range(n):
            for j, chip in enumerate(chips):
                copy(w, 1 + j, (*chip, c), (x, y, c)).wait_recv()
                cp = copy(w, 4 + j, (*chip, c), sibling)
                cp.start()
                passed.append(cp)
        for w in range(n):
            copy(w, 0, sibling, (x, y, c)).wait_recv()
            for j, chip in enumerate(chips):
                copy(w, 4 + j, (*chip, 1 - c), (x, y, c)).wait_recv()
        for cp in first + passed:
            cp.wait_send()
        for cp in mine:
            cp.wait()

    hbm = pl.BlockSpec(memory_space=pltpu.HBM)
    return pl.pallas_call(
        body, name=name, out_shape=[jax.ShapeDtypeStruct((N_DEV, *s.shape), s.dtype) for s in shards],
        in_specs=[hbm] * n, out_specs=[hbm] * n,
        scratch_shapes=[pltpu.SemaphoreType.DMA((n, 7)), pltpu.SemaphoreType.DMA((n, 7)), pltpu.SemaphoreType.DMA((n,))],
    )(*shards)


_HBM = pl.BlockSpec(memory_space=pltpu.HBM)
_SEM = pl.BlockSpec(memory_space=pltpu.SEMAPHORE)
_EFFECT = pltpu.SideEffectType.DATAFLOW_SIDE_EFFECTING


def _split_start(name, bufs, n_sems, copies_fn, after=None):
    nb = len(bufs)
    extra = [] if after is None else [after]
    k = nb + len(extra)

    def body(*refs):
        for cp in copies_fn(refs[:nb], refs[k], refs[k + 1]):
            cp.start()
        refs[-1][...] = jnp.zeros_like(refs[-1])

    sems = pltpu.SemaphoreType.DMA((n_sems,))
    res = pl.pallas_call(
        body, name=name,
        out_shape=(sems, sems, *[pltpu.HBM(a.shape, a.dtype) for a in bufs], jax.ShapeDtypeStruct((8, LANES), F32)),
        in_specs=[_HBM] * nb + [pl.BlockSpec(memory_space=pl.ANY)] * len(extra),
        out_specs=(_SEM, _SEM, *[_HBM] * nb, pl.BlockSpec(memory_space=pltpu.VMEM)),
        input_output_aliases={i: 2 + i for i in range(nb)},
        compiler_params=pltpu.CompilerParams(has_side_effects=_EFFECT),
    )(*[pltpu.with_memory_space_constraint(a, pltpu.HBM) for a in bufs], *extra)
    return res[0], res[1], list(res[2:2 + nb]), res[-1]


def _split_wait(name, bufs, send_sems, recv_sems, after, copies_fn):
    nb = len(bufs)

    def body(*refs):
        for cp in copies_fn(refs[:nb], refs[nb], refs[nb + 1]):
            cp.wait_send()
            cp.wait_recv()

    res = pl.pallas_call(
        body, name=name, out_shape=tuple(pltpu.HBM(a.shape, a.dtype) for a in bufs),
        in_specs=[_HBM] * nb + [_SEM, _SEM, pl.BlockSpec(memory_space=pl.ANY)], out_specs=tuple([_HBM] * nb),
        input_output_aliases={i: i for i in range(nb)},
        compiler_params=pltpu.CompilerParams(has_side_effects=_EFFECT),
    )(*bufs, send_sems, recv_sems, after)
    return list(res)


def _split_relay(name, bufs, send_sems, recv_sems, after, wait_fn, n_sems, start_fn):
    nb = len(bufs)

    def body(*refs):
        for cp in wait_fn(refs[:nb], refs[nb], refs[nb + 1]):
            cp.wait_send()
            cp.wait_recv()
        for cp in start_fn(refs[:nb], refs[nb + 3], refs[nb + 4]):
            cp.start()
        refs[-1][...] = jnp.zeros_like(refs[-1])

    sems = pltpu.SemaphoreType.DMA((n_sems,))
    res = pl.pallas_call(
        body, name=name, out_shape=(sems, sems, *[pltpu.HBM(a.shape, a.dtype) for a in bufs], jax.ShapeDtypeStruct((8, LANES), F32)),
        in_specs=[_HBM] * nb + [_SEM, _SEM, pl.BlockSpec(memory_space=pl.ANY)],
        out_specs=(_SEM, _SEM, *[_HBM] * nb, pl.BlockSpec(memory_space=pltpu.VMEM)),
        input_output_aliases={i: 2 + i for i in range(nb)},
        compiler_params=pltpu.CompilerParams(has_side_effects=_EFFECT),
    )(*bufs, send_sems, recv_sems, after)
    return res[0], res[1], list(res[2:2 + nb]), res[-1]


N_CHIP = 4


def _chip_flip(x, y, k):
    return (1 - x if k & 2 else x), (1 - y if k & 1 else y)


def _gather_first_copies(n):
    def copies(bufs, send_sems, recv_sems):
        x, y, c, me = _position()
        out = []
        for w in range(n):
            for k in range(N_CHIP):
                to = (x, y, 1 - c) if k == 0 else (*_chip_flip(x, y, k), c)
                out.append(pltpu.make_async_remote_copy(
                    src_ref=bufs[w], dst_ref=bufs[n + w].at[me], send_sem=send_sems.at[w * N_CHIP + k],
                    recv_sem=recv_sems.at[w * N_CHIP + k], device_id=to, device_id_type=MESH))
        return out
    return copies


def _gather_relay_copies(n):
    def copies(bufs, send_sems, recv_sems):
        x, y, c, _ = _position()
        out = []
        for w in range(n):
            for k in range(1, N_CHIP):
                px, py = _chip_flip(x, y, k)
                blk = bufs[n + w].at[4 * px + 2 * py + c]
                out.append(pltpu.make_async_remote_copy(
                    src_ref=blk, dst_ref=blk, send_sem=send_sems.at[w * (N_CHIP - 1) + k - 1],
                    recv_sem=recv_sems.at[w * (N_CHIP - 1) + k - 1], device_id=(x, y, 1 - c), device_id_type=MESH))
        return out
    return copies


def _small_gather_copies(bufs, send_sems, recv_sems):
    x, y, c, me = _position()
    return [pltpu.make_async_remote_copy(src_ref=bufs[0], dst_ref=bufs[1].at[me], send_sem=send_sems.at[k - 1], recv_sem=recv_sems.at[k - 1],
                                         device_id=_flip(x, y, c, k), device_id_type=MESH) for k in range(1, N_DEV)]


def _xor(a, b):
    return a + b - 2 * a * b


def _forward_first_copies(n):
    def copies(bufs, send_sems, recv_sems):
        x, y, c, me = _position()
        out = []
        for w in range(n):
            for k, to in enumerate([(x, y, 1 - c), (1 - x, y, c), (x, 1 - y, c)]):
                out.append(pltpu.make_async_remote_copy(
                    src_ref=bufs[w], dst_ref=bufs[n + w].at[me], send_sem=send_sems.at[w * 3 + k],
                    recv_sem=recv_sems.at[w * 3 + k], device_id=to, device_id_type=MESH))
        return out
    return copies


def _forward_second_copies(n):
    def copies(bufs, send_sems, recv_sems):
        x, y, c, _ = _position()
        out = []
        for w in range(n):
            half = bufs[n + w].shape[1] // 2
            for k, (src_chip, rows, to) in enumerate([((1 - x, y), pl.ds(0, half), (x, 1 - y, c)), ((x, 1 - y), pl.ds(half, half), (1 - x, y, c))]):
                blk = bufs[n + w].at[4 * src_chip[0] + 2 * src_chip[1] + c, rows]
                out.append(pltpu.make_async_remote_copy(src_ref=blk, dst_ref=blk, send_sem=send_sems.at[w * 4 + k],
                                                        recv_sem=recv_sems.at[w * 4 + k], device_id=to, device_id_type=MESH))
            for k, (px, py) in enumerate([(1 - x, y), (x, 1 - y)]):
                blk = bufs[n + w].at[4 * px + 2 * py + c]
                out.append(pltpu.make_async_remote_copy(src_ref=blk, dst_ref=blk, send_sem=send_sems.at[w * 4 + 2 + k],
                                                        recv_sem=recv_sems.at[w * 4 + 2 + k], device_id=(x, y, 1 - c), device_id_type=MESH))
        return out
    return copies


def _forward_third_copies(n):
    def copies(bufs, send_sems, recv_sems):
        x, y, c, _ = _position()
        out = []
        for w in range(n):
            blk = bufs[n + w].at[4 * (1 - x) + 2 * (1 - y) + c]
            out.append(pltpu.make_async_remote_copy(src_ref=blk, dst_ref=blk, send_sem=send_sems.at[w], recv_sem=recv_sems.at[w],
                                                    device_id=(x, y, 1 - c), device_id_type=MESH))
        return out
    return copies


def _to_sibling_copies(n):
    def copies(bufs, send_sems, recv_sems):
        x, y, c, _ = _position()
        out = []
        for w in range(n):
            for q in range(N_CHIP):
                out.append(pltpu.make_async_remote_copy(
                    src_ref=bufs[w].at[2 * q + 1 - c], dst_ref=bufs[n + w].at[q], send_sem=send_sems.at[w * N_CHIP + q],
                    recv_sem=recv_sems.at[w * N_CHIP + q], device_id=(x, y, 1 - c), device_id_type=MESH))
        return out
    return copies


def _to_owner_copies(n):
    def copies(bufs, send_sems, recv_sems):
        x, y, c, _ = _position()
        out = []
        for w in range(n):
            for k in range(1, N_CHIP):
                px, py = (1 - x if k & 2 else x), (1 - y if k & 1 else y)
                out.append(pltpu.make_async_remote_copy(
                    src_ref=bufs[w].at[2 * px + py], dst_ref=bufs[n + w].at[k - 1], send_sem=send_sems.at[w * (N_CHIP - 1) + k - 1],
                    recv_sem=recv_sems.at[w * (N_CHIP - 1) + k - 1], device_id=(px, py, c), device_id_type=MESH))
        return out
    return copies


def _chip_sum(name, stack, landed, c_idx):
    _, R, C = stack.shape
    tr = _tile(R, max(16, 1048576 // C), 16)

    def body(c_ref, a_ref, b_ref, o_ref):
        o_ref[...] = (a_ref[...].astype(F32) + b_ref[...].astype(F32)).astype(o_ref.dtype)

    return pl.pallas_call(
        body, name=name,
        grid_spec=pltpu.PrefetchScalarGridSpec(
            num_scalar_prefetch=1, grid=(N_CHIP, R // tr),
            in_specs=[pl.BlockSpec((None, tr, C), lambda q, i, c_ref: (2 * q + c_ref[0], i, 0)),
                      pl.BlockSpec((None, tr, C), lambda q, i, c_ref: (q, i, 0))],
            out_specs=pl.BlockSpec((None, tr, C), lambda q, i, c_ref: (q, i, 0))),
        out_shape=jax.ShapeDtypeStruct((N_CHIP, R, C), stack.dtype),
        compiler_params=_cparams(("parallel", "parallel")))(c_idx, stack, landed)


def _ada_mod(c16, w):
    _, D = c16.shape
    n = w.shape[1]
    tk = _tile(D, 512)
    nk = D // tk

    def body(c_ref, w_ref, o_ref, ca_ref):
        @pl.when(pl.program_id(0) == 0)
        def _():
            o_ref[...] = jnp.zeros_like(o_ref)

        cv = c_ref[...]
        ca = cv * _sigmoid(cv)
        ca_ref[...] = ca
        o_ref[...] += _dot(ca, w_ref[...])

    return pl.pallas_call(
        body, name="ada_mod", grid=(nk,),
        in_specs=[pl.BlockSpec((16, tk), lambda k: (0, k)), pl.BlockSpec((tk, n), lambda k: (k, 0))],
        out_specs=[pl.BlockSpec((16, n), lambda k: (0, 0)), pl.BlockSpec((16, tk), lambda k: (0, k))],
        out_shape=[jax.ShapeDtypeStruct((16, n), F32), jax.ShapeDtypeStruct((16, D), F32)],
        compiler_params=_cparams(("arbitrary",)))(c16, w)


def _cast_shard(name, wf, slot):
    r, c = wf.shape
    tr = _tile(r, max(16, 1048576 // c), 16)

    def body(slot_ref, w_ref, s_ref, g_ref):
        v = w_ref[...].astype(BF16)
        s_ref[...] = v
        g_ref[...] = v

    return pl.pallas_call(
        body, name=name,
        grid_spec=pltpu.PrefetchScalarGridSpec(
            num_scalar_prefetch=1, grid=(r // tr,), in_specs=[pl.BlockSpec((tr, c), lambda i, s: (i, 0))],
            out_specs=[pl.BlockSpec((tr, c), lambda i, s: (i, 0)), pl.BlockSpec((None, tr, c), lambda i, s: (s[0], i, 0))]),
        out_shape=[jax.ShapeDtypeStruct((r, c), BF16), jax.ShapeDtypeStruct((N_DEV, r, c), BF16)],
        compiler_params=_cparams(("parallel",)))(slot, wf)


def _adam_math(w, g, m, v):
    m2 = ADAM_B1 * m + (1.0 - ADAM_B1) * g
    v2 = ADAM_B2 * v + (1.0 - ADAM_B2) * (g * g)
    m_hat = m2 / (1.0 - ADAM_B1 ** ADAM_STEP)
    v_hat = v2 / (1.0 - ADAM_B2 ** ADAM_STEP)
    delta = -ADAM_LR * (m_hat / (jnp.sqrt(v_hat) + ADAM_EPS) + ADAM_WD * w)
    return delta, m2, v2


def _adamw(name, w, m, v, own, own_slot, parts=(), row0=0, into=None):
    R, C = w.shape
    Rp = own.shape[1]
    tr = _tile(Rp, max(16, 393216 // C), 16)
    off = row0 // tr
    n_p = len(parts)
    held = [] if into is None else list(into)

    def body(slot_ref, *refs):
        w_ref, m_ref, v_ref, own_ref = refs[:4]
        g_ref, d_ref, m2_ref, v2_ref = refs[4 + n_p + len(held):]
        g = own_ref[...].astype(F32)
        for p_ref in refs[4:4 + n_p]:
            for s in range(p_ref.shape[0]):
                g = g + p_ref[s].astype(F32)
        delta, m2, v2 = _adam_math(w_ref[...], g, m_ref[...], v_ref[...])
        g_ref[...] = g
        d_ref[...] = delta
        m2_ref[...] = m2
        v2_ref[...] = v2

    blk = pl.BlockSpec((tr, C), lambda i, s: (i + off, 0))
    out = jax.ShapeDtypeStruct((R, C), F32)
    return pl.pallas_call(
        body, name=name,
        grid_spec=pltpu.PrefetchScalarGridSpec(
            num_scalar_prefetch=1, grid=(Rp // tr,),
            in_specs=[blk, blk, blk, pl.BlockSpec((None, tr, C), lambda i, s: (s[0], i, 0))]
            + [pl.BlockSpec((a.shape[0], tr, C), lambda i, s: (0, i, 0)) for a in parts]
            + [pl.BlockSpec(memory_space=pl.ANY)] * len(held),
            out_specs=[blk] * 4),
        out_shape=[out] * 4, input_output_aliases={5 + n_p + i: i for i in range(len(held))},
        compiler_params=_cparams(("parallel",)))(own_slot, w, m, v, own, *parts, *held)


def _small_update(gathered, w, m, v, after, rows):
    _, R, L = gathered.shape
    rs = w.shape[0]
    n = len(rows)
    assert all(r % 8 == 0 for r in rows) and sum(rows) <= rs and rs + 8 <= R

    def body(p_ref, w_ref, m_ref, v_ref, after_ref, *outs):
        g = p_ref[0]
        for p in range(1, N_DEV):
            g = g + p_ref[p]
        kinds = (g,) + _adam_math(w_ref[...], g[0:rs, :], m_ref[...], v_ref[...])
        at = 0
        for k, r in enumerate(rows):
            for idx, val in enumerate(kinds):
                outs[idx * n + k][...] = val[at:at + r, :]
            at += r
        outs[4 * n][...] = g[at:at + 8, :]

    vm = pl.BlockSpec(memory_space=pltpu.VMEM)
    shapes = [jax.ShapeDtypeStruct((r, L), F32) for _ in range(4) for r in rows] + [jax.ShapeDtypeStruct((8, L), F32)]
    return pl.pallas_call(body, name="small_update", in_specs=[vm] * 4 + [pl.BlockSpec(memory_space=pl.ANY)], out_specs=[vm] * len(shapes),
                          out_shape=shapes, compiler_params=pltpu.CompilerParams(vmem_limit_bytes=VMEM_LIMIT))(gathered, w, m, v, after)


class _Fetched(dict):
    def __init__(self, fetch):
        super().__init__()
        self.fetch = fetch

    def first(self, key, after):
        self[key] = self.fetch(key, after)
        return self[key]


def _local_step(x, tgt, mod, p, fetch, F, scatter=None):
    S, D = x.shape
    GW, HW = p["ln_g"].shape[1], p["hg_ng"].shape[1]
    G, T, _ = p["ws"].shape
    w = _Fetched(fetch)
    INW = 2 * GW + 4 * HW + 2 * D
    in_loc, br_loc, fi_loc = INW // N_DEV, D // N_DEV, 2 * F // N_DEV
    assert GW == HW and F % fi_loc == 0
    sh1, sc1, gt1, sh2, sc2, gt2 = (mod[:, k * D:(k + 1) * D] for k in range(6))
    bsb = jnp.broadcast_to(p["bs"][:, :, None], (G, T, GW // G))

    tm = _tile(S, 1024, 16)
    tmh = _tile(S, 512, 16)
    tn_in = _tile(in_loc, 1280)
    tn_d = _tile(D, 512)
    tn_br = _tile(br_loc, 512)
    tk_s = S
    tm_w = _tile(D, 1024)
    g_off = 2 * GW + 4 * HW

    h1 = _norm_mod("norm1", x, p["norm1_g"], sc1, sh1)
    def zero(token):
        return 0.0 if token is None else token[0:1, 0:1]

    z = _mm_nn_stacked("proj_in", h1, w.first("in", h1), tm=tm, tn=tn_in, tk=D)[0]
    t_mixer = w.first("mixer_last_hop", z)
    ya = _gmlp_fwd(z, p["ln_g"], p["ln_b"], p["ws"], bsb, GW)
    yb, o_hg, states = _hg_fwd(z, p["hg_lb"], p["hg_ng"] + zero(t_mixer), HW)
    flat = {k: jnp.swapaxes(w.first(k, yb), 0, 1).reshape(GW, D) for k in ("bg", "bh")}
    tn_f = _tile(D, 1024)
    pa = _matmul(
        "branch_gmlp", ya, flat["bg"], dims=_NN, grid_mnk=(S // tm, D // tn_f, 1), tiles=(tm, tn_f),
        a_spec=pl.BlockSpec((tm, GW), lambda i, j, k: (i, 0)), b_spec=pl.BlockSpec((GW, tn_f), lambda i, j, k: (0, j)),
        out_shapes=[jax.ShapeDtypeStruct((S, D), F32)], out_specs=[pl.BlockSpec((tm, tn_f), lambda i, j, k: (i, j))], epilogue=_store(F32))[0]
    t_fi = w.first("fi_early", pa)

    def gates(ga_ref, gb_ref, ba_ref, bb_ref):
        return _sigmoid(ga_ref[...] + ba_ref[...]), _sigmoid(gb_ref[...] + bb_ref[...])

    def gate_specs(tn_, tm_=tm):
        o1, o2 = g_off // tn_, (g_off + D) // tn_
        return [pl.BlockSpec((tm_, tn_), lambda i, j, k: (i, o1 + j)), pl.BlockSpec((tm_, tn_), lambda i, j, k: (i, o2 + j)),
                pl.BlockSpec((1, tn_), lambda i, j, k: (0, j)), pl.BlockSpec((1, tn_), lambda i, j, k: (0, D // tn_ + j))]

    def merge_ep(acc, ex, outs):
        ga, gb = gates(*ex[1:5])
        outs[0][...] = acc
        outs[1][...] = (ga * ex[0][...] + gb * acc).astype(BF16)

    tile_o = pl.BlockSpec((tmh, tn_f), lambda i, j, k: (i, j))
    pb, y = _matmul(
        "branch_hg_merge", yb, flat["bh"], dims=_NN, grid_mnk=(S // tmh, D // tn_f, 1), tiles=(tmh, tn_f),
        a_spec=pl.BlockSpec((tmh, HW), lambda i, j, k: (i, 0)), b_spec=pl.BlockSpec((HW, tn_f), lambda i, j, k: (0, j)),
        extras=[pa, z, z, p["b_gate"], p["b_gate"]], extra_specs=[tile_o, *gate_specs(tn_f, tmh)],
        out_shapes=[jax.ShapeDtypeStruct((S, D), F32), jax.ShapeDtypeStruct((S, D), BF16)], out_specs=[tile_o, tile_o],
        epilogue=merge_ep, after=t_fi)

    def resid_ep(acc, ex, outs):
        outs[0][...] = acc
        outs[1][...] = ex[0][...] + ex[1][...] * acc

    def resid_mm(name, a, b, res, gt, tm_):
        K = a.shape[1]
        t_o = pl.BlockSpec((tm_, tn_d), lambda i, j, k: (i, j))
        return _matmul(
            name, a, b, dims=_NN, grid_mnk=(S // tm_, D // tn_d, 1), tiles=(tm_, tn_d),
            a_spec=pl.BlockSpec((tm_, K), lambda i, j, k: (i, 0)), b_spec=pl.BlockSpec((K, tn_d), lambda i, j, k: (0, j)),
            extras=[res, gt], extra_specs=[t_o, pl.BlockSpec((1, tn_d), lambda i, j, k: (0, j))],
            out_shapes=[jax.ShapeDtypeStruct((S, D), F32)] * 2, out_specs=[t_o, t_o], epilogue=resid_ep)

    o1, xm = resid_mm("proj_out", y, w.first("out", z), x, gt1 + zero(w.first("fi_last_hop", y)), tm)
    h2 = _norm_mod("norm2", xm, p["norm2_g"], sc2, sh2)
    hf, hf_fac = _ffn_in_swiglu(h2, w.first("fi", h2))
    o2, x3 = resid_mm("ffn_out", hf, w.first("fo", hf), xm, gt2, tmh)
    dx3, do2, vec_l = _loss_head(x3, tgt, p["final_g"], o2, gt2)

    nf = F // fi_loc

    def dswiglu_ep(acc, ex, outs):
        outs[0][0] = (acc * ex[0][0].astype(F32)).astype(BF16)
        outs[0][1] = (acc * ex[0][1].astype(F32)).astype(BF16)

    pair = pl.BlockSpec((2, tmh, fi_loc), lambda i, j, k: (0, i, j))
    dab = _matmul(
        "ffn_out_dx", do2, w["fo"], dims=_NT, grid_mnk=(S // tmh, nf, 1), tiles=(tmh, fi_loc),
        a_spec=pl.BlockSpec((tmh, D), lambda i, j, k: (i, 0)), b_spec=pl.BlockSpec((fi_loc, D), lambda i, j, k: (j, 0)),
        extras=[hf_fac], extra_specs=[pair], out_shapes=[jax.ShapeDtypeStruct((2, S, F), BF16)], out_specs=[pair],
        epilogue=dswiglu_ep)[0]
    start = (lambda name, grads: scatter[0](name, grads)) if scatter is not None else (lambda name, grads: None)
    push = (lambda name, after: scatter[1](name, after)) if scatter is not None else (lambda name, after: None)

    tm_f = _tile(F, 512)
    g_fo = _mm_tn("ffn_out_dw", hf, do2, pl.BlockSpec((tk_s, D), lambda i, j, k: (k, j)), Mo=F, No=D, S=S, tm=tm_f, tn=D, tk=tk_s)
    g_fi = _mm_tn("ffn_in_dw", h2, dab, pl.BlockSpec((None, tk_s, fi_loc), lambda i, j, k: (j // nf, k, j % nf)),
                  Mo=D, No=2 * F, S=S, tm=tm_w, tn=fi_loc, tk=tk_s, stacked_nloc=fi_loc, after=g_fo)
    t_ffn = start("scatter_ffn", dict(fo=g_fo, fi=g_fi))
    dh2 = _mm_nt_stacked("ffn_in_dx", pl.BlockSpec((None, tm, fi_loc), lambda i, j, k: (k // nf, i, k % nf)), dab, w["fi"],
                         M=S, tm=tm, tn=tm_w, tk=fi_loc, after=t_ffn)
    dxm, vec2, do1 = _norm_mod_bwd("norm2_bwd", dh2, xm, p["norm2_g"], sc2, dx3, o1, gt1)
    t_ffn = push("scatter_ffn", dxm)

    def dmerge_ep(acc, ex, outs):
        ga, gb = gates(*ex[2:6])
        outs[0][...] = (acc * ga).astype(BF16)
        outs[1][...] = (acc * gb).astype(BF16)
        outs[2][0] = (acc * ex[0][...] * ga * (1.0 - ga)).astype(BF16)
        outs[2][1] = (acc * ex[1][...] * gb * (1.0 - gb)).astype(BF16)

    t_o = pl.BlockSpec((tm, tn_d), lambda i, j, k: (i, j))
    dpa, dpb, dg2 = _matmul(
        "proj_out_dx", do1, w["out"], dims=_NT, grid_mnk=(S // tm, D // tn_d, 1), tiles=(tm, tn_d),
        a_spec=pl.BlockSpec((tm, D), lambda i, j, k: (i, 0)), b_spec=pl.BlockSpec((tn_d, D), lambda i, j, k: (j, 0)),
        extras=[pa, pb, z, z, p["b_gate"], p["b_gate"]], extra_specs=[t_o, t_o, *gate_specs(tn_d)],
        out_shapes=[jax.ShapeDtypeStruct((S, D), BF16), jax.ShapeDtypeStruct((S, D), BF16), jax.ShapeDtypeStruct((2, S, D), BF16)],
        out_specs=[t_o, t_o, pl.BlockSpec((2, tm, tn_d), lambda i, j, k: (0, i, j))], epilogue=dmerge_ep, after=t_ffn)
    g_out = _mm_tn("proj_out_dw", y, do1, pl.BlockSpec((tk_s, D), lambda i, j, k: (k, j)), Mo=D, No=D, S=S, tm=tn_d, tn=D, tk=tk_s)
    tn_g = _tile(GW, 512)
    b_br = pl.BlockSpec((tk_s, br_loc), lambda i, j, k: (k, j))
    g_bg = _mm_tn("branch_gmlp_dw", ya, dpa, b_br, Mo=GW, No=D, S=S, tm=tn_g, tn=br_loc, tk=tk_s, stacked_nloc=br_loc)
    g_bh = _mm_tn("branch_hg_dw", yb, dpb, b_br, Mo=HW, No=D, S=S, tm=tn_g, tn=br_loc, tk=tk_s, stacked_nloc=br_loc)
    t_mix = start("scatter_mixer", dict(out=g_out, bg=g_bg, bh=g_bh))
    def branch_dx(name, dp, w_flat):
        return _matmul(
            name, dp, w_flat, dims=_NT, grid_mnk=(S // tm, GW // tn_g, 1), tiles=(tm, tn_g),
            a_spec=pl.BlockSpec((tm, D), lambda i, j, k: (i, 0)), b_spec=pl.BlockSpec((tn_g, D), lambda i, j, k: (j, 0)),
            out_shapes=[jax.ShapeDtypeStruct((S, GW), F32)], out_specs=[pl.BlockSpec((tm, tn_g), lambda i, j, k: (i, j))],
            epilogue=_store(F32), after=t_mix)[0]

    dya = branch_dx("branch_gmlp_dx", dpa, flat["bg"])
    dyb = branch_dx("branch_hg_dx", dpb, flat["bh"])
    db_gate = _colsum2(dg2)
    dz_gmlp, dln, dws, dbs = _gmlp_bwd(z, dya, p["ln_g"], p["ln_b"], p["ws"], bsb, GW)
    t_mix = push("scatter_mixer", dz_gmlp)
    dz, dng, dhlb = _hg_bwd(z, o_hg, states, dyb, p["hg_lb"], p["hg_ng"] + zero(t_mix), HW, dz_gmlp, dg2)
    half = D // 2
    tm_h = _tile(half, 1024)
    g_in = []
    t_in = None
    for hname, h in (("a", 0), ("b", 1)):
        g_in.append(_mm_tn("proj_in_dw_" + hname, h1, dz, pl.BlockSpec((tk_s, in_loc), lambda i, j, k: (k, j)), Mo=half, No=INW, S=S,
                           tm=tm_h, tn=in_loc, tk=tk_s, stacked_nloc=in_loc, after=t_in, a_off=h * (half // tm_h)))
        t_in = start("scatter_proj_in_" + hname, {"w_in_" + hname: g_in[-1]})
    t_in = push("scatter_proj_in_a", t_in)
    dh1 = _mm_nt_stacked("proj_in_dx", pl.BlockSpec((tm, in_loc), lambda i, j, k: (i, k)), dz, w["in"], M=S, tm=tm, tn=tm_w, tk=in_loc,
                         after=t_in)
    dx, vec1 = _norm_mod_bwd("norm1_bwd", dh1, x, p["norm1_g"], sc1, dxm)

    dmod = jnp.concatenate([vec1[0:1], vec1[1:2], vec2[3:4], vec2[0:1], vec2[1:2], vec_l[2:3]], axis=1)
    small = dict(norm1_g=vec1[2:3], b_gate=db_gate.reshape(1, 2 * D), ln_g=dln[0:1], ln_b=dln[1:2], ws=dws, bs=dbs.reshape(G, T),
                 hg_lb=dhlb, hg_ng=dng[0:1], norm2_g=vec2[2:3], final_g=vec_l[1:2], loss=vec_l[0:1, 0:LANES])
    big = dict(w_in_a=g_in[0], w_in_b=g_in[1], bg=g_bg, bh=g_bh, out=g_out, fi=g_fi, fo=g_fo)
    return dx, big, small, dmod


_SMALL = ("b_ada", "norm1_g", "b_gate", "ln_g", "ln_b", "ws", "bs", "hg_lb", "hg_ng", "norm2_g", "final_g")


def _pack(parts, rows_mult=8):
    flat = [a.reshape(-1) for a in parts]
    offs, n = [], 0
    for a in flat:
        offs.append(n)
        n += a.shape[0]
    pad = (-n) % (LANES * rows_mult)
    if pad:
        flat.append(jnp.zeros((pad,), F32))
    return jnp.concatenate(flat).reshape(-1, LANES), offs


def kernel(x, c, w_ada, b_ada, norm1_g, w_in, b_gate, gmlp_ln_g, gmlp_ln_b, gmlp_ws, gmlp_bs, hg_lb, hg_norm_g, w_branch_gmlp, w_branch_hg, w_out, norm2_g, w_ffn_in, w_ffn_out, final_norm_g, loss_target, m_w_ada, m_b_ada, m_norm1_g, m_w_in, m_b_gate, m_gmlp_ln_g, m_gmlp_ln_b, m_gmlp_ws, m_gmlp_bs, m_hg_lb, m_hg_norm_g, m_w_branch_gmlp, m_w_branch_hg, m_w_out, m_norm2_g, m_w_ffn_in, m_w_ffn_out, m_final_norm_g, v_w_ada, v_b_ada, v_norm1_g, v_w_in, v_b_gate, v_gmlp_ln_g, v_gmlp_ln_b, v_gmlp_ws, v_gmlp_bs, v_hg_lb, v_hg_norm_g, v_w_branch_gmlp, v_w_branch_hg, v_w_out, v_norm2_g, v_w_ffn_in, v_w_ffn_out, v_final_norm_g):
    S, D = x.shape[1], x.shape[2]
    ada_loc = w_ada.shape[2]
    me = 4 * lax.axis_index("x") + 2 * lax.axis_index("y") + lax.axis_index("c")
    me_idx = me.astype(jnp.int32).reshape(1)

    def empty_hbm(shape, dtype):
        return pltpu.with_memory_space_constraint(lax.empty(shape, dtype), pltpu.HBM)

    groups = dict(gather_in=dict(keys=["in"], src=[w_in], forward=True),
                  gather_mixer=dict(keys=["bg", "bh", "out"], src=[w_branch_gmlp, w_branch_hg, w_out], forward=False),
                  gather_ffn_in=dict(keys=["fi"], src=[w_ffn_in], forward=True),
                  gather_ffn_out=dict(keys=["fo"], src=[w_ffn_out], forward=False))
    group_of = {k: gname for gname, g in groups.items() for k in g["keys"]}

    def first_hop(gname, after):
        g = groups[gname]
        n = len(g["keys"])
        cast = [_cast_shard(f"{gname}_cast_{k}", a[0], me_idx) for k, a in zip(g["keys"], g["src"])]
        shards, outs = [s for s, _ in cast], [o for _, o in cast]
        if g["forward"]:
            *g["hop"], token = _split_start(gname + "_hop1", shards + outs, n * 3, _forward_first_copies(n), after=after)
        else:
            *g["hop"], token = _split_start(gname + "_hop1", shards + outs, n * N_CHIP, _gather_first_copies(n), after=after)
        return token

    def second_hop(gname, after):
        g = groups[gname]
        n = len(g["keys"])
        *g["hop"], token = _split_relay(gname + "_hop2", g["hop"][2], g["hop"][0], g["hop"][1], after,
                                        _forward_first_copies(n), n * 4, _forward_second_copies(n))
        return token

    def last_hop(gname, after):
        g = groups[gname]
        n = len(g["keys"])
        send_sems, recv_sems, bufs = g["hop"]
        if g["forward"]:
            *g["hop"], token = _split_relay(gname + "_hop3", bufs, send_sems, recv_sems, after, _forward_second_copies(n), n, _forward_third_copies(n))
        else:
            *g["hop"], token = _split_relay(gname + "_relay", bufs, send_sems, recv_sems, after,
                                            _gather_first_copies(n), n * (N_CHIP - 1), _gather_relay_copies(n))
        g["last"] = True
        return token

    def finish(gname, after):
        g = groups[gname]
        n = len(g["keys"])
        if "last" not in g:
            last_hop(gname, after)
        send_sems, recv_sems, bufs = g["hop"]
        bufs = _split_wait(gname + "_wait", bufs, send_sems, recv_sems, after, _forward_third_copies(n) if g["forward"] else _gather_relay_copies(n))
        g["done"] = dict(zip(g["keys"], bufs[n:]))

    c_all = _allgather_small("gather_c", c.reshape(D // LANES, LANES)).reshape(N_DEV, D)
    token = first_hop("gather_in", c_all)
    mod_cols, c_act = _ada_mod(jnp.pad(c_all, ((0, 16 - N_DEV), (0, 0))) + token[0:1, 0:1], w_ada[0])
    mod_vec = mod_cols[:N_DEV].reshape(-1, LANES)
    mg_send, mg_recv, mg_bufs, token = _split_start(
        "gather_mod_start", [mod_vec, lax.dynamic_update_slice(lax.empty((N_DEV, *mod_vec.shape), F32), mod_vec[None], (me, 0, 0))],
        N_DEV - 1, _small_gather_copies)
    token = second_hop("gather_in", token)
    token = first_hop("gather_ffn_in", first_hop("gather_mixer", token))
    mod_all = _split_wait("gather_mod_wait", mg_bufs, mg_send, mg_recv, token, _small_gather_copies)[1].reshape(N_DEV, N_DEV, ada_loc)
    mod = lax.dynamic_index_in_dim(mod_all, me, axis=1, keepdims=False).reshape(1, N_DEV * ada_loc) + b_ada

    def fetch(key, after):
        if key == "in":
            finish("gather_in", after)
        elif key == "fi_early":
            return first_hop("gather_ffn_out", second_hop("gather_ffn_in", after))
        elif key in ("mixer_last_hop", "fi_last_hop"):
            return last_hop("gather_mixer" if key == "mixer_last_hop" else "gather_ffn_in", after)
        elif "done" not in groups[group_of[key]]:
            finish(group_of[key], after)
        arr = groups[group_of[key]]["done"][key]
        return arr.reshape(-1, D) if key in ("out", "fo") else arr

    p = dict(norm1_g=norm1_g, b_gate=b_gate, ln_g=gmlp_ln_g, ln_b=gmlp_ln_b, ws=gmlp_ws[0], bs=gmlp_bs[0], hg_lb=hg_lb,
             hg_ng=hg_norm_g, norm2_g=norm2_g, final_g=final_norm_g.reshape(1, D))

    in_flight = {}
    c_idx = lax.axis_index("c").astype(jnp.int32).reshape(1)
    chip_idx = (2 * lax.axis_index("x") + lax.axis_index("y")).astype(jnp.int32).reshape(1)

    def scatter_start(name, grads):
        keys = list(grads)
        n = len(keys)
        stacks = [grads[k].reshape(N_DEV, -1, grads[k].shape[-1]) for k in keys]
        lands = [empty_hbm((N_CHIP, *g.shape[1:]), g.dtype) for g in stacks]
        send_sems, recv_sems, bufs, token = _split_start(name + "_d2d", stacks + lands, n * N_CHIP, _to_sibling_copies(n))
        in_flight[name] = dict(keys=keys, stage1=(send_sems, recv_sems, bufs))
        return token

    def scatter_push(name, after):
        f = in_flight[name]
        n = len(f["keys"])
        send_sems, recv_sems, bufs = f["stage1"]
        bufs = _split_wait(name + "_d2d_wait", bufs, send_sems, recv_sems, after, _to_sibling_copies(n))
        sums = [_chip_sum(f"{name}_sum_{k}", bufs[i], bufs[n + i], c_idx) for i, k in enumerate(f["keys"])]
        lands = [empty_hbm((N_CHIP - 1, *s.shape[1:]), s.dtype) for s in sums]
        send_sems, recv_sems, bufs, token = _split_start(name + "_ici", sums + lands, n * (N_CHIP - 1), _to_owner_copies(n))
        f["stage2"] = (send_sems, recv_sems, bufs)
        return token

    grad_x, _, small, dmod = _local_step(x[0], loss_target[0], mod, p, fetch, w_ffn_out.shape[1] * N_DEV, (scatter_start, scatter_push))

    small["b_ada"] = dmod
    packed, offs = _pack([small[k] for k in _SMALL] + [small["loss"]])
    sg_send, sg_recv, sg_bufs, t_tail = _split_start(
        "gather_small_start", [packed, lax.dynamic_update_slice(lax.empty((N_DEV, *packed.shape), F32), packed[None], (me, 0, 0))],
        N_DEV - 1, _small_gather_copies)
    t_tail = scatter_push("scatter_proj_in_b", t_tail)
    big_w = dict(w_in=(w_in, m_w_in, v_w_in, "w_in"), bg=(w_branch_gmlp, m_w_branch_gmlp, v_w_branch_gmlp, "w_branch_gmlp"),
                 bh=(w_branch_hg, m_w_branch_hg, v_w_branch_hg, "w_branch_hg"), out=(w_out, m_w_out, v_w_out, "w_out"),
                 fi=(w_ffn_in, m_w_ffn_in, v_w_ffn_in, "w_ffn_in"), fo=(w_ffn_out, m_w_ffn_out, v_w_ffn_out, "w_ffn_out"))
    upd = {}

    def land_and_update(name, after):
        keys = in_flight[name]["keys"]
        n = len(keys)
        send_sems, recv_sems, bufs = in_flight[name]["stage2"]
        bufs = _split_wait(name + "_ici_wait", bufs, send_sems, recv_sems, after, _to_owner_copies(n))
        for i, k in enumerate(keys):
            if k in big_w:
                wt, mt, vt, out_name = big_w[k]
                upd[out_name] = _adamw("adamw_" + out_name, wt[0], mt[0], vt[0], bufs[i], chip_idx, [bufs[n + i]])
            else:
                wt, mt, vt, out_name = big_w["w_in"]
                upd[out_name] = _adamw("adamw_" + k, wt[0], mt[0], vt[0], bufs[i], chip_idx, [bufs[n + i]],
                                       row0=0 if k == "w_in_a" else bufs[i].shape[1], into=upd.get(out_name))
            after = upd[out_name][1]
        return after

    after = land_and_update("scatter_mixer", land_and_update("scatter_ffn", t_tail))
    gathered = _split_wait("gather_small_wait", sg_bufs, sg_send, sg_recv, after, _small_gather_copies)[1]
    wp = dict(p, b_ada=b_ada)
    ms = dict(b_ada=m_b_ada, norm1_g=m_norm1_g, b_gate=m_b_gate, ln_g=m_gmlp_ln_g, ln_b=m_gmlp_ln_b, ws=m_gmlp_ws, bs=m_gmlp_bs,
              hg_lb=m_hg_lb, hg_ng=m_hg_norm_g, norm2_g=m_norm2_g, final_g=m_final_norm_g)
    vs = dict(b_ada=v_b_ada, norm1_g=v_norm1_g, b_gate=v_b_gate, ln_g=v_gmlp_ln_g, ln_b=v_gmlp_ln_b, ws=v_gmlp_ws, bs=v_gmlp_bs,
              hg_lb=v_hg_lb, hg_ng=v_hg_norm_g, norm2_g=v_norm2_g, final_g=v_final_norm_g)
    w_sm, _ = _pack([wp[k] for k in _SMALL])
    m_sm, _ = _pack([ms[k] for k in _SMALL])
    v_sm, _ = _pack([vs[k] for k in _SMALL])
    shapes = dict(b_ada=b_ada.shape, norm1_g=norm1_g.shape, b_gate=b_gate.shape, ln_g=gmlp_ln_g.shape, ln_b=gmlp_ln_b.shape,
                  ws=gmlp_ws.shape, bs=gmlp_bs.shape, hg_lb=hg_lb.shape, hg_ng=hg_norm_g.shape, norm2_g=norm2_g.shape,
                  final_g=final_norm_g.shape)
    sm_out = _small_update(gathered, w_sm, m_sm, v_sm, after, [math.prod(shapes[k]) // LANES for k in _SMALL])

    def unpack(idx, k):
        return sm_out[idx * len(_SMALL) + _SMALL.index(k)].reshape(shapes[k])

    loss = sm_out[-1][0, 0]

    dmod_all = gathered.reshape(N_DEV, -1)[:, offs[0]:offs[0] + N_DEV * ada_loc]
    dmod_loc = lax.dynamic_slice_in_dim(dmod_all, me * ada_loc, ada_loc, axis=1)
    ca_t = jnp.pad(c_act[:N_DEV].T, ((0, 0), (0, LANES - N_DEV))).astype(BF16)
    dm_p = jnp.pad(dmod_loc, ((0, LANES - N_DEV), (0, 0))).astype(BF16)
    tm_a = _tile(D, 512)
    g_ada = _matmul(
        "ada_dw", ca_t, dm_p, dims=_NN, grid_mnk=(D // tm_a, 1, 1), tiles=(tm_a, ada_loc),
        a_spec=pl.BlockSpec((tm_a, LANES), lambda i, j, k: (i, 0)), b_spec=pl.BlockSpec((LANES, ada_loc), lambda i, j, k: (0, 0)),
        out_shapes=[jax.ShapeDtypeStruct((1, D, ada_loc), F32)], out_specs=[pl.BlockSpec((None, tm_a, ada_loc), lambda i, j, k: (0, i, 0))],
        epilogue=_store(F32))[0]
    upd["w_ada"] = _adamw("adamw_w_ada", w_ada[0], m_w_ada[0], v_w_ada[0], g_ada, jnp.zeros((1,), jnp.int32))
    land_and_update("scatter_proj_in_b", land_and_update("scatter_proj_in_a", upd["w_ada"][1]))

    order = ("w_ada", "b_ada", "norm1_g", "w_in", "b_gate", "ln_g", "ln_b", "ws", "bs", "hg_lb", "hg_ng", "w_branch_gmlp", "w_branch_hg",
             "w_out", "norm2_g", "w_ffn_in", "w_ffn_out", "final_g")
    outs = [loss, grad_x[None]]
    for idx in range(4):
        for k in order:
            outs.append(upd[k][idx][None] if k in upd else unpack(idx, k))
    return tuple(outs)
```

```python
import functools
import math

import jax
import jax.numpy as jnp
from jax import lax
from jax.experimental import pallas as pl
from jax.experimental.pallas import tpu as pltpu

F32 = jnp.float32
BF16 = jnp.bfloat16
N_DEV = 8
EPS = 1e-6
LANES = 128
HG_DK = 128
HG_CHUNK = 64
HG_MID = HG_CHUNK // 2 - 1
EXP_CLAMP = 80.0
VMEM_LIMIT = 48 * 1024 * 1024
BF16_ROWS = 16
STREAM_TILE = 1 << 20
ADAMW_TILE = 3 << 17
ADAM_LR, ADAM_B1, ADAM_B2, ADAM_EPS, ADAM_WD, ADAM_STEP = 0.001, 0.9, 0.999, 1e-08, 0.01, 10
MESH = pl.DeviceIdType.MESH

_NN = (((1,), (0,)), ((), ()))
_NT = (((1,), (1,)), ((), ()))
_TN = (((0,), (0,)), ((), ()))


def _dot(a, b, dims=_NN):
    return lax.dot_general(a.astype(BF16), b.astype(BF16), dims, preferred_element_type=F32)


def _tile(n, target, mult=LANES):
    best = None
    for t in range(mult, min(n, target) + 1, mult):
        if n % t == 0:
            best = t
    return n if best is None else best


def _cparams(sem):
    return pltpu.CompilerParams(dimension_semantics=sem, vmem_limit_bytes=VMEM_LIMIT)


def _sigmoid(x):
    return 1.0 / (1.0 + jnp.exp(-x))


def _gelu_parts(x):
    k0 = math.sqrt(2.0 / math.pi)
    x2 = x * x
    t = jnp.tanh(k0 * (x + 0.044715 * x * x2))
    g = 0.5 * x * (1.0 + t)
    dg = 0.5 * (1.0 + t) + 0.5 * x * (1.0 - t * t) * (k0 * (1.0 + 3.0 * 0.044715 * x2))
    return g, dg


def _split3(x):
    h = x.astype(BF16)
    r = x - h.astype(F32)
    m = r.astype(BF16)
    lo = (r - m.astype(F32)).astype(BF16)
    return h, m, lo


def _ones_dot(mat01, x):
    h, m, lo = _split3(x)
    d = functools.partial(lax.dot_general, dimension_numbers=_NN, preferred_element_type=F32)
    return d(mat01, h) + d(mat01, m) + d(mat01, lo)


def _matmul(name, a, b, *, dims, grid_mnk, tiles, a_spec, b_spec, extras=(), extra_specs=(), out_shapes, out_specs, epilogue, after=None,
            sem=None):
    gm, gn, nk = grid_mnk
    tm, tn = tiles
    n_ex, n_out = len(extras), len(out_shapes)
    held = [] if after is None else [after]

    def body(*refs):
        a_ref, b_ref = refs[0], refs[1]
        ex = refs[2:2 + n_ex]
        outs = refs[2 + n_ex + len(held):2 + n_ex + len(held) + n_out]
        more = () if sem is None else (pl.program_id(0) == 0,)
        if nk == 1:
            epilogue(lax.dot_general(a_ref[...], b_ref[...], dims, preferred_element_type=F32), ex, outs, *more)
            return
        acc = refs[-1]
        k = pl.program_id(2)

        @pl.when(k == 0)
        def _():
            acc[...] = jnp.zeros_like(acc)

        acc[...] += lax.dot_general(a_ref[...], b_ref[...], dims, preferred_element_type=F32)

        @pl.when(k == nk - 1)
        def _():
            epilogue(acc[...], ex, outs, *more)

    return pl.pallas_call(
        body, name=name, grid=(gm, gn, nk), in_specs=[a_spec, b_spec, *extra_specs] + [pl.BlockSpec(memory_space=pl.ANY)] * len(held),
        out_specs=list(out_specs), out_shape=list(out_shapes), scratch_shapes=[] if nk == 1 else [pltpu.VMEM((tm, tn), F32)],
        compiler_params=_cparams(sem or ("parallel", "parallel", "arbitrary")),
    )(a, b, *extras, *held)


def _store(dtype):
    def ep(acc, ex, outs):
        outs[0][...] = acc.astype(dtype)
    return ep


def _mm_nn_stacked(name, a, wg, *, tm, tn, tk, out_dtype=F32, extras=(), extra_specs=(), out_shapes=None, out_specs=None, epilogue=None,
                   after=None):
    M, K = a.shape
    _, _, nloc = wg.shape
    N = nloc * N_DEV
    q = nloc // tn
    if out_shapes is None:
        out_shapes = [jax.ShapeDtypeStruct((M, N), out_dtype)]
        out_specs = [pl.BlockSpec((tm, tn), lambda i, j, k: (i, j))]
        epilogue = _store(out_dtype)
    return _matmul(
        name, a, wg, dims=_NN, grid_mnk=(M // tm, N // tn, K // tk), tiles=(tm, tn),
        a_spec=pl.BlockSpec((tm, tk), lambda i, j, k: (i, k)),
        b_spec=pl.BlockSpec((None, tk, tn), lambda i, j, k: (j // q, k, j % q)),
        extras=extras, extra_specs=extra_specs, out_shapes=out_shapes, out_specs=out_specs, epilogue=epilogue, after=after)


def _mm_nt_stacked(name, a_spec, a, wg, *, M, tm, tn, tk, out_dtype=F32, after=None, extras=(), extra_specs=(), out_shapes=None,
                   out_specs=None, epilogue=None, sem=None):
    _, Kw, nloc = wg.shape
    q = nloc // tk
    single = out_shapes is None
    if single:
        out_shapes = [jax.ShapeDtypeStruct((M, Kw), out_dtype)]
        out_specs = [pl.BlockSpec((tm, tn), lambda i, j, k: (i, j))]
        epilogue = _store(out_dtype)
    res = _matmul(
        name, a, wg, dims=_NT, grid_mnk=(M // tm, Kw // tn, (nloc * N_DEV) // tk), tiles=(tm, tn),
        a_spec=a_spec, b_spec=pl.BlockSpec((None, tn, tk), lambda i, j, k: (k // q, j, k % q)),
        extras=extras, extra_specs=extra_specs, out_shapes=out_shapes, out_specs=out_specs, epilogue=epilogue, after=after, sem=sem)
    return res[0] if single else res


def _mm_tn(name, a, b, b_spec, *, Mo, No, S, tm, tn, tk, stacked_nloc=None, after=None, a_off=0):
    if stacked_nloc is None:
        out_shape = jax.ShapeDtypeStruct((Mo, No), BF16)
        out_spec = pl.BlockSpec((tm, tn), lambda i, j, k: (i, j))
    else:
        q = stacked_nloc // tn
        out_shape = jax.ShapeDtypeStruct((N_DEV, Mo, stacked_nloc), BF16)
        out_spec = pl.BlockSpec((None, tm, tn), lambda i, j, k: (j // q, i, j % q))
    return _matmul(
        name, a, b, dims=_TN, grid_mnk=(Mo // tm, No // tn, S // tk), tiles=(tm, tn),
        a_spec=pl.BlockSpec((tk, tm), lambda i, j, k: (k, i + a_off)), b_spec=b_spec,
        out_shapes=[out_shape], out_specs=[out_spec], epilogue=_store(BF16), after=after)[0]


def _norm_mod(name, x, g, sc, sh):
    S, D = x.shape
    tm = _tile(S, 256, 8)

    def body(x_ref, g_ref, sc_ref, sh_ref, h_ref):
        xv = x_ref[...]
        r = lax.rsqrt(jnp.mean(xv * xv, axis=-1, keepdims=True) + EPS)
        h = (xv * r) * g_ref[...]
        h_ref[...] = (h * (1.0 + sc_ref[...]) + sh_ref[...]).astype(BF16)

    row = pl.BlockSpec((tm, D), lambda i: (i, 0))
    vec = pl.BlockSpec((1, D), lambda i: (0, 0))
    return pl.pallas_call(body, name=name, grid=(S // tm,), in_specs=[row, vec, vec, vec], out_specs=row,
                          out_shape=jax.ShapeDtypeStruct((S, D), BF16), compiler_params=_cparams(("parallel",)))(x, g, sc, sh)


def _norm_mod_bwd_rows(first, dh_v, x_ref, g_ref, sc_ref, dres_ref, dx_ref, vec_ref, o_ref=None, gt_ref=None, do_ref=None):
    @pl.when(first)
    def _():
        vec_ref[...] = jnp.zeros_like(vec_ref)

    xv, gv = x_ref[...], g_ref[...]
    r = lax.rsqrt(jnp.mean(xv * xv, axis=-1, keepdims=True) + EPS)
    xn = xv * r
    one_sc = 1.0 + sc_ref[...]
    vec_ref[0:1, :] += jnp.sum(dh_v, axis=0, keepdims=True)
    vec_ref[1:2, :] += jnp.sum(dh_v * (xn * gv), axis=0, keepdims=True)
    vec_ref[2:3, :] += jnp.sum(dh_v * one_sc * xn, axis=0, keepdims=True)
    dxn = dh_v * one_sc * gv
    dx = dres_ref[...] + r * (dxn - xn * jnp.mean(dxn * xn, axis=-1, keepdims=True))
    dx_ref[...] = dx
    if o_ref is not None:
        vec_ref[3:4, :] += jnp.sum(dx * o_ref[...], axis=0, keepdims=True)
        do_ref[...] = (dx * gt_ref[...]).astype(BF16)


def _norm_mod_bwd(name, dh, x, g, sc, dres, o=None, gt=None):
    S, D = x.shape
    tm = _tile(S, 256, 8)
    gated = o is not None

    def body(*refs):
        if gated:
            dh_ref, x_ref, g_ref, sc_ref, dres_ref, o_ref, gt_ref, dx_ref, vec_ref, do_ref = refs
        else:
            dh_ref, x_ref, g_ref, sc_ref, dres_ref, dx_ref, vec_ref = refs
            o_ref = gt_ref = do_ref = None
        _norm_mod_bwd_rows(pl.program_id(0) == 0, dh_ref[...], x_ref, g_ref, sc_ref, dres_ref, dx_ref, vec_ref, o_ref, gt_ref, do_ref)

    row = pl.BlockSpec((tm, D), lambda i: (i, 0))
    vec = pl.BlockSpec((1, D), lambda i: (0, 0))
    acc = pl.BlockSpec((8, D), lambda i: (0, 0))
    ins = [dh, x, g, sc, dres] + ([o, gt] if gated else [])
    in_specs = [row, row, vec, vec, row] + ([row, vec] if gated else [])
    out_shape = [jax.ShapeDtypeStruct((S, D), F32), jax.ShapeDtypeStruct((8, D), F32)]
    out_specs = [row, acc]
    if gated:
        out_shape.append(jax.ShapeDtypeStruct((S, D), BF16))
        out_specs.append(row)
    return pl.pallas_call(body, name=name, grid=(S // tm,), in_specs=in_specs, out_specs=out_specs, out_shape=out_shape,
                          compiler_params=_cparams(("arbitrary",)))(*ins)


def _loss_head(x3, tgt, gf, o2, gt2):
    S, D = x3.shape
    tm = _tile(S, 256, 8)

    def body(x_ref, t_ref, g_ref, o_ref, gt_ref, dx_ref, do_ref, vec_ref):
        i = pl.program_id(0)

        @pl.when(i == 0)
        def _():
            vec_ref[...] = jnp.zeros_like(vec_ref)

        xv, gv = x_ref[...], g_ref[...]
        r = lax.rsqrt(jnp.mean(xv * xv, axis=-1, keepdims=True) + EPS)
        xn = xv * r
        e = xn * gv - t_ref[...]
        tok = 0.5 * jnp.mean(e * e, axis=-1, keepdims=True)
        vec_ref[0:1, :] += jnp.broadcast_to(jnp.sum(tok, axis=0, keepdims=True), (1, D))
        dy = e * (1.0 / D)
        vec_ref[1:2, :] += jnp.sum(dy * xn, axis=0, keepdims=True)
        dxn = dy * gv
        dx = r * (dxn - xn * jnp.mean(dxn * xn, axis=-1, keepdims=True))
        dx_ref[...] = dx
        vec_ref[2:3, :] += jnp.sum(dx * o_ref[...], axis=0, keepdims=True)
        do_ref[...] = (dx * gt_ref[...]).astype(BF16)

    row = pl.BlockSpec((tm, D), lambda i: (i, 0))
    vec = pl.BlockSpec((1, D), lambda i: (0, 0))
    return pl.pallas_call(
        body, name="loss_head", grid=(S // tm,), in_specs=[row, row, vec, row, vec],
        out_specs=[row, row, pl.BlockSpec((8, D), lambda i: (0, 0))],
        out_shape=[jax.ShapeDtypeStruct((S, D), F32), jax.ShapeDtypeStruct((S, D), BF16), jax.ShapeDtypeStruct((8, D), F32)],
        compiler_params=_cparams(("arbitrary",)))(x3, tgt, gf, o2, gt2)


def _ffn_in_swiglu(h, wg):
    S, D = h.shape
    _, _, tf = wg.shape
    nf = N_DEV // 2
    F = nf * tf
    tm = _tile(S, 256, 16)

    def body(h_ref, wa_ref, wu_ref, hf_ref, fac_ref):
        hv = h_ref[...]
        a = lax.dot_general(hv, wa_ref[...], _NN, preferred_element_type=F32)
        up = lax.dot_general(hv, wu_ref[...], _NN, preferred_element_type=F32)
        sa = _sigmoid(a)
        silu = a * sa
        hf_ref[...] = (silu * up).astype(BF16)
        fac_ref[0] = (up * (sa * (1.0 + a * (1.0 - sa)))).astype(BF16)
        fac_ref[1] = silu.astype(BF16)

    return pl.pallas_call(
        body, name="ffn_in_swiglu", grid=(nf, S // tm),
        in_specs=[pl.BlockSpec((tm, D), lambda j, i: (i, 0)), pl.BlockSpec((None, D, tf), lambda j, i: (j, 0, 0)),
                  pl.BlockSpec((None, D, tf), lambda j, i: (j + nf, 0, 0))],
        out_specs=[pl.BlockSpec((tm, tf), lambda j, i: (i, j)), pl.BlockSpec((2, tm, tf), lambda j, i: (0, i, j))],
        out_shape=[jax.ShapeDtypeStruct((S, F), BF16), jax.ShapeDtypeStruct((2, S, F), BF16)],
        compiler_params=_cparams(("parallel", "parallel")))(h, wg, wg)


def _gmlp_common(u_ref, v_ref, lg_ref, lb_ref, ws_ref, bsb_ref, G, T, Dg):
    ug, dug = _gelu_parts(u_ref[...])
    vg, dvg = _gelu_parts(v_ref[...])
    mu = jnp.mean(vg, axis=-1, keepdims=True)
    vc = vg - mu
    rstd = lax.rsqrt(jnp.mean(vc * vc, axis=-1, keepdims=True) + EPS)
    vhat = vc * rstd
    vn = vhat * lg_ref[...] + lb_ref[...]
    row = lax.broadcasted_iota(jnp.int32, (T, T), 0)
    col = lax.broadcasted_iota(jnp.int32, (T, T), 1)
    tril = row >= col
    s = []
    for g in range(G):
        w = jnp.where(tril, ws_ref[g], 0.0)
        s.append(_dot(w, vn[:, g * Dg:(g + 1) * Dg]) + bsb_ref[g])
    return ug, dug, dvg, rstd, vhat, vn, tril, s


def _gmlp_fwd(z, ln_g, ln_b, ws, bsb, GW):
    S = z.shape[0]
    G, T, _ = ws.shape
    Dg = GW // G

    def body(u_ref, v_ref, lg_ref, lb_ref, ws_ref, bsb_ref, ya_ref):
        ug, _, _, _, _, _, _, s = _gmlp_common(u_ref, v_ref, lg_ref, lb_ref, ws_ref, bsb_ref, G, T, Dg)
        for g in range(G):
            sl = slice(g * Dg, (g + 1) * Dg)
            ya_ref[:, sl] = (ug[:, sl] * s[g]).astype(BF16)

    vec = pl.BlockSpec((1, GW), lambda c: (0, 0))
    return pl.pallas_call(
        body, name="gmlp_fwd", grid=(S // T,),
        in_specs=[pl.BlockSpec((T, GW), lambda c: (c, 0)), pl.BlockSpec((T, GW), lambda c: (c, 1)), vec, vec,
                  pl.BlockSpec((G, T, T), lambda c: (0, 0, 0)), pl.BlockSpec((G, T, Dg), lambda c: (0, 0, 0))],
        out_specs=pl.BlockSpec((T, GW), lambda c: (c, 0)), out_shape=jax.ShapeDtypeStruct((S, GW), BF16),
        compiler_params=_cparams(("parallel",)))(z, z, ln_g, ln_b, ws, bsb)


def _gmlp_bwd(z, dya, ln_g, ln_b, ws, bsb, GW):
    S = z.shape[0]
    G, T, _ = ws.shape
    Dg = GW // G
    nc = S // T

    def body(u_ref, v_ref, dya_ref, lg_ref, lb_ref, ws_ref, bsb_ref, dz_ref, dln_ref, dws_ref, dbs_ref, dbs_acc, dvh):
        c = pl.program_id(0)

        @pl.when(c == 0)
        def _():
            dln_ref[...] = jnp.zeros_like(dln_ref)
            dws_ref[...] = jnp.zeros_like(dws_ref)
            dbs_acc[...] = jnp.zeros_like(dbs_acc)

        ug, dug, dvg, rstd, vhat, vn, tril, s = _gmlp_common(u_ref, v_ref, lg_ref, lb_ref, ws_ref, bsb_ref, G, T, Dg)
        dya_v = dya_ref[...]
        for g in range(G):
            sl = slice(g * Dg, (g + 1) * Dg)
            dy_g = dya_v[:, sl]
            dz_ref[:, sl] = (dy_g * s[g] * dug[:, sl]).astype(BF16)
            ds = dy_g * ug[:, sl]
            dbs_acc[g] += ds
            w = jnp.where(tril, ws_ref[g], 0.0)
            dvn_g = _dot(w, ds, _TN)
            dws_ref[g] += jnp.where(tril, _dot(ds, vn[:, sl], _NT), 0.0)
            dln_ref[0:1, sl] += jnp.sum(dvn_g * vhat[:, sl], axis=0, keepdims=True)
            dln_ref[1:2, sl] += jnp.sum(dvn_g, axis=0, keepdims=True)
            dvh[:, sl] = dvn_g * lg_ref[:, sl]
        dvhat = dvh[...]
        m1 = jnp.mean(dvhat, axis=-1, keepdims=True)
        m2 = jnp.mean(dvhat * vhat, axis=-1, keepdims=True)
        dz_ref[:, GW:2 * GW] = (rstd * (dvhat - m1 - vhat * m2) * dvg).astype(BF16)

        @pl.when(c == nc - 1)
        def _():
            for g in range(G):
                dbs_ref[g] = jnp.sum(dbs_acc[g], axis=-1, keepdims=True)

    vec = pl.BlockSpec((1, GW), lambda c: (0, 0))
    return pl.pallas_call(
        body, name="gmlp_bwd", grid=(nc,),
        in_specs=[pl.BlockSpec((T, GW), lambda c: (c, 0)), pl.BlockSpec((T, GW), lambda c: (c, 1)),
                  pl.BlockSpec((T, GW), lambda c: (c, 0)), vec, vec,
                  pl.BlockSpec((G, T, T), lambda c: (0, 0, 0)), pl.BlockSpec((G, T, Dg), lambda c: (0, 0, 0))],
        out_specs=[pl.BlockSpec((T, 2 * GW), lambda c: (c, 0)), pl.BlockSpec((8, GW), lambda c: (0, 0)),
                   pl.BlockSpec((G, T, T), lambda c: (0, 0, 0)), pl.BlockSpec((G, T, 1), lambda c: (0, 0, 0))],
        out_shape=[jax.ShapeDtypeStruct((S, 2 * GW), BF16), jax.ShapeDtypeStruct((8, GW), F32),
                   jax.ShapeDtypeStruct((G, T, T), F32), jax.ShapeDtypeStruct((G, T, 1), F32)],
        scratch_shapes=[pltpu.VMEM((G, T, Dg), F32), pltpu.VMEM((T, GW), F32)],
        compiler_params=_cparams(("arbitrary",)))(z, z, dya, ln_g, ln_b, ws, bsb)


def _hg_common(q_ref, f_ref, hlb_ref):
    C = HG_CHUNK
    a = hlb_ref[...]
    lb = _sigmoid(a[0:1, :] - a[1:2, :])
    sig = _sigmoid(f_ref[...])
    f = lb + (1.0 - lb) * sig
    lf = jnp.log(f)
    kk = 1.0 - f
    q = q_ref[...]
    sq = _sigmoid(q)
    qa = q * sq
    row = lax.broadcasted_iota(jnp.int32, (C, C), 0)
    col = lax.broadcasted_iota(jnp.int32, (C, C), 1)
    tril = row >= col
    b = _ones_dot(tril.astype(BF16), lf)
    bm = b[HG_MID:HG_MID + 1, :]
    bl = b[C - 1:C, :]
    e_b = jnp.exp(b)
    e_qm = jnp.exp(jnp.minimum(b - bm, EXP_CLAMP))
    e_km = jnp.exp(jnp.minimum(bm - b, EXP_CLAMP))
    e_kl = jnp.exp(bl - b)
    return dict(lb=lb, sig=sig, f=f, kk=kk, q=q, sq=sq, qa=qa, tril=tril, e_b=e_b, e_qm=e_qm, e_km=e_km, e_kl=e_kl,
                e_l=jnp.exp(bl), qh=qa * e_b, qt=qa * e_qm, kt=kk * e_km, kh=kk * e_kl)


def _hg_fwd(z, hg_lb, ng, HW):
    S = z.shape[0]
    C, H, dk = HG_CHUNK, HW // HG_DK, HG_DK
    nc = S // C

    def body(q_ref, f_ref, i_ref, og_ref, hlb_ref, ng_ref, yb_ref, o_ref, st_ref, state):
        @pl.when(pl.program_id(0) == 0)
        def _():
            state[...] = jnp.zeros_like(state)

        t = _hg_common(q_ref, f_ref, hlb_ref)
        iv = i_ref[...]
        for h in range(H):
            sl = slice(h * dk, (h + 1) * dk)
            st = state[h]
            st_ref[h] = st
            a = jnp.where(t["tril"], _dot(t["qt"][:, sl], t["kt"][:, sl], _NT), 0.0)
            o_h = _dot(a, iv[:, sl]) + _dot(t["qh"][:, sl], st, _NT)
            state[h] = st * t["e_l"][:, sl] + _dot(iv[:, sl], t["kh"][:, sl], _TN)
            o_ref[:, sl] = o_h
            rr = lax.rsqrt(jnp.mean(o_h * o_h, axis=-1, keepdims=True) + EPS)
            og = og_ref[:, sl]
            yb_ref[:, sl] = (o_h * rr * ng_ref[:, sl] * (og * _sigmoid(og))).astype(BF16)

    def col(k):
        return pl.BlockSpec((C, HW), lambda c: (c, k))

    base = 2
    return pl.pallas_call(
        body, name="hgrn_fwd", grid=(nc,),
        in_specs=[col(base), col(base + 1), col(base + 2), col(base + 3),
                  pl.BlockSpec((2, HW), lambda c: (0, 0)), pl.BlockSpec((1, HW), lambda c: (0, 0))],
        out_specs=[pl.BlockSpec((C, HW), lambda c: (c, 0)), pl.BlockSpec((C, HW), lambda c: (c, 0)),
                   pl.BlockSpec((None, H, dk, dk), lambda c: (c, 0, 0, 0))],
        out_shape=[jax.ShapeDtypeStruct((S, HW), BF16), jax.ShapeDtypeStruct((S, HW), F32),
                   jax.ShapeDtypeStruct((nc, H, dk, dk), F32)],
        scratch_shapes=[pltpu.VMEM((H, dk, dk), F32)],
        compiler_params=_cparams(("arbitrary",)))(z, z, z, z, hg_lb, ng)


def _hg_bwd(z, o, states, dyb, hg_lb, ng, HW, dz_head, dz_tail):
    S = z.shape[0]
    C, H, dk = HG_CHUNK, HW // HG_DK, HG_DK
    nc = S // C
    B0 = dz_head.shape[1]
    DT = dz_tail.shape[2]
    INW = B0 + 4 * HW + 2 * DT

    def body(q_ref, f_ref, i_ref, og_ref, o_ref, st_ref, stn_ref, dyb_ref, hlb_ref, ng_ref, head_ref, tail_ref,
             dzf_ref, dng_ref, dhlb_ref, dtail_ref, dstate, cross, dqa_buf, dkk_buf, db_buf, dlb_acc):
        c = pl.program_id(0)
        dzf_ref[:, 0:B0] = head_ref[...]
        dzf_ref[:, B0 + 4 * HW:B0 + 4 * HW + DT] = tail_ref[0]
        dzf_ref[:, B0 + 4 * HW + DT:INW] = tail_ref[1]
        dz_ref = dzf_ref.at[:, B0:B0 + 4 * HW]

        @pl.when(c == 0)
        def _():
            dtail_ref[...] = jnp.zeros_like(dtail_ref)

        dtail_ref[0:1, :] += jnp.sum(tail_ref[0].astype(F32), axis=0, keepdims=True)
        dtail_ref[1:2, :] += jnp.sum(tail_ref[1].astype(F32), axis=0, keepdims=True)

        @pl.when(c == 0)
        def _():
            dstate[...] = jnp.zeros_like(dstate)
            dlb_acc[...] = jnp.zeros_like(dlb_acc)
            dng_ref[...] = jnp.zeros_like(dng_ref)

        def r16(v):
            return v.astype(BF16).astype(F32)

        t = _hg_common(q_ref, f_ref, hlb_ref)
        iv = i_ref[...]
        for h in range(H):
            sl = slice(h * dk, (h + 1) * dk)
            o_h, og, dyb_h, ng_h = o_ref[:, sl], og_ref[:, sl], dyb_ref[:, sl], ng_ref[:, sl]
            sg = _sigmoid(og)
            silu_og = og * sg
            rr = lax.rsqrt(jnp.mean(o_h * o_h, axis=-1, keepdims=True) + EPS)
            on = o_h * rr
            dng_ref[0:1, sl] += jnp.sum(dyb_h * on * silu_og, axis=0, keepdims=True)
            dz_ref[:, 3 * HW + h * dk:3 * HW + (h + 1) * dk] = (dyb_h * on * ng_h * (sg * (1.0 + og * (1.0 - sg)))).astype(BF16)
            don = dyb_h * ng_h * silu_og
            do_h = rr * (don - on * jnp.mean(don * on, axis=-1, keepdims=True))

            qt, kt, qh, kh, iv_h = t["qt"][:, sl], t["kt"][:, sl], t["qh"][:, sl], t["kh"][:, sl], iv[:, sl]
            a = jnp.where(t["tril"], _dot(qt, kt, _NT), 0.0)
            da = jnp.where(t["tril"], _dot(do_h, iv_h, _NT), 0.0)
            st, dst = st_ref[h], dstate[h]
            cross[:, sl] = jnp.sum(dst * stn_ref[h], axis=0, keepdims=True)
            dqh = _dot(do_h, st)
            dstate[h] = _dot(do_h, qh, _TN) + dst * t["e_l"][:, sl]
            div = _dot(a, do_h, _TN) + _dot(kh, dst, _NT)
            dkh = _dot(iv_h, dst)
            dqt = _dot(da, kt)
            dkt = _dot(da, qt, _TN)
            dz_ref[:, 2 * HW + h * dk:2 * HW + (h + 1) * dk] = div.astype(BF16)
            dqa_buf[:, sl] = dqh * t["e_b"][:, sl] + dqt * t["e_qm"][:, sl]
            dkk_buf[:, sl] = dkt * t["e_km"][:, sl] + dkh * t["e_kl"][:, sl]
            db_buf[:, sl] = r16(qt) * dqt - r16(kt) * dkt + r16(qh) * dqh - r16(kh) * dkh

        dqa, dkk = dqa_buf[...], dkk_buf[...]
        triu = jnp.logical_not(t["tril"]) | (lax.broadcasted_iota(jnp.int32, (C, C), 0) == lax.broadcasted_iota(jnp.int32, (C, C), 1))
        dlf = _ones_dot(triu.astype(BF16), db_buf[...]) + cross[...]
        df = dlf / t["f"] - dkk
        sig, lb = t["sig"], t["lb"]
        dz_ref[:, HW:2 * HW] = (df * (1.0 - lb) * sig * (1.0 - sig)).astype(BF16)
        dlb_acc[...] += jnp.sum(df * (1.0 - sig), axis=0, keepdims=True)
        q, sq = t["q"], t["sq"]
        dz_ref[:, 0:HW] = (dqa * (sq * (1.0 + q * (1.0 - sq)))).astype(BF16)

        @pl.when(c == nc - 1)
        def _():
            da0 = dlb_acc[...] * lb * (1.0 - lb)
            dhlb_ref[0:1, :] = da0
            dhlb_ref[1:2, :] = -da0

    def col(k):
        return pl.BlockSpec((C, HW), lambda c: (nc - 1 - c, k))

    base = 2
    return pl.pallas_call(
        body, name="hgrn_bwd", grid=(nc,),
        in_specs=[col(base), col(base + 1), col(base + 2), col(base + 3), col(0),
                  pl.BlockSpec((None, H, dk, dk), lambda c: (nc - 1 - c, 0, 0, 0)),
                  pl.BlockSpec((None, H, dk, dk), lambda c: (jnp.minimum(nc - c, nc - 1), 0, 0, 0)), col(0),
                  pl.BlockSpec((2, HW), lambda c: (0, 0)), pl.BlockSpec((1, HW), lambda c: (0, 0)),
                  pl.BlockSpec((C, B0), lambda c: (nc - 1 - c, 0)), pl.BlockSpec((2, C, DT), lambda c: (0, nc - 1 - c, 0))],
        out_specs=[pl.BlockSpec((C, INW), lambda c: (nc - 1 - c, 0)), pl.BlockSpec((8, HW), lambda c: (0, 0)),
                   pl.BlockSpec((2, HW), lambda c: (0, 0)), pl.BlockSpec((2, DT), lambda c: (0, 0))],
        out_shape=[jax.ShapeDtypeStruct((S, INW), BF16), jax.ShapeDtypeStruct((8, HW), F32), jax.ShapeDtypeStruct((2, HW), F32),
                   jax.ShapeDtypeStruct((2, DT), F32)],
        scratch_shapes=[pltpu.VMEM((H, dk, dk), F32), pltpu.VMEM((1, HW), F32), pltpu.VMEM((C, HW), F32), pltpu.VMEM((C, HW), F32),
                        pltpu.VMEM((C, HW), F32), pltpu.VMEM((1, HW), F32)],
        compiler_params=_cparams(("arbitrary",)))(z, z, z, z, o, states, states, dyb, hg_lb, ng, dz_head, dz_tail)


def _position():
    x, y, c = lax.axis_index("x"), lax.axis_index("y"), lax.axis_index("c")
    return x, y, c, 4 * x + 2 * y + c


def _flip(x, y, c, k):
    return (1 - x if k & 4 else x, 1 - y if k & 2 else y, 1 - c if k & 1 else c)


def _allgather_small(name, v):
    R, L = v.shape

    def body(v_ref, out_ref, send_sems, recv_sems):
        x, y, c, me = _position()
        out_ref[me] = v_ref[...]
        copies = []
        for k in range(1, N_DEV):
            cp = pltpu.make_async_remote_copy(src_ref=v_ref, dst_ref=out_ref.at[me], send_sem=send_sems.at[k - 1],
                                              recv_sem=recv_sems.at[k - 1], device_id=_flip(x, y, c, k), device_id_type=MESH)
            cp.start()
            copies.append(cp)
        for cp in copies:
            cp.wait()

    return pl.pallas_call(
        body, name=name, out_shape=jax.ShapeDtypeStruct((N_DEV, R, L), v.dtype),
        in_specs=[pl.BlockSpec(memory_space=pltpu.VMEM)], out_specs=pl.BlockSpec(memory_space=pltpu.VMEM),
        scratch_shapes=[pltpu.SemaphoreType.DMA((N_DEV - 1,)), pltpu.SemaphoreType.DMA((N_DEV - 1,))],
        compiler_params=pltpu.CompilerParams(vmem_limit_bytes=VMEM_LIMIT),
    )(v)


_HBM = pl.BlockSpec(memory_space=pltpu.HBM)
_SEM = pl.BlockSpec(memory_space=pltpu.SEMAPHORE)
_EFFECT = pltpu.SideEffectType.DATAFLOW_SIDE_EFFECTING


def _split_start(name, bufs, n_sems, copies_fn, after=None):
    nb = len(bufs)
    extra = [] if after is None else [after]
    k = nb + len(extra)

    def body(*refs):
        for cp in copies_fn(refs[:nb], refs[k], refs[k + 1]):
            cp.start()
        refs[-1][...] = jnp.zeros_like(refs[-1])

    sems = pltpu.SemaphoreType.DMA((n_sems,))
    res = pl.pallas_call(
        body, name=name,
        out_shape=(sems, sems, *[pltpu.HBM(a.shape, a.dtype) for a in bufs], jax.ShapeDtypeStruct((8, LANES), F32)),
        in_specs=[_HBM] * nb + [pl.BlockSpec(memory_space=pl.ANY)] * len(extra),
        out_specs=(_SEM, _SEM, *[_HBM] * nb, pl.BlockSpec(memory_space=pltpu.VMEM)),
        input_output_aliases={i: 2 + i for i in range(nb)},
        compiler_params=pltpu.CompilerParams(has_side_effects=_EFFECT),
    )(*[pltpu.with_memory_space_constraint(a, pltpu.HBM) for a in bufs], *extra)
    return res[0], res[1], list(res[2:2 + nb]), res[-1]


def _split_wait(name, bufs, send_sems, recv_sems, after, copies_fn):
    nb = len(bufs)

    def body(*refs):
        for cp in copies_fn(refs[:nb], refs[nb], refs[nb + 1]):
            cp.wait_send()
            cp.wait_recv()

    res = pl.pallas_call(
        body, name=name, out_shape=tuple(pltpu.HBM(a.shape, a.dtype) for a in bufs),
        in_specs=[_HBM] * nb + [_SEM, _SEM, pl.BlockSpec(memory_space=pl.ANY)], out_specs=tuple([_HBM] * nb),
        input_output_aliases={i: i for i in range(nb)},
        compiler_params=pltpu.CompilerParams(has_side_effects=_EFFECT),
    )(*bufs, send_sems, recv_sems, after)
    return list(res)


def _split_relay(name, bufs, send_sems, recv_sems, after, wait_fn, n_sems, start_fn):
    nb = len(bufs)

    def body(*refs):
        for cp in wait_fn(refs[:nb], refs[nb], refs[nb + 1]):
            cp.wait_send()
            cp.wait_recv()
        for cp in start_fn(refs[:nb], refs[nb + 3], refs[nb + 4]):
            cp.start()
        refs[-1][...] = jnp.zeros_like(refs[-1])

    sems = pltpu.SemaphoreType.DMA((n_sems,))
    res = pl.pallas_call(
        body, name=name, out_shape=(sems, sems, *[pltpu.HBM(a.shape, a.dtype) for a in bufs], jax.ShapeDtypeStruct((8, LANES), F32)),
        in_specs=[_HBM] * nb + [_SEM, _SEM, pl.BlockSpec(memory_space=pl.ANY)],
        out_specs=(_SEM, _SEM, *[_HBM] * nb, pl.BlockSpec(memory_space=pltpu.VMEM)),
        input_output_aliases={i: 2 + i for i in range(nb)},
        compiler_params=pltpu.CompilerParams(has_side_effects=_EFFECT),
    )(*bufs, send_sems, recv_sems, after)
    return res[0], res[1], list(res[2:2 + nb]), res[-1]


N_CHIP = 4


def _chip_flip(x, y, k):
    return (1 - x if k & 2 else x), (1 - y if k & 1 else y)


def _gather_first_copies(n):
    def copies(bufs, send_sems, recv_sems):
        x, y, c, me = _position()
        out = []
        for w in range(n):
            for k in range(N_CHIP):
                to = (x, y, 1 - c) if k == 0 else (*_chip_flip(x, y, k), c)
                out.append(pltpu.make_async_remote_copy(
                    src_ref=bufs[w], dst_ref=bufs[n + w].at[me], send_sem=send_sems.at[w * N_CHIP + k],
                    recv_sem=recv_sems.at[w * N_CHIP + k], device_id=to, device_id_type=MESH))
        return out
    return copies


def _gather_relay_copies(n):
    def copies(bufs, send_sems, recv_sems):
        x, y, c, _ = _position()
        out = []
        for w in range(n):
            for k in range(1, N_CHIP):
                px, py = _chip_flip(x, y, k)
                blk = bufs[n + w].at[4 * px + 2 * py + c]
                out.append(pltpu.make_async_remote_copy(
                    src_ref=blk, dst_ref=blk, send_sem=send_sems.at[w * (N_CHIP - 1) + k - 1],
                    recv_sem=recv_sems.at[w * (N_CHIP - 1) + k - 1], device_id=(x, y, 1 - c), device_id_type=MESH))
        return out
    return copies


def _small_gather_copies(bufs, send_sems, recv_sems):
    x, y, c, me = _position()
    return [pltpu.make_async_remote_copy(src_ref=bufs[0], dst_ref=bufs[1].at[me], send_sem=send_sems.at[k - 1], recv_sem=recv_sems.at[k - 1],
                                         device_id=_flip(x, y, c, k), device_id_type=MESH) for k in range(1, N_DEV)]


def _forward_first_copies(n):
    def copies(bufs, send_sems, recv_sems):
        x, y, c, me = _position()
        out = []
        for w in range(n):
            for k, to in enumerate([(x, y, 1 - c), (1 - x, y, c), (x, 1 - y, c)]):
                out.append(pltpu.make_async_remote_copy(
                    src_ref=bufs[w], dst_ref=bufs[n + w].at[me], send_sem=send_sems.at[w * 3 + k],
                    recv_sem=recv_sems.at[w * 3 + k], device_id=to, device_id_type=MESH))
        return out
    return copies


def _forward_second_copies(n):
    def copies(bufs, send_sems, recv_sems):
        x, y, c, _ = _position()
        out = []
        for w in range(n):
            half = bufs[n + w].shape[1] // 2
            for k, (src_chip, rows, to) in enumerate([((1 - x, y), pl.ds(0, half), (x, 1 - y, c)), ((x, 1 - y), pl.ds(half, half), (1 - x, y, c))]):
                blk = bufs[n + w].at[4 * src_chip[0] + 2 * src_chip[1] + c, rows]
                out.append(pltpu.make_async_remote_copy(src_ref=blk, dst_ref=blk, send_sem=send_sems.at[w * 4 + k],
                                                        recv_sem=recv_sems.at[w * 4 + k], device_id=to, device_id_type=MESH))
            for k, (px, py) in enumerate([(1 - x, y), (x, 1 - y)]):
                blk = bufs[n + w].at[4 * px + 2 * py + c]
                out.append(pltpu.make_async_remote_copy(src_ref=blk, dst_ref=blk, send_sem=send_sems.at[w * 4 + 2 + k],
                                                        recv_sem=recv_sems.at[w * 4 + 2 + k], device_id=(x, y, 1 - c), device_id_type=MESH))
        return out
    return copies


def _forward_third_copies(n):
    def copies(bufs, send_sems, recv_sems):
        x, y, c, _ = _position()
        out = []
        for w in range(n):
            blk = bufs[n + w].at[4 * (1 - x) + 2 * (1 - y) + c]
            out.append(pltpu.make_async_remote_copy(src_ref=blk, dst_ref=blk, send_sem=send_sems.at[w], recv_sem=recv_sems.at[w],
                                                    device_id=(x, y, 1 - c), device_id_type=MESH))
        return out
    return copies


def _to_sibling_copies(n):
    def copies(bufs, send_sems, recv_sems):
        x, y, c, _ = _position()
        out = []
        for w in range(n):
            for q in range(N_CHIP):
                out.append(pltpu.make_async_remote_copy(
                    src_ref=bufs[w].at[2 * q + 1 - c], dst_ref=bufs[n + w].at[q], send_sem=send_sems.at[w * N_CHIP + q],
                    recv_sem=recv_sems.at[w * N_CHIP + q], device_id=(x, y, 1 - c), device_id_type=MESH))
        return out
    return copies


def _to_owner_copies(n):
    def copies(bufs, send_sems, recv_sems):
        x, y, c, _ = _position()
        out = []
        for w in range(n):
            for k in range(1, N_CHIP):
                px, py = (1 - x if k & 2 else x), (1 - y if k & 1 else y)
                out.append(pltpu.make_async_remote_copy(
                    src_ref=bufs[w].at[2 * px + py], dst_ref=bufs[n + w].at[k - 1], send_sem=send_sems.at[w * (N_CHIP - 1) + k - 1],
                    recv_sem=recv_sems.at[w * (N_CHIP - 1) + k - 1], device_id=(px, py, c), device_id_type=MESH))
        return out
    return copies


def _chip_sum(name, stack, landed, c_idx):
    _, R, C = stack.shape
    tr = _tile(R, max(BF16_ROWS, STREAM_TILE // C), BF16_ROWS)

    def body(c_ref, a_ref, b_ref, o_ref):
        o_ref[...] = (a_ref[...].astype(F32) + b_ref[...].astype(F32)).astype(o_ref.dtype)

    return pl.pallas_call(
        body, name=name,
        grid_spec=pltpu.PrefetchScalarGridSpec(
            num_scalar_prefetch=1, grid=(N_CHIP, R // tr),
            in_specs=[pl.BlockSpec((None, tr, C), lambda q, i, c_ref: (2 * q + c_ref[0], i, 0)),
                      pl.BlockSpec((None, tr, C), lambda q, i, c_ref: (q, i, 0))],
            out_specs=pl.BlockSpec((None, tr, C), lambda q, i, c_ref: (q, i, 0))),
        out_shape=jax.ShapeDtypeStruct((N_CHIP, R, C), stack.dtype),
        compiler_params=_cparams(("parallel", "parallel")))(c_idx, stack, landed)


def _ada_mod(c16, w):
    _, D = c16.shape
    n = w.shape[1]
    tk = _tile(D, 512)
    nk = D // tk

    def body(c_ref, w_ref, o_ref, ca_ref):
        @pl.when(pl.program_id(0) == 0)
        def _():
            o_ref[...] = jnp.zeros_like(o_ref)

        cv = c_ref[...]
        ca = cv * _sigmoid(cv)
        ca_ref[...] = ca
        o_ref[...] += _dot(ca, w_ref[...])

    return pl.pallas_call(
        body, name="ada_mod", grid=(nk,),
        in_specs=[pl.BlockSpec((16, tk), lambda k: (0, k)), pl.BlockSpec((tk, n), lambda k: (k, 0))],
        out_specs=[pl.BlockSpec((16, n), lambda k: (0, 0)), pl.BlockSpec((16, tk), lambda k: (0, k))],
        out_shape=[jax.ShapeDtypeStruct((16, n), F32), jax.ShapeDtypeStruct((16, D), F32)],
        compiler_params=_cparams(("arbitrary",)))(c16, w)


def _cast_shard(name, wf, slot):
    r, c = wf.shape
    tr = _tile(r, max(BF16_ROWS, STREAM_TILE // c), BF16_ROWS)

    def body(slot_ref, w_ref, s_ref, g_ref):
        v = w_ref[...].astype(BF16)
        s_ref[...] = v
        g_ref[...] = v

    return pl.pallas_call(
        body, name=name,
        grid_spec=pltpu.PrefetchScalarGridSpec(
            num_scalar_prefetch=1, grid=(r // tr,), in_specs=[pl.BlockSpec((tr, c), lambda i, s: (i, 0))],
            out_specs=[pl.BlockSpec((tr, c), lambda i, s: (i, 0)), pl.BlockSpec((None, tr, c), lambda i, s: (s[0], i, 0))]),
        out_shape=[jax.ShapeDtypeStruct((r, c), BF16), jax.ShapeDtypeStruct((N_DEV, r, c), BF16)],
        compiler_params=_cparams(("parallel",)))(slot, wf)


def _adam_math(w, g, m, v):
    m2 = ADAM_B1 * m + (1.0 - ADAM_B1) * g
    v2 = ADAM_B2 * v + (1.0 - ADAM_B2) * (g * g)
    m_hat = m2 / (1.0 - ADAM_B1 ** ADAM_STEP)
    v_hat = v2 / (1.0 - ADAM_B2 ** ADAM_STEP)
    delta = -ADAM_LR * (m_hat / (jnp.sqrt(v_hat) + ADAM_EPS) + ADAM_WD * w)
    return delta, m2, v2


def _adamw(name, w, m, v, own, own_slot, parts=(), row0=0, into=None):
    R, C = w.shape
    Rp = own.shape[1]
    tr = _tile(Rp, max(BF16_ROWS, ADAMW_TILE // C), BF16_ROWS)
    off = row0 // tr
    n_p = len(parts)
    held = [] if into is None else list(into)

    def body(slot_ref, *refs):
        w_ref, m_ref, v_ref, own_ref = refs[:4]
        g_ref, d_ref, m2_ref, v2_ref = refs[4 + n_p + len(held):]
        g = own_ref[...].astype(F32)
        for p_ref in refs[4:4 + n_p]:
            for s in range(p_ref.shape[0]):
                g = g + p_ref[s].astype(F32)
        delta, m2, v2 = _adam_math(w_ref[...], g, m_ref[...], v_ref[...])
        g_ref[...] = g
        d_ref[...] = delta
        m2_ref[...] = m2
        v2_ref[...] = v2

    blk = pl.BlockSpec((tr, C), lambda i, s: (i + off, 0))
    out = jax.ShapeDtypeStruct((R, C), F32)
    return pl.pallas_call(
        body, name=name,
        grid_spec=pltpu.PrefetchScalarGridSpec(
            num_scalar_prefetch=1, grid=(Rp // tr,),
            in_specs=[blk, blk, blk, pl.BlockSpec((None, tr, C), lambda i, s: (s[0], i, 0))]
            + [pl.BlockSpec((a.shape[0], tr, C), lambda i, s: (0, i, 0)) for a in parts]
            + [pl.BlockSpec(memory_space=pl.ANY)] * len(held),
            out_specs=[blk] * 4),
        out_shape=[out] * 4, input_output_aliases={5 + n_p + i: i for i in range(len(held))},
        compiler_params=_cparams(("parallel",)))(own_slot, w, m, v, own, *parts, *held)


def _small_update(gathered, w, m, v, after, rows):
    _, R, L = gathered.shape
    rs = w.shape[0]
    n = len(rows)
    assert all(r % 8 == 0 for r in rows) and sum(rows) <= rs and rs + 8 <= R

    def body(p_ref, w_ref, m_ref, v_ref, after_ref, *outs):
        g = p_ref[0]
        for p in range(1, N_DEV):
            g = g + p_ref[p]
        kinds = (g,) + _adam_math(w_ref[...], g[0:rs, :], m_ref[...], v_ref[...])
        at = 0
        for k, r in enumerate(rows):
            for idx, val in enumerate(kinds):
                outs[idx * n + k][...] = val[at:at + r, :]
            at += r
        outs[4 * n][...] = g[at:at + 8, :]

    vm = pl.BlockSpec(memory_space=pltpu.VMEM)
    shapes = [jax.ShapeDtypeStruct((r, L), F32) for _ in range(4) for r in rows] + [jax.ShapeDtypeStruct((8, L), F32)]
    return pl.pallas_call(body, name="small_update", in_specs=[vm] * 4 + [pl.BlockSpec(memory_space=pl.ANY)], out_specs=[vm] * len(shapes),
                          out_shape=shapes, compiler_params=pltpu.CompilerParams(vmem_limit_bytes=VMEM_LIMIT))(gathered, w, m, v, after)


class _Fetched(dict):
    def __init__(self, fetch):
        super().__init__()
        self.fetch = fetch

    def first(self, key, after):
        self[key] = self.fetch(key, after)
        return self[key]


def _local_step(x, tgt, mod, p, fetch, F, scatter=None):
    S, D = x.shape
    GW, HW = p["ln_g"].shape[1], p["hg_ng"].shape[1]
    G, T, _ = p["ws"].shape
    w = _Fetched(fetch)
    INW = 2 * GW + 4 * HW + 2 * D
    in_loc, br_loc, fi_loc = INW // N_DEV, D // N_DEV, 2 * F // N_DEV
    assert GW == HW and F % fi_loc == 0
    sh1, sc1, gt1, sh2, sc2, gt2 = (mod[:, k * D:(k + 1) * D] for k in range(6))
    bsb = jnp.broadcast_to(p["bs"][:, :, None], (G, T, GW // G))

    tm = _tile(S, 1024, 16)
    tmh = _tile(S, 512, 16)
    tn_in = _tile(in_loc, 1280)
    tn_d = _tile(D, 512)
    tn_br = _tile(br_loc, 512)
    tk_s = S
    tm_w = _tile(D, 1024)
    g_off = 2 * GW + 4 * HW

    h1 = _norm_mod("norm1", x, p["norm1_g"], sc1, sh1)
    z = _mm_nn_stacked("proj_in", h1, w.first("in", h1), tm=tm, tn=tn_in, tk=D)[0]
    ya = _gmlp_fwd(z, p["ln_g"], p["ln_b"], p["ws"], bsb, GW)
    yb, o_hg, states = _hg_fwd(z, p["hg_lb"], p["hg_ng"], HW)
    flat = {k: jnp.swapaxes(w.first(k, yb), 0, 1).reshape(GW, D) for k in ("bg", "bh")}
    tn_f = _tile(D, 1024)
    pa = _matmul(
        "branch_gmlp", ya, flat["bg"], dims=_NN, grid_mnk=(S // tm, D // tn_f, 1), tiles=(tm, tn_f),
        a_spec=pl.BlockSpec((tm, GW), lambda i, j, k: (i, 0)), b_spec=pl.BlockSpec((GW, tn_f), lambda i, j, k: (0, j)),
        out_shapes=[jax.ShapeDtypeStruct((S, D), F32)], out_specs=[pl.BlockSpec((tm, tn_f), lambda i, j, k: (i, j))], epilogue=_store(F32))[0]
    t_fi = w.first("fi_early", pa)

    def gates(ga_ref, gb_ref, ba_ref, bb_ref):
        return _sigmoid(ga_ref[...] + ba_ref[...]), _sigmoid(gb_ref[...] + bb_ref[...])

    def gate_specs(tn_, tm_=tm):
        o1, o2 = g_off // tn_, (g_off + D) // tn_
        return [pl.BlockSpec((tm_, tn_), lambda i, j, k: (i, o1 + j)), pl.BlockSpec((tm_, tn_), lambda i, j, k: (i, o2 + j)),
                pl.BlockSpec((1, tn_), lambda i, j, k: (0, j)), pl.BlockSpec((1, tn_), lambda i, j, k: (0, D // tn_ + j))]

    def merge_ep(acc, ex, outs):
        ga, gb = gates(*ex[1:5])
        outs[0][...] = acc.astype(BF16)
        outs[1][...] = (ga * ex[0][...] + gb * acc).astype(BF16)

    tile_o = pl.BlockSpec((tmh, tn_f), lambda i, j, k: (i, j))
    pb, y = _matmul(
        "branch_hg_merge", yb, flat["bh"], dims=_NN, grid_mnk=(S // tmh, D // tn_f, 1), tiles=(tmh, tn_f),
        a_spec=pl.BlockSpec((tmh, HW), lambda i, j, k: (i, 0)), b_spec=pl.BlockSpec((HW, tn_f), lambda i, j, k: (0, j)),
        extras=[pa, z, z, p["b_gate"], p["b_gate"]], extra_specs=[tile_o, *gate_specs(tn_f, tmh)],
        out_shapes=[jax.ShapeDtypeStruct((S, D), BF16), jax.ShapeDtypeStruct((S, D), BF16)], out_specs=[tile_o, tile_o],
        epilogue=merge_ep, after=t_fi)

    def resid_ep(acc, ex, outs):
        outs[0][...] = acc.astype(BF16)
        outs[1][...] = ex[0][...] + ex[1][...] * acc

    def resid_mm(name, a, b, res, gt, tm_):
        K = a.shape[1]
        t_o = pl.BlockSpec((tm_, tn_d), lambda i, j, k: (i, j))
        return _matmul(
            name, a, b, dims=_NN, grid_mnk=(S // tm_, D // tn_d, 1), tiles=(tm_, tn_d),
            a_spec=pl.BlockSpec((tm_, K), lambda i, j, k: (i, 0)), b_spec=pl.BlockSpec((K, tn_d), lambda i, j, k: (0, j)),
            extras=[res, gt], extra_specs=[t_o, pl.BlockSpec((1, tn_d), lambda i, j, k: (0, j))],
            out_shapes=[jax.ShapeDtypeStruct((S, D), BF16), jax.ShapeDtypeStruct((S, D), F32)], out_specs=[t_o, t_o], epilogue=resid_ep)

    o1, xm = resid_mm("proj_out", y, w.first("out", z), x, gt1, tm)
    h2 = _norm_mod("norm2", xm, p["norm2_g"], sc2, sh2)
    hf, hf_fac = _ffn_in_swiglu(h2, w.first("fi", h2))
    o2, x3 = resid_mm("ffn_out", hf, w.first("fo", hf), xm, gt2, tmh)
    dx3, do2, vec_l = _loss_head(x3, tgt, p["final_g"], o2, gt2)

    nf = F // fi_loc

    def dswiglu_ep(acc, ex, outs):
        outs[0][0] = (acc * ex[0][0].astype(F32)).astype(BF16)
        outs[0][1] = (acc * ex[0][1].astype(F32)).astype(BF16)

    pair = pl.BlockSpec((2, tmh, fi_loc), lambda i, j, k: (0, i, j))
    dab = _matmul(
        "ffn_out_dx", do2, w["fo"], dims=_NT, grid_mnk=(S // tmh, nf, 1), tiles=(tmh, fi_loc),
        a_spec=pl.BlockSpec((tmh, D), lambda i, j, k: (i, 0)), b_spec=pl.BlockSpec((fi_loc, D), lambda i, j, k: (j, 0)),
        extras=[hf_fac], extra_specs=[pair], out_shapes=[jax.ShapeDtypeStruct((2, S, F), BF16)], out_specs=[pair],
        epilogue=dswiglu_ep)[0]
    start = (lambda name, grads: scatter[0](name, grads)) if scatter is not None else (lambda name, grads: None)
    push = (lambda name, after: scatter[1](name, after)) if scatter is not None else (lambda name, after: None)

    def zero(token):
        return 0.0 if token is None else token[0:1, 0:1]

    tm_f = _tile(F, 512)
    g_fo = _mm_tn("ffn_out_dw", hf, do2, pl.BlockSpec((tk_s, D), lambda i, j, k: (k, j)), Mo=F, No=D, S=S, tm=tm_f, tn=D, tk=tk_s)
    g_fi = _mm_tn("ffn_in_dw", h2, dab, pl.BlockSpec((None, tk_s, fi_loc), lambda i, j, k: (j // nf, k, j % nf)),
                  Mo=D, No=2 * F, S=S, tm=tm_w, tn=fi_loc, tk=tk_s, stacked_nloc=fi_loc, after=g_fo)
    t_ffn = start("scatter_ffn", dict(fo=g_fo, fi=g_fi))
    dh2 = _mm_nt_stacked("ffn_in_dx", pl.BlockSpec((None, tm, fi_loc), lambda i, j, k: (k // nf, i, k % nf)), dab, w["fi"],
                         M=S, tm=tm, tn=tm_w, tk=fi_loc, after=t_ffn)
    dxm, vec2, do1 = _norm_mod_bwd("norm2_bwd", dh2, xm, p["norm2_g"], sc2, dx3, o1, gt1)
    t_ffn = push("scatter_ffn", dxm)

    def dmerge_ep(acc, ex, outs):
        ga, gb = gates(*ex[2:6])
        outs[0][...] = (acc * ga).astype(BF16)
        outs[1][...] = (acc * gb).astype(BF16)
        outs[2][0] = (acc * ex[0][...] * ga * (1.0 - ga)).astype(BF16)
        outs[2][1] = (acc * ex[1][...] * gb * (1.0 - gb)).astype(BF16)

    t_o = pl.BlockSpec((tm, tn_d), lambda i, j, k: (i, j))
    dpa, dpb, dg2 = _matmul(
        "proj_out_dx", do1, w["out"], dims=_NT, grid_mnk=(S // tm, D // tn_d, 1), tiles=(tm, tn_d),
        a_spec=pl.BlockSpec((tm, D), lambda i, j, k: (i, 0)), b_spec=pl.BlockSpec((tn_d, D), lambda i, j, k: (j, 0)),
        extras=[pa, pb, z, z, p["b_gate"], p["b_gate"]], extra_specs=[t_o, t_o, *gate_specs(tn_d)],
        out_shapes=[jax.ShapeDtypeStruct((S, D), BF16), jax.ShapeDtypeStruct((S, D), BF16), jax.ShapeDtypeStruct((2, S, D), BF16)],
        out_specs=[t_o, t_o, pl.BlockSpec((2, tm, tn_d), lambda i, j, k: (0, i, j))], epilogue=dmerge_ep, after=t_ffn)
    g_out = _mm_tn("proj_out_dw", y, do1, pl.BlockSpec((tk_s, D), lambda i, j, k: (k, j)), Mo=D, No=D, S=S, tm=tn_d, tn=D, tk=tk_s)
    tn_g = _tile(GW, 512)
    b_br = pl.BlockSpec((tk_s, br_loc), lambda i, j, k: (k, j))
    g_bg = _mm_tn("branch_gmlp_dw", ya, dpa, b_br, Mo=GW, No=D, S=S, tm=tn_g, tn=br_loc, tk=tk_s, stacked_nloc=br_loc)
    g_bh = _mm_tn("branch_hg_dw", yb, dpb, b_br, Mo=HW, No=D, S=S, tm=tn_g, tn=br_loc, tk=tk_s, stacked_nloc=br_loc)
    t_mix = start("scatter_mixer", dict(out=g_out, bg=g_bg, bh=g_bh))
    def branch_dx(name, dp, w_flat):
        return _matmul(
            name, dp, w_flat, dims=_NT, grid_mnk=(S // tm, GW // tn_g, 1), tiles=(tm, tn_g),
            a_spec=pl.BlockSpec((tm, D), lambda i, j, k: (i, 0)), b_spec=pl.BlockSpec((tn_g, D), lambda i, j, k: (j, 0)),
            out_shapes=[jax.ShapeDtypeStruct((S, GW), F32)], out_specs=[pl.BlockSpec((tm, tn_g), lambda i, j, k: (i, j))],
            epilogue=_store(F32), after=t_mix)[0]

    dya = branch_dx("branch_gmlp_dx", dpa, flat["bg"])
    dyb = branch_dx("branch_hg_dx", dpb, flat["bh"])
    dz_gmlp, dln, dws, dbs = _gmlp_bwd(z, dya, p["ln_g"], p["ln_b"], p["ws"], bsb, GW)
    t_mix = push("scatter_mixer", dz_gmlp)
    dz, dng, dhlb, db_gate = _hg_bwd(z, o_hg, states, dyb, p["hg_lb"], p["hg_ng"] + zero(t_mix), HW, dz_gmlp, dg2)
    half = D // 2
    tm_h = _tile(half, 1024)
    g_in = []
    t_in = None
    for hname, h in (("a", 0), ("b", 1)):
        g_in.append(_mm_tn("proj_in_dw_" + hname, h1, dz, pl.BlockSpec((tk_s, in_loc), lambda i, j, k: (k, j)), Mo=half, No=INW, S=S,
                           tm=tm_h, tn=in_loc, tk=tk_s, stacked_nloc=in_loc, after=t_in, a_off=h * (half // tm_h)))
        t_in = start("scatter_proj_in_" + hname, {"w_in_" + hname: g_in[-1]})
    t_in = push("scatter_proj_in_a", t_in)
    dh1 = _mm_nt_stacked("proj_in_dx", pl.BlockSpec((tm, in_loc), lambda i, j, k: (i, k)), dz, w["in"], M=S, tm=tm, tn=tm_w, tk=in_loc,
                         after=t_in)
    dx, vec1 = _norm_mod_bwd("norm1_bwd", dh1, x, p["norm1_g"], sc1, dxm)

    dmod = jnp.concatenate([vec1[0:1], vec1[1:2], vec2[3:4], vec2[0:1], vec2[1:2], vec_l[2:3]], axis=1)
    small = dict(norm1_g=vec1[2:3], b_gate=db_gate.reshape(1, 2 * D), ln_g=dln[0:1], ln_b=dln[1:2], ws=dws, bs=dbs.reshape(G, T),
                 hg_lb=dhlb, hg_ng=dng[0:1], norm2_g=vec2[2:3], final_g=vec_l[1:2], loss=vec_l[0:1, 0:LANES])
    big = dict(w_in_a=g_in[0], w_in_b=g_in[1], bg=g_bg, bh=g_bh, out=g_out, fi=g_fi, fo=g_fo)
    return dx, big, small, dmod


_SMALL = ("b_ada", "norm1_g", "b_gate", "ln_g", "ln_b", "ws", "bs", "hg_lb", "hg_ng", "norm2_g", "final_g")


def _pack(parts, rows_mult=8):
    flat = [a.reshape(-1) for a in parts]
    offs, n = [], 0
    for a in flat:
        offs.append(n)
        n += a.shape[0]
    pad = (-n) % (LANES * rows_mult)
    if pad:
        flat.append(jnp.zeros((pad,), F32))
    return jnp.concatenate(flat).reshape(-1, LANES), offs


def kernel(x, c, w_ada, b_ada, norm1_g, w_in, b_gate, gmlp_ln_g, gmlp_ln_b, gmlp_ws, gmlp_bs, hg_lb, hg_norm_g, w_branch_gmlp, w_branch_hg, w_out, norm2_g, w_ffn_in, w_ffn_out, final_norm_g, loss_target, m_w_ada, m_b_ada, m_norm1_g, m_w_in, m_b_gate, m_gmlp_ln_g, m_gmlp_ln_b, m_gmlp_ws, m_gmlp_bs, m_hg_lb, m_hg_norm_g, m_w_branch_gmlp, m_w_branch_hg, m_w_out, m_norm2_g, m_w_ffn_in, m_w_ffn_out, m_final_norm_g, v_w_ada, v_b_ada, v_norm1_g, v_w_in, v_b_gate, v_gmlp_ln_g, v_gmlp_ln_b, v_gmlp_ws, v_gmlp_bs, v_hg_lb, v_hg_norm_g, v_w_branch_gmlp, v_w_branch_hg, v_w_out, v_norm2_g, v_w_ffn_in, v_w_ffn_out, v_final_norm_g):
    S, D = x.shape[1], x.shape[2]
    ada_loc = w_ada.shape[2]
    me = 4 * lax.axis_index("x") + 2 * lax.axis_index("y") + lax.axis_index("c")
    me_idx = me.astype(jnp.int32).reshape(1)

    def empty_hbm(shape, dtype):
        return pltpu.with_memory_space_constraint(lax.empty(shape, dtype), pltpu.HBM)

    groups = dict(gather_in=dict(keys=["in"], src=[w_in], forward=True),
                  gather_mixer=dict(keys=["bg", "bh", "out"], src=[w_branch_gmlp, w_branch_hg, w_out], forward=False),
                  gather_ffn_in=dict(keys=["fi"], src=[w_ffn_in], forward=True),
                  gather_ffn_out=dict(keys=["fo"], src=[w_ffn_out], forward=False))
    group_of = {k: gname for gname, g in groups.items() for k in g["keys"]}

    def first_hop(gname, after):
        g = groups[gname]
        n = len(g["keys"])
        cast = [_cast_shard(f"{gname}_cast_{k}", a[0], me_idx) for k, a in zip(g["keys"], g["src"])]
        shards, outs = [s for s, _ in cast], [o for _, o in cast]
        if g["forward"]:
            *g["hop"], token = _split_start(gname + "_hop1", shards + outs, n * 3, _forward_first_copies(n), after=after)
        else:
            *g["hop"], token = _split_start(gname + "_hop1", shards + outs, n * N_CHIP, _gather_first_copies(n), after=after)
        return token

    def second_hop(gname, after):
        g = groups[gname]
        n = len(g["keys"])
        *g["hop"], token = _split_relay(gname + "_hop2", g["hop"][2], g["hop"][0], g["hop"][1], after,
                                        _forward_first_copies(n), n * 4, _forward_second_copies(n))
        return token

    def finish(gname, after):
        g = groups[gname]
        n = len(g["keys"])
        send_sems, recv_sems, bufs = g["hop"]
        if g["forward"]:
            send_sems, recv_sems, bufs, _ = _split_relay(gname + "_hop3", bufs, send_sems, recv_sems, after,
                                                         _forward_second_copies(n), n, _forward_third_copies(n))
            bufs = _split_wait(gname + "_wait", bufs, send_sems, recv_sems, after, _forward_third_copies(n))
        else:
            send_sems, recv_sems, bufs, _ = _split_relay(gname + "_relay", bufs, send_sems, recv_sems, after,
                                                         _gather_first_copies(n), n * (N_CHIP - 1), _gather_relay_copies(n))
            bufs = _split_wait(gname + "_wait", bufs, send_sems, recv_sems, after, _gather_relay_copies(n))
        g["done"] = dict(zip(g["keys"], bufs[n:]))

    c_all = _allgather_small("gather_c", c.reshape(D // LANES, LANES)).reshape(N_DEV, D)
    token = first_hop("gather_in", c_all)
    mod_cols, c_act = _ada_mod(jnp.pad(c_all, ((0, 16 - N_DEV), (0, 0))) + token[0:1, 0:1], w_ada[0])
    mod_vec = mod_cols[:N_DEV].reshape(-1, LANES)
    mg_send, mg_recv, mg_bufs, token = _split_start(
        "gather_mod_start", [mod_vec, lax.dynamic_update_slice(lax.empty((N_DEV, *mod_vec.shape), F32), mod_vec[None], (me, 0, 0))],
        N_DEV - 1, _small_gather_copies)
    token = second_hop("gather_in", token)
    token = first_hop("gather_ffn_in", first_hop("gather_mixer", token))
    mod_all = _split_wait("gather_mod_wait", mg_bufs, mg_send, mg_recv, token, _small_gather_copies)[1].reshape(N_DEV, N_DEV, ada_loc)
    mod = lax.dynamic_index_in_dim(mod_all, me, axis=1, keepdims=False).reshape(1, N_DEV * ada_loc) + b_ada

    def fetch(key, after):
        if key == "in":
            finish("gather_in", after)
        elif key == "fi_early":
            return first_hop("gather_ffn_out", second_hop("gather_ffn_in", after))
        elif "done" not in groups[group_of[key]]:
            finish(group_of[key], after)
        arr = groups[group_of[key]]["done"][key]
        return arr.reshape(-1, D) if key in ("out", "fo") else arr

    p = dict(norm1_g=norm1_g, b_gate=b_gate, ln_g=gmlp_ln_g, ln_b=gmlp_ln_b, ws=gmlp_ws[0], bs=gmlp_bs[0], hg_lb=hg_lb,
             hg_ng=hg_norm_g, norm2_g=norm2_g, final_g=final_norm_g.reshape(1, D))

    in_flight = {}
    c_idx = lax.axis_index("c").astype(jnp.int32).reshape(1)
    chip_idx = (2 * lax.axis_index("x") + lax.axis_index("y")).astype(jnp.int32).reshape(1)

    def scatter_start(name, grads):
        keys = list(grads)
        n = len(keys)
        stacks = [grads[k].reshape(N_DEV, -1, grads[k].shape[-1]) for k in keys]
        lands = [empty_hbm((N_CHIP, *g.shape[1:]), g.dtype) for g in stacks]
        send_sems, recv_sems, bufs, token = _split_start(name + "_d2d", stacks + lands, n * N_CHIP, _to_sibling_copies(n))
        in_flight[name] = dict(keys=keys, stage1=(send_sems, recv_sems, bufs))
        return token

    def scatter_push(name, after):
        f = in_flight[name]
        n = len(f["keys"])
        send_sems, recv_sems, bufs = f["stage1"]
        bufs = _split_wait(name + "_d2d_wait", bufs, send_sems, recv_sems, after, _to_sibling_copies(n))
        sums = [_chip_sum(f"{name}_sum_{k}", bufs[i], bufs[n + i], c_idx) for i, k in enumerate(f["keys"])]
        lands = [empty_hbm((N_CHIP - 1, *s.shape[1:]), s.dtype) for s in sums]
        send_sems, recv_sems, bufs, token = _split_start(name + "_ici", sums + lands, n * (N_CHIP - 1), _to_owner_copies(n))
        f["stage2"] = (send_sems, recv_sems, bufs)
        return token

    grad_x, _, small, dmod = _local_step(x[0], loss_target[0], mod, p, fetch, w_ffn_out.shape[1] * N_DEV, (scatter_start, scatter_push))

    small["b_ada"] = dmod
    packed, offs = _pack([small[k] for k in _SMALL] + [small["loss"]])
    sg_send, sg_recv, sg_bufs, t_tail = _split_start(
        "gather_small_start", [packed, lax.dynamic_update_slice(lax.empty((N_DEV, *packed.shape), F32), packed[None], (me, 0, 0))],
        N_DEV - 1, _small_gather_copies)
    t_tail = scatter_push("scatter_proj_in_b", t_tail)
    big_w = dict(w_in=(w_in, m_w_in, v_w_in, "w_in"), bg=(w_branch_gmlp, m_w_branch_gmlp, v_w_branch_gmlp, "w_branch_gmlp"),
                 bh=(w_branch_hg, m_w_branch_hg, v_w_branch_hg, "w_branch_hg"), out=(w_out, m_w_out, v_w_out, "w_out"),
                 fi=(w_ffn_in, m_w_ffn_in, v_w_ffn_in, "w_ffn_in"), fo=(w_ffn_out, m_w_ffn_out, v_w_ffn_out, "w_ffn_out"))
    upd = {}

    def land_and_update(name, after):
        keys = in_flight[name]["keys"]
        n = len(keys)
        send_sems, recv_sems, bufs = in_flight[name]["stage2"]
        bufs = _split_wait(name + "_ici_wait", bufs, send_sems, recv_sems, after, _to_owner_copies(n))
        for i, k in enumerate(keys):
            if k in big_w:
                wt, mt, vt, out_name = big_w[k]
                upd[out_name] = _adamw("adamw_" + out_name, wt[0], mt[0], vt[0], bufs[i], chip_idx, [bufs[n + i]])
            else:
                wt, mt, vt, out_name = big_w["w_in"]
                upd[out_name] = _adamw("adamw_" + k, wt[0], mt[0], vt[0], bufs[i], chip_idx, [bufs[n + i]],
                                       row0=0 if k == "w_in_a" else bufs[i].shape[1], into=upd.get(out_name))
            after = upd[out_name][1]
        return after

    after = land_and_update("scatter_mixer", land_and_update("scatter_ffn", t_tail))
    gathered = _split_wait("gather_small_wait", sg_bufs, sg_send, sg_recv, after, _small_gather_copies)[1]
    wp = dict(p, b_ada=b_ada)
    ms = dict(b_ada=m_b_ada, norm1_g=m_norm1_g, b_gate=m_b_gate, ln_g=m_gmlp_ln_g, ln_b=m_gmlp_ln_b, ws=m_gmlp_ws, bs=m_gmlp_bs,
              hg_lb=m_hg_lb, hg_ng=m_hg_norm_g, norm2_g=m_norm2_g, final_g=m_final_norm_g)
    vs = dict(b_ada=v_b_ada, norm1_g=v_norm1_g, b_gate=v_b_gate, ln_g=v_gmlp_ln_g, ln_b=v_gmlp_ln_b, ws=v_gmlp_ws, bs=v_gmlp_bs,
              hg_lb=v_hg_lb, hg_ng=v_hg_norm_g, norm2_g=v_norm2_g, final_g=v_final_norm_g)
    w_sm, _ = _pack([wp[k] for k in _SMALL])
    m_sm, _ = _pack([ms[k] for k in _SMALL])
    v_sm, _ = _pack([vs[k] for k in _SMALL])
    shapes = dict(b_ada=b_ada.shape, norm1_g=norm1_g.shape, b_gate=b_gate.shape, ln_g=gmlp_ln_g.shape, ln_b=gmlp_ln_b.shape,
                  ws=gmlp_ws.shape, bs=gmlp_bs.shape, hg_lb=hg_lb.shape, hg_ng=hg_norm_g.shape, norm2_g=norm2_g.shape,
                  final_g=final_norm_g.shape)
    sm_out = _small_update(gathered, w_sm, m_sm, v_sm, after, [math.prod(shapes[k]) // LANES for k in _SMALL])

    def unpack(idx, k):
        return sm_out[idx * len(_SMALL) + _SMALL.index(k)].reshape(shapes[k])

    loss = sm_out[-1][0, 0]

    dmod_all = gathered.reshape(N_DEV, -1)[:, offs[0]:offs[0] + N_DEV * ada_loc]
    dmod_loc = lax.dynamic_slice_in_dim(dmod_all, me * ada_loc, ada_loc, axis=1)
    ca_t = jnp.pad(c_act[:N_DEV].T, ((0, 0), (0, LANES - N_DEV))).astype(BF16)
    dm_p = jnp.pad(dmod_loc, ((0, LANES - N_DEV), (0, 0))).astype(BF16)
    tm_a = _tile(D, 512)
    g_ada = _matmul(
        "ada_dw", ca_t, dm_p, dims=_NN, grid_mnk=(D // tm_a, 1, 1), tiles=(tm_a, ada_loc),
        a_spec=pl.BlockSpec((tm_a, LANES), lambda i, j, k: (i, 0)), b_spec=pl.BlockSpec((LANES, ada_loc), lambda i, j, k: (0, 0)),
        out_shapes=[jax.ShapeDtypeStruct((1, D, ada_loc), F32)], out_specs=[pl.BlockSpec((None, tm_a, ada_loc), lambda i, j, k: (0, i, 0))],
        epilogue=_store(F32))[0]
    upd["w_ada"] = _adamw("adamw_w_ada", w_ada[0], m_w_ada[0], v_w_ada[0], g_ada, jnp.zeros((1,), jnp.int32))
    land_and_update("scatter_proj_in_b", land_and_update("scatter_proj_in_a", upd["w_ada"][1]))

    order = ("w_ada", "b_ada", "norm1_g", "w_in", "b_gate", "ln_g", "ln_b", "ws", "bs", "hg_lb", "hg_ng", "w_branch_gmlp", "w_branch_hg",
             "w_out", "norm2_g", "w_ffn_in", "w_ffn_out", "final_g")
    outs = [loss, grad_x[None]]
    for idx in range(4):
        for k in order:
            outs.append(upd[k][idx][None] if k in upd else unpack(idx, k))
    return tuple(outs)
```

```python
import functools
import math

import jax
import jax.numpy as jnp
from jax import lax
from jax.experimental import pallas as pl
from jax.experimental.pallas import tpu as pltpu

F32 = jnp.float32
BF16 = jnp.bfloat16
N_DEV = 8
EPS = 1e-6
LANES = 128
HG_DK = 128
HG_CHUNK = 64
HG_MID = HG_CHUNK // 2 - 1
EXP_CLAMP = 80.0
VMEM_LIMIT = 48 * 1024 * 1024
BF16_ROWS = 16
STREAM_TILE = 1 << 20
ADAMW_TILE = 3 << 17
ADAM_LR, ADAM_B1, ADAM_B2, ADAM_EPS, ADAM_WD, ADAM_STEP = 0.001, 0.9, 0.999, 1e-08, 0.01, 10
MESH = pl.DeviceIdType.MESH

_NN = (((1,), (0,)), ((), ()))
_NT = (((1,), (1,)), ((), ()))
_TN = (((0,), (0,)), ((), ()))


def _dot(a, b, dims=_NN):
    return lax.dot_general(a.astype(BF16), b.astype(BF16), dims, preferred_element_type=F32)


def _tile(n, target, mult=LANES):
    best = None
    for t in range(mult, min(n, target) + 1, mult):
        if n % t == 0:
            best = t
    return n if best is None else best


def _cparams(sem):
    return pltpu.CompilerParams(dimension_semantics=sem, vmem_limit_bytes=VMEM_LIMIT)


def _sigmoid(x):
    return 1.0 / (1.0 + jnp.exp(-x))


def _gelu_parts(x):
    k0 = math.sqrt(2.0 / math.pi)
    x2 = x * x
    t = jnp.tanh(k0 * (x + 0.044715 * x * x2))
    g = 0.5 * x * (1.0 + t)
    dg = 0.5 * (1.0 + t) + 0.5 * x * (1.0 - t * t) * (k0 * (1.0 + 3.0 * 0.044715 * x2))
    return g, dg


def _split3(x):
    h = x.astype(BF16)
    r = x - h.astype(F32)
    m = r.astype(BF16)
    lo = (r - m.astype(F32)).astype(BF16)
    return h, m, lo


def _ones_dot(mat01, x):
    h, m, lo = _split3(x)
    d = functools.partial(lax.dot_general, dimension_numbers=_NN, preferred_element_type=F32)
    return d(mat01, h) + d(mat01, m) + d(mat01, lo)


def _matmul(name, a, b, *, dims, grid_mnk, tiles, a_spec, b_spec, extras=(), extra_specs=(), out_shapes, out_specs, epilogue, after=None,
            sem=None):
    gm, gn, nk = grid_mnk
    tm, tn = tiles
    n_ex, n_out = len(extras), len(out_shapes)
    held = [] if after is None else [after]

    def body(*refs):
        a_ref, b_ref = refs[0], refs[1]
        ex = refs[2:2 + n_ex]
        outs = refs[2 + n_ex + len(held):2 + n_ex + len(held) + n_out]
        more = () if sem is None else (pl.program_id(0) == 0,)
        if nk == 1:
            epilogue(lax.dot_general(a_ref[...], b_ref[...], dims, preferred_element_type=F32), ex, outs, *more)
            return
        acc = refs[-1]
        k = pl.program_id(2)

        @pl.when(k == 0)
        def _():
            acc[...] = jnp.zeros_like(acc)

        acc[...] += lax.dot_general(a_ref[...], b_ref[...], dims, preferred_element_type=F32)

        @pl.when(k == nk - 1)
        def _():
            epilogue(acc[...], ex, outs, *more)

    return pl.pallas_call(
        body, name=name, grid=(gm, gn, nk), in_specs=[a_spec, b_spec, *extra_specs] + [pl.BlockSpec(memory_space=pl.ANY)] * len(held),
        out_specs=list(out_specs), out_shape=list(out_shapes), scratch_shapes=[] if nk == 1 else [pltpu.VMEM((tm, tn), F32)],
        compiler_params=_cparams(sem or ("parallel", "parallel", "arbitrary")),
    )(a, b, *extras, *held)


def _store(dtype):
    def ep(acc, ex, outs):
        outs[0][...] = acc.astype(dtype)
    return ep


def _mm_nn_stacked(name, a, wg, *, tm, tn, tk, out_dtype=F32, extras=(), extra_specs=(), out_shapes=None, out_specs=None, epilogue=None,
                   after=None):
    M, K = a.shape
    _, _, nloc = wg.shape
    N = nloc * N_DEV
    q = nloc // tn
    if out_shapes is None:
        out_shapes = [jax.ShapeDtypeStruct((M, N), out_dtype)]
        out_specs = [pl.BlockSpec((tm, tn), lambda i, j, k: (i, j))]
        epilogue = _store(out_dtype)
    return _matmul(
        name, a, wg, dims=_NN, grid_mnk=(M // tm, N // tn, K // tk), tiles=(tm, tn),
        a_spec=pl.BlockSpec((tm, tk), lambda i, j, k: (i, k)),
        b_spec=pl.BlockSpec((None, tk, tn), lambda i, j, k: (j // q, k, j % q)),
        extras=extras, extra_specs=extra_specs, out_shapes=out_shapes, out_specs=out_specs, epilogue=epilogue, after=after)


def _mm_nt_stacked(name, a_spec, a, wg, *, M, tm, tn, tk, out_dtype=F32, after=None, extras=(), extra_specs=(), out_shapes=None,
                   out_specs=None, epilogue=None, sem=None):
    _, Kw, nloc = wg.shape
    q = nloc // tk
    single = out_shapes is None
    if single:
        out_shapes = [jax.ShapeDtypeStruct((M, Kw), out_dtype)]
        out_specs = [pl.BlockSpec((tm, tn), lambda i, j, k: (i, j))]
        epilogue = _store(out_dtype)
    res = _matmul(
        name, a, wg, dims=_NT, grid_mnk=(M // tm, Kw // tn, (nloc * N_DEV) // tk), tiles=(tm, tn),
        a_spec=a_spec, b_spec=pl.BlockSpec((None, tn, tk), lambda i, j, k: (k // q, j, k % q)),
        extras=extras, extra_specs=extra_specs, out_shapes=out_shapes, out_specs=out_specs, epilogue=epilogue, after=after, sem=sem)
    return res[0] if single else res


def _mm_tn(name, a, b, b_spec, *, Mo, No, S, tm, tn, tk, stacked_nloc=None, after=None, a_off=0):
    if stacked_nloc is None:
        out_shape = jax.ShapeDtypeStruct((Mo, No), BF16)
        out_spec = pl.BlockSpec((tm, tn), lambda i, j, k: (i, j))
    else:
        q = stacked_nloc // tn
        out_shape = jax.ShapeDtypeStruct((N_DEV, Mo, stacked_nloc), BF16)
        out_spec = pl.BlockSpec((None, tm, tn), lambda i, j, k: (j // q, i, j % q))
    return _matmul(
        name, a, b, dims=_TN, grid_mnk=(Mo // tm, No // tn, S // tk), tiles=(tm, tn),
        a_spec=pl.BlockSpec((tk, tm), lambda i, j, k: (k, i + a_off)), b_spec=b_spec,
        out_shapes=[out_shape], out_specs=[out_spec], epilogue=_store(BF16), after=after)[0]


def _norm_mod(name, x, g, sc, sh):
    S, D = x.shape
    tm = _tile(S, 256, 8)

    def body(x_ref, g_ref, sc_ref, sh_ref, h_ref):
        xv = x_ref[...]
        r = lax.rsqrt(jnp.mean(xv * xv, axis=-1, keepdims=True) + EPS)
        h = (xv * r) * g_ref[...]
        h_ref[...] = (h * (1.0 + sc_ref[...]) + sh_ref[...]).astype(BF16)

    row = pl.BlockSpec((tm, D), lambda i: (i, 0))
    vec = pl.BlockSpec((1, D), lambda i: (0, 0))
    return pl.pallas_call(body, name=name, grid=(S // tm,), in_specs=[row, vec, vec, vec], out_specs=row,
                          out_shape=jax.ShapeDtypeStruct((S, D), BF16), compiler_params=_cparams(("parallel",)))(x, g, sc, sh)


def _norm_mod_bwd_rows(first, dh_v, x_ref, g_ref, sc_ref, dres_ref, dx_ref, vec_ref, o_ref=None, gt_ref=None, do_ref=None):
    @pl.when(first)
    def _():
        vec_ref[...] = jnp.zeros_like(vec_ref)

    xv, gv = x_ref[...], g_ref[...]
    r = lax.rsqrt(jnp.mean(xv * xv, axis=-1, keepdims=True) + EPS)
    xn = xv * r
    one_sc = 1.0 + sc_ref[...]
    vec_ref[0:1, :] += jnp.sum(dh_v, axis=0, keepdims=True)
    vec_ref[1:2, :] += jnp.sum(dh_v * (xn * gv), axis=0, keepdims=True)
    vec_ref[2:3, :] += jnp.sum(dh_v * one_sc * xn, axis=0, keepdims=True)
    dxn = dh_v * one_sc * gv
    dx = dres_ref[...] + r * (dxn - xn * jnp.mean(dxn * xn, axis=-1, keepdims=True))
    dx_ref[...] = dx
    if o_ref is not None:
        vec_ref[3:4, :] += jnp.sum(dx * o_ref[...], axis=0, keepdims=True)
        do_ref[...] = (dx * gt_ref[...]).astype(BF16)


def _norm_mod_bwd(name, dh, x, g, sc, dres, o=None, gt=None):
    S, D = x.shape
    tm = _tile(S, 256, 8)
    gated = o is not None

    def body(*refs):
        if gated:
            dh_ref, x_ref, g_ref, sc_ref, dres_ref, o_ref, gt_ref, dx_ref, vec_ref, do_ref = refs
        else:
            dh_ref, x_ref, g_ref, sc_ref, dres_ref, dx_ref, vec_ref = refs
            o_ref = gt_ref = do_ref = None
        _norm_mod_bwd_rows(pl.program_id(0) == 0, dh_ref[...], x_ref, g_ref, sc_ref, dres_ref, dx_ref, vec_ref, o_ref, gt_ref, do_ref)

    row = pl.BlockSpec((tm, D), lambda i: (i, 0))
    vec = pl.BlockSpec((1, D), lambda i: (0, 0))
    acc = pl.BlockSpec((8, D), lambda i: (0, 0))
    ins = [dh, x, g, sc, dres] + ([o, gt] if gated else [])
    in_specs = [row, row, vec, vec, row] + ([row, vec] if gated else [])
    out_shape = [jax.ShapeDtypeStruct((S, D), F32), jax.ShapeDtypeStruct((8, D), F32)]
    out_specs = [row, acc]
    if gated:
        out_shape.append(jax.ShapeDtypeStruct((S, D), BF16))
        out_specs.append(row)
    return pl.pallas_call(body, name=name, grid=(S // tm,), in_specs=in_specs, out_specs=out_specs, out_shape=out_shape,
                          compiler_params=_cparams(("arbitrary",)))(*ins)


def _loss_head(x3, tgt, gf, o2, gt2):
    S, D = x3.shape
    tm = _tile(S, 256, 8)

    def body(x_ref, t_ref, g_ref, o_ref, gt_ref, dx_ref, do_ref, vec_ref):
        i = pl.program_id(0)

        @pl.when(i == 0)
        def _():
            vec_ref[...] = jnp.zeros_like(vec_ref)

        xv, gv = x_ref[...], g_ref[...]
        r = lax.rsqrt(jnp.mean(xv * xv, axis=-1, keepdims=True) + EPS)
        xn = xv * r
        e = xn * gv - t_ref[...]
        tok = 0.5 * jnp.mean(e * e, axis=-1, keepdims=True)
        vec_ref[0:1, :] += jnp.broadcast_to(jnp.sum(tok, axis=0, keepdims=True), (1, D))
        dy = e * (1.0 / D)
        vec_ref[1:2, :] += jnp.sum(dy * xn, axis=0, keepdims=True)
        dxn = dy * gv
        dx = r * (dxn - xn * jnp.mean(dxn * xn, axis=-1, keepdims=True))
        dx_ref[...] = dx
        vec_ref[2:3, :] += jnp.sum(dx * o_ref[...], axis=0, keepdims=True)
        do_ref[...] = (dx * gt_ref[...]).astype(BF16)

    row = pl.BlockSpec((tm, D), lambda i: (i, 0))
    vec = pl.BlockSpec((1, D), lambda i: (0, 0))
    return pl.pallas_call(
        body, name="loss_head", grid=(S // tm,), in_specs=[row, row, vec, row, vec],
        out_specs=[row, row, pl.BlockSpec((8, D), lambda i: (0, 0))],
        out_shape=[jax.ShapeDtypeStruct((S, D), F32), jax.ShapeDtypeStruct((S, D), BF16), jax.ShapeDtypeStruct((8, D), F32)],
        compiler_params=_cparams(("arbitrary",)))(x3, tgt, gf, o2, gt2)


def _ffn_in_swiglu(h, wg):
    S, D = h.shape
    _, _, tf = wg.shape
    nf = N_DEV // 2
    F = nf * tf
    tm = _tile(S, 256, 16)

    def body(h_ref, wa_ref, wu_ref, hf_ref, fac_ref):
        hv = h_ref[...]
        a = lax.dot_general(hv, wa_ref[...], _NN, preferred_element_type=F32)
        up = lax.dot_general(hv, wu_ref[...], _NN, preferred_element_type=F32)
        sa = _sigmoid(a)
        silu = a * sa
        hf_ref[...] = (silu * up).astype(BF16)
        fac_ref[0] = (up * (sa * (1.0 + a * (1.0 - sa)))).astype(BF16)
        fac_ref[1] = silu.astype(BF16)

    return pl.pallas_call(
        body, name="ffn_in_swiglu", grid=(nf, S // tm),
        in_specs=[pl.BlockSpec((tm, D), lambda j, i: (i, 0)), pl.BlockSpec((None, D, tf), lambda j, i: (j, 0, 0)),
                  pl.BlockSpec((None, D, tf), lambda j, i: (j + nf, 0, 0))],
        out_specs=[pl.BlockSpec((tm, tf), lambda j, i: (i, j)), pl.BlockSpec((2, tm, tf), lambda j, i: (0, i, j))],
        out_shape=[jax.ShapeDtypeStruct((S, F), BF16), jax.ShapeDtypeStruct((2, S, F), BF16)],
        compiler_params=_cparams(("parallel", "parallel")))(h, wg, wg)


def _gmlp_common(u_ref, v_ref, lg_ref, lb_ref, ws_ref, bsb_ref, G, T, Dg):
    ug, dug = _gelu_parts(u_ref[...])
    vg, dvg = _gelu_parts(v_ref[...])
    mu = jnp.mean(vg, axis=-1, keepdims=True)
    vc = vg - mu
    rstd = lax.rsqrt(jnp.mean(vc * vc, axis=-1, keepdims=True) + EPS)
    vhat = vc * rstd
    vn = vhat * lg_ref[...] + lb_ref[...]
    row = lax.broadcasted_iota(jnp.int32, (T, T), 0)
    col = lax.broadcasted_iota(jnp.int32, (T, T), 1)
    tril = row >= col
    s = []
    for g in range(G):
        w = jnp.where(tril, ws_ref[g], 0.0)
        s.append(_dot(w, vn[:, g * Dg:(g + 1) * Dg]) + bsb_ref[g])
    return ug, dug, dvg, rstd, vhat, vn, tril, s


def _gmlp_fwd(z, ln_g, ln_b, ws, bsb, GW):
    S = z.shape[0]
    G, T, _ = ws.shape
    Dg = GW // G

    def body(u_ref, v_ref, lg_ref, lb_ref, ws_ref, bsb_ref, ya_ref):
        ug, _, _, _, _, _, _, s = _gmlp_common(u_ref, v_ref, lg_ref, lb_ref, ws_ref, bsb_ref, G, T, Dg)
        for g in range(G):
            sl = slice(g * Dg, (g + 1) * Dg)
            ya_ref[:, sl] = (ug[:, sl] * s[g]).astype(BF16)

    vec = pl.BlockSpec((1, GW), lambda c: (0, 0))
    return pl.pallas_call(
        body, name="gmlp_fwd", grid=(S // T,),
        in_specs=[pl.BlockSpec((T, GW), lambda c: (c, 0)), pl.BlockSpec((T, GW), lambda c: (c, 1)), vec, vec,
                  pl.BlockSpec((G, T, T), lambda c: (0, 0, 0)), pl.BlockSpec((G, T, Dg), lambda c: (0, 0, 0))],
        out_specs=pl.BlockSpec((T, GW), lambda c: (c, 0)), out_shape=jax.ShapeDtypeStruct((S, GW), BF16),
        compiler_params=_cparams(("parallel",)))(z, z, ln_g, ln_b, ws, bsb)


def _gmlp_bwd(z, dya, ln_g, ln_b, ws, bsb, GW):
    S = z.shape[0]
    G, T, _ = ws.shape
    Dg = GW // G
    nc = S // T

    def body(u_ref, v_ref, dya_ref, lg_ref, lb_ref, ws_ref, bsb_ref, dz_ref, dln_ref, dws_ref, dbs_ref, dbs_acc, dvh):
        c = pl.program_id(0)

        @pl.when(c == 0)
        def _():
            dln_ref[...] = jnp.zeros_like(dln_ref)
            dws_ref[...] = jnp.zeros_like(dws_ref)
            dbs_acc[...] = jnp.zeros_like(dbs_acc)

        ug, dug, dvg, rstd, vhat, vn, tril, s = _gmlp_common(u_ref, v_ref, lg_ref, lb_ref, ws_ref, bsb_ref, G, T, Dg)
        dya_v = dya_ref[...]
        for g in range(G):
            sl = slice(g * Dg, (g + 1) * Dg)
            dy_g = dya_v[:, sl]
            dz_ref[:, sl] = (dy_g * s[g] * dug[:, sl]).astype(BF16)
            ds = dy_g * ug[:, sl]
            dbs_acc[g] += ds
            w = jnp.where(tril, ws_ref[g], 0.0)
            dvn_g = _dot(w, ds, _TN)
            dws_ref[g] += jnp.where(tril, _dot(ds, vn[:, sl], _NT), 0.0)
            dln_ref[0:1, sl] += jnp.sum(dvn_g * vhat[:, sl], axis=0, keepdims=True)
            dln_ref[1:2, sl] += jnp.sum(dvn_g, axis=0, keepdims=True)
            dvh[:, sl] = dvn_g * lg_ref[:, sl]
        dvhat = dvh[...]
        m1 = jnp.mean(dvhat, axis=-1, keepdims=True)
        m2 = jnp.mean(dvhat * vhat, axis=-1, keepdims=True)
        dz_ref[:, GW:2 * GW] = (rstd * (dvhat - m1 - vhat * m2) * dvg).astype(BF16)

        @pl.when(c == nc - 1)
        def _():
            for g in range(G):
                dbs_ref[g] = jnp.sum(dbs_acc[g], axis=-1, keepdims=True)

    vec = pl.BlockSpec((1, GW), lambda c: (0, 0))
    return pl.pallas_call(
        body, name="gmlp_bwd", grid=(nc,),
        in_specs=[pl.BlockSpec((T, GW), lambda c: (c, 0)), pl.BlockSpec((T, GW), lambda c: (c, 1)),
                  pl.BlockSpec((T, GW), lambda c: (c, 0)), vec, vec,
                  pl.BlockSpec((G, T, T), lambda c: (0, 0, 0)), pl.BlockSpec((G, T, Dg), lambda c: (0, 0, 0))],
        out_specs=[pl.BlockSpec((T, 2 * GW), lambda c: (c, 0)), pl.BlockSpec((8, GW), lambda c: (0, 0)),
                   pl.BlockSpec((G, T, T), lambda c: (0, 0, 0)), pl.BlockSpec((G, T, 1), lambda c: (0, 0, 0))],
        out_shape=[jax.ShapeDtypeStruct((S, 2 * GW), BF16), jax.ShapeDtypeStruct((8, GW), F32),
                   jax.ShapeDtypeStruct((G, T, T), F32), jax.ShapeDtypeStruct((G, T, 1), F32)],
        scratch_shapes=[pltpu.VMEM((G, T, Dg), F32), pltpu.VMEM((T, GW), F32)],
        compiler_params=_cparams(("arbitrary",)))(z, z, dya, ln_g, ln_b, ws, bsb)


def _hg_common(q_ref, f_ref, hlb_ref):
    C = HG_CHUNK
    a = hlb_ref[...]
    lb = _sigmoid(a[0:1, :] - a[1:2, :])
    sig = _sigmoid(f_ref[...])
    f = lb + (1.0 - lb) * sig
    lf = jnp.log(f)
    kk = 1.0 - f
    q = q_ref[...]
    sq = _sigmoid(q)
    qa = q * sq
    row = lax.broadcasted_iota(jnp.int32, (C, C), 0)
    col = lax.broadcasted_iota(jnp.int32, (C, C), 1)
    tril = row >= col
    b = _ones_dot(tril.astype(BF16), lf)
    bm = b[HG_MID:HG_MID + 1, :]
    bl = b[C - 1:C, :]
    e_b = jnp.exp(b)
    e_qm = jnp.exp(jnp.minimum(b - bm, EXP_CLAMP))
    e_km = jnp.exp(jnp.minimum(bm - b, EXP_CLAMP))
    e_kl = jnp.exp(bl - b)
    return dict(lb=lb, sig=sig, f=f, kk=kk, q=q, sq=sq, qa=qa, tril=tril, e_b=e_b, e_qm=e_qm, e_km=e_km, e_kl=e_kl,
                e_l=jnp.exp(bl), qh=qa * e_b, qt=qa * e_qm, kt=kk * e_km, kh=kk * e_kl)


def _hg_fwd(z, hg_lb, ng, HW):
    S = z.shape[0]
    C, H, dk = HG_CHUNK, HW // HG_DK, HG_DK
    nc = S // C

    def body(q_ref, f_ref, i_ref, og_ref, hlb_ref, ng_ref, yb_ref, o_ref, st_ref, state):
        @pl.when(pl.program_id(0) == 0)
        def _():
            state[...] = jnp.zeros_like(state)

        t = _hg_common(q_ref, f_ref, hlb_ref)
        iv = i_ref[...]
        for h in range(H):
            sl = slice(h * dk, (h + 1) * dk)
            st = state[h]
            st_ref[h] = st
            a = jnp.where(t["tril"], _dot(t["qt"][:, sl], t["kt"][:, sl], _NT), 0.0)
            o_h = _dot(a, iv[:, sl]) + _dot(t["qh"][:, sl], st, _NT)
            state[h] = st * t["e_l"][:, sl] + _dot(iv[:, sl], t["kh"][:, sl], _TN)
            o_ref[:, sl] = o_h
            rr = lax.rsqrt(jnp.mean(o_h * o_h, axis=-1, keepdims=True) + EPS)
            og = og_ref[:, sl]
            yb_ref[:, sl] = (o_h * rr * ng_ref[:, sl] * (og * _sigmoid(og))).astype(BF16)

    def col(k):
        return pl.BlockSpec((C, HW), lambda c: (c, k))

    base = 2
    return pl.pallas_call(
        body, name="hgrn_fwd", grid=(nc,),
        in_specs=[col(base), col(base + 1), col(base + 2), col(base + 3),
                  pl.BlockSpec((2, HW), lambda c: (0, 0)), pl.BlockSpec((1, HW), lambda c: (0, 0))],
        out_specs=[pl.BlockSpec((C, HW), lambda c: (c, 0)), pl.BlockSpec((C, HW), lambda c: (c, 0)),
                   pl.BlockSpec((None, H, dk, dk), lambda c: (c, 0, 0, 0))],
        out_shape=[jax.ShapeDtypeStruct((S, HW), BF16), jax.ShapeDtypeStruct((S, HW), F32),
                   jax.ShapeDtypeStruct((nc, H, dk, dk), F32)],
        scratch_shapes=[pltpu.VMEM((H, dk, dk), F32)],
        compiler_params=_cparams(("arbitrary",)))(z, z, z, z, hg_lb, ng)


def _hg_bwd(z, o, states, dyb, hg_lb, ng, HW, dz_head, dz_tail):
    S = z.shape[0]
    C, H, dk = HG_CHUNK, HW // HG_DK, HG_DK
    nc = S // C
    B0 = dz_head.shape[1]
    DT = dz_tail.shape[2]
    INW = B0 + 4 * HW + 2 * DT

    def body(q_ref, f_ref, i_ref, og_ref, o_ref, st_ref, stn_ref, dyb_ref, hlb_ref, ng_ref, head_ref, tail_ref,
             dzf_ref, dng_ref, dhlb_ref, dtail_ref, dstate, cross, dqa_buf, dkk_buf, db_buf, dlb_acc):
        c = pl.program_id(0)
        dzf_ref[:, 0:B0] = head_ref[...]
        dzf_ref[:, B0 + 4 * HW:B0 + 4 * HW + DT] = tail_ref[0]
        dzf_ref[:, B0 + 4 * HW + DT:INW] = tail_ref[1]
        dz_ref = dzf_ref.at[:, B0:B0 + 4 * HW]

        @pl.when(c == 0)
        def _():
            dtail_ref[...] = jnp.zeros_like(dtail_ref)

        dtail_ref[0:1, :] += jnp.sum(tail_ref[0].astype(F32), axis=0, keepdims=True)
        dtail_ref[1:2, :] += jnp.sum(tail_ref[1].astype(F32), axis=0, keepdims=True)

        @pl.when(c == 0)
        def _():
            dstate[...] = jnp.zeros_like(dstate)
            dlb_acc[...] = jnp.zeros_like(dlb_acc)
            dng_ref[...] = jnp.zeros_like(dng_ref)

        def r16(v):
            return v.astype(BF16).astype(F32)

        t = _hg_common(q_ref, f_ref, hlb_ref)
        iv = i_ref[...]
        for h in range(H):
            sl = slice(h * dk, (h + 1) * dk)
            o_h, og, dyb_h, ng_h = o_ref[:, sl], og_ref[:, sl], dyb_ref[:, sl], ng_ref[:, sl]
            sg = _sigmoid(og)
            silu_og = og * sg
            rr = lax.rsqrt(jnp.mean(o_h * o_h, axis=-1, keepdims=True) + EPS)
            on = o_h * rr
            dng_ref[0:1, sl] += jnp.sum(dyb_h * on * silu_og, axis=0, keepdims=True)
            dz_ref[:, 3 * HW + h * dk:3 * HW + (h + 1) * dk] = (dyb_h * on * ng_h * (sg * (1.0 + og * (1.0 - sg)))).astype(BF16)
            don = dyb_h * ng_h * silu_og
            do_h = rr * (don - on * jnp.mean(don * on, axis=-1, keepdims=True))

            qt, kt, qh, kh, iv_h = t["qt"][:, sl], t["kt"][:, sl], t["qh"][:, sl], t["kh"][:, sl], iv[:, sl]
            a = jnp.where(t["tril"], _dot(qt, kt, _NT), 0.0)
            da = jnp.where(t["tril"], _dot(do_h, iv_h, _NT), 0.0)
            st, dst = st_ref[h], dstate[h]
            cross[:, sl] = jnp.sum(dst * stn_ref[h], axis=0, keepdims=True)
            dqh = _dot(do_h, st)
            dstate[h] = _dot(do_h, qh, _TN) + dst * t["e_l"][:, sl]
            div = _dot(a, do_h, _TN) + _dot(kh, dst, _NT)
            dkh = _dot(iv_h, dst)
            dqt = _dot(da, kt)
            dkt = _dot(da, qt, _TN)
            dz_ref[:, 2 * HW + h * dk:2 * HW + (h + 1) * dk] = div.astype(BF16)
            dqa_buf[:, sl] = dqh * t["e_b"][:, sl] + dqt * t["e_qm"][:, sl]
            dkk_buf[:, sl] = dkt * t["e_km"][:, sl] + dkh * t["e_kl"][:, sl]
            db_buf[:, sl] = r16(qt) * dqt - r16(kt) * dkt + r16(qh) * dqh - r16(kh) * dkh

        dqa, dkk = dqa_buf[...], dkk_buf[...]
        triu = jnp.logical_not(t["tril"]) | (lax.broadcasted_iota(jnp.int32, (C, C), 0) == lax.broadcasted_iota(jnp.int32, (C, C), 1))
        dlf = _ones_dot(triu.astype(BF16), db_buf[...]) + cross[...]
        df = dlf / t["f"] - dkk
        sig, lb = t["sig"], t["lb"]
        dz_ref[:, HW:2 * HW] = (df * (1.0 - lb) * sig * (1.0 - sig)).astype(BF16)
        dlb_acc[...] += jnp.sum(df * (1.0 - sig), axis=0, keepdims=True)
        q, sq = t["q"], t["sq"]
        dz_ref[:, 0:HW] = (dqa * (sq * (1.0 + q * (1.0 - sq)))).astype(BF16)

        @pl.when(c == nc - 1)
        def _():
            da0 = dlb_acc[...] * lb * (1.0 - lb)
            dhlb_ref[0:1, :] = da0
            dhlb_ref[1:2, :] = -da0

    def col(k):
        return pl.BlockSpec((C, HW), lambda c: (nc - 1 - c, k))

    base = 2
    return pl.pallas_call(
        body, name="hgrn_bwd", grid=(nc,),
        in_specs=[col(base), col(base + 1), col(base + 2), col(base + 3), col(0),
                  pl.BlockSpec((None, H, dk, dk), lambda c: (nc - 1 - c, 0, 0, 0)),
                  pl.BlockSpec((None, H, dk, dk), lambda c: (jnp.minimum(nc - c, nc - 1), 0, 0, 0)), col(0),
                  pl.BlockSpec((2, HW), lambda c: (0, 0)), pl.BlockSpec((1, HW), lambda c: (0, 0)),
                  pl.BlockSpec((C, B0), lambda c: (nc - 1 - c, 0)), pl.BlockSpec((2, C, DT), lambda c: (0, nc - 1 - c, 0))],
        out_specs=[pl.BlockSpec((C, INW), lambda c: (nc - 1 - c, 0)), pl.BlockSpec((8, HW), lambda c: (0, 0)),
                   pl.BlockSpec((2, HW), lambda c: (0, 0)), pl.BlockSpec((2, DT), lambda c: (0, 0))],
        out_shape=[jax.ShapeDtypeStruct((S, INW), BF16), jax.ShapeDtypeStruct((8, HW), F32), jax.ShapeDtypeStruct((2, HW), F32),
                   jax.ShapeDtypeStruct((2, DT), F32)],
        scratch_shapes=[pltpu.VMEM((H, dk, dk), F32), pltpu.VMEM((1, HW), F32), pltpu.VMEM((C, HW), F32), pltpu.VMEM((C, HW), F32),
                        pltpu.VMEM((C, HW), F32), pltpu.VMEM((1, HW), F32)],
        compiler_params=_cparams(("arbitrary",)))(z, z, z, z, o, states, states, dyb, hg_lb, ng, dz_head, dz_tail)


def _position():
    x, y, c = lax.axis_index("x"), lax.axis_index("y"), lax.axis_index("c")
    return x, y, c, 4 * x + 2 * y + c


def _flip(x, y, c, k):
    return (1 - x if k & 4 else x, 1 - y if k & 2 else y, 1 - c if k & 1 else c)


def _allgather_small(name, v):
    R, L = v.shape

    def body(v_ref, out_ref, send_sems, recv_sems):
        x, y, c, me = _position()
        out_ref[me] = v_ref[...]
        copies = []
        for k in range(1, N_DEV):
            cp = pltpu.make_async_remote_copy(src_ref=v_ref, dst_ref=out_ref.at[me], send_sem=send_sems.at[k - 1],
                                              recv_sem=recv_sems.at[k - 1], device_id=_flip(x, y, c, k), device_id_type=MESH)
            cp.start()
            copies.append(cp)
        for cp in copies:
            cp.wait()

    return pl.pallas_call(
        body, name=name, out_shape=jax.ShapeDtypeStruct((N_DEV, R, L), v.dtype),
        in_specs=[pl.BlockSpec(memory_space=pltpu.VMEM)], out_specs=pl.BlockSpec(memory_space=pltpu.VMEM),
        scratch_shapes=[pltpu.SemaphoreType.DMA((N_DEV - 1,)), pltpu.SemaphoreType.DMA((N_DEV - 1,))],
        compiler_params=pltpu.CompilerParams(vmem_limit_bytes=VMEM_LIMIT),
    )(v)


_HBM = pl.BlockSpec(memory_space=pltpu.HBM)
_SEM = pl.BlockSpec(memory_space=pltpu.SEMAPHORE)
_EFFECT = pltpu.SideEffectType.DATAFLOW_SIDE_EFFECTING


def _split_start(name, bufs, n_sems, copies_fn, after=None):
    nb = len(bufs)
    extra = [] if after is None else [after]
    k = nb + len(extra)

    def body(*refs):
        for cp in copies_fn(refs[:nb], refs[k], refs[k + 1]):
            cp.start()
        refs[-1][...] = jnp.zeros_like(refs[-1])

    sems = pltpu.SemaphoreType.DMA((n_sems,))
    res = pl.pallas_call(
        body, name=name,
        out_shape=(sems, sems, *[pltpu.HBM(a.shape, a.dtype) for a in bufs], jax.ShapeDtypeStruct((8, LANES), F32)),
        in_specs=[_HBM] * nb + [pl.BlockSpec(memory_space=pl.ANY)] * len(extra),
        out_specs=(_SEM, _SEM, *[_HBM] * nb, pl.BlockSpec(memory_space=pltpu.VMEM)),
        input_output_aliases={i: 2 + i for i in range(nb)},
        compiler_params=pltpu.CompilerParams(has_side_effects=_EFFECT),
    )(*[pltpu.with_memory_space_constraint(a, pltpu.HBM) for a in bufs], *extra)
    return res[0], res[1], list(res[2:2 + nb]), res[-1]


def _split_wait(name, bufs, send_sems, recv_sems, after, copies_fn):
    nb = len(bufs)

    def body(*refs):
        for cp in copies_fn(refs[:nb], refs[nb], refs[nb + 1]):
            cp.wait_send()
            cp.wait_recv()

    res = pl.pallas_call(
        body, name=name, out_shape=tuple(pltpu.HBM(a.shape, a.dtype) for a in bufs),
        in_specs=[_HBM] * nb + [_SEM, _SEM, pl.BlockSpec(memory_space=pl.ANY)], out_specs=tuple([_HBM] * nb),
        input_output_aliases={i: i for i in range(nb)},
        compiler_params=pltpu.CompilerParams(has_side_effects=_EFFECT),
    )(*bufs, send_sems, recv_sems, after)
    return list(res)


def _split_relay(name, bufs, send_sems, recv_sems, after, wait_fn, n_sems, start_fn):
    nb = len(bufs)

    def body(*refs):
        for cp in wait_fn(refs[:nb], refs[nb], refs[nb + 1]):
            cp.wait_send()
            cp.wait_recv()
        for cp in start_fn(refs[:nb], refs[nb + 3], refs[nb + 4]):
            cp.start()
        refs[-1][...] = jnp.zeros_like(refs[-1])

    sems = pltpu.SemaphoreType.DMA((n_sems,))
    res = pl.pallas_call(
        body, name=name, out_shape=(sems, sems, *[pltpu.HBM(a.shape, a.dtype) for a in bufs], jax.ShapeDtypeStruct((8, LANES), F32)),
        in_specs=[_HBM] * nb + [_SEM, _SEM, pl.BlockSpec(memory_space=pl.ANY)],
        out_specs=(_SEM, _SEM, *[_HBM] * nb, pl.BlockSpec(memory_space=pltpu.VMEM)),
        input_output_aliases={i: 2 + i for i in range(nb)},
        compiler_params=pltpu.CompilerParams(has_side_effects=_EFFECT),
    )(*bufs, send_sems, recv_sems, after)
    return res[0], res[1], list(res[2:2 + nb]), res[-1]


N_CHIP = 4


def _chip_flip(x, y, k):
    return (1 - x if k & 2 else x), (1 - y if k & 1 else y)


def _slot(ref, k, width):
    return ref.at[k] if len(ref.shape) == 3 else ref.at[:, pl.ds(pl.multiple_of(k * width, width), width)]


def _gather_first_copies(n):
    def copies(bufs, send_sems, recv_sems):
        x, y, c, me = _position()
        out = []
        for w in range(n):
            for k in range(N_CHIP):
                to = (x, y, 1 - c) if k == 0 else (*_chip_flip(x, y, k), c)
                out.append(pltpu.make_async_remote_copy(
                    src_ref=bufs[w], dst_ref=_slot(bufs[n + w], me, bufs[w].shape[1]), send_sem=send_sems.at[w * N_CHIP + k],
                    recv_sem=recv_sems.at[w * N_CHIP + k], device_id=to, device_id_type=MESH))
        return out
    return copies


def _gather_relay_copies(n):
    def copies(bufs, send_sems, recv_sems):
        x, y, c, _ = _position()
        out = []
        for w in range(n):
            for k in range(1, N_CHIP):
                px, py = _chip_flip(x, y, k)
                blk = _slot(bufs[n + w], 4 * px + 2 * py + c, bufs[w].shape[1])
                out.append(pltpu.make_async_remote_copy(
                    src_ref=blk, dst_ref=blk, send_sem=send_sems.at[w * (N_CHIP - 1) + k - 1],
                    recv_sem=recv_sems.at[w * (N_CHIP - 1) + k - 1], device_id=(x, y, 1 - c), device_id_type=MESH))
        return out
    return copies


def _small_gather_copies(bufs, send_sems, recv_sems):
    x, y, c, me = _position()
    return [pltpu.make_async_remote_copy(src_ref=bufs[0], dst_ref=bufs[1].at[me], send_sem=send_sems.at[k - 1], recv_sem=recv_sems.at[k - 1],
                                         device_id=_flip(x, y, c, k), device_id_type=MESH) for k in range(1, N_DEV)]


def _forward_first_copies(n):
    def copies(bufs, send_sems, recv_sems):
        x, y, c, me = _position()
        out = []
        for w in range(n):
            for k, to in enumerate([(x, y, 1 - c), (1 - x, y, c), (x, 1 - y, c)]):
                out.append(pltpu.make_async_remote_copy(
                    src_ref=bufs[w], dst_ref=bufs[n + w].at[me], send_sem=send_sems.at[w * 3 + k],
                    recv_sem=recv_sems.at[w * 3 + k], device_id=to, device_id_type=MESH))
        return out
    return copies


def _forward_second_copies(n):
    def copies(bufs, send_sems, recv_sems):
        x, y, c, _ = _position()
        out = []
        for w in range(n):
            half = bufs[n + w].shape[1] // 2
            for k, (src_chip, rows, to) in enumerate([((1 - x, y), pl.ds(0, half), (x, 1 - y, c)), ((x, 1 - y), pl.ds(half, half), (1 - x, y, c))]):
                blk = bufs[n + w].at[4 * src_chip[0] + 2 * src_chip[1] + c, rows]
                out.append(pltpu.make_async_remote_copy(src_ref=blk, dst_ref=blk, send_sem=send_sems.at[w * 4 + k],
                                                        recv_sem=recv_sems.at[w * 4 + k], device_id=to, device_id_type=MESH))
            for k, (px, py) in enumerate([(1 - x, y), (x, 1 - y)]):
                blk = bufs[n + w].at[4 * px + 2 * py + c]
                out.append(pltpu.make_async_remote_copy(src_ref=blk, dst_ref=blk, send_sem=send_sems.at[w * 4 + 2 + k],
                                                        recv_sem=recv_sems.at[w * 4 + 2 + k], device_id=(x, y, 1 - c), device_id_type=MESH))
        return out
    return copies


def _forward_third_copies(n):
    def copies(bufs, send_sems, recv_sems):
        x, y, c, _ = _position()
        out = []
        for w in range(n):
            blk = bufs[n + w].at[4 * (1 - x) + 2 * (1 - y) + c]
            out.append(pltpu.make_async_remote_copy(src_ref=blk, dst_ref=blk, send_sem=send_sems.at[w], recv_sem=recv_sems.at[w],
                                                    device_id=(x, y, 1 - c), device_id_type=MESH))
        return out
    return copies


def _to_sibling_copies(n):
    def copies(bufs, send_sems, recv_sems):
        x, y, c, _ = _position()
        out = []
        for w in range(n):
            for q in range(N_CHIP):
                out.append(pltpu.make_async_remote_copy(
                    src_ref=bufs[w].at[2 * q + 1 - c], dst_ref=bufs[n + w].at[q], send_sem=send_sems.at[w * N_CHIP + q],
                    recv_sem=recv_sems.at[w * N_CHIP + q], device_id=(x, y, 1 - c), device_id_type=MESH))
        return out
    return copies


def _to_owner_copies(n):
    def copies(bufs, send_sems, recv_sems):
        x, y, c, _ = _position()
        out = []
        for w in range(n):
            for k in range(1, N_CHIP):
                px, py = (1 - x if k & 2 else x), (1 - y if k & 1 else y)
                out.append(pltpu.make_async_remote_copy(
                    src_ref=bufs[w].at[2 * px + py], dst_ref=bufs[n + w].at[k - 1], send_sem=send_sems.at[w * (N_CHIP - 1) + k - 1],
                    recv_sem=recv_sems.at[w * (N_CHIP - 1) + k - 1], device_id=(px, py, c), device_id_type=MESH))
        return out
    return copies


def _chip_sum(name, stack, landed, c_idx):
    _, R, C = stack.shape
    tr = _tile(R, max(BF16_ROWS, STREAM_TILE // C), BF16_ROWS)

    def body(c_ref, a_ref, b_ref, o_ref):
        o_ref[...] = (a_ref[...].astype(F32) + b_ref[...].astype(F32)).astype(o_ref.dtype)

    return pl.pallas_call(
        body, name=name,
        grid_spec=pltpu.PrefetchScalarGridSpec(
            num_scalar_prefetch=1, grid=(N_CHIP, R // tr),
            in_specs=[pl.BlockSpec((None, tr, C), lambda q, i, c_ref: (2 * q + c_ref[0], i, 0)),
                      pl.BlockSpec((None, tr, C), lambda q, i, c_ref: (q, i, 0))],
            out_specs=pl.BlockSpec((None, tr, C), lambda q, i, c_ref: (q, i, 0))),
        out_shape=jax.ShapeDtypeStruct((N_CHIP, R, C), stack.dtype),
        compiler_params=_cparams(("parallel", "parallel")))(c_idx, stack, landed)


def _ada_mod(c16, w):
    _, D = c16.shape
    n = w.shape[1]
    tk = _tile(D, 512)
    nk = D // tk

    def body(c_ref, w_ref, o_ref, ca_ref):
        @pl.when(pl.program_id(0) == 0)
        def _():
            o_ref[...] = jnp.zeros_like(o_ref)

        cv = c_ref[...]
        ca = cv * _sigmoid(cv)
        ca_ref[...] = ca
        o_ref[...] += _dot(ca, w_ref[...])

    return pl.pallas_call(
        body, name="ada_mod", grid=(nk,),
        in_specs=[pl.BlockSpec((16, tk), lambda k: (0, k)), pl.BlockSpec((tk, n), lambda k: (k, 0))],
        out_specs=[pl.BlockSpec((16, n), lambda k: (0, 0)), pl.BlockSpec((16, tk), lambda k: (0, k))],
        out_shape=[jax.ShapeDtypeStruct((16, n), F32), jax.ShapeDtypeStruct((16, D), F32)],
        compiler_params=_cparams(("arbitrary",)))(c16, w)


def _cast_shard(name, wf, slot, natural=False):
    r, c = wf.shape
    tr = _tile(r, max(BF16_ROWS, STREAM_TILE // c), BF16_ROWS)

    def body(slot_ref, w_ref, s_ref, g_ref):
        v = w_ref[...].astype(BF16)
        s_ref[...] = v
        g_ref[...] = v

    if natural:
        g_shape, g_spec = (r, N_DEV * c), pl.BlockSpec((tr, c), lambda i, s: (i, s[0]))
    else:
        g_shape, g_spec = (N_DEV, r, c), pl.BlockSpec((None, tr, c), lambda i, s: (s[0], i, 0))
    return pl.pallas_call(
        body, name=name,
        grid_spec=pltpu.PrefetchScalarGridSpec(
            num_scalar_prefetch=1, grid=(r // tr,), in_specs=[pl.BlockSpec((tr, c), lambda i, s: (i, 0))],
            out_specs=[pl.BlockSpec((tr, c), lambda i, s: (i, 0)), g_spec]),
        out_shape=[jax.ShapeDtypeStruct((r, c), BF16), jax.ShapeDtypeStruct(g_shape, BF16)],
        compiler_params=_cparams(("parallel",)))(slot, wf)


def _adam_math(w, g, m, v):
    m2 = ADAM_B1 * m + (1.0 - ADAM_B1) * g
    v2 = ADAM_B2 * v + (1.0 - ADAM_B2) * (g * g)
    m_hat = m2 / (1.0 - ADAM_B1 ** ADAM_STEP)
    v_hat = v2 / (1.0 - ADAM_B2 ** ADAM_STEP)
    delta = -ADAM_LR * (m_hat / (jnp.sqrt(v_hat) + ADAM_EPS) + ADAM_WD * w)
    return delta, m2, v2


def _adamw(name, w, m, v, own, own_slot, parts=(), row0=0, into=None):
    R, C = w.shape
    Rp = own.shape[1]
    tr = _tile(Rp, max(BF16_ROWS, ADAMW_TILE // C), BF16_ROWS)
    off = row0 // tr
    n_p = len(parts)
    held = [] if into is None else list(into)

    def body(slot_ref, *refs):
        w_ref, m_ref, v_ref, own_ref = refs[:4]
        g_ref, d_ref, m2_ref, v2_ref = refs[4 + n_p + len(held):]
        g = own_ref[...].astype(F32)
        for p_ref in refs[4:4 + n_p]:
            for s in range(p_ref.shape[0]):
                g = g + p_ref[s].astype(F32)
        delta, m2, v2 = _adam_math(w_ref[...], g, m_ref[...], v_ref[...])
        g_ref[...] = g
        d_ref[...] = delta
        m2_ref[...] = m2
        v2_ref[...] = v2

    blk = pl.BlockSpec((tr, C), lambda i, s: (i + off, 0))
    out = jax.ShapeDtypeStruct((R, C), F32)
    return pl.pallas_call(
        body, name=name,
        grid_spec=pltpu.PrefetchScalarGridSpec(
            num_scalar_prefetch=1, grid=(Rp // tr,),
            in_specs=[blk, blk, blk, pl.BlockSpec((None, tr, C), lambda i, s: (s[0], i, 0))]
            + [pl.BlockSpec((a.shape[0], tr, C), lambda i, s: (0, i, 0)) for a in parts]
            + [pl.BlockSpec(memory_space=pl.ANY)] * len(held),
            out_specs=[blk] * 4),
        out_shape=[out] * 4, input_output_aliases={5 + n_p + i: i for i in range(len(held))},
        compiler_params=_cparams(("parallel",)))(own_slot, w, m, v, own, *parts, *held)


def _small_update(gathered, w, m, v, after, rows):
    _, R, L = gathered.shape
    rs = w.shape[0]
    n = len(rows)
    assert all(r % 8 == 0 for r in rows) and sum(rows) <= rs and rs + 8 <= R

    def body(p_ref, w_ref, m_ref, v_ref, after_ref, *outs):
        g = p_ref[0]
        for p in range(1, N_DEV):
            g = g + p_ref[p]
        kinds = (g,) + _adam_math(w_ref[...], g[0:rs, :], m_ref[...], v_ref[...])
        at = 0
        for k, r in enumerate(rows):
            for idx, val in enumerate(kinds):
                outs[idx * n + k][...] = val[at:at + r, :]
            at += r
        outs[4 * n][...] = g[at:at + 8, :]

    vm = pl.BlockSpec(memory_space=pltpu.VMEM)
    shapes = [jax.ShapeDtypeStruct((r, L), F32) for _ in range(4) for r in rows] + [jax.ShapeDtypeStruct((8, L), F32)]
    return pl.pallas_call(body, name="small_update", in_specs=[vm] * 4 + [pl.BlockSpec(memory_space=pl.ANY)], out_specs=[vm] * len(shapes),
                          out_shape=shapes, compiler_params=pltpu.CompilerParams(vmem_limit_bytes=VMEM_LIMIT))(gathered, w, m, v, after)


class _Fetched(dict):
    def __init__(self, fetch):
        super().__init__()
        self.fetch = fetch

    def first(self, key, after):
        self[key] = self.fetch(key, after)
        return self[key]


def _local_step(x, tgt, mod, p, fetch, F, scatter=None):
    S, D = x.shape
    GW, HW = p["ln_g"].shape[1], p["hg_ng"].shape[1]
    G, T, _ = p["ws"].shape
    w = _Fetched(fetch)
    INW = 2 * GW + 4 * HW + 2 * D
    in_loc, br_loc, fi_loc = INW // N_DEV, D // N_DEV, 2 * F // N_DEV
    assert GW == HW and F % fi_loc == 0
    sh1, sc1, gt1, sh2, sc2, gt2 = (mod[:, k * D:(k + 1) * D] for k in range(6))
    bsb = jnp.broadcast_to(p["bs"][:, :, None], (G, T, GW // G))

    tm = _tile(S, 1024, 16)
    tmh = _tile(S, 512, 16)
    tn_in = _tile(in_loc, 1280)
    tn_d = _tile(D, 512)
    tn_br = _tile(br_loc, 512)
    tk_s = S
    tm_w = _tile(D, 1024)
    g_off = 2 * GW + 4 * HW

    h1 = _norm_mod("norm1", x, p["norm1_g"], sc1, sh1)
    z = _mm_nn_stacked("proj_in", h1, w.first("in", h1), tm=tm, tn=tn_in, tk=D)[0]
    ya = _gmlp_fwd(z, p["ln_g"], p["ln_b"], p["ws"], bsb, GW)
    yb, o_hg, states = _hg_fwd(z, p["hg_lb"], p["hg_ng"], HW)
    flat = {k: w.first(k, yb) for k in ("bg", "bh")}
    tn_f = _tile(D, 1024)
    pa = _matmul(
        "branch_gmlp", ya, flat["bg"], dims=_NN, grid_mnk=(S // tm, D // tn_f, 1), tiles=(tm, tn_f),
        a_spec=pl.BlockSpec((tm, GW), lambda i, j, k: (i, 0)), b_spec=pl.BlockSpec((GW, tn_f), lambda i, j, k: (0, j)),
        out_shapes=[jax.ShapeDtypeStruct((S, D), F32)], out_specs=[pl.BlockSpec((tm, tn_f), lambda i, j, k: (i, j))], epilogue=_store(F32))[0]
    t_fi = w.first("fi_early", pa)

    def gates(ga_ref, gb_ref, ba_ref, bb_ref):
        return _sigmoid(ga_ref[...] + ba_ref[...]), _sigmoid(gb_ref[...] + bb_ref[...])

    def gate_specs(tn_, tm_=tm):
        o1, o2 = g_off // tn_, (g_off + D) // tn_
        return [pl.BlockSpec((tm_, tn_), lambda i, j, k: (i, o1 + j)), pl.BlockSpec((tm_, tn_), lambda i, j, k: (i, o2 + j)),
                pl.BlockSpec((1, tn_), lambda i, j, k: (0, j)), pl.BlockSpec((1, tn_), lambda i, j, k: (0, D // tn_ + j))]

    def merge_ep(acc, ex, outs):
        ga, gb = gates(*ex[1:5])
        outs[0][...] = acc.astype(BF16)
        outs[1][...] = (ga * ex[0][...] + gb * acc).astype(BF16)

    tile_o = pl.BlockSpec((tmh, tn_f), lambda i, j, k: (i, j))
    pb, y = _matmul(
        "branch_hg_merge", yb, flat["bh"], dims=_NN, grid_mnk=(S // tmh, D // tn_f, 1), tiles=(tmh, tn_f),
        a_spec=pl.BlockSpec((tmh, HW), lambda i, j, k: (i, 0)), b_spec=pl.BlockSpec((HW, tn_f), lambda i, j, k: (0, j)),
        extras=[pa, z, z, p["b_gate"], p["b_gate"]], extra_specs=[tile_o, *gate_specs(tn_f, tmh)],
        out_shapes=[jax.ShapeDtypeStruct((S, D), BF16), jax.ShapeDtypeStruct((S, D), BF16)], out_specs=[tile_o, tile_o],
        epilogue=merge_ep, after=t_fi)

    def resid_ep(acc, ex, outs):
        outs[0][...] = acc.astype(BF16)
        outs[1][...] = ex[0][...] + ex[1][...] * acc

    def resid_mm(name, a, b, res, gt, tm_):
        K = a.shape[1]
        t_o = pl.BlockSpec((tm_, tn_d), lambda i, j, k: (i, j))
        return _matmul(
            name, a, b, dims=_NN, grid_mnk=(S // tm_, D // tn_d, 1), tiles=(tm_, tn_d),
            a_spec=pl.BlockSpec((tm_, K), lambda i, j, k: (i, 0)), b_spec=pl.BlockSpec((K, tn_d), lambda i, j, k: (0, j)),
            extras=[res, gt], extra_specs=[t_o, pl.BlockSpec((1, tn_d), lambda i, j, k: (0, j))],
            out_shapes=[jax.ShapeDtypeStruct((S, D), BF16), jax.ShapeDtypeStruct((S, D), F32)], out_specs=[t_o, t_o], epilogue=resid_ep)

    o1, xm = resid_mm("proj_out", y, w.first("out", z), x, gt1, tm)
    h2 = _norm_mod("norm2", xm, p["norm2_g"], sc2, sh2)
    hf, hf_fac = _ffn_in_swiglu(h2, w.first("fi", h2))
    o2, x3 = resid_mm("ffn_out", hf, w.first("fo", hf), xm, gt2, tmh)
    dx3, do2, vec_l = _loss_head(x3, tgt, p["final_g"], o2, gt2)

    nf = F // fi_loc

    def dswiglu_ep(acc, ex, outs):
        outs[0][0] = (acc * ex[0][0].astype(F32)).astype(BF16)
        outs[0][1] = (acc * ex[0][1].astype(F32)).astype(BF16)

    pair = pl.BlockSpec((2, tmh, fi_loc), lambda i, j, k: (0, i, j))
    dab = _matmul(
        "ffn_out_dx", do2, w["fo"], dims=_NT, grid_mnk=(S // tmh, nf, 1), tiles=(tmh, fi_loc),
        a_spec=pl.BlockSpec((tmh, D), lambda i, j, k: (i, 0)), b_spec=pl.BlockSpec((fi_loc, D), lambda i, j, k: (j, 0)),
        extras=[hf_fac], extra_specs=[pair], out_shapes=[jax.ShapeDtypeStruct((2, S, F), BF16)], out_specs=[pair],
        epilogue=dswiglu_ep)[0]
    start = (lambda name, grads: scatter[0](name, grads)) if scatter is not None else (lambda name, grads: None)
    push = (lambda name, after: scatter[1](name, after)) if scatter is not None else (lambda name, after: None)

    def zero(token):
        return 0.0 if token is None else token[0:1, 0:1]

    tm_f = _tile(F, 512)
    g_fo = _mm_tn("ffn_out_dw", hf, do2, pl.BlockSpec((tk_s, D), lambda i, j, k: (k, j)), Mo=F, No=D, S=S, tm=tm_f, tn=D, tk=tk_s)
    g_fi = _mm_tn("ffn_in_dw", h2, dab, pl.BlockSpec((None, tk_s, fi_loc), lambda i, j, k: (j // nf, k, j % nf)),
                  Mo=D, No=2 * F, S=S, tm=tm_w, tn=fi_loc, tk=tk_s, stacked_nloc=fi_loc, after=g_fo)
    t_ffn = start("scatter_ffn", dict(fo=g_fo, fi=g_fi))
    dh2 = _mm_nt_stacked("ffn_in_dx", pl.BlockSpec((None, tm, fi_loc), lambda i, j, k: (k // nf, i, k % nf)), dab, w["fi"],
                         M=S, tm=tm, tn=tm_w, tk=fi_loc, after=t_ffn)
    dxm, vec2, do1 = _norm_mod_bwd("norm2_bwd", dh2, xm, p["norm2_g"], sc2, dx3, o1, gt1)
    t_ffn = push("scatter_ffn", dxm)

    def dmerge_ep(acc, ex, outs):
        ga, gb = gates(*ex[2:6])
        outs[0][...] = (acc * ga).astype(BF16)
        outs[1][...] = (acc * gb).astype(BF16)
        outs[2][0] = (acc * ex[0][...] * ga * (1.0 - ga)).astype(BF16)
        outs[2][1] = (acc * ex[1][...] * gb * (1.0 - gb)).astype(BF16)

    t_o = pl.BlockSpec((tm, tn_d), lambda i, j, k: (i, j))
    dpa, dpb, dg2 = _matmul(
        "proj_out_dx", do1, w["out"], dims=_NT, grid_mnk=(S // tm, D // tn_d, 1), tiles=(tm, tn_d),
        a_spec=pl.BlockSpec((tm, D), lambda i, j, k: (i, 0)), b_spec=pl.BlockSpec((tn_d, D), lambda i, j, k: (j, 0)),
        extras=[pa, pb, z, z, p["b_gate"], p["b_gate"]], extra_specs=[t_o, t_o, *gate_specs(tn_d)],
        out_shapes=[jax.ShapeDtypeStruct((S, D), BF16), jax.ShapeDtypeStruct((S, D), BF16), jax.ShapeDtypeStruct((2, S, D), BF16)],
        out_specs=[t_o, t_o, pl.BlockSpec((2, tm, tn_d), lambda i, j, k: (0, i, j))], epilogue=dmerge_ep, after=t_ffn)
    g_out = _mm_tn("proj_out_dw", y, do1, pl.BlockSpec((tk_s, D), lambda i, j, k: (k, j)), Mo=D, No=D, S=S, tm=tn_d, tn=D, tk=tk_s)
    tn_g = _tile(GW, 512)
    b_br = pl.BlockSpec((tk_s, br_loc), lambda i, j, k: (k, j))
    g_bg = _mm_tn("branch_gmlp_dw", ya, dpa, b_br, Mo=GW, No=D, S=S, tm=tn_g, tn=br_loc, tk=tk_s, stacked_nloc=br_loc)
    g_bh = _mm_tn("branch_hg_dw", yb, dpb, b_br, Mo=HW, No=D, S=S, tm=tn_g, tn=br_loc, tk=tk_s, stacked_nloc=br_loc)
    t_mix = start("scatter_mixer", dict(out=g_out, bg=g_bg, bh=g_bh))
    def branch_dx(name, dp, w_flat):
        return _matmul(
            name, dp, w_flat, dims=_NT, grid_mnk=(S // tm, GW // tn_g, 1), tiles=(tm, tn_g),
            a_spec=pl.BlockSpec((tm, D), lambda i, j, k: (i, 0)), b_spec=pl.BlockSpec((tn_g, D), lambda i, j, k: (j, 0)),
            out_shapes=[jax.ShapeDtypeStruct((S, GW), F32)], out_specs=[pl.BlockSpec((tm, tn_g), lambda i, j, k: (i, j))],
            epilogue=_store(F32), after=t_mix)[0]

    dya = branch_dx("branch_gmlp_dx", dpa, flat["bg"])
    dyb = branch_dx("branch_hg_dx", dpb, flat["bh"])
    dz_gmlp, dln, dws, dbs = _gmlp_bwd(z, dya, p["ln_g"], p["ln_b"], p["ws"], bsb, GW)
    t_mix = push("scatter_mixer", dz_gmlp)
    dz, dng, dhlb, db_gate = _hg_bwd(z, o_hg, states, dyb, p["hg_lb"], p["hg_ng"] + zero(t_mix), HW, dz_gmlp, dg2)
    half = D // 2
    tm_h = _tile(half, 1024)
    g_in = []
    t_in = None
    for hname, h in (("a", 0), ("b", 1)):
        g_in.append(_mm_tn("proj_in_dw_" + hname, h1, dz, pl.BlockSpec((tk_s, in_loc), lambda i, j, k: (k, j)), Mo=half, No=INW, S=S,
                           tm=tm_h, tn=in_loc, tk=tk_s, stacked_nloc=in_loc, after=t_in, a_off=h * (half // tm_h)))
        t_in = start("scatter_proj_in_" + hname, {"w_in_" + hname: g_in[-1]})
    t_in = push("scatter_proj_in_a", t_in)
    dh1 = _mm_nt_stacked("proj_in_dx", pl.BlockSpec((tm, in_loc), lambda i, j, k: (i, k)), dz, w["in"], M=S, tm=tm, tn=tm_w, tk=in_loc,
                         after=t_in)
    dx, vec1 = _norm_mod_bwd("norm1_bwd", dh1, x, p["norm1_g"], sc1, dxm)

    dmod = jnp.concatenate([vec1[0:1], vec1[1:2], vec2[3:4], vec2[0:1], vec2[1:2], vec_l[2:3]], axis=1)
    small = dict(norm1_g=vec1[2:3], b_gate=db_gate.reshape(1, 2 * D), ln_g=dln[0:1], ln_b=dln[1:2], ws=dws, bs=dbs.reshape(G, T),
                 hg_lb=dhlb, hg_ng=dng[0:1], norm2_g=vec2[2:3], final_g=vec_l[1:2], loss=vec_l[0:1, 0:LANES])
    big = dict(w_in_a=g_in[0], w_in_b=g_in[1], bg=g_bg, bh=g_bh, out=g_out, fi=g_fi, fo=g_fo)
    return dx, big, small, dmod


_SMALL = ("b_ada", "norm1_g", "b_gate", "ln_g", "ln_b", "ws", "bs", "hg_lb", "hg_ng", "norm2_g", "final_g")


def _pack(parts, rows_mult=8):
    flat = [a.reshape(-1) for a in parts]
    offs, n = [], 0
    for a in flat:
        offs.append(n)
        n += a.shape[0]
    pad = (-n) % (LANES * rows_mult)
    if pad:
        flat.append(jnp.zeros((pad,), F32))
    return jnp.concatenate(flat).reshape(-1, LANES), offs


def kernel(x, c, w_ada, b_ada, norm1_g, w_in, b_gate, gmlp_ln_g, gmlp_ln_b, gmlp_ws, gmlp_bs, hg_lb, hg_norm_g, w_branch_gmlp, w_branch_hg, w_out, norm2_g, w_ffn_in, w_ffn_out, final_norm_g, loss_target, m_w_ada, m_b_ada, m_norm1_g, m_w_in, m_b_gate, m_gmlp_ln_g, m_gmlp_ln_b, m_gmlp_ws, m_gmlp_bs, m_hg_lb, m_hg_norm_g, m_w_branch_gmlp, m_w_branch_hg, m_w_out, m_norm2_g, m_w_ffn_in, m_w_ffn_out, m_final_norm_g, v_w_ada, v_b_ada, v_norm1_g, v_w_in, v_b_gate, v_gmlp_ln_g, v_gmlp_ln_b, v_gmlp_ws, v_gmlp_bs, v_hg_lb, v_hg_norm_g, v_w_branch_gmlp, v_w_branch_hg, v_w_out, v_norm2_g, v_w_ffn_in, v_w_ffn_out, v_final_norm_g):
    S, D = x.shape[1], x.shape[2]
    ada_loc = w_ada.shape[2]
    me = 4 * lax.axis_index("x") + 2 * lax.axis_index("y") + lax.axis_index("c")
    me_idx = me.astype(jnp.int32).reshape(1)

    def empty_hbm(shape, dtype):
        return pltpu.with_memory_space_constraint(lax.empty(shape, dtype), pltpu.HBM)

    groups = dict(gather_in=dict(keys=["in"], src=[w_in], forward=True),
                  gather_mixer=dict(keys=["bg", "bh", "out"], src=[w_branch_gmlp, w_branch_hg, w_out], forward=False),
                  gather_ffn_in=dict(keys=["fi"], src=[w_ffn_in], forward=True),
                  gather_ffn_out=dict(keys=["fo"], src=[w_ffn_out], forward=False))
    group_of = {k: gname for gname, g in groups.items() for k in g["keys"]}

    def first_hop(gname, after):
        g = groups[gname]
        n = len(g["keys"])
        cast = [_cast_shard(f"{gname}_cast_{k}", a[0], me_idx, natural=k in ("bg", "bh")) for k, a in zip(g["keys"], g["src"])]
        shards, outs = [s for s, _ in cast], [o for _, o in cast]
        if g["forward"]:
            *g["hop"], token = _split_start(gname + "_hop1", shards + outs, n * 3, _forward_first_copies(n), after=after)
        else:
            *g["hop"], token = _split_start(gname + "_hop1", shards + outs, n * N_CHIP, _gather_first_copies(n), after=after)
        return token

    def second_hop(gname, after):
        g = groups[gname]
        n = len(g["keys"])
        *g["hop"], token = _split_relay(gname + "_hop2", g["hop"][2], g["hop"][0], g["hop"][1], after,
                                        _forward_first_copies(n), n * 4, _forward_second_copies(n))
        return token

    def finish(gname, after):
        g = groups[gname]
        n = len(g["keys"])
        send_sems, recv_sems, bufs = g["hop"]
        if g["forward"]:
            send_sems, recv_sems, bufs, _ = _split_relay(gname + "_hop3", bufs, send_sems, recv_sems, after,
                                                         _forward_second_copies(n), n, _forward_third_copies(n))
            bufs = _split_wait(gname + "_wait", bufs, send_sems, recv_sems, after, _forward_third_copies(n))
        else:
            send_sems, recv_sems, bufs, _ = _split_relay(gname + "_relay", bufs, send_sems, recv_sems, after,
                                                         _gather_first_copies(n), n * (N_CHIP - 1), _gather_relay_copies(n))
            bufs = _split_wait(gname + "_wait", bufs, send_sems, recv_sems, after, _gather_relay_copies(n))
        g["done"] = dict(zip(g["keys"], bufs[n:]))

    c_all = _allgather_small("gather_c", c.reshape(D // LANES, LANES)).reshape(N_DEV, D)
    token = first_hop("gather_in", c_all)
    mod_cols, c_act = _ada_mod(jnp.pad(c_all, ((0, 16 - N_DEV), (0, 0))) + token[0:1, 0:1], w_ada[0])
    mod_vec = mod_cols[:N_DEV].reshape(-1, LANES)
    mg_send, mg_recv, mg_bufs, token = _split_start(
        "gather_mod_start", [mod_vec, lax.dynamic_update_slice(lax.empty((N_DEV, *mod_vec.shape), F32), mod_vec[None], (me, 0, 0))],
        N_DEV - 1, _small_gather_copies)
    token = second_hop("gather_in", token)
    token = first_hop("gather_ffn_in", first_hop("gather_mixer", token))
    mod_all = _split_wait("gather_mod_wait", mg_bufs, mg_send, mg_recv, token, _small_gather_copies)[1].reshape(N_DEV, N_DEV, ada_loc)
    mod = lax.dynamic_index_in_dim(mod_all, me, axis=1, keepdims=False).reshape(1, N_DEV * ada_loc) + b_ada

    def fetch(key, after):
        if key == "in":
            finish("gather_in", after)
        elif key == "fi_early":
            return first_hop("gather_ffn_out", second_hop("gather_ffn_in", after))
        elif "done" not in groups[group_of[key]]:
            finish(group_of[key], after)
        arr = groups[group_of[key]]["done"][key]
        return arr.reshape(-1, D) if key in ("out", "fo") else arr

    p = dict(norm1_g=norm1_g, b_gate=b_gate, ln_g=gmlp_ln_g, ln_b=gmlp_ln_b, ws=gmlp_ws[0], bs=gmlp_bs[0], hg_lb=hg_lb,
             hg_ng=hg_norm_g, norm2_g=norm2_g, final_g=final_norm_g.reshape(1, D))

    in_flight = {}
    c_idx = lax.axis_index("c").astype(jnp.int32).reshape(1)
    chip_idx = (2 * lax.axis_index("x") + lax.axis_index("y")).astype(jnp.int32).reshape(1)

    def scatter_start(name, grads):
        keys = list(grads)
        n = len(keys)
        stacks = [grads[k].reshape(N_DEV, -1, grads[k].shape[-1]) for k in keys]
        lands = [empty_hbm((N_CHIP, *g.shape[1:]), g.dtype) for g in stacks]
        send_sems, recv_sems, bufs, token = _split_start(name + "_d2d", stacks + lands, n * N_CHIP, _to_sibling_copies(n))
        in_flight[name] = dict(keys=keys, stage1=(send_sems, recv_sems, bufs))
        return token

    def scatter_push(name, after):
        f = in_flight[name]
        n = len(f["keys"])
        send_sems, recv_sems, bufs = f["stage1"]
        bufs = _split_wait(name + "_d2d_wait", bufs, send_sems, recv_sems, after, _to_sibling_copies(n))
        sums = [_chip_sum(f"{name}_sum_{k}", bufs[i], bufs[n + i], c_idx) for i, k in enumerate(f["keys"])]
        lands = [empty_hbm((N_CHIP - 1, *s.shape[1:]), s.dtype) for s in sums]
        send_sems, recv_sems, bufs, token = _split_start(name + "_ici", sums + lands, n * (N_CHIP - 1), _to_owner_copies(n))
        f["stage2"] = (send_sems, recv_sems, bufs)
        return token

    grad_x, _, small, dmod = _local_step(x[0], loss_target[0], mod, p, fetch, w_ffn_out.shape[1] * N_DEV, (scatter_start, scatter_push))

    small["b_ada"] = dmod
    packed, offs = _pack([small[k] for k in _SMALL] + [small["loss"]])
    sg_send, sg_recv, sg_bufs, t_tail = _split_start(
        "gather_small_start", [packed, lax.dynamic_update_slice(lax.empty((N_DEV, *packed.shape), F32), packed[None], (me, 0, 0))],
        N_DEV - 1, _small_gather_copies)
    t_tail = scatter_push("scatter_proj_in_b", t_tail)
    big_w = dict(w_in=(w_in, m_w_in, v_w_in, "w_in"), bg=(w_branch_gmlp, m_w_branch_gmlp, v_w_branch_gmlp, "w_branch_gmlp"),
                 bh=(w_branch_hg, m_w_branch_hg, v_w_branch_hg, "w_branch_hg"), out=(w_out, m_w_out, v_w_out, "w_out"),
                 fi=(w_ffn_in, m_w_ffn_in, v_w_ffn_in, "w_ffn_in"), fo=(w_ffn_out, m_w_ffn_out, v_w_ffn_out, "w_ffn_out"))
    upd = {}

    def land_and_update(name, after):
        keys = in_flight[name]["keys"]
        n = len(keys)
        send_sems, recv_sems, bufs = in_flight[name]["stage2"]
        bufs = _split_wait(name + "_ici_wait", bufs, send_sems, recv_sems, after, _to_owner_copies(n))
        for i, k in enumerate(keys):
            if k in big_w:
                wt, mt, vt, out_name = big_w[k]
                upd[out_name] = _adamw("adamw_" + out_name, wt[0], mt[0], vt[0], bufs[i], chip_idx, [bufs[n + i]])
            else:
                wt, mt, vt, out_name = big_w["w_in"]
                upd[out_name] = _adamw("adamw_" + k, wt[0], mt[0], vt[0], bufs[i], chip_idx, [bufs[n + i]],
                                       row0=0 if k == "w_in_a" else bufs[i].shape[1], into=upd.get(out_name))
            after = upd[out_name][1]
        return after

    after = land_and_update("scatter_mixer", land_and_update("scatter_ffn", t_tail))
    gathered = _split_wait("gather_small_wait", sg_bufs, sg_send, sg_recv, after, _small_gather_copies)[1]
    wp = dict(p, b_ada=b_ada)
    ms = dict(b_ada=m_b_ada, norm1_g=m_norm1_g, b_gate=m_b_gate, ln_g=m_gmlp_ln_g, ln_b=m_gmlp_ln_b, ws=m_gmlp_ws, bs=m_gmlp_bs,
              hg_lb=m_hg_lb, hg_ng=m_hg_norm_g, norm2_g=m_norm2_g, final_g=m_final_norm_g)
    vs = dict(b_ada=v_b_ada, norm1_g=v_norm1_g, b_gate=v_b_gate, ln_g=v_gmlp_ln_g, ln_b=v_gmlp_ln_b, ws=v_gmlp_ws, bs=v_gmlp_bs,
              hg_lb=v_hg_lb, hg_ng=v_hg_norm_g, norm2_g=v_norm2_g, final_g=v_final_norm_g)
    w_sm, _ = _pack([wp[k] for k in _SMALL])
    m_sm, _ = _pack([ms[k] for k in _SMALL])
    v_sm, _ = _pack([vs[k] for k in _SMALL])
    shapes = dict(b_ada=b_ada.shape, norm1_g=norm1_g.shape, b_gate=b_gate.shape, ln_g=gmlp_ln_g.shape, ln_b=gmlp_ln_b.shape,
                  ws=gmlp_ws.shape, bs=gmlp_bs.shape, hg_lb=hg_lb.shape, hg_ng=hg_norm_g.shape, norm2_g=norm2_g.shape,
                  final_g=final_norm_g.shape)
    sm_out = _small_update(gathered, w_sm, m_sm, v_sm, after, [math.prod(shapes[k]) // LANES for k in _SMALL])

    def unpack(idx, k):
        return sm_out[idx * len(_SMALL) + _SMALL.index(k)].reshape(shapes[k])

    loss = sm_out[-1][0, 0]

    assert offs[0] == 0 and ada_loc % LANES == 0
    dmod_loc = lax.dynamic_slice_in_dim(gathered, me * (ada_loc // LANES), ada_loc // LANES, axis=1).reshape(N_DEV, ada_loc)
    ca_t = jnp.pad(c_act[:N_DEV].T, ((0, 0), (0, LANES - N_DEV))).astype(BF16)
    dm_p = jnp.pad(dmod_loc, ((0, LANES - N_DEV), (0, 0))).astype(BF16)
    tm_a = _tile(D, 512)
    g_ada = _matmul(
        "ada_dw", ca_t, dm_p, dims=_NN, grid_mnk=(D // tm_a, 1, 1), tiles=(tm_a, ada_loc),
        a_spec=pl.BlockSpec((tm_a, LANES), lambda i, j, k: (i, 0)), b_spec=pl.BlockSpec((LANES, ada_loc), lambda i, j, k: (0, 0)),
        out_shapes=[jax.ShapeDtypeStruct((1, D, ada_loc), F32)], out_specs=[pl.BlockSpec((None, tm_a, ada_loc), lambda i, j, k: (0, i, 0))],
        epilogue=_store(F32))[0]
    upd["w_ada"] = _adamw("adamw_w_ada", w_ada[0], m_w_ada[0], v_w_ada[0], g_ada, jnp.zeros((1,), jnp.int32))
    land_and_update("scatter_proj_in_b", land_and_update("scatter_proj_in_a", upd["w_ada"][1]))

    order = ("w_ada", "b_ada", "norm1_g", "w_in", "b_gate", "ln_g", "ln_b", "ws", "bs", "hg_lb", "hg_ng", "w_branch_gmlp", "w_branch_hg",
             "w_out", "norm2_g", "w_ffn_in", "w_ffn_out", "final_g")
    outs = [loss, grad_x[None]]
    for idx in range(4):
        for k in order:
            outs.append(upd[k][idx][None] if k in upd else unpack(idx, k))
    return tuple(outs)
```

```python
import functools
import math

import jax
import jax.numpy as jnp
from jax import lax
from jax.experimental import pallas as pl
from jax.experimental.pallas import tpu as pltpu

F32 = jnp.float32
BF16 = jnp.bfloat16
N_DEV = 8
EPS = 1e-6
LANES = 128
HG_DK = 128
HG_CHUNK = 64
HG_MID = HG_CHUNK // 2 - 1
EXP_CLAMP = 80.0
VMEM_LIMIT = 48 * 1024 * 1024
BF16_ROWS = 16
STREAM_TILE = 1 << 20
ADAMW_TILE = 3 << 17
ADAM_LR, ADAM_B1, ADAM_B2, ADAM_EPS, ADAM_WD, ADAM_STEP = 0.001, 0.9, 0.999, 1e-08, 0.01, 10
MESH = pl.DeviceIdType.MESH

_NN = (((1,), (0,)), ((), ()))
_NT = (((1,), (1,)), ((), ()))
_TN = (((0,), (0,)), ((), ()))


def _dot(a, b, dims=_NN):
    return lax.dot_general(a.astype(BF16), b.astype(BF16), dims, preferred_element_type=F32)


def _tile(n, target, mult=LANES):
    best = None
    for t in range(mult, min(n, target) + 1, mult):
        if n % t == 0:
            best = t
    return n if best is None else best


def _cparams(sem):
    return pltpu.CompilerParams(dimension_semantics=sem, vmem_limit_bytes=VMEM_LIMIT)


def _sigmoid(x):
    return 1.0 / (1.0 + jnp.exp(-x))


def _gelu_parts(x):
    k0 = math.sqrt(2.0 / math.pi)
    x2 = x * x
    t = jnp.tanh(k0 * (x + 0.044715 * x * x2))
    g = 0.5 * x * (1.0 + t)
    dg = 0.5 * (1.0 + t) + 0.5 * x * (1.0 - t * t) * (k0 * (1.0 + 3.0 * 0.044715 * x2))
    return g, dg


def _split3(x):
    h = x.astype(BF16)
    r = x - h.astype(F32)
    m = r.astype(BF16)
    lo = (r - m.astype(F32)).astype(BF16)
    return h, m, lo


def _ones_dot(mat01, x):
    h, m, lo = _split3(x)
    d = functools.partial(lax.dot_general, dimension_numbers=_NN, preferred_element_type=F32)
    return d(mat01, h) + d(mat01, m) + d(mat01, lo)


def _matmul(name, a, b, *, dims, grid_mnk, tiles, a_spec, b_spec, extras=(), extra_specs=(), out_shapes, out_specs, epilogue, after=None,
            sem=None):
    gm, gn, nk = grid_mnk
    tm, tn = tiles
    n_ex, n_out = len(extras), len(out_shapes)
    held = [] if after is None else [after]

    def body(*refs):
        a_ref, b_ref = refs[0], refs[1]
        ex = refs[2:2 + n_ex]
        outs = refs[2 + n_ex + len(held):2 + n_ex + len(held) + n_out]
        more = () if sem is None else (pl.program_id(0) == 0,)
        if nk == 1:
            epilogue(lax.dot_general(a_ref[...], b_ref[...], dims, preferred_element_type=F32), ex, outs, *more)
            return
        acc = refs[-1]
        k = pl.program_id(2)

        @pl.when(k == 0)
        def _():
            acc[...] = jnp.zeros_like(acc)

        acc[...] += lax.dot_general(a_ref[...], b_ref[...], dims, preferred_element_type=F32)

        @pl.when(k == nk - 1)
        def _():
            epilogue(acc[...], ex, outs, *more)

    return pl.pallas_call(
        body, name=name, grid=(gm, gn, nk), in_specs=[a_spec, b_spec, *extra_specs] + [pl.BlockSpec(memory_space=pl.ANY)] * len(held),
        out_specs=list(out_specs), out_shape=list(out_shapes), scratch_shapes=[] if nk == 1 else [pltpu.VMEM((tm, tn), F32)],
        compiler_params=_cparams(sem or ("parallel", "parallel", "arbitrary")),
    )(a, b, *extras, *held)


def _store(dtype):
    def ep(acc, ex, outs):
        outs[0][...] = acc.astype(dtype)
    return ep


def _mm_nn_stacked(name, a, wg, *, tm, tn, tk, out_dtype=F32, extras=(), extra_specs=(), out_shapes=None, out_specs=None, epilogue=None,
                   after=None):
    M, K = a.shape
    _, _, nloc = wg.shape
    N = nloc * N_DEV
    q = nloc // tn
    if out_shapes is None:
        out_shapes = [jax.ShapeDtypeStruct((M, N), out_dtype)]
        out_specs = [pl.BlockSpec((tm, tn), lambda i, j, k: (i, j))]
        epilogue = _store(out_dtype)
    return _matmul(
        name, a, wg, dims=_NN, grid_mnk=(M // tm, N // tn, K // tk), tiles=(tm, tn),
        a_spec=pl.BlockSpec((tm, tk), lambda i, j, k: (i, k)),
        b_spec=pl.BlockSpec((None, tk, tn), lambda i, j, k: (j // q, k, j % q)),
        extras=extras, extra_specs=extra_specs, out_shapes=out_shapes, out_specs=out_specs, epilogue=epilogue, after=after)


def _mm_nt_stacked(name, a_spec, a, wg, *, M, tm, tn, tk, out_dtype=F32, after=None, extras=(), extra_specs=(), out_shapes=None,
                   out_specs=None, epilogue=None, sem=None):
    _, Kw, nloc = wg.shape
    q = nloc // tk
    single = out_shapes is None
    if single:
        out_shapes = [jax.ShapeDtypeStruct((M, Kw), out_dtype)]
        out_specs = [pl.BlockSpec((tm, tn), lambda i, j, k: (i, j))]
        epilogue = _store(out_dtype)
    res = _matmul(
        name, a, wg, dims=_NT, grid_mnk=(M // tm, Kw // tn, (nloc * N_DEV) // tk), tiles=(tm, tn),
        a_spec=a_spec, b_spec=pl.BlockSpec((None, tn, tk), lambda i, j, k: (k // q, j, k % q)),
        extras=extras, extra_specs=extra_specs, out_shapes=out_shapes, out_specs=out_specs, epilogue=epilogue, after=after, sem=sem)
    return res[0] if single else res


def _mm_tn(name, a, b, b_spec, *, Mo, No, S, tm, tn, tk, stacked_nloc=None, after=None, a_off=0):
    if stacked_nloc is None:
        out_shape = jax.ShapeDtypeStruct((Mo, No), BF16)
        out_spec = pl.BlockSpec((tm, tn), lambda i, j, k: (i, j))
    else:
        q = stacked_nloc // tn
        out_shape = jax.ShapeDtypeStruct((N_DEV, Mo, stacked_nloc), BF16)
        out_spec = pl.BlockSpec((None, tm, tn), lambda i, j, k: (j // q, i, j % q))
    return _matmul(
        name, a, b, dims=_TN, grid_mnk=(Mo // tm, No // tn, S // tk), tiles=(tm, tn),
        a_spec=pl.BlockSpec((tk, tm), lambda i, j, k: (k, i + a_off)), b_spec=b_spec,
        out_shapes=[out_shape], out_specs=[out_spec], epilogue=_store(BF16), after=after)[0]


def _norm_mod(name, x, g, sc, sh):
    S, D = x.shape
    tm = _tile(S, 256, 8)

    def body(x_ref, g_ref, sc_ref, sh_ref, h_ref):
        xv = x_ref[...]
        r = lax.rsqrt(jnp.mean(xv * xv, axis=-1, keepdims=True) + EPS)
        h = (xv * r) * g_ref[...]
        h_ref[...] = (h * (1.0 + sc_ref[...]) + sh_ref[...]).astype(BF16)

    row = pl.BlockSpec((tm, D), lambda i: (i, 0))
    vec = pl.BlockSpec((1, D), lambda i: (0, 0))
    return pl.pallas_call(body, name=name, grid=(S // tm,), in_specs=[row, vec, vec, vec], out_specs=row,
                          out_shape=jax.ShapeDtypeStruct((S, D), BF16), compiler_params=_cparams(("parallel",)))(x, g, sc, sh)


def _norm_mod_bwd_rows(first, dh_v, x_ref, g_ref, sc_ref, dres_ref, dx_ref, vec_ref, o_ref=None, gt_ref=None, do_ref=None):
    @pl.when(first)
    def _():
        vec_ref[...] = jnp.zeros_like(vec_ref)

    xv, gv = x_ref[...], g_ref[...]
    r = lax.rsqrt(jnp.mean(xv * xv, axis=-1, keepdims=True) + EPS)
    xn = xv * r
    one_sc = 1.0 + sc_ref[...]
    vec_ref[0:1, :] += jnp.sum(dh_v, axis=0, keepdims=True)
    vec_ref[1:2, :] += jnp.sum(dh_v * (xn * gv), axis=0, keepdims=True)
    vec_ref[2:3, :] += jnp.sum(dh_v * one_sc * xn, axis=0, keepdims=True)
    dxn = dh_v * one_sc * gv
    dx = dres_ref[...] + r * (dxn - xn * jnp.mean(dxn * xn, axis=-1, keepdims=True))
    dx_ref[...] = dx
    if o_ref is not None:
        vec_ref[3:4, :] += jnp.sum(dx * o_ref[...], axis=0, keepdims=True)
        do_ref[...] = (dx * gt_ref[...]).astype(BF16)


def _norm_mod_bwd(name, dh, x, g, sc, dres, o=None, gt=None):
    S, D = x.shape
    tm = _tile(S, 256, 8)
    gated = o is not None

    def body(*refs):
        if gated:
            dh_ref, x_ref, g_ref, sc_ref, dres_ref, o_ref, gt_ref, dx_ref, vec_ref, do_ref = refs
        else:
            dh_ref, x_ref, g_ref, sc_ref, dres_ref, dx_ref, vec_ref = refs
            o_ref = gt_ref = do_ref = None
        _norm_mod_bwd_rows(pl.program_id(0) == 0, dh_ref[...], x_ref, g_ref, sc_ref, dres_ref, dx_ref, vec_ref, o_ref, gt_ref, do_ref)

    row = pl.BlockSpec((tm, D), lambda i: (i, 0))
    vec = pl.BlockSpec((1, D), lambda i: (0, 0))
    acc = pl.BlockSpec((8, D), lambda i: (0, 0))
    ins = [dh, x, g, sc, dres] + ([o, gt] if gated else [])
    in_specs = [row, row, vec, vec, row] + ([row, vec] if gated else [])
    out_shape = [jax.ShapeDtypeStruct((S, D), F32), jax.ShapeDtypeStruct((8, D), F32)]
    out_specs = [row, acc]
    if gated:
        out_shape.append(jax.ShapeDtypeStruct((S, D), BF16))
        out_specs.append(row)
    return pl.pallas_call(body, name=name, grid=(S // tm,), in_specs=in_specs, out_specs=out_specs, out_shape=out_shape,
                          compiler_params=_cparams(("arbitrary",)))(*ins)


def _loss_head(x3, tgt, gf, o2, gt2):
    S, D = x3.shape
    tm = _tile(S, 256, 8)

    def body(x_ref, t_ref, g_ref, o_ref, gt_ref, dx_ref, do_ref, vec_ref):
        i = pl.program_id(0)

        @pl.when(i == 0)
        def _():
            vec_ref[...] = jnp.zeros_like(vec_ref)

        xv, gv = x_ref[...], g_ref[...]
        r = lax.rsqrt(jnp.mean(xv * xv, axis=-1, keepdims=True) + EPS)
        xn = xv * r
        e = xn * gv - t_ref[...]
        tok = 0.5 * jnp.mean(e * e, axis=-1, keepdims=True)
        vec_ref[0:1, :] += jnp.broadcast_to(jnp.sum(tok, axis=0, keepdims=True), (1, D))
        dy = e * (1.0 / D)
        vec_ref[1:2, :] += jnp.sum(dy * xn, axis=0, keepdims=True)
        dxn = dy * gv
        dx = r * (dxn - xn * jnp.mean(dxn * xn, axis=-1, keepdims=True))
        dx_ref[...] = dx
        vec_ref[2:3, :] += jnp.sum(dx * o_ref[...], axis=0, keepdims=True)
        do_ref[...] = (dx * gt_ref[...]).astype(BF16)

    row = pl.BlockSpec((tm, D), lambda i: (i, 0))
    vec = pl.BlockSpec((1, D), lambda i: (0, 0))
    return pl.pallas_call(
        body, name="loss_head", grid=(S // tm,), in_specs=[row, row, vec, row, vec],
        out_specs=[row, row, pl.BlockSpec((8, D), lambda i: (0, 0))],
        out_shape=[jax.ShapeDtypeStruct((S, D), F32), jax.ShapeDtypeStruct((S, D), BF16), jax.ShapeDtypeStruct((8, D), F32)],
        compiler_params=_cparams(("arbitrary",)))(x3, tgt, gf, o2, gt2)


def _ffn_in_swiglu(h, wg):
    S, D = h.shape
    _, _, tf = wg.shape
    nf = N_DEV // 2
    F = nf * tf
    tm = _tile(S, 256, 16)

    def body(h_ref, wa_ref, wu_ref, hf_ref, fac_ref):
        hv = h_ref[...]
        a = lax.dot_general(hv, wa_ref[...], _NN, preferred_element_type=F32)
        up = lax.dot_general(hv, wu_ref[...], _NN, preferred_element_type=F32)
        sa = _sigmoid(a)
        silu = a * sa
        hf_ref[...] = (silu * up).astype(BF16)
        fac_ref[0] = (up * (sa * (1.0 + a * (1.0 - sa)))).astype(BF16)
        fac_ref[1] = silu.astype(BF16)

    return pl.pallas_call(
        body, name="ffn_in_swiglu", grid=(nf, S // tm),
        in_specs=[pl.BlockSpec((tm, D), lambda j, i: (i, 0)), pl.BlockSpec((None, D, tf), lambda j, i: (j, 0, 0)),
                  pl.BlockSpec((None, D, tf), lambda j, i: (j + nf, 0, 0))],
        out_specs=[pl.BlockSpec((tm, tf), lambda j, i: (i, j)), pl.BlockSpec((2, tm, tf), lambda j, i: (0, i, j))],
        out_shape=[jax.ShapeDtypeStruct((S, F), BF16), jax.ShapeDtypeStruct((2, S, F), BF16)],
        compiler_params=_cparams(("parallel", "parallel")))(h, wg, wg)


def _gmlp_common(u_ref, v_ref, lg_ref, lb_ref, ws_ref, bsb_ref, G, T, Dg):
    ug, dug = _gelu_parts(u_ref[...])
    vg, dvg = _gelu_parts(v_ref[...])
    mu = jnp.mean(vg, axis=-1, keepdims=True)
    vc = vg - mu
    rstd = lax.rsqrt(jnp.mean(vc * vc, axis=-1, keepdims=True) + EPS)
    vhat = vc * rstd
    vn = vhat * lg_ref[...] + lb_ref[...]
    row = lax.broadcasted_iota(jnp.int32, (T, T), 0)
    col = lax.broadcasted_iota(jnp.int32, (T, T), 1)
    tril = row >= col
    s = []
    for g in range(G):
        w = jnp.where(tril, ws_ref[g], 0.0)
        s.append(_dot(w, vn[:, g * Dg:(g + 1) * Dg]) + bsb_ref[g])
    return ug, dug, dvg, rstd, vhat, vn, tril, s


def _gmlp_fwd(z, ln_g, ln_b, ws, bsb, GW):
    S = z.shape[0]
    G, T, _ = ws.shape
    Dg = GW // G

    def body(u_ref, v_ref, lg_ref, lb_ref, ws_ref, bsb_ref, ya_ref):
        ug, _, _, _, _, _, _, s = _gmlp_common(u_ref, v_ref, lg_ref, lb_ref, ws_ref, bsb_ref, G, T, Dg)
        for g in range(G):
            sl = slice(g * Dg, (g + 1) * Dg)
            ya_ref[:, sl] = (ug[:, sl] * s[g]).astype(BF16)

    vec = pl.BlockSpec((1, GW), lambda c: (0, 0))
    return pl.pallas_call(
        body, name="gmlp_fwd", grid=(S // T,),
        in_specs=[pl.BlockSpec((T, GW), lambda c: (c, 0)), pl.BlockSpec((T, GW), lambda c: (c, 1)), vec, vec,
                  pl.BlockSpec((G, T, T), lambda c: (0, 0, 0)), pl.BlockSpec((G, T, Dg), lambda c: (0, 0, 0))],
        out_specs=pl.BlockSpec((T, GW), lambda c: (c, 0)), out_shape=jax.ShapeDtypeStruct((S, GW), BF16),
        compiler_params=_cparams(("parallel",)))(z, z, ln_g, ln_b, ws, bsb)


def _gmlp_bwd(z, dya, ln_g, ln_b, ws, bsb, GW):
    S = z.shape[0]
    G, T, _ = ws.shape
    Dg = GW // G
    nc = S // T

    def body(u_ref, v_ref, dya_ref, lg_ref, lb_ref, ws_ref, bsb_ref, dz_ref, dln_ref, dws_ref, dbs_ref, dbs_acc, dvh):
        c = pl.program_id(0)

        @pl.when(c == 0)
        def _():
            dln_ref[...] = jnp.zeros_like(dln_ref)
            dws_ref[...] = jnp.zeros_like(dws_ref)
            dbs_acc[...] = jnp.zeros_like(dbs_acc)

        ug, dug, dvg, rstd, vhat, vn, tril, s = _gmlp_common(u_ref, v_ref, lg_ref, lb_ref, ws_ref, bsb_ref, G, T, Dg)
        dya_v = dya_ref[...]
        for g in range(G):
            sl = slice(g * Dg, (g + 1) * Dg)
            dy_g = dya_v[:, sl]
            dz_ref[:, sl] = (dy_g * s[g] * dug[:, sl]).astype(BF16)
            ds = dy_g * ug[:, sl]
            dbs_acc[g] += ds
            w = jnp.where(tril, ws_ref[g], 0.0)
            dvn_g = _dot(w, ds, _TN)
            dws_ref[g] += jnp.where(tril, _dot(ds, vn[:, sl], _NT), 0.0)
            dln_ref[0:1, sl] += jnp.sum(dvn_g * vhat[:, sl], axis=0, keepdims=True)
            dln_ref[1:2, sl] += jnp.sum(dvn_g, axis=0, keepdims=True)
            dvh[:, sl] = dvn_g * lg_ref[:, sl]
        dvhat = dvh[...]
        m1 = jnp.mean(dvhat, axis=-1, keepdims=True)
        m2 = jnp.mean(dvhat * vhat, axis=-1, keepdims=True)
        dz_ref[:, GW:2 * GW] = (rstd * (dvhat - m1 - vhat * m2) * dvg).astype(BF16)

        @pl.when(c == nc - 1)
        def _():
            for g in range(G):
                dbs_ref[g] = jnp.sum(dbs_acc[g], axis=-1, keepdims=True)

    vec = pl.BlockSpec((1, GW), lambda c: (0, 0))
    return pl.pallas_call(
        body, name="gmlp_bwd", grid=(nc,),
        in_specs=[pl.BlockSpec((T, GW), lambda c: (c, 0)), pl.BlockSpec((T, GW), lambda c: (c, 1)),
                  pl.BlockSpec((T, GW), lambda c: (c, 0)), vec, vec,
                  pl.BlockSpec((G, T, T), lambda c: (0, 0, 0)), pl.BlockSpec((G, T, Dg), lambda c: (0, 0, 0))],
        out_specs=[pl.BlockSpec((T, 2 * GW), lambda c: (c, 0)), pl.BlockSpec((8, GW), lambda c: (0, 0)),
                   pl.BlockSpec((G, T, T), lambda c: (0, 0, 0)), pl.BlockSpec((G, T, 1), lambda c: (0, 0, 0))],
        out_shape=[jax.ShapeDtypeStruct((S, 2 * GW), BF16), jax.ShapeDtypeStruct((8, GW), F32),
                   jax.ShapeDtypeStruct((G, T, T), F32), jax.ShapeDtypeStruct((G, T, 1), F32)],
        scratch_shapes=[pltpu.VMEM((G, T, Dg), F32), pltpu.VMEM((T, GW), F32)],
        compiler_params=_cparams(("arbitrary",)))(z, z, dya, ln_g, ln_b, ws, bsb)


def _hg_common(q_ref, f_ref, hlb_ref):
    C = HG_CHUNK
    a = hlb_ref[...]
    lb = _sigmoid(a[0:1, :] - a[1:2, :])
    sig = _sigmoid(f_ref[...])
    f = lb + (1.0 - lb) * sig
    lf = jnp.log(f)
    kk = 1.0 - f
    q = q_ref[...]
    sq = _sigmoid(q)
    qa = q * sq
    row = lax.broadcasted_iota(jnp.int32, (C, C), 0)
    col = lax.broadcasted_iota(jnp.int32, (C, C), 1)
    tril = row >= col
    b = _ones_dot(tril.astype(BF16), lf)
    bm = b[HG_MID:HG_MID + 1, :]
    bl = b[C - 1:C, :]
    e_b = jnp.exp(b)
    e_qm = jnp.exp(jnp.minimum(b - bm, EXP_CLAMP))
    e_km = jnp.exp(jnp.minimum(bm - b, EXP_CLAMP))
    e_kl = jnp.exp(bl - b)
    return dict(lb=lb, sig=sig, f=f, kk=kk, q=q, sq=sq, qa=qa, tril=tril, e_b=e_b, e_qm=e_qm, e_km=e_km, e_kl=e_kl,
                e_l=jnp.exp(bl), qh=qa * e_b, qt=qa * e_qm, kt=kk * e_km, kh=kk * e_kl)


def _hg_fwd(z, hg_lb, ng, HW):
    S = z.shape[0]
    C, H, dk = HG_CHUNK, HW // HG_DK, HG_DK
    nc = S // C

    def body(q_ref, f_ref, i_ref, og_ref, hlb_ref, ng_ref, yb_ref, o_ref, st_ref, state):
        @pl.when(pl.program_id(0) == 0)
        def _():
            state[...] = jnp.zeros_like(state)

        t = _hg_common(q_ref, f_ref, hlb_ref)
        iv = i_ref[...]
        for h in range(H):
            sl = slice(h * dk, (h + 1) * dk)
            st = state[h]
            st_ref[h] = st
            a = jnp.where(t["tril"], _dot(t["qt"][:, sl], t["kt"][:, sl], _NT), 0.0)
            o_h = _dot(a, iv[:, sl]) + _dot(t["qh"][:, sl], st, _NT)
            state[h] = st * t["e_l"][:, sl] + _dot(iv[:, sl], t["kh"][:, sl], _TN)
            o_ref[:, sl] = o_h
            rr = lax.rsqrt(jnp.mean(o_h * o_h, axis=-1, keepdims=True) + EPS)
            og = og_ref[:, sl]
            yb_ref[:, sl] = (o_h * rr * ng_ref[:, sl] * (og * _sigmoid(og))).astype(BF16)

    def col(k):
        return pl.BlockSpec((C, HW), lambda c: (c, k))

    base = 2
    return pl.pallas_call(
        body, name="hgrn_fwd", grid=(nc,),
        in_specs=[col(base), col(base + 1), col(base + 2), col(base + 3),
                  pl.BlockSpec((2, HW), lambda c: (0, 0)), pl.BlockSpec((1, HW), lambda c: (0, 0))],
        out_specs=[pl.BlockSpec((C, HW), lambda c: (c, 0)), pl.BlockSpec((C, HW), lambda c: (c, 0)),
                   pl.BlockSpec((None, H, dk, dk), lambda c: (c, 0, 0, 0))],
        out_shape=[jax.ShapeDtypeStruct((S, HW), BF16), jax.ShapeDtypeStruct((S, HW), F32),
                   jax.ShapeDtypeStruct((nc, H, dk, dk), F32)],
        scratch_shapes=[pltpu.VMEM((H, dk, dk), F32)],
        compiler_params=_cparams(("arbitrary",)))(z, z, z, z, hg_lb, ng)


def _hg_bwd(z, o, states, dyb, hg_lb, ng, HW, dz_head, dz_tail):
    S = z.shape[0]
    C, H, dk = HG_CHUNK, HW // HG_DK, HG_DK
    nc = S // C
    B0 = dz_head.shape[1]
    DT = dz_tail.shape[2]
    INW = B0 + 4 * HW + 2 * DT

    def body(q_ref, f_ref, i_ref, og_ref, o_ref, st_ref, stn_ref, dyb_ref, hlb_ref, ng_ref, head_ref, tail_ref,
             dzf_ref, dng_ref, dhlb_ref, dtail_ref, dstate, cross, dqa_buf, dkk_buf, db_buf, dlb_acc):
        c = pl.program_id(0)
        dzf_ref[:, 0:B0] = head_ref[...]
        dzf_ref[:, B0 + 4 * HW:B0 + 4 * HW + DT] = tail_ref[0]
        dzf_ref[:, B0 + 4 * HW + DT:INW] = tail_ref[1]
        dz_ref = dzf_ref.at[:, B0:B0 + 4 * HW]

        @pl.when(c == 0)
        def _():
            dtail_ref[...] = jnp.zeros_like(dtail_ref)

        dtail_ref[0:1, :] += jnp.sum(tail_ref[0].astype(F32), axis=0, keepdims=True)
        dtail_ref[1:2, :] += jnp.sum(tail_ref[1].astype(F32), axis=0, keepdims=True)

        @pl.when(c == 0)
        def _():
            dstate[...] = jnp.zeros_like(dstate)
            dlb_acc[...] = jnp.zeros_like(dlb_acc)
            dng_ref[...] = jnp.zeros_like(dng_ref)

        def r16(v):
            return v.astype(BF16).astype(F32)

        t = _hg_common(q_ref, f_ref, hlb_ref)
        iv = i_ref[...]
        for h in range(H):
            sl = slice(h * dk, (h + 1) * dk)
            o_h, og, dyb_h, ng_h = o_ref[:, sl], og_ref[:, sl], dyb_ref[:, sl], ng_ref[:, sl]
            sg = _sigmoid(og)
            silu_og = og * sg
            rr = lax.rsqrt(jnp.mean(o_h * o_h, axis=-1, keepdims=True) + EPS)
            on = o_h * rr
            dng_ref[0:1, sl] += jnp.sum(dyb_h * on * silu_og, axis=0, keepdims=True)
            dz_ref[:, 3 * HW + h * dk:3 * HW + (h + 1) * dk] = (dyb_h * on * ng_h * (sg * (1.0 + og * (1.0 - sg)))).astype(BF16)
            don = dyb_h * ng_h * silu_og
            do_h = rr * (don - on * jnp.mean(don * on, axis=-1, keepdims=True))

            qt, kt, qh, kh, iv_h = t["qt"][:, sl], t["kt"][:, sl], t["qh"][:, sl], t["kh"][:, sl], iv[:, sl]
            a = jnp.where(t["tril"], _dot(qt, kt, _NT), 0.0)
            da = jnp.where(t["tril"], _dot(do_h, iv_h, _NT), 0.0)
            st, dst = st_ref[h], dstate[h]
            cross[:, sl] = jnp.sum(dst * stn_ref[h], axis=0, keepdims=True)
            dqh = _dot(do_h, st)
            dstate[h] = _dot(do_h, qh, _TN) + dst * t["e_l"][:, sl]
            div = _dot(a, do_h, _TN) + _dot(kh, dst, _NT)
            dkh = _dot(iv_h, dst)
            dqt = _dot(da, kt)
            dkt = _dot(da, qt, _TN)
            dz_ref[:, 2 * HW + h * dk:2 * HW + (h + 1) * dk] = div.astype(BF16)
            dqa_buf[:, sl] = dqh * t["e_b"][:, sl] + dqt * t["e_qm"][:, sl]
            dkk_buf[:, sl] = dkt * t["e_km"][:, sl] + dkh * t["e_kl"][:, sl]
            db_buf[:, sl] = r16(qt) * dqt - r16(kt) * dkt + r16(qh) * dqh - r16(kh) * dkh

        dqa, dkk = dqa_buf[...], dkk_buf[...]
        triu = jnp.logical_not(t["tril"]) | (lax.broadcasted_iota(jnp.int32, (C, C), 0) == lax.broadcasted_iota(jnp.int32, (C, C), 1))
        dlf = _ones_dot(triu.astype(BF16), db_buf[...]) + cross[...]
        df = dlf / t["f"] - dkk
        sig, lb = t["sig"], t["lb"]
        dz_ref[:, HW:2 * HW] = (df * (1.0 - lb) * sig * (1.0 - sig)).astype(BF16)
        dlb_acc[...] += jnp.sum(df * (1.0 - sig), axis=0, keepdims=True)
        q, sq = t["q"], t["sq"]
        dz_ref[:, 0:HW] = (dqa * (sq * (1.0 + q * (1.0 - sq)))).astype(BF16)

        @pl.when(c == nc - 1)
        def _():
            da0 = dlb_acc[...] * lb * (1.0 - lb)
            dhlb_ref[0:1, :] = da0
            dhlb_ref[1:2, :] = -da0

    def col(k):
        return pl.BlockSpec((C, HW), lambda c: (nc - 1 - c, k))

    base = 2
    return pl.pallas_call(
        body, name="hgrn_bwd", grid=(nc,),
        in_specs=[col(base), col(base + 1), col(base + 2), col(base + 3), col(0),
                  pl.BlockSpec((None, H, dk, dk), lambda c: (nc - 1 - c, 0, 0, 0)),
                  pl.BlockSpec((None, H, dk, dk), lambda c: (jnp.minimum(nc - c, nc - 1), 0, 0, 0)), col(0),
                  pl.BlockSpec((2, HW), lambda c: (0, 0)), pl.BlockSpec((1, HW), lambda c: (0, 0)),
                  pl.BlockSpec((C, B0), lambda c: (nc - 1 - c, 0)), pl.BlockSpec((2, C, DT), lambda c: (0, nc - 1 - c, 0))],
        out_specs=[pl.BlockSpec((C, INW), lambda c: (nc - 1 - c, 0)), pl.BlockSpec((8, HW), lambda c: (0, 0)),
                   pl.BlockSpec((2, HW), lambda c: (0, 0)), pl.BlockSpec((2, DT), lambda c: (0, 0))],
        out_shape=[jax.ShapeDtypeStruct((S, INW), BF16), jax.ShapeDtypeStruct((8, HW), F32), jax.ShapeDtypeStruct((2, HW), F32),
                   jax.ShapeDtypeStruct((2, DT), F32)],
        scratch_shapes=[pltpu.VMEM((H, dk, dk), F32), pltpu.VMEM((1, HW), F32), pltpu.VMEM((C, HW), F32), pltpu.VMEM((C, HW), F32),
                        pltpu.VMEM((C, HW), F32), pltpu.VMEM((1, HW), F32)],
        compiler_params=_cparams(("arbitrary",)))(z, z, z, z, o, states, states, dyb, hg_lb, ng, dz_head, dz_tail)


def _position():
    x, y, c = lax.axis_index("x"), lax.axis_index("y"), lax.axis_index("c")
    return x, y, c, 4 * x + 2 * y + c


def _flip(x, y, c, k):
    return (1 - x if k & 4 else x, 1 - y if k & 2 else y, 1 - c if k & 1 else c)


def _allgather_small(name, v):
    R, L = v.shape

    def body(v_ref, out_ref, send_sems, recv_sems):
        x, y, c, me = _position()
        out_ref[me] = v_ref[...]
        copies = []
        for k in range(1, N_DEV):
            cp = pltpu.make_async_remote_copy(src_ref=v_ref, dst_ref=out_ref.at[me], send_sem=send_sems.at[k - 1],
                                              recv_sem=recv_sems.at[k - 1], device_id=_flip(x, y, c, k), device_id_type=MESH)
            cp.start()
            copies.append(cp)
        for cp in copies:
            cp.wait()

    return pl.pallas_call(
        body, name=name, out_shape=jax.ShapeDtypeStruct((N_DEV, R, L), v.dtype),
        in_specs=[pl.BlockSpec(memory_space=pltpu.VMEM)], out_specs=pl.BlockSpec(memory_space=pltpu.VMEM),
        scratch_shapes=[pltpu.SemaphoreType.DMA((N_DEV - 1,)), pltpu.SemaphoreType.DMA((N_DEV - 1,))],
        compiler_params=pltpu.CompilerParams(vmem_limit_bytes=VMEM_LIMIT),
    )(v)


_HBM = pl.BlockSpec(memory_space=pltpu.HBM)
_SEM = pl.BlockSpec(memory_space=pltpu.SEMAPHORE)
_EFFECT = pltpu.SideEffectType.DATAFLOW_SIDE_EFFECTING


def _split_start(name, bufs, n_sems, copies_fn, after=None):
    nb = len(bufs)
    extra = [] if after is None else [after]
    k = nb + len(extra)

    def body(*refs):
        for cp in copies_fn(refs[:nb], refs[k], refs[k + 1]):
            cp.start()
        refs[-1][...] = jnp.zeros_like(refs[-1])

    sems = pltpu.SemaphoreType.DMA((n_sems,))
    res = pl.pallas_call(
        body, name=name,
        out_shape=(sems, sems, *[pltpu.HBM(a.shape, a.dtype) for a in bufs], jax.ShapeDtypeStruct((8, LANES), F32)),
        in_specs=[_HBM] * nb + [pl.BlockSpec(memory_space=pl.ANY)] * len(extra),
        out_specs=(_SEM, _SEM, *[_HBM] * nb, pl.BlockSpec(memory_space=pltpu.VMEM)),
        input_output_aliases={i: 2 + i for i in range(nb)},
        compiler_params=pltpu.CompilerParams(has_side_effects=_EFFECT),
    )(*[pltpu.with_memory_space_constraint(a, pltpu.HBM) for a in bufs], *extra)
    return res[0], res[1], list(res[2:2 + nb]), res[-1]


def _split_wait(name, bufs, send_sems, recv_sems, after, copies_fn):
    nb = len(bufs)

    def body(*refs):
        for cp in copies_fn(refs[:nb], refs[nb], refs[nb + 1]):
            cp.wait_send()
            cp.wait_recv()

    res = pl.pallas_call(
        body, name=name, out_shape=tuple(pltpu.HBM(a.shape, a.dtype) for a in bufs),
        in_specs=[_HBM] * nb + [_SEM, _SEM, pl.BlockSpec(memory_space=pl.ANY)], out_specs=tuple([_HBM] * nb),
        input_output_aliases={i: i for i in range(nb)},
        compiler_params=pltpu.CompilerParams(has_side_effects=_EFFECT),
    )(*bufs, send_sems, recv_sems, after)
    return list(res)


def _split_relay(name, bufs, send_sems, recv_sems, after, wait_fn, n_sems, start_fn):
    nb = len(bufs)

    def body(*refs):
        for cp in wait_fn(refs[:nb], refs[nb], refs[nb + 1]):
            cp.wait_send()
            cp.wait_recv()
        for cp in start_fn(refs[:nb], refs[nb + 3], refs[nb + 4]):
            cp.start()
        refs[-1][...] = jnp.zeros_like(refs[-1])

    sems = pltpu.SemaphoreType.DMA((n_sems,))
    res = pl.pallas_call(
        body, name=name, out_shape=(sems, sems, *[pltpu.HBM(a.shape, a.dtype) for a in bufs], jax.ShapeDtypeStruct((8, LANES), F32)),
        in_specs=[_HBM] * nb + [_SEM, _SEM, pl.BlockSpec(memory_space=pl.ANY)],
        out_specs=(_SEM, _SEM, *[_HBM] * nb, pl.BlockSpec(memory_space=pltpu.VMEM)),
        input_output_aliases={i: 2 + i for i in range(nb)},
        compiler_params=pltpu.CompilerParams(has_side_effects=_EFFECT),
    )(*bufs, send_sems, recv_sems, after)
    return res[0], res[1], list(res[2:2 + nb]), res[-1]


N_CHIP = 4


def _chip_flip(x, y, k):
    return (1 - x if k & 2 else x), (1 - y if k & 1 else y)


def _gather_first_copies(n):
    def copies(bufs, send_sems, recv_sems):
        x, y, c, me = _position()
        out = []
        for w in range(n):
            for k in range(N_CHIP):
                to = (x, y, 1 - c) if k == 0 else (*_chip_flip(x, y, k), c)
                out.append(pltpu.make_async_remote_copy(
                    src_ref=bufs[w], dst_ref=bufs[n + w].at[me], send_sem=send_sems.at[w * N_CHIP + k],
                    recv_sem=recv_sems.at[w * N_CHIP + k], device_id=to, device_id_type=MESH))
        return out
    return copies


def _gather_relay_copies(n):
    def copies(bufs, send_sems, recv_sems):
        x, y, c, _ = _position()
        out = []
        for w in range(n):
            for k in range(1, N_CHIP):
                px, py = _chip_flip(x, y, k)
                blk = bufs[n + w].at[4 * px + 2 * py + c]
                out.append(pltpu.make_async_remote_copy(
                    src_ref=blk, dst_ref=blk, send_sem=send_sems.at[w * (N_CHIP - 1) + k - 1],
                    recv_sem=recv_sems.at[w * (N_CHIP - 1) + k - 1], device_id=(x, y, 1 - c), device_id_type=MESH))
        return out
    return copies


def _small_gather_copies(bufs, send_sems, recv_sems):
    x, y, c, me = _position()
    return [pltpu.make_async_remote_copy(src_ref=bufs[0], dst_ref=bufs[1].at[me], send_sem=send_sems.at[k - 1], recv_sem=recv_sems.at[k - 1],
                                         device_id=_flip(x, y, c, k), device_id_type=MESH) for k in range(1, N_DEV)]


def _forward_first_copies(n):
    def copies(bufs, send_sems, recv_sems):
        x, y, c, me = _position()
        out = []
        for w in range(n):
            for k, to in enumerate([(x, y, 1 - c), (1 - x, y, c), (x, 1 - y, c)]):
                out.append(pltpu.make_async_remote_copy(
                    src_ref=bufs[w], dst_ref=bufs[n + w].at[me], send_sem=send_sems.at[w * 3 + k],
                    recv_sem=recv_sems.at[w * 3 + k], device_id=to, device_id_type=MESH))
        return out
    return copies


def _forward_second_copies(n):
    def copies(bufs, send_sems, recv_sems):
        x, y, c, _ = _position()
        out = []
        for w in range(n):
            half = bufs[n + w].shape[1] // 2
            for k, (src_chip, rows, to) in enumerate([((1 - x, y), pl.ds(0, half), (x, 1 - y, c)), ((x, 1 - y), pl.ds(half, half), (1 - x, y, c))]):
                blk = bufs[n + w].at[4 * src_chip[0] + 2 * src_chip[1] + c, rows]
                out.append(pltpu.make_async_remote_copy(src_ref=blk, dst_ref=blk, send_sem=send_sems.at[w * 4 + k],
                                                        recv_sem=recv_sems.at[w * 4 + k], device_id=to, device_id_type=MESH))
            for k, (px, py) in enumerate([(1 - x, y), (x, 1 - y)]):
                blk = bufs[n + w].at[4 * px + 2 * py + c]
                out.append(pltpu.make_async_remote_copy(src_ref=blk, dst_ref=blk, send_sem=send_sems.at[w * 4 + 2 + k],
                                                        recv_sem=recv_sems.at[w * 4 + 2 + k], device_id=(x, y, 1 - c), device_id_type=MESH))
        return out
    return copies


def _forward_third_copies(n):
    def copies(bufs, send_sems, recv_sems):
        x, y, c, _ = _position()
        out = []
        for w in range(n):
            blk = bufs[n + w].at[4 * (1 - x) + 2 * (1 - y) + c]
            out.append(pltpu.make_async_remote_copy(src_ref=blk, dst_ref=blk, send_sem=send_sems.at[w], recv_sem=recv_sems.at[w],
                                                    device_id=(x, y, 1 - c), device_id_type=MESH))
        return out
    return copies


def _to_sibling_copies(n):
    def copies(bufs, send_sems, recv_sems):
        x, y, c, _ = _position()
        out = []
        for w in range(n):
            for q in range(N_CHIP):
                out.append(pltpu.make_async_remote_copy(
                    src_ref=bufs[w].at[2 * q + 1 - c], dst_ref=bufs[n + w].at[q], send_sem=send_sems.at[w * N_CHIP + q],
                    recv_sem=recv_sems.at[w * N_CHIP + q], device_id=(x, y, 1 - c), device_id_type=MESH))
        return out
    return copies


def _to_owner_copies(n):
    def copies(bufs, send_sems, recv_sems):
        x, y, c, _ = _position()
        out = []
        for w in range(n):
            for k in range(1, N_CHIP):
                px, py = (1 - x if k & 2 else x), (1 - y if k & 1 else y)
                out.append(pltpu.make_async_remote_copy(
                    src_ref=bufs[w].at[2 * px + py], dst_ref=bufs[n + w].at[k - 1], send_sem=send_sems.at[w * (N_CHIP - 1) + k - 1],
                    recv_sem=recv_sems.at[w * (N_CHIP - 1) + k - 1], device_id=(px, py, c), device_id_type=MESH))
        return out
    return copies


def _chip_sum(name, stack, landed, c_idx):
    _, R, C = stack.shape
    tr = _tile(R, max(BF16_ROWS, STREAM_TILE // C), BF16_ROWS)

    def body(c_ref, a_ref, b_ref, o_ref):
        o_ref[...] = (a_ref[...].astype(F32) + b_ref[...].astype(F32)).astype(o_ref.dtype)

    return pl.pallas_call(
        body, name=name,
        grid_spec=pltpu.PrefetchScalarGridSpec(
            num_scalar_prefetch=1, grid=(N_CHIP, R // tr),
            in_specs=[pl.BlockSpec((None, tr, C), lambda q, i, c_ref: (2 * q + c_ref[0], i, 0)),
                      pl.BlockSpec((None, tr, C), lambda q, i, c_ref: (q, i, 0))],
            out_specs=pl.BlockSpec((None, tr, C), lambda q, i, c_ref: (q, i, 0))),
        out_shape=jax.ShapeDtypeStruct((N_CHIP, R, C), stack.dtype),
        compiler_params=_cparams(("parallel", "parallel")))(c_idx, stack, landed)


def _ada_mod(c16, w):
    _, D = c16.shape
    n = w.shape[1]
    tk = _tile(D, 512)
    nk = D // tk

    def body(c_ref, w_ref, o_ref, ca_ref):
        @pl.when(pl.program_id(0) == 0)
        def _():
            o_ref[...] = jnp.zeros_like(o_ref)

        cv = c_ref[...]
        ca = cv * _sigmoid(cv)
        ca_ref[...] = ca
        o_ref[...] += _dot(ca, w_ref[...])

    return pl.pallas_call(
        body, name="ada_mod", grid=(nk,),
        in_specs=[pl.BlockSpec((16, tk), lambda k: (0, k)), pl.BlockSpec((tk, n), lambda k: (k, 0))],
        out_specs=[pl.BlockSpec((16, n), lambda k: (0, 0)), pl.BlockSpec((16, tk), lambda k: (0, k))],
        out_shape=[jax.ShapeDtypeStruct((16, n), F32), jax.ShapeDtypeStruct((16, D), F32)],
        compiler_params=_cparams(("arbitrary",)))(c16, w)


def _cast_shard(name, wf, slot):
    r, c = wf.shape
    tr = _tile(r, max(BF16_ROWS, STREAM_TILE // c), BF16_ROWS)

    def body(slot_ref, w_ref, s_ref, g_ref):
        v = w_ref[...].astype(BF16)
        s_ref[...] = v
        g_ref[...] = v

    return pl.pallas_call(
        body, name=name,
        grid_spec=pltpu.PrefetchScalarGridSpec(
            num_scalar_prefetch=1, grid=(r // tr,), in_specs=[pl.BlockSpec((tr, c), lambda i, s: (i, 0))],
            out_specs=[pl.BlockSpec((tr, c), lambda i, s: (i, 0)), pl.BlockSpec((None, tr, c), lambda i, s: (s[0], i, 0))]),
        out_shape=[jax.ShapeDtypeStruct((r, c), BF16), jax.ShapeDtypeStruct((N_DEV, r, c), BF16)],
        compiler_params=_cparams(("parallel",)))(slot, wf)


def _adam_math(w, g, m, v):
    m2 = ADAM_B1 * m + (1.0 - ADAM_B1) * g
    v2 = ADAM_B2 * v + (1.0 - ADAM_B2) * (g * g)
    m_hat = m2 / (1.0 - ADAM_B1 ** ADAM_STEP)
    v_hat = v2 / (1.0 - ADAM_B2 ** ADAM_STEP)
    delta = -ADAM_LR * (m_hat / (jnp.sqrt(v_hat) + ADAM_EPS) + ADAM_WD * w)
    return delta, m2, v2


def _adamw(name, w, m, v, own, own_slot, parts=(), row0=0, into=None):
    R, C = w.shape
    Rp = own.shape[1]
    tr = _tile(Rp, max(BF16_ROWS, ADAMW_TILE // C), BF16_ROWS)
    off = row0 // tr
    n_p = len(parts)
    held = [] if into is None else list(into)

    def body(slot_ref, *refs):
        w_ref, m_ref, v_ref, own_ref = refs[:4]
        g_ref, d_ref, m2_ref, v2_ref = refs[4 + n_p + len(held):]
        g = own_ref[...].astype(F32)
        for p_ref in refs[4:4 + n_p]:
            for s in range(p_ref.shape[0]):
                g = g + p_ref[s].astype(F32)
        delta, m2, v2 = _adam_math(w_ref[...], g, m_ref[...], v_ref[...])
        g_ref[...] = g
        d_ref[...] = delta
        m2_ref[...] = m2
        v2_ref[...] = v2

    blk = pl.BlockSpec((tr, C), lambda i, s: (i + off, 0))
    out = jax.ShapeDtypeStruct((R, C), F32)
    return pl.pallas_call(
        body, name=name,
        grid_spec=pltpu.PrefetchScalarGridSpec(
            num_scalar_prefetch=1, grid=(Rp // tr,),
            in_specs=[blk, blk, blk, pl.BlockSpec((None, tr, C), lambda i, s: (s[0], i, 0))]
            + [pl.BlockSpec((a.shape[0], tr, C), lambda i, s: (0, i, 0)) for a in parts]
            + [pl.BlockSpec(memory_space=pl.ANY)] * len(held),
            out_specs=[blk] * 4),
        out_shape=[out] * 4, input_output_aliases={5 + n_p + i: i for i in range(len(held))},
        compiler_params=_cparams(("parallel",)))(own_slot, w, m, v, own, *parts, *held)


def _small_update(gathered, w, m, v, after, rows):
    _, R, L = gathered.shape
    rs = w.shape[0]
    n = len(rows)
    assert all(r % 8 == 0 for r in rows) and sum(rows) <= rs and rs + 8 <= R

    def body(p_ref, w_ref, m_ref, v_ref, after_ref, *outs):
        g = p_ref[0]
        for p in range(1, N_DEV):
            g = g + p_ref[p]
        kinds = (g,) + _adam_math(w_ref[...], g[0:rs, :], m_ref[...], v_ref[...])
        at = 0
        for k, r in enumerate(rows):
            for idx, val in enumerate(kinds):
                outs[idx * n + k][...] = val[at:at + r, :]
            at += r
        outs[4 * n][...] = g[at:at + 8, :]

    vm = pl.BlockSpec(memory_space=pltpu.VMEM)
    shapes = [jax.ShapeDtypeStruct((r, L), F32) for _ in range(4) for r in rows] + [jax.ShapeDtypeStruct((8, L), F32)]
    return pl.pallas_call(body, name="small_update", in_specs=[vm] * 4 + [pl.BlockSpec(memory_space=pl.ANY)], out_specs=[vm] * len(shapes),
                          out_shape=shapes, compiler_params=pltpu.CompilerParams(vmem_limit_bytes=VMEM_LIMIT))(gathered, w, m, v, after)


class _Fetched(dict):
    def __init__(self, fetch):
        super().__init__()
        self.fetch = fetch

    def first(self, key, after):
        self[key] = self.fetch(key, after)
        return self[key]


def _local_step(x, tgt, mod, p, fetch, F, scatter=None):
    S, D = x.shape
    GW, HW = p["ln_g"].shape[1], p["hg_ng"].shape[1]
    G, T, _ = p["ws"].shape
    w = _Fetched(fetch)
    INW = 2 * GW + 4 * HW + 2 * D
    in_loc, br_loc, fi_loc = INW // N_DEV, D // N_DEV, 2 * F // N_DEV
    assert GW == HW and F % fi_loc == 0
    sh1, sc1, gt1, sh2, sc2, gt2 = (mod[:, k * D:(k + 1) * D] for k in range(6))
    bsb = jnp.broadcast_to(p["bs"][:, :, None], (G, T, GW // G))

    tm = _tile(S, 1024, 16)
    tmh = _tile(S, 512, 16)
    tn_in = _tile(in_loc, 1280)
    tn_d = _tile(D, 512)
    tn_br = _tile(br_loc, 512)
    tk_s = S
    tm_w = _tile(D, 1024)
    g_off = 2 * GW + 4 * HW

    h1 = _norm_mod("norm1", x, p["norm1_g"], sc1, sh1)
    z = _mm_nn_stacked("proj_in", h1, w.first("in", h1), tm=tm, tn=tn_in, tk=D)[0]
    ya = _gmlp_fwd(z, p["ln_g"], p["ln_b"], p["ws"], bsb, GW)
    yb, o_hg, states = _hg_fwd(z, p["hg_lb"], p["hg_ng"], HW)
    flat = {k: jnp.swapaxes(w.first(k, yb), 0, 1).reshape(GW, D) for k in ("bg", "bh")}
    tn_f = _tile(D, 1024)
    pa = _matmul(
        "branch_gmlp", ya, flat["bg"], dims=_NN, grid_mnk=(S // tm, D // tn_f, 1), tiles=(tm, tn_f),
        a_spec=pl.BlockSpec((tm, GW), lambda i, j, k: (i, 0)), b_spec=pl.BlockSpec((GW, tn_f), lambda i, j, k: (0, j)),
        out_shapes=[jax.ShapeDtypeStruct((S, D), F32)], out_specs=[pl.BlockSpec((tm, tn_f), lambda i, j, k: (i, j))], epilogue=_store(F32))[0]
    t_fi = w.first("fi_early", pa)

    def gates(ga_ref, gb_ref, ba_ref, bb_ref):
        return _sigmoid(ga_ref[...] + ba_ref[...]), _sigmoid(gb_ref[...] + bb_ref[...])

    def gate_specs(tn_, tm_=tm):
        o1, o2 = g_off // tn_, (g_off + D) // tn_
        return [pl.BlockSpec((tm_, tn_), lambda i, j, k: (i, o1 + j)), pl.BlockSpec((tm_, tn_), lambda i, j, k: (i, o2 + j)),
                pl.BlockSpec((1, tn_), lambda i, j, k: (0, j)), pl.BlockSpec((1, tn_), lambda i, j, k: (0, D // tn_ + j))]

    def merge_ep(acc, ex, outs):
        ga, gb = gates(*ex[1:5])
        outs[0][...] = acc.astype(BF16)
        outs[1][...] = (ga * ex[0][...] + gb * acc).astype(BF16)

    tile_o = pl.BlockSpec((tmh, tn_f), lambda i, j, k: (i, j))
    pb, y = _matmul(
        "branch_hg_merge", yb, flat["bh"], dims=_NN, grid_mnk=(S // tmh, D // tn_f, 1), tiles=(tmh, tn_f),
        a_spec=pl.BlockSpec((tmh, HW), lambda i, j, k: (i, 0)), b_spec=pl.BlockSpec((HW, tn_f), lambda i, j, k: (0, j)),
        extras=[pa, z, z, p["b_gate"], p["b_gate"]], extra_specs=[tile_o, *gate_specs(tn_f, tmh)],
        out_shapes=[jax.ShapeDtypeStruct((S, D), BF16), jax.ShapeDtypeStruct((S, D), BF16)], out_specs=[tile_o, tile_o],
        epilogue=merge_ep, after=t_fi)

    def resid_ep(acc, ex, outs):
        outs[0][...] = acc.astype(BF16)
        outs[1][...] = ex[0][...] + ex[1][...] * acc

    def resid_mm(name, a, b, res, gt, tm_):
        K = a.shape[1]
        t_o = pl.BlockSpec((tm_, tn_d), lambda i, j, k: (i, j))
        return _matmul(
            name, a, b, dims=_NN, grid_mnk=(S // tm_, D // tn_d, 1), tiles=(tm_, tn_d),
            a_spec=pl.BlockSpec((tm_, K), lambda i, j, k: (i, 0)), b_spec=pl.BlockSpec((K, tn_d), lambda i, j, k: (0, j)),
            extras=[res, gt], extra_specs=[t_o, pl.BlockSpec((1, tn_d), lambda i, j, k: (0, j))],
            out_shapes=[jax.ShapeDtypeStruct((S, D), BF16), jax.ShapeDtypeStruct((S, D), F32)], out_specs=[t_o, t_o], epilogue=resid_ep)

    o1, xm = resid_mm("proj_out", y, w.first("out", z), x, gt1, tm)
    h2 = _norm_mod("norm2", xm, p["norm2_g"], sc2, sh2)
    hf, hf_fac = _ffn_in_swiglu(h2, w.first("fi", h2))
    o2, x3 = resid_mm("ffn_out", hf, w.first("fo", hf), xm, gt2, tmh)
    dx3, do2, vec_l = _loss_head(x3, tgt, p["final_g"], o2, gt2)

    nf = F // fi_loc

    def dswiglu_ep(acc, ex, outs):
        outs[0][0] = (acc * ex[0][0].astype(F32)).astype(BF16)
        outs[0][1] = (acc * ex[0][1].astype(F32)).astype(BF16)

    pair = pl.BlockSpec((2, tmh, fi_loc), lambda i, j, k: (0, i, j))
    dab = _matmul(
        "ffn_out_dx", do2, w["fo"], dims=_NT, grid_mnk=(S // tmh, nf, 1), tiles=(tmh, fi_loc),
        a_spec=pl.BlockSpec((tmh, D), lambda i, j, k: (i, 0)), b_spec=pl.BlockSpec((fi_loc, D), lambda i, j, k: (j, 0)),
        extras=[hf_fac], extra_specs=[pair], out_shapes=[jax.ShapeDtypeStruct((2, S, F), BF16)], out_specs=[pair],
        epilogue=dswiglu_ep)[0]
    start = (lambda name, grads: scatter[0](name, grads)) if scatter is not None else (lambda name, grads: None)
    push = (lambda name, after: scatter[1](name, after)) if scatter is not None else (lambda name, after: None)

    def zero(token):
        return 0.0 if token is None else token[0:1, 0:1]

    tm_f = _tile(F, 512)
    g_fo = _mm_tn("ffn_out_dw", hf, do2, pl.BlockSpec((tk_s, D), lambda i, j, k: (k, j)), Mo=F, No=D, S=S, tm=tm_f, tn=D, tk=tk_s)
    g_fi = _mm_tn("ffn_in_dw", h2, dab, pl.BlockSpec((None, tk_s, fi_loc), lambda i, j, k: (j // nf, k, j % nf)),
                  Mo=D, No=2 * F, S=S, tm=tm_w, tn=fi_loc, tk=tk_s, stacked_nloc=fi_loc, after=g_fo)
    t_ffn = start("scatter_ffn", dict(fo=g_fo, fi=g_fi))
    dh2 = _mm_nt_stacked("ffn_in_dx", pl.BlockSpec((None, tm, fi_loc), lambda i, j, k: (k // nf, i, k % nf)), dab, w["fi"],
                         M=S, tm=tm, tn=D, tk=fi_loc, after=t_ffn)
    dxm, vec2, do1 = _norm_mod_bwd("norm2_bwd", dh2, xm, p["norm2_g"], sc2, dx3, o1, gt1)
    t_ffn = push("scatter_ffn", dxm)

    def dmerge_ep(acc, ex, outs):
        ga, gb = gates(*ex[2:6])
        outs[0][...] = (acc * ga).astype(BF16)
        outs[1][...] = (acc * gb).astype(BF16)
        outs[2][0] = (acc * ex[0][...] * ga * (1.0 - ga)).astype(BF16)
        outs[2][1] = (acc * ex[1][...] * gb * (1.0 - gb)).astype(BF16)

    t_o = pl.BlockSpec((tm, tn_d), lambda i, j, k: (i, j))
    dpa, dpb, dg2 = _matmul(
        "proj_out_dx", do1, w["out"], dims=_NT, grid_mnk=(S // tm, D // tn_d, 1), tiles=(tm, tn_d),
        a_spec=pl.BlockSpec((tm, D), lambda i, j, k: (i, 0)), b_spec=pl.BlockSpec((tn_d, D), lambda i, j, k: (j, 0)),
        extras=[pa, pb, z, z, p["b_gate"], p["b_gate"]], extra_specs=[t_o, t_o, *gate_specs(tn_d)],
        out_shapes=[jax.ShapeDtypeStruct((S, D), BF16), jax.ShapeDtypeStruct((S, D), BF16), jax.ShapeDtypeStruct((2, S, D), BF16)],
        out_specs=[t_o, t_o, pl.BlockSpec((2, tm, tn_d), lambda i, j, k: (0, i, j))], epilogue=dmerge_ep, after=t_ffn)
    g_out = _mm_tn("proj_out_dw", y, do1, pl.BlockSpec((tk_s, D), lambda i, j, k: (k, j)), Mo=D, No=D, S=S, tm=tn_d, tn=D, tk=tk_s)
    tn_g = _tile(GW, 512)
    b_br = pl.BlockSpec((tk_s, br_loc), lambda i, j, k: (k, j))
    g_bg = _mm_tn("branch_gmlp_dw", ya, dpa, b_br, Mo=GW, No=D, S=S, tm=tn_g, tn=br_loc, tk=tk_s, stacked_nloc=br_loc)
    g_bh = _mm_tn("branch_hg_dw", yb, dpb, b_br, Mo=HW, No=D, S=S, tm=tn_g, tn=br_loc, tk=tk_s, stacked_nloc=br_loc)
    t_mix = start("scatter_mixer", dict(out=g_out, bg=g_bg, bh=g_bh))
    def branch_dx(name, dp, w_flat):
        return _matmul(
            name, dp, w_flat, dims=_NT, grid_mnk=(S // tm, GW // tn_g, 1), tiles=(tm, tn_g),
            a_spec=pl.BlockSpec((tm, D), lambda i, j, k: (i, 0)), b_spec=pl.BlockSpec((tn_g, D), lambda i, j, k: (j, 0)),
            out_shapes=[jax.ShapeDtypeStruct((S, GW), F32)], out_specs=[pl.BlockSpec((tm, tn_g), lambda i, j, k: (i, j))],
            epilogue=_store(F32), after=t_mix)[0]

    dya = branch_dx("branch_gmlp_dx", dpa, flat["bg"])
    dyb = branch_dx("branch_hg_dx", dpb, flat["bh"])
    dz_gmlp, dln, dws, dbs = _gmlp_bwd(z, dya, p["ln_g"], p["ln_b"], p["ws"], bsb, GW)
    t_mix = push("scatter_mixer", dz_gmlp)
    dz, dng, dhlb, db_gate = _hg_bwd(z, o_hg, states, dyb, p["hg_lb"], p["hg_ng"] + zero(t_mix), HW, dz_gmlp, dg2)
    half = D // 2
    tm_h = _tile(half, 1024)
    g_in = []
    t_in = None
    for hname, h in (("a", 0), ("b", 1)):
        g_in.append(_mm_tn("proj_in_dw_" + hname, h1, dz, pl.BlockSpec((tk_s, in_loc), lambda i, j, k: (k, j)), Mo=half, No=INW, S=S,
                           tm=tm_h, tn=in_loc, tk=tk_s, stacked_nloc=in_loc, after=t_in, a_off=h * (half // tm_h)))
        t_in = start("scatter_proj_in_" + hname, {"w_in_" + hname: g_in[-1]})
    t_in = push("scatter_proj_in_a", t_in)
    dh1 = _mm_nt_stacked("proj_in_dx", pl.BlockSpec((tm, in_loc), lambda i, j, k: (i, k)), dz, w["in"], M=S, tm=tm, tn=D, tk=in_loc,
                         after=t_in)
    dx, vec1 = _norm_mod_bwd("norm1_bwd", dh1, x, p["norm1_g"], sc1, dxm)

    dmod = jnp.concatenate([vec1[0:1], vec1[1:2], vec2[3:4], vec2[0:1], vec2[1:2], vec_l[2:3]], axis=1)
    small = dict(norm1_g=vec1[2:3], b_gate=db_gate.reshape(1, 2 * D), ln_g=dln[0:1], ln_b=dln[1:2], ws=dws, bs=dbs.reshape(G, T),
                 hg_lb=dhlb, hg_ng=dng[0:1], norm2_g=vec2[2:3], final_g=vec_l[1:2], loss=vec_l[0:1, 0:LANES])
    big = dict(w_in_a=g_in[0], w_in_b=g_in[1], bg=g_bg, bh=g_bh, out=g_out, fi=g_fi, fo=g_fo)
    return dx, big, small, dmod


_SMALL = ("b_ada", "norm1_g", "b_gate", "ln_g", "ln_b", "ws", "bs", "hg_lb", "hg_ng", "norm2_g", "final_g")


def _pack(parts, rows_mult=8):
    flat = [a.reshape(-1) for a in parts]
    offs, n = [], 0
    for a in flat:
        offs.append(n)
        n += a.shape[0]
    pad = (-n) % (LANES * rows_mult)
    if pad:
        flat.append(jnp.zeros((pad,), F32))
    return jnp.concatenate(flat).reshape(-1, LANES), offs


def kernel(x, c, w_ada, b_ada, norm1_g, w_in, b_gate, gmlp_ln_g, gmlp_ln_b, gmlp_ws, gmlp_bs, hg_lb, hg_norm_g, w_branch_gmlp, w_branch_hg, w_out, norm2_g, w_ffn_in, w_ffn_out, final_norm_g, loss_target, m_w_ada, m_b_ada, m_norm1_g, m_w_in, m_b_gate, m_gmlp_ln_g, m_gmlp_ln_b, m_gmlp_ws, m_gmlp_bs, m_hg_lb, m_hg_norm_g, m_w_branch_gmlp, m_w_branch_hg, m_w_out, m_norm2_g, m_w_ffn_in, m_w_ffn_out, m_final_norm_g, v_w_ada, v_b_ada, v_norm1_g, v_w_in, v_b_gate, v_gmlp_ln_g, v_gmlp_ln_b, v_gmlp_ws, v_gmlp_bs, v_hg_lb, v_hg_norm_g, v_w_branch_gmlp, v_w_branch_hg, v_w_out, v_norm2_g, v_w_ffn_in, v_w_ffn_out, v_final_norm_g):
    S, D = x.shape[1], x.shape[2]
    ada_loc = w_ada.shape[2]
    me = 4 * lax.axis_index("x") + 2 * lax.axis_index("y") + lax.axis_index("c")
    me_idx = me.astype(jnp.int32).reshape(1)

    def empty_hbm(shape, dtype):
        return pltpu.with_memory_space_constraint(lax.empty(shape, dtype), pltpu.HBM)

    groups = dict(gather_in=dict(keys=["in"], src=[w_in], forward=True),
                  gather_mixer=dict(keys=["bg", "bh", "out"], src=[w_branch_gmlp, w_branch_hg, w_out], forward=False),
                  gather_ffn_in=dict(keys=["fi"], src=[w_ffn_in], forward=True),
                  gather_ffn_out=dict(keys=["fo"], src=[w_ffn_out], forward=False))
    group_of = {k: gname for gname, g in groups.items() for k in g["keys"]}

    def first_hop(gname, after):
        g = groups[gname]
        n = len(g["keys"])
        cast = [_cast_shard(f"{gname}_cast_{k}", a[0], me_idx) for k, a in zip(g["keys"], g["src"])]
        shards, outs = [s for s, _ in cast], [o for _, o in cast]
        if g["forward"]:
            *g["hop"], token = _split_start(gname + "_hop1", shards + outs, n * 3, _forward_first_copies(n), after=after)
        else:
            *g["hop"], token = _split_start(gname + "_hop1", shards + outs, n * N_CHIP, _gather_first_copies(n), after=after)
        return token

    def second_hop(gname, after):
        g = groups[gname]
        n = len(g["keys"])
        *g["hop"], token = _split_relay(gname + "_hop2", g["hop"][2], g["hop"][0], g["hop"][1], after,
                                        _forward_first_copies(n), n * 4, _forward_second_copies(n))
        return token

    def finish(gname, after):
        g = groups[gname]
        n = len(g["keys"])
        send_sems, recv_sems, bufs = g["hop"]
        if g["forward"]:
            send_sems, recv_sems, bufs, _ = _split_relay(gname + "_hop3", bufs, send_sems, recv_sems, after,
                                                         _forward_second_copies(n), n, _forward_third_copies(n))
            bufs = _split_wait(gname + "_wait", bufs, send_sems, recv_sems, after, _forward_third_copies(n))
        else:
            send_sems, recv_sems, bufs, _ = _split_relay(gname + "_relay", bufs, send_sems, recv_sems, after,
                                                         _gather_first_copies(n), n * (N_CHIP - 1), _gather_relay_copies(n))
            bufs = _split_wait(gname + "_wait", bufs, send_sems, recv_sems, after, _gather_relay_copies(n))
        g["done"] = dict(zip(g["keys"], bufs[n:]))

    c_all = _allgather_small("gather_c", c.reshape(D // LANES, LANES)).reshape(N_DEV, D)
    token = first_hop("gather_in", c_all)
    mod_cols, c_act = _ada_mod(jnp.pad(c_all, ((0, 16 - N_DEV), (0, 0))) + token[0:1, 0:1], w_ada[0])
    mod_vec = mod_cols[:N_DEV].reshape(-1, LANES)
    mg_send, mg_recv, mg_bufs, token = _split_start(
        "gather_mod_start", [mod_vec, lax.dynamic_update_slice(lax.empty((N_DEV, *mod_vec.shape), F32), mod_vec[None], (me, 0, 0))],
        N_DEV - 1, _small_gather_copies)
    token = second_hop("gather_in", token)
    token = first_hop("gather_ffn_in", first_hop("gather_mixer", token))
    mod_all = _split_wait("gather_mod_wait", mg_bufs, mg_send, mg_recv, token, _small_gather_copies)[1].reshape(N_DEV, N_DEV, ada_loc)
    mod = lax.dynamic_index_in_dim(mod_all, me, axis=1, keepdims=False).reshape(1, N_DEV * ada_loc) + b_ada

    def fetch(key, after):
        if key == "in":
            finish("gather_in", after)
        elif key == "fi_early":
            return first_hop("gather_ffn_out", second_hop("gather_ffn_in", after))
        elif "done" not in groups[group_of[key]]:
            finish(group_of[key], after)
        arr = groups[group_of[key]]["done"][key]
        return arr.reshape(-1, D) if key in ("out", "fo") else arr

    p = dict(norm1_g=norm1_g, b_gate=b_gate, ln_g=gmlp_ln_g, ln_b=gmlp_ln_b, ws=gmlp_ws[0], bs=gmlp_bs[0], hg_lb=hg_lb,
             hg_ng=hg_norm_g, norm2_g=norm2_g, final_g=final_norm_g.reshape(1, D))

    in_flight = {}
    c_idx = lax.axis_index("c").astype(jnp.int32).reshape(1)
    chip_idx = (2 * lax.axis_index("x") + lax.axis_index("y")).astype(jnp.int32).reshape(1)

    def scatter_start(name, grads):
        keys = list(grads)
        n = len(keys)
        stacks = [grads[k].reshape(N_DEV, -1, grads[k].shape[-1]) for k in keys]
        lands = [empty_hbm((N_CHIP, *g.shape[1:]), g.dtype) for g in stacks]
        send_sems, recv_sems, bufs, token = _split_start(name + "_d2d", stacks + lands, n * N_CHIP, _to_sibling_copies(n))
        in_flight[name] = dict(keys=keys, stage1=(send_sems, recv_sems, bufs))
        return token

    def scatter_push(name, after):
        f = in_flight[name]
        n = len(f["keys"])
        send_sems, recv_sems, bufs = f["stage1"]
        bufs = _split_wait(name + "_d2d_wait", bufs, send_sems, recv_sems, after, _to_sibling_copies(n))
        sums = [_chip_sum(f"{name}_sum_{k}", bufs[i], bufs[n + i], c_idx) for i, k in enumerate(f["keys"])]
        lands = [empty_hbm((N_CHIP - 1, *s.shape[1:]), s.dtype) for s in sums]
        send_sems, recv_sems, bufs, token = _split_start(name + "_ici", sums + lands, n * (N_CHIP - 1), _to_owner_copies(n))
        f["stage2"] = (send_sems, recv_sems, bufs)
        return token

    grad_x, _, small, dmod = _local_step(x[0], loss_target[0], mod, p, fetch, w_ffn_out.shape[1] * N_DEV, (scatter_start, scatter_push))

    small["b_ada"] = dmod
    packed, offs = _pack([small[k] for k in _SMALL] + [small["loss"]])
    sg_send, sg_recv, sg_bufs, t_tail = _split_start(
        "gather_small_start", [packed, lax.dynamic_update_slice(lax.empty((N_DEV, *packed.shape), F32), packed[None], (me, 0, 0))],
        N_DEV - 1, _small_gather_copies)
    t_tail = scatter_push("scatter_proj_in_b", t_tail)
    big_w = dict(w_in=(w_in, m_w_in, v_w_in, "w_in"), bg=(w_branch_gmlp, m_w_branch_gmlp, v_w_branch_gmlp, "w_branch_gmlp"),
                 bh=(w_branch_hg, m_w_branch_hg, v_w_branch_hg, "w_branch_hg"), out=(w_out, m_w_out, v_w_out, "w_out"),
                 fi=(w_ffn_in, m_w_ffn_in, v_w_ffn_in, "w_ffn_in"), fo=(w_ffn_out, m_w_ffn_out, v_w_ffn_out, "w_ffn_out"))
    upd = {}

    def land_and_update(name, after):
        keys = in_flight[name]["keys"]
        n = len(keys)
        send_sems, recv_sems, bufs = in_flight[name]["stage2"]
        bufs = _split_wait(name + "_ici_wait", bufs, send_sems, recv_sems, after, _to_owner_copies(n))
        for i, k in enumerate(keys):
            if k in big_w:
                wt, mt, vt, out_name = big_w[k]
                upd[out_name] = _adamw("adamw_" + out_name, wt[0], mt[0], vt[0], bufs[i], chip_idx, [bufs[n + i]])
            else:
                wt, mt, vt, out_name = big_w["w_in"]
                upd[out_name] = _adamw("adamw_" + k, wt[0], mt[0], vt[0], bufs[i], chip_idx, [bufs[n + i]],
                                       row0=0 if k == "w_in_a" else bufs[i].shape[1], into=upd.get(out_name))
            after = upd[out_name][1]
        return after

    after = land_and_update("scatter_mixer", land_and_update("scatter_ffn", t_tail))
    gathered = _split_wait("gather_small_wait", sg_bufs, sg_send, sg_recv, after, _small_gather_copies)[1]
    wp = dict(p, b_ada=b_ada)
    ms = dict(b_ada=m_b_ada, norm1_g=m_norm1_g, b_gate=m_b_gate, ln_g=m_gmlp_ln_g, ln_b=m_gmlp_ln_b, ws=m_gmlp_ws, bs=m_gmlp_bs,
              hg_lb=m_hg_lb, hg_ng=m_hg_norm_g, norm2_g=m_norm2_g, final_g=m_final_norm_g)
    vs = dict(b_ada=v_b_ada, norm1_g=v_norm1_g, b_gate=v_b_gate, ln_g=v_gmlp_ln_g, ln_b=v_gmlp_ln_b, ws=v_gmlp_ws, bs=v_gmlp_bs,
              hg_lb=v_hg_lb, hg_ng=v_hg_norm_g, norm2_g=v_norm2_g, final_g=v_final_norm_g)
    w_sm, _ = _pack([wp[k] for k in _SMALL])
    m_sm, _ = _pack([ms[k] for k in _SMALL])
    v_sm, _ = _pack([vs[k] for k in _SMALL])
    shapes = dict(b_ada=b_ada.shape, norm1_g=norm1_g.shape, b_gate=b_gate.shape, ln_g=gmlp_ln_g.shape, ln_b=gmlp_ln_b.shape,
                  ws=gmlp_ws.shape, bs=gmlp_bs.shape, hg_lb=hg_lb.shape, hg_ng=hg_norm_g.shape, norm2_g=norm2_g.shape,
                  final_g=final_norm_g.shape)
    sm_out = _small_update(gathered, w_sm, m_sm, v_sm, after, [math.prod(shapes[k]) // LANES for k in _SMALL])

    def unpack(idx, k):
        return sm_out[idx * len(_SMALL) + _SMALL.index(k)].reshape(shapes[k])

    loss = sm_out[-1][0, 0]

    assert offs[0] == 0 and ada_loc % LANES == 0
    dmod_loc = lax.dynamic_slice_in_dim(gathered, me * (ada_loc // LANES), ada_loc // LANES, axis=1).reshape(N_DEV, ada_loc)
    ca_t = jnp.pad(c_act[:N_DEV].T, ((0, 0), (0, LANES - N_DEV))).astype(BF16)
    dm_p = jnp.pad(dmod_loc, ((0, LANES - N_DEV), (0, 0))).astype(BF16)
    tm_a = _tile(D, 512)
    g_ada = _matmul(
        "ada_dw", ca_t, dm_p, dims=_NN, grid_mnk=(D // tm_a, 1, 1), tiles=(tm_a, ada_loc),
        a_spec=pl.BlockSpec((tm_a, LANES), lambda i, j, k: (i, 0)), b_spec=pl.BlockSpec((LANES, ada_loc), lambda i, j, k: (0, 0)),
        out_shapes=[jax.ShapeDtypeStruct((1, D, ada_loc), F32)], out_specs=[pl.BlockSpec((None, tm_a, ada_loc), lambda i, j, k: (0, i, 0))],
        epilogue=_store(F32))[0]
    upd["w_ada"] = _adamw("adamw_w_ada", w_ada[0], m_w_ada[0], v_w_ada[0], g_ada, jnp.zeros((1,), jnp.int32))
    land_and_update("scatter_proj_in_b", land_and_update("scatter_proj_in_a", upd["w_ada"][1]))

    order = ("w_ada", "b_ada", "norm1_g", "w_in", "b_gate", "ln_g", "ln_b", "ws", "bs", "hg_lb", "hg_ng", "w_branch_gmlp", "w_branch_hg",
             "w_out", "norm2_g", "w_ffn_in", "w_ffn_out", "final_g")
    outs = [loss, grad_x[None]]
    for idx in range(4):
        for k in order:
            outs.append(upd[k][idx][None] if k in upd else unpack(idx, k))
    return tuple(outs)
```

```python
import functools
import math

import jax
import jax.numpy as jnp
from jax import lax
from jax.experimental import pallas as pl
from jax.experimental.pallas import tpu as pltpu

F32 = jnp.float32
BF16 = jnp.bfloat16
N_DEV = 8
EPS = 1e-6
LANES = 128
HG_DK = 128
HG_CHUNK = 64
HG_MID = HG_CHUNK // 2 - 1
EXP_CLAMP = 80.0
VMEM_LIMIT = 48 * 1024 * 1024
BF16_ROWS = 16
STREAM_TILE = 1 << 20
ADAMW_TILE = 3 << 17
ADAM_LR, ADAM_B1, ADAM_B2, ADAM_EPS, ADAM_WD, ADAM_STEP = 0.001, 0.9, 0.999, 1e-08, 0.01, 10
MESH = pl.DeviceIdType.MESH

_NN = (((1,), (0,)), ((), ()))
_NT = (((1,), (1,)), ((), ()))
_TN = (((0,), (0,)), ((), ()))


def _dot(a, b, dims=_NN):
    return lax.dot_general(a.astype(BF16), b.astype(BF16), dims, preferred_element_type=F32)


def _tile(n, target, mult=LANES):
    best = None
    for t in range(mult, min(n, target) + 1, mult):
        if n % t == 0:
            best = t
    return n if best is None else best


def _cparams(sem):
    return pltpu.CompilerParams(dimension_semantics=sem, vmem_limit_bytes=VMEM_LIMIT)


def _sigmoid(x):
    return 1.0 / (1.0 + jnp.exp(-x))


def _gelu_parts(x):
    k0 = math.sqrt(2.0 / math.pi)
    x2 = x * x
    t = jnp.tanh(k0 * (x + 0.044715 * x * x2))
    g = 0.5 * x * (1.0 + t)
    dg = 0.5 * (1.0 + t) + 0.5 * x * (1.0 - t * t) * (k0 * (1.0 + 3.0 * 0.044715 * x2))
    return g, dg


def _split3(x):
    h = x.astype(BF16)
    r = x - h.astype(F32)
    m = r.astype(BF16)
    lo = (r - m.astype(F32)).astype(BF16)
    return h, m, lo


def _ones_dot(mat01, x):
    h, m, lo = _split3(x)
    d = functools.partial(lax.dot_general, dimension_numbers=_NN, preferred_element_type=F32)
    return d(mat01, h) + d(mat01, m) + d(mat01, lo)


def _matmul(name, a, b, *, dims, grid_mnk, tiles, a_spec, b_spec, extras=(), extra_specs=(), out_shapes, out_specs, epilogue, after=None,
            sem=None):
    gm, gn, nk = grid_mnk
    tm, tn = tiles
    n_ex, n_out = len(extras), len(out_shapes)
    held = [] if after is None else [after]

    def body(*refs):
        a_ref, b_ref = refs[0], refs[1]
        ex = refs[2:2 + n_ex]
        outs = refs[2 + n_ex + len(held):2 + n_ex + len(held) + n_out]
        more = () if sem is None else (pl.program_id(0) == 0,)
        if nk == 1:
            epilogue(lax.dot_general(a_ref[...], b_ref[...], dims, preferred_element_type=F32), ex, outs, *more)
            return
        acc = refs[-1]
        k = pl.program_id(2)

        @pl.when(k == 0)
        def _():
            acc[...] = jnp.zeros_like(acc)

        acc[...] += lax.dot_general(a_ref[...], b_ref[...], dims, preferred_element_type=F32)

        @pl.when(k == nk - 1)
        def _():
            epilogue(acc[...], ex, outs, *more)

    return pl.pallas_call(
        body, name=name, grid=(gm, gn, nk), in_specs=[a_spec, b_spec, *extra_specs] + [pl.BlockSpec(memory_space=pl.ANY)] * len(held),
        out_specs=list(out_specs), out_shape=list(out_shapes), scratch_shapes=[] if nk == 1 else [pltpu.VMEM((tm, tn), F32)],
        compiler_params=_cparams(sem or ("parallel", "parallel", "arbitrary")),
    )(a, b, *extras, *held)


def _store(dtype):
    def ep(acc, ex, outs):
        outs[0][...] = acc.astype(dtype)
    return ep


def _mm_nn_stacked(name, a, wg, *, tm, tn, tk, out_dtype=F32, extras=(), extra_specs=(), out_shapes=None, out_specs=None, epilogue=None,
                   after=None):
    M, K = a.shape
    _, _, nloc = wg.shape
    N = nloc * N_DEV
    q = nloc // tn
    if out_shapes is None:
        out_shapes = [jax.ShapeDtypeStruct((M, N), out_dtype)]
        out_specs = [pl.BlockSpec((tm, tn), lambda i, j, k: (i, j))]
        epilogue = _store(out_dtype)
    return _matmul(
        name, a, wg, dims=_NN, grid_mnk=(M // tm, N // tn, K // tk), tiles=(tm, tn),
        a_spec=pl.BlockSpec((tm, tk), lambda i, j, k: (i, k)),
        b_spec=pl.BlockSpec((None, tk, tn), lambda i, j, k: (j // q, k, j % q)),
        extras=extras, extra_specs=extra_specs, out_shapes=out_shapes, out_specs=out_specs, epilogue=epilogue, after=after)


def _mm_nt_stacked(name, a_spec, a, wg, *, M, tm, tn, tk, out_dtype=F32, after=None, extras=(), extra_specs=(), out_shapes=None,
                   out_specs=None, epilogue=None, sem=None):
    _, Kw, nloc = wg.shape
    q = nloc // tk
    single = out_shapes is None
    if single:
        out_shapes = [jax.ShapeDtypeStruct((M, Kw), out_dtype)]
        out_specs = [pl.BlockSpec((tm, tn), lambda i, j, k: (i, j))]
        epilogue = _store(out_dtype)
    res = _matmul(
        name, a, wg, dims=_NT, grid_mnk=(M // tm, Kw // tn, (nloc * N_DEV) // tk), tiles=(tm, tn),
        a_spec=a_spec, b_spec=pl.BlockSpec((None, tn, tk), lambda i, j, k: (k // q, j, k % q)),
        extras=extras, extra_specs=extra_specs, out_shapes=out_shapes, out_specs=out_specs, epilogue=epilogue, after=after, sem=sem)
    return res[0] if single else res


def _mm_tn(name, a, b, b_spec, *, Mo, No, S, tm, tn, tk, stacked_nloc=None, after=None, a_off=0):
    if stacked_nloc is None:
        out_shape = jax.ShapeDtypeStruct((Mo, No), BF16)
        out_spec = pl.BlockSpec((tm, tn), lambda i, j, k: (i, j))
    else:
        q = stacked_nloc // tn
        out_shape = jax.ShapeDtypeStruct((N_DEV, Mo, stacked_nloc), BF16)
        out_spec = pl.BlockSpec((None, tm, tn), lambda i, j, k: (j // q, i, j % q))
    return _matmul(
        name, a, b, dims=_TN, grid_mnk=(Mo // tm, No // tn, S // tk), tiles=(tm, tn),
        a_spec=pl.BlockSpec((tk, tm), lambda i, j, k: (k, i + a_off)), b_spec=b_spec,
        out_shapes=[out_shape], out_specs=[out_spec], epilogue=_store(BF16), after=after)[0]


def _norm_mod(name, x, g, sc, sh):
    S, D = x.shape
    tm = _tile(S, 256, 8)

    def body(x_ref, g_ref, sc_ref, sh_ref, h_ref):
        xv = x_ref[...]
        r = lax.rsqrt(jnp.mean(xv * xv, axis=-1, keepdims=True) + EPS)
        h = (xv * r) * g_ref[...]
        h_ref[...] = (h * (1.0 + sc_ref[...]) + sh_ref[...]).astype(BF16)

    row = pl.BlockSpec((tm, D), lambda i: (i, 0))
    vec = pl.BlockSpec((1, D), lambda i: (0, 0))
    return pl.pallas_call(body, name=name, grid=(S // tm,), in_specs=[row, vec, vec, vec], out_specs=row,
                          out_shape=jax.ShapeDtypeStruct((S, D), BF16), compiler_params=_cparams(("parallel",)))(x, g, sc, sh)


def _norm_mod_bwd_rows(first, dh_v, x_ref, g_ref, sc_ref, dres_ref, dx_ref, vec_ref, o_ref=None, gt_ref=None, do_ref=None):
    @pl.when(first)
    def _():
        vec_ref[...] = jnp.zeros_like(vec_ref)

    xv, gv = x_ref[...], g_ref[...]
    r = lax.rsqrt(jnp.mean(xv * xv, axis=-1, keepdims=True) + EPS)
    xn = xv * r
    one_sc = 1.0 + sc_ref[...]
    vec_ref[0:1, :] += jnp.sum(dh_v, axis=0, keepdims=True)
    vec_ref[1:2, :] += jnp.sum(dh_v * (xn * gv), axis=0, keepdims=True)
    vec_ref[2:3, :] += jnp.sum(dh_v * one_sc * xn, axis=0, keepdims=True)
    dxn = dh_v * one_sc * gv
    dx = dres_ref[...] + r * (dxn - xn * jnp.mean(dxn * xn, axis=-1, keepdims=True))
    dx_ref[...] = dx
    if o_ref is not None:
        vec_ref[3:4, :] += jnp.sum(dx * o_ref[...], axis=0, keepdims=True)
        do_ref[...] = (dx * gt_ref[...]).astype(BF16)


def _norm_mod_bwd(name, dh, x, g, sc, dres, o=None, gt=None):
    S, D = x.shape
    tm = _tile(S, 256, 8)
    gated = o is not None

    def body(*refs):
        if gated:
            dh_ref, x_ref, g_ref, sc_ref, dres_ref, o_ref, gt_ref, dx_ref, vec_ref, do_ref = refs
        else:
            dh_ref, x_ref, g_ref, sc_ref, dres_ref, dx_ref, vec_ref = refs
            o_ref = gt_ref = do_ref = None
        _norm_mod_bwd_rows(pl.program_id(0) == 0, dh_ref[...], x_ref, g_ref, sc_ref, dres_ref, dx_ref, vec_ref, o_ref, gt_ref, do_ref)

    row = pl.BlockSpec((tm, D), lambda i: (i, 0))
    vec = pl.BlockSpec((1, D), lambda i: (0, 0))
    acc = pl.BlockSpec((8, D), lambda i: (0, 0))
    ins = [dh, x, g, sc, dres] + ([o, gt] if gated else [])
    in_specs = [row, row, vec, vec, row] + ([row, vec] if gated else [])
    out_shape = [jax.ShapeDtypeStruct((S, D), F32), jax.ShapeDtypeStruct((8, D), F32)]
    out_specs = [row, acc]
    if gated:
        out_shape.append(jax.ShapeDtypeStruct((S, D), BF16))
        out_specs.append(row)
    return pl.pallas_call(body, name=name, grid=(S // tm,), in_specs=in_specs, out_specs=out_specs, out_shape=out_shape,
                          compiler_params=_cparams(("arbitrary",)))(*ins)


def _loss_head(x3, tgt, gf, o2, gt2):
    S, D = x3.shape
    tm = _tile(S, 256, 8)

    def body(x_ref, t_ref, g_ref, o_ref, gt_ref, dx_ref, do_ref, vec_ref):
        i = pl.program_id(0)

        @pl.when(i == 0)
        def _():
            vec_ref[...] = jnp.zeros_like(vec_ref)

        xv, gv = x_ref[...], g_ref[...]
        r = lax.rsqrt(jnp.mean(xv * xv, axis=-1, keepdims=True) + EPS)
        xn = xv * r
        e = xn * gv - t_ref[...]
        tok = 0.5 * jnp.mean(e * e, axis=-1, keepdims=True)
        vec_ref[0:1, :] += jnp.broadcast_to(jnp.sum(tok, axis=0, keepdims=True), (1, D))
        dy = e * (1.0 / D)
        vec_ref[1:2, :] += jnp.sum(dy * xn, axis=0, keepdims=True)
        dxn = dy * gv
        dx = r * (dxn - xn * jnp.mean(dxn * xn, axis=-1, keepdims=True))
        dx_ref[...] = dx
        vec_ref[2:3, :] += jnp.sum(dx * o_ref[...], axis=0, keepdims=True)
        do_ref[...] = (dx * gt_ref[...]).astype(BF16)

    row = pl.BlockSpec((tm, D), lambda i: (i, 0))
    vec = pl.BlockSpec((1, D), lambda i: (0, 0))
    return pl.pallas_call(
        body, name="loss_head", grid=(S // tm,), in_specs=[row, row, vec, row, vec],
        out_specs=[row, row, pl.BlockSpec((8, D), lambda i: (0, 0))],
        out_shape=[jax.ShapeDtypeStruct((S, D), F32), jax.ShapeDtypeStruct((S, D), BF16), jax.ShapeDtypeStruct((8, D), F32)],
        compiler_params=_cparams(("arbitrary",)))(x3, tgt, gf, o2, gt2)


def _ffn_in_swiglu(h, wg):
    S, D = h.shape
    _, _, tf = wg.shape
    nf = N_DEV // 2
    F = nf * tf
    tm = _tile(S, 256, 16)

    def body(h_ref, wa_ref, wu_ref, hf_ref, fac_ref):
        hv = h_ref[...]
        a = lax.dot_general(hv, wa_ref[...], _NN, preferred_element_type=F32)
        up = lax.dot_general(hv, wu_ref[...], _NN, preferred_element_type=F32)
        sa = _sigmoid(a)
        silu = a * sa
        hf_ref[...] = (silu * up).astype(BF16)
        fac_ref[0] = (up * (sa * (1.0 + a * (1.0 - sa)))).astype(BF16)
        fac_ref[1] = silu.astype(BF16)

    return pl.pallas_call(
        body, name="ffn_in_swiglu", grid=(nf, S // tm),
        in_specs=[pl.BlockSpec((tm, D), lambda j, i: (i, 0)), pl.BlockSpec((None, D, tf), lambda j, i: (j, 0, 0)),
                  pl.BlockSpec((None, D, tf), lambda j, i: (j + nf, 0, 0))],
        out_specs=[pl.BlockSpec((tm, tf), lambda j, i: (i, j)), pl.BlockSpec((2, tm, tf), lambda j, i: (0, i, j))],
        out_shape=[jax.ShapeDtypeStruct((S, F), BF16), jax.ShapeDtypeStruct((2, S, F), BF16)],
        compiler_params=_cparams(("parallel", "parallel")))(h, wg, wg)


def _gmlp_common(u_ref, v_ref, lg_ref, lb_ref, ws_ref, bsb_ref, G, T, Dg):
    ug, dug = _gelu_parts(u_ref[...])
    vg, dvg = _gelu_parts(v_ref[...])
    mu = jnp.mean(vg, axis=-1, keepdims=True)
    vc = vg - mu
    rstd = lax.rsqrt(jnp.mean(vc * vc, axis=-1, keepdims=True) + EPS)
    vhat = vc * rstd
    vn = vhat * lg_ref[...] + lb_ref[...]
    row = lax.broadcasted_iota(jnp.int32, (T, T), 0)
    col = lax.broadcasted_iota(jnp.int32, (T, T), 1)
    tril = row >= col
    s = []
    for g in range(G):
        w = jnp.where(tril, ws_ref[g], 0.0)
        s.append(_dot(w, vn[:, g * Dg:(g + 1) * Dg]) + bsb_ref[g])
    return ug, dug, dvg, rstd, vhat, vn, tril, s


def _gmlp_fwd(z, ln_g, ln_b, ws, bsb, GW):
    S = z.shape[0]
    G, T, _ = ws.shape
    Dg = GW // G

    def body(u_ref, v_ref, lg_ref, lb_ref, ws_ref, bsb_ref, ya_ref):
        ug, _, _, _, _, _, _, s = _gmlp_common(u_ref, v_ref, lg_ref, lb_ref, ws_ref, bsb_ref, G, T, Dg)
        for g in range(G):
            sl = slice(g * Dg, (g + 1) * Dg)
            ya_ref[:, sl] = (ug[:, sl] * s[g]).astype(BF16)

    vec = pl.BlockSpec((1, GW), lambda c: (0, 0))
    return pl.pallas_call(
        body, name="gmlp_fwd", grid=(S // T,),
        in_specs=[pl.BlockSpec((T, GW), lambda c: (c, 0)), pl.BlockSpec((T, GW), lambda c: (c, 1)), vec, vec,
                  pl.BlockSpec((G, T, T), lambda c: (0, 0, 0)), pl.BlockSpec((G, T, Dg), lambda c: (0, 0, 0))],
        out_specs=pl.BlockSpec((T, GW), lambda c: (c, 0)), out_shape=jax.ShapeDtypeStruct((S, GW), BF16),
        compiler_params=_cparams(("parallel",)))(z, z, ln_g, ln_b, ws, bsb)


def _gmlp_bwd(z, dya, ln_g, ln_b, ws, bsb, GW):
    S = z.shape[0]
    G, T, _ = ws.shape
    Dg = GW // G
    nc = S // T

    def body(u_ref, v_ref, dya_ref, lg_ref, lb_ref, ws_ref, bsb_ref, dz_ref, dln_ref, dws_ref, dbs_ref, dbs_acc, dvh):
        c = pl.program_id(0)

        @pl.when(c == 0)
        def _():
            dln_ref[...] = jnp.zeros_like(dln_ref)
            dws_ref[...] = jnp.zeros_like(dws_ref)
            dbs_acc[...] = jnp.zeros_like(dbs_acc)

        ug, dug, dvg, rstd, vhat, vn, tril, s = _gmlp_common(u_ref, v_ref, lg_ref, lb_ref, ws_ref, bsb_ref, G, T, Dg)
        dya_v = dya_ref[...]
        for g in range(G):
            sl = slice(g * Dg, (g + 1) * Dg)
            dy_g = dya_v[:, sl]
            dz_ref[:, sl] = (dy_g * s[g] * dug[:, sl]).astype(BF16)
            ds = dy_g * ug[:, sl]
            dbs_acc[g] += ds
            w = jnp.where(tril, ws_ref[g], 0.0)
            dvn_g = _dot(w, ds, _TN)
            dws_ref[g] += jnp.where(tril, _dot(ds, vn[:, sl], _NT), 0.0)
            dln_ref[0:1, sl] += jnp.sum(dvn_g * vhat[:, sl], axis=0, keepdims=True)
            dln_ref[1:2, sl] += jnp.sum(dvn_g, axis=0, keepdims=True)
            dvh[:, sl] = dvn_g * lg_ref[:, sl]
        dvhat = dvh[...]
        m1 = jnp.mean(dvhat, axis=-1, keepdims=True)
        m2 = jnp.mean(dvhat * vhat, axis=-1, keepdims=True)
        dz_ref[:, GW:2 * GW] = (rstd * (dvhat - m1 - vhat * m2) * dvg).astype(BF16)

        @pl.when(c == nc - 1)
        def _():
            for g in range(G):
                dbs_ref[g] = jnp.sum(dbs_acc[g], axis=-1, keepdims=True)

    vec = pl.BlockSpec((1, GW), lambda c: (0, 0))
    return pl.pallas_call(
        body, name="gmlp_bwd", grid=(nc,),
        in_specs=[pl.BlockSpec((T, GW), lambda c: (c, 0)), pl.BlockSpec((T, GW), lambda c: (c, 1)),
                  pl.BlockSpec((T, GW), lambda c: (c, 0)), vec, vec,
                  pl.BlockSpec((G, T, T), lambda c: (0, 0, 0)), pl.BlockSpec((G, T, Dg), lambda c: (0, 0, 0))],
        out_specs=[pl.BlockSpec((T, 2 * GW), lambda c: (c, 0)), pl.BlockSpec((8, GW), lambda c: (0, 0)),
                   pl.BlockSpec((G, T, T), lambda c: (0, 0, 0)), pl.BlockSpec((G, T, 1), lambda c: (0, 0, 0))],
        out_shape=[jax.ShapeDtypeStruct((S, 2 * GW), BF16), jax.ShapeDtypeStruct((8, GW), F32),
                   jax.ShapeDtypeStruct((G, T, T), F32), jax.ShapeDtypeStruct((G, T, 1), F32)],
        scratch_shapes=[pltpu.VMEM((G, T, Dg), F32), pltpu.VMEM((T, GW), F32)],
        compiler_params=_cparams(("arbitrary",)))(z, z, dya, ln_g, ln_b, ws, bsb)


def _hg_common(q_ref, f_ref, hlb_ref):
    C = HG_CHUNK
    a = hlb_ref[...]
    lb = _sigmoid(a[0:1, :] - a[1:2, :])
    sig = _sigmoid(f_ref[...])
    f = lb + (1.0 - lb) * sig
    lf = jnp.log(f)
    kk = 1.0 - f
    q = q_ref[...]
    sq = _sigmoid(q)
    qa = q * sq
    row = lax.broadcasted_iota(jnp.int32, (C, C), 0)
    col = lax.broadcasted_iota(jnp.int32, (C, C), 1)
    tril = row >= col
    b = _ones_dot(tril.astype(BF16), lf)
    bm = b[HG_MID:HG_MID + 1, :]
    bl = b[C - 1:C, :]
    e_b = jnp.exp(b)
    e_qm = jnp.exp(jnp.minimum(b - bm, EXP_CLAMP))
    e_km = jnp.exp(jnp.minimum(bm - b, EXP_CLAMP))
    e_kl = jnp.exp(bl - b)
    return dict(lb=lb, sig=sig, f=f, kk=kk, q=q, sq=sq, qa=qa, tril=tril, e_b=e_b, e_qm=e_qm, e_km=e_km, e_kl=e_kl,
                e_l=jnp.exp(bl), qh=qa * e_b, qt=qa * e_qm, kt=kk * e_km, kh=kk * e_kl)


def _hg_fwd(z, hg_lb, ng, HW):
    S = z.shape[0]
    C, H, dk = HG_CHUNK, HW // HG_DK, HG_DK
    nc = S // C

    def body(q_ref, f_ref, i_ref, og_ref, hlb_ref, ng_ref, yb_ref, o_ref, st_ref, state):
        @pl.when(pl.program_id(0) == 0)
        def _():
            state[...] = jnp.zeros_like(state)

        t = _hg_common(q_ref, f_ref, hlb_ref)
        iv = i_ref[...]
        for h in range(H):
            sl = slice(h * dk, (h + 1) * dk)
            st = state[h]
            st_ref[h] = st
            a = jnp.where(t["tril"], _dot(t["qt"][:, sl], t["kt"][:, sl], _NT), 0.0)
            o_h = _dot(a, iv[:, sl]) + _dot(t["qh"][:, sl], st, _NT)
            state[h] = st * t["e_l"][:, sl] + _dot(iv[:, sl], t["kh"][:, sl], _TN)
            o_ref[:, sl] = o_h
            rr = lax.rsqrt(jnp.mean(o_h * o_h, axis=-1, keepdims=True) + EPS)
            og = og_ref[:, sl]
            yb_ref[:, sl] = (o_h * rr * ng_ref[:, sl] * (og * _sigmoid(og))).astype(BF16)

    def col(k):
        return pl.BlockSpec((C, HW), lambda c: (c, k))

    base = 2
    return pl.pallas_call(
        body, name="hgrn_fwd", grid=(nc,),
        in_specs=[col(base), col(base + 1), col(base + 2), col(base + 3),
                  pl.BlockSpec((2, HW), lambda c: (0, 0)), pl.BlockSpec((1, HW), lambda c: (0, 0))],
        out_specs=[pl.BlockSpec((C, HW), lambda c: (c, 0)), pl.BlockSpec((C, HW), lambda c: (c, 0)),
                   pl.BlockSpec((None, H, dk, dk), lambda c: (c, 0, 0, 0))],
        out_shape=[jax.ShapeDtypeStruct((S, HW), BF16), jax.ShapeDtypeStruct((S, HW), F32),
                   jax.ShapeDtypeStruct((nc, H, dk, dk), F32)],
        scratch_shapes=[pltpu.VMEM((H, dk, dk), F32)],
        compiler_params=_cparams(("arbitrary",)))(z, z, z, z, hg_lb, ng)


def _hg_bwd(z, o, states, dyb, hg_lb, ng, HW, dz_head, dz_tail):
    S = z.shape[0]
    C, H, dk = HG_CHUNK, HW // HG_DK, HG_DK
    nc = S // C
    B0 = dz_head.shape[1]
    DT = dz_tail.shape[2]
    INW = B0 + 4 * HW + 2 * DT

    def body(q_ref, f_ref, i_ref, og_ref, o_ref, st_ref, stn_ref, dyb_ref, hlb_ref, ng_ref, head_ref, tail_ref,
             dzf_ref, dng_ref, dhlb_ref, dtail_ref, dstate, cross, dqa_buf, dkk_buf, db_buf, dlb_acc):
        c = pl.program_id(0)
        dzf_ref[:, 0:B0] = head_ref[...]
        dzf_ref[:, B0 + 4 * HW:B0 + 4 * HW + DT] = tail_ref[0]
        dzf_ref[:, B0 + 4 * HW + DT:INW] = tail_ref[1]
        dz_ref = dzf_ref.at[:, B0:B0 + 4 * HW]

        @pl.when(c == 0)
        def _():
            dtail_ref[...] = jnp.zeros_like(dtail_ref)

        dtail_ref[0:1, :] += jnp.sum(tail_ref[0].astype(F32), axis=0, keepdims=True)
        dtail_ref[1:2, :] += jnp.sum(tail_ref[1].astype(F32), axis=0, keepdims=True)

        @pl.when(c == 0)
        def _():
            dstate[...] = jnp.zeros_like(dstate)
            dlb_acc[...] = jnp.zeros_like(dlb_acc)
            dng_ref[...] = jnp.zeros_like(dng_ref)

        def r16(v):
            return v.astype(BF16).astype(F32)

        t = _hg_common(q_ref, f_ref, hlb_ref)
        iv = i_ref[...]
        for h in range(H):
            sl = slice(h * dk, (h + 1) * dk)
            o_h, og, dyb_h, ng_h = o_ref[:, sl], og_ref[:, sl], dyb_ref[:, sl], ng_ref[:, sl]
            sg = _sigmoid(og)
            silu_og = og * sg
            rr = lax.rsqrt(jnp.mean(o_h * o_h, axis=-1, keepdims=True) + EPS)
            on = o_h * rr
            dng_ref[0:1, sl] += jnp.sum(dyb_h * on * silu_og, axis=0, keepdims=True)
            dz_ref[:, 3 * HW + h * dk:3 * HW + (h + 1) * dk] = (dyb_h * on * ng_h * (sg * (1.0 + og * (1.0 - sg)))).astype(BF16)
            don = dyb_h * ng_h * silu_og
            do_h = rr * (don - on * jnp.mean(don * on, axis=-1, keepdims=True))

            qt, kt, qh, kh, iv_h = t["qt"][:, sl], t["kt"][:, sl], t["qh"][:, sl], t["kh"][:, sl], iv[:, sl]
            a = jnp.where(t["tril"], _dot(qt, kt, _NT), 0.0)
            da = jnp.where(t["tril"], _dot(do_h, iv_h, _NT), 0.0)
            st, dst = st_ref[h], dstate[h]
            cross[:, sl] = jnp.sum(dst * stn_ref[h], axis=0, keepdims=True)
            dqh = _dot(do_h, st)
            dstate[h] = _dot(do_h, qh, _TN) + dst * t["e_l"][:, sl]
            div = _dot(a, do_h, _TN) + _dot(kh, dst, _NT)
            dkh = _dot(iv_h, dst)
            dqt = _dot(da, kt)
            dkt = _dot(da, qt, _TN)
            dz_ref[:, 2 * HW + h * dk:2 * HW + (h + 1) * dk] = div.astype(BF16)
            dqa_buf[:, sl] = dqh * t["e_b"][:, sl] + dqt * t["e_qm"][:, sl]
            dkk_buf[:, sl] = dkt * t["e_km"][:, sl] + dkh * t["e_kl"][:, sl]
            db_buf[:, sl] = r16(qt) * dqt - r16(kt) * dkt + r16(qh) * dqh - r16(kh) * dkh

        dqa, dkk = dqa_buf[...], dkk_buf[...]
        triu = jnp.logical_not(t["tril"]) | (lax.broadcasted_iota(jnp.int32, (C, C), 0) == lax.broadcasted_iota(jnp.int32, (C, C), 1))
        dlf = _ones_dot(triu.astype(BF16), db_buf[...]) + cross[...]
        df = dlf / t["f"] - dkk
        sig, lb = t["sig"], t["lb"]
        dz_ref[:, HW:2 * HW] = (df * (1.0 - lb) * sig * (1.0 - sig)).astype(BF16)
        dlb_acc[...] += jnp.sum(df * (1.0 - sig), axis=0, keepdims=True)
        q, sq = t["q"], t["sq"]
        dz_ref[:, 0:HW] = (dqa * (sq * (1.0 + q * (1.0 - sq)))).astype(BF16)

        @pl.when(c == nc - 1)
        def _():
            da0 = dlb_acc[...] * lb * (1.0 - lb)
            dhlb_ref[0:1, :] = da0
            dhlb_ref[1:2, :] = -da0

    def col(k):
        return pl.BlockSpec((C, HW), lambda c: (nc - 1 - c, k))

    base = 2
    return pl.pallas_call(
        body, name="hgrn_bwd", grid=(nc,),
        in_specs=[col(base), col(base + 1), col(base + 2), col(base + 3), col(0),
                  pl.BlockSpec((None, H, dk, dk), lambda c: (nc - 1 - c, 0, 0, 0)),
                  pl.BlockSpec((None, H, dk, dk), lambda c: (jnp.minimum(nc - c, nc - 1), 0, 0, 0)), col(0),
                  pl.BlockSpec((2, HW), lambda c: (0, 0)), pl.BlockSpec((1, HW), lambda c: (0, 0)),
                  pl.BlockSpec((C, B0), lambda c: (nc - 1 - c, 0)), pl.BlockSpec((2, C, DT), lambda c: (0, nc - 1 - c, 0))],
        out_specs=[pl.BlockSpec((C, INW), lambda c: (nc - 1 - c, 0)), pl.BlockSpec((8, HW), lambda c: (0, 0)),
                   pl.BlockSpec((2, HW), lambda c: (0, 0)), pl.BlockSpec((2, DT), lambda c: (0, 0))],
        out_shape=[jax.ShapeDtypeStruct((S, INW), BF16), jax.ShapeDtypeStruct((8, HW), F32), jax.ShapeDtypeStruct((2, HW), F32),
                   jax.ShapeDtypeStruct((2, DT), F32)],
        scratch_shapes=[pltpu.VMEM((H, dk, dk), F32), pltpu.VMEM((1, HW), F32), pltpu.VMEM((C, HW), F32), pltpu.VMEM((C, HW), F32),
                        pltpu.VMEM((C, HW), F32), pltpu.VMEM((1, HW), F32)],
        compiler_params=_cparams(("arbitrary",)))(z, z, z, z, o, states, states, dyb, hg_lb, ng, dz_head, dz_tail)


def _position():
    x, y, c = lax.axis_index("x"), lax.axis_index("y"), lax.axis_index("c")
    return x, y, c, 4 * x + 2 * y + c


def _flip(x, y, c, k):
    return (1 - x if k & 4 else x, 1 - y if k & 2 else y, 1 - c if k & 1 else c)


def _allgather_small(name, v):
    R, L = v.shape

    def body(v_ref, out_ref, send_sems, recv_sems):
        x, y, c, me = _position()
        out_ref[me] = v_ref[...]
        copies = []
        for k in range(1, N_DEV):
            cp = pltpu.make_async_remote_copy(src_ref=v_ref, dst_ref=out_ref.at[me], send_sem=send_sems.at[k - 1],
                                              recv_sem=recv_sems.at[k - 1], device_id=_flip(x, y, c, k), device_id_type=MESH)
            cp.start()
            copies.append(cp)
        for cp in copies:
            cp.wait()

    return pl.pallas_call(
        body, name=name, out_shape=jax.ShapeDtypeStruct((N_DEV, R, L), v.dtype),
        in_specs=[pl.BlockSpec(memory_space=pltpu.VMEM)], out_specs=pl.BlockSpec(memory_space=pltpu.VMEM),
        scratch_shapes=[pltpu.SemaphoreType.DMA((N_DEV - 1,)), pltpu.SemaphoreType.DMA((N_DEV - 1,))],
        compiler_params=pltpu.CompilerParams(vmem_limit_bytes=VMEM_LIMIT),
    )(v)


_HBM = pl.BlockSpec(memory_space=pltpu.HBM)
_SEM = pl.BlockSpec(memory_space=pltpu.SEMAPHORE)
_EFFECT = pltpu.SideEffectType.DATAFLOW_SIDE_EFFECTING


def _split_start(name, bufs, n_sems, copies_fn, after=None):
    nb = len(bufs)
    extra = [] if after is None else [after]
    k = nb + len(extra)

    def body(*refs):
        for cp in copies_fn(refs[:nb], refs[k], refs[k + 1]):
            cp.start()
        refs[-1][...] = jnp.zeros_like(refs[-1])

    sems = pltpu.SemaphoreType.DMA((n_sems,))
    res = pl.pallas_call(
        body, name=name,
        out_shape=(sems, sems, *[pltpu.HBM(a.shape, a.dtype) for a in bufs], jax.ShapeDtypeStruct((8, LANES), F32)),
        in_specs=[_HBM] * nb + [pl.BlockSpec(memory_space=pl.ANY)] * len(extra),
        out_specs=(_SEM, _SEM, *[_HBM] * nb, pl.BlockSpec(memory_space=pltpu.VMEM)),
        input_output_aliases={i: 2 + i for i in range(nb)},
        compiler_params=pltpu.CompilerParams(has_side_effects=_EFFECT),
    )(*[pltpu.with_memory_space_constraint(a, pltpu.HBM) for a in bufs], *extra)
    return res[0], res[1], list(res[2:2 + nb]), res[-1]


def _split_wait(name, bufs, send_sems, recv_sems, after, copies_fn):
    nb = len(bufs)

    def body(*refs):
        for cp in copies_fn(refs[:nb], refs[nb], refs[nb + 1]):
            cp.wait_send()
            cp.wait_recv()

    res = pl.pallas_call(
        body, name=name, out_shape=tuple(pltpu.HBM(a.shape, a.dtype) for a in bufs),
        in_specs=[_HBM] * nb + [_SEM, _SEM, pl.BlockSpec(memory_space=pl.ANY)], out_specs=tuple([_HBM] * nb),
        input_output_aliases={i: i for i in range(nb)},
        compiler_params=pltpu.CompilerParams(has_side_effects=_EFFECT),
    )(*bufs, send_sems, recv_sems, after)
    return list(res)


def _split_relay(name, bufs, send_sems, recv_sems, after, wait_fn, n_sems, start_fn):
    nb = len(bufs)

    def body(*refs):
        for cp in wait_fn(refs[:nb], refs[nb], refs[nb + 1]):
            cp.wait_send()
            cp.wait_recv()
        for cp in start_fn(refs[:nb], refs[nb + 3], refs[nb + 4]):
            cp.start()
        refs[-1][...] = jnp.zeros_like(refs[-1])

    sems = pltpu.SemaphoreType.DMA((n_sems,))
    res = pl.pallas_call(
        body, name=name, out_shape=(sems, sems, *[pltpu.HBM(a.shape, a.dtype) for a in bufs], jax.ShapeDtypeStruct((8, LANES), F32)),
        in_specs=[_HBM] * nb + [_SEM, _SEM, pl.BlockSpec(memory_space=pl.ANY)],
        out_specs=(_SEM, _SEM, *[_HBM] * nb, pl.BlockSpec(memory_space=pltpu.VMEM)),
        input_output_aliases={i: 2 + i for i in range(nb)},
        compiler_params=pltpu.CompilerParams(has_side_effects=_EFFECT),
    )(*bufs, send_sems, recv_sems, after)
    return res[0], res[1], list(res[2:2 + nb]), res[-1]


N_CHIP = 4


def _chip_flip(x, y, k):
    return (1 - x if k & 2 else x), (1 - y if k & 1 else y)


def _gather_first_copies(n):
    def copies(bufs, send_sems, recv_sems):
        x, y, c, me = _position()
        out = []
        for w in range(n):
            for k in range(N_CHIP):
                to = (x, y, 1 - c) if k == 0 else (*_chip_flip(x, y, k), c)
                out.append(pltpu.make_async_remote_copy(
                    src_ref=bufs[w], dst_ref=bufs[n + w].at[me], send_sem=send_sems.at[w * N_CHIP + k],
                    recv_sem=recv_sems.at[w * N_CHIP + k], device_id=to, device_id_type=MESH))
        return out
    return copies


def _gather_relay_copies(n):
    def copies(bufs, send_sems, recv_sems):
        x, y, c, _ = _position()
        out = []
        for w in range(n):
            for k in range(1, N_CHIP):
                px, py = _chip_flip(x, y, k)
                blk = bufs[n + w].at[4 * px + 2 * py + c]
                out.append(pltpu.make_async_remote_copy(
                    src_ref=blk, dst_ref=blk, send_sem=send_sems.at[w * (N_CHIP - 1) + k - 1],
                    recv_sem=recv_sems.at[w * (N_CHIP - 1) + k - 1], device_id=(x, y, 1 - c), device_id_type=MESH))
        return out
    return copies


def _small_gather_copies(bufs, send_sems, recv_sems):
    x, y, c, me = _position()
    return [pltpu.make_async_remote_copy(src_ref=bufs[0], dst_ref=bufs[1].at[me], send_sem=send_sems.at[k - 1], recv_sem=recv_sems.at[k - 1],
                                         device_id=_flip(x, y, c, k), device_id_type=MESH) for k in range(1, N_DEV)]


def _forward_first_copies(n):
    def copies(bufs, send_sems, recv_sems):
        x, y, c, me = _position()
        out = []
        for w in range(n):
            for k, to in enumerate([(x, y, 1 - c), (1 - x, y, c), (x, 1 - y, c)]):
                out.append(pltpu.make_async_remote_copy(
                    src_ref=bufs[w], dst_ref=bufs[n + w].at[me], send_sem=send_sems.at[w * 3 + k],
                    recv_sem=recv_sems.at[w * 3 + k], device_id=to, device_id_type=MESH))
        return out
    return copies


def _forward_second_copies(n):
    def copies(bufs, send_sems, recv_sems):
        x, y, c, _ = _position()
        out = []
        for w in range(n):
            half = bufs[n + w].shape[1] // 2
            for k, (src_chip, rows, to) in enumerate([((1 - x, y), pl.ds(0, half), (x, 1 - y, c)), ((x, 1 - y), pl.ds(half, half), (1 - x, y, c))]):
                blk = bufs[n + w].at[4 * src_chip[0] + 2 * src_chip[1] + c, rows]
                out.append(pltpu.make_async_remote_copy(src_ref=blk, dst_ref=blk, send_sem=send_sems.at[w * 4 + k],
                                                        recv_sem=recv_sems.at[w * 4 + k], device_id=to, device_id_type=MESH))
            for k, (px, py) in enumerate([(1 - x, y), (x, 1 - y)]):
                blk = bufs[n + w].at[4 * px + 2 * py + c]
                out.append(pltpu.make_async_remote_copy(src_ref=blk, dst_ref=blk, send_sem=send_sems.at[w * 4 + 2 + k],
                                                        recv_sem=recv_sems.at[w * 4 + 2 + k], device_id=(x, y, 1 - c), device_id_type=MESH))
        return out
    return copies


def _forward_third_copies(n):
    def copies(bufs, send_sems, recv_sems):
        x, y, c, _ = _position()
        out = []
        for w in range(n):
            blk = bufs[n + w].at[4 * (1 - x) + 2 * (1 - y) + c]
            out.append(pltpu.make_async_remote_copy(src_ref=blk, dst_ref=blk, send_sem=send_sems.at[w], recv_sem=recv_sems.at[w],
                                                    device_id=(x, y, 1 - c), device_id_type=MESH))
        return out
    return copies


def _to_sibling_copies(n):
    def copies(bufs, send_sems, recv_sems):
        x, y, c, _ = _position()
        out = []
        for w in range(n):
            for q in range(N_CHIP):
                out.append(pltpu.make_async_remote_copy(
                    src_ref=bufs[w].at[2 * q + 1 - c], dst_ref=bufs[n + w].at[q], send_sem=send_sems.at[w * N_CHIP + q],
                    recv_sem=recv_sems.at[w * N_CHIP + q], device_id=(x, y, 1 - c), device_id_type=MESH))
        return out
    return copies


def _to_owner_copies(n):
    def copies(bufs, send_sems, recv_sems):
        x, y, c, _ = _position()
        out = []
        for w in range(n):
            for k in range(1, N_CHIP):
                px, py = (1 - x if k & 2 else x), (1 - y if k & 1 else y)
                out.append(pltpu.make_async_remote_copy(
                    src_ref=bufs[w].at[2 * px + py], dst_ref=bufs[n + w].at[k - 1], send_sem=send_sems.at[w * (N_CHIP - 1) + k - 1],
                    recv_sem=recv_sems.at[w * (N_CHIP - 1) + k - 1], device_id=(px, py, c), device_id_type=MESH))
        return out
    return copies


def _chip_sum(name, stack, landed, c_idx):
    _, R, C = stack.shape
    tr = _tile(R, max(BF16_ROWS, STREAM_TILE // C), BF16_ROWS)

    def body(c_ref, a_ref, b_ref, o_ref):
        o_ref[...] = (a_ref[...].astype(F32) + b_ref[...].astype(F32)).astype(o_ref.dtype)

    return pl.pallas_call(
        body, name=name,
        grid_spec=pltpu.PrefetchScalarGridSpec(
            num_scalar_prefetch=1, grid=(N_CHIP, R // tr),
            in_specs=[pl.BlockSpec((None, tr, C), lambda q, i, c_ref: (2 * q + c_ref[0], i, 0)),
                      pl.BlockSpec((None, tr, C), lambda q, i, c_ref: (q, i, 0))],
            out_specs=pl.BlockSpec((None, tr, C), lambda q, i, c_ref: (q, i, 0))),
        out_shape=jax.ShapeDtypeStruct((N_CHIP, R, C), stack.dtype),
        compiler_params=_cparams(("parallel", "parallel")))(c_idx, stack, landed)


def _ada_mod(c16, w):
    _, D = c16.shape
    n = w.shape[1]
    tk = _tile(D, 512)
    nk = D // tk

    def body(c_ref, w_ref, o_ref, ca_ref):
        @pl.when(pl.program_id(0) == 0)
        def _():
            o_ref[...] = jnp.zeros_like(o_ref)

        cv = c_ref[...]
        ca = cv * _sigmoid(cv)
        ca_ref[...] = ca
        o_ref[...] += _dot(ca, w_ref[...])

    return pl.pallas_call(
        body, name="ada_mod", grid=(nk,),
        in_specs=[pl.BlockSpec((16, tk), lambda k: (0, k)), pl.BlockSpec((tk, n), lambda k: (k, 0))],
        out_specs=[pl.BlockSpec((16, n), lambda k: (0, 0)), pl.BlockSpec((16, tk), lambda k: (0, k))],
        out_shape=[jax.ShapeDtypeStruct((16, n), F32), jax.ShapeDtypeStruct((16, D), F32)],
        compiler_params=_cparams(("arbitrary",)))(c16, w)


def _cast_shard(name, wf, slot):
    r, c = wf.shape
    tr = _tile(r, max(BF16_ROWS, STREAM_TILE // c), BF16_ROWS)

    def body(slot_ref, w_ref, s_ref, g_ref):
        v = w_ref[...].astype(BF16)
        s_ref[...] = v
        g_ref[...] = v

    return pl.pallas_call(
        body, name=name,
        grid_spec=pltpu.PrefetchScalarGridSpec(
            num_scalar_prefetch=1, grid=(r // tr,), in_specs=[pl.BlockSpec((tr, c), lambda i, s: (i, 0))],
            out_specs=[pl.BlockSpec((tr, c), lambda i, s: (i, 0)), pl.BlockSpec((None, tr, c), lambda i, s: (s[0], i, 0))]),
        out_shape=[jax.ShapeDtypeStruct((r, c), BF16), jax.ShapeDtypeStruct((N_DEV, r, c), BF16)],
        compiler_params=_cparams(("parallel",)))(slot, wf)


def _adam_math(w, g, m, v):
    m2 = ADAM_B1 * m + (1.0 - ADAM_B1) * g
    v2 = ADAM_B2 * v + (1.0 - ADAM_B2) * (g * g)
    m_hat = m2 / (1.0 - ADAM_B1 ** ADAM_STEP)
    v_hat = v2 / (1.0 - ADAM_B2 ** ADAM_STEP)
    delta = -ADAM_LR * (m_hat / (jnp.sqrt(v_hat) + ADAM_EPS) + ADAM_WD * w)
    return delta, m2, v2


def _adamw(name, w, m, v, own, own_slot, parts=(), row0=0, into=None):
    R, C = w.shape
    Rp = own.shape[1]
    tr = _tile(Rp, max(BF16_ROWS, ADAMW_TILE // C), BF16_ROWS)
    off = row0 // tr
    n_p = len(parts)
    held = [] if into is None else list(into)

    def body(slot_ref, *refs):
        w_ref, m_ref, v_ref, own_ref = refs[:4]
        g_ref, d_ref, m2_ref, v2_ref = refs[4 + n_p + len(held):]
        g = own_ref[...].astype(F32)
        for p_ref in refs[4:4 + n_p]:
            for s in range(p_ref.shape[0]):
                g = g + p_ref[s].astype(F32)
        delta, m2, v2 = _adam_math(w_ref[...], g, m_ref[...], v_ref[...])
        g_ref[...] = g
        d_ref[...] = delta
        m2_ref[...] = m2
        v2_ref[...] = v2

    blk = pl.BlockSpec((tr, C), lambda i, s: (i + off, 0))
    out = jax.ShapeDtypeStruct((R, C), F32)
    return pl.pallas_call(
        body, name=name,
        grid_spec=pltpu.PrefetchScalarGridSpec(
            num_scalar_prefetch=1, grid=(Rp // tr,),
            in_specs=[blk, blk, blk, pl.BlockSpec((None, tr, C), lambda i, s: (s[0], i, 0))]
            + [pl.BlockSpec((a.shape[0], tr, C), lambda i, s: (0, i, 0)) for a in parts]
            + [pl.BlockSpec(memory_space=pl.ANY)] * len(held),
            out_specs=[blk] * 4),
        out_shape=[out] * 4, input_output_aliases={5 + n_p + i: i for i in range(len(held))},
        compiler_params=_cparams(("parallel",)))(own_slot, w, m, v, own, *parts, *held)


def _small_update(gathered, w, m, v, after, rows):
    _, R, L = gathered.shape
    rs = w.shape[0]
    n = len(rows)
    assert all(r % 8 == 0 for r in rows) and sum(rows) <= rs and rs + 8 <= R

    def body(p_ref, w_ref, m_ref, v_ref, after_ref, *outs):
        g = p_ref[0]
        for p in range(1, N_DEV):
            g = g + p_ref[p]
        kinds = (g,) + _adam_math(w_ref[...], g[0:rs, :], m_ref[...], v_ref[...])
        at = 0
        for k, r in enumerate(rows):
            for idx, val in enumerate(kinds):
                outs[idx * n + k][...] = val[at:at + r, :]
            at += r
        outs[4 * n][...] = g[at:at + 8, :]

    vm = pl.BlockSpec(memory_space=pltpu.VMEM)
    shapes = [jax.ShapeDtypeStruct((r, L), F32) for _ in range(4) for r in rows] + [jax.ShapeDtypeStruct((8, L), F32)]
    return pl.pallas_call(body, name="small_update", in_specs=[vm] * 4 + [pl.BlockSpec(memory_space=pl.ANY)], out_specs=[vm] * len(shapes),
                          out_shape=shapes, compiler_params=pltpu.CompilerParams(vmem_limit_bytes=VMEM_LIMIT))(gathered, w, m, v, after)


class _Fetched(dict):
    def __init__(self, fetch):
        super().__init__()
        self.fetch = fetch

    def first(self, key, after):
        self[key] = self.fetch(key, after)
        return self[key]


def _local_step(x, tgt, mod, p, fetch, F, scatter=None):
    S, D = x.shape
    GW, HW = p["ln_g"].shape[1], p["hg_ng"].shape[1]
    G, T, _ = p["ws"].shape
    w = _Fetched(fetch)
    INW = 2 * GW + 4 * HW + 2 * D
    in_loc, br_loc, fi_loc = INW // N_DEV, D // N_DEV, 2 * F // N_DEV
    assert GW == HW and F % fi_loc == 0
    sh1, sc1, gt1, sh2, sc2, gt2 = (mod[:, k * D:(k + 1) * D] for k in range(6))
    bsb = jnp.broadcast_to(p["bs"][:, :, None], (G, T, GW // G))

    tm = _tile(S, 1024, 16)
    tmh = _tile(S, 512, 16)
    tn_in = _tile(in_loc, 1280)
    tn_d = _tile(D, 512)
    tn_br = _tile(br_loc, 512)
    tk_s = S
    tm_w = _tile(D, 1024)
    g_off = 2 * GW + 4 * HW

    h1 = _norm_mod("norm1", x, p["norm1_g"], sc1, sh1)
    z = _mm_nn_stacked("proj_in", h1, w.first("in", h1), tm=tm, tn=tn_in, tk=D)[0]
    ya = _gmlp_fwd(z, p["ln_g"], p["ln_b"], p["ws"], bsb, GW)
    yb, o_hg, states = _hg_fwd(z, p["hg_lb"], p["hg_ng"], HW)
    flat = {k: jnp.swapaxes(w.first(k, yb), 0, 1).reshape(GW, D) for k in ("bg", "bh")}
    tn_f = _tile(D, 1024)
    pa = _matmul(
        "branch_gmlp", ya, flat["bg"], dims=_NN, grid_mnk=(S // tm, D // tn_f, 1), tiles=(tm, tn_f),
        a_spec=pl.BlockSpec((tm, GW), lambda i, j, k: (i, 0)), b_spec=pl.BlockSpec((GW, tn_f), lambda i, j, k: (0, j)),
        out_shapes=[jax.ShapeDtypeStruct((S, D), F32)], out_specs=[pl.BlockSpec((tm, tn_f), lambda i, j, k: (i, j))], epilogue=_store(F32))[0]
    t_fi = w.first("fi_early", pa)

    def gates(ga_ref, gb_ref, ba_ref, bb_ref):
        return _sigmoid(ga_ref[...] + ba_ref[...]), _sigmoid(gb_ref[...] + bb_ref[...])

    def gate_specs(tn_, tm_=tm):
        o1, o2 = g_off // tn_, (g_off + D) // tn_
        return [pl.BlockSpec((tm_, tn_), lambda i, j, k: (i, o1 + j)), pl.BlockSpec((tm_, tn_), lambda i, j, k: (i, o2 + j)),
                pl.BlockSpec((1, tn_), lambda i, j, k: (0, j)), pl.BlockSpec((1, tn_), lambda i, j, k: (0, D // tn_ + j))]

    def merge_ep(acc, ex, outs):
        ga, gb = gates(*ex[1:5])
        outs[0][...] = acc.astype(BF16)
        outs[1][...] = (ga * ex[0][...] + gb * acc).astype(BF16)

    tile_o = pl.BlockSpec((tmh, tn_f), lambda i, j, k: (i, j))
    pb, y = _matmul(
        "branch_hg_merge", yb, flat["bh"], dims=_NN, grid_mnk=(S // tmh, D // tn_f, 1), tiles=(tmh, tn_f),
        a_spec=pl.BlockSpec((tmh, HW), lambda i, j, k: (i, 0)), b_spec=pl.BlockSpec((HW, tn_f), lambda i, j, k: (0, j)),
        extras=[pa, z, z, p["b_gate"], p["b_gate"]], extra_specs=[tile_o, *gate_specs(tn_f, tmh)],
        out_shapes=[jax.ShapeDtypeStruct((S, D), BF16), jax.ShapeDtypeStruct((S, D), BF16)], out_specs=[tile_o, tile_o],
        epilogue=merge_ep, after=t_fi)

    def resid_ep(acc, ex, outs):
        outs[0][...] = acc.astype(BF16)
        outs[1][...] = ex[0][...] + ex[1][...] * acc

    def resid_mm(name, a, b, res, gt, tm_, tn_):
        K = a.shape[1]
        t_o = pl.BlockSpec((tm_, tn_), lambda i, j, k: (i, j))
        return _matmul(
            name, a, b, dims=_NN, grid_mnk=(S // tm_, D // tn_, 1), tiles=(tm_, tn_),
            a_spec=pl.BlockSpec((tm_, K), lambda i, j, k: (i, 0)), b_spec=pl.BlockSpec((K, tn_), lambda i, j, k: (0, j)),
            extras=[res, gt], extra_specs=[t_o, pl.BlockSpec((1, tn_), lambda i, j, k: (0, j))],
            out_shapes=[jax.ShapeDtypeStruct((S, D), BF16), jax.ShapeDtypeStruct((S, D), F32)], out_specs=[t_o, t_o], epilogue=resid_ep)

    o1, xm = resid_mm("proj_out", y, w.first("out", z), x, gt1, tm, tn_f)
    h2 = _norm_mod("norm2", xm, p["norm2_g"], sc2, sh2)
    hf, hf_fac = _ffn_in_swiglu(h2, w.first("fi", h2))
    o2, x3 = resid_mm("ffn_out", hf, w.first("fo", hf), xm, gt2, tmh, tn_d)
    dx3, do2, vec_l = _loss_head(x3, tgt, p["final_g"], o2, gt2)

    nf = F // fi_loc

    def dswiglu_ep(acc, ex, outs):
        outs[0][0] = (acc * ex[0][0].astype(F32)).astype(BF16)
        outs[0][1] = (acc * ex[0][1].astype(F32)).astype(BF16)

    pair = pl.BlockSpec((2, tmh, fi_loc), lambda i, j, k: (0, i, j))
    dab = _matmul(
        "ffn_out_dx", do2, w["fo"], dims=_NT, grid_mnk=(S // tmh, nf, 1), tiles=(tmh, fi_loc),
        a_spec=pl.BlockSpec((tmh, D), lambda i, j, k: (i, 0)), b_spec=pl.BlockSpec((fi_loc, D), lambda i, j, k: (j, 0)),
        extras=[hf_fac], extra_specs=[pair], out_shapes=[jax.ShapeDtypeStruct((2, S, F), BF16)], out_specs=[pair],
        epilogue=dswiglu_ep)[0]
    start = (lambda name, grads: scatter[0](name, grads)) if scatter is not None else (lambda name, grads: None)
    push = (lambda name, after: scatter[1](name, after)) if scatter is not None else (lambda name, after: None)

    def zero(token):
        return 0.0 if token is None else token[0:1, 0:1]

    tm_f = _tile(F, 512)
    g_fo = _mm_tn("ffn_out_dw", hf, do2, pl.BlockSpec((tk_s, D), lambda i, j, k: (k, j)), Mo=F, No=D, S=S, tm=tm_f, tn=D, tk=tk_s)
    g_fi = _mm_tn("ffn_in_dw", h2, dab, pl.BlockSpec((None, tk_s, fi_loc), lambda i, j, k: (j // nf, k, j % nf)),
                  Mo=D, No=2 * F, S=S, tm=D, tn=fi_loc, tk=tk_s, stacked_nloc=fi_loc, after=g_fo)
    t_ffn = start("scatter_ffn", dict(fo=g_fo, fi=g_fi))
    dh2 = _mm_nt_stacked("ffn_in_dx", pl.BlockSpec((None, tm, fi_loc), lambda i, j, k: (k // nf, i, k % nf)), dab, w["fi"],
                         M=S, tm=tm, tn=D, tk=fi_loc, after=t_ffn)
    dxm, vec2, do1 = _norm_mod_bwd("norm2_bwd", dh2, xm, p["norm2_g"], sc2, dx3, o1, gt1)
    t_ffn = push("scatter_ffn", dxm)

    def dmerge_ep(acc, ex, outs):
        ga, gb = gates(*ex[2:6])
        outs[0][...] = (acc * ga).astype(BF16)
        outs[1][...] = (acc * gb).astype(BF16)
        outs[2][0] = (acc * ex[0][...] * ga * (1.0 - ga)).astype(BF16)
        outs[2][1] = (acc * ex[1][...] * gb * (1.0 - gb)).astype(BF16)

    t_o = pl.BlockSpec((tm, tn_d), lambda i, j, k: (i, j))
    dpa, dpb, dg2 = _matmul(
        "proj_out_dx", do1, w["out"], dims=_NT, grid_mnk=(S // tm, D // tn_d, 1), tiles=(tm, tn_d),
        a_spec=pl.BlockSpec((tm, D), lambda i, j, k: (i, 0)), b_spec=pl.BlockSpec((tn_d, D), lambda i, j, k: (j, 0)),
        extras=[pa, pb, z, z, p["b_gate"], p["b_gate"]], extra_specs=[t_o, t_o, *gate_specs(tn_d)],
        out_shapes=[jax.ShapeDtypeStruct((S, D), BF16), jax.ShapeDtypeStruct((S, D), BF16), jax.ShapeDtypeStruct((2, S, D), BF16)],
        out_specs=[t_o, t_o, pl.BlockSpec((2, tm, tn_d), lambda i, j, k: (0, i, j))], epilogue=dmerge_ep, after=t_ffn)
    g_out = _mm_tn("proj_out_dw", y, do1, pl.BlockSpec((tk_s, D), lambda i, j, k: (k, j)), Mo=D, No=D, S=S, tm=tn_d, tn=D, tk=tk_s)
    tn_g = _tile(GW, 512)
    b_br = pl.BlockSpec((tk_s, br_loc), lambda i, j, k: (k, j))
    tm_b = _tile(GW, 1024)
    g_bg = _mm_tn("branch_gmlp_dw", ya, dpa, b_br, Mo=GW, No=D, S=S, tm=tm_b, tn=br_loc, tk=tk_s, stacked_nloc=br_loc)
    g_bh = _mm_tn("branch_hg_dw", yb, dpb, b_br, Mo=HW, No=D, S=S, tm=tm_b, tn=br_loc, tk=tk_s, stacked_nloc=br_loc)
    t_mix = start("scatter_mixer", dict(out=g_out, bg=g_bg, bh=g_bh))
    def branch_dx(name, dp, w_flat):
        return _matmul(
            name, dp, w_flat, dims=_NT, grid_mnk=(S // tm, GW // tn_g, 1), tiles=(tm, tn_g),
            a_spec=pl.BlockSpec((tm, D), lambda i, j, k: (i, 0)), b_spec=pl.BlockSpec((tn_g, D), lambda i, j, k: (j, 0)),
            out_shapes=[jax.ShapeDtypeStruct((S, GW), F32)], out_specs=[pl.BlockSpec((tm, tn_g), lambda i, j, k: (i, j))],
            epilogue=_store(F32), after=t_mix)[0]

    dya = branch_dx("branch_gmlp_dx", dpa, flat["bg"])
    dyb = branch_dx("branch_hg_dx", dpb, flat["bh"])
    dz_gmlp, dln, dws, dbs = _gmlp_bwd(z, dya, p["ln_g"], p["ln_b"], p["ws"], bsb, GW)
    t_mix = push("scatter_mixer", dz_gmlp)
    dz, dng, dhlb, db_gate = _hg_bwd(z, o_hg, states, dyb, p["hg_lb"], p["hg_ng"] + zero(t_mix), HW, dz_gmlp, dg2)
    half = D // 2
    tm_h = _tile(half, 1024)
    g_in = []
    t_in = None
    for hname, h in (("a", 0), ("b", 1)):
        g_in.append(_mm_tn("proj_in_dw_" + hname, h1, dz, pl.BlockSpec((tk_s, in_loc), lambda i, j, k: (k, j)), Mo=half, No=INW, S=S,
                           tm=tm_h, tn=in_loc, tk=tk_s, stacked_nloc=in_loc, after=t_in, a_off=h * (half // tm_h)))
        t_in = start("scatter_proj_in_" + hname, {"w_in_" + hname: g_in[-1]})
    t_in = push("scatter_proj_in_a", t_in)
    dh1 = _mm_nt_stacked("proj_in_dx", pl.BlockSpec((tm, in_loc), lambda i, j, k: (i, k)), dz, w["in"], M=S, tm=tm, tn=D, tk=in_loc,
                         after=t_in)
    dx, vec1 = _norm_mod_bwd("norm1_bwd", dh1, x, p["norm1_g"], sc1, dxm)

    dmod = jnp.concatenate([vec1[0:1], vec1[1:2], vec2[3:4], vec2[0:1], vec2[1:2], vec_l[2:3]], axis=1)
    small = dict(norm1_g=vec1[2:3], b_gate=db_gate.reshape(1, 2 * D), ln_g=dln[0:1], ln_b=dln[1:2], ws=dws, bs=dbs.reshape(G, T),
                 hg_lb=dhlb, hg_ng=dng[0:1], norm2_g=vec2[2:3], final_g=vec_l[1:2], loss=vec_l[0:1, 0:LANES])
    big = dict(w_in_a=g_in[0], w_in_b=g_in[1], bg=g_bg, bh=g_bh, out=g_out, fi=g_fi, fo=g_fo)
    return dx, big, small, dmod


_SMALL = ("b_ada", "norm1_g", "b_gate", "ln_g", "ln_b", "ws", "bs", "hg_lb", "hg_ng", "norm2_g", "final_g")


def _pack(parts, rows_mult=8):
    flat = [a.reshape(-1) for a in parts]
    offs, n = [], 0
    for a in flat:
        offs.append(n)
        n += a.shape[0]
    pad = (-n) % (LANES * rows_mult)
    if pad:
        flat.append(jnp.zeros((pad,), F32))
    return jnp.concatenate(flat).reshape(-1, LANES), offs


def kernel(x, c, w_ada, b_ada, norm1_g, w_in, b_gate, gmlp_ln_g, gmlp_ln_b, gmlp_ws, gmlp_bs, hg_lb, hg_norm_g, w_branch_gmlp, w_branch_hg, w_out, norm2_g, w_ffn_in, w_ffn_out, final_norm_g, loss_target, m_w_ada, m_b_ada, m_norm1_g, m_w_in, m_b_gate, m_gmlp_ln_g, m_gmlp_ln_b, m_gmlp_ws, m_gmlp_bs, m_hg_lb, m_hg_norm_g, m_w_branch_gmlp, m_w_branch_hg, m_w_out, m_norm2_g, m_w_ffn_in, m_w_ffn_out, m_final_norm_g, v_w_ada, v_b_ada, v_norm1_g, v_w_in, v_b_gate, v_gmlp_ln_g, v_gmlp_ln_b, v_gmlp_ws, v_gmlp_bs, v_hg_lb, v_hg_norm_g, v_w_branch_gmlp, v_w_branch_hg, v_w_out, v_norm2_g, v_w_ffn_in, v_w_ffn_out, v_final_norm_g):
    S, D = x.shape[1], x.shape[2]
    ada_loc = w_ada.shape[2]
    me = 4 * lax.axis_index("x") + 2 * lax.axis_index("y") + lax.axis_index("c")
    me_idx = me.astype(jnp.int32).reshape(1)

    def empty_hbm(shape, dtype):
        return pltpu.with_memory_space_constraint(lax.empty(shape, dtype), pltpu.HBM)

    groups = dict(gather_in=dict(keys=["in"], src=[w_in], forward=True),
                  gather_mixer=dict(keys=["bg", "bh", "out"], src=[w_branch_gmlp, w_branch_hg, w_out], forward=False),
                  gather_ffn_in=dict(keys=["fi"], src=[w_ffn_in], forward=True),
                  gather_ffn_out=dict(keys=["fo"], src=[w_ffn_out], forward=False))
    group_of = {k: gname for gname, g in groups.items() for k in g["keys"]}

    def first_hop(gname, after):
        g = groups[gname]
        n = len(g["keys"])
        cast = [_cast_shard(f"{gname}_cast_{k}", a[0], me_idx) for k, a in zip(g["keys"], g["src"])]
        shards, outs = [s for s, _ in cast], [o for _, o in cast]
        if g["forward"]:
            *g["hop"], token = _split_start(gname + "_hop1", shards + outs, n * 3, _forward_first_copies(n), after=after)
        else:
            *g["hop"], token = _split_start(gname + "_hop1", shards + outs, n * N_CHIP, _gather_first_copies(n), after=after)
        return token

    def second_hop(gname, after):
        g = groups[gname]
        n = len(g["keys"])
        *g["hop"], token = _split_relay(gname + "_hop2", g["hop"][2], g["hop"][0], g["hop"][1], after,
                                        _forward_first_copies(n), n * 4, _forward_second_copies(n))
        return token

    def finish(gname, after):
        g = groups[gname]
        n = len(g["keys"])
        send_sems, recv_sems, bufs = g["hop"]
        if g["forward"]:
            send_sems, recv_sems, bufs, _ = _split_relay(gname + "_hop3", bufs, send_sems, recv_sems, after,
                                                         _forward_second_copies(n), n, _forward_third_copies(n))
            bufs = _split_wait(gname + "_wait", bufs, send_sems, recv_sems, after, _forward_third_copies(n))
        else:
            send_sems, recv_sems, bufs, _ = _split_relay(gname + "_relay", bufs, send_sems, recv_sems, after,
                                                         _gather_first_copies(n), n * (N_CHIP - 1), _gather_relay_copies(n))
            bufs = _split_wait(gname + "_wait", bufs, send_sems, recv_sems, after, _gather_relay_copies(n))
        g["done"] = dict(zip(g["keys"], bufs[n:]))

    c_all = _allgather_small("gather_c", c.reshape(D // LANES, LANES)).reshape(N_DEV, D)
    token = first_hop("gather_in", c_all)
    mod_cols, c_act = _ada_mod(jnp.pad(c_all, ((0, 16 - N_DEV), (0, 0))) + token[0:1, 0:1], w_ada[0])
    mod_vec = mod_cols[:N_DEV].reshape(-1, LANES)
    mg_send, mg_recv, mg_bufs, token = _split_start(
        "gather_mod_start", [mod_vec, lax.dynamic_update_slice(lax.empty((N_DEV, *mod_vec.shape), F32), mod_vec[None], (me, 0, 0))],
        N_DEV - 1, _small_gather_copies)
    token = second_hop("gather_in", token)
    token = first_hop("gather_ffn_in", first_hop("gather_mixer", token))
    mod_all = _split_wait("gather_mod_wait", mg_bufs, mg_send, mg_recv, token, _small_gather_copies)[1].reshape(N_DEV, N_DEV, ada_loc)
    mod = lax.dynamic_index_in_dim(mod_all, me, axis=1, keepdims=False).reshape(1, N_DEV * ada_loc) + b_ada

    def fetch(key, after):
        if key == "in":
            finish("gather_in", after)
        elif key == "fi_early":
            return first_hop("gather_ffn_out", second_hop("gather_ffn_in", after))
        elif "done" not in groups[group_of[key]]:
            finish(group_of[key], after)
        arr = groups[group_of[key]]["done"][key]
        return arr.reshape(-1, D) if key in ("out", "fo") else arr

    p = dict(norm1_g=norm1_g, b_gate=b_gate, ln_g=gmlp_ln_g, ln_b=gmlp_ln_b, ws=gmlp_ws[0], bs=gmlp_bs[0], hg_lb=hg_lb,
             hg_ng=hg_norm_g, norm2_g=norm2_g, final_g=final_norm_g.reshape(1, D))

    in_flight = {}
    c_idx = lax.axis_index("c").astype(jnp.int32).reshape(1)
    chip_idx = (2 * lax.axis_index("x") + lax.axis_index("y")).astype(jnp.int32).reshape(1)

    def scatter_start(name, grads):
        keys = list(grads)
        n = len(keys)
        stacks = [grads[k].reshape(N_DEV, -1, grads[k].shape[-1]) for k in keys]
        lands = [empty_hbm((N_CHIP, *g.shape[1:]), g.dtype) for g in stacks]
        send_sems, recv_sems, bufs, token = _split_start(name + "_d2d", stacks + lands, n * N_CHIP, _to_sibling_copies(n))
        in_flight[name] = dict(keys=keys, stage1=(send_sems, recv_sems, bufs))
        return token

    def scatter_push(name, after):
        f = in_flight[name]
        n = len(f["keys"])
        send_sems, recv_sems, bufs = f["stage1"]
        bufs = _split_wait(name + "_d2d_wait", bufs, send_sems, recv_sems, after, _to_sibling_copies(n))
        sums = [_chip_sum(f"{name}_sum_{k}", bufs[i], bufs[n + i], c_idx) for i, k in enumerate(f["keys"])]
        lands = [empty_hbm((N_CHIP - 1, *s.shape[1:]), s.dtype) for s in sums]
        send_sems, recv_sems, bufs, token = _split_start(name + "_ici", sums + lands, n * (N_CHIP - 1), _to_owner_copies(n))
        f["stage2"] = (send_sems, recv_sems, bufs)
        return token

    grad_x, _, small, dmod = _local_step(x[0], loss_target[0], mod, p, fetch, w_ffn_out.shape[1] * N_DEV, (scatter_start, scatter_push))

    small["b_ada"] = dmod
    packed, offs = _pack([small[k] for k in _SMALL] + [small["loss"]])
    sg_send, sg_recv, sg_bufs, t_tail = _split_start(
        "gather_small_start", [packed, lax.dynamic_update_slice(lax.empty((N_DEV, *packed.shape), F32), packed[None], (me, 0, 0))],
        N_DEV - 1, _small_gather_copies)
    t_tail = scatter_push("scatter_proj_in_b", t_tail)
    big_w = dict(w_in=(w_in, m_w_in, v_w_in, "w_in"), bg=(w_branch_gmlp, m_w_branch_gmlp, v_w_branch_gmlp, "w_branch_gmlp"),
                 bh=(w_branch_hg, m_w_branch_hg, v_w_branch_hg, "w_branch_hg"), out=(w_out, m_w_out, v_w_out, "w_out"),
                 fi=(w_ffn_in, m_w_ffn_in, v_w_ffn_in, "w_ffn_in"), fo=(w_ffn_out, m_w_ffn_out, v_w_ffn_out, "w_ffn_out"))
    upd = {}

    def land_and_update(name, after):
        keys = in_flight[name]["keys"]
        n = len(keys)
        send_sems, recv_sems, bufs = in_flight[name]["stage2"]
        bufs = _split_wait(name + "_ici_wait", bufs, send_sems, recv_sems, after, _to_owner_copies(n))
        for i, k in enumerate(keys):
            if k in big_w:
                wt, mt, vt, out_name = big_w[k]
                upd[out_name] = _adamw("adamw_" + out_name, wt[0], mt[0], vt[0], bufs[i], chip_idx, [bufs[n + i]])
            else:
                wt, mt, vt, out_name = big_w["w_in"]
                upd[out_name] = _adamw("adamw_" + k, wt[0], mt[0], vt[0], bufs[i], chip_idx, [bufs[n + i]],
                                       row0=0 if k == "w_in_a" else bufs[i].shape[1], into=upd.get(out_name))
            after = upd[out_name][1]
        return after

    after = land_and_update("scatter_mixer", land_and_update("scatter_ffn", t_tail))
    gathered = _split_wait("gather_small_wait", sg_bufs, sg_send, sg_recv, after, _small_gather_copies)[1]
    wp = dict(p, b_ada=b_ada)
    ms = dict(b_ada=m_b_ada, norm1_g=m_norm1_g, b_gate=m_b_gate, ln_g=m_gmlp_ln_g, ln_b=m_gmlp_ln_b, ws=m_gmlp_ws, bs=m_gmlp_bs,
              hg_lb=m_hg_lb, hg_ng=m_hg_norm_g, norm2_g=m_norm2_g, final_g=m_final_norm_g)
    vs = dict(b_ada=v_b_ada, norm1_g=v_norm1_g, b_gate=v_b_gate, ln_g=v_gmlp_ln_g, ln_b=v_gmlp_ln_b, ws=v_gmlp_ws, bs=v_gmlp_bs,
              hg_lb=v_hg_lb, hg_ng=v_hg_norm_g, norm2_g=v_norm2_g, final_g=v_final_norm_g)
    w_sm, _ = _pack([wp[k] for k in _SMALL])
    m_sm, _ = _pack([ms[k] for k in _SMALL])
    v_sm, _ = _pack([vs[k] for k in _SMALL])
    shapes = dict(b_ada=b_ada.shape, norm1_g=norm1_g.shape, b_gate=b_gate.shape, ln_g=gmlp_ln_g.shape, ln_b=gmlp_ln_b.shape,
                  ws=gmlp_ws.shape, bs=gmlp_bs.shape, hg_lb=hg_lb.shape, hg_ng=hg_norm_g.shape, norm2_g=norm2_g.shape,
                  final_g=final_norm_g.shape)
    sm_out = _small_update(gathered, w_sm, m_sm, v_sm, after, [math.prod(shapes[k]) // LANES for k in _SMALL])

    def unpack(idx, k):
        return sm_out[idx * len(_SMALL) + _SMALL.index(k)].reshape(shapes[k])

    loss = sm_out[-1][0, 0]

    assert offs[0] == 0 and ada_loc % LANES == 0
    dmod_loc = lax.dynamic_slice_in_dim(gathered, me * (ada_loc // LANES), ada_loc // LANES, axis=1).reshape(N_DEV, ada_loc)
    ca_t = jnp.pad(c_act[:N_DEV].T, ((0, 0), (0, LANES - N_DEV))).astype(BF16)
    dm_p = jnp.pad(dmod_loc, ((0, LANES - N_DEV), (0, 0))).astype(BF16)
    tm_a = _tile(D, 512)
    g_ada = _matmul(
        "ada_dw", ca_t, dm_p, dims=_NN, grid_mnk=(D // tm_a, 1, 1), tiles=(tm_a, ada_loc),
        a_spec=pl.BlockSpec((tm_a, LANES), lambda i, j, k: (i, 0)), b_spec=pl.BlockSpec((LANES, ada_loc), lambda i, j, k: (0, 0)),
        out_shapes=[jax.ShapeDtypeStruct((1, D, ada_loc), F32)], out_specs=[pl.BlockSpec((None, tm_a, ada_loc), lambda i, j, k: (0, i, 0))],
        epilogue=_store(F32))[0]
    upd["w_ada"] = _adamw("adamw_w_ada", w_ada[0], m_w_ada[0], v_w_ada[0], g_ada, jnp.zeros((1,), jnp.int32))
    land_and_update("scatter_proj_in_b", land_and_update("scatter_proj_in_a", upd["w_ada"][1]))

    order = ("w_ada", "b_ada", "norm1_g", "w_in", "b_gate", "ln_g", "ln_b", "ws", "bs", "hg_lb", "hg_ng", "w_branch_gmlp", "w_branch_hg",
             "w_out", "norm2_g", "w_ffn_in", "w_ffn_out", "final_g")
    outs = [loss, grad_x[None]]
    for idx in range(4):
        for k in order:
            outs.append(upd[k][idx][None] if k in upd else unpack(idx, k))
    return tuple(outs)
```

```python
import functools
import math

import jax
import jax.numpy as jnp
from jax import lax
from jax.experimental import pallas as pl
from jax.experimental.pallas import tpu as pltpu

F32 = jnp.float32
BF16 = jnp.bfloat16
N_DEV = 8
EPS = 1e-6
LANES = 128
HG_DK = 128
HG_CHUNK = 64
HG_MID = HG_CHUNK // 2 - 1
EXP_CLAMP = 80.0
VMEM_LIMIT = 48 * 1024 * 1024
BF16_ROWS = 16
STREAM_TILE = 1 << 20
ADAMW_TILE = 3 << 17
ADAM_LR, ADAM_B1, ADAM_B2, ADAM_EPS, ADAM_WD, ADAM_STEP = 0.001, 0.9, 0.999, 1e-08, 0.01, 10
MESH = pl.DeviceIdType.MESH

_NN = (((1,), (0,)), ((), ()))
_NT = (((1,), (1,)), ((), ()))
_TN = (((0,), (0,)), ((), ()))


def _dot(a, b, dims=_NN):
    return lax.dot_general(a.astype(BF16), b.astype(BF16), dims, preferred_element_type=F32)


def _tile(n, target, mult=LANES):
    best = None
    for t in range(mult, min(n, target) + 1, mult):
        if n % t == 0:
            best = t
    return n if best is None else best


def _cparams(sem):
    return pltpu.CompilerParams(dimension_semantics=sem, vmem_limit_bytes=VMEM_LIMIT)


def _sigmoid(x):
    return 1.0 / (1.0 + jnp.exp(-x))


def _gelu_parts(x):
    k0 = math.sqrt(2.0 / math.pi)
    x2 = x * x
    t = jnp.tanh(k0 * (x + 0.044715 * x * x2))
    g = 0.5 * x * (1.0 + t)
    dg = 0.5 * (1.0 + t) + 0.5 * x * (1.0 - t * t) * (k0 * (1.0 + 3.0 * 0.044715 * x2))
    return g, dg


def _split3(x):
    h = x.astype(BF16)
    r = x - h.astype(F32)
    m = r.astype(BF16)
    lo = (r - m.astype(F32)).astype(BF16)
    return h, m, lo


def _ones_dot(mat01, x):
    h, m, lo = _split3(x)
    d = functools.partial(lax.dot_general, dimension_numbers=_NN, preferred_element_type=F32)
    return d(mat01, h) + d(mat01, m) + d(mat01, lo)


def _matmul(name, a, b, *, dims, grid_mnk, tiles, a_spec, b_spec, extras=(), extra_specs=(), out_shapes, out_specs, epilogue, after=None,
            sem=None):
    gm, gn, nk = grid_mnk
    tm, tn = tiles
    n_ex, n_out = len(extras), len(out_shapes)
    held = [] if after is None else [after]

    def body(*refs):
        a_ref, b_ref = refs[0], refs[1]
        ex = refs[2:2 + n_ex]
        outs = refs[2 + n_ex + len(held):2 + n_ex + len(held) + n_out]
        more = () if sem is None else (pl.program_id(0) == 0,)
        if nk == 1:
            epilogue(lax.dot_general(a_ref[...], b_ref[...], dims, preferred_element_type=F32), ex, outs, *more)
            return
        acc = refs[-1]
        k = pl.program_id(2)

        @pl.when(k == 0)
        def _():
            acc[...] = jnp.zeros_like(acc)

        acc[...] += lax.dot_general(a_ref[...], b_ref[...], dims, preferred_element_type=F32)

        @pl.when(k == nk - 1)
        def _():
            epilogue(acc[...], ex, outs, *more)

    return pl.pallas_call(
        body, name=name, grid=(gm, gn, nk), in_specs=[a_spec, b_spec, *extra_specs] + [pl.BlockSpec(memory_space=pl.ANY)] * len(held),
        out_specs=list(out_specs), out_shape=list(out_shapes), scratch_shapes=[] if nk == 1 else [pltpu.VMEM((tm, tn), F32)],
        compiler_params=_cparams(sem or ("parallel", "parallel", "arbitrary")),
    )(a, b, *extras, *held)


def _store(dtype):
    def ep(acc, ex, outs):
        outs[0][...] = acc.astype(dtype)
    return ep


def _mm_nn_stacked(name, a, wg, *, tm, tn, tk, out_dtype=F32, extras=(), extra_specs=(), out_shapes=None, out_specs=None, epilogue=None,
                   after=None):
    M, K = a.shape
    _, _, nloc = wg.shape
    N = nloc * N_DEV
    q = nloc // tn
    if out_shapes is None:
        out_shapes = [jax.ShapeDtypeStruct((M, N), out_dtype)]
        out_specs = [pl.BlockSpec((tm, tn), lambda i, j, k: (i, j))]
        epilogue = _store(out_dtype)
    return _matmul(
        name, a, wg, dims=_NN, grid_mnk=(M // tm, N // tn, K // tk), tiles=(tm, tn),
        a_spec=pl.BlockSpec((tm, tk), lambda i, j, k: (i, k)),
        b_spec=pl.BlockSpec((None, tk, tn), lambda i, j, k: (j // q, k, j % q)),
        extras=extras, extra_specs=extra_specs, out_shapes=out_shapes, out_specs=out_specs, epilogue=epilogue, after=after)


def _mm_nt_stacked(name, a_spec, a, wg, *, M, tm, tn, tk, out_dtype=F32, after=None, extras=(), extra_specs=(), out_shapes=None,
                   out_specs=None, epilogue=None, sem=None):
    _, Kw, nloc = wg.shape
    q = nloc // tk
    single = out_shapes is None
    if single:
        out_shapes = [jax.ShapeDtypeStruct((M, Kw), out_dtype)]
        out_specs = [pl.BlockSpec((tm, tn), lambda i, j, k: (i, j))]
        epilogue = _store(out_dtype)
    res = _matmul(
        name, a, wg, dims=_NT, grid_mnk=(M // tm, Kw // tn, (nloc * N_DEV) // tk), tiles=(tm, tn),
        a_spec=a_spec, b_spec=pl.BlockSpec((None, tn, tk), lambda i, j, k: (k // q, j, k % q)),
        extras=extras, extra_specs=extra_specs, out_shapes=out_shapes, out_specs=out_specs, epilogue=epilogue, after=after, sem=sem)
    return res[0] if single else res


def _mm_tn(name, a, b, b_spec, *, Mo, No, S, tm, tn, tk, stacked_nloc=None, after=None, a_off=0):
    if stacked_nloc is None:
        out_shape = jax.ShapeDtypeStruct((Mo, No), BF16)
        out_spec = pl.BlockSpec((tm, tn), lambda i, j, k: (i, j))
    else:
        q = stacked_nloc // tn
        out_shape = jax.ShapeDtypeStruct((N_DEV, Mo, stacked_nloc), BF16)
        out_spec = pl.BlockSpec((None, tm, tn), lambda i, j, k: (j // q, i, j % q))
    return _matmul(
        name, a, b, dims=_TN, grid_mnk=(Mo // tm, No // tn, S // tk), tiles=(tm, tn),
        a_spec=pl.BlockSpec((tk, tm), lambda i, j, k: (k, i + a_off)), b_spec=b_spec,
        out_shapes=[out_shape], out_specs=[out_spec], epilogue=_store(BF16), after=after)[0]


def _norm_mod(name, x, g, sc, sh):
    S, D = x.shape
    tm = _tile(S, 256, 8)

    def body(x_ref, g_ref, sc_ref, sh_ref, h_ref):
        xv = x_ref[...]
        r = lax.rsqrt(jnp.mean(xv * xv, axis=-1, keepdims=True) + EPS)
        h = (xv * r) * g_ref[...]
        h_ref[...] = (h * (1.0 + sc_ref[...]) + sh_ref[...]).astype(BF16)

    row = pl.BlockSpec((tm, D), lambda i: (i, 0))
    vec = pl.BlockSpec((1, D), lambda i: (0, 0))
    return pl.pallas_call(body, name=name, grid=(S // tm,), in_specs=[row, vec, vec, vec], out_specs=row,
                          out_shape=jax.ShapeDtypeStruct((S, D), BF16), compiler_params=_cparams(("parallel",)))(x, g, sc, sh)


def _norm_mod_bwd_rows(first, dh_v, x_ref, g_ref, sc_ref, dres_ref, dx_ref, vec_ref, o_ref=None, gt_ref=None, do_ref=None):
    @pl.when(first)
    def _():
        vec_ref[...] = jnp.zeros_like(vec_ref)

    xv, gv = x_ref[...], g_ref[...]
    r = lax.rsqrt(jnp.mean(xv * xv, axis=-1, keepdims=True) + EPS)
    xn = xv * r
    one_sc = 1.0 + sc_ref[...]
    vec_ref[0:1, :] += jnp.sum(dh_v, axis=0, keepdims=True)
    vec_ref[1:2, :] += jnp.sum(dh_v * (xn * gv), axis=0, keepdims=True)
    vec_ref[2:3, :] += jnp.sum(dh_v * one_sc * xn, axis=0, keepdims=True)
    dxn = dh_v * one_sc * gv
    dx = dres_ref[...] + r * (dxn - xn * jnp.mean(dxn * xn, axis=-1, keepdims=True))
    dx_ref[...] = dx
    if o_ref is not None:
        vec_ref[3:4, :] += jnp.sum(dx * o_ref[...], axis=0, keepdims=True)
        do_ref[...] = (dx * gt_ref[...]).astype(BF16)


def _norm_mod_bwd(name, dh, x, g, sc, dres, o=None, gt=None, after=None):
    S, D = x.shape
    tm = _tile(S, 256, 8)
    gated = o is not None
    held = [] if after is None else [after]

    def body(*refs):
        if gated:
            dh_ref, x_ref, g_ref, sc_ref, dres_ref, o_ref, gt_ref = refs[:7]
            dx_ref, vec_ref, do_ref = refs[7 + len(held):]
        else:
            dh_ref, x_ref, g_ref, sc_ref, dres_ref = refs[:5]
            dx_ref, vec_ref = refs[5 + len(held):]
            o_ref = gt_ref = do_ref = None
        _norm_mod_bwd_rows(pl.program_id(0) == 0, dh_ref[...], x_ref, g_ref, sc_ref, dres_ref, dx_ref, vec_ref, o_ref, gt_ref, do_ref)

    row = pl.BlockSpec((tm, D), lambda i: (i, 0))
    vec = pl.BlockSpec((1, D), lambda i: (0, 0))
    acc = pl.BlockSpec((8, D), lambda i: (0, 0))
    ins = [dh, x, g, sc, dres] + ([o, gt] if gated else []) + held
    in_specs = [row, row, vec, vec, row] + ([row, vec] if gated else []) + [pl.BlockSpec(memory_space=pl.ANY)] * len(held)
    out_shape = [jax.ShapeDtypeStruct((S, D), F32), jax.ShapeDtypeStruct((8, D), F32)]
    out_specs = [row, acc]
    if gated:
        out_shape.append(jax.ShapeDtypeStruct((S, D), BF16))
        out_specs.append(row)
    return pl.pallas_call(body, name=name, grid=(S // tm,), in_specs=in_specs, out_specs=out_specs, out_shape=out_shape,
                          compiler_params=_cparams(("arbitrary",)))(*ins)


def _loss_head(x3, tgt, gf, o2, gt2):
    S, D = x3.shape
    tm = _tile(S, 256, 8)

    def body(x_ref, t_ref, g_ref, o_ref, gt_ref, dx_ref, do_ref, vec_ref):
        i = pl.program_id(0)

        @pl.when(i == 0)
        def _():
            vec_ref[...] = jnp.zeros_like(vec_ref)

        xv, gv = x_ref[...], g_ref[...]
        r = lax.rsqrt(jnp.mean(xv * xv, axis=-1, keepdims=True) + EPS)
        xn = xv * r
        e = xn * gv - t_ref[...]
        tok = 0.5 * jnp.mean(e * e, axis=-1, keepdims=True)
        vec_ref[0:1, :] += jnp.broadcast_to(jnp.sum(tok, axis=0, keepdims=True), (1, D))
        dy = e * (1.0 / D)
        vec_ref[1:2, :] += jnp.sum(dy * xn, axis=0, keepdims=True)
        dxn = dy * gv
        dx = r * (dxn - xn * jnp.mean(dxn * xn, axis=-1, keepdims=True))
        dx_ref[...] = dx
        vec_ref[2:3, :] += jnp.sum(dx * o_ref[...], axis=0, keepdims=True)
        do_ref[...] = (dx * gt_ref[...]).astype(BF16)

    row = pl.BlockSpec((tm, D), lambda i: (i, 0))
    vec = pl.BlockSpec((1, D), lambda i: (0, 0))
    return pl.pallas_call(
        body, name="loss_head", grid=(S // tm,), in_specs=[row, row, vec, row, vec],
        out_specs=[row, row, pl.BlockSpec((8, D), lambda i: (0, 0))],
        out_shape=[jax.ShapeDtypeStruct((S, D), F32), jax.ShapeDtypeStruct((S, D), BF16), jax.ShapeDtypeStruct((8, D), F32)],
        compiler_params=_cparams(("arbitrary",)))(x3, tgt, gf, o2, gt2)


def _ffn_in_swiglu(h, wg):
    S, D = h.shape
    _, _, tf = wg.shape
    nf = N_DEV // 2
    F = nf * tf
    tm = _tile(S, 256, 16)

    def body(h_ref, wa_ref, wu_ref, hf_ref, fac_ref):
        hv = h_ref[...]
        a = lax.dot_general(hv, wa_ref[...], _NN, preferred_element_type=F32)
        up = lax.dot_general(hv, wu_ref[...], _NN, preferred_element_type=F32)
        sa = _sigmoid(a)
        silu = a * sa
        hf_ref[...] = (silu * up).astype(BF16)
        fac_ref[0] = (up * (sa * (1.0 + a * (1.0 - sa)))).astype(BF16)
        fac_ref[1] = silu.astype(BF16)

    return pl.pallas_call(
        body, name="ffn_in_swiglu", grid=(nf, S // tm),
        in_specs=[pl.BlockSpec((tm, D), lambda j, i: (i, 0)), pl.BlockSpec((None, D, tf), lambda j, i: (j, 0, 0)),
                  pl.BlockSpec((None, D, tf), lambda j, i: (j + nf, 0, 0))],
        out_specs=[pl.BlockSpec((tm, tf), lambda j, i: (i, j)), pl.BlockSpec((2, tm, tf), lambda j, i: (0, i, j))],
        out_shape=[jax.ShapeDtypeStruct((S, F), BF16), jax.ShapeDtypeStruct((2, S, F), BF16)],
        compiler_params=_cparams(("parallel", "parallel")))(h, wg, wg)


def _gmlp_common(u_ref, v_ref, lg_ref, lb_ref, ws_ref, bsb_ref, G, T, Dg):
    ug, dug = _gelu_parts(u_ref[...])
    vg, dvg = _gelu_parts(v_ref[...])
    mu = jnp.mean(vg, axis=-1, keepdims=True)
    vc = vg - mu
    rstd = lax.rsqrt(jnp.mean(vc * vc, axis=-1, keepdims=True) + EPS)
    vhat = vc * rstd
    vn = vhat * lg_ref[...] + lb_ref[...]
    row = lax.broadcasted_iota(jnp.int32, (T, T), 0)
    col = lax.broadcasted_iota(jnp.int32, (T, T), 1)
    tril = row >= col
    s = []
    for g in range(G):
        w = jnp.where(tril, ws_ref[g], 0.0)
        s.append(_dot(w, vn[:, g * Dg:(g + 1) * Dg]) + bsb_ref[g])
    return ug, dug, dvg, rstd, vhat, vn, tril, s


def _gmlp_fwd(z, ln_g, ln_b, ws, bsb, GW):
    S = z.shape[0]
    G, T, _ = ws.shape
    Dg = GW // G

    def body(u_ref, v_ref, lg_ref, lb_ref, ws_ref, bsb_ref, ya_ref):
        ug, _, _, _, _, _, _, s = _gmlp_common(u_ref, v_ref, lg_ref, lb_ref, ws_ref, bsb_ref, G, T, Dg)
        for g in range(G):
            sl = slice(g * Dg, (g + 1) * Dg)
            ya_ref[:, sl] = (ug[:, sl] * s[g]).astype(BF16)

    vec = pl.BlockSpec((1, GW), lambda c: (0, 0))
    return pl.pallas_call(
        body, name="gmlp_fwd", grid=(S // T,),
        in_specs=[pl.BlockSpec((T, GW), lambda c: (c, 0)), pl.BlockSpec((T, GW), lambda c: (c, 1)), vec, vec,
                  pl.BlockSpec((G, T, T), lambda c: (0, 0, 0)), pl.BlockSpec((G, T, Dg), lambda c: (0, 0, 0))],
        out_specs=pl.BlockSpec((T, GW), lambda c: (c, 0)), out_shape=jax.ShapeDtypeStruct((S, GW), BF16),
        compiler_params=_cparams(("parallel",)))(z, z, ln_g, ln_b, ws, bsb)


def _gmlp_bwd(z, dya, ln_g, ln_b, ws, bsb, GW):
    S = z.shape[0]
    G, T, _ = ws.shape
    Dg = GW // G
    nc = S // T

    def body(u_ref, v_ref, dya_ref, lg_ref, lb_ref, ws_ref, bsb_ref, dz_ref, dln_ref, dws_ref, dbs_ref, dbs_acc, dvh):
        c = pl.program_id(0)

        @pl.when(c == 0)
        def _():
            dln_ref[...] = jnp.zeros_like(dln_ref)
            dws_ref[...] = jnp.zeros_like(dws_ref)
            dbs_acc[...] = jnp.zeros_like(dbs_acc)

        ug, dug, dvg, rstd, vhat, vn, tril, s = _gmlp_common(u_ref, v_ref, lg_ref, lb_ref, ws_ref, bsb_ref, G, T, Dg)
        dya_v = dya_ref[...]
        for g in range(G):
            sl = slice(g * Dg, (g + 1) * Dg)
            dy_g = dya_v[:, sl]
            dz_ref[:, sl] = (dy_g * s[g] * dug[:, sl]).astype(BF16)
            ds = dy_g * ug[:, sl]
            dbs_acc[g] += ds
            w = jnp.where(tril, ws_ref[g], 0.0)
            dvn_g = _dot(w, ds, _TN)
            dws_ref[g] += jnp.where(tril, _dot(ds, vn[:, sl], _NT), 0.0)
            dln_ref[0:1, sl] += jnp.sum(dvn_g * vhat[:, sl], axis=0, keepdims=True)
            dln_ref[1:2, sl] += jnp.sum(dvn_g, axis=0, keepdims=True)
            dvh[:, sl] = dvn_g * lg_ref[:, sl]
        dvhat = dvh[...]
        m1 = jnp.mean(dvhat, axis=-1, keepdims=True)
        m2 = jnp.mean(dvhat * vhat, axis=-1, keepdims=True)
        dz_ref[:, GW:2 * GW] = (rstd * (dvhat - m1 - vhat * m2) * dvg).astype(BF16)

        @pl.when(c == nc - 1)
        def _():
            for g in range(G):
                dbs_ref[g] = jnp.sum(dbs_acc[g], axis=-1, keepdims=True)

    vec = pl.BlockSpec((1, GW), lambda c: (0, 0))
    return pl.pallas_call(
        body, name="gmlp_bwd", grid=(nc,),
        in_specs=[pl.BlockSpec((T, GW), lambda c: (c, 0)), pl.BlockSpec((T, GW), lambda c: (c, 1)),
                  pl.BlockSpec((T, GW), lambda c: (c, 0)), vec, vec,
                  pl.BlockSpec((G, T, T), lambda c: (0, 0, 0)), pl.BlockSpec((G, T, Dg), lambda c: (0, 0, 0))],
        out_specs=[pl.BlockSpec((T, 2 * GW), lambda c: (c, 0)), pl.BlockSpec((8, GW), lambda c: (0, 0)),
                   pl.BlockSpec((G, T, T), lambda c: (0, 0, 0)), pl.BlockSpec((G, T, 1), lambda c: (0, 0, 0))],
        out_shape=[jax.ShapeDtypeStruct((S, 2 * GW), BF16), jax.ShapeDtypeStruct((8, GW), F32),
                   jax.ShapeDtypeStruct((G, T, T), F32), jax.ShapeDtypeStruct((G, T, 1), F32)],
        scratch_shapes=[pltpu.VMEM((G, T, Dg), F32), pltpu.VMEM((T, GW), F32)],
        compiler_params=_cparams(("arbitrary",)))(z, z, dya, ln_g, ln_b, ws, bsb)


def _hg_common(q_ref, f_ref, hlb_ref):
    C = HG_CHUNK
    a = hlb_ref[...]
    lb = _sigmoid(a[0:1, :] - a[1:2, :])
    sig = _sigmoid(f_ref[...])
    f = lb + (1.0 - lb) * sig
    lf = jnp.log(f)
    kk = 1.0 - f
    q = q_ref[...]
    sq = _sigmoid(q)
    qa = q * sq
    row = lax.broadcasted_iota(jnp.int32, (C, C), 0)
    col = lax.broadcasted_iota(jnp.int32, (C, C), 1)
    tril = row >= col
    b = _ones_dot(tril.astype(BF16), lf)
    bm = b[HG_MID:HG_MID + 1, :]
    bl = b[C - 1:C, :]
    e_b = jnp.exp(b)
    e_qm = jnp.exp(jnp.minimum(b - bm, EXP_CLAMP))
    e_km = jnp.exp(jnp.minimum(bm - b, EXP_CLAMP))
    e_kl = jnp.exp(bl - b)
    return dict(lb=lb, sig=sig, f=f, kk=kk, q=q, sq=sq, qa=qa, tril=tril, e_b=e_b, e_qm=e_qm, e_km=e_km, e_kl=e_kl,
                e_l=jnp.exp(bl), qh=qa * e_b, qt=qa * e_qm, kt=kk * e_km, kh=kk * e_kl)


def _hg_fwd(z, hg_lb, ng, HW):
    S = z.shape[0]
    C, H, dk = HG_CHUNK, HW // HG_DK, HG_DK
    nc = S // C

    def body(q_ref, f_ref, i_ref, og_ref, hlb_ref, ng_ref, yb_ref, o_ref, st_ref, state):
        @pl.when(pl.program_id(0) == 0)
        def _():
            state[...] = jnp.zeros_like(state)

        t = _hg_common(q_ref, f_ref, hlb_ref)
        iv = i_ref[...]
        for h in range(H):
            sl = slice(h * dk, (h + 1) * dk)
            st = state[h]
            st_ref[h] = st
            a = jnp.where(t["tril"], _dot(t["qt"][:, sl], t["kt"][:, sl], _NT), 0.0)
            o_h = _dot(a, iv[:, sl]) + _dot(t["qh"][:, sl], st, _NT)
            state[h] = st * t["e_l"][:, sl] + _dot(iv[:, sl], t["kh"][:, sl], _TN)
            o_ref[:, sl] = o_h
            rr = lax.rsqrt(jnp.mean(o_h * o_h, axis=-1, keepdims=True) + EPS)
            og = og_ref[:, sl]
            yb_ref[:, sl] = (o_h * rr * ng_ref[:, sl] * (og * _sigmoid(og))).astype(BF16)

    def col(k):
        return pl.BlockSpec((C, HW), lambda c: (c, k))

    base = 2
    return pl.pallas_call(
        body, name="hgrn_fwd", grid=(nc,),
        in_specs=[col(base), col(base + 1), col(base + 2), col(base + 3),
                  pl.BlockSpec((2, HW), lambda c: (0, 0)), pl.BlockSpec((1, HW), lambda c: (0, 0))],
        out_specs=[pl.BlockSpec((C, HW), lambda c: (c, 0)), pl.BlockSpec((C, HW), lambda c: (c, 0)),
                   pl.BlockSpec((None, H, dk, dk), lambda c: (c, 0, 0, 0))],
        out_shape=[jax.ShapeDtypeStruct((S, HW), BF16), jax.ShapeDtypeStruct((S, HW), F32),
                   jax.ShapeDtypeStruct((nc, H, dk, dk), F32)],
        scratch_shapes=[pltpu.VMEM((H, dk, dk), F32)],
        compiler_params=_cparams(("arbitrary",)))(z, z, z, z, hg_lb, ng)


def _hg_bwd(z, o, states, dyb, hg_lb, ng, HW, dz_head, dz_tail):
    S = z.shape[0]
    C, H, dk = HG_CHUNK, HW // HG_DK, HG_DK
    nc = S // C
    B0 = dz_head.shape[1]
    DT = dz_tail.shape[2]
    INW = B0 + 4 * HW + 2 * DT

    def body(q_ref, f_ref, i_ref, og_ref, o_ref, st_ref, stn_ref, dyb_ref, hlb_ref, ng_ref, head_ref, tail_ref,
             dzf_ref, dng_ref, dhlb_ref, dtail_ref, dstate, cross, dqa_buf, dkk_buf, db_buf, dlb_acc):
        c = pl.program_id(0)
        dzf_ref[:, 0:B0] = head_ref[...]
        dzf_ref[:, B0 + 4 * HW:B0 + 4 * HW + DT] = tail_ref[0]
        dzf_ref[:, B0 + 4 * HW + DT:INW] = tail_ref[1]
        dz_ref = dzf_ref.at[:, B0:B0 + 4 * HW]

        @pl.when(c == 0)
        def _():
            dtail_ref[...] = jnp.zeros_like(dtail_ref)

        dtail_ref[0:1, :] += jnp.sum(tail_ref[0].astype(F32), axis=0, keepdims=True)
        dtail_ref[1:2, :] += jnp.sum(tail_ref[1].astype(F32), axis=0, keepdims=True)

        @pl.when(c == 0)
        def _():
            dstate[...] = jnp.zeros_like(dstate)
            dlb_acc[...] = jnp.zeros_like(dlb_acc)
            dng_ref[...] = jnp.zeros_like(dng_ref)

        def r16(v):
            return v.astype(BF16).astype(F32)

        t = _hg_common(q_ref, f_ref, hlb_ref)
        iv = i_ref[...]
        for h in range(H):
            sl = slice(h * dk, (h + 1) * dk)
            o_h, og, dyb_h, ng_h = o_ref[:, sl], og_ref[:, sl], dyb_ref[:, sl], ng_ref[:, sl]
            sg = _sigmoid(og)
            silu_og = og * sg
            rr = lax.rsqrt(jnp.mean(o_h * o_h, axis=-1, keepdims=True) + EPS)
            on = o_h * rr
            dng_ref[0:1, sl] += jnp.sum(dyb_h * on * silu_og, axis=0, keepdims=True)
            dz_ref[:, 3 * HW + h * dk:3 * HW + (h + 1) * dk] = (dyb_h * on * ng_h * (sg * (1.0 + og * (1.0 - sg)))).astype(BF16)
            don = dyb_h * ng_h * silu_og
            do_h = rr * (don - on * jnp.mean(don * on, axis=-1, keepdims=True))

            qt, kt, qh, kh, iv_h = t["qt"][:, sl], t["kt"][:, sl], t["qh"][:, sl], t["kh"][:, sl], iv[:, sl]
            a = jnp.where(t["tril"], _dot(qt, kt, _NT), 0.0)
            da = jnp.where(t["tril"], _dot(do_h, iv_h, _NT), 0.0)
            st, dst = st_ref[h], dstate[h]
            cross[:, sl] = jnp.sum(dst * stn_ref[h], axis=0, keepdims=True)
            dqh = _dot(do_h, st)
            dstate[h] = _dot(do_h, qh, _TN) + dst * t["e_l"][:, sl]
            div = _dot(a, do_h, _TN) + _dot(kh, dst, _NT)
            dkh = _dot(iv_h, dst)
            dqt = _dot(da, kt)
            dkt = _dot(da, qt, _TN)
            dz_ref[:, 2 * HW + h * dk:2 * HW + (h + 1) * dk] = div.astype(BF16)
            dqa_buf[:, sl] = dqh * t["e_b"][:, sl] + dqt * t["e_qm"][:, sl]
            dkk_buf[:, sl] = dkt * t["e_km"][:, sl] + dkh * t["e_kl"][:, sl]
            db_buf[:, sl] = r16(qt) * dqt - r16(kt) * dkt + r16(qh) * dqh - r16(kh) * dkh

        dqa, dkk = dqa_buf[...], dkk_buf[...]
        triu = jnp.logical_not(t["tril"]) | (lax.broadcasted_iota(jnp.int32, (C, C), 0) == lax.broadcasted_iota(jnp.int32, (C, C), 1))
        dlf = _ones_dot(triu.astype(BF16), db_buf[...]) + cross[...]
        df = dlf / t["f"] - dkk
        sig, lb = t["sig"], t["lb"]
        dz_ref[:, HW:2 * HW] = (df * (1.0 - lb) * sig * (1.0 - sig)).astype(BF16)
        dlb_acc[...] += jnp.sum(df * (1.0 - sig), axis=0, keepdims=True)
        q, sq = t["q"], t["sq"]
        dz_ref[:, 0:HW] = (dqa * (sq * (1.0 + q * (1.0 - sq)))).astype(BF16)

        @pl.when(c == nc - 1)
        def _():
            da0 = dlb_acc[...] * lb * (1.0 - lb)
            dhlb_ref[0:1, :] = da0
            dhlb_ref[1:2, :] = -da0

    def col(k):
        return pl.BlockSpec((C, HW), lambda c: (nc - 1 - c, k))

    base = 2
    return pl.pallas_call(
        body, name="hgrn_bwd", grid=(nc,),
        in_specs=[col(base), col(base + 1), col(base + 2), col(base + 3), col(0),
                  pl.BlockSpec((None, H, dk, dk), lambda c: (nc - 1 - c, 0, 0, 0)),
                  pl.BlockSpec((None, H, dk, dk), lambda c: (jnp.minimum(nc - c, nc - 1), 0, 0, 0)), col(0),
                  pl.BlockSpec((2, HW), lambda c: (0, 0)), pl.BlockSpec((1, HW), lambda c: (0, 0)),
                  pl.BlockSpec((C, B0), lambda c: (nc - 1 - c, 0)), pl.BlockSpec((2, C, DT), lambda c: (0, nc - 1 - c, 0))],
        out_specs=[pl.BlockSpec((C, INW), lambda c: (nc - 1 - c, 0)), pl.BlockSpec((8, HW), lambda c: (0, 0)),
                   pl.BlockSpec((2, HW), lambda c: (0, 0)), pl.BlockSpec((2, DT), lambda c: (0, 0))],
        out_shape=[jax.ShapeDtypeStruct((S, INW), BF16), jax.ShapeDtypeStruct((8, HW), F32), jax.ShapeDtypeStruct((2, HW), F32),
                   jax.ShapeDtypeStruct((2, DT), F32)],
        scratch_shapes=[pltpu.VMEM((H, dk, dk), F32), pltpu.VMEM((1, HW), F32), pltpu.VMEM((C, HW), F32), pltpu.VMEM((C, HW), F32),
                        pltpu.VMEM((C, HW), F32), pltpu.VMEM((1, HW), F32)],
        compiler_params=_cparams(("arbitrary",)))(z, z, z, z, o, states, states, dyb, hg_lb, ng, dz_head, dz_tail)


def _position():
    x, y, c = lax.axis_index("x"), lax.axis_index("y"), lax.axis_index("c")
    return x, y, c, 4 * x + 2 * y + c


def _flip(x, y, c, k):
    return (1 - x if k & 4 else x, 1 - y if k & 2 else y, 1 - c if k & 1 else c)


def _allgather_small(name, v):
    R, L = v.shape

    def body(v_ref, out_ref, send_sems, recv_sems):
        x, y, c, me = _position()
        out_ref[me] = v_ref[...]
        copies = []
        for k in range(1, N_DEV):
            cp = pltpu.make_async_remote_copy(src_ref=v_ref, dst_ref=out_ref.at[me], send_sem=send_sems.at[k - 1],
                                              recv_sem=recv_sems.at[k - 1], device_id=_flip(x, y, c, k), device_id_type=MESH)
            cp.start()
            copies.append(cp)
        for cp in copies:
            cp.wait()

    return pl.pallas_call(
        body, name=name, out_shape=jax.ShapeDtypeStruct((N_DEV, R, L), v.dtype),
        in_specs=[pl.BlockSpec(memory_space=pltpu.VMEM)], out_specs=pl.BlockSpec(memory_space=pltpu.VMEM),
        scratch_shapes=[pltpu.SemaphoreType.DMA((N_DEV - 1,)), pltpu.SemaphoreType.DMA((N_DEV - 1,))],
        compiler_params=pltpu.CompilerParams(vmem_limit_bytes=VMEM_LIMIT),
    )(v)


_HBM = pl.BlockSpec(memory_space=pltpu.HBM)
_SEM = pl.BlockSpec(memory_space=pltpu.SEMAPHORE)
_EFFECT = pltpu.SideEffectType.DATAFLOW_SIDE_EFFECTING


def _split_start(name, bufs, n_sems, copies_fn, after=None):
    nb = len(bufs)
    extra = [] if after is None else [after]
    k = nb + len(extra)

    def body(*refs):
        for cp in copies_fn(refs[:nb], refs[k], refs[k + 1]):
            cp.start()
        refs[-1][...] = jnp.zeros_like(refs[-1])

    sems = pltpu.SemaphoreType.DMA((n_sems,))
    res = pl.pallas_call(
        body, name=name,
        out_shape=(sems, sems, *[pltpu.HBM(a.shape, a.dtype) for a in bufs], jax.ShapeDtypeStruct((8, LANES), F32)),
        in_specs=[_HBM] * nb + [pl.BlockSpec(memory_space=pl.ANY)] * len(extra),
        out_specs=(_SEM, _SEM, *[_HBM] * nb, pl.BlockSpec(memory_space=pltpu.VMEM)),
        input_output_aliases={i: 2 + i for i in range(nb)},
        compiler_params=pltpu.CompilerParams(has_side_effects=_EFFECT),
    )(*[pltpu.with_memory_space_constraint(a, pltpu.HBM) for a in bufs], *extra)
    return res[0], res[1], list(res[2:2 + nb]), res[-1]


def _split_wait(name, bufs, send_sems, recv_sems, after, copies_fn):
    nb = len(bufs)

    def body(*refs):
        for cp in copies_fn(refs[:nb], refs[nb], refs[nb + 1]):
            cp.wait_send()
            cp.wait_recv()

    res = pl.pallas_call(
        body, name=name, out_shape=tuple(pltpu.HBM(a.shape, a.dtype) for a in bufs),
        in_specs=[_HBM] * nb + [_SEM, _SEM, pl.BlockSpec(memory_space=pl.ANY)], out_specs=tuple([_HBM] * nb),
        input_output_aliases={i: i for i in range(nb)},
        compiler_params=pltpu.CompilerParams(has_side_effects=_EFFECT),
    )(*bufs, send_sems, recv_sems, after)
    return list(res)


def _split_relay(name, bufs, send_sems, recv_sems, after, wait_fn, n_sems, start_fn):
    nb = len(bufs)

    def body(*refs):
        for cp in wait_fn(refs[:nb], refs[nb], refs[nb + 1]):
            cp.wait_send()
            cp.wait_recv()
        for cp in start_fn(refs[:nb], refs[nb + 3], refs[nb + 4]):
            cp.start()
        refs[-1][...] = jnp.zeros_like(refs[-1])

    sems = pltpu.SemaphoreType.DMA((n_sems,))
    res = pl.pallas_call(
        body, name=name, out_shape=(sems, sems, *[pltpu.HBM(a.shape, a.dtype) for a in bufs], jax.ShapeDtypeStruct((8, LANES), F32)),
        in_specs=[_HBM] * nb + [_SEM, _SEM, pl.BlockSpec(memory_space=pl.ANY)],
        out_specs=(_SEM, _SEM, *[_HBM] * nb, pl.BlockSpec(memory_space=pltpu.VMEM)),
        input_output_aliases={i: 2 + i for i in range(nb)},
        compiler_params=pltpu.CompilerParams(has_side_effects=_EFFECT),
    )(*bufs, send_sems, recv_sems, after)
    return res[0], res[1], list(res[2:2 + nb]), res[-1]


N_CHIP = 4


def _chip_flip(x, y, k):
    return (1 - x if k & 2 else x), (1 - y if k & 1 else y)


def _gather_first_copies(n):
    def copies(bufs, send_sems, recv_sems):
        x, y, c, me = _position()
        out = []
        for w in range(n):
            for k in range(N_CHIP):
                to = (x, y, 1 - c) if k == 0 else (*_chip_flip(x, y, k), c)
                out.append(pltpu.make_async_remote_copy(
                    src_ref=bufs[w], dst_ref=bufs[n + w].at[me], send_sem=send_sems.at[w * N_CHIP + k],
                    recv_sem=recv_sems.at[w * N_CHIP + k], device_id=to, device_id_type=MESH))
        return out
    return copies


def _gather_relay_copies(n):
    def copies(bufs, send_sems, recv_sems):
        x, y, c, _ = _position()
        out = []
        for w in range(n):
            for k in range(1, N_CHIP):
                px, py = _chip_flip(x, y, k)
                blk = bufs[n + w].at[4 * px + 2 * py + c]
                out.append(pltpu.make_async_remote_copy(
                    src_ref=blk, dst_ref=blk, send_sem=send_sems.at[w * (N_CHIP - 1) + k - 1],
                    recv_sem=recv_sems.at[w * (N_CHIP - 1) + k - 1], device_id=(x, y, 1 - c), device_id_type=MESH))
        return out
    return copies


def _small_gather_copies(bufs, send_sems, recv_sems):
    x, y, c, me = _position()
    return [pltpu.make_async_remote_copy(src_ref=bufs[0], dst_ref=bufs[1].at[me], send_sem=send_sems.at[k - 1], recv_sem=recv_sems.at[k - 1],
                                         device_id=_flip(x, y, c, k), device_id_type=MESH) for k in range(1, N_DEV)]


def _forward_first_copies(n):
    def copies(bufs, send_sems, recv_sems):
        x, y, c, me = _position()
        out = []
        for w in range(n):
            for k, to in enumerate([(x, y, 1 - c), (1 - x, y, c), (x, 1 - y, c)]):
                out.append(pltpu.make_async_remote_copy(
                    src_ref=bufs[w], dst_ref=bufs[n + w].at[me], send_sem=send_sems.at[w * 3 + k],
                    recv_sem=recv_sems.at[w * 3 + k], device_id=to, device_id_type=MESH))
        return out
    return copies


def _forward_second_copies(n):
    def copies(bufs, send_sems, recv_sems):
        x, y, c, _ = _position()
        out = []
        for w in range(n):
            half = bufs[n + w].shape[1] // 2
            for k, (src_chip, rows, to) in enumerate([((1 - x, y), pl.ds(0, half), (x, 1 - y, c)), ((x, 1 - y), pl.ds(half, half), (1 - x, y, c))]):
                blk = bufs[n + w].at[4 * src_chip[0] + 2 * src_chip[1] + c, rows]
                out.append(pltpu.make_async_remote_copy(src_ref=blk, dst_ref=blk, send_sem=send_sems.at[w * 4 + k],
                                                        recv_sem=recv_sems.at[w * 4 + k], device_id=to, device_id_type=MESH))
            for k, (px, py) in enumerate([(1 - x, y), (x, 1 - y)]):
                blk = bufs[n + w].at[4 * px + 2 * py + c]
                out.append(pltpu.make_async_remote_copy(src_ref=blk, dst_ref=blk, send_sem=send_sems.at[w * 4 + 2 + k],
                                                        recv_sem=recv_sems.at[w * 4 + 2 + k], device_id=(x, y, 1 - c), device_id_type=MESH))
        return out
    return copies


def _forward_third_copies(n):
    def copies(bufs, send_sems, recv_sems):
        x, y, c, _ = _position()
        out = []
        for w in range(n):
            blk = bufs[n + w].at[4 * (1 - x) + 2 * (1 - y) + c]
            out.append(pltpu.make_async_remote_copy(src_ref=blk, dst_ref=blk, send_sem=send_sems.at[w], recv_sem=recv_sems.at[w],
                                                    device_id=(x, y, 1 - c), device_id_type=MESH))
        return out
    return copies


def _to_sibling_copies(n):
    def copies(bufs, send_sems, recv_sems):
        x, y, c, _ = _position()
        out = []
        for w in range(n):
            for q in range(N_CHIP):
                out.append(pltpu.make_async_remote_copy(
                    src_ref=bufs[w].at[2 * q + 1 - c], dst_ref=bufs[n + w].at[q], send_sem=send_sems.at[w * N_CHIP + q],
                    recv_sem=recv_sems.at[w * N_CHIP + q], device_id=(x, y, 1 - c), device_id_type=MESH))
        return out
    return copies


def _to_owner_copies(n):
    def copies(bufs, send_sems, recv_sems):
        x, y, c, _ = _position()
        out = []
        for w in range(n):
            for k in range(1, N_CHIP):
                px, py = (1 - x if k & 2 else x), (1 - y if k & 1 else y)
                out.append(pltpu.make_async_remote_copy(
                    src_ref=bufs[w].at[2 * px + py], dst_ref=bufs[n + w].at[k - 1], send_sem=send_sems.at[w * (N_CHIP - 1) + k - 1],
                    recv_sem=recv_sems.at[w * (N_CHIP - 1) + k - 1], device_id=(px, py, c), device_id_type=MESH))
        return out
    return copies


def _chip_sum(name, stack, landed, c_idx):
    _, R, C = stack.shape
    tr = _tile(R, max(BF16_ROWS, STREAM_TILE // C), BF16_ROWS)

    def body(c_ref, a_ref, b_ref, o_ref):
        o_ref[...] = (a_ref[...].astype(F32) + b_ref[...].astype(F32)).astype(o_ref.dtype)

    return pl.pallas_call(
        body, name=name,
        grid_spec=pltpu.PrefetchScalarGridSpec(
            num_scalar_prefetch=1, grid=(N_CHIP, R // tr),
            in_specs=[pl.BlockSpec((None, tr, C), lambda q, i, c_ref: (2 * q + c_ref[0], i, 0)),
                      pl.BlockSpec((None, tr, C), lambda q, i, c_ref: (q, i, 0))],
            out_specs=pl.BlockSpec((None, tr, C), lambda q, i, c_ref: (q, i, 0))),
        out_shape=jax.ShapeDtypeStruct((N_CHIP, R, C), stack.dtype),
        compiler_params=_cparams(("parallel", "parallel")))(c_idx, stack, landed)


def _ada_mod(c16, w):
    _, D = c16.shape
    n = w.shape[1]
    tk = _tile(D, 512)
    nk = D // tk

    def body(c_ref, w_ref, o_ref, ca_ref):
        @pl.when(pl.program_id(0) == 0)
        def _():
            o_ref[...] = jnp.zeros_like(o_ref)

        cv = c_ref[...]
        ca = cv * _sigmoid(cv)
        ca_ref[...] = ca
        o_ref[...] += _dot(ca, w_ref[...])

    return pl.pallas_call(
        body, name="ada_mod", grid=(nk,),
        in_specs=[pl.BlockSpec((16, tk), lambda k: (0, k)), pl.BlockSpec((tk, n), lambda k: (k, 0))],
        out_specs=[pl.BlockSpec((16, n), lambda k: (0, 0)), pl.BlockSpec((16, tk), lambda k: (0, k))],
        out_shape=[jax.ShapeDtypeStruct((16, n), F32), jax.ShapeDtypeStruct((16, D), F32)],
        compiler_params=_cparams(("arbitrary",)))(c16, w)


def _cast_shard(name, wf, slot):
    r, c = wf.shape
    tr = _tile(r, max(BF16_ROWS, STREAM_TILE // c), BF16_ROWS)

    def body(slot_ref, w_ref, s_ref, g_ref):
        v = w_ref[...].astype(BF16)
        s_ref[...] = v
        g_ref[...] = v

    return pl.pallas_call(
        body, name=name,
        grid_spec=pltpu.PrefetchScalarGridSpec(
            num_scalar_prefetch=1, grid=(r // tr,), in_specs=[pl.BlockSpec((tr, c), lambda i, s: (i, 0))],
            out_specs=[pl.BlockSpec((tr, c), lambda i, s: (i, 0)), pl.BlockSpec((None, tr, c), lambda i, s: (s[0], i, 0))]),
        out_shape=[jax.ShapeDtypeStruct((r, c), BF16), jax.ShapeDtypeStruct((N_DEV, r, c), BF16)],
        compiler_params=_cparams(("parallel",)))(slot, wf)


def _adam_math(w, g, m, v):
    m2 = ADAM_B1 * m + (1.0 - ADAM_B1) * g
    v2 = ADAM_B2 * v + (1.0 - ADAM_B2) * (g * g)
    m_hat = m2 / (1.0 - ADAM_B1 ** ADAM_STEP)
    v_hat = v2 / (1.0 - ADAM_B2 ** ADAM_STEP)
    delta = -ADAM_LR * (m_hat / (jnp.sqrt(v_hat) + ADAM_EPS) + ADAM_WD * w)
    return delta, m2, v2


def _adamw(name, w, m, v, own, own_slot, parts=(), row0=0, into=None):
    R, C = w.shape
    Rp = own.shape[1]
    tr = _tile(Rp, max(BF16_ROWS, ADAMW_TILE // C), BF16_ROWS)
    off = row0 // tr
    n_p = len(parts)
    held = [] if into is None else list(into)

    def body(slot_ref, *refs):
        w_ref, m_ref, v_ref, own_ref = refs[:4]
        g_ref, d_ref, m2_ref, v2_ref = refs[4 + n_p + len(held):]
        g = own_ref[...].astype(F32)
        for p_ref in refs[4:4 + n_p]:
            for s in range(p_ref.shape[0]):
                g = g + p_ref[s].astype(F32)
        delta, m2, v2 = _adam_math(w_ref[...], g, m_ref[...], v_ref[...])
        g_ref[...] = g
        d_ref[...] = delta
        m2_ref[...] = m2
        v2_ref[...] = v2

    blk = pl.BlockSpec((tr, C), lambda i, s: (i + off, 0))
    out = jax.ShapeDtypeStruct((R, C), F32)
    return pl.pallas_call(
        body, name=name,
        grid_spec=pltpu.PrefetchScalarGridSpec(
            num_scalar_prefetch=1, grid=(Rp // tr,),
            in_specs=[blk, blk, blk, pl.BlockSpec((None, tr, C), lambda i, s: (s[0], i, 0))]
            + [pl.BlockSpec((a.shape[0], tr, C), lambda i, s: (0, i, 0)) for a in parts]
            + [pl.BlockSpec(memory_space=pl.ANY)] * len(held),
            out_specs=[blk] * 4),
        out_shape=[out] * 4, input_output_aliases={5 + n_p + i: i for i in range(len(held))},
        compiler_params=_cparams(("parallel",)))(own_slot, w, m, v, own, *parts, *held)


def _small_update(gathered, w, m, v, after, rows):
    _, R, L = gathered.shape
    rs = w.shape[0]
    n = len(rows)
    assert all(r % 8 == 0 for r in rows) and sum(rows) <= rs and rs + 8 <= R

    def body(p_ref, w_ref, m_ref, v_ref, after_ref, *outs):
        g = p_ref[0]
        for p in range(1, N_DEV):
            g = g + p_ref[p]
        kinds = (g,) + _adam_math(w_ref[...], g[0:rs, :], m_ref[...], v_ref[...])
        at = 0
        for k, r in enumerate(rows):
            for idx, val in enumerate(kinds):
                outs[idx * n + k][...] = val[at:at + r, :]
            at += r
        outs[4 * n][...] = g[at:at + 8, :]

    vm = pl.BlockSpec(memory_space=pltpu.VMEM)
    shapes = [jax.ShapeDtypeStruct((r, L), F32) for _ in range(4) for r in rows] + [jax.ShapeDtypeStruct((8, L), F32)]
    return pl.pallas_call(body, name="small_update", in_specs=[vm] * 4 + [pl.BlockSpec(memory_space=pl.ANY)], out_specs=[vm] * len(shapes),
                          out_shape=shapes, compiler_params=pltpu.CompilerParams(vmem_limit_bytes=VMEM_LIMIT))(gathered, w, m, v, after)


class _Fetched(dict):
    def __init__(self, fetch):
        super().__init__()
        self.fetch = fetch

    def first(self, key, after):
        self[key] = self.fetch(key, after)
        return self[key]


def _local_step(x, tgt, mod, p, fetch, F, scatter=None):
    S, D = x.shape
    GW, HW = p["ln_g"].shape[1], p["hg_ng"].shape[1]
    G, T, _ = p["ws"].shape
    w = _Fetched(fetch)
    INW = 2 * GW + 4 * HW + 2 * D
    in_loc, br_loc, fi_loc = INW // N_DEV, D // N_DEV, 2 * F // N_DEV
    assert GW == HW and F % fi_loc == 0
    sh1, sc1, gt1, sh2, sc2, gt2 = (mod[:, k * D:(k + 1) * D] for k in range(6))
    bsb = jnp.broadcast_to(p["bs"][:, :, None], (G, T, GW // G))

    tm = _tile(S, 1024, 16)
    tmh = _tile(S, 512, 16)
    tn_in = _tile(in_loc, 1280)
    tn_d = _tile(D, 512)
    tn_br = _tile(br_loc, 512)
    tk_s = S
    tm_w = _tile(D, 1024)
    g_off = 2 * GW + 4 * HW

    h1 = _norm_mod("norm1", x, p["norm1_g"], sc1, sh1)
    z = _mm_nn_stacked("proj_in", h1, w.first("in", h1), tm=tm, tn=tn_in, tk=D)[0]
    ya = _gmlp_fwd(z, p["ln_g"], p["ln_b"], p["ws"], bsb, GW)
    yb, o_hg, states = _hg_fwd(z, p["hg_lb"], p["hg_ng"], HW)
    flat = {k: jnp.swapaxes(w.first(k, yb), 0, 1).reshape(GW, D) for k in ("bg", "bh")}
    tn_f = _tile(D, 1024)
    pa = _matmul(
        "branch_gmlp", ya, flat["bg"], dims=_NN, grid_mnk=(S // tm, D // tn_f, 1), tiles=(tm, tn_f),
        a_spec=pl.BlockSpec((tm, GW), lambda i, j, k: (i, 0)), b_spec=pl.BlockSpec((GW, tn_f), lambda i, j, k: (0, j)),
        out_shapes=[jax.ShapeDtypeStruct((S, D), F32)], out_specs=[pl.BlockSpec((tm, tn_f), lambda i, j, k: (i, j))], epilogue=_store(F32))[0]
    t_fi = w.first("fi_early", pa)

    def gates(ga_ref, gb_ref, ba_ref, bb_ref):
        return _sigmoid(ga_ref[...] + ba_ref[...]), _sigmoid(gb_ref[...] + bb_ref[...])

    def gate_specs(tn_, tm_=tm):
        o1, o2 = g_off // tn_, (g_off + D) // tn_
        return [pl.BlockSpec((tm_, tn_), lambda i, j, k: (i, o1 + j)), pl.BlockSpec((tm_, tn_), lambda i, j, k: (i, o2 + j)),
                pl.BlockSpec((1, tn_), lambda i, j, k: (0, j)), pl.BlockSpec((1, tn_), lambda i, j, k: (0, D // tn_ + j))]

    def merge_ep(acc, ex, outs):
        ga, gb = gates(*ex[1:5])
        outs[0][...] = acc.astype(BF16)
        outs[1][...] = (ga * ex[0][...] + gb * acc).astype(BF16)

    tile_o = pl.BlockSpec((tmh, tn_f), lambda i, j, k: (i, j))
    pb, y = _matmul(
        "branch_hg_merge", yb, flat["bh"], dims=_NN, grid_mnk=(S // tmh, D // tn_f, 1), tiles=(tmh, tn_f),
        a_spec=pl.BlockSpec((tmh, HW), lambda i, j, k: (i, 0)), b_spec=pl.BlockSpec((HW, tn_f), lambda i, j, k: (0, j)),
        extras=[pa, z, z, p["b_gate"], p["b_gate"]], extra_specs=[tile_o, *gate_specs(tn_f, tmh)],
        out_shapes=[jax.ShapeDtypeStruct((S, D), BF16), jax.ShapeDtypeStruct((S, D), BF16)], out_specs=[tile_o, tile_o],
        epilogue=merge_ep, after=t_fi)

    def resid_ep(acc, ex, outs):
        outs[0][...] = acc.astype(BF16)
        outs[1][...] = ex[0][...] + ex[1][...] * acc

    def resid_mm(name, a, b, res, gt, tm_, tn_):
        K = a.shape[1]
        t_o = pl.BlockSpec((tm_, tn_), lambda i, j, k: (i, j))
        return _matmul(
            name, a, b, dims=_NN, grid_mnk=(S // tm_, D // tn_, 1), tiles=(tm_, tn_),
            a_spec=pl.BlockSpec((tm_, K), lambda i, j, k: (i, 0)), b_spec=pl.BlockSpec((K, tn_), lambda i, j, k: (0, j)),
            extras=[res, gt], extra_specs=[t_o, pl.BlockSpec((1, tn_), lambda i, j, k: (0, j))],
            out_shapes=[jax.ShapeDtypeStruct((S, D), BF16), jax.ShapeDtypeStruct((S, D), F32)], out_specs=[t_o, t_o], epilogue=resid_ep)

    o1, xm = resid_mm("proj_out", y, w.first("out", z), x, gt1, tm, tn_f)
    h2 = _norm_mod("norm2", xm, p["norm2_g"], sc2, sh2)
    hf, hf_fac = _ffn_in_swiglu(h2, w.first("fi", h2))
    o2, x3 = resid_mm("ffn_out", hf, w.first("fo", hf), xm, gt2, tmh, tn_d)
    dx3, do2, vec_l = _loss_head(x3, tgt, p["final_g"], o2, gt2)

    nf = F // fi_loc

    def dswiglu_ep(acc, ex, outs):
        outs[0][0] = (acc * ex[0][0].astype(F32)).astype(BF16)
        outs[0][1] = (acc * ex[0][1].astype(F32)).astype(BF16)

    pair = pl.BlockSpec((2, tmh, fi_loc), lambda i, j, k: (0, i, j))
    dab = _matmul(
        "ffn_out_dx", do2, w["fo"], dims=_NT, grid_mnk=(S // tmh, nf, 1), tiles=(tmh, fi_loc),
        a_spec=pl.BlockSpec((tmh, D), lambda i, j, k: (i, 0)), b_spec=pl.BlockSpec((fi_loc, D), lambda i, j, k: (j, 0)),
        extras=[hf_fac], extra_specs=[pair], out_shapes=[jax.ShapeDtypeStruct((2, S, F), BF16)], out_specs=[pair],
        epilogue=dswiglu_ep)[0]
    start = (lambda name, grads: scatter[0](name, grads)) if scatter is not None else (lambda name, grads: None)
    push = (lambda name, after: scatter[1](name, after)) if scatter is not None else (lambda name, after: None)

    def zero(token):
        return 0.0 if token is None else token[0:1, 0:1]

    tm_f = _tile(F, 512)
    g_fo = _mm_tn("ffn_out_dw", hf, do2, pl.BlockSpec((tk_s, D), lambda i, j, k: (k, j)), Mo=F, No=D, S=S, tm=tm_f, tn=D, tk=tk_s)
    g_fi = _mm_tn("ffn_in_dw", h2, dab, pl.BlockSpec((None, tk_s, fi_loc), lambda i, j, k: (j // nf, k, j % nf)),
                  Mo=D, No=2 * F, S=S, tm=D, tn=fi_loc, tk=tk_s, stacked_nloc=fi_loc, after=g_fo)
    t_ffn = start("scatter_ffn", dict(fo=g_fo, fi=g_fi))
    dh2 = _mm_nt_stacked("ffn_in_dx", pl.BlockSpec((None, tm, fi_loc), lambda i, j, k: (k // nf, i, k % nf)), dab, w["fi"],
                         M=S, tm=tm, tn=D, tk=fi_loc, after=t_ffn)
    dxm, vec2, do1 = _norm_mod_bwd("norm2_bwd", dh2, xm, p["norm2_g"], sc2, dx3, o1, gt1)
    t_ffn = push("scatter_ffn", dxm)

    def dmerge_ep(acc, ex, outs):
        ga, gb = gates(*ex[2:6])
        outs[0][...] = (acc * ga).astype(BF16)
        outs[1][...] = (acc * gb).astype(BF16)
        outs[2][0] = (acc * ex[0][...] * ga * (1.0 - ga)).astype(BF16)
        outs[2][1] = (acc * ex[1][...] * gb * (1.0 - gb)).astype(BF16)

    t_o = pl.BlockSpec((tm, tn_d), lambda i, j, k: (i, j))
    dpa, dpb, dg2 = _matmul(
        "proj_out_dx", do1, w["out"], dims=_NT, grid_mnk=(S // tm, D // tn_d, 1), tiles=(tm, tn_d),
        a_spec=pl.BlockSpec((tm, D), lambda i, j, k: (i, 0)), b_spec=pl.BlockSpec((tn_d, D), lambda i, j, k: (j, 0)),
        extras=[pa, pb, z, z, p["b_gate"], p["b_gate"]], extra_specs=[t_o, t_o, *gate_specs(tn_d)],
        out_shapes=[jax.ShapeDtypeStruct((S, D), BF16), jax.ShapeDtypeStruct((S, D), BF16), jax.ShapeDtypeStruct((2, S, D), BF16)],
        out_specs=[t_o, t_o, pl.BlockSpec((2, tm, tn_d), lambda i, j, k: (0, i, j))], epilogue=dmerge_ep, after=t_ffn)
    g_out = _mm_tn("proj_out_dw", y, do1, pl.BlockSpec((tk_s, D), lambda i, j, k: (k, j)), Mo=D, No=D, S=S, tm=tn_d, tn=D, tk=tk_s)
    tn_g = _tile(GW, 512)
    b_br = pl.BlockSpec((tk_s, br_loc), lambda i, j, k: (k, j))
    tm_b = _tile(GW, 1024)
    g_bg = _mm_tn("branch_gmlp_dw", ya, dpa, b_br, Mo=GW, No=D, S=S, tm=tm_b, tn=br_loc, tk=tk_s, stacked_nloc=br_loc)
    g_bh = _mm_tn("branch_hg_dw", yb, dpb, b_br, Mo=HW, No=D, S=S, tm=tm_b, tn=br_loc, tk=tk_s, stacked_nloc=br_loc)
    t_mix = start("scatter_mixer", dict(out=g_out, bg=g_bg, bh=g_bh))
    def branch_dx(name, dp, w_flat):
        return _matmul(
            name, dp, w_flat, dims=_NT, grid_mnk=(S // tm, GW // tn_g, 1), tiles=(tm, tn_g),
            a_spec=pl.BlockSpec((tm, D), lambda i, j, k: (i, 0)), b_spec=pl.BlockSpec((tn_g, D), lambda i, j, k: (j, 0)),
            out_shapes=[jax.ShapeDtypeStruct((S, GW), F32)], out_specs=[pl.BlockSpec((tm, tn_g), lambda i, j, k: (i, j))],
            epilogue=_store(F32), after=t_mix)[0]

    dya = branch_dx("branch_gmlp_dx", dpa, flat["bg"])
    dyb = branch_dx("branch_hg_dx", dpb, flat["bh"])
    dz_gmlp, dln, dws, dbs = _gmlp_bwd(z, dya, p["ln_g"], p["ln_b"], p["ws"], bsb, GW)
    t_mix = push("scatter_mixer", dz_gmlp)
    dz, dng, dhlb, db_gate = _hg_bwd(z, o_hg, states, dyb, p["hg_lb"], p["hg_ng"] + zero(t_mix), HW, dz_gmlp, dg2)
    half = D // 2
    tm_h = _tile(half, 1024)
    g_in = []
    t_in = None
    for hname, h in (("a", 0), ("b", 1)):
        g_in.append(_mm_tn("proj_in_dw_" + hname, h1, dz, pl.BlockSpec((tk_s, in_loc), lambda i, j, k: (k, j)), Mo=half, No=INW, S=S,
                           tm=tm_h, tn=in_loc, tk=tk_s, stacked_nloc=in_loc, after=t_in, a_off=h * (half // tm_h)))
        t_in = start("scatter_proj_in_" + hname, {"w_in_" + hname: g_in[-1]})
    t_in = push("scatter_proj_in_a", t_in)
    dh1 = _mm_nt_stacked("proj_in_dx", pl.BlockSpec((tm, in_loc), lambda i, j, k: (i, k)), dz, w["in"], M=S, tm=tm, tn=D, tk=in_loc,
                         after=t_in)
    t_sum = scatter[2]("scatter_proj_in_b", dh1) if scatter is not None else None
    dx, vec1 = _norm_mod_bwd("norm1_bwd", dh1, x, p["norm1_g"], sc1, dxm, after=t_sum)

    dmod = jnp.concatenate([vec1[0:1], vec1[1:2], vec2[3:4], vec2[0:1], vec2[1:2], vec_l[2:3]], axis=1)
    small = dict(norm1_g=vec1[2:3], b_gate=db_gate.reshape(1, 2 * D), ln_g=dln[0:1], ln_b=dln[1:2], ws=dws, bs=dbs.reshape(G, T),
                 hg_lb=dhlb, hg_ng=dng[0:1], norm2_g=vec2[2:3], final_g=vec_l[1:2], loss=vec_l[0:1, 0:LANES])
    big = dict(w_in_a=g_in[0], w_in_b=g_in[1], bg=g_bg, bh=g_bh, out=g_out, fi=g_fi, fo=g_fo)
    return dx, big, small, dmod


_SMALL = ("b_ada", "norm1_g", "b_gate", "ln_g", "ln_b", "ws", "bs", "hg_lb", "hg_ng", "norm2_g", "final_g")


def _pack(parts, rows_mult=8):
    flat = [a.reshape(-1) for a in parts]
    offs, n = [], 0
    for a in flat:
        offs.append(n)
        n += a.shape[0]
    pad = (-n) % (LANES * rows_mult)
    if pad:
        flat.append(jnp.zeros((pad,), F32))
    return jnp.concatenate(flat).reshape(-1, LANES), offs


def kernel(x, c, w_ada, b_ada, norm1_g, w_in, b_gate, gmlp_ln_g, gmlp_ln_b, gmlp_ws, gmlp_bs, hg_lb, hg_norm_g, w_branch_gmlp, w_branch_hg, w_out, norm2_g, w_ffn_in, w_ffn_out, final_norm_g, loss_target, m_w_ada, m_b_ada, m_norm1_g, m_w_in, m_b_gate, m_gmlp_ln_g, m_gmlp_ln_b, m_gmlp_ws, m_gmlp_bs, m_hg_lb, m_hg_norm_g, m_w_branch_gmlp, m_w_branch_hg, m_w_out, m_norm2_g, m_w_ffn_in, m_w_ffn_out, m_final_norm_g, v_w_ada, v_b_ada, v_norm1_g, v_w_in, v_b_gate, v_gmlp_ln_g, v_gmlp_ln_b, v_gmlp_ws, v_gmlp_bs, v_hg_lb, v_hg_norm_g, v_w_branch_gmlp, v_w_branch_hg, v_w_out, v_norm2_g, v_w_ffn_in, v_w_ffn_out, v_final_norm_g):
    S, D = x.shape[1], x.shape[2]
    ada_loc = w_ada.shape[2]
    me = 4 * lax.axis_index("x") + 2 * lax.axis_index("y") + lax.axis_index("c")
    me_idx = me.astype(jnp.int32).reshape(1)

    def empty_hbm(shape, dtype):
        return pltpu.with_memory_space_constraint(lax.empty(shape, dtype), pltpu.HBM)

    groups = dict(gather_in=dict(keys=["in"], src=[w_in], forward=True),
                  gather_mixer=dict(keys=["bg", "bh", "out"], src=[w_branch_gmlp, w_branch_hg, w_out], forward=False),
                  gather_ffn_in=dict(keys=["fi"], src=[w_ffn_in], forward=True),
                  gather_ffn_out=dict(keys=["fo"], src=[w_ffn_out], forward=False))
    group_of = {k: gname for gname, g in groups.items() for k in g["keys"]}

    def first_hop(gname, after):
        g = groups[gname]
        n = len(g["keys"])
        cast = [_cast_shard(f"{gname}_cast_{k}", a[0], me_idx) for k, a in zip(g["keys"], g["src"])]
        shards, outs = [s for s, _ in cast], [o for _, o in cast]
        if g["forward"]:
            *g["hop"], token = _split_start(gname + "_hop1", shards + outs, n * 3, _forward_first_copies(n), after=after)
        else:
            *g["hop"], token = _split_start(gname + "_hop1", shards + outs, n * N_CHIP, _gather_first_copies(n), after=after)
        return token

    def second_hop(gname, after):
        g = groups[gname]
        n = len(g["keys"])
        *g["hop"], token = _split_relay(gname + "_hop2", g["hop"][2], g["hop"][0], g["hop"][1], after,
                                        _forward_first_copies(n), n * 4, _forward_second_copies(n))
        return token

    def finish(gname, after):
        g = groups[gname]
        n = len(g["keys"])
        send_sems, recv_sems, bufs = g["hop"]
        if g["forward"]:
            send_sems, recv_sems, bufs, _ = _split_relay(gname + "_hop3", bufs, send_sems, recv_sems, after,
                                                         _forward_second_copies(n), n, _forward_third_copies(n))
            bufs = _split_wait(gname + "_wait", bufs, send_sems, recv_sems, after, _forward_third_copies(n))
        else:
            send_sems, recv_sems, bufs, _ = _split_relay(gname + "_relay", bufs, send_sems, recv_sems, after,
                                                         _gather_first_copies(n), n * (N_CHIP - 1), _gather_relay_copies(n))
            bufs = _split_wait(gname + "_wait", bufs, send_sems, recv_sems, after, _gather_relay_copies(n))
        g["done"] = dict(zip(g["keys"], bufs[n:]))

    c_all = _allgather_small("gather_c", c.reshape(D // LANES, LANES)).reshape(N_DEV, D)
    token = first_hop("gather_in", c_all)
    mod_cols, c_act = _ada_mod(jnp.pad(c_all, ((0, 16 - N_DEV), (0, 0))) + token[0:1, 0:1], w_ada[0])
    mod_vec = mod_cols[:N_DEV].reshape(-1, LANES)
    mg_send, mg_recv, mg_bufs, token = _split_start(
        "gather_mod_start", [mod_vec, lax.dynamic_update_slice(lax.empty((N_DEV, *mod_vec.shape), F32), mod_vec[None], (me, 0, 0))],
        N_DEV - 1, _small_gather_copies)
    token = second_hop("gather_in", token)
    token = first_hop("gather_ffn_in", first_hop("gather_mixer", token))
    mod_all = _split_wait("gather_mod_wait", mg_bufs, mg_send, mg_recv, token, _small_gather_copies)[1].reshape(N_DEV, N_DEV, ada_loc)
    mod = lax.dynamic_index_in_dim(mod_all, me, axis=1, keepdims=False).reshape(1, N_DEV * ada_loc) + b_ada

    def fetch(key, after):
        if key == "in":
            finish("gather_in", after)
        elif key == "fi_early":
            return first_hop("gather_ffn_out", second_hop("gather_ffn_in", after))
        elif "done" not in groups[group_of[key]]:
            finish(group_of[key], after)
        arr = groups[group_of[key]]["done"][key]
        return arr.reshape(-1, D) if key in ("out", "fo") else arr

    p = dict(norm1_g=norm1_g, b_gate=b_gate, ln_g=gmlp_ln_g, ln_b=gmlp_ln_b, ws=gmlp_ws[0], bs=gmlp_bs[0], hg_lb=hg_lb,
             hg_ng=hg_norm_g, norm2_g=norm2_g, final_g=final_norm_g.reshape(1, D))

    in_flight = {}
    c_idx = lax.axis_index("c").astype(jnp.int32).reshape(1)
    chip_idx = (2 * lax.axis_index("x") + lax.axis_index("y")).astype(jnp.int32).reshape(1)

    def scatter_start(name, grads):
        keys = list(grads)
        n = len(keys)
        stacks = [grads[k].reshape(N_DEV, -1, grads[k].shape[-1]) for k in keys]
        lands = [empty_hbm((N_CHIP, *g.shape[1:]), g.dtype) for g in stacks]
        send_sems, recv_sems, bufs, token = _split_start(name + "_d2d", stacks + lands, n * N_CHIP, _to_sibling_copies(n))
        in_flight[name] = dict(keys=keys, stage1=(send_sems, recv_sems, bufs))
        return token

    def scatter_sum(name, after):
        f = in_flight[name]
        n = len(f["keys"])
        send_sems, recv_sems, bufs = f["stage1"]
        bufs = _split_wait(name + "_d2d_wait", bufs, send_sems, recv_sems, after, _to_sibling_copies(n))
        f["sums"] = [_chip_sum(f"{name}_sum_{k}", bufs[i], bufs[n + i], c_idx) for i, k in enumerate(f["keys"])]
        return f["sums"][-1]

    def scatter_push(name, after):
        f = in_flight[name]
        n = len(f["keys"])
        if "sums" not in f:
            scatter_sum(name, after)
        lands = [empty_hbm((N_CHIP - 1, *s.shape[1:]), s.dtype) for s in f["sums"]]
        send_sems, recv_sems, bufs, token = _split_start(name + "_ici", f["sums"] + lands, n * (N_CHIP - 1), _to_owner_copies(n), after=after)
        f["stage2"] = (send_sems, recv_sems, bufs)
        return token

    grad_x, _, small, dmod = _local_step(x[0], loss_target[0], mod, p, fetch, w_ffn_out.shape[1] * N_DEV,
                                         (scatter_start, scatter_push, scatter_sum))

    small["b_ada"] = dmod
    packed, offs = _pack([small[k] for k in _SMALL] + [small["loss"]])
    sg_send, sg_recv, sg_bufs, t_tail = _split_start(
        "gather_small_start", [packed, lax.dynamic_update_slice(lax.empty((N_DEV, *packed.shape), F32), packed[None], (me, 0, 0))],
        N_DEV - 1, _small_gather_copies)
    t_tail = scatter_push("scatter_proj_in_b", t_tail)
    big_w = dict(w_in=(w_in, m_w_in, v_w_in, "w_in"), bg=(w_branch_gmlp, m_w_branch_gmlp, v_w_branch_gmlp, "w_branch_gmlp"),
                 bh=(w_branch_hg, m_w_branch_hg, v_w_branch_hg, "w_branch_hg"), out=(w_out, m_w_out, v_w_out, "w_out"),
                 fi=(w_ffn_in, m_w_ffn_in, v_w_ffn_in, "w_ffn_in"), fo=(w_ffn_out, m_w_ffn_out, v_w_ffn_out, "w_ffn_out"))
    upd = {}

    def land_and_update(name, after):
        keys = in_flight[name]["keys"]
        n = len(keys)
        send_sems, recv_sems, bufs = in_flight[name]["stage2"]
        bufs = _split_wait(name + "_ici_wait", bufs, send_sems, recv_sems, after, _to_owner_copies(n))
        for i, k in enumerate(keys):
            if k in big_w:
                wt, mt, vt, out_name = big_w[k]
                upd[out_name] = _adamw("adamw_" + out_name, wt[0], mt[0], vt[0], bufs[i], chip_idx, [bufs[n + i]])
            else:
                wt, mt, vt, out_name = big_w["w_in"]
                upd[out_name] = _adamw("adamw_" + k, wt[0], mt[0], vt[0], bufs[i], chip_idx, [bufs[n + i]],
                                       row0=0 if k == "w_in_a" else bufs[i].shape[1], into=upd.get(out_name))
            after = upd[out_name][1]
        return after

    after = land_and_update("scatter_mixer", land_and_update("scatter_ffn", t_tail))
    gathered = _split_wait("gather_small_wait", sg_bufs, sg_send, sg_recv, after, _small_gather_copies)[1]
    wp = dict(p, b_ada=b_ada)
    ms = dict(b_ada=m_b_ada, norm1_g=m_norm1_g, b_gate=m_b_gate, ln_g=m_gmlp_ln_g, ln_b=m_gmlp_ln_b, ws=m_gmlp_ws, bs=m_gmlp_bs,
              hg_lb=m_hg_lb, hg_ng=m_hg_norm_g, norm2_g=m_norm2_g, final_g=m_final_norm_g)
    vs = dict(b_ada=v_b_ada, norm1_g=v_norm1_g, b_gate=v_b_gate, ln_g=v_gmlp_ln_g, ln_b=v_gmlp_ln_b, ws=v_gmlp_ws, bs=v_gmlp_bs,
              hg_lb=v_hg_lb, hg_ng=v_hg_norm_g, norm2_g=v_norm2_g, final_g=v_final_norm_g)
    w_sm, _ = _pack([wp[k] for k in _SMALL])
    m_sm, _ = _pack([ms[k] for k in _SMALL])
    v_sm, _ = _pack([vs[k] for k in _SMALL])
    shapes = dict(b_ada=b_ada.shape, norm1_g=norm1_g.shape, b_gate=b_gate.shape, ln_g=gmlp_ln_g.shape, ln_b=gmlp_ln_b.shape,
                  ws=gmlp_ws.shape, bs=gmlp_bs.shape, hg_lb=hg_lb.shape, hg_ng=hg_norm_g.shape, norm2_g=norm2_g.shape,
                  final_g=final_norm_g.shape)
    sm_out = _small_update(gathered, w_sm, m_sm, v_sm, after, [math.prod(shapes[k]) // LANES for k in _SMALL])

    def unpack(idx, k):
        return sm_out[idx * len(_SMALL) + _SMALL.index(k)].reshape(shapes[k])

    loss = sm_out[-1][0, 0]

    assert offs[0] == 0 and ada_loc % LANES == 0
    dmod_loc = lax.dynamic_slice_in_dim(gathered, me * (ada_loc // LANES), ada_loc // LANES, axis=1).reshape(N_DEV, ada_loc)
    ca_t = jnp.pad(c_act[:N_DEV].T, ((0, 0), (0, LANES - N_DEV))).astype(BF16)
    dm_p = jnp.pad(dmod_loc, ((0, LANES - N_DEV), (0, 0))).astype(BF16)
    tm_a = _tile(D, 512)
    g_ada = _matmul(
        "ada_dw", ca_t, dm_p, dims=_NN, grid_mnk=(D // tm_a, 1, 1), tiles=(tm_a, ada_loc),
        a_spec=pl.BlockSpec((tm_a, LANES), lambda i, j, k: (i, 0)), b_spec=pl.BlockSpec((LANES, ada_loc), lambda i, j, k: (0, 0)),
        out_shapes=[jax.ShapeDtypeStruct((1, D, ada_loc), F32)], out_specs=[pl.BlockSpec((None, tm_a, ada_loc), lambda i, j, k: (0, i, 0))],
        epilogue=_store(F32))[0]
    upd["w_ada"] = _adamw("adamw_w_ada", w_ada[0], m_w_ada[0], v_w_ada[0], g_ada, jnp.zeros((1,), jnp.int32))
    land_and_update("scatter_proj_in_b", land_and_update("scatter_proj_in_a", upd["w_ada"][1]))

    order = ("w_ada", "b_ada", "norm1_g", "w_in", "b_gate", "ln_g", "ln_b", "ws", "bs", "hg_lb", "hg_ng", "w_branch_gmlp", "w_branch_hg",
             "w_out", "norm2_g", "w_ffn_in", "w_ffn_out", "final_g")
    outs = [loss, grad_x[None]]
    for idx in range(4):
        for k in order:
            outs.append(upd[k][idx][None] if k in upd else unpack(idx, k))
    return tuple(outs)
```

```python
import functools
import math

import jax
import jax.numpy as jnp
from jax import lax
from jax.experimental import pallas as pl
from jax.experimental.pallas import tpu as pltpu

F32 = jnp.float32
BF16 = jnp.bfloat16
N_DEV = 8
EPS = 1e-6
LANES = 128
HG_DK = 128
HG_CHUNK = 64
HG_MID = HG_CHUNK // 2 - 1
EXP_CLAMP = 80.0
VMEM_LIMIT = 48 * 1024 * 1024
BF16_ROWS = 16
STREAM_TILE = 1 << 20
ADAMW_TILE = 3 << 17
ADAM_LR, ADAM_B1, ADAM_B2, ADAM_EPS, ADAM_WD, ADAM_STEP = 0.001, 0.9, 0.999, 1e-08, 0.01, 10
MESH = pl.DeviceIdType.MESH

_NN = (((1,), (0,)), ((), ()))
_NT = (((1,), (1,)), ((), ()))
_TN = (((0,), (0,)), ((), ()))


def _dot(a, b, dims=_NN):
    return lax.dot_general(a.astype(BF16), b.astype(BF16), dims, preferred_element_type=F32)


def _tile(n, target, mult=LANES):
    best = None
    for t in range(mult, min(n, target) + 1, mult):
        if n % t == 0:
            best = t
    return n if best is None else best


def _cparams(sem):
    return pltpu.CompilerParams(dimension_semantics=sem, vmem_limit_bytes=VMEM_LIMIT)


def _sigmoid(x):
    return 1.0 / (1.0 + jnp.exp(-x))


def _gelu_parts(x):
    k0 = math.sqrt(2.0 / math.pi)
    x2 = x * x
    t = jnp.tanh(k0 * (x + 0.044715 * x * x2))
    g = 0.5 * x * (1.0 + t)
    dg = 0.5 * (1.0 + t) + 0.5 * x * (1.0 - t * t) * (k0 * (1.0 + 3.0 * 0.044715 * x2))
    return g, dg


def _split3(x):
    h = x.astype(BF16)
    r = x - h.astype(F32)
    m = r.astype(BF16)
    lo = (r - m.astype(F32)).astype(BF16)
    return h, m, lo


def _ones_dot(mat01, x):
    h, m, lo = _split3(x)
    d = functools.partial(lax.dot_general, dimension_numbers=_NN, preferred_element_type=F32)
    return d(mat01, h) + d(mat01, m) + d(mat01, lo)


def _matmul(name, a, b, *, dims, grid_mnk, tiles, a_spec, b_spec, extras=(), extra_specs=(), out_shapes, out_specs, epilogue, after=None,
            sem=None):
    gm, gn, nk = grid_mnk
    tm, tn = tiles
    n_ex, n_out = len(extras), len(out_shapes)
    held = [] if after is None else [after]

    def body(*refs):
        a_ref, b_ref = refs[0], refs[1]
        ex = refs[2:2 + n_ex]
        outs = refs[2 + n_ex + len(held):2 + n_ex + len(held) + n_out]
        more = () if sem is None else (pl.program_id(0) == 0,)
        if nk == 1:
            epilogue(lax.dot_general(a_ref[...], b_ref[...], dims, preferred_element_type=F32), ex, outs, *more)
            return
        acc = refs[-1]
        k = pl.program_id(2)

        @pl.when(k == 0)
        def _():
            acc[...] = jnp.zeros_like(acc)

        acc[...] += lax.dot_general(a_ref[...], b_ref[...], dims, preferred_element_type=F32)

        @pl.when(k == nk - 1)
        def _():
            epilogue(acc[...], ex, outs, *more)

    return pl.pallas_call(
        body, name=name, grid=(gm, gn, nk), in_specs=[a_spec, b_spec, *extra_specs] + [pl.BlockSpec(memory_space=pl.ANY)] * len(held),
        out_specs=list(out_specs), out_shape=list(out_shapes), scratch_shapes=[] if nk == 1 else [pltpu.VMEM((tm, tn), F32)],
        compiler_params=_cparams(sem or ("parallel", "parallel", "arbitrary")),
    )(a, b, *extras, *held)


def _store(dtype):
    def ep(acc, ex, outs):
        outs[0][...] = acc.astype(dtype)
    return ep


def _mm_nn_stacked(name, a, wg, *, tm, tn, tk, out_dtype=F32, extras=(), extra_specs=(), out_shapes=None, out_specs=None, epilogue=None,
                   after=None):
    M, K = a.shape
    _, _, nloc = wg.shape
    N = nloc * N_DEV
    q = nloc // tn
    if out_shapes is None:
        out_shapes = [jax.ShapeDtypeStruct((M, N), out_dtype)]
        out_specs = [pl.BlockSpec((tm, tn), lambda i, j, k: (i, j))]
        epilogue = _store(out_dtype)
    return _matmul(
        name, a, wg, dims=_NN, grid_mnk=(M // tm, N // tn, K // tk), tiles=(tm, tn),
        a_spec=pl.BlockSpec((tm, tk), lambda i, j, k: (i, k)),
        b_spec=pl.BlockSpec((None, tk, tn), lambda i, j, k: (j // q, k, j % q)),
        extras=extras, extra_specs=extra_specs, out_shapes=out_shapes, out_specs=out_specs, epilogue=epilogue, after=after)


def _mm_nt_stacked(name, a_spec, a, wg, *, M, tm, tn, tk, out_dtype=F32, after=None, extras=(), extra_specs=(), out_shapes=None,
                   out_specs=None, epilogue=None, sem=None):
    _, Kw, nloc = wg.shape
    q = nloc // tk
    single = out_shapes is None
    if single:
        out_shapes = [jax.ShapeDtypeStruct((M, Kw), out_dtype)]
        out_specs = [pl.BlockSpec((tm, tn), lambda i, j, k: (i, j))]
        epilogue = _store(out_dtype)
    res = _matmul(
        name, a, wg, dims=_NT, grid_mnk=(M // tm, Kw // tn, (nloc * N_DEV) // tk), tiles=(tm, tn),
        a_spec=a_spec, b_spec=pl.BlockSpec((None, tn, tk), lambda i, j, k: (k // q, j, k % q)),
        extras=extras, extra_specs=extra_specs, out_shapes=out_shapes, out_specs=out_specs, epilogue=epilogue, after=after, sem=sem)
    return res[0] if single else res


def _mm_tn(name, a, b, b_spec, *, Mo, No, S, tm, tn, tk, stacked_nloc=None, after=None, a_off=0):
    if stacked_nloc is None:
        out_shape = jax.ShapeDtypeStruct((Mo, No), BF16)
        out_spec = pl.BlockSpec((tm, tn), lambda i, j, k: (i, j))
    else:
        q = stacked_nloc // tn
        out_shape = jax.ShapeDtypeStruct((N_DEV, Mo, stacked_nloc), BF16)
        out_spec = pl.BlockSpec((None, tm, tn), lambda i, j, k: (j // q, i, j % q))
    return _matmul(
        name, a, b, dims=_TN, grid_mnk=(Mo // tm, No // tn, S // tk), tiles=(tm, tn),
        a_spec=pl.BlockSpec((tk, tm), lambda i, j, k: (k, i + a_off)), b_spec=b_spec,
        out_shapes=[out_shape], out_specs=[out_spec], epilogue=_store(BF16), after=after)[0]


def _norm_mod(name, x, g, sc, sh):
    S, D = x.shape
    tm = _tile(S, 256, 8)

    def body(x_ref, g_ref, sc_ref, sh_ref, h_ref):
        xv = x_ref[...]
        r = lax.rsqrt(jnp.mean(xv * xv, axis=-1, keepdims=True) + EPS)
        h = (xv * r) * g_ref[...]
        h_ref[...] = (h * (1.0 + sc_ref[...]) + sh_ref[...]).astype(BF16)

    row = pl.BlockSpec((tm, D), lambda i: (i, 0))
    vec = pl.BlockSpec((1, D), lambda i: (0, 0))
    return pl.pallas_call(body, name=name, grid=(S // tm,), in_specs=[row, vec, vec, vec], out_specs=row,
                          out_shape=jax.ShapeDtypeStruct((S, D), BF16), compiler_params=_cparams(("parallel",)))(x, g, sc, sh)


def _norm_mod_bwd_rows(first, dh_v, x_ref, g_ref, sc_ref, dres_ref, dx_ref, vec_ref, o_ref=None, gt_ref=None, do_ref=None):
    @pl.when(first)
    def _():
        vec_ref[...] = jnp.zeros_like(vec_ref)

    xv, gv = x_ref[...], g_ref[...]
    r = lax.rsqrt(jnp.mean(xv * xv, axis=-1, keepdims=True) + EPS)
    xn = xv * r
    one_sc = 1.0 + sc_ref[...]
    vec_ref[0:1, :] += jnp.sum(dh_v, axis=0, keepdims=True)
    vec_ref[1:2, :] += jnp.sum(dh_v * (xn * gv), axis=0, keepdims=True)
    vec_ref[2:3, :] += jnp.sum(dh_v * one_sc * xn, axis=0, keepdims=True)
    dxn = dh_v * one_sc * gv
    dx = dres_ref[...] + r * (dxn - xn * jnp.mean(dxn * xn, axis=-1, keepdims=True))
    dx_ref[...] = dx
    if o_ref is not None:
        vec_ref[3:4, :] += jnp.sum(dx * o_ref[...], axis=0, keepdims=True)
        do_ref[...] = (dx * gt_ref[...]).astype(BF16)


def _norm_mod_bwd(name, dh, x, g, sc, dres, o=None, gt=None):
    S, D = x.shape
    tm = _tile(S, 256, 8)
    gated = o is not None

    def body(*refs):
        if gated:
            dh_ref, x_ref, g_ref, sc_ref, dres_ref, o_ref, gt_ref, dx_ref, vec_ref, do_ref = refs
        else:
            dh_ref, x_ref, g_ref, sc_ref, dres_ref, dx_ref, vec_ref = refs
            o_ref = gt_ref = do_ref = None
        _norm_mod_bwd_rows(pl.program_id(0) == 0, dh_ref[...], x_ref, g_ref, sc_ref, dres_ref, dx_ref, vec_ref, o_ref, gt_ref, do_ref)

    row = pl.BlockSpec((tm, D), lambda i: (i, 0))
    vec = pl.BlockSpec((1, D), lambda i: (0, 0))
    acc = pl.BlockSpec((8, D), lambda i: (0, 0))
    ins = [dh, x, g, sc, dres] + ([o, gt] if gated else [])
    in_specs = [row, row, vec, vec, row] + ([row, vec] if gated else [])
    out_shape = [jax.ShapeDtypeStruct((S, D), F32), jax.ShapeDtypeStruct((8, D), F32)]
    out_specs = [row, acc]
    if gated:
        out_shape.append(jax.ShapeDtypeStruct((S, D), BF16))
        out_specs.append(row)
    return pl.pallas_call(body, name=name, grid=(S // tm,), in_specs=in_specs, out_specs=out_specs, out_shape=out_shape,
                          compiler_params=_cparams(("arbitrary",)))(*ins)


def _loss_head(x3, tgt, gf, o2, gt2):
    S, D = x3.shape
    tm = _tile(S, 256, 8)

    def body(x_ref, t_ref, g_ref, o_ref, gt_ref, dx_ref, do_ref, vec_ref):
        i = pl.program_id(0)

        @pl.when(i == 0)
        def _():
            vec_ref[...] = jnp.zeros_like(vec_ref)

        xv, gv = x_ref[...], g_ref[...]
        r = lax.rsqrt(jnp.mean(xv * xv, axis=-1, keepdims=True) + EPS)
        xn = xv * r
        e = xn * gv - t_ref[...]
        tok = 0.5 * jnp.mean(e * e, axis=-1, keepdims=True)
        vec_ref[0:1, :] += jnp.broadcast_to(jnp.sum(tok, axis=0, keepdims=True), (1, D))
        dy = e * (1.0 / D)
        vec_ref[1:2, :] += jnp.sum(dy * xn, axis=0, keepdims=True)
        dxn = dy * gv
        dx = r * (dxn - xn * jnp.mean(dxn * xn, axis=-1, keepdims=True))
        dx_ref[...] = dx
        vec_ref[2:3, :] += jnp.sum(dx * o_ref[...], axis=0, keepdims=True)
        do_ref[...] = (dx * gt_ref[...]).astype(BF16)

    row = pl.BlockSpec((tm, D), lambda i: (i, 0))
    vec = pl.BlockSpec((1, D), lambda i: (0, 0))
    return pl.pallas_call(
        body, name="loss_head", grid=(S // tm,), in_specs=[row, row, vec, row, vec],
        out_specs=[row, row, pl.BlockSpec((8, D), lambda i: (0, 0))],
        out_shape=[jax.ShapeDtypeStruct((S, D), F32), jax.ShapeDtypeStruct((S, D), BF16), jax.ShapeDtypeStruct((8, D), F32)],
        compiler_params=_cparams(("arbitrary",)))(x3, tgt, gf, o2, gt2)


def _ffn_in_swiglu(h, wg):
    S, D = h.shape
    _, _, tf = wg.shape
    nf = N_DEV // 2
    F = nf * tf
    tm = _tile(S, 256, 16)

    def body(h_ref, wa_ref, wu_ref, hf_ref, fac_ref):
        hv = h_ref[...]
        a = lax.dot_general(hv, wa_ref[...], _NN, preferred_element_type=F32)
        up = lax.dot_general(hv, wu_ref[...], _NN, preferred_element_type=F32)
        sa = _sigmoid(a)
        silu = a * sa
        hf_ref[...] = (silu * up).astype(BF16)
        fac_ref[0] = (up * (sa * (1.0 + a * (1.0 - sa)))).astype(BF16)
        fac_ref[1] = silu.astype(BF16)

    return pl.pallas_call(
        body, name="ffn_in_swiglu", grid=(nf, S // tm),
        in_specs=[pl.BlockSpec((tm, D), lambda j, i: (i, 0)), pl.BlockSpec((None, D, tf), lambda j, i: (j, 0, 0)),
                  pl.BlockSpec((None, D, tf), lambda j, i: (j + nf, 0, 0))],
        out_specs=[pl.BlockSpec((tm, tf), lambda j, i: (i, j)), pl.BlockSpec((2, tm, tf), lambda j, i: (0, i, j))],
        out_shape=[jax.ShapeDtypeStruct((S, F), BF16), jax.ShapeDtypeStruct((2, S, F), BF16)],
        compiler_params=_cparams(("parallel", "parallel")))(h, wg, wg)


def _gmlp_common(u_ref, v_ref, lg_ref, lb_ref, ws_ref, bsb_ref, G, T, Dg):
    ug, dug = _gelu_parts(u_ref[...])
    vg, dvg = _gelu_parts(v_ref[...])
    mu = jnp.mean(vg, axis=-1, keepdims=True)
    vc = vg - mu
    rstd = lax.rsqrt(jnp.mean(vc * vc, axis=-1, keepdims=True) + EPS)
    vhat = vc * rstd
    vn = vhat * lg_ref[...] + lb_ref[...]
    row = lax.broadcasted_iota(jnp.int32, (T, T), 0)
    col = lax.broadcasted_iota(jnp.int32, (T, T), 1)
    tril = row >= col
    s = []
    for g in range(G):
        w = jnp.where(tril, ws_ref[g], 0.0)
        s.append(_dot(w, vn[:, g * Dg:(g + 1) * Dg]) + bsb_ref[g])
    return ug, dug, dvg, rstd, vhat, vn, tril, s


def _gmlp_fwd(z, ln_g, ln_b, ws, bsb, GW):
    S = z.shape[0]
    G, T, _ = ws.shape
    Dg = GW // G

    def body(u_ref, v_ref, lg_ref, lb_ref, ws_ref, bsb_ref, ya_ref):
        ug, _, _, _, _, _, _, s = _gmlp_common(u_ref, v_ref, lg_ref, lb_ref, ws_ref, bsb_ref, G, T, Dg)
        for g in range(G):
            sl = slice(g * Dg, (g + 1) * Dg)
            ya_ref[:, sl] = (ug[:, sl] * s[g]).astype(BF16)

    vec = pl.BlockSpec((1, GW), lambda c: (0, 0))
    return pl.pallas_call(
        body, name="gmlp_fwd", grid=(S // T,),
        in_specs=[pl.BlockSpec((T, GW), lambda c: (c, 0)), pl.BlockSpec((T, GW), lambda c: (c, 1)), vec, vec,
                  pl.BlockSpec((G, T, T), lambda c: (0, 0, 0)), pl.BlockSpec((G, T, Dg), lambda c: (0, 0, 0))],
        out_specs=pl.BlockSpec((T, GW), lambda c: (c, 0)), out_shape=jax.ShapeDtypeStruct((S, GW), BF16),
        compiler_params=_cparams(("parallel",)))(z, z, ln_g, ln_b, ws, bsb)


def _gmlp_bwd(z, dya, ln_g, ln_b, ws, bsb, GW):
    S = z.shape[0]
    G, T, _ = ws.shape
    Dg = GW // G
    nc = S // T

    def body(u_ref, v_ref, dya_ref, lg_ref, lb_ref, ws_ref, bsb_ref, dz_ref, dln_ref, dws_ref, dbs_ref, dbs_acc, dvh):
        c = pl.program_id(0)

        @pl.when(c == 0)
        def _():
            dln_ref[...] = jnp.zeros_like(dln_ref)
            dws_ref[...] = jnp.zeros_like(dws_ref)
            dbs_acc[...] = jnp.zeros_like(dbs_acc)

        ug, dug, dvg, rstd, vhat, vn, tril, s = _gmlp_common(u_ref, v_ref, lg_ref, lb_ref, ws_ref, bsb_ref, G, T, Dg)
        dya_v = dya_ref[...]
        for g in range(G):
            sl = slice(g * Dg, (g + 1) * Dg)
            dy_g = dya_v[:, sl]
            dz_ref[:, sl] = (dy_g * s[g] * dug[:, sl]).astype(BF16)
            ds = dy_g * ug[:, sl]
            dbs_acc[g] += ds
            w = jnp.where(tril, ws_ref[g], 0.0)
            dvn_g = _dot(w, ds, _TN)
            dws_ref[g] += jnp.where(tril, _dot(ds, vn[:, sl], _NT), 0.0)
            dln_ref[0:1, sl] += jnp.sum(dvn_g * vhat[:, sl], axis=0, keepdims=True)
            dln_ref[1:2, sl] += jnp.sum(dvn_g, axis=0, keepdims=True)
            dvh[:, sl] = dvn_g * lg_ref[:, sl]
        dvhat = dvh[...]
        m1 = jnp.mean(dvhat, axis=-1, keepdims=True)
        m2 = jnp.mean(dvhat * vhat, axis=-1, keepdims=True)
        dz_ref[:, GW:2 * GW] = (rstd * (dvhat - m1 - vhat * m2) * dvg).astype(BF16)

        @pl.when(c == nc - 1)
        def _():
            for g in range(G):
                dbs_ref[g] = jnp.sum(dbs_acc[g], axis=-1, keepdims=True)

    vec = pl.BlockSpec((1, GW), lambda c: (0, 0))
    return pl.pallas_call(
        body, name="gmlp_bwd", grid=(nc,),
        in_specs=[pl.BlockSpec((T, GW), lambda c: (c, 0)), pl.BlockSpec((T, GW), lambda c: (c, 1)),
                  pl.BlockSpec((T, GW), lambda c: (c, 0)), vec, vec,
                  pl.BlockSpec((G, T, T), lambda c: (0, 0, 0)), pl.BlockSpec((G, T, Dg), lambda c: (0, 0, 0))],
        out_specs=[pl.BlockSpec((T, 2 * GW), lambda c: (c, 0)), pl.BlockSpec((8, GW), lambda c: (0, 0)),
                   pl.BlockSpec((G, T, T), lambda c: (0, 0, 0)), pl.BlockSpec((G, T, 1), lambda c: (0, 0, 0))],
        out_shape=[jax.ShapeDtypeStruct((S, 2 * GW), BF16), jax.ShapeDtypeStruct((8, GW), F32),
                   jax.ShapeDtypeStruct((G, T, T), F32), jax.ShapeDtypeStruct((G, T, 1), F32)],
        scratch_shapes=[pltpu.VMEM((G, T, Dg), F32), pltpu.VMEM((T, GW), F32)],
        compiler_params=_cparams(("arbitrary",)))(z, z, dya, ln_g, ln_b, ws, bsb)


def _hg_common(q_ref, f_ref, hlb_ref):
    C = HG_CHUNK
    a = hlb_ref[...]
    lb = _sigmoid(a[0:1, :] - a[1:2, :])
    sig = _sigmoid(f_ref[...])
    f = lb + (1.0 - lb) * sig
    lf = jnp.log(f)
    kk = 1.0 - f
    q = q_ref[...]
    sq = _sigmoid(q)
    qa = q * sq
    row = lax.broadcasted_iota(jnp.int32, (C, C), 0)
    col = lax.broadcasted_iota(jnp.int32, (C, C), 1)
    tril = row >= col
    b = _ones_dot(tril.astype(BF16), lf)
    bm = b[HG_MID:HG_MID + 1, :]
    bl = b[C - 1:C, :]
    e_b = jnp.exp(b)
    e_qm = jnp.exp(jnp.minimum(b - bm, EXP_CLAMP))
    e_km = jnp.exp(jnp.minimum(bm - b, EXP_CLAMP))
    e_kl = jnp.exp(bl - b)
    return dict(lb=lb, sig=sig, f=f, kk=kk, q=q, sq=sq, qa=qa, tril=tril, e_b=e_b, e_qm=e_qm, e_km=e_km, e_kl=e_kl,
                e_l=jnp.exp(bl), qh=qa * e_b, qt=qa * e_qm, kt=kk * e_km, kh=kk * e_kl)


def _hg_fwd(z, hg_lb, ng, HW):
    S = z.shape[0]
    C, H, dk = HG_CHUNK, HW // HG_DK, HG_DK
    nc = S // C

    def body(q_ref, f_ref, i_ref, og_ref, hlb_ref, ng_ref, yb_ref, o_ref, st_ref, state):
        @pl.when(pl.program_id(0) == 0)
        def _():
            state[...] = jnp.zeros_like(state)

        t = _hg_common(q_ref, f_ref, hlb_ref)
        iv = i_ref[...]
        for h in range(H):
            sl = slice(h * dk, (h + 1) * dk)
            st = state[h]
            st_ref[h] = st
            a = jnp.where(t["tril"], _dot(t["qt"][:, sl], t["kt"][:, sl], _NT), 0.0)
            o_h = _dot(a, iv[:, sl]) + _dot(t["qh"][:, sl], st, _NT)
            state[h] = st * t["e_l"][:, sl] + _dot(iv[:, sl], t["kh"][:, sl], _TN)
            o_ref[:, sl] = o_h
            rr = lax.rsqrt(jnp.mean(o_h * o_h, axis=-1, keepdims=True) + EPS)
            og = og_ref[:, sl]
            yb_ref[:, sl] = (o_h * rr * ng_ref[:, sl] * (og * _sigmoid(og))).astype(BF16)

    def col(k):
        return pl.BlockSpec((C, HW), lambda c: (c, k))

    base = 2
    return pl.pallas_call(
        body, name="hgrn_fwd", grid=(nc,),
        in_specs=[col(base), col(base + 1), col(base + 2), col(base + 3),
                  pl.BlockSpec((2, HW), lambda c: (0, 0)), pl.BlockSpec((1, HW), lambda c: (0, 0))],
        out_specs=[pl.BlockSpec((C, HW), lambda c: (c, 0)), pl.BlockSpec((C, HW), lambda c: (c, 0)),
                   pl.BlockSpec((None, H, dk, dk), lambda c: (c, 0, 0, 0))],
        out_shape=[jax.ShapeDtypeStruct((S, HW), BF16), jax.ShapeDtypeStruct((S, HW), F32),
                   jax.ShapeDtypeStruct((nc, H, dk, dk), F32)],
        scratch_shapes=[pltpu.VMEM((H, dk, dk), F32)],
        compiler_params=_cparams(("arbitrary",)))(z, z, z, z, hg_lb, ng)


def _hg_bwd(z, o, states, dyb, hg_lb, ng, HW, dz_head, dz_tail):
    S = z.shape[0]
    C, H, dk = HG_CHUNK, HW // HG_DK, HG_DK
    nc = S // C
    B0 = dz_head.shape[1]
    DT = dz_tail.shape[2]
    INW = B0 + 4 * HW + 2 * DT

    def body(q_ref, f_ref, i_ref, og_ref, o_ref, st_ref, stn_ref, dyb_ref, hlb_ref, ng_ref, head_ref, tail_ref,
             dzf_ref, dng_ref, dhlb_ref, dtail_ref, dstate, cross, dqa_buf, dkk_buf, db_buf, dlb_acc):
        c = pl.program_id(0)
        dzf_ref[:, 0:B0] = head_ref[...]
        dzf_ref[:, B0 + 4 * HW:B0 + 4 * HW + DT] = tail_ref[0]
        dzf_ref[:, B0 + 4 * HW + DT:INW] = tail_ref[1]
        dz_ref = dzf_ref.at[:, B0:B0 + 4 * HW]

        @pl.when(c == 0)
        def _():
            dtail_ref[...] = jnp.zeros_like(dtail_ref)

        dtail_ref[0:1, :] += jnp.sum(tail_ref[0].astype(F32), axis=0, keepdims=True)
        dtail_ref[1:2, :] += jnp.sum(tail_ref[1].astype(F32), axis=0, keepdims=True)

        @pl.when(c == 0)
        def _():
            dstate[...] = jnp.zeros_like(dstate)
            dlb_acc[...] = jnp.zeros_like(dlb_acc)
            dng_ref[...] = jnp.zeros_like(dng_ref)

        def r16(v):
            return v.astype(BF16).astype(F32)

        t = _hg_common(q_ref, f_ref, hlb_ref)
        iv = i_ref[...]
        for h in range(H):
            sl = slice(h * dk, (h + 1) * dk)
            o_h, og, dyb_h, ng_h = o_ref[:, sl], og_ref[:, sl], dyb_ref[:, sl], ng_ref[:, sl]
            sg = _sigmoid(og)
            silu_og = og * sg
            rr = lax.rsqrt(jnp.mean(o_h * o_h, axis=-1, keepdims=True) + EPS)
            on = o_h * rr
            dng_ref[0:1, sl] += jnp.sum(dyb_h * on * silu_og, axis=0, keepdims=True)
            dz_ref[:, 3 * HW + h * dk:3 * HW + (h + 1) * dk] = (dyb_h * on * ng_h * (sg * (1.0 + og * (1.0 - sg)))).astype(BF16)
            don = dyb_h * ng_h * silu_og
            do_h = rr * (don - on * jnp.mean(don * on, axis=-1, keepdims=True))

            qt, kt, qh, kh, iv_h = t["qt"][:, sl], t["kt"][:, sl], t["qh"][:, sl], t["kh"][:, sl], iv[:, sl]
            a = jnp.where(t["tril"], _dot(qt, kt, _NT), 0.0)
            da = jnp.where(t["tril"], _dot(do_h, iv_h, _NT), 0.0)
            st, dst = st_ref[h], dstate[h]
            cross[:, sl] = jnp.sum(dst * stn_ref[h], axis=0, keepdims=True)
            dqh = _dot(do_h, st)
            dstate[h] = _dot(do_h, qh, _TN) + dst * t["e_l"][:, sl]
            div = _dot(a, do_h, _TN) + _dot(kh, dst, _NT)
            dkh = _dot(iv_h, dst)
            dqt = _dot(da, kt)
            dkt = _dot(da, qt, _TN)
            dz_ref[:, 2 * HW + h * dk:2 * HW + (h + 1) * dk] = div.astype(BF16)
            dqa_buf[:, sl] = dqh * t["e_b"][:, sl] + dqt * t["e_qm"][:, sl]
            dkk_buf[:, sl] = dkt * t["e_km"][:, sl] + dkh * t["e_kl"][:, sl]
            db_buf[:, sl] = r16(qt) * dqt - r16(kt) * dkt + r16(qh) * dqh - r16(kh) * dkh

        dqa, dkk = dqa_buf[...], dkk_buf[...]
        triu = jnp.logical_not(t["tril"]) | (lax.broadcasted_iota(jnp.int32, (C, C), 0) == lax.broadcasted_iota(jnp.int32, (C, C), 1))
        dlf = _ones_dot(triu.astype(BF16), db_buf[...]) + cross[...]
        df = dlf / t["f"] - dkk
        sig, lb = t["sig"], t["lb"]
        dz_ref[:, HW:2 * HW] = (df * (1.0 - lb) * sig * (1.0 - sig)).astype(BF16)
        dlb_acc[...] += jnp.sum(df * (1.0 - sig), axis=0, keepdims=True)
        q, sq = t["q"], t["sq"]
        dz_ref[:, 0:HW] = (dqa * (sq * (1.0 + q * (1.0 - sq)))).astype(BF16)

        @pl.when(c == nc - 1)
        def _():
            da0 = dlb_acc[...] * lb * (1.0 - lb)
            dhlb_ref[0:1, :] = da0
            dhlb_ref[1:2, :] = -da0

    def col(k):
        return pl.BlockSpec((C, HW), lambda c: (nc - 1 - c, k))

    base = 2
    return pl.pallas_call(
        body, name="hgrn_bwd", grid=(nc,),
        in_specs=[col(base), col(base + 1), col(base + 2), col(base + 3), col(0),
                  pl.BlockSpec((None, H, dk, dk), lambda c: (nc - 1 - c, 0, 0, 0)),
                  pl.BlockSpec((None, H, dk, dk), lambda c: (jnp.minimum(nc - c, nc - 1), 0, 0, 0)), col(0),
                  pl.BlockSpec((2, HW), lambda c: (0, 0)), pl.BlockSpec((1, HW), lambda c: (0, 0)),
                  pl.BlockSpec((C, B0), lambda c: (nc - 1 - c, 0)), pl.BlockSpec((2, C, DT), lambda c: (0, nc - 1 - c, 0))],
        out_specs=[pl.BlockSpec((C, INW), lambda c: (nc - 1 - c, 0)), pl.BlockSpec((8, HW), lambda c: (0, 0)),
                   pl.BlockSpec((2, HW), lambda c: (0, 0)), pl.BlockSpec((2, DT), lambda c: (0, 0))],
        out_shape=[jax.ShapeDtypeStruct((S, INW), BF16), jax.ShapeDtypeStruct((8, HW), F32), jax.ShapeDtypeStruct((2, HW), F32),
                   jax.ShapeDtypeStruct((2, DT), F32)],
        scratch_shapes=[pltpu.VMEM((H, dk, dk), F32), pltpu.VMEM((1, HW), F32), pltpu.VMEM((C, HW), F32), pltpu.VMEM((C, HW), F32),
                        pltpu.VMEM((C, HW), F32), pltpu.VMEM((1, HW), F32)],
        compiler_params=_cparams(("arbitrary",)))(z, z, z, z, o, states, states, dyb, hg_lb, ng, dz_head, dz_tail)


def _position():
    x, y, c = lax.axis_index("x"), lax.axis_index("y"), lax.axis_index("c")
    return x, y, c, 4 * x + 2 * y + c


def _flip(x, y, c, k):
    return (1 - x if k & 4 else x, 1 - y if k & 2 else y, 1 - c if k & 1 else c)


def _allgather_small(name, v):
    R, L = v.shape

    def body(v_ref, out_ref, send_sems, recv_sems):
        x, y, c, me = _position()
        out_ref[me] = v_ref[...]
        copies = []
        for k in range(1, N_DEV):
            cp = pltpu.make_async_remote_copy(src_ref=v_ref, dst_ref=out_ref.at[me], send_sem=send_sems.at[k - 1],
                                              recv_sem=recv_sems.at[k - 1], device_id=_flip(x, y, c, k), device_id_type=MESH)
            cp.start()
            copies.append(cp)
        for cp in copies:
            cp.wait()

    return pl.pallas_call(
        body, name=name, out_shape=jax.ShapeDtypeStruct((N_DEV, R, L), v.dtype),
        in_specs=[pl.BlockSpec(memory_space=pltpu.VMEM)], out_specs=pl.BlockSpec(memory_space=pltpu.VMEM),
        scratch_shapes=[pltpu.SemaphoreType.DMA((N_DEV - 1,)), pltpu.SemaphoreType.DMA((N_DEV - 1,))],
        compiler_params=pltpu.CompilerParams(vmem_limit_bytes=VMEM_LIMIT),
    )(v)


_HBM = pl.BlockSpec(memory_space=pltpu.HBM)
_SEM = pl.BlockSpec(memory_space=pltpu.SEMAPHORE)
_EFFECT = pltpu.SideEffectType.DATAFLOW_SIDE_EFFECTING


def _split_start(name, bufs, n_sems, copies_fn, after=None):
    nb = len(bufs)
    extra = [] if after is None else [after]
    k = nb + len(extra)

    def body(*refs):
        for cp in copies_fn(refs[:nb], refs[k], refs[k + 1]):
            cp.start()
        refs[-1][...] = jnp.zeros_like(refs[-1])

    sems = pltpu.SemaphoreType.DMA((n_sems,))
    res = pl.pallas_call(
        body, name=name,
        out_shape=(sems, sems, *[pltpu.HBM(a.shape, a.dtype) for a in bufs], jax.ShapeDtypeStruct((8, LANES), F32)),
        in_specs=[_HBM] * nb + [pl.BlockSpec(memory_space=pl.ANY)] * len(extra),
        out_specs=(_SEM, _SEM, *[_HBM] * nb, pl.BlockSpec(memory_space=pltpu.VMEM)),
        input_output_aliases={i: 2 + i for i in range(nb)},
        compiler_params=pltpu.CompilerParams(has_side_effects=_EFFECT),
    )(*[pltpu.with_memory_space_constraint(a, pltpu.HBM) for a in bufs], *extra)
    return res[0], res[1], list(res[2:2 + nb]), res[-1]


def _split_wait(name, bufs, send_sems, recv_sems, after, copies_fn):
    nb = len(bufs)

    def body(*refs):
        for cp in copies_fn(refs[:nb], refs[nb], refs[nb + 1]):
            cp.wait_send()
            cp.wait_recv()

    res = pl.pallas_call(
        body, name=name, out_shape=tuple(pltpu.HBM(a.shape, a.dtype) for a in bufs),
        in_specs=[_HBM] * nb + [_SEM, _SEM, pl.BlockSpec(memory_space=pl.ANY)], out_specs=tuple([_HBM] * nb),
        input_output_aliases={i: i for i in range(nb)},
        compiler_params=pltpu.CompilerParams(has_side_effects=_EFFECT),
    )(*bufs, send_sems, recv_sems, after)
    return list(res)


def _split_relay(name, bufs, send_sems, recv_sems, after, wait_fn, n_sems, start_fn):
    nb = len(bufs)

    def body(*refs):
        for cp in wait_fn(refs[:nb], refs[nb], refs[nb + 1]):
            cp.wait_send()
            cp.wait_recv()
        for cp in start_fn(refs[:nb], refs[nb + 3], refs[nb + 4]):
            cp.start()
        refs[-1][...] = jnp.zeros_like(refs[-1])

    sems = pltpu.SemaphoreType.DMA((n_sems,))
    res = pl.pallas_call(
        body, name=name, out_shape=(sems, sems, *[pltpu.HBM(a.shape, a.dtype) for a in bufs], jax.ShapeDtypeStruct((8, LANES), F32)),
        in_specs=[_HBM] * nb + [_SEM, _SEM, pl.BlockSpec(memory_space=pl.ANY)],
        out_specs=(_SEM, _SEM, *[_HBM] * nb, pl.BlockSpec(memory_space=pltpu.VMEM)),
        input_output_aliases={i: 2 + i for i in range(nb)},
        compiler_params=pltpu.CompilerParams(has_side_effects=_EFFECT),
    )(*bufs, send_sems, recv_sems, after)
    return res[0], res[1], list(res[2:2 + nb]), res[-1]


N_CHIP = 4


def _chip_flip(x, y, k):
    return (1 - x if k & 2 else x), (1 - y if k & 1 else y)


def _gather_first_copies(n):
    def copies(bufs, send_sems, recv_sems):
        x, y, c, me = _position()
        out = []
        for w in range(n):
            for k in range(N_CHIP):
                to = (x, y, 1 - c) if k == 0 else (*_chip_flip(x, y, k), c)
                out.append(pltpu.make_async_remote_copy(
                    src_ref=bufs[w], dst_ref=bufs[n + w].at[me], send_sem=send_sems.at[w * N_CHIP + k],
                    recv_sem=recv_sems.at[w * N_CHIP + k], device_id=to, device_id_type=MESH))
        return out
    return copies


def _gather_relay_copies(n):
    def copies(bufs, send_sems, recv_sems):
        x, y, c, _ = _position()
        out = []
        for w in range(n):
            for k in range(1, N_CHIP):
                px, py = _chip_flip(x, y, k)
                blk = bufs[n + w].at[4 * px + 2 * py + c]
                out.append(pltpu.make_async_remote_copy(
                    src_ref=blk, dst_ref=blk, send_sem=send_sems.at[w * (N_CHIP - 1) + k - 1],
                    recv_sem=recv_sems.at[w * (N_CHIP - 1) + k - 1], device_id=(x, y, 1 - c), device_id_type=MESH))
        return out
    return copies


def _small_gather_copies(bufs, send_sems, recv_sems):
    x, y, c, me = _position()
    return [pltpu.make_async_remote_copy(src_ref=bufs[0], dst_ref=bufs[1].at[me], send_sem=send_sems.at[k - 1], recv_sem=recv_sems.at[k - 1],
                                         device_id=_flip(x, y, c, k), device_id_type=MESH) for k in range(1, N_DEV)]


def _forward_first_copies(n):
    def copies(bufs, send_sems, recv_sems):
        x, y, c, me = _position()
        out = []
        for w in range(n):
            for k, to in enumerate([(x, y, 1 - c), (1 - x, y, c), (x, 1 - y, c)]):
                out.append(pltpu.make_async_remote_copy(
                    src_ref=bufs[w], dst_ref=bufs[n + w].at[me], send_sem=send_sems.at[w * 3 + k],
                    recv_sem=recv_sems.at[w * 3 + k], device_id=to, device_id_type=MESH))
        return out
    return copies


def _forward_second_copies(n):
    def copies(bufs, send_sems, recv_sems):
        x, y, c, _ = _position()
        out = []
        for w in range(n):
            half = bufs[n + w].shape[1] // 2
            for k, (src_chip, rows, to) in enumerate([((1 - x, y), pl.ds(0, half), (x, 1 - y, c)), ((x, 1 - y), pl.ds(half, half), (1 - x, y, c))]):
                blk = bufs[n + w].at[4 * src_chip[0] + 2 * src_chip[1] + c, rows]
                out.append(pltpu.make_async_remote_copy(src_ref=blk, dst_ref=blk, send_sem=send_sems.at[w * 4 + k],
                                                        recv_sem=recv_sems.at[w * 4 + k], device_id=to, device_id_type=MESH))
            for k, (px, py) in enumerate([(1 - x, y), (x, 1 - y)]):
                blk = bufs[n + w].at[4 * px + 2 * py + c]
                out.append(pltpu.make_async_remote_copy(src_ref=blk, dst_ref=blk, send_sem=send_sems.at[w * 4 + 2 + k],
                                                        recv_sem=recv_sems.at[w * 4 + 2 + k], device_id=(x, y, 1 - c), device_id_type=MESH))
        return out
    return copies


def _forward_third_copies(n):
    def copies(bufs, send_sems, recv_sems):
        x, y, c, _ = _position()
        out = []
        for w in range(n):
            blk = bufs[n + w].at[4 * (1 - x) + 2 * (1 - y) + c]
            out.append(pltpu.make_async_remote_copy(src_ref=blk, dst_ref=blk, send_sem=send_sems.at[w], recv_sem=recv_sems.at[w],
                                                    device_id=(x, y, 1 - c), device_id_type=MESH))
        return out
    return copies


def _to_sibling_copies(n):
    def copies(bufs, send_sems, recv_sems):
        x, y, c, _ = _position()
        out = []
        for w in range(n):
            for q in range(N_CHIP):
                out.append(pltpu.make_async_remote_copy(
                    src_ref=bufs[w].at[2 * q + 1 - c], dst_ref=bufs[n + w].at[q], send_sem=send_sems.at[w * N_CHIP + q],
                    recv_sem=recv_sems.at[w * N_CHIP + q], device_id=(x, y, 1 - c), device_id_type=MESH))
        return out
    return copies


def _to_owner_copies(n):
    def copies(bufs, send_sems, recv_sems):
        x, y, c, _ = _position()
        out = []
        for w in range(n):
            for k in range(1, N_CHIP):
                px, py = (1 - x if k & 2 else x), (1 - y if k & 1 else y)
                out.append(pltpu.make_async_remote_copy(
                    src_ref=bufs[w].at[2 * px + py], dst_ref=bufs[n + w].at[k - 1], send_sem=send_sems.at[w * (N_CHIP - 1) + k - 1],
                    recv_sem=recv_sems.at[w * (N_CHIP - 1) + k - 1], device_id=(px, py, c), device_id_type=MESH))
        return out
    return copies


def _chip_sum(name, stack, landed, c_idx):
    _, R, C = stack.shape
    tr = _tile(R, max(BF16_ROWS, STREAM_TILE // C), BF16_ROWS)

    def body(c_ref, a_ref, b_ref, o_ref):
        o_ref[...] = (a_ref[...].astype(F32) + b_ref[...].astype(F32)).astype(o_ref.dtype)

    return pl.pallas_call(
        body, name=name,
        grid_spec=pltpu.PrefetchScalarGridSpec(
            num_scalar_prefetch=1, grid=(N_CHIP, R // tr),
            in_specs=[pl.BlockSpec((None, tr, C), lambda q, i, c_ref: (2 * q + c_ref[0], i, 0)),
                      pl.BlockSpec((None, tr, C), lambda q, i, c_ref: (q, i, 0))],
            out_specs=pl.BlockSpec((None, tr, C), lambda q, i, c_ref: (q, i, 0))),
        out_shape=jax.ShapeDtypeStruct((N_CHIP, R, C), stack.dtype),
        compiler_params=_cparams(("parallel", "parallel")))(c_idx, stack, landed)


def _ada_mod(c16, w):
    _, D = c16.shape
    n = w.shape[1]
    tk = _tile(D, 512)
    nk = D // tk

    def body(c_ref, w_ref, o_ref, ca_ref):
        @pl.when(pl.program_id(0) == 0)
        def _():
            o_ref[...] = jnp.zeros_like(o_ref)

        cv = c_ref[...]
        ca = cv * _sigmoid(cv)
        ca_ref[...] = ca
        o_ref[...] += _dot(ca, w_ref[...])

    return pl.pallas_call(
        body, name="ada_mod", grid=(nk,),
        in_specs=[pl.BlockSpec((16, tk), lambda k: (0, k)), pl.BlockSpec((tk, n), lambda k: (k, 0))],
        out_specs=[pl.BlockSpec((16, n), lambda k: (0, 0)), pl.BlockSpec((16, tk), lambda k: (0, k))],
        out_shape=[jax.ShapeDtypeStruct((16, n), F32), jax.ShapeDtypeStruct((16, D), F32)],
        compiler_params=_cparams(("arbitrary",)))(c16, w)


def _cast_shard(name, wf, slot):
    r, c = wf.shape
    tr = _tile(r, max(BF16_ROWS, STREAM_TILE // c), BF16_ROWS)

    def body(slot_ref, w_ref, s_ref, g_ref):
        v = w_ref[...].astype(BF16)
        s_ref[...] = v
        g_ref[...] = v

    return pl.pallas_call(
        body, name=name,
        grid_spec=pltpu.PrefetchScalarGridSpec(
            num_scalar_prefetch=1, grid=(r // tr,), in_specs=[pl.BlockSpec((tr, c), lambda i, s: (i, 0))],
            out_specs=[pl.BlockSpec((tr, c), lambda i, s: (i, 0)), pl.BlockSpec((None, tr, c), lambda i, s: (s[0], i, 0))]),
        out_shape=[jax.ShapeDtypeStruct((r, c), BF16), jax.ShapeDtypeStruct((N_DEV, r, c), BF16)],
        compiler_params=_cparams(("parallel",)))(slot, wf)


def _adam_math(w, g, m, v):
    m2 = ADAM_B1 * m + (1.0 - ADAM_B1) * g
    v2 = ADAM_B2 * v + (1.0 - ADAM_B2) * (g * g)
    m_hat = m2 / (1.0 - ADAM_B1 ** ADAM_STEP)
    v_hat = v2 / (1.0 - ADAM_B2 ** ADAM_STEP)
    delta = -ADAM_LR * (m_hat / (jnp.sqrt(v_hat) + ADAM_EPS) + ADAM_WD * w)
    return delta, m2, v2


def _adamw(name, w, m, v, own, own_slot, parts=(), row0=0, into=None):
    R, C = w.shape
    Rp = own.shape[1]
    tr = _tile(Rp, max(BF16_ROWS, ADAMW_TILE // C), BF16_ROWS)
    off = row0 // tr
    n_p = len(parts)
    held = [] if into is None else list(into)

    def body(slot_ref, *refs):
        w_ref, m_ref, v_ref, own_ref = refs[:4]
        g_ref, d_ref, m2_ref, v2_ref = refs[4 + n_p + len(held):]
        g = own_ref[...].astype(F32)
        for p_ref in refs[4:4 + n_p]:
            for s in range(p_ref.shape[0]):
                g = g + p_ref[s].astype(F32)
        delta, m2, v2 = _adam_math(w_ref[...], g, m_ref[...], v_ref[...])
        g_ref[...] = g
        d_ref[...] = delta
        m2_ref[...] = m2
        v2_ref[...] = v2

    blk = pl.BlockSpec((tr, C), lambda i, s: (i + off, 0))
    out = jax.ShapeDtypeStruct((R, C), F32)
    return pl.pallas_call(
        body, name=name,
        grid_spec=pltpu.PrefetchScalarGridSpec(
            num_scalar_prefetch=1, grid=(Rp // tr,),
            in_specs=[blk, blk, blk, pl.BlockSpec((None, tr, C), lambda i, s: (s[0], i, 0))]
            + [pl.BlockSpec((a.shape[0], tr, C), lambda i, s: (0, i, 0)) for a in parts]
            + [pl.BlockSpec(memory_space=pl.ANY)] * len(held),
            out_specs=[blk] * 4),
        out_shape=[out] * 4, input_output_aliases={5 + n_p + i: i for i in range(len(held))},
        compiler_params=_cparams(("parallel",)))(own_slot, w, m, v, own, *parts, *held)


def _small_update(gathered, w, m, v, after, rows):
    L = gathered[0].shape[2]
    R = sum(a.shape[1] for a in gathered)
    rs = w.shape[0]
    n = len(rows)
    n_g = len(gathered)
    assert all(r % 8 == 0 for r in rows) and sum(rows) <= rs and rs + 8 <= R and all(a.shape[1] % 8 == 0 for a in gathered)

    def body(*refs):
        w_ref, m_ref, v_ref = refs[n_g:n_g + 3]
        outs = refs[n_g + 4:]
        pieces = []
        for p_ref in refs[:n_g]:
            s = p_ref[0]
            for p in range(1, N_DEV):
                s = s + p_ref[p]
            pieces.append(s)
        g = pieces[0] if n_g == 1 else jnp.concatenate(pieces, axis=0)
        kinds = (g,) + _adam_math(w_ref[...], g[0:rs, :], m_ref[...], v_ref[...])
        at = 0
        for k, r in enumerate(rows):
            for idx, val in enumerate(kinds):
                outs[idx * n + k][...] = val[at:at + r, :]
            at += r
        outs[4 * n][...] = g[at:at + 8, :]

    vm = pl.BlockSpec(memory_space=pltpu.VMEM)
    shapes = [jax.ShapeDtypeStruct((r, L), F32) for _ in range(4) for r in rows] + [jax.ShapeDtypeStruct((8, L), F32)]
    return pl.pallas_call(body, name="small_update", in_specs=[vm] * (n_g + 3) + [pl.BlockSpec(memory_space=pl.ANY)],
                          out_specs=[vm] * len(shapes), out_shape=shapes,
                          compiler_params=pltpu.CompilerParams(vmem_limit_bytes=VMEM_LIMIT))(*gathered, w, m, v, after)


class _Fetched(dict):
    def __init__(self, fetch):
        super().__init__()
        self.fetch = fetch

    def first(self, key, after):
        self[key] = self.fetch(key, after)
        return self[key]


def _local_step(x, tgt, mod, p, fetch, F, scatter=None):
    S, D = x.shape
    GW, HW = p["ln_g"].shape[1], p["hg_ng"].shape[1]
    G, T, _ = p["ws"].shape
    w = _Fetched(fetch)
    INW = 2 * GW + 4 * HW + 2 * D
    in_loc, br_loc, fi_loc = INW // N_DEV, D // N_DEV, 2 * F // N_DEV
    assert GW == HW and F % fi_loc == 0
    sh1, sc1, gt1, sh2, sc2, gt2 = (mod[:, k * D:(k + 1) * D] for k in range(6))
    bsb = jnp.broadcast_to(p["bs"][:, :, None], (G, T, GW // G))

    tm = _tile(S, 1024, 16)
    tmh = _tile(S, 512, 16)
    tn_in = _tile(in_loc, 1280)
    tn_d = _tile(D, 512)
    tn_br = _tile(br_loc, 512)
    tk_s = S
    tm_w = _tile(D, 1024)
    g_off = 2 * GW + 4 * HW

    h1 = _norm_mod("norm1", x, p["norm1_g"], sc1, sh1)
    z = _mm_nn_stacked("proj_in", h1, w.first("in", h1), tm=tm, tn=tn_in, tk=D)[0]
    ya = _gmlp_fwd(z, p["ln_g"], p["ln_b"], p["ws"], bsb, GW)
    yb, o_hg, states = _hg_fwd(z, p["hg_lb"], p["hg_ng"], HW)
    flat = {k: jnp.swapaxes(w.first(k, yb), 0, 1).reshape(GW, D) for k in ("bg", "bh")}
    tn_f = _tile(D, 1024)
    pa = _matmul(
        "branch_gmlp", ya, flat["bg"], dims=_NN, grid_mnk=(S // tm, D // tn_f, 1), tiles=(tm, tn_f),
        a_spec=pl.BlockSpec((tm, GW), lambda i, j, k: (i, 0)), b_spec=pl.BlockSpec((GW, tn_f), lambda i, j, k: (0, j)),
        out_shapes=[jax.ShapeDtypeStruct((S, D), F32)], out_specs=[pl.BlockSpec((tm, tn_f), lambda i, j, k: (i, j))], epilogue=_store(F32))[0]
    t_fi = w.first("fi_early", pa)

    def gates(ga_ref, gb_ref, ba_ref, bb_ref):
        return _sigmoid(ga_ref[...] + ba_ref[...]), _sigmoid(gb_ref[...] + bb_ref[...])

    def gate_specs(tn_, tm_=tm):
        o1, o2 = g_off // tn_, (g_off + D) // tn_
        return [pl.BlockSpec((tm_, tn_), lambda i, j, k: (i, o1 + j)), pl.BlockSpec((tm_, tn_), lambda i, j, k: (i, o2 + j)),
                pl.BlockSpec((1, tn_), lambda i, j, k: (0, j)), pl.BlockSpec((1, tn_), lambda i, j, k: (0, D // tn_ + j))]

    def merge_ep(acc, ex, outs):
        ga, gb = gates(*ex[1:5])
        outs[0][...] = acc.astype(BF16)
        outs[1][...] = (ga * ex[0][...] + gb * acc).astype(BF16)

    tile_o = pl.BlockSpec((tmh, tn_f), lambda i, j, k: (i, j))
    pb, y = _matmul(
        "branch_hg_merge", yb, flat["bh"], dims=_NN, grid_mnk=(S // tmh, D // tn_f, 1), tiles=(tmh, tn_f),
        a_spec=pl.BlockSpec((tmh, HW), lambda i, j, k: (i, 0)), b_spec=pl.BlockSpec((HW, tn_f), lambda i, j, k: (0, j)),
        extras=[pa, z, z, p["b_gate"], p["b_gate"]], extra_specs=[tile_o, *gate_specs(tn_f, tmh)],
        out_shapes=[jax.ShapeDtypeStruct((S, D), BF16), jax.ShapeDtypeStruct((S, D), BF16)], out_specs=[tile_o, tile_o],
        epilogue=merge_ep, after=t_fi)

    def resid_ep(acc, ex, outs):
        outs[0][...] = acc.astype(BF16)
        outs[1][...] = ex[0][...] + ex[1][...] * acc

    def resid_mm(name, a, b, res, gt, tm_, tn_):
        K = a.shape[1]
        t_o = pl.BlockSpec((tm_, tn_), lambda i, j, k: (i, j))
        return _matmul(
            name, a, b, dims=_NN, grid_mnk=(S // tm_, D // tn_, 1), tiles=(tm_, tn_),
            a_spec=pl.BlockSpec((tm_, K), lambda i, j, k: (i, 0)), b_spec=pl.BlockSpec((K, tn_), lambda i, j, k: (0, j)),
            extras=[res, gt], extra_specs=[t_o, pl.BlockSpec((1, tn_), lambda i, j, k: (0, j))],
            out_shapes=[jax.ShapeDtypeStruct((S, D), BF16), jax.ShapeDtypeStruct((S, D), F32)], out_specs=[t_o, t_o], epilogue=resid_ep)

    o1, xm = resid_mm("proj_out", y, w.first("out", z), x, gt1, tm, tn_f)
    h2 = _norm_mod("norm2", xm, p["norm2_g"], sc2, sh2)
    hf, hf_fac = _ffn_in_swiglu(h2, w.first("fi", h2))
    o2, x3 = resid_mm("ffn_out", hf, w.first("fo", hf), xm, gt2, tmh, tn_d)
    dx3, do2, vec_l = _loss_head(x3, tgt, p["final_g"], o2, gt2)

    nf = F // fi_loc

    def dswiglu_ep(acc, ex, outs):
        outs[0][0] = (acc * ex[0][0].astype(F32)).astype(BF16)
        outs[0][1] = (acc * ex[0][1].astype(F32)).astype(BF16)

    pair = pl.BlockSpec((2, tmh, fi_loc), lambda i, j, k: (0, i, j))
    dab = _matmul(
        "ffn_out_dx", do2, w["fo"], dims=_NT, grid_mnk=(S // tmh, nf, 1), tiles=(tmh, fi_loc),
        a_spec=pl.BlockSpec((tmh, D), lambda i, j, k: (i, 0)), b_spec=pl.BlockSpec((fi_loc, D), lambda i, j, k: (j, 0)),
        extras=[hf_fac], extra_specs=[pair], out_shapes=[jax.ShapeDtypeStruct((2, S, F), BF16)], out_specs=[pair],
        epilogue=dswiglu_ep)[0]
    start = (lambda name, grads: scatter[0](name, grads)) if scatter is not None else (lambda name, grads: None)
    push = (lambda name, after: scatter[1](name, after)) if scatter is not None else (lambda name, after: None)

    def zero(token):
        return 0.0 if token is None else token[0:1, 0:1]

    tm_f = _tile(F, 512)
    g_fo = _mm_tn("ffn_out_dw", hf, do2, pl.BlockSpec((tk_s, D), lambda i, j, k: (k, j)), Mo=F, No=D, S=S, tm=tm_f, tn=D, tk=tk_s)
    g_fi = _mm_tn("ffn_in_dw", h2, dab, pl.BlockSpec((None, tk_s, fi_loc), lambda i, j, k: (j // nf, k, j % nf)),
                  Mo=D, No=2 * F, S=S, tm=D, tn=fi_loc, tk=tk_s, stacked_nloc=fi_loc, after=g_fo)
    t_ffn = start("scatter_ffn", dict(fo=g_fo, fi=g_fi))
    dh2 = _mm_nt_stacked("ffn_in_dx", pl.BlockSpec((None, tm, fi_loc), lambda i, j, k: (k // nf, i, k % nf)), dab, w["fi"],
                         M=S, tm=tm, tn=D, tk=fi_loc, after=t_ffn)
    dxm, vec2, do1 = _norm_mod_bwd("norm2_bwd", dh2, xm, p["norm2_g"], sc2, dx3, o1, gt1)
    t_ffn = push("scatter_ffn", dxm)

    def dmerge_ep(acc, ex, outs):
        ga, gb = gates(*ex[2:6])
        outs[0][...] = (acc * ga).astype(BF16)
        outs[1][...] = (acc * gb).astype(BF16)
        outs[2][0] = (acc * ex[0][...] * ga * (1.0 - ga)).astype(BF16)
        outs[2][1] = (acc * ex[1][...] * gb * (1.0 - gb)).astype(BF16)

    t_o = pl.BlockSpec((tm, tn_d), lambda i, j, k: (i, j))
    dpa, dpb, dg2 = _matmul(
        "proj_out_dx", do1, w["out"], dims=_NT, grid_mnk=(S // tm, D // tn_d, 1), tiles=(tm, tn_d),
        a_spec=pl.BlockSpec((tm, D), lambda i, j, k: (i, 0)), b_spec=pl.BlockSpec((tn_d, D), lambda i, j, k: (j, 0)),
        extras=[pa, pb, z, z, p["b_gate"], p["b_gate"]], extra_specs=[t_o, t_o, *gate_specs(tn_d)],
        out_shapes=[jax.ShapeDtypeStruct((S, D), BF16), jax.ShapeDtypeStruct((S, D), BF16), jax.ShapeDtypeStruct((2, S, D), BF16)],
        out_specs=[t_o, t_o, pl.BlockSpec((2, tm, tn_d), lambda i, j, k: (0, i, j))], epilogue=dmerge_ep, after=t_ffn)
    g_out = _mm_tn("proj_out_dw", y, do1, pl.BlockSpec((tk_s, D), lambda i, j, k: (k, j)), Mo=D, No=D, S=S, tm=tn_d, tn=D, tk=tk_s)
    tn_g = _tile(GW, 512)
    b_br = pl.BlockSpec((tk_s, br_loc), lambda i, j, k: (k, j))
    tm_b = _tile(GW, 1024)
    g_bg = _mm_tn("branch_gmlp_dw", ya, dpa, b_br, Mo=GW, No=D, S=S, tm=tm_b, tn=br_loc, tk=tk_s, stacked_nloc=br_loc)
    g_bh = _mm_tn("branch_hg_dw", yb, dpb, b_br, Mo=HW, No=D, S=S, tm=tm_b, tn=br_loc, tk=tk_s, stacked_nloc=br_loc)
    t_mix = start("scatter_mixer", dict(out=g_out, bg=g_bg, bh=g_bh))
    def branch_dx(name, dp, w_flat):
        return _matmul(
            name, dp, w_flat, dims=_NT, grid_mnk=(S // tm, GW // tn_g, 1), tiles=(tm, tn_g),
            a_spec=pl.BlockSpec((tm, D), lambda i, j, k: (i, 0)), b_spec=pl.BlockSpec((tn_g, D), lambda i, j, k: (j, 0)),
            out_shapes=[jax.ShapeDtypeStruct((S, GW), F32)], out_specs=[pl.BlockSpec((tm, tn_g), lambda i, j, k: (i, j))],
            epilogue=_store(F32), after=t_mix)[0]

    dya = branch_dx("branch_gmlp_dx", dpa, flat["bg"])
    dyb = branch_dx("branch_hg_dx", dpb, flat["bh"])
    dz_gmlp, dln, dws, dbs = _gmlp_bwd(z, dya, p["ln_g"], p["ln_b"], p["ws"], bsb, GW)
    t_mix = push("scatter_mixer", dz_gmlp)
    dz, dng, dhlb, db_gate = _hg_bwd(z, o_hg, states, dyb, p["hg_lb"], p["hg_ng"] + zero(t_mix), HW, dz_gmlp, dg2)
    small = dict(b_gate=db_gate.reshape(1, 2 * D), ln_g=dln[0:1], ln_b=dln[1:2], ws=dws, bs=dbs.reshape(G, T),
                 hg_lb=dhlb, hg_ng=dng[0:1], norm2_g=vec2[2:3], final_g=vec_l[1:2], loss=vec_l[0:1, 0:LANES])
    t_in = scatter[2](small) if scatter is not None else None
    half = D // 2
    tm_h = _tile(half, 1024)
    g_in = []
    for hname, h in (("a", 0), ("b", 1)):
        g_in.append(_mm_tn("proj_in_dw_" + hname, h1, dz, pl.BlockSpec((tk_s, in_loc), lambda i, j, k: (k, j)), Mo=half, No=INW, S=S,
                           tm=tm_h, tn=in_loc, tk=tk_s, stacked_nloc=in_loc, after=t_in, a_off=h * (half // tm_h)))
        t_in = start("scatter_proj_in_" + hname, {"w_in_" + hname: g_in[-1]})
    t_in = push("scatter_proj_in_b", push("scatter_proj_in_a", t_in))
    dh1 = _mm_nt_stacked("proj_in_dx", pl.BlockSpec((tm, in_loc), lambda i, j, k: (i, k)), dz, w["in"], M=S, tm=tm, tn=D, tk=in_loc,
                         after=t_in)
    dx, vec1 = _norm_mod_bwd("norm1_bwd", dh1, x, p["norm1_g"], sc1, dxm)

    dmod = jnp.concatenate([vec1[0:1], vec1[1:2], vec2[3:4], vec2[0:1], vec2[1:2], vec_l[2:3]], axis=1)
    small["norm1_g"] = vec1[2:3]
    big = dict(w_in_a=g_in[0], w_in_b=g_in[1], bg=g_bg, bh=g_bh, out=g_out, fi=g_fi, fo=g_fo)
    return dx, big, small, dmod


_SMALL = ("b_ada", "norm1_g", "b_gate", "ln_g", "ln_b", "ws", "bs", "hg_lb", "hg_ng", "norm2_g", "final_g")


def _pack(parts, rows_mult=8):
    flat = [a.reshape(-1) for a in parts]
    offs, n = [], 0
    for a in flat:
        offs.append(n)
        n += a.shape[0]
    pad = (-n) % (LANES * rows_mult)
    if pad:
        flat.append(jnp.zeros((pad,), F32))
    return jnp.concatenate(flat).reshape(-1, LANES), offs


def kernel(x, c, w_ada, b_ada, norm1_g, w_in, b_gate, gmlp_ln_g, gmlp_ln_b, gmlp_ws, gmlp_bs, hg_lb, hg_norm_g, w_branch_gmlp, w_branch_hg, w_out, norm2_g, w_ffn_in, w_ffn_out, final_norm_g, loss_target, m_w_ada, m_b_ada, m_norm1_g, m_w_in, m_b_gate, m_gmlp_ln_g, m_gmlp_ln_b, m_gmlp_ws, m_gmlp_bs, m_hg_lb, m_hg_norm_g, m_w_branch_gmlp, m_w_branch_hg, m_w_out, m_norm2_g, m_w_ffn_in, m_w_ffn_out, m_final_norm_g, v_w_ada, v_b_ada, v_norm1_g, v_w_in, v_b_gate, v_gmlp_ln_g, v_gmlp_ln_b, v_gmlp_ws, v_gmlp_bs, v_hg_lb, v_hg_norm_g, v_w_branch_gmlp, v_w_branch_hg, v_w_out, v_norm2_g, v_w_ffn_in, v_w_ffn_out, v_final_norm_g):
    S, D = x.shape[1], x.shape[2]
    ada_loc = w_ada.shape[2]
    me = 4 * lax.axis_index("x") + 2 * lax.axis_index("y") + lax.axis_index("c")
    me_idx = me.astype(jnp.int32).reshape(1)

    def empty_hbm(shape, dtype):
        return pltpu.with_memory_space_constraint(lax.empty(shape, dtype), pltpu.HBM)

    groups = dict(gather_in=dict(keys=["in"], src=[w_in], forward=True),
                  gather_mixer=dict(keys=["bg", "bh", "out"], src=[w_branch_gmlp, w_branch_hg, w_out], forward=False),
                  gather_ffn_in=dict(keys=["fi"], src=[w_ffn_in], forward=True),
                  gather_ffn_out=dict(keys=["fo"], src=[w_ffn_out], forward=False))
    group_of = {k: gname for gname, g in groups.items() for k in g["keys"]}

    def first_hop(gname, after):
        g = groups[gname]
        n = len(g["keys"])
        cast = [_cast_shard(f"{gname}_cast_{k}", a[0], me_idx) for k, a in zip(g["keys"], g["src"])]
        shards, outs = [s for s, _ in cast], [o for _, o in cast]
        if g["forward"]:
            *g["hop"], token = _split_start(gname + "_hop1", shards + outs, n * 3, _forward_first_copies(n), after=after)
        else:
            *g["hop"], token = _split_start(gname + "_hop1", shards + outs, n * N_CHIP, _gather_first_copies(n), after=after)
        return token

    def second_hop(gname, after):
        g = groups[gname]
        n = len(g["keys"])
        *g["hop"], token = _split_relay(gname + "_hop2", g["hop"][2], g["hop"][0], g["hop"][1], after,
                                        _forward_first_copies(n), n * 4, _forward_second_copies(n))
        return token

    def finish(gname, after):
        g = groups[gname]
        n = len(g["keys"])
        send_sems, recv_sems, bufs = g["hop"]
        if g["forward"]:
            send_sems, recv_sems, bufs, _ = _split_relay(gname + "_hop3", bufs, send_sems, recv_sems, after,
                                                         _forward_second_copies(n), n, _forward_third_copies(n))
            bufs = _split_wait(gname + "_wait", bufs, send_sems, recv_sems, after, _forward_third_copies(n))
        else:
            send_sems, recv_sems, bufs, _ = _split_relay(gname + "_relay", bufs, send_sems, recv_sems, after,
                                                         _gather_first_copies(n), n * (N_CHIP - 1), _gather_relay_copies(n))
            bufs = _split_wait(gname + "_wait", bufs, send_sems, recv_sems, after, _gather_relay_copies(n))
        g["done"] = dict(zip(g["keys"], bufs[n:]))

    c_all = _allgather_small("gather_c", c.reshape(D // LANES, LANES)).reshape(N_DEV, D)
    token = first_hop("gather_in", c_all)
    mod_cols, c_act = _ada_mod(jnp.pad(c_all, ((0, 16 - N_DEV), (0, 0))) + token[0:1, 0:1], w_ada[0])
    mod_vec = mod_cols[:N_DEV].reshape(-1, LANES)
    mg_send, mg_recv, mg_bufs, token = _split_start(
        "gather_mod_start", [mod_vec, lax.dynamic_update_slice(lax.empty((N_DEV, *mod_vec.shape), F32), mod_vec[None], (me, 0, 0))],
        N_DEV - 1, _small_gather_copies)
    token = second_hop("gather_in", token)
    token = first_hop("gather_ffn_in", first_hop("gather_mixer", token))
    mod_all = _split_wait("gather_mod_wait", mg_bufs, mg_send, mg_recv, token, _small_gather_copies)[1].reshape(N_DEV, N_DEV, ada_loc)
    mod = lax.dynamic_index_in_dim(mod_all, me, axis=1, keepdims=False).reshape(1, N_DEV * ada_loc) + b_ada

    def fetch(key, after):
        if key == "in":
            finish("gather_in", after)
        elif key == "fi_early":
            return first_hop("gather_ffn_out", second_hop("gather_ffn_in", after))
        elif "done" not in groups[group_of[key]]:
            finish(group_of[key], after)
        arr = groups[group_of[key]]["done"][key]
        return arr.reshape(-1, D) if key in ("out", "fo") else arr

    p = dict(norm1_g=norm1_g, b_gate=b_gate, ln_g=gmlp_ln_g, ln_b=gmlp_ln_b, ws=gmlp_ws[0], bs=gmlp_bs[0], hg_lb=hg_lb,
             hg_ng=hg_norm_g, norm2_g=norm2_g, final_g=final_norm_g.reshape(1, D))

    in_flight = {}
    c_idx = lax.axis_index("c").astype(jnp.int32).reshape(1)
    chip_idx = (2 * lax.axis_index("x") + lax.axis_index("y")).astype(jnp.int32).reshape(1)

    def scatter_start(name, grads):
        keys = list(grads)
        n = len(keys)
        stacks = [grads[k].reshape(N_DEV, -1, grads[k].shape[-1]) for k in keys]
        lands = [empty_hbm((N_CHIP, *g.shape[1:]), g.dtype) for g in stacks]
        send_sems, recv_sems, bufs, token = _split_start(name + "_d2d", stacks + lands, n * N_CHIP, _to_sibling_copies(n))
        in_flight[name] = dict(keys=keys, stage1=(send_sems, recv_sems, bufs))
        return token

    def scatter_push(name, after):
        f = in_flight[name]
        n = len(f["keys"])
        send_sems, recv_sems, bufs = f["stage1"]
        bufs = _split_wait(name + "_d2d_wait", bufs, send_sems, recv_sems, after, _to_sibling_copies(n))
        sums = [_chip_sum(f"{name}_sum_{k}", bufs[i], bufs[n + i], c_idx) for i, k in enumerate(f["keys"])]
        lands = [empty_hbm((N_CHIP - 1, *s.shape[1:]), s.dtype) for s in sums]
        send_sems, recv_sems, bufs, token = _split_start(name + "_ici", sums + lands, n * (N_CHIP - 1), _to_owner_copies(n))
        f["stage2"] = (send_sems, recv_sems, bufs)
        return token

    small_gathers = []

    def small_gather(name, parts):
        packed, _ = _pack(parts)
        send_sems, recv_sems, bufs, token = _split_start(
            name + "_start", [packed, lax.dynamic_update_slice(lax.empty((N_DEV, *packed.shape), F32), packed[None], (me, 0, 0))],
            N_DEV - 1, _small_gather_copies)
        small_gathers.append((name, send_sems, recv_sems, bufs))
        return token

    grad_x, _, small, dmod = _local_step(
        x[0], loss_target[0], mod, p, fetch, w_ffn_out.shape[1] * N_DEV,
        (scatter_start, scatter_push, lambda sm: small_gather("gather_small_rest", [sm[k] for k in _SMALL[2:]] + [sm["loss"]])))
    assert _SMALL[:2] == ("b_ada", "norm1_g")
    t_tail = small_gather("gather_small_head", [dmod, small["norm1_g"]])
    big_w = dict(w_in=(w_in, m_w_in, v_w_in, "w_in"), bg=(w_branch_gmlp, m_w_branch_gmlp, v_w_branch_gmlp, "w_branch_gmlp"),
                 bh=(w_branch_hg, m_w_branch_hg, v_w_branch_hg, "w_branch_hg"), out=(w_out, m_w_out, v_w_out, "w_out"),
                 fi=(w_ffn_in, m_w_ffn_in, v_w_ffn_in, "w_ffn_in"), fo=(w_ffn_out, m_w_ffn_out, v_w_ffn_out, "w_ffn_out"))
    upd = {}

    def land_and_update(name, after):
        keys = in_flight[name]["keys"]
        n = len(keys)
        send_sems, recv_sems, bufs = in_flight[name]["stage2"]
        bufs = _split_wait(name + "_ici_wait", bufs, send_sems, recv_sems, after, _to_owner_copies(n))
        for i, k in enumerate(keys):
            if k in big_w:
                wt, mt, vt, out_name = big_w[k]
                upd[out_name] = _adamw("adamw_" + out_name, wt[0], mt[0], vt[0], bufs[i], chip_idx, [bufs[n + i]])
            else:
                wt, mt, vt, out_name = big_w["w_in"]
                upd[out_name] = _adamw("adamw_" + k, wt[0], mt[0], vt[0], bufs[i], chip_idx, [bufs[n + i]],
                                       row0=0 if k == "w_in_a" else bufs[i].shape[1], into=upd.get(out_name))
            after = upd[out_name][1]
        return after

    after = land_and_update("scatter_mixer", land_and_update("scatter_ffn", t_tail))
    landed = {}
    for name, send_sems, recv_sems, bufs in small_gathers:
        landed[name] = _split_wait(name + "_wait", bufs, send_sems, recv_sems, after, _small_gather_copies)[1]
    gathered = [landed["gather_small_head"], landed["gather_small_rest"]]
    assert gathered[0].shape[1] * LANES == math.prod(b_ada.shape) + math.prod(norm1_g.shape)
    wp = dict(p, b_ada=b_ada)
    ms = dict(b_ada=m_b_ada, norm1_g=m_norm1_g, b_gate=m_b_gate, ln_g=m_gmlp_ln_g, ln_b=m_gmlp_ln_b, ws=m_gmlp_ws, bs=m_gmlp_bs,
              hg_lb=m_hg_lb, hg_ng=m_hg_norm_g, norm2_g=m_norm2_g, final_g=m_final_norm_g)
    vs = dict(b_ada=v_b_ada, norm1_g=v_norm1_g, b_gate=v_b_gate, ln_g=v_gmlp_ln_g, ln_b=v_gmlp_ln_b, ws=v_gmlp_ws, bs=v_gmlp_bs,
              hg_lb=v_hg_lb, hg_ng=v_hg_norm_g, norm2_g=v_norm2_g, final_g=v_final_norm_g)
    w_sm, _ = _pack([wp[k] for k in _SMALL])
    m_sm, _ = _pack([ms[k] for k in _SMALL])
    v_sm, _ = _pack([vs[k] for k in _SMALL])
    shapes = dict(b_ada=b_ada.shape, norm1_g=norm1_g.shape, b_gate=b_gate.shape, ln_g=gmlp_ln_g.shape, ln_b=gmlp_ln_b.shape,
                  ws=gmlp_ws.shape, bs=gmlp_bs.shape, hg_lb=hg_lb.shape, hg_ng=hg_norm_g.shape, norm2_g=norm2_g.shape,
                  final_g=final_norm_g.shape)
    sm_out = _small_update(gathered, w_sm, m_sm, v_sm, after, [math.prod(shapes[k]) // LANES for k in _SMALL])

    def unpack(idx, k):
        return sm_out[idx * len(_SMALL) + _SMALL.index(k)].reshape(shapes[k])

    loss = sm_out[-1][0, 0]

    assert ada_loc % LANES == 0
    dmod_loc = lax.dynamic_slice_in_dim(gathered[0], me * (ada_loc // LANES), ada_loc // LANES, axis=1).reshape(N_DEV, ada_loc)
    ca_t = jnp.pad(c_act[:N_DEV].T, ((0, 0), (0, LANES - N_DEV))).astype(BF16)
    dm_p = jnp.pad(dmod_loc, ((0, LANES - N_DEV), (0, 0))).astype(BF16)
    tm_a = _tile(D, 512)
    g_ada = _matmul(
        "ada_dw", ca_t, dm_p, dims=_NN, grid_mnk=(D // tm_a, 1, 1), tiles=(tm_a, ada_loc),
        a_spec=pl.BlockSpec((tm_a, LANES), lambda i, j, k: (i, 0)), b_spec=pl.BlockSpec((LANES, ada_loc), lambda i, j, k: (0, 0)),
        out_shapes=[jax.ShapeDtypeStruct((1, D, ada_loc), F32)], out_specs=[pl.BlockSpec((None, tm_a, ada_loc), lambda i, j, k: (0, i, 0))],
        epilogue=_store(F32))[0]
    upd["w_ada"] = _adamw("adamw_w_ada", w_ada[0], m_w_ada[0], v_w_ada[0], g_ada, jnp.zeros((1,), jnp.int32))
    land_and_update("scatter_proj_in_b", land_and_update("scatter_proj_in_a", upd["w_ada"][1]))

    order = ("w_ada", "b_ada", "norm1_g", "w_in", "b_gate", "ln_g", "ln_b", "ws", "bs", "hg_lb", "hg_ng", "w_branch_gmlp", "w_branch_hg",
             "w_out", "norm2_g", "w_ffn_in", "w_ffn_out", "final_g")
    outs = [loss, grad_x[None]]
    for idx in range(4):
        for k in order:
            outs.append(upd[k][idx][None] if k in upd else unpack(idx, k))
    return tuple(outs)
```

```python
import functools
import math

import jax
import jax.numpy as jnp
from jax import lax
from jax.experimental import pallas as pl
from jax.experimental.pallas import tpu as pltpu

F32 = jnp.float32
BF16 = jnp.bfloat16
N_DEV = 8
EPS = 1e-6
LANES = 128
HG_DK = 128
HG_CHUNK = 64
HG_MID = HG_CHUNK // 2 - 1
EXP_CLAMP = 80.0
VMEM_LIMIT = 48 * 1024 * 1024
BF16_ROWS = 16
STREAM_TILE = 1 << 20
ADAMW_TILE = 3 << 17
ADAM_LR, ADAM_B1, ADAM_B2, ADAM_EPS, ADAM_WD, ADAM_STEP = 0.001, 0.9, 0.999, 1e-08, 0.01, 10
MESH = pl.DeviceIdType.MESH

_NN = (((1,), (0,)), ((), ()))
_NT = (((1,), (1,)), ((), ()))
_TN = (((0,), (0,)), ((), ()))


def _dot(a, b, dims=_NN):
    return lax.dot_general(a.astype(BF16), b.astype(BF16), dims, preferred_element_type=F32)


def _tile(n, target, mult=LANES):
    best = None
    for t in range(mult, min(n, target) + 1, mult):
        if n % t == 0:
            best = t
    return n if best is None else best


def _cparams(sem):
    return pltpu.CompilerParams(dimension_semantics=sem, vmem_limit_bytes=VMEM_LIMIT)


def _sigmoid(x):
    return 1.0 / (1.0 + jnp.exp(-x))


def _gelu_parts(x):
    k0 = math.sqrt(2.0 / math.pi)
    x2 = x * x
    t = jnp.tanh(k0 * (x + 0.044715 * x * x2))
    g = 0.5 * x * (1.0 + t)
    dg = 0.5 * (1.0 + t) + 0.5 * x * (1.0 - t * t) * (k0 * (1.0 + 3.0 * 0.044715 * x2))
    return g, dg


def _split3(x):
    h = x.astype(BF16)
    r = x - h.astype(F32)
    m = r.astype(BF16)
    lo = (r - m.astype(F32)).astype(BF16)
    return h, m, lo


def _ones_dot(mat01, x):
    h, m, lo = _split3(x)
    d = functools.partial(lax.dot_general, dimension_numbers=_NN, preferred_element_type=F32)
    return d(mat01, h) + d(mat01, m) + d(mat01, lo)


def _matmul(name, a, b, *, dims, grid_mnk, tiles, a_spec, b_spec, extras=(), extra_specs=(), out_shapes, out_specs, epilogue, after=None,
            sem=None):
    gm, gn, nk = grid_mnk
    tm, tn = tiles
    n_ex, n_out = len(extras), len(out_shapes)
    held = [] if after is None else [after]

    def body(*refs):
        a_ref, b_ref = refs[0], refs[1]
        ex = refs[2:2 + n_ex]
        outs = refs[2 + n_ex + len(held):2 + n_ex + len(held) + n_out]
        more = () if sem is None else (pl.program_id(0) == 0,)
        if nk == 1:
            epilogue(lax.dot_general(a_ref[...], b_ref[...], dims, preferred_element_type=F32), ex, outs, *more)
            return
        acc = refs[-1]
        k = pl.program_id(2)

        @pl.when(k == 0)
        def _():
            acc[...] = jnp.zeros_like(acc)

        acc[...] += lax.dot_general(a_ref[...], b_ref[...], dims, preferred_element_type=F32)

        @pl.when(k == nk - 1)
        def _():
            epilogue(acc[...], ex, outs, *more)

    return pl.pallas_call(
        body, name=name, grid=(gm, gn, nk), in_specs=[a_spec, b_spec, *extra_specs] + [pl.BlockSpec(memory_space=pl.ANY)] * len(held),
        out_specs=list(out_specs), out_shape=list(out_shapes), scratch_shapes=[] if nk == 1 else [pltpu.VMEM((tm, tn), F32)],
        compiler_params=_cparams(sem or ("parallel", "parallel", "arbitrary")),
    )(a, b, *extras, *held)


def _store(dtype):
    def ep(acc, ex, outs):
        outs[0][...] = acc.astype(dtype)
    return ep


def _mm_nn_stacked(name, a, wg, *, tm, tn, tk, out_dtype=F32, extras=(), extra_specs=(), out_shapes=None, out_specs=None, epilogue=None,
                   after=None):
    M, K = a.shape
    _, _, nloc = wg.shape
    N = nloc * N_DEV
    q = nloc // tn
    if out_shapes is None:
        out_shapes = [jax.ShapeDtypeStruct((M, N), out_dtype)]
        out_specs = [pl.BlockSpec((tm, tn), lambda i, j, k: (i, j))]
        epilogue = _store(out_dtype)
    return _matmul(
        name, a, wg, dims=_NN, grid_mnk=(M // tm, N // tn, K // tk), tiles=(tm, tn),
        a_spec=pl.BlockSpec((tm, tk), lambda i, j, k: (i, k)),
        b_spec=pl.BlockSpec((None, tk, tn), lambda i, j, k: (j // q, k, j % q)),
        extras=extras, extra_specs=extra_specs, out_shapes=out_shapes, out_specs=out_specs, epilogue=epilogue, after=after)


def _mm_nt_stacked(name, a_spec, a, wg, *, M, tm, tn, tk, out_dtype=F32, after=None, extras=(), extra_specs=(), out_shapes=None,
                   out_specs=None, epilogue=None, sem=None):
    _, Kw, nloc = wg.shape
    q = nloc // tk
    single = out_shapes is None
    if single:
        out_shapes = [jax.ShapeDtypeStruct((M, Kw), out_dtype)]
        out_specs = [pl.BlockSpec((tm, tn), lambda i, j, k: (i, j))]
        epilogue = _store(out_dtype)
    res = _matmul(
        name, a, wg, dims=_NT, grid_mnk=(M // tm, Kw // tn, (nloc * N_DEV) // tk), tiles=(tm, tn),
        a_spec=a_spec, b_spec=pl.BlockSpec((None, tn, tk), lambda i, j, k: (k // q, j, k % q)),
        extras=extras, extra_specs=extra_specs, out_shapes=out_shapes, out_specs=out_specs, epilogue=epilogue, after=after, sem=sem)
    return res[0] if single else res


def _mm_tn(name, a, b, b_spec, *, Mo, No, S, tm, tn, tk, stacked_nloc=None, after=None, a_off=0):
    if stacked_nloc is None:
        out_shape = jax.ShapeDtypeStruct((Mo, No), BF16)
        out_spec = pl.BlockSpec((tm, tn), lambda i, j, k: (i, j))
    else:
        q = stacked_nloc // tn
        out_shape = jax.ShapeDtypeStruct((N_DEV, Mo, stacked_nloc), BF16)
        out_spec = pl.BlockSpec((None, tm, tn), lambda i, j, k: (j // q, i, j % q))
    return _matmul(
        name, a, b, dims=_TN, grid_mnk=(Mo // tm, No // tn, S // tk), tiles=(tm, tn),
        a_spec=pl.BlockSpec((tk, tm), lambda i, j, k: (k, i + a_off)), b_spec=b_spec,
        out_shapes=[out_shape], out_specs=[out_spec], epilogue=_store(BF16), after=after)[0]


def _norm_mod(name, x, g, sc, sh):
    S, D = x.shape
    tm = _tile(S, 256, 8)

    def body(x_ref, g_ref, sc_ref, sh_ref, h_ref):
        xv = x_ref[...]
        r = lax.rsqrt(jnp.mean(xv * xv, axis=-1, keepdims=True) + EPS)
        h = (xv * r) * g_ref[...]
        h_ref[...] = (h * (1.0 + sc_ref[...]) + sh_ref[...]).astype(BF16)

    row = pl.BlockSpec((tm, D), lambda i: (i, 0))
    vec = pl.BlockSpec((1, D), lambda i: (0, 0))
    return pl.pallas_call(body, name=name, grid=(S // tm,), in_specs=[row, vec, vec, vec], out_specs=row,
                          out_shape=jax.ShapeDtypeStruct((S, D), BF16), compiler_params=_cparams(("parallel",)))(x, g, sc, sh)


def _norm_mod_bwd_rows(first, dh_v, x_ref, g_ref, sc_ref, dres_ref, dx_ref, vec_ref, o_ref=None, gt_ref=None, do_ref=None):
    @pl.when(first)
    def _():
        vec_ref[...] = jnp.zeros_like(vec_ref)

    xv, gv = x_ref[...], g_ref[...]
    r = lax.rsqrt(jnp.mean(xv * xv, axis=-1, keepdims=True) + EPS)
    xn = xv * r
    one_sc = 1.0 + sc_ref[...]
    vec_ref[0:1, :] += jnp.sum(dh_v, axis=0, keepdims=True)
    vec_ref[1:2, :] += jnp.sum(dh_v * (xn * gv), axis=0, keepdims=True)
    vec_ref[2:3, :] += jnp.sum(dh_v * one_sc * xn, axis=0, keepdims=True)
    dxn = dh_v * one_sc * gv
    dx = dres_ref[...] + r * (dxn - xn * jnp.mean(dxn * xn, axis=-1, keepdims=True))
    dx_ref[...] = dx
    if o_ref is not None:
        vec_ref[3:4, :] += jnp.sum(dx * o_ref[...], axis=0, keepdims=True)
        do_ref[...] = (dx * gt_ref[...]).astype(BF16)


def _norm_mod_bwd(name, dh, x, g, sc, dres, o=None, gt=None):
    S, D = x.shape
    tm = _tile(S, 256, 8)
    gated = o is not None

    def body(*refs):
        if gated:
            dh_ref, x_ref, g_ref, sc_ref, dres_ref, o_ref, gt_ref, dx_ref, vec_ref, do_ref = refs
        else:
            dh_ref, x_ref, g_ref, sc_ref, dres_ref, dx_ref, vec_ref = refs
            o_ref = gt_ref = do_ref = None
        _norm_mod_bwd_rows(pl.program_id(0) == 0, dh_ref[...], x_ref, g_ref, sc_ref, dres_ref, dx_ref, vec_ref, o_ref, gt_ref, do_ref)

    row = pl.BlockSpec((tm, D), lambda i: (i, 0))
    vec = pl.BlockSpec((1, D), lambda i: (0, 0))
    acc = pl.BlockSpec((8, D), lambda i: (0, 0))
    ins = [dh, x, g, sc, dres] + ([o, gt] if gated else [])
    in_specs = [row, row, vec, vec, row] + ([row, vec] if gated else [])
    out_shape = [jax.ShapeDtypeStruct((S, D), F32), jax.ShapeDtypeStruct((8, D), F32)]
    out_specs = [row, acc]
    if gated:
        out_shape.append(jax.ShapeDtypeStruct((S, D), BF16))
        out_specs.append(row)
    return pl.pallas_call(body, name=name, grid=(S // tm,), in_specs=in_specs, out_specs=out_specs, out_shape=out_shape,
                          compiler_params=_cparams(("arbitrary",)))(*ins)


def _loss_head(x3, tgt, gf, o2, gt2):
    S, D = x3.shape
    tm = _tile(S, 256, 8)

    def body(x_ref, t_ref, g_ref, o_ref, gt_ref, dx_ref, do_ref, vec_ref):
        i = pl.program_id(0)

        @pl.when(i == 0)
        def _():
            vec_ref[...] = jnp.zeros_like(vec_ref)

        xv, gv = x_ref[...], g_ref[...]
        r = lax.rsqrt(jnp.mean(xv * xv, axis=-1, keepdims=True) + EPS)
        xn = xv * r
        e = xn * gv - t_ref[...]
        tok = 0.5 * jnp.mean(e * e, axis=-1, keepdims=True)
        vec_ref[0:1, :] += jnp.broadcast_to(jnp.sum(tok, axis=0, keepdims=True), (1, D))
        dy = e * (1.0 / D)
        vec_ref[1:2, :] += jnp.sum(dy * xn, axis=0, keepdims=True)
        dxn = dy * gv
        dx = r * (dxn - xn * jnp.mean(dxn * xn, axis=-1, keepdims=True))
        dx_ref[...] = dx
        vec_ref[2:3, :] += jnp.sum(dx * o_ref[...], axis=0, keepdims=True)
        do_ref[...] = (dx * gt_ref[...]).astype(BF16)

    row = pl.BlockSpec((tm, D), lambda i: (i, 0))
    vec = pl.BlockSpec((1, D), lambda i: (0, 0))
    return pl.pallas_call(
        body, name="loss_head", grid=(S // tm,), in_specs=[row, row, vec, row, vec],
        out_specs=[row, row, pl.BlockSpec((8, D), lambda i: (0, 0))],
        out_shape=[jax.ShapeDtypeStruct((S, D), F32), jax.ShapeDtypeStruct((S, D), BF16), jax.ShapeDtypeStruct((8, D), F32)],
        compiler_params=_cparams(("arbitrary",)))(x3, tgt, gf, o2, gt2)


def _ffn_in_swiglu(h, wg):
    S, D = h.shape
    _, _, tf = wg.shape
    nf = N_DEV // 2
    F = nf * tf
    tm = _tile(S, 256, 16)

    def body(h_ref, wa_ref, wu_ref, hf_ref, fac_ref):
        hv = h_ref[...]
        a = lax.dot_general(hv, wa_ref[...], _NN, preferred_element_type=F32)
        up = lax.dot_general(hv, wu_ref[...], _NN, preferred_element_type=F32)
        sa = _sigmoid(a)
        silu = a * sa
        hf_ref[...] = (silu * up).astype(BF16)
        fac_ref[0] = (up * (sa * (1.0 + a * (1.0 - sa)))).astype(BF16)
        fac_ref[1] = silu.astype(BF16)

    return pl.pallas_call(
        body, name="ffn_in_swiglu", grid=(nf, S // tm),
        in_specs=[pl.BlockSpec((tm, D), lambda j, i: (i, 0)), pl.BlockSpec((None, D, tf), lambda j, i: (j, 0, 0)),
                  pl.BlockSpec((None, D, tf), lambda j, i: (j + nf, 0, 0))],
        out_specs=[pl.BlockSpec((tm, tf), lambda j, i: (i, j)), pl.BlockSpec((2, tm, tf), lambda j, i: (0, i, j))],
        out_shape=[jax.ShapeDtypeStruct((S, F), BF16), jax.ShapeDtypeStruct((2, S, F), BF16)],
        compiler_params=_cparams(("parallel", "parallel")))(h, wg, wg)


def _gmlp_common(u_ref, v_ref, lg_ref, lb_ref, ws_ref, bsb_ref, G, T, Dg):
    ug, dug = _gelu_parts(u_ref[...])
    vg, dvg = _gelu_parts(v_ref[...])
    mu = jnp.mean(vg, axis=-1, keepdims=True)
    vc = vg - mu
    rstd = lax.rsqrt(jnp.mean(vc * vc, axis=-1, keepdims=True) + EPS)
    vhat = vc * rstd
    vn = vhat * lg_ref[...] + lb_ref[...]
    row = lax.broadcasted_iota(jnp.int32, (T, T), 0)
    col = lax.broadcasted_iota(jnp.int32, (T, T), 1)
    tril = row >= col
    s = []
    for g in range(G):
        w = jnp.where(tril, ws_ref[g], 0.0)
        s.append(_dot(w, vn[:, g * Dg:(g + 1) * Dg]) + bsb_ref[g])
    return ug, dug, dvg, rstd, vhat, vn, tril, s


def _gmlp_fwd(z, ln_g, ln_b, ws, bsb, GW):
    S = z.shape[0]
    G, T, _ = ws.shape
    Dg = GW // G

    def body(u_ref, v_ref, lg_ref, lb_ref, ws_ref, bsb_ref, ya_ref):
        ug, _, _, _, _, _, _, s = _gmlp_common(u_ref, v_ref, lg_ref, lb_ref, ws_ref, bsb_ref, G, T, Dg)
        for g in range(G):
            sl = slice(g * Dg, (g + 1) * Dg)
            ya_ref[:, sl] = (ug[:, sl] * s[g]).astype(BF16)

    vec = pl.BlockSpec((1, GW), lambda c: (0, 0))
    return pl.pallas_call(
        body, name="gmlp_fwd", grid=(S // T,),
        in_specs=[pl.BlockSpec((T, GW), lambda c: (c, 0)), pl.BlockSpec((T, GW), lambda c: (c, 1)), vec, vec,
                  pl.BlockSpec((G, T, T), lambda c: (0, 0, 0)), pl.BlockSpec((G, T, Dg), lambda c: (0, 0, 0))],
        out_specs=pl.BlockSpec((T, GW), lambda c: (c, 0)), out_shape=jax.ShapeDtypeStruct((S, GW), BF16),
        compiler_params=_cparams(("parallel",)))(z, z, ln_g, ln_b, ws, bsb)


def _gmlp_bwd(z, dya, ln_g, ln_b, ws, bsb, GW):
    S = z.shape[0]
    G, T, _ = ws.shape
    Dg = GW // G
    nc = S // T

    def body(u_ref, v_ref, dya_ref, lg_ref, lb_ref, ws_ref, bsb_ref, dz_ref, dln_ref, dws_ref, dbs_ref, dbs_acc, dvh):
        c = pl.program_id(0)

        @pl.when(c == 0)
        def _():
            dln_ref[...] = jnp.zeros_like(dln_ref)
            dws_ref[...] = jnp.zeros_like(dws_ref)
            dbs_acc[...] = jnp.zeros_like(dbs_acc)

        ug, dug, dvg, rstd, vhat, vn, tril, s = _gmlp_common(u_ref, v_ref, lg_ref, lb_ref, ws_ref, bsb_ref, G, T, Dg)
        dya_v = dya_ref[...]
        for g in range(G):
            sl = slice(g * Dg, (g + 1) * Dg)
            dy_g = dya_v[:, sl]
            dz_ref[:, sl] = (dy_g * s[g] * dug[:, sl]).astype(BF16)
            ds = dy_g * ug[:, sl]
            dbs_acc[g] += ds
            w = jnp.where(tril, ws_ref[g], 0.0)
            dvn_g = _dot(w, ds, _TN)
            dws_ref[g] += jnp.where(tril, _dot(ds, vn[:, sl], _NT), 0.0)
            dln_ref[0:1, sl] += jnp.sum(dvn_g * vhat[:, sl], axis=0, keepdims=True)
            dln_ref[1:2, sl] += jnp.sum(dvn_g, axis=0, keepdims=True)
            dvh[:, sl] = dvn_g * lg_ref[:, sl]
        dvhat = dvh[...]
        m1 = jnp.mean(dvhat, axis=-1, keepdims=True)
        m2 = jnp.mean(dvhat * vhat, axis=-1, keepdims=True)
        dz_ref[:, GW:2 * GW] = (rstd * (dvhat - m1 - vhat * m2) * dvg).astype(BF16)

        @pl.when(c == nc - 1)
        def _():
            for g in range(G):
                dbs_ref[g] = jnp.sum(dbs_acc[g], axis=-1, keepdims=True)

    vec = pl.BlockSpec((1, GW), lambda c: (0, 0))
    return pl.pallas_call(
        body, name="gmlp_bwd", grid=(nc,),
        in_specs=[pl.BlockSpec((T, GW), lambda c: (c, 0)), pl.BlockSpec((T, GW), lambda c: (c, 1)),
                  pl.BlockSpec((T, GW), lambda c: (c, 0)), vec, vec,
                  pl.BlockSpec((G, T, T), lambda c: (0, 0, 0)), pl.BlockSpec((G, T, Dg), lambda c: (0, 0, 0))],
        out_specs=[pl.BlockSpec((T, 2 * GW), lambda c: (c, 0)), pl.BlockSpec((8, GW), lambda c: (0, 0)),
                   pl.BlockSpec((G, T, T), lambda c: (0, 0, 0)), pl.BlockSpec((G, T, 1), lambda c: (0, 0, 0))],
        out_shape=[jax.ShapeDtypeStruct((S, 2 * GW), BF16), jax.ShapeDtypeStruct((8, GW), F32),
                   jax.ShapeDtypeStruct((G, T, T), F32), jax.ShapeDtypeStruct((G, T, 1), F32)],
        scratch_shapes=[pltpu.VMEM((G, T, Dg), F32), pltpu.VMEM((T, GW), F32)],
        compiler_params=_cparams(("arbitrary",)))(z, z, dya, ln_g, ln_b, ws, bsb)


def _hg_common(q_ref, f_ref, hlb_ref):
    C = HG_CHUNK
    a = hlb_ref[...]
    lb = _sigmoid(a[0:1, :] - a[1:2, :])
    sig = _sigmoid(f_ref[...])
    f = lb + (1.0 - lb) * sig
    lf = jnp.log(f)
    kk = 1.0 - f
    q = q_ref[...]
    sq = _sigmoid(q)
    qa = q * sq
    row = lax.broadcasted_iota(jnp.int32, (C, C), 0)
    col = lax.broadcasted_iota(jnp.int32, (C, C), 1)
    tril = row >= col
    b = _ones_dot(tril.astype(BF16), lf)
    bm = b[HG_MID:HG_MID + 1, :]
    bl = b[C - 1:C, :]
    e_b = jnp.exp(b)
    e_qm = jnp.exp(jnp.minimum(b - bm, EXP_CLAMP))
    e_km = jnp.exp(jnp.minimum(bm - b, EXP_CLAMP))
    e_kl = jnp.exp(bl - b)
    return dict(lb=lb, sig=sig, f=f, kk=kk, q=q, sq=sq, qa=qa, tril=tril, e_b=e_b, e_qm=e_qm, e_km=e_km, e_kl=e_kl,
                e_l=jnp.exp(bl), qh=qa * e_b, qt=qa * e_qm, kt=kk * e_km, kh=kk * e_kl)


def _hg_fwd(z, hg_lb, ng, HW):
    S = z.shape[0]
    C, H, dk = HG_CHUNK, HW // HG_DK, HG_DK
    nc = S // C

    def body(q_ref, f_ref, i_ref, og_ref, hlb_ref, ng_ref, yb_ref, o_ref, st_ref, state):
        @pl.when(pl.program_id(0) == 0)
        def _():
            state[...] = jnp.zeros_like(state)

        t = _hg_common(q_ref, f_ref, hlb_ref)
        iv = i_ref[...]
        for h in range(H):
            sl = slice(h * dk, (h + 1) * dk)
            st = state[h]
            st_ref[h] = st
            a = jnp.where(t["tril"], _dot(t["qt"][:, sl], t["kt"][:, sl], _NT), 0.0)
            o_h = _dot(a, iv[:, sl]) + _dot(t["qh"][:, sl], st, _NT)
            state[h] = st * t["e_l"][:, sl] + _dot(iv[:, sl], t["kh"][:, sl], _TN)
            o_ref[:, sl] = o_h
            rr = lax.rsqrt(jnp.mean(o_h * o_h, axis=-1, keepdims=True) + EPS)
            og = og_ref[:, sl]
            yb_ref[:, sl] = (o_h * rr * ng_ref[:, sl] * (og * _sigmoid(og))).astype(BF16)

    def col(k):
        return pl.BlockSpec((C, HW), lambda c: (c, k))

    base = 2
    return pl.pallas_call(
        body, name="hgrn_fwd", grid=(nc,),
        in_specs=[col(base), col(base + 1), col(base + 2), col(base + 3),
                  pl.BlockSpec((2, HW), lambda c: (0, 0)), pl.BlockSpec((1, HW), lambda c: (0, 0))],
        out_specs=[pl.BlockSpec((C, HW), lambda c: (c, 0)), pl.BlockSpec((C, HW), lambda c: (c, 0)),
                   pl.BlockSpec((None, H, dk, dk), lambda c: (c, 0, 0, 0))],
        out_shape=[jax.ShapeDtypeStruct((S, HW), BF16), jax.ShapeDtypeStruct((S, HW), F32),
                   jax.ShapeDtypeStruct((nc, H, dk, dk), F32)],
        scratch_shapes=[pltpu.VMEM((H, dk, dk), F32)],
        compiler_params=_cparams(("arbitrary",)))(z, z, z, z, hg_lb, ng)


def _hg_bwd(z, o, states, dyb, hg_lb, ng, HW, dz_head, dz_tail):
    S = z.shape[0]
    C, H, dk = HG_CHUNK, HW // HG_DK, HG_DK
    nc = S // C
    B0 = dz_head.shape[1]
    DT = dz_tail.shape[2]
    INW = B0 + 4 * HW + 2 * DT

    def body(q_ref, f_ref, i_ref, og_ref, o_ref, st_ref, stn_ref, dyb_ref, hlb_ref, ng_ref, head_ref, tail_ref,
             dzf_ref, dng_ref, dhlb_ref, dtail_ref, dstate, cross, dqa_buf, dkk_buf, db_buf, dlb_acc):
        c = pl.program_id(0)
        dzf_ref[:, 0:B0] = head_ref[...]
        dzf_ref[:, B0 + 4 * HW:B0 + 4 * HW + DT] = tail_ref[0]
        dzf_ref[:, B0 + 4 * HW + DT:INW] = tail_ref[1]
        dz_ref = dzf_ref.at[:, B0:B0 + 4 * HW]

        @pl.when(c == 0)
        def _():
            dtail_ref[...] = jnp.zeros_like(dtail_ref)

        dtail_ref[0:1, :] += jnp.sum(tail_ref[0].astype(F32), axis=0, keepdims=True)
        dtail_ref[1:2, :] += jnp.sum(tail_ref[1].astype(F32), axis=0, keepdims=True)

        @pl.when(c == 0)
        def _():
            dstate[...] = jnp.zeros_like(dstate)
            dlb_acc[...] = jnp.zeros_like(dlb_acc)
            dng_ref[...] = jnp.zeros_like(dng_ref)

        def r16(v):
            return v.astype(BF16).astype(F32)

        t = _hg_common(q_ref, f_ref, hlb_ref)
        iv = i_ref[...]
        for h in range(H):
            sl = slice(h * dk, (h + 1) * dk)
            o_h, og, dyb_h, ng_h = o_ref[:, sl], og_ref[:, sl], dyb_ref[:, sl], ng_ref[:, sl]
            sg = _sigmoid(og)
            silu_og = og * sg
            rr = lax.rsqrt(jnp.mean(o_h * o_h, axis=-1, keepdims=True) + EPS)
            on = o_h * rr
            dng_ref[0:1, sl] += jnp.sum(dyb_h * on * silu_og, axis=0, keepdims=True)
            dz_ref[:, 3 * HW + h * dk:3 * HW + (h + 1) * dk] = (dyb_h * on * ng_h * (sg * (1.0 + og * (1.0 - sg)))).astype(BF16)
            don = dyb_h * ng_h * silu_og
            do_h = rr * (don - on * jnp.mean(don * on, axis=-1, keepdims=True))

            qt, kt, qh, kh, iv_h = t["qt"][:, sl], t["kt"][:, sl], t["qh"][:, sl], t["kh"][:, sl], iv[:, sl]
            a = jnp.where(t["tril"], _dot(qt, kt, _NT), 0.0)
            da = jnp.where(t["tril"], _dot(do_h, iv_h, _NT), 0.0)
            st, dst = st_ref[h], dstate[h]
            cross[:, sl] = jnp.sum(dst * stn_ref[h], axis=0, keepdims=True)
            dqh = _dot(do_h, st)
            dstate[h] = _dot(do_h, qh, _TN) + dst * t["e_l"][:, sl]
            div = _dot(a, do_h, _TN) + _dot(kh, dst, _NT)
            dkh = _dot(iv_h, dst)
            dqt = _dot(da, kt)
            dkt = _dot(da, qt, _TN)
            dz_ref[:, 2 * HW + h * dk:2 * HW + (h + 1) * dk] = div.astype(BF16)
            dqa_buf[:, sl] = dqh * t["e_b"][:, sl] + dqt * t["e_qm"][:, sl]
            dkk_buf[:, sl] = dkt * t["e_km"][:, sl] + dkh * t["e_kl"][:, sl]
            db_buf[:, sl] = r16(qt) * dqt - r16(kt) * dkt + r16(qh) * dqh - r16(kh) * dkh

        dqa, dkk = dqa_buf[...], dkk_buf[...]
        triu = jnp.logical_not(t["tril"]) | (lax.broadcasted_iota(jnp.int32, (C, C), 0) == lax.broadcasted_iota(jnp.int32, (C, C), 1))
        dlf = _ones_dot(triu.astype(BF16), db_buf[...]) + cross[...]
        df = dlf / t["f"] - dkk
        sig, lb = t["sig"], t["lb"]
        dz_ref[:, HW:2 * HW] = (df * (1.0 - lb) * sig * (1.0 - sig)).astype(BF16)
        dlb_acc[...] += jnp.sum(df * (1.0 - sig), axis=0, keepdims=True)
        q, sq = t["q"], t["sq"]
        dz_ref[:, 0:HW] = (dqa * (sq * (1.0 + q * (1.0 - sq)))).astype(BF16)

        @pl.when(c == nc - 1)
        def _():
            da0 = dlb_acc[...] * lb * (1.0 - lb)
            dhlb_ref[0:1, :] = da0
            dhlb_ref[1:2, :] = -da0

    def col(k):
        return pl.BlockSpec((C, HW), lambda c: (nc - 1 - c, k))

    base = 2
    return pl.pallas_call(
        body, name="hgrn_bwd", grid=(nc,),
        in_specs=[col(base), col(base + 1), col(base + 2), col(base + 3), col(0),
                  pl.BlockSpec((None, H, dk, dk), lambda c: (nc - 1 - c, 0, 0, 0)),
                  pl.BlockSpec((None, H, dk, dk), lambda c: (jnp.minimum(nc - c, nc - 1), 0, 0, 0)), col(0),
                  pl.BlockSpec((2, HW), lambda c: (0, 0)), pl.BlockSpec((1, HW), lambda c: (0, 0)),
                  pl.BlockSpec((C, B0), lambda c: (nc - 1 - c, 0)), pl.BlockSpec((2, C, DT), lambda c: (0, nc - 1 - c, 0))],
        out_specs=[pl.BlockSpec((C, INW), lambda c: (nc - 1 - c, 0)), pl.BlockSpec((8, HW), lambda c: (0, 0)),
                   pl.BlockSpec((2, HW), lambda c: (0, 0)), pl.BlockSpec((2, DT), lambda c: (0, 0))],
        out_shape=[jax.ShapeDtypeStruct((S, INW), BF16), jax.ShapeDtypeStruct((8, HW), F32), jax.ShapeDtypeStruct((2, HW), F32),
                   jax.ShapeDtypeStruct((2, DT), F32)],
        scratch_shapes=[pltpu.VMEM((H, dk, dk), F32), pltpu.VMEM((1, HW), F32), pltpu.VMEM((C, HW), F32), pltpu.VMEM((C, HW), F32),
                        pltpu.VMEM((C, HW), F32), pltpu.VMEM((1, HW), F32)],
        compiler_params=_cparams(("arbitrary",)))(z, z, z, z, o, states, states, dyb, hg_lb, ng, dz_head, dz_tail)


def _position():
    x, y, c = lax.axis_index("x"), lax.axis_index("y"), lax.axis_index("c")
    return x, y, c, 4 * x + 2 * y + c


def _flip(x, y, c, k):
    return (1 - x if k & 4 else x, 1 - y if k & 2 else y, 1 - c if k & 1 else c)


def _allgather_small(name, v):
    R, L = v.shape

    def body(v_ref, out_ref, send_sems, recv_sems):
        x, y, c, me = _position()
        out_ref[me] = v_ref[...]
        copies = []
        for k in range(1, N_DEV):
            cp = pltpu.make_async_remote_copy(src_ref=v_ref, dst_ref=out_ref.at[me], send_sem=send_sems.at[k - 1],
                                              recv_sem=recv_sems.at[k - 1], device_id=_flip(x, y, c, k), device_id_type=MESH)
            cp.start()
            copies.append(cp)
        for cp in copies:
            cp.wait()

    return pl.pallas_call(
        body, name=name, out_shape=jax.ShapeDtypeStruct((N_DEV, R, L), v.dtype),
        in_specs=[pl.BlockSpec(memory_space=pltpu.VMEM)], out_specs=pl.BlockSpec(memory_space=pltpu.VMEM),
        scratch_shapes=[pltpu.SemaphoreType.DMA((N_DEV - 1,)), pltpu.SemaphoreType.DMA((N_DEV - 1,))],
        compiler_params=pltpu.CompilerParams(vmem_limit_bytes=VMEM_LIMIT),
    )(v)


_HBM = pl.BlockSpec(memory_space=pltpu.HBM)
_SEM = pl.BlockSpec(memory_space=pltpu.SEMAPHORE)
_EFFECT = pltpu.SideEffectType.DATAFLOW_SIDE_EFFECTING


def _split_start(name, bufs, n_sems, copies_fn, after=None):
    nb = len(bufs)
    extra = [] if after is None else [after]
    k = nb + len(extra)

    def body(*refs):
        for cp in copies_fn(refs[:nb], refs[k], refs[k + 1]):
            cp.start()
        refs[-1][...] = jnp.zeros_like(refs[-1])

    sems = pltpu.SemaphoreType.DMA((n_sems,))
    res = pl.pallas_call(
        body, name=name,
        out_shape=(sems, sems, *[pltpu.HBM(a.shape, a.dtype) for a in bufs], jax.ShapeDtypeStruct((8, LANES), F32)),
        in_specs=[_HBM] * nb + [pl.BlockSpec(memory_space=pl.ANY)] * len(extra),
        out_specs=(_SEM, _SEM, *[_HBM] * nb, pl.BlockSpec(memory_space=pltpu.VMEM)),
        input_output_aliases={i: 2 + i for i in range(nb)},
        compiler_params=pltpu.CompilerParams(has_side_effects=_EFFECT),
    )(*[pltpu.with_memory_space_constraint(a, pltpu.HBM) for a in bufs], *extra)
    return res[0], res[1], list(res[2:2 + nb]), res[-1]


def _split_wait(name, bufs, send_sems, recv_sems, after, copies_fn):
    nb = len(bufs)

    def body(*refs):
        for cp in copies_fn(refs[:nb], refs[nb], refs[nb + 1]):
            cp.wait_send()
            cp.wait_recv()

    res = pl.pallas_call(
        body, name=name, out_shape=tuple(pltpu.HBM(a.shape, a.dtype) for a in bufs),
        in_specs=[_HBM] * nb + [_SEM, _SEM, pl.BlockSpec(memory_space=pl.ANY)], out_specs=tuple([_HBM] * nb),
        input_output_aliases={i: i for i in range(nb)},
        compiler_params=pltpu.CompilerParams(has_side_effects=_EFFECT),
    )(*bufs, send_sems, recv_sems, after)
    return list(res)


def _split_relay(name, bufs, send_sems, recv_sems, after, wait_fn, n_sems, start_fn):
    nb = len(bufs)

    def body(*refs):
        for cp in wait_fn(refs[:nb], refs[nb], refs[nb + 1]):
            cp.wait_send()
            cp.wait_recv()
        for cp in start_fn(refs[:nb], refs[nb + 3], refs[nb + 4]):
            cp.start()
        refs[-1][...] = jnp.zeros_like(refs[-1])

    sems = pltpu.SemaphoreType.DMA((n_sems,))
    res = pl.pallas_call(
        body, name=name, out_shape=(sems, sems, *[pltpu.HBM(a.shape, a.dtype) for a in bufs], jax.ShapeDtypeStruct((8, LANES), F32)),
        in_specs=[_HBM] * nb + [_SEM, _SEM, pl.BlockSpec(memory_space=pl.ANY)],
        out_specs=(_SEM, _SEM, *[_HBM] * nb, pl.BlockSpec(memory_space=pltpu.VMEM)),
        input_output_aliases={i: 2 + i for i in range(nb)},
        compiler_params=pltpu.CompilerParams(has_side_effects=_EFFECT),
    )(*bufs, send_sems, recv_sems, after)
    return res[0], res[1], list(res[2:2 + nb]), res[-1]


N_CHIP = 4


def _chip_flip(x, y, k):
    return (1 - x if k & 2 else x), (1 - y if k & 1 else y)


def _gather_first_copies(n):
    def copies(bufs, send_sems, recv_sems):
        x, y, c, me = _position()
        out = []
        for w in range(n):
            for k in range(N_CHIP):
                to = (x, y, 1 - c) if k == 0 else (*_chip_flip(x, y, k), c)
                out.append(pltpu.make_async_remote_copy(
                    src_ref=bufs[w], dst_ref=bufs[n + w].at[me], send_sem=send_sems.at[w * N_CHIP + k],
                    recv_sem=recv_sems.at[w * N_CHIP + k], device_id=to, device_id_type=MESH))
        return out
    return copies


def _gather_relay_copies(n):
    def copies(bufs, send_sems, recv_sems):
        x, y, c, _ = _position()
        out = []
        for w in range(n):
            for k in range(1, N_CHIP):
                px, py = _chip_flip(x, y, k)
                blk = bufs[n + w].at[4 * px + 2 * py + c]
                out.append(pltpu.make_async_remote_copy(
                    src_ref=blk, dst_ref=blk, send_sem=send_sems.at[w * (N_CHIP - 1) + k - 1],
                    recv_sem=recv_sems.at[w * (N_CHIP - 1) + k - 1], device_id=(x, y, 1 - c), device_id_type=MESH))
        return out
    return copies


def _small_gather_copies(bufs, send_sems, recv_sems):
    x, y, c, me = _position()
    return [pltpu.make_async_remote_copy(src_ref=bufs[0], dst_ref=bufs[1].at[me], send_sem=send_sems.at[k - 1], recv_sem=recv_sems.at[k - 1],
                                         device_id=_flip(x, y, c, k), device_id_type=MESH) for k in range(1, N_DEV)]


def _forward_first_copies(n):
    def copies(bufs, send_sems, recv_sems):
        x, y, c, me = _position()
        out = []
        for w in range(n):
            for k, to in enumerate([(x, y, 1 - c), (1 - x, y, c), (x, 1 - y, c)]):
                out.append(pltpu.make_async_remote_copy(
                    src_ref=bufs[w], dst_ref=bufs[n + w].at[me], send_sem=send_sems.at[w * 3 + k],
                    recv_sem=recv_sems.at[w * 3 + k], device_id=to, device_id_type=MESH))
        return out
    return copies


def _forward_second_copies(n):
    def copies(bufs, send_sems, recv_sems):
        x, y, c, _ = _position()
        out = []
        for w in range(n):
            half = bufs[n + w].shape[1] // 2
            for k, (src_chip, rows, to) in enumerate([((1 - x, y), pl.ds(0, half), (x, 1 - y, c)), ((x, 1 - y), pl.ds(half, half), (1 - x, y, c))]):
                blk = bufs[n + w].at[4 * src_chip[0] + 2 * src_chip[1] + c, rows]
                out.append(pltpu.make_async_remote_copy(src_ref=blk, dst_ref=blk, send_sem=send_sems.at[w * 4 + k],
                                                        recv_sem=recv_sems.at[w * 4 + k], device_id=to, device_id_type=MESH))
            for k, (px, py) in enumerate([(1 - x, y), (x, 1 - y)]):
                blk = bufs[n + w].at[4 * px + 2 * py + c]
                out.append(pltpu.make_async_remote_copy(src_ref=blk, dst_ref=blk, send_sem=send_sems.at[w * 4 + 2 + k],
                                                        recv_sem=recv_sems.at[w * 4 + 2 + k], device_id=(x, y, 1 - c), device_id_type=MESH))
        return out
    return copies


def _forward_third_copies(n):
    def copies(bufs, send_sems, recv_sems):
        x, y, c, _ = _position()
        out = []
        for w in range(n):
            blk = bufs[n + w].at[4 * (1 - x) + 2 * (1 - y) + c]
            out.append(pltpu.make_async_remote_copy(src_ref=blk, dst_ref=blk, send_sem=send_sems.at[w], recv_sem=recv_sems.at[w],
                                                    device_id=(x, y, 1 - c), device_id_type=MESH))
        return out
    return copies


def _to_sibling_copies(n):
    def copies(bufs, send_sems, recv_sems):
        x, y, c, _ = _position()
        out = []
        for w in range(n):
            for q in range(N_CHIP):
                out.append(pltpu.make_async_remote_copy(
                    src_ref=bufs[w].at[2 * q + 1 - c], dst_ref=bufs[n + w].at[q], send_sem=send_sems.at[w * N_CHIP + q],
                    recv_sem=recv_sems.at[w * N_CHIP + q], device_id=(x, y, 1 - c), device_id_type=MESH))
        return out
    return copies


def _to_owner_copies(n):
    def copies(bufs, send_sems, recv_sems):
        x, y, c, _ = _position()
        out = []
        for w in range(n):
            for k in range(1, N_CHIP):
                px, py = (1 - x if k & 2 else x), (1 - y if k & 1 else y)
                out.append(pltpu.make_async_remote_copy(
                    src_ref=bufs[w].at[2 * px + py], dst_ref=bufs[n + w].at[k - 1], send_sem=send_sems.at[w * (N_CHIP - 1) + k - 1],
                    recv_sem=recv_sems.at[w * (N_CHIP - 1) + k - 1], device_id=(px, py, c), device_id_type=MESH))
        return out
    return copies


def _chip_sum(name, stack, landed, c_idx):
    _, R, C = stack.shape
    tr = _tile(R, max(BF16_ROWS, STREAM_TILE // C), BF16_ROWS)

    def body(c_ref, a_ref, b_ref, o_ref):
        o_ref[...] = (a_ref[...].astype(F32) + b_ref[...].astype(F32)).astype(o_ref.dtype)

    return pl.pallas_call(
        body, name=name,
        grid_spec=pltpu.PrefetchScalarGridSpec(
            num_scalar_prefetch=1, grid=(N_CHIP, R // tr),
            in_specs=[pl.BlockSpec((None, tr, C), lambda q, i, c_ref: (2 * q + c_ref[0], i, 0)),
                      pl.BlockSpec((None, tr, C), lambda q, i, c_ref: (q, i, 0))],
            out_specs=pl.BlockSpec((None, tr, C), lambda q, i, c_ref: (q, i, 0))),
        out_shape=jax.ShapeDtypeStruct((N_CHIP, R, C), stack.dtype),
        compiler_params=_cparams(("parallel", "parallel")))(c_idx, stack, landed)


def _ada_mod(c16, w):
    _, D = c16.shape
    n = w.shape[1]
    tk = _tile(D, 512)
    nk = D // tk

    def body(c_ref, w_ref, o_ref, ca_ref):
        @pl.when(pl.program_id(0) == 0)
        def _():
            o_ref[...] = jnp.zeros_like(o_ref)

        cv = c_ref[...]
        ca = cv * _sigmoid(cv)
        ca_ref[...] = ca
        o_ref[...] += _dot(ca, w_ref[...])

    return pl.pallas_call(
        body, name="ada_mod", grid=(nk,),
        in_specs=[pl.BlockSpec((16, tk), lambda k: (0, k)), pl.BlockSpec((tk, n), lambda k: (k, 0))],
        out_specs=[pl.BlockSpec((16, n), lambda k: (0, 0)), pl.BlockSpec((16, tk), lambda k: (0, k))],
        out_shape=[jax.ShapeDtypeStruct((16, n), F32), jax.ShapeDtypeStruct((16, D), F32)],
        compiler_params=_cparams(("arbitrary",)))(c16, w)


def _cast_shard(name, wf, slot):
    r, c = wf.shape
    tr = _tile(r, max(BF16_ROWS, STREAM_TILE // c), BF16_ROWS)

    def body(slot_ref, w_ref, s_ref, g_ref):
        v = w_ref[...].astype(BF16)
        s_ref[...] = v
        g_ref[...] = v

    return pl.pallas_call(
        body, name=name,
        grid_spec=pltpu.PrefetchScalarGridSpec(
            num_scalar_prefetch=1, grid=(r // tr,), in_specs=[pl.BlockSpec((tr, c), lambda i, s: (i, 0))],
            out_specs=[pl.BlockSpec((tr, c), lambda i, s: (i, 0)), pl.BlockSpec((None, tr, c), lambda i, s: (s[0], i, 0))]),
        out_shape=[jax.ShapeDtypeStruct((r, c), BF16), jax.ShapeDtypeStruct((N_DEV, r, c), BF16)],
        compiler_params=_cparams(("parallel",)))(slot, wf)


def _adam_math(w, g, m, v):
    m2 = ADAM_B1 * m + (1.0 - ADAM_B1) * g
    v2 = ADAM_B2 * v + (1.0 - ADAM_B2) * (g * g)
    m_hat = m2 / (1.0 - ADAM_B1 ** ADAM_STEP)
    v_hat = v2 / (1.0 - ADAM_B2 ** ADAM_STEP)
    delta = -ADAM_LR * (m_hat / (jnp.sqrt(v_hat) + ADAM_EPS) + ADAM_WD * w)
    return delta, m2, v2


def _adamw(name, w, m, v, own, own_slot, parts=(), row0=0, into=None):
    R, C = w.shape
    Rp = own.shape[1]
    tr = _tile(Rp, max(BF16_ROWS, ADAMW_TILE // C), BF16_ROWS)
    off = row0 // tr
    n_p = len(parts)
    held = [] if into is None else list(into)

    def body(slot_ref, *refs):
        w_ref, m_ref, v_ref, own_ref = refs[:4]
        g_ref, d_ref, m2_ref, v2_ref = refs[4 + n_p + len(held):]
        g = own_ref[...].astype(F32)
        for p_ref in refs[4:4 + n_p]:
            for s in range(p_ref.shape[0]):
                g = g + p_ref[s].astype(F32)
        delta, m2, v2 = _adam_math(w_ref[...], g, m_ref[...], v_ref[...])
        g_ref[...] = g
        d_ref[...] = delta
        m2_ref[...] = m2
        v2_ref[...] = v2

    blk = pl.BlockSpec((tr, C), lambda i, s: (i + off, 0))
    out = jax.ShapeDtypeStruct((R, C), F32)
    return pl.pallas_call(
        body, name=name,
        grid_spec=pltpu.PrefetchScalarGridSpec(
            num_scalar_prefetch=1, grid=(Rp // tr,),
            in_specs=[blk, blk, blk, pl.BlockSpec((None, tr, C), lambda i, s: (s[0], i, 0))]
            + [pl.BlockSpec((a.shape[0], tr, C), lambda i, s: (0, i, 0)) for a in parts]
            + [pl.BlockSpec(memory_space=pl.ANY)] * len(held),
            out_specs=[blk] * 4),
        out_shape=[out] * 4, input_output_aliases={5 + n_p + i: i for i in range(len(held))},
        compiler_params=_cparams(("parallel",)))(own_slot, w, m, v, own, *parts, *held)


def _small_update(gathered, w, m, v, after, rows):
    _, R, L = gathered.shape
    rs = w.shape[0]
    n = len(rows)
    assert all(r % 8 == 0 for r in rows) and sum(rows) <= rs and rs + 8 <= R

    def body(p_ref, w_ref, m_ref, v_ref, after_ref, *outs):
        g = p_ref[0]
        for p in range(1, N_DEV):
            g = g + p_ref[p]
        kinds = (g,) + _adam_math(w_ref[...], g[0:rs, :], m_ref[...], v_ref[...])
        at = 0
        for k, r in enumerate(rows):
            for idx, val in enumerate(kinds):
                outs[idx * n + k][...] = val[at:at + r, :]
            at += r
        outs[4 * n][...] = g[at:at + 8, :]

    vm = pl.BlockSpec(memory_space=pltpu.VMEM)
    shapes = [jax.ShapeDtypeStruct((r, L), F32) for _ in range(4) for r in rows] + [jax.ShapeDtypeStruct((8, L), F32)]
    return pl.pallas_call(body, name="small_update", in_specs=[vm] * 4 + [pl.BlockSpec(memory_space=pl.ANY)], out_specs=[vm] * len(shapes),
                          out_shape=shapes, compiler_params=pltpu.CompilerParams(vmem_limit_bytes=VMEM_LIMIT))(gathered, w, m, v, after)


class _Fetched(dict):
    def __init__(self, fetch):
        super().__init__()
        self.fetch = fetch

    def first(self, key, after):
        self[key] = self.fetch(key, after)
        return self[key]


def _local_step(x, tgt, mod, p, fetch, F, scatter=None):
    S, D = x.shape
    GW, HW = p["ln_g"].shape[1], p["hg_ng"].shape[1]
    G, T, _ = p["ws"].shape
    w = _Fetched(fetch)
    INW = 2 * GW + 4 * HW + 2 * D
    in_loc, br_loc, fi_loc = INW // N_DEV, D // N_DEV, 2 * F // N_DEV
    assert GW == HW and F % fi_loc == 0
    sh1, sc1, gt1, sh2, sc2, gt2 = (mod[:, k * D:(k + 1) * D] for k in range(6))
    bsb = jnp.broadcast_to(p["bs"][:, :, None], (G, T, GW // G))

    tm = _tile(S, 1024, 16)
    tmh = _tile(S, 512, 16)
    tn_in = _tile(in_loc, 1280)
    tn_d = _tile(D, 512)
    tn_br = _tile(br_loc, 512)
    tk_s = S
    tm_w = _tile(D, 1024)
    g_off = 2 * GW + 4 * HW

    h1 = _norm_mod("norm1", x, p["norm1_g"], sc1, sh1)
    z = _mm_nn_stacked("proj_in", h1, w.first("in", h1), tm=tm, tn=tn_in, tk=D)[0]
    ya = _gmlp_fwd(z, p["ln_g"], p["ln_b"], p["ws"], bsb, GW)
    yb, o_hg, states = _hg_fwd(z, p["hg_lb"], p["hg_ng"], HW)
    flat = {k: jnp.swapaxes(w.first(k, yb), 0, 1).reshape(GW, D) for k in ("bg", "bh")}
    tn_f = _tile(D, 1024)
    pa = _matmul(
        "branch_gmlp", ya, flat["bg"], dims=_NN, grid_mnk=(S // tm, D // tn_f, 1), tiles=(tm, tn_f),
        a_spec=pl.BlockSpec((tm, GW), lambda i, j, k: (i, 0)), b_spec=pl.BlockSpec((GW, tn_f), lambda i, j, k: (0, j)),
        out_shapes=[jax.ShapeDtypeStruct((S, D), F32)], out_specs=[pl.BlockSpec((tm, tn_f), lambda i, j, k: (i, j))], epilogue=_store(F32))[0]
    t_fi = w.first("fi_early", pa)

    def gates(ga_ref, gb_ref, ba_ref, bb_ref):
        return _sigmoid(ga_ref[...] + ba_ref[...]), _sigmoid(gb_ref[...] + bb_ref[...])

    def gate_specs(tn_, tm_=tm):
        o1, o2 = g_off // tn_, (g_off + D) // tn_
        return [pl.BlockSpec((tm_, tn_), lambda i, j, k: (i, o1 + j)), pl.BlockSpec((tm_, tn_), lambda i, j, k: (i, o2 + j)),
                pl.BlockSpec((1, tn_), lambda i, j, k: (0, j)), pl.BlockSpec((1, tn_), lambda i, j, k: (0, D // tn_ + j))]

    def merge_ep(acc, ex, outs):
        ga, gb = gates(*ex[1:5])
        outs[0][...] = acc.astype(BF16)
        outs[1][...] = (ga * ex[0][...] + gb * acc).astype(BF16)

    tile_o = pl.BlockSpec((tmh, tn_f), lambda i, j, k: (i, j))
    pb, y = _matmul(
        "branch_hg_merge", yb, flat["bh"], dims=_NN, grid_mnk=(S // tmh, D // tn_f, 1), tiles=(tmh, tn_f),
        a_spec=pl.BlockSpec((tmh, HW), lambda i, j, k: (i, 0)), b_spec=pl.BlockSpec((HW, tn_f), lambda i, j, k: (0, j)),
        extras=[pa, z, z, p["b_gate"], p["b_gate"]], extra_specs=[tile_o, *gate_specs(tn_f, tmh)],
        out_shapes=[jax.ShapeDtypeStruct((S, D), BF16), jax.ShapeDtypeStruct((S, D), BF16)], out_specs=[tile_o, tile_o],
        epilogue=merge_ep, after=t_fi)

    def resid_ep(acc, ex, outs):
        outs[0][...] = acc.astype(BF16)
        outs[1][...] = ex[0][...] + ex[1][...] * acc

    def resid_mm(name, a, b, res, gt, tm_, tn_):
        K = a.shape[1]
        t_o = pl.BlockSpec((tm_, tn_), lambda i, j, k: (i, j))
        return _matmul(
            name, a, b, dims=_NN, grid_mnk=(S // tm_, D // tn_, 1), tiles=(tm_, tn_),
            a_spec=pl.BlockSpec((tm_, K), lambda i, j, k: (i, 0)), b_spec=pl.BlockSpec((K, tn_), lambda i, j, k: (0, j)),
            extras=[res, gt], extra_specs=[t_o, pl.BlockSpec((1, tn_), lambda i, j, k: (0, j))],
            out_shapes=[jax.ShapeDtypeStruct((S, D), BF16), jax.ShapeDtypeStruct((S, D), F32)], out_specs=[t_o, t_o], epilogue=resid_ep)

    o1, xm = resid_mm("proj_out", y, w.first("out", z), x, gt1, tm, tn_f)
    h2 = _norm_mod("norm2", xm, p["norm2_g"], sc2, sh2)
    hf, hf_fac = _ffn_in_swiglu(h2, w.first("fi", h2))
    o2, x3 = resid_mm("ffn_out", hf, w.first("fo", hf), xm, gt2, tmh, tn_d)
    dx3, do2, vec_l = _loss_head(x3, tgt, p["final_g"], o2, gt2)

    nf = F // fi_loc

    def dswiglu_ep(acc, ex, outs):
        outs[0][0] = (acc * ex[0][0].astype(F32)).astype(BF16)
        outs[0][1] = (acc * ex[0][1].astype(F32)).astype(BF16)

    pair = pl.BlockSpec((2, tmh, fi_loc), lambda i, j, k: (0, i, j))
    dab = _matmul(
        "ffn_out_dx", do2, w["fo"], dims=_NT, grid_mnk=(S // tmh, nf, 1), tiles=(tmh, fi_loc),
        a_spec=pl.BlockSpec((tmh, D), lambda i, j, k: (i, 0)), b_spec=pl.BlockSpec((fi_loc, D), lambda i, j, k: (j, 0)),
        extras=[hf_fac], extra_specs=[pair], out_shapes=[jax.ShapeDtypeStruct((2, S, F), BF16)], out_specs=[pair],
        epilogue=dswiglu_ep)[0]
    start = (lambda name, grads: scatter[0](name, grads)) if scatter is not None else (lambda name, grads: None)
    push = (lambda name, after: scatter[1](name, after)) if scatter is not None else (lambda name, after: None)

    def zero(token):
        return 0.0 if token is None else token[0:1, 0:1]

    tm_f = _tile(F, 1536)
    g_fo = _mm_tn("ffn_out_dw", hf, do2, pl.BlockSpec((tk_s, D), lambda i, j, k: (k, j)), Mo=F, No=D, S=S, tm=tm_f, tn=D, tk=tk_s)
    g_fi = _mm_tn("ffn_in_dw", h2, dab, pl.BlockSpec((None, tk_s, fi_loc), lambda i, j, k: (j // nf, k, j % nf)),
                  Mo=D, No=2 * F, S=S, tm=D, tn=fi_loc, tk=tk_s, stacked_nloc=fi_loc, after=g_fo)
    t_ffn = start("scatter_ffn", dict(fo=g_fo, fi=g_fi))
    dh2 = _mm_nt_stacked("ffn_in_dx", pl.BlockSpec((None, tm, fi_loc), lambda i, j, k: (k // nf, i, k % nf)), dab, w["fi"],
                         M=S, tm=tm, tn=D, tk=fi_loc, after=t_ffn)
    dxm, vec2, do1 = _norm_mod_bwd("norm2_bwd", dh2, xm, p["norm2_g"], sc2, dx3, o1, gt1)
    t_ffn = push("scatter_ffn", dxm)

    def dmerge_ep(acc, ex, outs):
        ga, gb = gates(*ex[2:6])
        outs[0][...] = (acc * ga).astype(BF16)
        outs[1][...] = (acc * gb).astype(BF16)
        outs[2][0] = (acc * ex[0][...] * ga * (1.0 - ga)).astype(BF16)
        outs[2][1] = (acc * ex[1][...] * gb * (1.0 - gb)).astype(BF16)

    t_o = pl.BlockSpec((tm, tn_d), lambda i, j, k: (i, j))
    dpa, dpb, dg2 = _matmul(
        "proj_out_dx", do1, w["out"], dims=_NT, grid_mnk=(S // tm, D // tn_d, 1), tiles=(tm, tn_d),
        a_spec=pl.BlockSpec((tm, D), lambda i, j, k: (i, 0)), b_spec=pl.BlockSpec((tn_d, D), lambda i, j, k: (j, 0)),
        extras=[pa, pb, z, z, p["b_gate"], p["b_gate"]], extra_specs=[t_o, t_o, *gate_specs(tn_d)],
        out_shapes=[jax.ShapeDtypeStruct((S, D), BF16), jax.ShapeDtypeStruct((S, D), BF16), jax.ShapeDtypeStruct((2, S, D), BF16)],
        out_specs=[t_o, t_o, pl.BlockSpec((2, tm, tn_d), lambda i, j, k: (0, i, j))], epilogue=dmerge_ep, after=t_ffn)
    g_out = _mm_tn("proj_out_dw", y, do1, pl.BlockSpec((tk_s, D), lambda i, j, k: (k, j)), Mo=D, No=D, S=S, tm=tm_w, tn=D, tk=tk_s)
    tn_g = _tile(GW, 512)
    b_br = pl.BlockSpec((tk_s, br_loc), lambda i, j, k: (k, j))
    tm_b = _tile(GW, 1024)
    g_bg = _mm_tn("branch_gmlp_dw", ya, dpa, b_br, Mo=GW, No=D, S=S, tm=tm_b, tn=br_loc, tk=tk_s, stacked_nloc=br_loc)
    g_bh = _mm_tn("branch_hg_dw", yb, dpb, b_br, Mo=HW, No=D, S=S, tm=tm_b, tn=br_loc, tk=tk_s, stacked_nloc=br_loc)
    t_mix = start("scatter_mixer", dict(out=g_out, bg=g_bg, bh=g_bh))
    def branch_dx(name, dp, w_flat):
        return _matmul(
            name, dp, w_flat, dims=_NT, grid_mnk=(S // tm, GW // tn_g, 1), tiles=(tm, tn_g),
            a_spec=pl.BlockSpec((tm, D), lambda i, j, k: (i, 0)), b_spec=pl.BlockSpec((tn_g, D), lambda i, j, k: (j, 0)),
            out_shapes=[jax.ShapeDtypeStruct((S, GW), F32)], out_specs=[pl.BlockSpec((tm, tn_g), lambda i, j, k: (i, j))],
            epilogue=_store(F32), after=t_mix)[0]

    dya = branch_dx("branch_gmlp_dx", dpa, flat["bg"])
    dyb = branch_dx("branch_hg_dx", dpb, flat["bh"])
    dz_gmlp, dln, dws, dbs = _gmlp_bwd(z, dya, p["ln_g"], p["ln_b"], p["ws"], bsb, GW)
    t_mix = push("scatter_mixer", dz_gmlp)
    dz, dng, dhlb, db_gate = _hg_bwd(z, o_hg, states, dyb, p["hg_lb"], p["hg_ng"] + zero(t_mix), HW, dz_gmlp, dg2)
    half = D // 2
    tm_h = _tile(half, 1024)
    g_in = []
    t_in = None
    for hname, h in (("a", 0), ("b", 1)):
        g_in.append(_mm_tn("proj_in_dw_" + hname, h1, dz, pl.BlockSpec((tk_s, in_loc), lambda i, j, k: (k, j)), Mo=half, No=INW, S=S,
                           tm=tm_h, tn=in_loc, tk=tk_s, stacked_nloc=in_loc, after=t_in, a_off=h * (half // tm_h)))
        t_in = start("scatter_proj_in_" + hname, {"w_in_" + hname: g_in[-1]})
    t_in = push("scatter_proj_in_a", t_in)
    dh1 = _mm_nt_stacked("proj_in_dx", pl.BlockSpec((tm, in_loc), lambda i, j, k: (i, k)), dz, w["in"], M=S, tm=tm, tn=D, tk=in_loc,
                         after=t_in)
    dx, vec1 = _norm_mod_bwd("norm1_bwd", dh1, x, p["norm1_g"], sc1, dxm)

    dmod = jnp.concatenate([vec1[0:1], vec1[1:2], vec2[3:4], vec2[0:1], vec2[1:2], vec_l[2:3]], axis=1)
    small = dict(norm1_g=vec1[2:3], b_gate=db_gate.reshape(1, 2 * D), ln_g=dln[0:1], ln_b=dln[1:2], ws=dws, bs=dbs.reshape(G, T),
                 hg_lb=dhlb, hg_ng=dng[0:1], norm2_g=vec2[2:3], final_g=vec_l[1:2], loss=vec_l[0:1, 0:LANES])
    big = dict(w_in_a=g_in[0], w_in_b=g_in[1], bg=g_bg, bh=g_bh, out=g_out, fi=g_fi, fo=g_fo)
    return dx, big, small, dmod


_SMALL = ("b_ada", "norm1_g", "b_gate", "ln_g", "ln_b", "ws", "bs", "hg_lb", "hg_ng", "norm2_g", "final_g")


def _pack(parts, rows_mult=8):
    flat = [a.reshape(-1) for a in parts]
    offs, n = [], 0
    for a in flat:
        offs.append(n)
        n += a.shape[0]
    pad = (-n) % (LANES * rows_mult)
    if pad:
        flat.append(jnp.zeros((pad,), F32))
    return jnp.concatenate(flat).reshape(-1, LANES), offs


def kernel(x, c, w_ada, b_ada, norm1_g, w_in, b_gate, gmlp_ln_g, gmlp_ln_b, gmlp_ws, gmlp_bs, hg_lb, hg_norm_g, w_branch_gmlp, w_branch_hg, w_out, norm2_g, w_ffn_in, w_ffn_out, final_norm_g, loss_target, m_w_ada, m_b_ada, m_norm1_g, m_w_in, m_b_gate, m_gmlp_ln_g, m_gmlp_ln_b, m_gmlp_ws, m_gmlp_bs, m_hg_lb, m_hg_norm_g, m_w_branch_gmlp, m_w_branch_hg, m_w_out, m_norm2_g, m_w_ffn_in, m_w_ffn_out, m_final_norm_g, v_w_ada, v_b_ada, v_norm1_g, v_w_in, v_b_gate, v_gmlp_ln_g, v_gmlp_ln_b, v_gmlp_ws, v_gmlp_bs, v_hg_lb, v_hg_norm_g, v_w_branch_gmlp, v_w_branch_hg, v_w_out, v_norm2_g, v_w_ffn_in, v_w_ffn_out, v_final_norm_g):
    S, D = x.shape[1], x.shape[2]
    ada_loc = w_ada.shape[2]
    me = 4 * lax.axis_index("x") + 2 * lax.axis_index("y") + lax.axis_index("c")
    me_idx = me.astype(jnp.int32).reshape(1)

    def empty_hbm(shape, dtype):
        return pltpu.with_memory_space_constraint(lax.empty(shape, dtype), pltpu.HBM)

    groups = dict(gather_in=dict(keys=["in"], src=[w_in], forward=True),
                  gather_mixer=dict(keys=["bg", "bh", "out"], src=[w_branch_gmlp, w_branch_hg, w_out], forward=False),
                  gather_ffn_in=dict(keys=["fi"], src=[w_ffn_in], forward=True),
                  gather_ffn_out=dict(keys=["fo"], src=[w_ffn_out], forward=False))
    group_of = {k: gname for gname, g in groups.items() for k in g["keys"]}

    def first_hop(gname, after):
        g = groups[gname]
        n = len(g["keys"])
        cast = [_cast_shard(f"{gname}_cast_{k}", a[0], me_idx) for k, a in zip(g["keys"], g["src"])]
        shards, outs = [s for s, _ in cast], [o for _, o in cast]
        if g["forward"]:
            *g["hop"], token = _split_start(gname + "_hop1", shards + outs, n * 3, _forward_first_copies(n), after=after)
        else:
            *g["hop"], token = _split_start(gname + "_hop1", shards + outs, n * N_CHIP, _gather_first_copies(n), after=after)
        return token

    def second_hop(gname, after):
        g = groups[gname]
        n = len(g["keys"])
        *g["hop"], token = _split_relay(gname + "_hop2", g["hop"][2], g["hop"][0], g["hop"][1], after,
                                        _forward_first_copies(n), n * 4, _forward_second_copies(n))
        return token

    def finish(gname, after):
        g = groups[gname]
        n = len(g["keys"])
        send_sems, recv_sems, bufs = g["hop"]
        if g["forward"]:
            send_sems, recv_sems, bufs, _ = _split_relay(gname + "_hop3", bufs, send_sems, recv_sems, after,
                                                         _forward_second_copies(n), n, _forward_third_copies(n))
            bufs = _split_wait(gname + "_wait", bufs, send_sems, recv_sems, after, _forward_third_copies(n))
        else:
            send_sems, recv_sems, bufs, _ = _split_relay(gname + "_relay", bufs, send_sems, recv_sems, after,
                                                         _gather_first_copies(n), n * (N_CHIP - 1), _gather_relay_copies(n))
            bufs = _split_wait(gname + "_wait", bufs, send_sems, recv_sems, after, _gather_relay_copies(n))
        g["done"] = dict(zip(g["keys"], bufs[n:]))

    c_all = _allgather_small("gather_c", c.reshape(D // LANES, LANES)).reshape(N_DEV, D)
    token = first_hop("gather_in", c_all)
    mod_cols, c_act = _ada_mod(jnp.pad(c_all, ((0, 16 - N_DEV), (0, 0))) + token[0:1, 0:1], w_ada[0])
    mod_vec = mod_cols[:N_DEV].reshape(-1, LANES)
    mg_send, mg_recv, mg_bufs, token = _split_start(
        "gather_mod_start", [mod_vec, lax.dynamic_update_slice(lax.empty((N_DEV, *mod_vec.shape), F32), mod_vec[None], (me, 0, 0))],
        N_DEV - 1, _small_gather_copies)
    token = second_hop("gather_in", token)
    token = first_hop("gather_ffn_in", first_hop("gather_mixer", token))
    mod_all = _split_wait("gather_mod_wait", mg_bufs, mg_send, mg_recv, token, _small_gather_copies)[1].reshape(N_DEV, N_DEV, ada_loc)
    mod = lax.dynamic_index_in_dim(mod_all, me, axis=1, keepdims=False).reshape(1, N_DEV * ada_loc) + b_ada

    def fetch(key, after):
        if key == "in":
            finish("gather_in", after)
        elif key == "fi_early":
            return first_hop("gather_ffn_out", second_hop("gather_ffn_in", after))
        elif "done" not in groups[group_of[key]]:
            finish(group_of[key], after)
        arr = groups[group_of[key]]["done"][key]
        return arr.reshape(-1, D) if key in ("out", "fo") else arr

    p = dict(norm1_g=norm1_g, b_gate=b_gate, ln_g=gmlp_ln_g, ln_b=gmlp_ln_b, ws=gmlp_ws[0], bs=gmlp_bs[0], hg_lb=hg_lb,
             hg_ng=hg_norm_g, norm2_g=norm2_g, final_g=final_norm_g.reshape(1, D))

    in_flight = {}
    c_idx = lax.axis_index("c").astype(jnp.int32).reshape(1)
    chip_idx = (2 * lax.axis_index("x") + lax.axis_index("y")).astype(jnp.int32).reshape(1)

    def scatter_start(name, grads):
        keys = list(grads)
        n = len(keys)
        stacks = [grads[k].reshape(N_DEV, -1, grads[k].shape[-1]) for k in keys]
        lands = [empty_hbm((N_CHIP, *g.shape[1:]), g.dtype) for g in stacks]
        send_sems, recv_sems, bufs, token = _split_start(name + "_d2d", stacks + lands, n * N_CHIP, _to_sibling_copies(n))
        in_flight[name] = dict(keys=keys, stage1=(send_sems, recv_sems, bufs))
        return token

    def scatter_push(name, after):
        f = in_flight[name]
        n = len(f["keys"])
        send_sems, recv_sems, bufs = f["stage1"]
        bufs = _split_wait(name + "_d2d_wait", bufs, send_sems, recv_sems, after, _to_sibling_copies(n))
        sums = [_chip_sum(f"{name}_sum_{k}", bufs[i], bufs[n + i], c_idx) for i, k in enumerate(f["keys"])]
        lands = [empty_hbm((N_CHIP - 1, *s.shape[1:]), s.dtype) for s in sums]
        send_sems, recv_sems, bufs, token = _split_start(name + "_ici", sums + lands, n * (N_CHIP - 1), _to_owner_copies(n))
        f["stage2"] = (send_sems, recv_sems, bufs)
        return token

    grad_x, _, small, dmod = _local_step(x[0], loss_target[0], mod, p, fetch, w_ffn_out.shape[1] * N_DEV, (scatter_start, scatter_push))

    small["b_ada"] = dmod
    packed, offs = _pack([small[k] for k in _SMALL] + [small["loss"]])
    sg_send, sg_recv, sg_bufs, t_tail = _split_start(
        "gather_small_start", [packed, lax.dynamic_update_slice(lax.empty((N_DEV, *packed.shape), F32), packed[None], (me, 0, 0))],
        N_DEV - 1, _small_gather_copies)
    t_tail = scatter_push("scatter_proj_in_b", t_tail)
    big_w = dict(w_in=(w_in, m_w_in, v_w_in, "w_in"), bg=(w_branch_gmlp, m_w_branch_gmlp, v_w_branch_gmlp, "w_branch_gmlp"),
                 bh=(w_branch_hg, m_w_branch_hg, v_w_branch_hg, "w_branch_hg"), out=(w_out, m_w_out, v_w_out, "w_out"),
                 fi=(w_ffn_in, m_w_ffn_in, v_w_ffn_in, "w_ffn_in"), fo=(w_ffn_out, m_w_ffn_out, v_w_ffn_out, "w_ffn_out"))
    upd = {}

    def land_and_update(name, after):
        keys = in_flight[name]["keys"]
        n = len(keys)
        send_sems, recv_sems, bufs = in_flight[name]["stage2"]
        bufs = _split_wait(name + "_ici_wait", bufs, send_sems, recv_sems, after, _to_owner_copies(n))
        for i, k in enumerate(keys):
            if k in big_w:
                wt, mt, vt, out_name = big_w[k]
                upd[out_name] = _adamw("adamw_" + out_name, wt[0], mt[0], vt[0], bufs[i], chip_idx, [bufs[n + i]])
            else:
                wt, mt, vt, out_name = big_w["w_in"]
                upd[out_name] = _adamw("adamw_" + k, wt[0], mt[0], vt[0], bufs[i], chip_idx, [bufs[n + i]],
                                       row0=0 if k == "w_in_a" else bufs[i].shape[1], into=upd.get(out_name))
            after = upd[out_name][1]
        return after

    after = land_and_update("scatter_mixer", land_and_update("scatter_ffn", t_tail))
    gathered = _split_wait("gather_small_wait", sg_bufs, sg_send, sg_recv, after, _small_gather_copies)[1]
    wp = dict(p, b_ada=b_ada)
    ms = dict(b_ada=m_b_ada, norm1_g=m_norm1_g, b_gate=m_b_gate, ln_g=m_gmlp_ln_g, ln_b=m_gmlp_ln_b, ws=m_gmlp_ws, bs=m_gmlp_bs,
              hg_lb=m_hg_lb, hg_ng=m_hg_norm_g, norm2_g=m_norm2_g, final_g=m_final_norm_g)
    vs = dict(b_ada=v_b_ada, norm1_g=v_norm1_g, b_gate=v_b_gate, ln_g=v_gmlp_ln_g, ln_b=v_gmlp_ln_b, ws=v_gmlp_ws, bs=v_gmlp_bs,
              hg_lb=v_hg_lb, hg_ng=v_hg_norm_g, norm2_g=v_norm2_g, final_g=v_final_norm_g)
    w_sm, _ = _pack([wp[k] for k in _SMALL])
    m_sm, _ = _pack([ms[k] for k in _SMALL])
    v_sm, _ = _pack([vs[k] for k in _SMALL])
    shapes = dict(b_ada=b_ada.shape, norm1_g=norm1_g.shape, b_gate=b_gate.shape, ln_g=gmlp_ln_g.shape, ln_b=gmlp_ln_b.shape,
                  ws=gmlp_ws.shape, bs=gmlp_bs.shape, hg_lb=hg_lb.shape, hg_ng=hg_norm_g.shape, norm2_g=norm2_g.shape,
                  final_g=final_norm_g.shape)
    sm_out = _small_update(gathered, w_sm, m_sm, v_sm, after, [math.prod(shapes[k]) // LANES for k in _SMALL])

    def unpack(idx, k):
        return sm_out[idx * len(_SMALL) + _SMALL.index(k)].reshape(shapes[k])

    loss = sm_out[-1][0, 0]

    assert offs[0] == 0 and ada_loc % LANES == 0
    dmod_loc = lax.dynamic_slice_in_dim(gathered, me * (ada_loc // LANES), ada_loc // LANES, axis=1).reshape(N_DEV, ada_loc)
    ca_t = jnp.pad(c_act[:N_DEV].T, ((0, 0), (0, LANES - N_DEV))).astype(BF16)
    dm_p = jnp.pad(dmod_loc, ((0, LANES - N_DEV), (0, 0))).astype(BF16)
    tm_a = _tile(D, 512)
    g_ada = _matmul(
        "ada_dw", ca_t, dm_p, dims=_NN, grid_mnk=(D // tm_a, 1, 1), tiles=(tm_a, ada_loc),
        a_spec=pl.BlockSpec((tm_a, LANES), lambda i, j, k: (i, 0)), b_spec=pl.BlockSpec((LANES, ada_loc), lambda i, j, k: (0, 0)),
        out_shapes=[jax.ShapeDtypeStruct((1, D, ada_loc), F32)], out_specs=[pl.BlockSpec((None, tm_a, ada_loc), lambda i, j, k: (0, i, 0))],
        epilogue=_store(F32))[0]
    upd["w_ada"] = _adamw("adamw_w_ada", w_ada[0], m_w_ada[0], v_w_ada[0], g_ada, jnp.zeros((1,), jnp.int32))
    land_and_update("scatter_proj_in_b", land_and_update("scatter_proj_in_a", upd["w_ada"][1]))

    order = ("w_ada", "b_ada", "norm1_g", "w_in", "b_gate", "ln_g", "ln_b", "ws", "bs", "hg_lb", "hg_ng", "w_branch_gmlp", "w_branch_hg",
             "w_out", "norm2_g", "w_ffn_in", "w_ffn_out", "final_g")
    outs = [loss, grad_x[None]]
    for idx in range(4):
        for k in order:
            outs.append(upd[k][idx][None] if k in upd else unpack(idx, k))
    return tuple(outs)
```

```python
import functools
import math

import jax
import jax.numpy as jnp
from jax import lax
from jax.experimental import pallas as pl
from jax.experimental.pallas import tpu as pltpu

F32 = jnp.float32
BF16 = jnp.bfloat16
N_DEV = 8
EPS = 1e-6
LANES = 128
HG_DK = 128
HG_CHUNK = 64
HG_MID = HG_CHUNK // 2 - 1
EXP_CLAMP = 80.0
VMEM_LIMIT = 48 * 1024 * 1024
BF16_ROWS = 16
STREAM_TILE = 1 << 20
ADAMW_TILE = 3 << 17
ADAM_LR, ADAM_B1, ADAM_B2, ADAM_EPS, ADAM_WD, ADAM_STEP = 0.001, 0.9, 0.999, 1e-08, 0.01, 10
MESH = pl.DeviceIdType.MESH

_NN = (((1,), (0,)), ((), ()))
_NT = (((1,), (1,)), ((), ()))
_TN = (((0,), (0,)), ((), ()))


def _dot(a, b, dims=_NN):
    return lax.dot_general(a.astype(BF16), b.astype(BF16), dims, preferred_element_type=F32)


def _tile(n, target, mult=LANES):
    best = None
    for t in range(mult, min(n, target) + 1, mult):
        if n % t == 0:
            best = t
    return n if best is None else best


def _cparams(sem):
    return pltpu.CompilerParams(dimension_semantics=sem, vmem_limit_bytes=VMEM_LIMIT)


def _sigmoid(x):
    return 1.0 / (1.0 + jnp.exp(-x))


def _gelu_parts(x):
    k0 = math.sqrt(2.0 / math.pi)
    x2 = x * x
    t = jnp.tanh(k0 * (x + 0.044715 * x * x2))
    g = 0.5 * x * (1.0 + t)
    dg = 0.5 * (1.0 + t) + 0.5 * x * (1.0 - t * t) * (k0 * (1.0 + 3.0 * 0.044715 * x2))
    return g, dg


def _split3(x):
    h = x.astype(BF16)
    r = x - h.astype(F32)
    m = r.astype(BF16)
    lo = (r - m.astype(F32)).astype(BF16)
    return h, m, lo


def _ones_dot(mat01, x):
    h, m, lo = _split3(x)
    d = functools.partial(lax.dot_general, dimension_numbers=_NN, preferred_element_type=F32)
    return d(mat01, h) + d(mat01, m) + d(mat01, lo)


def _matmul(name, a, b, *, dims, grid_mnk, tiles, a_spec, b_spec, extras=(), extra_specs=(), out_shapes, out_specs, epilogue, after=None,
            sem=None):
    gm, gn, nk = grid_mnk
    tm, tn = tiles
    n_ex, n_out = len(extras), len(out_shapes)
    held = [] if after is None else [after]

    def body(*refs):
        a_ref, b_ref = refs[0], refs[1]
        ex = refs[2:2 + n_ex]
        outs = refs[2 + n_ex + len(held):2 + n_ex + len(held) + n_out]
        more = () if sem is None else (pl.program_id(0) == 0,)
        if nk == 1:
            epilogue(lax.dot_general(a_ref[...], b_ref[...], dims, preferred_element_type=F32), ex, outs, *more)
            return
        acc = refs[-1]
        k = pl.program_id(2)

        @pl.when(k == 0)
        def _():
            acc[...] = jnp.zeros_like(acc)

        acc[...] += lax.dot_general(a_ref[...], b_ref[...], dims, preferred_element_type=F32)

        @pl.when(k == nk - 1)
        def _():
            epilogue(acc[...], ex, outs, *more)

    return pl.pallas_call(
        body, name=name, grid=(gm, gn, nk), in_specs=[a_spec, b_spec, *extra_specs] + [pl.BlockSpec(memory_space=pl.ANY)] * len(held),
        out_specs=list(out_specs), out_shape=list(out_shapes), scratch_shapes=[] if nk == 1 else [pltpu.VMEM((tm, tn), F32)],
        compiler_params=_cparams(sem or ("parallel", "parallel", "arbitrary")),
    )(a, b, *extras, *held)


def _store(dtype):
    def ep(acc, ex, outs):
        outs[0][...] = acc.astype(dtype)
    return ep


def _mm_nn_stacked(name, a, wg, *, tm, tn, tk, out_dtype=F32, extras=(), extra_specs=(), out_shapes=None, out_specs=None, epilogue=None,
                   after=None):
    M, K = a.shape
    _, _, nloc = wg.shape
    N = nloc * N_DEV
    q = nloc // tn
    if out_shapes is None:
        out_shapes = [jax.ShapeDtypeStruct((M, N), out_dtype)]
        out_specs = [pl.BlockSpec((tm, tn), lambda i, j, k: (i, j))]
        epilogue = _store(out_dtype)
    return _matmul(
        name, a, wg, dims=_NN, grid_mnk=(M // tm, N // tn, K // tk), tiles=(tm, tn),
        a_spec=pl.BlockSpec((tm, tk), lambda i, j, k: (i, k)),
        b_spec=pl.BlockSpec((None, tk, tn), lambda i, j, k: (j // q, k, j % q)),
        extras=extras, extra_specs=extra_specs, out_shapes=out_shapes, out_specs=out_specs, epilogue=epilogue, after=after)


def _mm_nt_stacked(name, a_spec, a, wg, *, M, tm, tn, tk, out_dtype=F32, after=None, extras=(), extra_specs=(), out_shapes=None,
                   out_specs=None, epilogue=None, sem=None):
    _, Kw, nloc = wg.shape
    q = nloc // tk
    single = out_shapes is None
    if single:
        out_shapes = [jax.ShapeDtypeStruct((M, Kw), out_dtype)]
        out_specs = [pl.BlockSpec((tm, tn), lambda i, j, k: (i, j))]
        epilogue = _store(out_dtype)
    res = _matmul(
        name, a, wg, dims=_NT, grid_mnk=(M // tm, Kw // tn, (nloc * N_DEV) // tk), tiles=(tm, tn),
        a_spec=a_spec, b_spec=pl.BlockSpec((None, tn, tk), lambda i, j, k: (k // q, j, k % q)),
        extras=extras, extra_specs=extra_specs, out_shapes=out_shapes, out_specs=out_specs, epilogue=epilogue, after=after, sem=sem)
    return res[0] if single else res


def _mm_tn(name, a, b, b_spec, *, Mo, No, S, tm, tn, tk, stacked_nloc=None, after=None, a_off=0):
    if stacked_nloc is None:
        out_shape = jax.ShapeDtypeStruct((Mo, No), BF16)
        out_spec = pl.BlockSpec((tm, tn), lambda i, j, k: (i, j))
    else:
        q = stacked_nloc // tn
        out_shape = jax.ShapeDtypeStruct((N_DEV, Mo, stacked_nloc), BF16)
        out_spec = pl.BlockSpec((None, tm, tn), lambda i, j, k: (j // q, i, j % q))
    return _matmul(
        name, a, b, dims=_TN, grid_mnk=(Mo // tm, No // tn, S // tk), tiles=(tm, tn),
        a_spec=pl.BlockSpec((tk, tm), lambda i, j, k: (k, i + a_off)), b_spec=b_spec,
        out_shapes=[out_shape], out_specs=[out_spec], epilogue=_store(BF16), after=after)[0]


def _norm_mod(name, x, g, sc, sh):
    S, D = x.shape
    tm = _tile(S, 256, 8)

    def body(x_ref, g_ref, sc_ref, sh_ref, h_ref):
        xv = x_ref[...]
        r = lax.rsqrt(jnp.mean(xv * xv, axis=-1, keepdims=True) + EPS)
        h = (xv * r) * g_ref[...]
        h_ref[...] = (h * (1.0 + sc_ref[...]) + sh_ref[...]).astype(BF16)

    row = pl.BlockSpec((tm, D), lambda i: (i, 0))
    vec = pl.BlockSpec((1, D), lambda i: (0, 0))
    return pl.pallas_call(body, name=name, grid=(S // tm,), in_specs=[row, vec, vec, vec], out_specs=row,
                          out_shape=jax.ShapeDtypeStruct((S, D), BF16), compiler_params=_cparams(("parallel",)))(x, g, sc, sh)


def _norm_mod_bwd_rows(first, dh_v, x_ref, g_ref, sc_ref, dres_ref, dx_ref, vec_ref, o_ref=None, gt_ref=None, do_ref=None):
    @pl.when(first)
    def _():
        vec_ref[...] = jnp.zeros_like(vec_ref)

    xv, gv = x_ref[...], g_ref[...]
    r = lax.rsqrt(jnp.mean(xv * xv, axis=-1, keepdims=True) + EPS)
    xn = xv * r
    one_sc = 1.0 + sc_ref[...]
    vec_ref[0:1, :] += jnp.sum(dh_v, axis=0, keepdims=True)
    vec_ref[1:2, :] += jnp.sum(dh_v * (xn * gv), axis=0, keepdims=True)
    vec_ref[2:3, :] += jnp.sum(dh_v * one_sc * xn, axis=0, keepdims=True)
    dxn = dh_v * one_sc * gv
    dx = dres_ref[...] + r * (dxn - xn * jnp.mean(dxn * xn, axis=-1, keepdims=True))
    dx_ref[...] = dx
    if o_ref is not None:
        vec_ref[3:4, :] += jnp.sum(dx * o_ref[...], axis=0, keepdims=True)
        do_ref[...] = (dx * gt_ref[...]).astype(BF16)


def _norm_mod_bwd(name, dh, x, g, sc, dres, o=None, gt=None):
    S, D = x.shape
    tm = _tile(S, 256, 8)
    gated = o is not None

    def body(*refs):
        if gated:
            dh_ref, x_ref, g_ref, sc_ref, dres_ref, o_ref, gt_ref, dx_ref, vec_ref, do_ref = refs
        else:
            dh_ref, x_ref, g_ref, sc_ref, dres_ref, dx_ref, vec_ref = refs
            o_ref = gt_ref = do_ref = None
        _norm_mod_bwd_rows(pl.program_id(0) == 0, dh_ref[...], x_ref, g_ref, sc_ref, dres_ref, dx_ref, vec_ref, o_ref, gt_ref, do_ref)

    row = pl.BlockSpec((tm, D), lambda i: (i, 0))
    vec = pl.BlockSpec((1, D), lambda i: (0, 0))
    acc = pl.BlockSpec((8, D), lambda i: (0, 0))
    ins = [dh, x, g, sc, dres] + ([o, gt] if gated else [])
    in_specs = [row, row, vec, vec, row] + ([row, vec] if gated else [])
    out_shape = [jax.ShapeDtypeStruct((S, D), F32), jax.ShapeDtypeStruct((8, D), F32)]
    out_specs = [row, acc]
    if gated:
        out_shape.append(jax.ShapeDtypeStruct((S, D), BF16))
        out_specs.append(row)
    return pl.pallas_call(body, name=name, grid=(S // tm,), in_specs=in_specs, out_specs=out_specs, out_shape=out_shape,
                          compiler_params=_cparams(("arbitrary",)))(*ins)


def _loss_head(x3, tgt, gf, o2, gt2):
    S, D = x3.shape
    tm = _tile(S, 256, 8)

    def body(x_ref, t_ref, g_ref, o_ref, gt_ref, dx_ref, do_ref, vec_ref):
        i = pl.program_id(0)

        @pl.when(i == 0)
        def _():
            vec_ref[...] = jnp.zeros_like(vec_ref)

        xv, gv = x_ref[...], g_ref[...]
        r = lax.rsqrt(jnp.mean(xv * xv, axis=-1, keepdims=True) + EPS)
        xn = xv * r
        e = xn * gv - t_ref[...]
        tok = 0.5 * jnp.mean(e * e, axis=-1, keepdims=True)
        vec_ref[0:1, :] += jnp.broadcast_to(jnp.sum(tok, axis=0, keepdims=True), (1, D))
        dy = e * (1.0 / D)
        vec_ref[1:2, :] += jnp.sum(dy * xn, axis=0, keepdims=True)
        dxn = dy * gv
        dx = r * (dxn - xn * jnp.mean(dxn * xn, axis=-1, keepdims=True))
        dx_ref[...] = dx
        vec_ref[2:3, :] += jnp.sum(dx * o_ref[...], axis=0, keepdims=True)
        do_ref[...] = (dx * gt_ref[...]).astype(BF16)

    row = pl.BlockSpec((tm, D), lambda i: (i, 0))
    vec = pl.BlockSpec((1, D), lambda i: (0, 0))
    return pl.pallas_call(
        body, name="loss_head", grid=(S // tm,), in_specs=[row, row, vec, row, vec],
        out_specs=[row, row, pl.BlockSpec((8, D), lambda i: (0, 0))],
        out_shape=[jax.ShapeDtypeStruct((S, D), F32), jax.ShapeDtypeStruct((S, D), BF16), jax.ShapeDtypeStruct((8, D), F32)],
        compiler_params=_cparams(("arbitrary",)))(x3, tgt, gf, o2, gt2)


def _ffn_in_swiglu(h, wg):
    S, D = h.shape
    _, _, tf = wg.shape
    nf = N_DEV // 2
    F = nf * tf
    tm = _tile(S, 256, 16)

    def body(h_ref, wa_ref, wu_ref, hf_ref, fac_ref):
        hv = h_ref[...]
        a = lax.dot_general(hv, wa_ref[...], _NN, preferred_element_type=F32)
        up = lax.dot_general(hv, wu_ref[...], _NN, preferred_element_type=F32)
        sa = _sigmoid(a)
        silu = a * sa
        hf_ref[...] = (silu * up).astype(BF16)
        fac_ref[0] = (up * (sa * (1.0 + a * (1.0 - sa)))).astype(BF16)
        fac_ref[1] = silu.astype(BF16)

    return pl.pallas_call(
        body, name="ffn_in_swiglu", grid=(nf, S // tm),
        in_specs=[pl.BlockSpec((tm, D), lambda j, i: (i, 0)), pl.BlockSpec((None, D, tf), lambda j, i: (j, 0, 0)),
                  pl.BlockSpec((None, D, tf), lambda j, i: (j + nf, 0, 0))],
        out_specs=[pl.BlockSpec((tm, tf), lambda j, i: (i, j)), pl.BlockSpec((2, tm, tf), lambda j, i: (0, i, j))],
        out_shape=[jax.ShapeDtypeStruct((S, F), BF16), jax.ShapeDtypeStruct((2, S, F), BF16)],
        compiler_params=_cparams(("parallel", "parallel")))(h, wg, wg)


def _gmlp_common(u_ref, v_ref, lg_ref, lb_ref, ws_ref, bsb_ref, G, T, Dg):
    ug, dug = _gelu_parts(u_ref[...])
    vg, dvg = _gelu_parts(v_ref[...])
    mu = jnp.mean(vg, axis=-1, keepdims=True)
    vc = vg - mu
    rstd = lax.rsqrt(jnp.mean(vc * vc, axis=-1, keepdims=True) + EPS)
    vhat = vc * rstd
    vn = vhat * lg_ref[...] + lb_ref[...]
    row = lax.broadcasted_iota(jnp.int32, (T, T), 0)
    col = lax.broadcasted_iota(jnp.int32, (T, T), 1)
    tril = row >= col
    s = []
    for g in range(G):
        w = jnp.where(tril, ws_ref[g], 0.0)
        s.append(_dot(w, vn[:, g * Dg:(g + 1) * Dg]) + bsb_ref[g])
    return ug, dug, dvg, rstd, vhat, vn, tril, s


def _gmlp_fwd(z, ln_g, ln_b, ws, bsb, GW):
    S = z.shape[0]
    G, T, _ = ws.shape
    Dg = GW // G

    def body(u_ref, v_ref, lg_ref, lb_ref, ws_ref, bsb_ref, ya_ref):
        ug, _, _, _, _, _, _, s = _gmlp_common(u_ref, v_ref, lg_ref, lb_ref, ws_ref, bsb_ref, G, T, Dg)
        for g in range(G):
            sl = slice(g * Dg, (g + 1) * Dg)
            ya_ref[:, sl] = (ug[:, sl] * s[g]).astype(BF16)

    vec = pl.BlockSpec((1, GW), lambda c: (0, 0))
    return pl.pallas_call(
        body, name="gmlp_fwd", grid=(S // T,),
        in_specs=[pl.BlockSpec((T, GW), lambda c: (c, 0)), pl.BlockSpec((T, GW), lambda c: (c, 1)), vec, vec,
                  pl.BlockSpec((G, T, T), lambda c: (0, 0, 0)), pl.BlockSpec((G, T, Dg), lambda c: (0, 0, 0))],
        out_specs=pl.BlockSpec((T, GW), lambda c: (c, 0)), out_shape=jax.ShapeDtypeStruct((S, GW), BF16),
        compiler_params=_cparams(("parallel",)))(z, z, ln_g, ln_b, ws, bsb)


def _gmlp_bwd(z, dya, ln_g, ln_b, ws, bsb, GW):
    S = z.shape[0]
    G, T, _ = ws.shape
    Dg = GW // G
    nc = S // T

    def body(u_ref, v_ref, dya_ref, lg_ref, lb_ref, ws_ref, bsb_ref, dz_ref, dln_ref, dws_ref, dbs_ref, dbs_acc, dvh):
        c = pl.program_id(0)

        @pl.when(c == 0)
        def _():
            dln_ref[...] = jnp.zeros_like(dln_ref)
            dws_ref[...] = jnp.zeros_like(dws_ref)
            dbs_acc[...] = jnp.zeros_like(dbs_acc)

        ug, dug, dvg, rstd, vhat, vn, tril, s = _gmlp_common(u_ref, v_ref, lg_ref, lb_ref, ws_ref, bsb_ref, G, T, Dg)
        dya_v = dya_ref[...]
        for g in range(G):
            sl = slice(g * Dg, (g + 1) * Dg)
            dy_g = dya_v[:, sl]
            dz_ref[:, sl] = (dy_g * s[g] * dug[:, sl]).astype(BF16)
            ds = dy_g * ug[:, sl]
            dbs_acc[g] += ds
            w = jnp.where(tril, ws_ref[g], 0.0)
            dvn_g = _dot(w, ds, _TN)
            dws_ref[g] += jnp.where(tril, _dot(ds, vn[:, sl], _NT), 0.0)
            dln_ref[0:1, sl] += jnp.sum(dvn_g * vhat[:, sl], axis=0, keepdims=True)
            dln_ref[1:2, sl] += jnp.sum(dvn_g, axis=0, keepdims=True)
            dvh[:, sl] = dvn_g * lg_ref[:, sl]
        dvhat = dvh[...]
        m1 = jnp.mean(dvhat, axis=-1, keepdims=True)
        m2 = jnp.mean(dvhat * vhat, axis=-1, keepdims=True)
        dz_ref[:, GW:2 * GW] = (rstd * (dvhat - m1 - vhat * m2) * dvg).astype(BF16)

        @pl.when(c == nc - 1)
        def _():
            for g in range(G):
                dbs_ref[g] = jnp.sum(dbs_acc[g], axis=-1, keepdims=True)

    vec = pl.BlockSpec((1, GW), lambda c: (0, 0))
    return pl.pallas_call(
        body, name="gmlp_bwd", grid=(nc,),
        in_specs=[pl.BlockSpec((T, GW), lambda c: (c, 0)), pl.BlockSpec((T, GW), lambda c: (c, 1)),
                  pl.BlockSpec((T, GW), lambda c: (c, 0)), vec, vec,
                  pl.BlockSpec((G, T, T), lambda c: (0, 0, 0)), pl.BlockSpec((G, T, Dg), lambda c: (0, 0, 0))],
        out_specs=[pl.BlockSpec((T, 2 * GW), lambda c: (c, 0)), pl.BlockSpec((8, GW), lambda c: (0, 0)),
                   pl.BlockSpec((G, T, T), lambda c: (0, 0, 0)), pl.BlockSpec((G, T, 1), lambda c: (0, 0, 0))],
        out_shape=[jax.ShapeDtypeStruct((S, 2 * GW), BF16), jax.ShapeDtypeStruct((8, GW), F32),
                   jax.ShapeDtypeStruct((G, T, T), F32), jax.ShapeDtypeStruct((G, T, 1), F32)],
        scratch_shapes=[pltpu.VMEM((G, T, Dg), F32), pltpu.VMEM((T, GW), F32)],
        compiler_params=_cparams(("arbitrary",)))(z, z, dya, ln_g, ln_b, ws, bsb)


def _hg_common(q_ref, f_ref, hlb_ref):
    C = HG_CHUNK
    a = hlb_ref[...]
    lb = _sigmoid(a[0:1, :] - a[1:2, :])
    sig = _sigmoid(f_ref[...])
    f = lb + (1.0 - lb) * sig
    lf = jnp.log(f)
    kk = 1.0 - f
    q = q_ref[...]
    sq = _sigmoid(q)
    qa = q * sq
    row = lax.broadcasted_iota(jnp.int32, (C, C), 0)
    col = lax.broadcasted_iota(jnp.int32, (C, C), 1)
    tril = row >= col
    b = _ones_dot(tril.astype(BF16), lf)
    bm = b[HG_MID:HG_MID + 1, :]
    bl = b[C - 1:C, :]
    e_b = jnp.exp(b)
    e_qm = jnp.exp(jnp.minimum(b - bm, EXP_CLAMP))
    e_km = jnp.exp(jnp.minimum(bm - b, EXP_CLAMP))
    e_kl = jnp.exp(bl - b)
    return dict(lb=lb, sig=sig, f=f, kk=kk, q=q, sq=sq, qa=qa, tril=tril, e_b=e_b, e_qm=e_qm, e_km=e_km, e_kl=e_kl,
                e_l=jnp.exp(bl), qh=qa * e_b, qt=qa * e_qm, kt=kk * e_km, kh=kk * e_kl)


def _hg_fwd(z, hg_lb, ng, HW):
    S = z.shape[0]
    C, H, dk = HG_CHUNK, HW // HG_DK, HG_DK
    nc = S // C

    def body(q_ref, f_ref, i_ref, og_ref, hlb_ref, ng_ref, yb_ref, o_ref, st_ref, state):
        @pl.when(pl.program_id(0) == 0)
        def _():
            state[...] = jnp.zeros_like(state)

        t = _hg_common(q_ref, f_ref, hlb_ref)
        iv = i_ref[...]
        for h in range(H):
            sl = slice(h * dk, (h + 1) * dk)
            st = state[h]
            st_ref[h] = st
            a = jnp.where(t["tril"], _dot(t["qt"][:, sl], t["kt"][:, sl], _NT), 0.0)
            o_h = _dot(a, iv[:, sl]) + _dot(t["qh"][:, sl], st, _NT)
            state[h] = st * t["e_l"][:, sl] + _dot(iv[:, sl], t["kh"][:, sl], _TN)
            o_ref[:, sl] = o_h
            rr = lax.rsqrt(jnp.mean(o_h * o_h, axis=-1, keepdims=True) + EPS)
            og = og_ref[:, sl]
            yb_ref[:, sl] = (o_h * rr * ng_ref[:, sl] * (og * _sigmoid(og))).astype(BF16)

    def col(k):
        return pl.BlockSpec((C, HW), lambda c: (c, k))

    base = 2
    return pl.pallas_call(
        body, name="hgrn_fwd", grid=(nc,),
        in_specs=[col(base), col(base + 1), col(base + 2), col(base + 3),
                  pl.BlockSpec((2, HW), lambda c: (0, 0)), pl.BlockSpec((1, HW), lambda c: (0, 0))],
        out_specs=[pl.BlockSpec((C, HW), lambda c: (c, 0)), pl.BlockSpec((C, HW), lambda c: (c, 0)),
                   pl.BlockSpec((None, H, dk, dk), lambda c: (c, 0, 0, 0))],
        out_shape=[jax.ShapeDtypeStruct((S, HW), BF16), jax.ShapeDtypeStruct((S, HW), F32),
                   jax.ShapeDtypeStruct((nc, H, dk, dk), F32)],
        scratch_shapes=[pltpu.VMEM((H, dk, dk), F32)],
        compiler_params=_cparams(("arbitrary",)))(z, z, z, z, hg_lb, ng)


def _hg_bwd(z, o, states, dyb, hg_lb, ng, HW, dz_head, dz_tail):
    S = z.shape[0]
    C, H, dk = HG_CHUNK, HW // HG_DK, HG_DK
    nc = S // C
    B0 = dz_head.shape[1]
    DT = dz_tail.shape[2]
    INW = B0 + 4 * HW + 2 * DT

    def body(q_ref, f_ref, i_ref, og_ref, o_ref, st_ref, stn_ref, dyb_ref, hlb_ref, ng_ref, head_ref, tail_ref,
             dzf_ref, dng_ref, dhlb_ref, dtail_ref, dstate, cross, dqa_buf, dkk_buf, db_buf, dlb_acc):
        c = pl.program_id(0)
        dzf_ref[:, 0:B0] = head_ref[...]
        dzf_ref[:, B0 + 4 * HW:B0 + 4 * HW + DT] = tail_ref[0]
        dzf_ref[:, B0 + 4 * HW + DT:INW] = tail_ref[1]
        dz_ref = dzf_ref.at[:, B0:B0 + 4 * HW]

        @pl.when(c == 0)
        def _():
            dtail_ref[...] = jnp.zeros_like(dtail_ref)

        dtail_ref[0:1, :] += jnp.sum(tail_ref[0].astype(F32), axis=0, keepdims=True)
        dtail_ref[1:2, :] += jnp.sum(tail_ref[1].astype(F32), axis=0, keepdims=True)

        @pl.when(c == 0)
        def _():
            dstate[...] = jnp.zeros_like(dstate)
            dlb_acc[...] = jnp.zeros_like(dlb_acc)
            dng_ref[...] = jnp.zeros_like(dng_ref)

        def r16(v):
            return v.astype(BF16).astype(F32)

        t = _hg_common(q_ref, f_ref, hlb_ref)
        iv = i_ref[...]
        for h in range(H):
            sl = slice(h * dk, (h + 1) * dk)
            o_h, og, dyb_h, ng_h = o_ref[:, sl], og_ref[:, sl], dyb_ref[:, sl], ng_ref[:, sl]
            sg = _sigmoid(og)
            silu_og = og * sg
            rr = lax.rsqrt(jnp.mean(o_h * o_h, axis=-1, keepdims=True) + EPS)
            on = o_h * rr
            dng_ref[0:1, sl] += jnp.sum(dyb_h * on * silu_og, axis=0, keepdims=True)
            dz_ref[:, 3 * HW + h * dk:3 * HW + (h + 1) * dk] = (dyb_h * on * ng_h * (sg * (1.0 + og * (1.0 - sg)))).astype(BF16)
            don = dyb_h * ng_h * silu_og
            do_h = rr * (don - on * jnp.mean(don * on, axis=-1, keepdims=True))

            qt, kt, qh, kh, iv_h = t["qt"][:, sl], t["kt"][:, sl], t["qh"][:, sl], t["kh"][:, sl], iv[:, sl]
            a = jnp.where(t["tril"], _dot(qt, kt, _NT), 0.0)
            da = jnp.where(t["tril"], _dot(do_h, iv_h, _NT), 0.0)
            st, dst = st_ref[h], dstate[h]
            cross[:, sl] = jnp.sum(dst * stn_ref[h], axis=0, keepdims=True)
            dqh = _dot(do_h, st)
            dstate[h] = _dot(do_h, qh, _TN) + dst * t["e_l"][:, sl]
            div = _dot(a, do_h, _TN) + _dot(kh, dst, _NT)
            dkh = _dot(iv_h, dst)
            dqt = _dot(da, kt)
            dkt = _dot(da, qt, _TN)
            dz_ref[:, 2 * HW + h * dk:2 * HW + (h + 1) * dk] = div.astype(BF16)
            dqa_buf[:, sl] = dqh * t["e_b"][:, sl] + dqt * t["e_qm"][:, sl]
            dkk_buf[:, sl] = dkt * t["e_km"][:, sl] + dkh * t["e_kl"][:, sl]
            db_buf[:, sl] = r16(qt) * dqt - r16(kt) * dkt + r16(qh) * dqh - r16(kh) * dkh

        dqa, dkk = dqa_buf[...], dkk_buf[...]
        triu = jnp.logical_not(t["tril"]) | (lax.broadcasted_iota(jnp.int32, (C, C), 0) == lax.broadcasted_iota(jnp.int32, (C, C), 1))
        dlf = _ones_dot(triu.astype(BF16), db_buf[...]) + cross[...]
        df = dlf / t["f"] - dkk
        sig, lb = t["sig"], t["lb"]
        dz_ref[:, HW:2 * HW] = (df * (1.0 - lb) * sig * (1.0 - sig)).astype(BF16)
        dlb_acc[...] += jnp.sum(df * (1.0 - sig), axis=0, keepdims=True)
        q, sq = t["q"], t["sq"]
        dz_ref[:, 0:HW] = (dqa * (sq * (1.0 + q * (1.0 - sq)))).astype(BF16)

        @pl.when(c == nc - 1)
        def _():
            da0 = dlb_acc[...] * lb * (1.0 - lb)
            dhlb_ref[0:1, :] = da0
            dhlb_ref[1:2, :] = -da0

    def col(k):
        return pl.BlockSpec((C, HW), lambda c: (nc - 1 - c, k))

    base = 2
    return pl.pallas_call(
        body, name="hgrn_bwd", grid=(nc,),
        in_specs=[col(base), col(base + 1), col(base + 2), col(base + 3), col(0),
                  pl.BlockSpec((None, H, dk, dk), lambda c: (nc - 1 - c, 0, 0, 0)),
                  pl.BlockSpec((None, H, dk, dk), lambda c: (jnp.minimum(nc - c, nc - 1), 0, 0, 0)), col(0),
                  pl.BlockSpec((2, HW), lambda c: (0, 0)), pl.BlockSpec((1, HW), lambda c: (0, 0)),
                  pl.BlockSpec((C, B0), lambda c: (nc - 1 - c, 0)), pl.BlockSpec((2, C, DT), lambda c: (0, nc - 1 - c, 0))],
        out_specs=[pl.BlockSpec((C, INW), lambda c: (nc - 1 - c, 0)), pl.BlockSpec((8, HW), lambda c: (0, 0)),
                   pl.BlockSpec((2, HW), lambda c: (0, 0)), pl.BlockSpec((2, DT), lambda c: (0, 0))],
        out_shape=[jax.ShapeDtypeStruct((S, INW), BF16), jax.ShapeDtypeStruct((8, HW), F32), jax.ShapeDtypeStruct((2, HW), F32),
                   jax.ShapeDtypeStruct((2, DT), F32)],
        scratch_shapes=[pltpu.VMEM((H, dk, dk), F32), pltpu.VMEM((1, HW), F32), pltpu.VMEM((C, HW), F32), pltpu.VMEM((C, HW), F32),
                        pltpu.VMEM((C, HW), F32), pltpu.VMEM((1, HW), F32)],
        compiler_params=_cparams(("arbitrary",)))(z, z, z, z, o, states, states, dyb, hg_lb, ng, dz_head, dz_tail)


def _position():
    x, y, c = lax.axis_index("x"), lax.axis_index("y"), lax.axis_index("c")
    return x, y, c, 4 * x + 2 * y + c


def _flip(x, y, c, k):
    return (1 - x if k & 4 else x, 1 - y if k & 2 else y, 1 - c if k & 1 else c)


def _allgather_small(name, v):
    R, L = v.shape

    def body(v_ref, out_ref, send_sems, recv_sems):
        x, y, c, me = _position()
        out_ref[me] = v_ref[...]
        copies = []
        for k in range(1, N_DEV):
            cp = pltpu.make_async_remote_copy(src_ref=v_ref, dst_ref=out_ref.at[me], send_sem=send_sems.at[k - 1],
                                              recv_sem=recv_sems.at[k - 1], device_id=_flip(x, y, c, k), device_id_type=MESH)
            cp.start()
            copies.append(cp)
        for cp in copies:
            cp.wait()

    return pl.pallas_call(
        body, name=name, out_shape=jax.ShapeDtypeStruct((N_DEV, R, L), v.dtype),
        in_specs=[pl.BlockSpec(memory_space=pltpu.VMEM)], out_specs=pl.BlockSpec(memory_space=pltpu.VMEM),
        scratch_shapes=[pltpu.SemaphoreType.DMA((N_DEV - 1,)), pltpu.SemaphoreType.DMA((N_DEV - 1,))],
        compiler_params=pltpu.CompilerParams(vmem_limit_bytes=VMEM_LIMIT),
    )(v)


_HBM = pl.BlockSpec(memory_space=pltpu.HBM)
_SEM = pl.BlockSpec(memory_space=pltpu.SEMAPHORE)
_EFFECT = pltpu.SideEffectType.DATAFLOW_SIDE_EFFECTING


def _split_start(name, bufs, n_sems, copies_fn, after=None):
    nb = len(bufs)
    extra = [] if after is None else [after]
    k = nb + len(extra)

    def body(*refs):
        for cp in copies_fn(refs[:nb], refs[k], refs[k + 1]):
            cp.start()
        refs[-1][...] = jnp.zeros_like(refs[-1])

    sems = pltpu.SemaphoreType.DMA((n_sems,))
    res = pl.pallas_call(
        body, name=name,
        out_shape=(sems, sems, *[pltpu.HBM(a.shape, a.dtype) for a in bufs], jax.ShapeDtypeStruct((8, LANES), F32)),
        in_specs=[_HBM] * nb + [pl.BlockSpec(memory_space=pl.ANY)] * len(extra),
        out_specs=(_SEM, _SEM, *[_HBM] * nb, pl.BlockSpec(memory_space=pltpu.VMEM)),
        input_output_aliases={i: 2 + i for i in range(nb)},
        compiler_params=pltpu.CompilerParams(has_side_effects=_EFFECT),
    )(*[pltpu.with_memory_space_constraint(a, pltpu.HBM) for a in bufs], *extra)
    return res[0], res[1], list(res[2:2 + nb]), res[-1]


def _split_wait(name, bufs, send_sems, recv_sems, after, copies_fn):
    nb = len(bufs)

    def body(*refs):
        for cp in copies_fn(refs[:nb], refs[nb], refs[nb + 1]):
            cp.wait_send()
            cp.wait_recv()

    res = pl.pallas_call(
        body, name=name, out_shape=tuple(pltpu.HBM(a.shape, a.dtype) for a in bufs),
        in_specs=[_HBM] * nb + [_SEM, _SEM, pl.BlockSpec(memory_space=pl.ANY)], out_specs=tuple([_HBM] * nb),
        input_output_aliases={i: i for i in range(nb)},
        compiler_params=pltpu.CompilerParams(has_side_effects=_EFFECT),
    )(*bufs, send_sems, recv_sems, after)
    return list(res)


def _split_relay(name, bufs, send_sems, recv_sems, after, wait_fn, n_sems, start_fn):
    nb = len(bufs)

    def body(*refs):
        for cp in wait_fn(refs[:nb], refs[nb], refs[nb + 1]):
            cp.wait_send()
            cp.wait_recv()
        for cp in start_fn(refs[:nb], refs[nb + 3], refs[nb + 4]):
            cp.start()
        refs[-1][...] = jnp.zeros_like(refs[-1])

    sems = pltpu.SemaphoreType.DMA((n_sems,))
    res = pl.pallas_call(
        body, name=name, out_shape=(sems, sems, *[pltpu.HBM(a.shape, a.dtype) for a in bufs], jax.ShapeDtypeStruct((8, LANES), F32)),
        in_specs=[_HBM] * nb + [_SEM, _SEM, pl.BlockSpec(memory_space=pl.ANY)],
        out_specs=(_SEM, _SEM, *[_HBM] * nb, pl.BlockSpec(memory_space=pltpu.VMEM)),
        input_output_aliases={i: 2 + i for i in range(nb)},
        compiler_params=pltpu.CompilerParams(has_side_effects=_EFFECT),
    )(*bufs, send_sems, recv_sems, after)
    return res[0], res[1], list(res[2:2 + nb]), res[-1]


N_CHIP = 4


def _chip_flip(x, y, k):
    return (1 - x if k & 2 else x), (1 - y if k & 1 else y)


def _gather_first_copies(n):
    def copies(bufs, send_sems, recv_sems):
        x, y, c, me = _position()
        out = []
        for w in range(n):
            for k in range(N_CHIP):
                to = (x, y, 1 - c) if k == 0 else (*_chip_flip(x, y, k), c)
                out.append(pltpu.make_async_remote_copy(
                    src_ref=bufs[w], dst_ref=bufs[n + w].at[me], send_sem=send_sems.at[w * N_CHIP + k],
                    recv_sem=recv_sems.at[w * N_CHIP + k], device_id=to, device_id_type=MESH))
        return out
    return copies


def _gather_relay_copies(n):
    def copies(bufs, send_sems, recv_sems):
        x, y, c, _ = _position()
        out = []
        for w in range(n):
            for k in range(1, N_CHIP):
                px, py = _chip_flip(x, y, k)
                blk = bufs[n + w].at[4 * px + 2 * py + c]
                out.append(pltpu.make_async_remote_copy(
                    src_ref=blk, dst_ref=blk, send_sem=send_sems.at[w * (N_CHIP - 1) + k - 1],
                    recv_sem=recv_sems.at[w * (N_CHIP - 1) + k - 1], device_id=(x, y, 1 - c), device_id_type=MESH))
        return out
    return copies


def _small_gather_copies(bufs, send_sems, recv_sems):
    x, y, c, me = _position()
    return [pltpu.make_async_remote_copy(src_ref=bufs[0], dst_ref=bufs[1].at[me], send_sem=send_sems.at[k - 1], recv_sem=recv_sems.at[k - 1],
                                         device_id=_flip(x, y, c, k), device_id_type=MESH) for k in range(1, N_DEV)]


def _forward_first_copies(n):
    def copies(bufs, send_sems, recv_sems):
        x, y, c, me = _position()
        out = []
        for w in range(n):
            for k, to in enumerate([(x, y, 1 - c), (1 - x, y, c), (x, 1 - y, c)]):
                out.append(pltpu.make_async_remote_copy(
                    src_ref=bufs[w], dst_ref=bufs[n + w].at[me], send_sem=send_sems.at[w * 3 + k],
                    recv_sem=recv_sems.at[w * 3 + k], device_id=to, device_id_type=MESH))
        return out
    return copies


def _forward_second_copies(n):
    def copies(bufs, send_sems, recv_sems):
        x, y, c, _ = _position()
        out = []
        for w in range(n):
            half = bufs[n + w].shape[1] // 2
            for k, (src_chip, rows, to) in enumerate([((1 - x, y), pl.ds(0, half), (x, 1 - y, c)), ((x, 1 - y), pl.ds(half, half), (1 - x, y, c))]):
                blk = bufs[n + w].at[4 * src_chip[0] + 2 * src_chip[1] + c, rows]
                out.append(pltpu.make_async_remote_copy(src_ref=blk, dst_ref=blk, send_sem=send_sems.at[w * 4 + k],
                                                        recv_sem=recv_sems.at[w * 4 + k], device_id=to, device_id_type=MESH))
            for k, (px, py) in enumerate([(1 - x, y), (x, 1 - y)]):
                blk = bufs[n + w].at[4 * px + 2 * py + c]
                out.append(pltpu.make_async_remote_copy(src_ref=blk, dst_ref=blk, send_sem=send_sems.at[w * 4 + 2 + k],
                                                        recv_sem=recv_sems.at[w * 4 + 2 + k], device_id=(x, y, 1 - c), device_id_type=MESH))
        return out
    return copies


def _forward_third_copies(n):
    def copies(bufs, send_sems, recv_sems):
        x, y, c, _ = _position()
        out = []
        for w in range(n):
            blk = bufs[n + w].at[4 * (1 - x) + 2 * (1 - y) + c]
            out.append(pltpu.make_async_remote_copy(src_ref=blk, dst_ref=blk, send_sem=send_sems.at[w], recv_sem=recv_sems.at[w],
                                                    device_id=(x, y, 1 - c), device_id_type=MESH))
        return out
    return copies


def _to_sibling_copies(n):
    def copies(bufs, send_sems, recv_sems):
        x, y, c, _ = _position()
        out = []
        for w in range(n):
            for q in range(N_CHIP):
                out.append(pltpu.make_async_remote_copy(
                    src_ref=bufs[w].at[2 * q + 1 - c], dst_ref=bufs[n + w].at[q], send_sem=send_sems.at[w * N_CHIP + q],
                    recv_sem=recv_sems.at[w * N_CHIP + q], device_id=(x, y, 1 - c), device_id_type=MESH))
        return out
    return copies


def _to_owner_copies(n):
    def copies(bufs, send_sems, recv_sems):
        x, y, c, _ = _position()
        out = []
        for w in range(n):
            for k in range(1, N_CHIP):
                px, py = (1 - x if k & 2 else x), (1 - y if k & 1 else y)
                out.append(pltpu.make_async_remote_copy(
                    src_ref=bufs[w].at[2 * px + py], dst_ref=bufs[n + w].at[k - 1], send_sem=send_sems.at[w * (N_CHIP - 1) + k - 1],
                    recv_sem=recv_sems.at[w * (N_CHIP - 1) + k - 1], device_id=(px, py, c), device_id_type=MESH))
        return out
    return copies


def _chip_sum(name, stack, landed, c_idx):
    _, R, C = stack.shape
    tr = _tile(R, max(BF16_ROWS, STREAM_TILE // C), BF16_ROWS)

    def body(c_ref, a_ref, b_ref, o_ref):
        o_ref[...] = (a_ref[...].astype(F32) + b_ref[...].astype(F32)).astype(o_ref.dtype)

    return pl.pallas_call(
        body, name=name,
        grid_spec=pltpu.PrefetchScalarGridSpec(
            num_scalar_prefetch=1, grid=(N_CHIP, R // tr),
            in_specs=[pl.BlockSpec((None, tr, C), lambda q, i, c_ref: (2 * q + c_ref[0], i, 0)),
                      pl.BlockSpec((None, tr, C), lambda q, i, c_ref: (q, i, 0))],
            out_specs=pl.BlockSpec((None, tr, C), lambda q, i, c_ref: (q, i, 0))),
        out_shape=jax.ShapeDtypeStruct((N_CHIP, R, C), stack.dtype),
        compiler_params=_cparams(("parallel", "parallel")))(c_idx, stack, landed)


def _ada_mod(c16, w):
    _, D = c16.shape
    n = w.shape[1]
    tk = _tile(D, 512)
    nk = D // tk

    def body(c_ref, w_ref, o_ref, ca_ref):
        @pl.when(pl.program_id(0) == 0)
        def _():
            o_ref[...] = jnp.zeros_like(o_ref)

        cv = c_ref[...]
        ca = cv * _sigmoid(cv)
        ca_ref[...] = ca
        o_ref[...] += _dot(ca, w_ref[...])

    return pl.pallas_call(
        body, name="ada_mod", grid=(nk,),
        in_specs=[pl.BlockSpec((16, tk), lambda k: (0, k)), pl.BlockSpec((tk, n), lambda k: (k, 0))],
        out_specs=[pl.BlockSpec((16, n), lambda k: (0, 0)), pl.BlockSpec((16, tk), lambda k: (0, k))],
        out_shape=[jax.ShapeDtypeStruct((16, n), F32), jax.ShapeDtypeStruct((16, D), F32)],
        compiler_params=_cparams(("arbitrary",)))(c16, w)


def _cast_shard(name, wf, slot):
    r, c = wf.shape
    tr = _tile(r, max(BF16_ROWS, STREAM_TILE // c), BF16_ROWS)

    def body(slot_ref, w_ref, s_ref, g_ref):
        v = w_ref[...].astype(BF16)
        s_ref[...] = v
        g_ref[...] = v

    return pl.pallas_call(
        body, name=name,
        grid_spec=pltpu.PrefetchScalarGridSpec(
            num_scalar_prefetch=1, grid=(r // tr,), in_specs=[pl.BlockSpec((tr, c), lambda i, s: (i, 0))],
            out_specs=[pl.BlockSpec((tr, c), lambda i, s: (i, 0)), pl.BlockSpec((None, tr, c), lambda i, s: (s[0], i, 0))]),
        out_shape=[jax.ShapeDtypeStruct((r, c), BF16), jax.ShapeDtypeStruct((N_DEV, r, c), BF16)],
        compiler_params=_cparams(("parallel",)))(slot, wf)


def _adam_math(w, g, m, v):
    m2 = ADAM_B1 * m + (1.0 - ADAM_B1) * g
    v2 = ADAM_B2 * v + (1.0 - ADAM_B2) * (g * g)
    m_hat = m2 / (1.0 - ADAM_B1 ** ADAM_STEP)
    v_hat = v2 / (1.0 - ADAM_B2 ** ADAM_STEP)
    delta = -ADAM_LR * (m_hat / (jnp.sqrt(v_hat) + ADAM_EPS) + ADAM_WD * w)
    return delta, m2, v2


def _adamw(name, w, m, v, own, own_slot, parts=(), row0=0, into=None):
    R, C = w.shape
    Rp = own.shape[1]
    tr = _tile(Rp, max(BF16_ROWS, ADAMW_TILE // C), BF16_ROWS)
    off = row0 // tr
    n_p = len(parts)
    held = [] if into is None else list(into)

    def body(slot_ref, *refs):
        w_ref, m_ref, v_ref, own_ref = refs[:4]
        g_ref, d_ref, m2_ref, v2_ref = refs[4 + n_p + len(held):]
        g = own_ref[...].astype(F32)
        for p_ref in refs[4:4 + n_p]:
            for s in range(p_ref.shape[0]):
                g = g + p_ref[s].astype(F32)
        delta, m2, v2 = _adam_math(w_ref[...], g, m_ref[...], v_ref[...])
        g_ref[...] = g
        d_ref[...] = delta
        m2_ref[...] = m2
        v2_ref[...] = v2

    blk = pl.BlockSpec((tr, C), lambda i, s: (i + off, 0))
    out = jax.ShapeDtypeStruct((R, C), F32)
    return pl.pallas_call(
        body, name=name,
        grid_spec=pltpu.PrefetchScalarGridSpec(
            num_scalar_prefetch=1, grid=(Rp // tr,),
            in_specs=[blk, blk, blk, pl.BlockSpec((None, tr, C), lambda i, s: (s[0], i, 0))]
            + [pl.BlockSpec((a.shape[0], tr, C), lambda i, s: (0, i, 0)) for a in parts]
            + [pl.BlockSpec(memory_space=pl.ANY)] * len(held),
            out_specs=[blk] * 4),
        out_shape=[out] * 4, input_output_aliases={5 + n_p + i: i for i in range(len(held))},
        compiler_params=_cparams(("parallel",)))(own_slot, w, m, v, own, *parts, *held)


def _small_update(gathered, w, m, v, after, rows):
    _, R, L = gathered.shape
    rs = w.shape[0]
    n = len(rows)
    assert all(r % 8 == 0 for r in rows) and sum(rows) <= rs and rs + 8 <= R

    def body(p_ref, w_ref, m_ref, v_ref, after_ref, *outs):
        g = p_ref[0]
        for p in range(1, N_DEV):
            g = g + p_ref[p]
        kinds = (g,) + _adam_math(w_ref[...], g[0:rs, :], m_ref[...], v_ref[...])
        at = 0
        for k, r in enumerate(rows):
            for idx, val in enumerate(kinds):
                outs[idx * n + k][...] = val[at:at + r, :]
            at += r
        outs[4 * n][...] = g[at:at + 8, :]

    vm = pl.BlockSpec(memory_space=pltpu.VMEM)
    shapes = [jax.ShapeDtypeStruct((r, L), F32) for _ in range(4) for r in rows] + [jax.ShapeDtypeStruct((8, L), F32)]
    return pl.pallas_call(body, name="small_update", in_specs=[vm] * 4 + [pl.BlockSpec(memory_space=pl.ANY)], out_specs=[vm] * len(shapes),
                          out_shape=shapes, compiler_params=pltpu.CompilerParams(vmem_limit_bytes=VMEM_LIMIT))(gathered, w, m, v, after)


class _Fetched(dict):
    def __init__(self, fetch):
        super().__init__()
        self.fetch = fetch

    def first(self, key, after):
        self[key] = self.fetch(key, after)
        return self[key]


def _local_step(x, tgt, mod, p, fetch, F, scatter=None):
    S, D = x.shape
    GW, HW = p["ln_g"].shape[1], p["hg_ng"].shape[1]
    G, T, _ = p["ws"].shape
    w = _Fetched(fetch)
    INW = 2 * GW + 4 * HW + 2 * D
    in_loc, br_loc, fi_loc = INW // N_DEV, D // N_DEV, 2 * F // N_DEV
    assert GW == HW and F % fi_loc == 0
    sh1, sc1, gt1, sh2, sc2, gt2 = (mod[:, k * D:(k + 1) * D] for k in range(6))
    bsb = jnp.broadcast_to(p["bs"][:, :, None], (G, T, GW // G))

    tm = _tile(S, 1024, 16)
    tmh = _tile(S, 512, 16)
    tn_in = _tile(in_loc, 1280)
    tn_d = _tile(D, 512)
    tn_br = _tile(br_loc, 512)
    tk_s = S
    tm_w = _tile(D, 1024)
    g_off = 2 * GW + 4 * HW

    h1 = _norm_mod("norm1", x, p["norm1_g"], sc1, sh1)
    z = _mm_nn_stacked("proj_in", h1, w.first("in", h1), tm=tm, tn=tn_in, tk=D)[0]
    ya = _gmlp_fwd(z, p["ln_g"], p["ln_b"], p["ws"], bsb, GW)
    yb, o_hg, states = _hg_fwd(z, p["hg_lb"], p["hg_ng"], HW)
    flat = {k: jnp.swapaxes(w.first(k, yb), 0, 1).reshape(GW, D) for k in ("bg", "bh")}
    tn_f = _tile(D, 1024)
    pa = _matmul(
        "branch_gmlp", ya, flat["bg"], dims=_NN, grid_mnk=(S // tm, D // tn_f, 1), tiles=(tm, tn_f),
        a_spec=pl.BlockSpec((tm, GW), lambda i, j, k: (i, 0)), b_spec=pl.BlockSpec((GW, tn_f), lambda i, j, k: (0, j)),
        out_shapes=[jax.ShapeDtypeStruct((S, D), F32)], out_specs=[pl.BlockSpec((tm, tn_f), lambda i, j, k: (i, j))], epilogue=_store(F32))[0]
    t_fi = w.first("fi_early", pa)

    def gates(ga_ref, gb_ref, ba_ref, bb_ref):
        return _sigmoid(ga_ref[...] + ba_ref[...]), _sigmoid(gb_ref[...] + bb_ref[...])

    def gate_specs(tn_, tm_=tm):
        o1, o2 = g_off // tn_, (g_off + D) // tn_
        return [pl.BlockSpec((tm_, tn_), lambda i, j, k: (i, o1 + j)), pl.BlockSpec((tm_, tn_), lambda i, j, k: (i, o2 + j)),
                pl.BlockSpec((1, tn_), lambda i, j, k: (0, j)), pl.BlockSpec((1, tn_), lambda i, j, k: (0, D // tn_ + j))]

    def merge_ep(acc, ex, outs):
        ga, gb = gates(*ex[1:5])
        outs[0][...] = acc.astype(BF16)
        outs[1][...] = (ga * ex[0][...] + gb * acc).astype(BF16)

    tile_o = pl.BlockSpec((tmh, tn_f), lambda i, j, k: (i, j))
    pb, y = _matmul(
        "branch_hg_merge", yb, flat["bh"], dims=_NN, grid_mnk=(S // tmh, D // tn_f, 1), tiles=(tmh, tn_f),
        a_spec=pl.BlockSpec((tmh, HW), lambda i, j, k: (i, 0)), b_spec=pl.BlockSpec((HW, tn_f), lambda i, j, k: (0, j)),
        extras=[pa, z, z, p["b_gate"], p["b_gate"]], extra_specs=[tile_o, *gate_specs(tn_f, tmh)],
        out_shapes=[jax.ShapeDtypeStruct((S, D), BF16), jax.ShapeDtypeStruct((S, D), BF16)], out_specs=[tile_o, tile_o],
        epilogue=merge_ep, after=t_fi)

    def resid_ep(acc, ex, outs):
        outs[0][...] = acc.astype(BF16)
        outs[1][...] = ex[0][...] + ex[1][...] * acc

    def resid_mm(name, a, b, res, gt, tm_, tn_):
        K = a.shape[1]
        t_o = pl.BlockSpec((tm_, tn_), lambda i, j, k: (i, j))
        return _matmul(
            name, a, b, dims=_NN, grid_mnk=(S // tm_, D // tn_, 1), tiles=(tm_, tn_),
            a_spec=pl.BlockSpec((tm_, K), lambda i, j, k: (i, 0)), b_spec=pl.BlockSpec((K, tn_), lambda i, j, k: (0, j)),
            extras=[res, gt], extra_specs=[t_o, pl.BlockSpec((1, tn_), lambda i, j, k: (0, j))],
            out_shapes=[jax.ShapeDtypeStruct((S, D), BF16), jax.ShapeDtypeStruct((S, D), F32)], out_specs=[t_o, t_o], epilogue=resid_ep)

    o1, xm = resid_mm("proj_out", y, w.first("out", z), x, gt1, tm, tn_f)
    h2 = _norm_mod("norm2", xm, p["norm2_g"], sc2, sh2)
    hf, hf_fac = _ffn_in_swiglu(h2, w.first("fi", h2))
    o2, x3 = resid_mm("ffn_out", hf, w.first("fo", hf), xm, gt2, tm, _tile(D, 256))
    dx3, do2, vec_l = _loss_head(x3, tgt, p["final_g"], o2, gt2)

    nf = F // fi_loc

    def dswiglu_ep(acc, ex, outs):
        outs[0][0] = (acc * ex[0][0].astype(F32)).astype(BF16)
        outs[0][1] = (acc * ex[0][1].astype(F32)).astype(BF16)

    pair = pl.BlockSpec((2, tmh, fi_loc), lambda i, j, k: (0, j, i))
    dab = _matmul(
        "ffn_out_dx", do2, w["fo"], dims=_NT, grid_mnk=(nf, S // tmh, 1), tiles=(tmh, fi_loc),
        a_spec=pl.BlockSpec((tmh, D), lambda i, j, k: (j, 0)), b_spec=pl.BlockSpec((fi_loc, D), lambda i, j, k: (i, 0)),
        extras=[hf_fac], extra_specs=[pair], out_shapes=[jax.ShapeDtypeStruct((2, S, F), BF16)], out_specs=[pair],
        epilogue=dswiglu_ep)[0]
    start = (lambda name, grads: scatter[0](name, grads)) if scatter is not None else (lambda name, grads: None)
    push = (lambda name, after: scatter[1](name, after)) if scatter is not None else (lambda name, after: None)

    def zero(token):
        return 0.0 if token is None else token[0:1, 0:1]

    tm_f = _tile(F, 512)
    g_fo = _mm_tn("ffn_out_dw", hf, do2, pl.BlockSpec((tk_s, D), lambda i, j, k: (k, j)), Mo=F, No=D, S=S, tm=tm_f, tn=D, tk=tk_s)
    g_fi = _mm_tn("ffn_in_dw", h2, dab, pl.BlockSpec((None, tk_s, fi_loc), lambda i, j, k: (j // nf, k, j % nf)),
                  Mo=D, No=2 * F, S=S, tm=D, tn=fi_loc, tk=tk_s, stacked_nloc=fi_loc, after=g_fo)
    t_ffn = start("scatter_ffn", dict(fo=g_fo, fi=g_fi))
    dh2 = _mm_nt_stacked("ffn_in_dx", pl.BlockSpec((None, tm, fi_loc), lambda i, j, k: (k // nf, i, k % nf)), dab, w["fi"],
                         M=S, tm=tm, tn=D, tk=fi_loc, after=t_ffn)
    dxm, vec2, do1 = _norm_mod_bwd("norm2_bwd", dh2, xm, p["norm2_g"], sc2, dx3, o1, gt1)
    t_ffn = push("scatter_ffn", dxm)

    def dmerge_ep(acc, ex, outs):
        ga, gb = gates(*ex[2:6])
        outs[0][...] = (acc * ga).astype(BF16)
        outs[1][...] = (acc * gb).astype(BF16)
        outs[2][0] = (acc * ex[0][...] * ga * (1.0 - ga)).astype(BF16)
        outs[2][1] = (acc * ex[1][...] * gb * (1.0 - gb)).astype(BF16)

    t_o = pl.BlockSpec((tm, tn_d), lambda i, j, k: (i, j))
    dpa, dpb, dg2 = _matmul(
        "proj_out_dx", do1, w["out"], dims=_NT, grid_mnk=(S // tm, D // tn_d, 1), tiles=(tm, tn_d),
        a_spec=pl.BlockSpec((tm, D), lambda i, j, k: (i, 0)), b_spec=pl.BlockSpec((tn_d, D), lambda i, j, k: (j, 0)),
        extras=[pa, pb, z, z, p["b_gate"], p["b_gate"]], extra_specs=[t_o, t_o, *gate_specs(tn_d)],
        out_shapes=[jax.ShapeDtypeStruct((S, D), BF16), jax.ShapeDtypeStruct((S, D), BF16), jax.ShapeDtypeStruct((2, S, D), BF16)],
        out_specs=[t_o, t_o, pl.BlockSpec((2, tm, tn_d), lambda i, j, k: (0, i, j))], epilogue=dmerge_ep, after=t_ffn)
    g_out = _mm_tn("proj_out_dw", y, do1, pl.BlockSpec((tk_s, D), lambda i, j, k: (k, j)), Mo=D, No=D, S=S, tm=tn_d, tn=D, tk=tk_s)
    tn_g = _tile(GW, 512)
    b_br = pl.BlockSpec((tk_s, br_loc), lambda i, j, k: (k, j))
    tm_b = _tile(GW, 1024)
    g_bg = _mm_tn("branch_gmlp_dw", ya, dpa, b_br, Mo=GW, No=D, S=S, tm=tm_b, tn=br_loc, tk=tk_s, stacked_nloc=br_loc)
    g_bh = _mm_tn("branch_hg_dw", yb, dpb, b_br, Mo=HW, No=D, S=S, tm=tm_b, tn=br_loc, tk=tk_s, stacked_nloc=br_loc)
    t_mix = start("scatter_mixer", dict(out=g_out, bg=g_bg, bh=g_bh))
    def branch_dx(name, dp, w_flat):
        return _matmul(
            name, dp, w_flat, dims=_NT, grid_mnk=(S // tm, GW // tn_g, 1), tiles=(tm, tn_g),
            a_spec=pl.BlockSpec((tm, D), lambda i, j, k: (i, 0)), b_spec=pl.BlockSpec((tn_g, D), lambda i, j, k: (j, 0)),
            out_shapes=[jax.ShapeDtypeStruct((S, GW), F32)], out_specs=[pl.BlockSpec((tm, tn_g), lambda i, j, k: (i, j))],
            epilogue=_store(F32), after=t_mix)[0]

    dya = branch_dx("branch_gmlp_dx", dpa, flat["bg"])
    dyb = branch_dx("branch_hg_dx", dpb, flat["bh"])
    dz_gmlp, dln, dws, dbs = _gmlp_bwd(z, dya, p["ln_g"], p["ln_b"], p["ws"], bsb, GW)
    t_mix = push("scatter_mixer", dz_gmlp)
    dz, dng, dhlb, db_gate = _hg_bwd(z, o_hg, states, dyb, p["hg_lb"], p["hg_ng"] + zero(t_mix), HW, dz_gmlp, dg2)
    half = D // 2
    tm_h = _tile(half, 1024)
    g_in = []
    t_in = None
    for hname, h in (("a", 0), ("b", 1)):
        g_in.append(_mm_tn("proj_in_dw_" + hname, h1, dz, pl.BlockSpec((tk_s, in_loc), lambda i, j, k: (k, j)), Mo=half, No=INW, S=S,
                           tm=tm_h, tn=in_loc, tk=tk_s, stacked_nloc=in_loc, after=t_in, a_off=h * (half // tm_h)))
        t_in = start("scatter_proj_in_" + hname, {"w_in_" + hname: g_in[-1]})
    t_in = push("scatter_proj_in_a", t_in)
    dh1 = _mm_nt_stacked("proj_in_dx", pl.BlockSpec((tm, in_loc), lambda i, j, k: (i, k)), dz, w["in"], M=S, tm=tm, tn=D, tk=in_loc,
                         after=t_in)
    dx, vec1 = _norm_mod_bwd("norm1_bwd", dh1, x, p["norm1_g"], sc1, dxm)

    dmod = jnp.concatenate([vec1[0:1], vec1[1:2], vec2[3:4], vec2[0:1], vec2[1:2], vec_l[2:3]], axis=1)
    small = dict(norm1_g=vec1[2:3], b_gate=db_gate.reshape(1, 2 * D), ln_g=dln[0:1], ln_b=dln[1:2], ws=dws, bs=dbs.reshape(G, T),
                 hg_lb=dhlb, hg_ng=dng[0:1], norm2_g=vec2[2:3], final_g=vec_l[1:2], loss=vec_l[0:1, 0:LANES])
    big = dict(w_in_a=g_in[0], w_in_b=g_in[1], bg=g_bg, bh=g_bh, out=g_out, fi=g_fi, fo=g_fo)
    return dx, big, small, dmod


_SMALL = ("b_ada", "norm1_g", "b_gate", "ln_g", "ln_b", "ws", "bs", "hg_lb", "hg_ng", "norm2_g", "final_g")


def _pack(parts, rows_mult=8):
    flat = [a.reshape(-1) for a in parts]
    offs, n = [], 0
    for a in flat:
        offs.append(n)
        n += a.shape[0]
    pad = (-n) % (LANES * rows_mult)
    if pad:
        flat.append(jnp.zeros((pad,), F32))
    return jnp.concatenate(flat).reshape(-1, LANES), offs


def kernel(x, c, w_ada, b_ada, norm1_g, w_in, b_gate, gmlp_ln_g, gmlp_ln_b, gmlp_ws, gmlp_bs, hg_lb, hg_norm_g, w_branch_gmlp, w_branch_hg, w_out, norm2_g, w_ffn_in, w_ffn_out, final_norm_g, loss_target, m_w_ada, m_b_ada, m_norm1_g, m_w_in, m_b_gate, m_gmlp_ln_g, m_gmlp_ln_b, m_gmlp_ws, m_gmlp_bs, m_hg_lb, m_hg_norm_g, m_w_branch_gmlp, m_w_branch_hg, m_w_out, m_norm2_g, m_w_ffn_in, m_w_ffn_out, m_final_norm_g, v_w_ada, v_b_ada, v_norm1_g, v_w_in, v_b_gate, v_gmlp_ln_g, v_gmlp_ln_b, v_gmlp_ws, v_gmlp_bs, v_hg_lb, v_hg_norm_g, v_w_branch_gmlp, v_w_branch_hg, v_w_out, v_norm2_g, v_w_ffn_in, v_w_ffn_out, v_final_norm_g):
    S, D = x.shape[1], x.shape[2]
    ada_loc = w_ada.shape[2]
    me = 4 * lax.axis_index("x") + 2 * lax.axis_index("y") + lax.axis_index("c")
    me_idx = me.astype(jnp.int32).reshape(1)

    def empty_hbm(shape, dtype):
        return pltpu.with_memory_space_constraint(lax.empty(shape, dtype), pltpu.HBM)

    groups = dict(gather_in=dict(keys=["in"], src=[w_in], forward=True),
                  gather_mixer=dict(keys=["bg", "bh", "out"], src=[w_branch_gmlp, w_branch_hg, w_out], forward=False),
                  gather_ffn_in=dict(keys=["fi"], src=[w_ffn_in], forward=True),
                  gather_ffn_out=dict(keys=["fo"], src=[w_ffn_out], forward=False))
    group_of = {k: gname for gname, g in groups.items() for k in g["keys"]}

    def first_hop(gname, after):
        g = groups[gname]
        n = len(g["keys"])
        cast = [_cast_shard(f"{gname}_cast_{k}", a[0], me_idx) for k, a in zip(g["keys"], g["src"])]
        shards, outs = [s for s, _ in cast], [o for _, o in cast]
        if g["forward"]:
            *g["hop"], token = _split_start(gname + "_hop1", shards + outs, n * 3, _forward_first_copies(n), after=after)
        else:
            *g["hop"], token = _split_start(gname + "_hop1", shards + outs, n * N_CHIP, _gather_first_copies(n), after=after)
        return token

    def second_hop(gname, after):
        g = groups[gname]
        n = len(g["keys"])
        *g["hop"], token = _split_relay(gname + "_hop2", g["hop"][2], g["hop"][0], g["hop"][1], after,
                                        _forward_first_copies(n), n * 4, _forward_second_copies(n))
        return token

    def finish(gname, after):
        g = groups[gname]
        n = len(g["keys"])
        send_sems, recv_sems, bufs = g["hop"]
        if g["forward"]:
            send_sems, recv_sems, bufs, _ = _split_relay(gname + "_hop3", bufs, send_sems, recv_sems, after,
                                                         _forward_second_copies(n), n, _forward_third_copies(n))
            bufs = _split_wait(gname + "_wait", bufs, send_sems, recv_sems, after, _forward_third_copies(n))
        else:
            send_sems, recv_sems, bufs, _ = _split_relay(gname + "_relay", bufs, send_sems, recv_sems, after,
                                                         _gather_first_copies(n), n * (N_CHIP - 1), _gather_relay_copies(n))
            bufs = _split_wait(gname + "_wait", bufs, send_sems, recv_sems, after, _gather_relay_copies(n))
        g["done"] = dict(zip(g["keys"], bufs[n:]))

    c_all = _allgather_small("gather_c", c.reshape(D // LANES, LANES)).reshape(N_DEV, D)
    token = first_hop("gather_in", c_all)
    mod_cols, c_act = _ada_mod(jnp.pad(c_all, ((0, 16 - N_DEV), (0, 0))) + token[0:1, 0:1], w_ada[0])
    mod_vec = mod_cols[:N_DEV].reshape(-1, LANES)
    mg_send, mg_recv, mg_bufs, token = _split_start(
        "gather_mod_start", [mod_vec, lax.dynamic_update_slice(lax.empty((N_DEV, *mod_vec.shape), F32), mod_vec[None], (me, 0, 0))],
        N_DEV - 1, _small_gather_copies)
    token = second_hop("gather_in", token)
    token = first_hop("gather_ffn_in", first_hop("gather_mixer", token))
    mod_all = _split_wait("gather_mod_wait", mg_bufs, mg_send, mg_recv, token, _small_gather_copies)[1].reshape(N_DEV, N_DEV, ada_loc)
    mod = lax.dynamic_index_in_dim(mod_all, me, axis=1, keepdims=False).reshape(1, N_DEV * ada_loc) + b_ada

    def fetch(key, after):
        if key == "in":
            finish("gather_in", after)
        elif key == "fi_early":
            return first_hop("gather_ffn_out", second_hop("gather_ffn_in", after))
        elif "done" not in groups[group_of[key]]:
            finish(group_of[key], after)
        arr = groups[group_of[key]]["done"][key]
        return arr.reshape(-1, D) if key in ("out", "fo") else arr

    p = dict(norm1_g=norm1_g, b_gate=b_gate, ln_g=gmlp_ln_g, ln_b=gmlp_ln_b, ws=gmlp_ws[0], bs=gmlp_bs[0], hg_lb=hg_lb,
             hg_ng=hg_norm_g, norm2_g=norm2_g, final_g=final_norm_g.reshape(1, D))

    in_flight = {}
    c_idx = lax.axis_index("c").astype(jnp.int32).reshape(1)
    chip_idx = (2 * lax.axis_index("x") + lax.axis_index("y")).astype(jnp.int32).reshape(1)

    def scatter_start(name, grads):
        keys = list(grads)
        n = len(keys)
        stacks = [grads[k].reshape(N_DEV, -1, grads[k].shape[-1]) for k in keys]
        lands = [empty_hbm((N_CHIP, *g.shape[1:]), g.dtype) for g in stacks]
        send_sems, recv_sems, bufs, token = _split_start(name + "_d2d", stacks + lands, n * N_CHIP, _to_sibling_copies(n))
        in_flight[name] = dict(keys=keys, stage1=(send_sems, recv_sems, bufs))
        return token

    def scatter_push(name, after):
        f = in_flight[name]
        n = len(f["keys"])
        send_sems, recv_sems, bufs = f["stage1"]
        bufs = _split_wait(name + "_d2d_wait", bufs, send_sems, recv_sems, after, _to_sibling_copies(n))
        sums = [_chip_sum(f"{name}_sum_{k}", bufs[i], bufs[n + i], c_idx) for i, k in enumerate(f["keys"])]
        lands = [empty_hbm((N_CHIP - 1, *s.shape[1:]), s.dtype) for s in sums]
        send_sems, recv_sems, bufs, token = _split_start(name + "_ici", sums + lands, n * (N_CHIP - 1), _to_owner_copies(n))
        f["stage2"] = (send_sems, recv_sems, bufs)
        return token

    grad_x, _, small, dmod = _local_step(x[0], loss_target[0], mod, p, fetch, w_ffn_out.shape[1] * N_DEV, (scatter_start, scatter_push))

    small["b_ada"] = dmod
    packed, offs = _pack([small[k] for k in _SMALL] + [small["loss"]])
    sg_send, sg_recv, sg_bufs, t_tail = _split_start(
        "gather_small_start", [packed, lax.dynamic_update_slice(lax.empty((N_DEV, *packed.shape), F32), packed[None], (me, 0, 0))],
        N_DEV - 1, _small_gather_copies)
    t_tail = scatter_push("scatter_proj_in_b", t_tail)
    big_w = dict(w_in=(w_in, m_w_in, v_w_in, "w_in"), bg=(w_branch_gmlp, m_w_branch_gmlp, v_w_branch_gmlp, "w_branch_gmlp"),
                 bh=(w_branch_hg, m_w_branch_hg, v_w_branch_hg, "w_branch_hg"), out=(w_out, m_w_out, v_w_out, "w_out"),
                 fi=(w_ffn_in, m_w_ffn_in, v_w_ffn_in, "w_ffn_in"), fo=(w_ffn_out, m_w_ffn_out, v_w_ffn_out, "w_ffn_out"))
    upd = {}

    def land_and_update(name, after):
        keys = in_flight[name]["keys"]
        n = len(keys)
        send_sems, recv_sems, bufs = in_flight[name]["stage2"]
        bufs = _split_wait(name + "_ici_wait", bufs, send_sems, recv_sems, after, _to_owner_copies(n))
        for i, k in enumerate(keys):
            if k in big_w:
                wt, mt, vt, out_name = big_w[k]
                upd[out_name] = _adamw("adamw_" + out_name, wt[0], mt[0], vt[0], bufs[i], chip_idx, [bufs[n + i]])
            else:
                wt, mt, vt, out_name = big_w["w_in"]
                upd[out_name] = _adamw("adamw_" + k, wt[0], mt[0], vt[0], bufs[i], chip_idx, [bufs[n + i]],
                                       row0=0 if k == "w_in_a" else bufs[i].shape[1], into=upd.get(out_name))
            after = upd[out_name][1]
        return after

    after = land_and_update("scatter_mixer", land_and_update("scatter_ffn", t_tail))
    gathered = _split_wait("gather_small_wait", sg_bufs, sg_send, sg_recv, after, _small_gather_copies)[1]
    wp = dict(p, b_ada=b_ada)
    ms = dict(b_ada=m_b_ada, norm1_g=m_norm1_g, b_gate=m_b_gate, ln_g=m_gmlp_ln_g, ln_b=m_gmlp_ln_b, ws=m_gmlp_ws, bs=m_gmlp_bs,
              hg_lb=m_hg_lb, hg_ng=m_hg_norm_g, norm2_g=m_norm2_g, final_g=m_final_norm_g)
    vs = dict(b_ada=v_b_ada, norm1_g=v_norm1_g, b_gate=v_b_gate, ln_g=v_gmlp_ln_g, ln_b=v_gmlp_ln_b, ws=v_gmlp_ws, bs=v_gmlp_bs,
              hg_lb=v_hg_lb, hg_ng=v_hg_norm_g, norm2_g=v_norm2_g, final_g=v_final_norm_g)
    w_sm, _ = _pack([wp[k] for k in _SMALL])
    m_sm, _ = _pack([ms[k] for k in _SMALL])
    v_sm, _ = _pack([vs[k] for k in _SMALL])
    shapes = dict(b_ada=b_ada.shape, norm1_g=norm1_g.shape, b_gate=b_gate.shape, ln_g=gmlp_ln_g.shape, ln_b=gmlp_ln_b.shape,
                  ws=gmlp_ws.shape, bs=gmlp_bs.shape, hg_lb=hg_lb.shape, hg_ng=hg_norm_g.shape, norm2_g=norm2_g.shape,
                  final_g=final_norm_g.shape)
    sm_out = _small_update(gathered, w_sm, m_sm, v_sm, after, [math.prod(shapes[k]) // LANES for k in _SMALL])

    def unpack(idx, k):
        return sm_out[idx * len(_SMALL) + _SMALL.index(k)].reshape(shapes[k])

    loss = sm_out[-1][0, 0]

    assert offs[0] == 0 and ada_loc % LANES == 0
    dmod_loc = lax.dynamic_slice_in_dim(gathered, me * (ada_loc // LANES), ada_loc // LANES, axis=1).reshape(N_DEV, ada_loc)
    ca_t = jnp.pad(c_act[:N_DEV].T, ((0, 0), (0, LANES - N_DEV))).astype(BF16)
    dm_p = jnp.pad(dmod_loc, ((0, LANES - N_DEV), (0, 0))).astype(BF16)
    tm_a = _tile(D, 512)
    g_ada = _matmul(
        "ada_dw", ca_t, dm_p, dims=_NN, grid_mnk=(D // tm_a, 1, 1), tiles=(tm_a, ada_loc),
        a_spec=pl.BlockSpec((tm_a, LANES), lambda i, j, k: (i, 0)), b_spec=pl.BlockSpec((LANES, ada_loc), lambda i, j, k: (0, 0)),
        out_shapes=[jax.ShapeDtypeStruct((1, D, ada_loc), F32)], out_specs=[pl.BlockSpec((None, tm_a, ada_loc), lambda i, j, k: (0, i, 0))],
        epilogue=_store(F32))[0]
    upd["w_ada"] = _adamw("adamw_w_ada", w_ada[0], m_w_ada[0], v_w_ada[0], g_ada, jnp.zeros((1,), jnp.int32))
    land_and_update("scatter_proj_in_b", land_and_update("scatter_proj_in_a", upd["w_ada"][1]))

    order = ("w_ada", "b_ada", "norm1_g", "w_in", "b_gate", "ln_g", "ln_b", "ws", "bs", "hg_lb", "hg_ng", "w_branch_gmlp", "w_branch_hg",
             "w_out", "norm2_g", "w_ffn_in", "w_ffn_out", "final_g")
    outs = [loss, grad_x[None]]
    for idx in range(4):
        for k in order:
            outs.append(upd[k][idx][None] if k in upd else unpack(idx, k))
    return tuple(outs)
```

```python
import functools
import math

import jax
import jax.numpy as jnp
from jax import lax
from jax.experimental import pallas as pl
from jax.experimental.pallas import tpu as pltpu

F32 = jnp.float32
BF16 = jnp.bfloat16
N_DEV = 8
EPS = 1e-6
LANES = 128
HG_DK = 128
HG_CHUNK = 64
HG_MID = HG_CHUNK // 2 - 1
EXP_CLAMP = 80.0
VMEM_LIMIT = 48 * 1024 * 1024
BF16_ROWS = 16
STREAM_TILE = 1 << 20
ADAMW_TILE = 3 << 17
ADAM_LR, ADAM_B1, ADAM_B2, ADAM_EPS, ADAM_WD, ADAM_STEP = 0.001, 0.9, 0.999, 1e-08, 0.01, 10
MESH = pl.DeviceIdType.MESH

_NN = (((1,), (0,)), ((), ()))
_NT = (((1,), (1,)), ((), ()))
_TN = (((0,), (0,)), ((), ()))


def _dot(a, b, dims=_NN):
    return lax.dot_general(a.astype(BF16), b.astype(BF16), dims, preferred_element_type=F32)


def _tile(n, target, mult=LANES):
    best = None
    for t in range(mult, min(n, target) + 1, mult):
        if n % t == 0:
            best = t
    return n if best is None else best


def _cparams(sem):
    return pltpu.CompilerParams(dimension_semantics=sem, vmem_limit_bytes=VMEM_LIMIT)


def _sigmoid(x):
    return 1.0 / (1.0 + jnp.exp(-x))


def _gelu_parts(x):
    k0 = math.sqrt(2.0 / math.pi)
    x2 = x * x
    t = jnp.tanh(k0 * (x + 0.044715 * x * x2))
    g = 0.5 * x * (1.0 + t)
    dg = 0.5 * (1.0 + t) + 0.5 * x * (1.0 - t * t) * (k0 * (1.0 + 3.0 * 0.044715 * x2))
    return g, dg


def _split3(x):
    h = x.astype(BF16)
    r = x - h.astype(F32)
    m = r.astype(BF16)
    lo = (r - m.astype(F32)).astype(BF16)
    return h, m, lo


def _ones_dot(mat01, x):
    h, m, lo = _split3(x)
    d = functools.partial(lax.dot_general, dimension_numbers=_NN, preferred_element_type=F32)
    return d(mat01, h) + d(mat01, m) + d(mat01, lo)


def _matmul(name, a, b, *, dims, grid_mnk, tiles, a_spec, b_spec, extras=(), extra_specs=(), out_shapes, out_specs, epilogue, after=None,
            sem=None):
    gm, gn, nk = grid_mnk
    tm, tn = tiles
    n_ex, n_out = len(extras), len(out_shapes)
    held = [] if after is None else [after]

    def body(*refs):
        a_ref, b_ref = refs[0], refs[1]
        ex = refs[2:2 + n_ex]
        outs = refs[2 + n_ex + len(held):2 + n_ex + len(held) + n_out]
        more = () if sem is None else (pl.program_id(0) == 0,)
        if nk == 1:
            epilogue(lax.dot_general(a_ref[...], b_ref[...], dims, preferred_element_type=F32), ex, outs, *more)
            return
        acc = refs[-1]
        k = pl.program_id(2)

        @pl.when(k == 0)
        def _():
            acc[...] = jnp.zeros_like(acc)

        acc[...] += lax.dot_general(a_ref[...], b_ref[...], dims, preferred_element_type=F32)

        @pl.when(k == nk - 1)
        def _():
            epilogue(acc[...], ex, outs, *more)

    return pl.pallas_call(
        body, name=name, grid=(gm, gn, nk), in_specs=[a_spec, b_spec, *extra_specs] + [pl.BlockSpec(memory_space=pl.ANY)] * len(held),
        out_specs=list(out_specs), out_shape=list(out_shapes), scratch_shapes=[] if nk == 1 else [pltpu.VMEM((tm, tn), F32)],
        compiler_params=_cparams(sem or ("parallel", "parallel", "arbitrary")),
    )(a, b, *extras, *held)


def _store(dtype):
    def ep(acc, ex, outs):
        outs[0][...] = acc.astype(dtype)
    return ep


def _mm_nn_stacked(name, a, wg, *, tm, tn, tk, out_dtype=F32, extras=(), extra_specs=(), out_shapes=None, out_specs=None, epilogue=None,
                   after=None):
    M, K = a.shape
    _, _, nloc = wg.shape
    N = nloc * N_DEV
    q = nloc // tn
    if out_shapes is None:
        out_shapes = [jax.ShapeDtypeStruct((M, N), out_dtype)]
        out_specs = [pl.BlockSpec((tm, tn), lambda i, j, k: (i, j))]
        epilogue = _store(out_dtype)
    return _matmul(
        name, a, wg, dims=_NN, grid_mnk=(M // tm, N // tn, K // tk), tiles=(tm, tn),
        a_spec=pl.BlockSpec((tm, tk), lambda i, j, k: (i, k)),
        b_spec=pl.BlockSpec((None, tk, tn), lambda i, j, k: (j // q, k, j % q)),
        extras=extras, extra_specs=extra_specs, out_shapes=out_shapes, out_specs=out_specs, epilogue=epilogue, after=after)


def _mm_nt_stacked(name, a_spec, a, wg, *, M, tm, tn, tk, out_dtype=F32, after=None, extras=(), extra_specs=(), out_shapes=None,
                   out_specs=None, epilogue=None, sem=None):
    _, Kw, nloc = wg.shape
    q = nloc // tk
    single = out_shapes is None
    if single:
        out_shapes = [jax.ShapeDtypeStruct((M, Kw), out_dtype)]
        out_specs = [pl.BlockSpec((tm, tn), lambda i, j, k: (i, j))]
        epilogue = _store(out_dtype)
    res = _matmul(
        name, a, wg, dims=_NT, grid_mnk=(M // tm, Kw // tn, (nloc * N_DEV) // tk), tiles=(tm, tn),
        a_spec=a_spec, b_spec=pl.BlockSpec((None, tn, tk), lambda i, j, k: (k // q, j, k % q)),
        extras=extras, extra_specs=extra_specs, out_shapes=out_shapes, out_specs=out_specs, epilogue=epilogue, after=after, sem=sem)
    return res[0] if single else res


def _mm_tn(name, a, b, b_spec, *, Mo, No, S, tm, tn, tk, stacked_nloc=None, after=None, a_off=0):
    if stacked_nloc is None:
        out_shape = jax.ShapeDtypeStruct((Mo, No), BF16)
        out_spec = pl.BlockSpec((tm, tn), lambda i, j, k: (i, j))
    else:
        q = stacked_nloc // tn
        out_shape = jax.ShapeDtypeStruct((N_DEV, Mo, stacked_nloc), BF16)
        out_spec = pl.BlockSpec((None, tm, tn), lambda i, j, k: (j // q, i, j % q))
    return _matmul(
        name, a, b, dims=_TN, grid_mnk=(Mo // tm, No // tn, S // tk), tiles=(tm, tn),
        a_spec=pl.BlockSpec((tk, tm), lambda i, j, k: (k, i + a_off)), b_spec=b_spec,
        out_shapes=[out_shape], out_specs=[out_spec], epilogue=_store(BF16), after=after)[0]


def _norm_mod(name, x, g, sc, sh):
    S, D = x.shape
    tm = _tile(S, 256, 8)

    def body(x_ref, g_ref, sc_ref, sh_ref, h_ref):
        xv = x_ref[...]
        r = lax.rsqrt(jnp.mean(xv * xv, axis=-1, keepdims=True) + EPS)
        h = (xv * r) * g_ref[...]
        h_ref[...] = (h * (1.0 + sc_ref[...]) + sh_ref[...]).astype(BF16)

    row = pl.BlockSpec((tm, D), lambda i: (i, 0))
    vec = pl.BlockSpec((1, D), lambda i: (0, 0))
    return pl.pallas_call(body, name=name, grid=(S // tm,), in_specs=[row, vec, vec, vec], out_specs=row,
                          out_shape=jax.ShapeDtypeStruct((S, D), BF16), compiler_params=_cparams(("parallel",)))(x, g, sc, sh)


def _norm_mod_bwd_rows(first, dh_v, x_ref, g_ref, sc_ref, dres_ref, dx_ref, vec_ref, o_ref=None, gt_ref=None, do_ref=None):
    @pl.when(first)
    def _():
        vec_ref[...] = jnp.zeros_like(vec_ref)

    xv, gv = x_ref[...], g_ref[...]
    r = lax.rsqrt(jnp.mean(xv * xv, axis=-1, keepdims=True) + EPS)
    xn = xv * r
    one_sc = 1.0 + sc_ref[...]
    vec_ref[0:1, :] += jnp.sum(dh_v, axis=0, keepdims=True)
    vec_ref[1:2, :] += jnp.sum(dh_v * (xn * gv), axis=0, keepdims=True)
    vec_ref[2:3, :] += jnp.sum(dh_v * one_sc * xn, axis=0, keepdims=True)
    dxn = dh_v * one_sc * gv
    dx = dres_ref[...] + r * (dxn - xn * jnp.mean(dxn * xn, axis=-1, keepdims=True))
    dx_ref[...] = dx
    if o_ref is not None:
        vec_ref[3:4, :] += jnp.sum(dx * o_ref[...], axis=0, keepdims=True)
        do_ref[...] = (dx * gt_ref[...]).astype(BF16)


def _norm_mod_bwd(name, dh, x, g, sc, dres, o=None, gt=None):
    S, D = x.shape
    tm = _tile(S, 256, 8)
    gated = o is not None

    def body(*refs):
        if gated:
            dh_ref, x_ref, g_ref, sc_ref, dres_ref, o_ref, gt_ref, dx_ref, vec_ref, do_ref = refs
        else:
            dh_ref, x_ref, g_ref, sc_ref, dres_ref, dx_ref, vec_ref = refs
            o_ref = gt_ref = do_ref = None
        _norm_mod_bwd_rows(pl.program_id(0) == 0, dh_ref[...], x_ref, g_ref, sc_ref, dres_ref, dx_ref, vec_ref, o_ref, gt_ref, do_ref)

    row = pl.BlockSpec((tm, D), lambda i: (i, 0))
    vec = pl.BlockSpec((1, D), lambda i: (0, 0))
    acc = pl.BlockSpec((8, D), lambda i: (0, 0))
    ins = [dh, x, g, sc, dres] + ([o, gt] if gated else [])
    in_specs = [row, row, vec, vec, row] + ([row, vec] if gated else [])
    out_shape = [jax.ShapeDtypeStruct((S, D), F32), jax.ShapeDtypeStruct((8, D), F32)]
    out_specs = [row, acc]
    if gated:
        out_shape.append(jax.ShapeDtypeStruct((S, D), BF16))
        out_specs.append(row)
    return pl.pallas_call(body, name=name, grid=(S // tm,), in_specs=in_specs, out_specs=out_specs, out_shape=out_shape,
                          compiler_params=_cparams(("arbitrary",)))(*ins)


def _loss_head(x3, tgt, gf, o2, gt2):
    S, D = x3.shape
    tm = _tile(S, 256, 8)

    def body(x_ref, t_ref, g_ref, o_ref, gt_ref, dx_ref, do_ref, vec_ref):
        i = pl.program_id(0)

        @pl.when(i == 0)
        def _():
            vec_ref[...] = jnp.zeros_like(vec_ref)

        xv, gv = x_ref[...], g_ref[...]
        r = lax.rsqrt(jnp.mean(xv * xv, axis=-1, keepdims=True) + EPS)
        xn = xv * r
        e = xn * gv - t_ref[...]
        tok = 0.5 * jnp.mean(e * e, axis=-1, keepdims=True)
        vec_ref[0:1, :] += jnp.broadcast_to(jnp.sum(tok, axis=0, keepdims=True), (1, D))
        dy = e * (1.0 / D)
        vec_ref[1:2, :] += jnp.sum(dy * xn, axis=0, keepdims=True)
        dxn = dy * gv
        dx = r * (dxn - xn * jnp.mean(dxn * xn, axis=-1, keepdims=True))
        dx_ref[...] = dx
        vec_ref[2:3, :] += jnp.sum(dx * o_ref[...], axis=0, keepdims=True)
        do_ref[...] = (dx * gt_ref[...]).astype(BF16)

    row = pl.BlockSpec((tm, D), lambda i: (i, 0))
    vec = pl.BlockSpec((1, D), lambda i: (0, 0))
    return pl.pallas_call(
        body, name="loss_head", grid=(S // tm,), in_specs=[row, row, vec, row, vec],
        out_specs=[row, row, pl.BlockSpec((8, D), lambda i: (0, 0))],
        out_shape=[jax.ShapeDtypeStruct((S, D), F32), jax.ShapeDtypeStruct((S, D), BF16), jax.ShapeDtypeStruct((8, D), F32)],
        compiler_params=_cparams(("arbitrary",)))(x3, tgt, gf, o2, gt2)


def _ffn_in_swiglu(h, wg):
    S, D = h.shape
    _, _, tf = wg.shape
    nf = N_DEV // 2
    F = nf * tf
    tm = _tile(S, 256, 16)

    def body(h_ref, wa_ref, wu_ref, hf_ref, fac_ref):
        hv = h_ref[...]
        a = lax.dot_general(hv, wa_ref[...], _NN, preferred_element_type=F32)
        up = lax.dot_general(hv, wu_ref[...], _NN, preferred_element_type=F32)
        sa = _sigmoid(a)
        silu = a * sa
        hf_ref[...] = (silu * up).astype(BF16)
        fac_ref[0] = (up * (sa * (1.0 + a * (1.0 - sa)))).astype(BF16)
        fac_ref[1] = silu.astype(BF16)

    return pl.pallas_call(
        body, name="ffn_in_swiglu", grid=(nf, S // tm),
        in_specs=[pl.BlockSpec((tm, D), lambda j, i: (i, 0)), pl.BlockSpec((None, D, tf), lambda j, i: (j, 0, 0)),
                  pl.BlockSpec((None, D, tf), lambda j, i: (j + nf, 0, 0))],
        out_specs=[pl.BlockSpec((tm, tf), lambda j, i: (i, j)), pl.BlockSpec((2, tm, tf), lambda j, i: (0, i, j))],
        out_shape=[jax.ShapeDtypeStruct((S, F), BF16), jax.ShapeDtypeStruct((2, S, F), BF16)],
        compiler_params=_cparams(("parallel", "parallel")))(h, wg, wg)


def _gmlp_common(u_ref, v_ref, lg_ref, lb_ref, ws_ref, bsb_ref, G, T, Dg):
    ug, dug = _gelu_parts(u_ref[...])
    vg, dvg = _gelu_parts(v_ref[...])
    mu = jnp.mean(vg, axis=-1, keepdims=True)
    vc = vg - mu
    rstd = lax.rsqrt(jnp.mean(vc * vc, axis=-1, keepdims=True) + EPS)
    vhat = vc * rstd
    vn = vhat * lg_ref[...] + lb_ref[...]
    row = lax.broadcasted_iota(jnp.int32, (T, T), 0)
    col = lax.broadcasted_iota(jnp.int32, (T, T), 1)
    tril = row >= col
    s = []
    for g in range(G):
        w = jnp.where(tril, ws_ref[g], 0.0)
        s.append(_dot(w, vn[:, g * Dg:(g + 1) * Dg]) + bsb_ref[g])
    return ug, dug, dvg, rstd, vhat, vn, tril, s


def _gmlp_fwd(z, ln_g, ln_b, ws, bsb, GW):
    S = z.shape[0]
    G, T, _ = ws.shape
    Dg = GW // G

    def body(u_ref, v_ref, lg_ref, lb_ref, ws_ref, bsb_ref, ya_ref):
        ug, _, _, _, _, _, _, s = _gmlp_common(u_ref, v_ref, lg_ref, lb_ref, ws_ref, bsb_ref, G, T, Dg)
        for g in range(G):
            sl = slice(g * Dg, (g + 1) * Dg)
            ya_ref[:, sl] = (ug[:, sl] * s[g]).astype(BF16)

    vec = pl.BlockSpec((1, GW), lambda c: (0, 0))
    return pl.pallas_call(
        body, name="gmlp_fwd", grid=(S // T,),
        in_specs=[pl.BlockSpec((T, GW), lambda c: (c, 0)), pl.BlockSpec((T, GW), lambda c: (c, 1)), vec, vec,
                  pl.BlockSpec((G, T, T), lambda c: (0, 0, 0)), pl.BlockSpec((G, T, Dg), lambda c: (0, 0, 0))],
        out_specs=pl.BlockSpec((T, GW), lambda c: (c, 0)), out_shape=jax.ShapeDtypeStruct((S, GW), BF16),
        compiler_params=_cparams(("parallel",)))(z, z, ln_g, ln_b, ws, bsb)


def _gmlp_bwd(z, dya, ln_g, ln_b, ws, bsb, GW):
    S = z.shape[0]
    G, T, _ = ws.shape
    Dg = GW // G
    nc = S // T

    def body(u_ref, v_ref, dya_ref, lg_ref, lb_ref, ws_ref, bsb_ref, dz_ref, dln_ref, dws_ref, dbs_ref, dbs_acc, dvh):
        c = pl.program_id(0)

        @pl.when(c == 0)
        def _():
            dln_ref[...] = jnp.zeros_like(dln_ref)
            dws_ref[...] = jnp.zeros_like(dws_ref)
            dbs_acc[...] = jnp.zeros_like(dbs_acc)

        ug, dug, dvg, rstd, vhat, vn, tril, s = _gmlp_common(u_ref, v_ref, lg_ref, lb_ref, ws_ref, bsb_ref, G, T, Dg)
        dya_v = dya_ref[...]
        for g in range(G):
            sl = slice(g * Dg, (g + 1) * Dg)
            dy_g = dya_v[:, sl]
            dz_ref[:, sl] = (dy_g * s[g] * dug[:, sl]).astype(BF16)
            ds = dy_g * ug[:, sl]
            dbs_acc[g] += ds
            w = jnp.where(tril, ws_ref[g], 0.0)
            dvn_g = _dot(w, ds, _TN)
            dws_ref[g] += jnp.where(tril, _dot(ds, vn[:, sl], _NT), 0.0)
            dln_ref[0:1, sl] += jnp.sum(dvn_g * vhat[:, sl], axis=0, keepdims=True)
            dln_ref[1:2, sl] += jnp.sum(dvn_g, axis=0, keepdims=True)
            dvh[:, sl] = dvn_g * lg_ref[:, sl]
        dvhat = dvh[...]
        m1 = jnp.mean(dvhat, axis=-1, keepdims=True)
        m2 = jnp.mean(dvhat * vhat, axis=-1, keepdims=True)
        dz_ref[:, GW:2 * GW] = (rstd * (dvhat - m1 - vhat * m2) * dvg).astype(BF16)

        @pl.when(c == nc - 1)
        def _():
            for g in range(G):
                dbs_ref[g] = jnp.sum(dbs_acc[g], axis=-1, keepdims=True)

    vec = pl.BlockSpec((1, GW), lambda c: (0, 0))
    return pl.pallas_call(
        body, name="gmlp_bwd", grid=(nc,),
        in_specs=[pl.BlockSpec((T, GW), lambda c: (c, 0)), pl.BlockSpec((T, GW), lambda c: (c, 1)),
                  pl.BlockSpec((T, GW), lambda c: (c, 0)), vec, vec,
                  pl.BlockSpec((G, T, T), lambda c: (0, 0, 0)), pl.BlockSpec((G, T, Dg), lambda c: (0, 0, 0))],
        out_specs=[pl.BlockSpec((T, 2 * GW), lambda c: (c, 0)), pl.BlockSpec((8, GW), lambda c: (0, 0)),
                   pl.BlockSpec((G, T, T), lambda c: (0, 0, 0)), pl.BlockSpec((G, T, 1), lambda c: (0, 0, 0))],
        out_shape=[jax.ShapeDtypeStruct((S, 2 * GW), BF16), jax.ShapeDtypeStruct((8, GW), F32),
                   jax.ShapeDtypeStruct((G, T, T), F32), jax.ShapeDtypeStruct((G, T, 1), F32)],
        scratch_shapes=[pltpu.VMEM((G, T, Dg), F32), pltpu.VMEM((T, GW), F32)],
        compiler_params=_cparams(("arbitrary",)))(z, z, dya, ln_g, ln_b, ws, bsb)


def _hg_common(q_ref, f_ref, hlb_ref):
    C = HG_CHUNK
    a = hlb_ref[...]
    lb = _sigmoid(a[0:1, :] - a[1:2, :])
    sig = _sigmoid(f_ref[...])
    f = lb + (1.0 - lb) * sig
    lf = jnp.log(f)
    kk = 1.0 - f
    q = q_ref[...]
    sq = _sigmoid(q)
    qa = q * sq
    row = lax.broadcasted_iota(jnp.int32, (C, C), 0)
    col = lax.broadcasted_iota(jnp.int32, (C, C), 1)
    tril = row >= col
    b = _ones_dot(tril.astype(BF16), lf)
    bm = b[HG_MID:HG_MID + 1, :]
    bl = b[C - 1:C, :]
    e_b = jnp.exp(b)
    e_qm = jnp.exp(jnp.minimum(b - bm, EXP_CLAMP))
    e_km = jnp.exp(jnp.minimum(bm - b, EXP_CLAMP))
    e_kl = jnp.exp(bl - b)
    return dict(lb=lb, sig=sig, f=f, kk=kk, q=q, sq=sq, qa=qa, tril=tril, e_b=e_b, e_qm=e_qm, e_km=e_km, e_kl=e_kl,
                e_l=jnp.exp(bl), qh=qa * e_b, qt=qa * e_qm, kt=kk * e_km, kh=kk * e_kl)


def _hg_fwd(z, hg_lb, ng, HW):
    S = z.shape[0]
    C, H, dk = HG_CHUNK, HW // HG_DK, HG_DK
    nc = S // C

    def body(q_ref, f_ref, i_ref, og_ref, hlb_ref, ng_ref, yb_ref, o_ref, st_ref, state):
        @pl.when(pl.program_id(0) == 0)
        def _():
            state[...] = jnp.zeros_like(state)

        t = _hg_common(q_ref, f_ref, hlb_ref)
        iv = i_ref[...]
        for h in range(H):
            sl = slice(h * dk, (h + 1) * dk)
            st = state[h]
            st_ref[h] = st
            a = jnp.where(t["tril"], _dot(t["qt"][:, sl], t["kt"][:, sl], _NT), 0.0)
            o_h = _dot(a, iv[:, sl]) + _dot(t["qh"][:, sl], st, _NT)
            state[h] = st * t["e_l"][:, sl] + _dot(iv[:, sl], t["kh"][:, sl], _TN)
            o_ref[:, sl] = o_h
            rr = lax.rsqrt(jnp.mean(o_h * o_h, axis=-1, keepdims=True) + EPS)
            og = og_ref[:, sl]
            yb_ref[:, sl] = (o_h * rr * ng_ref[:, sl] * (og * _sigmoid(og))).astype(BF16)

    def col(k):
        return pl.BlockSpec((C, HW), lambda c: (c, k))

    base = 2
    return pl.pallas_call(
        body, name="hgrn_fwd", grid=(nc,),
        in_specs=[col(base), col(base + 1), col(base + 2), col(base + 3),
                  pl.BlockSpec((2, HW), lambda c: (0, 0)), pl.BlockSpec((1, HW), lambda c: (0, 0))],
        out_specs=[pl.BlockSpec((C, HW), lambda c: (c, 0)), pl.BlockSpec((C, HW), lambda c: (c, 0)),
                   pl.BlockSpec((None, H, dk, dk), lambda c: (c, 0, 0, 0))],
        out_shape=[jax.ShapeDtypeStruct((S, HW), BF16), jax.ShapeDtypeStruct((S, HW), F32),
                   jax.ShapeDtypeStruct((nc, H, dk, dk), F32)],
        scratch_shapes=[pltpu.VMEM((H, dk, dk), F32)],
        compiler_params=_cparams(("arbitrary",)))(z, z, z, z, hg_lb, ng)


def _hg_bwd(z, o, states, dyb, hg_lb, ng, HW, dz_head, dz_tail):
    S = z.shape[0]
    C, H, dk = HG_CHUNK, HW // HG_DK, HG_DK
    nc = S // C
    B0 = dz_head.shape[1]
    DT = dz_tail.shape[2]
    INW = B0 + 4 * HW + 2 * DT

    def body(q_ref, f_ref, i_ref, og_ref, o_ref, st_ref, stn_ref, dyb_ref, hlb_ref, ng_ref, head_ref, tail_ref,
             dzf_ref, dng_ref, dhlb_ref, dtail_ref, dstate, cross, dqa_buf, dkk_buf, db_buf, dlb_acc):
        c = pl.program_id(0)
        dzf_ref[:, 0:B0] = head_ref[...]
        dzf_ref[:, B0 + 4 * HW:B0 + 4 * HW + DT] = tail_ref[0]
        dzf_ref[:, B0 + 4 * HW + DT:INW] = tail_ref[1]
        dz_ref = dzf_ref.at[:, B0:B0 + 4 * HW]

        @pl.when(c == 0)
        def _():
            dtail_ref[...] = jnp.zeros_like(dtail_ref)

        dtail_ref[0:1, :] += jnp.sum(tail_ref[0].astype(F32), axis=0, keepdims=True)
        dtail_ref[1:2, :] += jnp.sum(tail_ref[1].astype(F32), axis=0, keepdims=True)

        @pl.when(c == 0)
        def _():
            dstate[...] = jnp.zeros_like(dstate)
            dlb_acc[...] = jnp.zeros_like(dlb_acc)
            dng_ref[...] = jnp.zeros_like(dng_ref)

        def r16(v):
            return v.astype(BF16).astype(F32)

        t = _hg_common(q_ref, f_ref, hlb_ref)
        iv = i_ref[...]
        for h in range(H):
            sl = slice(h * dk, (h + 1) * dk)
            o_h, og, dyb_h, ng_h = o_ref[:, sl], og_ref[:, sl], dyb_ref[:, sl], ng_ref[:, sl]
            sg = _sigmoid(og)
            silu_og = og * sg
            rr = lax.rsqrt(jnp.mean(o_h * o_h, axis=-1, keepdims=True) + EPS)
            on = o_h * rr
            dng_ref[0:1, sl] += jnp.sum(dyb_h * on * silu_og, axis=0, keepdims=True)
            dz_ref[:, 3 * HW + h * dk:3 * HW + (h + 1) * dk] = (dyb_h * on * ng_h * (sg * (1.0 + og * (1.0 - sg)))).astype(BF16)
            don = dyb_h * ng_h * silu_og
            do_h = rr * (don - on * jnp.mean(don * on, axis=-1, keepdims=True))

            qt, kt, qh, kh, iv_h = t["qt"][:, sl], t["kt"][:, sl], t["qh"][:, sl], t["kh"][:, sl], iv[:, sl]
            a = jnp.where(t["tril"], _dot(qt, kt, _NT), 0.0)
            da = jnp.where(t["tril"], _dot(do_h, iv_h, _NT), 0.0)
            st, dst = st_ref[h], dstate[h]
            cross[:, sl] = jnp.sum(dst * stn_ref[h], axis=0, keepdims=True)
            dqh = _dot(do_h, st)
            dstate[h] = _dot(do_h, qh, _TN) + dst * t["e_l"][:, sl]
            div = _dot(a, do_h, _TN) + _dot(kh, dst, _NT)
            dkh = _dot(iv_h, dst)
            dqt = _dot(da, kt)
            dkt = _dot(da, qt, _TN)
            dz_ref[:, 2 * HW + h * dk:2 * HW + (h + 1) * dk] = div.astype(BF16)
            dqa_buf[:, sl] = dqh * t["e_b"][:, sl] + dqt * t["e_qm"][:, sl]
            dkk_buf[:, sl] = dkt * t["e_km"][:, sl] + dkh * t["e_kl"][:, sl]
            db_buf[:, sl] = r16(qt) * dqt - r16(kt) * dkt + r16(qh) * dqh - r16(kh) * dkh

        dqa, dkk = dqa_buf[...], dkk_buf[...]
        triu = jnp.logical_not(t["tril"]) | (lax.broadcasted_iota(jnp.int32, (C, C), 0) == lax.broadcasted_iota(jnp.int32, (C, C), 1))
        dlf = _ones_dot(triu.astype(BF16), db_buf[...]) + cross[...]
        df = dlf / t["f"] - dkk
        sig, lb = t["sig"], t["lb"]
        dz_ref[:, HW:2 * HW] = (df * (1.0 - lb) * sig * (1.0 - sig)).astype(BF16)
        dlb_acc[...] += jnp.sum(df * (1.0 - sig), axis=0, keepdims=True)
        q, sq = t["q"], t["sq"]
        dz_ref[:, 0:HW] = (dqa * (sq * (1.0 + q * (1.0 - sq)))).astype(BF16)

        @pl.when(c == nc - 1)
        def _():
            da0 = dlb_acc[...] * lb * (1.0 - lb)
            dhlb_ref[0:1, :] = da0
            dhlb_ref[1:2, :] = -da0

    def col(k):
        return pl.BlockSpec((C, HW), lambda c: (nc - 1 - c, k))

    base = 2
    return pl.pallas_call(
        body, name="hgrn_bwd", grid=(nc,),
        in_specs=[col(base), col(base + 1), col(base + 2), col(base + 3), col(0),
                  pl.BlockSpec((None, H, dk, dk), lambda c: (nc - 1 - c, 0, 0, 0)),
                  pl.BlockSpec((None, H, dk, dk), lambda c: (jnp.minimum(nc - c, nc - 1), 0, 0, 0)), col(0),
                  pl.BlockSpec((2, HW), lambda c: (0, 0)), pl.BlockSpec((1, HW), lambda c: (0, 0)),
                  pl.BlockSpec((C, B0), lambda c: (nc - 1 - c, 0)), pl.BlockSpec((2, C, DT), lambda c: (0, nc - 1 - c, 0))],
        out_specs=[pl.BlockSpec((C, INW), lambda c: (nc - 1 - c, 0)), pl.BlockSpec((8, HW), lambda c: (0, 0)),
                   pl.BlockSpec((2, HW), lambda c: (0, 0)), pl.BlockSpec((2, DT), lambda c: (0, 0))],
        out_shape=[jax.ShapeDtypeStruct((S, INW), BF16), jax.ShapeDtypeStruct((8, HW), F32), jax.ShapeDtypeStruct((2, HW), F32),
                   jax.ShapeDtypeStruct((2, DT), F32)],
        scratch_shapes=[pltpu.VMEM((H, dk, dk), F32), pltpu.VMEM((1, HW), F32), pltpu.VMEM((C, HW), F32), pltpu.VMEM((C, HW), F32),
                        pltpu.VMEM((C, HW), F32), pltpu.VMEM((1, HW), F32)],
        compiler_params=_cparams(("arbitrary",)))(z, z, z, z, o, states, states, dyb, hg_lb, ng, dz_head, dz_tail)


def _position():
    x, y, c = lax.axis_index("x"), lax.axis_index("y"), lax.axis_index("c")
    return x, y, c, 4 * x + 2 * y + c


def _flip(x, y, c, k):
    return (1 - x if k & 4 else x, 1 - y if k & 2 else y, 1 - c if k & 1 else c)


def _allgather_small(name, v):
    R, L = v.shape

    def body(v_ref, out_ref, send_sems, recv_sems):
        x, y, c, me = _position()
        out_ref[me] = v_ref[...]
        copies = []
        for k in range(1, N_DEV):
            cp = pltpu.make_async_remote_copy(src_ref=v_ref, dst_ref=out_ref.at[me], send_sem=send_sems.at[k - 1],
                                              recv_sem=recv_sems.at[k - 1], device_id=_flip(x, y, c, k), device_id_type=MESH)
            cp.start()
            copies.append(cp)
        for cp in copies:
            cp.wait()

    return pl.pallas_call(
        body, name=name, out_shape=jax.ShapeDtypeStruct((N_DEV, R, L), v.dtype),
        in_specs=[pl.BlockSpec(memory_space=pltpu.VMEM)], out_specs=pl.BlockSpec(memory_space=pltpu.VMEM),
        scratch_shapes=[pltpu.SemaphoreType.DMA((N_DEV - 1,)), pltpu.SemaphoreType.DMA((N_DEV - 1,))],
        compiler_params=pltpu.CompilerParams(vmem_limit_bytes=VMEM_LIMIT),
    )(v)


_HBM = pl.BlockSpec(memory_space=pltpu.HBM)
_SEM = pl.BlockSpec(memory_space=pltpu.SEMAPHORE)
_EFFECT = pltpu.SideEffectType.DATAFLOW_SIDE_EFFECTING


def _split_start(name, bufs, n_sems, copies_fn, after=None):
    nb = len(bufs)
    extra = [] if after is None else [after]
    k = nb + len(extra)

    def body(*refs):
        for cp in copies_fn(refs[:nb], refs[k], refs[k + 1]):
            cp.start()
        refs[-1][...] = jnp.zeros_like(refs[-1])

    sems = pltpu.SemaphoreType.DMA((n_sems,))
    res = pl.pallas_call(
        body, name=name,
        out_shape=(sems, sems, *[pltpu.HBM(a.shape, a.dtype) for a in bufs], jax.ShapeDtypeStruct((8, LANES), F32)),
        in_specs=[_HBM] * nb + [pl.BlockSpec(memory_space=pl.ANY)] * len(extra),
        out_specs=(_SEM, _SEM, *[_HBM] * nb, pl.BlockSpec(memory_space=pltpu.VMEM)),
        input_output_aliases={i: 2 + i for i in range(nb)},
        compiler_params=pltpu.CompilerParams(has_side_effects=_EFFECT),
    )(*[pltpu.with_memory_space_constraint(a, pltpu.HBM) for a in bufs], *extra)
    return res[0], res[1], list(res[2:2 + nb]), res[-1]


def _split_wait(name, bufs, send_sems, recv_sems, after, copies_fn):
    nb = len(bufs)

    def body(*refs):
        for cp in copies_fn(refs[:nb], refs[nb], refs[nb + 1]):
            cp.wait_send()
            cp.wait_recv()

    res = pl.pallas_call(
        body, name=name, out_shape=tuple(pltpu.HBM(a.shape, a.dtype) for a in bufs),
        in_specs=[_HBM] * nb + [_SEM, _SEM, pl.BlockSpec(memory_space=pl.ANY)], out_specs=tuple([_HBM] * nb),
        input_output_aliases={i: i for i in range(nb)},
        compiler_params=pltpu.CompilerParams(has_side_effects=_EFFECT),
    )(*bufs, send_sems, recv_sems, after)
    return list(res)


def _split_relay(name, bufs, send_sems, recv_sems, after, wait_fn, n_sems, start_fn):
    nb = len(bufs)

    def body(*refs):
        for cp in wait_fn(refs[:nb], refs[nb], refs[nb + 1]):
            cp.wait_send()
            cp.wait_recv()
        for cp in start_fn(refs[:nb], refs[nb + 3], refs[nb + 4]):
            cp.start()
        refs[-1][...] = jnp.zeros_like(refs[-1])

    sems = pltpu.SemaphoreType.DMA((n_sems,))
    res = pl.pallas_call(
        body, name=name, out_shape=(sems, sems, *[pltpu.HBM(a.shape, a.dtype) for a in bufs], jax.ShapeDtypeStruct((8, LANES), F32)),
        in_specs=[_HBM] * nb + [_SEM, _SEM, pl.BlockSpec(memory_space=pl.ANY)],
        out_specs=(_SEM, _SEM, *[_HBM] * nb, pl.BlockSpec(memory_space=pltpu.VMEM)),
        input_output_aliases={i: 2 + i for i in range(nb)},
        compiler_params=pltpu.CompilerParams(has_side_effects=_EFFECT),
    )(*bufs, send_sems, recv_sems, after)
    return res[0], res[1], list(res[2:2 + nb]), res[-1]


N_CHIP = 4


def _chip_flip(x, y, k):
    return (1 - x if k & 2 else x), (1 - y if k & 1 else y)


def _gather_first_copies(n):
    def copies(bufs, send_sems, recv_sems):
        x, y, c, me = _position()
        out = []
        for w in range(n):
            for k in range(N_CHIP):
                to = (x, y, 1 - c) if k == 0 else (*_chip_flip(x, y, k), c)
                out.append(pltpu.make_async_remote_copy(
                    src_ref=bufs[w], dst_ref=bufs[n + w].at[me], send_sem=send_sems.at[w * N_CHIP + k],
                    recv_sem=recv_sems.at[w * N_CHIP + k], device_id=to, device_id_type=MESH))
        return out
    return copies


def _gather_relay_copies(n):
    def copies(bufs, send_sems, recv_sems):
        x, y, c, _ = _position()
        out = []
        for w in range(n):
            for k in range(1, N_CHIP):
                px, py = _chip_flip(x, y, k)
                blk = bufs[n + w].at[4 * px + 2 * py + c]
                out.append(pltpu.make_async_remote_copy(
                    src_ref=blk, dst_ref=blk, send_sem=send_sems.at[w * (N_CHIP - 1) + k - 1],
                    recv_sem=recv_sems.at[w * (N_CHIP - 1) + k - 1], device_id=(x, y, 1 - c), device_id_type=MESH))
        return out
    return copies


def _small_gather_copies(bufs, send_sems, recv_sems):
    x, y, c, me = _position()
    return [pltpu.make_async_remote_copy(src_ref=bufs[0], dst_ref=bufs[1].at[me], send_sem=send_sems.at[k - 1], recv_sem=recv_sems.at[k - 1],
                                         device_id=_flip(x, y, c, k), device_id_type=MESH) for k in range(1, N_DEV)]


def _forward_first_copies(n):
    def copies(bufs, send_sems, recv_sems):
        x, y, c, me = _position()
        out = []
        for w in range(n):
            for k, to in enumerate([(x, y, 1 - c), (1 - x, y, c), (x, 1 - y, c)]):
                out.append(pltpu.make_async_remote_copy(
                    src_ref=bufs[w], dst_ref=bufs[n + w].at[me], send_sem=send_sems.at[w * 3 + k],
                    recv_sem=recv_sems.at[w * 3 + k], device_id=to, device_id_type=MESH))
        return out
    return copies


def _forward_second_copies(n):
    def copies(bufs, send_sems, recv_sems):
        x, y, c, _ = _position()
        out = []
        for w in range(n):
            half = bufs[n + w].shape[1] // 2
            for k, (src_chip, rows, to) in enumerate([((1 - x, y), pl.ds(0, half), (x, 1 - y, c)), ((x, 1 - y), pl.ds(half, half), (1 - x, y, c))]):
                blk = bufs[n + w].at[4 * src_chip[0] + 2 * src_chip[1] + c, rows]
                out.append(pltpu.make_async_remote_copy(src_ref=blk, dst_ref=blk, send_sem=send_sems.at[w * 4 + k],
                                                        recv_sem=recv_sems.at[w * 4 + k], device_id=to, device_id_type=MESH))
            for k, (px, py) in enumerate([(1 - x, y), (x, 1 - y)]):
                blk = bufs[n + w].at[4 * px + 2 * py + c]
                out.append(pltpu.make_async_remote_copy(src_ref=blk, dst_ref=blk, send_sem=send_sems.at[w * 4 + 2 + k],
                                                        recv_sem=recv_sems.at[w * 4 + 2 + k], device_id=(x, y, 1 - c), device_id_type=MESH))
        return out
    return copies


def _forward_third_copies(n):
    def copies(bufs, send_sems, recv_sems):
        x, y, c, _ = _position()
        out = []
        for w in range(n):
            blk = bufs[n + w].at[4 * (1 - x) + 2 * (1 - y) + c]
            out.append(pltpu.make_async_remote_copy(src_ref=blk, dst_ref=blk, send_sem=send_sems.at[w], recv_sem=recv_sems.at[w],
                                                    device_id=(x, y, 1 - c), device_id_type=MESH))
        return out
    return copies


def _to_sibling_copies(n):
    def copies(bufs, send_sems, recv_sems):
        x, y, c, _ = _position()
        out = []
        for w in range(n):
            for q in range(N_CHIP):
                out.append(pltpu.make_async_remote_copy(
                    src_ref=bufs[w].at[2 * q + 1 - c], dst_ref=bufs[n + w].at[q], send_sem=send_sems.at[w * N_CHIP + q],
                    recv_sem=recv_sems.at[w * N_CHIP + q], device_id=(x, y, 1 - c), device_id_type=MESH))
        return out
    return copies


def _to_owner_copies(n):
    def copies(bufs, send_sems, recv_sems):
        x, y, c, _ = _position()
        out = []
        for w in range(n):
            for k in range(1, N_CHIP):
                px, py = (1 - x if k & 2 else x), (1 - y if k & 1 else y)
                out.append(pltpu.make_async_remote_copy(
                    src_ref=bufs[w].at[2 * px + py], dst_ref=bufs[n + w].at[k - 1], send_sem=send_sems.at[w * (N_CHIP - 1) + k - 1],
                    recv_sem=recv_sems.at[w * (N_CHIP - 1) + k - 1], device_id=(px, py, c), device_id_type=MESH))
        return out
    return copies


def _chip_sum(name, stack, landed, c_idx):
    _, R, C = stack.shape
    tr = _tile(R, max(BF16_ROWS, STREAM_TILE // C), BF16_ROWS)

    def body(c_ref, a_ref, b_ref, o_ref):
        o_ref[...] = (a_ref[...].astype(F32) + b_ref[...].astype(F32)).astype(o_ref.dtype)

    return pl.pallas_call(
        body, name=name,
        grid_spec=pltpu.PrefetchScalarGridSpec(
            num_scalar_prefetch=1, grid=(N_CHIP, R // tr),
            in_specs=[pl.BlockSpec((None, tr, C), lambda q, i, c_ref: (2 * q + c_ref[0], i, 0)),
                      pl.BlockSpec((None, tr, C), lambda q, i, c_ref: (q, i, 0))],
            out_specs=pl.BlockSpec((None, tr, C), lambda q, i, c_ref: (q, i, 0))),
        out_shape=jax.ShapeDtypeStruct((N_CHIP, R, C), stack.dtype),
        compiler_params=_cparams(("parallel", "parallel")))(c_idx, stack, landed)


def _ada_mod(c16, w):
    _, D = c16.shape
    n = w.shape[1]
    tk = _tile(D, 512)
    nk = D // tk

    def body(c_ref, w_ref, o_ref, ca_ref):
        @pl.when(pl.program_id(0) == 0)
        def _():
            o_ref[...] = jnp.zeros_like(o_ref)

        cv = c_ref[...]
        ca = cv * _sigmoid(cv)
        ca_ref[...] = ca
        o_ref[...] += _dot(ca, w_ref[...])

    return pl.pallas_call(
        body, name="ada_mod", grid=(nk,),
        in_specs=[pl.BlockSpec((16, tk), lambda k: (0, k)), pl.BlockSpec((tk, n), lambda k: (k, 0))],
        out_specs=[pl.BlockSpec((16, n), lambda k: (0, 0)), pl.BlockSpec((16, tk), lambda k: (0, k))],
        out_shape=[jax.ShapeDtypeStruct((16, n), F32), jax.ShapeDtypeStruct((16, D), F32)],
        compiler_params=_cparams(("arbitrary",)))(c16, w)


def _cast_shard(name, wf, slot):
    r, c = wf.shape
    tr = _tile(r, max(BF16_ROWS, STREAM_TILE // c), BF16_ROWS)

    def body(slot_ref, w_ref, s_ref, g_ref):
        v = w_ref[...].astype(BF16)
        s_ref[...] = v
        g_ref[...] = v

    return pl.pallas_call(
        body, name=name,
        grid_spec=pltpu.PrefetchScalarGridSpec(
            num_scalar_prefetch=1, grid=(r // tr,), in_specs=[pl.BlockSpec((tr, c), lambda i, s: (i, 0))],
            out_specs=[pl.BlockSpec((tr, c), lambda i, s: (i, 0)), pl.BlockSpec((None, tr, c), lambda i, s: (s[0], i, 0))]),
        out_shape=[jax.ShapeDtypeStruct((r, c), BF16), jax.ShapeDtypeStruct((N_DEV, r, c), BF16)],
        compiler_params=_cparams(("parallel",)))(slot, wf)


def _adam_math(w, g, m, v):
    m2 = ADAM_B1 * m + (1.0 - ADAM_B1) * g
    v2 = ADAM_B2 * v + (1.0 - ADAM_B2) * (g * g)
    m_hat = m2 / (1.0 - ADAM_B1 ** ADAM_STEP)
    v_hat = v2 / (1.0 - ADAM_B2 ** ADAM_STEP)
    delta = -ADAM_LR * (m_hat / (jnp.sqrt(v_hat) + ADAM_EPS) + ADAM_WD * w)
    return delta, m2, v2


def _adamw(name, w, m, v, own, own_slot, parts=(), row0=0, into=None):
    R, C = w.shape
    Rp = own.shape[1]
    tr = _tile(Rp, max(BF16_ROWS, ADAMW_TILE // C), BF16_ROWS)
    off = row0 // tr
    n_p = len(parts)
    held = [] if into is None else list(into)

    def body(slot_ref, *refs):
        w_ref, m_ref, v_ref, own_ref = refs[:4]
        g_ref, d_ref, m2_ref, v2_ref = refs[4 + n_p + len(held):]
        g = own_ref[...].astype(F32)
        for p_ref in refs[4:4 + n_p]:
            for s in range(p_ref.shape[0]):
                g = g + p_ref[s].astype(F32)
        delta, m2, v2 = _adam_math(w_ref[...], g, m_ref[...], v_ref[...])
        g_ref[...] = g
        d_ref[...] = delta
        m2_ref[...] = m2
        v2_ref[...] = v2

    blk = pl.BlockSpec((tr, C), lambda i, s: (i + off, 0))
    out = jax.ShapeDtypeStruct((R, C), F32)
    return pl.pallas_call(
        body, name=name,
        grid_spec=pltpu.PrefetchScalarGridSpec(
            num_scalar_prefetch=1, grid=(Rp // tr,),
            in_specs=[blk, blk, blk, pl.BlockSpec((None, tr, C), lambda i, s: (s[0], i, 0))]
            + [pl.BlockSpec((a.shape[0], tr, C), lambda i, s: (0, i, 0)) for a in parts]
            + [pl.BlockSpec(memory_space=pl.ANY)] * len(held),
            out_specs=[blk] * 4),
        out_shape=[out] * 4, input_output_aliases={5 + n_p + i: i for i in range(len(held))},
        compiler_params=_cparams(("parallel",)))(own_slot, w, m, v, own, *parts, *held)


def _small_update(gathered, w, m, v, after, rows):
    _, R, L = gathered.shape
    rs = w.shape[0]
    n = len(rows)
    assert all(r % 8 == 0 for r in rows) and sum(rows) <= rs and rs + 8 <= R

    def body(p_ref, w_ref, m_ref, v_ref, after_ref, *outs):
        g = p_ref[0]
        for p in range(1, N_DEV):
            g = g + p_ref[p]
        kinds = (g,) + _adam_math(w_ref[...], g[0:rs, :], m_ref[...], v_ref[...])
        at = 0
        for k, r in enumerate(rows):
            for idx, val in enumerate(kinds):
                outs[idx * n + k][...] = val[at:at + r, :]
            at += r
        outs[4 * n][...] = g[at:at + 8, :]

    vm = pl.BlockSpec(memory_space=pltpu.VMEM)
    shapes = [jax.ShapeDtypeStruct((r, L), F32) for _ in range(4) for r in rows] + [jax.ShapeDtypeStruct((8, L), F32)]
    return pl.pallas_call(body, name="small_update", in_specs=[vm] * 4 + [pl.BlockSpec(memory_space=pl.ANY)], out_specs=[vm] * len(shapes),
                          out_shape=shapes, compiler_params=pltpu.CompilerParams(vmem_limit_bytes=VMEM_LIMIT))(gathered, w, m, v, after)


class _Fetched(dict):
    def __init__(self, fetch):
        super().__init__()
        self.fetch = fetch

    def first(self, key, after):
        self[key] = self.fetch(key, after)
        return self[key]


def _local_step(x, tgt, mod, p, fetch, F, scatter=None):
    S, D = x.shape
    GW, HW = p["ln_g"].shape[1], p["hg_ng"].shape[1]
    G, T, _ = p["ws"].shape
    w = _Fetched(fetch)
    INW = 2 * GW + 4 * HW + 2 * D
    in_loc, br_loc, fi_loc = INW // N_DEV, D // N_DEV, 2 * F // N_DEV
    assert GW == HW and F % fi_loc == 0
    sh1, sc1, gt1, sh2, sc2, gt2 = (mod[:, k * D:(k + 1) * D] for k in range(6))
    bsb = jnp.broadcast_to(p["bs"][:, :, None], (G, T, GW // G))

    tm = _tile(S, 1024, 16)
    tmh = _tile(S, 512, 16)
    tn_in = _tile(in_loc, 1280)
    tn_d = _tile(D, 512)
    tn_br = _tile(br_loc, 512)
    tk_s = S
    tm_w = _tile(D, 1024)
    g_off = 2 * GW + 4 * HW

    h1 = _norm_mod("norm1", x, p["norm1_g"], sc1, sh1)
    z = _mm_nn_stacked("proj_in", h1, w.first("in", h1), tm=tm, tn=tn_in, tk=D)[0]
    ya = _gmlp_fwd(z, p["ln_g"], p["ln_b"], p["ws"], bsb, GW)
    yb, o_hg, states = _hg_fwd(z, p["hg_lb"], p["hg_ng"], HW)
    flat = {k: jnp.swapaxes(w.first(k, yb), 0, 1).reshape(GW, D) for k in ("bg", "bh")}
    tn_f = _tile(D, 1024)
    t_fi = w.first("fi_early", ya)

    def gates(ga_ref, gb_ref, ba_ref, bb_ref):
        return _sigmoid(ga_ref[...] + ba_ref[...]), _sigmoid(gb_ref[...] + bb_ref[...])

    def gate_specs(tn_, tm_=tm):
        o1, o2 = g_off // tn_, (g_off + D) // tn_
        return [pl.BlockSpec((tm_, tn_), lambda i, j, k: (i, o1 + j)), pl.BlockSpec((tm_, tn_), lambda i, j, k: (i, o2 + j)),
                pl.BlockSpec((1, tn_), lambda i, j, k: (0, j)), pl.BlockSpec((1, tn_), lambda i, j, k: (0, D // tn_ + j))]

    def merge_ep(acc, ex, outs):
        ga, gb = gates(*ex[2:6])
        acc_b = lax.dot_general(ex[0][...], ex[1][...], _NN, preferred_element_type=F32)
        outs[0][...] = acc.astype(BF16)
        outs[1][...] = acc_b.astype(BF16)
        outs[2][...] = (ga * acc + gb * acc_b).astype(BF16)

    tile_o = pl.BlockSpec((tmh, tn_f), lambda i, j, k: (i, j))
    rows, cols = pl.BlockSpec((tmh, GW), lambda i, j, k: (i, 0)), pl.BlockSpec((GW, tn_f), lambda i, j, k: (0, j))
    pa, pb, y = _matmul(
        "branch_merge", ya, flat["bg"], dims=_NN, grid_mnk=(S // tmh, D // tn_f, 1), tiles=(tmh, tn_f), a_spec=rows, b_spec=cols,
        extras=[yb, flat["bh"], z, z, p["b_gate"], p["b_gate"]], extra_specs=[rows, cols, *gate_specs(tn_f, tmh)],
        out_shapes=[jax.ShapeDtypeStruct((S, D), BF16)] * 3, out_specs=[tile_o] * 3, epilogue=merge_ep, after=t_fi)

    def resid_ep(acc, ex, outs):
        outs[0][...] = acc.astype(BF16)
        outs[1][...] = ex[0][...] + ex[1][...] * acc

    def resid_mm(name, a, b, res, gt, tm_, tn_):
        K = a.shape[1]
        t_o = pl.BlockSpec((tm_, tn_), lambda i, j, k: (i, j))
        return _matmul(
            name, a, b, dims=_NN, grid_mnk=(S // tm_, D // tn_, 1), tiles=(tm_, tn_),
            a_spec=pl.BlockSpec((tm_, K), lambda i, j, k: (i, 0)), b_spec=pl.BlockSpec((K, tn_), lambda i, j, k: (0, j)),
            extras=[res, gt], extra_specs=[t_o, pl.BlockSpec((1, tn_), lambda i, j, k: (0, j))],
            out_shapes=[jax.ShapeDtypeStruct((S, D), BF16), jax.ShapeDtypeStruct((S, D), F32)], out_specs=[t_o, t_o], epilogue=resid_ep)

    o1, xm = resid_mm("proj_out", y, w.first("out", z), x, gt1, tm, tn_f)
    h2 = _norm_mod("norm2", xm, p["norm2_g"], sc2, sh2)
    hf, hf_fac = _ffn_in_swiglu(h2, w.first("fi", h2))
    o2, x3 = resid_mm("ffn_out", hf, w.first("fo", hf), xm, gt2, tm, _tile(D, 256))
    dx3, do2, vec_l = _loss_head(x3, tgt, p["final_g"], o2, gt2)

    nf = F // fi_loc

    def dswiglu_ep(acc, ex, outs):
        outs[0][0] = (acc * ex[0][0].astype(F32)).astype(BF16)
        outs[0][1] = (acc * ex[0][1].astype(F32)).astype(BF16)

    pair = pl.BlockSpec((2, tmh, fi_loc), lambda i, j, k: (0, j, i))
    dab = _matmul(
        "ffn_out_dx", do2, w["fo"], dims=_NT, grid_mnk=(nf, S // tmh, 1), tiles=(tmh, fi_loc),
        a_spec=pl.BlockSpec((tmh, D), lambda i, j, k: (j, 0)), b_spec=pl.BlockSpec((fi_loc, D), lambda i, j, k: (i, 0)),
        extras=[hf_fac], extra_specs=[pair], out_shapes=[jax.ShapeDtypeStruct((2, S, F), BF16)], out_specs=[pair],
        epilogue=dswiglu_ep)[0]
    start = (lambda name, grads: scatter[0](name, grads)) if scatter is not None else (lambda name, grads: None)
    push = (lambda name, after: scatter[1](name, after)) if scatter is not None else (lambda name, after: None)

    def zero(token):
        return 0.0 if token is None else token[0:1, 0:1]

    tm_f = _tile(F, 512)
    g_fo = _mm_tn("ffn_out_dw", hf, do2, pl.BlockSpec((tk_s, D), lambda i, j, k: (k, j)), Mo=F, No=D, S=S, tm=tm_f, tn=D, tk=tk_s)
    g_fi = _mm_tn("ffn_in_dw", h2, dab, pl.BlockSpec((None, tk_s, fi_loc), lambda i, j, k: (j // nf, k, j % nf)),
                  Mo=D, No=2 * F, S=S, tm=D, tn=fi_loc, tk=tk_s, stacked_nloc=fi_loc, after=g_fo)
    t_ffn = start("scatter_ffn", dict(fo=g_fo, fi=g_fi))
    dh2 = _mm_nt_stacked("ffn_in_dx", pl.BlockSpec((None, tm, fi_loc), lambda i, j, k: (k // nf, i, k % nf)), dab, w["fi"],
                         M=S, tm=tm, tn=D, tk=fi_loc, after=t_ffn)
    dxm, vec2, do1 = _norm_mod_bwd("norm2_bwd", dh2, xm, p["norm2_g"], sc2, dx3, o1, gt1)
    t_ffn = push("scatter_ffn", dxm)

    def dmerge_ep(acc, ex, outs):
        ga, gb = gates(*ex[2:6])
        outs[0][...] = (acc * ga).astype(BF16)
        outs[1][...] = (acc * gb).astype(BF16)
        outs[2][0] = (acc * ex[0][...] * ga * (1.0 - ga)).astype(BF16)
        outs[2][1] = (acc * ex[1][...] * gb * (1.0 - gb)).astype(BF16)

    t_o = pl.BlockSpec((tm, tn_d), lambda i, j, k: (i, j))
    dpa, dpb, dg2 = _matmul(
        "proj_out_dx", do1, w["out"], dims=_NT, grid_mnk=(S // tm, D // tn_d, 1), tiles=(tm, tn_d),
        a_spec=pl.BlockSpec((tm, D), lambda i, j, k: (i, 0)), b_spec=pl.BlockSpec((tn_d, D), lambda i, j, k: (j, 0)),
        extras=[pa, pb, z, z, p["b_gate"], p["b_gate"]], extra_specs=[t_o, t_o, *gate_specs(tn_d)],
        out_shapes=[jax.ShapeDtypeStruct((S, D), BF16), jax.ShapeDtypeStruct((S, D), BF16), jax.ShapeDtypeStruct((2, S, D), BF16)],
        out_specs=[t_o, t_o, pl.BlockSpec((2, tm, tn_d), lambda i, j, k: (0, i, j))], epilogue=dmerge_ep, after=t_ffn)
    g_out = _mm_tn("proj_out_dw", y, do1, pl.BlockSpec((tk_s, D), lambda i, j, k: (k, j)), Mo=D, No=D, S=S, tm=tn_d, tn=D, tk=tk_s)
    tn_g = _tile(GW, 512)
    b_br = pl.BlockSpec((tk_s, br_loc), lambda i, j, k: (k, j))
    tm_b = _tile(GW, 1024)
    g_bg = _mm_tn("branch_gmlp_dw", ya, dpa, b_br, Mo=GW, No=D, S=S, tm=tm_b, tn=br_loc, tk=tk_s, stacked_nloc=br_loc)
    g_bh = _mm_tn("branch_hg_dw", yb, dpb, b_br, Mo=HW, No=D, S=S, tm=tm_b, tn=br_loc, tk=tk_s, stacked_nloc=br_loc)
    t_mix = start("scatter_mixer", dict(out=g_out, bg=g_bg, bh=g_bh))
    def branch_dx(name, dp, w_flat):
        return _matmul(
            name, dp, w_flat, dims=_NT, grid_mnk=(S // tm, GW // tn_g, 1), tiles=(tm, tn_g),
            a_spec=pl.BlockSpec((tm, D), lambda i, j, k: (i, 0)), b_spec=pl.BlockSpec((tn_g, D), lambda i, j, k: (j, 0)),
            out_shapes=[jax.ShapeDtypeStruct((S, GW), F32)], out_specs=[pl.BlockSpec((tm, tn_g), lambda i, j, k: (i, j))],
            epilogue=_store(F32), after=t_mix)[0]

    dya = branch_dx("branch_gmlp_dx", dpa, flat["bg"])
    dyb = branch_dx("branch_hg_dx", dpb, flat["bh"])
    dz_gmlp, dln, dws, dbs = _gmlp_bwd(z, dya, p["ln_g"], p["ln_b"], p["ws"], bsb, GW)
    t_mix = push("scatter_mixer", dz_gmlp)
    dz, dng, dhlb, db_gate = _hg_bwd(z, o_hg, states, dyb, p["hg_lb"], p["hg_ng"] + zero(t_mix), HW, dz_gmlp, dg2)
    half = D // 2
    tm_h = _tile(half, 1024)
    g_in = []
    t_in = None
    for hname, h in (("a", 0), ("b", 1)):
        g_in.append(_mm_tn("proj_in_dw_" + hname, h1, dz, pl.BlockSpec((tk_s, in_loc), lambda i, j, k: (k, j)), Mo=half, No=INW, S=S,
                           tm=tm_h, tn=in_loc, tk=tk_s, stacked_nloc=in_loc, after=t_in, a_off=h * (half // tm_h)))
        t_in = start("scatter_proj_in_" + hname, {"w_in_" + hname: g_in[-1]})
    t_in = push("scatter_proj_in_a", t_in)
    dh1 = _mm_nt_stacked("proj_in_dx", pl.BlockSpec((tm, in_loc), lambda i, j, k: (i, k)), dz, w["in"], M=S, tm=tm, tn=D, tk=in_loc,
                         after=t_in)
    dx, vec1 = _norm_mod_bwd("norm1_bwd", dh1, x, p["norm1_g"], sc1, dxm)

    dmod = jnp.concatenate([vec1[0:1], vec1[1:2], vec2[3:4], vec2[0:1], vec2[1:2], vec_l[2:3]], axis=1)
    small = dict(norm1_g=vec1[2:3], b_gate=db_gate.reshape(1, 2 * D), ln_g=dln[0:1], ln_b=dln[1:2], ws=dws, bs=dbs.reshape(G, T),
                 hg_lb=dhlb, hg_ng=dng[0:1], norm2_g=vec2[2:3], final_g=vec_l[1:2], loss=vec_l[0:1, 0:LANES])
    big = dict(w_in_a=g_in[0], w_in_b=g_in[1], bg=g_bg, bh=g_bh, out=g_out, fi=g_fi, fo=g_fo)
    return dx, big, small, dmod


_SMALL = ("b_ada", "norm1_g", "b_gate", "ln_g", "ln_b", "ws", "bs", "hg_lb", "hg_ng", "norm2_g", "final_g")


def _pack(parts, rows_mult=8):
    flat = [a.reshape(-1) for a in parts]
    offs, n = [], 0
    for a in flat:
        offs.append(n)
        n += a.shape[0]
    pad = (-n) % (LANES * rows_mult)
    if pad:
        flat.append(jnp.zeros((pad,), F32))
    return jnp.concatenate(flat).reshape(-1, LANES), offs


def kernel(x, c, w_ada, b_ada, norm1_g, w_in, b_gate, gmlp_ln_g, gmlp_ln_b, gmlp_ws, gmlp_bs, hg_lb, hg_norm_g, w_branch_gmlp, w_branch_hg, w_out, norm2_g, w_ffn_in, w_ffn_out, final_norm_g, loss_target, m_w_ada, m_b_ada, m_norm1_g, m_w_in, m_b_gate, m_gmlp_ln_g, m_gmlp_ln_b, m_gmlp_ws, m_gmlp_bs, m_hg_lb, m_hg_norm_g, m_w_branch_gmlp, m_w_branch_hg, m_w_out, m_norm2_g, m_w_ffn_in, m_w_ffn_out, m_final_norm_g, v_w_ada, v_b_ada, v_norm1_g, v_w_in, v_b_gate, v_gmlp_ln_g, v_gmlp_ln_b, v_gmlp_ws, v_gmlp_bs, v_hg_lb, v_hg_norm_g, v_w_branch_gmlp, v_w_branch_hg, v_w_out, v_norm2_g, v_w_ffn_in, v_w_ffn_out, v_final_norm_g):
    S, D = x.shape[1], x.shape[2]
    ada_loc = w_ada.shape[2]
    me = 4 * lax.axis_index("x") + 2 * lax.axis_index("y") + lax.axis_index("c")
    me_idx = me.astype(jnp.int32).reshape(1)

    def empty_hbm(shape, dtype):
        return pltpu.with_memory_space_constraint(lax.empty(shape, dtype), pltpu.HBM)

    groups = dict(gather_in=dict(keys=["in"], src=[w_in], forward=True),
                  gather_mixer=dict(keys=["bg", "bh", "out"], src=[w_branch_gmlp, w_branch_hg, w_out], forward=False),
                  gather_ffn_in=dict(keys=["fi"], src=[w_ffn_in], forward=True),
                  gather_ffn_out=dict(keys=["fo"], src=[w_ffn_out], forward=False))
    group_of = {k: gname for gname, g in groups.items() for k in g["keys"]}

    def first_hop(gname, after):
        g = groups[gname]
        n = len(g["keys"])
        cast = [_cast_shard(f"{gname}_cast_{k}", a[0], me_idx) for k, a in zip(g["keys"], g["src"])]
        shards, outs = [s for s, _ in cast], [o for _, o in cast]
        if g["forward"]:
            *g["hop"], token = _split_start(gname + "_hop1", shards + outs, n * 3, _forward_first_copies(n), after=after)
        else:
            *g["hop"], token = _split_start(gname + "_hop1", shards + outs, n * N_CHIP, _gather_first_copies(n), after=after)
        return token

    def second_hop(gname, after):
        g = groups[gname]
        n = len(g["keys"])
        *g["hop"], token = _split_relay(gname + "_hop2", g["hop"][2], g["hop"][0], g["hop"][1], after,
                                        _forward_first_copies(n), n * 4, _forward_second_copies(n))
        return token

    def finish(gname, after):
        g = groups[gname]
        n = len(g["keys"])
        send_sems, recv_sems, bufs = g["hop"]
        if g["forward"]:
            send_sems, recv_sems, bufs, _ = _split_relay(gname + "_hop3", bufs, send_sems, recv_sems, after,
                                                         _forward_second_copies(n), n, _forward_third_copies(n))
            bufs = _split_wait(gname + "_wait", bufs, send_sems, recv_sems, after, _forward_third_copies(n))
        else:
            send_sems, recv_sems, bufs, _ = _split_relay(gname + "_relay", bufs, send_sems, recv_sems, after,
                                                         _gather_first_copies(n), n * (N_CHIP - 1), _gather_relay_copies(n))
            bufs = _split_wait(gname + "_wait", bufs, send_sems, recv_sems, after, _gather_relay_copies(n))
        g["done"] = dict(zip(g["keys"], bufs[n:]))

    c_all = _allgather_small("gather_c", c.reshape(D // LANES, LANES)).reshape(N_DEV, D)
    token = first_hop("gather_in", c_all)
    mod_cols, c_act = _ada_mod(jnp.pad(c_all, ((0, 16 - N_DEV), (0, 0))) + token[0:1, 0:1], w_ada[0])
    mod_vec = mod_cols[:N_DEV].reshape(-1, LANES)
    mg_send, mg_recv, mg_bufs, token = _split_start(
        "gather_mod_start", [mod_vec, lax.dynamic_update_slice(lax.empty((N_DEV, *mod_vec.shape), F32), mod_vec[None], (me, 0, 0))],
        N_DEV - 1, _small_gather_copies)
    token = second_hop("gather_in", token)
    token = first_hop("gather_ffn_in", first_hop("gather_mixer", token))
    mod_all = _split_wait("gather_mod_wait", mg_bufs, mg_send, mg_recv, token, _small_gather_copies)[1].reshape(N_DEV, N_DEV, ada_loc)
    mod = lax.dynamic_index_in_dim(mod_all, me, axis=1, keepdims=False).reshape(1, N_DEV * ada_loc) + b_ada

    def fetch(key, after):
        if key == "in":
            finish("gather_in", after)
        elif key == "fi_early":
            return first_hop("gather_ffn_out", second_hop("gather_ffn_in", after))
        elif "done" not in groups[group_of[key]]:
            finish(group_of[key], after)
        arr = groups[group_of[key]]["done"][key]
        return arr.reshape(-1, D) if key in ("out", "fo") else arr

    p = dict(norm1_g=norm1_g, b_gate=b_gate, ln_g=gmlp_ln_g, ln_b=gmlp_ln_b, ws=gmlp_ws[0], bs=gmlp_bs[0], hg_lb=hg_lb,
             hg_ng=hg_norm_g, norm2_g=norm2_g, final_g=final_norm_g.reshape(1, D))

    in_flight = {}
    c_idx = lax.axis_index("c").astype(jnp.int32).reshape(1)
    chip_idx = (2 * lax.axis_index("x") + lax.axis_index("y")).astype(jnp.int32).reshape(1)

    def scatter_start(name, grads):
        keys = list(grads)
        n = len(keys)
        stacks = [grads[k].reshape(N_DEV, -1, grads[k].shape[-1]) for k in keys]
        lands = [empty_hbm((N_CHIP, *g.shape[1:]), g.dtype) for g in stacks]
        send_sems, recv_sems, bufs, token = _split_start(name + "_d2d", stacks + lands, n * N_CHIP, _to_sibling_copies(n))
        in_flight[name] = dict(keys=keys, stage1=(send_sems, recv_sems, bufs))
        return token

    def scatter_push(name, after):
        f = in_flight[name]
        n = len(f["keys"])
        send_sems, recv_sems, bufs = f["stage1"]
        bufs = _split_wait(name + "_d2d_wait", bufs, send_sems, recv_sems, after, _to_sibling_copies(n))
        sums = [_chip_sum(f"{name}_sum_{k}", bufs[i], bufs[n + i], c_idx) for i, k in enumerate(f["keys"])]
        lands = [empty_hbm((N_CHIP - 1, *s.shape[1:]), s.dtype) for s in sums]
        send_sems, recv_sems, bufs, token = _split_start(name + "_ici", sums + lands, n * (N_CHIP - 1), _to_owner_copies(n))
        f["stage2"] = (send_sems, recv_sems, bufs)
        return token

    grad_x, _, small, dmod = _local_step(x[0], loss_target[0], mod, p, fetch, w_ffn_out.shape[1] * N_DEV, (scatter_start, scatter_push))

    small["b_ada"] = dmod
    packed, offs = _pack([small[k] for k in _SMALL] + [small["loss"]])
    sg_send, sg_recv, sg_bufs, t_tail = _split_start(
        "gather_small_start", [packed, lax.dynamic_update_slice(lax.empty((N_DEV, *packed.shape), F32), packed[None], (me, 0, 0))],
        N_DEV - 1, _small_gather_copies)
    t_tail = scatter_push("scatter_proj_in_b", t_tail)
    big_w = dict(w_in=(w_in, m_w_in, v_w_in, "w_in"), bg=(w_branch_gmlp, m_w_branch_gmlp, v_w_branch_gmlp, "w_branch_gmlp"),
                 bh=(w_branch_hg, m_w_branch_hg, v_w_branch_hg, "w_branch_hg"), out=(w_out, m_w_out, v_w_out, "w_out"),
                 fi=(w_ffn_in, m_w_ffn_in, v_w_ffn_in, "w_ffn_in"), fo=(w_ffn_out, m_w_ffn_out, v_w_ffn_out, "w_ffn_out"))
    upd = {}

    def land_and_update(name, after):
        keys = in_flight[name]["keys"]
        n = len(keys)
        send_sems, recv_sems, bufs = in_flight[name]["stage2"]
        bufs = _split_wait(name + "_ici_wait", bufs, send_sems, recv_sems, after, _to_owner_copies(n))
        for i, k in enumerate(keys):
            if k in big_w:
                wt, mt, vt, out_name = big_w[k]
                upd[out_name] = _adamw("adamw_" + out_name, wt[0], mt[0], vt[0], bufs[i], chip_idx, [bufs[n + i]])
            else:
                wt, mt, vt, out_name = big_w["w_in"]
                upd[out_name] = _adamw("adamw_" + k, wt[0], mt[0], vt[0], bufs[i], chip_idx, [bufs[n + i]],
                                       row0=0 if k == "w_in_a" else bufs[i].shape[1], into=upd.get(out_name))
            after = upd[out_name][1]
        return after

    after = land_and_update("scatter_mixer", land_and_update("scatter_ffn", t_tail))
    gathered = _split_wait("gather_small_wait", sg_bufs, sg_send, sg_recv, after, _small_gather_copies)[1]
    wp = dict(p, b_ada=b_ada)
    ms = dict(b_ada=m_b_ada, norm1_g=m_norm1_g, b_gate=m_b_gate, ln_g=m_gmlp_ln_g, ln_b=m_gmlp_ln_b, ws=m_gmlp_ws, bs=m_gmlp_bs,
              hg_lb=m_hg_lb, hg_ng=m_hg_norm_g, norm2_g=m_norm2_g, final_g=m_final_norm_g)
    vs = dict(b_ada=v_b_ada, norm1_g=v_norm1_g, b_gate=v_b_gate, ln_g=v_gmlp_ln_g, ln_b=v_gmlp_ln_b, ws=v_gmlp_ws, bs=v_gmlp_bs,
              hg_lb=v_hg_lb, hg_ng=v_hg_norm_g, norm2_g=v_norm2_g, final_g=v_final_norm_g)
    w_sm, _ = _pack([wp[k] for k in _SMALL])
    m_sm, _ = _pack([ms[k] for k in _SMALL])
    v_sm, _ = _pack([vs[k] for k in _SMALL])
    shapes = dict(b_ada=b_ada.shape, norm1_g=norm1_g.shape, b_gate=b_gate.shape, ln_g=gmlp_ln_g.shape, ln_b=gmlp_ln_b.shape,
                  ws=gmlp_ws.shape, bs=gmlp_bs.shape, hg_lb=hg_lb.shape, hg_ng=hg_norm_g.shape, norm2_g=norm2_g.shape,
                  final_g=final_norm_g.shape)
    sm_out = _small_update(gathered, w_sm, m_sm, v_sm, after, [math.prod(shapes[k]) // LANES for k in _SMALL])

    def unpack(idx, k):
        return sm_out[idx * len(_SMALL) + _SMALL.index(k)].reshape(shapes[k])

    loss = sm_out[-1][0, 0]

    assert offs[0] == 0 and ada_loc % LANES == 0
    dmod_loc = lax.dynamic_slice_in_dim(gathered, me * (ada_loc // LANES), ada_loc // LANES, axis=1).reshape(N_DEV, ada_loc)
    ca_t = jnp.pad(c_act[:N_DEV].T, ((0, 0), (0, LANES - N_DEV))).astype(BF16)
    dm_p = jnp.pad(dmod_loc, ((0, LANES - N_DEV), (0, 0))).astype(BF16)
    tm_a = _tile(D, 512)
    g_ada = _matmul(
        "ada_dw", ca_t, dm_p, dims=_NN, grid_mnk=(D // tm_a, 1, 1), tiles=(tm_a, ada_loc),
        a_spec=pl.BlockSpec((tm_a, LANES), lambda i, j, k: (i, 0)), b_spec=pl.BlockSpec((LANES, ada_loc), lambda i, j, k: (0, 0)),
        out_shapes=[jax.ShapeDtypeStruct((1, D, ada_loc), F32)], out_specs=[pl.BlockSpec((None, tm_a, ada_loc), lambda i, j, k: (0, i, 0))],
        epilogue=_store(F32))[0]
    upd["w_ada"] = _adamw("adamw_w_ada", w_ada[0], m_w_ada[0], v_w_ada[0], g_ada, jnp.zeros((1,), jnp.int32))
    land_and_update("scatter_proj_in_b", land_and_update("scatter_proj_in_a", upd["w_ada"][1]))

    order = ("w_ada", "b_ada", "norm1_g", "w_in", "b_gate", "ln_g", "ln_b", "ws", "bs", "hg_lb", "hg_ng", "w_branch_gmlp", "w_branch_hg",
             "w_out", "norm2_g", "w_ffn_in", "w_ffn_out", "final_g")
    outs = [loss, grad_x[None]]
    for idx in range(4):
        for k in order:
            outs.append(upd[k][idx][None] if k in upd else unpack(idx, k))
    return tuple(outs)
```

```python
import functools
import math

import jax
import jax.numpy as jnp
from jax import lax
from jax.experimental import pallas as pl
from jax.experimental.pallas import tpu as pltpu

F32 = jnp.float32
BF16 = jnp.bfloat16
N_DEV = 8
EPS = 1e-6
LANES = 128
HG_DK = 128
HG_CHUNK = 64
HG_MID = HG_CHUNK // 2 - 1
EXP_CLAMP = 80.0
VMEM_LIMIT = 48 * 1024 * 1024
BF16_ROWS = 16
STREAM_TILE = 1 << 20
ADAMW_TILE = 3 << 17
ADAM_LR, ADAM_B1, ADAM_B2, ADAM_EPS, ADAM_WD, ADAM_STEP = 0.001, 0.9, 0.999, 1e-08, 0.01, 10
MESH = pl.DeviceIdType.MESH

_NN = (((1,), (0,)), ((), ()))
_NT = (((1,), (1,)), ((), ()))
_TN = (((0,), (0,)), ((), ()))


def _dot(a, b, dims=_NN):
    return lax.dot_general(a.astype(BF16), b.astype(BF16), dims, preferred_element_type=F32)


def _tile(n, target, mult=LANES):
    best = None
    for t in range(mult, min(n, target) + 1, mult):
        if n % t == 0:
            best = t
    return n if best is None else best


def _cparams(sem):
    return pltpu.CompilerParams(dimension_semantics=sem, vmem_limit_bytes=VMEM_LIMIT)


def _sigmoid(x):
    return 1.0 / (1.0 + jnp.exp(-x))


def _gelu_parts(x):
    k0 = math.sqrt(2.0 / math.pi)
    x2 = x * x
    t = jnp.tanh(k0 * (x + 0.044715 * x * x2))
    g = 0.5 * x * (1.0 + t)
    dg = 0.5 * (1.0 + t) + 0.5 * x * (1.0 - t * t) * (k0 * (1.0 + 3.0 * 0.044715 * x2))
    return g, dg


def _split3(x):
    h = x.astype(BF16)
    r = x - h.astype(F32)
    m = r.astype(BF16)
    lo = (r - m.astype(F32)).astype(BF16)
    return h, m, lo


def _ones_dot(mat01, x):
    h, m, lo = _split3(x)
    d = functools.partial(lax.dot_general, dimension_numbers=_NN, preferred_element_type=F32)
    return d(mat01, h) + d(mat01, m) + d(mat01, lo)


def _matmul(name, a, b, *, dims, grid_mnk, tiles, a_spec, b_spec, extras=(), extra_specs=(), out_shapes, out_specs, epilogue, after=None,
            sem=None):
    gm, gn, nk = grid_mnk
    tm, tn = tiles
    n_ex, n_out = len(extras), len(out_shapes)
    held = [] if after is None else [after]

    def body(*refs):
        a_ref, b_ref = refs[0], refs[1]
        ex = refs[2:2 + n_ex]
        outs = refs[2 + n_ex + len(held):2 + n_ex + len(held) + n_out]
        more = () if sem is None else (pl.program_id(0) == 0,)
        if nk == 1:
            epilogue(lax.dot_general(a_ref[...], b_ref[...], dims, preferred_element_type=F32), ex, outs, *more)
            return
        acc = refs[-1]
        k = pl.program_id(2)

        @pl.when(k == 0)
        def _():
            acc[...] = jnp.zeros_like(acc)

        acc[...] += lax.dot_general(a_ref[...], b_ref[...], dims, preferred_element_type=F32)

        @pl.when(k == nk - 1)
        def _():
            epilogue(acc[...], ex, outs, *more)

    return pl.pallas_call(
        body, name=name, grid=(gm, gn, nk), in_specs=[a_spec, b_spec, *extra_specs] + [pl.BlockSpec(memory_space=pl.ANY)] * len(held),
        out_specs=list(out_specs), out_shape=list(out_shapes), scratch_shapes=[] if nk == 1 else [pltpu.VMEM((tm, tn), F32)],
        compiler_params=_cparams(sem or ("parallel", "parallel", "arbitrary")),
    )(a, b, *extras, *held)


def _store(dtype):
    def ep(acc, ex, outs):
        outs[0][...] = acc.astype(dtype)
    return ep


def _mm_nn_stacked(name, a, wg, *, tm, tn, tk, out_dtype=F32, extras=(), extra_specs=(), out_shapes=None, out_specs=None, epilogue=None,
                   after=None):
    M, K = a.shape
    _, _, nloc = wg.shape
    N = nloc * N_DEV
    q = nloc // tn
    if out_shapes is None:
        out_shapes = [jax.ShapeDtypeStruct((M, N), out_dtype)]
        out_specs = [pl.BlockSpec((tm, tn), lambda i, j, k: (i, j))]
        epilogue = _store(out_dtype)
    return _matmul(
        name, a, wg, dims=_NN, grid_mnk=(M // tm, N // tn, K // tk), tiles=(tm, tn),
        a_spec=pl.BlockSpec((tm, tk), lambda i, j, k: (i, k)),
        b_spec=pl.BlockSpec((None, tk, tn), lambda i, j, k: (j // q, k, j % q)),
        extras=extras, extra_specs=extra_specs, out_shapes=out_shapes, out_specs=out_specs, epilogue=epilogue, after=after)


def _mm_nt_stacked(name, a_spec, a, wg, *, M, tm, tn, tk, out_dtype=F32, after=None, extras=(), extra_specs=(), out_shapes=None,
                   out_specs=None, epilogue=None, sem=None):
    _, Kw, nloc = wg.shape
    q = nloc // tk
    single = out_shapes is None
    if single:
        out_shapes = [jax.ShapeDtypeStruct((M, Kw), out_dtype)]
        out_specs = [pl.BlockSpec((tm, tn), lambda i, j, k: (i, j))]
        epilogue = _store(out_dtype)
    res = _matmul(
        name, a, wg, dims=_NT, grid_mnk=(M // tm, Kw // tn, (nloc * N_DEV) // tk), tiles=(tm, tn),
        a_spec=a_spec, b_spec=pl.BlockSpec((None, tn, tk), lambda i, j, k: (k // q, j, k % q)),
        extras=extras, extra_specs=extra_specs, out_shapes=out_shapes, out_specs=out_specs, epilogue=epilogue, after=after, sem=sem)
    return res[0] if single else res


def _mm_tn(name, a, b, b_spec, *, Mo, No, S, tm, tn, tk, stacked_nloc=None, after=None, a_off=0):
    if stacked_nloc is None:
        out_shape = jax.ShapeDtypeStruct((Mo, No), BF16)
        out_spec = pl.BlockSpec((tm, tn), lambda i, j, k: (i, j))
    else:
        q = stacked_nloc // tn
        out_shape = jax.ShapeDtypeStruct((N_DEV, Mo, stacked_nloc), BF16)
        out_spec = pl.BlockSpec((None, tm, tn), lambda i, j, k: (j // q, i, j % q))
    return _matmul(
        name, a, b, dims=_TN, grid_mnk=(Mo // tm, No // tn, S // tk), tiles=(tm, tn),
        a_spec=pl.BlockSpec((tk, tm), lambda i, j, k: (k, i + a_off)), b_spec=b_spec,
        out_shapes=[out_shape], out_specs=[out_spec], epilogue=_store(BF16), after=after)[0]


def _norm_mod(name, x, g, sc, sh):
    S, D = x.shape
    tm = _tile(S, 256, 8)

    def body(x_ref, g_ref, sc_ref, sh_ref, h_ref):
        xv = x_ref[...]
        r = lax.rsqrt(jnp.mean(xv * xv, axis=-1, keepdims=True) + EPS)
        h = (xv * r) * g_ref[...]
        h_ref[...] = (h * (1.0 + sc_ref[...]) + sh_ref[...]).astype(BF16)

    row = pl.BlockSpec((tm, D), lambda i: (i, 0))
    vec = pl.BlockSpec((1, D), lambda i: (0, 0))
    return pl.pallas_call(body, name=name, grid=(S // tm,), in_specs=[row, vec, vec, vec], out_specs=row,
                          out_shape=jax.ShapeDtypeStruct((S, D), BF16), compiler_params=_cparams(("parallel",)))(x, g, sc, sh)


def _norm_mod_bwd_rows(first, dh_v, x_ref, g_ref, sc_ref, dres_ref, dx_ref, vec_ref, o_ref=None, gt_ref=None, do_ref=None):
    @pl.when(first)
    def _():
        vec_ref[...] = jnp.zeros_like(vec_ref)

    xv, gv = x_ref[...], g_ref[...]
    r = lax.rsqrt(jnp.mean(xv * xv, axis=-1, keepdims=True) + EPS)
    xn = xv * r
    one_sc = 1.0 + sc_ref[...]
    vec_ref[0:1, :] += jnp.sum(dh_v, axis=0, keepdims=True)
    vec_ref[1:2, :] += jnp.sum(dh_v * (xn * gv), axis=0, keepdims=True)
    vec_ref[2:3, :] += jnp.sum(dh_v * one_sc * xn, axis=0, keepdims=True)
    dxn = dh_v * one_sc * gv
    dx = dres_ref[...] + r * (dxn - xn * jnp.mean(dxn * xn, axis=-1, keepdims=True))
    dx_ref[...] = dx
    if o_ref is not None:
        vec_ref[3:4, :] += jnp.sum(dx * o_ref[...], axis=0, keepdims=True)
        do_ref[...] = (dx * gt_ref[...]).astype(BF16)


def _norm_mod_bwd(name, dh, x, g, sc, dres, o=None, gt=None):
    S, D = x.shape
    tm = _tile(S, 256, 8)
    gated = o is not None

    def body(*refs):
        if gated:
            dh_ref, x_ref, g_ref, sc_ref, dres_ref, o_ref, gt_ref, dx_ref, vec_ref, do_ref = refs
        else:
            dh_ref, x_ref, g_ref, sc_ref, dres_ref, dx_ref, vec_ref = refs
            o_ref = gt_ref = do_ref = None
        _norm_mod_bwd_rows(pl.program_id(0) == 0, dh_ref[...], x_ref, g_ref, sc_ref, dres_ref, dx_ref, vec_ref, o_ref, gt_ref, do_ref)

    row = pl.BlockSpec((tm, D), lambda i: (i, 0))
    vec = pl.BlockSpec((1, D), lambda i: (0, 0))
    acc = pl.BlockSpec((8, D), lambda i: (0, 0))
    ins = [dh, x, g, sc, dres] + ([o, gt] if gated else [])
    in_specs = [row, row, vec, vec, row] + ([row, vec] if gated else [])
    out_shape = [jax.ShapeDtypeStruct((S, D), F32), jax.ShapeDtypeStruct((8, D), F32)]
    out_specs = [row, acc]
    if gated:
        out_shape.append(jax.ShapeDtypeStruct((S, D), BF16))
        out_specs.append(row)
    return pl.pallas_call(body, name=name, grid=(S // tm,), in_specs=in_specs, out_specs=out_specs, out_shape=out_shape,
                          compiler_params=_cparams(("arbitrary",)))(*ins)


def _loss_head(x3, tgt, gf, o2, gt2):
    S, D = x3.shape
    tm = _tile(S, 256, 8)

    def body(x_ref, t_ref, g_ref, o_ref, gt_ref, dx_ref, do_ref, vec_ref):
        i = pl.program_id(0)

        @pl.when(i == 0)
        def _():
            vec_ref[...] = jnp.zeros_like(vec_ref)

        xv, gv = x_ref[...], g_ref[...]
        r = lax.rsqrt(jnp.mean(xv * xv, axis=-1, keepdims=True) + EPS)
        xn = xv * r
        e = xn * gv - t_ref[...]
        tok = 0.5 * jnp.mean(e * e, axis=-1, keepdims=True)
        vec_ref[0:1, :] += jnp.broadcast_to(jnp.sum(tok, axis=0, keepdims=True), (1, D))
        dy = e * (1.0 / D)
        vec_ref[1:2, :] += jnp.sum(dy * xn, axis=0, keepdims=True)
        dxn = dy * gv
        dx = r * (dxn - xn * jnp.mean(dxn * xn, axis=-1, keepdims=True))
        dx_ref[...] = dx
        vec_ref[2:3, :] += jnp.sum(dx * o_ref[...], axis=0, keepdims=True)
        do_ref[...] = (dx * gt_ref[...]).astype(BF16)

    row = pl.BlockSpec((tm, D), lambda i: (i, 0))
    vec = pl.BlockSpec((1, D), lambda i: (0, 0))
    return pl.pallas_call(
        body, name="loss_head", grid=(S // tm,), in_specs=[row, row, vec, row, vec],
        out_specs=[row, row, pl.BlockSpec((8, D), lambda i: (0, 0))],
        out_shape=[jax.ShapeDtypeStruct((S, D), F32), jax.ShapeDtypeStruct((S, D), BF16), jax.ShapeDtypeStruct((8, D), F32)],
        compiler_params=_cparams(("arbitrary",)))(x3, tgt, gf, o2, gt2)


def _ffn_in_swiglu(h, wg):
    S, D = h.shape
    _, _, tf = wg.shape
    nf = N_DEV // 2
    F = nf * tf
    tm = _tile(S, 256, 16)

    def body(h_ref, wa_ref, wu_ref, hf_ref, fac_ref):
        hv = h_ref[...]
        a = lax.dot_general(hv, wa_ref[...], _NN, preferred_element_type=F32)
        up = lax.dot_general(hv, wu_ref[...], _NN, preferred_element_type=F32)
        sa = _sigmoid(a)
        silu = a * sa
        hf_ref[...] = (silu * up).astype(BF16)
        fac_ref[0] = (up * (sa * (1.0 + a * (1.0 - sa)))).astype(BF16)
        fac_ref[1] = silu.astype(BF16)

    return pl.pallas_call(
        body, name="ffn_in_swiglu", grid=(nf, S // tm),
        in_specs=[pl.BlockSpec((tm, D), lambda j, i: (i, 0)), pl.BlockSpec((None, D, tf), lambda j, i: (j, 0, 0)),
                  pl.BlockSpec((None, D, tf), lambda j, i: (j + nf, 0, 0))],
        out_specs=[pl.BlockSpec((tm, tf), lambda j, i: (i, j)), pl.BlockSpec((2, tm, tf), lambda j, i: (0, i, j))],
        out_shape=[jax.ShapeDtypeStruct((S, F), BF16), jax.ShapeDtypeStruct((2, S, F), BF16)],
        compiler_params=_cparams(("parallel", "parallel")))(h, wg, wg)


def _gmlp_common(u_ref, v_ref, lg_ref, lb_ref, ws_ref, bsb_ref, G, T, Dg):
    ug, dug = _gelu_parts(u_ref[...])
    vg, dvg = _gelu_parts(v_ref[...])
    mu = jnp.mean(vg, axis=-1, keepdims=True)
    vc = vg - mu
    rstd = lax.rsqrt(jnp.mean(vc * vc, axis=-1, keepdims=True) + EPS)
    vhat = vc * rstd
    vn = vhat * lg_ref[...] + lb_ref[...]
    row = lax.broadcasted_iota(jnp.int32, (T, T), 0)
    col = lax.broadcasted_iota(jnp.int32, (T, T), 1)
    tril = row >= col
    s = []
    for g in range(G):
        w = jnp.where(tril, ws_ref[g], 0.0)
        s.append(_dot(w, vn[:, g * Dg:(g + 1) * Dg]) + bsb_ref[g])
    return ug, dug, dvg, rstd, vhat, vn, tril, s


def _gmlp_fwd(z, ln_g, ln_b, ws, bsb, GW):
    S = z.shape[0]
    G, T, _ = ws.shape
    Dg = GW // G

    def body(u_ref, v_ref, lg_ref, lb_ref, ws_ref, bsb_ref, ya_ref):
        ug, _, _, _, _, _, _, s = _gmlp_common(u_ref, v_ref, lg_ref, lb_ref, ws_ref, bsb_ref, G, T, Dg)
        for g in range(G):
            sl = slice(g * Dg, (g + 1) * Dg)
            ya_ref[:, sl] = (ug[:, sl] * s[g]).astype(BF16)

    vec = pl.BlockSpec((1, GW), lambda c: (0, 0))
    return pl.pallas_call(
        body, name="gmlp_fwd", grid=(S // T,),
        in_specs=[pl.BlockSpec((T, GW), lambda c: (c, 0)), pl.BlockSpec((T, GW), lambda c: (c, 1)), vec, vec,
                  pl.BlockSpec((G, T, T), lambda c: (0, 0, 0)), pl.BlockSpec((G, T, Dg), lambda c: (0, 0, 0))],
        out_specs=pl.BlockSpec((T, GW), lambda c: (c, 0)), out_shape=jax.ShapeDtypeStruct((S, GW), BF16),
        compiler_params=_cparams(("parallel",)))(z, z, ln_g, ln_b, ws, bsb)


def _gmlp_bwd(z, dya, ln_g, ln_b, ws, bsb, GW):
    S = z.shape[0]
    G, T, _ = ws.shape
    Dg = GW // G
    nc = S // T

    def body(u_ref, v_ref, dya_ref, lg_ref, lb_ref, ws_ref, bsb_ref, dz_ref, dln_ref, dws_ref, dbs_ref, dbs_acc, dvh):
        c = pl.program_id(0)

        @pl.when(c == 0)
        def _():
            dln_ref[...] = jnp.zeros_like(dln_ref)
            dws_ref[...] = jnp.zeros_like(dws_ref)
            dbs_acc[...] = jnp.zeros_like(dbs_acc)

        ug, dug, dvg, rstd, vhat, vn, tril, s = _gmlp_common(u_ref, v_ref, lg_ref, lb_ref, ws_ref, bsb_ref, G, T, Dg)
        dya_v = dya_ref[...]
        for g in range(G):
            sl = slice(g * Dg, (g + 1) * Dg)
            dy_g = dya_v[:, sl]
            dz_ref[:, sl] = (dy_g * s[g] * dug[:, sl]).astype(BF16)
            ds = dy_g * ug[:, sl]
            dbs_acc[g] += ds
            w = jnp.where(tril, ws_ref[g], 0.0)
            dvn_g = _dot(w, ds, _TN)
            dws_ref[g] += jnp.where(tril, _dot(ds, vn[:, sl], _NT), 0.0)
            dln_ref[0:1, sl] += jnp.sum(dvn_g * vhat[:, sl], axis=0, keepdims=True)
            dln_ref[1:2, sl] += jnp.sum(dvn_g, axis=0, keepdims=True)
            dvh[:, sl] = dvn_g * lg_ref[:, sl]
        dvhat = dvh[...]
        m1 = jnp.mean(dvhat, axis=-1, keepdims=True)
        m2 = jnp.mean(dvhat * vhat, axis=-1, keepdims=True)
        dz_ref[:, GW:2 * GW] = (rstd * (dvhat - m1 - vhat * m2) * dvg).astype(BF16)

        @pl.when(c == nc - 1)
        def _():
            for g in range(G):
                dbs_ref[g] = jnp.sum(dbs_acc[g], axis=-1, keepdims=True)

    vec = pl.BlockSpec((1, GW), lambda c: (0, 0))
    return pl.pallas_call(
        body, name="gmlp_bwd", grid=(nc,),
        in_specs=[pl.BlockSpec((T, GW), lambda c: (c, 0)), pl.BlockSpec((T, GW), lambda c: (c, 1)),
                  pl.BlockSpec((T, GW), lambda c: (c, 0)), vec, vec,
                  pl.BlockSpec((G, T, T), lambda c: (0, 0, 0)), pl.BlockSpec((G, T, Dg), lambda c: (0, 0, 0))],
        out_specs=[pl.BlockSpec((T, 2 * GW), lambda c: (c, 0)), pl.BlockSpec((8, GW), lambda c: (0, 0)),
                   pl.BlockSpec((G, T, T), lambda c: (0, 0, 0)), pl.BlockSpec((G, T, 1), lambda c: (0, 0, 0))],
        out_shape=[jax.ShapeDtypeStruct((S, 2 * GW), BF16), jax.ShapeDtypeStruct((8, GW), F32),
                   jax.ShapeDtypeStruct((G, T, T), F32), jax.ShapeDtypeStruct((G, T, 1), F32)],
        scratch_shapes=[pltpu.VMEM((G, T, Dg), F32), pltpu.VMEM((T, GW), F32)],
        compiler_params=_cparams(("arbitrary",)))(z, z, dya, ln_g, ln_b, ws, bsb)


def _hg_common(q_ref, f_ref, hlb_ref):
    C = HG_CHUNK
    a = hlb_ref[...]
    lb = _sigmoid(a[0:1, :] - a[1:2, :])
    sig = _sigmoid(f_ref[...])
    f = lb + (1.0 - lb) * sig
    lf = jnp.log(f)
    kk = 1.0 - f
    q = q_ref[...]
    sq = _sigmoid(q)
    qa = q * sq
    row = lax.broadcasted_iota(jnp.int32, (C, C), 0)
    col = lax.broadcasted_iota(jnp.int32, (C, C), 1)
    tril = row >= col
    b = _ones_dot(tril.astype(BF16), lf)
    bm = b[HG_MID:HG_MID + 1, :]
    bl = b[C - 1:C, :]
    e_b = jnp.exp(b)
    e_qm = jnp.exp(jnp.minimum(b - bm, EXP_CLAMP))
    e_km = jnp.exp(jnp.minimum(bm - b, EXP_CLAMP))
    e_kl = jnp.exp(bl - b)
    return dict(lb=lb, sig=sig, f=f, kk=kk, q=q, sq=sq, qa=qa, tril=tril, e_b=e_b, e_qm=e_qm, e_km=e_km, e_kl=e_kl,
                e_l=jnp.exp(bl), qh=qa * e_b, qt=qa * e_qm, kt=kk * e_km, kh=kk * e_kl)


def _hg_fwd(z, hg_lb, ng, HW):
    S = z.shape[0]
    C, H, dk = HG_CHUNK, HW // HG_DK, HG_DK
    nc = S // C

    def body(q_ref, f_ref, i_ref, og_ref, hlb_ref, ng_ref, yb_ref, o_ref, st_ref, state):
        @pl.when(pl.program_id(0) == 0)
        def _():
            state[...] = jnp.zeros_like(state)

        t = _hg_common(q_ref, f_ref, hlb_ref)
        iv = i_ref[...]
        for h in range(H):
            sl = slice(h * dk, (h + 1) * dk)
            st = state[h]
            st_ref[h] = st
            a = jnp.where(t["tril"], _dot(t["qt"][:, sl], t["kt"][:, sl], _NT), 0.0)
            o_h = _dot(a, iv[:, sl]) + _dot(t["qh"][:, sl], st, _NT)
            state[h] = st * t["e_l"][:, sl] + _dot(iv[:, sl], t["kh"][:, sl], _TN)
            o_ref[:, sl] = o_h
            rr = lax.rsqrt(jnp.mean(o_h * o_h, axis=-1, keepdims=True) + EPS)
            og = og_ref[:, sl]
            yb_ref[:, sl] = (o_h * rr * ng_ref[:, sl] * (og * _sigmoid(og))).astype(BF16)

    def col(k):
        return pl.BlockSpec((C, HW), lambda c: (c, k))

    base = 2
    return pl.pallas_call(
        body, name="hgrn_fwd", grid=(nc,),
        in_specs=[col(base), col(base + 1), col(base + 2), col(base + 3),
                  pl.BlockSpec((2, HW), lambda c: (0, 0)), pl.BlockSpec((1, HW), lambda c: (0, 0))],
        out_specs=[pl.BlockSpec((C, HW), lambda c: (c, 0)), pl.BlockSpec((C, HW), lambda c: (c, 0)),
                   pl.BlockSpec((None, H, dk, dk), lambda c: (c, 0, 0, 0))],
        out_shape=[jax.ShapeDtypeStruct((S, HW), BF16), jax.ShapeDtypeStruct((S, HW), F32),
                   jax.ShapeDtypeStruct((nc, H, dk, dk), F32)],
        scratch_shapes=[pltpu.VMEM((H, dk, dk), F32)],
        compiler_params=_cparams(("arbitrary",)))(z, z, z, z, hg_lb, ng)


def _hg_bwd(z, o, states, dyb, hg_lb, ng, HW, dz_head, dz_tail):
    S = z.shape[0]
    C, H, dk = HG_CHUNK, HW // HG_DK, HG_DK
    nc = S // C
    B0 = dz_head.shape[1]
    DT = dz_tail.shape[2]
    INW = B0 + 4 * HW + 2 * DT

    def body(q_ref, f_ref, i_ref, og_ref, o_ref, st_ref, stn_ref, dyb_ref, hlb_ref, ng_ref, head_ref, tail_ref,
             dzf_ref, dng_ref, dhlb_ref, dtail_ref, dstate, cross, dqa_buf, dkk_buf, db_buf, dlb_acc):
        c = pl.program_id(0)
        dzf_ref[:, 0:B0] = head_ref[...]
        dzf_ref[:, B0 + 4 * HW:B0 + 4 * HW + DT] = tail_ref[0]
        dzf_ref[:, B0 + 4 * HW + DT:INW] = tail_ref[1]
        dz_ref = dzf_ref.at[:, B0:B0 + 4 * HW]

        @pl.when(c == 0)
        def _():
            dtail_ref[...] = jnp.zeros_like(dtail_ref)

        dtail_ref[0:1, :] += jnp.sum(tail_ref[0].astype(F32), axis=0, keepdims=True)
        dtail_ref[1:2, :] += jnp.sum(tail_ref[1].astype(F32), axis=0, keepdims=True)

        @pl.when(c == 0)
        def _():
            dstate[...] = jnp.zeros_like(dstate)
            dlb_acc[...] = jnp.zeros_like(dlb_acc)
            dng_ref[...] = jnp.zeros_like(dng_ref)

        def r16(v):
            return v.astype(BF16).astype(F32)

        t = _hg_common(q_ref, f_ref, hlb_ref)
        iv = i_ref[...]
        for h in range(H):
            sl = slice(h * dk, (h + 1) * dk)
            o_h, og, dyb_h, ng_h = o_ref[:, sl], og_ref[:, sl], dyb_ref[:, sl], ng_ref[:, sl]
            sg = _sigmoid(og)
            silu_og = og * sg
            rr = lax.rsqrt(jnp.mean(o_h * o_h, axis=-1, keepdims=True) + EPS)
            on = o_h * rr
            dng_ref[0:1, sl] += jnp.sum(dyb_h * on * silu_og, axis=0, keepdims=True)
            dz_ref[:, 3 * HW + h * dk:3 * HW + (h + 1) * dk] = (dyb_h * on * ng_h * (sg * (1.0 + og * (1.0 - sg)))).astype(BF16)
            don = dyb_h * ng_h * silu_og
            do_h = rr * (don - on * jnp.mean(don * on, axis=-1, keepdims=True))

            qt, kt, qh, kh, iv_h = t["qt"][:, sl], t["kt"][:, sl], t["qh"][:, sl], t["kh"][:, sl], iv[:, sl]
            a = jnp.where(t["tril"], _dot(qt, kt, _NT), 0.0)
            da = jnp.where(t["tril"], _dot(do_h, iv_h, _NT), 0.0)
            st, dst = st_ref[h], dstate[h]
            cross[:, sl] = jnp.sum(dst * stn_ref[h], axis=0, keepdims=True)
            dqh = _dot(do_h, st)
            dstate[h] = _dot(do_h, qh, _TN) + dst * t["e_l"][:, sl]
            div = _dot(a, do_h, _TN) + _dot(kh, dst, _NT)
            dkh = _dot(iv_h, dst)
            dqt = _dot(da, kt)
            dkt = _dot(da, qt, _TN)
            dz_ref[:, 2 * HW + h * dk:2 * HW + (h + 1) * dk] = div.astype(BF16)
            dqa_buf[:, sl] = dqh * t["e_b"][:, sl] + dqt * t["e_qm"][:, sl]
            dkk_buf[:, sl] = dkt * t["e_km"][:, sl] + dkh * t["e_kl"][:, sl]
            db_buf[:, sl] = r16(qt) * dqt - r16(kt) * dkt + r16(qh) * dqh - r16(kh) * dkh

        dqa, dkk = dqa_buf[...], dkk_buf[...]
        triu = jnp.logical_not(t["tril"]) | (lax.broadcasted_iota(jnp.int32, (C, C), 0) == lax.broadcasted_iota(jnp.int32, (C, C), 1))
        dlf = _ones_dot(triu.astype(BF16), db_buf[...]) + cross[...]
        df = dlf / t["f"] - dkk
        sig, lb = t["sig"], t["lb"]
        dz_ref[:, HW:2 * HW] = (df * (1.0 - lb) * sig * (1.0 - sig)).astype(BF16)
        dlb_acc[...] += jnp.sum(df * (1.0 - sig), axis=0, keepdims=True)
        q, sq = t["q"], t["sq"]
        dz_ref[:, 0:HW] = (dqa * (sq * (1.0 + q * (1.0 - sq)))).astype(BF16)

        @pl.when(c == nc - 1)
        def _():
            da0 = dlb_acc[...] * lb * (1.0 - lb)
            dhlb_ref[0:1, :] = da0
            dhlb_ref[1:2, :] = -da0

    def col(k):
        return pl.BlockSpec((C, HW), lambda c: (nc - 1 - c, k))

    base = 2
    return pl.pallas_call(
        body, name="hgrn_bwd", grid=(nc,),
        in_specs=[col(base), col(base + 1), col(base + 2), col(base + 3), col(0),
                  pl.BlockSpec((None, H, dk, dk), lambda c: (nc - 1 - c, 0, 0, 0)),
                  pl.BlockSpec((None, H, dk, dk), lambda c: (jnp.minimum(nc - c, nc - 1), 0, 0, 0)), col(0),
                  pl.BlockSpec((2, HW), lambda c: (0, 0)), pl.BlockSpec((1, HW), lambda c: (0, 0)),
                  pl.BlockSpec((C, B0), lambda c: (nc - 1 - c, 0)), pl.BlockSpec((2, C, DT), lambda c: (0, nc - 1 - c, 0))],
        out_specs=[pl.BlockSpec((C, INW), lambda c: (nc - 1 - c, 0)), pl.BlockSpec((8, HW), lambda c: (0, 0)),
                   pl.BlockSpec((2, HW), lambda c: (0, 0)), pl.BlockSpec((2, DT), lambda c: (0, 0))],
        out_shape=[jax.ShapeDtypeStruct((S, INW), BF16), jax.ShapeDtypeStruct((8, HW), F32), jax.ShapeDtypeStruct((2, HW), F32),
                   jax.ShapeDtypeStruct((2, DT), F32)],
        scratch_shapes=[pltpu.VMEM((H, dk, dk), F32), pltpu.VMEM((1, HW), F32), pltpu.VMEM((C, HW), F32), pltpu.VMEM((C, HW), F32),
                        pltpu.VMEM((C, HW), F32), pltpu.VMEM((1, HW), F32)],
        compiler_params=_cparams(("arbitrary",)))(z, z, z, z, o, states, states, dyb, hg_lb, ng, dz_head, dz_tail)


def _position():
    x, y, c = lax.axis_index("x"), lax.axis_index("y"), lax.axis_index("c")
    return x, y, c, 4 * x + 2 * y + c


def _flip(x, y, c, k):
    return (1 - x if k & 4 else x, 1 - y if k & 2 else y, 1 - c if k & 1 else c)


def _allgather_small(name, v):
    R, L = v.shape

    def body(v_ref, out_ref, send_sems, recv_sems):
        x, y, c, me = _position()
        out_ref[me] = v_ref[...]
        copies = []
        for k in range(1, N_DEV):
            cp = pltpu.make_async_remote_copy(src_ref=v_ref, dst_ref=out_ref.at[me], send_sem=send_sems.at[k - 1],
                                              recv_sem=recv_sems.at[k - 1], device_id=_flip(x, y, c, k), device_id_type=MESH)
            cp.start()
            copies.append(cp)
        for cp in copies:
            cp.wait()

    return pl.pallas_call(
        body, name=name, out_shape=jax.ShapeDtypeStruct((N_DEV, R, L), v.dtype),
        in_specs=[pl.BlockSpec(memory_space=pltpu.VMEM)], out_specs=pl.BlockSpec(memory_space=pltpu.VMEM),
        scratch_shapes=[pltpu.SemaphoreType.DMA((N_DEV - 1,)), pltpu.SemaphoreType.DMA((N_DEV - 1,))],
        compiler_params=pltpu.CompilerParams(vmem_limit_bytes=VMEM_LIMIT),
    )(v)


_HBM = pl.BlockSpec(memory_space=pltpu.HBM)
_SEM = pl.BlockSpec(memory_space=pltpu.SEMAPHORE)
_EFFECT = pltpu.SideEffectType.DATAFLOW_SIDE_EFFECTING


def _split_start(name, bufs, n_sems, copies_fn, after=None):
    nb = len(bufs)
    extra = [] if after is None else [after]
    k = nb + len(extra)

    def body(*refs):
        for cp in copies_fn(refs[:nb], refs[k], refs[k + 1]):
            cp.start()
        refs[-1][...] = jnp.zeros_like(refs[-1])

    sems = pltpu.SemaphoreType.DMA((n_sems,))
    res = pl.pallas_call(
        body, name=name,
        out_shape=(sems, sems, *[pltpu.HBM(a.shape, a.dtype) for a in bufs], jax.ShapeDtypeStruct((8, LANES), F32)),
        in_specs=[_HBM] * nb + [pl.BlockSpec(memory_space=pl.ANY)] * len(extra),
        out_specs=(_SEM, _SEM, *[_HBM] * nb, pl.BlockSpec(memory_space=pltpu.VMEM)),
        input_output_aliases={i: 2 + i for i in range(nb)},
        compiler_params=pltpu.CompilerParams(has_side_effects=_EFFECT),
    )(*[pltpu.with_memory_space_constraint(a, pltpu.HBM) for a in bufs], *extra)
    return res[0], res[1], list(res[2:2 + nb]), res[-1]


def _split_wait(name, bufs, send_sems, recv_sems, after, copies_fn):
    nb = len(bufs)

    def body(*refs):
        for cp in copies_fn(refs[:nb], refs[nb], refs[nb + 1]):
            cp.wait_send()
            cp.wait_recv()

    res = pl.pallas_call(
        body, name=name, out_shape=tuple(pltpu.HBM(a.shape, a.dtype) for a in bufs),
        in_specs=[_HBM] * nb + [_SEM, _SEM, pl.BlockSpec(memory_space=pl.ANY)], out_specs=tuple([_HBM] * nb),
        input_output_aliases={i: i for i in range(nb)},
        compiler_params=pltpu.CompilerParams(has_side_effects=_EFFECT),
    )(*bufs, send_sems, recv_sems, after)
    return list(res)


def _split_relay(name, bufs, send_sems, recv_sems, after, wait_fn, n_sems, start_fn):
    nb = len(bufs)

    def body(*refs):
        for cp in wait_fn(refs[:nb], refs[nb], refs[nb + 1]):
            cp.wait_send()
            cp.wait_recv()
        for cp in start_fn(refs[:nb], refs[nb + 3], refs[nb + 4]):
            cp.start()
        refs[-1][...] = jnp.zeros_like(refs[-1])

    sems = pltpu.SemaphoreType.DMA((n_sems,))
    res = pl.pallas_call(
        body, name=name, out_shape=(sems, sems, *[pltpu.HBM(a.shape, a.dtype) for a in bufs], jax.ShapeDtypeStruct((8, LANES), F32)),
        in_specs=[_HBM] * nb + [_SEM, _SEM, pl.BlockSpec(memory_space=pl.ANY)],
        out_specs=(_SEM, _SEM, *[_HBM] * nb, pl.BlockSpec(memory_space=pltpu.VMEM)),
        input_output_aliases={i: 2 + i for i in range(nb)},
        compiler_params=pltpu.CompilerParams(has_side_effects=_EFFECT),
    )(*bufs, send_sems, recv_sems, after)
    return res[0], res[1], list(res[2:2 + nb]), res[-1]


N_CHIP = 4


def _chip_flip(x, y, k):
    return (1 - x if k & 2 else x), (1 - y if k & 1 else y)


def _gather_first_copies(n):
    def copies(bufs, send_sems, recv_sems):
        x, y, c, me = _position()
        out = []
        for w in range(n):
            for k in range(N_CHIP):
                to = (x, y, 1 - c) if k == 0 else (*_chip_flip(x, y, k), c)
                out.append(pltpu.make_async_remote_copy(
                    src_ref=bufs[w], dst_ref=bufs[n + w].at[me], send_sem=send_sems.at[w * N_CHIP + k],
                    recv_sem=recv_sems.at[w * N_CHIP + k], device_id=to, device_id_type=MESH))
        return out
    return copies


def _gather_relay_copies(n):
    def copies(bufs, send_sems, recv_sems):
        x, y, c, _ = _position()
        out = []
        for w in range(n):
            for k in range(1, N_CHIP):
                px, py = _chip_flip(x, y, k)
                blk = bufs[n + w].at[4 * px + 2 * py + c]
                out.append(pltpu.make_async_remote_copy(
                    src_ref=blk, dst_ref=blk, send_sem=send_sems.at[w * (N_CHIP - 1) + k - 1],
                    recv_sem=recv_sems.at[w * (N_CHIP - 1) + k - 1], device_id=(x, y, 1 - c), device_id_type=MESH))
        return out
    return copies


def _small_gather_copies(bufs, send_sems, recv_sems):
    x, y, c, me = _position()
    return [pltpu.make_async_remote_copy(src_ref=bufs[0], dst_ref=bufs[1].at[me], send_sem=send_sems.at[k - 1], recv_sem=recv_sems.at[k - 1],
                                         device_id=_flip(x, y, c, k), device_id_type=MESH) for k in range(1, N_DEV)]


def _forward_first_copies(n):
    def copies(bufs, send_sems, recv_sems):
        x, y, c, me = _position()
        out = []
        for w in range(n):
            for k, to in enumerate([(x, y, 1 - c), (1 - x, y, c), (x, 1 - y, c)]):
                out.append(pltpu.make_async_remote_copy(
                    src_ref=bufs[w], dst_ref=bufs[n + w].at[me], send_sem=send_sems.at[w * 3 + k],
                    recv_sem=recv_sems.at[w * 3 + k], device_id=to, device_id_type=MESH))
        return out
    return copies


def _forward_second_copies(n):
    def copies(bufs, send_sems, recv_sems):
        x, y, c, _ = _position()
        out = []
        for w in range(n):
            half = bufs[n + w].shape[1] // 2
            for k, (src_chip, rows, to) in enumerate([((1 - x, y), pl.ds(0, half), (x, 1 - y, c)), ((x, 1 - y), pl.ds(half, half), (1 - x, y, c))]):
                blk = bufs[n + w].at[4 * src_chip[0] + 2 * src_chip[1] + c, rows]
                out.append(pltpu.make_async_remote_copy(src_ref=blk, dst_ref=blk, send_sem=send_sems.at[w * 4 + k],
                                                        recv_sem=recv_sems.at[w * 4 + k], device_id=to, device_id_type=MESH))
            for k, (px, py) in enumerate([(1 - x, y), (x, 1 - y)]):
                blk = bufs[n + w].at[4 * px + 2 * py + c]
                out.append(pltpu.make_async_remote_copy(src_ref=blk, dst_ref=blk, send_sem=send_sems.at[w * 4 + 2 + k],
                                                        recv_sem=recv_sems.at[w * 4 + 2 + k], device_id=(x, y, 1 - c), device_id_type=MESH))
        return out
    return copies


def _forward_third_copies(n):
    def copies(bufs, send_sems, recv_sems):
        x, y, c, _ = _position()
        out = []
        for w in range(n):
            blk = bufs[n + w].at[4 * (1 - x) + 2 * (1 - y) + c]
            out.append(pltpu.make_async_remote_copy(src_ref=blk, dst_ref=blk, send_sem=send_sems.at[w], recv_sem=recv_sems.at[w],
                                                    device_id=(x, y, 1 - c), device_id_type=MESH))
        return out
    return copies


def _to_sibling_copies(n):
    def copies(bufs, send_sems, recv_sems):
        x, y, c, _ = _position()
        out = []
        for w in range(n):
            for q in range(N_CHIP):
                out.append(pltpu.make_async_remote_copy(
                    src_ref=bufs[w].at[2 * q + 1 - c], dst_ref=bufs[n + w].at[q], send_sem=send_sems.at[w * N_CHIP + q],
                    recv_sem=recv_sems.at[w * N_CHIP + q], device_id=(x, y, 1 - c), device_id_type=MESH))
        return out
    return copies


def _to_owner_copies(n):
    def copies(bufs, send_sems, recv_sems):
        x, y, c, _ = _position()
        out = []
        for w in range(n):
            for k in range(1, N_CHIP):
                px, py = (1 - x if k & 2 else x), (1 - y if k & 1 else y)
                out.append(pltpu.make_async_remote_copy(
                    src_ref=bufs[w].at[2 * px + py], dst_ref=bufs[n + w].at[k - 1], send_sem=send_sems.at[w * (N_CHIP - 1) + k - 1],
                    recv_sem=recv_sems.at[w * (N_CHIP - 1) + k - 1], device_id=(px, py, c), device_id_type=MESH))
        return out
    return copies


def _chip_sum(name, stack, landed, c_idx):
    _, R, C = stack.shape
    tr = _tile(R, max(BF16_ROWS, STREAM_TILE // C), BF16_ROWS)

    def body(c_ref, a_ref, b_ref, o_ref):
        o_ref[...] = (a_ref[...].astype(F32) + b_ref[...].astype(F32)).astype(o_ref.dtype)

    return pl.pallas_call(
        body, name=name,
        grid_spec=pltpu.PrefetchScalarGridSpec(
            num_scalar_prefetch=1, grid=(N_CHIP, R // tr),
            in_specs=[pl.BlockSpec((None, tr, C), lambda q, i, c_ref: (2 * q + c_ref[0], i, 0)),
                      pl.BlockSpec((None, tr, C), lambda q, i, c_ref: (q, i, 0))],
            out_specs=pl.BlockSpec((None, tr, C), lambda q, i, c_ref: (q, i, 0))),
        out_shape=jax.ShapeDtypeStruct((N_CHIP, R, C), stack.dtype),
        compiler_params=_cparams(("parallel", "parallel")))(c_idx, stack, landed)


def _ada_mod(c16, w):
    _, D = c16.shape
    n = w.shape[1]
    tk = _tile(D, 512)
    nk = D // tk

    def body(c_ref, w_ref, o_ref, ca_ref):
        @pl.when(pl.program_id(0) == 0)
        def _():
            o_ref[...] = jnp.zeros_like(o_ref)

        cv = c_ref[...]
        ca = cv * _sigmoid(cv)
        ca_ref[...] = ca
        o_ref[...] += _dot(ca, w_ref[...])

    return pl.pallas_call(
        body, name="ada_mod", grid=(nk,),
        in_specs=[pl.BlockSpec((16, tk), lambda k: (0, k)), pl.BlockSpec((tk, n), lambda k: (k, 0))],
        out_specs=[pl.BlockSpec((16, n), lambda k: (0, 0)), pl.BlockSpec((16, tk), lambda k: (0, k))],
        out_shape=[jax.ShapeDtypeStruct((16, n), F32), jax.ShapeDtypeStruct((16, D), F32)],
        compiler_params=_cparams(("arbitrary",)))(c16, w)


def _cast_shard(name, wf, slot):
    r, c = wf.shape
    tr = _tile(r, max(BF16_ROWS, STREAM_TILE // c), BF16_ROWS)

    def body(slot_ref, w_ref, s_ref, g_ref):
        v = w_ref[...].astype(BF16)
        s_ref[...] = v
        g_ref[...] = v

    return pl.pallas_call(
        body, name=name,
        grid_spec=pltpu.PrefetchScalarGridSpec(
            num_scalar_prefetch=1, grid=(r // tr,), in_specs=[pl.BlockSpec((tr, c), lambda i, s: (i, 0))],
            out_specs=[pl.BlockSpec((tr, c), lambda i, s: (i, 0)), pl.BlockSpec((None, tr, c), lambda i, s: (s[0], i, 0))]),
        out_shape=[jax.ShapeDtypeStruct((r, c), BF16), jax.ShapeDtypeStruct((N_DEV, r, c), BF16)],
        compiler_params=_cparams(("parallel",)))(slot, wf)


def _adam_math(w, g, m, v):
    m2 = ADAM_B1 * m + (1.0 - ADAM_B1) * g
    v2 = ADAM_B2 * v + (1.0 - ADAM_B2) * (g * g)
    m_hat = m2 / (1.0 - ADAM_B1 ** ADAM_STEP)
    v_hat = v2 / (1.0 - ADAM_B2 ** ADAM_STEP)
    delta = -ADAM_LR * (m_hat / (jnp.sqrt(v_hat) + ADAM_EPS) + ADAM_WD * w)
    return delta, m2, v2


def _adamw(name, w, m, v, own, own_slot, parts=(), row0=0, into=None):
    R, C = w.shape
    Rp = own.shape[1]
    tr = _tile(Rp, max(BF16_ROWS, ADAMW_TILE // C), BF16_ROWS)
    off = row0 // tr
    n_p = len(parts)
    held = [] if into is None else list(into)

    def step(*refs):
        w_ref, m_ref, v_ref, own_ref = refs[:4]
        g_ref, d_ref, m2_ref, v2_ref = refs[4 + n_p:]
        g = own_ref[...].astype(F32)
        for p_ref in refs[4:4 + n_p]:
            for s in range(p_ref.shape[0]):
                g = g + p_ref[s].astype(F32)
        delta, m2, v2 = _adam_math(w_ref[...], g, m_ref[...], v_ref[...])
        g_ref[...] = g
        d_ref[...] = delta
        m2_ref[...] = m2
        v2_ref[...] = v2

    deep = pl.Buffered(3)
    blk = pl.BlockSpec((tr, C), lambda i: (i + off, 0))
    blk_in = pl.BlockSpec((tr, C), lambda i: (i + off, 0), pipeline_mode=deep)
    in_specs = ([blk_in, blk_in, blk_in, pl.BlockSpec((tr, C), lambda i: (i, 0), pipeline_mode=deep)]
                + [pl.BlockSpec((a.shape[0], tr, C), lambda i: (0, i, 0), pipeline_mode=deep) for a in parts])

    def body(slot_ref, w_ref, m_ref, v_ref, own_ref, *refs):
        pltpu.emit_pipeline(step, grid=(Rp // tr,), in_specs=in_specs, out_specs=[blk] * 4)(
            w_ref, m_ref, v_ref, own_ref.at[slot_ref[0]], *refs[:n_p], *refs[n_p + len(held):])

    out = jax.ShapeDtypeStruct((R, C), F32)
    whole = pl.BlockSpec(memory_space=pl.ANY)
    return pl.pallas_call(
        body, name=name, in_specs=[pl.BlockSpec(memory_space=pltpu.SMEM)] + [whole] * (4 + n_p + len(held)), out_specs=[whole] * 4,
        out_shape=[out] * 4, input_output_aliases={5 + n_p + i: i for i in range(len(held))},
        compiler_params=pltpu.CompilerParams(vmem_limit_bytes=VMEM_LIMIT))(own_slot, w, m, v, own, *parts, *held)


def _small_update(gathered, w, m, v, after, rows):
    _, R, L = gathered.shape
    rs = w.shape[0]
    n = len(rows)
    assert all(r % 8 == 0 for r in rows) and sum(rows) <= rs and rs + 8 <= R

    def body(p_ref, w_ref, m_ref, v_ref, after_ref, *outs):
        g = p_ref[0]
        for p in range(1, N_DEV):
            g = g + p_ref[p]
        kinds = (g,) + _adam_math(w_ref[...], g[0:rs, :], m_ref[...], v_ref[...])
        at = 0
        for k, r in enumerate(rows):
            for idx, val in enumerate(kinds):
                outs[idx * n + k][...] = val[at:at + r, :]
            at += r
        outs[4 * n][...] = g[at:at + 8, :]

    vm = pl.BlockSpec(memory_space=pltpu.VMEM)
    shapes = [jax.ShapeDtypeStruct((r, L), F32) for _ in range(4) for r in rows] + [jax.ShapeDtypeStruct((8, L), F32)]
    return pl.pallas_call(body, name="small_update", in_specs=[vm] * 4 + [pl.BlockSpec(memory_space=pl.ANY)], out_specs=[vm] * len(shapes),
                          out_shape=shapes, compiler_params=pltpu.CompilerParams(vmem_limit_bytes=VMEM_LIMIT))(gathered, w, m, v, after)


class _Fetched(dict):
    def __init__(self, fetch):
        super().__init__()
        self.fetch = fetch

    def first(self, key, after):
        self[key] = self.fetch(key, after)
        return self[key]


def _local_step(x, tgt, mod, p, fetch, F, scatter=None):
    S, D = x.shape
    GW, HW = p["ln_g"].shape[1], p["hg_ng"].shape[1]
    G, T, _ = p["ws"].shape
    w = _Fetched(fetch)
    INW = 2 * GW + 4 * HW + 2 * D
    in_loc, br_loc, fi_loc = INW // N_DEV, D // N_DEV, 2 * F // N_DEV
    assert GW == HW and F % fi_loc == 0
    sh1, sc1, gt1, sh2, sc2, gt2 = (mod[:, k * D:(k + 1) * D] for k in range(6))
    bsb = jnp.broadcast_to(p["bs"][:, :, None], (G, T, GW // G))

    tm = _tile(S, 1024, 16)
    tmh = _tile(S, 512, 16)
    tn_in = _tile(in_loc, 1280)
    tn_d = _tile(D, 512)
    tn_br = _tile(br_loc, 512)
    tk_s = S
    tm_w = _tile(D, 1024)
    g_off = 2 * GW + 4 * HW

    h1 = _norm_mod("norm1", x, p["norm1_g"], sc1, sh1)
    z = _mm_nn_stacked("proj_in", h1, w.first("in", h1), tm=tm, tn=tn_in, tk=D)[0]
    ya = _gmlp_fwd(z, p["ln_g"], p["ln_b"], p["ws"], bsb, GW)
    yb, o_hg, states = _hg_fwd(z, p["hg_lb"], p["hg_ng"], HW)
    flat = {k: jnp.swapaxes(w.first(k, yb), 0, 1).reshape(GW, D) for k in ("bg", "bh")}
    tn_f = _tile(D, 1024)
    pa = _matmul(
        "branch_gmlp", ya, flat["bg"], dims=_NN, grid_mnk=(S // tm, D // tn_f, 1), tiles=(tm, tn_f),
        a_spec=pl.BlockSpec((tm, GW), lambda i, j, k: (i, 0)), b_spec=pl.BlockSpec((GW, tn_f), lambda i, j, k: (0, j)),
        out_shapes=[jax.ShapeDtypeStruct((S, D), F32)], out_specs=[pl.BlockSpec((tm, tn_f), lambda i, j, k: (i, j))], epilogue=_store(F32))[0]
    t_fi = w.first("fi_early", pa)

    def gates(ga_ref, gb_ref, ba_ref, bb_ref):
        return _sigmoid(ga_ref[...] + ba_ref[...]), _sigmoid(gb_ref[...] + bb_ref[...])

    def gate_specs(tn_, tm_=tm):
        o1, o2 = g_off // tn_, (g_off + D) // tn_
        return [pl.BlockSpec((tm_, tn_), lambda i, j, k: (i, o1 + j)), pl.BlockSpec((tm_, tn_), lambda i, j, k: (i, o2 + j)),
                pl.BlockSpec((1, tn_), lambda i, j, k: (0, j)), pl.BlockSpec((1, tn_), lambda i, j, k: (0, D // tn_ + j))]

    def merge_ep(acc, ex, outs):
        ga, gb = gates(*ex[1:5])
        outs[0][...] = acc.astype(BF16)
        outs[1][...] = (ga * ex[0][...] + gb * acc).astype(BF16)

    tile_o = pl.BlockSpec((tmh, tn_f), lambda i, j, k: (i, j))
    pb, y = _matmul(
        "branch_hg_merge", yb, flat["bh"], dims=_NN, grid_mnk=(S // tmh, D // tn_f, 1), tiles=(tmh, tn_f),
        a_spec=pl.BlockSpec((tmh, HW), lambda i, j, k: (i, 0)), b_spec=pl.BlockSpec((HW, tn_f), lambda i, j, k: (0, j)),
        extras=[pa, z, z, p["b_gate"], p["b_gate"]], extra_specs=[tile_o, *gate_specs(tn_f, tmh)],
        out_shapes=[jax.ShapeDtypeStruct((S, D), BF16), jax.ShapeDtypeStruct((S, D), BF16)], out_specs=[tile_o, tile_o],
        epilogue=merge_ep, after=t_fi)

    def resid_ep(acc, ex, outs):
        outs[0][...] = acc.astype(BF16)
        outs[1][...] = ex[0][...] + ex[1][...] * acc

    def resid_mm(name, a, b, res, gt, tm_, tn_):
        K = a.shape[1]
        t_o = pl.BlockSpec((tm_, tn_), lambda i, j, k: (i, j))
        return _matmul(
            name, a, b, dims=_NN, grid_mnk=(S // tm_, D // tn_, 1), tiles=(tm_, tn_),
            a_spec=pl.BlockSpec((tm_, K), lambda i, j, k: (i, 0)), b_spec=pl.BlockSpec((K, tn_), lambda i, j, k: (0, j)),
            extras=[res, gt], extra_specs=[t_o, pl.BlockSpec((1, tn_), lambda i, j, k: (0, j))],
            out_shapes=[jax.ShapeDtypeStruct((S, D), BF16), jax.ShapeDtypeStruct((S, D), F32)], out_specs=[t_o, t_o], epilogue=resid_ep)

    o1, xm = resid_mm("proj_out", y, w.first("out", z), x, gt1, tm, tn_f)
    h2 = _norm_mod("norm2", xm, p["norm2_g"], sc2, sh2)
    hf, hf_fac = _ffn_in_swiglu(h2, w.first("fi", h2))
    o2, x3 = resid_mm("ffn_out", hf, w.first("fo", hf), xm, gt2, tm, _tile(D, 256))
    dx3, do2, vec_l = _loss_head(x3, tgt, p["final_g"], o2, gt2)

    nf = F // fi_loc

    def dswiglu_ep(acc, ex, outs):
        outs[0][0] = (acc * ex[0][0].astype(F32)).astype(BF16)
        outs[0][1] = (acc * ex[0][1].astype(F32)).astype(BF16)

    pair = pl.BlockSpec((2, tmh, fi_loc), lambda i, j, k: (0, j, i))
    dab = _matmul(
        "ffn_out_dx", do2, w["fo"], dims=_NT, grid_mnk=(nf, S // tmh, 1), tiles=(tmh, fi_loc),
        a_spec=pl.BlockSpec((tmh, D), lambda i, j, k: (j, 0)), b_spec=pl.BlockSpec((fi_loc, D), lambda i, j, k: (i, 0)),
        extras=[hf_fac], extra_specs=[pair], out_shapes=[jax.ShapeDtypeStruct((2, S, F), BF16)], out_specs=[pair],
        epilogue=dswiglu_ep)[0]
    start = (lambda name, grads: scatter[0](name, grads)) if scatter is not None else (lambda name, grads: None)
    push = (lambda name, after: scatter[1](name, after)) if scatter is not None else (lambda name, after: None)

    def zero(token):
        return 0.0 if token is None else token[0:1, 0:1]

    tm_f = _tile(F, 512)
    g_fo = _mm_tn("ffn_out_dw", hf, do2, pl.BlockSpec((tk_s, D), lambda i, j, k: (k, j)), Mo=F, No=D, S=S, tm=tm_f, tn=D, tk=tk_s)
    g_fi = _mm_tn("ffn_in_dw", h2, dab, pl.BlockSpec((None, tk_s, fi_loc), lambda i, j, k: (j // nf, k, j % nf)),
                  Mo=D, No=2 * F, S=S, tm=D, tn=fi_loc, tk=tk_s, stacked_nloc=fi_loc, after=g_fo)
    t_ffn = start("scatter_ffn", dict(fo=g_fo, fi=g_fi))
    dh2 = _mm_nt_stacked("ffn_in_dx", pl.BlockSpec((None, tm, fi_loc), lambda i, j, k: (k // nf, i, k % nf)), dab, w["fi"],
                         M=S, tm=tm, tn=D, tk=fi_loc, after=t_ffn)
    dxm, vec2, do1 = _norm_mod_bwd("norm2_bwd", dh2, xm, p["norm2_g"], sc2, dx3, o1, gt1)
    t_ffn = push("scatter_ffn", dxm)

    def dmerge_ep(acc, ex, outs):
        ga, gb = gates(*ex[2:6])
        outs[0][...] = (acc * ga).astype(BF16)
        outs[1][...] = (acc * gb).astype(BF16)
        outs[2][0] = (acc * ex[0][...] * ga * (1.0 - ga)).astype(BF16)
        outs[2][1] = (acc * ex[1][...] * gb * (1.0 - gb)).astype(BF16)

    t_o = pl.BlockSpec((tm, tn_d), lambda i, j, k: (i, j))
    dpa, dpb, dg2 = _matmul(
        "proj_out_dx", do1, w["out"], dims=_NT, grid_mnk=(S // tm, D // tn_d, 1), tiles=(tm, tn_d),
        a_spec=pl.BlockSpec((tm, D), lambda i, j, k: (i, 0)), b_spec=pl.BlockSpec((tn_d, D), lambda i, j, k: (j, 0)),
        extras=[pa, pb, z, z, p["b_gate"], p["b_gate"]], extra_specs=[t_o, t_o, *gate_specs(tn_d)],
        out_shapes=[jax.ShapeDtypeStruct((S, D), BF16), jax.ShapeDtypeStruct((S, D), BF16), jax.ShapeDtypeStruct((2, S, D), BF16)],
        out_specs=[t_o, t_o, pl.BlockSpec((2, tm, tn_d), lambda i, j, k: (0, i, j))], epilogue=dmerge_ep, after=t_ffn)
    g_out = _mm_tn("proj_out_dw", y, do1, pl.BlockSpec((tk_s, D), lambda i, j, k: (k, j)), Mo=D, No=D, S=S, tm=tn_d, tn=D, tk=tk_s)
    tn_g = _tile(GW, 512)
    b_br = pl.BlockSpec((tk_s, br_loc), lambda i, j, k: (k, j))
    tm_b = _tile(GW, 1024)
    g_bg = _mm_tn("branch_gmlp_dw", ya, dpa, b_br, Mo=GW, No=D, S=S, tm=tm_b, tn=br_loc, tk=tk_s, stacked_nloc=br_loc)
    g_bh = _mm_tn("branch_hg_dw", yb, dpb, b_br, Mo=HW, No=D, S=S, tm=tm_b, tn=br_loc, tk=tk_s, stacked_nloc=br_loc)
    t_mix = start("scatter_mixer", dict(out=g_out, bg=g_bg, bh=g_bh))
    def branch_dx(name, dp, w_flat):
        return _matmul(
            name, dp, w_flat, dims=_NT, grid_mnk=(S // tm, GW // tn_g, 1), tiles=(tm, tn_g),
            a_spec=pl.BlockSpec((tm, D), lambda i, j, k: (i, 0)), b_spec=pl.BlockSpec((tn_g, D), lambda i, j, k: (j, 0)),
            out_shapes=[jax.ShapeDtypeStruct((S, GW), F32)], out_specs=[pl.BlockSpec((tm, tn_g), lambda i, j, k: (i, j))],
            epilogue=_store(F32), after=t_mix)[0]

    dya = branch_dx("branch_gmlp_dx", dpa, flat["bg"])
    dyb = branch_dx("branch_hg_dx", dpb, flat["bh"])
    dz_gmlp, dln, dws, dbs = _gmlp_bwd(z, dya, p["ln_g"], p["ln_b"], p["ws"], bsb, GW)
    t_mix = push("scatter_mixer", dz_gmlp)
    dz, dng, dhlb, db_gate = _hg_bwd(z, o_hg, states, dyb, p["hg_lb"], p["hg_ng"] + zero(t_mix), HW, dz_gmlp, dg2)
    half = D // 2
    tm_h = _tile(half, 1024)
    g_in = []
    t_in = None
    for hname, h in (("a", 0), ("b", 1)):
        g_in.append(_mm_tn("proj_in_dw_" + hname, h1, dz, pl.BlockSpec((tk_s, in_loc), lambda i, j, k: (k, j)), Mo=half, No=INW, S=S,
                           tm=tm_h, tn=in_loc, tk=tk_s, stacked_nloc=in_loc, after=t_in, a_off=h * (half // tm_h)))
        t_in = start("scatter_proj_in_" + hname, {"w_in_" + hname: g_in[-1]})
    t_in = push("scatter_proj_in_a", t_in)
    dh1 = _mm_nt_stacked("proj_in_dx", pl.BlockSpec((tm, in_loc), lambda i, j, k: (i, k)), dz, w["in"], M=S, tm=tm, tn=D, tk=in_loc,
                         after=t_in)
    dx, vec1 = _norm_mod_bwd("norm1_bwd", dh1, x, p["norm1_g"], sc1, dxm)

    dmod = jnp.concatenate([vec1[0:1], vec1[1:2], vec2[3:4], vec2[0:1], vec2[1:2], vec_l[2:3]], axis=1)
    small = dict(norm1_g=vec1[2:3], b_gate=db_gate.reshape(1, 2 * D), ln_g=dln[0:1], ln_b=dln[1:2], ws=dws, bs=dbs.reshape(G, T),
                 hg_lb=dhlb, hg_ng=dng[0:1], norm2_g=vec2[2:3], final_g=vec_l[1:2], loss=vec_l[0:1, 0:LANES])
    big = dict(w_in_a=g_in[0], w_in_b=g_in[1], bg=g_bg, bh=g_bh, out=g_out, fi=g_fi, fo=g_fo)
    return dx, big, small, dmod


_SMALL = ("b_ada", "norm1_g", "b_gate", "ln_g", "ln_b", "ws", "bs", "hg_lb", "hg_ng", "norm2_g", "final_g")


def _pack(parts, rows_mult=8):
    flat = [a.reshape(-1) for a in parts]
    offs, n = [], 0
    for a in flat:
        offs.append(n)
        n += a.shape[0]
    pad = (-n) % (LANES * rows_mult)
    if pad:
        flat.append(jnp.zeros((pad,), F32))
    return jnp.concatenate(flat).reshape(-1, LANES), offs


def kernel(x, c, w_ada, b_ada, norm1_g, w_in, b_gate, gmlp_ln_g, gmlp_ln_b, gmlp_ws, gmlp_bs, hg_lb, hg_norm_g, w_branch_gmlp, w_branch_hg, w_out, norm2_g, w_ffn_in, w_ffn_out, final_norm_g, loss_target, m_w_ada, m_b_ada, m_norm1_g, m_w_in, m_b_gate, m_gmlp_ln_g, m_gmlp_ln_b, m_gmlp_ws, m_gmlp_bs, m_hg_lb, m_hg_norm_g, m_w_branch_gmlp, m_w_branch_hg, m_w_out, m_norm2_g, m_w_ffn_in, m_w_ffn_out, m_final_norm_g, v_w_ada, v_b_ada, v_norm1_g, v_w_in, v_b_gate, v_gmlp_ln_g, v_gmlp_ln_b, v_gmlp_ws, v_gmlp_bs, v_hg_lb, v_hg_norm_g, v_w_branch_gmlp, v_w_branch_hg, v_w_out, v_norm2_g, v_w_ffn_in, v_w_ffn_out, v_final_norm_g):
    S, D = x.shape[1], x.shape[2]
    ada_loc = w_ada.shape[2]
    me = 4 * lax.axis_index("x") + 2 * lax.axis_index("y") + lax.axis_index("c")
    me_idx = me.astype(jnp.int32).reshape(1)

    def empty_hbm(shape, dtype):
        return pltpu.with_memory_space_constraint(lax.empty(shape, dtype), pltpu.HBM)

    groups = dict(gather_in=dict(keys=["in"], src=[w_in], forward=True),
                  gather_mixer=dict(keys=["bg", "bh", "out"], src=[w_branch_gmlp, w_branch_hg, w_out], forward=False),
                  gather_ffn_in=dict(keys=["fi"], src=[w_ffn_in], forward=True),
                  gather_ffn_out=dict(keys=["fo"], src=[w_ffn_out], forward=False))
    group_of = {k: gname for gname, g in groups.items() for k in g["keys"]}

    def first_hop(gname, after):
        g = groups[gname]
        n = len(g["keys"])
        cast = [_cast_shard(f"{gname}_cast_{k}", a[0], me_idx) for k, a in zip(g["keys"], g["src"])]
        shards, outs = [s for s, _ in cast], [o for _, o in cast]
        if g["forward"]:
            *g["hop"], token = _split_start(gname + "_hop1", shards + outs, n * 3, _forward_first_copies(n), after=after)
        else:
            *g["hop"], token = _split_start(gname + "_hop1", shards + outs, n * N_CHIP, _gather_first_copies(n), after=after)
        return token

    def second_hop(gname, after):
        g = groups[gname]
        n = len(g["keys"])
        *g["hop"], token = _split_relay(gname + "_hop2", g["hop"][2], g["hop"][0], g["hop"][1], after,
                                        _forward_first_copies(n), n * 4, _forward_second_copies(n))
        return token

    def finish(gname, after):
        g = groups[gname]
        n = len(g["keys"])
        send_sems, recv_sems, bufs = g["hop"]
        if g["forward"]:
            send_sems, recv_sems, bufs, _ = _split_relay(gname + "_hop3", bufs, send_sems, recv_sems, after,
                                                         _forward_second_copies(n), n, _forward_third_copies(n))
            bufs = _split_wait(gname + "_wait", bufs, send_sems, recv_sems, after, _forward_third_copies(n))
        else:
            send_sems, recv_sems, bufs, _ = _split_relay(gname + "_relay", bufs, send_sems, recv_sems, after,
                                                         _gather_first_copies(n), n * (N_CHIP - 1), _gather_relay_copies(n))
            bufs = _split_wait(gname + "_wait", bufs, send_sems, recv_sems, after, _gather_relay_copies(n))
        g["done"] = dict(zip(g["keys"], bufs[n:]))

    c_all = _allgather_small("gather_c", c.reshape(D // LANES, LANES)).reshape(N_DEV, D)
    token = first_hop("gather_in", c_all)
    mod_cols, c_act = _ada_mod(jnp.pad(c_all, ((0, 16 - N_DEV), (0, 0))) + token[0:1, 0:1], w_ada[0])
    mod_vec = mod_cols[:N_DEV].reshape(-1, LANES)
    mg_send, mg_recv, mg_bufs, token = _split_start(
        "gather_mod_start", [mod_vec, lax.dynamic_update_slice(lax.empty((N_DEV, *mod_vec.shape), F32), mod_vec[None], (me, 0, 0))],
        N_DEV - 1, _small_gather_copies)
    token = second_hop("gather_in", token)
    token = first_hop("gather_ffn_in", first_hop("gather_mixer", token))
    mod_all = _split_wait("gather_mod_wait", mg_bufs, mg_send, mg_recv, token, _small_gather_copies)[1].reshape(N_DEV, N_DEV, ada_loc)
    mod = lax.dynamic_index_in_dim(mod_all, me, axis=1, keepdims=False).reshape(1, N_DEV * ada_loc) + b_ada

    def fetch(key, after):
        if key == "in":
            finish("gather_in", after)
        elif key == "fi_early":
            return first_hop("gather_ffn_out", second_hop("gather_ffn_in", after))
        elif "done" not in groups[group_of[key]]:
            finish(group_of[key], after)
        arr = groups[group_of[key]]["done"][key]
        return arr.reshape(-1, D) if key in ("out", "fo") else arr

    p = dict(norm1_g=norm1_g, b_gate=b_gate, ln_g=gmlp_ln_g, ln_b=gmlp_ln_b, ws=gmlp_ws[0], bs=gmlp_bs[0], hg_lb=hg_lb,
             hg_ng=hg_norm_g, norm2_g=norm2_g, final_g=final_norm_g.reshape(1, D))

    in_flight = {}
    c_idx = lax.axis_index("c").astype(jnp.int32).reshape(1)
    chip_idx = (2 * lax.axis_index("x") + lax.axis_index("y")).astype(jnp.int32).reshape(1)

    def scatter_start(name, grads):
        keys = list(grads)
        n = len(keys)
        stacks = [grads[k].reshape(N_DEV, -1, grads[k].shape[-1]) for k in keys]
        lands = [empty_hbm((N_CHIP, *g.shape[1:]), g.dtype) for g in stacks]
        send_sems, recv_sems, bufs, token = _split_start(name + "_d2d", stacks + lands, n * N_CHIP, _to_sibling_copies(n))
        in_flight[name] = dict(keys=keys, stage1=(send_sems, recv_sems, bufs))
        return token

    def scatter_push(name, after):
        f = in_flight[name]
        n = len(f["keys"])
        send_sems, recv_sems, bufs = f["stage1"]
        bufs = _split_wait(name + "_d2d_wait", bufs, send_sems, recv_sems, after, _to_sibling_copies(n))
        sums = [_chip_sum(f"{name}_sum_{k}", bufs[i], bufs[n + i], c_idx) for i, k in enumerate(f["keys"])]
        lands = [empty_hbm((N_CHIP - 1, *s.shape[1:]), s.dtype) for s in sums]
        send_sems, recv_sems, bufs, token = _split_start(name + "_ici", sums + lands, n * (N_CHIP - 1), _to_owner_copies(n))
        f["stage2"] = (send_sems, recv_sems, bufs)
        return token

    grad_x, _, small, dmod = _local_step(x[0], loss_target[0], mod, p, fetch, w_ffn_out.shape[1] * N_DEV, (scatter_start, scatter_push))

    small["b_ada"] = dmod
    packed, offs = _pack([small[k] for k in _SMALL] + [small["loss"]])
    sg_send, sg_recv, sg_bufs, t_tail = _split_start(
        "gather_small_start", [packed, lax.dynamic_update_slice(lax.empty((N_DEV, *packed.shape), F32), packed[None], (me, 0, 0))],
        N_DEV - 1, _small_gather_copies)
    t_tail = scatter_push("scatter_proj_in_b", t_tail)
    big_w = dict(w_in=(w_in, m_w_in, v_w_in, "w_in"), bg=(w_branch_gmlp, m_w_branch_gmlp, v_w_branch_gmlp, "w_branch_gmlp"),
                 bh=(w_branch_hg, m_w_branch_hg, v_w_branch_hg, "w_branch_hg"), out=(w_out, m_w_out, v_w_out, "w_out"),
                 fi=(w_ffn_in, m_w_ffn_in, v_w_ffn_in, "w_ffn_in"), fo=(w_ffn_out, m_w_ffn_out, v_w_ffn_out, "w_ffn_out"))
    upd = {}

    def land_and_update(name, after):
        keys = in_flight[name]["keys"]
        n = len(keys)
        send_sems, recv_sems, bufs = in_flight[name]["stage2"]
        bufs = _split_wait(name + "_ici_wait", bufs, send_sems, recv_sems, after, _to_owner_copies(n))
        for i, k in enumerate(keys):
            if k in big_w:
                wt, mt, vt, out_name = big_w[k]
                upd[out_name] = _adamw("adamw_" + out_name, wt[0], mt[0], vt[0], bufs[i], chip_idx, [bufs[n + i]])
            else:
                wt, mt, vt, out_name = big_w["w_in"]
                upd[out_name] = _adamw("adamw_" + k, wt[0], mt[0], vt[0], bufs[i], chip_idx, [bufs[n + i]],
                                       row0=0 if k == "w_in_a" else bufs[i].shape[1], into=upd.get(out_name))
            after = upd[out_name][1]
        return after

    after = land_and_update("scatter_mixer", land_and_update("scatter_ffn", t_tail))
    gathered = _split_wait("gather_small_wait", sg_bufs, sg_send, sg_recv, after, _small_gather_copies)[1]
    wp = dict(p, b_ada=b_ada)
    ms = dict(b_ada=m_b_ada, norm1_g=m_norm1_g, b_gate=m_b_gate, ln_g=m_gmlp_ln_g, ln_b=m_gmlp_ln_b, ws=m_gmlp_ws, bs=m_gmlp_bs,
              hg_lb=m_hg_lb, hg_ng=m_hg_norm_g, norm2_g=m_norm2_g, final_g=m_final_norm_g)
    vs = dict(b_ada=v_b_ada, norm1_g=v_norm1_g, b_gate=v_b_gate, ln_g=v_gmlp_ln_g, ln_b=v_gmlp_ln_b, ws=v_gmlp_ws, bs=v_gmlp_bs,
              hg_lb=v_hg_lb, hg_ng=v_hg_norm_g, norm2_g=v_norm2_g, final_g=v_final_norm_g)
    w_sm, _ = _pack([wp[k] for k in _SMALL])
    m_sm, _ = _pack([ms[k] for k in _SMALL])
    v_sm, _ = _pack([vs[k] for k in _SMALL])
    shapes = dict(b_ada=b_ada.shape, norm1_g=norm1_g.shape, b_gate=b_gate.shape, ln_g=gmlp_ln_g.shape, ln_b=gmlp_ln_b.shape,
                  ws=gmlp_ws.shape, bs=gmlp_bs.shape, hg_lb=hg_lb.shape, hg_ng=hg_norm_g.shape, norm2_g=norm2_g.shape,
                  final_g=final_norm_g.shape)
    sm_out = _small_update(gathered, w_sm, m_sm, v_sm, after, [math.prod(shapes[k]) // LANES for k in _SMALL])

    def unpack(idx, k):
        return sm_out[idx * len(_SMALL) + _SMALL.index(k)].reshape(shapes[k])

    loss = sm_out[-1][0, 0]

    assert offs[0] == 0 and ada_loc % LANES == 0
    dmod_loc = lax.dynamic_slice_in_dim(gathered, me * (ada_loc // LANES), ada_loc // LANES, axis=1).reshape(N_DEV, ada_loc)
    ca_t = jnp.pad(c_act[:N_DEV].T, ((0, 0), (0, LANES - N_DEV))).astype(BF16)
    dm_p = jnp.pad(dmod_loc, ((0, LANES - N_DEV), (0, 0))).astype(BF16)
    tm_a = _tile(D, 512)
    g_ada = _matmul(
        "ada_dw", ca_t, dm_p, dims=_NN, grid_mnk=(D // tm_a, 1, 1), tiles=(tm_a, ada_loc),
        a_spec=pl.BlockSpec((tm_a, LANES), lambda i, j, k: (i, 0)), b_spec=pl.BlockSpec((LANES, ada_loc), lambda i, j, k: (0, 0)),
        out_shapes=[jax.ShapeDtypeStruct((1, D, ada_loc), F32)], out_specs=[pl.BlockSpec((None, tm_a, ada_loc), lambda i, j, k: (0, i, 0))],
        epilogue=_store(F32))[0]
    upd["w_ada"] = _adamw("adamw_w_ada", w_ada[0], m_w_ada[0], v_w_ada[0], g_ada, jnp.zeros((1,), jnp.int32))
    land_and_update("scatter_proj_in_b", land_and_update("scatter_proj_in_a", upd["w_ada"][1]))

    order = ("w_ada", "b_ada", "norm1_g", "w_in", "b_gate", "ln_g", "ln_b", "ws", "bs", "hg_lb", "hg_ng", "w_branch_gmlp", "w_branch_hg",
             "w_out", "norm2_g", "w_ffn_in", "w_ffn_out", "final_g")
    outs = [loss, grad_x[None]]
    for idx in range(4):
        for k in order:
            outs.append(upd[k][idx][None] if k in upd else unpack(idx, k))
    return tuple(outs)
```

```python
import functools
import math

import jax
import jax.numpy as jnp
from jax import lax
from jax.experimental import pallas as pl
from jax.experimental.pallas import tpu as pltpu

F32 = jnp.float32
BF16 = jnp.bfloat16
N_DEV = 8
EPS = 1e-6
LANES = 128
HG_DK = 128
HG_CHUNK = 64
HG_MID = HG_CHUNK // 2 - 1
EXP_CLAMP = 80.0
VMEM_LIMIT = 48 * 1024 * 1024
BF16_ROWS = 16
STREAM_TILE = 1 << 20
ADAMW_TILE = 3 << 17
ADAM_LR, ADAM_B1, ADAM_B2, ADAM_EPS, ADAM_WD, ADAM_STEP = 0.001, 0.9, 0.999, 1e-08, 0.01, 10
MESH = pl.DeviceIdType.MESH

_NN = (((1,), (0,)), ((), ()))
_NT = (((1,), (1,)), ((), ()))
_TN = (((0,), (0,)), ((), ()))


def _dot(a, b, dims=_NN):
    return lax.dot_general(a.astype(BF16), b.astype(BF16), dims, preferred_element_type=F32)


def _tile(n, target, mult=LANES):
    best = None
    for t in range(mult, min(n, target) + 1, mult):
        if n % t == 0:
            best = t
    return n if best is None else best


def _cparams(sem):
    return pltpu.CompilerParams(dimension_semantics=sem, vmem_limit_bytes=VMEM_LIMIT)


def _sigmoid(x):
    return 1.0 / (1.0 + jnp.exp(-x))


def _gelu_parts(x):
    k0 = math.sqrt(2.0 / math.pi)
    x2 = x * x
    t = jnp.tanh(k0 * (x + 0.044715 * x * x2))
    g = 0.5 * x * (1.0 + t)
    dg = 0.5 * (1.0 + t) + 0.5 * x * (1.0 - t * t) * (k0 * (1.0 + 3.0 * 0.044715 * x2))
    return g, dg


def _split3(x):
    h = x.astype(BF16)
    r = x - h.astype(F32)
    m = r.astype(BF16)
    lo = (r - m.astype(F32)).astype(BF16)
    return h, m, lo


def _ones_dot(mat01, x):
    h, m, lo = _split3(x)
    d = functools.partial(lax.dot_general, dimension_numbers=_NN, preferred_element_type=F32)
    return d(mat01, h) + d(mat01, m) + d(mat01, lo)


def _matmul(name, a, b, *, dims, grid_mnk, tiles, a_spec, b_spec, extras=(), extra_specs=(), out_shapes, out_specs, epilogue, after=None,
            sem=None):
    gm, gn, nk = grid_mnk
    tm, tn = tiles
    n_ex, n_out = len(extras), len(out_shapes)
    held = [] if after is None else [after]

    def body(*refs):
        a_ref, b_ref = refs[0], refs[1]
        ex = refs[2:2 + n_ex]
        outs = refs[2 + n_ex + len(held):2 + n_ex + len(held) + n_out]
        more = () if sem is None else (pl.program_id(0) == 0,)
        if nk == 1:
            epilogue(lax.dot_general(a_ref[...], b_ref[...], dims, preferred_element_type=F32), ex, outs, *more)
            return
        acc = refs[-1]
        k = pl.program_id(2)

        @pl.when(k == 0)
        def _():
            acc[...] = jnp.zeros_like(acc)

        acc[...] += lax.dot_general(a_ref[...], b_ref[...], dims, preferred_element_type=F32)

        @pl.when(k == nk - 1)
        def _():
            epilogue(acc[...], ex, outs, *more)

    return pl.pallas_call(
        body, name=name, grid=(gm, gn, nk), in_specs=[a_spec, b_spec, *extra_specs] + [pl.BlockSpec(memory_space=pl.ANY)] * len(held),
        out_specs=list(out_specs), out_shape=list(out_shapes), scratch_shapes=[] if nk == 1 else [pltpu.VMEM((tm, tn), F32)],
        compiler_params=_cparams(sem or ("parallel", "parallel", "arbitrary")),
    )(a, b, *extras, *held)


def _store(dtype):
    def ep(acc, ex, outs):
        outs[0][...] = acc.astype(dtype)
    return ep


def _mm_nn_stacked(name, a, wg, *, tm, tn, tk, out_dtype=F32, extras=(), extra_specs=(), out_shapes=None, out_specs=None, epilogue=None,
                   after=None):
    M, K = a.shape
    _, _, nloc = wg.shape
    N = nloc * N_DEV
    q = nloc // tn
    if out_shapes is None:
        out_shapes = [jax.ShapeDtypeStruct((M, N), out_dtype)]
        out_specs = [pl.BlockSpec((tm, tn), lambda i, j, k: (i, j))]
        epilogue = _store(out_dtype)
    return _matmul(
        name, a, wg, dims=_NN, grid_mnk=(M // tm, N // tn, K // tk), tiles=(tm, tn),
        a_spec=pl.BlockSpec((tm, tk), lambda i, j, k: (i, k)),
        b_spec=pl.BlockSpec((None, tk, tn), lambda i, j, k: (j // q, k, j % q)),
        extras=extras, extra_specs=extra_specs, out_shapes=out_shapes, out_specs=out_specs, epilogue=epilogue, after=after)


def _mm_nt_stacked(name, a_spec, a, wg, *, M, tm, tn, tk, out_dtype=F32, after=None, extras=(), extra_specs=(), out_shapes=None,
                   out_specs=None, epilogue=None, sem=None):
    _, Kw, nloc = wg.shape
    q = nloc // tk
    single = out_shapes is None
    if single:
        out_shapes = [jax.ShapeDtypeStruct((M, Kw), out_dtype)]
        out_specs = [pl.BlockSpec((tm, tn), lambda i, j, k: (i, j))]
        epilogue = _store(out_dtype)
    res = _matmul(
        name, a, wg, dims=_NT, grid_mnk=(M // tm, Kw // tn, (nloc * N_DEV) // tk), tiles=(tm, tn),
        a_spec=a_spec, b_spec=pl.BlockSpec((None, tn, tk), lambda i, j, k: (k // q, j, k % q)),
        extras=extras, extra_specs=extra_specs, out_shapes=out_shapes, out_specs=out_specs, epilogue=epilogue, after=after, sem=sem)
    return res[0] if single else res


def _mm_tn(name, a, b, b_spec, *, Mo, No, S, tm, tn, tk, stacked_nloc=None, after=None, a_off=0):
    if stacked_nloc is None:
        out_shape = jax.ShapeDtypeStruct((Mo, No), BF16)
        out_spec = pl.BlockSpec((tm, tn), lambda i, j, k: (i, j))
    else:
        q = stacked_nloc // tn
        out_shape = jax.ShapeDtypeStruct((N_DEV, Mo, stacked_nloc), BF16)
        out_spec = pl.BlockSpec((None, tm, tn), lambda i, j, k: (j // q, i, j % q))
    return _matmul(
        name, a, b, dims=_TN, grid_mnk=(Mo // tm, No // tn, S // tk), tiles=(tm, tn),
        a_spec=pl.BlockSpec((tk, tm), lambda i, j, k: (k, i + a_off)), b_spec=b_spec,
        out_shapes=[out_shape], out_specs=[out_spec], epilogue=_store(BF16), after=after)[0]


def _norm_mod(name, x, g, sc, sh):
    S, D = x.shape
    tm = _tile(S, 256, 8)

    def body(x_ref, g_ref, sc_ref, sh_ref, h_ref):
        xv = x_ref[...]
        r = lax.rsqrt(jnp.mean(xv * xv, axis=-1, keepdims=True) + EPS)
        h = (xv * r) * g_ref[...]
        h_ref[...] = (h * (1.0 + sc_ref[...]) + sh_ref[...]).astype(BF16)

    row = pl.BlockSpec((tm, D), lambda i: (i, 0))
    vec = pl.BlockSpec((1, D), lambda i: (0, 0))
    return pl.pallas_call(body, name=name, grid=(S // tm,), in_specs=[row, vec, vec, vec], out_specs=row,
                          out_shape=jax.ShapeDtypeStruct((S, D), BF16), compiler_params=_cparams(("parallel",)))(x, g, sc, sh)


def _norm_mod_bwd_rows(first, dh_v, x_ref, g_ref, sc_ref, dres_ref, dx_ref, vec_ref, o_ref=None, gt_ref=None, do_ref=None):
    @pl.when(first)
    def _():
        vec_ref[...] = jnp.zeros_like(vec_ref)

    xv, gv = x_ref[...], g_ref[...]
    r = lax.rsqrt(jnp.mean(xv * xv, axis=-1, keepdims=True) + EPS)
    xn = xv * r
    one_sc = 1.0 + sc_ref[...]
    vec_ref[0:1, :] += jnp.sum(dh_v, axis=0, keepdims=True)
    vec_ref[1:2, :] += jnp.sum(dh_v * (xn * gv), axis=0, keepdims=True)
    vec_ref[2:3, :] += jnp.sum(dh_v * one_sc * xn, axis=0, keepdims=True)
    dxn = dh_v * one_sc * gv
    dx = dres_ref[...] + r * (dxn - xn * jnp.mean(dxn * xn, axis=-1, keepdims=True))
    dx_ref[...] = dx
    if o_ref is not None:
        vec_ref[3:4, :] += jnp.sum(dx * o_ref[...], axis=0, keepdims=True)
        do_ref[...] = (dx * gt_ref[...]).astype(BF16)


def _norm_mod_bwd(name, dh, x, g, sc, dres, o=None, gt=None):
    S, D = x.shape
    tm = _tile(S, 256, 8)
    gated = o is not None

    def body(*refs):
        if gated:
            dh_ref, x_ref, g_ref, sc_ref, dres_ref, o_ref, gt_ref, dx_ref, vec_ref, do_ref = refs
        else:
            dh_ref, x_ref, g_ref, sc_ref, dres_ref, dx_ref, vec_ref = refs
            o_ref = gt_ref = do_ref = None
        _norm_mod_bwd_rows(pl.program_id(0) == 0, dh_ref[...], x_ref, g_ref, sc_ref, dres_ref, dx_ref, vec_ref, o_ref, gt_ref, do_ref)

    row = pl.BlockSpec((tm, D), lambda i: (i, 0))
    vec = pl.BlockSpec((1, D), lambda i: (0, 0))
    acc = pl.BlockSpec((8, D), lambda i: (0, 0))
    ins = [dh, x, g, sc, dres] + ([o, gt] if gated else [])
    in_specs = [row, row, vec, vec, row] + ([row, vec] if gated else [])
    out_shape = [jax.ShapeDtypeStruct((S, D), F32), jax.ShapeDtypeStruct((8, D), F32)]
    out_specs = [row, acc]
    if gated:
        out_shape.append(jax.ShapeDtypeStruct((S, D), BF16))
        out_specs.append(row)
    return pl.pallas_call(body, name=name, grid=(S // tm,), in_specs=in_specs, out_specs=out_specs, out_shape=out_shape,
                          compiler_params=_cparams(("arbitrary",)))(*ins)


def _loss_head(x3, tgt, gf, o2, gt2):
    S, D = x3.shape
    tm = _tile(S, 256, 8)

    def body(x_ref, t_ref, g_ref, o_ref, gt_ref, dx_ref, do_ref, vec_ref):
        i = pl.program_id(0)

        @pl.when(i == 0)
        def _():
            vec_ref[...] = jnp.zeros_like(vec_ref)

        xv, gv = x_ref[...], g_ref[...]
        r = lax.rsqrt(jnp.mean(xv * xv, axis=-1, keepdims=True) + EPS)
        xn = xv * r
        e = xn * gv - t_ref[...]
        tok = 0.5 * jnp.mean(e * e, axis=-1, keepdims=True)
        vec_ref[0:1, :] += jnp.broadcast_to(jnp.sum(tok, axis=0, keepdims=True), (1, D))
        dy = e * (1.0 / D)
        vec_ref[1:2, :] += jnp.sum(dy * xn, axis=0, keepdims=True)
        dxn = dy * gv
        dx = r * (dxn - xn * jnp.mean(dxn * xn, axis=-1, keepdims=True))
        dx_ref[...] = dx
        vec_ref[2:3, :] += jnp.sum(dx * o_ref[...], axis=0, keepdims=True)
        do_ref[...] = (dx * gt_ref[...]).astype(BF16)

    row = pl.BlockSpec((tm, D), lambda i: (i, 0))
    vec = pl.BlockSpec((1, D), lambda i: (0, 0))
    return pl.pallas_call(
        body, name="loss_head", grid=(S // tm,), in_specs=[row, row, vec, row, vec],
        out_specs=[row, row, pl.BlockSpec((8, D), lambda i: (0, 0))],
        out_shape=[jax.ShapeDtypeStruct((S, D), F32), jax.ShapeDtypeStruct((S, D), BF16), jax.ShapeDtypeStruct((8, D), F32)],
        compiler_params=_cparams(("arbitrary",)))(x3, tgt, gf, o2, gt2)


def _ffn_in_swiglu(h, wg):
    S, D = h.shape
    _, _, tf = wg.shape
    nf = N_DEV // 2
    F = nf * tf
    tm = _tile(S, 256, 16)

    def body(h_ref, wa_ref, wu_ref, hf_ref, fac_ref):
        hv = h_ref[...]
        a = lax.dot_general(hv, wa_ref[...], _NN, preferred_element_type=F32)
        up = lax.dot_general(hv, wu_ref[...], _NN, preferred_element_type=F32)
        sa = _sigmoid(a)
        silu = a * sa
        hf_ref[...] = (silu * up).astype(BF16)
        fac_ref[0] = (up * (sa * (1.0 + a * (1.0 - sa)))).astype(BF16)
        fac_ref[1] = silu.astype(BF16)

    return pl.pallas_call(
        body, name="ffn_in_swiglu", grid=(nf, S // tm),
        in_specs=[pl.BlockSpec((tm, D), lambda j, i: (i, 0)), pl.BlockSpec((None, D, tf), lambda j, i: (j, 0, 0)),
                  pl.BlockSpec((None, D, tf), lambda j, i: (j + nf, 0, 0))],
        out_specs=[pl.BlockSpec((tm, tf), lambda j, i: (i, j)), pl.BlockSpec((2, tm, tf), lambda j, i: (0, i, j))],
        out_shape=[jax.ShapeDtypeStruct((S, F), BF16), jax.ShapeDtypeStruct((2, S, F), BF16)],
        compiler_params=_cparams(("parallel", "parallel")))(h, wg, wg)


def _gmlp_common(u_ref, v_ref, lg_ref, lb_ref, ws_ref, bsb_ref, G, T, Dg):
    ug, dug = _gelu_parts(u_ref[...])
    vg, dvg = _gelu_parts(v_ref[...])
    mu = jnp.mean(vg, axis=-1, keepdims=True)
    vc = vg - mu
    rstd = lax.rsqrt(jnp.mean(vc * vc, axis=-1, keepdims=True) + EPS)
    vhat = vc * rstd
    vn = vhat * lg_ref[...] + lb_ref[...]
    row = lax.broadcasted_iota(jnp.int32, (T, T), 0)
    col = lax.broadcasted_iota(jnp.int32, (T, T), 1)
    tril = row >= col
    s = []
    for g in range(G):
        w = jnp.where(tril, ws_ref[g], 0.0)
        s.append(_dot(w, vn[:, g * Dg:(g + 1) * Dg]) + bsb_ref[g])
    return ug, dug, dvg, rstd, vhat, vn, tril, s


def _gmlp_fwd(z, ln_g, ln_b, ws, bsb, GW):
    S = z.shape[0]
    G, T, _ = ws.shape
    Dg = GW // G

    def body(u_ref, v_ref, lg_ref, lb_ref, ws_ref, bsb_ref, ya_ref):
        ug, _, _, _, _, _, _, s = _gmlp_common(u_ref, v_ref, lg_ref, lb_ref, ws_ref, bsb_ref, G, T, Dg)
        for g in range(G):
            sl = slice(g * Dg, (g + 1) * Dg)
            ya_ref[:, sl] = (ug[:, sl] * s[g]).astype(BF16)

    vec = pl.BlockSpec((1, GW), lambda c: (0, 0))
    return pl.pallas_call(
        body, name="gmlp_fwd", grid=(S // T,),
        in_specs=[pl.BlockSpec((T, GW), lambda c: (c, 0)), pl.BlockSpec((T, GW), lambda c: (c, 1)), vec, vec,
                  pl.BlockSpec((G, T, T), lambda c: (0, 0, 0)), pl.BlockSpec((G, T, Dg), lambda c: (0, 0, 0))],
        out_specs=pl.BlockSpec((T, GW), lambda c: (c, 0)), out_shape=jax.ShapeDtypeStruct((S, GW), BF16),
        compiler_params=_cparams(("parallel",)))(z, z, ln_g, ln_b, ws, bsb)


def _gmlp_bwd(z, dya, ln_g, ln_b, ws, bsb, GW):
    S = z.shape[0]
    G, T, _ = ws.shape
    Dg = GW // G
    nc = S // T

    def body(u_ref, v_ref, dya_ref, lg_ref, lb_ref, ws_ref, bsb_ref, dz_ref, dln_ref, dws_ref, dbs_ref, dbs_acc, dvh):
        c = pl.program_id(0)

        @pl.when(c == 0)
        def _():
            dln_ref[...] = jnp.zeros_like(dln_ref)
            dws_ref[...] = jnp.zeros_like(dws_ref)
            dbs_acc[...] = jnp.zeros_like(dbs_acc)

        ug, dug, dvg, rstd, vhat, vn, tril, s = _gmlp_common(u_ref, v_ref, lg_ref, lb_ref, ws_ref, bsb_ref, G, T, Dg)
        dya_v = dya_ref[...]
        for g in range(G):
            sl = slice(g * Dg, (g + 1) * Dg)
            dy_g = dya_v[:, sl]
            dz_ref[:, sl] = (dy_g * s[g] * dug[:, sl]).astype(BF16)
            ds = dy_g * ug[:, sl]
            dbs_acc[g] += ds
            w = jnp.where(tril, ws_ref[g], 0.0)
            dvn_g = _dot(w, ds, _TN)
            dws_ref[g] += jnp.where(tril, _dot(ds, vn[:, sl], _NT), 0.0)
            dln_ref[0:1, sl] += jnp.sum(dvn_g * vhat[:, sl], axis=0, keepdims=True)
            dln_ref[1:2, sl] += jnp.sum(dvn_g, axis=0, keepdims=True)
            dvh[:, sl] = dvn_g * lg_ref[:, sl]
        dvhat = dvh[...]
        m1 = jnp.mean(dvhat, axis=-1, keepdims=True)
        m2 = jnp.mean(dvhat * vhat, axis=-1, keepdims=True)
        dz_ref[:, GW:2 * GW] = (rstd * (dvhat - m1 - vhat * m2) * dvg).astype(BF16)

        @pl.when(c == nc - 1)
        def _():
            for g in range(G):
                dbs_ref[g] = jnp.sum(dbs_acc[g], axis=-1, keepdims=True)

    vec = pl.BlockSpec((1, GW), lambda c: (0, 0))
    return pl.pallas_call(
        body, name="gmlp_bwd", grid=(nc,),
        in_specs=[pl.BlockSpec((T, GW), lambda c: (c, 0)), pl.BlockSpec((T, GW), lambda c: (c, 1)),
                  pl.BlockSpec((T, GW), lambda c: (c, 0)), vec, vec,
                  pl.BlockSpec((G, T, T), lambda c: (0, 0, 0)), pl.BlockSpec((G, T, Dg), lambda c: (0, 0, 0))],
        out_specs=[pl.BlockSpec((T, 2 * GW), lambda c: (c, 0)), pl.BlockSpec((8, GW), lambda c: (0, 0)),
                   pl.BlockSpec((G, T, T), lambda c: (0, 0, 0)), pl.BlockSpec((G, T, 1), lambda c: (0, 0, 0))],
        out_shape=[jax.ShapeDtypeStruct((S, 2 * GW), BF16), jax.ShapeDtypeStruct((8, GW), F32),
                   jax.ShapeDtypeStruct((G, T, T), F32), jax.ShapeDtypeStruct((G, T, 1), F32)],
        scratch_shapes=[pltpu.VMEM((G, T, Dg), F32), pltpu.VMEM((T, GW), F32)],
        compiler_params=_cparams(("arbitrary",)))(z, z, dya, ln_g, ln_b, ws, bsb)


def _hg_common(q_ref, f_ref, hlb_ref):
    C = HG_CHUNK
    a = hlb_ref[...]
    lb = _sigmoid(a[0:1, :] - a[1:2, :])
    sig = _sigmoid(f_ref[...])
    f = lb + (1.0 - lb) * sig
    lf = jnp.log(f)
    kk = 1.0 - f
    q = q_ref[...]
    sq = _sigmoid(q)
    qa = q * sq
    row = lax.broadcasted_iota(jnp.int32, (C, C), 0)
    col = lax.broadcasted_iota(jnp.int32, (C, C), 1)
    tril = row >= col
    b = _ones_dot(tril.astype(BF16), lf)
    bm = b[HG_MID:HG_MID + 1, :]
    bl = b[C - 1:C, :]
    e_b = jnp.exp(b)
    e_qm = jnp.exp(jnp.minimum(b - bm, EXP_CLAMP))
    e_km = jnp.exp(jnp.minimum(bm - b, EXP_CLAMP))
    e_kl = jnp.exp(bl - b)
    return dict(lb=lb, sig=sig, f=f, kk=kk, q=q, sq=sq, qa=qa, tril=tril, e_b=e_b, e_qm=e_qm, e_km=e_km, e_kl=e_kl,
                e_l=jnp.exp(bl), qh=qa * e_b, qt=qa * e_qm, kt=kk * e_km, kh=kk * e_kl)


def _hg_fwd(z, hg_lb, ng, HW):
    S = z.shape[0]
    C, H, dk = HG_CHUNK, HW // HG_DK, HG_DK
    nc = S // C

    def body(q_ref, f_ref, i_ref, og_ref, hlb_ref, ng_ref, yb_ref, o_ref, st_ref, state):
        @pl.when(pl.program_id(0) == 0)
        def _():
            state[...] = jnp.zeros_like(state)

        t = _hg_common(q_ref, f_ref, hlb_ref)
        iv = i_ref[...]
        for h in range(H):
            sl = slice(h * dk, (h + 1) * dk)
            st = state[h]
            st_ref[h] = st
            a = jnp.where(t["tril"], _dot(t["qt"][:, sl], t["kt"][:, sl], _NT), 0.0)
            o_h = _dot(a, iv[:, sl]) + _dot(t["qh"][:, sl], st, _NT)
            state[h] = st * t["e_l"][:, sl] + _dot(iv[:, sl], t["kh"][:, sl], _TN)
            o_ref[:, sl] = o_h
            rr = lax.rsqrt(jnp.mean(o_h * o_h, axis=-1, keepdims=True) + EPS)
            og = og_ref[:, sl]
            yb_ref[:, sl] = (o_h * rr * ng_ref[:, sl] * (og * _sigmoid(og))).astype(BF16)

    def col(k):
        return pl.BlockSpec((C, HW), lambda c: (c, k))

    base = 2
    return pl.pallas_call(
        body, name="hgrn_fwd", grid=(nc,),
        in_specs=[col(base), col(base + 1), col(base + 2), col(base + 3),
                  pl.BlockSpec((2, HW), lambda c: (0, 0)), pl.BlockSpec((1, HW), lambda c: (0, 0))],
        out_specs=[pl.BlockSpec((C, HW), lambda c: (c, 0)), pl.BlockSpec((C, HW), lambda c: (c, 0)),
                   pl.BlockSpec((None, H, dk, dk), lambda c: (c, 0, 0, 0))],
        out_shape=[jax.ShapeDtypeStruct((S, HW), BF16), jax.ShapeDtypeStruct((S, HW), F32),
                   jax.ShapeDtypeStruct((nc, H, dk, dk), F32)],
        scratch_shapes=[pltpu.VMEM((H, dk, dk), F32)],
        compiler_params=_cparams(("arbitrary",)))(z, z, z, z, hg_lb, ng)


def _hg_bwd(z, o, states, dyb, hg_lb, ng, HW, dz_head, dz_tail):
    S = z.shape[0]
    C, H, dk = HG_CHUNK, HW // HG_DK, HG_DK
    nc = S // C
    B0 = dz_head.shape[1]
    DT = dz_tail.shape[2]
    INW = B0 + 4 * HW + 2 * DT

    def body(q_ref, f_ref, i_ref, og_ref, o_ref, st_ref, stn_ref, dyb_ref, hlb_ref, ng_ref, head_ref, tail_ref,
             dzf_ref, dng_ref, dhlb_ref, dtail_ref, dstate, cross, dqa_buf, dkk_buf, db_buf, dlb_acc):
        c = pl.program_id(0)
        dzf_ref[:, 0:B0] = head_ref[...]
        dzf_ref[:, B0 + 4 * HW:B0 + 4 * HW + DT] = tail_ref[0]
        dzf_ref[:, B0 + 4 * HW + DT:INW] = tail_ref[1]
        dz_ref = dzf_ref.at[:, B0:B0 + 4 * HW]

        @pl.when(c == 0)
        def _():
            dtail_ref[...] = jnp.zeros_like(dtail_ref)

        dtail_ref[0:1, :] += jnp.sum(tail_ref[0].astype(F32), axis=0, keepdims=True)
        dtail_ref[1:2, :] += jnp.sum(tail_ref[1].astype(F32), axis=0, keepdims=True)

        @pl.when(c == 0)
        def _():
            dstate[...] = jnp.zeros_like(dstate)
            dlb_acc[...] = jnp.zeros_like(dlb_acc)
            dng_ref[...] = jnp.zeros_like(dng_ref)

        def r16(v):
            return v.astype(BF16).astype(F32)

        t = _hg_common(q_ref, f_ref, hlb_ref)
        iv = i_ref[...]
        for h in range(H):
            sl = slice(h * dk, (h + 1) * dk)
            o_h, og, dyb_h, ng_h = o_ref[:, sl], og_ref[:, sl], dyb_ref[:, sl], ng_ref[:, sl]
            sg = _sigmoid(og)
            silu_og = og * sg
            rr = lax.rsqrt(jnp.mean(o_h * o_h, axis=-1, keepdims=True) + EPS)
            on = o_h * rr
            dng_ref[0:1, sl] += jnp.sum(dyb_h * on * silu_og, axis=0, keepdims=True)
            dz_ref[:, 3 * HW + h * dk:3 * HW + (h + 1) * dk] = (dyb_h * on * ng_h * (sg * (1.0 + og * (1.0 - sg)))).astype(BF16)
            don = dyb_h * ng_h * silu_og
            do_h = rr * (don - on * jnp.mean(don * on, axis=-1, keepdims=True))

            qt, kt, qh, kh, iv_h = t["qt"][:, sl], t["kt"][:, sl], t["qh"][:, sl], t["kh"][:, sl], iv[:, sl]
            a = jnp.where(t["tril"], _dot(qt, kt, _NT), 0.0)
            da = jnp.where(t["tril"], _dot(do_h, iv_h, _NT), 0.0)
            st, dst = st_ref[h], dstate[h]
            cross[:, sl] = jnp.sum(dst * stn_ref[h], axis=0, keepdims=True)
            dqh = _dot(do_h, st)
            dstate[h] = _dot(do_h, qh, _TN) + dst * t["e_l"][:, sl]
            div = _dot(a, do_h, _TN) + _dot(kh, dst, _NT)
            dkh = _dot(iv_h, dst)
            dqt = _dot(da, kt)
            dkt = _dot(da, qt, _TN)
            dz_ref[:, 2 * HW + h * dk:2 * HW + (h + 1) * dk] = div.astype(BF16)
            dqa_buf[:, sl] = dqh * t["e_b"][:, sl] + dqt * t["e_qm"][:, sl]
            dkk_buf[:, sl] = dkt * t["e_km"][:, sl] + dkh * t["e_kl"][:, sl]
            db_buf[:, sl] = r16(qt) * dqt - r16(kt) * dkt + r16(qh) * dqh - r16(kh) * dkh

        dqa, dkk = dqa_buf[...], dkk_buf[...]
        triu = jnp.logical_not(t["tril"]) | (lax.broadcasted_iota(jnp.int32, (C, C), 0) == lax.broadcasted_iota(jnp.int32, (C, C), 1))
        dlf = _ones_dot(triu.astype(BF16), db_buf[...]) + cross[...]
        df = dlf / t["f"] - dkk
        sig, lb = t["sig"], t["lb"]
        dz_ref[:, HW:2 * HW] = (df * (1.0 - lb) * sig * (1.0 - sig)).astype(BF16)
        dlb_acc[...] += jnp.sum(df * (1.0 - sig), axis=0, keepdims=True)
        q, sq = t["q"], t["sq"]
        dz_ref[:, 0:HW] = (dqa * (sq * (1.0 + q * (1.0 - sq)))).astype(BF16)

        @pl.when(c == nc - 1)
        def _():
            da0 = dlb_acc[...] * lb * (1.0 - lb)
            dhlb_ref[0:1, :] = da0
            dhlb_ref[1:2, :] = -da0

    def col(k):
        return pl.BlockSpec((C, HW), lambda c: (nc - 1 - c, k))

    base = 2
    return pl.pallas_call(
        body, name="hgrn_bwd", grid=(nc,),
        in_specs=[col(base), col(base + 1), col(base + 2), col(base + 3), col(0),
                  pl.BlockSpec((None, H, dk, dk), lambda c: (nc - 1 - c, 0, 0, 0)),
                  pl.BlockSpec((None, H, dk, dk), lambda c: (jnp.minimum(nc - c, nc - 1), 0, 0, 0)), col(0),
                  pl.BlockSpec((2, HW), lambda c: (0, 0)), pl.BlockSpec((1, HW), lambda c: (0, 0)),
                  pl.BlockSpec((C, B0), lambda c: (nc - 1 - c, 0)), pl.BlockSpec((2, C, DT), lambda c: (0, nc - 1 - c, 0))],
        out_specs=[pl.BlockSpec((C, INW), lambda c: (nc - 1 - c, 0)), pl.BlockSpec((8, HW), lambda c: (0, 0)),
                   pl.BlockSpec((2, HW), lambda c: (0, 0)), pl.BlockSpec((2, DT), lambda c: (0, 0))],
        out_shape=[jax.ShapeDtypeStruct((S, INW), BF16), jax.ShapeDtypeStruct((8, HW), F32), jax.ShapeDtypeStruct((2, HW), F32),
                   jax.ShapeDtypeStruct((2, DT), F32)],
        scratch_shapes=[pltpu.VMEM((H, dk, dk), F32), pltpu.VMEM((1, HW), F32), pltpu.VMEM((C, HW), F32), pltpu.VMEM((C, HW), F32),
                        pltpu.VMEM((C, HW), F32), pltpu.VMEM((1, HW), F32)],
        compiler_params=_cparams(("arbitrary",)))(z, z, z, z, o, states, states, dyb, hg_lb, ng, dz_head, dz_tail)


def _position():
    x, y, c = lax.axis_index("x"), lax.axis_index("y"), lax.axis_index("c")
    return x, y, c, 4 * x + 2 * y + c


def _flip(x, y, c, k):
    return (1 - x if k & 4 else x, 1 - y if k & 2 else y, 1 - c if k & 1 else c)


def _allgather_small(name, v):
    R, L = v.shape

    def body(v_ref, out_ref, send_sems, recv_sems):
        x, y, c, me = _position()
        out_ref[me] = v_ref[...]
        copies = []
        for k in range(1, N_DEV):
            cp = pltpu.make_async_remote_copy(src_ref=v_ref, dst_ref=out_ref.at[me], send_sem=send_sems.at[k - 1],
                                              recv_sem=recv_sems.at[k - 1], device_id=_flip(x, y, c, k), device_id_type=MESH)
            cp.start()
            copies.append(cp)
        for cp in copies:
            cp.wait()

    return pl.pallas_call(
        body, name=name, out_shape=jax.ShapeDtypeStruct((N_DEV, R, L), v.dtype),
        in_specs=[pl.BlockSpec(memory_space=pltpu.VMEM)], out_specs=pl.BlockSpec(memory_space=pltpu.VMEM),
        scratch_shapes=[pltpu.SemaphoreType.DMA((N_DEV - 1,)), pltpu.SemaphoreType.DMA((N_DEV - 1,))],
        compiler_params=pltpu.CompilerParams(vmem_limit_bytes=VMEM_LIMIT),
    )(v)


_HBM = pl.BlockSpec(memory_space=pltpu.HBM)
_SEM = pl.BlockSpec(memory_space=pltpu.SEMAPHORE)
_EFFECT = pltpu.SideEffectType.DATAFLOW_SIDE_EFFECTING


def _split_start(name, bufs, n_sems, copies_fn, after=None):
    nb = len(bufs)
    extra = [] if after is None else [after]
    k = nb + len(extra)

    def body(*refs):
        for cp in copies_fn(refs[:nb], refs[k], refs[k + 1]):
            cp.start()
        refs[-1][...] = jnp.zeros_like(refs[-1])

    sems = pltpu.SemaphoreType.DMA((n_sems,))
    res = pl.pallas_call(
        body, name=name,
        out_shape=(sems, sems, *[pltpu.HBM(a.shape, a.dtype) for a in bufs], jax.ShapeDtypeStruct((8, LANES), F32)),
        in_specs=[_HBM] * nb + [pl.BlockSpec(memory_space=pl.ANY)] * len(extra),
        out_specs=(_SEM, _SEM, *[_HBM] * nb, pl.BlockSpec(memory_space=pltpu.VMEM)),
        input_output_aliases={i: 2 + i for i in range(nb)},
        compiler_params=pltpu.CompilerParams(has_side_effects=_EFFECT),
    )(*[pltpu.with_memory_space_constraint(a, pltpu.HBM) for a in bufs], *extra)
    return res[0], res[1], list(res[2:2 + nb]), res[-1]


def _split_wait(name, bufs, send_sems, recv_sems, after, copies_fn):
    nb = len(bufs)

    def body(*refs):
        for cp in copies_fn(refs[:nb], refs[nb], refs[nb + 1]):
            cp.wait_send()
            cp.wait_recv()

    res = pl.pallas_call(
        body, name=name, out_shape=tuple(pltpu.HBM(a.shape, a.dtype) for a in bufs),
        in_specs=[_HBM] * nb + [_SEM, _SEM, pl.BlockSpec(memory_space=pl.ANY)], out_specs=tuple([_HBM] * nb),
        input_output_aliases={i: i for i in range(nb)},
        compiler_params=pltpu.CompilerParams(has_side_effects=_EFFECT),
    )(*bufs, send_sems, recv_sems, after)
    return list(res)


def _split_relay(name, bufs, send_sems, recv_sems, after, wait_fn, n_sems, start_fn):
    nb = len(bufs)

    def body(*refs):
        for cp in wait_fn(refs[:nb], refs[nb], refs[nb + 1]):
            cp.wait_send()
            cp.wait_recv()
        for cp in start_fn(refs[:nb], refs[nb + 3], refs[nb + 4]):
            cp.start()
        refs[-1][...] = jnp.zeros_like(refs[-1])

    sems = pltpu.SemaphoreType.DMA((n_sems,))
    res = pl.pallas_call(
        body, name=name, out_shape=(sems, sems, *[pltpu.HBM(a.shape, a.dtype) for a in bufs], jax.ShapeDtypeStruct((8, LANES), F32)),
        in_specs=[_HBM] * nb + [_SEM, _SEM, pl.BlockSpec(memory_space=pl.ANY)],
        out_specs=(_SEM, _SEM, *[_HBM] * nb, pl.BlockSpec(memory_space=pltpu.VMEM)),
        input_output_aliases={i: 2 + i for i in range(nb)},
        compiler_params=pltpu.CompilerParams(has_side_effects=_EFFECT),
    )(*bufs, send_sems, recv_sems, after)
    return res[0], res[1], list(res[2:2 + nb]), res[-1]


N_CHIP = 4


def _chip_flip(x, y, k):
    return (1 - x if k & 2 else x), (1 - y if k & 1 else y)


def _gather_first_copies(n):
    def copies(bufs, send_sems, recv_sems):
        x, y, c, me = _position()
        out = []
        for w in range(n):
            for k in range(N_CHIP):
                to = (x, y, 1 - c) if k == 0 else (*_chip_flip(x, y, k), c)
                out.append(pltpu.make_async_remote_copy(
                    src_ref=bufs[w], dst_ref=bufs[n + w].at[me], send_sem=send_sems.at[w * N_CHIP + k],
                    recv_sem=recv_sems.at[w * N_CHIP + k], device_id=to, device_id_type=MESH))
        return out
    return copies


def _gather_relay_copies(n):
    def copies(bufs, send_sems, recv_sems):
        x, y, c, _ = _position()
        out = []
        for w in range(n):
            for k in range(1, N_CHIP):
                px, py = _chip_flip(x, y, k)
                blk = bufs[n + w].at[4 * px + 2 * py + c]
                out.append(pltpu.make_async_remote_copy(
                    src_ref=blk, dst_ref=blk, send_sem=send_sems.at[w * (N_CHIP - 1) + k - 1],
                    recv_sem=recv_sems.at[w * (N_CHIP - 1) + k - 1], device_id=(x, y, 1 - c), device_id_type=MESH))
        return out
    return copies


def _small_gather_copies(bufs, send_sems, recv_sems):
    x, y, c, me = _position()
    return [pltpu.make_async_remote_copy(src_ref=bufs[0], dst_ref=bufs[1].at[me], send_sem=send_sems.at[k - 1], recv_sem=recv_sems.at[k - 1],
                                         device_id=_flip(x, y, c, k), device_id_type=MESH) for k in range(1, N_DEV)]


def _forward_first_copies(n):
    def copies(bufs, send_sems, recv_sems):
        x, y, c, me = _position()
        out = []
        for w in range(n):
            for k, to in enumerate([(x, y, 1 - c), (1 - x, y, c), (x, 1 - y, c)]):
                out.append(pltpu.make_async_remote_copy(
                    src_ref=bufs[w], dst_ref=bufs[n + w].at[me], send_sem=send_sems.at[w * 3 + k],
                    recv_sem=recv_sems.at[w * 3 + k], device_id=to, device_id_type=MESH))
        return out
    return copies


def _forward_second_copies(n):
    def copies(bufs, send_sems, recv_sems):
        x, y, c, _ = _position()
        out = []
        for w in range(n):
            half = bufs[n + w].shape[1] // 2
            for k, (src_chip, rows, to) in enumerate([((1 - x, y), pl.ds(0, half), (x, 1 - y, c)), ((x, 1 - y), pl.ds(half, half), (1 - x, y, c))]):
                blk = bufs[n + w].at[4 * src_chip[0] + 2 * src_chip[1] + c, rows]
                out.append(pltpu.make_async_remote_copy(src_ref=blk, dst_ref=blk, send_sem=send_sems.at[w * 4 + k],
                                                        recv_sem=recv_sems.at[w * 4 + k], device_id=to, device_id_type=MESH))
            for k, (px, py) in enumerate([(1 - x, y), (x, 1 - y)]):
                blk = bufs[n + w].at[4 * px + 2 * py + c]
                out.append(pltpu.make_async_remote_copy(src_ref=blk, dst_ref=blk, send_sem=send_sems.at[w * 4 + 2 + k],
                                                        recv_sem=recv_sems.at[w * 4 + 2 + k], device_id=(x, y, 1 - c), device_id_type=MESH))
        return out
    return copies


def _forward_third_copies(n):
    def copies(bufs, send_sems, recv_sems):
        x, y, c, _ = _position()
        out = []
        for w in range(n):
            blk = bufs[n + w].at[4 * (1 - x) + 2 * (1 - y) + c]
            out.append(pltpu.make_async_remote_copy(src_ref=blk, dst_ref=blk, send_sem=send_sems.at[w], recv_sem=recv_sems.at[w],
                                                    device_id=(x, y, 1 - c), device_id_type=MESH))
        return out
    return copies


def _to_sibling_copies(n):
    def copies(bufs, send_sems, recv_sems):
        x, y, c, _ = _position()
        out = []
        for w in range(n):
            for q in range(N_CHIP):
                out.append(pltpu.make_async_remote_copy(
                    src_ref=bufs[w].at[2 * q + 1 - c], dst_ref=bufs[n + w].at[q], send_sem=send_sems.at[w * N_CHIP + q],
                    recv_sem=recv_sems.at[w * N_CHIP + q], device_id=(x, y, 1 - c), device_id_type=MESH))
        return out
    return copies


def _to_owner_copies(n):
    def copies(bufs, send_sems, recv_sems):
        x, y, c, _ = _position()
        out = []
        for w in range(n):
            for k in range(1, N_CHIP):
                px, py = (1 - x if k & 2 else x), (1 - y if k & 1 else y)
                out.append(pltpu.make_async_remote_copy(
                    src_ref=bufs[w].at[2 * px + py], dst_ref=bufs[n + w].at[k - 1], send_sem=send_sems.at[w * (N_CHIP - 1) + k - 1],
                    recv_sem=recv_sems.at[w * (N_CHIP - 1) + k - 1], device_id=(px, py, c), device_id_type=MESH))
        return out
    return copies


def _chip_sum(name, stack, landed, c_idx):
    _, R, C = stack.shape
    tr = _tile(R, max(BF16_ROWS, STREAM_TILE // C), BF16_ROWS)

    def step(a_ref, b_ref, o_ref):
        o_ref[...] = (a_ref[...].astype(F32) + b_ref[...].astype(F32)).astype(o_ref.dtype)

    def body(c_ref, a_ref, b_ref, o_ref):
        c = c_ref[0]
        deep = pl.Buffered(3)
        pltpu.emit_pipeline(
            step, grid=(N_CHIP, R // tr),
            in_specs=[pl.BlockSpec((None, tr, C), lambda q, i: (2 * q + c, i, 0), pipeline_mode=deep),
                      pl.BlockSpec((None, tr, C), lambda q, i: (q, i, 0), pipeline_mode=deep)],
            out_specs=[pl.BlockSpec((None, tr, C), lambda q, i: (q, i, 0))])(a_ref, b_ref, o_ref)

    whole = pl.BlockSpec(memory_space=pl.ANY)
    return pl.pallas_call(
        body, name=name, in_specs=[pl.BlockSpec(memory_space=pltpu.SMEM), whole, whole], out_specs=whole,
        out_shape=jax.ShapeDtypeStruct((N_CHIP, R, C), stack.dtype),
        compiler_params=pltpu.CompilerParams(vmem_limit_bytes=VMEM_LIMIT))(c_idx, stack, landed)


def _ada_mod(c16, w):
    _, D = c16.shape
    n = w.shape[1]
    tk = _tile(D, 512)
    nk = D // tk

    def body(c_ref, w_ref, o_ref, ca_ref):
        @pl.when(pl.program_id(0) == 0)
        def _():
            o_ref[...] = jnp.zeros_like(o_ref)

        cv = c_ref[...]
        ca = cv * _sigmoid(cv)
        ca_ref[...] = ca
        o_ref[...] += _dot(ca, w_ref[...])

    return pl.pallas_call(
        body, name="ada_mod", grid=(nk,),
        in_specs=[pl.BlockSpec((16, tk), lambda k: (0, k)), pl.BlockSpec((tk, n), lambda k: (k, 0))],
        out_specs=[pl.BlockSpec((16, n), lambda k: (0, 0)), pl.BlockSpec((16, tk), lambda k: (0, k))],
        out_shape=[jax.ShapeDtypeStruct((16, n), F32), jax.ShapeDtypeStruct((16, D), F32)],
        compiler_params=_cparams(("arbitrary",)))(c16, w)


def _cast_shard(name, wf, slot):
    r, c = wf.shape
    tr = _tile(r, max(BF16_ROWS, STREAM_TILE // c), BF16_ROWS)

    def body(slot_ref, w_ref, s_ref, g_ref):
        v = w_ref[...].astype(BF16)
        s_ref[...] = v
        g_ref[...] = v

    return pl.pallas_call(
        body, name=name,
        grid_spec=pltpu.PrefetchScalarGridSpec(
            num_scalar_prefetch=1, grid=(r // tr,), in_specs=[pl.BlockSpec((tr, c), lambda i, s: (i, 0))],
            out_specs=[pl.BlockSpec((tr, c), lambda i, s: (i, 0)), pl.BlockSpec((None, tr, c), lambda i, s: (s[0], i, 0))]),
        out_shape=[jax.ShapeDtypeStruct((r, c), BF16), jax.ShapeDtypeStruct((N_DEV, r, c), BF16)],
        compiler_params=_cparams(("parallel",)))(slot, wf)


def _adam_math(w, g, m, v):
    m2 = ADAM_B1 * m + (1.0 - ADAM_B1) * g
    v2 = ADAM_B2 * v + (1.0 - ADAM_B2) * (g * g)
    m_hat = m2 / (1.0 - ADAM_B1 ** ADAM_STEP)
    v_hat = v2 / (1.0 - ADAM_B2 ** ADAM_STEP)
    delta = -ADAM_LR * (m_hat / (jnp.sqrt(v_hat) + ADAM_EPS) + ADAM_WD * w)
    return delta, m2, v2


def _adamw(name, w, m, v, own, own_slot, parts=(), row0=0, into=None):
    R, C = w.shape
    Rp = own.shape[1]
    tr = _tile(Rp, max(BF16_ROWS, ADAMW_TILE // C), BF16_ROWS)
    off = row0 // tr
    n_p = len(parts)
    held = [] if into is None else list(into)

    def step(*refs):
        w_ref, m_ref, v_ref, own_ref = refs[:4]
        g_ref, d_ref, m2_ref, v2_ref = refs[4 + n_p:]
        g = own_ref[...].astype(F32)
        for p_ref in refs[4:4 + n_p]:
            for s in range(p_ref.shape[0]):
                g = g + p_ref[s].astype(F32)
        delta, m2, v2 = _adam_math(w_ref[...], g, m_ref[...], v_ref[...])
        g_ref[...] = g
        d_ref[...] = delta
        m2_ref[...] = m2
        v2_ref[...] = v2

    deep = pl.Buffered(3)
    blk = pl.BlockSpec((tr, C), lambda i: (i + off, 0))
    blk_in = pl.BlockSpec((tr, C), lambda i: (i + off, 0), pipeline_mode=deep)
    in_specs = ([blk_in, blk_in, blk_in, pl.BlockSpec((tr, C), lambda i: (i, 0), pipeline_mode=deep)]
                + [pl.BlockSpec((a.shape[0], tr, C), lambda i: (0, i, 0), pipeline_mode=deep) for a in parts])

    def body(slot_ref, w_ref, m_ref, v_ref, own_ref, *refs):
        pltpu.emit_pipeline(step, grid=(Rp // tr,), in_specs=in_specs, out_specs=[blk] * 4)(
            w_ref, m_ref, v_ref, own_ref.at[slot_ref[0]], *refs[:n_p], *refs[n_p + len(held):])

    out = jax.ShapeDtypeStruct((R, C), F32)
    whole = pl.BlockSpec(memory_space=pl.ANY)
    return pl.pallas_call(
        body, name=name, in_specs=[pl.BlockSpec(memory_space=pltpu.SMEM)] + [whole] * (4 + n_p + len(held)), out_specs=[whole] * 4,
        out_shape=[out] * 4, input_output_aliases={5 + n_p + i: i for i in range(len(held))},
        compiler_params=pltpu.CompilerParams(vmem_limit_bytes=VMEM_LIMIT))(own_slot, w, m, v, own, *parts, *held)


def _small_update(gathered, w, m, v, after, rows):
    _, R, L = gathered.shape
    rs = w.shape[0]
    n = len(rows)
    assert all(r % 8 == 0 for r in rows) and sum(rows) <= rs and rs + 8 <= R

    def body(p_ref, w_ref, m_ref, v_ref, after_ref, *outs):
        g = p_ref[0]
        for p in range(1, N_DEV):
            g = g + p_ref[p]
        kinds = (g,) + _adam_math(w_ref[...], g[0:rs, :], m_ref[...], v_ref[...])
        at = 0
        for k, r in enumerate(rows):
            for idx, val in enumerate(kinds):
                outs[idx * n + k][...] = val[at:at + r, :]
            at += r
        outs[4 * n][...] = g[at:at + 8, :]

    vm = pl.BlockSpec(memory_space=pltpu.VMEM)
    shapes = [jax.ShapeDtypeStruct((r, L), F32) for _ in range(4) for r in rows] + [jax.ShapeDtypeStruct((8, L), F32)]
    return pl.pallas_call(body, name="small_update", in_specs=[vm] * 4 + [pl.BlockSpec(memory_space=pl.ANY)], out_specs=[vm] * len(shapes),
                          out_shape=shapes, compiler_params=pltpu.CompilerParams(vmem_limit_bytes=VMEM_LIMIT))(gathered, w, m, v, after)


class _Fetched(dict):
    def __init__(self, fetch):
        super().__init__()
        self.fetch = fetch

    def first(self, key, after):
        self[key] = self.fetch(key, after)
        return self[key]


def _local_step(x, tgt, mod, p, fetch, F, scatter=None):
    S, D = x.shape
    GW, HW = p["ln_g"].shape[1], p["hg_ng"].shape[1]
    G, T, _ = p["ws"].shape
    w = _Fetched(fetch)
    INW = 2 * GW + 4 * HW + 2 * D
    in_loc, br_loc, fi_loc = INW // N_DEV, D // N_DEV, 2 * F // N_DEV
    assert GW == HW and F % fi_loc == 0
    sh1, sc1, gt1, sh2, sc2, gt2 = (mod[:, k * D:(k + 1) * D] for k in range(6))
    bsb = jnp.broadcast_to(p["bs"][:, :, None], (G, T, GW // G))

    tm = _tile(S, 1024, 16)
    tmh = _tile(S, 512, 16)
    tn_in = _tile(in_loc, 1280)
    tn_d = _tile(D, 512)
    tn_br = _tile(br_loc, 512)
    tk_s = S
    tm_w = _tile(D, 1024)
    g_off = 2 * GW + 4 * HW

    h1 = _norm_mod("norm1", x, p["norm1_g"], sc1, sh1)
    z = _mm_nn_stacked("proj_in", h1, w.first("in", h1), tm=tm, tn=tn_in, tk=D)[0]
    ya = _gmlp_fwd(z, p["ln_g"], p["ln_b"], p["ws"], bsb, GW)
    yb, o_hg, states = _hg_fwd(z, p["hg_lb"], p["hg_ng"], HW)
    flat = {k: jnp.swapaxes(w.first(k, yb), 0, 1).reshape(GW, D) for k in ("bg", "bh")}
    tn_f = _tile(D, 1024)
    pa = _matmul(
        "branch_gmlp", ya, flat["bg"], dims=_NN, grid_mnk=(S // tm, D // tn_f, 1), tiles=(tm, tn_f),
        a_spec=pl.BlockSpec((tm, GW), lambda i, j, k: (i, 0)), b_spec=pl.BlockSpec((GW, tn_f), lambda i, j, k: (0, j)),
        out_shapes=[jax.ShapeDtypeStruct((S, D), F32)], out_specs=[pl.BlockSpec((tm, tn_f), lambda i, j, k: (i, j))], epilogue=_store(F32))[0]
    t_fi = w.first("fi_early", pa)

    def gates(ga_ref, gb_ref, ba_ref, bb_ref):
        return _sigmoid(ga_ref[...] + ba_ref[...]), _sigmoid(gb_ref[...] + bb_ref[...])

    def gate_specs(tn_, tm_=tm):
        o1, o2 = g_off // tn_, (g_off + D) // tn_
        return [pl.BlockSpec((tm_, tn_), lambda i, j, k: (i, o1 + j)), pl.BlockSpec((tm_, tn_), lambda i, j, k: (i, o2 + j)),
                pl.BlockSpec((1, tn_), lambda i, j, k: (0, j)), pl.BlockSpec((1, tn_), lambda i, j, k: (0, D // tn_ + j))]

    def merge_ep(acc, ex, outs):
        ga, gb = gates(*ex[1:5])
        outs[0][...] = acc.astype(BF16)
        outs[1][...] = (ga * ex[0][...] + gb * acc).astype(BF16)

    tile_o = pl.BlockSpec((tmh, tn_f), lambda i, j, k: (i, j))
    pb, y = _matmul(
        "branch_hg_merge", yb, flat["bh"], dims=_NN, grid_mnk=(S // tmh, D // tn_f, 1), tiles=(tmh, tn_f),
        a_spec=pl.BlockSpec((tmh, HW), lambda i, j, k: (i, 0)), b_spec=pl.BlockSpec((HW, tn_f), lambda i, j, k: (0, j)),
        extras=[pa, z, z, p["b_gate"], p["b_gate"]], extra_specs=[tile_o, *gate_specs(tn_f, tmh)],
        out_shapes=[jax.ShapeDtypeStruct((S, D), BF16), jax.ShapeDtypeStruct((S, D), BF16)], out_specs=[tile_o, tile_o],
        epilogue=merge_ep, after=t_fi)

    def resid_ep(acc, ex, outs):
        outs[0][...] = acc.astype(BF16)
        outs[1][...] = ex[0][...] + ex[1][...] * acc

    def resid_mm(name, a, b, res, gt, tm_, tn_):
        K = a.shape[1]
        t_o = pl.BlockSpec((tm_, tn_), lambda i, j, k: (i, j))
        return _matmul(
            name, a, b, dims=_NN, grid_mnk=(S // tm_, D // tn_, 1), tiles=(tm_, tn_),
            a_spec=pl.BlockSpec((tm_, K), lambda i, j, k: (i, 0)), b_spec=pl.BlockSpec((K, tn_), lambda i, j, k: (0, j)),
            extras=[res, gt], extra_specs=[t_o, pl.BlockSpec((1, tn_), lambda i, j, k: (0, j))],
            out_shapes=[jax.ShapeDtypeStruct((S, D), BF16), jax.ShapeDtypeStruct((S, D), F32)], out_specs=[t_o, t_o], epilogue=resid_ep)

    o1, xm = resid_mm("proj_out", y, w.first("out", z), x, gt1, tm, tn_f)
    h2 = _norm_mod("norm2", xm, p["norm2_g"], sc2, sh2)
    hf, hf_fac = _ffn_in_swiglu(h2, w.first("fi", h2))
    o2, x3 = resid_mm("ffn_out", hf, w.first("fo", hf), xm, gt2, tm, _tile(D, 256))
    dx3, do2, vec_l = _loss_head(x3, tgt, p["final_g"], o2, gt2)

    nf = F // fi_loc

    def dswiglu_ep(acc, ex, outs):
        outs[0][0] = (acc * ex[0][0].astype(F32)).astype(BF16)
        outs[0][1] = (acc * ex[0][1].astype(F32)).astype(BF16)

    pair = pl.BlockSpec((2, tmh, fi_loc), lambda i, j, k: (0, j, i))
    dab = _matmul(
        "ffn_out_dx", do2, w["fo"], dims=_NT, grid_mnk=(nf, S // tmh, 1), tiles=(tmh, fi_loc),
        a_spec=pl.BlockSpec((tmh, D), lambda i, j, k: (j, 0)), b_spec=pl.BlockSpec((fi_loc, D), lambda i, j, k: (i, 0)),
        extras=[hf_fac], extra_specs=[pair], out_shapes=[jax.ShapeDtypeStruct((2, S, F), BF16)], out_specs=[pair],
        epilogue=dswiglu_ep)[0]
    start = (lambda name, grads: scatter[0](name, grads)) if scatter is not None else (lambda name, grads: None)
    push = (lambda name, after: scatter[1](name, after)) if scatter is not None else (lambda name, after: None)

    def zero(token):
        return 0.0 if token is None else token[0:1, 0:1]

    tm_f = _tile(F, 512)
    g_fo = _mm_tn("ffn_out_dw", hf, do2, pl.BlockSpec((tk_s, D), lambda i, j, k: (k, j)), Mo=F, No=D, S=S, tm=tm_f, tn=D, tk=tk_s)
    g_fi = _mm_tn("ffn_in_dw", h2, dab, pl.BlockSpec((None, tk_s, fi_loc), lambda i, j, k: (j // nf, k, j % nf)),
                  Mo=D, No=2 * F, S=S, tm=D, tn=fi_loc, tk=tk_s, stacked_nloc=fi_loc, after=g_fo)
    t_ffn = start("scatter_ffn", dict(fo=g_fo, fi=g_fi))
    dh2 = _mm_nt_stacked("ffn_in_dx", pl.BlockSpec((None, tm, fi_loc), lambda i, j, k: (k // nf, i, k % nf)), dab, w["fi"],
                         M=S, tm=tm, tn=D, tk=fi_loc, after=t_ffn)
    dxm, vec2, do1 = _norm_mod_bwd("norm2_bwd", dh2, xm, p["norm2_g"], sc2, dx3, o1, gt1)
    t_ffn = push("scatter_ffn", dxm)

    def dmerge_ep(acc, ex, outs):
        ga, gb = gates(*ex[2:6])
        outs[0][...] = (acc * ga).astype(BF16)
        outs[1][...] = (acc * gb).astype(BF16)
        outs[2][0] = (acc * ex[0][...] * ga * (1.0 - ga)).astype(BF16)
        outs[2][1] = (acc * ex[1][...] * gb * (1.0 - gb)).astype(BF16)

    t_o = pl.BlockSpec((tm, tn_d), lambda i, j, k: (i, j))
    dpa, dpb, dg2 = _matmul(
        "proj_out_dx", do1, w["out"], dims=_NT, grid_mnk=(S // tm, D // tn_d, 1), tiles=(tm, tn_d),
        a_spec=pl.BlockSpec((tm, D), lambda i, j, k: (i, 0)), b_spec=pl.BlockSpec((tn_d, D), lambda i, j, k: (j, 0)),
        extras=[pa, pb, z, z, p["b_gate"], p["b_gate"]], extra_specs=[t_o, t_o, *gate_specs(tn_d)],
        out_shapes=[jax.ShapeDtypeStruct((S, D), BF16), jax.ShapeDtypeStruct((S, D), BF16), jax.ShapeDtypeStruct((2, S, D), BF16)],
        out_specs=[t_o, t_o, pl.BlockSpec((2, tm, tn_d), lambda i, j, k: (0, i, j))], epilogue=dmerge_ep, after=t_ffn)
    g_out = _mm_tn("proj_out_dw", y, do1, pl.BlockSpec((tk_s, D), lambda i, j, k: (k, j)), Mo=D, No=D, S=S, tm=tn_d, tn=D, tk=tk_s)
    tn_g = _tile(GW, 512)
    b_br = pl.BlockSpec((tk_s, br_loc), lambda i, j, k: (k, j))
    tm_b = _tile(GW, 1024)
    g_bg = _mm_tn("branch_gmlp_dw", ya, dpa, b_br, Mo=GW, No=D, S=S, tm=tm_b, tn=br_loc, tk=tk_s, stacked_nloc=br_loc)
    g_bh = _mm_tn("branch_hg_dw", yb, dpb, b_br, Mo=HW, No=D, S=S, tm=tm_b, tn=br_loc, tk=tk_s, stacked_nloc=br_loc)
    t_mix = start("scatter_mixer", dict(out=g_out, bg=g_bg, bh=g_bh))
    def branch_dx(name, dp, w_flat):
        return _matmul(
            name, dp, w_flat, dims=_NT, grid_mnk=(S // tm, GW // tn_g, 1), tiles=(tm, tn_g),
            a_spec=pl.BlockSpec((tm, D), lambda i, j, k: (i, 0)), b_spec=pl.BlockSpec((tn_g, D), lambda i, j, k: (j, 0)),
            out_shapes=[jax.ShapeDtypeStruct((S, GW), F32)], out_specs=[pl.BlockSpec((tm, tn_g), lambda i, j, k: (i, j))],
            epilogue=_store(F32), after=t_mix)[0]

    dya = branch_dx("branch_gmlp_dx", dpa, flat["bg"])
    dyb = branch_dx("branch_hg_dx", dpb, flat["bh"])
    dz_gmlp, dln, dws, dbs = _gmlp_bwd(z, dya, p["ln_g"], p["ln_b"], p["ws"], bsb, GW)
    t_mix = push("scatter_mixer", dz_gmlp)
    dz, dng, dhlb, db_gate = _hg_bwd(z, o_hg, states, dyb, p["hg_lb"], p["hg_ng"] + zero(t_mix), HW, dz_gmlp, dg2)
    half = D // 2
    tm_h = _tile(half, 1024)
    g_in = []
    t_in = None
    for hname, h in (("a", 0), ("b", 1)):
        g_in.append(_mm_tn("proj_in_dw_" + hname, h1, dz, pl.BlockSpec((tk_s, in_loc), lambda i, j, k: (k, j)), Mo=half, No=INW, S=S,
                           tm=tm_h, tn=in_loc, tk=tk_s, stacked_nloc=in_loc, after=t_in, a_off=h * (half // tm_h)))
        t_in = start("scatter_proj_in_" + hname, {"w_in_" + hname: g_in[-1]})
    t_in = push("scatter_proj_in_a", t_in)
    dh1 = _mm_nt_stacked("proj_in_dx", pl.BlockSpec((tm, in_loc), lambda i, j, k: (i, k)), dz, w["in"], M=S, tm=tm, tn=D, tk=in_loc,
                         after=t_in)
    dx, vec1 = _norm_mod_bwd("norm1_bwd", dh1, x, p["norm1_g"], sc1, dxm)

    dmod = jnp.concatenate([vec1[0:1], vec1[1:2], vec2[3:4], vec2[0:1], vec2[1:2], vec_l[2:3]], axis=1)
    small = dict(norm1_g=vec1[2:3], b_gate=db_gate.reshape(1, 2 * D), ln_g=dln[0:1], ln_b=dln[1:2], ws=dws, bs=dbs.reshape(G, T),
                 hg_lb=dhlb, hg_ng=dng[0:1], norm2_g=vec2[2:3], final_g=vec_l[1:2], loss=vec_l[0:1, 0:LANES])
    big = dict(w_in_a=g_in[0], w_in_b=g_in[1], bg=g_bg, bh=g_bh, out=g_out, fi=g_fi, fo=g_fo)
    return dx, big, small, dmod


_SMALL = ("b_ada", "norm1_g", "b_gate", "ln_g", "ln_b", "ws", "bs", "hg_lb", "hg_ng", "norm2_g", "final_g")


def _pack(parts, rows_mult=8):
    flat = [a.reshape(-1) for a in parts]
    offs, n = [], 0
    for a in flat:
        offs.append(n)
        n += a.shape[0]
    pad = (-n) % (LANES * rows_mult)
    if pad:
        flat.append(jnp.zeros((pad,), F32))
    return jnp.concatenate(flat).reshape(-1, LANES), offs


def kernel(x, c, w_ada, b_ada, norm1_g, w_in, b_gate, gmlp_ln_g, gmlp_ln_b, gmlp_ws, gmlp_bs, hg_lb, hg_norm_g, w_branch_gmlp, w_branch_hg, w_out, norm2_g, w_ffn_in, w_ffn_out, final_norm_g, loss_target, m_w_ada, m_b_ada, m_norm1_g, m_w_in, m_b_gate, m_gmlp_ln_g, m_gmlp_ln_b, m_gmlp_ws, m_gmlp_bs, m_hg_lb, m_hg_norm_g, m_w_branch_gmlp, m_w_branch_hg, m_w_out, m_norm2_g, m_w_ffn_in, m_w_ffn_out, m_final_norm_g, v_w_ada, v_b_ada, v_norm1_g, v_w_in, v_b_gate, v_gmlp_ln_g, v_gmlp_ln_b, v_gmlp_ws, v_gmlp_bs, v_hg_lb, v_hg_norm_g, v_w_branch_gmlp, v_w_branch_hg, v_w_out, v_norm2_g, v_w_ffn_in, v_w_ffn_out, v_final_norm_g):
    S, D = x.shape[1], x.shape[2]
    ada_loc = w_ada.shape[2]
    me = 4 * lax.axis_index("x") + 2 * lax.axis_index("y") + lax.axis_index("c")
    me_idx = me.astype(jnp.int32).reshape(1)

    def empty_hbm(shape, dtype):
        return pltpu.with_memory_space_constraint(lax.empty(shape, dtype), pltpu.HBM)

    groups = dict(gather_in=dict(keys=["in"], src=[w_in], forward=True),
                  gather_mixer=dict(keys=["bg", "bh", "out"], src=[w_branch_gmlp, w_branch_hg, w_out], forward=False),
                  gather_ffn_in=dict(keys=["fi"], src=[w_ffn_in], forward=True),
                  gather_ffn_out=dict(keys=["fo"], src=[w_ffn_out], forward=False))
    group_of = {k: gname for gname, g in groups.items() for k in g["keys"]}

    def first_hop(gname, after):
        g = groups[gname]
        n = len(g["keys"])
        cast = [_cast_shard(f"{gname}_cast_{k}", a[0], me_idx) for k, a in zip(g["keys"], g["src"])]
        shards, outs = [s for s, _ in cast], [o for _, o in cast]
        if g["forward"]:
            *g["hop"], token = _split_start(gname + "_hop1", shards + outs, n * 3, _forward_first_copies(n), after=after)
        else:
            *g["hop"], token = _split_start(gname + "_hop1", shards + outs, n * N_CHIP, _gather_first_copies(n), after=after)
        return token

    def second_hop(gname, after):
        g = groups[gname]
        n = len(g["keys"])
        *g["hop"], token = _split_relay(gname + "_hop2", g["hop"][2], g["hop"][0], g["hop"][1], after,
                                        _forward_first_copies(n), n * 4, _forward_second_copies(n))
        return token

    def finish(gname, after):
        g = groups[gname]
        n = len(g["keys"])
        send_sems, recv_sems, bufs = g["hop"]
        if g["forward"]:
            send_sems, recv_sems, bufs, _ = _split_relay(gname + "_hop3", bufs, send_sems, recv_sems, after,
                                                         _forward_second_copies(n), n, _forward_third_copies(n))
            bufs = _split_wait(gname + "_wait", bufs, send_sems, recv_sems, after, _forward_third_copies(n))
        else:
            send_sems, recv_sems, bufs, _ = _split_relay(gname + "_relay", bufs, send_sems, recv_sems, after,
                                                         _gather_first_copies(n), n * (N_CHIP - 1), _gather_relay_copies(n))
            bufs = _split_wait(gname + "_wait", bufs, send_sems, recv_sems, after, _gather_relay_copies(n))
        g["done"] = dict(zip(g["keys"], bufs[n:]))

    c_all = _allgather_small("gather_c", c.reshape(D // LANES, LANES)).reshape(N_DEV, D)
    token = first_hop("gather_in", c_all)
    mod_cols, c_act = _ada_mod(jnp.pad(c_all, ((0, 16 - N_DEV), (0, 0))) + token[0:1, 0:1], w_ada[0])
    mod_vec = mod_cols[:N_DEV].reshape(-1, LANES)
    mg_send, mg_recv, mg_bufs, token = _split_start(
        "gather_mod_start", [mod_vec, lax.dynamic_update_slice(lax.empty((N_DEV, *mod_vec.shape), F32), mod_vec[None], (me, 0, 0))],
        N_DEV - 1, _small_gather_copies)
    token = second_hop("gather_in", token)
    token = first_hop("gather_ffn_in", first_hop("gather_mixer", token))
    mod_all = _split_wait("gather_mod_wait", mg_bufs, mg_send, mg_recv, token, _small_gather_copies)[1].reshape(N_DEV, N_DEV, ada_loc)
    mod = lax.dynamic_index_in_dim(mod_all, me, axis=1, keepdims=False).reshape(1, N_DEV * ada_loc) + b_ada

    def fetch(key, after):
        if key == "in":
            finish("gather_in", after)
        elif key == "fi_early":
            return first_hop("gather_ffn_out", second_hop("gather_ffn_in", after))
        elif "done" not in groups[group_of[key]]:
            finish(group_of[key], after)
        arr = groups[group_of[key]]["done"][key]
        return arr.reshape(-1, D) if key in ("out", "fo") else arr

    p = dict(norm1_g=norm1_g, b_gate=b_gate, ln_g=gmlp_ln_g, ln_b=gmlp_ln_b, ws=gmlp_ws[0], bs=gmlp_bs[0], hg_lb=hg_lb,
             hg_ng=hg_norm_g, norm2_g=norm2_g, final_g=final_norm_g.reshape(1, D))

    in_flight = {}
    c_idx = lax.axis_index("c").astype(jnp.int32).reshape(1)
    chip_idx = (2 * lax.axis_index("x") + lax.axis_index("y")).astype(jnp.int32).reshape(1)

    def scatter_start(name, grads):
        keys = list(grads)
        n = len(keys)
        stacks = [grads[k].reshape(N_DEV, -1, grads[k].shape[-1]) for k in keys]
        lands = [empty_hbm((N_CHIP, *g.shape[1:]), g.dtype) for g in stacks]
        send_sems, recv_sems, bufs, token = _split_start(name + "_d2d", stacks + lands, n * N_CHIP, _to_sibling_copies(n))
        in_flight[name] = dict(keys=keys, stage1=(send_sems, recv_sems, bufs))
        return token

    def scatter_push(name, after):
        f = in_flight[name]
        n = len(f["keys"])
        send_sems, recv_sems, bufs = f["stage1"]
        bufs = _split_wait(name + "_d2d_wait", bufs, send_sems, recv_sems, after, _to_sibling_copies(n))
        sums = [_chip_sum(f"{name}_sum_{k}", bufs[i], bufs[n + i], c_idx) for i, k in enumerate(f["keys"])]
        lands = [empty_hbm((N_CHIP - 1, *s.shape[1:]), s.dtype) for s in sums]
        send_sems, recv_sems, bufs, token = _split_start(name + "_ici", sums + lands, n * (N_CHIP - 1), _to_owner_copies(n))
        f["stage2"] = (send_sems, recv_sems, bufs)
        return token

    grad_x, _, small, dmod = _local_step(x[0], loss_target[0], mod, p, fetch, w_ffn_out.shape[1] * N_DEV, (scatter_start, scatter_push))

    small["b_ada"] = dmod
    packed, offs = _pack([small[k] for k in _SMALL] + [small["loss"]])
    sg_send, sg_recv, sg_bufs, t_tail = _split_start(
        "gather_small_start", [packed, lax.dynamic_update_slice(lax.empty((N_DEV, *packed.shape), F32), packed[None], (me, 0, 0))],
        N_DEV - 1, _small_gather_copies)
    t_tail = scatter_push("scatter_proj_in_b", t_tail)
    big_w = dict(w_in=(w_in, m_w_in, v_w_in, "w_in"), bg=(w_branch_gmlp, m_w_branch_gmlp, v_w_branch_gmlp, "w_branch_gmlp"),
                 bh=(w_branch_hg, m_w_branch_hg, v_w_branch_hg, "w_branch_hg"), out=(w_out, m_w_out, v_w_out, "w_out"),
                 fi=(w_ffn_in, m_w_ffn_in, v_w_ffn_in, "w_ffn_in"), fo=(w_ffn_out, m_w_ffn_out, v_w_ffn_out, "w_ffn_out"))
    upd = {}

    def land_and_update(name, after):
        keys = in_flight[name]["keys"]
        n = len(keys)
        send_sems, recv_sems, bufs = in_flight[name]["stage2"]
        bufs = _split_wait(name + "_ici_wait", bufs, send_sems, recv_sems, after, _to_owner_copies(n))
        for i, k in enumerate(keys):
            if k in big_w:
                wt, mt, vt, out_name = big_w[k]
                upd[out_name] = _adamw("adamw_" + out_name, wt[0], mt[0], vt[0], bufs[i], chip_idx, [bufs[n + i]])
            else:
                wt, mt, vt, out_name = big_w["w_in"]
                upd[out_name] = _adamw("adamw_" + k, wt[0], mt[0], vt[0], bufs[i], chip_idx, [bufs[n + i]],
                                       row0=0 if k == "w_in_a" else bufs[i].shape[1], into=upd.get(out_name))
            after = upd[out_name][1]
        return after

    after = land_and_update("scatter_mixer", land_and_update("scatter_ffn", t_tail))
    gathered = _split_wait("gather_small_wait", sg_bufs, sg_send, sg_recv, after, _small_gather_copies)[1]
    wp = dict(p, b_ada=b_ada)
    ms = dict(b_ada=m_b_ada, norm1_g=m_norm1_g, b_gate=m_b_gate, ln_g=m_gmlp_ln_g, ln_b=m_gmlp_ln_b, ws=m_gmlp_ws, bs=m_gmlp_bs,
              hg_lb=m_hg_lb, hg_ng=m_hg_norm_g, norm2_g=m_norm2_g, final_g=m_final_norm_g)
    vs = dict(b_ada=v_b_ada, norm1_g=v_norm1_g, b_gate=v_b_gate, ln_g=v_gmlp_ln_g, ln_b=v_gmlp_ln_b, ws=v_gmlp_ws, bs=v_gmlp_bs,
              hg_lb=v_hg_lb, hg_ng=v_hg_norm_g, norm2_g=v_norm2_g, final_g=v_final_norm_g)
    w_sm, _ = _pack([wp[k] for k in _SMALL])
    m_sm, _ = _pack([ms[k] for k in _SMALL])
    v_sm, _ = _pack([vs[k] for k in _SMALL])
    shapes = dict(b_ada=b_ada.shape, norm1_g=norm1_g.shape, b_gate=b_gate.shape, ln_g=gmlp_ln_g.shape, ln_b=gmlp_ln_b.shape,
                  ws=gmlp_ws.shape, bs=gmlp_bs.shape, hg_lb=hg_lb.shape, hg_ng=hg_norm_g.shape, norm2_g=norm2_g.shape,
                  final_g=final_norm_g.shape)
    sm_out = _small_update(gathered, w_sm, m_sm, v_sm, after, [math.prod(shapes[k]) // LANES for k in _SMALL])

    def unpack(idx, k):
        return sm_out[idx * len(_SMALL) + _SMALL.index(k)].reshape(shapes[k])

    loss = sm_out[-1][0, 0]

    assert offs[0] == 0 and ada_loc % LANES == 0
    dmod_loc = lax.dynamic_slice_in_dim(gathered, me * (ada_loc // LANES), ada_loc // LANES, axis=1).reshape(N_DEV, ada_loc)
    ca_t = jnp.pad(c_act[:N_DEV].T, ((0, 0), (0, LANES - N_DEV))).astype(BF16)
    dm_p = jnp.pad(dmod_loc, ((0, LANES - N_DEV), (0, 0))).astype(BF16)
    tm_a = _tile(D, 512)
    g_ada = _matmul(
        "ada_dw", ca_t, dm_p, dims=_NN, grid_mnk=(D // tm_a, 1, 1), tiles=(tm_a, ada_loc),
        a_spec=pl.BlockSpec((tm_a, LANES), lambda i, j, k: (i, 0)), b_spec=pl.BlockSpec((LANES, ada_loc), lambda i, j, k: (0, 0)),
        out_shapes=[jax.ShapeDtypeStruct((1, D, ada_loc), F32)], out_specs=[pl.BlockSpec((None, tm_a, ada_loc), lambda i, j, k: (0, i, 0))],
        epilogue=_store(F32))[0]
    upd["w_ada"] = _adamw("adamw_w_ada", w_ada[0], m_w_ada[0], v_w_ada[0], g_ada, jnp.zeros((1,), jnp.int32))
    land_and_update("scatter_proj_in_b", land_and_update("scatter_proj_in_a", upd["w_ada"][1]))

    order = ("w_ada", "b_ada", "norm1_g", "w_in", "b_gate", "ln_g", "ln_b", "ws", "bs", "hg_lb", "hg_ng", "w_branch_gmlp", "w_branch_hg",
             "w_out", "norm2_g", "w_ffn_in", "w_ffn_out", "final_g")
    outs = [loss, grad_x[None]]
    for idx in range(4):
        for k in order:
            outs.append(upd[k][idx][None] if k in upd else unpack(idx, k))
    return tuple(outs)
```
